```python
import math
import jax, jax.numpy as jnp
from jax import lax
import numpy as np

D_MODEL = 1024
BATCH = 8
SEQ = 4096
DEPTH = 1

N_MEM = 256
SWA_HEADS = 8
SWA_KV_HEADS = 2
SWA_HEAD_DIM = 64
WINDOW = 128
FOX_HEADS = 8
FOX_HEAD_DIM = 64
MEM_HEADS = 4
MEM_HEAD_DIM = 128
N_BRANCHES = 3
D_FF = 4 * D_MODEL
REL_BUCKETS = 32
REL_MAX_DIST = 128
BLOCK = 128
EPS = 1e-6
NEG = -1e30

SWA_Q = SWA_HEADS * SWA_HEAD_DIM
SWA_KV = SWA_KV_HEADS * SWA_HEAD_DIM
FOX_W = FOX_HEADS * FOX_HEAD_DIM
MEM_W = MEM_HEADS * MEM_HEAD_DIM
GATE_W = N_BRANCHES * D_MODEL
SPLIT_POINTS = (
    SWA_Q,
    SWA_Q + SWA_KV,
    SWA_Q + 2 * SWA_KV,
    SWA_Q + 2 * SWA_KV + FOX_W,
    SWA_Q + 2 * SWA_KV + 2 * FOX_W,
    SWA_Q + 2 * SWA_KV + 3 * FOX_W,
    SWA_Q + 2 * SWA_KV + 3 * FOX_W + FOX_HEADS,
    SWA_Q + 2 * SWA_KV + 3 * FOX_W + FOX_HEADS + MEM_W,
)
IN_WIDTH = SWA_Q + 2 * SWA_KV + 3 * FOX_W + FOX_HEADS + MEM_W + GATE_W

kernel_name = "hybrid_swa_fox_memory_gated_block"


def rmsnorm(x, g):
    xf = x.astype(jnp.float32)
    y = xf * lax.rsqrt(jnp.mean(xf * xf, axis=-1, keepdims=True) + EPS)
    return (y * g.astype(jnp.float32)).astype(x.dtype)


def t5_causal_bucket(dist):
    max_exact = REL_BUCKETS // 2
    d = jnp.maximum(dist, 0)
    df = jnp.maximum(d, 1).astype(jnp.float32)
    large = max_exact + (jnp.log(df / max_exact) / math.log(REL_MAX_DIST / max_exact)
                         * (REL_BUCKETS - max_exact)).astype(jnp.int32)
    large = jnp.minimum(large, REL_BUCKETS - 1)
    return jnp.where(d < max_exact, d, large)


def swa_attention(q, k, v, sinks, rel_bias):
    B, S = q.shape[0], q.shape[1]
    nb = S // BLOCK
    G = SWA_HEADS // SWA_KV_HEADS
    qb = q.reshape(B, nb, BLOCK, SWA_KV_HEADS, G, SWA_HEAD_DIM)
    pad = ((0, 0), (BLOCK, 0), (0, 0), (0, 0))
    kp = jnp.pad(k, pad).reshape(B, nb + 1, BLOCK, SWA_KV_HEADS, SWA_HEAD_DIM)
    vp = jnp.pad(v, pad).reshape(B, nb + 1, BLOCK, SWA_KV_HEADS, SWA_HEAD_DIM)
    kb = jnp.concatenate([kp[:, :-1], kp[:, 1:]], axis=2)
    vb = jnp.concatenate([vp[:, :-1], vp[:, 1:]], axis=2)
    scale = SWA_HEAD_DIM ** -0.5
    s = jnp.einsum('bnqhgd,bnkhd->bnhgqk', qb, kb).astype(jnp.float32) * scale
    t_loc = jnp.arange(BLOCK)[:, None] + BLOCK
    s_loc = jnp.arange(2 * BLOCK)[None, :]
    dist = t_loc - s_loc
    bias = rel_bias[t5_causal_bucket(dist)]
    bias = bias.reshape(BLOCK, 2 * BLOCK, SWA_KV_HEADS, G).transpose(2, 3, 0, 1)
    s = s + bias.astype(jnp.float32)
    band = (dist >= 0) & (dist < WINDOW)
    key_pos = jnp.arange(nb)[:, None] * BLOCK + s_loc - BLOCK
    valid = band[None] & (key_pos >= 0)[:, None, :]
    s = jnp.where(valid[None, :, None, None], s, NEG)
    sink = sinks.astype(jnp.float32).reshape(1, 1, SWA_KV_HEADS, G, 1, 1)
    m = jnp.maximum(jnp.max(s, axis=-1, keepdims=True), sink)
    p = jnp.exp(s - m)
    denom = jnp.sum(p, axis=-1, keepdims=True) + jnp.exp(sink - m)
    p = (p / denom).astype(v.dtype)
    o = jnp.einsum('bnhgqk,bnkhd->bnqhgd', p, vb)
    return o.reshape(B, S, SWA_Q)


def forgetting_attention(q, k, v, log_f):
    B, S = q.shape[0], q.shape[1]
    nb = S // BLOCK
    scale = FOX_HEAD_DIM ** -0.5
    c = jnp.cumsum(log_f, axis=1)
    c_k = c.transpose(0, 2, 1)
    qb = q.reshape(B, nb, BLOCK, FOX_HEADS, FOX_HEAD_DIM).transpose(1, 0, 2, 3, 4)
    cqb = c.reshape(B, nb, BLOCK, FOX_HEADS).transpose(1, 0, 3, 2)
    key_pos = jnp.arange(S)

    def one_block(args):
        qi, cqi, i = args
        s = jnp.einsum('bqhd,bkhd->bhqk', qi, k).astype(jnp.float32) * scale
        s = s + cqi[..., None] - c_k[:, :, None, :]
        q_pos = i * BLOCK + jnp.arange(BLOCK)
        causal = key_pos[None, :] <= q_pos[:, None]
        s = jnp.where(causal, s, NEG)
        p = jax.nn.softmax(s, axis=-1).astype(v.dtype)
        return jnp.einsum('bhqk,bkhd->bqhd', p, v)

    o = lax.map(one_block, (qb, cqb, jnp.arange(nb)))
    return o.transpose(1, 0, 2, 3, 4).reshape(B, S, FOX_W)


def memory_attention(q, mk, mv):
    B, S = q.shape[0], q.shape[1]
    scale = MEM_HEAD_DIM ** -0.5
    s = jnp.einsum('bshd,bmhd->bhsm', q, mk).astype(jnp.float32) * scale
    p = jax.nn.softmax(s, axis=-1).astype(mv.dtype)
    o = jnp.einsum('bhsm,bmhd->bshd', p, mv)
    return o.reshape(B, S, MEM_W)


def _fwd_setup_inputs(seed: int = 0) -> dict:
    key = jax.random.key(seed)
    ks = jax.random.split(key, 24)
    f32 = jnp.float32

    def w(k, shape, fan_in):
        return jax.random.normal(k, shape, f32) * fan_in ** -0.5

    def gain(k, shape):
        return 1.0 + 0.05 * jax.random.normal(k, shape, f32)

    return {
        "x": jax.random.normal(ks[0], (BATCH, SEQ, D_MODEL), f32),
        "mem": jax.random.normal(ks[1], (BATCH, N_MEM, D_MODEL), f32),
        "g_mix": gain(ks[2], (DEPTH, D_MODEL)),
        "w_in": w(ks[3], (DEPTH, D_MODEL, IN_WIDTH), D_MODEL),
        "b_gate": 0.02 * jax.random.normal(ks[4], (DEPTH, GATE_W), f32),
        "b_forget": 3.0 + 0.1 * jax.random.normal(ks[5], (DEPTH, FOX_HEADS), f32),
        "qn_swa": gain(ks[6], (DEPTH, SWA_HEAD_DIM)),
        "kn_swa": gain(ks[7], (DEPTH, SWA_HEAD_DIM)),
        "sink_swa": 0.5 * jax.random.normal(ks[8], (DEPTH, SWA_HEADS), f32),
        "rel_bias": 0.5 * jax.random.normal(ks[9], (REL_BUCKETS, SWA_HEADS), f32),
        "qn_fox": gain(ks[10], (DEPTH, FOX_HEAD_DIM)),
        "kn_fox": gain(ks[11], (DEPTH, FOX_HEAD_DIM)),
        "g_mem": gain(ks[12], (DEPTH, D_MODEL)),
        "w_mem_kv": w(ks[13], (DEPTH, D_MODEL, 2 * MEM_W), D_MODEL),
        "qn_mem": gain(ks[14], (DEPTH, MEM_HEAD_DIM)),
        "kn_mem": gain(ks[15], (DEPTH, MEM_HEAD_DIM)),
        "w_o_swa": w(ks[16], (DEPTH, SWA_Q, D_MODEL), SWA_Q),
        "w_o_fox": w(ks[17], (DEPTH, FOX_W, D_MODEL), FOX_W),
        "w_o_mem": w(ks[18], (DEPTH, MEM_W, D_MODEL), MEM_W),
        "w_out": w(ks[19], (DEPTH, D_MODEL, D_MODEL), D_MODEL),
        "g_mlp": gain(ks[20], (DEPTH, D_MODEL)),
        "w_mlp_up": w(ks[21], (DEPTH, D_MODEL, D_FF), D_MODEL),
        "w_mlp_down": w(ks[22], (DEPTH, D_FF, D_MODEL), D_FF),
    }


def _fwd_reference(x, mem, g_mix, w_in, b_gate, b_forget, qn_swa, kn_swa, sink_swa, rel_bias,
              qn_fox, kn_fox, g_mem, w_mem_kv, qn_mem, kn_mem, w_o_swa, w_o_fox, w_o_mem,
              w_out, g_mlp, w_mlp_up, w_mlp_down):
    B, S = x.shape[0], x.shape[1]
    M = mem.shape[1]
    for layer in range(DEPTH):
        h = rmsnorm(x, g_mix[layer])
        proj = h @ w_in[layer]
        qa, ka, va, qf, kf, vf, fl, qm, gl = jnp.split(proj, SPLIT_POINTS, axis=-1)

        qa = rmsnorm(qa.reshape(B, S, SWA_HEADS, SWA_HEAD_DIM), qn_swa[layer])
        ka = rmsnorm(ka.reshape(B, S, SWA_KV_HEADS, SWA_HEAD_DIM), kn_swa[layer])
        va = va.reshape(B, S, SWA_KV_HEADS, SWA_HEAD_DIM)
        ya = swa_attention(qa, ka, va, sink_swa[layer], rel_bias) @ w_o_swa[layer]

        qf = rmsnorm(qf.reshape(B, S, FOX_HEADS, FOX_HEAD_DIM), qn_fox[layer])
        kf = rmsnorm(kf.reshape(B, S, FOX_HEADS, FOX_HEAD_DIM), kn_fox[layer])
        vf = vf.reshape(B, S, FOX_HEADS, FOX_HEAD_DIM)
        log_f = jax.nn.log_sigmoid(fl.astype(jnp.float32) + b_forget[layer].astype(jnp.float32))
        yf = forgetting_attention(qf, kf, vf, log_f) @ w_o_fox[layer]

        mem_n = rmsnorm(mem, g_mem[layer])
        mk, mv = jnp.split(mem_n @ w_mem_kv[layer], 2, axis=-1)
        mk = rmsnorm(mk.reshape(B, M, MEM_HEADS, MEM_HEAD_DIM), kn_mem[layer])
        mv = mv.reshape(B, M, MEM_HEADS, MEM_HEAD_DIM)
        qm = rmsnorm(qm.reshape(B, S, MEM_HEADS, MEM_HEAD_DIM), qn_mem[layer])
        ym = memory_attention(qm, mk, mv) @ w_o_mem[layer]

        gates = jax.nn.sigmoid((gl + b_gate[layer]).astype(jnp.float32)).astype(x.dtype)
        gates = gates.reshape(B, S, N_BRANCHES, D_MODEL)
        merged = gates[:, :, 0] * ya + gates[:, :, 1] * yf + gates[:, :, 2] * ym
        x = x + merged @ w_out[layer]

        hm = rmsnorm(x, g_mlp[layer])
        u = jnp.square(jax.nn.relu(hm @ w_mlp_up[layer]))
        x = x + u @ w_mlp_down[layer]
    return x


import jax as _jax
import jax.numpy as _jnp

TWIN_FORMAT = 'train_step'
FWD_PARAMS = ['x', 'mem', 'g_mix', 'w_in', 'b_gate', 'b_forget', 'qn_swa', 'kn_swa', 'sink_swa', 'rel_bias', 'qn_fox', 'kn_fox', 'g_mem', 'w_mem_kv', 'qn_mem', 'kn_mem', 'w_o_swa', 'w_o_fox', 'w_o_mem', 'w_out', 'g_mlp', 'w_mlp_up', 'w_mlp_down']
TWIN_WEIGHTS = ['g_mix', 'w_in', 'b_gate', 'b_forget', 'qn_swa', 'kn_swa', 'sink_swa', 'rel_bias', 'qn_fox', 'kn_fox', 'g_mem', 'w_mem_kv', 'qn_mem', 'kn_mem', 'w_o_swa', 'w_o_fox', 'w_o_mem', 'w_out', 'g_mlp', 'w_mlp_up', 'w_mlp_down']
TWIN_DIFF_INPUT = 'x'
TWIN_INPUTS = ['x', 'mem', 'g_mix', 'w_in', 'b_gate', 'b_forget', 'qn_swa', 'kn_swa', 'sink_swa', 'rel_bias', 'qn_fox', 'kn_fox', 'g_mem', 'w_mem_kv', 'qn_mem', 'kn_mem', 'w_o_swa', 'w_o_fox', 'w_o_mem', 'w_out', 'g_mlp', 'w_mlp_up', 'w_mlp_down', 'loss_target', 'm_g_mix', 'm_w_in', 'm_b_gate', 'm_b_forget', 'm_qn_swa', 'm_kn_swa', 'm_sink_swa', 'm_rel_bias', 'm_qn_fox', 'm_kn_fox', 'm_g_mem', 'm_w_mem_kv', 'm_qn_mem', 'm_kn_mem', 'm_w_o_swa', 'm_w_o_fox', 'm_w_o_mem', 'm_w_out', 'm_g_mlp', 'm_w_mlp_up', 'm_w_mlp_down', 'v_g_mix', 'v_w_in', 'v_b_gate', 'v_b_forget', 'v_qn_swa', 'v_kn_swa', 'v_sink_swa', 'v_rel_bias', 'v_qn_fox', 'v_kn_fox', 'v_g_mem', 'v_w_mem_kv', 'v_qn_mem', 'v_kn_mem', 'v_w_o_swa', 'v_w_o_fox', 'v_w_o_mem', 'v_w_out', 'v_g_mlp', 'v_w_mlp_up', 'v_w_mlp_down']
TWIN_OUTPUTS = ['loss', 'grad_x', 'grad_g_mix', 'grad_w_in', 'grad_b_gate', 'grad_b_forget', 'grad_qn_swa', 'grad_kn_swa', 'grad_sink_swa', 'grad_rel_bias', 'grad_qn_fox', 'grad_kn_fox', 'grad_g_mem', 'grad_w_mem_kv', 'grad_qn_mem', 'grad_kn_mem', 'grad_w_o_swa', 'grad_w_o_fox', 'grad_w_o_mem', 'grad_w_out', 'grad_g_mlp', 'grad_w_mlp_up', 'grad_w_mlp_down', 'delta_g_mix', 'delta_w_in', 'delta_b_gate', 'delta_b_forget', 'delta_qn_swa', 'delta_kn_swa', 'delta_sink_swa', 'delta_rel_bias', 'delta_qn_fox', 'delta_kn_fox', 'delta_g_mem', 'delta_w_mem_kv', 'delta_qn_mem', 'delta_kn_mem', 'delta_w_o_swa', 'delta_w_o_fox', 'delta_w_o_mem', 'delta_w_out', 'delta_g_mlp', 'delta_w_mlp_up', 'delta_w_mlp_down', 'new_m_g_mix', 'new_m_w_in', 'new_m_b_gate', 'new_m_b_forget', 'new_m_qn_swa', 'new_m_kn_swa', 'new_m_sink_swa', 'new_m_rel_bias', 'new_m_qn_fox', 'new_m_kn_fox', 'new_m_g_mem', 'new_m_w_mem_kv', 'new_m_qn_mem', 'new_m_kn_mem', 'new_m_w_o_swa', 'new_m_w_o_fox', 'new_m_w_o_mem', 'new_m_w_out', 'new_m_g_mlp', 'new_m_w_mlp_up', 'new_m_w_mlp_down', 'new_v_g_mix', 'new_v_w_in', 'new_v_b_gate', 'new_v_b_forget', 'new_v_qn_swa', 'new_v_kn_swa', 'new_v_sink_swa', 'new_v_rel_bias', 'new_v_qn_fox', 'new_v_kn_fox', 'new_v_g_mem', 'new_v_w_mem_kv', 'new_v_qn_mem', 'new_v_kn_mem', 'new_v_w_o_swa', 'new_v_w_o_fox', 'new_v_w_o_mem', 'new_v_w_out', 'new_v_g_mlp', 'new_v_w_mlp_up', 'new_v_w_mlp_down']
TWIN_LEAF_KINDS = {'loss': 'loss', 'grad_x': 'grad_x', 'grad_g_mix': 'grad_w', 'grad_w_in': 'grad_w', 'grad_b_gate': 'grad_w', 'grad_b_forget': 'grad_w', 'grad_qn_swa': 'grad_w', 'grad_kn_swa': 'grad_w', 'grad_sink_swa': 'grad_w', 'grad_rel_bias': 'grad_w', 'grad_qn_fox': 'grad_w', 'grad_kn_fox': 'grad_w', 'grad_g_mem': 'grad_w', 'grad_w_mem_kv': 'grad_w', 'grad_qn_mem': 'grad_w', 'grad_kn_mem': 'grad_w', 'grad_w_o_swa': 'grad_w', 'grad_w_o_fox': 'grad_w', 'grad_w_o_mem': 'grad_w', 'grad_w_out': 'grad_w', 'grad_g_mlp': 'grad_w', 'grad_w_mlp_up': 'grad_w', 'grad_w_mlp_down': 'grad_w', 'delta_g_mix': 'delta_w', 'delta_w_in': 'delta_w', 'delta_b_gate': 'delta_w', 'delta_b_forget': 'delta_w', 'delta_qn_swa': 'delta_w', 'delta_kn_swa': 'delta_w', 'delta_sink_swa': 'delta_w', 'delta_rel_bias': 'delta_w', 'delta_qn_fox': 'delta_w', 'delta_kn_fox': 'delta_w', 'delta_g_mem': 'delta_w', 'delta_w_mem_kv': 'delta_w', 'delta_qn_mem': 'delta_w', 'delta_kn_mem': 'delta_w', 'delta_w_o_swa': 'delta_w', 'delta_w_o_fox': 'delta_w', 'delta_w_o_mem': 'delta_w', 'delta_w_out': 'delta_w', 'delta_g_mlp': 'delta_w', 'delta_w_mlp_up': 'delta_w', 'delta_w_mlp_down': 'delta_w', 'new_m_g_mix': 'new_m', 'new_m_w_in': 'new_m', 'new_m_b_gate': 'new_m', 'new_m_b_forget': 'new_m', 'new_m_qn_swa': 'new_m', 'new_m_kn_swa': 'new_m', 'new_m_sink_swa': 'new_m', 'new_m_rel_bias': 'new_m', 'new_m_qn_fox': 'new_m', 'new_m_kn_fox': 'new_m', 'new_m_g_mem': 'new_m', 'new_m_w_mem_kv': 'new_m', 'new_m_qn_mem': 'new_m', 'new_m_kn_mem': 'new_m', 'new_m_w_o_swa': 'new_m', 'new_m_w_o_fox': 'new_m', 'new_m_w_o_mem': 'new_m', 'new_m_w_out': 'new_m', 'new_m_g_mlp': 'new_m', 'new_m_w_mlp_up': 'new_m', 'new_m_w_mlp_down': 'new_m', 'new_v_g_mix': 'new_v', 'new_v_w_in': 'new_v', 'new_v_b_gate': 'new_v', 'new_v_b_forget': 'new_v', 'new_v_qn_swa': 'new_v', 'new_v_kn_swa': 'new_v', 'new_v_sink_swa': 'new_v', 'new_v_rel_bias': 'new_v', 'new_v_qn_fox': 'new_v', 'new_v_kn_fox': 'new_v', 'new_v_g_mem': 'new_v', 'new_v_w_mem_kv': 'new_v', 'new_v_qn_mem': 'new_v', 'new_v_kn_mem': 'new_v', 'new_v_w_o_swa': 'new_v', 'new_v_w_o_fox': 'new_v', 'new_v_w_o_mem': 'new_v', 'new_v_w_out': 'new_v', 'new_v_g_mlp': 'new_v', 'new_v_w_mlp_up': 'new_v', 'new_v_w_mlp_down': 'new_v'}


def _forward(args):
    return _fwd_reference(*[args[k] for k in FWD_PARAMS])


def _output_shape():
    out = _jax.eval_shape(lambda: _forward(_fwd_setup_inputs(0)))
    return out.shape, out.dtype

N_MICROBATCH = 1
ADAM_LR = 0.001
ADAM_B1 = 0.9
ADAM_B2 = 0.999
ADAM_EPS = 1e-08
ADAM_WD = 0.01
ADAM_STEP = 10
PER_EXAMPLE_BATCH_AXIS = {'x': 0, 'mem': 0, 'loss_target': 0}
SHARED_INPUTS = []
_WEIGHT_DTYPES = {'g_mix': _jnp.float32, 'w_in': _jnp.float32, 'b_gate': _jnp.float32, 'b_forget': _jnp.float32, 'qn_swa': _jnp.float32, 'kn_swa': _jnp.float32, 'sink_swa': _jnp.float32, 'rel_bias': _jnp.float32, 'qn_fox': _jnp.float32, 'kn_fox': _jnp.float32, 'g_mem': _jnp.float32, 'w_mem_kv': _jnp.float32, 'qn_mem': _jnp.float32, 'kn_mem': _jnp.float32, 'w_o_swa': _jnp.float32, 'w_o_fox': _jnp.float32, 'w_o_mem': _jnp.float32, 'w_out': _jnp.float32, 'g_mlp': _jnp.float32, 'w_mlp_up': _jnp.float32, 'w_mlp_down': _jnp.float32}
MOMENT_SCALE = {'g_mix': 9.081573e-01, 'w_in': 1.201185e-01, 'b_gate': 1.309871e-01, 'b_forget': 2.993419e+01, 'qn_swa': 2.110880e+00, 'kn_swa': 2.141834e+00, 'sink_swa': 4.134512e-01, 'rel_bias': 2.844018e-01, 'qn_fox': 6.268462e+00, 'kn_fox': 6.327390e+00, 'g_mem': 4.456839e-01, 'w_mem_kv': 3.896624e-01, 'qn_mem': 7.200743e-01, 'kn_mem': 7.427269e-01, 'w_o_swa': 1.307464e-01, 'w_o_fox': 1.565293e-01, 'w_o_mem': 4.062371e-01, 'w_out': 3.233891e-01, 'g_mlp': 9.631489e+01, 'w_mlp_up': 7.022000e-01, 'w_mlp_down': 7.961900e+00}


def _to_microbatches(a, axis):
    t = _jnp.moveaxis(a, axis, 0)
    t = t.reshape((N_MICROBATCH, t.shape[0] // N_MICROBATCH) + t.shape[1:])
    return _jnp.moveaxis(t, 1, axis + 1)


def setup_inputs(seed: int = 0) -> dict:
    inp = _fwd_setup_inputs(seed)
    key = _jax.random.fold_in(_jax.random.key(seed), 7919)
    shape, _ = _output_shape()
    out = dict(inp)
    out["loss_target"] = _jax.random.normal(_jax.random.fold_in(key, 0), shape, _jnp.float32)
    for i, name in enumerate(TWIN_WEIGHTS):
        w = inp[name].astype(_jnp.float32)
        if MOMENT_SCALE is None:
            s = _jnp.sqrt(_jnp.mean(_jnp.square(w)) + 1e-30)
        else:
            s = MOMENT_SCALE[name]
        km, kv = _jax.random.split(_jax.random.fold_in(key, i + 1))
        out[name] = w
        out["m_" + name] = s * _jax.random.normal(km, w.shape, _jnp.float32)
        out["v_" + name] = (s * s) * _jax.random.uniform(kv, w.shape, _jnp.float32, 0.5, 1.5)
    if N_MICROBATCH > 1:
        for name, axis in PER_EXAMPLE_BATCH_AXIS.items():
            out[name] = _to_microbatches(out[name], axis)
    return {'x': out['x'], 'mem': out['mem'], 'g_mix': out['g_mix'], 'w_in': out['w_in'], 'b_gate': out['b_gate'], 'b_forget': out['b_forget'], 'qn_swa': out['qn_swa'], 'kn_swa': out['kn_swa'], 'sink_swa': out['sink_swa'], 'rel_bias': out['rel_bias'], 'qn_fox': out['qn_fox'], 'kn_fox': out['kn_fox'], 'g_mem': out['g_mem'], 'w_mem_kv': out['w_mem_kv'], 'qn_mem': out['qn_mem'], 'kn_mem': out['kn_mem'], 'w_o_swa': out['w_o_swa'], 'w_o_fox': out['w_o_fox'], 'w_o_mem': out['w_o_mem'], 'w_out': out['w_out'], 'g_mlp': out['g_mlp'], 'w_mlp_up': out['w_mlp_up'], 'w_mlp_down': out['w_mlp_down'], 'loss_target': out['loss_target'], 'm_g_mix': out['m_g_mix'], 'm_w_in': out['m_w_in'], 'm_b_gate': out['m_b_gate'], 'm_b_forget': out['m_b_forget'], 'm_qn_swa': out['m_qn_swa'], 'm_kn_swa': out['m_kn_swa'], 'm_sink_swa': out['m_sink_swa'], 'm_rel_bias': out['m_rel_bias'], 'm_qn_fox': out['m_qn_fox'], 'm_kn_fox': out['m_kn_fox'], 'm_g_mem': out['m_g_mem'], 'm_w_mem_kv': out['m_w_mem_kv'], 'm_qn_mem': out['m_qn_mem'], 'm_kn_mem': out['m_kn_mem'], 'm_w_o_swa': out['m_w_o_swa'], 'm_w_o_fox': out['m_w_o_fox'], 'm_w_o_mem': out['m_w_o_mem'], 'm_w_out': out['m_w_out'], 'm_g_mlp': out['m_g_mlp'], 'm_w_mlp_up': out['m_w_mlp_up'], 'm_w_mlp_down': out['m_w_mlp_down'], 'v_g_mix': out['v_g_mix'], 'v_w_in': out['v_w_in'], 'v_b_gate': out['v_b_gate'], 'v_b_forget': out['v_b_forget'], 'v_qn_swa': out['v_qn_swa'], 'v_kn_swa': out['v_kn_swa'], 'v_sink_swa': out['v_sink_swa'], 'v_rel_bias': out['v_rel_bias'], 'v_qn_fox': out['v_qn_fox'], 'v_kn_fox': out['v_kn_fox'], 'v_g_mem': out['v_g_mem'], 'v_w_mem_kv': out['v_w_mem_kv'], 'v_qn_mem': out['v_qn_mem'], 'v_kn_mem': out['v_kn_mem'], 'v_w_o_swa': out['v_w_o_swa'], 'v_w_o_fox': out['v_w_o_fox'], 'v_w_o_mem': out['v_w_o_mem'], 'v_w_out': out['v_w_out'], 'v_g_mlp': out['v_g_mlp'], 'v_w_mlp_up': out['v_w_mlp_up'], 'v_w_mlp_down': out['v_w_mlp_down']}


def _loss(weights, diff, rest, loss_target):
    with _jax.named_scope("forward"):
        args = {**rest, TWIN_DIFF_INPUT: diff, **{k: w.astype(_WEIGHT_DTYPES[k]) for k, w in weights.items()}}
        y = _forward(args)
    with _jax.named_scope("loss_head"):
        err = _jnp.square(y.astype(_jnp.float32) - loss_target)
        return 0.5 * _jnp.sum(_jnp.mean(err, axis=-1)) if err.ndim else 0.5 * err


def _adamw(w, g, m, v):
    m = ADAM_B1 * m + (1.0 - ADAM_B1) * g
    v = ADAM_B2 * v + (1.0 - ADAM_B2) * _jnp.square(g)
    m_hat = m / (1.0 - ADAM_B1 ** ADAM_STEP)
    v_hat = v / (1.0 - ADAM_B2 ** ADAM_STEP)
    delta = -ADAM_LR * (m_hat / (_jnp.sqrt(v_hat) + ADAM_EPS) + ADAM_WD * w)
    return delta, m, v


def reference(x, mem, g_mix, w_in, b_gate, b_forget, qn_swa, kn_swa, sink_swa, rel_bias, qn_fox, kn_fox, g_mem, w_mem_kv, qn_mem, kn_mem, w_o_swa, w_o_fox, w_o_mem, w_out, g_mlp, w_mlp_up, w_mlp_down, loss_target, m_g_mix, m_w_in, m_b_gate, m_b_forget, m_qn_swa, m_kn_swa, m_sink_swa, m_rel_bias, m_qn_fox, m_kn_fox, m_g_mem, m_w_mem_kv, m_qn_mem, m_kn_mem, m_w_o_swa, m_w_o_fox, m_w_o_mem, m_w_out, m_g_mlp, m_w_mlp_up, m_w_mlp_down, v_g_mix, v_w_in, v_b_gate, v_b_forget, v_qn_swa, v_kn_swa, v_sink_swa, v_rel_bias, v_qn_fox, v_kn_fox, v_g_mem, v_w_mem_kv, v_qn_mem, v_kn_mem, v_w_o_swa, v_w_o_fox, v_w_o_mem, v_w_out, v_g_mlp, v_w_mlp_up, v_w_mlp_down):
    given = dict(x=x, mem=mem, g_mix=g_mix, w_in=w_in, b_gate=b_gate, b_forget=b_forget, qn_swa=qn_swa, kn_swa=kn_swa, sink_swa=sink_swa, rel_bias=rel_bias, qn_fox=qn_fox, kn_fox=kn_fox, g_mem=g_mem, w_mem_kv=w_mem_kv, qn_mem=qn_mem, kn_mem=kn_mem, w_o_swa=w_o_swa, w_o_fox=w_o_fox, w_o_mem=w_o_mem, w_out=w_out, g_mlp=g_mlp, w_mlp_up=w_mlp_up, w_mlp_down=w_mlp_down, loss_target=loss_target, m_g_mix=m_g_mix, m_w_in=m_w_in, m_b_gate=m_b_gate, m_b_forget=m_b_forget, m_qn_swa=m_qn_swa, m_kn_swa=m_kn_swa, m_sink_swa=m_sink_swa, m_rel_bias=m_rel_bias, m_qn_fox=m_qn_fox, m_kn_fox=m_kn_fox, m_g_mem=m_g_mem, m_w_mem_kv=m_w_mem_kv, m_qn_mem=m_qn_mem, m_kn_mem=m_kn_mem, m_w_o_swa=m_w_o_swa, m_w_o_fox=m_w_o_fox, m_w_o_mem=m_w_o_mem, m_w_out=m_w_out, m_g_mlp=m_g_mlp, m_w_mlp_up=m_w_mlp_up, m_w_mlp_down=m_w_mlp_down, v_g_mix=v_g_mix, v_w_in=v_w_in, v_b_gate=v_b_gate, v_b_forget=v_b_forget, v_qn_swa=v_qn_swa, v_kn_swa=v_kn_swa, v_sink_swa=v_sink_swa, v_rel_bias=v_rel_bias, v_qn_fox=v_qn_fox, v_kn_fox=v_kn_fox, v_g_mem=v_g_mem, v_w_mem_kv=v_w_mem_kv, v_qn_mem=v_qn_mem, v_kn_mem=v_kn_mem, v_w_o_swa=v_w_o_swa, v_w_o_fox=v_w_o_fox, v_w_o_mem=v_w_o_mem, v_w_out=v_w_out, v_g_mlp=v_g_mlp, v_w_mlp_up=v_w_mlp_up, v_w_mlp_down=v_w_mlp_down)
    weights = {n: given[n] for n in TWIN_WEIGHTS}
    shared = {n: given[n] for n in SHARED_INPUTS}
    per_example = {n: given[n] for n in ['x', 'mem']}
    grad_fn = _jax.value_and_grad(_loss, argnums=(0, 1))

    def one_microbatch(ex, loss_target):
        ex = dict(ex)
        diff = ex.pop(TWIN_DIFF_INPUT)
        return grad_fn(weights, diff, {**shared, **ex}, loss_target)

    if N_MICROBATCH == 1:
        loss, (grad_w, grad_x) = one_microbatch(per_example, given["loss_target"])
    else:
        def body(carry, xs):
            loss_sum, grad_sum = carry
            l_k, (gw_k, gx_k) = one_microbatch(xs[0], xs[1])
            with _jax.named_scope("update"):
                return (loss_sum + l_k, _jax.tree.map(_jnp.add, grad_sum, gw_k)), gx_k

        init = (_jnp.zeros((), _jnp.float32), _jax.tree.map(_jnp.zeros_like, weights))
        (loss, grad_w), grad_x = _jax.lax.scan(body, init, (per_example, given["loss_target"]))
    with _jax.named_scope("update"):
        delta_w, new_m, new_v = {}, {}, {}
        for n in TWIN_WEIGHTS:
            delta_w[n], new_m[n], new_v[n] = _adamw(weights[n], grad_w[n], given["m_" + n], given["v_" + n])
    return (loss, grad_x, *[grad_w[n] for n in TWIN_WEIGHTS], *[delta_w[n] for n in TWIN_WEIGHTS],
            *[new_m[n] for n in TWIN_WEIGHTS], *[new_v[n] for n in TWIN_WEIGHTS])
```

```python
import functools
import math

import jax
import jax.numpy as jnp
from jax import lax
from jax.experimental import pallas as pl
from jax.experimental.pallas import tpu as pltpu

F32 = jnp.float32
BF16 = jnp.bfloat16

D_MODEL = 1024
N_MEM = 256
D_FF = 4096
HEAD = 64
SWA_HEADS = 8
SWA_BLOCK = 128
MEM_HEADS = 4
MEM_HEAD = 128
EPS = 1e-6
NEG = -1e30
REL_BUCKETS = 32
REL_MAX_DIST = 128

ADAM_LR = 0.001
ADAM_B1 = 0.9
ADAM_B2 = 0.999
ADAM_EPS = 1e-08
ADAM_WD = 0.01
ADAM_STEP = 10

GL0, QF0, KF0, VF0, QM0, QA0, KA0, VA0, FL0 = 0, 3072, 3584, 4096, 4608, 5120, 5632, 5760, 5888
PROJ_W = 6144
HALF_W = 3072
H_QF, H_KF, H_VF, H_QM, H_QA, H_KA, H_VA, H_FL = 0, 512, 1024, 1536, 2048, 2560, 2688, 2816

VMEM_LIMIT = 56 * 1024 * 1024
N_DEV = 8
MESH = pl.DeviceIdType.MESH

NN = (((1,), (0,)), ((), ()))
NT = (((1,), (1,)), ((), ()))
TN = (((0,), (0,)), ((), ()))


def _dot(a, b, dims=NN):
    return lax.dot_general(a, b, dims, preferred_element_type=F32)


def _params(sem):
    return pltpu.CompilerParams(dimension_semantics=sem, vmem_limit_bytes=VMEM_LIMIT)


def _full(shape):
    nd = len(shape)
    return pl.BlockSpec(shape, lambda *_: (0,) * nd)


def _sigmoid(z):
    return 1.0 / (1.0 + jnp.exp(-z))


def _group_mean(v, hd):
    if hd == 128:
        return jnp.mean(v, axis=-1, keepdims=True)
    lane = lax.broadcasted_iota(jnp.int32, v.shape, 1)
    lo = lane < HEAD
    s_lo = jnp.sum(jnp.where(lo, v, 0.0), axis=-1, keepdims=True)
    s_hi = jnp.sum(jnp.where(lo, 0.0, v), axis=-1, keepdims=True)
    return jnp.where(lo, s_lo, s_hi) * (1.0 / HEAD)


def _mm(a, b, mode, out_dtype, tm, tn, tk, name):
    if mode == "nn":
        m, k = a.shape
        n = b.shape[1]
    elif mode == "nt":
        m, k = a.shape
        n = b.shape[0]
    else:
        k, m = a.shape
        n = b.shape[1]
    tm, tn, tk = min(tm, m), min(tn, n), min(tk, k)
    nk = k // tk
    dims = {"nn": NN, "nt": NT, "tn": TN}[mode]
    a_spec = pl.BlockSpec((tk, tm), lambda i, j, kk: (kk, i)) if mode == "tn" else pl.BlockSpec((tm, tk), lambda i, j, kk: (i, kk))
    b_spec = pl.BlockSpec((tn, tk), lambda i, j, kk: (j, kk)) if mode == "nt" else pl.BlockSpec((tk, tn), lambda i, j, kk: (kk, j))

    def body(a_ref, b_ref, o_ref, acc_ref):
        prod = _dot(a_ref[...].astype(BF16), b_ref[...].astype(BF16), dims)
        if nk == 1:
            o_ref[...] = prod.astype(o_ref.dtype)
        else:
            kk = pl.program_id(2)

            @pl.when(kk == 0)
            def _():
                acc_ref[...] = prod

            @pl.when(kk > 0)
            def _():
                acc_ref[...] += prod

            @pl.when(kk == nk - 1)
            def _():
                o_ref[...] = acc_ref[...].astype(o_ref.dtype)

    return pl.pallas_call(
        body, name=name, grid=(m // tm, n // tn, nk),
        in_specs=[a_spec, b_spec],
        out_specs=pl.BlockSpec((tm, tn), lambda i, j, kk: (i, j)),
        out_shape=jax.ShapeDtypeStruct((m, n), out_dtype),
        scratch_shapes=[pltpu.VMEM((tm, tn), F32)],
        compiler_params=_params(("parallel", "parallel", "arbitrary")),
    )(a, b)


def _rms_fwd(x, g, name):
    s, d = x.shape
    tm = min(512, s)

    def body(x_ref, g_ref, h_ref):
        xv = x_ref[...]
        r = lax.rsqrt(jnp.mean(xv * xv, axis=-1, keepdims=True) + EPS)
        h_ref[...] = (xv * r * g_ref[...]).astype(BF16)

    return pl.pallas_call(
        body, name=name, grid=(s // tm,),
        in_specs=[pl.BlockSpec((tm, d), lambda i: (i, 0)), _full((1, d))],
        out_specs=pl.BlockSpec((tm, d), lambda i: (i, 0)),
        out_shape=jax.ShapeDtypeStruct((s, d), BF16),
        compiler_params=_params(("parallel",)),
    )(x, g)


def _proj_post(proj, gq_fox, gk_fox, gq_mem, gq_swa, gk_swa):
    s = proj.shape[0]
    tm = min(256, s)

    def body(p_ref, gqf, gkf, gqm, gqa, gka, qf_ref, kf_ref, vf_ref, qm_ref, qa_ref, ka_ref, va_ref):
        def norm(off, width, hd, g_ref, o_ref):
            for b in range(width // 128):
                v = p_ref[:, off + b * 128: off + (b + 1) * 128]
                r = lax.rsqrt(_group_mean(v * v, hd) + EPS)
                o_ref[:, b * 128:(b + 1) * 128] = (v * r * g_ref[...]).astype(BF16)

        norm(H_QF, 512, HEAD, gqf, qf_ref)
        norm(H_KF, 512, HEAD, gkf, kf_ref)
        vf_ref[...] = p_ref[:, H_VF:H_VF + 512].astype(BF16)
        norm(H_QM, 512, MEM_HEAD, gqm, qm_ref)
        norm(H_QA, 512, HEAD, gqa, qa_ref)
        norm(H_KA, 128, HEAD, gka, ka_ref)
        va_ref[...] = p_ref[:, H_VA:H_VA + 128].astype(BF16)

    g_spec = _full((1, 128))
    o512 = pl.BlockSpec((tm, 512), lambda i: (i, 0))
    o128 = pl.BlockSpec((tm, 128), lambda i: (i, 0))
    s512 = jax.ShapeDtypeStruct((s, 512), BF16)
    s128 = jax.ShapeDtypeStruct((s, 128), BF16)
    return pl.pallas_call(
        body, name="proj_post", grid=(s // tm,),
        in_specs=[pl.BlockSpec((tm, HALF_W), lambda i: (i, 1)), g_spec, g_spec, g_spec, g_spec, g_spec],
        out_specs=[o512, o512, o512, o512, o512, o128, o128],
        out_shape=[s512, s512, s512, s512, s512, s128, s128],
        compiler_params=_params(("parallel",)),
    )(proj, gq_fox, gk_fox, gq_mem, gq_swa, gk_swa)


def _tri(n, lower):
    r = lax.broadcasted_iota(jnp.int32, (n, n), 0)
    c = lax.broadcasted_iota(jnp.int32, (n, n), 1)
    return jnp.where((c <= r) if lower else (c >= r), 1.0, 0.0).astype(F32)


def _fox_gate_fwd(proj, b_forget128):
    s = proj.shape[0]
    tm = min(512, s)

    def body(p_ref, b_ref, cc_ref, cr_ref, carry_ref):
        i = pl.program_id(0)

        @pl.when(i == 0)
        def _():
            carry_ref[...] = jnp.zeros_like(carry_ref)

        z = p_ref[...] + b_ref[...]
        logf = jnp.minimum(z, 0.0) - jnp.log(1.0 + jnp.exp(-jnp.abs(z)))
        c = jnp.dot(_tri(tm, True), logf, precision=lax.Precision.HIGHEST, preferred_element_type=F32) + carry_ref[...]
        carry_ref[...] = c[tm - 1:tm, :]
        for hp in range(4):
            ch = c if hp == 0 else pltpu.roll(c, 128 - 2 * hp, 1)
            cc_ref[hp] = ch
            cr_ref[hp] = ch.T[0:8, :]

    return pl.pallas_call(
        body, name="fox_gate_fwd", grid=(s // tm,),
        in_specs=[pl.BlockSpec((tm, 128), lambda i: (i, FL0 // 128)), _full((1, 128))],
        out_specs=[pl.BlockSpec((4, tm, 128), lambda i: (0, i, 0)), pl.BlockSpec((4, 8, tm), lambda i: (0, 0, i))],
        out_shape=[jax.ShapeDtypeStruct((4, s, 128), F32), jax.ShapeDtypeStruct((4, 8, s), F32)],
        scratch_shapes=[pltpu.VMEM((1, 128), F32)],
        compiler_params=_params(("arbitrary",)),
    )(proj, b_forget128)


def _memkv_fwd(mem, g_mem, w_kv, kn_mem):
    m = mem.shape[0]

    def body(mem_ref, g_ref, w_ref, kn_ref, memn_ref, kv_ref, mk_ref, mv_ref):
        xv = mem_ref[...]
        r = lax.rsqrt(jnp.mean(xv * xv, axis=-1, keepdims=True) + EPS)
        mn = (xv * r * g_ref[...]).astype(BF16)
        memn_ref[...] = mn
        kv = _dot(mn, w_ref[...])
        kv_ref[...] = kv
        for h in range(MEM_HEADS):
            v = kv[:, h * 128:(h + 1) * 128]
            rr = lax.rsqrt(jnp.mean(v * v, axis=-1, keepdims=True) + EPS)
            mk_ref[:, h * 128:(h + 1) * 128] = (v * rr * kn_ref[...]).astype(BF16)
        mv_ref[...] = kv[:, 512:1024].astype(BF16)

    return pl.pallas_call(
        body, name="memkv_fwd",
        out_shape=[jax.ShapeDtypeStruct((m, D_MODEL), BF16), jax.ShapeDtypeStruct((m, 1024), F32),
                   jax.ShapeDtypeStruct((m, 512), BF16), jax.ShapeDtypeStruct((m, 512), BF16)],
        compiler_params=pltpu.CompilerParams(vmem_limit_bytes=VMEM_LIMIT),
    )(mem, g_mem, w_kv, kn_mem)


def _swa_valid(n):
    row = lax.broadcasted_iota(jnp.int32, (SWA_BLOCK, 2 * SWA_BLOCK), 0)
    col = lax.broadcasted_iota(jnp.int32, (SWA_BLOCK, 2 * SWA_BLOCK), 1)
    dist = row + SWA_BLOCK - col
    return (dist >= 0) & (dist < SWA_BLOCK) & ((col >= SWA_BLOCK) | (n > 0))


def _swa_fwd(qa, kp, vp, bias, sink):
    s = qa.shape[0]
    nb = s // SWA_BLOCK

    def body(sink_ref, q_ref, kp_ref, vp_ref, bias_ref, o_ref):
        n = pl.program_id(0)
        start = pl.multiple_of(n * SWA_BLOCK, SWA_BLOCK)
        k2 = kp_ref[pl.ds(start, 2 * SWA_BLOCK), :]
        v2 = vp_ref[pl.ds(start, 2 * SWA_BLOCK), :]
        valid = _swa_valid(n)
        for h in range(SWA_HEADS):
            kv = h // 4
            qh = q_ref[:, h * HEAD:(h + 1) * HEAD]
            kh = k2[:, kv * HEAD:(kv + 1) * HEAD]
            vh = v2[:, kv * HEAD:(kv + 1) * HEAD]
            sc = _dot(qh, kh, NT) * 0.125 + bias_ref[h]
            sc = jnp.where(valid, sc, NEG)
            sk = sink_ref[h]
            mx = jnp.maximum(jnp.max(sc, axis=-1, keepdims=True), sk)
            p = jnp.exp(sc - mx)
            den = jnp.sum(p, axis=-1, keepdims=True) + jnp.exp(sk - mx)
            p = p / den
            o_ref[:, h * HEAD:(h + 1) * HEAD] = _dot(p.astype(BF16), vh).astype(BF16)

    return pl.pallas_call(
        body, name="swa_fwd", grid=(nb,),
        in_specs=[pl.BlockSpec(memory_space=pltpu.SMEM),
                  pl.BlockSpec((SWA_BLOCK, 512), lambda n: (n, 0)),
                  _full(kp.shape), _full(vp.shape), _full(bias.shape)],
        out_specs=pl.BlockSpec((SWA_BLOCK, 512), lambda n: (n, 0)),
        out_shape=jax.ShapeDtypeStruct((s, 512), BF16),
        compiler_params=_params(("parallel",)),
    )(sink, qa, kp, vp, bias)


def _head_mask(e):
    lane = lax.broadcasted_iota(jnp.int32, (1, 128), 1)
    return (lane >= e * HEAD) & (lane < (e + 1) * HEAD)


FOX_T = 256


def _fox_fwd(q, k, v, cc4, cr4):
    s = q.shape[0]
    t = min(FOX_T, s)
    nq = s // t

    def body(q_ref, k_ref, v_ref, cc_ref, cr_ref, o_ref, st_ref):
        i = pl.program_id(1)
        qv = q_ref[...]
        row = lax.broadcasted_iota(jnp.int32, (t, t), 0)
        col = lax.broadcasted_iota(jnp.int32, (t, t), 1)
        lane = lax.broadcasted_iota(jnp.int32, (t, 128), 1)
        o_acc = jnp.zeros((t, 128), F32)
        st = jnp.zeros((t, 128), F32)
        for e in range(2):
            msk = _head_mask(e)
            qe = jnp.where(msk, qv, jnp.zeros_like(qv))
            cq = cc_ref[0, :, e:e + 1]

            def step(j, carry, masked, qe=qe, cq=cq, msk=msk, e=e):
                m, l, acc = carry
                ks = pl.ds(pl.multiple_of(j * t, t), t)
                kj = k_ref[ks, :]
                vj = v_ref[ks, :]
                ck = cr_ref[0, e:e + 1, ks]
                sc = (_dot(qe, kj, NT) * 0.125 + cq) - ck
                if masked:
                    sc = jnp.where(col <= row, sc, NEG)
                m_new = jnp.maximum(m, jnp.max(sc, axis=-1, keepdims=True))
                alpha = jnp.exp(m - m_new)
                p = jnp.exp(sc - m_new)
                l = alpha * l + jnp.sum(p, axis=-1, keepdims=True)
                ve = jnp.where(msk, vj, jnp.zeros_like(vj))
                acc = alpha * acc + _dot(p.astype(BF16), ve)
                return m_new, l, acc

            carry = (jnp.full((t, 1), NEG, F32), jnp.zeros((t, 1), F32), jnp.zeros((t, 128), F32))
            carry = lax.fori_loop(0, i, functools.partial(step, masked=False), carry)
            m, l, acc = step(i, carry, True)
            o_acc = o_acc + acc / l
            st = jnp.where(lane == e, m + jnp.log(l), st)
        o_ref[...] = o_acc.astype(BF16)
        st_ref[0] = st

    return pl.pallas_call(
        body, name="fox_fwd", grid=(4, nq),
        in_specs=[pl.BlockSpec((t, 128), lambda hp, i: (i, hp)),
                  pl.BlockSpec((s, 128), lambda hp, i: (0, hp)),
                  pl.BlockSpec((s, 128), lambda hp, i: (0, hp)),
                  pl.BlockSpec((1, t, 128), lambda hp, i: (hp, i, 0)),
                  pl.BlockSpec((1, 8, s), lambda hp, i: (hp, 0, 0))],
        out_specs=[pl.BlockSpec((t, 128), lambda hp, i: (i, hp)),
                   pl.BlockSpec((1, t, 128), lambda hp, i: (hp, i, 0))],
        out_shape=[jax.ShapeDtypeStruct((s, 512), BF16), jax.ShapeDtypeStruct((4, s, 128), F32)],
        compiler_params=_params(("parallel", "parallel")),
    )(q, k, v, cc4, cr4)


MEM_SCALE = MEM_HEAD ** -0.5


def _mem_fwd(qm, mk, mv):
    s = qm.shape[0]
    tq = min(512, s)

    def body(q_ref, mk_ref, mv_ref, o_ref):
        for h in range(MEM_HEADS):
            hs = slice(h * 128, (h + 1) * 128)
            sc = _dot(q_ref[:, hs], mk_ref[:, hs], NT) * MEM_SCALE
            mx = jnp.max(sc, axis=-1, keepdims=True)
            p = jnp.exp(sc - mx)
            p = p / jnp.sum(p, axis=-1, keepdims=True)
            o_ref[:, hs] = _dot(p.astype(BF16), mv_ref[:, hs]).astype(BF16)

    return pl.pallas_call(
        body, name="mem_fwd", grid=(s // tq,),
        in_specs=[pl.BlockSpec((tq, 512), lambda i: (i, 0)), _full(mk.shape), _full(mv.shape)],
        out_specs=pl.BlockSpec((tq, 512), lambda i: (i, 0)),
        out_shape=jax.ShapeDtypeStruct((s, 512), BF16),
        compiler_params=_params(("parallel",)),
    )(qm, mk, mv)


def _merge_fwd(x, oa, of, om, proj, b_gate, wa, wf, wm, w_out, g_mlp):
    s = x.shape[0]
    tm = min(256, s)

    def body(x_ref, oa_ref, of_ref, om_ref, gl_ref, bg_ref, wa_ref, wf_ref, wm_ref, wo_ref, g_ref, x1_ref, hm_ref, mg_ref):
        merged = None
        for b, (o_ref, w_ref) in enumerate(((oa_ref, wa_ref), (of_ref, wf_ref), (om_ref, wm_ref))):
            cs = slice(b * D_MODEL, (b + 1) * D_MODEL)
            y = _dot(o_ref[...], w_ref[...])
            t = _sigmoid(gl_ref[:, cs] + bg_ref[:, cs]) * y
            merged = t if merged is None else merged + t
        mb = merged.astype(BF16)
        mg_ref[...] = mb
        x1 = x_ref[...] + _dot(mb, wo_ref[...])
        x1_ref[...] = x1
        r = lax.rsqrt(jnp.mean(x1 * x1, axis=-1, keepdims=True) + EPS)
        hm_ref[...] = (x1 * r * g_ref[...]).astype(BF16)

    row = lambda w: pl.BlockSpec((tm, w), lambda i: (i, 0))
    return pl.pallas_call(
        body, name="merge_fwd", grid=(s // tm,),
        in_specs=[row(D_MODEL), row(512), row(512), row(512), row(HALF_W), _full((1, HALF_W)),
                  _full(wa.shape), _full(wf.shape), _full(wm.shape), _full(w_out.shape), _full((1, D_MODEL))],
        out_specs=[row(D_MODEL), row(D_MODEL), row(D_MODEL)],
        out_shape=[jax.ShapeDtypeStruct((s, D_MODEL), F32), jax.ShapeDtypeStruct((s, D_MODEL), BF16),
                   jax.ShapeDtypeStruct((s, D_MODEL), BF16)],
        compiler_params=_params(("parallel",)),
    )(x, oa, of, om, proj, b_gate, wa, wf, wm, w_out, g_mlp)


def _mlp_up(hm, w_up):
    s = hm.shape[0]
    tm, tn = min(512, s), 1024

    def body(h_ref, w_ref, a_ref, u_ref):
        a = _dot(h_ref[...], w_ref[...])
        a_ref[...] = a
        r = jnp.maximum(a, 0.0)
        u_ref[...] = (r * r).astype(BF16)

    return pl.pallas_call(
        body, name="mlp_up", grid=(s // tm, D_FF // tn),
        in_specs=[pl.BlockSpec((tm, D_MODEL), lambda i, j: (i, 0)), pl.BlockSpec((D_MODEL, tn), lambda i, j: (0, j))],
        out_specs=[pl.BlockSpec((tm, tn), lambda i, j: (i, j)), pl.BlockSpec((tm, tn), lambda i, j: (i, j))],
        out_shape=[jax.ShapeDtypeStruct((s, D_FF), F32), jax.ShapeDtypeStruct((s, D_FF), BF16)],
        compiler_params=_params(("parallel", "parallel")),
    )(hm, w_up)


def _mlp_down_loss(u, w_down, x1, target):
    s = u.shape[0]
    tm = min(256, s)

    def body(u_ref, w_ref, x1_ref, t_ref, dy_ref, loss_ref):
        i = pl.program_id(0)

        @pl.when(i == 0)
        def _():
            loss_ref[...] = jnp.zeros_like(loss_ref)

        y = x1_ref[...] + _dot(u_ref[...], w_ref[...])
        err = y - t_ref[...]
        dy_ref[...] = err * (1.0 / D_MODEL)
        part = jnp.sum(jnp.sum(err * err, axis=-1, keepdims=True) * (1.0 / D_MODEL), axis=0, keepdims=True)
        loss_ref[...] += 0.5 * part

    row = pl.BlockSpec((tm, D_MODEL), lambda i: (i, 0))
    return pl.pallas_call(
        body, name="mlp_down_loss", grid=(s // tm,),
        in_specs=[pl.BlockSpec((tm, D_FF), lambda i: (i, 0)), _full(w_down.shape), row, row],
        out_specs=[row, _full((1, 1))],
        out_shape=[jax.ShapeDtypeStruct((s, D_MODEL), F32), jax.ShapeDtypeStruct((1, 1), F32)],
        compiler_params=_params(("arbitrary",)),
    )(u, w_down, x1, target)


def _mlp_bwd_act(dy, w_down, a):
    s = dy.shape[0]
    tm, tn = min(512, s), 1024

    def body(dy_ref, w_ref, a_ref, da_ref):
        du = _dot(dy_ref[...].astype(BF16), w_ref[...], NT)
        da_ref[...] = (du * (2.0 * jnp.maximum(a_ref[...], 0.0))).astype(BF16)

    return pl.pallas_call(
        body, name="mlp_bwd_act", grid=(s // tm, D_FF // tn),
        in_specs=[pl.BlockSpec((tm, D_MODEL), lambda i, j: (i, 0)), pl.BlockSpec((tn, D_MODEL), lambda i, j: (j, 0)),
                  pl.BlockSpec((tm, tn), lambda i, j: (i, j))],
        out_specs=pl.BlockSpec((tm, tn), lambda i, j: (i, j)),
        out_shape=jax.ShapeDtypeStruct((s, D_FF), BF16),
        compiler_params=_params(("parallel", "parallel")),
    )(dy, w_down, a)


def _rms_bwd(xv, g, dh, skip):
    r = lax.rsqrt(jnp.mean(xv * xv, axis=-1, keepdims=True) + EPS)
    n = xv * r
    dn = dh * g
    dx = skip + r * (dn - n * jnp.mean(dn * n, axis=-1, keepdims=True))
    return dx, jnp.sum(dh * n, axis=0, keepdims=True)


def _mlp_bwd_x(da, w_up, x1, dy, g_mlp):
    s = da.shape[0]
    tm = min(256, s)

    def body(da_ref, w_ref, x1_ref, dy_ref, g_ref, dx1_ref, dg_ref):
        i = pl.program_id(0)

        @pl.when(i == 0)
        def _():
            dg_ref[...] = jnp.zeros_like(dg_ref)

        dhm = _dot(da_ref[...], w_ref[...], NT)
        dx, dg = _rms_bwd(x1_ref[...], g_ref[...], dhm, dy_ref[...])
        dx1_ref[...] = dx
        dg_ref[...] += dg

    row = pl.BlockSpec((tm, D_MODEL), lambda i: (i, 0))
    return pl.pallas_call(
        body, name="mlp_bwd_x", grid=(s // tm,),
        in_specs=[pl.BlockSpec((tm, D_FF), lambda i: (i, 0)), _full(w_up.shape), row, row, _full((1, D_MODEL))],
        out_specs=[row, _full((1, D_MODEL))],
        out_shape=[jax.ShapeDtypeStruct((s, D_MODEL), F32), jax.ShapeDtypeStruct((1, D_MODEL), F32)],
        compiler_params=_params(("arbitrary",)),
    )(da, w_up, x1, dy, g_mlp)


def _merge_bwd(dx1, oa, of, om, proj, b_gate, wa, wf, wm, w_out):
    s = dx1.shape[0]
    tm = min(256, s)

    def body(dx1_ref, oa_ref, of_ref, om_ref, gl_ref, bg_ref, wa_ref, wf_ref, wm_ref, wo_ref,
             dp_ref, doa_ref, dof_ref, dom_ref, dya_ref, dyf_ref, dym_ref, dbg_ref):
        i = pl.program_id(0)

        @pl.when(i == 0)
        def _():
            dbg_ref[...] = jnp.zeros_like(dbg_ref)

        dmerged = _dot(dx1_ref[...].astype(BF16), wo_ref[...], NT)
        branches = ((oa_ref, wa_ref, doa_ref, dya_ref), (of_ref, wf_ref, dof_ref, dyf_ref), (om_ref, wm_ref, dom_ref, dym_ref))
        for b, (o_ref, w_ref, do_ref, dyb_ref) in enumerate(branches):
            cs = slice(b * D_MODEL, (b + 1) * D_MODEL)
            y = _dot(o_ref[...], w_ref[...])
            g = _sigmoid(gl_ref[:, cs] + bg_ref[:, cs])
            dz = (dmerged * y) * g * (1.0 - g)
            dp_ref[:, cs] = dz.astype(BF16)
            dbg_ref[:, cs] += jnp.sum(dz, axis=0, keepdims=True)
            dyb = (dmerged * g).astype(BF16)
            dyb_ref[...] = dyb
            do_ref[...] = _dot(dyb, w_ref[...], NT).astype(BF16)

    row = lambda w: pl.BlockSpec((tm, w), lambda i: (i, 0))
    sd = lambda w: jax.ShapeDtypeStruct((s, w), BF16)
    return pl.pallas_call(
        body, name="merge_bwd", grid=(s // tm,),
        in_specs=[row(D_MODEL), row(512), row(512), row(512), row(HALF_W), _full((1, HALF_W)),
                  _full(wa.shape), _full(wf.shape), _full(wm.shape), _full(w_out.shape)],
        out_specs=[row(HALF_W), row(512), row(512), row(512), row(D_MODEL), row(D_MODEL), row(D_MODEL), _full((1, HALF_W))],
        out_shape=[sd(PROJ_W), sd(512), sd(512), sd(512), sd(D_MODEL), sd(D_MODEL), sd(D_MODEL),
                   jax.ShapeDtypeStruct((1, HALF_W), F32)],
        compiler_params=_params(("arbitrary",)),
    )(dx1, oa, of, om, proj, b_gate, wa, wf, wm, w_out)


def _swa_bwd(qa, kp, vp, bias, sink, doa):
    s = qa.shape[0]
    nb = s // SWA_BLOCK

    def body(sink_ref, q_ref, kp_ref, vp_ref, bias_ref, do_ref, dq_ref, dkp_ref, dvp_ref, dbias_ref, dsink_ref, sk_acc):
        n = pl.program_id(0)

        @pl.when(n == 0)
        def _():
            dkp_ref[...] = jnp.zeros_like(dkp_ref)
            dvp_ref[...] = jnp.zeros_like(dvp_ref)
            dbias_ref[...] = jnp.zeros_like(dbias_ref)
            sk_acc[...] = jnp.zeros_like(sk_acc)

        start = pl.multiple_of(n * SWA_BLOCK, SWA_BLOCK)
        win = pl.ds(start, 2 * SWA_BLOCK)
        k2 = kp_ref[win, :]
        v2 = vp_ref[win, :]
        valid = _swa_valid(n)
        for kv in range(2):
            hs_kv = slice(kv * HEAD, (kv + 1) * HEAD)
            kh = k2[:, hs_kv]
            vh = v2[:, hs_kv]
            dk2 = jnp.zeros((2 * SWA_BLOCK, HEAD), F32)
            dv2 = jnp.zeros((2 * SWA_BLOCK, HEAD), F32)
            for g in range(4):
                h = kv * 4 + g
                hs = slice(h * HEAD, (h + 1) * HEAD)
                qh = q_ref[:, hs]
                doh = do_ref[:, hs]
                sc = _dot(qh, kh, NT) * 0.125 + bias_ref[h]
                sc = jnp.where(valid, sc, NEG)
                sk = sink_ref[h]
                mx = jnp.maximum(jnp.max(sc, axis=-1, keepdims=True), sk)
                p = jnp.exp(sc - mx)
                esk = jnp.exp(sk - mx)
                den = jnp.sum(p, axis=-1, keepdims=True) + esk
                p = p / den
                dp = _dot(doh, vh, NT)
                delta = jnp.sum(p * dp, axis=-1, keepdims=True)
                ds = p * (dp - delta)
                sk_acc[:, h:h + 1] += -(esk / den) * delta
                dbias_ref[h] += ds
                dsb = (ds * 0.125).astype(BF16)
                dq_ref[:, hs] = _dot(dsb, kh)
                dk2 = dk2 + _dot(dsb, qh, TN)
                dv2 = dv2 + _dot(p.astype(BF16), doh, TN)
            dkp_ref[win, hs_kv] += dk2
            dvp_ref[win, hs_kv] += dv2

        @pl.when(n == nb - 1)
        def _():
            dsink_ref[...] = jnp.sum(sk_acc[...], axis=0, keepdims=True)

    return pl.pallas_call(
        body, name="swa_bwd", grid=(nb,),
        in_specs=[pl.BlockSpec(memory_space=pltpu.SMEM),
                  pl.BlockSpec((SWA_BLOCK, 512), lambda n: (n, 0)),
                  _full(kp.shape), _full(vp.shape), _full(bias.shape),
                  pl.BlockSpec((SWA_BLOCK, 512), lambda n: (n, 0))],
        out_specs=[pl.BlockSpec((SWA_BLOCK, 512), lambda n: (n, 0)), _full(kp.shape), _full(vp.shape),
                   _full(bias.shape), _full((1, 128))],
        out_shape=[jax.ShapeDtypeStruct((s, 512), F32), jax.ShapeDtypeStruct(kp.shape, F32),
                   jax.ShapeDtypeStruct(vp.shape, F32), jax.ShapeDtypeStruct(bias.shape, F32),
                   jax.ShapeDtypeStruct((1, 128), F32)],
        scratch_shapes=[pltpu.VMEM((SWA_BLOCK, 128), F32)],
        compiler_params=_params(("arbitrary",)),
    )(sink, qa, kp, vp, bias, doa)


def _fox_bwd(q, k, v, do, o, cc4, cr4, lse4):
    s = q.shape[0]
    t = min(FOX_T, s)
    nq = s // t

    def body(q_ref, k_ref, v_ref, do_ref, o_ref, cc_ref, cr_ref, lse_ref,
             dq_ref, dk_ref, dv_ref, dcq_ref, dck_ref, delta_ref, dk_acc, dv_acc):
        j = pl.program_id(1)
        lane = lax.broadcasted_iota(jnp.int32, (t, 128), 1)

        @pl.when(j == 0)
        def _():
            dq_ref[...] = jnp.zeros_like(dq_ref)
            dcq_ref[...] = jnp.zeros_like(dcq_ref)

            def dl(i, c):
                rows = pl.ds(pl.multiple_of(i * t, t), t)
                pr = do_ref[rows, :].astype(F32) * o_ref[rows, :].astype(F32)
                lo = lane < HEAD
                d0 = jnp.sum(jnp.where(lo, pr, 0.0), axis=-1, keepdims=True)
                d1 = jnp.sum(jnp.where(lo, 0.0, pr), axis=-1, keepdims=True)
                delta_ref[rows, :] = jnp.where(lane == 0, d0, jnp.where(lane == 1, d1, 0.0))
                return c

            lax.fori_loop(0, nq, dl, 0)

        kj = k_ref[...]
        vj = v_ref[...]
        ks = pl.ds(pl.multiple_of(j * t, t), t)
        row = lax.broadcasted_iota(jnp.int32, (t, t), 0)
        col = lax.broadcasted_iota(jnp.int32, (t, t), 1)
        dk_acc[...] = jnp.zeros_like(dk_acc)
        dv_acc[...] = jnp.zeros_like(dv_acc)
        dck_rows = []
        for e in range(2):
            msk = _head_mask(e)
            ke = jnp.where(msk, kj, jnp.zeros_like(kj))
            ck = cr_ref[0, e:e + 1, ks]

            def step(i, dck, masked, msk=msk, ke=ke, ck=ck, e=e):
                rows = pl.ds(pl.multiple_of(i * t, t), t)
                qi = q_ref[rows, :]
                doi = do_ref[rows, :]
                qe = jnp.where(msk, qi, jnp.zeros_like(qi))
                doe = jnp.where(msk, doi, jnp.zeros_like(doi))
                cq = cc_ref[0, rows, e:e + 1]
                lse = lse_ref[0, rows, e:e + 1]
                dlt = delta_ref[rows, e:e + 1]
                sc = (_dot(qe, kj, NT) * 0.125 + cq) - ck
                if masked:
                    sc = jnp.where(col <= row, sc, NEG)
                p = jnp.exp(sc - lse)
                dp = _dot(doe, vj, NT)
                ds = p * (dp - dlt)
                dv_acc[...] += _dot(p.astype(BF16), doe, TN)
                dsb = (ds * 0.125).astype(BF16)
                dk_acc[...] += _dot(dsb, qe, TN)
                dq_ref[rows, :] += _dot(dsb, ke)
                dcq_ref[0, rows, :] += jnp.where(lane == e, jnp.sum(ds, axis=-1, keepdims=True), 0.0)
                return dck + jnp.sum(ds, axis=0, keepdims=True)

            dck = step(j, jnp.zeros((1, t), F32), True)
            dck = lax.fori_loop(j + 1, nq, functools.partial(step, masked=False), dck)
            dck_rows.append(dck)
        dk_ref[...] = dk_acc[...]
        dv_ref[...] = dv_acc[...]
        r8 = lax.broadcasted_iota(jnp.int32, (8, t), 0)
        dck_ref[0] = jnp.where(r8 == 0, dck_rows[0], jnp.where(r8 == 1, dck_rows[1], 0.0))

    res = lambda: pl.BlockSpec((s, 128), lambda hp, j: (0, hp))
    blk = lambda: pl.BlockSpec((t, 128), lambda hp, j: (j, hp))
    st = lambda: pl.BlockSpec((1, s, 128), lambda hp, j: (hp, 0, 0))
    return pl.pallas_call(
        body, name="fox_bwd", grid=(4, nq),
        in_specs=[res(), blk(), blk(), res(), res(), st(), pl.BlockSpec((1, 8, s), lambda hp, j: (hp, 0, 0)), st()],
        out_specs=[res(), blk(), blk(), st(), pl.BlockSpec((1, 8, t), lambda hp, j: (hp, 0, j))],
        out_shape=[jax.ShapeDtypeStruct((s, 512), F32), jax.ShapeDtypeStruct((s, 512), F32),
                   jax.ShapeDtypeStruct((s, 512), F32), jax.ShapeDtypeStruct((4, s, 128), F32),
                   jax.ShapeDtypeStruct((4, 8, s), F32)],
        scratch_shapes=[pltpu.VMEM((s, 128), F32), pltpu.VMEM((t, 128), F32), pltpu.VMEM((t, 128), F32)],
        compiler_params=_params(("arbitrary", "arbitrary")),
    )(q, k, v, do, o, cc4, cr4, lse4)


def _mem_bwd(qm, mk, mv, dom):
    s = qm.shape[0]
    tq = min(512, s)

    def body(q_ref, mk_ref, mv_ref, do_ref, dq_ref, dmk_ref, dmv_ref):
        i = pl.program_id(0)

        @pl.when(i == 0)
        def _():
            dmk_ref[...] = jnp.zeros_like(dmk_ref)
            dmv_ref[...] = jnp.zeros_like(dmv_ref)

        for h in range(MEM_HEADS):
            hs = slice(h * 128, (h + 1) * 128)
            qh = q_ref[:, hs]
            doh = do_ref[:, hs]
            sc = _dot(qh, mk_ref[:, hs], NT) * MEM_SCALE
            mx = jnp.max(sc, axis=-1, keepdims=True)
            p = jnp.exp(sc - mx)
            p = p / jnp.sum(p, axis=-1, keepdims=True)
            dp = _dot(doh, mv_ref[:, hs], NT)
            ds = p * (dp - jnp.sum(p * dp, axis=-1, keepdims=True))
            dsb = (ds * MEM_SCALE).astype(BF16)
            dq_ref[:, hs] = _dot(dsb, mk_ref[:, hs])
            dmk_ref[:, hs] += _dot(dsb, qh, TN)
            dmv_ref[:, hs] += _dot(p.astype(BF16), doh, TN)

    return pl.pallas_call(
        body, name="mem_bwd", grid=(s // tq,),
        in_specs=[pl.BlockSpec((tq, 512), lambda i: (i, 0)), _full(mk.shape), _full(mv.shape),
                  pl.BlockSpec((tq, 512), lambda i: (i, 0))],
        out_specs=[pl.BlockSpec((tq, 512), lambda i: (i, 0)), _full(mk.shape), _full(mv.shape)],
        out_shape=[jax.ShapeDtypeStruct((s, 512), F32), jax.ShapeDtypeStruct(mk.shape, F32),
                   jax.ShapeDtypeStruct(mv.shape, F32)],
        compiler_params=_params(("arbitrary",)),
    )(qm, mk, mv, dom)


def _memkv_bwd(dmk, dmv, kv_raw, kn_mem, mem, g_mem, mem_n, w_kv):
    def body(dmk_ref, dmv_ref, kv_ref, kn_ref, mem_ref, g_ref, mn_ref, w_ref, dw_ref, dkn_ref, dg_ref, dkv_ref):
        dkn = jnp.zeros((1, 128), F32)
        for h in range(MEM_HEADS):
            hs = slice(h * 128, (h + 1) * 128)
            v = kv_ref[:, hs]
            r = lax.rsqrt(jnp.mean(v * v, axis=-1, keepdims=True) + EPS)
            n = v * r
            dn = dmk_ref[:, hs]
            dkn = dkn + jnp.sum(dn * n, axis=0, keepdims=True)
            dng = dn * kn_ref[...]
            dkv_ref[:, hs] = (r * (dng - n * jnp.mean(dng * n, axis=-1, keepdims=True))).astype(BF16)
        dkv_ref[:, 512:1024] = dmv_ref[...].astype(BF16)
        dkn_ref[...] = dkn
        dkv = dkv_ref[...]
        dw_ref[...] = _dot(mn_ref[...], dkv, TN)
        dmn = _dot(dkv, w_ref[...], NT)
        xv = mem_ref[...]
        r = lax.rsqrt(jnp.mean(xv * xv, axis=-1, keepdims=True) + EPS)
        dg_ref[...] = jnp.sum(dmn * (xv * r), axis=0, keepdims=True)

    m = mem.shape[0]
    return pl.pallas_call(
        body, name="memkv_bwd",
        out_shape=[jax.ShapeDtypeStruct((D_MODEL, 1024), F32), jax.ShapeDtypeStruct((1, 128), F32),
                   jax.ShapeDtypeStruct((1, D_MODEL), F32)],
        scratch_shapes=[pltpu.VMEM((m, 1024), BF16)],
        compiler_params=pltpu.CompilerParams(vmem_limit_bytes=VMEM_LIMIT),
    )(dmk, dmv, kv_raw, kn_mem, mem, g_mem, mem_n, w_kv)


def _fox_gate_bwd(dc, proj, b_forget128):
    s = dc.shape[0]
    tm = min(512, s)
    nt = s // tm

    def body(dc_ref, p_ref, b_ref, dfl_ref, db_ref, carry_ref):
        i = pl.program_id(0)

        @pl.when(i == 0)
        def _():
            carry_ref[...] = jnp.zeros_like(carry_ref)
            db_ref[...] = jnp.zeros_like(db_ref)

        dcv = dc_ref[...]
        dlogf = jnp.dot(_tri(tm, False), dcv, precision=lax.Precision.HIGHEST, preferred_element_type=F32) + carry_ref[...]
        carry_ref[...] += jnp.sum(dcv, axis=0, keepdims=True)
        z = p_ref[...] + b_ref[...]
        dfl = dlogf * (1.0 / (1.0 + jnp.exp(z)))
        dfl_ref[...] = dfl.astype(BF16)
        db_ref[...] += jnp.sum(dfl, axis=0, keepdims=True)

    return pl.pallas_call(
        body, name="fox_gate_bwd", grid=(nt,),
        in_specs=[pl.BlockSpec((tm, 128), lambda i: (nt - 1 - i, 0)),
                  pl.BlockSpec((tm, 128), lambda i: (nt - 1 - i, FL0 // 128)), _full((1, 128))],
        out_specs=[pl.BlockSpec((tm, 128), lambda i: (nt - 1 - i, 0)), _full((1, 128))],
        out_shape=[jax.ShapeDtypeStruct((s, 128), BF16), jax.ShapeDtypeStruct((1, 128), F32)],
        scratch_shapes=[pltpu.VMEM((1, 128), F32)],
        compiler_params=_params(("arbitrary",)),
    )(dc, proj, b_forget128)


def _proj_pre_bwd(dproj, proj, dqf, dkf, dvf, dqm, dqa, dka, dva, dfl, gq_fox, gk_fox, gq_mem, gq_swa, gk_swa):
    s = proj.shape[0]
    tm = min(256, s)

    def body(dp_in, p_ref, dqf_ref, dkf_ref, dvf_ref, dqm_ref, dqa_ref, dka_ref, dva_ref, dfl_ref,
             gqf, gkf, gqm, gqa, gka, dp_ref, dgn_ref):
        i = pl.program_id(0)

        @pl.when(i == 0)
        def _():
            dgn_ref[...] = jnp.zeros_like(dgn_ref)

        def norm_bwd(off, width, hd, g_ref, dn_ref, slot):
            acc = jnp.zeros((1, 128), F32)
            for b in range(width // 128):
                v = p_ref[:, off + b * 128: off + (b + 1) * 128]
                r = lax.rsqrt(_group_mean(v * v, hd) + EPS)
                n = v * r
                dn = dn_ref[:, b * 128:(b + 1) * 128]
                acc = acc + jnp.sum(dn * n, axis=0, keepdims=True)
                dng = dn * g_ref[...]
                dp_ref[:, off + b * 128: off + (b + 1) * 128] = (r * (dng - n * _group_mean(dng * n, hd))).astype(BF16)
            dgn_ref[slot:slot + 1, :] += acc

        norm_bwd(H_QF, 512, HEAD, gqf, dqf_ref, 0)
        norm_bwd(H_KF, 512, HEAD, gkf, dkf_ref, 1)
        dp_ref[:, H_VF:H_VF + 512] = dvf_ref[...].astype(BF16)
        norm_bwd(H_QM, 512, MEM_HEAD, gqm, dqm_ref, 2)
        norm_bwd(H_QA, 512, HEAD, gqa, dqa_ref, 3)
        norm_bwd(H_KA, 128, HEAD, gka, dka_ref, 4)
        dp_ref[:, H_VA:H_VA + 128] = dva_ref[...].astype(BF16)
        dp_ref[:, H_FL:H_FL + 128] = dfl_ref[...]
        dp_ref[:, H_FL + 128:HALF_W] = jnp.zeros((tm, HALF_W - H_FL - 128), BF16)

    row = lambda w: pl.BlockSpec((tm, w), lambda i: (i, 0))
    g_spec = _full((1, 128))
    return pl.pallas_call(
        body, name="proj_pre_bwd", grid=(s // tm,),
        in_specs=[pl.BlockSpec(memory_space=pl.ANY), pl.BlockSpec((tm, HALF_W), lambda i: (i, 1)),
                  row(512), row(512), row(512), row(512), row(512), row(128), row(128), row(128),
                  g_spec, g_spec, g_spec, g_spec, g_spec],
        out_specs=[pl.BlockSpec((tm, HALF_W), lambda i: (i, 1)), _full((8, 128))],
        out_shape=[jax.ShapeDtypeStruct((s, PROJ_W), BF16), jax.ShapeDtypeStruct((8, 128), F32)],
        input_output_aliases={0: 0},
        compiler_params=_params(("arbitrary",)),
    )(dproj, proj, dqf, dkf, dvf, dqm, dqa, dka, dva, dfl, gq_fox, gk_fox, gq_mem, gq_swa, gk_swa)


def _in_bwd_x(dproj, w_in_p, x, g_mix, dx1):
    s = x.shape[0]
    tm, tk = min(512, s), 1536
    nk = PROJ_W // tk

    def body(dp_ref, w_ref, x_ref, g_ref, dx1_ref, gx_ref, dg_ref, acc_ref):
        i, kk = pl.program_id(0), pl.program_id(1)

        @pl.when((i == 0) & (kk == 0))
        def _():
            dg_ref[...] = jnp.zeros_like(dg_ref)

        prod = _dot(dp_ref[...], w_ref[...], NT)

        @pl.when(kk == 0)
        def _():
            acc_ref[...] = prod

        @pl.when(kk > 0)
        def _():
            acc_ref[...] += prod

        @pl.when(kk == nk - 1)
        def _():
            dx, dg = _rms_bwd(x_ref[...], g_ref[...], acc_ref[...], dx1_ref[...])
            gx_ref[...] = dx
            dg_ref[...] += dg

    row = pl.BlockSpec((tm, D_MODEL), lambda i, kk: (i, 0))
    return pl.pallas_call(
        body, name="in_bwd_x", grid=(s // tm, nk),
        in_specs=[pl.BlockSpec((tm, tk), lambda i, kk: (i, kk)), pl.BlockSpec((D_MODEL, tk), lambda i, kk: (0, kk)),
                  row, _full((1, D_MODEL)), row],
        out_specs=[row, _full((1, D_MODEL))],
        out_shape=[jax.ShapeDtypeStruct((s, D_MODEL), F32), jax.ShapeDtypeStruct((1, D_MODEL), F32)],
        scratch_shapes=[pltpu.VMEM((tm, D_MODEL), F32)],
        compiler_params=_params(("arbitrary", "arbitrary")),
    )(dproj, w_in_p, x, g_mix, dx1)


def _rel_bias_bwd(dbias, bucket):
    def body(db_ref, bk_ref, o_ref):
        bk = bk_ref[...]
        lane = lax.broadcasted_iota(jnp.int32, (1, 128), 1)
        for b in range(REL_BUCKETS):
            sel = bk == b
            acc = jnp.zeros((1, 128), F32)
            for h in range(SWA_HEADS):
                tot = jnp.sum(jnp.sum(jnp.where(sel, db_ref[h], 0.0), axis=-1, keepdims=True), axis=0, keepdims=True)
                acc = jnp.where(lane == h, tot, acc)
            o_ref[b:b + 1, :] = acc

    return pl.pallas_call(
        body, name="rel_bias_bwd",
        out_shape=jax.ShapeDtypeStruct((REL_BUCKETS, 128), F32),
        compiler_params=pltpu.CompilerParams(vmem_limit_bytes=VMEM_LIMIT),
    )(dbias, bucket)


def _my_place():
    return lax.axis_index("x"), lax.axis_index("y"), lax.axis_index("c")


def _peer(place, k):
    x, y, c = place
    return (1 - x if k & 4 else x, 1 - y if k & 2 else y, 1 - c if k & 1 else c)


def _index(place):
    x, y, c = place
    return 4 * x + 2 * y + c


def _all_gather(shard, name):
    def body(x_ref, out_ref, send_sems, recv_sems, local_sem):
        me = _my_place()
        mine = pltpu.make_async_copy(x_ref, out_ref.at[_index(me)], local_sem)
        mine.start()
        sends = []
        for k in range(1, N_DEV):
            cp = pltpu.make_async_remote_copy(
                src_ref=x_ref, dst_ref=out_ref.at[_index(me)], send_sem=send_sems.at[k - 1], recv_sem=recv_sems.at[k - 1],
                device_id=_peer(me, k), device_id_type=MESH)
            cp.start()
            sends.append(cp)
        for k in range(1, N_DEV):
            peer = _peer(me, k)
            pltpu.make_async_remote_copy(
                src_ref=x_ref, dst_ref=out_ref.at[_index(peer)], send_sem=send_sems.at[k - 1], recv_sem=recv_sems.at[k - 1],
                device_id=peer, device_id_type=MESH).wait_recv()
        for cp in sends:
            cp.wait_send()
        mine.wait()

    any_spec = pl.BlockSpec(memory_space=pl.ANY)
    return pl.pallas_call(
        body, name=name, in_specs=[any_spec], out_specs=any_spec,
        out_shape=jax.ShapeDtypeStruct((N_DEV,) + shard.shape, shard.dtype),
        scratch_shapes=[pltpu.SemaphoreType.DMA((N_DEV - 1,)), pltpu.SemaphoreType.DMA((N_DEV - 1,)), pltpu.SemaphoreType.DMA(())],
    )(shard)


def _exchange(parts, name):
    def body(p_ref, out_ref, send_sems, recv_sems, local_sem):
        me = _my_place()
        mine = pltpu.make_async_copy(p_ref.at[_index(me)], out_ref.at[0], local_sem)
        mine.start()
        sends = []
        for k in range(1, N_DEV):
            peer = _peer(me, k)
            cp = pltpu.make_async_remote_copy(
                src_ref=p_ref.at[_index(peer)], dst_ref=out_ref.at[k], send_sem=send_sems.at[k - 1], recv_sem=recv_sems.at[k - 1],
                device_id=peer, device_id_type=MESH)
            cp.start()
            sends.append(cp)
        for cp in sends:
            cp.wait_recv()
        for cp in sends:
            cp.wait_send()
        mine.wait()

    any_spec = pl.BlockSpec(memory_space=pl.ANY)
    return pl.pallas_call(
        body, name=name, in_specs=[any_spec], out_specs=any_spec,
        out_shape=jax.ShapeDtypeStruct(parts.shape, parts.dtype),
        scratch_shapes=[pltpu.SemaphoreType.DMA((N_DEV - 1,)), pltpu.SemaphoreType.DMA((N_DEV - 1,)), pltpu.SemaphoreType.DMA(())],
    )(parts)


def _adam_math(w, g, m, v):
    m2 = ADAM_B1 * m + (1.0 - ADAM_B1) * g
    v2 = ADAM_B2 * v + (1.0 - ADAM_B2) * (g * g)
    m_hat = m2 / (1.0 - ADAM_B1 ** ADAM_STEP)
    v_hat = v2 / (1.0 - ADAM_B2 ** ADAM_STEP)
    delta = -ADAM_LR * (m_hat / (jnp.sqrt(v_hat) + ADAM_EPS) + ADAM_WD * w)
    return delta, m2, v2


def _adamw(parts, w, m, v, name):
    a, b = w.shape
    ta = min(128, a)

    def body(p_ref, w_ref, m_ref, v_ref, g_ref, d_ref, m2_ref, v2_ref):
        g = p_ref[0].astype(F32)
        for k in range(1, N_DEV):
            g = g + p_ref[k].astype(F32)
        delta, m2, v2 = _adam_math(w_ref[...], g, m_ref[...], v_ref[...])
        g_ref[...] = g
        d_ref[...] = delta
        m2_ref[...] = m2
        v2_ref[...] = v2

    blk = pl.BlockSpec((ta, b), lambda i: (i, 0))
    sd = jax.ShapeDtypeStruct((a, b), F32)
    return pl.pallas_call(
        body, name=name, grid=(a // ta,),
        in_specs=[pl.BlockSpec((N_DEV, ta, b), lambda i: (0, i, 0)), blk, blk, blk],
        out_specs=[blk, blk, blk, blk], out_shape=[sd, sd, sd, sd],
        compiler_params=_params(("parallel",)),
    )(parts, w, m, v)


def _bucket_table():
    t_loc = jnp.arange(SWA_BLOCK)[:, None] + SWA_BLOCK
    s_loc = jnp.arange(2 * SWA_BLOCK)[None, :]
    dist = t_loc - s_loc
    max_exact = REL_BUCKETS // 2
    d = jnp.maximum(dist, 0)
    df = jnp.maximum(d, 1).astype(F32)
    large = max_exact + (jnp.log(df / max_exact) / math.log(REL_MAX_DIST / max_exact) * (REL_BUCKETS - max_exact)).astype(jnp.int32)
    large = jnp.minimum(large, REL_BUCKETS - 1)
    bucket = jnp.where(d < max_exact, d, large)
    band = (dist >= 0) & (dist < SWA_BLOCK)
    return bucket, band


def _tile2(g):
    return jnp.concatenate([g, g], axis=1) if g.shape[1] == HEAD else g


def _w_in_padded(w_in):
    z = jnp.zeros((w_in.shape[0], PROJ_W - FL0 - 8), w_in.dtype)
    return jnp.concatenate([w_in[:, 2824:5896], w_in[:, 768:1280], w_in[:, 1280:1792], w_in[:, 1792:2304], w_in[:, 2312:2824],
                            w_in[:, 0:512], w_in[:, 512:640], w_in[:, 640:768], w_in[:, 2304:2312], z], axis=1)


def _w_in_unpadded(dwp):
    return jnp.concatenate([dwp[:, QA0:QA0 + 512], dwp[:, KA0:KA0 + 128], dwp[:, VA0:VA0 + 128], dwp[:, QF0:QF0 + 512],
                            dwp[:, KF0:KF0 + 512], dwp[:, VF0:VF0 + 512], dwp[:, FL0:FL0 + 8], dwp[:, QM0:QM0 + 512],
                            dwp[:, GL0:GL0 + 3072]], axis=1)


def _local_step(x, mem, target, p, w):
    s = x.shape[0]
    bucket, band = _bucket_table()
    bias = jnp.transpose(p["rel_bias"][bucket], (2, 0, 1))
    bucket_m = jnp.where(band, bucket, -1).astype(jnp.int32)
    gqf, gkf, gqa, gka = _tile2(p["qn_fox"]), _tile2(p["kn_fox"]), _tile2(p["qn_swa"]), _tile2(p["kn_swa"])
    gqm = p["qn_mem"]
    bf128 = jnp.pad(p["b_forget"], ((0, 0), (0, 120)))
    sink = p["sink_swa"].reshape(8)

    h = _rms_fwd(x, p["g_mix"], "rms_mix")
    proj = _mm(h, w["w_in"], "nn", F32, 512, 768, 1024, "proj")
    qf, kf, vf, qm, qa, ka, va = _proj_post(proj, gqf, gkf, gqm, gqa, gka)
    cc4, cr4 = _fox_gate_fwd(proj, bf128)
    mem_n, kv_raw, mk, mv = _memkv_fwd(mem, p["g_mem"], w["w_mem_kv"], p["kn_mem"])
    kp = jnp.pad(ka, ((SWA_BLOCK, 0), (0, 0)))
    vp = jnp.pad(va, ((SWA_BLOCK, 0), (0, 0)))
    oa = _swa_fwd(qa, kp, vp, bias, sink)
    of, lse4 = _fox_fwd(qf, kf, vf, cc4, cr4)
    om = _mem_fwd(qm, mk, mv)
    x1, hm, merged = _merge_fwd(x, oa, of, om, proj, p["b_gate"], w["w_o_swa"], w["w_o_fox"], w["w_o_mem"], w["w_out"], p["g_mlp"])
    a, u = _mlp_up(hm, w["w_mlp_up"])
    dy, loss = _mlp_down_loss(u, w["w_mlp_down"], x1, target)

    da = _mlp_bwd_act(dy, w["w_mlp_down"], a)
    dx1, dg_mlp = _mlp_bwd_x(da, w["w_mlp_up"], x1, dy, p["g_mlp"])
    dw_up = _mm(hm, da, "tn", F32, 512, 1024, 512, "dw_up")
    dw_down = _mm(u, dy, "tn", F32, 1024, 1024, 512, "dw_down")
    dproj, doa, dof, dom, dya, dyf, dym, db_gate = _merge_bwd(
        dx1, oa, of, om, proj, p["b_gate"], w["w_o_swa"], w["w_o_fox"], w["w_o_mem"], w["w_out"])
    dw_oa = _mm(oa, dya, "tn", F32, 512, 1024, 512, "dw_o_swa")
    dw_of = _mm(of, dyf, "tn", F32, 512, 1024, 512, "dw_o_fox")
    dw_om = _mm(om, dym, "tn", F32, 512, 1024, 512, "dw_o_mem")
    dw_out = _mm(merged, dx1, "tn", F32, 512, 1024, 512, "dw_out")

    dqa, dkp, dvp, dbias, dsink = _swa_bwd(qa, kp, vp, bias, sink, doa)
    dqf, dkf, dvf, dcq4, dck4 = _fox_bwd(qf, kf, vf, dof, of, cc4, cr4, lse4)
    dqm, dmk, dmv = _mem_bwd(qm, mk, mv, dom)
    dw_kv, dkn_mem, dg_mem = _memkv_bwd(dmk, dmv, kv_raw, p["kn_mem"], mem, p["g_mem"], mem_n, w["w_mem_kv"])

    dcq = jnp.transpose(dcq4[:, :, 0:2], (1, 0, 2)).reshape(s, 8)
    dck = jnp.transpose(dck4[:, 0:2, :], (2, 0, 1)).reshape(s, 8)
    dc = jnp.pad(dcq - dck, ((0, 0), (0, 120)))
    dfl, db_forget = _fox_gate_bwd(dc, proj, bf128)

    dproj, dgn = _proj_pre_bwd(dproj, proj, dqf, dkf, dvf, dqm, dqa, dkp[SWA_BLOCK:], dvp[SWA_BLOCK:], dfl,
                               gqf, gkf, gqm, gqa, gka)
    grad_x, dg_mix = _in_bwd_x(dproj, w["w_in"], x, p["g_mix"], dx1)
    dw_in = _mm(h, dproj, "tn", F32, 512, 1536, 512, "dw_in")
    d_rel = _rel_bias_bwd(dbias, bucket_m)

    fold = lambda r: dgn[r:r + 1, 0:HEAD] + dgn[r:r + 1, HEAD:128]
    small = {
        "g_mix": dg_mix, "b_gate": db_gate, "b_forget": db_forget[:, 0:8],
        "qn_swa": fold(3), "kn_swa": fold(4), "sink_swa": dsink[:, 0:8], "rel_bias": d_rel[:, 0:8],
        "qn_fox": fold(0), "kn_fox": fold(1), "g_mem": dg_mem, "qn_mem": dgn[2:3, :], "kn_mem": dkn_mem,
        "g_mlp": dg_mlp,
    }
    big = {"w_in": dw_in, "w_mem_kv": dw_kv, "w_o_swa": dw_oa, "w_o_fox": dw_of, "w_o_mem": dw_om,
           "w_out": dw_out, "w_mlp_up": dw_up, "w_mlp_down": dw_down}
    return loss, grad_x, big, small


SMALL = ("g_mix", "b_gate", "b_forget", "qn_swa", "kn_swa", "sink_swa", "rel_bias", "qn_fox", "kn_fox", "g_mem",
         "qn_mem", "kn_mem", "g_mlp")
BIG = ("w_in", "w_mem_kv", "w_o_swa", "w_o_fox", "w_o_mem", "w_out", "w_mlp_up", "w_mlp_down")
COL_SHARDED = ("w_in", "w_o_swa", "w_o_fox", "w_o_mem", "w_mlp_up")
WEIGHTS = ("g_mix", "w_in", "b_gate", "b_forget", "qn_swa", "kn_swa", "sink_swa", "rel_bias", "qn_fox", "kn_fox", "g_mem",
           "w_mem_kv", "qn_mem", "kn_mem", "w_o_swa", "w_o_fox", "w_o_mem", "w_out", "g_mlp", "w_mlp_up", "w_mlp_down")
SMALL_PAD = 7168


def _gathered_to_full(name, g):
    if name in COL_SHARDED:
        return jnp.transpose(g, (1, 0, 2)).reshape(g.shape[1], N_DEV * g.shape[2])
    return g.reshape(N_DEV * g.shape[1], g.shape[2])


def _full_to_parts(name, full, b):
    if name in COL_SHARDED:
        return jnp.transpose(full.reshape(full.shape[0], N_DEV, b), (1, 0, 2)).astype(BF16)
    return full.reshape(N_DEV, full.shape[0] // N_DEV, full.shape[1]).astype(BF16)


def _pack_small(d):
    flat = jnp.concatenate([d[n].reshape(-1) for n in SMALL])
    return jnp.pad(flat, (0, SMALL_PAD - flat.shape[0])).reshape(8, SMALL_PAD // 8)


def _unpack_small(packed, like):
    flat = packed.reshape(-1)
    out, off = {}, 0
    for n in SMALL:
        size = like[n].size
        out[n] = flat[off:off + size].reshape(like[n].shape)
        off += size
    return out


def _adamw_small(parts, w, m, v):
    def body(p_ref, w_ref, m_ref, v_ref, g_ref, d_ref, m2_ref, v2_ref):
        g = p_ref[0]
        for k in range(1, N_DEV):
            g = g + p_ref[k]
        delta, m2, v2 = _adam_math(w_ref[...], g, m_ref[...], v_ref[...])
        g_ref[...] = g
        d_ref[...] = delta
        m2_ref[...] = m2
        v2_ref[...] = v2

    sd = jax.ShapeDtypeStruct(w.shape, F32)
    return pl.pallas_call(body, name="adamw_small", out_shape=[sd, sd, sd, sd])(parts, w, m, v)


def kernel(x, mem, g_mix, w_in, b_gate, b_forget, qn_swa, kn_swa, sink_swa, rel_bias, qn_fox, kn_fox, g_mem, w_mem_kv, qn_mem, kn_mem, w_o_swa, w_o_fox, w_o_mem, w_out, g_mlp, w_mlp_up, w_mlp_down, loss_target, m_g_mix, m_w_in, m_b_gate, m_b_forget, m_qn_swa, m_kn_swa, m_sink_swa, m_rel_bias, m_qn_fox, m_kn_fox, m_g_mem, m_w_mem_kv, m_qn_mem, m_kn_mem, m_w_o_swa, m_w_o_fox, m_w_o_mem, m_w_out, m_g_mlp, m_w_mlp_up, m_w_mlp_down, v_g_mix, v_w_in, v_b_gate, v_b_forget, v_qn_swa, v_kn_swa, v_sink_swa, v_rel_bias, v_qn_fox, v_kn_fox, v_g_mem, v_w_mem_kv, v_qn_mem, v_kn_mem, v_w_o_swa, v_w_o_fox, v_w_o_mem, v_w_out, v_g_mlp, v_w_mlp_up, v_w_mlp_down):
    wts = dict(g_mix=g_mix, w_in=w_in, b_gate=b_gate, b_forget=b_forget, qn_swa=qn_swa, kn_swa=kn_swa, sink_swa=sink_swa,
               rel_bias=rel_bias, qn_fox=qn_fox, kn_fox=kn_fox, g_mem=g_mem, w_mem_kv=w_mem_kv, qn_mem=qn_mem, kn_mem=kn_mem,
               w_o_swa=w_o_swa, w_o_fox=w_o_fox, w_o_mem=w_o_mem, w_out=w_out, g_mlp=g_mlp, w_mlp_up=w_mlp_up,
               w_mlp_down=w_mlp_down)
    mom = dict(g_mix=m_g_mix, w_in=m_w_in, b_gate=m_b_gate, b_forget=m_b_forget, qn_swa=m_qn_swa, kn_swa=m_kn_swa,
               sink_swa=m_sink_swa, rel_bias=m_rel_bias, qn_fox=m_qn_fox, kn_fox=m_kn_fox, g_mem=m_g_mem, w_mem_kv=m_w_mem_kv,
               qn_mem=m_qn_mem, kn_mem=m_kn_mem, w_o_swa=m_w_o_swa, w_o_fox=m_w_o_fox, w_o_mem=m_w_o_mem, w_out=m_w_out,
               g_mlp=m_g_mlp, w_mlp_up=m_w_mlp_up, w_mlp_down=m_w_mlp_down)
    var = dict(g_mix=v_g_mix, w_in=v_w_in, b_gate=v_b_gate, b_forget=v_b_forget, qn_swa=v_qn_swa, kn_swa=v_kn_swa,
               sink_swa=v_sink_swa, rel_bias=v_rel_bias, qn_fox=v_qn_fox, kn_fox=v_kn_fox, g_mem=v_g_mem, w_mem_kv=v_w_mem_kv,
               qn_mem=v_qn_mem, kn_mem=v_kn_mem, w_o_swa=v_w_o_swa, w_o_fox=v_w_o_fox, w_o_mem=v_w_o_mem, w_out=v_w_out,
               g_mlp=v_g_mlp, w_mlp_up=v_w_mlp_up, w_mlp_down=v_w_mlp_down)

    full = {}
    for n in BIG:
        shard = wts[n][0].astype(BF16)
        full[n] = _gathered_to_full(n, _all_gather(shard, "ag_" + n))
    full["w_in"] = _w_in_padded(full["w_in"])
    small_p = {n: wts[n] for n in SMALL}

    loss, grad_x, big_g, small_g = _local_step(x[0], mem[0], loss_target[0], small_p, full)
    big_g["w_in"] = _w_in_unpadded(big_g["w_in"])

    grads, delta, new_m, new_v = {}, {}, {}, {}
    for n in BIG:
        shard_shape = wts[n].shape[1:]
        parts = _exchange(_full_to_parts(n, big_g[n], shard_shape[1]), "rs_" + n)
        g, d, m2, v2 = _adamw(parts, wts[n][0], mom[n][0], var[n][0], "adamw_" + n)
        grads[n], delta[n], new_m[n], new_v[n] = g[None], d[None], m2[None], v2[None]

    gathered = _all_gather(_pack_small(small_g), "ag_small")
    g, d, m2, v2 = _adamw_small(gathered, _pack_small(small_p), _pack_small({n: mom[n] for n in SMALL}),
                                _pack_small({n: var[n] for n in SMALL}))
    for dst, packed in ((grads, g), (delta, d), (new_m, m2), (new_v, v2)):
        dst.update(_unpack_small(packed, small_p))

    total = lax.psum(loss[0, 0], ("x", "y", "c"))
    return (total, grad_x[None], *[grads[n] for n in WEIGHTS], *[delta[n] for n in WEIGHTS],
            *[new_m[n] for n in WEIGHTS], *[new_v[n] for n in WEIGHTS])
```

```python
import functools
import math

import jax
import jax.numpy as jnp
from jax import lax
from jax.experimental import pallas as pl
from jax.experimental.pallas import tpu as pltpu

F32 = jnp.float32
BF16 = jnp.bfloat16

D_MODEL = 1024
N_MEM = 256
D_FF = 4096
HEAD = 64
SWA_HEADS = 8
SWA_BLOCK = 128
MEM_HEADS = 4
MEM_HEAD = 128
EPS = 1e-6
NEG = -1e30
REL_BUCKETS = 32
REL_MAX_DIST = 128

ADAM_LR = 0.001
ADAM_B1 = 0.9
ADAM_B2 = 0.999
ADAM_EPS = 1e-08
ADAM_WD = 0.01
ADAM_STEP = 10

GL0, QF0, KF0, VF0, QM0, QA0, KA0, VA0, FL0 = 0, 3072, 3584, 4096, 4608, 5120, 5632, 5760, 5888
PROJ_W = 6144
HALF_W = 3072
H_QF, H_KF, H_VF, H_QM, H_QA, H_KA, H_VA, H_FL = 0, 512, 1024, 1536, 2048, 2560, 2688, 2816

VMEM_LIMIT = 56 * 1024 * 1024
N_DEV = 8
MESH = pl.DeviceIdType.MESH

NN = (((1,), (0,)), ((), ()))
NT = (((1,), (1,)), ((), ()))
TN = (((0,), (0,)), ((), ()))


def _dot(a, b, dims=NN):
    return lax.dot_general(a, b, dims, preferred_element_type=F32)


def _params(sem):
    return pltpu.CompilerParams(dimension_semantics=sem, vmem_limit_bytes=VMEM_LIMIT)


def _full(shape):
    nd = len(shape)
    return pl.BlockSpec(shape, lambda *_: (0,) * nd)


def _sigmoid(z):
    return 1.0 / (1.0 + jnp.exp(-z))


def _group_mean(v, hd):
    if hd == 128:
        return jnp.mean(v, axis=-1, keepdims=True)
    lane = lax.broadcasted_iota(jnp.int32, v.shape, 1)
    lo = lane < HEAD
    s_lo = jnp.sum(jnp.where(lo, v, 0.0), axis=-1, keepdims=True)
    s_hi = jnp.sum(jnp.where(lo, 0.0, v), axis=-1, keepdims=True)
    return jnp.where(lo, s_lo, s_hi) * (1.0 / HEAD)


def _mm(a, b, mode, out_dtype, tm, tn, tk, name):
    if mode == "nn":
        m, k = a.shape
        n = b.shape[1]
    elif mode == "nt":
        m, k = a.shape
        n = b.shape[0]
    else:
        k, m = a.shape
        n = b.shape[1]
    tm, tn, tk = min(tm, m), min(tn, n), min(tk, k)
    nk = k // tk
    dims = {"nn": NN, "nt": NT, "tn": TN}[mode]
    a_spec = pl.BlockSpec((tk, tm), lambda i, j, kk: (kk, i)) if mode == "tn" else pl.BlockSpec((tm, tk), lambda i, j, kk: (i, kk))
    b_spec = pl.BlockSpec((tn, tk), lambda i, j, kk: (j, kk)) if mode == "nt" else pl.BlockSpec((tk, tn), lambda i, j, kk: (kk, j))

    def body(a_ref, b_ref, o_ref, acc_ref):
        prod = _dot(a_ref[...].astype(BF16), b_ref[...].astype(BF16), dims)
        if nk == 1:
            o_ref[...] = prod.astype(o_ref.dtype)
        else:
            kk = pl.program_id(2)

            @pl.when(kk == 0)
            def _():
                acc_ref[...] = prod

            @pl.when(kk > 0)
            def _():
                acc_ref[...] += prod

            @pl.when(kk == nk - 1)
            def _():
                o_ref[...] = acc_ref[...].astype(o_ref.dtype)

    return pl.pallas_call(
        body, name=name, grid=(m // tm, n // tn, nk),
        in_specs=[a_spec, b_spec],
        out_specs=pl.BlockSpec((tm, tn), lambda i, j, kk: (i, j)),
        out_shape=jax.ShapeDtypeStruct((m, n), out_dtype),
        scratch_shapes=[pltpu.VMEM((tm, tn), F32)],
        compiler_params=_params(("parallel", "parallel", "arbitrary")),
    )(a, b)


def _rms_fwd(x, g, name):
    s, d = x.shape
    tm = min(512, s)

    def body(x_ref, g_ref, h_ref):
        xv = x_ref[...]
        r = lax.rsqrt(jnp.mean(xv * xv, axis=-1, keepdims=True) + EPS)
        h_ref[...] = (xv * r * g_ref[...]).astype(BF16)

    return pl.pallas_call(
        body, name=name, grid=(s // tm,),
        in_specs=[pl.BlockSpec((tm, d), lambda i: (i, 0)), _full((1, d))],
        out_specs=pl.BlockSpec((tm, d), lambda i: (i, 0)),
        out_shape=jax.ShapeDtypeStruct((s, d), BF16),
        compiler_params=_params(("parallel",)),
    )(x, g)


def _proj_post(proj, gq_fox, gk_fox, gq_mem, gq_swa, gk_swa):
    s = proj.shape[0]
    tm = min(256, s)

    def body(p_ref, gqf, gkf, gqm, gqa, gka, qf_ref, kf_ref, vf_ref, qm_ref, qa_ref, ka_ref, va_ref):
        def norm(off, width, hd, g_ref, o_ref):
            for b in range(width // 128):
                v = p_ref[:, off + b * 128: off + (b + 1) * 128]
                r = lax.rsqrt(_group_mean(v * v, hd) + EPS)
                o_ref[:, b * 128:(b + 1) * 128] = (v * r * g_ref[...]).astype(BF16)

        norm(H_QF, 512, HEAD, gqf, qf_ref)
        norm(H_KF, 512, HEAD, gkf, kf_ref)
        vf_ref[...] = p_ref[:, H_VF:H_VF + 512].astype(BF16)
        norm(H_QM, 512, MEM_HEAD, gqm, qm_ref)
        norm(H_QA, 512, HEAD, gqa, qa_ref)
        norm(H_KA, 128, HEAD, gka, ka_ref)
        va_ref[...] = p_ref[:, H_VA:H_VA + 128].astype(BF16)

    g_spec = _full((1, 128))
    o512 = pl.BlockSpec((tm, 512), lambda i: (i, 0))
    o128 = pl.BlockSpec((tm, 128), lambda i: (i, 0))
    s512 = jax.ShapeDtypeStruct((s, 512), BF16)
    s128 = jax.ShapeDtypeStruct((s, 128), BF16)
    return pl.pallas_call(
        body, name="proj_post", grid=(s // tm,),
        in_specs=[pl.BlockSpec((tm, HALF_W), lambda i: (i, 1)), g_spec, g_spec, g_spec, g_spec, g_spec],
        out_specs=[o512, o512, o512, o512, o512, o128, o128],
        out_shape=[s512, s512, s512, s512, s512, s128, s128],
        compiler_params=_params(("parallel",)),
    )(proj, gq_fox, gk_fox, gq_mem, gq_swa, gk_swa)


def _tri(n, lower):
    r = lax.broadcasted_iota(jnp.int32, (n, n), 0)
    c = lax.broadcasted_iota(jnp.int32, (n, n), 1)
    return jnp.where((c <= r) if lower else (c >= r), 1.0, 0.0).astype(F32)


def _fox_gate_fwd(proj, b_forget128):
    s = proj.shape[0]
    tm = min(512, s)

    def body(p_ref, b_ref, cc_ref, carry_ref):
        i = pl.program_id(0)

        @pl.when(i == 0)
        def _():
            carry_ref[...] = jnp.zeros_like(carry_ref)

        z = p_ref[...] + b_ref[...]
        logf = jnp.minimum(z, 0.0) - jnp.log(1.0 + jnp.exp(-jnp.abs(z)))
        c = jnp.dot(_tri(tm, True), logf, precision=lax.Precision.HIGHEST, preferred_element_type=F32) + carry_ref[...]
        carry_ref[...] = c[tm - 1:tm, :]
        for hp in range(4):
            cc_ref[hp] = c if hp == 0 else pltpu.roll(c, 128 - 2 * hp, 1)

    return pl.pallas_call(
        body, name="fox_gate_fwd", grid=(s // tm,),
        in_specs=[pl.BlockSpec((tm, 128), lambda i: (i, FL0 // 128)), _full((1, 128))],
        out_specs=pl.BlockSpec((4, tm, 128), lambda i: (0, i, 0)),
        out_shape=jax.ShapeDtypeStruct((4, s, 128), F32),
        scratch_shapes=[pltpu.VMEM((1, 128), F32)],
        compiler_params=_params(("arbitrary",)),
    )(proj, b_forget128)


def _memkv_fwd(mem, g_mem, w_kv, kn_mem):
    m = mem.shape[0]

    def body(mem_ref, g_ref, w_ref, kn_ref, memn_ref, kv_ref, mk_ref, mv_ref):
        xv = mem_ref[...]
        r = lax.rsqrt(jnp.mean(xv * xv, axis=-1, keepdims=True) + EPS)
        mn = (xv * r * g_ref[...]).astype(BF16)
        memn_ref[...] = mn
        kv = _dot(mn, w_ref[...])
        kv_ref[...] = kv
        for h in range(MEM_HEADS):
            v = kv[:, h * 128:(h + 1) * 128]
            rr = lax.rsqrt(jnp.mean(v * v, axis=-1, keepdims=True) + EPS)
            mk_ref[:, h * 128:(h + 1) * 128] = (v * rr * kn_ref[...]).astype(BF16)
        mv_ref[...] = kv[:, 512:1024].astype(BF16)

    return pl.pallas_call(
        body, name="memkv_fwd",
        out_shape=[jax.ShapeDtypeStruct((m, D_MODEL), BF16), jax.ShapeDtypeStruct((m, 1024), F32),
                   jax.ShapeDtypeStruct((m, 512), BF16), jax.ShapeDtypeStruct((m, 512), BF16)],
        compiler_params=pltpu.CompilerParams(vmem_limit_bytes=VMEM_LIMIT),
    )(mem, g_mem, w_kv, kn_mem)


def _bias_table(rel_bias, bucket):
    def body(rb_ref, bk_ref, o_ref):
        bk = bk_ref[...]
        for h in range(SWA_HEADS):
            acc = jnp.zeros(bk.shape, F32)
            for b in range(REL_BUCKETS):
                acc = jnp.where(bk == b, rb_ref[b, h], acc)
            o_ref[h] = acc

    return pl.pallas_call(
        body, name="bias_table",
        in_specs=[pl.BlockSpec(memory_space=pltpu.SMEM), pl.BlockSpec(memory_space=pltpu.VMEM)],
        out_shape=jax.ShapeDtypeStruct((SWA_HEADS,) + bucket.shape, F32),
    )(rel_bias, bucket)


def _swa_valid(n):
    row = lax.broadcasted_iota(jnp.int32, (SWA_BLOCK, 2 * SWA_BLOCK), 0)
    col = lax.broadcasted_iota(jnp.int32, (SWA_BLOCK, 2 * SWA_BLOCK), 1)
    dist = row + SWA_BLOCK - col
    return (dist >= 0) & (dist < SWA_BLOCK) & ((col >= SWA_BLOCK) | (n > 0))


def _swa_fwd(qa, kp, vp, bias, sink):
    s = qa.shape[0]
    nb = s // SWA_BLOCK

    def body(sink_ref, q_ref, kp_ref, vp_ref, bias_ref, o_ref):
        n = pl.program_id(0)
        start = pl.multiple_of(n * SWA_BLOCK, SWA_BLOCK)
        k2 = kp_ref[pl.ds(start, 2 * SWA_BLOCK), :]
        v2 = vp_ref[pl.ds(start, 2 * SWA_BLOCK), :]
        valid = _swa_valid(n)
        for h in range(SWA_HEADS):
            kv = h // 4
            qh = q_ref[:, h * HEAD:(h + 1) * HEAD]
            kh = k2[:, kv * HEAD:(kv + 1) * HEAD]
            vh = v2[:, kv * HEAD:(kv + 1) * HEAD]
            sc = _dot(qh, kh, NT) * 0.125 + bias_ref[h]
            sc = jnp.where(valid, sc, NEG)
            sk = sink_ref[h]
            mx = jnp.maximum(jnp.max(sc, axis=-1, keepdims=True), sk)
            p = jnp.exp(sc - mx)
            den = jnp.sum(p, axis=-1, keepdims=True) + jnp.exp(sk - mx)
            p = p / den
            o_ref[:, h * HEAD:(h + 1) * HEAD] = _dot(p.astype(BF16), vh).astype(BF16)

    return pl.pallas_call(
        body, name="swa_fwd", grid=(nb,),
        in_specs=[pl.BlockSpec(memory_space=pltpu.SMEM),
                  pl.BlockSpec((SWA_BLOCK, 512), lambda n: (n, 0)),
                  _full(kp.shape), _full(vp.shape), _full(bias.shape)],
        out_specs=pl.BlockSpec((SWA_BLOCK, 512), lambda n: (n, 0)),
        out_shape=jax.ShapeDtypeStruct((s, 512), BF16),
        compiler_params=_params(("parallel",)),
    )(sink, qa, kp, vp, bias)


def _head_mask(e):
    lane = lax.broadcasted_iota(jnp.int32, (1, 128), 1)
    return (lane >= e * HEAD) & (lane < (e + 1) * HEAD)


FOX_T = 256
FOX_TK = 512


def _head_rows(e):
    row = lax.broadcasted_iota(jnp.int32, (128, 1), 0)
    return (row >= e * HEAD) & (row < (e + 1) * HEAD)


def _fox_fwd(q, k, v_t, cc4):
    s = q.shape[0]
    t = min(FOX_T, s)
    tk = min(FOX_TK, s)
    nq = s // t

    def body(q_ref, k_ref, vt_ref, cc_ref, o_ref, lse_ref):
        i = pl.program_id(1)
        qs = q_ref[...] * jnp.asarray(0.125, BF16)
        qe = [jnp.where(_head_mask(e), qs, jnp.zeros_like(qs)) for e in range(2)]
        n_full = (i * t) // tk
        krow = lax.broadcasted_iota(jnp.int32, (tk, t), 0) + n_full * tk
        qcol = lax.broadcasted_iota(jnp.int32, (tk, t), 1) + i * t

        def step(j, carry, masked):
            ks = pl.ds(pl.multiple_of(j * tk, tk), tk)
            kj = k_ref[ks, :]
            vtj = vt_ref[:, ks]
            out = []
            for e in range(2):
                m, acc = carry[2 * e], carry[2 * e + 1]
                st = _dot(kj, qe[e], NT) - cc_ref[0, ks, e:e + 1]
                if masked:
                    st = jnp.where(krow <= qcol, st, NEG)
                m_new = jnp.maximum(m, jnp.max(st, axis=0, keepdims=True))
                alpha = jnp.exp(m - m_new)
                pt = jnp.exp(st - m_new).astype(BF16)
                vte = jnp.where(_head_rows(e), vtj, jnp.ones_like(vtj))
                out += [m_new, alpha * acc + _dot(vte, pt)]
            return tuple(out)

        init = (jnp.full((1, t), NEG, F32), jnp.zeros((128, t), F32)) * 2
        carry = lax.fori_loop(0, n_full, functools.partial(step, masked=False), init)
        m0, a0, m1, a1 = step(n_full, carry, True)
        l0 = a0[HEAD:HEAD + 1, :]
        l1 = a1[0:1, :]
        o_t = jnp.where(_head_rows(0), a0 / l0, a1 / l1)
        o_ref[...] = o_t.T.astype(BF16)
        r8 = lax.broadcasted_iota(jnp.int32, (8, t), 0)
        lse_ref[0] = jnp.where(r8 == 0, m0 + jnp.log(l0), jnp.where(r8 == 1, m1 + jnp.log(l1), 0.0))

    return pl.pallas_call(
        body, name="fox_fwd", grid=(4, nq),
        in_specs=[pl.BlockSpec((t, 128), lambda hp, i: (i, hp)),
                  pl.BlockSpec((s, 128), lambda hp, i: (0, hp)),
                  pl.BlockSpec((128, s), lambda hp, i: (hp, 0)),
                  pl.BlockSpec((1, s, 128), lambda hp, i: (hp, 0, 0))],
        out_specs=[pl.BlockSpec((t, 128), lambda hp, i: (i, hp)),
                   pl.BlockSpec((1, 8, t), lambda hp, i: (hp, 0, i))],
        out_shape=[jax.ShapeDtypeStruct((s, 512), BF16), jax.ShapeDtypeStruct((4, 8, s), F32)],
        compiler_params=_params(("parallel", "parallel")),
    )(q, k, v_t, cc4)


MEM_SCALE = MEM_HEAD ** -0.5


def _mem_fwd(qm, mk, mv):
    s = qm.shape[0]
    tq = min(512, s)

    def body(q_ref, mk_ref, mv_ref, o_ref):
        for h in range(MEM_HEADS):
            hs = slice(h * 128, (h + 1) * 128)
            sc = _dot(q_ref[:, hs], mk_ref[:, hs], NT) * MEM_SCALE
            mx = jnp.max(sc, axis=-1, keepdims=True)
            p = jnp.exp(sc - mx)
            p = p / jnp.sum(p, axis=-1, keepdims=True)
            o_ref[:, hs] = _dot(p.astype(BF16), mv_ref[:, hs]).astype(BF16)

    return pl.pallas_call(
        body, name="mem_fwd", grid=(s // tq,),
        in_specs=[pl.BlockSpec((tq, 512), lambda i: (i, 0)), _full(mk.shape), _full(mv.shape)],
        out_specs=pl.BlockSpec((tq, 512), lambda i: (i, 0)),
        out_shape=jax.ShapeDtypeStruct((s, 512), BF16),
        compiler_params=_params(("parallel",)),
    )(qm, mk, mv)


def _merge_fwd(x, oa, of, om, proj, b_gate, wa, wf, wm, w_out, g_mlp):
    s = x.shape[0]
    tm = min(256, s)

    def body(x_ref, oa_ref, of_ref, om_ref, gl_ref, bg_ref, wa_ref, wf_ref, wm_ref, wo_ref, g_ref, x1_ref, hm_ref, mg_ref):
        merged = None
        for b, (o_ref, w_ref) in enumerate(((oa_ref, wa_ref), (of_ref, wf_ref), (om_ref, wm_ref))):
            cs = slice(b * D_MODEL, (b + 1) * D_MODEL)
            y = _dot(o_ref[...], w_ref[...])
            t = _sigmoid(gl_ref[:, cs] + bg_ref[:, cs]) * y
            merged = t if merged is None else merged + t
        mb = merged.astype(BF16)
        mg_ref[...] = mb
        x1 = x_ref[...] + _dot(mb, wo_ref[...])
        x1_ref[...] = x1
        r = lax.rsqrt(jnp.mean(x1 * x1, axis=-1, keepdims=True) + EPS)
        hm_ref[...] = (x1 * r * g_ref[...]).astype(BF16)

    row = lambda w: pl.BlockSpec((tm, w), lambda i: (i, 0))
    return pl.pallas_call(
        body, name="merge_fwd", grid=(s // tm,),
        in_specs=[row(D_MODEL), row(512), row(512), row(512), row(HALF_W), _full((1, HALF_W)),
                  _full(wa.shape), _full(wf.shape), _full(wm.shape), _full(w_out.shape), _full((1, D_MODEL))],
        out_specs=[row(D_MODEL), row(D_MODEL), row(D_MODEL)],
        out_shape=[jax.ShapeDtypeStruct((s, D_MODEL), F32), jax.ShapeDtypeStruct((s, D_MODEL), BF16),
                   jax.ShapeDtypeStruct((s, D_MODEL), BF16)],
        compiler_params=_params(("parallel",)),
    )(x, oa, of, om, proj, b_gate, wa, wf, wm, w_out, g_mlp)


def _mlp_up(hm, w_up):
    s = hm.shape[0]
    tm, tn = min(512, s), 1024

    def body(h_ref, w_ref, a_ref, u_ref):
        a = _dot(h_ref[...], w_ref[...])
        a_ref[...] = a
        r = jnp.maximum(a, 0.0)
        u_ref[...] = (r * r).astype(BF16)

    return pl.pallas_call(
        body, name="mlp_up", grid=(s // tm, D_FF // tn),
        in_specs=[pl.BlockSpec((tm, D_MODEL), lambda i, j: (i, 0)), pl.BlockSpec((D_MODEL, tn), lambda i, j: (0, j))],
        out_specs=[pl.BlockSpec((tm, tn), lambda i, j: (i, j)), pl.BlockSpec((tm, tn), lambda i, j: (i, j))],
        out_shape=[jax.ShapeDtypeStruct((s, D_FF), F32), jax.ShapeDtypeStruct((s, D_FF), BF16)],
        compiler_params=_params(("parallel", "parallel")),
    )(hm, w_up)


def _mlp_down_loss(u, w_down, x1, target):
    s = u.shape[0]
    tm = min(256, s)

    def body(u_ref, w_ref, x1_ref, t_ref, dy_ref, loss_ref):
        i = pl.program_id(0)

        @pl.when(i == 0)
        def _():
            loss_ref[...] = jnp.zeros_like(loss_ref)

        y = x1_ref[...] + _dot(u_ref[...], w_ref[...])
        err = y - t_ref[...]
        dy_ref[...] = err * (1.0 / D_MODEL)
        part = jnp.sum(jnp.sum(err * err, axis=-1, keepdims=True) * (1.0 / D_MODEL), axis=0, keepdims=True)
        loss_ref[...] += 0.5 * part

    row = pl.BlockSpec((tm, D_MODEL), lambda i: (i, 0))
    return pl.pallas_call(
        body, name="mlp_down_loss", grid=(s // tm,),
        in_specs=[pl.BlockSpec((tm, D_FF), lambda i: (i, 0)), _full(w_down.shape), row, row],
        out_specs=[row, _full((1, 1))],
        out_shape=[jax.ShapeDtypeStruct((s, D_MODEL), F32), jax.ShapeDtypeStruct((1, 1), F32)],
        compiler_params=_params(("arbitrary",)),
    )(u, w_down, x1, target)


def _mlp_bwd_act(dy, w_down, a):
    s = dy.shape[0]
    tm, tn = min(512, s), 1024

    def body(dy_ref, w_ref, a_ref, da_ref):
        du = _dot(dy_ref[...].astype(BF16), w_ref[...], NT)
        da_ref[...] = (du * (2.0 * jnp.maximum(a_ref[...], 0.0))).astype(BF16)

    return pl.pallas_call(
        body, name="mlp_bwd_act", grid=(s // tm, D_FF // tn),
        in_specs=[pl.BlockSpec((tm, D_MODEL), lambda i, j: (i, 0)), pl.BlockSpec((tn, D_MODEL), lambda i, j: (j, 0)),
                  pl.BlockSpec((tm, tn), lambda i, j: (i, j))],
        out_specs=pl.BlockSpec((tm, tn), lambda i, j: (i, j)),
        out_shape=jax.ShapeDtypeStruct((s, D_FF), BF16),
        compiler_params=_params(("parallel", "parallel")),
    )(dy, w_down, a)


def _rms_bwd(xv, g, dh, skip):
    r = lax.rsqrt(jnp.mean(xv * xv, axis=-1, keepdims=True) + EPS)
    n = xv * r
    dn = dh * g
    dx = skip + r * (dn - n * jnp.mean(dn * n, axis=-1, keepdims=True))
    return dx, jnp.sum(dh * n, axis=0, keepdims=True)


def _mlp_bwd_x(da, w_up, x1, dy, g_mlp):
    s = da.shape[0]
    tm = min(256, s)

    def body(da_ref, w_ref, x1_ref, dy_ref, g_ref, dx1_ref, dg_ref):
        i = pl.program_id(0)

        @pl.when(i == 0)
        def _():
            dg_ref[...] = jnp.zeros_like(dg_ref)

        dhm = _dot(da_ref[...], w_ref[...], NT)
        dx, dg = _rms_bwd(x1_ref[...], g_ref[...], dhm, dy_ref[...])
        dx1_ref[...] = dx
        dg_ref[...] += dg

    row = pl.BlockSpec((tm, D_MODEL), lambda i: (i, 0))
    return pl.pallas_call(
        body, name="mlp_bwd_x", grid=(s // tm,),
        in_specs=[pl.BlockSpec((tm, D_FF), lambda i: (i, 0)), _full(w_up.shape), row, row, _full((1, D_MODEL))],
        out_specs=[row, _full((1, D_MODEL))],
        out_shape=[jax.ShapeDtypeStruct((s, D_MODEL), F32), jax.ShapeDtypeStruct((1, D_MODEL), F32)],
        compiler_params=_params(("arbitrary",)),
    )(da, w_up, x1, dy, g_mlp)


def _merge_bwd(dx1, oa, of, om, proj, b_gate, wa, wf, wm, w_out):
    s = dx1.shape[0]
    tm = min(256, s)

    def body(dx1_ref, oa_ref, of_ref, om_ref, gl_ref, bg_ref, wa_ref, wf_ref, wm_ref, wo_ref,
             dp_ref, doa_ref, dof_ref, dom_ref, dya_ref, dyf_ref, dym_ref, dbg_ref):
        i = pl.program_id(0)

        @pl.when(i == 0)
        def _():
            dbg_ref[...] = jnp.zeros_like(dbg_ref)

        dmerged = _dot(dx1_ref[...].astype(BF16), wo_ref[...], NT)
        branches = ((oa_ref, wa_ref, doa_ref, dya_ref), (of_ref, wf_ref, dof_ref, dyf_ref), (om_ref, wm_ref, dom_ref, dym_ref))
        for b, (o_ref, w_ref, do_ref, dyb_ref) in enumerate(branches):
            cs = slice(b * D_MODEL, (b + 1) * D_MODEL)
            y = _dot(o_ref[...], w_ref[...])
            g = _sigmoid(gl_ref[:, cs] + bg_ref[:, cs])
            dz = (dmerged * y) * g * (1.0 - g)
            dp_ref[:, cs] = dz.astype(BF16)
            dbg_ref[:, cs] += jnp.sum(dz, axis=0, keepdims=True)
            dyb = (dmerged * g).astype(BF16)
            dyb_ref[...] = dyb
            do_ref[...] = _dot(dyb, w_ref[...], NT).astype(BF16)

    row = lambda w: pl.BlockSpec((tm, w), lambda i: (i, 0))
    sd = lambda w: jax.ShapeDtypeStruct((s, w), BF16)
    return pl.pallas_call(
        body, name="merge_bwd", grid=(s // tm,),
        in_specs=[row(D_MODEL), row(512), row(512), row(512), row(HALF_W), _full((1, HALF_W)),
                  _full(wa.shape), _full(wf.shape), _full(wm.shape), _full(w_out.shape)],
        out_specs=[row(HALF_W), row(512), row(512), row(512), row(D_MODEL), row(D_MODEL), row(D_MODEL), _full((1, HALF_W))],
        out_shape=[sd(PROJ_W), sd(512), sd(512), sd(512), sd(D_MODEL), sd(D_MODEL), sd(D_MODEL),
                   jax.ShapeDtypeStruct((1, HALF_W), F32)],
        compiler_params=_params(("arbitrary",)),
    )(dx1, oa, of, om, proj, b_gate, wa, wf, wm, w_out)


def _swa_bwd(qa, kp, vp, bias, sink, doa):
    s = qa.shape[0]
    nb = s // SWA_BLOCK

    def body(sink_ref, q_ref, kp_ref, vp_ref, bias_ref, do_ref, dq_ref, dkp_ref, dvp_ref, dbias_ref, dsink_ref, sk_acc):
        n = pl.program_id(0)

        @pl.when(n == 0)
        def _():
            dkp_ref[...] = jnp.zeros_like(dkp_ref)
            dvp_ref[...] = jnp.zeros_like(dvp_ref)
            dbias_ref[...] = jnp.zeros_like(dbias_ref)
            sk_acc[...] = jnp.zeros_like(sk_acc)

        start = pl.multiple_of(n * SWA_BLOCK, SWA_BLOCK)
        win = pl.ds(start, 2 * SWA_BLOCK)
        k2 = kp_ref[win, :]
        v2 = vp_ref[win, :]
        valid = _swa_valid(n)
        for kv in range(2):
            hs_kv = slice(kv * HEAD, (kv + 1) * HEAD)
            kh = k2[:, hs_kv]
            vh = v2[:, hs_kv]
            dk2 = jnp.zeros((2 * SWA_BLOCK, HEAD), F32)
            dv2 = jnp.zeros((2 * SWA_BLOCK, HEAD), F32)
            for g in range(4):
                h = kv * 4 + g
                hs = slice(h * HEAD, (h + 1) * HEAD)
                qh = q_ref[:, hs]
                doh = do_ref[:, hs]
                sc = _dot(qh, kh, NT) * 0.125 + bias_ref[h]
                sc = jnp.where(valid, sc, NEG)
                sk = sink_ref[h]
                mx = jnp.maximum(jnp.max(sc, axis=-1, keepdims=True), sk)
                p = jnp.exp(sc - mx)
                esk = jnp.exp(sk - mx)
                den = jnp.sum(p, axis=-1, keepdims=True) + esk
                p = p / den
                dp = _dot(doh, vh, NT)
                delta = jnp.sum(p * dp, axis=-1, keepdims=True)
                ds = p * (dp - delta)
                sk_acc[:, h:h + 1] += -(esk / den) * delta
                dbias_ref[h] += ds
                dsb = (ds * 0.125).astype(BF16)
                dq_ref[:, hs] = _dot(dsb, kh)
                dk2 = dk2 + _dot(dsb, qh, TN)
                dv2 = dv2 + _dot(p.astype(BF16), doh, TN)
            dkp_ref[win, hs_kv] += dk2
            dvp_ref[win, hs_kv] += dv2

        @pl.when(n == nb - 1)
        def _():
            dsink_ref[...] = jnp.sum(sk_acc[...], axis=0, keepdims=True)

    return pl.pallas_call(
        body, name="swa_bwd", grid=(nb,),
        in_specs=[pl.BlockSpec(memory_space=pltpu.SMEM),
                  pl.BlockSpec((SWA_BLOCK, 512), lambda n: (n, 0)),
                  _full(kp.shape), _full(vp.shape), _full(bias.shape),
                  pl.BlockSpec((SWA_BLOCK, 512), lambda n: (n, 0))],
        out_specs=[pl.BlockSpec((SWA_BLOCK, 512), lambda n: (n, 0)), _full(kp.shape), _full(vp.shape),
                   _full(bias.shape), _full((1, 128))],
        out_shape=[jax.ShapeDtypeStruct((s, 512), F32), jax.ShapeDtypeStruct(kp.shape, F32),
                   jax.ShapeDtypeStruct(vp.shape, F32), jax.ShapeDtypeStruct(bias.shape, F32),
                   jax.ShapeDtypeStruct((1, 128), F32)],
        scratch_shapes=[pltpu.VMEM((SWA_BLOCK, 128), F32)],
        compiler_params=_params(("arbitrary",)),
    )(sink, qa, kp, vp, bias, doa)


def _fox_bwd(q, k, v, do, o, cc4, lse4):
    s = q.shape[0]
    t = min(FOX_T, s)
    tq = min(FOX_TK, s)
    nq = s // t
    nqt = s // tq

    def body(q_ref, k_ref, v_ref, do_ref, o_ref, cc_ref, lse_ref,
             dqt_ref, dk_ref, dv_ref, dck_ref, dcq_ref, delta_ref, dk0, dk1, dv0, dv1, ds0, ds1):
        j = pl.program_id(1)

        @pl.when(j == 0)
        def _():
            dqt_ref[...] = jnp.zeros_like(dqt_ref)
            dcq_ref[...] = jnp.zeros_like(dcq_ref)
            lane8 = lax.broadcasted_iota(jnp.int32, (8, 128), 1)
            row8 = lax.broadcasted_iota(jnp.int32, (8, 128), 0)
            sel = jnp.where((lane8 // HEAD) == row8, 1.0, 0.0).astype(F32)

            def dl(i, c):
                rows = pl.ds(pl.multiple_of(i * tq, tq), tq)
                pr = do_ref[rows, :].astype(F32) * o_ref[rows, :].astype(F32)
                delta_ref[:, rows] = lax.dot_general(sel, pr, NT, precision=lax.Precision.HIGHEST,
                                                     preferred_element_type=F32)
                return c

            lax.fori_loop(0, nqt, dl, 0)

        kj = k_ref[...]
        vj = v_ref[...]
        ks = pl.ds(pl.multiple_of(j * t, t), t)
        kt = (kj.astype(F32) * 0.125).T.astype(BF16)
        ke = [jnp.where(_head_mask(e), kj, jnp.zeros_like(kj)) for e in range(2)]
        ve = [jnp.where(_head_mask(e), vj, jnp.zeros_like(vj)) for e in range(2)]
        kte = [jnp.where(_head_rows(e), kt, jnp.zeros_like(kt)) for e in range(2)]
        ck = [cc_ref[0, ks, e:e + 1] for e in range(2)]
        accs = ((dk0, dv0, ds0), (dk1, dv1, ds1))
        for refs in accs:
            for r in refs:
                r[...] = jnp.zeros_like(r)
        i_first = (j * t) // tq
        krow = lax.broadcasted_iota(jnp.int32, (t, tq), 0) + j * t
        qcol = lax.broadcasted_iota(jnp.int32, (t, tq), 1) + i_first * tq

        def step(i, c, masked):
            rows = pl.ds(pl.multiple_of(i * tq, tq), tq)
            qs = q_ref[rows, :] * jnp.asarray(0.125, BF16)
            doi = do_ref[rows, :]
            for e in range(2):
                dk_acc, dv_acc, ds_acc = accs[e]
                st = _dot(ke[e], qs, NT) - ck[e]
                if masked:
                    st = jnp.where(krow <= qcol, st, NEG)
                pt = jnp.exp(st - lse_ref[0, e:e + 1, rows])
                dpt = _dot(ve[e], doi, NT)
                dst = pt * (dpt - delta_ref[e:e + 1, rows])
                dsb = dst.astype(BF16)
                dv_acc[...] += _dot(pt.astype(BF16), doi)
                dk_acc[...] += _dot(dsb, qs)
                dqt_ref[:, rows] += _dot(kte[e], dsb)
                ds_acc[...] += dst
                dcq_ref[0, e:e + 1, rows] += jnp.sum(dst, axis=0, keepdims=True)
            return c

        step(i_first, 0, True)
        lax.fori_loop(i_first + 1, nqt, functools.partial(step, masked=False), 0)
        m0 = _head_mask(0)
        dk_ref[...] = jnp.where(m0, dk0[...], dk1[...])
        dv_ref[...] = jnp.where(m0, dv0[...], dv1[...])
        lane = lax.broadcasted_iota(jnp.int32, (t, 128), 1)
        c0 = jnp.sum(ds0[...], axis=-1, keepdims=True)
        c1 = jnp.sum(ds1[...], axis=-1, keepdims=True)
        dck_ref[0] = jnp.where(lane == 0, c0, jnp.where(lane == 1, c1, 0.0))

    res = lambda: pl.BlockSpec((s, 128), lambda hp, j: (0, hp))
    blk = lambda: pl.BlockSpec((t, 128), lambda hp, j: (j, hp))
    return pl.pallas_call(
        body, name="fox_bwd", grid=(4, nq),
        in_specs=[res(), blk(), blk(), res(), res(), pl.BlockSpec((1, s, 128), lambda hp, j: (hp, 0, 0)),
                  pl.BlockSpec((1, 8, s), lambda hp, j: (hp, 0, 0))],
        out_specs=[pl.BlockSpec((128, s), lambda hp, j: (hp, 0)), blk(), blk(),
                   pl.BlockSpec((1, t, 128), lambda hp, j: (hp, j, 0)),
                   pl.BlockSpec((1, 8, s), lambda hp, j: (hp, 0, 0))],
        out_shape=[jax.ShapeDtypeStruct((512, s), F32), jax.ShapeDtypeStruct((s, 512), F32),
                   jax.ShapeDtypeStruct((s, 512), F32), jax.ShapeDtypeStruct((4, s, 128), F32),
                   jax.ShapeDtypeStruct((4, 8, s), F32)],
        scratch_shapes=[pltpu.VMEM((8, s), F32)] + [pltpu.VMEM((t, 128), F32)] * 4 + [pltpu.VMEM((t, tq), F32)] * 2,
        compiler_params=_params(("arbitrary", "arbitrary")),
    )(q, k, v, do, o, cc4, lse4)


def _mem_bwd(qm, mk, mv, dom):
    s = qm.shape[0]
    tq = min(512, s)

    def body(q_ref, mk_ref, mv_ref, do_ref, dq_ref, dmk_ref, dmv_ref):
        i = pl.program_id(0)

        @pl.when(i == 0)
        def _():
            dmk_ref[...] = jnp.zeros_like(dmk_ref)
            dmv_ref[...] = jnp.zeros_like(dmv_ref)

        for h in range(MEM_HEADS):
            hs = slice(h * 128, (h + 1) * 128)
            qh = q_ref[:, hs]
            doh = do_ref[:, hs]
            sc = _dot(qh, mk_ref[:, hs], NT) * MEM_SCALE
            mx = jnp.max(sc, axis=-1, keepdims=True)
            p = jnp.exp(sc - mx)
            p = p / jnp.sum(p, axis=-1, keepdims=True)
            dp = _dot(doh, mv_ref[:, hs], NT)
            ds = p * (dp - jnp.sum(p * dp, axis=-1, keepdims=True))
            dsb = (ds * MEM_SCALE).astype(BF16)
            dq_ref[:, hs] = _dot(dsb, mk_ref[:, hs])
            dmk_ref[:, hs] += _dot(dsb, qh, TN)
            dmv_ref[:, hs] += _dot(p.astype(BF16), doh, TN)

    return pl.pallas_call(
        body, name="mem_bwd", grid=(s // tq,),
        in_specs=[pl.BlockSpec((tq, 512), lambda i: (i, 0)), _full(mk.shape), _full(mv.shape),
                  pl.BlockSpec((tq, 512), lambda i: (i, 0))],
        out_specs=[pl.BlockSpec((tq, 512), lambda i: (i, 0)), _full(mk.shape), _full(mv.shape)],
        out_shape=[jax.ShapeDtypeStruct((s, 512), F32), jax.ShapeDtypeStruct(mk.shape, F32),
                   jax.ShapeDtypeStruct(mv.shape, F32)],
        compiler_params=_params(("arbitrary",)),
    )(qm, mk, mv, dom)


def _memkv_bwd(dmk, dmv, kv_raw, kn_mem, mem, g_mem, mem_n, w_kv):
    def body(dmk_ref, dmv_ref, kv_ref, kn_ref, mem_ref, g_ref, mn_ref, w_ref, dw_ref, dkn_ref, dg_ref, dkv_ref):
        dkn = jnp.zeros((1, 128), F32)
        for h in range(MEM_HEADS):
            hs = slice(h * 128, (h + 1) * 128)
            v = kv_ref[:, hs]
            r = lax.rsqrt(jnp.mean(v * v, axis=-1, keepdims=True) + EPS)
            n = v * r
            dn = dmk_ref[:, hs]
            dkn = dkn + jnp.sum(dn * n, axis=0, keepdims=True)
            dng = dn * kn_ref[...]
            dkv_ref[:, hs] = (r * (dng - n * jnp.mean(dng * n, axis=-1, keepdims=True))).astype(BF16)
        dkv_ref[:, 512:1024] = dmv_ref[...].astype(BF16)
        dkn_ref[...] = dkn
        dkv = dkv_ref[...]
        dw_ref[...] = _dot(mn_ref[...], dkv, TN)
        dmn = _dot(dkv, w_ref[...], NT)
        xv = mem_ref[...]
        r = lax.rsqrt(jnp.mean(xv * xv, axis=-1, keepdims=True) + EPS)
        dg_ref[...] = jnp.sum(dmn * (xv * r), axis=0, keepdims=True)

    m = mem.shape[0]
    return pl.pallas_call(
        body, name="memkv_bwd",
        out_shape=[jax.ShapeDtypeStruct((D_MODEL, 1024), F32), jax.ShapeDtypeStruct((1, 128), F32),
                   jax.ShapeDtypeStruct((1, D_MODEL), F32)],
        scratch_shapes=[pltpu.VMEM((m, 1024), BF16)],
        compiler_params=pltpu.CompilerParams(vmem_limit_bytes=VMEM_LIMIT),
    )(dmk, dmv, kv_raw, kn_mem, mem, g_mem, mem_n, w_kv)


def _fox_gate_bwd(dc, proj, b_forget128):
    s = dc.shape[0]
    tm = min(512, s)
    nt = s // tm

    def body(dc_ref, p_ref, b_ref, dfl_ref, db_ref, carry_ref):
        i = pl.program_id(0)

        @pl.when(i == 0)
        def _():
            carry_ref[...] = jnp.zeros_like(carry_ref)
            db_ref[...] = jnp.zeros_like(db_ref)

        dcv = dc_ref[...]
        dlogf = jnp.dot(_tri(tm, False), dcv, precision=lax.Precision.HIGHEST, preferred_element_type=F32) + carry_ref[...]
        carry_ref[...] += jnp.sum(dcv, axis=0, keepdims=True)
        z = p_ref[...] + b_ref[...]
        dfl = dlogf * (1.0 / (1.0 + jnp.exp(z)))
        dfl_ref[...] = dfl.astype(BF16)
        db_ref[...] += jnp.sum(dfl, axis=0, keepdims=True)

    return pl.pallas_call(
        body, name="fox_gate_bwd", grid=(nt,),
        in_specs=[pl.BlockSpec((tm, 128), lambda i: (nt - 1 - i, 0)),
                  pl.BlockSpec((tm, 128), lambda i: (nt - 1 - i, FL0 // 128)), _full((1, 128))],
        out_specs=[pl.BlockSpec((tm, 128), lambda i: (nt - 1 - i, 0)), _full((1, 128))],
        out_shape=[jax.ShapeDtypeStruct((s, 128), BF16), jax.ShapeDtypeStruct((1, 128), F32)],
        scratch_shapes=[pltpu.VMEM((1, 128), F32)],
        compiler_params=_params(("arbitrary",)),
    )(dc, proj, b_forget128)


def _proj_pre_bwd(dproj, proj, dqf, dkf, dvf, dqm, dqa, dka, dva, dfl, gq_fox, gk_fox, gq_mem, gq_swa, gk_swa):
    s = proj.shape[0]
    tm = min(256, s)

    def body(dp_in, p_ref, dqf_ref, dkf_ref, dvf_ref, dqm_ref, dqa_ref, dka_ref, dva_ref, dfl_ref,
             gqf, gkf, gqm, gqa, gka, dp_ref, dgn_ref):
        i = pl.program_id(0)

        @pl.when(i == 0)
        def _():
            dgn_ref[...] = jnp.zeros_like(dgn_ref)

        def norm_bwd(off, width, hd, g_ref, dn_ref, slot):
            acc = jnp.zeros((1, 128), F32)
            for b in range(width // 128):
                v = p_ref[:, off + b * 128: off + (b + 1) * 128]
                r = lax.rsqrt(_group_mean(v * v, hd) + EPS)
                n = v * r
                dn = dn_ref[:, b * 128:(b + 1) * 128]
                acc = acc + jnp.sum(dn * n, axis=0, keepdims=True)
                dng = dn * g_ref[...]
                dp_ref[:, off + b * 128: off + (b + 1) * 128] = (r * (dng - n * _group_mean(dng * n, hd))).astype(BF16)
            dgn_ref[slot:slot + 1, :] += acc

        norm_bwd(H_QF, 512, HEAD, gqf, dqf_ref, 0)
        norm_bwd(H_KF, 512, HEAD, gkf, dkf_ref, 1)
        dp_ref[:, H_VF:H_VF + 512] = dvf_ref[...].astype(BF16)
        norm_bwd(H_QM, 512, MEM_HEAD, gqm, dqm_ref, 2)
        norm_bwd(H_QA, 512, HEAD, gqa, dqa_ref, 3)
        norm_bwd(H_KA, 128, HEAD, gka, dka_ref, 4)
        dp_ref[:, H_VA:H_VA + 128] = dva_ref[...].astype(BF16)
        dp_ref[:, H_FL:H_FL + 128] = dfl_ref[...]
        dp_ref[:, H_FL + 128:HALF_W] = jnp.zeros((tm, HALF_W - H_FL - 128), BF16)

    row = lambda w: pl.BlockSpec((tm, w), lambda i: (i, 0))
    g_spec = _full((1, 128))
    return pl.pallas_call(
        body, name="proj_pre_bwd", grid=(s // tm,),
        in_specs=[pl.BlockSpec(memory_space=pl.ANY), pl.BlockSpec((tm, HALF_W), lambda i: (i, 1)),
                  row(512), row(512), row(512), row(512), row(512), row(128), row(128), row(128),
                  g_spec, g_spec, g_spec, g_spec, g_spec],
        out_specs=[pl.BlockSpec((tm, HALF_W), lambda i: (i, 1)), _full((8, 128))],
        out_shape=[jax.ShapeDtypeStruct((s, PROJ_W), BF16), jax.ShapeDtypeStruct((8, 128), F32)],
        input_output_aliases={0: 0},
        compiler_params=_params(("arbitrary",)),
    )(dproj, proj, dqf, dkf, dvf, dqm, dqa, dka, dva, dfl, gq_fox, gk_fox, gq_mem, gq_swa, gk_swa)


def _in_bwd_x(dproj, w_in_p, x, g_mix, dx1):
    s = x.shape[0]
    tm, tk = min(512, s), 1536
    nk = PROJ_W // tk

    def body(dp_ref, w_ref, x_ref, g_ref, dx1_ref, gx_ref, dg_ref, acc_ref):
        i, kk = pl.program_id(0), pl.program_id(1)

        @pl.when((i == 0) & (kk == 0))
        def _():
            dg_ref[...] = jnp.zeros_like(dg_ref)

        prod = _dot(dp_ref[...], w_ref[...], NT)

        @pl.when(kk == 0)
        def _():
            acc_ref[...] = prod

        @pl.when(kk > 0)
        def _():
            acc_ref[...] += prod

        @pl.when(kk == nk - 1)
        def _():
            dx, dg = _rms_bwd(x_ref[...], g_ref[...], acc_ref[...], dx1_ref[...])
            gx_ref[...] = dx
            dg_ref[...] += dg

    row = pl.BlockSpec((tm, D_MODEL), lambda i, kk: (i, 0))
    return pl.pallas_call(
        body, name="in_bwd_x", grid=(s // tm, nk),
        in_specs=[pl.BlockSpec((tm, tk), lambda i, kk: (i, kk)), pl.BlockSpec((D_MODEL, tk), lambda i, kk: (0, kk)),
                  row, _full((1, D_MODEL)), row],
        out_specs=[row, _full((1, D_MODEL))],
        out_shape=[jax.ShapeDtypeStruct((s, D_MODEL), F32), jax.ShapeDtypeStruct((1, D_MODEL), F32)],
        scratch_shapes=[pltpu.VMEM((tm, D_MODEL), F32)],
        compiler_params=_params(("arbitrary", "arbitrary")),
    )(dproj, w_in_p, x, g_mix, dx1)


def _rel_bias_bwd(dbias, bucket):
    def body(db_ref, bk_ref, o_ref):
        bk = bk_ref[...]
        lane = lax.broadcasted_iota(jnp.int32, (1, 128), 1)
        for b in range(REL_BUCKETS):
            sel = bk == b
            acc = jnp.zeros((1, 128), F32)
            for h in range(SWA_HEADS):
                tot = jnp.sum(jnp.sum(jnp.where(sel, db_ref[h], 0.0), axis=-1, keepdims=True), axis=0, keepdims=True)
                acc = jnp.where(lane == h, tot, acc)
            o_ref[b:b + 1, :] = acc

    return pl.pallas_call(
        body, name="rel_bias_bwd",
        out_shape=jax.ShapeDtypeStruct((REL_BUCKETS, 128), F32),
        compiler_params=pltpu.CompilerParams(vmem_limit_bytes=VMEM_LIMIT),
    )(dbias, bucket)


def _my_place():
    return lax.axis_index("x"), lax.axis_index("y"), lax.axis_index("c")


def _peer(place, k):
    x, y, c = place
    return (1 - x if k & 4 else x, 1 - y if k & 2 else y, 1 - c if k & 1 else c)


def _index(place):
    x, y, c = place
    return 4 * x + 2 * y + c


def _all_gather(shard, name):
    def body(x_ref, out_ref, send_sems, recv_sems, local_sem):
        me = _my_place()
        mine = pltpu.make_async_copy(x_ref, out_ref.at[_index(me)], local_sem)
        mine.start()
        sends = []
        for k in range(1, N_DEV):
            cp = pltpu.make_async_remote_copy(
                src_ref=x_ref, dst_ref=out_ref.at[_index(me)], send_sem=send_sems.at[k - 1], recv_sem=recv_sems.at[k - 1],
                device_id=_peer(me, k), device_id_type=MESH)
            cp.start()
            sends.append(cp)
        for k in range(1, N_DEV):
            peer = _peer(me, k)
            pltpu.make_async_remote_copy(
                src_ref=x_ref, dst_ref=out_ref.at[_index(peer)], send_sem=send_sems.at[k - 1], recv_sem=recv_sems.at[k - 1],
                device_id=peer, device_id_type=MESH).wait_recv()
        for cp in sends:
            cp.wait_send()
        mine.wait()

    any_spec = pl.BlockSpec(memory_space=pl.ANY)
    return pl.pallas_call(
        body, name=name, in_specs=[any_spec], out_specs=any_spec,
        out_shape=jax.ShapeDtypeStruct((N_DEV,) + shard.shape, shard.dtype),
        scratch_shapes=[pltpu.SemaphoreType.DMA((N_DEV - 1,)), pltpu.SemaphoreType.DMA((N_DEV - 1,)), pltpu.SemaphoreType.DMA(())],
    )(shard)


def _exchange(parts, name):
    def body(p_ref, out_ref, send_sems, recv_sems, local_sem):
        me = _my_place()
        mine = pltpu.make_async_copy(p_ref.at[_index(me)], out_ref.at[0], local_sem)
        mine.start()
        sends = []
        for k in range(1, N_DEV):
            peer = _peer(me, k)
            cp = pltpu.make_async_remote_copy(
                src_ref=p_ref.at[_index(peer)], dst_ref=out_ref.at[k], send_sem=send_sems.at[k - 1], recv_sem=recv_sems.at[k - 1],
                device_id=peer, device_id_type=MESH)
            cp.start()
            sends.append(cp)
        for cp in sends:
            cp.wait_recv()
        for cp in sends:
            cp.wait_send()
        mine.wait()

    any_spec = pl.BlockSpec(memory_space=pl.ANY)
    return pl.pallas_call(
        body, name=name, in_specs=[any_spec], out_specs=any_spec,
        out_shape=jax.ShapeDtypeStruct(parts.shape, parts.dtype),
        scratch_shapes=[pltpu.SemaphoreType.DMA((N_DEV - 1,)), pltpu.SemaphoreType.DMA((N_DEV - 1,)), pltpu.SemaphoreType.DMA(())],
    )(parts)


def _adam_math(w, g, m, v):
    m2 = ADAM_B1 * m + (1.0 - ADAM_B1) * g
    v2 = ADAM_B2 * v + (1.0 - ADAM_B2) * (g * g)
    m_hat = m2 / (1.0 - ADAM_B1 ** ADAM_STEP)
    v_hat = v2 / (1.0 - ADAM_B2 ** ADAM_STEP)
    delta = -ADAM_LR * (m_hat / (jnp.sqrt(v_hat) + ADAM_EPS) + ADAM_WD * w)
    return delta, m2, v2


def _adamw(parts, w, m, v, name):
    a, b = w.shape
    ta = min(128, a)

    def body(p_ref, w_ref, m_ref, v_ref, g_ref, d_ref, m2_ref, v2_ref):
        g = p_ref[0].astype(F32)
        for k in range(1, N_DEV):
            g = g + p_ref[k].astype(F32)
        delta, m2, v2 = _adam_math(w_ref[...], g, m_ref[...], v_ref[...])
        g_ref[...] = g
        d_ref[...] = delta
        m2_ref[...] = m2
        v2_ref[...] = v2

    blk = pl.BlockSpec((ta, b), lambda i: (i, 0))
    sd = jax.ShapeDtypeStruct((a, b), F32)
    return pl.pallas_call(
        body, name=name, grid=(a // ta,),
        in_specs=[pl.BlockSpec((N_DEV, ta, b), lambda i: (0, i, 0)), blk, blk, blk],
        out_specs=[blk, blk, blk, blk], out_shape=[sd, sd, sd, sd],
        compiler_params=_params(("parallel",)),
    )(parts, w, m, v)


def _bucket_table():
    t_loc = jnp.arange(SWA_BLOCK)[:, None] + SWA_BLOCK
    s_loc = jnp.arange(2 * SWA_BLOCK)[None, :]
    dist = t_loc - s_loc
    max_exact = REL_BUCKETS // 2
    d = jnp.maximum(dist, 0)
    df = jnp.maximum(d, 1).astype(F32)
    large = max_exact + (jnp.log(df / max_exact) / math.log(REL_MAX_DIST / max_exact) * (REL_BUCKETS - max_exact)).astype(jnp.int32)
    large = jnp.minimum(large, REL_BUCKETS - 1)
    bucket = jnp.where(d < max_exact, d, large)
    band = (dist >= 0) & (dist < SWA_BLOCK)
    return bucket, band


def _tile2(g):
    return jnp.concatenate([g, g], axis=1) if g.shape[1] == HEAD else g


def _w_in_padded(w_in):
    z = jnp.zeros((w_in.shape[0], PROJ_W - FL0 - 8), w_in.dtype)
    return jnp.concatenate([w_in[:, 2824:5896], w_in[:, 768:1280], w_in[:, 1280:1792], w_in[:, 1792:2304], w_in[:, 2312:2824],
                            w_in[:, 0:512], w_in[:, 512:640], w_in[:, 640:768], w_in[:, 2304:2312], z], axis=1)


def _w_in_unpadded(dwp):
    return jnp.concatenate([dwp[:, QA0:QA0 + 512], dwp[:, KA0:KA0 + 128], dwp[:, VA0:VA0 + 128], dwp[:, QF0:QF0 + 512],
                            dwp[:, KF0:KF0 + 512], dwp[:, VF0:VF0 + 512], dwp[:, FL0:FL0 + 8], dwp[:, QM0:QM0 + 512],
                            dwp[:, GL0:GL0 + 3072]], axis=1)


def _local_step(x, mem, target, p, w):
    s = x.shape[0]
    bucket, band = _bucket_table()
    bucket_m = jnp.where(band, bucket, -1).astype(jnp.int32)
    bias = _bias_table(p["rel_bias"], bucket_m)
    gqf, gkf, gqa, gka = _tile2(p["qn_fox"]), _tile2(p["kn_fox"]), _tile2(p["qn_swa"]), _tile2(p["kn_swa"])
    gqm = p["qn_mem"]
    bf128 = jnp.pad(p["b_forget"], ((0, 0), (0, 120)))
    sink = p["sink_swa"].reshape(8)

    h = _rms_fwd(x, p["g_mix"], "rms_mix")
    proj = _mm(h, w["w_in"], "nn", F32, 512, 768, 1024, "proj")
    qf, kf, vf, qm, qa, ka, va = _proj_post(proj, gqf, gkf, gqm, gqa, gka)
    cc4 = _fox_gate_fwd(proj, bf128)
    mem_n, kv_raw, mk, mv = _memkv_fwd(mem, p["g_mem"], w["w_mem_kv"], p["kn_mem"])
    kp = jnp.pad(ka, ((SWA_BLOCK, 0), (0, 0)))
    vp = jnp.pad(va, ((SWA_BLOCK, 0), (0, 0)))
    oa = _swa_fwd(qa, kp, vp, bias, sink)
    of, lse4 = _fox_fwd(qf, kf, jnp.transpose(vf), cc4)
    om = _mem_fwd(qm, mk, mv)
    x1, hm, merged = _merge_fwd(x, oa, of, om, proj, p["b_gate"], w["w_o_swa"], w["w_o_fox"], w["w_o_mem"], w["w_out"], p["g_mlp"])
    a, u = _mlp_up(hm, w["w_mlp_up"])
    dy, loss = _mlp_down_loss(u, w["w_mlp_down"], x1, target)

    da = _mlp_bwd_act(dy, w["w_mlp_down"], a)
    dx1, dg_mlp = _mlp_bwd_x(da, w["w_mlp_up"], x1, dy, p["g_mlp"])
    dw_up = _mm(hm, da, "tn", F32, 512, 1024, 512, "dw_up")
    dw_down = _mm(u, dy, "tn", F32, 1024, 1024, 512, "dw_down")
    dproj, doa, dof, dom, dya, dyf, dym, db_gate = _merge_bwd(
        dx1, oa, of, om, proj, p["b_gate"], w["w_o_swa"], w["w_o_fox"], w["w_o_mem"], w["w_out"])
    dw_oa = _mm(oa, dya, "tn", F32, 512, 1024, 512, "dw_o_swa")
    dw_of = _mm(of, dyf, "tn", F32, 512, 1024, 512, "dw_o_fox")
    dw_om = _mm(om, dym, "tn", F32, 512, 1024, 512, "dw_o_mem")
    dw_out = _mm(merged, dx1, "tn", F32, 512, 1024, 512, "dw_out")

    dqa, dkp, dvp, dbias, dsink = _swa_bwd(qa, kp, vp, bias, sink, doa)
    dqf_t, dkf, dvf, dck4, dcq4 = _fox_bwd(qf, kf, vf, dof, of, cc4, lse4)
    dqf = jnp.transpose(dqf_t)
    dqm, dmk, dmv = _mem_bwd(qm, mk, mv, dom)
    dw_kv, dkn_mem, dg_mem = _memkv_bwd(dmk, dmv, kv_raw, p["kn_mem"], mem, p["g_mem"], mem_n, w["w_mem_kv"])

    dcq = jnp.transpose(dcq4[:, 0:2, :], (2, 0, 1)).reshape(s, 8)
    dck = jnp.transpose(dck4[:, :, 0:2], (1, 0, 2)).reshape(s, 8)
    dc = jnp.pad(dcq - dck, ((0, 0), (0, 120)))
    dfl, db_forget = _fox_gate_bwd(dc, proj, bf128)

    dproj, dgn = _proj_pre_bwd(dproj, proj, dqf, dkf, dvf, dqm, dqa, dkp[SWA_BLOCK:], dvp[SWA_BLOCK:], dfl,
                               gqf, gkf, gqm, gqa, gka)
    grad_x, dg_mix = _in_bwd_x(dproj, w["w_in"], x, p["g_mix"], dx1)
    dw_in = _mm(h, dproj, "tn", F32, 512, 1536, 512, "dw_in")
    d_rel = _rel_bias_bwd(dbias, bucket_m)

    fold = lambda r: dgn[r:r + 1, 0:HEAD] + dgn[r:r + 1, HEAD:128]
    small = {
        "g_mix": dg_mix, "b_gate": db_gate, "b_forget": db_forget[:, 0:8],
        "qn_swa": fold(3), "kn_swa": fold(4), "sink_swa": dsink[:, 0:8], "rel_bias": d_rel[:, 0:8],
        "qn_fox": fold(0), "kn_fox": fold(1), "g_mem": dg_mem, "qn_mem": dgn[2:3, :], "kn_mem": dkn_mem,
        "g_mlp": dg_mlp,
    }
    big = {"w_in": dw_in, "w_mem_kv": dw_kv, "w_o_swa": dw_oa, "w_o_fox": dw_of, "w_o_mem": dw_om,
           "w_out": dw_out, "w_mlp_up": dw_up, "w_mlp_down": dw_down}
    return loss, grad_x, big, small


SMALL = ("g_mix", "b_gate", "b_forget", "qn_swa", "kn_swa", "sink_swa", "rel_bias", "qn_fox", "kn_fox", "g_mem",
         "qn_mem", "kn_mem", "g_mlp")
BIG = ("w_in", "w_mem_kv", "w_o_swa", "w_o_fox", "w_o_mem", "w_out", "w_mlp_up", "w_mlp_down")
COL_SHARDED = ("w_in", "w_o_swa", "w_o_fox", "w_o_mem", "w_mlp_up")
WEIGHTS = ("g_mix", "w_in", "b_gate", "b_forget", "qn_swa", "kn_swa", "sink_swa", "rel_bias", "qn_fox", "kn_fox", "g_mem",
           "w_mem_kv", "qn_mem", "kn_mem", "w_o_swa", "w_o_fox", "w_o_mem", "w_out", "g_mlp", "w_mlp_up", "w_mlp_down")
SMALL_PAD = 7168


def _gathered_to_full(name, g):
    if name in COL_SHARDED:
        return jnp.transpose(g, (1, 0, 2)).reshape(g.shape[1], N_DEV * g.shape[2])
    return g.reshape(N_DEV * g.shape[1], g.shape[2])


def _full_to_parts(name, full, b):
    if name in COL_SHARDED:
        return jnp.transpose(full.reshape(full.shape[0], N_DEV, b), (1, 0, 2)).astype(BF16)
    return full.reshape(N_DEV, full.shape[0] // N_DEV, full.shape[1]).astype(BF16)


def _pack_small(d):
    flat = jnp.concatenate([d[n].reshape(-1) for n in SMALL])
    return jnp.pad(flat, (0, SMALL_PAD - flat.shape[0])).reshape(8, SMALL_PAD // 8)


def _unpack_small(packed, like):
    flat = packed.reshape(-1)
    out, off = {}, 0
    for n in SMALL:
        size = like[n].size
        out[n] = flat[off:off + size].reshape(like[n].shape)
        off += size
    return out


def _adamw_small(parts, w, m, v):
    def body(p_ref, w_ref, m_ref, v_ref, g_ref, d_ref, m2_ref, v2_ref):
        g = p_ref[0]
        for k in range(1, N_DEV):
            g = g + p_ref[k]
        delta, m2, v2 = _adam_math(w_ref[...], g, m_ref[...], v_ref[...])
        g_ref[...] = g
        d_ref[...] = delta
        m2_ref[...] = m2
        v2_ref[...] = v2

    sd = jax.ShapeDtypeStruct(w.shape, F32)
    return pl.pallas_call(body, name="adamw_small", out_shape=[sd, sd, sd, sd])(parts, w, m, v)


def kernel(x, mem, g_mix, w_in, b_gate, b_forget, qn_swa, kn_swa, sink_swa, rel_bias, qn_fox, kn_fox, g_mem, w_mem_kv, qn_mem, kn_mem, w_o_swa, w_o_fox, w_o_mem, w_out, g_mlp, w_mlp_up, w_mlp_down, loss_target, m_g_mix, m_w_in, m_b_gate, m_b_forget, m_qn_swa, m_kn_swa, m_sink_swa, m_rel_bias, m_qn_fox, m_kn_fox, m_g_mem, m_w_mem_kv, m_qn_mem, m_kn_mem, m_w_o_swa, m_w_o_fox, m_w_o_mem, m_w_out, m_g_mlp, m_w_mlp_up, m_w_mlp_down, v_g_mix, v_w_in, v_b_gate, v_b_forget, v_qn_swa, v_kn_swa, v_sink_swa, v_rel_bias, v_qn_fox, v_kn_fox, v_g_mem, v_w_mem_kv, v_qn_mem, v_kn_mem, v_w_o_swa, v_w_o_fox, v_w_o_mem, v_w_out, v_g_mlp, v_w_mlp_up, v_w_mlp_down):
    wts = dict(g_mix=g_mix, w_in=w_in, b_gate=b_gate, b_forget=b_forget, qn_swa=qn_swa, kn_swa=kn_swa, sink_swa=sink_swa,
               rel_bias=rel_bias, qn_fox=qn_fox, kn_fox=kn_fox, g_mem=g_mem, w_mem_kv=w_mem_kv, qn_mem=qn_mem, kn_mem=kn_mem,
               w_o_swa=w_o_swa, w_o_fox=w_o_fox, w_o_mem=w_o_mem, w_out=w_out, g_mlp=g_mlp, w_mlp_up=w_mlp_up,
               w_mlp_down=w_mlp_down)
    mom = dict(g_mix=m_g_mix, w_in=m_w_in, b_gate=m_b_gate, b_forget=m_b_forget, qn_swa=m_qn_swa, kn_swa=m_kn_swa,
               sink_swa=m_sink_swa, rel_bias=m_rel_bias, qn_fox=m_qn_fox, kn_fox=m_kn_fox, g_mem=m_g_mem, w_mem_kv=m_w_mem_kv,
               qn_mem=m_qn_mem, kn_mem=m_kn_mem, w_o_swa=m_w_o_swa, w_o_fox=m_w_o_fox, w_o_mem=m_w_o_mem, w_out=m_w_out,
               g_mlp=m_g_mlp, w_mlp_up=m_w_mlp_up, w_mlp_down=m_w_mlp_down)
    var = dict(g_mix=v_g_mix, w_in=v_w_in, b_gate=v_b_gate, b_forget=v_b_forget, qn_swa=v_qn_swa, kn_swa=v_kn_swa,
               sink_swa=v_sink_swa, rel_bias=v_rel_bias, qn_fox=v_qn_fox, kn_fox=v_kn_fox, g_mem=v_g_mem, w_mem_kv=v_w_mem_kv,
               qn_mem=v_qn_mem, kn_mem=v_kn_mem, w_o_swa=v_w_o_swa, w_o_fox=v_w_o_fox, w_o_mem=v_w_o_mem, w_out=v_w_out,
               g_mlp=v_g_mlp, w_mlp_up=v_w_mlp_up, w_mlp_down=v_w_mlp_down)

    full = {}
    for n in BIG:
        shard = wts[n][0].astype(BF16)
        full[n] = _gathered_to_full(n, _all_gather(shard, "ag_" + n))
    full["w_in"] = _w_in_padded(full["w_in"])
    small_p = {n: wts[n] for n in SMALL}

    loss, grad_x, big_g, small_g = _local_step(x[0], mem[0], loss_target[0], small_p, full)
    big_g["w_in"] = _w_in_unpadded(big_g["w_in"])

    grads, delta, new_m, new_v = {}, {}, {}, {}
    for n in BIG:
        shard_shape = wts[n].shape[1:]
        parts = _exchange(_full_to_parts(n, big_g[n], shard_shape[1]), "rs_" + n)
        g, d, m2, v2 = _adamw(parts, wts[n][0], mom[n][0], var[n][0], "adamw_" + n)
        grads[n], delta[n], new_m[n], new_v[n] = g[None], d[None], m2[None], v2[None]

    gathered = _all_gather(_pack_small(small_g), "ag_small")
    g, d, m2, v2 = _adamw_small(gathered, _pack_small(small_p), _pack_small({n: mom[n] for n in SMALL}),
                                _pack_small({n: var[n] for n in SMALL}))
    for dst, packed in ((grads, g), (delta, d), (new_m, m2), (new_v, v2)):
        dst.update(_unpack_small(packed, small_p))

    total = lax.psum(loss[0, 0], ("x", "y", "c"))
    return (total, grad_x[None], *[grads[n] for n in WEIGHTS], *[delta[n] for n in WEIGHTS],
            *[new_m[n] for n in WEIGHTS], *[new_v[n] for n in WEIGHTS])
```

```python
import functools
import math

import jax
import jax.numpy as jnp
from jax import lax
from jax.experimental import pallas as pl
from jax.experimental.pallas import tpu as pltpu

F32 = jnp.float32
BF16 = jnp.bfloat16

D_MODEL = 1024
N_MEM = 256
D_FF = 4096
HEAD = 64
SWA_HEADS = 8
SWA_BLOCK = 128
MEM_HEADS = 4
MEM_HEAD = 128
EPS = 1e-6
NEG = -1e30
REL_BUCKETS = 32
REL_MAX_DIST = 128

ADAM_LR = 0.001
ADAM_B1 = 0.9
ADAM_B2 = 0.999
ADAM_EPS = 1e-08
ADAM_WD = 0.01
ADAM_STEP = 10

GL0, QF0, KF0, VF0, QM0, QA0, KA0, VA0, FL0 = 0, 3072, 3584, 4096, 4608, 5120, 5632, 5760, 5888
PROJ_W = 6144
HALF_W = 3072
H_QF, H_KF, H_VF, H_QM, H_QA, H_KA, H_VA, H_FL = 0, 512, 1024, 1536, 2048, 2560, 2688, 2816

VMEM_LIMIT = 56 * 1024 * 1024
N_DEV = 8
MESH = pl.DeviceIdType.MESH

NN = (((1,), (0,)), ((), ()))
NT = (((1,), (1,)), ((), ()))
TN = (((0,), (0,)), ((), ()))


def _dot(a, b, dims=NN):
    return lax.dot_general(a, b, dims, preferred_element_type=F32)


def _params(sem):
    return pltpu.CompilerParams(dimension_semantics=sem, vmem_limit_bytes=VMEM_LIMIT)


def _full(shape):
    nd = len(shape)
    return pl.BlockSpec(shape, lambda *_: (0,) * nd)


def _sigmoid(z):
    return 1.0 / (1.0 + jnp.exp(-z))


def _group_mean(v, hd):
    if hd == 128:
        return jnp.mean(v, axis=-1, keepdims=True)
    lane = lax.broadcasted_iota(jnp.int32, v.shape, 1)
    lo = lane < HEAD
    s_lo = jnp.sum(jnp.where(lo, v, 0.0), axis=-1, keepdims=True)
    s_hi = jnp.sum(jnp.where(lo, 0.0, v), axis=-1, keepdims=True)
    return jnp.where(lo, s_lo, s_hi) * (1.0 / HEAD)


def _mm(a, b, mode, out_dtype, tm, tn, tk, name):
    if mode == "nn":
        m, k = a.shape
        n = b.shape[1]
    elif mode == "nt":
        m, k = a.shape
        n = b.shape[0]
    else:
        k, m = a.shape
        n = b.shape[1]
    tm, tn, tk = min(tm, m), min(tn, n), min(tk, k)
    nk = k // tk
    dims = {"nn": NN, "nt": NT, "tn": TN}[mode]
    a_spec = pl.BlockSpec((tk, tm), lambda i, j, kk: (kk, i)) if mode == "tn" else pl.BlockSpec((tm, tk), lambda i, j, kk: (i, kk))
    b_spec = pl.BlockSpec((tn, tk), lambda i, j, kk: (j, kk)) if mode == "nt" else pl.BlockSpec((tk, tn), lambda i, j, kk: (kk, j))

    def body(a_ref, b_ref, o_ref, acc_ref):
        prod = _dot(a_ref[...].astype(BF16), b_ref[...].astype(BF16), dims)
        if nk == 1:
            o_ref[...] = prod.astype(o_ref.dtype)
        else:
            kk = pl.program_id(2)

            @pl.when(kk == 0)
            def _():
                acc_ref[...] = prod

            @pl.when(kk > 0)
            def _():
                acc_ref[...] += prod

            @pl.when(kk == nk - 1)
            def _():
                o_ref[...] = acc_ref[...].astype(o_ref.dtype)

    return pl.pallas_call(
        body, name=name, grid=(m // tm, n // tn, nk),
        in_specs=[a_spec, b_spec],
        out_specs=pl.BlockSpec((tm, tn), lambda i, j, kk: (i, j)),
        out_shape=jax.ShapeDtypeStruct((m, n), out_dtype),
        scratch_shapes=[pltpu.VMEM((tm, tn), F32)],
        compiler_params=_params(("parallel", "parallel", "arbitrary")),
    )(a, b)


def _rms_fwd(x, g, name, deps=()):
    s, d = x.shape
    tm = min(512, s)

    def body(x_ref, g_ref, *rest):
        h_ref = rest[len(deps)]
        xv = x_ref[...]
        r = lax.rsqrt(jnp.mean(xv * xv, axis=-1, keepdims=True) + EPS)
        h_ref[...] = (xv * r * g_ref[...]).astype(BF16)

    return pl.pallas_call(
        body, name=name, grid=(s // tm,),
        in_specs=[pl.BlockSpec((tm, d), lambda i: (i, 0)), _full((1, d))] + [pl.BlockSpec(memory_space=pl.ANY)] * len(deps),
        out_specs=pl.BlockSpec((tm, d), lambda i: (i, 0)),
        out_shape=jax.ShapeDtypeStruct((s, d), BF16),
        compiler_params=_params(("parallel",)),
    )(x, g, *deps)


def _proj_post(proj, gq_fox, gk_fox, gq_mem, gq_swa, gk_swa):
    s = proj.shape[0]
    tm = min(256, s)

    def body(p_ref, gqf, gkf, gqm, gqa, gka, qf_ref, kf_ref, vf_ref, qm_ref, qa_ref, ka_ref, va_ref):
        def norm(off, width, hd, g_ref, o_ref):
            for b in range(width // 128):
                v = p_ref[:, off + b * 128: off + (b + 1) * 128]
                r = lax.rsqrt(_group_mean(v * v, hd) + EPS)
                o_ref[:, b * 128:(b + 1) * 128] = (v * r * g_ref[...]).astype(BF16)

        norm(H_QF, 512, HEAD, gqf, qf_ref)
        norm(H_KF, 512, HEAD, gkf, kf_ref)
        vf_ref[...] = p_ref[:, H_VF:H_VF + 512].astype(BF16)
        norm(H_QM, 512, MEM_HEAD, gqm, qm_ref)
        norm(H_QA, 512, HEAD, gqa, qa_ref)
        norm(H_KA, 128, HEAD, gka, ka_ref)
        va_ref[...] = p_ref[:, H_VA:H_VA + 128].astype(BF16)

    g_spec = _full((1, 128))
    o512 = pl.BlockSpec((tm, 512), lambda i: (i, 0))
    o128 = pl.BlockSpec((tm, 128), lambda i: (i, 0))
    s512 = jax.ShapeDtypeStruct((s, 512), BF16)
    s128 = jax.ShapeDtypeStruct((s, 128), BF16)
    return pl.pallas_call(
        body, name="proj_post", grid=(s // tm,),
        in_specs=[pl.BlockSpec((tm, HALF_W), lambda i: (i, 1)), g_spec, g_spec, g_spec, g_spec, g_spec],
        out_specs=[o512, o512, o512, o512, o512, o128, o128],
        out_shape=[s512, s512, s512, s512, s512, s128, s128],
        compiler_params=_params(("parallel",)),
    )(proj, gq_fox, gk_fox, gq_mem, gq_swa, gk_swa)


def _tri(n, lower):
    r = lax.broadcasted_iota(jnp.int32, (n, n), 0)
    c = lax.broadcasted_iota(jnp.int32, (n, n), 1)
    return jnp.where((c <= r) if lower else (c >= r), 1.0, 0.0).astype(F32)


def _fox_gate_fwd(proj, b_forget128):
    s = proj.shape[0]
    tm = min(512, s)

    def body(p_ref, b_ref, cc_ref, carry_ref):
        i = pl.program_id(0)

        @pl.when(i == 0)
        def _():
            carry_ref[...] = jnp.zeros_like(carry_ref)

        z = p_ref[...] + b_ref[...]
        logf = jnp.minimum(z, 0.0) - jnp.log(1.0 + jnp.exp(-jnp.abs(z)))
        c = jnp.dot(_tri(tm, True), logf, precision=lax.Precision.HIGHEST, preferred_element_type=F32) + carry_ref[...]
        carry_ref[...] = c[tm - 1:tm, :]
        for hp in range(4):
            cc_ref[hp] = c if hp == 0 else pltpu.roll(c, 128 - 2 * hp, 1)

    return pl.pallas_call(
        body, name="fox_gate_fwd", grid=(s // tm,),
        in_specs=[pl.BlockSpec((tm, 128), lambda i: (i, FL0 // 128)), _full((1, 128))],
        out_specs=pl.BlockSpec((4, tm, 128), lambda i: (0, i, 0)),
        out_shape=jax.ShapeDtypeStruct((4, s, 128), F32),
        scratch_shapes=[pltpu.VMEM((1, 128), F32)],
        compiler_params=_params(("arbitrary",)),
    )(proj, b_forget128)


def _memkv_fwd(mem, g_mem, w_kv, kn_mem):
    m = mem.shape[0]

    def body(mem_ref, g_ref, w_ref, kn_ref, memn_ref, kv_ref, mk_ref, mv_ref):
        xv = mem_ref[...]
        r = lax.rsqrt(jnp.mean(xv * xv, axis=-1, keepdims=True) + EPS)
        mn = (xv * r * g_ref[...]).astype(BF16)
        memn_ref[...] = mn
        kv = _dot(mn, w_ref[...])
        kv_ref[...] = kv
        for h in range(MEM_HEADS):
            v = kv[:, h * 128:(h + 1) * 128]
            rr = lax.rsqrt(jnp.mean(v * v, axis=-1, keepdims=True) + EPS)
            mk_ref[:, h * 128:(h + 1) * 128] = (v * rr * kn_ref[...]).astype(BF16)
        mv_ref[...] = kv[:, 512:1024].astype(BF16)

    return pl.pallas_call(
        body, name="memkv_fwd",
        out_shape=[jax.ShapeDtypeStruct((m, D_MODEL), BF16), jax.ShapeDtypeStruct((m, 1024), F32),
                   jax.ShapeDtypeStruct((m, 512), BF16), jax.ShapeDtypeStruct((m, 512), BF16)],
        compiler_params=pltpu.CompilerParams(vmem_limit_bytes=VMEM_LIMIT),
    )(mem, g_mem, w_kv, kn_mem)


def _bias_table(rel_bias, bucket):
    def body(rb_ref, bk_ref, o_ref):
        bk = bk_ref[...]
        for h in range(SWA_HEADS):
            acc = jnp.zeros(bk.shape, F32)
            for b in range(REL_BUCKETS):
                acc = jnp.where(bk == b, rb_ref[b, h], acc)
            o_ref[h] = acc

    return pl.pallas_call(
        body, name="bias_table",
        in_specs=[pl.BlockSpec(memory_space=pltpu.SMEM), pl.BlockSpec(memory_space=pltpu.VMEM)],
        out_shape=jax.ShapeDtypeStruct((SWA_HEADS,) + bucket.shape, F32),
    )(rel_bias, bucket)


def _swa_valid(n):
    row = lax.broadcasted_iota(jnp.int32, (SWA_BLOCK, 2 * SWA_BLOCK), 0)
    col = lax.broadcasted_iota(jnp.int32, (SWA_BLOCK, 2 * SWA_BLOCK), 1)
    dist = row + SWA_BLOCK - col
    return (dist >= 0) & (dist < SWA_BLOCK) & ((col >= SWA_BLOCK) | (n > 0))


def _swa_fwd(qa, kp, vp, bias, sink):
    s = qa.shape[0]
    nb = s // SWA_BLOCK

    def body(sink_ref, q_ref, kp_ref, vp_ref, bias_ref, o_ref):
        n = pl.program_id(0)
        start = pl.multiple_of(n * SWA_BLOCK, SWA_BLOCK)
        k2 = kp_ref[pl.ds(start, 2 * SWA_BLOCK), :]
        v2 = vp_ref[pl.ds(start, 2 * SWA_BLOCK), :]
        valid = _swa_valid(n)
        for h in range(SWA_HEADS):
            kv = h // 4
            qh = q_ref[:, h * HEAD:(h + 1) * HEAD]
            kh = k2[:, kv * HEAD:(kv + 1) * HEAD]
            vh = v2[:, kv * HEAD:(kv + 1) * HEAD]
            sc = _dot(qh, kh, NT) * 0.125 + bias_ref[h]
            sc = jnp.where(valid, sc, NEG)
            sk = sink_ref[h]
            mx = jnp.maximum(jnp.max(sc, axis=-1, keepdims=True), sk)
            p = jnp.exp(sc - mx)
            den = jnp.sum(p, axis=-1, keepdims=True) + jnp.exp(sk - mx)
            p = p / den
            o_ref[:, h * HEAD:(h + 1) * HEAD] = _dot(p.astype(BF16), vh).astype(BF16)

    return pl.pallas_call(
        body, name="swa_fwd", grid=(nb,),
        in_specs=[pl.BlockSpec(memory_space=pltpu.SMEM),
                  pl.BlockSpec((SWA_BLOCK, 512), lambda n: (n, 0)),
                  _full(kp.shape), _full(vp.shape), _full(bias.shape)],
        out_specs=pl.BlockSpec((SWA_BLOCK, 512), lambda n: (n, 0)),
        out_shape=jax.ShapeDtypeStruct((s, 512), BF16),
        compiler_params=_params(("parallel",)),
    )(sink, qa, kp, vp, bias)


def _head_mask(e):
    lane = lax.broadcasted_iota(jnp.int32, (1, 128), 1)
    return (lane >= e * HEAD) & (lane < (e + 1) * HEAD)


FOX_T = 256
FOX_TK = 512


def _head_rows(e):
    row = lax.broadcasted_iota(jnp.int32, (128, 1), 0)
    return (row >= e * HEAD) & (row < (e + 1) * HEAD)


def _fox_fwd(q, k, v_t, cc4):
    s = q.shape[0]
    t = min(FOX_T, s)
    tk = min(FOX_TK, s)
    nq = s // t

    def body(q_ref, k_ref, vt_ref, cc_ref, o_ref, lse_ref):
        i = pl.program_id(1)
        qs = q_ref[...] * jnp.asarray(0.125, BF16)
        qe = [jnp.where(_head_mask(e), qs, jnp.zeros_like(qs)) for e in range(2)]
        n_full = (i * t) // tk
        krow = lax.broadcasted_iota(jnp.int32, (tk, t), 0) + n_full * tk
        qcol = lax.broadcasted_iota(jnp.int32, (tk, t), 1) + i * t

        def step(j, carry, masked):
            ks = pl.ds(pl.multiple_of(j * tk, tk), tk)
            kj = k_ref[ks, :]
            vtj = vt_ref[:, ks]
            out = []
            for e in range(2):
                m, acc = carry[2 * e], carry[2 * e + 1]
                st = _dot(kj, qe[e], NT) - cc_ref[0, ks, e:e + 1]
                if masked:
                    st = jnp.where(krow <= qcol, st, NEG)
                m_new = jnp.maximum(m, jnp.max(st, axis=0, keepdims=True))
                alpha = jnp.exp(m - m_new)
                pt = jnp.exp(st - m_new).astype(BF16)
                vte = jnp.where(_head_rows(e), vtj, jnp.ones_like(vtj))
                out += [m_new, alpha * acc + _dot(vte, pt)]
            return tuple(out)

        init = (jnp.full((1, t), NEG, F32), jnp.zeros((128, t), F32)) * 2
        carry = lax.fori_loop(0, n_full, functools.partial(step, masked=False), init)
        m0, a0, m1, a1 = step(n_full, carry, True)
        l0 = a0[HEAD:HEAD + 1, :]
        l1 = a1[0:1, :]
        o_t = jnp.where(_head_rows(0), a0 / l0, a1 / l1)
        o_ref[...] = o_t.T.astype(BF16)
        r8 = lax.broadcasted_iota(jnp.int32, (8, t), 0)
        lse_ref[0] = jnp.where(r8 == 0, m0 + jnp.log(l0), jnp.where(r8 == 1, m1 + jnp.log(l1), 0.0))

    return pl.pallas_call(
        body, name="fox_fwd", grid=(4, nq),
        in_specs=[pl.BlockSpec((t, 128), lambda hp, i: (i, hp)),
                  pl.BlockSpec((s, 128), lambda hp, i: (0, hp)),
                  pl.BlockSpec((128, s), lambda hp, i: (hp, 0)),
                  pl.BlockSpec((1, s, 128), lambda hp, i: (hp, 0, 0))],
        out_specs=[pl.BlockSpec((t, 128), lambda hp, i: (i, hp)),
                   pl.BlockSpec((1, 8, t), lambda hp, i: (hp, 0, i))],
        out_shape=[jax.ShapeDtypeStruct((s, 512), BF16), jax.ShapeDtypeStruct((4, 8, s), F32)],
        compiler_params=_params(("parallel", "parallel")),
    )(q, k, v_t, cc4)


MEM_SCALE = MEM_HEAD ** -0.5


def _mem_fwd(qm, mk, mv):
    s = qm.shape[0]
    tq = min(512, s)

    def body(q_ref, mk_ref, mv_ref, o_ref):
        for h in range(MEM_HEADS):
            hs = slice(h * 128, (h + 1) * 128)
            sc = _dot(q_ref[:, hs], mk_ref[:, hs], NT) * MEM_SCALE
            mx = jnp.max(sc, axis=-1, keepdims=True)
            p = jnp.exp(sc - mx)
            p = p / jnp.sum(p, axis=-1, keepdims=True)
            o_ref[:, hs] = _dot(p.astype(BF16), mv_ref[:, hs]).astype(BF16)

    return pl.pallas_call(
        body, name="mem_fwd", grid=(s // tq,),
        in_specs=[pl.BlockSpec((tq, 512), lambda i: (i, 0)), _full(mk.shape), _full(mv.shape)],
        out_specs=pl.BlockSpec((tq, 512), lambda i: (i, 0)),
        out_shape=jax.ShapeDtypeStruct((s, 512), BF16),
        compiler_params=_params(("parallel",)),
    )(qm, mk, mv)


def _merge_fwd(x, oa, of, om, proj, b_gate, wa, wf, wm, w_out, g_mlp):
    s = x.shape[0]
    tm = min(256, s)

    def body(x_ref, oa_ref, of_ref, om_ref, gl_ref, bg_ref, wa_ref, wf_ref, wm_ref, wo_ref, g_ref, x1_ref, hm_ref, mg_ref):
        merged = None
        for b, (o_ref, w_ref) in enumerate(((oa_ref, wa_ref), (of_ref, wf_ref), (om_ref, wm_ref))):
            cs = slice(b * D_MODEL, (b + 1) * D_MODEL)
            y = _dot(o_ref[...], w_ref[...])
            t = _sigmoid(gl_ref[:, cs] + bg_ref[:, cs]) * y
            merged = t if merged is None else merged + t
        mb = merged.astype(BF16)
        mg_ref[...] = mb
        x1 = x_ref[...] + _dot(mb, wo_ref[...])
        x1_ref[...] = x1
        r = lax.rsqrt(jnp.mean(x1 * x1, axis=-1, keepdims=True) + EPS)
        hm_ref[...] = (x1 * r * g_ref[...]).astype(BF16)

    row = lambda w: pl.BlockSpec((tm, w), lambda i: (i, 0))
    return pl.pallas_call(
        body, name="merge_fwd", grid=(s // tm,),
        in_specs=[row(D_MODEL), row(512), row(512), row(512), row(HALF_W), _full((1, HALF_W)),
                  _full(wa.shape), _full(wf.shape), _full(wm.shape), _full(w_out.shape), _full((1, D_MODEL))],
        out_specs=[row(D_MODEL), row(D_MODEL), row(D_MODEL)],
        out_shape=[jax.ShapeDtypeStruct((s, D_MODEL), F32), jax.ShapeDtypeStruct((s, D_MODEL), BF16),
                   jax.ShapeDtypeStruct((s, D_MODEL), BF16)],
        compiler_params=_params(("parallel",)),
    )(x, oa, of, om, proj, b_gate, wa, wf, wm, w_out, g_mlp)


def _mlp_up(hm, w_up):
    s = hm.shape[0]
    tm, tn = min(512, s), 1024

    def body(h_ref, w_ref, a_ref, u_ref):
        a = _dot(h_ref[...], w_ref[...])
        a_ref[...] = a
        r = jnp.maximum(a, 0.0)
        u_ref[...] = (r * r).astype(BF16)

    return pl.pallas_call(
        body, name="mlp_up", grid=(s // tm, D_FF // tn),
        in_specs=[pl.BlockSpec((tm, D_MODEL), lambda i, j: (i, 0)), pl.BlockSpec((D_MODEL, tn), lambda i, j: (0, j))],
        out_specs=[pl.BlockSpec((tm, tn), lambda i, j: (i, j)), pl.BlockSpec((tm, tn), lambda i, j: (i, j))],
        out_shape=[jax.ShapeDtypeStruct((s, D_FF), F32), jax.ShapeDtypeStruct((s, D_FF), BF16)],
        compiler_params=_params(("parallel", "parallel")),
    )(hm, w_up)


def _mlp_down_loss(u, w_down, x1, target):
    s = u.shape[0]
    tm = min(256, s)

    def body(u_ref, w_ref, x1_ref, t_ref, dy_ref, loss_ref):
        i = pl.program_id(0)

        @pl.when(i == 0)
        def _():
            loss_ref[...] = jnp.zeros_like(loss_ref)

        y = x1_ref[...] + _dot(u_ref[...], w_ref[...])
        err = y - t_ref[...]
        dy_ref[...] = err * (1.0 / D_MODEL)
        part = jnp.sum(jnp.sum(err * err, axis=-1, keepdims=True) * (1.0 / D_MODEL), axis=0, keepdims=True)
        loss_ref[...] += 0.5 * part

    row = pl.BlockSpec((tm, D_MODEL), lambda i: (i, 0))
    return pl.pallas_call(
        body, name="mlp_down_loss", grid=(s // tm,),
        in_specs=[pl.BlockSpec((tm, D_FF), lambda i: (i, 0)), _full(w_down.shape), row, row],
        out_specs=[row, _full((1, 1))],
        out_shape=[jax.ShapeDtypeStruct((s, D_MODEL), F32), jax.ShapeDtypeStruct((1, 1), F32)],
        compiler_params=_params(("arbitrary",)),
    )(u, w_down, x1, target)


def _mlp_bwd_act(dy, w_down, a):
    s = dy.shape[0]
    tm, tn = min(512, s), 1024

    def body(dy_ref, w_ref, a_ref, da_ref):
        du = _dot(dy_ref[...].astype(BF16), w_ref[...], NT)
        da_ref[...] = (du * (2.0 * jnp.maximum(a_ref[...], 0.0))).astype(BF16)

    return pl.pallas_call(
        body, name="mlp_bwd_act", grid=(s // tm, D_FF // tn),
        in_specs=[pl.BlockSpec((tm, D_MODEL), lambda i, j: (i, 0)), pl.BlockSpec((tn, D_MODEL), lambda i, j: (j, 0)),
                  pl.BlockSpec((tm, tn), lambda i, j: (i, j))],
        out_specs=pl.BlockSpec((tm, tn), lambda i, j: (i, j)),
        out_shape=jax.ShapeDtypeStruct((s, D_FF), BF16),
        compiler_params=_params(("parallel", "parallel")),
    )(dy, w_down, a)


def _rms_bwd(xv, g, dh, skip):
    r = lax.rsqrt(jnp.mean(xv * xv, axis=-1, keepdims=True) + EPS)
    n = xv * r
    dn = dh * g
    dx = skip + r * (dn - n * jnp.mean(dn * n, axis=-1, keepdims=True))
    return dx, jnp.sum(dh * n, axis=0, keepdims=True)


def _mlp_bwd_x(da, w_up, x1, dy, g_mlp):
    s = da.shape[0]
    tm = min(256, s)

    def body(da_ref, w_ref, x1_ref, dy_ref, g_ref, dx1_ref, dg_ref):
        i = pl.program_id(0)

        @pl.when(i == 0)
        def _():
            dg_ref[...] = jnp.zeros_like(dg_ref)

        dhm = _dot(da_ref[...], w_ref[...], NT)
        dx, dg = _rms_bwd(x1_ref[...], g_ref[...], dhm, dy_ref[...])
        dx1_ref[...] = dx
        dg_ref[...] += dg

    row = pl.BlockSpec((tm, D_MODEL), lambda i: (i, 0))
    return pl.pallas_call(
        body, name="mlp_bwd_x", grid=(s // tm,),
        in_specs=[pl.BlockSpec((tm, D_FF), lambda i: (i, 0)), _full(w_up.shape), row, row, _full((1, D_MODEL))],
        out_specs=[row, _full((1, D_MODEL))],
        out_shape=[jax.ShapeDtypeStruct((s, D_MODEL), F32), jax.ShapeDtypeStruct((1, D_MODEL), F32)],
        compiler_params=_params(("arbitrary",)),
    )(da, w_up, x1, dy, g_mlp)


def _merge_bwd(dx1, oa, of, om, proj, b_gate, wa, wf, wm, w_out):
    s = dx1.shape[0]
    tm = min(256, s)

    def body(dx1_ref, oa_ref, of_ref, om_ref, gl_ref, bg_ref, wa_ref, wf_ref, wm_ref, wo_ref,
             dp_ref, doa_ref, dof_ref, dom_ref, dya_ref, dyf_ref, dym_ref, dbg_ref):
        i = pl.program_id(0)

        @pl.when(i == 0)
        def _():
            dbg_ref[...] = jnp.zeros_like(dbg_ref)

        dmerged = _dot(dx1_ref[...].astype(BF16), wo_ref[...], NT)
        branches = ((oa_ref, wa_ref, doa_ref, dya_ref), (of_ref, wf_ref, dof_ref, dyf_ref), (om_ref, wm_ref, dom_ref, dym_ref))
        for b, (o_ref, w_ref, do_ref, dyb_ref) in enumerate(branches):
            cs = slice(b * D_MODEL, (b + 1) * D_MODEL)
            y = _dot(o_ref[...], w_ref[...])
            g = _sigmoid(gl_ref[:, cs] + bg_ref[:, cs])
            dz = (dmerged * y) * g * (1.0 - g)
            dp_ref[:, cs] = dz.astype(BF16)
            dbg_ref[:, cs] += jnp.sum(dz, axis=0, keepdims=True)
            dyb = (dmerged * g).astype(BF16)
            dyb_ref[...] = dyb
            do_ref[...] = _dot(dyb, w_ref[...], NT).astype(BF16)

    row = lambda w: pl.BlockSpec((tm, w), lambda i: (i, 0))
    sd = lambda w: jax.ShapeDtypeStruct((s, w), BF16)
    return pl.pallas_call(
        body, name="merge_bwd", grid=(s // tm,),
        in_specs=[row(D_MODEL), row(512), row(512), row(512), row(HALF_W), _full((1, HALF_W)),
                  _full(wa.shape), _full(wf.shape), _full(wm.shape), _full(w_out.shape)],
        out_specs=[row(HALF_W), row(512), row(512), row(512), row(D_MODEL), row(D_MODEL), row(D_MODEL), _full((1, HALF_W))],
        out_shape=[sd(PROJ_W), sd(512), sd(512), sd(512), sd(D_MODEL), sd(D_MODEL), sd(D_MODEL),
                   jax.ShapeDtypeStruct((1, HALF_W), F32)],
        compiler_params=_params(("arbitrary",)),
    )(dx1, oa, of, om, proj, b_gate, wa, wf, wm, w_out)


def _swa_bwd(qa, kp, vp, bias, sink, doa):
    s = qa.shape[0]
    nb = s // SWA_BLOCK

    def body(sink_ref, q_ref, kp_ref, vp_ref, bias_ref, do_ref, dq_ref, dkp_ref, dvp_ref, dbias_ref, dsink_ref, sk_acc):
        n = pl.program_id(0)

        @pl.when(n == 0)
        def _():
            dkp_ref[...] = jnp.zeros_like(dkp_ref)
            dvp_ref[...] = jnp.zeros_like(dvp_ref)
            dbias_ref[...] = jnp.zeros_like(dbias_ref)
            sk_acc[...] = jnp.zeros_like(sk_acc)

        start = pl.multiple_of(n * SWA_BLOCK, SWA_BLOCK)
        win = pl.ds(start, 2 * SWA_BLOCK)
        k2 = kp_ref[win, :]
        v2 = vp_ref[win, :]
        valid = _swa_valid(n)
        for kv in range(2):
            hs_kv = slice(kv * HEAD, (kv + 1) * HEAD)
            kh = k2[:, hs_kv]
            vh = v2[:, hs_kv]
            dk2 = jnp.zeros((2 * SWA_BLOCK, HEAD), F32)
            dv2 = jnp.zeros((2 * SWA_BLOCK, HEAD), F32)
            for g in range(4):
                h = kv * 4 + g
                hs = slice(h * HEAD, (h + 1) * HEAD)
                qh = q_ref[:, hs]
                doh = do_ref[:, hs]
                sc = _dot(qh, kh, NT) * 0.125 + bias_ref[h]
                sc = jnp.where(valid, sc, NEG)
                sk = sink_ref[h]
                mx = jnp.maximum(jnp.max(sc, axis=-1, keepdims=True), sk)
                p = jnp.exp(sc - mx)
                esk = jnp.exp(sk - mx)
                den = jnp.sum(p, axis=-1, keepdims=True) + esk
                p = p / den
                dp = _dot(doh, vh, NT)
                delta = jnp.sum(p * dp, axis=-1, keepdims=True)
                ds = p * (dp - delta)
                sk_acc[:, h:h + 1] += -(esk / den) * delta
                dbias_ref[h] += ds
                dsb = (ds * 0.125).astype(BF16)
                dq_ref[:, hs] = _dot(dsb, kh)
                dk2 = dk2 + _dot(dsb, qh, TN)
                dv2 = dv2 + _dot(p.astype(BF16), doh, TN)
            dkp_ref[win, hs_kv] += dk2
            dvp_ref[win, hs_kv] += dv2

        @pl.when(n == nb - 1)
        def _():
            dsink_ref[...] = jnp.sum(sk_acc[...], axis=0, keepdims=True)

    return pl.pallas_call(
        body, name="swa_bwd", grid=(nb,),
        in_specs=[pl.BlockSpec(memory_space=pltpu.SMEM),
                  pl.BlockSpec((SWA_BLOCK, 512), lambda n: (n, 0)),
                  _full(kp.shape), _full(vp.shape), _full(bias.shape),
                  pl.BlockSpec((SWA_BLOCK, 512), lambda n: (n, 0))],
        out_specs=[pl.BlockSpec((SWA_BLOCK, 512), lambda n: (n, 0)), _full(kp.shape), _full(vp.shape),
                   _full(bias.shape), _full((1, 128))],
        out_shape=[jax.ShapeDtypeStruct((s, 512), F32), jax.ShapeDtypeStruct(kp.shape, F32),
                   jax.ShapeDtypeStruct(vp.shape, F32), jax.ShapeDtypeStruct(bias.shape, F32),
                   jax.ShapeDtypeStruct((1, 128), F32)],
        scratch_shapes=[pltpu.VMEM((SWA_BLOCK, 128), F32)],
        compiler_params=_params(("arbitrary",)),
    )(sink, qa, kp, vp, bias, doa)


def _fox_bwd(q, k, v, do, o, cc4, lse4):
    s = q.shape[0]
    t = min(FOX_T, s)
    tq = min(FOX_TK, s)
    nq = s // t
    nqt = s // tq

    def body(q_ref, k_ref, v_ref, do_ref, o_ref, cc_ref, lse_ref,
             dqt_ref, dk_ref, dv_ref, dck_ref, dcq_ref, delta_ref, dk0, dk1, dv0, dv1, ds0, ds1):
        j = pl.program_id(1)

        @pl.when(j == 0)
        def _():
            dqt_ref[...] = jnp.zeros_like(dqt_ref)
            dcq_ref[...] = jnp.zeros_like(dcq_ref)
            lane8 = lax.broadcasted_iota(jnp.int32, (8, 128), 1)
            row8 = lax.broadcasted_iota(jnp.int32, (8, 128), 0)
            sel = jnp.where((lane8 // HEAD) == row8, 1.0, 0.0).astype(F32)

            def dl(i, c):
                rows = pl.ds(pl.multiple_of(i * tq, tq), tq)
                pr = do_ref[rows, :].astype(F32) * o_ref[rows, :].astype(F32)
                delta_ref[:, rows] = lax.dot_general(sel, pr, NT, precision=lax.Precision.HIGHEST,
                                                     preferred_element_type=F32)
                return c

            lax.fori_loop(0, nqt, dl, 0)

        kj = k_ref[...]
        vj = v_ref[...]
        ks = pl.ds(pl.multiple_of(j * t, t), t)
        kt = (kj.astype(F32) * 0.125).T.astype(BF16)
        ke = [jnp.where(_head_mask(e), kj, jnp.zeros_like(kj)) for e in range(2)]
        ve = [jnp.where(_head_mask(e), vj, jnp.zeros_like(vj)) for e in range(2)]
        kte = [jnp.where(_head_rows(e), kt, jnp.zeros_like(kt)) for e in range(2)]
        ck = [cc_ref[0, ks, e:e + 1] for e in range(2)]
        accs = ((dk0, dv0, ds0), (dk1, dv1, ds1))
        for refs in accs:
            for r in refs:
                r[...] = jnp.zeros_like(r)
        i_first = (j * t) // tq
        krow = lax.broadcasted_iota(jnp.int32, (t, tq), 0) + j * t
        qcol = lax.broadcasted_iota(jnp.int32, (t, tq), 1) + i_first * tq

        def step(i, c, masked):
            rows = pl.ds(pl.multiple_of(i * tq, tq), tq)
            qs = q_ref[rows, :] * jnp.asarray(0.125, BF16)
            doi = do_ref[rows, :]
            for e in range(2):
                dk_acc, dv_acc, ds_acc = accs[e]
                st = _dot(ke[e], qs, NT) - ck[e]
                if masked:
                    st = jnp.where(krow <= qcol, st, NEG)
                pt = jnp.exp(st - lse_ref[0, e:e + 1, rows])
                dpt = _dot(ve[e], doi, NT)
                dst = pt * (dpt - delta_ref[e:e + 1, rows])
                dsb = dst.astype(BF16)
                dv_acc[...] += _dot(pt.astype(BF16), doi)
                dk_acc[...] += _dot(dsb, qs)
                dqt_ref[:, rows] += _dot(kte[e], dsb)
                ds_acc[...] += dst
                dcq_ref[0, e:e + 1, rows] += jnp.sum(dst, axis=0, keepdims=True)
            return c

        step(i_first, 0, True)
        lax.fori_loop(i_first + 1, nqt, functools.partial(step, masked=False), 0)
        m0 = _head_mask(0)
        dk_ref[...] = jnp.where(m0, dk0[...], dk1[...])
        dv_ref[...] = jnp.where(m0, dv0[...], dv1[...])
        lane = lax.broadcasted_iota(jnp.int32, (t, 128), 1)
        c0 = jnp.sum(ds0[...], axis=-1, keepdims=True)
        c1 = jnp.sum(ds1[...], axis=-1, keepdims=True)
        dck_ref[0] = jnp.where(lane == 0, c0, jnp.where(lane == 1, c1, 0.0))

    res = lambda: pl.BlockSpec((s, 128), lambda hp, j: (0, hp))
    blk = lambda: pl.BlockSpec((t, 128), lambda hp, j: (j, hp))
    return pl.pallas_call(
        body, name="fox_bwd", grid=(4, nq),
        in_specs=[res(), blk(), blk(), res(), res(), pl.BlockSpec((1, s, 128), lambda hp, j: (hp, 0, 0)),
                  pl.BlockSpec((1, 8, s), lambda hp, j: (hp, 0, 0))],
        out_specs=[pl.BlockSpec((128, s), lambda hp, j: (hp, 0)), blk(), blk(),
                   pl.BlockSpec((1, t, 128), lambda hp, j: (hp, j, 0)),
                   pl.BlockSpec((1, 8, s), lambda hp, j: (hp, 0, 0))],
        out_shape=[jax.ShapeDtypeStruct((512, s), F32), jax.ShapeDtypeStruct((s, 512), F32),
                   jax.ShapeDtypeStruct((s, 512), F32), jax.ShapeDtypeStruct((4, s, 128), F32),
                   jax.ShapeDtypeStruct((4, 8, s), F32)],
        scratch_shapes=[pltpu.VMEM((8, s), F32)] + [pltpu.VMEM((t, 128), F32)] * 4 + [pltpu.VMEM((t, tq), F32)] * 2,
        compiler_params=_params(("arbitrary", "arbitrary")),
    )(q, k, v, do, o, cc4, lse4)


def _mem_bwd(qm, mk, mv, dom):
    s = qm.shape[0]
    tq = min(512, s)

    def body(q_ref, mk_ref, mv_ref, do_ref, dq_ref, dmk_ref, dmv_ref):
        i = pl.program_id(0)

        @pl.when(i == 0)
        def _():
            dmk_ref[...] = jnp.zeros_like(dmk_ref)
            dmv_ref[...] = jnp.zeros_like(dmv_ref)

        for h in range(MEM_HEADS):
            hs = slice(h * 128, (h + 1) * 128)
            qh = q_ref[:, hs]
            doh = do_ref[:, hs]
            sc = _dot(qh, mk_ref[:, hs], NT) * MEM_SCALE
            mx = jnp.max(sc, axis=-1, keepdims=True)
            p = jnp.exp(sc - mx)
            p = p / jnp.sum(p, axis=-1, keepdims=True)
            dp = _dot(doh, mv_ref[:, hs], NT)
            ds = p * (dp - jnp.sum(p * dp, axis=-1, keepdims=True))
            dsb = (ds * MEM_SCALE).astype(BF16)
            dq_ref[:, hs] = _dot(dsb, mk_ref[:, hs])
            dmk_ref[:, hs] += _dot(dsb, qh, TN)
            dmv_ref[:, hs] += _dot(p.astype(BF16), doh, TN)

    return pl.pallas_call(
        body, name="mem_bwd", grid=(s // tq,),
        in_specs=[pl.BlockSpec((tq, 512), lambda i: (i, 0)), _full(mk.shape), _full(mv.shape),
                  pl.BlockSpec((tq, 512), lambda i: (i, 0))],
        out_specs=[pl.BlockSpec((tq, 512), lambda i: (i, 0)), _full(mk.shape), _full(mv.shape)],
        out_shape=[jax.ShapeDtypeStruct((s, 512), F32), jax.ShapeDtypeStruct(mk.shape, F32),
                   jax.ShapeDtypeStruct(mv.shape, F32)],
        compiler_params=_params(("arbitrary",)),
    )(qm, mk, mv, dom)


def _memkv_bwd(dmk, dmv, kv_raw, kn_mem, mem, g_mem, mem_n, w_kv):
    def body(dmk_ref, dmv_ref, kv_ref, kn_ref, mem_ref, g_ref, mn_ref, w_ref, dw_ref, dkn_ref, dg_ref, dkv_ref):
        dkn = jnp.zeros((1, 128), F32)
        for h in range(MEM_HEADS):
            hs = slice(h * 128, (h + 1) * 128)
            v = kv_ref[:, hs]
            r = lax.rsqrt(jnp.mean(v * v, axis=-1, keepdims=True) + EPS)
            n = v * r
            dn = dmk_ref[:, hs]
            dkn = dkn + jnp.sum(dn * n, axis=0, keepdims=True)
            dng = dn * kn_ref[...]
            dkv_ref[:, hs] = (r * (dng - n * jnp.mean(dng * n, axis=-1, keepdims=True))).astype(BF16)
        dkv_ref[:, 512:1024] = dmv_ref[...].astype(BF16)
        dkn_ref[...] = dkn
        dkv = dkv_ref[...]
        dw_ref[...] = _dot(mn_ref[...], dkv, TN)
        dmn = _dot(dkv, w_ref[...], NT)
        xv = mem_ref[...]
        r = lax.rsqrt(jnp.mean(xv * xv, axis=-1, keepdims=True) + EPS)
        dg_ref[...] = jnp.sum(dmn * (xv * r), axis=0, keepdims=True)

    m = mem.shape[0]
    return pl.pallas_call(
        body, name="memkv_bwd",
        out_shape=[jax.ShapeDtypeStruct((D_MODEL, 1024), F32), jax.ShapeDtypeStruct((1, 128), F32),
                   jax.ShapeDtypeStruct((1, D_MODEL), F32)],
        scratch_shapes=[pltpu.VMEM((m, 1024), BF16)],
        compiler_params=pltpu.CompilerParams(vmem_limit_bytes=VMEM_LIMIT),
    )(dmk, dmv, kv_raw, kn_mem, mem, g_mem, mem_n, w_kv)


def _fox_gate_bwd(dc, proj, b_forget128):
    s = dc.shape[0]
    tm = min(512, s)
    nt = s // tm

    def body(dc_ref, p_ref, b_ref, dfl_ref, db_ref, carry_ref):
        i = pl.program_id(0)

        @pl.when(i == 0)
        def _():
            carry_ref[...] = jnp.zeros_like(carry_ref)
            db_ref[...] = jnp.zeros_like(db_ref)

        dcv = dc_ref[...]
        dlogf = jnp.dot(_tri(tm, False), dcv, precision=lax.Precision.HIGHEST, preferred_element_type=F32) + carry_ref[...]
        carry_ref[...] += jnp.sum(dcv, axis=0, keepdims=True)
        z = p_ref[...] + b_ref[...]
        dfl = dlogf * (1.0 / (1.0 + jnp.exp(z)))
        dfl_ref[...] = dfl.astype(BF16)
        db_ref[...] += jnp.sum(dfl, axis=0, keepdims=True)

    return pl.pallas_call(
        body, name="fox_gate_bwd", grid=(nt,),
        in_specs=[pl.BlockSpec((tm, 128), lambda i: (nt - 1 - i, 0)),
                  pl.BlockSpec((tm, 128), lambda i: (nt - 1 - i, FL0 // 128)), _full((1, 128))],
        out_specs=[pl.BlockSpec((tm, 128), lambda i: (nt - 1 - i, 0)), _full((1, 128))],
        out_shape=[jax.ShapeDtypeStruct((s, 128), BF16), jax.ShapeDtypeStruct((1, 128), F32)],
        scratch_shapes=[pltpu.VMEM((1, 128), F32)],
        compiler_params=_params(("arbitrary",)),
    )(dc, proj, b_forget128)


def _proj_pre_bwd(dproj, proj, dqf, dkf, dvf, dqm, dqa, dka, dva, dfl, gq_fox, gk_fox, gq_mem, gq_swa, gk_swa):
    s = proj.shape[0]
    tm = min(256, s)

    def body(dp_in, p_ref, dqf_ref, dkf_ref, dvf_ref, dqm_ref, dqa_ref, dka_ref, dva_ref, dfl_ref,
             gqf, gkf, gqm, gqa, gka, dp_ref, dgn_ref):
        i = pl.program_id(0)

        @pl.when(i == 0)
        def _():
            dgn_ref[...] = jnp.zeros_like(dgn_ref)

        def norm_bwd(off, width, hd, g_ref, dn_ref, slot):
            acc = jnp.zeros((1, 128), F32)
            for b in range(width // 128):
                v = p_ref[:, off + b * 128: off + (b + 1) * 128]
                r = lax.rsqrt(_group_mean(v * v, hd) + EPS)
                n = v * r
                dn = dn_ref[:, b * 128:(b + 1) * 128]
                acc = acc + jnp.sum(dn * n, axis=0, keepdims=True)
                dng = dn * g_ref[...]
                dp_ref[:, off + b * 128: off + (b + 1) * 128] = (r * (dng - n * _group_mean(dng * n, hd))).astype(BF16)
            dgn_ref[slot:slot + 1, :] += acc

        norm_bwd(H_QF, 512, HEAD, gqf, dqf_ref, 0)
        norm_bwd(H_KF, 512, HEAD, gkf, dkf_ref, 1)
        dp_ref[:, H_VF:H_VF + 512] = dvf_ref[...].astype(BF16)
        norm_bwd(H_QM, 512, MEM_HEAD, gqm, dqm_ref, 2)
        norm_bwd(H_QA, 512, HEAD, gqa, dqa_ref, 3)
        norm_bwd(H_KA, 128, HEAD, gka, dka_ref, 4)
        dp_ref[:, H_VA:H_VA + 128] = dva_ref[...].astype(BF16)
        dp_ref[:, H_FL:H_FL + 128] = dfl_ref[...]
        dp_ref[:, H_FL + 128:HALF_W] = jnp.zeros((tm, HALF_W - H_FL - 128), BF16)

    row = lambda w: pl.BlockSpec((tm, w), lambda i: (i, 0))
    g_spec = _full((1, 128))
    return pl.pallas_call(
        body, name="proj_pre_bwd", grid=(s // tm,),
        in_specs=[pl.BlockSpec(memory_space=pl.ANY), pl.BlockSpec((tm, HALF_W), lambda i: (i, 1)),
                  row(512), row(512), row(512), row(512), row(512), row(128), row(128), row(128),
                  g_spec, g_spec, g_spec, g_spec, g_spec],
        out_specs=[pl.BlockSpec((tm, HALF_W), lambda i: (i, 1)), _full((8, 128))],
        out_shape=[jax.ShapeDtypeStruct((s, PROJ_W), BF16), jax.ShapeDtypeStruct((8, 128), F32)],
        input_output_aliases={0: 0},
        compiler_params=_params(("arbitrary",)),
    )(dproj, proj, dqf, dkf, dvf, dqm, dqa, dka, dva, dfl, gq_fox, gk_fox, gq_mem, gq_swa, gk_swa)


def _in_bwd_x(dproj, w_in_p, x, g_mix, dx1):
    s = x.shape[0]
    tm, tk = min(512, s), 1536
    nk = PROJ_W // tk

    def body(dp_ref, w_ref, x_ref, g_ref, dx1_ref, gx_ref, dg_ref, acc_ref):
        i, kk = pl.program_id(0), pl.program_id(1)

        @pl.when((i == 0) & (kk == 0))
        def _():
            dg_ref[...] = jnp.zeros_like(dg_ref)

        prod = _dot(dp_ref[...], w_ref[...], NT)

        @pl.when(kk == 0)
        def _():
            acc_ref[...] = prod

        @pl.when(kk > 0)
        def _():
            acc_ref[...] += prod

        @pl.when(kk == nk - 1)
        def _():
            dx, dg = _rms_bwd(x_ref[...], g_ref[...], acc_ref[...], dx1_ref[...])
            gx_ref[...] = dx
            dg_ref[...] += dg

    row = pl.BlockSpec((tm, D_MODEL), lambda i, kk: (i, 0))
    return pl.pallas_call(
        body, name="in_bwd_x", grid=(s // tm, nk),
        in_specs=[pl.BlockSpec((tm, tk), lambda i, kk: (i, kk)), pl.BlockSpec((D_MODEL, tk), lambda i, kk: (0, kk)),
                  row, _full((1, D_MODEL)), row],
        out_specs=[row, _full((1, D_MODEL))],
        out_shape=[jax.ShapeDtypeStruct((s, D_MODEL), F32), jax.ShapeDtypeStruct((1, D_MODEL), F32)],
        scratch_shapes=[pltpu.VMEM((tm, D_MODEL), F32)],
        compiler_params=_params(("arbitrary", "arbitrary")),
    )(dproj, w_in_p, x, g_mix, dx1)


def _rel_bias_bwd(dbias, bucket):
    def body(db_ref, bk_ref, o_ref):
        bk = bk_ref[...]
        lane = lax.broadcasted_iota(jnp.int32, (1, 128), 1)
        for b in range(REL_BUCKETS):
            sel = bk == b
            acc = jnp.zeros((1, 128), F32)
            for h in range(SWA_HEADS):
                tot = jnp.sum(jnp.sum(jnp.where(sel, db_ref[h], 0.0), axis=-1, keepdims=True), axis=0, keepdims=True)
                acc = jnp.where(lane == h, tot, acc)
            o_ref[b:b + 1, :] = acc

    return pl.pallas_call(
        body, name="rel_bias_bwd",
        out_shape=jax.ShapeDtypeStruct((REL_BUCKETS, 128), F32),
        compiler_params=pltpu.CompilerParams(vmem_limit_bytes=VMEM_LIMIT),
    )(dbias, bucket)


def _my_place():
    return lax.axis_index("x"), lax.axis_index("y"), lax.axis_index("c")


def _peer(place, k):
    x, y, c = place
    return (1 - x if k & 4 else x, 1 - y if k & 2 else y, 1 - c if k & 1 else c)


def _index(place):
    x, y, c = place
    return 4 * x + 2 * y + c


def _all_gather(shard, name):
    def body(x_ref, out_ref, send_sems, recv_sems, local_sem):
        me = _my_place()
        mine = pltpu.make_async_copy(x_ref, out_ref.at[_index(me)], local_sem)
        mine.start()
        sends = []
        for k in range(1, N_DEV):
            cp = pltpu.make_async_remote_copy(
                src_ref=x_ref, dst_ref=out_ref.at[_index(me)], send_sem=send_sems.at[k - 1], recv_sem=recv_sems.at[k - 1],
                device_id=_peer(me, k), device_id_type=MESH)
            cp.start()
            sends.append(cp)
        for k in range(1, N_DEV):
            peer = _peer(me, k)
            pltpu.make_async_remote_copy(
                src_ref=x_ref, dst_ref=out_ref.at[_index(peer)], send_sem=send_sems.at[k - 1], recv_sem=recv_sems.at[k - 1],
                device_id=peer, device_id_type=MESH).wait_recv()
        for cp in sends:
            cp.wait_send()
        mine.wait()

    any_spec = pl.BlockSpec(memory_space=pl.ANY)
    return pl.pallas_call(
        body, name=name, in_specs=[any_spec], out_specs=any_spec,
        out_shape=jax.ShapeDtypeStruct((N_DEV,) + shard.shape, shard.dtype),
        scratch_shapes=[pltpu.SemaphoreType.DMA((N_DEV - 1,)), pltpu.SemaphoreType.DMA((N_DEV - 1,)), pltpu.SemaphoreType.DMA(())],
    )(shard)


HBM_SPEC = pl.BlockSpec(memory_space=pltpu.HBM)
SEM_SPEC = pl.BlockSpec(memory_space=pltpu.SEMAPHORE)
DATAFLOW = pltpu.SideEffectType.DATAFLOW_SIDE_EFFECTING
PEER_SEMS = pltpu.SemaphoreType.DMA((N_DEV - 1,))


def _split_copy(src_ref, land_ref, send_sems, recv_sems, me, k, gather):
    peer = _peer(me, k)
    if gather:
        src, dst = src_ref, land_ref.at[_index(me)]
    else:
        src, dst = src_ref.at[_index(peer)], land_ref.at[k - 1]
    return pltpu.make_async_remote_copy(src_ref=src, dst_ref=dst, send_sem=send_sems.at[k - 1], recv_sem=recv_sems.at[k - 1],
                                        device_id=peer, device_id_type=MESH)


def _split_start(src, slots, gather, name):
    def body(src_ref, land_ref, send_sems, recv_sems, src_thru, land_thru, token):
        me = _my_place()
        for k in range(1, N_DEV):
            _split_copy(src_ref, land_ref, send_sems, recv_sems, me, k, gather).start()
        token[...] = jnp.zeros_like(token)

    chunk = src.shape if gather else src.shape[1:]
    land = lax.empty((slots,) + chunk, src.dtype)
    return pl.pallas_call(
        body, name=name,
        out_shape=(PEER_SEMS, PEER_SEMS, pltpu.HBM(src.shape, src.dtype), pltpu.HBM(land.shape, land.dtype),
                   jax.ShapeDtypeStruct((8, 128), F32)),
        in_specs=(HBM_SPEC, HBM_SPEC),
        out_specs=(SEM_SPEC, SEM_SPEC, HBM_SPEC, HBM_SPEC, pl.BlockSpec(memory_space=pltpu.VMEM)),
        input_output_aliases={0: 2, 1: 3},
        compiler_params=pltpu.CompilerParams(has_side_effects=DATAFLOW),
    )(pltpu.with_memory_space_constraint(src, pltpu.HBM), pltpu.with_memory_space_constraint(land, pltpu.HBM))


def _split_wait(started, after, gather, name):
    send_sems, recv_sems, src_thru, land_thru, _ = started

    def body(src_ref, land_ref, send_sems, recv_sems, after_ref, src_out, land_out):
        me = _my_place()
        for k in range(1, N_DEV):
            cp = _split_copy(src_ref, land_ref, send_sems, recv_sems, me, k, gather)
            cp.wait_send()
            cp.wait_recv()

    return pl.pallas_call(
        body, name=name,
        out_shape=(pltpu.HBM(src_thru.shape, src_thru.dtype), pltpu.HBM(land_thru.shape, land_thru.dtype)),
        in_specs=(HBM_SPEC, HBM_SPEC, SEM_SPEC, SEM_SPEC, pl.BlockSpec(memory_space=pl.ANY)),
        out_specs=(HBM_SPEC, HBM_SPEC), input_output_aliases={0: 0, 1: 1},
        compiler_params=pltpu.CompilerParams(has_side_effects=DATAFLOW),
    )(src_thru, land_thru, send_sems, recv_sems, after)


def _adam_math(w, g, m, v):
    m2 = ADAM_B1 * m + (1.0 - ADAM_B1) * g
    v2 = ADAM_B2 * v + (1.0 - ADAM_B2) * (g * g)
    m_hat = m2 / (1.0 - ADAM_B1 ** ADAM_STEP)
    v_hat = v2 / (1.0 - ADAM_B2 ** ADAM_STEP)
    delta = -ADAM_LR * (m_hat / (jnp.sqrt(v_hat) + ADAM_EPS) + ADAM_WD * w)
    return delta, m2, v2


def _adamw(own, land, w, m, v, name):
    a, b = w.shape
    ta = min(128, a)

    def body(o_ref, p_ref, w_ref, m_ref, v_ref, g_ref, d_ref, m2_ref, v2_ref):
        g = o_ref[...].astype(F32)
        for k in range(N_DEV - 1):
            g = g + p_ref[k].astype(F32)
        delta, m2, v2 = _adam_math(w_ref[...], g, m_ref[...], v_ref[...])
        g_ref[...] = g
        d_ref[...] = delta
        m2_ref[...] = m2
        v2_ref[...] = v2

    blk = pl.BlockSpec((ta, b), lambda i: (i, 0))
    sd = jax.ShapeDtypeStruct((a, b), F32)
    return pl.pallas_call(
        body, name=name, grid=(a // ta,),
        in_specs=[blk, pl.BlockSpec((N_DEV - 1, ta, b), lambda i: (0, i, 0)), blk, blk, blk],
        out_specs=[blk, blk, blk, blk], out_shape=[sd, sd, sd, sd],
        compiler_params=_params(("parallel",)),
    )(own, land, w, m, v)


def _bucket_table():
    t_loc = jnp.arange(SWA_BLOCK)[:, None] + SWA_BLOCK
    s_loc = jnp.arange(2 * SWA_BLOCK)[None, :]
    dist = t_loc - s_loc
    max_exact = REL_BUCKETS // 2
    d = jnp.maximum(dist, 0)
    df = jnp.maximum(d, 1).astype(F32)
    large = max_exact + (jnp.log(df / max_exact) / math.log(REL_MAX_DIST / max_exact) * (REL_BUCKETS - max_exact)).astype(jnp.int32)
    large = jnp.minimum(large, REL_BUCKETS - 1)
    bucket = jnp.where(d < max_exact, d, large)
    band = (dist >= 0) & (dist < SWA_BLOCK)
    return bucket, band


def _tile2(g):
    return jnp.concatenate([g, g], axis=1) if g.shape[1] == HEAD else g


def _w_in_padded(w_in):
    z = jnp.zeros((w_in.shape[0], PROJ_W - FL0 - 8), w_in.dtype)
    return jnp.concatenate([w_in[:, 2824:5896], w_in[:, 768:1280], w_in[:, 1280:1792], w_in[:, 1792:2304], w_in[:, 2312:2824],
                            w_in[:, 0:512], w_in[:, 512:640], w_in[:, 640:768], w_in[:, 2304:2312], z], axis=1)


def _w_in_unpadded(dwp):
    return jnp.concatenate([dwp[:, QA0:QA0 + 512], dwp[:, KA0:KA0 + 128], dwp[:, VA0:VA0 + 128], dwp[:, QF0:QF0 + 512],
                            dwp[:, KF0:KF0 + 512], dwp[:, VF0:VF0 + 512], dwp[:, FL0:FL0 + 8], dwp[:, QM0:QM0 + 512],
                            dwp[:, GL0:GL0 + 3072]], axis=1)


def _local_step(x, mem, target, p, getw, emit, deps=()):
    s = x.shape[0]
    bucket, band = _bucket_table()
    bucket_m = jnp.where(band, bucket, -1).astype(jnp.int32)
    bias = _bias_table(p["rel_bias"], bucket_m)
    gqf, gkf, gqa, gka = _tile2(p["qn_fox"]), _tile2(p["kn_fox"]), _tile2(p["qn_swa"]), _tile2(p["kn_swa"])
    gqm = p["qn_mem"]
    bf128 = jnp.pad(p["b_forget"], ((0, 0), (0, 120)))
    sink = p["sink_swa"].reshape(8)

    h = _rms_fwd(x, p["g_mix"], "rms_mix", deps)
    w_in = getw("w_in", h)
    proj = _mm(h, w_in, "nn", F32, 512, 768, 1024, "proj")
    qf, kf, vf, qm, qa, ka, va = _proj_post(proj, gqf, gkf, gqm, gqa, gka)
    cc4 = _fox_gate_fwd(proj, bf128)
    w_kv = getw("w_mem_kv", cc4)
    mem_n, kv_raw, mk, mv = _memkv_fwd(mem, p["g_mem"], w_kv, p["kn_mem"])
    kp = jnp.pad(ka, ((SWA_BLOCK, 0), (0, 0)))
    vp = jnp.pad(va, ((SWA_BLOCK, 0), (0, 0)))
    oa = _swa_fwd(qa, kp, vp, bias, sink)
    of, lse4 = _fox_fwd(qf, kf, jnp.transpose(vf), cc4)
    om = _mem_fwd(qm, mk, mv)
    wa, wf, wm, w_out = getw("w_o_swa", oa), getw("w_o_fox", oa), getw("w_o_mem", oa), getw("w_out", oa)
    x1, hm, merged = _merge_fwd(x, oa, of, om, proj, p["b_gate"], wa, wf, wm, w_out, p["g_mlp"])
    w_up = getw("w_mlp_up", of)
    a, u = _mlp_up(hm, w_up)
    w_down = getw("w_mlp_down", hm)
    dy, loss = _mlp_down_loss(u, w_down, x1, target)

    da = _mlp_bwd_act(dy, w_down, a)
    emit("w_mlp_down", _mm(u, dy, "tn", F32, 1024, 1024, 512, "dw_down"))
    dx1, dg_mlp = _mlp_bwd_x(da, w_up, x1, dy, p["g_mlp"])
    emit("w_mlp_up", _mm(hm, da, "tn", F32, 512, 1024, 512, "dw_up"))
    dproj, doa, dof, dom, dya, dyf, dym, db_gate = _merge_bwd(dx1, oa, of, om, proj, p["b_gate"], wa, wf, wm, w_out)
    emit("w_out", _mm(merged, dx1, "tn", F32, 512, 1024, 512, "dw_out"))
    emit("w_o_swa", _mm(oa, dya, "tn", F32, 512, 1024, 512, "dw_o_swa"))
    emit("w_o_fox", _mm(of, dyf, "tn", F32, 512, 1024, 512, "dw_o_fox"))
    emit("w_o_mem", _mm(om, dym, "tn", F32, 512, 1024, 512, "dw_o_mem"))

    dqm, dmk, dmv = _mem_bwd(qm, mk, mv, dom)
    dw_kv, dkn_mem, dg_mem = _memkv_bwd(dmk, dmv, kv_raw, p["kn_mem"], mem, p["g_mem"], mem_n, w_kv)
    emit("w_mem_kv", dw_kv)
    dqa, dkp, dvp, dbias, dsink = _swa_bwd(qa, kp, vp, bias, sink, doa)
    dqf_t, dkf, dvf, dck4, dcq4 = _fox_bwd(qf, kf, vf, dof, of, cc4, lse4)
    dqf = jnp.transpose(dqf_t)

    dcq = jnp.transpose(dcq4[:, 0:2, :], (2, 0, 1)).reshape(s, 8)
    dck = jnp.transpose(dck4[:, :, 0:2], (1, 0, 2)).reshape(s, 8)
    dc = jnp.pad(dcq - dck, ((0, 0), (0, 120)))
    dfl, db_forget = _fox_gate_bwd(dc, proj, bf128)

    dproj, dgn = _proj_pre_bwd(dproj, proj, dqf, dkf, dvf, dqm, dqa, dkp[SWA_BLOCK:], dvp[SWA_BLOCK:], dfl,
                               gqf, gkf, gqm, gqa, gka)
    emit("w_in", _mm(h, dproj, "tn", F32, 512, 1536, 512, "dw_in"))
    grad_x, dg_mix = _in_bwd_x(dproj, w_in, x, p["g_mix"], dx1)
    d_rel = _rel_bias_bwd(dbias, bucket_m)

    fold = lambda r: dgn[r:r + 1, 0:HEAD] + dgn[r:r + 1, HEAD:128]
    small = {
        "g_mix": dg_mix, "b_gate": db_gate, "b_forget": db_forget[:, 0:8],
        "qn_swa": fold(3), "kn_swa": fold(4), "sink_swa": dsink[:, 0:8], "rel_bias": d_rel[:, 0:8],
        "qn_fox": fold(0), "kn_fox": fold(1), "g_mem": dg_mem, "qn_mem": dgn[2:3, :], "kn_mem": dkn_mem,
        "g_mlp": dg_mlp,
    }
    return loss, grad_x, small


SMALL = ("g_mix", "b_gate", "b_forget", "qn_swa", "kn_swa", "sink_swa", "rel_bias", "qn_fox", "kn_fox", "g_mem",
         "qn_mem", "kn_mem", "g_mlp")
BIG = ("w_in", "w_mem_kv", "w_o_swa", "w_o_fox", "w_o_mem", "w_out", "w_mlp_up", "w_mlp_down")
COL_SHARDED = ("w_in", "w_o_swa", "w_o_fox", "w_o_mem", "w_mlp_up")
WEIGHTS = ("g_mix", "w_in", "b_gate", "b_forget", "qn_swa", "kn_swa", "sink_swa", "rel_bias", "qn_fox", "kn_fox", "g_mem",
           "w_mem_kv", "qn_mem", "kn_mem", "w_o_swa", "w_o_fox", "w_o_mem", "w_out", "g_mlp", "w_mlp_up", "w_mlp_down")
SMALL_PAD = 7168


def _gathered_to_full(name, g):
    if name in COL_SHARDED:
        return jnp.transpose(g, (1, 0, 2)).reshape(g.shape[1], N_DEV * g.shape[2])
    return g.reshape(N_DEV * g.shape[1], g.shape[2])


def _full_to_parts(name, full, b):
    if name in COL_SHARDED:
        return jnp.transpose(full.reshape(full.shape[0], N_DEV, b), (1, 0, 2)).astype(BF16)
    return full.reshape(N_DEV, full.shape[0] // N_DEV, full.shape[1]).astype(BF16)


def _pack_small(d):
    flat = jnp.concatenate([d[n].reshape(-1) for n in SMALL])
    return jnp.pad(flat, (0, SMALL_PAD - flat.shape[0])).reshape(8, SMALL_PAD // 8)


def _unpack_small(packed, like):
    flat = packed.reshape(-1)
    out, off = {}, 0
    for n in SMALL:
        size = like[n].size
        out[n] = flat[off:off + size].reshape(like[n].shape)
        off += size
    return out


def _adamw_small(parts, w, m, v):
    def body(p_ref, w_ref, m_ref, v_ref, g_ref, d_ref, m2_ref, v2_ref):
        g = p_ref[0]
        for k in range(1, N_DEV):
            g = g + p_ref[k]
        delta, m2, v2 = _adam_math(w_ref[...], g, m_ref[...], v_ref[...])
        g_ref[...] = g
        d_ref[...] = delta
        m2_ref[...] = m2
        v2_ref[...] = v2

    sd = jax.ShapeDtypeStruct(w.shape, F32)
    return pl.pallas_call(body, name="adamw_small", out_shape=[sd, sd, sd, sd])(parts, w, m, v)


def kernel(x, mem, g_mix, w_in, b_gate, b_forget, qn_swa, kn_swa, sink_swa, rel_bias, qn_fox, kn_fox, g_mem, w_mem_kv, qn_mem, kn_mem, w_o_swa, w_o_fox, w_o_mem, w_out, g_mlp, w_mlp_up, w_mlp_down, loss_target, m_g_mix, m_w_in, m_b_gate, m_b_forget, m_qn_swa, m_kn_swa, m_sink_swa, m_rel_bias, m_qn_fox, m_kn_fox, m_g_mem, m_w_mem_kv, m_qn_mem, m_kn_mem, m_w_o_swa, m_w_o_fox, m_w_o_mem, m_w_out, m_g_mlp, m_w_mlp_up, m_w_mlp_down, v_g_mix, v_w_in, v_b_gate, v_b_forget, v_qn_swa, v_kn_swa, v_sink_swa, v_rel_bias, v_qn_fox, v_kn_fox, v_g_mem, v_w_mem_kv, v_qn_mem, v_kn_mem, v_w_o_swa, v_w_o_fox, v_w_o_mem, v_w_out, v_g_mlp, v_w_mlp_up, v_w_mlp_down):
    wts = dict(g_mix=g_mix, w_in=w_in, b_gate=b_gate, b_forget=b_forget, qn_swa=qn_swa, kn_swa=kn_swa, sink_swa=sink_swa,
               rel_bias=rel_bias, qn_fox=qn_fox, kn_fox=kn_fox, g_mem=g_mem, w_mem_kv=w_mem_kv, qn_mem=qn_mem, kn_mem=kn_mem,
               w_o_swa=w_o_swa, w_o_fox=w_o_fox, w_o_mem=w_o_mem, w_out=w_out, g_mlp=g_mlp, w_mlp_up=w_mlp_up,
               w_mlp_down=w_mlp_down)
    mom = dict(g_mix=m_g_mix, w_in=m_w_in, b_gate=m_b_gate, b_forget=m_b_forget, qn_swa=m_qn_swa, kn_swa=m_kn_swa,
               sink_swa=m_sink_swa, rel_bias=m_rel_bias, qn_fox=m_qn_fox, kn_fox=m_kn_fox, g_mem=m_g_mem, w_mem_kv=m_w_mem_kv,
               qn_mem=m_qn_mem, kn_mem=m_kn_mem, w_o_swa=m_w_o_swa, w_o_fox=m_w_o_fox, w_o_mem=m_w_o_mem, w_out=m_w_out,
               g_mlp=m_g_mlp, w_mlp_up=m_w_mlp_up, w_mlp_down=m_w_mlp_down)
    var = dict(g_mix=v_g_mix, w_in=v_w_in, b_gate=v_b_gate, b_forget=v_b_forget, qn_swa=v_qn_swa, kn_swa=v_kn_swa,
               sink_swa=v_sink_swa, rel_bias=v_rel_bias, qn_fox=v_qn_fox, kn_fox=v_kn_fox, g_mem=v_g_mem, w_mem_kv=v_w_mem_kv,
               qn_mem=v_qn_mem, kn_mem=v_kn_mem, w_o_swa=v_w_o_swa, w_o_fox=v_w_o_fox, w_o_mem=v_w_o_mem, w_out=v_w_out,
               g_mlp=v_g_mlp, w_mlp_up=v_w_mlp_up, w_mlp_down=v_w_mlp_down)

    me = _index(_my_place())
    dev = lax.broadcasted_iota(jnp.int32, (N_DEV, 1, 1), 0)

    shards = {n: wts[n][0].astype(BF16) for n in BIG}
    gathers = {n: _split_start(shards[n], N_DEV, True, "ag_start_" + n) for n in BIG}
    full = {}

    def getw(n, after):
        if n not in full:
            _, land = _split_wait(gathers[n], after, True, "ag_wait_" + n)
            w = _gathered_to_full(n, jnp.where(dev == me, shards[n][None], land))
            full[n] = _w_in_padded(w) if n == "w_in" else w
        return full[n]

    exchanges = {}

    def emit(n, grad):
        grad = _w_in_unpadded(grad) if n == "w_in" else grad
        exchanges[n] = _split_start(_full_to_parts(n, grad, wts[n].shape[2]), N_DEV - 1, False, "rs_start_" + n)

    small_p = {n: wts[n] for n in SMALL}
    loss, grad_x, small_g = _local_step(x[0], mem[0], loss_target[0], small_p, getw, emit,
                                        tuple(gathers[n][4] for n in BIG))

    grads, delta, new_m, new_v = {}, {}, {}, {}
    for n in exchanges:
        parts, land = _split_wait(exchanges[n], grad_x, False, "rs_wait_" + n)
        own = lax.dynamic_index_in_dim(parts, me, 0, keepdims=False)
        g, d, m2, v2 = _adamw(own, land, wts[n][0], mom[n][0], var[n][0], "adamw_" + n)
        grads[n], delta[n], new_m[n], new_v[n] = g[None], d[None], m2[None], v2[None]

    gathered = _all_gather(_pack_small(small_g), "ag_small")
    g, d, m2, v2 = _adamw_small(gathered, _pack_small(small_p), _pack_small({n: mom[n] for n in SMALL}),
                                _pack_small({n: var[n] for n in SMALL}))
    for dst, packed in ((grads, g), (delta, d), (new_m, m2), (new_v, v2)):
        dst.update(_unpack_small(packed, small_p))

    total = lax.psum(loss[0, 0], ("x", "y", "c"))
    return (total, grad_x[None], *[grads[n] for n in WEIGHTS], *[delta[n] for n in WEIGHTS],
            *[new_m[n] for n in WEIGHTS], *[new_v[n] for n in WEIGHTS])
```

```python
import functools
import math

import jax
import jax.numpy as jnp
from jax import lax
from jax.experimental import pallas as pl
from jax.experimental.pallas import tpu as pltpu

F32 = jnp.float32
BF16 = jnp.bfloat16

D_MODEL = 1024
N_MEM = 256
D_FF = 4096
HEAD = 64
SWA_HEADS = 8
SWA_BLOCK = 128
MEM_HEADS = 4
MEM_HEAD = 128
EPS = 1e-6
NEG = -1e30
REL_BUCKETS = 32
REL_MAX_DIST = 128

ADAM_LR = 0.001
ADAM_B1 = 0.9
ADAM_B2 = 0.999
ADAM_EPS = 1e-08
ADAM_WD = 0.01
ADAM_STEP = 10

GL0, QF0, KF0, VF0, QM0, QA0, KA0, VA0, FL0 = 0, 3072, 3584, 4096, 4608, 5120, 5632, 5760, 5888
PROJ_W = 6144
HALF_W = 3072
H_QF, H_KF, H_VF, H_QM, H_QA, H_KA, H_VA, H_FL = 0, 512, 1024, 1536, 2048, 2560, 2688, 2816

VMEM_LIMIT = 56 * 1024 * 1024
N_DEV = 8
MESH = pl.DeviceIdType.MESH

NN = (((1,), (0,)), ((), ()))
NT = (((1,), (1,)), ((), ()))
TN = (((0,), (0,)), ((), ()))


def _dot(a, b, dims=NN):
    return lax.dot_general(a, b, dims, preferred_element_type=F32)


def _params(sem):
    return pltpu.CompilerParams(dimension_semantics=sem, vmem_limit_bytes=VMEM_LIMIT)


def _full(shape):
    nd = len(shape)
    return pl.BlockSpec(shape, lambda *_: (0,) * nd)


def _sigmoid(z):
    return 1.0 / (1.0 + jnp.exp(-z))


def _group_mean(v, hd):
    if hd == 128:
        return jnp.mean(v, axis=-1, keepdims=True)
    lane = lax.broadcasted_iota(jnp.int32, v.shape, 1)
    lo = lane < HEAD
    s_lo = jnp.sum(jnp.where(lo, v, 0.0), axis=-1, keepdims=True)
    s_hi = jnp.sum(jnp.where(lo, 0.0, v), axis=-1, keepdims=True)
    return jnp.where(lo, s_lo, s_hi) * (1.0 / HEAD)


def _mm(a, b, mode, out_dtype, tm, tn, tk, name):
    if mode == "nn":
        m, k = a.shape
        n = b.shape[1]
    elif mode == "nt":
        m, k = a.shape
        n = b.shape[0]
    else:
        k, m = a.shape
        n = b.shape[1]
    tm, tn, tk = min(tm, m), min(tn, n), min(tk, k)
    nk = k // tk
    dims = {"nn": NN, "nt": NT, "tn": TN}[mode]
    a_spec = pl.BlockSpec((tk, tm), lambda i, j, kk: (kk, i)) if mode == "tn" else pl.BlockSpec((tm, tk), lambda i, j, kk: (i, kk))
    b_spec = pl.BlockSpec((tn, tk), lambda i, j, kk: (j, kk)) if mode == "nt" else pl.BlockSpec((tk, tn), lambda i, j, kk: (kk, j))

    def body(a_ref, b_ref, o_ref, acc_ref):
        prod = _dot(a_ref[...].astype(BF16), b_ref[...].astype(BF16), dims)
        if nk == 1:
            o_ref[...] = prod.astype(o_ref.dtype)
        else:
            kk = pl.program_id(2)

            @pl.when(kk == 0)
            def _():
                acc_ref[...] = prod

            @pl.when(kk > 0)
            def _():
                acc_ref[...] += prod

            @pl.when(kk == nk - 1)
            def _():
                o_ref[...] = acc_ref[...].astype(o_ref.dtype)

    return pl.pallas_call(
        body, name=name, grid=(m // tm, n // tn, nk),
        in_specs=[a_spec, b_spec],
        out_specs=pl.BlockSpec((tm, tn), lambda i, j, kk: (i, j)),
        out_shape=jax.ShapeDtypeStruct((m, n), out_dtype),
        scratch_shapes=[pltpu.VMEM((tm, tn), F32)],
        compiler_params=_params(("parallel", "parallel", "arbitrary")),
    )(a, b)


def _rms_fwd(x, g, name, deps=()):
    s, d = x.shape
    tm = min(512, s)

    def body(x_ref, g_ref, *rest):
        h_ref = rest[len(deps)]
        xv = x_ref[...]
        r = lax.rsqrt(jnp.mean(xv * xv, axis=-1, keepdims=True) + EPS)
        h_ref[...] = (xv * r * g_ref[...]).astype(BF16)

    return pl.pallas_call(
        body, name=name, grid=(s // tm,),
        in_specs=[pl.BlockSpec((tm, d), lambda i: (i, 0)), _full((1, d))] + [pl.BlockSpec(memory_space=pl.ANY)] * len(deps),
        out_specs=pl.BlockSpec((tm, d), lambda i: (i, 0)),
        out_shape=jax.ShapeDtypeStruct((s, d), BF16),
        compiler_params=_params(("parallel",)),
    )(x, g, *deps)


def _proj_post(proj, gq_fox, gk_fox, gq_mem, gq_swa, gk_swa):
    s = proj.shape[0]
    tm = min(256, s)

    def body(p_ref, gqf, gkf, gqm, gqa, gka, qf_ref, kf_ref, vf_ref, qm_ref, qa_ref, ka_ref, va_ref):
        def norm(off, width, hd, g_ref, o_ref):
            for b in range(width // 128):
                v = p_ref[:, off + b * 128: off + (b + 1) * 128]
                r = lax.rsqrt(_group_mean(v * v, hd) + EPS)
                o_ref[:, b * 128:(b + 1) * 128] = (v * r * g_ref[...]).astype(BF16)

        norm(H_QF, 512, HEAD, gqf, qf_ref)
        norm(H_KF, 512, HEAD, gkf, kf_ref)
        vf_ref[...] = p_ref[:, H_VF:H_VF + 512].astype(BF16)
        norm(H_QM, 512, MEM_HEAD, gqm, qm_ref)
        norm(H_QA, 512, HEAD, gqa, qa_ref)
        norm(H_KA, 128, HEAD, gka, ka_ref)
        va_ref[...] = p_ref[:, H_VA:H_VA + 128].astype(BF16)

    g_spec = _full((1, 128))
    o512 = pl.BlockSpec((tm, 512), lambda i: (i, 0))
    o128 = pl.BlockSpec((tm, 128), lambda i: (i, 0))
    s512 = jax.ShapeDtypeStruct((s, 512), BF16)
    s128 = jax.ShapeDtypeStruct((s, 128), BF16)
    return pl.pallas_call(
        body, name="proj_post", grid=(s // tm,),
        in_specs=[pl.BlockSpec((tm, HALF_W), lambda i: (i, 1)), g_spec, g_spec, g_spec, g_spec, g_spec],
        out_specs=[o512, o512, o512, o512, o512, o128, o128],
        out_shape=[s512, s512, s512, s512, s512, s128, s128],
        compiler_params=_params(("parallel",)),
    )(proj, gq_fox, gk_fox, gq_mem, gq_swa, gk_swa)


def _tri(n, lower):
    r = lax.broadcasted_iota(jnp.int32, (n, n), 0)
    c = lax.broadcasted_iota(jnp.int32, (n, n), 1)
    return jnp.where((c <= r) if lower else (c >= r), 1.0, 0.0).astype(F32)


def _fox_gate_fwd(proj, b_forget128):
    s = proj.shape[0]
    tm = min(512, s)

    def body(p_ref, b_ref, cc_ref, carry_ref):
        i = pl.program_id(0)

        @pl.when(i == 0)
        def _():
            carry_ref[...] = jnp.zeros_like(carry_ref)

        z = p_ref[...] + b_ref[...]
        logf = jnp.minimum(z, 0.0) - jnp.log(1.0 + jnp.exp(-jnp.abs(z)))
        c = jnp.dot(_tri(tm, True), logf, precision=lax.Precision.HIGHEST, preferred_element_type=F32) + carry_ref[...]
        carry_ref[...] = c[tm - 1:tm, :]
        for hp in range(4):
            cc_ref[hp] = c if hp == 0 else pltpu.roll(c, 128 - 2 * hp, 1)

    return pl.pallas_call(
        body, name="fox_gate_fwd", grid=(s // tm,),
        in_specs=[pl.BlockSpec((tm, 128), lambda i: (i, FL0 // 128)), _full((1, 128))],
        out_specs=pl.BlockSpec((4, tm, 128), lambda i: (0, i, 0)),
        out_shape=jax.ShapeDtypeStruct((4, s, 128), F32),
        scratch_shapes=[pltpu.VMEM((1, 128), F32)],
        compiler_params=_params(("arbitrary",)),
    )(proj, b_forget128)


def _memkv_fwd(mem, g_mem, w_kv, kn_mem):
    m = mem.shape[0]

    def body(mem_ref, g_ref, w_ref, kn_ref, memn_ref, kv_ref, mk_ref, mv_ref):
        xv = mem_ref[...]
        r = lax.rsqrt(jnp.mean(xv * xv, axis=-1, keepdims=True) + EPS)
        mn = (xv * r * g_ref[...]).astype(BF16)
        memn_ref[...] = mn
        kv = _dot(mn, w_ref[...])
        kv_ref[...] = kv
        for h in range(MEM_HEADS):
            v = kv[:, h * 128:(h + 1) * 128]
            rr = lax.rsqrt(jnp.mean(v * v, axis=-1, keepdims=True) + EPS)
            mk_ref[:, h * 128:(h + 1) * 128] = (v * rr * kn_ref[...]).astype(BF16)
        mv_ref[...] = kv[:, 512:1024].astype(BF16)

    return pl.pallas_call(
        body, name="memkv_fwd",
        out_shape=[jax.ShapeDtypeStruct((m, D_MODEL), BF16), jax.ShapeDtypeStruct((m, 1024), F32),
                   jax.ShapeDtypeStruct((m, 512), BF16), jax.ShapeDtypeStruct((m, 512), BF16)],
        compiler_params=pltpu.CompilerParams(vmem_limit_bytes=VMEM_LIMIT),
    )(mem, g_mem, w_kv, kn_mem)


def _bias_table(rel_bias, bucket):
    def body(rb_ref, bk_ref, o_ref):
        bk = bk_ref[...]
        for h in range(SWA_HEADS):
            acc = jnp.zeros(bk.shape, F32)
            for b in range(REL_BUCKETS):
                acc = jnp.where(bk == b, rb_ref[b, h], acc)
            o_ref[h] = acc

    return pl.pallas_call(
        body, name="bias_table",
        in_specs=[pl.BlockSpec(memory_space=pltpu.SMEM), pl.BlockSpec(memory_space=pltpu.VMEM)],
        out_shape=jax.ShapeDtypeStruct((SWA_HEADS,) + bucket.shape, F32),
    )(rel_bias, bucket)


def _swa_valid(n):
    row = lax.broadcasted_iota(jnp.int32, (SWA_BLOCK, 2 * SWA_BLOCK), 0)
    col = lax.broadcasted_iota(jnp.int32, (SWA_BLOCK, 2 * SWA_BLOCK), 1)
    dist = row + SWA_BLOCK - col
    return (dist >= 0) & (dist < SWA_BLOCK) & ((col >= SWA_BLOCK) | (n > 0))


def _swa_fwd(qa, kp, vp, bias, sink):
    s = qa.shape[0]
    nb = s // SWA_BLOCK

    def body(sink_ref, q_ref, kp_ref, vp_ref, bias_ref, o_ref):
        n = pl.program_id(0)
        start = pl.multiple_of(n * SWA_BLOCK, SWA_BLOCK)
        k2 = kp_ref[pl.ds(start, 2 * SWA_BLOCK), :]
        v2 = vp_ref[pl.ds(start, 2 * SWA_BLOCK), :]
        valid = _swa_valid(n)
        for h in range(SWA_HEADS):
            kv = h // 4
            qh = q_ref[:, h * HEAD:(h + 1) * HEAD]
            kh = k2[:, kv * HEAD:(kv + 1) * HEAD]
            vh = v2[:, kv * HEAD:(kv + 1) * HEAD]
            sc = _dot(qh, kh, NT) * 0.125 + bias_ref[h]
            sc = jnp.where(valid, sc, NEG)
            sk = sink_ref[h]
            mx = jnp.maximum(jnp.max(sc, axis=-1, keepdims=True), sk)
            p = jnp.exp(sc - mx)
            den = jnp.sum(p, axis=-1, keepdims=True) + jnp.exp(sk - mx)
            p = p / den
            o_ref[:, h * HEAD:(h + 1) * HEAD] = _dot(p.astype(BF16), vh).astype(BF16)

    return pl.pallas_call(
        body, name="swa_fwd", grid=(nb,),
        in_specs=[pl.BlockSpec(memory_space=pltpu.SMEM),
                  pl.BlockSpec((SWA_BLOCK, 512), lambda n: (n, 0)),
                  _full(kp.shape), _full(vp.shape), _full(bias.shape)],
        out_specs=pl.BlockSpec((SWA_BLOCK, 512), lambda n: (n, 0)),
        out_shape=jax.ShapeDtypeStruct((s, 512), BF16),
        compiler_params=_params(("parallel",)),
    )(sink, qa, kp, vp, bias)


def _head_mask(e):
    lane = lax.broadcasted_iota(jnp.int32, (1, 128), 1)
    return (lane >= e * HEAD) & (lane < (e + 1) * HEAD)


FOX_T = 256
FOX_TK = 512


def _head_rows(e):
    row = lax.broadcasted_iota(jnp.int32, (128, 1), 0)
    return (row >= e * HEAD) & (row < (e + 1) * HEAD)


def _fox_fwd(q, k, v_t, cc4):
    s = q.shape[0]
    t = min(FOX_T, s)
    tk = min(FOX_TK, s)
    nq = s // t

    def body(q_ref, k_ref, vt_ref, cc_ref, o_ref, lse_ref):
        i = pl.program_id(1)
        qs = q_ref[...] * jnp.asarray(0.125, BF16)
        qe = [jnp.where(_head_mask(e), qs, jnp.zeros_like(qs)) for e in range(2)]
        n_full = (i * t) // tk
        krow = lax.broadcasted_iota(jnp.int32, (tk, t), 0) + n_full * tk
        qcol = lax.broadcasted_iota(jnp.int32, (tk, t), 1) + i * t

        def step(j, carry, masked):
            ks = pl.ds(pl.multiple_of(j * tk, tk), tk)
            kj = k_ref[ks, :]
            vtj = vt_ref[:, ks]
            out = []
            for e in range(2):
                m, acc = carry[2 * e], carry[2 * e + 1]
                st = _dot(kj, qe[e], NT) - cc_ref[0, ks, e:e + 1]
                if masked:
                    st = jnp.where(krow <= qcol, st, NEG)
                m_new = jnp.maximum(m, jnp.max(st, axis=0, keepdims=True))
                alpha = jnp.exp(m - m_new)
                pt = jnp.exp(st - m_new).astype(BF16)
                vte = jnp.where(_head_rows(e), vtj, jnp.ones_like(vtj))
                out += [m_new, alpha * acc + _dot(vte, pt)]
            return tuple(out)

        init = (jnp.full((1, t), NEG, F32), jnp.zeros((128, t), F32)) * 2
        carry = lax.fori_loop(0, n_full, functools.partial(step, masked=False), init)
        m0, a0, m1, a1 = step(n_full, carry, True)
        l0 = a0[HEAD:HEAD + 1, :]
        l1 = a1[0:1, :]
        o_t = jnp.where(_head_rows(0), a0 / l0, a1 / l1)
        o_ref[...] = o_t.T.astype(BF16)
        r8 = lax.broadcasted_iota(jnp.int32, (8, t), 0)
        lse_ref[0] = jnp.where(r8 == 0, m0 + jnp.log(l0), jnp.where(r8 == 1, m1 + jnp.log(l1), 0.0))

    return pl.pallas_call(
        body, name="fox_fwd", grid=(4, nq),
        in_specs=[pl.BlockSpec((t, 128), lambda hp, i: (i, hp)),
                  pl.BlockSpec((s, 128), lambda hp, i: (0, hp)),
                  pl.BlockSpec((128, s), lambda hp, i: (hp, 0)),
                  pl.BlockSpec((1, s, 128), lambda hp, i: (hp, 0, 0))],
        out_specs=[pl.BlockSpec((t, 128), lambda hp, i: (i, hp)),
                   pl.BlockSpec((1, 8, t), lambda hp, i: (hp, 0, i))],
        out_shape=[jax.ShapeDtypeStruct((s, 512), BF16), jax.ShapeDtypeStruct((4, 8, s), F32)],
        compiler_params=_params(("parallel", "parallel")),
    )(q, k, v_t, cc4)


MEM_SCALE = MEM_HEAD ** -0.5


def _mem_fwd(qm, mk, mv):
    s = qm.shape[0]
    tq = min(512, s)

    def body(q_ref, mk_ref, mv_ref, o_ref):
        for h in range(MEM_HEADS):
            hs = slice(h * 128, (h + 1) * 128)
            sc = _dot(q_ref[:, hs], mk_ref[:, hs], NT) * MEM_SCALE
            mx = jnp.max(sc, axis=-1, keepdims=True)
            p = jnp.exp(sc - mx)
            p = p / jnp.sum(p, axis=-1, keepdims=True)
            o_ref[:, hs] = _dot(p.astype(BF16), mv_ref[:, hs]).astype(BF16)

    return pl.pallas_call(
        body, name="mem_fwd", grid=(s // tq,),
        in_specs=[pl.BlockSpec((tq, 512), lambda i: (i, 0)), _full(mk.shape), _full(mv.shape)],
        out_specs=pl.BlockSpec((tq, 512), lambda i: (i, 0)),
        out_shape=jax.ShapeDtypeStruct((s, 512), BF16),
        compiler_params=_params(("parallel",)),
    )(qm, mk, mv)


def _merge_fwd(x, oa, of, om, proj, b_gate, wa, wf, wm, w_out, g_mlp):
    s = x.shape[0]
    tm = min(256, s)

    def body(x_ref, oa_ref, of_ref, om_ref, gl_ref, bg_ref, wa_ref, wf_ref, wm_ref, wo_ref, g_ref, x1_ref, hm_ref, mg_ref):
        merged = None
        for b, (o_ref, w_ref) in enumerate(((oa_ref, wa_ref), (of_ref, wf_ref), (om_ref, wm_ref))):
            cs = slice(b * D_MODEL, (b + 1) * D_MODEL)
            y = _dot(o_ref[...], w_ref[...])
            t = _sigmoid(gl_ref[:, cs] + bg_ref[:, cs]) * y
            merged = t if merged is None else merged + t
        mb = merged.astype(BF16)
        mg_ref[...] = mb
        x1 = x_ref[...] + _dot(mb, wo_ref[...])
        x1_ref[...] = x1
        r = lax.rsqrt(jnp.mean(x1 * x1, axis=-1, keepdims=True) + EPS)
        hm_ref[...] = (x1 * r * g_ref[...]).astype(BF16)

    row = lambda w: pl.BlockSpec((tm, w), lambda i: (i, 0))
    return pl.pallas_call(
        body, name="merge_fwd", grid=(s // tm,),
        in_specs=[row(D_MODEL), row(512), row(512), row(512), row(HALF_W), _full((1, HALF_W)),
                  _full(wa.shape), _full(wf.shape), _full(wm.shape), _full(w_out.shape), _full((1, D_MODEL))],
        out_specs=[row(D_MODEL), row(D_MODEL), row(D_MODEL)],
        out_shape=[jax.ShapeDtypeStruct((s, D_MODEL), F32), jax.ShapeDtypeStruct((s, D_MODEL), BF16),
                   jax.ShapeDtypeStruct((s, D_MODEL), BF16)],
        compiler_params=_params(("parallel",)),
    )(x, oa, of, om, proj, b_gate, wa, wf, wm, w_out, g_mlp)


def _mlp_up(hm, w_up):
    s = hm.shape[0]
    tm, tn = min(512, s), 1024

    def body(h_ref, w_ref, a_ref, u_ref):
        a = _dot(h_ref[...], w_ref[...])
        a_ref[...] = a
        r = jnp.maximum(a, 0.0)
        u_ref[...] = (r * r).astype(BF16)

    return pl.pallas_call(
        body, name="mlp_up", grid=(s // tm, D_FF // tn),
        in_specs=[pl.BlockSpec((tm, D_MODEL), lambda i, j: (i, 0)), pl.BlockSpec((D_MODEL, tn), lambda i, j: (0, j))],
        out_specs=[pl.BlockSpec((tm, tn), lambda i, j: (i, j)), pl.BlockSpec((tm, tn), lambda i, j: (i, j))],
        out_shape=[jax.ShapeDtypeStruct((s, D_FF), F32), jax.ShapeDtypeStruct((s, D_FF), BF16)],
        compiler_params=_params(("parallel", "parallel")),
    )(hm, w_up)


def _mlp_down_loss(u, w_down, x1, target):
    s = u.shape[0]
    tm = min(256, s)

    def body(u_ref, w_ref, x1_ref, t_ref, dy_ref, loss_ref):
        i = pl.program_id(0)

        @pl.when(i == 0)
        def _():
            loss_ref[...] = jnp.zeros_like(loss_ref)

        y = x1_ref[...] + _dot(u_ref[...], w_ref[...])
        err = y - t_ref[...]
        dy_ref[...] = err * (1.0 / D_MODEL)
        part = jnp.sum(jnp.sum(err * err, axis=-1, keepdims=True) * (1.0 / D_MODEL), axis=0, keepdims=True)
        loss_ref[...] += 0.5 * part

    row = pl.BlockSpec((tm, D_MODEL), lambda i: (i, 0))
    return pl.pallas_call(
        body, name="mlp_down_loss", grid=(s // tm,),
        in_specs=[pl.BlockSpec((tm, D_FF), lambda i: (i, 0)), _full(w_down.shape), row, row],
        out_specs=[row, _full((1, 1))],
        out_shape=[jax.ShapeDtypeStruct((s, D_MODEL), F32), jax.ShapeDtypeStruct((1, 1), F32)],
        compiler_params=_params(("arbitrary",)),
    )(u, w_down, x1, target)


def _mlp_bwd_act(dy, w_down, a):
    s = dy.shape[0]
    tm, tn = min(512, s), 1024

    def body(dy_ref, w_ref, a_ref, da_ref):
        du = _dot(dy_ref[...].astype(BF16), w_ref[...], NT)
        da_ref[...] = (du * (2.0 * jnp.maximum(a_ref[...], 0.0))).astype(BF16)

    return pl.pallas_call(
        body, name="mlp_bwd_act", grid=(s // tm, D_FF // tn),
        in_specs=[pl.BlockSpec((tm, D_MODEL), lambda i, j: (i, 0)), pl.BlockSpec((tn, D_MODEL), lambda i, j: (j, 0)),
                  pl.BlockSpec((tm, tn), lambda i, j: (i, j))],
        out_specs=pl.BlockSpec((tm, tn), lambda i, j: (i, j)),
        out_shape=jax.ShapeDtypeStruct((s, D_FF), BF16),
        compiler_params=_params(("parallel", "parallel")),
    )(dy, w_down, a)


def _rms_bwd(xv, g, dh, skip):
    r = lax.rsqrt(jnp.mean(xv * xv, axis=-1, keepdims=True) + EPS)
    n = xv * r
    dn = dh * g
    dx = skip + r * (dn - n * jnp.mean(dn * n, axis=-1, keepdims=True))
    return dx, jnp.sum(dh * n, axis=0, keepdims=True)


def _mlp_bwd_x(da, w_up, x1, dy, g_mlp):
    s = da.shape[0]
    tm = min(256, s)

    def body(da_ref, w_ref, x1_ref, dy_ref, g_ref, dx1_ref, dg_ref):
        i = pl.program_id(0)

        @pl.when(i == 0)
        def _():
            dg_ref[...] = jnp.zeros_like(dg_ref)

        dhm = _dot(da_ref[...], w_ref[...], NT)
        dx, dg = _rms_bwd(x1_ref[...], g_ref[...], dhm, dy_ref[...])
        dx1_ref[...] = dx
        dg_ref[...] += dg

    row = pl.BlockSpec((tm, D_MODEL), lambda i: (i, 0))
    return pl.pallas_call(
        body, name="mlp_bwd_x", grid=(s // tm,),
        in_specs=[pl.BlockSpec((tm, D_FF), lambda i: (i, 0)), _full(w_up.shape), row, row, _full((1, D_MODEL))],
        out_specs=[row, _full((1, D_MODEL))],
        out_shape=[jax.ShapeDtypeStruct((s, D_MODEL), F32), jax.ShapeDtypeStruct((1, D_MODEL), F32)],
        compiler_params=_params(("arbitrary",)),
    )(da, w_up, x1, dy, g_mlp)


def _merge_bwd(dx1, oa, of, om, proj, b_gate, wa, wf, wm, w_out):
    s = dx1.shape[0]
    tm = min(256, s)

    def body(dx1_ref, oa_ref, of_ref, om_ref, gl_ref, bg_ref, wa_ref, wf_ref, wm_ref, wo_ref,
             dp_ref, doa_ref, dof_ref, dom_ref, dya_ref, dyf_ref, dym_ref, dbg_ref):
        i = pl.program_id(0)

        @pl.when(i == 0)
        def _():
            dbg_ref[...] = jnp.zeros_like(dbg_ref)

        dmerged = _dot(dx1_ref[...].astype(BF16), wo_ref[...], NT)
        branches = ((oa_ref, wa_ref, doa_ref, dya_ref), (of_ref, wf_ref, dof_ref, dyf_ref), (om_ref, wm_ref, dom_ref, dym_ref))
        for b, (o_ref, w_ref, do_ref, dyb_ref) in enumerate(branches):
            cs = slice(b * D_MODEL, (b + 1) * D_MODEL)
            y = _dot(o_ref[...], w_ref[...])
            g = _sigmoid(gl_ref[:, cs] + bg_ref[:, cs])
            dz = (dmerged * y) * g * (1.0 - g)
            dp_ref[:, cs] = dz.astype(BF16)
            dbg_ref[:, cs] += jnp.sum(dz, axis=0, keepdims=True)
            dyb = (dmerged * g).astype(BF16)
            dyb_ref[...] = dyb
            do_ref[...] = _dot(dyb, w_ref[...], NT).astype(BF16)

    row = lambda w: pl.BlockSpec((tm, w), lambda i: (i, 0))
    sd = lambda w: jax.ShapeDtypeStruct((s, w), BF16)
    return pl.pallas_call(
        body, name="merge_bwd", grid=(s // tm,),
        in_specs=[row(D_MODEL), row(512), row(512), row(512), row(HALF_W), _full((1, HALF_W)),
                  _full(wa.shape), _full(wf.shape), _full(wm.shape), _full(w_out.shape)],
        out_specs=[row(HALF_W), row(512), row(512), row(512), row(D_MODEL), row(D_MODEL), row(D_MODEL), _full((1, HALF_W))],
        out_shape=[sd(PROJ_W), sd(512), sd(512), sd(512), sd(D_MODEL), sd(D_MODEL), sd(D_MODEL),
                   jax.ShapeDtypeStruct((1, HALF_W), F32)],
        compiler_params=_params(("arbitrary",)),
    )(dx1, oa, of, om, proj, b_gate, wa, wf, wm, w_out)


def _swa_bwd(qa, kp, vp, bias, sink, doa):
    s = qa.shape[0]
    nb = s // SWA_BLOCK

    def body(sink_ref, q_ref, kp_ref, vp_ref, bias_ref, do_ref, dq_ref, dkp_ref, dvp_ref, dbias_ref, dsink_ref, sk_acc):
        n = pl.program_id(0)

        @pl.when(n == 0)
        def _():
            dkp_ref[...] = jnp.zeros_like(dkp_ref)
            dvp_ref[...] = jnp.zeros_like(dvp_ref)
            dbias_ref[...] = jnp.zeros_like(dbias_ref)
            sk_acc[...] = jnp.zeros_like(sk_acc)

        start = pl.multiple_of(n * SWA_BLOCK, SWA_BLOCK)
        win = pl.ds(start, 2 * SWA_BLOCK)
        k2 = kp_ref[win, :]
        v2 = vp_ref[win, :]
        valid = _swa_valid(n)
        for kv in range(2):
            hs_kv = slice(kv * HEAD, (kv + 1) * HEAD)
            kh = k2[:, hs_kv]
            vh = v2[:, hs_kv]
            dk2 = jnp.zeros((2 * SWA_BLOCK, HEAD), F32)
            dv2 = jnp.zeros((2 * SWA_BLOCK, HEAD), F32)
            for g in range(4):
                h = kv * 4 + g
                hs = slice(h * HEAD, (h + 1) * HEAD)
                qh = q_ref[:, hs]
                doh = do_ref[:, hs]
                sc = _dot(qh, kh, NT) * 0.125 + bias_ref[h]
                sc = jnp.where(valid, sc, NEG)
                sk = sink_ref[h]
                mx = jnp.maximum(jnp.max(sc, axis=-1, keepdims=True), sk)
                p = jnp.exp(sc - mx)
                esk = jnp.exp(sk - mx)
                den = jnp.sum(p, axis=-1, keepdims=True) + esk
                p = p / den
                dp = _dot(doh, vh, NT)
                delta = jnp.sum(p * dp, axis=-1, keepdims=True)
                ds = p * (dp - delta)
                sk_acc[:, h:h + 1] += -(esk / den) * delta
                dbias_ref[h] += ds
                dsb = (ds * 0.125).astype(BF16)
                dq_ref[:, hs] = _dot(dsb, kh)
                dk2 = dk2 + _dot(dsb, qh, TN)
                dv2 = dv2 + _dot(p.astype(BF16), doh, TN)
            dkp_ref[win, hs_kv] += dk2
            dvp_ref[win, hs_kv] += dv2

        @pl.when(n == nb - 1)
        def _():
            dsink_ref[...] = jnp.sum(sk_acc[...], axis=0, keepdims=True)

    return pl.pallas_call(
        body, name="swa_bwd", grid=(nb,),
        in_specs=[pl.BlockSpec(memory_space=pltpu.SMEM),
                  pl.BlockSpec((SWA_BLOCK, 512), lambda n: (n, 0)),
                  _full(kp.shape), _full(vp.shape), _full(bias.shape),
                  pl.BlockSpec((SWA_BLOCK, 512), lambda n: (n, 0))],
        out_specs=[pl.BlockSpec((SWA_BLOCK, 512), lambda n: (n, 0)), _full(kp.shape), _full(vp.shape),
                   _full(bias.shape), _full((1, 128))],
        out_shape=[jax.ShapeDtypeStruct((s, 512), F32), jax.ShapeDtypeStruct(kp.shape, F32),
                   jax.ShapeDtypeStruct(vp.shape, F32), jax.ShapeDtypeStruct(bias.shape, F32),
                   jax.ShapeDtypeStruct((1, 128), F32)],
        scratch_shapes=[pltpu.VMEM((SWA_BLOCK, 128), F32)],
        compiler_params=_params(("arbitrary",)),
    )(sink, qa, kp, vp, bias, doa)


def _fox_bwd(q, k, v, do, o, cc4, lse4):
    s = q.shape[0]
    t = min(FOX_T, s)
    tq = min(FOX_TK, s)
    nq = s // t
    nqt = s // tq

    def body(q_ref, k_ref, v_ref, do_ref, o_ref, cc_ref, lse_ref,
             dqt_ref, dk_ref, dv_ref, dck_ref, dcq_ref, delta_ref, dk0, dk1, dv0, dv1, ds0, ds1):
        j = pl.program_id(1)

        @pl.when(j == 0)
        def _():
            dqt_ref[...] = jnp.zeros_like(dqt_ref)
            dcq_ref[...] = jnp.zeros_like(dcq_ref)
            lane8 = lax.broadcasted_iota(jnp.int32, (8, 128), 1)
            row8 = lax.broadcasted_iota(jnp.int32, (8, 128), 0)
            sel = jnp.where((lane8 // HEAD) == row8, 1.0, 0.0).astype(F32)

            def dl(i, c):
                rows = pl.ds(pl.multiple_of(i * tq, tq), tq)
                pr = do_ref[rows, :].astype(F32) * o_ref[rows, :].astype(F32)
                delta_ref[:, rows] = lax.dot_general(sel, pr, NT, precision=lax.Precision.HIGHEST,
                                                     preferred_element_type=F32)
                return c

            lax.fori_loop(0, nqt, dl, 0)

        kj = k_ref[...]
        vj = v_ref[...]
        ks = pl.ds(pl.multiple_of(j * t, t), t)
        kt = (kj.astype(F32) * 0.125).T.astype(BF16)
        ke = [jnp.where(_head_mask(e), kj, jnp.zeros_like(kj)) for e in range(2)]
        ve = [jnp.where(_head_mask(e), vj, jnp.zeros_like(vj)) for e in range(2)]
        kte = [jnp.where(_head_rows(e), kt, jnp.zeros_like(kt)) for e in range(2)]
        ck = [cc_ref[0, ks, e:e + 1] for e in range(2)]
        accs = ((dk0, dv0, ds0), (dk1, dv1, ds1))
        for refs in accs:
            for r in refs:
                r[...] = jnp.zeros_like(r)
        i_first = (j * t) // tq
        krow = lax.broadcasted_iota(jnp.int32, (t, tq), 0) + j * t
        qcol = lax.broadcasted_iota(jnp.int32, (t, tq), 1) + i_first * tq

        def step(i, c, masked):
            rows = pl.ds(pl.multiple_of(i * tq, tq), tq)
            qs = q_ref[rows, :] * jnp.asarray(0.125, BF16)
            doi = do_ref[rows, :]
            for e in range(2):
                dk_acc, dv_acc, ds_acc = accs[e]
                st = _dot(ke[e], qs, NT) - ck[e]
                if masked:
                    st = jnp.where(krow <= qcol, st, NEG)
                pt = jnp.exp(st - lse_ref[0, e:e + 1, rows])
                dpt = _dot(ve[e], doi, NT)
                dst = pt * (dpt - delta_ref[e:e + 1, rows])
                dsb = dst.astype(BF16)
                dv_acc[...] += _dot(pt.astype(BF16), doi)
                dk_acc[...] += _dot(dsb, qs)
                dqt_ref[:, rows] += _dot(kte[e], dsb)
                ds_acc[...] += dst
                dcq_ref[0, e:e + 1, rows] += jnp.sum(dst, axis=0, keepdims=True)
            return c

        step(i_first, 0, True)
        lax.fori_loop(i_first + 1, nqt, functools.partial(step, masked=False), 0)
        m0 = _head_mask(0)
        dk_ref[...] = jnp.where(m0, dk0[...], dk1[...])
        dv_ref[...] = jnp.where(m0, dv0[...], dv1[...])
        lane = lax.broadcasted_iota(jnp.int32, (t, 128), 1)
        c0 = jnp.sum(ds0[...], axis=-1, keepdims=True)
        c1 = jnp.sum(ds1[...], axis=-1, keepdims=True)
        dck_ref[0] = jnp.where(lane == 0, c0, jnp.where(lane == 1, c1, 0.0))

    res = lambda: pl.BlockSpec((s, 128), lambda hp, j: (0, hp))
    blk = lambda: pl.BlockSpec((t, 128), lambda hp, j: (j, hp))
    return pl.pallas_call(
        body, name="fox_bwd", grid=(4, nq),
        in_specs=[res(), blk(), blk(), res(), res(), pl.BlockSpec((1, s, 128), lambda hp, j: (hp, 0, 0)),
                  pl.BlockSpec((1, 8, s), lambda hp, j: (hp, 0, 0))],
        out_specs=[pl.BlockSpec((128, s), lambda hp, j: (hp, 0)), blk(), blk(),
                   pl.BlockSpec((1, t, 128), lambda hp, j: (hp, j, 0)),
                   pl.BlockSpec((1, 8, s), lambda hp, j: (hp, 0, 0))],
        out_shape=[jax.ShapeDtypeStruct((512, s), F32), jax.ShapeDtypeStruct((s, 512), F32),
                   jax.ShapeDtypeStruct((s, 512), F32), jax.ShapeDtypeStruct((4, s, 128), F32),
                   jax.ShapeDtypeStruct((4, 8, s), F32)],
        scratch_shapes=[pltpu.VMEM((8, s), F32)] + [pltpu.VMEM((t, 128), F32)] * 4 + [pltpu.VMEM((t, tq), F32)] * 2,
        compiler_params=_params(("arbitrary", "arbitrary")),
    )(q, k, v, do, o, cc4, lse4)


def _mem_bwd(qm, mk, mv, dom):
    s = qm.shape[0]
    tq = min(512, s)

    def body(q_ref, mk_ref, mv_ref, do_ref, dq_ref, dmk_ref, dmv_ref):
        i = pl.program_id(0)

        @pl.when(i == 0)
        def _():
            dmk_ref[...] = jnp.zeros_like(dmk_ref)
            dmv_ref[...] = jnp.zeros_like(dmv_ref)

        for h in range(MEM_HEADS):
            hs = slice(h * 128, (h + 1) * 128)
            qh = q_ref[:, hs]
            doh = do_ref[:, hs]
            sc = _dot(qh, mk_ref[:, hs], NT) * MEM_SCALE
            mx = jnp.max(sc, axis=-1, keepdims=True)
            p = jnp.exp(sc - mx)
            p = p / jnp.sum(p, axis=-1, keepdims=True)
            dp = _dot(doh, mv_ref[:, hs], NT)
            ds = p * (dp - jnp.sum(p * dp, axis=-1, keepdims=True))
            dsb = (ds * MEM_SCALE).astype(BF16)
            dq_ref[:, hs] = _dot(dsb, mk_ref[:, hs])
            dmk_ref[:, hs] += _dot(dsb, qh, TN)
            dmv_ref[:, hs] += _dot(p.astype(BF16), doh, TN)

    return pl.pallas_call(
        body, name="mem_bwd", grid=(s // tq,),
        in_specs=[pl.BlockSpec((tq, 512), lambda i: (i, 0)), _full(mk.shape), _full(mv.shape),
                  pl.BlockSpec((tq, 512), lambda i: (i, 0))],
        out_specs=[pl.BlockSpec((tq, 512), lambda i: (i, 0)), _full(mk.shape), _full(mv.shape)],
        out_shape=[jax.ShapeDtypeStruct((s, 512), F32), jax.ShapeDtypeStruct(mk.shape, F32),
                   jax.ShapeDtypeStruct(mv.shape, F32)],
        compiler_params=_params(("arbitrary",)),
    )(qm, mk, mv, dom)


def _memkv_bwd(dmk, dmv, kv_raw, kn_mem, mem, g_mem, mem_n, w_kv):
    def body(dmk_ref, dmv_ref, kv_ref, kn_ref, mem_ref, g_ref, mn_ref, w_ref, dw_ref, dkn_ref, dg_ref, dkv_ref):
        dkn = jnp.zeros((1, 128), F32)
        for h in range(MEM_HEADS):
            hs = slice(h * 128, (h + 1) * 128)
            v = kv_ref[:, hs]
            r = lax.rsqrt(jnp.mean(v * v, axis=-1, keepdims=True) + EPS)
            n = v * r
            dn = dmk_ref[:, hs]
            dkn = dkn + jnp.sum(dn * n, axis=0, keepdims=True)
            dng = dn * kn_ref[...]
            dkv_ref[:, hs] = (r * (dng - n * jnp.mean(dng * n, axis=-1, keepdims=True))).astype(BF16)
        dkv_ref[:, 512:1024] = dmv_ref[...].astype(BF16)
        dkn_ref[...] = dkn
        dkv = dkv_ref[...]
        dw_ref[...] = _dot(mn_ref[...], dkv, TN)
        dmn = _dot(dkv, w_ref[...], NT)
        xv = mem_ref[...]
        r = lax.rsqrt(jnp.mean(xv * xv, axis=-1, keepdims=True) + EPS)
        dg_ref[...] = jnp.sum(dmn * (xv * r), axis=0, keepdims=True)

    m = mem.shape[0]
    return pl.pallas_call(
        body, name="memkv_bwd",
        out_shape=[jax.ShapeDtypeStruct((D_MODEL, 1024), F32), jax.ShapeDtypeStruct((1, 128), F32),
                   jax.ShapeDtypeStruct((1, D_MODEL), F32)],
        scratch_shapes=[pltpu.VMEM((m, 1024), BF16)],
        compiler_params=pltpu.CompilerParams(vmem_limit_bytes=VMEM_LIMIT),
    )(dmk, dmv, kv_raw, kn_mem, mem, g_mem, mem_n, w_kv)


def _fox_gate_bwd(dc, proj, b_forget128):
    s = dc.shape[0]
    tm = min(512, s)
    nt = s // tm

    def body(dc_ref, p_ref, b_ref, dfl_ref, db_ref, carry_ref):
        i = pl.program_id(0)

        @pl.when(i == 0)
        def _():
            carry_ref[...] = jnp.zeros_like(carry_ref)
            db_ref[...] = jnp.zeros_like(db_ref)

        dcv = dc_ref[...]
        dlogf = jnp.dot(_tri(tm, False), dcv, precision=lax.Precision.HIGHEST, preferred_element_type=F32) + carry_ref[...]
        carry_ref[...] += jnp.sum(dcv, axis=0, keepdims=True)
        z = p_ref[...] + b_ref[...]
        dfl = dlogf * (1.0 / (1.0 + jnp.exp(z)))
        dfl_ref[...] = dfl.astype(BF16)
        db_ref[...] += jnp.sum(dfl, axis=0, keepdims=True)

    return pl.pallas_call(
        body, name="fox_gate_bwd", grid=(nt,),
        in_specs=[pl.BlockSpec((tm, 128), lambda i: (nt - 1 - i, 0)),
                  pl.BlockSpec((tm, 128), lambda i: (nt - 1 - i, FL0 // 128)), _full((1, 128))],
        out_specs=[pl.BlockSpec((tm, 128), lambda i: (nt - 1 - i, 0)), _full((1, 128))],
        out_shape=[jax.ShapeDtypeStruct((s, 128), BF16), jax.ShapeDtypeStruct((1, 128), F32)],
        scratch_shapes=[pltpu.VMEM((1, 128), F32)],
        compiler_params=_params(("arbitrary",)),
    )(dc, proj, b_forget128)


def _proj_pre_bwd(dproj, proj, dqf, dkf, dvf, dqm, dqa, dka, dva, dfl, gq_fox, gk_fox, gq_mem, gq_swa, gk_swa):
    s = proj.shape[0]
    tm = min(256, s)

    def body(dp_in, p_ref, dqf_ref, dkf_ref, dvf_ref, dqm_ref, dqa_ref, dka_ref, dva_ref, dfl_ref,
             gqf, gkf, gqm, gqa, gka, dp_ref, dgn_ref):
        i = pl.program_id(0)

        @pl.when(i == 0)
        def _():
            dgn_ref[...] = jnp.zeros_like(dgn_ref)

        def norm_bwd(off, width, hd, g_ref, dn_ref, slot):
            acc = jnp.zeros((1, 128), F32)
            for b in range(width // 128):
                v = p_ref[:, off + b * 128: off + (b + 1) * 128]
                r = lax.rsqrt(_group_mean(v * v, hd) + EPS)
                n = v * r
                dn = dn_ref[:, b * 128:(b + 1) * 128]
                acc = acc + jnp.sum(dn * n, axis=0, keepdims=True)
                dng = dn * g_ref[...]
                dp_ref[:, off + b * 128: off + (b + 1) * 128] = (r * (dng - n * _group_mean(dng * n, hd))).astype(BF16)
            dgn_ref[slot:slot + 1, :] += acc

        norm_bwd(H_QF, 512, HEAD, gqf, dqf_ref, 0)
        norm_bwd(H_KF, 512, HEAD, gkf, dkf_ref, 1)
        dp_ref[:, H_VF:H_VF + 512] = dvf_ref[...].astype(BF16)
        norm_bwd(H_QM, 512, MEM_HEAD, gqm, dqm_ref, 2)
        norm_bwd(H_QA, 512, HEAD, gqa, dqa_ref, 3)
        norm_bwd(H_KA, 128, HEAD, gka, dka_ref, 4)
        dp_ref[:, H_VA:H_VA + 128] = dva_ref[...].astype(BF16)
        dp_ref[:, H_FL:H_FL + 128] = dfl_ref[...]
        dp_ref[:, H_FL + 128:HALF_W] = jnp.zeros((tm, HALF_W - H_FL - 128), BF16)

    row = lambda w: pl.BlockSpec((tm, w), lambda i: (i, 0))
    g_spec = _full((1, 128))
    return pl.pallas_call(
        body, name="proj_pre_bwd", grid=(s // tm,),
        in_specs=[pl.BlockSpec(memory_space=pl.ANY), pl.BlockSpec((tm, HALF_W), lambda i: (i, 1)),
                  row(512), row(512), row(512), row(512), row(512), row(128), row(128), row(128),
                  g_spec, g_spec, g_spec, g_spec, g_spec],
        out_specs=[pl.BlockSpec((tm, HALF_W), lambda i: (i, 1)), _full((8, 128))],
        out_shape=[jax.ShapeDtypeStruct((s, PROJ_W), BF16), jax.ShapeDtypeStruct((8, 128), F32)],
        input_output_aliases={0: 0},
        compiler_params=_params(("arbitrary",)),
    )(dproj, proj, dqf, dkf, dvf, dqm, dqa, dka, dva, dfl, gq_fox, gk_fox, gq_mem, gq_swa, gk_swa)


def _in_bwd_x(dproj, w_in_p, x, g_mix, dx1):
    s = x.shape[0]
    tm, tk = min(512, s), 1536
    nk = PROJ_W // tk

    def body(dp_ref, w_ref, x_ref, g_ref, dx1_ref, gx_ref, dg_ref, acc_ref):
        i, kk = pl.program_id(0), pl.program_id(1)

        @pl.when((i == 0) & (kk == 0))
        def _():
            dg_ref[...] = jnp.zeros_like(dg_ref)

        prod = _dot(dp_ref[...], w_ref[...], NT)

        @pl.when(kk == 0)
        def _():
            acc_ref[...] = prod

        @pl.when(kk > 0)
        def _():
            acc_ref[...] += prod

        @pl.when(kk == nk - 1)
        def _():
            dx, dg = _rms_bwd(x_ref[...], g_ref[...], acc_ref[...], dx1_ref[...])
            gx_ref[...] = dx
            dg_ref[...] += dg

    row = pl.BlockSpec((tm, D_MODEL), lambda i, kk: (i, 0))
    return pl.pallas_call(
        body, name="in_bwd_x", grid=(s // tm, nk),
        in_specs=[pl.BlockSpec((tm, tk), lambda i, kk: (i, kk)), pl.BlockSpec((D_MODEL, tk), lambda i, kk: (0, kk)),
                  row, _full((1, D_MODEL)), row],
        out_specs=[row, _full((1, D_MODEL))],
        out_shape=[jax.ShapeDtypeStruct((s, D_MODEL), F32), jax.ShapeDtypeStruct((1, D_MODEL), F32)],
        scratch_shapes=[pltpu.VMEM((tm, D_MODEL), F32)],
        compiler_params=_params(("arbitrary", "arbitrary")),
    )(dproj, w_in_p, x, g_mix, dx1)


def _rel_bias_bwd(dbias, bucket):
    def body(db_ref, bk_ref, o_ref):
        bk = bk_ref[...]
        lane = lax.broadcasted_iota(jnp.int32, (1, 128), 1)
        for b in range(REL_BUCKETS):
            sel = bk == b
            acc = jnp.zeros((1, 128), F32)
            for h in range(SWA_HEADS):
                tot = jnp.sum(jnp.sum(jnp.where(sel, db_ref[h], 0.0), axis=-1, keepdims=True), axis=0, keepdims=True)
                acc = jnp.where(lane == h, tot, acc)
            o_ref[b:b + 1, :] = acc

    return pl.pallas_call(
        body, name="rel_bias_bwd",
        out_shape=jax.ShapeDtypeStruct((REL_BUCKETS, 128), F32),
        compiler_params=pltpu.CompilerParams(vmem_limit_bytes=VMEM_LIMIT),
    )(dbias, bucket)


def _my_place():
    return lax.axis_index("x"), lax.axis_index("y"), lax.axis_index("c")


def _peer(place, k):
    x, y, c = place
    return (1 - x if k & 4 else x, 1 - y if k & 2 else y, 1 - c if k & 1 else c)


def _index(place):
    x, y, c = place
    return 4 * x + 2 * y + c


def _all_gather(shard, name):
    def body(x_ref, out_ref, send_sems, recv_sems, local_sem):
        me = _my_place()
        mine = pltpu.make_async_copy(x_ref, out_ref.at[_index(me)], local_sem)
        mine.start()
        sends = []
        for k in range(1, N_DEV):
            cp = pltpu.make_async_remote_copy(
                src_ref=x_ref, dst_ref=out_ref.at[_index(me)], send_sem=send_sems.at[k - 1], recv_sem=recv_sems.at[k - 1],
                device_id=_peer(me, k), device_id_type=MESH)
            cp.start()
            sends.append(cp)
        for k in range(1, N_DEV):
            peer = _peer(me, k)
            pltpu.make_async_remote_copy(
                src_ref=x_ref, dst_ref=out_ref.at[_index(peer)], send_sem=send_sems.at[k - 1], recv_sem=recv_sems.at[k - 1],
                device_id=peer, device_id_type=MESH).wait_recv()
        for cp in sends:
            cp.wait_send()
        mine.wait()

    any_spec = pl.BlockSpec(memory_space=pl.ANY)
    return pl.pallas_call(
        body, name=name, in_specs=[any_spec], out_specs=any_spec,
        out_shape=jax.ShapeDtypeStruct((N_DEV,) + shard.shape, shard.dtype),
        scratch_shapes=[pltpu.SemaphoreType.DMA((N_DEV - 1,)), pltpu.SemaphoreType.DMA((N_DEV - 1,)), pltpu.SemaphoreType.DMA(())],
    )(shard)


HBM_SPEC = pl.BlockSpec(memory_space=pltpu.HBM)
SEM_SPEC = pl.BlockSpec(memory_space=pltpu.SEMAPHORE)
DATAFLOW = pltpu.SideEffectType.DATAFLOW_SIDE_EFFECTING
PEER_SEMS = pltpu.SemaphoreType.DMA((N_DEV - 1,))


def _split_copy(src_ref, land_ref, send_sems, recv_sems, me, k, gather):
    peer = _peer(me, k)
    if gather:
        src, dst = src_ref, land_ref.at[_index(me)]
    else:
        src, dst = src_ref.at[_index(peer)], land_ref.at[k - 1]
    return pltpu.make_async_remote_copy(src_ref=src, dst_ref=dst, send_sem=send_sems.at[k - 1], recv_sem=recv_sems.at[k - 1],
                                        device_id=peer, device_id_type=MESH)


def _split_start(src, slots, gather, name):
    def body(src_ref, land_ref, send_sems, recv_sems, src_thru, land_thru, token):
        me = _my_place()
        for k in range(1, N_DEV):
            _split_copy(src_ref, land_ref, send_sems, recv_sems, me, k, gather).start()
        token[...] = jnp.zeros_like(token)

    chunk = src.shape if gather else src.shape[1:]
    land = lax.empty((slots,) + chunk, src.dtype)
    return pl.pallas_call(
        body, name=name,
        out_shape=(PEER_SEMS, PEER_SEMS, pltpu.HBM(src.shape, src.dtype), pltpu.HBM(land.shape, land.dtype),
                   jax.ShapeDtypeStruct((8, 128), F32)),
        in_specs=(HBM_SPEC, HBM_SPEC),
        out_specs=(SEM_SPEC, SEM_SPEC, HBM_SPEC, HBM_SPEC, pl.BlockSpec(memory_space=pltpu.VMEM)),
        input_output_aliases={0: 2, 1: 3},
        compiler_params=pltpu.CompilerParams(has_side_effects=DATAFLOW),
    )(pltpu.with_memory_space_constraint(src, pltpu.HBM), pltpu.with_memory_space_constraint(land, pltpu.HBM))


def _split_wait(started, after, gather, name):
    send_sems, recv_sems, src_thru, land_thru, _ = started

    def body(src_ref, land_ref, send_sems, recv_sems, after_ref, src_out, land_out):
        me = _my_place()
        for k in range(1, N_DEV):
            cp = _split_copy(src_ref, land_ref, send_sems, recv_sems, me, k, gather)
            cp.wait_send()
            cp.wait_recv()

    return pl.pallas_call(
        body, name=name,
        out_shape=(pltpu.HBM(src_thru.shape, src_thru.dtype), pltpu.HBM(land_thru.shape, land_thru.dtype)),
        in_specs=(HBM_SPEC, HBM_SPEC, SEM_SPEC, SEM_SPEC, pl.BlockSpec(memory_space=pl.ANY)),
        out_specs=(HBM_SPEC, HBM_SPEC), input_output_aliases={0: 0, 1: 1},
        compiler_params=pltpu.CompilerParams(has_side_effects=DATAFLOW),
    )(src_thru, land_thru, send_sems, recv_sems, after)


def _adam_math(w, g, m, v):
    m2 = ADAM_B1 * m + (1.0 - ADAM_B1) * g
    v2 = ADAM_B2 * v + (1.0 - ADAM_B2) * (g * g)
    m_hat = m2 / (1.0 - ADAM_B1 ** ADAM_STEP)
    v_hat = v2 / (1.0 - ADAM_B2 ** ADAM_STEP)
    delta = -ADAM_LR * (m_hat / (jnp.sqrt(v_hat) + ADAM_EPS) + ADAM_WD * w)
    return delta, m2, v2


def _adamw(own, land, w, m, v, name):
    a, b = w.shape
    ta = min(128, a)

    def body(o_ref, p_ref, w_ref, m_ref, v_ref, g_ref, d_ref, m2_ref, v2_ref):
        g = o_ref[...].astype(F32)
        for k in range(N_DEV - 1):
            g = g + p_ref[k].astype(F32)
        delta, m2, v2 = _adam_math(w_ref[...], g, m_ref[...], v_ref[...])
        g_ref[...] = g
        d_ref[...] = delta
        m2_ref[...] = m2
        v2_ref[...] = v2

    blk = pl.BlockSpec((ta, b), lambda i: (i, 0))
    sd = jax.ShapeDtypeStruct((a, b), F32)
    return pl.pallas_call(
        body, name=name, grid=(a // ta,),
        in_specs=[blk, pl.BlockSpec((N_DEV - 1, ta, b), lambda i: (0, i, 0)), blk, blk, blk],
        out_specs=[blk, blk, blk, blk], out_shape=[sd, sd, sd, sd],
        compiler_params=_params(("parallel",)),
    )(own, land, w, m, v)


def _bucket_table():
    t_loc = jnp.arange(SWA_BLOCK)[:, None] + SWA_BLOCK
    s_loc = jnp.arange(2 * SWA_BLOCK)[None, :]
    dist = t_loc - s_loc
    max_exact = REL_BUCKETS // 2
    d = jnp.maximum(dist, 0)
    df = jnp.maximum(d, 1).astype(F32)
    large = max_exact + (jnp.log(df / max_exact) / math.log(REL_MAX_DIST / max_exact) * (REL_BUCKETS - max_exact)).astype(jnp.int32)
    large = jnp.minimum(large, REL_BUCKETS - 1)
    bucket = jnp.where(d < max_exact, d, large)
    band = (dist >= 0) & (dist < SWA_BLOCK)
    return bucket, band


def _tile2(g):
    return jnp.concatenate([g, g], axis=1) if g.shape[1] == HEAD else g


def _w_in_padded(w_in):
    z = jnp.zeros((w_in.shape[0], PROJ_W - FL0 - 8), w_in.dtype)
    return jnp.concatenate([w_in[:, 2824:5896], w_in[:, 768:1280], w_in[:, 1280:1792], w_in[:, 1792:2304], w_in[:, 2312:2824],
                            w_in[:, 0:512], w_in[:, 512:640], w_in[:, 640:768], w_in[:, 2304:2312], z], axis=1)


def _w_in_unpadded(dwp):
    return jnp.concatenate([dwp[:, QA0:QA0 + 512], dwp[:, KA0:KA0 + 128], dwp[:, VA0:VA0 + 128], dwp[:, QF0:QF0 + 512],
                            dwp[:, KF0:KF0 + 512], dwp[:, VF0:VF0 + 512], dwp[:, FL0:FL0 + 8], dwp[:, QM0:QM0 + 512],
                            dwp[:, GL0:GL0 + 3072]], axis=1)


def _tie(x, *tokens):
    for t in tokens:
        if t is not None:
            x = x + t[0:1, 0:1]
    return x


def _local_step(x, mem, target, p, getw, emit, deps=()):
    s = x.shape[0]
    bucket, band = _bucket_table()
    bucket_m = jnp.where(band, bucket, -1).astype(jnp.int32)
    bias = _bias_table(p["rel_bias"], bucket_m)
    gqf, gkf, gqa, gka = _tile2(p["qn_fox"]), _tile2(p["kn_fox"]), _tile2(p["qn_swa"]), _tile2(p["kn_swa"])
    gqm = p["qn_mem"]
    bf128 = jnp.pad(p["b_forget"], ((0, 0), (0, 120)))
    sink = p["sink_swa"].reshape(8)

    h = _rms_fwd(x, p["g_mix"], "rms_mix", deps)
    w_in = getw("w_in", h)
    proj = _mm(h, w_in, "nn", F32, 512, 768, 1024, "proj")
    qf, kf, vf, qm, qa, ka, va = _proj_post(proj, gqf, gkf, gqm, gqa, gka)
    cc4 = _fox_gate_fwd(proj, bf128)
    w_kv = getw("w_mem_kv", cc4)
    mem_n, kv_raw, mk, mv = _memkv_fwd(mem, p["g_mem"], w_kv, p["kn_mem"])
    kp = jnp.pad(ka, ((SWA_BLOCK, 0), (0, 0)))
    vp = jnp.pad(va, ((SWA_BLOCK, 0), (0, 0)))
    oa = _swa_fwd(qa, kp, vp, bias, sink)
    of, lse4 = _fox_fwd(qf, kf, jnp.transpose(vf), cc4)
    om = _mem_fwd(qm, mk, mv)
    wa, wf, wm, w_out = getw("w_o_swa", oa), getw("w_o_fox", oa), getw("w_o_mem", oa), getw("w_out", oa)
    x1, hm, merged = _merge_fwd(x, oa, of, om, proj, p["b_gate"], wa, wf, wm, w_out, p["g_mlp"])
    w_up = getw("w_mlp_up", of)
    a, u = _mlp_up(hm, w_up)
    w_down = getw("w_mlp_down", hm)
    dy, loss = _mlp_down_loss(u, w_down, x1, target)

    da = _mlp_bwd_act(dy, w_down, a)
    t_down = emit("w_mlp_down", _mm(u, dy, "tn", F32, 1024, 1024, 512, "dw_down"))
    dx1, dg_mlp = _mlp_bwd_x(da, w_up, x1, dy, _tie(p["g_mlp"], t_down))
    t_up = emit("w_mlp_up", _mm(hm, da, "tn", F32, 512, 1024, 512, "dw_up"))
    dproj, doa, dof, dom, dya, dyf, dym, db_gate = _merge_bwd(
        dx1, oa, of, om, proj, _tie(p["b_gate"], t_up), wa, wf, wm, w_out)
    t_o = (emit("w_out", _mm(merged, dx1, "tn", F32, 512, 1024, 512, "dw_out")),
           emit("w_o_swa", _mm(oa, dya, "tn", F32, 512, 1024, 512, "dw_o_swa")),
           emit("w_o_fox", _mm(of, dyf, "tn", F32, 512, 1024, 512, "dw_o_fox")),
           emit("w_o_mem", _mm(om, dym, "tn", F32, 512, 1024, 512, "dw_o_mem")))

    dqm, dmk, dmv = _mem_bwd(qm, mk, mv, dom)
    dw_kv, dkn_mem, dg_mem = _memkv_bwd(dmk, dmv, kv_raw, _tie(p["kn_mem"], *t_o), mem, p["g_mem"], mem_n, w_kv)
    t_kv = emit("w_mem_kv", dw_kv)
    dqa, dkp, dvp, dbias, dsink = _swa_bwd(qa, kp, vp, bias, _tie(p["sink_swa"], t_kv).reshape(8), doa)
    dqf_t, dkf, dvf, dck4, dcq4 = _fox_bwd(qf, kf, vf, dof, of, cc4, lse4)
    dqf = jnp.transpose(dqf_t)

    dcq = jnp.transpose(dcq4[:, 0:2, :], (2, 0, 1)).reshape(s, 8)
    dck = jnp.transpose(dck4[:, :, 0:2], (1, 0, 2)).reshape(s, 8)
    dc = jnp.pad(dcq - dck, ((0, 0), (0, 120)))
    dfl, db_forget = _fox_gate_bwd(dc, proj, bf128)

    dproj, dgn = _proj_pre_bwd(dproj, proj, dqf, dkf, dvf, dqm, dqa, dkp[SWA_BLOCK:], dvp[SWA_BLOCK:], dfl,
                               gqf, gkf, gqm, gqa, gka)
    t_in = emit("w_in", _mm(h, dproj, "tn", F32, 512, 1536, 512, "dw_in"))
    grad_x, dg_mix = _in_bwd_x(dproj, w_in, x, _tie(p["g_mix"], t_in), dx1)
    d_rel = _rel_bias_bwd(dbias, bucket_m)

    fold = lambda r: dgn[r:r + 1, 0:HEAD] + dgn[r:r + 1, HEAD:128]
    small = {
        "g_mix": dg_mix, "b_gate": db_gate, "b_forget": db_forget[:, 0:8],
        "qn_swa": fold(3), "kn_swa": fold(4), "sink_swa": dsink[:, 0:8], "rel_bias": d_rel[:, 0:8],
        "qn_fox": fold(0), "kn_fox": fold(1), "g_mem": dg_mem, "qn_mem": dgn[2:3, :], "kn_mem": dkn_mem,
        "g_mlp": dg_mlp,
    }
    return loss, grad_x, small


SMALL = ("g_mix", "b_gate", "b_forget", "qn_swa", "kn_swa", "sink_swa", "rel_bias", "qn_fox", "kn_fox", "g_mem",
         "qn_mem", "kn_mem", "g_mlp")
BIG = ("w_in", "w_mem_kv", "w_o_swa", "w_o_fox", "w_o_mem", "w_out", "w_mlp_up", "w_mlp_down")
COL_SHARDED = ("w_in", "w_o_swa", "w_o_fox", "w_o_mem", "w_mlp_up")
WEIGHTS = ("g_mix", "w_in", "b_gate", "b_forget", "qn_swa", "kn_swa", "sink_swa", "rel_bias", "qn_fox", "kn_fox", "g_mem",
           "w_mem_kv", "qn_mem", "kn_mem", "w_o_swa", "w_o_fox", "w_o_mem", "w_out", "g_mlp", "w_mlp_up", "w_mlp_down")
SMALL_PAD = 7168


def _gathered_to_full(name, g):
    if name in COL_SHARDED:
        return jnp.transpose(g, (1, 0, 2)).reshape(g.shape[1], N_DEV * g.shape[2])
    return g.reshape(N_DEV * g.shape[1], g.shape[2])


def _full_to_parts(name, full, b):
    if name in COL_SHARDED:
        return jnp.transpose(full.reshape(full.shape[0], N_DEV, b), (1, 0, 2)).astype(BF16)
    return full.reshape(N_DEV, full.shape[0] // N_DEV, full.shape[1]).astype(BF16)


def _pack_small(d):
    flat = jnp.concatenate([d[n].reshape(-1) for n in SMALL])
    return jnp.pad(flat, (0, SMALL_PAD - flat.shape[0])).reshape(8, SMALL_PAD // 8)


def _unpack_small(packed, like):
    flat = packed.reshape(-1)
    out, off = {}, 0
    for n in SMALL:
        size = like[n].size
        out[n] = flat[off:off + size].reshape(like[n].shape)
        off += size
    return out


def _adamw_small(parts, w, m, v):
    def body(p_ref, w_ref, m_ref, v_ref, g_ref, d_ref, m2_ref, v2_ref):
        g = p_ref[0]
        for k in range(1, N_DEV):
            g = g + p_ref[k]
        delta, m2, v2 = _adam_math(w_ref[...], g, m_ref[...], v_ref[...])
        g_ref[...] = g
        d_ref[...] = delta
        m2_ref[...] = m2
        v2_ref[...] = v2

    sd = jax.ShapeDtypeStruct(w.shape, F32)
    return pl.pallas_call(body, name="adamw_small", out_shape=[sd, sd, sd, sd])(parts, w, m, v)


def kernel(x, mem, g_mix, w_in, b_gate, b_forget, qn_swa, kn_swa, sink_swa, rel_bias, qn_fox, kn_fox, g_mem, w_mem_kv, qn_mem, kn_mem, w_o_swa, w_o_fox, w_o_mem, w_out, g_mlp, w_mlp_up, w_mlp_down, loss_target, m_g_mix, m_w_in, m_b_gate, m_b_forget, m_qn_swa, m_kn_swa, m_sink_swa, m_rel_bias, m_qn_fox, m_kn_fox, m_g_mem, m_w_mem_kv, m_qn_mem, m_kn_mem, m_w_o_swa, m_w_o_fox, m_w_o_mem, m_w_out, m_g_mlp, m_w_mlp_up, m_w_mlp_down, v_g_mix, v_w_in, v_b_gate, v_b_forget, v_qn_swa, v_kn_swa, v_sink_swa, v_rel_bias, v_qn_fox, v_kn_fox, v_g_mem, v_w_mem_kv, v_qn_mem, v_kn_mem, v_w_o_swa, v_w_o_fox, v_w_o_mem, v_w_out, v_g_mlp, v_w_mlp_up, v_w_mlp_down):
    wts = dict(g_mix=g_mix, w_in=w_in, b_gate=b_gate, b_forget=b_forget, qn_swa=qn_swa, kn_swa=kn_swa, sink_swa=sink_swa,
               rel_bias=rel_bias, qn_fox=qn_fox, kn_fox=kn_fox, g_mem=g_mem, w_mem_kv=w_mem_kv, qn_mem=qn_mem, kn_mem=kn_mem,
               w_o_swa=w_o_swa, w_o_fox=w_o_fox, w_o_mem=w_o_mem, w_out=w_out, g_mlp=g_mlp, w_mlp_up=w_mlp_up,
               w_mlp_down=w_mlp_down)
    mom = dict(g_mix=m_g_mix, w_in=m_w_in, b_gate=m_b_gate, b_forget=m_b_forget, qn_swa=m_qn_swa, kn_swa=m_kn_swa,
               sink_swa=m_sink_swa, rel_bias=m_rel_bias, qn_fox=m_qn_fox, kn_fox=m_kn_fox, g_mem=m_g_mem, w_mem_kv=m_w_mem_kv,
               qn_mem=m_qn_mem, kn_mem=m_kn_mem, w_o_swa=m_w_o_swa, w_o_fox=m_w_o_fox, w_o_mem=m_w_o_mem, w_out=m_w_out,
               g_mlp=m_g_mlp, w_mlp_up=m_w_mlp_up, w_mlp_down=m_w_mlp_down)
    var = dict(g_mix=v_g_mix, w_in=v_w_in, b_gate=v_b_gate, b_forget=v_b_forget, qn_swa=v_qn_swa, kn_swa=v_kn_swa,
               sink_swa=v_sink_swa, rel_bias=v_rel_bias, qn_fox=v_qn_fox, kn_fox=v_kn_fox, g_mem=v_g_mem, w_mem_kv=v_w_mem_kv,
               qn_mem=v_qn_mem, kn_mem=v_kn_mem, w_o_swa=v_w_o_swa, w_o_fox=v_w_o_fox, w_o_mem=v_w_o_mem, w_out=v_w_out,
               g_mlp=v_g_mlp, w_mlp_up=v_w_mlp_up, w_mlp_down=v_w_mlp_down)

    me = _index(_my_place())
    dev = lax.broadcasted_iota(jnp.int32, (N_DEV, 1, 1), 0)

    shards = {n: wts[n][0].astype(BF16) for n in BIG}
    gathers = {n: _split_start(shards[n], N_DEV, True, "ag_start_" + n) for n in BIG}
    full = {}

    def getw(n, after):
        if n not in full:
            _, land = _split_wait(gathers[n], after, True, "ag_wait_" + n)
            w = _gathered_to_full(n, jnp.where(dev == me, shards[n][None], land))
            full[n] = _w_in_padded(w) if n == "w_in" else w
        return full[n]

    exchanges = {}

    def emit(n, grad):
        grad = _w_in_unpadded(grad) if n == "w_in" else grad
        exchanges[n] = _split_start(_full_to_parts(n, grad, wts[n].shape[2]), N_DEV - 1, False, "rs_start_" + n)
        return exchanges[n][4]

    small_p = {n: wts[n] for n in SMALL}
    loss, grad_x, small_g = _local_step(x[0], mem[0], loss_target[0], small_p, getw, emit,
                                        tuple(gathers[n][4] for n in BIG))

    grads, delta, new_m, new_v = {}, {}, {}, {}
    after = grad_x

    def update(n, after):
        parts, land = _split_wait(exchanges[n], after, False, "rs_wait_" + n)
        own = lax.dynamic_index_in_dim(parts, me, 0, keepdims=False)
        g, d, m2, v2 = _adamw(own, land, wts[n][0], mom[n][0], var[n][0], "adamw_" + n)
        grads[n], delta[n], new_m[n], new_v[n] = g[None], d[None], m2[None], v2[None]
        return d

    for n in exchanges:
        if n != "w_in":
            after = update(n, after)

    gathered = _all_gather(_pack_small(small_g), "ag_small")
    g, d, m2, v2 = _adamw_small(gathered, _pack_small(small_p), _pack_small({n: mom[n] for n in SMALL}),
                                _pack_small({n: var[n] for n in SMALL}))
    for dst, packed in ((grads, g), (delta, d), (new_m, m2), (new_v, v2)):
        dst.update(_unpack_small(packed, small_p))
    update("w_in", after)

    total = lax.psum(loss[0, 0], ("x", "y", "c"))
    return (total, grad_x[None], *[grads[n] for n in WEIGHTS], *[delta[n] for n in WEIGHTS],
            *[new_m[n] for n in WEIGHTS], *[new_v[n] for n in WEIGHTS])
```

```python
import functools
import math

import jax
import jax.numpy as jnp
from jax import lax
from jax.experimental import pallas as pl
from jax.experimental.pallas import tpu as pltpu

F32 = jnp.float32
BF16 = jnp.bfloat16

D_MODEL = 1024
N_MEM = 256
D_FF = 4096
HEAD = 64
SWA_HEADS = 8
SWA_BLOCK = 128
MEM_HEADS = 4
MEM_HEAD = 128
EPS = 1e-6
NEG = -1e30
REL_BUCKETS = 32
REL_MAX_DIST = 128

ADAM_LR = 0.001
ADAM_B1 = 0.9
ADAM_B2 = 0.999
ADAM_EPS = 1e-08
ADAM_WD = 0.01
ADAM_STEP = 10

GL0, QF0, KF0, VF0, QM0, QA0, KA0, VA0, FL0 = 0, 3072, 3584, 4096, 4608, 5120, 5632, 5760, 5888
PROJ_W = 6144
HALF_W = 3072
H_QF, H_KF, H_VF, H_QM, H_QA, H_KA, H_VA, H_FL = 0, 512, 1024, 1536, 2048, 2560, 2688, 2816

VMEM_LIMIT = 56 * 1024 * 1024
N_DEV = 8
MESH = pl.DeviceIdType.MESH

NN = (((1,), (0,)), ((), ()))
NT = (((1,), (1,)), ((), ()))
TN = (((0,), (0,)), ((), ()))


def _dot(a, b, dims=NN):
    return lax.dot_general(a, b, dims, preferred_element_type=F32)


def _params(sem):
    return pltpu.CompilerParams(dimension_semantics=sem, vmem_limit_bytes=VMEM_LIMIT)


def _full(shape):
    nd = len(shape)
    return pl.BlockSpec(shape, lambda *_: (0,) * nd)


def _sigmoid(z):
    return 1.0 / (1.0 + jnp.exp(-z))


def _group_mean(v, hd):
    if hd == 128:
        return jnp.mean(v, axis=-1, keepdims=True)
    lane = lax.broadcasted_iota(jnp.int32, v.shape, 1)
    lo = lane < HEAD
    s_lo = jnp.sum(jnp.where(lo, v, 0.0), axis=-1, keepdims=True)
    s_hi = jnp.sum(jnp.where(lo, 0.0, v), axis=-1, keepdims=True)
    return jnp.where(lo, s_lo, s_hi) * (1.0 / HEAD)


def _mm(a, b, mode, out_dtype, tm, tn, tk, name, column_chunks=False):
    if mode == "nn":
        m, k = a.shape
        n = b.shape[1]
    elif mode == "nt":
        m, k = a.shape
        n = b.shape[0]
    else:
        k, m = a.shape
        n = b.shape[1]
    tm, tn, tk = min(tm, m), min(tn, n), min(tk, k)
    if column_chunks:
        tn = n // N_DEV
    nk = k // tk
    dims = {"nn": NN, "nt": NT, "tn": TN}[mode]
    a_spec = pl.BlockSpec((tk, tm), lambda i, j, kk: (kk, i)) if mode == "tn" else pl.BlockSpec((tm, tk), lambda i, j, kk: (i, kk))
    b_spec = pl.BlockSpec((tn, tk), lambda i, j, kk: (j, kk)) if mode == "nt" else pl.BlockSpec((tk, tn), lambda i, j, kk: (kk, j))

    def body(a_ref, b_ref, o_ref, acc_ref):
        prod = _dot(a_ref[...].astype(BF16), b_ref[...].astype(BF16), dims)
        if column_chunks:
            o_ref = o_ref.at[0]
        if nk == 1:
            o_ref[...] = prod.astype(o_ref.dtype)
        else:
            kk = pl.program_id(2)

            @pl.when(kk == 0)
            def _():
                acc_ref[...] = prod

            @pl.when(kk > 0)
            def _():
                acc_ref[...] += prod

            @pl.when(kk == nk - 1)
            def _():
                o_ref[...] = acc_ref[...].astype(o_ref.dtype)

    return pl.pallas_call(
        body, name=name, grid=(m // tm, n // tn, nk),
        in_specs=[a_spec, b_spec],
        out_specs=(pl.BlockSpec((1, tm, tn), lambda i, j, kk: (j, i, 0)) if column_chunks
                   else pl.BlockSpec((tm, tn), lambda i, j, kk: (i, j))),
        out_shape=jax.ShapeDtypeStruct((N_DEV, m, tn) if column_chunks else (m, n), out_dtype),
        scratch_shapes=[pltpu.VMEM((tm, tn), F32)],
        compiler_params=_params(("parallel", "parallel", "arbitrary")),
    )(a, b)


def _rms_fwd(x, g, name, deps=()):
    s, d = x.shape
    tm = min(512, s)

    def body(x_ref, g_ref, *rest):
        h_ref = rest[len(deps)]
        xv = x_ref[...]
        r = lax.rsqrt(jnp.mean(xv * xv, axis=-1, keepdims=True) + EPS)
        h_ref[...] = (xv * r * g_ref[...]).astype(BF16)

    return pl.pallas_call(
        body, name=name, grid=(s // tm,),
        in_specs=[pl.BlockSpec((tm, d), lambda i: (i, 0)), _full((1, d))] + [pl.BlockSpec(memory_space=pl.ANY)] * len(deps),
        out_specs=pl.BlockSpec((tm, d), lambda i: (i, 0)),
        out_shape=jax.ShapeDtypeStruct((s, d), BF16),
        compiler_params=_params(("parallel",)),
    )(x, g, *deps)


def _proj_post(proj, gq_fox, gk_fox, gq_mem, gq_swa, gk_swa):
    s = proj.shape[0]
    tm = min(256, s)

    def body(p_ref, gqf, gkf, gqm, gqa, gka, qf_ref, kf_ref, vf_ref, qm_ref, qa_ref, ka_ref, va_ref):
        def norm(off, width, hd, g_ref, o_ref):
            for b in range(width // 128):
                v = p_ref[:, off + b * 128: off + (b + 1) * 128]
                r = lax.rsqrt(_group_mean(v * v, hd) + EPS)
                o_ref[:, b * 128:(b + 1) * 128] = (v * r * g_ref[...]).astype(BF16)

        norm(H_QF, 512, HEAD, gqf, qf_ref)
        norm(H_KF, 512, HEAD, gkf, kf_ref)
        vf_ref[...] = p_ref[:, H_VF:H_VF + 512].astype(BF16)
        norm(H_QM, 512, MEM_HEAD, gqm, qm_ref)
        norm(H_QA, 512, HEAD, gqa, qa_ref)
        norm(H_KA, 128, HEAD, gka, ka_ref)
        va_ref[...] = p_ref[:, H_VA:H_VA + 128].astype(BF16)

    g_spec = _full((1, 128))
    o512 = pl.BlockSpec((tm, 512), lambda i: (i, 0))
    o128 = pl.BlockSpec((tm, 128), lambda i: (i, 0))
    s512 = jax.ShapeDtypeStruct((s, 512), BF16)
    s128 = jax.ShapeDtypeStruct((s, 128), BF16)
    return pl.pallas_call(
        body, name="proj_post", grid=(s // tm,),
        in_specs=[pl.BlockSpec((tm, HALF_W), lambda i: (i, 1)), g_spec, g_spec, g_spec, g_spec, g_spec],
        out_specs=[o512, o512, o512, o512, o512, o128, o128],
        out_shape=[s512, s512, s512, s512, s512, s128, s128],
        compiler_params=_params(("parallel",)),
    )(proj, gq_fox, gk_fox, gq_mem, gq_swa, gk_swa)


def _tri(n, lower):
    r = lax.broadcasted_iota(jnp.int32, (n, n), 0)
    c = lax.broadcasted_iota(jnp.int32, (n, n), 1)
    return jnp.where((c <= r) if lower else (c >= r), 1.0, 0.0).astype(F32)


def _fox_gate_fwd(proj, b_forget128):
    s = proj.shape[0]
    tm = min(512, s)

    def body(p_ref, b_ref, cc_ref, carry_ref):
        i = pl.program_id(0)

        @pl.when(i == 0)
        def _():
            carry_ref[...] = jnp.zeros_like(carry_ref)

        z = p_ref[...] + b_ref[...]
        logf = jnp.minimum(z, 0.0) - jnp.log(1.0 + jnp.exp(-jnp.abs(z)))
        c = jnp.dot(_tri(tm, True), logf, precision=lax.Precision.HIGHEST, preferred_element_type=F32) + carry_ref[...]
        carry_ref[...] = c[tm - 1:tm, :]
        for hp in range(4):
            cc_ref[hp] = c if hp == 0 else pltpu.roll(c, 128 - 2 * hp, 1)

    return pl.pallas_call(
        body, name="fox_gate_fwd", grid=(s // tm,),
        in_specs=[pl.BlockSpec((tm, 128), lambda i: (i, FL0 // 128)), _full((1, 128))],
        out_specs=pl.BlockSpec((4, tm, 128), lambda i: (0, i, 0)),
        out_shape=jax.ShapeDtypeStruct((4, s, 128), F32),
        scratch_shapes=[pltpu.VMEM((1, 128), F32)],
        compiler_params=_params(("arbitrary",)),
    )(proj, b_forget128)


def _memkv_fwd(mem, g_mem, w_kv, kn_mem):
    m = mem.shape[0]

    def body(mem_ref, g_ref, w_ref, kn_ref, memn_ref, kv_ref, mk_ref, mv_ref):
        xv = mem_ref[...]
        r = lax.rsqrt(jnp.mean(xv * xv, axis=-1, keepdims=True) + EPS)
        mn = (xv * r * g_ref[...]).astype(BF16)
        memn_ref[...] = mn
        kv = _dot(mn, w_ref[...])
        kv_ref[...] = kv
        for h in range(MEM_HEADS):
            v = kv[:, h * 128:(h + 1) * 128]
            rr = lax.rsqrt(jnp.mean(v * v, axis=-1, keepdims=True) + EPS)
            mk_ref[:, h * 128:(h + 1) * 128] = (v * rr * kn_ref[...]).astype(BF16)
        mv_ref[...] = kv[:, 512:1024].astype(BF16)

    return pl.pallas_call(
        body, name="memkv_fwd",
        out_shape=[jax.ShapeDtypeStruct((m, D_MODEL), BF16), jax.ShapeDtypeStruct((m, 1024), F32),
                   jax.ShapeDtypeStruct((m, 512), BF16), jax.ShapeDtypeStruct((m, 512), BF16)],
        compiler_params=pltpu.CompilerParams(vmem_limit_bytes=VMEM_LIMIT),
    )(mem, g_mem, w_kv, kn_mem)


def _bias_table(rel_bias, bucket):
    def body(rb_ref, bk_ref, o_ref):
        bk = bk_ref[...]
        for h in range(SWA_HEADS):
            acc = jnp.zeros(bk.shape, F32)
            for b in range(REL_BUCKETS):
                acc = jnp.where(bk == b, rb_ref[b, h], acc)
            o_ref[h] = acc

    return pl.pallas_call(
        body, name="bias_table",
        in_specs=[pl.BlockSpec(memory_space=pltpu.SMEM), pl.BlockSpec(memory_space=pltpu.VMEM)],
        out_shape=jax.ShapeDtypeStruct((SWA_HEADS,) + bucket.shape, F32),
    )(rel_bias, bucket)


def _swa_valid(n):
    row = lax.broadcasted_iota(jnp.int32, (SWA_BLOCK, 2 * SWA_BLOCK), 0)
    col = lax.broadcasted_iota(jnp.int32, (SWA_BLOCK, 2 * SWA_BLOCK), 1)
    dist = row + SWA_BLOCK - col
    return (dist >= 0) & (dist < SWA_BLOCK) & ((col >= SWA_BLOCK) | (n > 0))


def _swa_fwd(qa, kp, vp, bias, sink):
    s = qa.shape[0]
    nb = s // SWA_BLOCK

    def body(sink_ref, q_ref, kp_ref, vp_ref, bias_ref, o_ref):
        n = pl.program_id(0)
        start = pl.multiple_of(n * SWA_BLOCK, SWA_BLOCK)
        k2 = kp_ref[pl.ds(start, 2 * SWA_BLOCK), :]
        v2 = vp_ref[pl.ds(start, 2 * SWA_BLOCK), :]
        valid = _swa_valid(n)
        for h in range(SWA_HEADS):
            kv = h // 4
            qh = q_ref[:, h * HEAD:(h + 1) * HEAD]
            kh = k2[:, kv * HEAD:(kv + 1) * HEAD]
            vh = v2[:, kv * HEAD:(kv + 1) * HEAD]
            sc = _dot(qh, kh, NT) * 0.125 + bias_ref[h]
            sc = jnp.where(valid, sc, NEG)
            sk = sink_ref[h]
            mx = jnp.maximum(jnp.max(sc, axis=-1, keepdims=True), sk)
            p = jnp.exp(sc - mx)
            den = jnp.sum(p, axis=-1, keepdims=True) + jnp.exp(sk - mx)
            p = p / den
            o_ref[:, h * HEAD:(h + 1) * HEAD] = _dot(p.astype(BF16), vh).astype(BF16)

    return pl.pallas_call(
        body, name="swa_fwd", grid=(nb,),
        in_specs=[pl.BlockSpec(memory_space=pltpu.SMEM),
                  pl.BlockSpec((SWA_BLOCK, 512), lambda n: (n, 0)),
                  _full(kp.shape), _full(vp.shape), _full(bias.shape)],
        out_specs=pl.BlockSpec((SWA_BLOCK, 512), lambda n: (n, 0)),
        out_shape=jax.ShapeDtypeStruct((s, 512), BF16),
        compiler_params=_params(("parallel",)),
    )(sink, qa, kp, vp, bias)


def _head_mask(e):
    lane = lax.broadcasted_iota(jnp.int32, (1, 128), 1)
    return (lane >= e * HEAD) & (lane < (e + 1) * HEAD)


FOX_T = 256
FOX_TK = 512


def _head_rows(e):
    row = lax.broadcasted_iota(jnp.int32, (128, 1), 0)
    return (row >= e * HEAD) & (row < (e + 1) * HEAD)


def _fox_fwd(q, k, v_t, cc4):
    s = q.shape[0]
    t = min(FOX_T, s)
    tk = min(FOX_TK, s)
    nq = s // t

    def body(q_ref, k_ref, vt_ref, cc_ref, o_ref, lse_ref):
        i = pl.program_id(1)
        qs = q_ref[...] * jnp.asarray(0.125, BF16)
        qe = [jnp.where(_head_mask(e), qs, jnp.zeros_like(qs)) for e in range(2)]
        n_full = (i * t) // tk
        krow = lax.broadcasted_iota(jnp.int32, (tk, t), 0) + n_full * tk
        qcol = lax.broadcasted_iota(jnp.int32, (tk, t), 1) + i * t

        def step(j, carry, masked):
            ks = pl.ds(pl.multiple_of(j * tk, tk), tk)
            kj = k_ref[ks, :]
            vtj = vt_ref[:, ks]
            out = []
            for e in range(2):
                m, acc = carry[2 * e], carry[2 * e + 1]
                st = _dot(kj, qe[e], NT) - cc_ref[0, ks, e:e + 1]
                if masked:
                    st = jnp.where(krow <= qcol, st, NEG)
                m_new = jnp.maximum(m, jnp.max(st, axis=0, keepdims=True))
                alpha = jnp.exp(m - m_new)
                pt = jnp.exp(st - m_new).astype(BF16)
                vte = jnp.where(_head_rows(e), vtj, jnp.ones_like(vtj))
                out += [m_new, alpha * acc + _dot(vte, pt)]
            return tuple(out)

        init = (jnp.full((1, t), NEG, F32), jnp.zeros((128, t), F32)) * 2
        carry = lax.fori_loop(0, n_full, functools.partial(step, masked=False), init)
        m0, a0, m1, a1 = step(n_full, carry, True)
        l0 = a0[HEAD:HEAD + 1, :]
        l1 = a1[0:1, :]
        o_t = jnp.where(_head_rows(0), a0 / l0, a1 / l1)
        o_ref[...] = o_t.T.astype(BF16)
        r8 = lax.broadcasted_iota(jnp.int32, (8, t), 0)
        lse_ref[0] = jnp.where(r8 == 0, m0 + jnp.log(l0), jnp.where(r8 == 1, m1 + jnp.log(l1), 0.0))

    return pl.pallas_call(
        body, name="fox_fwd", grid=(4, nq),
        in_specs=[pl.BlockSpec((t, 128), lambda hp, i: (i, hp)),
                  pl.BlockSpec((s, 128), lambda hp, i: (0, hp)),
                  pl.BlockSpec((128, s), lambda hp, i: (hp, 0)),
                  pl.BlockSpec((1, s, 128), lambda hp, i: (hp, 0, 0))],
        out_specs=[pl.BlockSpec((t, 128), lambda hp, i: (i, hp)),
                   pl.BlockSpec((1, 8, t), lambda hp, i: (hp, 0, i))],
        out_shape=[jax.ShapeDtypeStruct((s, 512), BF16), jax.ShapeDtypeStruct((4, 8, s), F32)],
        compiler_params=_params(("parallel", "parallel")),
    )(q, k, v_t, cc4)


MEM_SCALE = MEM_HEAD ** -0.5


def _mem_fwd(qm, mk, mv):
    s = qm.shape[0]
    tq = min(512, s)

    def body(q_ref, mk_ref, mv_ref, o_ref):
        for h in range(MEM_HEADS):
            hs = slice(h * 128, (h + 1) * 128)
            sc = _dot(q_ref[:, hs], mk_ref[:, hs], NT) * MEM_SCALE
            mx = jnp.max(sc, axis=-1, keepdims=True)
            p = jnp.exp(sc - mx)
            p = p / jnp.sum(p, axis=-1, keepdims=True)
            o_ref[:, hs] = _dot(p.astype(BF16), mv_ref[:, hs]).astype(BF16)

    return pl.pallas_call(
        body, name="mem_fwd", grid=(s // tq,),
        in_specs=[pl.BlockSpec((tq, 512), lambda i: (i, 0)), _full(mk.shape), _full(mv.shape)],
        out_specs=pl.BlockSpec((tq, 512), lambda i: (i, 0)),
        out_shape=jax.ShapeDtypeStruct((s, 512), BF16),
        compiler_params=_params(("parallel",)),
    )(qm, mk, mv)


def _merge_fwd(x, oa, of, om, proj, b_gate, wa, wf, wm, w_out, g_mlp):
    s = x.shape[0]
    tm = min(256, s)

    def body(x_ref, oa_ref, of_ref, om_ref, gl_ref, bg_ref, wa_ref, wf_ref, wm_ref, wo_ref, g_ref, x1_ref, hm_ref, mg_ref):
        merged = None
        for b, (o_ref, w_ref) in enumerate(((oa_ref, wa_ref), (of_ref, wf_ref), (om_ref, wm_ref))):
            cs = slice(b * D_MODEL, (b + 1) * D_MODEL)
            y = _dot(o_ref[...], w_ref[...])
            t = _sigmoid(gl_ref[:, cs] + bg_ref[:, cs]) * y
            merged = t if merged is None else merged + t
        mb = merged.astype(BF16)
        mg_ref[...] = mb
        x1 = x_ref[...] + _dot(mb, wo_ref[...])
        x1_ref[...] = x1
        r = lax.rsqrt(jnp.mean(x1 * x1, axis=-1, keepdims=True) + EPS)
        hm_ref[...] = (x1 * r * g_ref[...]).astype(BF16)

    row = lambda w: pl.BlockSpec((tm, w), lambda i: (i, 0))
    return pl.pallas_call(
        body, name="merge_fwd", grid=(s // tm,),
        in_specs=[row(D_MODEL), row(512), row(512), row(512), row(HALF_W), _full((1, HALF_W)),
                  _full(wa.shape), _full(wf.shape), _full(wm.shape), _full(w_out.shape), _full((1, D_MODEL))],
        out_specs=[row(D_MODEL), row(D_MODEL), row(D_MODEL)],
        out_shape=[jax.ShapeDtypeStruct((s, D_MODEL), F32), jax.ShapeDtypeStruct((s, D_MODEL), BF16),
                   jax.ShapeDtypeStruct((s, D_MODEL), BF16)],
        compiler_params=_params(("parallel",)),
    )(x, oa, of, om, proj, b_gate, wa, wf, wm, w_out, g_mlp)


def _mlp_up(hm, w_up):
    s = hm.shape[0]
    tm, tn = min(512, s), w_up.shape[2]

    def body(h_ref, w_ref, a_ref, u_ref):
        a = _dot(h_ref[...], w_ref[0])
        a_ref[...] = a
        r = jnp.maximum(a, 0.0)
        u_ref[...] = (r * r).astype(BF16)

    return pl.pallas_call(
        body, name="mlp_up", grid=(s // tm, D_FF // tn),
        in_specs=[pl.BlockSpec((tm, D_MODEL), lambda i, j: (i, 0)), pl.BlockSpec((1, D_MODEL, tn), lambda i, j: (j, 0, 0))],
        out_specs=[pl.BlockSpec((tm, tn), lambda i, j: (i, j)), pl.BlockSpec((tm, tn), lambda i, j: (i, j))],
        out_shape=[jax.ShapeDtypeStruct((s, D_FF), F32), jax.ShapeDtypeStruct((s, D_FF), BF16)],
        compiler_params=_params(("parallel", "parallel")),
    )(hm, w_up)


def _mlp_down_loss(u, w_down, x1, target):
    s = u.shape[0]
    tm = min(256, s)

    def body(u_ref, w_ref, x1_ref, t_ref, dy_ref, loss_ref):
        i = pl.program_id(0)

        @pl.when(i == 0)
        def _():
            loss_ref[...] = jnp.zeros_like(loss_ref)

        y = x1_ref[...] + _dot(u_ref[...], w_ref[...])
        err = y - t_ref[...]
        dy_ref[...] = err * (1.0 / D_MODEL)
        part = jnp.sum(jnp.sum(err * err, axis=-1, keepdims=True) * (1.0 / D_MODEL), axis=0, keepdims=True)
        loss_ref[...] += 0.5 * part

    row = pl.BlockSpec((tm, D_MODEL), lambda i: (i, 0))
    return pl.pallas_call(
        body, name="mlp_down_loss", grid=(s // tm,),
        in_specs=[pl.BlockSpec((tm, D_FF), lambda i: (i, 0)), _full(w_down.shape), row, row],
        out_specs=[row, _full((1, 1))],
        out_shape=[jax.ShapeDtypeStruct((s, D_MODEL), F32), jax.ShapeDtypeStruct((1, 1), F32)],
        compiler_params=_params(("arbitrary",)),
    )(u, w_down, x1, target)


def _mlp_bwd_act(dy, w_down, a):
    s = dy.shape[0]
    tm, tn = min(512, s), 1024

    def body(dy_ref, w_ref, a_ref, da_ref):
        du = _dot(dy_ref[...].astype(BF16), w_ref[...], NT)
        da_ref[...] = (du * (2.0 * jnp.maximum(a_ref[...], 0.0))).astype(BF16)

    return pl.pallas_call(
        body, name="mlp_bwd_act", grid=(s // tm, D_FF // tn),
        in_specs=[pl.BlockSpec((tm, D_MODEL), lambda i, j: (i, 0)), pl.BlockSpec((tn, D_MODEL), lambda i, j: (j, 0)),
                  pl.BlockSpec((tm, tn), lambda i, j: (i, j))],
        out_specs=pl.BlockSpec((tm, tn), lambda i, j: (i, j)),
        out_shape=jax.ShapeDtypeStruct((s, D_FF), BF16),
        compiler_params=_params(("parallel", "parallel")),
    )(dy, w_down, a)


def _rms_bwd(xv, g, dh, skip):
    r = lax.rsqrt(jnp.mean(xv * xv, axis=-1, keepdims=True) + EPS)
    n = xv * r
    dn = dh * g
    dx = skip + r * (dn - n * jnp.mean(dn * n, axis=-1, keepdims=True))
    return dx, jnp.sum(dh * n, axis=0, keepdims=True)


def _mlp_bwd_x(da, w_up, x1, dy, g_mlp):
    s = da.shape[0]
    tm = min(256, s)

    def body(da_ref, w_ref, x1_ref, dy_ref, g_ref, dx1_ref, dg_ref):
        i = pl.program_id(0)

        @pl.when(i == 0)
        def _():
            dg_ref[...] = jnp.zeros_like(dg_ref)

        tn = w_ref.shape[2]
        dhm = _dot(da_ref[:, 0:tn], w_ref[0], NT)
        for j in range(1, N_DEV):
            dhm = dhm + _dot(da_ref[:, j * tn:(j + 1) * tn], w_ref[j], NT)
        dx, dg = _rms_bwd(x1_ref[...], g_ref[...], dhm, dy_ref[...])
        dx1_ref[...] = dx
        dg_ref[...] += dg

    row = pl.BlockSpec((tm, D_MODEL), lambda i: (i, 0))
    return pl.pallas_call(
        body, name="mlp_bwd_x", grid=(s // tm,),
        in_specs=[pl.BlockSpec((tm, D_FF), lambda i: (i, 0)), _full(w_up.shape), row, row, _full((1, D_MODEL))],
        out_specs=[row, _full((1, D_MODEL))],
        out_shape=[jax.ShapeDtypeStruct((s, D_MODEL), F32), jax.ShapeDtypeStruct((1, D_MODEL), F32)],
        compiler_params=_params(("arbitrary",)),
    )(da, w_up, x1, dy, g_mlp)


def _merge_bwd(dx1, oa, of, om, proj, b_gate, wa, wf, wm, w_out):
    s = dx1.shape[0]
    tm = min(256, s)

    def body(dx1_ref, oa_ref, of_ref, om_ref, gl_ref, bg_ref, wa_ref, wf_ref, wm_ref, wo_ref,
             dp_ref, doa_ref, dof_ref, dom_ref, dya_ref, dyf_ref, dym_ref, dbg_ref):
        i = pl.program_id(0)

        @pl.when(i == 0)
        def _():
            dbg_ref[...] = jnp.zeros_like(dbg_ref)

        dmerged = _dot(dx1_ref[...].astype(BF16), wo_ref[...], NT)
        branches = ((oa_ref, wa_ref, doa_ref, dya_ref), (of_ref, wf_ref, dof_ref, dyf_ref), (om_ref, wm_ref, dom_ref, dym_ref))
        for b, (o_ref, w_ref, do_ref, dyb_ref) in enumerate(branches):
            cs = slice(b * D_MODEL, (b + 1) * D_MODEL)
            y = _dot(o_ref[...], w_ref[...])
            g = _sigmoid(gl_ref[:, cs] + bg_ref[:, cs])
            dz = (dmerged * y) * g * (1.0 - g)
            dp_ref[:, cs] = dz.astype(BF16)
            dbg_ref[:, cs] += jnp.sum(dz, axis=0, keepdims=True)
            dyb = (dmerged * g).astype(BF16)
            dyb_ref[...] = dyb
            do_ref[...] = _dot(dyb, w_ref[...], NT).astype(BF16)

    row = lambda w: pl.BlockSpec((tm, w), lambda i: (i, 0))
    sd = lambda w: jax.ShapeDtypeStruct((s, w), BF16)
    return pl.pallas_call(
        body, name="merge_bwd", grid=(s // tm,),
        in_specs=[row(D_MODEL), row(512), row(512), row(512), row(HALF_W), _full((1, HALF_W)),
                  _full(wa.shape), _full(wf.shape), _full(wm.shape), _full(w_out.shape)],
        out_specs=[row(HALF_W), row(512), row(512), row(512), row(D_MODEL), row(D_MODEL), row(D_MODEL), _full((1, HALF_W))],
        out_shape=[sd(PROJ_W), sd(512), sd(512), sd(512), sd(D_MODEL), sd(D_MODEL), sd(D_MODEL),
                   jax.ShapeDtypeStruct((1, HALF_W), F32)],
        compiler_params=_params(("arbitrary",)),
    )(dx1, oa, of, om, proj, b_gate, wa, wf, wm, w_out)


def _swa_bwd(qa, kp, vp, bias, sink, doa):
    s = qa.shape[0]
    nb = s // SWA_BLOCK

    def body(sink_ref, q_ref, kp_ref, vp_ref, bias_ref, do_ref, dq_ref, dkp_ref, dvp_ref, dbias_ref, dsink_ref, sk_acc):
        n = pl.program_id(0)

        @pl.when(n == 0)
        def _():
            dkp_ref[...] = jnp.zeros_like(dkp_ref)
            dvp_ref[...] = jnp.zeros_like(dvp_ref)
            dbias_ref[...] = jnp.zeros_like(dbias_ref)
            sk_acc[...] = jnp.zeros_like(sk_acc)

        start = pl.multiple_of(n * SWA_BLOCK, SWA_BLOCK)
        win = pl.ds(start, 2 * SWA_BLOCK)
        k2 = kp_ref[win, :]
        v2 = vp_ref[win, :]
        valid = _swa_valid(n)
        for kv in range(2):
            hs_kv = slice(kv * HEAD, (kv + 1) * HEAD)
            kh = k2[:, hs_kv]
            vh = v2[:, hs_kv]
            dk2 = jnp.zeros((2 * SWA_BLOCK, HEAD), F32)
            dv2 = jnp.zeros((2 * SWA_BLOCK, HEAD), F32)
            for g in range(4):
                h = kv * 4 + g
                hs = slice(h * HEAD, (h + 1) * HEAD)
                qh = q_ref[:, hs]
                doh = do_ref[:, hs]
                sc = _dot(qh, kh, NT) * 0.125 + bias_ref[h]
                sc = jnp.where(valid, sc, NEG)
                sk = sink_ref[h]
                mx = jnp.maximum(jnp.max(sc, axis=-1, keepdims=True), sk)
                p = jnp.exp(sc - mx)
                esk = jnp.exp(sk - mx)
                den = jnp.sum(p, axis=-1, keepdims=True) + esk
                p = p / den
                dp = _dot(doh, vh, NT)
                delta = jnp.sum(p * dp, axis=-1, keepdims=True)
                ds = p * (dp - delta)
                sk_acc[:, h:h + 1] += -(esk / den) * delta
                dbias_ref[h] += ds
                dsb = (ds * 0.125).astype(BF16)
                dq_ref[:, hs] = _dot(dsb, kh)
                dk2 = dk2 + _dot(dsb, qh, TN)
                dv2 = dv2 + _dot(p.astype(BF16), doh, TN)
            dkp_ref[win, hs_kv] += dk2
            dvp_ref[win, hs_kv] += dv2

        @pl.when(n == nb - 1)
        def _():
            dsink_ref[...] = jnp.sum(sk_acc[...], axis=0, keepdims=True)

    return pl.pallas_call(
        body, name="swa_bwd", grid=(nb,),
        in_specs=[pl.BlockSpec(memory_space=pltpu.SMEM),
                  pl.BlockSpec((SWA_BLOCK, 512), lambda n: (n, 0)),
                  _full(kp.shape), _full(vp.shape), _full(bias.shape),
                  pl.BlockSpec((SWA_BLOCK, 512), lambda n: (n, 0))],
        out_specs=[pl.BlockSpec((SWA_BLOCK, 512), lambda n: (n, 0)), _full(kp.shape), _full(vp.shape),
                   _full(bias.shape), _full((1, 128))],
        out_shape=[jax.ShapeDtypeStruct((s, 512), F32), jax.ShapeDtypeStruct(kp.shape, F32),
                   jax.ShapeDtypeStruct(vp.shape, F32), jax.ShapeDtypeStruct(bias.shape, F32),
                   jax.ShapeDtypeStruct((1, 128), F32)],
        scratch_shapes=[pltpu.VMEM((SWA_BLOCK, 128), F32)],
        compiler_params=_params(("arbitrary",)),
    )(sink, qa, kp, vp, bias, doa)


def _fox_bwd(q, k, v, do, o, cc4, lse4):
    s = q.shape[0]
    t = min(FOX_T, s)
    tq = min(FOX_TK, s)
    nq = s // t
    nqt = s // tq

    def body(q_ref, k_ref, v_ref, do_ref, o_ref, cc_ref, lse_ref,
             dqt_ref, dk_ref, dv_ref, dck_ref, dcq_ref, delta_ref, dk0, dk1, dv0, dv1, ds0, ds1):
        j = pl.program_id(1)

        @pl.when(j == 0)
        def _():
            dqt_ref[...] = jnp.zeros_like(dqt_ref)
            dcq_ref[...] = jnp.zeros_like(dcq_ref)
            lane8 = lax.broadcasted_iota(jnp.int32, (8, 128), 1)
            row8 = lax.broadcasted_iota(jnp.int32, (8, 128), 0)
            sel = jnp.where((lane8 // HEAD) == row8, 1.0, 0.0).astype(F32)

            def dl(i, c):
                rows = pl.ds(pl.multiple_of(i * tq, tq), tq)
                pr = do_ref[rows, :].astype(F32) * o_ref[rows, :].astype(F32)
                delta_ref[:, rows] = lax.dot_general(sel, pr, NT, precision=lax.Precision.HIGHEST,
                                                     preferred_element_type=F32)
                return c

            lax.fori_loop(0, nqt, dl, 0)

        kj = k_ref[...]
        vj = v_ref[...]
        ks = pl.ds(pl.multiple_of(j * t, t), t)
        kt = (kj.astype(F32) * 0.125).T.astype(BF16)
        ke = [jnp.where(_head_mask(e), kj, jnp.zeros_like(kj)) for e in range(2)]
        ve = [jnp.where(_head_mask(e), vj, jnp.zeros_like(vj)) for e in range(2)]
        kte = [jnp.where(_head_rows(e), kt, jnp.zeros_like(kt)) for e in range(2)]
        ck = [cc_ref[0, ks, e:e + 1] for e in range(2)]
        accs = ((dk0, dv0, ds0), (dk1, dv1, ds1))
        for refs in accs:
            for r in refs:
                r[...] = jnp.zeros_like(r)
        i_first = (j * t) // tq
        krow = lax.broadcasted_iota(jnp.int32, (t, tq), 0) + j * t
        qcol = lax.broadcasted_iota(jnp.int32, (t, tq), 1) + i_first * tq

        def step(i, c, masked):
            rows = pl.ds(pl.multiple_of(i * tq, tq), tq)
            qs = q_ref[rows, :] * jnp.asarray(0.125, BF16)
            doi = do_ref[rows, :]
            for e in range(2):
                dk_acc, dv_acc, ds_acc = accs[e]
                st = _dot(ke[e], qs, NT) - ck[e]
                if masked:
                    st = jnp.where(krow <= qcol, st, NEG)
                pt = jnp.exp(st - lse_ref[0, e:e + 1, rows])
                dpt = _dot(ve[e], doi, NT)
                dst = pt * (dpt - delta_ref[e:e + 1, rows])
                dsb = dst.astype(BF16)
                dv_acc[...] += _dot(pt.astype(BF16), doi)
                dk_acc[...] += _dot(dsb, qs)
                dqt_ref[:, rows] += _dot(kte[e], dsb)
                ds_acc[...] += dst
                dcq_ref[0, e:e + 1, rows] += jnp.sum(dst, axis=0, keepdims=True)
            return c

        step(i_first, 0, True)
        lax.fori_loop(i_first + 1, nqt, functools.partial(step, masked=False), 0)
        m0 = _head_mask(0)
        dk_ref[...] = jnp.where(m0, dk0[...], dk1[...])
        dv_ref[...] = jnp.where(m0, dv0[...], dv1[...])
        lane = lax.broadcasted_iota(jnp.int32, (t, 128), 1)
        c0 = jnp.sum(ds0[...], axis=-1, keepdims=True)
        c1 = jnp.sum(ds1[...], axis=-1, keepdims=True)
        dck_ref[0] = jnp.where(lane == 0, c0, jnp.where(lane == 1, c1, 0.0))

    res = lambda: pl.BlockSpec((s, 128), lambda hp, j: (0, hp))
    blk = lambda: pl.BlockSpec((t, 128), lambda hp, j: (j, hp))
    return pl.pallas_call(
        body, name="fox_bwd", grid=(4, nq),
        in_specs=[res(), blk(), blk(), res(), res(), pl.BlockSpec((1, s, 128), lambda hp, j: (hp, 0, 0)),
                  pl.BlockSpec((1, 8, s), lambda hp, j: (hp, 0, 0))],
        out_specs=[pl.BlockSpec((128, s), lambda hp, j: (hp, 0)), blk(), blk(),
                   pl.BlockSpec((1, t, 128), lambda hp, j: (hp, j, 0)),
                   pl.BlockSpec((1, 8, s), lambda hp, j: (hp, 0, 0))],
        out_shape=[jax.ShapeDtypeStruct((512, s), F32), jax.ShapeDtypeStruct((s, 512), F32),
                   jax.ShapeDtypeStruct((s, 512), F32), jax.ShapeDtypeStruct((4, s, 128), F32),
                   jax.ShapeDtypeStruct((4, 8, s), F32)],
        scratch_shapes=[pltpu.VMEM((8, s), F32)] + [pltpu.VMEM((t, 128), F32)] * 4 + [pltpu.VMEM((t, tq), F32)] * 2,
        compiler_params=_params(("arbitrary", "arbitrary")),
    )(q, k, v, do, o, cc4, lse4)


def _mem_bwd(qm, mk, mv, dom):
    s = qm.shape[0]
    tq = min(512, s)

    def body(q_ref, mk_ref, mv_ref, do_ref, dq_ref, dmk_ref, dmv_ref):
        i = pl.program_id(0)

        @pl.when(i == 0)
        def _():
            dmk_ref[...] = jnp.zeros_like(dmk_ref)
            dmv_ref[...] = jnp.zeros_like(dmv_ref)

        for h in range(MEM_HEADS):
            hs = slice(h * 128, (h + 1) * 128)
            qh = q_ref[:, hs]
            doh = do_ref[:, hs]
            sc = _dot(qh, mk_ref[:, hs], NT) * MEM_SCALE
            mx = jnp.max(sc, axis=-1, keepdims=True)
            p = jnp.exp(sc - mx)
            p = p / jnp.sum(p, axis=-1, keepdims=True)
            dp = _dot(doh, mv_ref[:, hs], NT)
            ds = p * (dp - jnp.sum(p * dp, axis=-1, keepdims=True))
            dsb = (ds * MEM_SCALE).astype(BF16)
            dq_ref[:, hs] = _dot(dsb, mk_ref[:, hs])
            dmk_ref[:, hs] += _dot(dsb, qh, TN)
            dmv_ref[:, hs] += _dot(p.astype(BF16), doh, TN)

    return pl.pallas_call(
        body, name="mem_bwd", grid=(s // tq,),
        in_specs=[pl.BlockSpec((tq, 512), lambda i: (i, 0)), _full(mk.shape), _full(mv.shape),
                  pl.BlockSpec((tq, 512), lambda i: (i, 0))],
        out_specs=[pl.BlockSpec((tq, 512), lambda i: (i, 0)), _full(mk.shape), _full(mv.shape)],
        out_shape=[jax.ShapeDtypeStruct((s, 512), F32), jax.ShapeDtypeStruct(mk.shape, F32),
                   jax.ShapeDtypeStruct(mv.shape, F32)],
        compiler_params=_params(("arbitrary",)),
    )(qm, mk, mv, dom)


def _memkv_bwd(dmk, dmv, kv_raw, kn_mem, mem, g_mem, mem_n, w_kv):
    def body(dmk_ref, dmv_ref, kv_ref, kn_ref, mem_ref, g_ref, mn_ref, w_ref, dw_ref, dkn_ref, dg_ref, dkv_ref):
        dkn = jnp.zeros((1, 128), F32)
        for h in range(MEM_HEADS):
            hs = slice(h * 128, (h + 1) * 128)
            v = kv_ref[:, hs]
            r = lax.rsqrt(jnp.mean(v * v, axis=-1, keepdims=True) + EPS)
            n = v * r
            dn = dmk_ref[:, hs]
            dkn = dkn + jnp.sum(dn * n, axis=0, keepdims=True)
            dng = dn * kn_ref[...]
            dkv_ref[:, hs] = (r * (dng - n * jnp.mean(dng * n, axis=-1, keepdims=True))).astype(BF16)
        dkv_ref[:, 512:1024] = dmv_ref[...].astype(BF16)
        dkn_ref[...] = dkn
        dkv = dkv_ref[...]
        dw_ref[...] = _dot(mn_ref[...], dkv, TN).astype(BF16)
        dmn = _dot(dkv, w_ref[...], NT)
        xv = mem_ref[...]
        r = lax.rsqrt(jnp.mean(xv * xv, axis=-1, keepdims=True) + EPS)
        dg_ref[...] = jnp.sum(dmn * (xv * r), axis=0, keepdims=True)

    m = mem.shape[0]
    return pl.pallas_call(
        body, name="memkv_bwd",
        out_shape=[jax.ShapeDtypeStruct((D_MODEL, 1024), BF16), jax.ShapeDtypeStruct((1, 128), F32),
                   jax.ShapeDtypeStruct((1, D_MODEL), F32)],
        scratch_shapes=[pltpu.VMEM((m, 1024), BF16)],
        compiler_params=pltpu.CompilerParams(vmem_limit_bytes=VMEM_LIMIT),
    )(dmk, dmv, kv_raw, kn_mem, mem, g_mem, mem_n, w_kv)


def _fox_gate_bwd(dc, proj, b_forget128):
    s = dc.shape[0]
    tm = min(512, s)
    nt = s // tm

    def body(dc_ref, p_ref, b_ref, dfl_ref, db_ref, carry_ref):
        i = pl.program_id(0)

        @pl.when(i == 0)
        def _():
            carry_ref[...] = jnp.zeros_like(carry_ref)
            db_ref[...] = jnp.zeros_like(db_ref)

        dcv = dc_ref[...]
        dlogf = jnp.dot(_tri(tm, False), dcv, precision=lax.Precision.HIGHEST, preferred_element_type=F32) + carry_ref[...]
        carry_ref[...] += jnp.sum(dcv, axis=0, keepdims=True)
        z = p_ref[...] + b_ref[...]
        dfl = dlogf * (1.0 / (1.0 + jnp.exp(z)))
        dfl_ref[...] = dfl.astype(BF16)
        db_ref[...] += jnp.sum(dfl, axis=0, keepdims=True)

    return pl.pallas_call(
        body, name="fox_gate_bwd", grid=(nt,),
        in_specs=[pl.BlockSpec((tm, 128), lambda i: (nt - 1 - i, 0)),
                  pl.BlockSpec((tm, 128), lambda i: (nt - 1 - i, FL0 // 128)), _full((1, 128))],
        out_specs=[pl.BlockSpec((tm, 128), lambda i: (nt - 1 - i, 0)), _full((1, 128))],
        out_shape=[jax.ShapeDtypeStruct((s, 128), BF16), jax.ShapeDtypeStruct((1, 128), F32)],
        scratch_shapes=[pltpu.VMEM((1, 128), F32)],
        compiler_params=_params(("arbitrary",)),
    )(dc, proj, b_forget128)


def _proj_pre_bwd(dproj, proj, dqf, dkf, dvf, dqm, dqa, dka, dva, dfl, gq_fox, gk_fox, gq_mem, gq_swa, gk_swa):
    s = proj.shape[0]
    tm = min(256, s)

    def body(dp_in, p_ref, dqf_ref, dkf_ref, dvf_ref, dqm_ref, dqa_ref, dka_ref, dva_ref, dfl_ref,
             gqf, gkf, gqm, gqa, gka, dp_ref, dgn_ref):
        i = pl.program_id(0)

        @pl.when(i == 0)
        def _():
            dgn_ref[...] = jnp.zeros_like(dgn_ref)

        def norm_bwd(off, width, hd, g_ref, dn_ref, slot):
            acc = jnp.zeros((1, 128), F32)
            for b in range(width // 128):
                v = p_ref[:, off + b * 128: off + (b + 1) * 128]
                r = lax.rsqrt(_group_mean(v * v, hd) + EPS)
                n = v * r
                dn = dn_ref[:, b * 128:(b + 1) * 128]
                acc = acc + jnp.sum(dn * n, axis=0, keepdims=True)
                dng = dn * g_ref[...]
                dp_ref[:, off + b * 128: off + (b + 1) * 128] = (r * (dng - n * _group_mean(dng * n, hd))).astype(BF16)
            dgn_ref[slot:slot + 1, :] += acc

        norm_bwd(H_QF, 512, HEAD, gqf, dqf_ref, 0)
        norm_bwd(H_KF, 512, HEAD, gkf, dkf_ref, 1)
        dp_ref[:, H_VF:H_VF + 512] = dvf_ref[...].astype(BF16)
        norm_bwd(H_QM, 512, MEM_HEAD, gqm, dqm_ref, 2)
        norm_bwd(H_QA, 512, HEAD, gqa, dqa_ref, 3)
        norm_bwd(H_KA, 128, HEAD, gka, dka_ref, 4)
        dp_ref[:, H_VA:H_VA + 128] = dva_ref[...].astype(BF16)
        dp_ref[:, H_FL:H_FL + 128] = dfl_ref[...]
        dp_ref[:, H_FL + 128:HALF_W] = jnp.zeros((tm, HALF_W - H_FL - 128), BF16)

    row = lambda w: pl.BlockSpec((tm, w), lambda i: (i, 0))
    g_spec = _full((1, 128))
    return pl.pallas_call(
        body, name="proj_pre_bwd", grid=(s // tm,),
        in_specs=[pl.BlockSpec(memory_space=pl.ANY), pl.BlockSpec((tm, HALF_W), lambda i: (i, 1)),
                  row(512), row(512), row(512), row(512), row(512), row(128), row(128), row(128),
                  g_spec, g_spec, g_spec, g_spec, g_spec],
        out_specs=[pl.BlockSpec((tm, HALF_W), lambda i: (i, 1)), _full((8, 128))],
        out_shape=[jax.ShapeDtypeStruct((s, PROJ_W), BF16), jax.ShapeDtypeStruct((8, 128), F32)],
        input_output_aliases={0: 0},
        compiler_params=_params(("arbitrary",)),
    )(dproj, proj, dqf, dkf, dvf, dqm, dqa, dka, dva, dfl, gq_fox, gk_fox, gq_mem, gq_swa, gk_swa)


def _in_bwd_x(dproj, w_in_p, x, g_mix, dx1):
    s = x.shape[0]
    tm, tk = min(512, s), 1536
    nk = PROJ_W // tk

    def body(dp_ref, w_ref, x_ref, g_ref, dx1_ref, gx_ref, dg_ref, acc_ref):
        i, kk = pl.program_id(0), pl.program_id(1)

        @pl.when((i == 0) & (kk == 0))
        def _():
            dg_ref[...] = jnp.zeros_like(dg_ref)

        prod = _dot(dp_ref[...], w_ref[...], NT)

        @pl.when(kk == 0)
        def _():
            acc_ref[...] = prod

        @pl.when(kk > 0)
        def _():
            acc_ref[...] += prod

        @pl.when(kk == nk - 1)
        def _():
            dx, dg = _rms_bwd(x_ref[...], g_ref[...], acc_ref[...], dx1_ref[...])
            gx_ref[...] = dx
            dg_ref[...] += dg

    row = pl.BlockSpec((tm, D_MODEL), lambda i, kk: (i, 0))
    return pl.pallas_call(
        body, name="in_bwd_x", grid=(s // tm, nk),
        in_specs=[pl.BlockSpec((tm, tk), lambda i, kk: (i, kk)), pl.BlockSpec((D_MODEL, tk), lambda i, kk: (0, kk)),
                  row, _full((1, D_MODEL)), row],
        out_specs=[row, _full((1, D_MODEL))],
        out_shape=[jax.ShapeDtypeStruct((s, D_MODEL), F32), jax.ShapeDtypeStruct((1, D_MODEL), F32)],
        scratch_shapes=[pltpu.VMEM((tm, D_MODEL), F32)],
        compiler_params=_params(("arbitrary", "arbitrary")),
    )(dproj, w_in_p, x, g_mix, dx1)


def _rel_bias_bwd(dbias, bucket):
    def body(db_ref, bk_ref, o_ref):
        bk = bk_ref[...]
        lane = lax.broadcasted_iota(jnp.int32, (1, 128), 1)
        for b in range(REL_BUCKETS):
            sel = bk == b
            acc = jnp.zeros((1, 128), F32)
            for h in range(SWA_HEADS):
                tot = jnp.sum(jnp.sum(jnp.where(sel, db_ref[h], 0.0), axis=-1, keepdims=True), axis=0, keepdims=True)
                acc = jnp.where(lane == h, tot, acc)
            o_ref[b:b + 1, :] = acc

    return pl.pallas_call(
        body, name="rel_bias_bwd",
        out_shape=jax.ShapeDtypeStruct((REL_BUCKETS, 128), F32),
        compiler_params=pltpu.CompilerParams(vmem_limit_bytes=VMEM_LIMIT),
    )(dbias, bucket)


def _my_place():
    return lax.axis_index("x"), lax.axis_index("y"), lax.axis_index("c")


def _peer(place, k):
    x, y, c = place
    return (1 - x if k & 4 else x, 1 - y if k & 2 else y, 1 - c if k & 1 else c)


def _index(place):
    x, y, c = place
    return 4 * x + 2 * y + c


def _all_gather(shard, name):
    def body(x_ref, out_ref, send_sems, recv_sems, local_sem):
        me = _my_place()
        mine = pltpu.make_async_copy(x_ref, out_ref.at[_index(me)], local_sem)
        mine.start()
        sends = []
        for k in range(1, N_DEV):
            cp = pltpu.make_async_remote_copy(
                src_ref=x_ref, dst_ref=out_ref.at[_index(me)], send_sem=send_sems.at[k - 1], recv_sem=recv_sems.at[k - 1],
                device_id=_peer(me, k), device_id_type=MESH)
            cp.start()
            sends.append(cp)
        for k in range(1, N_DEV):
            peer = _peer(me, k)
            pltpu.make_async_remote_copy(
                src_ref=x_ref, dst_ref=out_ref.at[_index(peer)], send_sem=send_sems.at[k - 1], recv_sem=recv_sems.at[k - 1],
                device_id=peer, device_id_type=MESH).wait_recv()
        for cp in sends:
            cp.wait_send()
        mine.wait()

    any_spec = pl.BlockSpec(memory_space=pl.ANY)
    return pl.pallas_call(
        body, name=name, in_specs=[any_spec], out_specs=any_spec,
        out_shape=jax.ShapeDtypeStruct((N_DEV,) + shard.shape, shard.dtype),
        scratch_shapes=[pltpu.SemaphoreType.DMA((N_DEV - 1,)), pltpu.SemaphoreType.DMA((N_DEV - 1,)), pltpu.SemaphoreType.DMA(())],
    )(shard)


HBM_SPEC = pl.BlockSpec(memory_space=pltpu.HBM)
SEM_SPEC = pl.BlockSpec(memory_space=pltpu.SEMAPHORE)
DATAFLOW = pltpu.SideEffectType.DATAFLOW_SIDE_EFFECTING
PEER_SEMS = pltpu.SemaphoreType.DMA((N_DEV - 1,))


def _split_copy(src_ref, land_ref, send_sems, recv_sems, me, k, gather):
    peer = _peer(me, k)
    if gather:
        src, dst = src_ref, land_ref.at[_index(me)]
    else:
        src, dst = src_ref.at[_index(peer)], land_ref.at[k - 1]
    return pltpu.make_async_remote_copy(src_ref=src, dst_ref=dst, send_sem=send_sems.at[k - 1], recv_sem=recv_sems.at[k - 1],
                                        device_id=peer, device_id_type=MESH)


def _split_start(src, slots, gather, name):
    def body(src_ref, land_ref, send_sems, recv_sems, src_thru, land_thru, token):
        me = _my_place()
        for k in range(1, N_DEV):
            _split_copy(src_ref, land_ref, send_sems, recv_sems, me, k, gather).start()
        token[...] = jnp.zeros_like(token)

    chunk = src.shape if gather else src.shape[1:]
    land = lax.empty((slots,) + chunk, src.dtype)
    return pl.pallas_call(
        body, name=name,
        out_shape=(PEER_SEMS, PEER_SEMS, pltpu.HBM(src.shape, src.dtype), pltpu.HBM(land.shape, land.dtype),
                   jax.ShapeDtypeStruct((8, 128), F32)),
        in_specs=(HBM_SPEC, HBM_SPEC),
        out_specs=(SEM_SPEC, SEM_SPEC, HBM_SPEC, HBM_SPEC, pl.BlockSpec(memory_space=pltpu.VMEM)),
        input_output_aliases={0: 2, 1: 3},
        compiler_params=pltpu.CompilerParams(has_side_effects=DATAFLOW),
    )(pltpu.with_memory_space_constraint(src, pltpu.HBM), pltpu.with_memory_space_constraint(land, pltpu.HBM))


def _split_wait(started, after, gather, name):
    send_sems, recv_sems, src_thru, land_thru, _ = started

    def body(src_ref, land_ref, send_sems, recv_sems, after_ref, src_out, land_out):
        me = _my_place()
        for k in range(1, N_DEV):
            cp = _split_copy(src_ref, land_ref, send_sems, recv_sems, me, k, gather)
            cp.wait_send()
            cp.wait_recv()

    return pl.pallas_call(
        body, name=name,
        out_shape=(pltpu.HBM(src_thru.shape, src_thru.dtype), pltpu.HBM(land_thru.shape, land_thru.dtype)),
        in_specs=(HBM_SPEC, HBM_SPEC, SEM_SPEC, SEM_SPEC, pl.BlockSpec(memory_space=pl.ANY)),
        out_specs=(HBM_SPEC, HBM_SPEC), input_output_aliases={0: 0, 1: 1},
        compiler_params=pltpu.CompilerParams(has_side_effects=DATAFLOW),
    )(src_thru, land_thru, send_sems, recv_sems, after)


def _adam_math(w, g, m, v):
    m2 = ADAM_B1 * m + (1.0 - ADAM_B1) * g
    v2 = ADAM_B2 * v + (1.0 - ADAM_B2) * (g * g)
    m_hat = m2 / (1.0 - ADAM_B1 ** ADAM_STEP)
    v_hat = v2 / (1.0 - ADAM_B2 ** ADAM_STEP)
    delta = -ADAM_LR * (m_hat / (jnp.sqrt(v_hat) + ADAM_EPS) + ADAM_WD * w)
    return delta, m2, v2


def _adamw(own, land, w, m, v, name):
    a, b = w.shape
    bp = own.shape[1]
    ta = min(128, a)

    def body(o_ref, p_ref, w_ref, m_ref, v_ref, g_ref, d_ref, m2_ref, v2_ref):
        g = o_ref[:, 0:b].astype(F32)
        for k in range(N_DEV - 1):
            g = g + p_ref[k, :, 0:b].astype(F32)
        delta, m2, v2 = _adam_math(w_ref[...], g, m_ref[...], v_ref[...])
        g_ref[...] = g
        d_ref[...] = delta
        m2_ref[...] = m2
        v2_ref[...] = v2

    blk = pl.BlockSpec((ta, b), lambda i: (i, 0))
    sd = jax.ShapeDtypeStruct((a, b), F32)
    return pl.pallas_call(
        body, name=name, grid=(a // ta,),
        in_specs=[pl.BlockSpec((ta, bp), lambda i: (i, 0)), pl.BlockSpec((N_DEV - 1, ta, bp), lambda i: (0, i, 0)), blk, blk, blk],
        out_specs=[blk, blk, blk, blk], out_shape=[sd, sd, sd, sd],
        compiler_params=_params(("parallel",)),
    )(own, land, w, m, v)


def _bucket_table():
    t_loc = jnp.arange(SWA_BLOCK)[:, None] + SWA_BLOCK
    s_loc = jnp.arange(2 * SWA_BLOCK)[None, :]
    dist = t_loc - s_loc
    max_exact = REL_BUCKETS // 2
    d = jnp.maximum(dist, 0)
    df = jnp.maximum(d, 1).astype(F32)
    large = max_exact + (jnp.log(df / max_exact) / math.log(REL_MAX_DIST / max_exact) * (REL_BUCKETS - max_exact)).astype(jnp.int32)
    large = jnp.minimum(large, REL_BUCKETS - 1)
    bucket = jnp.where(d < max_exact, d, large)
    band = (dist >= 0) & (dist < SWA_BLOCK)
    return bucket, band


def _tile2(g):
    return jnp.concatenate([g, g], axis=1) if g.shape[1] == HEAD else g


SHARD_W = 737
SHARD_WP = 768
IN_WIDTH = N_DEV * SHARD_W
SEGMENTS = ((GL0, 2824, 3072), (QF0, 768, 512), (KF0, 1280, 512), (VF0, 1792, 512), (QM0, 2312, 512),
            (QA0, 0, 512), (KA0, 512, 128), (VA0, 640, 128), (FL0, 2304, 8))


def _lane_plan(sources):
    plan = []
    for t in range(len(sources) // 128):
        groups = {}
        for lane in range(128):
            src = sources[128 * t + lane]
            if src is not None:
                slab, col = src
                groups.setdefault((slab, col // 128, (lane - col) % 128), []).append(lane)
        tile = []
        for key, lanes in groups.items():
            assert lanes == list(range(lanes[0], lanes[-1] + 1))
            tile.append((key, lanes[0], lanes[-1] + 1))
        plan.append(tile)
    return plan


def _assemble(tile_plan, load, rows):
    lane = lax.broadcasted_iota(jnp.int32, (1, 128), 1)
    out = jnp.zeros((rows, 128), F32)
    for (slab, st, roll), lo, hi in tile_plan:
        v = load(slab, st)
        if roll:
            v = pltpu.roll(v, roll, 1)
        out = v if (lo, hi) == (0, 128) else jnp.where((lane >= lo) & (lane < hi), v, out)
    return out


def _w_in_from_shards(land, own):
    ref_col = [None] * PROJ_W
    for p0, r0, n in SEGMENTS:
        for i in range(n):
            ref_col[p0 + i] = divmod(r0 + i, SHARD_W)
    plan = _lane_plan(ref_col)
    d_model = own.shape[0]
    tm = 256

    def body(land_ref, own_ref, o_ref):
        me = _index(_my_place())

        def load(slab, st):
            cols = slice(st * 128, (st + 1) * 128)
            return jnp.where(me == slab, own_ref[:, cols], land_ref[slab, :, cols]).astype(F32)

        for t, tile_plan in enumerate(plan):
            o_ref[:, t * 128:(t + 1) * 128] = _assemble(tile_plan, load, tm).astype(BF16)

    return pl.pallas_call(
        body, name="w_in_from_shards", grid=(d_model // tm,),
        in_specs=[pl.BlockSpec((N_DEV, tm, SHARD_WP), lambda i: (0, i, 0)), pl.BlockSpec((tm, SHARD_WP), lambda i: (i, 0))],
        out_specs=pl.BlockSpec((tm, PROJ_W), lambda i: (i, 0)),
        out_shape=jax.ShapeDtypeStruct((d_model, PROJ_W), BF16),
        compiler_params=_params(("parallel",)),
    )(land, own)


def _dw_in_to_parts(dwp):
    padded_col = [None] * IN_WIDTH
    for p0, r0, n in SEGMENTS:
        for i in range(n):
            padded_col[r0 + i] = p0 + i
    sources = []
    for d in range(N_DEV):
        sources += [(0, padded_col[SHARD_W * d + c]) if c < SHARD_W else None for c in range(SHARD_WP)]
    plan = _lane_plan(sources)
    d_model = dwp.shape[0]
    tm = 256
    tiles = SHARD_WP // 128

    def body(dw_ref, o_ref):
        load = lambda slab, st: dw_ref[:, st * 128:(st + 1) * 128]
        for t, tile_plan in enumerate(plan):
            d, c = divmod(t, tiles)
            o_ref[d, :, c * 128:(c + 1) * 128] = _assemble(tile_plan, load, tm).astype(BF16)

    return pl.pallas_call(
        body, name="dw_in_to_parts", grid=(d_model // tm,),
        in_specs=[pl.BlockSpec((tm, PROJ_W), lambda i: (i, 0))],
        out_specs=pl.BlockSpec((N_DEV, tm, SHARD_WP), lambda i: (0, i, 0)),
        out_shape=jax.ShapeDtypeStruct((N_DEV, d_model, SHARD_WP), BF16),
        compiler_params=_params(("parallel",)),
    )(dwp)


def _cast_shards(shards):
    names = list(shards)

    def body(*refs):
        for src, dst in zip(refs[:len(names)], refs[len(names):]):
            if dst.shape != src.shape:
                dst[...] = jnp.zeros(dst.shape, BF16)
                dst[:, 0:src.shape[1]] = src[...].astype(BF16)
            else:
                dst[...] = src[...].astype(BF16)

    out_shape = [jax.ShapeDtypeStruct((shards[n].shape[0], SHARD_WP if n == "w_in" else shards[n].shape[1]), BF16)
                 for n in names]
    outs = pl.pallas_call(body, name="cast_shards", out_shape=out_shape,
                          compiler_params=pltpu.CompilerParams(vmem_limit_bytes=VMEM_LIMIT))(*[shards[n] for n in names])
    return dict(zip(names, outs))


def _tie(x, *tokens):
    for t in tokens:
        if t is not None:
            x = x + t[0:1, 0:1]
    return x


def _local_step(x, mem, target, p, getw, emit, deps=()):
    s = x.shape[0]
    bucket, band = _bucket_table()
    bucket_m = jnp.where(band, bucket, -1).astype(jnp.int32)
    bias = _bias_table(p["rel_bias"], bucket_m)
    gqf, gkf, gqa, gka = _tile2(p["qn_fox"]), _tile2(p["kn_fox"]), _tile2(p["qn_swa"]), _tile2(p["kn_swa"])
    gqm = p["qn_mem"]
    bf128 = jnp.pad(p["b_forget"], ((0, 0), (0, 120)))
    sink = p["sink_swa"].reshape(8)

    h = _rms_fwd(x, p["g_mix"], "rms_mix", deps)
    w_in = getw("w_in", h)
    proj = _mm(h, w_in, "nn", F32, 512, 768, 1024, "proj")
    qf, kf, vf, qm, qa, ka, va = _proj_post(proj, gqf, gkf, gqm, gqa, gka)
    cc4 = _fox_gate_fwd(proj, bf128)
    w_kv = getw("w_mem_kv", cc4)
    mem_n, kv_raw, mk, mv = _memkv_fwd(mem, p["g_mem"], w_kv, p["kn_mem"])
    kp = jnp.pad(ka, ((SWA_BLOCK, 0), (0, 0)))
    vp = jnp.pad(va, ((SWA_BLOCK, 0), (0, 0)))
    oa = _swa_fwd(qa, kp, vp, bias, sink)
    of, lse4 = _fox_fwd(qf, kf, jnp.transpose(vf), cc4)
    om = _mem_fwd(qm, mk, mv)
    wa, wf, wm, w_out = getw("w_o_swa", oa), getw("w_o_fox", oa), getw("w_o_mem", oa), getw("w_out", oa)
    x1, hm, merged = _merge_fwd(x, oa, of, om, proj, p["b_gate"], wa, wf, wm, w_out, p["g_mlp"])
    w_up = getw("w_mlp_up", of)
    a, u = _mlp_up(hm, w_up)
    w_down = getw("w_mlp_down", hm)
    dy, loss = _mlp_down_loss(u, w_down, x1, target)

    da = _mlp_bwd_act(dy, w_down, a)
    t_down = emit("w_mlp_down", _mm(u, dy, "tn", BF16, 1024, 1024, 512, "dw_down"))
    dx1, dg_mlp = _mlp_bwd_x(da, w_up, x1, dy, _tie(p["g_mlp"], t_down))
    t_up = emit("w_mlp_up", _mm(hm, da, "tn", BF16, 512, 1024, 512, "dw_up", column_chunks=True))
    dproj, doa, dof, dom, dya, dyf, dym, db_gate = _merge_bwd(
        dx1, oa, of, om, proj, _tie(p["b_gate"], t_up), wa, wf, wm, w_out)
    t_o = (emit("w_out", _mm(merged, dx1, "tn", BF16, 512, 1024, 512, "dw_out")),
           emit("w_o_swa", _mm(oa, dya, "tn", BF16, 512, 1024, 512, "dw_o_swa")),
           emit("w_o_fox", _mm(of, dyf, "tn", BF16, 512, 1024, 512, "dw_o_fox")),
           emit("w_o_mem", _mm(om, dym, "tn", BF16, 512, 1024, 512, "dw_o_mem")))

    dqm, dmk, dmv = _mem_bwd(qm, mk, mv, dom)
    dw_kv, dkn_mem, dg_mem = _memkv_bwd(dmk, dmv, kv_raw, _tie(p["kn_mem"], *t_o), mem, p["g_mem"], mem_n, w_kv)
    t_kv = emit("w_mem_kv", dw_kv)
    dqa, dkp, dvp, dbias, dsink = _swa_bwd(qa, kp, vp, bias, _tie(p["sink_swa"], t_kv).reshape(8), doa)
    dqf_t, dkf, dvf, dck4, dcq4 = _fox_bwd(qf, kf, vf, dof, of, cc4, lse4)
    dqf = jnp.transpose(dqf_t)

    dcq = jnp.transpose(dcq4[:, 0:2, :], (2, 0, 1)).reshape(s, 8)
    dck = jnp.transpose(dck4[:, :, 0:2], (1, 0, 2)).reshape(s, 8)
    dc = jnp.pad(dcq - dck, ((0, 0), (0, 120)))
    dfl, db_forget = _fox_gate_bwd(dc, proj, bf128)

    dproj, dgn = _proj_pre_bwd(dproj, proj, dqf, dkf, dvf, dqm, dqa, dkp[SWA_BLOCK:], dvp[SWA_BLOCK:], dfl,
                               gqf, gkf, gqm, gqa, gka)
    t_in = emit("w_in", _mm(h, dproj, "tn", F32, 512, 1536, 512, "dw_in"))
    grad_x, dg_mix = _in_bwd_x(dproj, w_in, x, _tie(p["g_mix"], t_in), dx1)
    d_rel = _rel_bias_bwd(dbias, bucket_m)

    fold = lambda r: dgn[r:r + 1, 0:HEAD] + dgn[r:r + 1, HEAD:128]
    small = {
        "g_mix": dg_mix, "b_gate": db_gate, "b_forget": db_forget[:, 0:8],
        "qn_swa": fold(3), "kn_swa": fold(4), "sink_swa": dsink[:, 0:8], "rel_bias": d_rel[:, 0:8],
        "qn_fox": fold(0), "kn_fox": fold(1), "g_mem": dg_mem, "qn_mem": dgn[2:3, :], "kn_mem": dkn_mem,
        "g_mlp": dg_mlp,
    }
    return loss, grad_x, small


SMALL = ("g_mix", "b_gate", "b_forget", "qn_swa", "kn_swa", "sink_swa", "rel_bias", "qn_fox", "kn_fox", "g_mem",
         "qn_mem", "kn_mem", "g_mlp")
BIG = ("w_in", "w_mem_kv", "w_o_swa", "w_o_fox", "w_o_mem", "w_out", "w_mlp_up", "w_mlp_down")
COL_SHARDED = ("w_in", "w_o_swa", "w_o_fox", "w_o_mem", "w_mlp_up")
WEIGHTS = ("g_mix", "w_in", "b_gate", "b_forget", "qn_swa", "kn_swa", "sink_swa", "rel_bias", "qn_fox", "kn_fox", "g_mem",
           "w_mem_kv", "qn_mem", "kn_mem", "w_o_swa", "w_o_fox", "w_o_mem", "w_out", "g_mlp", "w_mlp_up", "w_mlp_down")
SMALL_PAD = 7168


def _gathered_to_full(name, g):
    if name in COL_SHARDED:
        return jnp.transpose(g, (1, 0, 2)).reshape(g.shape[1], N_DEV * g.shape[2])
    return g.reshape(N_DEV * g.shape[1], g.shape[2])


def _full_to_parts(name, full, b):
    if name in COL_SHARDED:
        return jnp.transpose(full.reshape(full.shape[0], N_DEV, b), (1, 0, 2)).astype(BF16)
    return full.reshape(N_DEV, full.shape[0] // N_DEV, full.shape[1]).astype(BF16)


def _pack_small(d):
    flat = jnp.concatenate([d[n].reshape(-1) for n in SMALL])
    return jnp.pad(flat, (0, SMALL_PAD - flat.shape[0])).reshape(8, SMALL_PAD // 8)


def _unpack_small(packed, like):
    flat = packed.reshape(-1)
    out, off = {}, 0
    for n in SMALL:
        size = like[n].size
        out[n] = flat[off:off + size].reshape(like[n].shape)
        off += size
    return out


def _adamw_small(parts, w, m, v):
    def body(p_ref, w_ref, m_ref, v_ref, g_ref, d_ref, m2_ref, v2_ref):
        g = p_ref[0]
        for k in range(1, N_DEV):
            g = g + p_ref[k]
        delta, m2, v2 = _adam_math(w_ref[...], g, m_ref[...], v_ref[...])
        g_ref[...] = g
        d_ref[...] = delta
        m2_ref[...] = m2
        v2_ref[...] = v2

    sd = jax.ShapeDtypeStruct(w.shape, F32)
    return pl.pallas_call(body, name="adamw_small", out_shape=[sd, sd, sd, sd])(parts, w, m, v)


def kernel(x, mem, g_mix, w_in, b_gate, b_forget, qn_swa, kn_swa, sink_swa, rel_bias, qn_fox, kn_fox, g_mem, w_mem_kv, qn_mem, kn_mem, w_o_swa, w_o_fox, w_o_mem, w_out, g_mlp, w_mlp_up, w_mlp_down, loss_target, m_g_mix, m_w_in, m_b_gate, m_b_forget, m_qn_swa, m_kn_swa, m_sink_swa, m_rel_bias, m_qn_fox, m_kn_fox, m_g_mem, m_w_mem_kv, m_qn_mem, m_kn_mem, m_w_o_swa, m_w_o_fox, m_w_o_mem, m_w_out, m_g_mlp, m_w_mlp_up, m_w_mlp_down, v_g_mix, v_w_in, v_b_gate, v_b_forget, v_qn_swa, v_kn_swa, v_sink_swa, v_rel_bias, v_qn_fox, v_kn_fox, v_g_mem, v_w_mem_kv, v_qn_mem, v_kn_mem, v_w_o_swa, v_w_o_fox, v_w_o_mem, v_w_out, v_g_mlp, v_w_mlp_up, v_w_mlp_down):
    wts = dict(g_mix=g_mix, w_in=w_in, b_gate=b_gate, b_forget=b_forget, qn_swa=qn_swa, kn_swa=kn_swa, sink_swa=sink_swa,
               rel_bias=rel_bias, qn_fox=qn_fox, kn_fox=kn_fox, g_mem=g_mem, w_mem_kv=w_mem_kv, qn_mem=qn_mem, kn_mem=kn_mem,
               w_o_swa=w_o_swa, w_o_fox=w_o_fox, w_o_mem=w_o_mem, w_out=w_out, g_mlp=g_mlp, w_mlp_up=w_mlp_up,
               w_mlp_down=w_mlp_down)
    mom = dict(g_mix=m_g_mix, w_in=m_w_in, b_gate=m_b_gate, b_forget=m_b_forget, qn_swa=m_qn_swa, kn_swa=m_kn_swa,
               sink_swa=m_sink_swa, rel_bias=m_rel_bias, qn_fox=m_qn_fox, kn_fox=m_kn_fox, g_mem=m_g_mem, w_mem_kv=m_w_mem_kv,
               qn_mem=m_qn_mem, kn_mem=m_kn_mem, w_o_swa=m_w_o_swa, w_o_fox=m_w_o_fox, w_o_mem=m_w_o_mem, w_out=m_w_out,
               g_mlp=m_g_mlp, w_mlp_up=m_w_mlp_up, w_mlp_down=m_w_mlp_down)
    var = dict(g_mix=v_g_mix, w_in=v_w_in, b_gate=v_b_gate, b_forget=v_b_forget, qn_swa=v_qn_swa, kn_swa=v_kn_swa,
               sink_swa=v_sink_swa, rel_bias=v_rel_bias, qn_fox=v_qn_fox, kn_fox=v_kn_fox, g_mem=v_g_mem, w_mem_kv=v_w_mem_kv,
               qn_mem=v_qn_mem, kn_mem=v_kn_mem, w_o_swa=v_w_o_swa, w_o_fox=v_w_o_fox, w_o_mem=v_w_o_mem, w_out=v_w_out,
               g_mlp=v_g_mlp, w_mlp_up=v_w_mlp_up, w_mlp_down=v_w_mlp_down)

    me = _index(_my_place())
    dev = lax.broadcasted_iota(jnp.int32, (N_DEV, 1, 1), 0)

    shards = _cast_shards({n: wts[n][0] for n in BIG})
    gathers = {n: _split_start(shards[n], N_DEV, True, "ag_start_" + n) for n in BIG}
    full = {}

    def getw(n, after):
        if n not in full:
            _, land = _split_wait(gathers[n], after, True, "ag_wait_" + n)
            if n == "w_in":
                full[n] = _w_in_from_shards(land, shards[n])
            else:
                w = jnp.where(dev == me, shards[n][None], land)
                full[n] = w if n == "w_mlp_up" else _gathered_to_full(n, w)
        return full[n]

    exchanges = {}

    def emit(n, grad):
        if n == "w_in":
            parts = _dw_in_to_parts(grad)
        else:
            parts = grad if n == "w_mlp_up" else _full_to_parts(n, grad, wts[n].shape[2])
        exchanges[n] = _split_start(parts, N_DEV - 1, False, "rs_start_" + n)
        return exchanges[n][4]

    small_p = {n: wts[n] for n in SMALL}
    loss, grad_x, small_g = _local_step(x[0], mem[0], loss_target[0], small_p, getw, emit,
                                        tuple(gathers[n][4] for n in BIG))

    grads, delta, new_m, new_v = {}, {}, {}, {}
    after = grad_x

    def update(n, after):
        parts, land = _split_wait(exchanges[n], after, False, "rs_wait_" + n)
        own = lax.dynamic_index_in_dim(parts, me, 0, keepdims=False)
        g, d, m2, v2 = _adamw(own, land, wts[n][0], mom[n][0], var[n][0], "adamw_" + n)
        grads[n], delta[n], new_m[n], new_v[n] = g[None], d[None], m2[None], v2[None]
        return d

    for n in exchanges:
        if n != "w_in":
            after = update(n, after)

    gathered = _all_gather(_pack_small(small_g), "ag_small")
    g, d, m2, v2 = _adamw_small(gathered, _pack_small(small_p), _pack_small({n: mom[n] for n in SMALL}),
                                _pack_small({n: var[n] for n in SMALL}))
    for dst, packed in ((grads, g), (delta, d), (new_m, m2), (new_v, v2)):
        dst.update(_unpack_small(packed, small_p))
    update("w_in", after)

    total = lax.psum(loss[0, 0], ("x", "y", "c"))
    return (total, grad_x[None], *[grads[n] for n in WEIGHTS], *[delta[n] for n in WEIGHTS],
            *[new_m[n] for n in WEIGHTS], *[new_v[n] for n in WEIGHTS])
```

```python
import functools
import math

import jax
import jax.numpy as jnp
from jax import lax
from jax.experimental import pallas as pl
from jax.experimental.pallas import tpu as pltpu

F32 = jnp.float32
BF16 = jnp.bfloat16

D_MODEL = 1024
N_MEM = 256
D_FF = 4096
HEAD = 64
SWA_HEADS = 8
SWA_BLOCK = 128
MEM_HEADS = 4
MEM_HEAD = 128
EPS = 1e-6
NEG = -1e30
REL_BUCKETS = 32
REL_MAX_DIST = 128

ADAM_LR = 0.001
ADAM_B1 = 0.9
ADAM_B2 = 0.999
ADAM_EPS = 1e-08
ADAM_WD = 0.01
ADAM_STEP = 10

GL0, QF0, KF0, VF0, QM0, QA0, KA0, VA0, FL0 = 0, 3072, 3584, 4096, 4608, 5120, 5632, 5760, 5888
PROJ_W = 6144
HALF_W = 3072
H_QF, H_KF, H_VF, H_QM, H_QA, H_KA, H_VA, H_FL = 0, 512, 1024, 1536, 2048, 2560, 2688, 2816

VMEM_LIMIT = 56 * 1024 * 1024
N_DEV = 8
MESH = pl.DeviceIdType.MESH

NN = (((1,), (0,)), ((), ()))
NT = (((1,), (1,)), ((), ()))
TN = (((0,), (0,)), ((), ()))


def _dot(a, b, dims=NN):
    return lax.dot_general(a, b, dims, preferred_element_type=F32)


def _params(sem):
    return pltpu.CompilerParams(dimension_semantics=sem, vmem_limit_bytes=VMEM_LIMIT)


def _full(shape):
    nd = len(shape)
    return pl.BlockSpec(shape, lambda *_: (0,) * nd)


def _sigmoid(z):
    return 1.0 / (1.0 + jnp.exp(-z))


def _group_mean(v, hd):
    if hd == 128:
        return jnp.mean(v, axis=-1, keepdims=True)
    lane = lax.broadcasted_iota(jnp.int32, v.shape, 1)
    lo = lane < HEAD
    s_lo = jnp.sum(jnp.where(lo, v, 0.0), axis=-1, keepdims=True)
    s_hi = jnp.sum(jnp.where(lo, 0.0, v), axis=-1, keepdims=True)
    return jnp.where(lo, s_lo, s_hi) * (1.0 / HEAD)


def _mm(a, b, mode, out_dtype, tm, tn, tk, name, column_chunks=False):
    if mode == "nn":
        m, k = a.shape
        n = b.shape[1]
    elif mode == "nt":
        m, k = a.shape
        n = b.shape[0]
    else:
        k, m = a.shape
        n = b.shape[1]
    tm, tn, tk = min(tm, m), min(tn, n), min(tk, k)
    nk = k // tk
    chunk = n // N_DEV
    per_tile = tn // chunk if column_chunks else 1
    dims = {"nn": NN, "nt": NT, "tn": TN}[mode]
    a_spec = pl.BlockSpec((tk, tm), lambda j, i, kk: (kk, i)) if mode == "tn" else pl.BlockSpec((tm, tk), lambda j, i, kk: (i, kk))
    b_spec = pl.BlockSpec((tn, tk), lambda j, i, kk: (j, kk)) if mode == "nt" else pl.BlockSpec((tk, tn), lambda j, i, kk: (kk, j))

    def body(a_ref, b_ref, o_ref, *acc):
        prod = _dot(a_ref[...].astype(BF16), b_ref[...].astype(BF16), dims)

        def write(res):
            if column_chunks:
                for c in range(per_tile):
                    o_ref[c] = res[:, c * chunk:(c + 1) * chunk].astype(o_ref.dtype)
            else:
                o_ref[...] = res.astype(o_ref.dtype)

        if nk == 1:
            write(prod)
        else:
            acc_ref, = acc
            kk = pl.program_id(2)

            @pl.when(kk == 0)
            def _():
                acc_ref[...] = prod

            @pl.when(kk > 0)
            def _():
                acc_ref[...] += prod

            @pl.when(kk == nk - 1)
            def _():
                write(acc_ref[...])

    return pl.pallas_call(
        body, name=name, grid=(n // tn, m // tm, nk),
        in_specs=[a_spec, b_spec],
        out_specs=(pl.BlockSpec((per_tile, tm, chunk), lambda j, i, kk: (j, i, 0)) if column_chunks
                   else pl.BlockSpec((tm, tn), lambda j, i, kk: (i, j))),
        out_shape=jax.ShapeDtypeStruct((N_DEV, m, chunk) if column_chunks else (m, n), out_dtype),
        scratch_shapes=[pltpu.VMEM((tm, tn), F32)] if nk > 1 else [],
        compiler_params=_params(("parallel", "parallel", "arbitrary")),
    )(a, b)


def _rms_fwd(x, g, name, deps=()):
    s, d = x.shape
    tm = min(512, s)

    def body(x_ref, g_ref, *rest):
        h_ref = rest[len(deps)]
        xv = x_ref[...]
        r = lax.rsqrt(jnp.mean(xv * xv, axis=-1, keepdims=True) + EPS)
        h_ref[...] = (xv * r * g_ref[...]).astype(BF16)

    return pl.pallas_call(
        body, name=name, grid=(s // tm,),
        in_specs=[pl.BlockSpec((tm, d), lambda i: (i, 0)), _full((1, d))] + [pl.BlockSpec(memory_space=pl.ANY)] * len(deps),
        out_specs=pl.BlockSpec((tm, d), lambda i: (i, 0)),
        out_shape=jax.ShapeDtypeStruct((s, d), BF16),
        compiler_params=_params(("parallel",)),
    )(x, g, *deps)


def _proj_post(proj, gq_fox, gk_fox, gq_mem, gq_swa, gk_swa):
    s = proj.shape[0]
    tm = min(256, s)

    def body(p_ref, gqf, gkf, gqm, gqa, gka, qf_ref, kf_ref, vf_ref, qm_ref, qa_ref, ka_ref, va_ref):
        def norm(off, width, hd, g_ref, o_ref):
            for b in range(width // 128):
                v = p_ref[:, off + b * 128: off + (b + 1) * 128]
                r = lax.rsqrt(_group_mean(v * v, hd) + EPS)
                o_ref[:, b * 128:(b + 1) * 128] = (v * r * g_ref[...]).astype(BF16)

        norm(H_QF, 512, HEAD, gqf, qf_ref)
        norm(H_KF, 512, HEAD, gkf, kf_ref)
        vf_ref[...] = p_ref[:, H_VF:H_VF + 512].astype(BF16)
        norm(H_QM, 512, MEM_HEAD, gqm, qm_ref)
        norm(H_QA, 512, HEAD, gqa, qa_ref)
        norm(H_KA, 128, HEAD, gka, ka_ref)
        va_ref[...] = p_ref[:, H_VA:H_VA + 128].astype(BF16)

    g_spec = _full((1, 128))
    o512 = pl.BlockSpec((tm, 512), lambda i: (i, 0))
    o128 = pl.BlockSpec((tm, 128), lambda i: (i, 0))
    s512 = jax.ShapeDtypeStruct((s, 512), BF16)
    s128 = jax.ShapeDtypeStruct((s, 128), BF16)
    return pl.pallas_call(
        body, name="proj_post", grid=(s // tm,),
        in_specs=[pl.BlockSpec((tm, HALF_W), lambda i: (i, 1)), g_spec, g_spec, g_spec, g_spec, g_spec],
        out_specs=[o512, o512, o512, o512, o512, o128, o128],
        out_shape=[s512, s512, s512, s512, s512, s128, s128],
        compiler_params=_params(("parallel",)),
    )(proj, gq_fox, gk_fox, gq_mem, gq_swa, gk_swa)


def _tri(n, lower):
    r = lax.broadcasted_iota(jnp.int32, (n, n), 0)
    c = lax.broadcasted_iota(jnp.int32, (n, n), 1)
    return jnp.where((c <= r) if lower else (c >= r), 1.0, 0.0).astype(F32)


def _fox_gate_fwd(proj, b_forget128):
    s = proj.shape[0]
    tm = min(512, s)

    def body(p_ref, b_ref, cc_ref, carry_ref):
        i = pl.program_id(0)

        @pl.when(i == 0)
        def _():
            carry_ref[...] = jnp.zeros_like(carry_ref)

        z = p_ref[...] + b_ref[...]
        logf = jnp.minimum(z, 0.0) - jnp.log(1.0 + jnp.exp(-jnp.abs(z)))
        c = jnp.dot(_tri(tm, True), logf, precision=lax.Precision.HIGHEST, preferred_element_type=F32) + carry_ref[...]
        carry_ref[...] = c[tm - 1:tm, :]
        for hp in range(4):
            cc_ref[hp] = c if hp == 0 else pltpu.roll(c, 128 - 2 * hp, 1)

    return pl.pallas_call(
        body, name="fox_gate_fwd", grid=(s // tm,),
        in_specs=[pl.BlockSpec((tm, 128), lambda i: (i, FL0 // 128)), _full((1, 128))],
        out_specs=pl.BlockSpec((4, tm, 128), lambda i: (0, i, 0)),
        out_shape=jax.ShapeDtypeStruct((4, s, 128), F32),
        scratch_shapes=[pltpu.VMEM((1, 128), F32)],
        compiler_params=_params(("arbitrary",)),
    )(proj, b_forget128)


def _memkv_fwd(mem, g_mem, w_kv, kn_mem):
    m = mem.shape[0]

    def body(mem_ref, g_ref, w_ref, kn_ref, memn_ref, kv_ref, mk_ref, mv_ref):
        xv = mem_ref[...]
        r = lax.rsqrt(jnp.mean(xv * xv, axis=-1, keepdims=True) + EPS)
        mn = (xv * r * g_ref[...]).astype(BF16)
        memn_ref[...] = mn
        kv = _dot(mn, w_ref[...])
        kv_ref[...] = kv
        for h in range(MEM_HEADS):
            v = kv[:, h * 128:(h + 1) * 128]
            rr = lax.rsqrt(jnp.mean(v * v, axis=-1, keepdims=True) + EPS)
            mk_ref[:, h * 128:(h + 1) * 128] = (v * rr * kn_ref[...]).astype(BF16)
        mv_ref[...] = kv[:, 512:1024].astype(BF16)

    return pl.pallas_call(
        body, name="memkv_fwd",
        out_shape=[jax.ShapeDtypeStruct((m, D_MODEL), BF16), jax.ShapeDtypeStruct((m, 1024), F32),
                   jax.ShapeDtypeStruct((m, 512), BF16), jax.ShapeDtypeStruct((m, 512), BF16)],
        compiler_params=pltpu.CompilerParams(vmem_limit_bytes=VMEM_LIMIT),
    )(mem, g_mem, w_kv, kn_mem)


def _bias_table(rel_bias, bucket):
    def body(rb_ref, bk_ref, o_ref):
        bk = bk_ref[...]
        for h in range(SWA_HEADS):
            acc = jnp.zeros(bk.shape, F32)
            for b in range(REL_BUCKETS):
                acc = jnp.where(bk == b, rb_ref[b, h], acc)
            o_ref[h] = acc

    return pl.pallas_call(
        body, name="bias_table",
        in_specs=[pl.BlockSpec(memory_space=pltpu.SMEM), pl.BlockSpec(memory_space=pltpu.VMEM)],
        out_shape=jax.ShapeDtypeStruct((SWA_HEADS,) + bucket.shape, F32),
    )(rel_bias, bucket)


def _swa_valid(n):
    row = lax.broadcasted_iota(jnp.int32, (SWA_BLOCK, 2 * SWA_BLOCK), 0)
    col = lax.broadcasted_iota(jnp.int32, (SWA_BLOCK, 2 * SWA_BLOCK), 1)
    dist = row + SWA_BLOCK - col
    return (dist >= 0) & (dist < SWA_BLOCK) & ((col >= SWA_BLOCK) | (n > 0))


def _swa_fwd(qa, kp, vp, bias, sink):
    s = qa.shape[0]
    nb = s // SWA_BLOCK

    def body(sink_ref, q_ref, kp_ref, vp_ref, bias_ref, o_ref):
        n = pl.program_id(0)
        start = pl.multiple_of(n * SWA_BLOCK, SWA_BLOCK)
        k2 = kp_ref[pl.ds(start, 2 * SWA_BLOCK), :]
        v2 = vp_ref[pl.ds(start, 2 * SWA_BLOCK), :]
        valid = _swa_valid(n)
        for h in range(SWA_HEADS):
            kv = h // 4
            qh = q_ref[:, h * HEAD:(h + 1) * HEAD]
            kh = k2[:, kv * HEAD:(kv + 1) * HEAD]
            vh = v2[:, kv * HEAD:(kv + 1) * HEAD]
            sc = _dot(qh, kh, NT) * 0.125 + bias_ref[h]
            sc = jnp.where(valid, sc, NEG)
            sk = sink_ref[h]
            mx = jnp.maximum(jnp.max(sc, axis=-1, keepdims=True), sk)
            p = jnp.exp(sc - mx)
            den = jnp.sum(p, axis=-1, keepdims=True) + jnp.exp(sk - mx)
            p = p / den
            o_ref[:, h * HEAD:(h + 1) * HEAD] = _dot(p.astype(BF16), vh).astype(BF16)

    return pl.pallas_call(
        body, name="swa_fwd", grid=(nb,),
        in_specs=[pl.BlockSpec(memory_space=pltpu.SMEM),
                  pl.BlockSpec((SWA_BLOCK, 512), lambda n: (n, 0)),
                  _full(kp.shape), _full(vp.shape), _full(bias.shape)],
        out_specs=pl.BlockSpec((SWA_BLOCK, 512), lambda n: (n, 0)),
        out_shape=jax.ShapeDtypeStruct((s, 512), BF16),
        compiler_params=_params(("parallel",)),
    )(sink, qa, kp, vp, bias)


def _head_mask(e):
    lane = lax.broadcasted_iota(jnp.int32, (1, 128), 1)
    return (lane >= e * HEAD) & (lane < (e + 1) * HEAD)


FOX_FWD_TQ, FOX_FWD_TK = 1024, 1024
FOX_BWD_TK, FOX_BWD_TQ = 256, 512


def _head_rows(e):
    row = lax.broadcasted_iota(jnp.int32, (128, 1), 0)
    return (row >= e * HEAD) & (row < (e + 1) * HEAD)


def _fox_fwd(q, k, v_t, cc4):
    s = q.shape[0]
    t = min(FOX_FWD_TQ, s)
    tk = min(FOX_FWD_TK, s)
    nq = s // t

    def body(q_ref, k_ref, vt_ref, cc_ref, o_ref, lse_ref):
        i = pl.program_id(1)
        qs = q_ref[...] * jnp.asarray(0.125, BF16)
        qe = [jnp.where(_head_mask(e), qs, jnp.zeros_like(qs)) for e in range(2)]
        n_full = (i * t) // tk
        krow = lax.broadcasted_iota(jnp.int32, (tk, t), 0) + n_full * tk
        qcol = lax.broadcasted_iota(jnp.int32, (tk, t), 1) + i * t

        def step(j, carry, masked):
            ks = pl.ds(pl.multiple_of(j * tk, tk), tk)
            kj = k_ref[ks, :]
            vtj = vt_ref[:, ks]
            out = []
            for e in range(2):
                m, acc = carry[2 * e], carry[2 * e + 1]
                st = _dot(kj, qe[e], NT) - cc_ref[0, ks, e:e + 1]
                if masked:
                    st = jnp.where(krow <= qcol, st, NEG)
                m_new = jnp.maximum(m, jnp.max(st, axis=0, keepdims=True))
                alpha = jnp.exp(m - m_new)
                pt = jnp.exp(st - m_new).astype(BF16)
                vte = jnp.where(_head_rows(e), vtj, jnp.ones_like(vtj))
                out += [m_new, alpha * acc + _dot(vte, pt)]
            return tuple(out)

        init = (jnp.full((1, t), NEG, F32), jnp.zeros((128, t), F32)) * 2
        carry = lax.fori_loop(0, n_full, functools.partial(step, masked=False), init)
        m0, a0, m1, a1 = step(n_full, carry, True)
        l0 = a0[HEAD:HEAD + 1, :]
        l1 = a1[0:1, :]
        o_t = jnp.where(_head_rows(0), a0 / l0, a1 / l1)
        o_ref[...] = o_t.T.astype(BF16)
        r8 = lax.broadcasted_iota(jnp.int32, (8, t), 0)
        lse_ref[0] = jnp.where(r8 == 0, m0 + jnp.log(l0), jnp.where(r8 == 1, m1 + jnp.log(l1), 0.0))

    return pl.pallas_call(
        body, name="fox_fwd", grid=(4, nq),
        in_specs=[pl.BlockSpec((t, 128), lambda hp, i: (i, hp)),
                  pl.BlockSpec((s, 128), lambda hp, i: (0, hp)),
                  pl.BlockSpec((128, s), lambda hp, i: (hp, 0)),
                  pl.BlockSpec((1, s, 128), lambda hp, i: (hp, 0, 0))],
        out_specs=[pl.BlockSpec((t, 128), lambda hp, i: (i, hp)),
                   pl.BlockSpec((1, 8, t), lambda hp, i: (hp, 0, i))],
        out_shape=[jax.ShapeDtypeStruct((s, 512), BF16), jax.ShapeDtypeStruct((4, 8, s), F32)],
        compiler_params=_params(("parallel", "parallel")),
    )(q, k, v_t, cc4)


MEM_SCALE = MEM_HEAD ** -0.5


def _mem_fwd(qm, mk, mv):
    s = qm.shape[0]
    tq = min(512, s)

    def body(q_ref, mk_ref, mv_ref, o_ref):
        for h in range(MEM_HEADS):
            hs = slice(h * 128, (h + 1) * 128)
            sc = _dot(q_ref[:, hs], mk_ref[:, hs], NT) * MEM_SCALE
            mx = jnp.max(sc, axis=-1, keepdims=True)
            p = jnp.exp(sc - mx)
            p = p / jnp.sum(p, axis=-1, keepdims=True)
            o_ref[:, hs] = _dot(p.astype(BF16), mv_ref[:, hs]).astype(BF16)

    return pl.pallas_call(
        body, name="mem_fwd", grid=(s // tq,),
        in_specs=[pl.BlockSpec((tq, 512), lambda i: (i, 0)), _full(mk.shape), _full(mv.shape)],
        out_specs=pl.BlockSpec((tq, 512), lambda i: (i, 0)),
        out_shape=jax.ShapeDtypeStruct((s, 512), BF16),
        compiler_params=_params(("parallel",)),
    )(qm, mk, mv)


def _merge_fwd(x, oa, of, om, proj, b_gate, wa, wf, wm, w_out, g_mlp):
    s = x.shape[0]
    tm = min(256, s)

    def body(x_ref, oa_ref, of_ref, om_ref, gl_ref, bg_ref, wa_ref, wf_ref, wm_ref, wo_ref, g_ref, x1_ref, hm_ref, mg_ref):
        merged = None
        for b, (o_ref, w_ref) in enumerate(((oa_ref, wa_ref), (of_ref, wf_ref), (om_ref, wm_ref))):
            cs = slice(b * D_MODEL, (b + 1) * D_MODEL)
            y = _dot(o_ref[...], w_ref[...])
            t = _sigmoid(gl_ref[:, cs] + bg_ref[:, cs]) * y
            merged = t if merged is None else merged + t
        mb = merged.astype(BF16)
        mg_ref[...] = mb
        x1 = x_ref[...] + _dot(mb, wo_ref[...])
        x1_ref[...] = x1
        r = lax.rsqrt(jnp.mean(x1 * x1, axis=-1, keepdims=True) + EPS)
        hm_ref[...] = (x1 * r * g_ref[...]).astype(BF16)

    row = lambda w: pl.BlockSpec((tm, w), lambda i: (i, 0))
    return pl.pallas_call(
        body, name="merge_fwd", grid=(s // tm,),
        in_specs=[row(D_MODEL), row(512), row(512), row(512), row(HALF_W), _full((1, HALF_W)),
                  _full(wa.shape), _full(wf.shape), _full(wm.shape), _full(w_out.shape), _full((1, D_MODEL))],
        out_specs=[row(D_MODEL), row(D_MODEL), row(D_MODEL)],
        out_shape=[jax.ShapeDtypeStruct((s, D_MODEL), F32), jax.ShapeDtypeStruct((s, D_MODEL), BF16),
                   jax.ShapeDtypeStruct((s, D_MODEL), BF16)],
        compiler_params=_params(("parallel",)),
    )(x, oa, of, om, proj, b_gate, wa, wf, wm, w_out, g_mlp)


def _mlp_up(hm, w_up):
    s = hm.shape[0]
    tm, tn = min(1024, s), w_up.shape[2]

    def body(h_ref, w_ref, u_ref):
        r = jnp.maximum(_dot(h_ref[...], w_ref[0]), 0.0)
        u_ref[...] = (r * r).astype(BF16)

    return pl.pallas_call(
        body, name="mlp_up", grid=(s // tm, D_FF // tn),
        in_specs=[pl.BlockSpec((tm, D_MODEL), lambda i, j: (i, 0)), pl.BlockSpec((1, D_MODEL, tn), lambda i, j: (j, 0, 0))],
        out_specs=pl.BlockSpec((tm, tn), lambda i, j: (i, j)),
        out_shape=jax.ShapeDtypeStruct((s, D_FF), BF16),
        compiler_params=_params(("parallel", "parallel")),
    )(hm, w_up)


def _mlp_down_loss(u, w_down, x1, target):
    s = u.shape[0]
    tm = min(256, s)

    def body(u_ref, w_ref, x1_ref, t_ref, dy_ref, dyb_ref, loss_ref):
        i = pl.program_id(0)

        @pl.when(i == 0)
        def _():
            loss_ref[...] = jnp.zeros_like(loss_ref)

        y = x1_ref[...] + _dot(u_ref[...], w_ref[...])
        err = y - t_ref[...]
        dy = err * (1.0 / D_MODEL)
        dy_ref[...] = dy
        dyb_ref[...] = dy.astype(BF16)
        part = jnp.sum(jnp.sum(err * err, axis=-1, keepdims=True) * (1.0 / D_MODEL), axis=0, keepdims=True)
        loss_ref[...] += 0.5 * part

    row = pl.BlockSpec((tm, D_MODEL), lambda i: (i, 0))
    return pl.pallas_call(
        body, name="mlp_down_loss", grid=(s // tm,),
        in_specs=[pl.BlockSpec((tm, D_FF), lambda i: (i, 0)), _full(w_down.shape), row, row],
        out_specs=[row, row, _full((1, 1))],
        out_shape=[jax.ShapeDtypeStruct((s, D_MODEL), F32), jax.ShapeDtypeStruct((s, D_MODEL), BF16),
                   jax.ShapeDtypeStruct((1, 1), F32)],
        compiler_params=_params(("arbitrary",)),
    )(u, w_down, x1, target)


def _mlp_bwd_act(dy, w_down, u):
    s = dy.shape[0]
    tm, tn = min(1024, s), 1024

    def body(dy_ref, w_ref, u_ref, da_ref):
        du = _dot(dy_ref[...], w_ref[...], NT)
        da_ref[...] = (du * (2.0 * jnp.sqrt(u_ref[...].astype(F32)))).astype(BF16)

    return pl.pallas_call(
        body, name="mlp_bwd_act", grid=(D_FF // tn, s // tm),
        in_specs=[pl.BlockSpec((tm, D_MODEL), lambda j, i: (i, 0)), pl.BlockSpec((tn, D_MODEL), lambda j, i: (j, 0)),
                  pl.BlockSpec((tm, tn), lambda j, i: (i, j))],
        out_specs=pl.BlockSpec((tm, tn), lambda j, i: (i, j)),
        out_shape=jax.ShapeDtypeStruct((s, D_FF), BF16),
        compiler_params=_params(("parallel", "parallel")),
    )(dy, w_down, u)


def _rms_bwd(xv, g, dh, skip):
    r = lax.rsqrt(jnp.mean(xv * xv, axis=-1, keepdims=True) + EPS)
    n = xv * r
    dn = dh * g
    dx = skip + r * (dn - n * jnp.mean(dn * n, axis=-1, keepdims=True))
    return dx, jnp.sum(dh * n, axis=0, keepdims=True)


def _mlp_bwd_x(da, w_up, x1, dy, g_mlp):
    s = da.shape[0]
    tm = min(256, s)

    def body(da_ref, w_ref, x1_ref, dy_ref, g_ref, dx1_ref, dg_ref):
        i = pl.program_id(0)

        @pl.when(i == 0)
        def _():
            dg_ref[...] = jnp.zeros_like(dg_ref)

        tn = w_ref.shape[2]
        dhm = _dot(da_ref[:, 0:tn], w_ref[0], NT)
        for j in range(1, N_DEV):
            dhm = dhm + _dot(da_ref[:, j * tn:(j + 1) * tn], w_ref[j], NT)
        dx, dg = _rms_bwd(x1_ref[...], g_ref[...], dhm, dy_ref[...])
        dx1_ref[...] = dx
        dg_ref[...] += dg

    row = pl.BlockSpec((tm, D_MODEL), lambda i: (i, 0))
    return pl.pallas_call(
        body, name="mlp_bwd_x", grid=(s // tm,),
        in_specs=[pl.BlockSpec((tm, D_FF), lambda i: (i, 0)), _full(w_up.shape), row, row, _full((1, D_MODEL))],
        out_specs=[row, _full((1, D_MODEL))],
        out_shape=[jax.ShapeDtypeStruct((s, D_MODEL), F32), jax.ShapeDtypeStruct((1, D_MODEL), F32)],
        compiler_params=_params(("arbitrary",)),
    )(da, w_up, x1, dy, g_mlp)


def _merge_bwd(dx1, oa, of, om, proj, b_gate, wa, wf, wm, w_out):
    s = dx1.shape[0]
    tm = min(256, s)

    def body(dx1_ref, oa_ref, of_ref, om_ref, gl_ref, bg_ref, wa_ref, wf_ref, wm_ref, wo_ref,
             dp_ref, doa_ref, dof_ref, dom_ref, dya_ref, dyf_ref, dym_ref, dbg_ref):
        i = pl.program_id(0)

        @pl.when(i == 0)
        def _():
            dbg_ref[...] = jnp.zeros_like(dbg_ref)

        dmerged = _dot(dx1_ref[...].astype(BF16), wo_ref[...], NT)
        branches = ((oa_ref, wa_ref, doa_ref, dya_ref), (of_ref, wf_ref, dof_ref, dyf_ref), (om_ref, wm_ref, dom_ref, dym_ref))
        for b, (o_ref, w_ref, do_ref, dyb_ref) in enumerate(branches):
            cs = slice(b * D_MODEL, (b + 1) * D_MODEL)
            y = _dot(o_ref[...], w_ref[...])
            g = _sigmoid(gl_ref[:, cs] + bg_ref[:, cs])
            dz = (dmerged * y) * g * (1.0 - g)
            dp_ref[:, cs] = dz.astype(BF16)
            dbg_ref[:, cs] += jnp.sum(dz, axis=0, keepdims=True)
            dyb = (dmerged * g).astype(BF16)
            dyb_ref[...] = dyb
            do_ref[...] = _dot(dyb, w_ref[...], NT).astype(BF16)

    row = lambda w: pl.BlockSpec((tm, w), lambda i: (i, 0))
    sd = lambda w: jax.ShapeDtypeStruct((s, w), BF16)
    return pl.pallas_call(
        body, name="merge_bwd", grid=(s // tm,),
        in_specs=[row(D_MODEL), row(512), row(512), row(512), row(HALF_W), _full((1, HALF_W)),
                  _full(wa.shape), _full(wf.shape), _full(wm.shape), _full(w_out.shape)],
        out_specs=[row(HALF_W), row(512), row(512), row(512), row(D_MODEL), row(D_MODEL), row(D_MODEL), _full((1, HALF_W))],
        out_shape=[sd(PROJ_W), sd(512), sd(512), sd(512), sd(D_MODEL), sd(D_MODEL), sd(D_MODEL),
                   jax.ShapeDtypeStruct((1, HALF_W), F32)],
        compiler_params=_params(("arbitrary",)),
    )(dx1, oa, of, om, proj, b_gate, wa, wf, wm, w_out)


def _swa_bwd(qa, kp, vp, bias, sink, doa):
    s = qa.shape[0]
    nb = s // SWA_BLOCK

    def body(sink_ref, q_ref, kp_ref, vp_ref, bias_ref, do_ref, dq_ref, dkp_ref, dvp_ref, dbias_ref, dsink_ref, sk_acc):
        n = pl.program_id(0)

        @pl.when(n == 0)
        def _():
            dkp_ref[...] = jnp.zeros_like(dkp_ref)
            dvp_ref[...] = jnp.zeros_like(dvp_ref)
            dbias_ref[...] = jnp.zeros_like(dbias_ref)
            sk_acc[...] = jnp.zeros_like(sk_acc)

        start = pl.multiple_of(n * SWA_BLOCK, SWA_BLOCK)
        win = pl.ds(start, 2 * SWA_BLOCK)
        k2 = kp_ref[win, :]
        v2 = vp_ref[win, :]
        valid = _swa_valid(n)
        for kv in range(2):
            hs_kv = slice(kv * HEAD, (kv + 1) * HEAD)
            kh = k2[:, hs_kv]
            vh = v2[:, hs_kv]
            dk2 = jnp.zeros((2 * SWA_BLOCK, HEAD), F32)
            dv2 = jnp.zeros((2 * SWA_BLOCK, HEAD), F32)
            for g in range(4):
                h = kv * 4 + g
                hs = slice(h * HEAD, (h + 1) * HEAD)
                qh = q_ref[:, hs]
                doh = do_ref[:, hs]
                sc = _dot(qh, kh, NT) * 0.125 + bias_ref[h]
                sc = jnp.where(valid, sc, NEG)
                sk = sink_ref[h]
                mx = jnp.maximum(jnp.max(sc, axis=-1, keepdims=True), sk)
                p = jnp.exp(sc - mx)
                esk = jnp.exp(sk - mx)
                den = jnp.sum(p, axis=-1, keepdims=True) + esk
                p = p / den
                dp = _dot(doh, vh, NT)
                delta = jnp.sum(p * dp, axis=-1, keepdims=True)
                ds = p * (dp - delta)
                sk_acc[:, h:h + 1] += -(esk / den) * delta
                dbias_ref[h] += ds
                dsb = (ds * 0.125).astype(BF16)
                dq_ref[:, hs] = _dot(dsb, kh)
                dk2 = dk2 + _dot(dsb, qh, TN)
                dv2 = dv2 + _dot(p.astype(BF16), doh, TN)
            dkp_ref[win, hs_kv] += dk2
            dvp_ref[win, hs_kv] += dv2

        @pl.when(n == nb - 1)
        def _():
            dsink_ref[...] = jnp.sum(sk_acc[...], axis=0, keepdims=True)

    return pl.pallas_call(
        body, name="swa_bwd", grid=(nb,),
        in_specs=[pl.BlockSpec(memory_space=pltpu.SMEM),
                  pl.BlockSpec((SWA_BLOCK, 512), lambda n: (n, 0)),
                  _full(kp.shape), _full(vp.shape), _full(bias.shape),
                  pl.BlockSpec((SWA_BLOCK, 512), lambda n: (n, 0))],
        out_specs=[pl.BlockSpec((SWA_BLOCK, 512), lambda n: (n, 0)), _full(kp.shape), _full(vp.shape),
                   _full(bias.shape), _full((1, 128))],
        out_shape=[jax.ShapeDtypeStruct((s, 512), F32), jax.ShapeDtypeStruct(kp.shape, F32),
                   jax.ShapeDtypeStruct(vp.shape, F32), jax.ShapeDtypeStruct(bias.shape, F32),
                   jax.ShapeDtypeStruct((1, 128), F32)],
        scratch_shapes=[pltpu.VMEM((SWA_BLOCK, 128), F32)],
        compiler_params=_params(("arbitrary",)),
    )(sink, qa, kp, vp, bias, doa)


def _fox_bwd(q, k, v, do, o, cc4, lse4):
    s = q.shape[0]
    t = min(FOX_BWD_TK, s)
    tq = min(FOX_BWD_TQ, s)
    nq = s // t
    nqt = s // tq

    def body(q_ref, k_ref, v_ref, do_ref, o_ref, cc_ref, lse_ref,
             dqt_ref, dk_ref, dv_ref, dck_ref, dcq_ref, delta_ref, dk0, dk1, dv0, dv1, ds0, ds1):
        j = pl.program_id(1)

        @pl.when(j == 0)
        def _():
            dqt_ref[...] = jnp.zeros_like(dqt_ref)
            dcq_ref[...] = jnp.zeros_like(dcq_ref)
            lane8 = lax.broadcasted_iota(jnp.int32, (8, 128), 1)
            row8 = lax.broadcasted_iota(jnp.int32, (8, 128), 0)
            sel = jnp.where((lane8 // HEAD) == row8, 1.0, 0.0).astype(F32)

            def dl(i, c):
                rows = pl.ds(pl.multiple_of(i * tq, tq), tq)
                pr = do_ref[rows, :].astype(F32) * o_ref[rows, :].astype(F32)
                delta_ref[:, rows] = lax.dot_general(sel, pr, NT, precision=lax.Precision.HIGHEST,
                                                     preferred_element_type=F32)
                return c

            lax.fori_loop(0, nqt, dl, 0)

        kj = k_ref[...]
        vj = v_ref[...]
        ks = pl.ds(pl.multiple_of(j * t, t), t)
        kt = (kj.astype(F32) * 0.125).T.astype(BF16)
        ke = [jnp.where(_head_mask(e), kj, jnp.zeros_like(kj)) for e in range(2)]
        ve = [jnp.where(_head_mask(e), vj, jnp.zeros_like(vj)) for e in range(2)]
        kte = [jnp.where(_head_rows(e), kt, jnp.zeros_like(kt)) for e in range(2)]
        ck = [cc_ref[0, ks, e:e + 1] for e in range(2)]
        accs = ((dk0, dv0, ds0), (dk1, dv1, ds1))
        for refs in accs:
            for r in refs:
                r[...] = jnp.zeros_like(r)
        i_first = (j * t) // tq
        krow = lax.broadcasted_iota(jnp.int32, (t, tq), 0) + j * t
        qcol = lax.broadcasted_iota(jnp.int32, (t, tq), 1) + i_first * tq

        def step(i, c, masked):
            rows = pl.ds(pl.multiple_of(i * tq, tq), tq)
            qs = q_ref[rows, :] * jnp.asarray(0.125, BF16)
            doi = do_ref[rows, :]
            for e in range(2):
                dk_acc, dv_acc, ds_acc = accs[e]
                st = _dot(ke[e], qs, NT) - ck[e]
                if masked:
                    st = jnp.where(krow <= qcol, st, NEG)
                pt = jnp.exp(st - lse_ref[0, e:e + 1, rows])
                dpt = _dot(ve[e], doi, NT)
                dst = pt * (dpt - delta_ref[e:e + 1, rows])
                dsb = dst.astype(BF16)
                dv_acc[...] += _dot(pt.astype(BF16), doi)
                dk_acc[...] += _dot(dsb, qs)
                dqt_ref[:, rows] += _dot(kte[e], dsb)
                ds_acc[...] += dst
                dcq_ref[0, e:e + 1, rows] += jnp.sum(dst, axis=0, keepdims=True)
            return c

        step(i_first, 0, True)
        lax.fori_loop(i_first + 1, nqt, functools.partial(step, masked=False), 0)
        m0 = _head_mask(0)
        dk_ref[...] = jnp.where(m0, dk0[...], dk1[...])
        dv_ref[...] = jnp.where(m0, dv0[...], dv1[...])
        lane = lax.broadcasted_iota(jnp.int32, (t, 128), 1)
        c0 = jnp.sum(ds0[...], axis=-1, keepdims=True)
        c1 = jnp.sum(ds1[...], axis=-1, keepdims=True)
        dck_ref[0] = jnp.where(lane == 0, c0, jnp.where(lane == 1, c1, 0.0))

    res = lambda: pl.BlockSpec((s, 128), lambda hp, j: (0, hp))
    blk = lambda: pl.BlockSpec((t, 128), lambda hp, j: (j, hp))
    return pl.pallas_call(
        body, name="fox_bwd", grid=(4, nq),
        in_specs=[res(), blk(), blk(), res(), res(), pl.BlockSpec((1, s, 128), lambda hp, j: (hp, 0, 0)),
                  pl.BlockSpec((1, 8, s), lambda hp, j: (hp, 0, 0))],
        out_specs=[pl.BlockSpec((128, s), lambda hp, j: (hp, 0)), blk(), blk(),
                   pl.BlockSpec((1, t, 128), lambda hp, j: (hp, j, 0)),
                   pl.BlockSpec((1, 8, s), lambda hp, j: (hp, 0, 0))],
        out_shape=[jax.ShapeDtypeStruct((512, s), F32), jax.ShapeDtypeStruct((s, 512), F32),
                   jax.ShapeDtypeStruct((s, 512), F32), jax.ShapeDtypeStruct((4, s, 128), F32),
                   jax.ShapeDtypeStruct((4, 8, s), F32)],
        scratch_shapes=[pltpu.VMEM((8, s), F32)] + [pltpu.VMEM((t, 128), F32)] * 4 + [pltpu.VMEM((t, tq), F32)] * 2,
        compiler_params=_params(("arbitrary", "arbitrary")),
    )(q, k, v, do, o, cc4, lse4)


def _mem_bwd(qm, mk, mv, dom):
    s = qm.shape[0]
    tq = min(512, s)

    def body(q_ref, mk_ref, mv_ref, do_ref, dq_ref, dmk_ref, dmv_ref):
        i = pl.program_id(0)

        @pl.when(i == 0)
        def _():
            dmk_ref[...] = jnp.zeros_like(dmk_ref)
            dmv_ref[...] = jnp.zeros_like(dmv_ref)

        for h in range(MEM_HEADS):
            hs = slice(h * 128, (h + 1) * 128)
            qh = q_ref[:, hs]
            doh = do_ref[:, hs]
            sc = _dot(qh, mk_ref[:, hs], NT) * MEM_SCALE
            mx = jnp.max(sc, axis=-1, keepdims=True)
            p = jnp.exp(sc - mx)
            p = p / jnp.sum(p, axis=-1, keepdims=True)
            dp = _dot(doh, mv_ref[:, hs], NT)
            ds = p * (dp - jnp.sum(p * dp, axis=-1, keepdims=True))
            dsb = (ds * MEM_SCALE).astype(BF16)
            dq_ref[:, hs] = _dot(dsb, mk_ref[:, hs])
            dmk_ref[:, hs] += _dot(dsb, qh, TN)
            dmv_ref[:, hs] += _dot(p.astype(BF16), doh, TN)

    return pl.pallas_call(
        body, name="mem_bwd", grid=(s // tq,),
        in_specs=[pl.BlockSpec((tq, 512), lambda i: (i, 0)), _full(mk.shape), _full(mv.shape),
                  pl.BlockSpec((tq, 512), lambda i: (i, 0))],
        out_specs=[pl.BlockSpec((tq, 512), lambda i: (i, 0)), _full(mk.shape), _full(mv.shape)],
        out_shape=[jax.ShapeDtypeStruct((s, 512), F32), jax.ShapeDtypeStruct(mk.shape, F32),
                   jax.ShapeDtypeStruct(mv.shape, F32)],
        compiler_params=_params(("arbitrary",)),
    )(qm, mk, mv, dom)


def _memkv_bwd(dmk, dmv, kv_raw, kn_mem, mem, g_mem, mem_n, w_kv):
    def body(dmk_ref, dmv_ref, kv_ref, kn_ref, mem_ref, g_ref, mn_ref, w_ref, dw_ref, dkn_ref, dg_ref, dkv_ref):
        dkn = jnp.zeros((1, 128), F32)
        for h in range(MEM_HEADS):
            hs = slice(h * 128, (h + 1) * 128)
            v = kv_ref[:, hs]
            r = lax.rsqrt(jnp.mean(v * v, axis=-1, keepdims=True) + EPS)
            n = v * r
            dn = dmk_ref[:, hs]
            dkn = dkn + jnp.sum(dn * n, axis=0, keepdims=True)
            dng = dn * kn_ref[...]
            dkv_ref[:, hs] = (r * (dng - n * jnp.mean(dng * n, axis=-1, keepdims=True))).astype(BF16)
        dkv_ref[:, 512:1024] = dmv_ref[...].astype(BF16)
        dkn_ref[...] = dkn
        dkv = dkv_ref[...]
        dw_ref[...] = _dot(mn_ref[...], dkv, TN).astype(BF16)
        dmn = _dot(dkv, w_ref[...], NT)
        xv = mem_ref[...]
        r = lax.rsqrt(jnp.mean(xv * xv, axis=-1, keepdims=True) + EPS)
        dg_ref[...] = jnp.sum(dmn * (xv * r), axis=0, keepdims=True)

    m = mem.shape[0]
    return pl.pallas_call(
        body, name="memkv_bwd",
        out_shape=[jax.ShapeDtypeStruct((D_MODEL, 1024), BF16), jax.ShapeDtypeStruct((1, 128), F32),
                   jax.ShapeDtypeStruct((1, D_MODEL), F32)],
        scratch_shapes=[pltpu.VMEM((m, 1024), BF16)],
        compiler_params=pltpu.CompilerParams(vmem_limit_bytes=VMEM_LIMIT),
    )(dmk, dmv, kv_raw, kn_mem, mem, g_mem, mem_n, w_kv)


def _fox_gate_bwd(dc, proj, b_forget128):
    s = dc.shape[0]
    tm = min(512, s)
    nt = s // tm

    def body(dc_ref, p_ref, b_ref, dfl_ref, db_ref, carry_ref):
        i = pl.program_id(0)

        @pl.when(i == 0)
        def _():
            carry_ref[...] = jnp.zeros_like(carry_ref)
            db_ref[...] = jnp.zeros_like(db_ref)

        dcv = dc_ref[...]
        dlogf = jnp.dot(_tri(tm, False), dcv, precision=lax.Precision.HIGHEST, preferred_element_type=F32) + carry_ref[...]
        carry_ref[...] += jnp.sum(dcv, axis=0, keepdims=True)
        z = p_ref[...] + b_ref[...]
        dfl = dlogf * (1.0 / (1.0 + jnp.exp(z)))
        dfl_ref[...] = dfl.astype(BF16)
        db_ref[...] += jnp.sum(dfl, axis=0, keepdims=True)

    return pl.pallas_call(
        body, name="fox_gate_bwd", grid=(nt,),
        in_specs=[pl.BlockSpec((tm, 128), lambda i: (nt - 1 - i, 0)),
                  pl.BlockSpec((tm, 128), lambda i: (nt - 1 - i, FL0 // 128)), _full((1, 128))],
        out_specs=[pl.BlockSpec((tm, 128), lambda i: (nt - 1 - i, 0)), _full((1, 128))],
        out_shape=[jax.ShapeDtypeStruct((s, 128), BF16), jax.ShapeDtypeStruct((1, 128), F32)],
        scratch_shapes=[pltpu.VMEM((1, 128), F32)],
        compiler_params=_params(("arbitrary",)),
    )(dc, proj, b_forget128)


def _proj_pre_bwd(dproj, proj, dqf, dkf, dvf, dqm, dqa, dka, dva, dfl, gq_fox, gk_fox, gq_mem, gq_swa, gk_swa):
    s = proj.shape[0]
    tm = min(256, s)

    def body(dp_in, p_ref, dqf_ref, dkf_ref, dvf_ref, dqm_ref, dqa_ref, dka_ref, dva_ref, dfl_ref,
             gqf, gkf, gqm, gqa, gka, dp_ref, dgn_ref):
        i = pl.program_id(0)

        @pl.when(i == 0)
        def _():
            dgn_ref[...] = jnp.zeros_like(dgn_ref)

        def norm_bwd(off, width, hd, g_ref, dn_ref, slot):
            acc = jnp.zeros((1, 128), F32)
            for b in range(width // 128):
                v = p_ref[:, off + b * 128: off + (b + 1) * 128]
                r = lax.rsqrt(_group_mean(v * v, hd) + EPS)
                n = v * r
                dn = dn_ref[:, b * 128:(b + 1) * 128]
                acc = acc + jnp.sum(dn * n, axis=0, keepdims=True)
                dng = dn * g_ref[...]
                dp_ref[:, off + b * 128: off + (b + 1) * 128] = (r * (dng - n * _group_mean(dng * n, hd))).astype(BF16)
            dgn_ref[slot:slot + 1, :] += acc

        norm_bwd(H_QF, 512, HEAD, gqf, dqf_ref, 0)
        norm_bwd(H_KF, 512, HEAD, gkf, dkf_ref, 1)
        dp_ref[:, H_VF:H_VF + 512] = dvf_ref[...].astype(BF16)
        norm_bwd(H_QM, 512, MEM_HEAD, gqm, dqm_ref, 2)
        norm_bwd(H_QA, 512, HEAD, gqa, dqa_ref, 3)
        norm_bwd(H_KA, 128, HEAD, gka, dka_ref, 4)
        dp_ref[:, H_VA:H_VA + 128] = dva_ref[...].astype(BF16)
        dp_ref[:, H_FL:H_FL + 128] = dfl_ref[...]
        dp_ref[:, H_FL + 128:HALF_W] = jnp.zeros((tm, HALF_W - H_FL - 128), BF16)

    row = lambda w: pl.BlockSpec((tm, w), lambda i: (i, 0))
    g_spec = _full((1, 128))
    return pl.pallas_call(
        body, name="proj_pre_bwd", grid=(s // tm,),
        in_specs=[pl.BlockSpec(memory_space=pl.ANY), pl.BlockSpec((tm, HALF_W), lambda i: (i, 1)),
                  row(512), row(512), row(512), row(512), row(512), row(128), row(128), row(128),
                  g_spec, g_spec, g_spec, g_spec, g_spec],
        out_specs=[pl.BlockSpec((tm, HALF_W), lambda i: (i, 1)), _full((8, 128))],
        out_shape=[jax.ShapeDtypeStruct((s, PROJ_W), BF16), jax.ShapeDtypeStruct((8, 128), F32)],
        input_output_aliases={0: 0},
        compiler_params=_params(("arbitrary",)),
    )(dproj, proj, dqf, dkf, dvf, dqm, dqa, dka, dva, dfl, gq_fox, gk_fox, gq_mem, gq_swa, gk_swa)


def _in_bwd_x(dproj, w_in_p, x, g_mix, dx1):
    s = x.shape[0]
    tm, tk = min(512, s), 1536
    nk = PROJ_W // tk

    def body(dp_ref, w_ref, x_ref, g_ref, dx1_ref, gx_ref, dg_ref, acc_ref):
        i, kk = pl.program_id(0), pl.program_id(1)

        @pl.when((i == 0) & (kk == 0))
        def _():
            dg_ref[...] = jnp.zeros_like(dg_ref)

        prod = _dot(dp_ref[...], w_ref[...], NT)

        @pl.when(kk == 0)
        def _():
            acc_ref[...] = prod

        @pl.when(kk > 0)
        def _():
            acc_ref[...] += prod

        @pl.when(kk == nk - 1)
        def _():
            dx, dg = _rms_bwd(x_ref[...], g_ref[...], acc_ref[...], dx1_ref[...])
            gx_ref[...] = dx
            dg_ref[...] += dg

    row = pl.BlockSpec((tm, D_MODEL), lambda i, kk: (i, 0))
    return pl.pallas_call(
        body, name="in_bwd_x", grid=(s // tm, nk),
        in_specs=[pl.BlockSpec((tm, tk), lambda i, kk: (i, kk)), pl.BlockSpec((D_MODEL, tk), lambda i, kk: (0, kk)),
                  row, _full((1, D_MODEL)), row],
        out_specs=[row, _full((1, D_MODEL))],
        out_shape=[jax.ShapeDtypeStruct((s, D_MODEL), F32), jax.ShapeDtypeStruct((1, D_MODEL), F32)],
        scratch_shapes=[pltpu.VMEM((tm, D_MODEL), F32)],
        compiler_params=_params(("arbitrary", "arbitrary")),
    )(dproj, w_in_p, x, g_mix, dx1)


def _rel_bias_bwd(dbias, bucket):
    def body(db_ref, bk_ref, o_ref):
        bk = bk_ref[...]
        lane = lax.broadcasted_iota(jnp.int32, (1, 128), 1)
        for b in range(REL_BUCKETS):
            sel = bk == b
            acc = jnp.zeros((1, 128), F32)
            for h in range(SWA_HEADS):
                tot = jnp.sum(jnp.sum(jnp.where(sel, db_ref[h], 0.0), axis=-1, keepdims=True), axis=0, keepdims=True)
                acc = jnp.where(lane == h, tot, acc)
            o_ref[b:b + 1, :] = acc

    return pl.pallas_call(
        body, name="rel_bias_bwd",
        out_shape=jax.ShapeDtypeStruct((REL_BUCKETS, 128), F32),
        compiler_params=pltpu.CompilerParams(vmem_limit_bytes=VMEM_LIMIT),
    )(dbias, bucket)


def _my_place():
    return lax.axis_index("x"), lax.axis_index("y"), lax.axis_index("c")


def _peer(place, k):
    x, y, c = place
    return (1 - x if k & 4 else x, 1 - y if k & 2 else y, 1 - c if k & 1 else c)


def _index(place):
    x, y, c = place
    return 4 * x + 2 * y + c


def _all_gather(shard, name):
    def body(x_ref, out_ref, send_sems, recv_sems, local_sem):
        me = _my_place()
        mine = pltpu.make_async_copy(x_ref, out_ref.at[_index(me)], local_sem)
        mine.start()
        sends = []
        for k in range(1, N_DEV):
            cp = pltpu.make_async_remote_copy(
                src_ref=x_ref, dst_ref=out_ref.at[_index(me)], send_sem=send_sems.at[k - 1], recv_sem=recv_sems.at[k - 1],
                device_id=_peer(me, k), device_id_type=MESH)
            cp.start()
            sends.append(cp)
        for k in range(1, N_DEV):
            peer = _peer(me, k)
            pltpu.make_async_remote_copy(
                src_ref=x_ref, dst_ref=out_ref.at[_index(peer)], send_sem=send_sems.at[k - 1], recv_sem=recv_sems.at[k - 1],
                device_id=peer, device_id_type=MESH).wait_recv()
        for cp in sends:
            cp.wait_send()
        mine.wait()

    any_spec = pl.BlockSpec(memory_space=pl.ANY)
    return pl.pallas_call(
        body, name=name, in_specs=[any_spec], out_specs=any_spec,
        out_shape=jax.ShapeDtypeStruct((N_DEV,) + shard.shape, shard.dtype),
        scratch_shapes=[pltpu.SemaphoreType.DMA((N_DEV - 1,)), pltpu.SemaphoreType.DMA((N_DEV - 1,)), pltpu.SemaphoreType.DMA(())],
    )(shard)


HBM_SPEC = pl.BlockSpec(memory_space=pltpu.HBM)
SEM_SPEC = pl.BlockSpec(memory_space=pltpu.SEMAPHORE)
DATAFLOW = pltpu.SideEffectType.DATAFLOW_SIDE_EFFECTING
PEER_SEMS = pltpu.SemaphoreType.DMA((N_DEV - 1,))


def _split_copy(src_ref, land_ref, send_sems, recv_sems, me, k, gather):
    peer = _peer(me, k)
    if gather:
        src, dst = src_ref, land_ref.at[_index(me)]
    else:
        src, dst = src_ref.at[_index(peer)], land_ref.at[k - 1]
    return pltpu.make_async_remote_copy(src_ref=src, dst_ref=dst, send_sem=send_sems.at[k - 1], recv_sem=recv_sems.at[k - 1],
                                        device_id=peer, device_id_type=MESH)


def _split_start(src, slots, gather, name):
    def body(src_ref, land_ref, send_sems, recv_sems, src_thru, land_thru, token):
        me = _my_place()
        for k in range(1, N_DEV):
            _split_copy(src_ref, land_ref, send_sems, recv_sems, me, k, gather).start()
        token[...] = jnp.zeros_like(token)

    chunk = src.shape if gather else src.shape[1:]
    land = lax.empty((slots,) + chunk, src.dtype)
    return pl.pallas_call(
        body, name=name,
        out_shape=(PEER_SEMS, PEER_SEMS, pltpu.HBM(src.shape, src.dtype), pltpu.HBM(land.shape, land.dtype),
                   jax.ShapeDtypeStruct((8, 128), F32)),
        in_specs=(HBM_SPEC, HBM_SPEC),
        out_specs=(SEM_SPEC, SEM_SPEC, HBM_SPEC, HBM_SPEC, pl.BlockSpec(memory_space=pltpu.VMEM)),
        input_output_aliases={0: 2, 1: 3},
        compiler_params=pltpu.CompilerParams(has_side_effects=DATAFLOW),
    )(pltpu.with_memory_space_constraint(src, pltpu.HBM), pltpu.with_memory_space_constraint(land, pltpu.HBM))


def _split_wait(started, after, gather, name):
    send_sems, recv_sems, src_thru, land_thru, _ = started

    def body(src_ref, land_ref, send_sems, recv_sems, after_ref, src_out, land_out):
        me = _my_place()
        for k in range(1, N_DEV):
            cp = _split_copy(src_ref, land_ref, send_sems, recv_sems, me, k, gather)
            cp.wait_send()
            cp.wait_recv()

    return pl.pallas_call(
        body, name=name,
        out_shape=(pltpu.HBM(src_thru.shape, src_thru.dtype), pltpu.HBM(land_thru.shape, land_thru.dtype)),
        in_specs=(HBM_SPEC, HBM_SPEC, SEM_SPEC, SEM_SPEC, pl.BlockSpec(memory_space=pl.ANY)),
        out_specs=(HBM_SPEC, HBM_SPEC), input_output_aliases={0: 0, 1: 1},
        compiler_params=pltpu.CompilerParams(has_side_effects=DATAFLOW),
    )(src_thru, land_thru, send_sems, recv_sems, after)


def _adam_math(w, g, m, v):
    m2 = ADAM_B1 * m + (1.0 - ADAM_B1) * g
    v2 = ADAM_B2 * v + (1.0 - ADAM_B2) * (g * g)
    m_hat = m2 / (1.0 - ADAM_B1 ** ADAM_STEP)
    v_hat = v2 / (1.0 - ADAM_B2 ** ADAM_STEP)
    delta = -ADAM_LR * (m_hat / (jnp.sqrt(v_hat) + ADAM_EPS) + ADAM_WD * w)
    return delta, m2, v2


def _adamw(own, land, w, m, v, name):
    a, b = w.shape
    bp = own.shape[1]
    ta = min(128, a)

    def body(o_ref, p_ref, w_ref, m_ref, v_ref, g_ref, d_ref, m2_ref, v2_ref):
        g = o_ref[:, 0:b].astype(F32)
        for k in range(N_DEV - 1):
            g = g + p_ref[k, :, 0:b].astype(F32)
        delta, m2, v2 = _adam_math(w_ref[...], g, m_ref[...], v_ref[...])
        g_ref[...] = g
        d_ref[...] = delta
        m2_ref[...] = m2
        v2_ref[...] = v2

    blk = pl.BlockSpec((ta, b), lambda i: (i, 0))
    sd = jax.ShapeDtypeStruct((a, b), F32)
    return pl.pallas_call(
        body, name=name, grid=(a // ta,),
        in_specs=[pl.BlockSpec((ta, bp), lambda i: (i, 0)), pl.BlockSpec((N_DEV - 1, ta, bp), lambda i: (0, i, 0)), blk, blk, blk],
        out_specs=[blk, blk, blk, blk], out_shape=[sd, sd, sd, sd],
        compiler_params=_params(("parallel",)),
    )(own, land, w, m, v)


def _bucket_table():
    t_loc = jnp.arange(SWA_BLOCK)[:, None] + SWA_BLOCK
    s_loc = jnp.arange(2 * SWA_BLOCK)[None, :]
    dist = t_loc - s_loc
    max_exact = REL_BUCKETS // 2
    d = jnp.maximum(dist, 0)
    df = jnp.maximum(d, 1).astype(F32)
    large = max_exact + (jnp.log(df / max_exact) / math.log(REL_MAX_DIST / max_exact) * (REL_BUCKETS - max_exact)).astype(jnp.int32)
    large = jnp.minimum(large, REL_BUCKETS - 1)
    bucket = jnp.where(d < max_exact, d, large)
    band = (dist >= 0) & (dist < SWA_BLOCK)
    return bucket, band


def _tile2(g):
    return jnp.concatenate([g, g], axis=1) if g.shape[1] == HEAD else g


SHARD_W = 737
SHARD_WP = 768
IN_WIDTH = N_DEV * SHARD_W
SEGMENTS = ((GL0, 2824, 3072), (QF0, 768, 512), (KF0, 1280, 512), (VF0, 1792, 512), (QM0, 2312, 512),
            (QA0, 0, 512), (KA0, 512, 128), (VA0, 640, 128), (FL0, 2304, 8))


def _lane_plan(sources):
    plan = []
    for t in range(len(sources) // 128):
        groups = {}
        for lane in range(128):
            src = sources[128 * t + lane]
            if src is not None:
                slab, col = src
                groups.setdefault((slab, col // 128, (lane - col) % 128), []).append(lane)
        tile = []
        for key, lanes in groups.items():
            assert lanes == list(range(lanes[0], lanes[-1] + 1))
            tile.append((key, lanes[0], lanes[-1] + 1))
        plan.append(tile)
    return plan


def _assemble(tile_plan, load, rows):
    lane = lax.broadcasted_iota(jnp.int32, (1, 128), 1)
    out = jnp.zeros((rows, 128), F32)
    for (slab, st, roll), lo, hi in tile_plan:
        v = load(slab, st)
        if roll:
            v = pltpu.roll(v, roll, 1)
        out = v if (lo, hi) == (0, 128) else jnp.where((lane >= lo) & (lane < hi), v, out)
    return out


def _w_in_from_shards(land, own):
    ref_col = [None] * PROJ_W
    for p0, r0, n in SEGMENTS:
        for i in range(n):
            ref_col[p0 + i] = divmod(r0 + i, SHARD_W)
    plan = _lane_plan(ref_col)
    d_model = own.shape[0]
    tm = 256

    def body(land_ref, own_ref, o_ref):
        me = _index(_my_place())

        def load(slab, st):
            cols = slice(st * 128, (st + 1) * 128)
            return jnp.where(me == slab, own_ref[:, cols], land_ref[slab, :, cols]).astype(F32)

        for t, tile_plan in enumerate(plan):
            o_ref[:, t * 128:(t + 1) * 128] = _assemble(tile_plan, load, tm).astype(BF16)

    return pl.pallas_call(
        body, name="w_in_from_shards", grid=(d_model // tm,),
        in_specs=[pl.BlockSpec((N_DEV, tm, SHARD_WP), lambda i: (0, i, 0)), pl.BlockSpec((tm, SHARD_WP), lambda i: (i, 0))],
        out_specs=pl.BlockSpec((tm, PROJ_W), lambda i: (i, 0)),
        out_shape=jax.ShapeDtypeStruct((d_model, PROJ_W), BF16),
        compiler_params=_params(("parallel",)),
    )(land, own)


def _dw_in_to_parts(dwp):
    padded_col = [None] * IN_WIDTH
    for p0, r0, n in SEGMENTS:
        for i in range(n):
            padded_col[r0 + i] = p0 + i
    sources = []
    for d in range(N_DEV):
        sources += [(0, padded_col[SHARD_W * d + c]) if c < SHARD_W else None for c in range(SHARD_WP)]
    plan = _lane_plan(sources)
    d_model = dwp.shape[0]
    tm = 256
    tiles = SHARD_WP // 128

    def body(dw_ref, o_ref):
        load = lambda slab, st: dw_ref[:, st * 128:(st + 1) * 128]
        for t, tile_plan in enumerate(plan):
            d, c = divmod(t, tiles)
            o_ref[d, :, c * 128:(c + 1) * 128] = _assemble(tile_plan, load, tm).astype(BF16)

    return pl.pallas_call(
        body, name="dw_in_to_parts", grid=(d_model // tm,),
        in_specs=[pl.BlockSpec((tm, PROJ_W), lambda i: (i, 0))],
        out_specs=pl.BlockSpec((N_DEV, tm, SHARD_WP), lambda i: (0, i, 0)),
        out_shape=jax.ShapeDtypeStruct((N_DEV, d_model, SHARD_WP), BF16),
        compiler_params=_params(("parallel",)),
    )(dwp)


def _cast_shards(shards):
    names = list(shards)

    def body(*refs):
        for src, dst in zip(refs[:len(names)], refs[len(names):]):
            if dst.shape != src.shape:
                dst[...] = jnp.zeros(dst.shape, BF16)
                dst[:, 0:src.shape[1]] = src[...].astype(BF16)
            else:
                dst[...] = src[...].astype(BF16)

    out_shape = [jax.ShapeDtypeStruct((shards[n].shape[0], SHARD_WP if n == "w_in" else shards[n].shape[1]), BF16)
                 for n in names]
    outs = pl.pallas_call(body, name="cast_shards", out_shape=out_shape,
                          compiler_params=pltpu.CompilerParams(vmem_limit_bytes=VMEM_LIMIT))(*[shards[n] for n in names])
    return dict(zip(names, outs))


def _tie(x, *tokens):
    for t in tokens:
        if t is not None:
            x = x + t[0:1, 0:1]
    return x


def _local_step(x, mem, target, p, getw, emit, deps=()):
    s = x.shape[0]
    bucket, band = _bucket_table()
    bucket_m = jnp.where(band, bucket, -1).astype(jnp.int32)
    bias = _bias_table(p["rel_bias"], bucket_m)
    gqf, gkf, gqa, gka = _tile2(p["qn_fox"]), _tile2(p["kn_fox"]), _tile2(p["qn_swa"]), _tile2(p["kn_swa"])
    gqm = p["qn_mem"]
    bf128 = jnp.pad(p["b_forget"], ((0, 0), (0, 120)))
    sink = p["sink_swa"].reshape(8)

    h = _rms_fwd(x, p["g_mix"], "rms_mix", deps)
    w_in = getw("w_in", h)
    proj = _mm(h, w_in, "nn", F32, 512, 1536, 1024, "proj")
    qf, kf, vf, qm, qa, ka, va = _proj_post(proj, gqf, gkf, gqm, gqa, gka)
    cc4 = _fox_gate_fwd(proj, bf128)
    w_kv = getw("w_mem_kv", cc4)
    mem_n, kv_raw, mk, mv = _memkv_fwd(mem, p["g_mem"], w_kv, p["kn_mem"])
    kp = jnp.pad(ka, ((SWA_BLOCK, 0), (0, 0)))
    vp = jnp.pad(va, ((SWA_BLOCK, 0), (0, 0)))
    oa = _swa_fwd(qa, kp, vp, bias, sink)
    of, lse4 = _fox_fwd(qf, kf, jnp.transpose(vf), cc4)
    om = _mem_fwd(qm, mk, mv)
    wa, wf, wm, w_out = getw("w_o_swa", oa), getw("w_o_fox", oa), getw("w_o_mem", oa), getw("w_out", oa)
    x1, hm, merged = _merge_fwd(x, oa, of, om, proj, p["b_gate"], wa, wf, wm, w_out, p["g_mlp"])
    w_up = getw("w_mlp_up", of)
    u = _mlp_up(hm, w_up)
    w_down = getw("w_mlp_down", hm)
    dy, dy_b, loss = _mlp_down_loss(u, w_down, x1, target)

    da = _mlp_bwd_act(dy_b, w_down, u)
    t_down = emit("w_mlp_down", _mm(u, dy_b, "tn", BF16, 1024, 1024, 512, "dw_down"))
    dx1, dg_mlp = _mlp_bwd_x(da, w_up, x1, dy, _tie(p["g_mlp"], t_down))
    t_up = emit("w_mlp_up", _mm(hm, da, "tn", BF16, 1024, 1024, 512, "dw_up", column_chunks=True))
    dproj, doa, dof, dom, dya, dyf, dym, db_gate = _merge_bwd(
        dx1, oa, of, om, proj, _tie(p["b_gate"], t_up), wa, wf, wm, w_out)
    t_o = (emit("w_out", _mm(merged, dx1, "tn", BF16, 512, 1024, 512, "dw_out")),
           emit("w_o_swa", _mm(oa, dya, "tn", BF16, 512, 1024, 512, "dw_o_swa")),
           emit("w_o_fox", _mm(of, dyf, "tn", BF16, 512, 1024, 512, "dw_o_fox")),
           emit("w_o_mem", _mm(om, dym, "tn", BF16, 512, 1024, 512, "dw_o_mem")))

    dqm, dmk, dmv = _mem_bwd(qm, mk, mv, dom)
    dw_kv, dkn_mem, dg_mem = _memkv_bwd(dmk, dmv, kv_raw, _tie(p["kn_mem"], *t_o), mem, p["g_mem"], mem_n, w_kv)
    t_kv = emit("w_mem_kv", dw_kv)
    dqa, dkp, dvp, dbias, dsink = _swa_bwd(qa, kp, vp, bias, _tie(p["sink_swa"], t_kv).reshape(8), doa)
    dqf_t, dkf, dvf, dck4, dcq4 = _fox_bwd(qf, kf, vf, dof, of, cc4, lse4)
    dqf = jnp.transpose(dqf_t)

    dcq = jnp.transpose(dcq4[:, 0:2, :], (2, 0, 1)).reshape(s, 8)
    dck = jnp.transpose(dck4[:, :, 0:2], (1, 0, 2)).reshape(s, 8)
    dc = jnp.pad(dcq - dck, ((0, 0), (0, 120)))
    dfl, db_forget = _fox_gate_bwd(dc, proj, bf128)

    dproj, dgn = _proj_pre_bwd(dproj, proj, dqf, dkf, dvf, dqm, dqa, dkp[SWA_BLOCK:], dvp[SWA_BLOCK:], dfl,
                               gqf, gkf, gqm, gqa, gka)
    t_in = emit("w_in", _mm(h, dproj, "tn", F32, 1024, 1536, 512, "dw_in"))
    grad_x, dg_mix = _in_bwd_x(dproj, w_in, x, _tie(p["g_mix"], t_in), dx1)
    d_rel = _rel_bias_bwd(dbias, bucket_m)

    fold = lambda r: dgn[r:r + 1, 0:HEAD] + dgn[r:r + 1, HEAD:128]
    small = {
        "g_mix": dg_mix, "b_gate": db_gate, "b_forget": db_forget[:, 0:8],
        "qn_swa": fold(3), "kn_swa": fold(4), "sink_swa": dsink[:, 0:8], "rel_bias": d_rel[:, 0:8],
        "qn_fox": fold(0), "kn_fox": fold(1), "g_mem": dg_mem, "qn_mem": dgn[2:3, :], "kn_mem": dkn_mem,
        "g_mlp": dg_mlp,
    }
    return loss, grad_x, small


SMALL = ("g_mix", "b_gate", "b_forget", "qn_swa", "kn_swa", "sink_swa", "rel_bias", "qn_fox", "kn_fox", "g_mem",
         "qn_mem", "kn_mem", "g_mlp")
BIG = ("w_in", "w_mem_kv", "w_o_swa", "w_o_fox", "w_o_mem", "w_out", "w_mlp_up", "w_mlp_down")
COL_SHARDED = ("w_in", "w_o_swa", "w_o_fox", "w_o_mem", "w_mlp_up")
WEIGHTS = ("g_mix", "w_in", "b_gate", "b_forget", "qn_swa", "kn_swa", "sink_swa", "rel_bias", "qn_fox", "kn_fox", "g_mem",
           "w_mem_kv", "qn_mem", "kn_mem", "w_o_swa", "w_o_fox", "w_o_mem", "w_out", "g_mlp", "w_mlp_up", "w_mlp_down")
SMALL_PAD = 7168


def _gathered_to_full(name, g):
    if name in COL_SHARDED:
        return jnp.transpose(g, (1, 0, 2)).reshape(g.shape[1], N_DEV * g.shape[2])
    return g.reshape(N_DEV * g.shape[1], g.shape[2])


def _full_to_parts(name, full, b):
    if name in COL_SHARDED:
        return jnp.transpose(full.reshape(full.shape[0], N_DEV, b), (1, 0, 2)).astype(BF16)
    return full.reshape(N_DEV, full.shape[0] // N_DEV, full.shape[1]).astype(BF16)


def _pack_small(d):
    flat = jnp.concatenate([d[n].reshape(-1) for n in SMALL])
    return jnp.pad(flat, (0, SMALL_PAD - flat.shape[0])).reshape(8, SMALL_PAD // 8)


def _unpack_small(packed, like):
    flat = packed.reshape(-1)
    out, off = {}, 0
    for n in SMALL:
        size = like[n].size
        out[n] = flat[off:off + size].reshape(like[n].shape)
        off += size
    return out


def _adamw_small(parts, w, m, v):
    def body(p_ref, w_ref, m_ref, v_ref, g_ref, d_ref, m2_ref, v2_ref):
        g = p_ref[0]
        for k in range(1, N_DEV):
            g = g + p_ref[k]
        delta, m2, v2 = _adam_math(w_ref[...], g, m_ref[...], v_ref[...])
        g_ref[...] = g
        d_ref[...] = delta
        m2_ref[...] = m2
        v2_ref[...] = v2

    sd = jax.ShapeDtypeStruct(w.shape, F32)
    return pl.pallas_call(body, name="adamw_small", out_shape=[sd, sd, sd, sd])(parts, w, m, v)


def kernel(x, mem, g_mix, w_in, b_gate, b_forget, qn_swa, kn_swa, sink_swa, rel_bias, qn_fox, kn_fox, g_mem, w_mem_kv, qn_mem, kn_mem, w_o_swa, w_o_fox, w_o_mem, w_out, g_mlp, w_mlp_up, w_mlp_down, loss_target, m_g_mix, m_w_in, m_b_gate, m_b_forget, m_qn_swa, m_kn_swa, m_sink_swa, m_rel_bias, m_qn_fox, m_kn_fox, m_g_mem, m_w_mem_kv, m_qn_mem, m_kn_mem, m_w_o_swa, m_w_o_fox, m_w_o_mem, m_w_out, m_g_mlp, m_w_mlp_up, m_w_mlp_down, v_g_mix, v_w_in, v_b_gate, v_b_forget, v_qn_swa, v_kn_swa, v_sink_swa, v_rel_bias, v_qn_fox, v_kn_fox, v_g_mem, v_w_mem_kv, v_qn_mem, v_kn_mem, v_w_o_swa, v_w_o_fox, v_w_o_mem, v_w_out, v_g_mlp, v_w_mlp_up, v_w_mlp_down):
    wts = dict(g_mix=g_mix, w_in=w_in, b_gate=b_gate, b_forget=b_forget, qn_swa=qn_swa, kn_swa=kn_swa, sink_swa=sink_swa,
               rel_bias=rel_bias, qn_fox=qn_fox, kn_fox=kn_fox, g_mem=g_mem, w_mem_kv=w_mem_kv, qn_mem=qn_mem, kn_mem=kn_mem,
               w_o_swa=w_o_swa, w_o_fox=w_o_fox, w_o_mem=w_o_mem, w_out=w_out, g_mlp=g_mlp, w_mlp_up=w_mlp_up,
               w_mlp_down=w_mlp_down)
    mom = dict(g_mix=m_g_mix, w_in=m_w_in, b_gate=m_b_gate, b_forget=m_b_forget, qn_swa=m_qn_swa, kn_swa=m_kn_swa,
               sink_swa=m_sink_swa, rel_bias=m_rel_bias, qn_fox=m_qn_fox, kn_fox=m_kn_fox, g_mem=m_g_mem, w_mem_kv=m_w_mem_kv,
               qn_mem=m_qn_mem, kn_mem=m_kn_mem, w_o_swa=m_w_o_swa, w_o_fox=m_w_o_fox, w_o_mem=m_w_o_mem, w_out=m_w_out,
               g_mlp=m_g_mlp, w_mlp_up=m_w_mlp_up, w_mlp_down=m_w_mlp_down)
    var = dict(g_mix=v_g_mix, w_in=v_w_in, b_gate=v_b_gate, b_forget=v_b_forget, qn_swa=v_qn_swa, kn_swa=v_kn_swa,
               sink_swa=v_sink_swa, rel_bias=v_rel_bias, qn_fox=v_qn_fox, kn_fox=v_kn_fox, g_mem=v_g_mem, w_mem_kv=v_w_mem_kv,
               qn_mem=v_qn_mem, kn_mem=v_kn_mem, w_o_swa=v_w_o_swa, w_o_fox=v_w_o_fox, w_o_mem=v_w_o_mem, w_out=v_w_out,
               g_mlp=v_g_mlp, w_mlp_up=v_w_mlp_up, w_mlp_down=v_w_mlp_down)

    me = _index(_my_place())
    dev = lax.broadcasted_iota(jnp.int32, (N_DEV, 1, 1), 0)

    shards = _cast_shards({n: wts[n][0] for n in BIG})
    gathers = {n: _split_start(shards[n], N_DEV, True, "ag_start_" + n) for n in BIG}
    full = {}

    def getw(n, after):
        if n not in full:
            _, land = _split_wait(gathers[n], after, True, "ag_wait_" + n)
            if n == "w_in":
                full[n] = _w_in_from_shards(land, shards[n])
            else:
                w = jnp.where(dev == me, shards[n][None], land)
                full[n] = w if n == "w_mlp_up" else _gathered_to_full(n, w)
        return full[n]

    exchanges = {}

    def emit(n, grad):
        if n == "w_in":
            parts = _dw_in_to_parts(grad)
        else:
            parts = grad if n == "w_mlp_up" else _full_to_parts(n, grad, wts[n].shape[2])
        exchanges[n] = _split_start(parts, N_DEV - 1, False, "rs_start_" + n)
        return exchanges[n][4]

    small_p = {n: wts[n] for n in SMALL}
    loss, grad_x, small_g = _local_step(x[0], mem[0], loss_target[0], small_p, getw, emit,
                                        tuple(gathers[n][4] for n in BIG))

    grads, delta, new_m, new_v = {}, {}, {}, {}
    after = grad_x

    def update(n, after):
        parts, land = _split_wait(exchanges[n], after, False, "rs_wait_" + n)
        own = lax.dynamic_index_in_dim(parts, me, 0, keepdims=False)
        g, d, m2, v2 = _adamw(own, land, wts[n][0], mom[n][0], var[n][0], "adamw_" + n)
        grads[n], delta[n], new_m[n], new_v[n] = g[None], d[None], m2[None], v2[None]
        return d

    for n in exchanges:
        if n != "w_in":
            after = update(n, after)

    gathered = _all_gather(_pack_small(small_g), "ag_small")
    g, d, m2, v2 = _adamw_small(gathered, _pack_small(small_p), _pack_small({n: mom[n] for n in SMALL}),
                                _pack_small({n: var[n] for n in SMALL}))
    for dst, packed in ((grads, g), (delta, d), (new_m, m2), (new_v, v2)):
        dst.update(_unpack_small(packed, small_p))
    update("w_in", after)

    total = lax.psum(loss[0, 0], ("x", "y", "c"))
    return (total, grad_x[None], *[grads[n] for n in WEIGHTS], *[delta[n] for n in WEIGHTS],
            *[new_m[n] for n in WEIGHTS], *[new_v[n] for n in WEIGHTS])
```

```python
import functools
import math

import jax
import jax.numpy as jnp
from jax import lax
from jax.experimental import pallas as pl
from jax.experimental.pallas import tpu as pltpu

F32 = jnp.float32
BF16 = jnp.bfloat16

D_MODEL = 1024
N_MEM = 256
D_FF = 4096
HEAD = 64
SWA_HEADS = 8
SWA_BLOCK = 128
MEM_HEADS = 4
MEM_HEAD = 128
EPS = 1e-6
NEG = -1e30
REL_BUCKETS = 32
REL_MAX_DIST = 128

ADAM_LR = 0.001
ADAM_B1 = 0.9
ADAM_B2 = 0.999
ADAM_EPS = 1e-08
ADAM_WD = 0.01
ADAM_STEP = 10

GL0, QF0, KF0, VF0, QM0, QA0, KA0, VA0, FL0 = 0, 3072, 3584, 4096, 4608, 5120, 5632, 5760, 5888
PROJ_W = 6144
HALF_W = 3072
H_QF, H_KF, H_VF, H_QM, H_QA, H_KA, H_VA, H_FL = 0, 512, 1024, 1536, 2048, 2560, 2688, 2816

VMEM_LIMIT = 56 * 1024 * 1024
N_DEV = 8
MESH = pl.DeviceIdType.MESH

NN = (((1,), (0,)), ((), ()))
NT = (((1,), (1,)), ((), ()))
TN = (((0,), (0,)), ((), ()))


def _dot(a, b, dims=NN):
    return lax.dot_general(a, b, dims, preferred_element_type=F32)


def _params(sem):
    return pltpu.CompilerParams(dimension_semantics=sem, vmem_limit_bytes=VMEM_LIMIT)


def _full(shape):
    nd = len(shape)
    return pl.BlockSpec(shape, lambda *_: (0,) * nd)


def _sigmoid(z):
    return 1.0 / (1.0 + jnp.exp(-z))


def _group_mean(v, hd):
    if hd == 128:
        return jnp.mean(v, axis=-1, keepdims=True)
    lane = lax.broadcasted_iota(jnp.int32, v.shape, 1)
    lo = lane < HEAD
    s_lo = jnp.sum(jnp.where(lo, v, 0.0), axis=-1, keepdims=True)
    s_hi = jnp.sum(jnp.where(lo, 0.0, v), axis=-1, keepdims=True)
    return jnp.where(lo, s_lo, s_hi) * (1.0 / HEAD)


def _mm(a, b, mode, out_dtype, tm, tn, tk, name, column_chunks=False):
    if mode == "nn":
        m, k = a.shape
        n = b.shape[1]
    elif mode == "nt":
        m, k = a.shape
        n = b.shape[0]
    else:
        k, m = a.shape
        n = b.shape[1]
    tm, tn, tk = min(tm, m), min(tn, n), min(tk, k)
    nk = k // tk
    chunk = n // N_DEV
    per_tile = tn // chunk if column_chunks else 1
    dims = {"nn": NN, "nt": NT, "tn": TN}[mode]
    a_spec = pl.BlockSpec((tk, tm), lambda j, i, kk: (kk, i)) if mode == "tn" else pl.BlockSpec((tm, tk), lambda j, i, kk: (i, kk))
    b_spec = pl.BlockSpec((tn, tk), lambda j, i, kk: (j, kk)) if mode == "nt" else pl.BlockSpec((tk, tn), lambda j, i, kk: (kk, j))

    def body(a_ref, b_ref, o_ref, *acc):
        prod = _dot(a_ref[...].astype(BF16), b_ref[...].astype(BF16), dims)

        def write(res):
            if column_chunks:
                for c in range(per_tile):
                    o_ref[c] = res[:, c * chunk:(c + 1) * chunk].astype(o_ref.dtype)
            else:
                o_ref[...] = res.astype(o_ref.dtype)

        if nk == 1:
            write(prod)
        else:
            acc_ref, = acc
            kk = pl.program_id(2)

            @pl.when(kk == 0)
            def _():
                acc_ref[...] = prod

            @pl.when(kk > 0)
            def _():
                acc_ref[...] += prod

            @pl.when(kk == nk - 1)
            def _():
                write(acc_ref[...])

    return pl.pallas_call(
        body, name=name, grid=(n // tn, m // tm, nk),
        in_specs=[a_spec, b_spec],
        out_specs=(pl.BlockSpec((per_tile, tm, chunk), lambda j, i, kk: (j, i, 0)) if column_chunks
                   else pl.BlockSpec((tm, tn), lambda j, i, kk: (i, j))),
        out_shape=jax.ShapeDtypeStruct((N_DEV, m, chunk) if column_chunks else (m, n), out_dtype),
        scratch_shapes=[pltpu.VMEM((tm, tn), F32)] if nk > 1 else [],
        compiler_params=_params(("parallel", "parallel", "arbitrary")),
    )(a, b)


def _rms_fwd(x, g, name, deps=()):
    s, d = x.shape
    tm = min(512, s)

    def body(x_ref, g_ref, *rest):
        h_ref = rest[len(deps)]
        xv = x_ref[...]
        r = lax.rsqrt(jnp.mean(xv * xv, axis=-1, keepdims=True) + EPS)
        h_ref[...] = (xv * r * g_ref[...]).astype(BF16)

    return pl.pallas_call(
        body, name=name, grid=(s // tm,),
        in_specs=[pl.BlockSpec((tm, d), lambda i: (i, 0)), _full((1, d))] + [pl.BlockSpec(memory_space=pl.ANY)] * len(deps),
        out_specs=pl.BlockSpec((tm, d), lambda i: (i, 0)),
        out_shape=jax.ShapeDtypeStruct((s, d), BF16),
        compiler_params=_params(("parallel",)),
    )(x, g, *deps)


def _proj_post(proj, gq_fox, gk_fox, gq_mem, gq_swa, gk_swa):
    s = proj.shape[0]
    tm = min(256, s)

    def body(p_ref, gqf, gkf, gqm, gqa, gka, qf_ref, kf_ref, vf_ref, qm_ref, qa_ref, ka_ref, va_ref):
        def norm(off, width, hd, g_ref, o_ref):
            for b in range(width // 128):
                v = p_ref[:, off + b * 128: off + (b + 1) * 128]
                r = lax.rsqrt(_group_mean(v * v, hd) + EPS)
                o_ref[:, b * 128:(b + 1) * 128] = (v * r * g_ref[...]).astype(BF16)

        norm(H_QF, 512, HEAD, gqf, qf_ref)
        norm(H_KF, 512, HEAD, gkf, kf_ref)
        vf_ref[...] = p_ref[:, H_VF:H_VF + 512].astype(BF16)
        norm(H_QM, 512, MEM_HEAD, gqm, qm_ref)
        norm(H_QA, 512, HEAD, gqa, qa_ref)
        norm(H_KA, 128, HEAD, gka, ka_ref)
        va_ref[...] = p_ref[:, H_VA:H_VA + 128].astype(BF16)

    g_spec = _full((1, 128))
    o512 = pl.BlockSpec((tm, 512), lambda i: (i, 0))
    o128 = pl.BlockSpec((tm, 128), lambda i: (i, 0))
    s512 = jax.ShapeDtypeStruct((s, 512), BF16)
    s128 = jax.ShapeDtypeStruct((s, 128), BF16)
    return pl.pallas_call(
        body, name="proj_post", grid=(s // tm,),
        in_specs=[pl.BlockSpec((tm, HALF_W), lambda i: (i, 1)), g_spec, g_spec, g_spec, g_spec, g_spec],
        out_specs=[o512, o512, o512, o512, o512, o128, o128],
        out_shape=[s512, s512, s512, s512, s512, s128, s128],
        compiler_params=_params(("parallel",)),
    )(proj, gq_fox, gk_fox, gq_mem, gq_swa, gk_swa)


def _tri(n, lower):
    r = lax.broadcasted_iota(jnp.int32, (n, n), 0)
    c = lax.broadcasted_iota(jnp.int32, (n, n), 1)
    return jnp.where((c <= r) if lower else (c >= r), 1.0, 0.0).astype(F32)


def _fox_gate_fwd(proj, b_forget128):
    s = proj.shape[0]
    tm = min(512, s)

    def body(p_ref, b_ref, cc_ref, carry_ref):
        i = pl.program_id(0)

        @pl.when(i == 0)
        def _():
            carry_ref[...] = jnp.zeros_like(carry_ref)

        z = p_ref[...] + b_ref[...]
        logf = jnp.minimum(z, 0.0) - jnp.log(1.0 + jnp.exp(-jnp.abs(z)))
        c = jnp.dot(_tri(tm, True), logf, precision=lax.Precision.HIGHEST, preferred_element_type=F32) + carry_ref[...]
        carry_ref[...] = c[tm - 1:tm, :]
        for hp in range(4):
            cc_ref[hp] = c if hp == 0 else pltpu.roll(c, 128 - 2 * hp, 1)

    return pl.pallas_call(
        body, name="fox_gate_fwd", grid=(s // tm,),
        in_specs=[pl.BlockSpec((tm, 128), lambda i: (i, FL0 // 128)), _full((1, 128))],
        out_specs=pl.BlockSpec((4, tm, 128), lambda i: (0, i, 0)),
        out_shape=jax.ShapeDtypeStruct((4, s, 128), F32),
        scratch_shapes=[pltpu.VMEM((1, 128), F32)],
        compiler_params=_params(("arbitrary",)),
    )(proj, b_forget128)


def _memkv_fwd(mem, g_mem, w_kv, kn_mem):
    m = mem.shape[0]

    def body(mem_ref, g_ref, w_ref, kn_ref, memn_ref, kv_ref, mk_ref, mv_ref):
        xv = mem_ref[...]
        r = lax.rsqrt(jnp.mean(xv * xv, axis=-1, keepdims=True) + EPS)
        mn = (xv * r * g_ref[...]).astype(BF16)
        memn_ref[...] = mn
        kv = _dot(mn, w_ref[...])
        kv_ref[...] = kv
        for h in range(MEM_HEADS):
            v = kv[:, h * 128:(h + 1) * 128]
            rr = lax.rsqrt(jnp.mean(v * v, axis=-1, keepdims=True) + EPS)
            mk_ref[:, h * 128:(h + 1) * 128] = (v * rr * kn_ref[...]).astype(BF16)
        mv_ref[...] = kv[:, 512:1024].astype(BF16)

    return pl.pallas_call(
        body, name="memkv_fwd",
        out_shape=[jax.ShapeDtypeStruct((m, D_MODEL), BF16), jax.ShapeDtypeStruct((m, 1024), F32),
                   jax.ShapeDtypeStruct((m, 512), BF16), jax.ShapeDtypeStruct((m, 512), BF16)],
        compiler_params=pltpu.CompilerParams(vmem_limit_bytes=VMEM_LIMIT),
    )(mem, g_mem, w_kv, kn_mem)


def _bias_table(rel_bias, bucket):
    def body(rb_ref, bk_ref, o_ref):
        bk = bk_ref[...]
        for h in range(SWA_HEADS):
            acc = jnp.zeros(bk.shape, F32)
            for b in range(REL_BUCKETS):
                acc = jnp.where(bk == b, rb_ref[b, h], acc)
            o_ref[h] = acc

    return pl.pallas_call(
        body, name="bias_table",
        in_specs=[pl.BlockSpec(memory_space=pltpu.SMEM), pl.BlockSpec(memory_space=pltpu.VMEM)],
        out_shape=jax.ShapeDtypeStruct((SWA_HEADS,) + bucket.shape, F32),
    )(rel_bias, bucket)


def _swa_valid(n):
    row = lax.broadcasted_iota(jnp.int32, (SWA_BLOCK, 2 * SWA_BLOCK), 0)
    col = lax.broadcasted_iota(jnp.int32, (SWA_BLOCK, 2 * SWA_BLOCK), 1)
    dist = row + SWA_BLOCK - col
    return (dist >= 0) & (dist < SWA_BLOCK) & ((col >= SWA_BLOCK) | (n > 0))


def _swa_fwd(qa, kp, vp, bias, sink):
    s = qa.shape[0]
    nb = s // SWA_BLOCK

    def body(sink_ref, q_ref, kp_ref, vp_ref, bias_ref, o_ref):
        n = pl.program_id(0)
        start = pl.multiple_of(n * SWA_BLOCK, SWA_BLOCK)
        k2 = kp_ref[pl.ds(start, 2 * SWA_BLOCK), :]
        v2 = vp_ref[pl.ds(start, 2 * SWA_BLOCK), :]
        valid = _swa_valid(n)
        for h in range(SWA_HEADS):
            kv = h // 4
            qh = q_ref[:, h * HEAD:(h + 1) * HEAD]
            kh = k2[:, kv * HEAD:(kv + 1) * HEAD]
            vh = v2[:, kv * HEAD:(kv + 1) * HEAD]
            sc = _dot(qh, kh, NT) * 0.125 + bias_ref[h]
            sc = jnp.where(valid, sc, NEG)
            sk = sink_ref[h]
            mx = jnp.maximum(jnp.max(sc, axis=-1, keepdims=True), sk)
            p = jnp.exp(sc - mx)
            den = jnp.sum(p, axis=-1, keepdims=True) + jnp.exp(sk - mx)
            p = p / den
            o_ref[:, h * HEAD:(h + 1) * HEAD] = _dot(p.astype(BF16), vh).astype(BF16)

    return pl.pallas_call(
        body, name="swa_fwd", grid=(nb,),
        in_specs=[pl.BlockSpec(memory_space=pltpu.SMEM),
                  pl.BlockSpec((SWA_BLOCK, 512), lambda n: (n, 0)),
                  _full(kp.shape), _full(vp.shape), _full(bias.shape)],
        out_specs=pl.BlockSpec((SWA_BLOCK, 512), lambda n: (n, 0)),
        out_shape=jax.ShapeDtypeStruct((s, 512), BF16),
        compiler_params=_params(("parallel",)),
    )(sink, qa, kp, vp, bias)


def _head_mask(e):
    lane = lax.broadcasted_iota(jnp.int32, (1, 128), 1)
    return (lane >= e * HEAD) & (lane < (e + 1) * HEAD)


FOX_FWD_TQ, FOX_FWD_TK = 1024, 1024
FOX_BWD_TK, FOX_BWD_TQ = 256, 512


def _head_rows(e):
    row = lax.broadcasted_iota(jnp.int32, (128, 1), 0)
    return (row >= e * HEAD) & (row < (e + 1) * HEAD)


def _fox_fwd(q, k, v_t, cc4):
    s = q.shape[0]
    t = min(FOX_FWD_TQ, s)
    tk = min(FOX_FWD_TK, s)
    nq = s // t

    def body(q_ref, k_ref, vt_ref, cc_ref, o_ref, lse_ref):
        i = pl.program_id(1)
        qs = q_ref[...] * jnp.asarray(0.125, BF16)
        qe = [jnp.where(_head_mask(e), qs, jnp.zeros_like(qs)) for e in range(2)]
        n_full = (i * t) // tk
        krow = lax.broadcasted_iota(jnp.int32, (tk, t), 0) + n_full * tk
        qcol = lax.broadcasted_iota(jnp.int32, (tk, t), 1) + i * t

        def step(j, carry, masked):
            ks = pl.ds(pl.multiple_of(j * tk, tk), tk)
            kj = k_ref[ks, :]
            vtj = vt_ref[:, ks]
            out = []
            for e in range(2):
                m, acc = carry[2 * e], carry[2 * e + 1]
                st = _dot(kj, qe[e], NT) - cc_ref[0, ks, e:e + 1]
                if masked:
                    st = jnp.where(krow <= qcol, st, NEG)
                m_new = jnp.maximum(m, jnp.max(st, axis=0, keepdims=True))
                alpha = jnp.exp(m - m_new)
                pt = jnp.exp(st - m_new).astype(BF16)
                vte = jnp.where(_head_rows(e), vtj, jnp.ones_like(vtj))
                out += [m_new, alpha * acc + _dot(vte, pt)]
            return tuple(out)

        init = (jnp.full((1, t), NEG, F32), jnp.zeros((128, t), F32)) * 2
        carry = lax.fori_loop(0, n_full, functools.partial(step, masked=False), init)
        m0, a0, m1, a1 = step(n_full, carry, True)
        l0 = a0[HEAD:HEAD + 1, :]
        l1 = a1[0:1, :]
        o_t = jnp.where(_head_rows(0), a0 / l0, a1 / l1)
        o_ref[...] = o_t.T.astype(BF16)
        r8 = lax.broadcasted_iota(jnp.int32, (8, t), 0)
        lse_ref[0] = jnp.where(r8 == 0, m0 + jnp.log(l0), jnp.where(r8 == 1, m1 + jnp.log(l1), 0.0))

    return pl.pallas_call(
        body, name="fox_fwd", grid=(4, nq),
        in_specs=[pl.BlockSpec((t, 128), lambda hp, i: (i, hp)),
                  pl.BlockSpec((s, 128), lambda hp, i: (0, hp)),
                  pl.BlockSpec((128, s), lambda hp, i: (hp, 0)),
                  pl.BlockSpec((1, s, 128), lambda hp, i: (hp, 0, 0))],
        out_specs=[pl.BlockSpec((t, 128), lambda hp, i: (i, hp)),
                   pl.BlockSpec((1, 8, t), lambda hp, i: (hp, 0, i))],
        out_shape=[jax.ShapeDtypeStruct((s, 512), BF16), jax.ShapeDtypeStruct((4, 8, s), F32)],
        compiler_params=_params(("parallel", "parallel")),
    )(q, k, v_t, cc4)


MEM_SCALE = MEM_HEAD ** -0.5


def _mem_fwd(qm, mk, mv):
    s = qm.shape[0]
    tq = min(512, s)

    def body(q_ref, mk_ref, mv_ref, o_ref):
        for h in range(MEM_HEADS):
            hs = slice(h * 128, (h + 1) * 128)
            sc = _dot(q_ref[:, hs], mk_ref[:, hs], NT) * MEM_SCALE
            mx = jnp.max(sc, axis=-1, keepdims=True)
            p = jnp.exp(sc - mx)
            p = p / jnp.sum(p, axis=-1, keepdims=True)
            o_ref[:, hs] = _dot(p.astype(BF16), mv_ref[:, hs]).astype(BF16)

    return pl.pallas_call(
        body, name="mem_fwd", grid=(s // tq,),
        in_specs=[pl.BlockSpec((tq, 512), lambda i: (i, 0)), _full(mk.shape), _full(mv.shape)],
        out_specs=pl.BlockSpec((tq, 512), lambda i: (i, 0)),
        out_shape=jax.ShapeDtypeStruct((s, 512), BF16),
        compiler_params=_params(("parallel",)),
    )(qm, mk, mv)


def _merge_fwd(x, oa, of, om, proj, b_gate, wa, wf, wm, w_out, g_mlp):
    s = x.shape[0]
    tm = min(256, s)

    def body(x_ref, oa_ref, of_ref, om_ref, gl_ref, bg_ref, wa_ref, wf_ref, wm_ref, wo_ref, g_ref, x1_ref, hm_ref, mg_ref):
        merged = None
        for b, (o_ref, w_ref) in enumerate(((oa_ref, wa_ref), (of_ref, wf_ref), (om_ref, wm_ref))):
            cs = slice(b * D_MODEL, (b + 1) * D_MODEL)
            y = _dot(o_ref[...], w_ref[...])
            t = _sigmoid(gl_ref[:, cs] + bg_ref[:, cs]) * y
            merged = t if merged is None else merged + t
        mb = merged.astype(BF16)
        mg_ref[...] = mb
        x1 = x_ref[...] + _dot(mb, wo_ref[...])
        x1_ref[...] = x1
        r = lax.rsqrt(jnp.mean(x1 * x1, axis=-1, keepdims=True) + EPS)
        hm_ref[...] = (x1 * r * g_ref[...]).astype(BF16)

    row = lambda w: pl.BlockSpec((tm, w), lambda i: (i, 0))
    return pl.pallas_call(
        body, name="merge_fwd", grid=(s // tm,),
        in_specs=[row(D_MODEL), row(512), row(512), row(512), row(HALF_W), _full((1, HALF_W)),
                  _full(wa.shape), _full(wf.shape), _full(wm.shape), _full(w_out.shape), _full((1, D_MODEL))],
        out_specs=[row(D_MODEL), row(D_MODEL), row(D_MODEL)],
        out_shape=[jax.ShapeDtypeStruct((s, D_MODEL), F32), jax.ShapeDtypeStruct((s, D_MODEL), BF16),
                   jax.ShapeDtypeStruct((s, D_MODEL), BF16)],
        compiler_params=_params(("parallel",)),
    )(x, oa, of, om, proj, b_gate, wa, wf, wm, w_out, g_mlp)


def _mlp_up(hm, w_up):
    s = hm.shape[0]
    tm, tn = min(1024, s), w_up.shape[2]

    def body(h_ref, w_ref, u_ref):
        r = jnp.maximum(_dot(h_ref[...], w_ref[0]), 0.0)
        u_ref[...] = (r * r).astype(BF16)

    return pl.pallas_call(
        body, name="mlp_up", grid=(s // tm, D_FF // tn),
        in_specs=[pl.BlockSpec((tm, D_MODEL), lambda i, j: (i, 0)), pl.BlockSpec((1, D_MODEL, tn), lambda i, j: (j, 0, 0))],
        out_specs=pl.BlockSpec((tm, tn), lambda i, j: (i, j)),
        out_shape=jax.ShapeDtypeStruct((s, D_FF), BF16),
        compiler_params=_params(("parallel", "parallel")),
    )(hm, w_up)


def _mlp_down_loss(u, w_down, x1, target):
    s = u.shape[0]
    tm = min(256, s)

    def body(u_ref, w_ref, x1_ref, t_ref, dy_ref, dyb_ref, loss_ref):
        i = pl.program_id(0)

        @pl.when(i == 0)
        def _():
            loss_ref[...] = jnp.zeros_like(loss_ref)

        y = x1_ref[...] + _dot(u_ref[...], w_ref[...])
        err = y - t_ref[...]
        dy = err * (1.0 / D_MODEL)
        dy_ref[...] = dy
        dyb_ref[...] = dy.astype(BF16)
        part = jnp.sum(jnp.sum(err * err, axis=-1, keepdims=True) * (1.0 / D_MODEL), axis=0, keepdims=True)
        loss_ref[...] += 0.5 * part

    row = pl.BlockSpec((tm, D_MODEL), lambda i: (i, 0))
    return pl.pallas_call(
        body, name="mlp_down_loss", grid=(s // tm,),
        in_specs=[pl.BlockSpec((tm, D_FF), lambda i: (i, 0)), _full(w_down.shape), row, row],
        out_specs=[row, row, _full((1, 1))],
        out_shape=[jax.ShapeDtypeStruct((s, D_MODEL), F32), jax.ShapeDtypeStruct((s, D_MODEL), BF16),
                   jax.ShapeDtypeStruct((1, 1), F32)],
        compiler_params=_params(("arbitrary",)),
    )(u, w_down, x1, target)


def _mlp_bwd_act(dy, w_down, u):
    s = dy.shape[0]
    tm, tn = min(1024, s), 1024

    def body(dy_ref, w_ref, u_ref, da_ref):
        du = _dot(dy_ref[...], w_ref[...], NT)
        da_ref[...] = (du * (2.0 * jnp.sqrt(u_ref[...].astype(F32)))).astype(BF16)

    return pl.pallas_call(
        body, name="mlp_bwd_act", grid=(D_FF // tn, s // tm),
        in_specs=[pl.BlockSpec((tm, D_MODEL), lambda j, i: (i, 0)), pl.BlockSpec((tn, D_MODEL), lambda j, i: (j, 0)),
                  pl.BlockSpec((tm, tn), lambda j, i: (i, j))],
        out_specs=pl.BlockSpec((tm, tn), lambda j, i: (i, j)),
        out_shape=jax.ShapeDtypeStruct((s, D_FF), BF16),
        compiler_params=_params(("parallel", "parallel")),
    )(dy, w_down, u)


def _rms_bwd(xv, g, dh, skip):
    r = lax.rsqrt(jnp.mean(xv * xv, axis=-1, keepdims=True) + EPS)
    n = xv * r
    dn = dh * g
    dx = skip + r * (dn - n * jnp.mean(dn * n, axis=-1, keepdims=True))
    return dx, jnp.sum(dh * n, axis=0, keepdims=True)


def _mlp_bwd_x(da, w_up, x1, dy, g_mlp):
    s = da.shape[0]
    tm = min(256, s)

    def body(da_ref, w_ref, x1_ref, dy_ref, g_ref, dx1_ref, dg_ref):
        i = pl.program_id(0)

        @pl.when(i == 0)
        def _():
            dg_ref[...] = jnp.zeros_like(dg_ref)

        tn = w_ref.shape[2]
        dhm = _dot(da_ref[:, 0:tn], w_ref[0], NT)
        for j in range(1, N_DEV):
            dhm = dhm + _dot(da_ref[:, j * tn:(j + 1) * tn], w_ref[j], NT)
        dx, dg = _rms_bwd(x1_ref[...], g_ref[...], dhm, dy_ref[...])
        dx1_ref[...] = dx
        dg_ref[...] += dg

    row = pl.BlockSpec((tm, D_MODEL), lambda i: (i, 0))
    return pl.pallas_call(
        body, name="mlp_bwd_x", grid=(s // tm,),
        in_specs=[pl.BlockSpec((tm, D_FF), lambda i: (i, 0)), _full(w_up.shape), row, row, _full((1, D_MODEL))],
        out_specs=[row, _full((1, D_MODEL))],
        out_shape=[jax.ShapeDtypeStruct((s, D_MODEL), F32), jax.ShapeDtypeStruct((1, D_MODEL), F32)],
        compiler_params=_params(("arbitrary",)),
    )(da, w_up, x1, dy, g_mlp)


def _merge_bwd(dx1, oa, of, om, proj, b_gate, wa, wf, wm, w_out):
    s = dx1.shape[0]
    tm = min(256, s)

    def body(dx1_ref, oa_ref, of_ref, om_ref, gl_ref, bg_ref, wa_ref, wf_ref, wm_ref, wo_ref,
             dp_ref, doa_ref, dof_ref, dom_ref, dya_ref, dyf_ref, dym_ref, dbg_ref):
        i = pl.program_id(0)

        @pl.when(i == 0)
        def _():
            dbg_ref[...] = jnp.zeros_like(dbg_ref)

        dmerged = _dot(dx1_ref[...].astype(BF16), wo_ref[...], NT)
        branches = ((oa_ref, wa_ref, doa_ref, dya_ref), (of_ref, wf_ref, dof_ref, dyf_ref), (om_ref, wm_ref, dom_ref, dym_ref))
        for b, (o_ref, w_ref, do_ref, dyb_ref) in enumerate(branches):
            cs = slice(b * D_MODEL, (b + 1) * D_MODEL)
            y = _dot(o_ref[...], w_ref[...])
            g = _sigmoid(gl_ref[:, cs] + bg_ref[:, cs])
            dz = (dmerged * y) * g * (1.0 - g)
            dp_ref[:, cs] = dz.astype(BF16)
            dbg_ref[:, cs] += jnp.sum(dz, axis=0, keepdims=True)
            dyb = (dmerged * g).astype(BF16)
            dyb_ref[...] = dyb
            do_ref[...] = _dot(dyb, w_ref[...], NT).astype(BF16)

    row = lambda w: pl.BlockSpec((tm, w), lambda i: (i, 0))
    sd = lambda w: jax.ShapeDtypeStruct((s, w), BF16)
    return pl.pallas_call(
        body, name="merge_bwd", grid=(s // tm,),
        in_specs=[row(D_MODEL), row(512), row(512), row(512), row(HALF_W), _full((1, HALF_W)),
                  _full(wa.shape), _full(wf.shape), _full(wm.shape), _full(w_out.shape)],
        out_specs=[row(HALF_W), row(512), row(512), row(512), row(D_MODEL), row(D_MODEL), row(D_MODEL), _full((1, HALF_W))],
        out_shape=[sd(PROJ_W), sd(512), sd(512), sd(512), sd(D_MODEL), sd(D_MODEL), sd(D_MODEL),
                   jax.ShapeDtypeStruct((1, HALF_W), F32)],
        compiler_params=_params(("arbitrary",)),
    )(dx1, oa, of, om, proj, b_gate, wa, wf, wm, w_out)


def _swa_bwd(qa, kp, vp, bias, sink, doa):
    s = qa.shape[0]
    nb = s // SWA_BLOCK

    def body(sink_ref, q_ref, kp_ref, vp_ref, bias_ref, do_ref, dq_ref, dkp_ref, dvp_ref, dbias_ref, dsink_ref, sk_acc):
        n = pl.program_id(0)

        @pl.when(n == 0)
        def _():
            dkp_ref[...] = jnp.zeros_like(dkp_ref)
            dvp_ref[...] = jnp.zeros_like(dvp_ref)
            dbias_ref[...] = jnp.zeros_like(dbias_ref)
            sk_acc[...] = jnp.zeros_like(sk_acc)

        start = pl.multiple_of(n * SWA_BLOCK, SWA_BLOCK)
        win = pl.ds(start, 2 * SWA_BLOCK)
        k2 = kp_ref[win, :]
        v2 = vp_ref[win, :]
        valid = _swa_valid(n)
        for kv in range(2):
            hs_kv = slice(kv * HEAD, (kv + 1) * HEAD)
            kh = k2[:, hs_kv]
            vh = v2[:, hs_kv]
            dk2 = jnp.zeros((2 * SWA_BLOCK, HEAD), F32)
            dv2 = jnp.zeros((2 * SWA_BLOCK, HEAD), F32)
            for g in range(4):
                h = kv * 4 + g
                hs = slice(h * HEAD, (h + 1) * HEAD)
                qh = q_ref[:, hs]
                doh = do_ref[:, hs]
                sc = _dot(qh, kh, NT) * 0.125 + bias_ref[h]
                sc = jnp.where(valid, sc, NEG)
                sk = sink_ref[h]
                mx = jnp.maximum(jnp.max(sc, axis=-1, keepdims=True), sk)
                p = jnp.exp(sc - mx)
                esk = jnp.exp(sk - mx)
                den = jnp.sum(p, axis=-1, keepdims=True) + esk
                p = p / den
                dp = _dot(doh, vh, NT)
                delta = jnp.sum(p * dp, axis=-1, keepdims=True)
                ds = p * (dp - delta)
                sk_acc[:, h:h + 1] += -(esk / den) * delta
                dbias_ref[h] += ds
                dsb = (ds * 0.125).astype(BF16)
                dq_ref[:, hs] = _dot(dsb, kh)
                dk2 = dk2 + _dot(dsb, qh, TN)
                dv2 = dv2 + _dot(p.astype(BF16), doh, TN)
            dkp_ref[win, hs_kv] += dk2
            dvp_ref[win, hs_kv] += dv2

        @pl.when(n == nb - 1)
        def _():
            dsink_ref[...] = jnp.sum(sk_acc[...], axis=0, keepdims=True)

    return pl.pallas_call(
        body, name="swa_bwd", grid=(nb,),
        in_specs=[pl.BlockSpec(memory_space=pltpu.SMEM),
                  pl.BlockSpec((SWA_BLOCK, 512), lambda n: (n, 0)),
                  _full(kp.shape), _full(vp.shape), _full(bias.shape),
                  pl.BlockSpec((SWA_BLOCK, 512), lambda n: (n, 0))],
        out_specs=[pl.BlockSpec((SWA_BLOCK, 512), lambda n: (n, 0)), _full(kp.shape), _full(vp.shape),
                   _full(bias.shape), _full((1, 128))],
        out_shape=[jax.ShapeDtypeStruct((s, 512), F32), jax.ShapeDtypeStruct(kp.shape, F32),
                   jax.ShapeDtypeStruct(vp.shape, F32), jax.ShapeDtypeStruct(bias.shape, F32),
                   jax.ShapeDtypeStruct((1, 128), F32)],
        scratch_shapes=[pltpu.VMEM((SWA_BLOCK, 128), F32)],
        compiler_params=_params(("arbitrary",)),
    )(sink, qa, kp, vp, bias, doa)


def _fox_bwd(q, k, v, do, o, cc4, lse4):
    s = q.shape[0]
    t = min(FOX_BWD_TK, s)
    tq = min(FOX_BWD_TQ, s)
    nq = s // t
    nqt = s // tq

    def body(q_ref, k_ref, v_ref, do_ref, o_ref, cc_ref, lse_ref,
             dqt_ref, dk_ref, dv_ref, dck_ref, dcq_ref, delta_ref, dk0, dk1, dv0, dv1, ds0, ds1):
        j = pl.program_id(1)

        @pl.when(j == 0)
        def _():
            dqt_ref[...] = jnp.zeros_like(dqt_ref)
            dcq_ref[...] = jnp.zeros_like(dcq_ref)
            lane8 = lax.broadcasted_iota(jnp.int32, (8, 128), 1)
            row8 = lax.broadcasted_iota(jnp.int32, (8, 128), 0)
            sel = jnp.where((lane8 // HEAD) == row8, 1.0, 0.0).astype(F32)

            def dl(i, c):
                rows = pl.ds(pl.multiple_of(i * tq, tq), tq)
                pr = do_ref[rows, :].astype(F32) * o_ref[rows, :].astype(F32)
                delta_ref[:, rows] = lax.dot_general(sel, pr, NT, precision=lax.Precision.HIGHEST,
                                                     preferred_element_type=F32)
                return c

            lax.fori_loop(0, nqt, dl, 0)

        kj = k_ref[...]
        vj = v_ref[...]
        ks = pl.ds(pl.multiple_of(j * t, t), t)
        kt = (kj.astype(F32) * 0.125).T.astype(BF16)
        ke = [jnp.where(_head_mask(e), kj, jnp.zeros_like(kj)) for e in range(2)]
        ve = [jnp.where(_head_mask(e), vj, jnp.zeros_like(vj)) for e in range(2)]
        kte = [jnp.where(_head_rows(e), kt, jnp.zeros_like(kt)) for e in range(2)]
        ck = [cc_ref[0, ks, e:e + 1] for e in range(2)]
        accs = ((dk0, dv0, ds0), (dk1, dv1, ds1))
        for refs in accs:
            for r in refs:
                r[...] = jnp.zeros_like(r)
        i_first = (j * t) // tq
        krow = lax.broadcasted_iota(jnp.int32, (t, tq), 0) + j * t
        qcol = lax.broadcasted_iota(jnp.int32, (t, tq), 1) + i_first * tq

        def step(i, c, masked):
            rows = pl.ds(pl.multiple_of(i * tq, tq), tq)
            qs = q_ref[rows, :] * jnp.asarray(0.125, BF16)
            doi = do_ref[rows, :]
            for e in range(2):
                dk_acc, dv_acc, ds_acc = accs[e]
                st = _dot(ke[e], qs, NT) - ck[e]
                if masked:
                    st = jnp.where(krow <= qcol, st, NEG)
                pt = jnp.exp(st - lse_ref[0, e:e + 1, rows])
                dpt = _dot(ve[e], doi, NT)
                dst = pt * (dpt - delta_ref[e:e + 1, rows])
                dsb = dst.astype(BF16)
                dv_acc[...] += _dot(pt.astype(BF16), doi)
                dk_acc[...] += _dot(dsb, qs)
                dqt_ref[:, rows] += _dot(kte[e], dsb)
                ds_acc[...] += dst
                dcq_ref[0, e:e + 1, rows] += jnp.sum(dst, axis=0, keepdims=True)
            return c

        step(i_first, 0, True)
        lax.fori_loop(i_first + 1, nqt, functools.partial(step, masked=False), 0)
        m0 = _head_mask(0)
        dk_ref[...] = jnp.where(m0, dk0[...], dk1[...])
        dv_ref[...] = jnp.where(m0, dv0[...], dv1[...])
        lane = lax.broadcasted_iota(jnp.int32, (t, 128), 1)
        c0 = jnp.sum(ds0[...], axis=-1, keepdims=True)
        c1 = jnp.sum(ds1[...], axis=-1, keepdims=True)
        dck_ref[0] = jnp.where(lane == 0, c0, jnp.where(lane == 1, c1, 0.0))

    res = lambda: pl.BlockSpec((s, 128), lambda hp, j: (0, hp))
    blk = lambda: pl.BlockSpec((t, 128), lambda hp, j: (j, hp))
    return pl.pallas_call(
        body, name="fox_bwd", grid=(4, nq),
        in_specs=[res(), blk(), blk(), res(), res(), pl.BlockSpec((1, s, 128), lambda hp, j: (hp, 0, 0)),
                  pl.BlockSpec((1, 8, s), lambda hp, j: (hp, 0, 0))],
        out_specs=[pl.BlockSpec((128, s), lambda hp, j: (hp, 0)), blk(), blk(),
                   pl.BlockSpec((1, t, 128), lambda hp, j: (hp, j, 0)),
                   pl.BlockSpec((1, 8, s), lambda hp, j: (hp, 0, 0))],
        out_shape=[jax.ShapeDtypeStruct((512, s), F32), jax.ShapeDtypeStruct((s, 512), F32),
                   jax.ShapeDtypeStruct((s, 512), F32), jax.ShapeDtypeStruct((4, s, 128), F32),
                   jax.ShapeDtypeStruct((4, 8, s), F32)],
        scratch_shapes=[pltpu.VMEM((8, s), F32)] + [pltpu.VMEM((t, 128), F32)] * 4 + [pltpu.VMEM((t, tq), F32)] * 2,
        compiler_params=_params(("arbitrary", "arbitrary")),
    )(q, k, v, do, o, cc4, lse4)


def _mem_bwd(qm, mk, mv, dom):
    s = qm.shape[0]
    tq = min(512, s)

    def body(q_ref, mk_ref, mv_ref, do_ref, dq_ref, dmk_ref, dmv_ref):
        i = pl.program_id(0)

        @pl.when(i == 0)
        def _():
            dmk_ref[...] = jnp.zeros_like(dmk_ref)
            dmv_ref[...] = jnp.zeros_like(dmv_ref)

        for h in range(MEM_HEADS):
            hs = slice(h * 128, (h + 1) * 128)
            qh = q_ref[:, hs]
            doh = do_ref[:, hs]
            sc = _dot(qh, mk_ref[:, hs], NT) * MEM_SCALE
            mx = jnp.max(sc, axis=-1, keepdims=True)
            p = jnp.exp(sc - mx)
            p = p / jnp.sum(p, axis=-1, keepdims=True)
            dp = _dot(doh, mv_ref[:, hs], NT)
            ds = p * (dp - jnp.sum(p * dp, axis=-1, keepdims=True))
            dsb = (ds * MEM_SCALE).astype(BF16)
            dq_ref[:, hs] = _dot(dsb, mk_ref[:, hs])
            dmk_ref[:, hs] += _dot(dsb, qh, TN)
            dmv_ref[:, hs] += _dot(p.astype(BF16), doh, TN)

    return pl.pallas_call(
        body, name="mem_bwd", grid=(s // tq,),
        in_specs=[pl.BlockSpec((tq, 512), lambda i: (i, 0)), _full(mk.shape), _full(mv.shape),
                  pl.BlockSpec((tq, 512), lambda i: (i, 0))],
        out_specs=[pl.BlockSpec((tq, 512), lambda i: (i, 0)), _full(mk.shape), _full(mv.shape)],
        out_shape=[jax.ShapeDtypeStruct((s, 512), F32), jax.ShapeDtypeStruct(mk.shape, F32),
                   jax.ShapeDtypeStruct(mv.shape, F32)],
        compiler_params=_params(("arbitrary",)),
    )(qm, mk, mv, dom)


def _memkv_bwd(dmk, dmv, kv_raw, kn_mem, mem, g_mem, mem_n, w_kv):
    def body(dmk_ref, dmv_ref, kv_ref, kn_ref, mem_ref, g_ref, mn_ref, w_ref, dw_ref, dkn_ref, dg_ref, dkv_ref):
        dkn = jnp.zeros((1, 128), F32)
        for h in range(MEM_HEADS):
            hs = slice(h * 128, (h + 1) * 128)
            v = kv_ref[:, hs]
            r = lax.rsqrt(jnp.mean(v * v, axis=-1, keepdims=True) + EPS)
            n = v * r
            dn = dmk_ref[:, hs]
            dkn = dkn + jnp.sum(dn * n, axis=0, keepdims=True)
            dng = dn * kn_ref[...]
            dkv_ref[:, hs] = (r * (dng - n * jnp.mean(dng * n, axis=-1, keepdims=True))).astype(BF16)
        dkv_ref[:, 512:1024] = dmv_ref[...].astype(BF16)
        dkn_ref[...] = dkn
        dkv = dkv_ref[...]
        dw_ref[...] = _dot(mn_ref[...], dkv, TN).astype(BF16)
        dmn = _dot(dkv, w_ref[...], NT)
        xv = mem_ref[...]
        r = lax.rsqrt(jnp.mean(xv * xv, axis=-1, keepdims=True) + EPS)
        dg_ref[...] = jnp.sum(dmn * (xv * r), axis=0, keepdims=True)

    m = mem.shape[0]
    return pl.pallas_call(
        body, name="memkv_bwd",
        out_shape=[jax.ShapeDtypeStruct((D_MODEL, 1024), BF16), jax.ShapeDtypeStruct((1, 128), F32),
                   jax.ShapeDtypeStruct((1, D_MODEL), F32)],
        scratch_shapes=[pltpu.VMEM((m, 1024), BF16)],
        compiler_params=pltpu.CompilerParams(vmem_limit_bytes=VMEM_LIMIT),
    )(dmk, dmv, kv_raw, kn_mem, mem, g_mem, mem_n, w_kv)


def _fox_gate_bwd(dc, proj, b_forget128):
    s = dc.shape[0]
    tm = min(512, s)
    nt = s // tm

    def body(dc_ref, p_ref, b_ref, dfl_ref, db_ref, carry_ref):
        i = pl.program_id(0)

        @pl.when(i == 0)
        def _():
            carry_ref[...] = jnp.zeros_like(carry_ref)
            db_ref[...] = jnp.zeros_like(db_ref)

        dcv = dc_ref[...]
        dlogf = jnp.dot(_tri(tm, False), dcv, precision=lax.Precision.HIGHEST, preferred_element_type=F32) + carry_ref[...]
        carry_ref[...] += jnp.sum(dcv, axis=0, keepdims=True)
        z = p_ref[...] + b_ref[...]
        dfl = dlogf * (1.0 / (1.0 + jnp.exp(z)))
        dfl_ref[...] = dfl.astype(BF16)
        db_ref[...] += jnp.sum(dfl, axis=0, keepdims=True)

    return pl.pallas_call(
        body, name="fox_gate_bwd", grid=(nt,),
        in_specs=[pl.BlockSpec((tm, 128), lambda i: (nt - 1 - i, 0)),
                  pl.BlockSpec((tm, 128), lambda i: (nt - 1 - i, FL0 // 128)), _full((1, 128))],
        out_specs=[pl.BlockSpec((tm, 128), lambda i: (nt - 1 - i, 0)), _full((1, 128))],
        out_shape=[jax.ShapeDtypeStruct((s, 128), BF16), jax.ShapeDtypeStruct((1, 128), F32)],
        scratch_shapes=[pltpu.VMEM((1, 128), F32)],
        compiler_params=_params(("arbitrary",)),
    )(dc, proj, b_forget128)


def _proj_pre_bwd(dproj, proj, dqf, dkf, dvf, dqm, dqa, dka, dva, dfl, gq_fox, gk_fox, gq_mem, gq_swa, gk_swa):
    s = proj.shape[0]
    tm = min(256, s)

    def body(dp_in, p_ref, dqf_ref, dkf_ref, dvf_ref, dqm_ref, dqa_ref, dka_ref, dva_ref, dfl_ref,
             gqf, gkf, gqm, gqa, gka, dp_ref, dgn_ref):
        i = pl.program_id(0)

        @pl.when(i == 0)
        def _():
            dgn_ref[...] = jnp.zeros_like(dgn_ref)

        def norm_bwd(off, width, hd, g_ref, dn_ref, slot):
            acc = jnp.zeros((1, 128), F32)
            for b in range(width // 128):
                v = p_ref[:, off + b * 128: off + (b + 1) * 128]
                r = lax.rsqrt(_group_mean(v * v, hd) + EPS)
                n = v * r
                dn = dn_ref[:, b * 128:(b + 1) * 128]
                acc = acc + jnp.sum(dn * n, axis=0, keepdims=True)
                dng = dn * g_ref[...]
                dp_ref[:, off + b * 128: off + (b + 1) * 128] = (r * (dng - n * _group_mean(dng * n, hd))).astype(BF16)
            dgn_ref[slot:slot + 1, :] += acc

        norm_bwd(H_QF, 512, HEAD, gqf, dqf_ref, 0)
        norm_bwd(H_KF, 512, HEAD, gkf, dkf_ref, 1)
        dp_ref[:, H_VF:H_VF + 512] = dvf_ref[...].astype(BF16)
        norm_bwd(H_QM, 512, MEM_HEAD, gqm, dqm_ref, 2)
        norm_bwd(H_QA, 512, HEAD, gqa, dqa_ref, 3)
        norm_bwd(H_KA, 128, HEAD, gka, dka_ref, 4)
        dp_ref[:, H_VA:H_VA + 128] = dva_ref[...].astype(BF16)
        dp_ref[:, H_FL:H_FL + 128] = dfl_ref[...]
        dp_ref[:, H_FL + 128:HALF_W] = jnp.zeros((tm, HALF_W - H_FL - 128), BF16)

    row = lambda w: pl.BlockSpec((tm, w), lambda i: (i, 0))
    g_spec = _full((1, 128))
    return pl.pallas_call(
        body, name="proj_pre_bwd", grid=(s // tm,),
        in_specs=[pl.BlockSpec(memory_space=pl.ANY), pl.BlockSpec((tm, HALF_W), lambda i: (i, 1)),
                  row(512), row(512), row(512), row(512), row(512), row(128), row(128), row(128),
                  g_spec, g_spec, g_spec, g_spec, g_spec],
        out_specs=[pl.BlockSpec((tm, HALF_W), lambda i: (i, 1)), _full((8, 128))],
        out_shape=[jax.ShapeDtypeStruct((s, PROJ_W), BF16), jax.ShapeDtypeStruct((8, 128), F32)],
        input_output_aliases={0: 0},
        compiler_params=_params(("arbitrary",)),
    )(dproj, proj, dqf, dkf, dvf, dqm, dqa, dka, dva, dfl, gq_fox, gk_fox, gq_mem, gq_swa, gk_swa)


def _in_bwd_x(dproj, w_in_p, x, g_mix, dx1):
    s = x.shape[0]
    tm = min(256, s)

    def body(dp_ref, w_ref, x_ref, g_ref, dx1_ref, gx_ref, dg_ref):
        i = pl.program_id(0)

        @pl.when(i == 0)
        def _():
            dg_ref[...] = jnp.zeros_like(dg_ref)

        dx, dg = _rms_bwd(x_ref[...], g_ref[...], _dot(dp_ref[...], w_ref[...], NT), dx1_ref[...])
        gx_ref[...] = dx
        dg_ref[...] += dg

    row = pl.BlockSpec((tm, D_MODEL), lambda i: (i, 0))
    return pl.pallas_call(
        body, name="in_bwd_x", grid=(s // tm,),
        in_specs=[pl.BlockSpec((tm, PROJ_W), lambda i: (i, 0)), _full(w_in_p.shape), row, _full((1, D_MODEL)), row],
        out_specs=[row, _full((1, D_MODEL))],
        out_shape=[jax.ShapeDtypeStruct((s, D_MODEL), F32), jax.ShapeDtypeStruct((1, D_MODEL), F32)],
        compiler_params=_params(("arbitrary",)),
    )(dproj, w_in_p, x, g_mix, dx1)


def _rel_bias_bwd(dbias, bucket):
    def body(db_ref, bk_ref, o_ref):
        bk = bk_ref[...]
        lane = lax.broadcasted_iota(jnp.int32, (1, 128), 1)
        for b in range(REL_BUCKETS):
            sel = bk == b
            acc = jnp.zeros((1, 128), F32)
            for h in range(SWA_HEADS):
                tot = jnp.sum(jnp.sum(jnp.where(sel, db_ref[h], 0.0), axis=-1, keepdims=True), axis=0, keepdims=True)
                acc = jnp.where(lane == h, tot, acc)
            o_ref[b:b + 1, :] = acc

    return pl.pallas_call(
        body, name="rel_bias_bwd",
        out_shape=jax.ShapeDtypeStruct((REL_BUCKETS, 128), F32),
        compiler_params=pltpu.CompilerParams(vmem_limit_bytes=VMEM_LIMIT),
    )(dbias, bucket)


def _my_place():
    return lax.axis_index("x"), lax.axis_index("y"), lax.axis_index("c")


def _peer(place, k):
    x, y, c = place
    return (1 - x if k & 4 else x, 1 - y if k & 2 else y, 1 - c if k & 1 else c)


def _index(place):
    x, y, c = place
    return 4 * x + 2 * y + c


HBM_SPEC = pl.BlockSpec(memory_space=pltpu.HBM)
SEM_SPEC = pl.BlockSpec(memory_space=pltpu.SEMAPHORE)
DATAFLOW = pltpu.SideEffectType.DATAFLOW_SIDE_EFFECTING


def _split_copy(src_ref, land_ref, send_sems, recv_sems, me, k, gather):
    peer = _peer(me, k)
    if gather:
        src, dst = src_ref, land_ref.at[_index(me)]
    else:
        src, dst = src_ref.at[_index(peer)], land_ref.at[k - 1]
    return pltpu.make_async_remote_copy(src_ref=src, dst_ref=dst, send_sem=send_sems.at[k - 1], recv_sem=recv_sems.at[k - 1],
                                        device_id=peer, device_id_type=MESH)


def _split_start(srcs, slots, gather, name):
    n = len(srcs)

    def body(*refs):
        src_refs, land_refs = refs[:n], refs[n:2 * n]
        send_sems, recv_sems, token = refs[2 * n:3 * n], refs[3 * n:4 * n], refs[-1]
        me = _my_place()
        for w in range(n):
            for k in range(1, N_DEV):
                _split_copy(src_refs[w], land_refs[w], send_sems[w], recv_sems[w], me, k, gather).start()
        token[...] = jnp.zeros_like(token)

    lands = [lax.empty((slots,) + (a.shape if gather else a.shape[1:]), a.dtype) for a in srcs]
    sems = [pltpu.SemaphoreType.DMA((N_DEV - 1,))] * (2 * n)
    hbm = [pltpu.HBM(a.shape, a.dtype) for a in list(srcs) + lands]
    outs = pl.pallas_call(
        body, name=name,
        out_shape=(*sems, *hbm, jax.ShapeDtypeStruct((8, 128), F32)),
        in_specs=(HBM_SPEC,) * (2 * n),
        out_specs=(SEM_SPEC,) * (2 * n) + (HBM_SPEC,) * (2 * n) + (pl.BlockSpec(memory_space=pltpu.VMEM),),
        input_output_aliases={i: 2 * n + i for i in range(2 * n)},
        compiler_params=pltpu.CompilerParams(has_side_effects=DATAFLOW),
    )(*[pltpu.with_memory_space_constraint(a, pltpu.HBM) for a in list(srcs) + lands])
    return list(outs[:n]), list(outs[n:2 * n]), list(outs[2 * n:3 * n]), list(outs[3 * n:4 * n]), outs[-1]


def _split_wait(started, w, after, gather, name):
    send_sems, recv_sems, srcs, lands, _ = started

    def body(src_ref, land_ref, send_sems, recv_sems, after_ref, src_out, land_out):
        me = _my_place()
        for k in range(1, N_DEV):
            cp = _split_copy(src_ref, land_ref, send_sems, recv_sems, me, k, gather)
            cp.wait_send()
            cp.wait_recv()

    return pl.pallas_call(
        body, name=name,
        out_shape=(pltpu.HBM(srcs[w].shape, srcs[w].dtype), pltpu.HBM(lands[w].shape, lands[w].dtype)),
        in_specs=(HBM_SPEC, HBM_SPEC, SEM_SPEC, SEM_SPEC, pl.BlockSpec(memory_space=pl.ANY)),
        out_specs=(HBM_SPEC, HBM_SPEC), input_output_aliases={0: 0, 1: 1},
        compiler_params=pltpu.CompilerParams(has_side_effects=DATAFLOW),
    )(srcs[w], lands[w], send_sems[w], recv_sems[w], after)


def _adam_math(w, g, m, v):
    m2 = ADAM_B1 * m + (1.0 - ADAM_B1) * g
    v2 = ADAM_B2 * v + (1.0 - ADAM_B2) * (g * g)
    m_hat = m2 / (1.0 - ADAM_B1 ** ADAM_STEP)
    v_hat = v2 / (1.0 - ADAM_B2 ** ADAM_STEP)
    delta = -ADAM_LR * (m_hat / (jnp.sqrt(v_hat) + ADAM_EPS) + ADAM_WD * w)
    return delta, m2, v2


def _adamw(own, land, w, m, v, name):
    a, b = w.shape
    bp = own.shape[1]
    ta = min(128, a)

    def body(o_ref, p_ref, w_ref, m_ref, v_ref, g_ref, d_ref, m2_ref, v2_ref):
        g = o_ref[:, 0:b].astype(F32)
        for k in range(N_DEV - 1):
            g = g + p_ref[k, :, 0:b].astype(F32)
        delta, m2, v2 = _adam_math(w_ref[...], g, m_ref[...], v_ref[...])
        g_ref[...] = g
        d_ref[...] = delta
        m2_ref[...] = m2
        v2_ref[...] = v2

    blk = pl.BlockSpec((ta, b), lambda i: (i, 0))
    sd = jax.ShapeDtypeStruct((a, b), F32)
    return pl.pallas_call(
        body, name=name, grid=(a // ta,),
        in_specs=[pl.BlockSpec((ta, bp), lambda i: (i, 0)), pl.BlockSpec((N_DEV - 1, ta, bp), lambda i: (0, i, 0)), blk, blk, blk],
        out_specs=[blk, blk, blk, blk], out_shape=[sd, sd, sd, sd],
        compiler_params=_params(("parallel",)),
    )(own, land, w, m, v)


def _bucket_table():
    t_loc = jnp.arange(SWA_BLOCK)[:, None] + SWA_BLOCK
    s_loc = jnp.arange(2 * SWA_BLOCK)[None, :]
    dist = t_loc - s_loc
    max_exact = REL_BUCKETS // 2
    d = jnp.maximum(dist, 0)
    df = jnp.maximum(d, 1).astype(F32)
    large = max_exact + (jnp.log(df / max_exact) / math.log(REL_MAX_DIST / max_exact) * (REL_BUCKETS - max_exact)).astype(jnp.int32)
    large = jnp.minimum(large, REL_BUCKETS - 1)
    bucket = jnp.where(d < max_exact, d, large)
    band = (dist >= 0) & (dist < SWA_BLOCK)
    return bucket, band


def _tile2(g):
    return jnp.concatenate([g, g], axis=1) if g.shape[1] == HEAD else g


SHARD_W = 737
SHARD_WP = 768
IN_WIDTH = N_DEV * SHARD_W
SEGMENTS = ((GL0, 2824, 3072), (QF0, 768, 512), (KF0, 1280, 512), (VF0, 1792, 512), (QM0, 2312, 512),
            (QA0, 0, 512), (KA0, 512, 128), (VA0, 640, 128), (FL0, 2304, 8))


def _lane_plan(sources):
    plan = []
    for t in range(len(sources) // 128):
        groups = {}
        for lane in range(128):
            src = sources[128 * t + lane]
            if src is not None:
                slab, col = src
                groups.setdefault((slab, col // 128, (lane - col) % 128), []).append(lane)
        tile = []
        for key, lanes in groups.items():
            assert lanes == list(range(lanes[0], lanes[-1] + 1))
            tile.append((key, lanes[0], lanes[-1] + 1))
        plan.append(tile)
    return plan


def _assemble(tile_plan, load, rows):
    lane = lax.broadcasted_iota(jnp.int32, (1, 128), 1)
    out = jnp.zeros((rows, 128), F32)
    for (slab, st, roll), lo, hi in tile_plan:
        v = load(slab, st)
        if roll:
            v = pltpu.roll(v, roll, 1)
        out = v if (lo, hi) == (0, 128) else jnp.where((lane >= lo) & (lane < hi), v, out)
    return out


def _w_in_from_shards(land, own):
    ref_col = [None] * PROJ_W
    for p0, r0, n in SEGMENTS:
        for i in range(n):
            ref_col[p0 + i] = divmod(r0 + i, SHARD_W)
    plan = _lane_plan(ref_col)
    d_model = own.shape[0]
    tm = 256

    def body(land_ref, own_ref, o_ref):
        me = _index(_my_place())

        def load(slab, st):
            cols = slice(st * 128, (st + 1) * 128)
            return jnp.where(me == slab, own_ref[:, cols], land_ref[slab, :, cols]).astype(F32)

        for t, tile_plan in enumerate(plan):
            o_ref[:, t * 128:(t + 1) * 128] = _assemble(tile_plan, load, tm).astype(BF16)

    return pl.pallas_call(
        body, name="w_in_from_shards", grid=(d_model // tm,),
        in_specs=[pl.BlockSpec((N_DEV, tm, SHARD_WP), lambda i: (0, i, 0)), pl.BlockSpec((tm, SHARD_WP), lambda i: (i, 0))],
        out_specs=pl.BlockSpec((tm, PROJ_W), lambda i: (i, 0)),
        out_shape=jax.ShapeDtypeStruct((d_model, PROJ_W), BF16),
        compiler_params=_params(("parallel",)),
    )(land, own)


def _dw_in_to_parts(dwp):
    padded_col = [None] * IN_WIDTH
    for p0, r0, n in SEGMENTS:
        for i in range(n):
            padded_col[r0 + i] = p0 + i
    sources = []
    for d in range(N_DEV):
        sources += [(0, padded_col[SHARD_W * d + c]) if c < SHARD_W else None for c in range(SHARD_WP)]
    plan = _lane_plan(sources)
    d_model = dwp.shape[0]
    tm = 256
    tiles = SHARD_WP // 128

    def body(dw_ref, o_ref):
        load = lambda slab, st: dw_ref[:, st * 128:(st + 1) * 128].astype(F32)
        for t, tile_plan in enumerate(plan):
            d, c = divmod(t, tiles)
            o_ref[d, :, c * 128:(c + 1) * 128] = _assemble(tile_plan, load, tm).astype(BF16)

    return pl.pallas_call(
        body, name="dw_in_to_parts", grid=(d_model // tm,),
        in_specs=[pl.BlockSpec((tm, PROJ_W), lambda i: (i, 0))],
        out_specs=pl.BlockSpec((N_DEV, tm, SHARD_WP), lambda i: (0, i, 0)),
        out_shape=jax.ShapeDtypeStruct((N_DEV, d_model, SHARD_WP), BF16),
        compiler_params=_params(("parallel",)),
    )(dwp)


def _cast_shards(shards):
    names = list(shards)

    def body(*refs):
        for src, dst in zip(refs[:len(names)], refs[len(names):]):
            if dst.shape != src.shape:
                dst[...] = jnp.zeros(dst.shape, BF16)
                dst[:, 0:src.shape[1]] = src[...].astype(BF16)
            else:
                dst[...] = src[...].astype(BF16)

    out_shape = [jax.ShapeDtypeStruct((shards[n].shape[0], SHARD_WP if n == "w_in" else shards[n].shape[1]), BF16)
                 for n in names]
    outs = pl.pallas_call(body, name="cast_shards", out_shape=out_shape,
                          compiler_params=pltpu.CompilerParams(vmem_limit_bytes=VMEM_LIMIT))(*[shards[n] for n in names])
    return dict(zip(names, outs))


def _tie(x, *tokens):
    for t in tokens:
        if t is not None:
            x = x + t[0:1, 0:1]
    return x


def _local_step(x, mem, target, p, getw, emit, deps=()):
    s = x.shape[0]
    bucket, band = _bucket_table()
    bucket_m = jnp.where(band, bucket, -1).astype(jnp.int32)
    bias = _bias_table(p["rel_bias"], bucket_m)
    gqf, gkf, gqa, gka = _tile2(p["qn_fox"]), _tile2(p["kn_fox"]), _tile2(p["qn_swa"]), _tile2(p["kn_swa"])
    gqm = p["qn_mem"]
    bf128 = jnp.pad(p["b_forget"], ((0, 0), (0, 120)))
    sink = p["sink_swa"].reshape(8)

    h = _rms_fwd(x, p["g_mix"], "rms_mix", deps)
    w_in = getw("w_in", h)
    proj = _mm(h, w_in, "nn", F32, 512, 1536, 1024, "proj")
    qf, kf, vf, qm, qa, ka, va = _proj_post(proj, gqf, gkf, gqm, gqa, gka)
    cc4 = _fox_gate_fwd(proj, bf128)
    w_kv = getw("w_mem_kv", cc4)
    mem_n, kv_raw, mk, mv = _memkv_fwd(mem, p["g_mem"], w_kv, p["kn_mem"])
    kp = jnp.pad(ka, ((SWA_BLOCK, 0), (0, 0)))
    vp = jnp.pad(va, ((SWA_BLOCK, 0), (0, 0)))
    oa = _swa_fwd(qa, kp, vp, bias, sink)
    of, lse4 = _fox_fwd(qf, kf, jnp.transpose(vf), cc4)
    om = _mem_fwd(qm, mk, mv)
    wa, wf, wm, w_out = getw("w_o_swa", oa), getw("w_o_fox", oa), getw("w_o_mem", oa), getw("w_out", oa)
    x1, hm, merged = _merge_fwd(x, oa, of, om, proj, p["b_gate"], wa, wf, wm, w_out, p["g_mlp"])
    w_up = getw("w_mlp_up", of)
    u = _mlp_up(hm, w_up)
    w_down = getw("w_mlp_down", hm)
    dy, dy_b, loss = _mlp_down_loss(u, w_down, x1, target)

    da = _mlp_bwd_act(dy_b, w_down, u)
    t_down = emit({"w_mlp_down": _mm(u, dy_b, "tn", BF16, 1024, 1024, 512, "dw_down")})
    dx1, dg_mlp = _mlp_bwd_x(da, w_up, x1, dy, _tie(p["g_mlp"], t_down))
    t_up = emit({"w_mlp_up": _mm(hm, da, "tn", BF16, 1024, 1024, 512, "dw_up", column_chunks=True)})
    dproj, doa, dof, dom, dya, dyf, dym, db_gate = _merge_bwd(
        dx1, oa, of, om, proj, _tie(p["b_gate"], t_up), wa, wf, wm, w_out)
    t_o = emit({"w_out": _mm(merged, dx1, "tn", BF16, 512, 1024, 512, "dw_out"),
                "w_o_swa": _mm(oa, dya, "tn", BF16, 512, 1024, 512, "dw_o_swa"),
                "w_o_fox": _mm(of, dyf, "tn", BF16, 512, 1024, 512, "dw_o_fox"),
                "w_o_mem": _mm(om, dym, "tn", BF16, 512, 1024, 512, "dw_o_mem")})

    dqm, dmk, dmv = _mem_bwd(qm, mk, mv, dom)
    dw_kv, dkn_mem, dg_mem = _memkv_bwd(dmk, dmv, kv_raw, _tie(p["kn_mem"], t_o), mem, p["g_mem"], mem_n, w_kv)
    t_kv = emit({"w_mem_kv": dw_kv})
    dqa, dkp, dvp, dbias, dsink = _swa_bwd(qa, kp, vp, bias, _tie(p["sink_swa"], t_kv).reshape(8), doa)
    dqf_t, dkf, dvf, dck4, dcq4 = _fox_bwd(qf, kf, vf, dof, of, cc4, lse4)
    dqf = jnp.transpose(dqf_t)

    dcq = jnp.transpose(dcq4[:, 0:2, :], (2, 0, 1)).reshape(s, 8)
    dck = jnp.transpose(dck4[:, :, 0:2], (1, 0, 2)).reshape(s, 8)
    dc = jnp.pad(dcq - dck, ((0, 0), (0, 120)))
    dfl, db_forget = _fox_gate_bwd(dc, proj, bf128)

    dproj, dgn = _proj_pre_bwd(dproj, proj, dqf, dkf, dvf, dqm, dqa, dkp[SWA_BLOCK:], dvp[SWA_BLOCK:], dfl,
                               gqf, gkf, gqm, gqa, gka)
    t_in = emit({"w_in": _mm(h, dproj, "tn", BF16, 1024, 3072, 512, "dw_in")})
    grad_x, dg_mix = _in_bwd_x(dproj, w_in, x, _tie(p["g_mix"], t_in), dx1)
    d_rel = _rel_bias_bwd(dbias, bucket_m)

    fold = lambda r: dgn[r:r + 1, 0:HEAD] + dgn[r:r + 1, HEAD:128]
    small = {
        "g_mix": dg_mix, "b_gate": db_gate, "b_forget": db_forget[:, 0:8],
        "qn_swa": fold(3), "kn_swa": fold(4), "sink_swa": dsink[:, 0:8], "rel_bias": d_rel[:, 0:8],
        "qn_fox": fold(0), "kn_fox": fold(1), "g_mem": dg_mem, "qn_mem": dgn[2:3, :], "kn_mem": dkn_mem,
        "g_mlp": dg_mlp,
    }
    return loss, grad_x, small


SMALL = ("g_mix", "b_gate", "b_forget", "qn_swa", "kn_swa", "sink_swa", "rel_bias", "qn_fox", "kn_fox", "g_mem",
         "qn_mem", "kn_mem", "g_mlp")
BIG = ("w_in", "w_mem_kv", "w_o_swa", "w_o_fox", "w_o_mem", "w_out", "w_mlp_up", "w_mlp_down")
COL_SHARDED = ("w_in", "w_o_swa", "w_o_fox", "w_o_mem", "w_mlp_up")
WEIGHTS = ("g_mix", "w_in", "b_gate", "b_forget", "qn_swa", "kn_swa", "sink_swa", "rel_bias", "qn_fox", "kn_fox", "g_mem",
           "w_mem_kv", "qn_mem", "kn_mem", "w_o_swa", "w_o_fox", "w_o_mem", "w_out", "g_mlp", "w_mlp_up", "w_mlp_down")
SMALL_USED = 6928
SMALL_PAD = 7168


def _gathered_to_full(name, g):
    if name in COL_SHARDED:
        return jnp.transpose(g, (1, 0, 2)).reshape(g.shape[1], N_DEV * g.shape[2])
    return g.reshape(N_DEV * g.shape[1], g.shape[2])


def _full_to_parts(name, full, b):
    if name in COL_SHARDED:
        return jnp.transpose(full.reshape(full.shape[0], N_DEV, b), (1, 0, 2)).astype(BF16)
    return full.reshape(N_DEV, full.shape[0] // N_DEV, full.shape[1]).astype(BF16)


def _pack_small(d, loss=None):
    flat = jnp.concatenate([d[n].reshape(-1) for n in SMALL])
    assert flat.shape[0] == SMALL_USED
    if loss is not None:
        flat = jnp.concatenate([flat, loss.reshape(-1)])
    return jnp.pad(flat, (0, SMALL_PAD - flat.shape[0])).reshape(8, SMALL_PAD // 8)


def _unpack_small(packed, like):
    flat = packed.reshape(-1)
    out, off = {}, 0
    for n in SMALL:
        size = like[n].size
        out[n] = flat[off:off + size].reshape(like[n].shape)
        off += size
    return out


def _adamw_small(parts, w, m, v):
    def body(p_ref, w_ref, m_ref, v_ref, g_ref, d_ref, m2_ref, v2_ref):
        g = p_ref[0]
        for k in range(1, N_DEV):
            g = g + p_ref[k]
        delta, m2, v2 = _adam_math(w_ref[...], g, m_ref[...], v_ref[...])
        g_ref[...] = g
        d_ref[...] = delta
        m2_ref[...] = m2
        v2_ref[...] = v2

    sd = jax.ShapeDtypeStruct(w.shape, F32)
    return pl.pallas_call(body, name="adamw_small", out_shape=[sd, sd, sd, sd])(parts, w, m, v)


def kernel(x, mem, g_mix, w_in, b_gate, b_forget, qn_swa, kn_swa, sink_swa, rel_bias, qn_fox, kn_fox, g_mem, w_mem_kv, qn_mem, kn_mem, w_o_swa, w_o_fox, w_o_mem, w_out, g_mlp, w_mlp_up, w_mlp_down, loss_target, m_g_mix, m_w_in, m_b_gate, m_b_forget, m_qn_swa, m_kn_swa, m_sink_swa, m_rel_bias, m_qn_fox, m_kn_fox, m_g_mem, m_w_mem_kv, m_qn_mem, m_kn_mem, m_w_o_swa, m_w_o_fox, m_w_o_mem, m_w_out, m_g_mlp, m_w_mlp_up, m_w_mlp_down, v_g_mix, v_w_in, v_b_gate, v_b_forget, v_qn_swa, v_kn_swa, v_sink_swa, v_rel_bias, v_qn_fox, v_kn_fox, v_g_mem, v_w_mem_kv, v_qn_mem, v_kn_mem, v_w_o_swa, v_w_o_fox, v_w_o_mem, v_w_out, v_g_mlp, v_w_mlp_up, v_w_mlp_down):
    wts = dict(g_mix=g_mix, w_in=w_in, b_gate=b_gate, b_forget=b_forget, qn_swa=qn_swa, kn_swa=kn_swa, sink_swa=sink_swa,
               rel_bias=rel_bias, qn_fox=qn_fox, kn_fox=kn_fox, g_mem=g_mem, w_mem_kv=w_mem_kv, qn_mem=qn_mem, kn_mem=kn_mem,
               w_o_swa=w_o_swa, w_o_fox=w_o_fox, w_o_mem=w_o_mem, w_out=w_out, g_mlp=g_mlp, w_mlp_up=w_mlp_up,
               w_mlp_down=w_mlp_down)
    mom = dict(g_mix=m_g_mix, w_in=m_w_in, b_gate=m_b_gate, b_forget=m_b_forget, qn_swa=m_qn_swa, kn_swa=m_kn_swa,
               sink_swa=m_sink_swa, rel_bias=m_rel_bias, qn_fox=m_qn_fox, kn_fox=m_kn_fox, g_mem=m_g_mem, w_mem_kv=m_w_mem_kv,
               qn_mem=m_qn_mem, kn_mem=m_kn_mem, w_o_swa=m_w_o_swa, w_o_fox=m_w_o_fox, w_o_mem=m_w_o_mem, w_out=m_w_out,
               g_mlp=m_g_mlp, w_mlp_up=m_w_mlp_up, w_mlp_down=m_w_mlp_down)
    var = dict(g_mix=v_g_mix, w_in=v_w_in, b_gate=v_b_gate, b_forget=v_b_forget, qn_swa=v_qn_swa, kn_swa=v_kn_swa,
               sink_swa=v_sink_swa, rel_bias=v_rel_bias, qn_fox=v_qn_fox, kn_fox=v_kn_fox, g_mem=v_g_mem, w_mem_kv=v_w_mem_kv,
               qn_mem=v_qn_mem, kn_mem=v_kn_mem, w_o_swa=v_w_o_swa, w_o_fox=v_w_o_fox, w_o_mem=v_w_o_mem, w_out=v_w_out,
               g_mlp=v_g_mlp, w_mlp_up=v_w_mlp_up, w_mlp_down=v_w_mlp_down)

    me = _index(_my_place())
    dev = lax.broadcasted_iota(jnp.int32, (N_DEV, 1, 1), 0)

    shards = _cast_shards({n: wts[n][0] for n in BIG})
    gather = _split_start([shards[n] for n in BIG], N_DEV, True, "ag_start")
    full = {}

    def getw(n, after):
        if n not in full:
            _, land = _split_wait(gather, BIG.index(n), after, True, "ag_wait_" + n)
            if n == "w_in":
                full[n] = _w_in_from_shards(land, shards[n])
            else:
                w = jnp.where(dev == me, shards[n][None], land)
                full[n] = w if n == "w_mlp_up" else _gathered_to_full(n, w)
        return full[n]

    exchanges = {}

    def emit(grads_by_name):
        parts = []
        for n, grad in grads_by_name.items():
            if n == "w_in":
                parts.append(_dw_in_to_parts(grad))
            else:
                parts.append(grad if n == "w_mlp_up" else _full_to_parts(n, grad, wts[n].shape[2]))
        started = _split_start(parts, N_DEV - 1, False, "rs_start_" + next(iter(grads_by_name)))
        for w, n in enumerate(grads_by_name):
            exchanges[n] = (started, w)
        return started[4]

    small_p = {n: wts[n] for n in SMALL}
    loss, grad_x, small_g = _local_step(x[0], mem[0], loss_target[0], small_p, getw, emit, (gather[4],))

    packed = _pack_small(small_g, loss)
    small_gather = _split_start([packed], N_DEV, True, "ag_start_small")

    grads, delta, new_m, new_v = {}, {}, {}, {}

    def update(n, after):
        parts, land = _split_wait(*exchanges[n], after, False, "rs_wait_" + n)
        own = lax.dynamic_index_in_dim(parts, me, 0, keepdims=False)
        g, d, m2, v2 = _adamw(own, land, wts[n][0], mom[n][0], var[n][0], "adamw_" + n)
        grads[n], delta[n], new_m[n], new_v[n] = g[None], d[None], m2[None], v2[None]
        return d

    after = small_gather[4]
    for n in exchanges:
        if n != "w_in":
            after = update(n, after)

    _, land = _split_wait(small_gather, 0, after, True, "ag_wait_small")
    gathered = jnp.where(dev == me, packed[None], land)
    g, d, m2, v2 = _adamw_small(gathered, _pack_small(small_p), _pack_small({n: mom[n] for n in SMALL}),
                                _pack_small({n: var[n] for n in SMALL}))
    for dst, flat in ((grads, g), (delta, d), (new_m, m2), (new_v, v2)):
        dst.update(_unpack_small(flat, small_p))
    total = g.reshape(-1)[SMALL_USED]
    update("w_in", d)

    return (total, grad_x[None], *[grads[n] for n in WEIGHTS], *[delta[n] for n in WEIGHTS],
            *[new_m[n] for n in WEIGHTS], *[new_v[n] for n in WEIGHTS])
```

```python
import functools
import math

import jax
import jax.numpy as jnp
from jax import lax
from jax.experimental import pallas as pl
from jax.experimental.pallas import tpu as pltpu

F32 = jnp.float32
BF16 = jnp.bfloat16

D_MODEL = 1024
N_MEM = 256
D_FF = 4096
HEAD = 64
SWA_HEADS = 8
SWA_BLOCK = 128
MEM_HEADS = 4
MEM_HEAD = 128
EPS = 1e-6
NEG = -1e30
REL_BUCKETS = 32
REL_MAX_DIST = 128

ADAM_LR = 0.001
ADAM_B1 = 0.9
ADAM_B2 = 0.999
ADAM_EPS = 1e-08
ADAM_WD = 0.01
ADAM_STEP = 10

GL0, QF0, KF0, VF0, QM0, QA0, KA0, VA0, FL0 = 0, 3072, 3584, 4096, 4608, 5120, 5632, 5760, 5888
PROJ_W = 6144
HALF_W = 3072
H_QF, H_KF, H_VF, H_QM, H_QA, H_KA, H_VA, H_FL = 0, 512, 1024, 1536, 2048, 2560, 2688, 2816

VMEM_LIMIT = 56 * 1024 * 1024
N_DEV = 8
MESH = pl.DeviceIdType.MESH

NN = (((1,), (0,)), ((), ()))
NT = (((1,), (1,)), ((), ()))
TN = (((0,), (0,)), ((), ()))


def _dot(a, b, dims=NN):
    return lax.dot_general(a, b, dims, preferred_element_type=F32)


def _params(sem):
    return pltpu.CompilerParams(dimension_semantics=sem, vmem_limit_bytes=VMEM_LIMIT)


def _full(shape):
    nd = len(shape)
    return pl.BlockSpec(shape, lambda *_: (0,) * nd)


def _sigmoid(z):
    return 1.0 / (1.0 + jnp.exp(-z))


def _group_mean(v, hd):
    if hd == 128:
        return jnp.mean(v, axis=-1, keepdims=True)
    lane = lax.broadcasted_iota(jnp.int32, v.shape, 1)
    lo = lane < HEAD
    s_lo = jnp.sum(jnp.where(lo, v, 0.0), axis=-1, keepdims=True)
    s_hi = jnp.sum(jnp.where(lo, 0.0, v), axis=-1, keepdims=True)
    return jnp.where(lo, s_lo, s_hi) * (1.0 / HEAD)


def _mm(a, b, mode, out_dtype, tm, tn, tk, name, column_chunks=False):
    if mode == "nn":
        m, k = a.shape
        n = b.shape[1]
    elif mode == "nt":
        m, k = a.shape
        n = b.shape[0]
    else:
        k, m = a.shape
        n = b.shape[1]
    tm, tn, tk = min(tm, m), min(tn, n), min(tk, k)
    nk = k // tk
    chunk = n // N_DEV
    per_tile = tn // chunk if column_chunks else 1
    dims = {"nn": NN, "nt": NT, "tn": TN}[mode]
    a_spec = pl.BlockSpec((tk, tm), lambda j, i, kk: (kk, i)) if mode == "tn" else pl.BlockSpec((tm, tk), lambda j, i, kk: (i, kk))
    b_spec = pl.BlockSpec((tn, tk), lambda j, i, kk: (j, kk)) if mode == "nt" else pl.BlockSpec((tk, tn), lambda j, i, kk: (kk, j))

    def body(a_ref, b_ref, o_ref, *acc):
        prod = _dot(a_ref[...].astype(BF16), b_ref[...].astype(BF16), dims)

        def write(res):
            if column_chunks:
                for c in range(per_tile):
                    o_ref[c] = res[:, c * chunk:(c + 1) * chunk].astype(o_ref.dtype)
            else:
                o_ref[...] = res.astype(o_ref.dtype)

        if nk == 1:
            write(prod)
        else:
            acc_ref, = acc
            kk = pl.program_id(2)

            @pl.when(kk == 0)
            def _():
                acc_ref[...] = prod

            @pl.when(kk > 0)
            def _():
                acc_ref[...] += prod

            @pl.when(kk == nk - 1)
            def _():
                write(acc_ref[...])

    return pl.pallas_call(
        body, name=name, grid=(n // tn, m // tm, nk),
        in_specs=[a_spec, b_spec],
        out_specs=(pl.BlockSpec((per_tile, tm, chunk), lambda j, i, kk: (j, i, 0)) if column_chunks
                   else pl.BlockSpec((tm, tn), lambda j, i, kk: (i, j))),
        out_shape=jax.ShapeDtypeStruct((N_DEV, m, chunk) if column_chunks else (m, n), out_dtype),
        scratch_shapes=[pltpu.VMEM((tm, tn), F32)] if nk > 1 else [],
        compiler_params=_params(("parallel", "parallel", "arbitrary")),
    )(a, b)


def _rms_fwd(x, g, name, deps=()):
    s, d = x.shape
    tm = min(512, s)

    def body(x_ref, g_ref, *rest):
        h_ref = rest[len(deps)]
        xv = x_ref[...]
        r = lax.rsqrt(jnp.mean(xv * xv, axis=-1, keepdims=True) + EPS)
        h_ref[...] = (xv * r * g_ref[...]).astype(BF16)

    return pl.pallas_call(
        body, name=name, grid=(s // tm,),
        in_specs=[pl.BlockSpec((tm, d), lambda i: (i, 0)), _full((1, d))] + [pl.BlockSpec(memory_space=pl.ANY)] * len(deps),
        out_specs=pl.BlockSpec((tm, d), lambda i: (i, 0)),
        out_shape=jax.ShapeDtypeStruct((s, d), BF16),
        compiler_params=_params(("parallel",)),
    )(x, g, *deps)


def _proj_post(proj, gq_fox, gk_fox, gq_mem, gq_swa, gk_swa):
    s = proj.shape[0]
    tm = min(256, s)

    def body(p_ref, gqf, gkf, gqm, gqa, gka, qf_ref, kf_ref, vf_ref, qm_ref, qa_ref, ka_ref, va_ref):
        def norm(off, width, hd, g_ref, o_ref):
            for b in range(width // 128):
                v = p_ref[:, off + b * 128: off + (b + 1) * 128].astype(F32)
                r = lax.rsqrt(_group_mean(v * v, hd) + EPS)
                o_ref[:, b * 128:(b + 1) * 128] = (v * r * g_ref[...]).astype(BF16)

        norm(H_QF, 512, HEAD, gqf, qf_ref)
        norm(H_KF, 512, HEAD, gkf, kf_ref)
        vf_ref[...] = p_ref[:, H_VF:H_VF + 512].astype(BF16)
        norm(H_QM, 512, MEM_HEAD, gqm, qm_ref)
        norm(H_QA, 512, HEAD, gqa, qa_ref)
        norm(H_KA, 128, HEAD, gka, ka_ref)
        va_ref[...] = p_ref[:, H_VA:H_VA + 128].astype(BF16)

    g_spec = _full((1, 128))
    o512 = pl.BlockSpec((tm, 512), lambda i: (i, 0))
    o128 = pl.BlockSpec((tm, 128), lambda i: (i, 0))
    s512 = jax.ShapeDtypeStruct((s, 512), BF16)
    s128 = jax.ShapeDtypeStruct((s, 128), BF16)
    return pl.pallas_call(
        body, name="proj_post", grid=(s // tm,),
        in_specs=[pl.BlockSpec((tm, HALF_W), lambda i: (i, 1)), g_spec, g_spec, g_spec, g_spec, g_spec],
        out_specs=[o512, o512, o512, o512, o512, o128, o128],
        out_shape=[s512, s512, s512, s512, s512, s128, s128],
        compiler_params=_params(("parallel",)),
    )(proj, gq_fox, gk_fox, gq_mem, gq_swa, gk_swa)


def _tri(n, lower):
    r = lax.broadcasted_iota(jnp.int32, (n, n), 0)
    c = lax.broadcasted_iota(jnp.int32, (n, n), 1)
    return jnp.where((c <= r) if lower else (c >= r), 1.0, 0.0).astype(F32)


def _fox_gate_fwd(proj, b_forget128):
    s = proj.shape[0]
    tm = min(512, s)

    def body(p_ref, b_ref, cc_ref, carry_ref):
        i = pl.program_id(0)

        @pl.when(i == 0)
        def _():
            carry_ref[...] = jnp.zeros_like(carry_ref)

        z = p_ref[...] + b_ref[...]
        logf = jnp.minimum(z, 0.0) - jnp.log(1.0 + jnp.exp(-jnp.abs(z)))
        c = jnp.dot(_tri(tm, True), logf, precision=lax.Precision.HIGHEST, preferred_element_type=F32) + carry_ref[...]
        carry_ref[...] = c[tm - 1:tm, :]
        for hp in range(4):
            cc_ref[hp] = c if hp == 0 else pltpu.roll(c, 128 - 2 * hp, 1)

    return pl.pallas_call(
        body, name="fox_gate_fwd", grid=(s // tm,),
        in_specs=[pl.BlockSpec((tm, 128), lambda i: (i, 0)), _full((1, 128))],
        out_specs=pl.BlockSpec((4, tm, 128), lambda i: (0, i, 0)),
        out_shape=jax.ShapeDtypeStruct((4, s, 128), F32),
        scratch_shapes=[pltpu.VMEM((1, 128), F32)],
        compiler_params=_params(("arbitrary",)),
    )(proj, b_forget128)


def _memkv_fwd(mem, g_mem, w_kv, kn_mem):
    m = mem.shape[0]

    def body(mem_ref, g_ref, w_ref, kn_ref, memn_ref, kv_ref, mk_ref, mv_ref):
        xv = mem_ref[...]
        r = lax.rsqrt(jnp.mean(xv * xv, axis=-1, keepdims=True) + EPS)
        mn = (xv * r * g_ref[...]).astype(BF16)
        memn_ref[...] = mn
        kv = _dot(mn, w_ref[...])
        kv_ref[...] = kv
        for h in range(MEM_HEADS):
            v = kv[:, h * 128:(h + 1) * 128]
            rr = lax.rsqrt(jnp.mean(v * v, axis=-1, keepdims=True) + EPS)
            mk_ref[:, h * 128:(h + 1) * 128] = (v * rr * kn_ref[...]).astype(BF16)
        mv_ref[...] = kv[:, 512:1024].astype(BF16)

    return pl.pallas_call(
        body, name="memkv_fwd",
        out_shape=[jax.ShapeDtypeStruct((m, D_MODEL), BF16), jax.ShapeDtypeStruct((m, 1024), F32),
                   jax.ShapeDtypeStruct((m, 512), BF16), jax.ShapeDtypeStruct((m, 512), BF16)],
        compiler_params=pltpu.CompilerParams(vmem_limit_bytes=VMEM_LIMIT),
    )(mem, g_mem, w_kv, kn_mem)


def _bias_table(rel_bias, bucket):
    def body(rb_ref, bk_ref, o_ref):
        bk = bk_ref[...]
        for h in range(SWA_HEADS):
            acc = jnp.zeros(bk.shape, F32)
            for b in range(REL_BUCKETS):
                acc = jnp.where(bk == b, rb_ref[b, h], acc)
            o_ref[h] = acc

    return pl.pallas_call(
        body, name="bias_table",
        in_specs=[pl.BlockSpec(memory_space=pltpu.SMEM), pl.BlockSpec(memory_space=pltpu.VMEM)],
        out_shape=jax.ShapeDtypeStruct((SWA_HEADS,) + bucket.shape, F32),
    )(rel_bias, bucket)


def _swa_valid(n):
    row = lax.broadcasted_iota(jnp.int32, (SWA_BLOCK, 2 * SWA_BLOCK), 0)
    col = lax.broadcasted_iota(jnp.int32, (SWA_BLOCK, 2 * SWA_BLOCK), 1)
    dist = row + SWA_BLOCK - col
    return (dist >= 0) & (dist < SWA_BLOCK) & ((col >= SWA_BLOCK) | (n > 0))


def _swa_fwd(qa, kp, vp, bias, sink):
    s = qa.shape[0]
    nb = s // SWA_BLOCK

    def body(sink_ref, q_ref, kp_ref, vp_ref, bias_ref, o_ref):
        n = pl.program_id(0)
        start = pl.multiple_of(n * SWA_BLOCK, SWA_BLOCK)
        k2 = kp_ref[pl.ds(start, 2 * SWA_BLOCK), :]
        v2 = vp_ref[pl.ds(start, 2 * SWA_BLOCK), :]
        valid = _swa_valid(n)
        for h in range(SWA_HEADS):
            kv = h // 4
            qh = q_ref[:, h * HEAD:(h + 1) * HEAD]
            kh = k2[:, kv * HEAD:(kv + 1) * HEAD]
            vh = v2[:, kv * HEAD:(kv + 1) * HEAD]
            sc = _dot(qh, kh, NT) * 0.125 + bias_ref[h]
            sc = jnp.where(valid, sc, NEG)
            sk = sink_ref[h]
            mx = jnp.maximum(jnp.max(sc, axis=-1, keepdims=True), sk)
            p = jnp.exp(sc - mx)
            den = jnp.sum(p, axis=-1, keepdims=True) + jnp.exp(sk - mx)
            p = p / den
            o_ref[:, h * HEAD:(h + 1) * HEAD] = _dot(p.astype(BF16), vh).astype(BF16)

    return pl.pallas_call(
        body, name="swa_fwd", grid=(nb,),
        in_specs=[pl.BlockSpec(memory_space=pltpu.SMEM),
                  pl.BlockSpec((SWA_BLOCK, 512), lambda n: (n, 0)),
                  _full(kp.shape), _full(vp.shape), _full(bias.shape)],
        out_specs=pl.BlockSpec((SWA_BLOCK, 512), lambda n: (n, 0)),
        out_shape=jax.ShapeDtypeStruct((s, 512), BF16),
        compiler_params=_params(("parallel",)),
    )(sink, qa, kp, vp, bias)


def _head_mask(e):
    lane = lax.broadcasted_iota(jnp.int32, (1, 128), 1)
    return (lane >= e * HEAD) & (lane < (e + 1) * HEAD)


FOX_FWD_TQ, FOX_FWD_TK = 1024, 1024
FOX_BWD_TK, FOX_BWD_TQ = 256, 512


def _head_rows(e):
    row = lax.broadcasted_iota(jnp.int32, (128, 1), 0)
    return (row >= e * HEAD) & (row < (e + 1) * HEAD)


def _fox_fwd(q, k, v_t, cc4):
    s = q.shape[0]
    t = min(FOX_FWD_TQ, s)
    tk = min(FOX_FWD_TK, s)
    nq = s // t

    def body(q_ref, k_ref, vt_ref, cc_ref, o_ref, lse_ref):
        i = pl.program_id(1)
        qs = q_ref[...] * jnp.asarray(0.125, BF16)
        qe = [jnp.where(_head_mask(e), qs, jnp.zeros_like(qs)) for e in range(2)]
        n_full = (i * t) // tk
        krow = lax.broadcasted_iota(jnp.int32, (tk, t), 0) + n_full * tk
        qcol = lax.broadcasted_iota(jnp.int32, (tk, t), 1) + i * t

        def step(j, carry, masked):
            ks = pl.ds(pl.multiple_of(j * tk, tk), tk)
            kj = k_ref[ks, :]
            vtj = vt_ref[:, ks]
            out = []
            for e in range(2):
                m, acc = carry[2 * e], carry[2 * e + 1]
                st = _dot(kj, qe[e], NT) - cc_ref[0, ks, e:e + 1]
                if masked:
                    st = jnp.where(krow <= qcol, st, NEG)
                m_new = jnp.maximum(m, jnp.max(st, axis=0, keepdims=True))
                alpha = jnp.exp(m - m_new)
                pt = jnp.exp(st - m_new).astype(BF16)
                vte = jnp.where(_head_rows(e), vtj, jnp.ones_like(vtj))
                out += [m_new, alpha * acc + _dot(vte, pt)]
            return tuple(out)

        init = (jnp.full((1, t), NEG, F32), jnp.zeros((128, t), F32)) * 2
        carry = lax.fori_loop(0, n_full, functools.partial(step, masked=False), init)
        m0, a0, m1, a1 = step(n_full, carry, True)
        l0 = a0[HEAD:HEAD + 1, :]
        l1 = a1[0:1, :]
        o_t = jnp.where(_head_rows(0), a0 / l0, a1 / l1)
        o_ref[...] = o_t.T.astype(BF16)
        r8 = lax.broadcasted_iota(jnp.int32, (8, t), 0)
        lse_ref[0] = jnp.where(r8 == 0, m0 + jnp.log(l0), jnp.where(r8 == 1, m1 + jnp.log(l1), 0.0))

    return pl.pallas_call(
        body, name="fox_fwd", grid=(4, nq),
        in_specs=[pl.BlockSpec((t, 128), lambda hp, i: (i, hp)),
                  pl.BlockSpec((s, 128), lambda hp, i: (0, hp)),
                  pl.BlockSpec((128, s), lambda hp, i: (hp, 0)),
                  pl.BlockSpec((1, s, 128), lambda hp, i: (hp, 0, 0))],
        out_specs=[pl.BlockSpec((t, 128), lambda hp, i: (i, hp)),
                   pl.BlockSpec((1, 8, t), lambda hp, i: (hp, 0, i))],
        out_shape=[jax.ShapeDtypeStruct((s, 512), BF16), jax.ShapeDtypeStruct((4, 8, s), F32)],
        compiler_params=_params(("parallel", "parallel")),
    )(q, k, v_t, cc4)


MEM_SCALE = MEM_HEAD ** -0.5


def _mem_fwd(qm, mk, mv):
    s = qm.shape[0]
    tq = min(512, s)

    def body(q_ref, mk_ref, mv_ref, o_ref):
        for h in range(MEM_HEADS):
            hs = slice(h * 128, (h + 1) * 128)
            sc = _dot(q_ref[:, hs], mk_ref[:, hs], NT) * MEM_SCALE
            mx = jnp.max(sc, axis=-1, keepdims=True)
            p = jnp.exp(sc - mx)
            p = p / jnp.sum(p, axis=-1, keepdims=True)
            o_ref[:, hs] = _dot(p.astype(BF16), mv_ref[:, hs]).astype(BF16)

    return pl.pallas_call(
        body, name="mem_fwd", grid=(s // tq,),
        in_specs=[pl.BlockSpec((tq, 512), lambda i: (i, 0)), _full(mk.shape), _full(mv.shape)],
        out_specs=pl.BlockSpec((tq, 512), lambda i: (i, 0)),
        out_shape=jax.ShapeDtypeStruct((s, 512), BF16),
        compiler_params=_params(("parallel",)),
    )(qm, mk, mv)


def _merge_fwd(x, oa, of, om, proj, b_gate, wa, wf, wm, w_out, g_mlp):
    s = x.shape[0]
    tm = min(256, s)

    def body(x_ref, oa_ref, of_ref, om_ref, gl_ref, bg_ref, wa_ref, wf_ref, wm_ref, wo_ref, g_ref, x1_ref, hm_ref, mg_ref):
        merged = None
        for b, (o_ref, w_ref) in enumerate(((oa_ref, wa_ref), (of_ref, wf_ref), (om_ref, wm_ref))):
            cs = slice(b * D_MODEL, (b + 1) * D_MODEL)
            y = _dot(o_ref[...], w_ref[...])
            t = _sigmoid(gl_ref[:, cs].astype(F32) + bg_ref[:, cs]) * y
            merged = t if merged is None else merged + t
        mb = merged.astype(BF16)
        mg_ref[...] = mb
        x1 = x_ref[...] + _dot(mb, wo_ref[...])
        x1_ref[...] = x1
        r = lax.rsqrt(jnp.mean(x1 * x1, axis=-1, keepdims=True) + EPS)
        hm_ref[...] = (x1 * r * g_ref[...]).astype(BF16)

    row = lambda w: pl.BlockSpec((tm, w), lambda i: (i, 0))
    return pl.pallas_call(
        body, name="merge_fwd", grid=(s // tm,),
        in_specs=[row(D_MODEL), row(512), row(512), row(512), row(HALF_W), _full((1, HALF_W)),
                  _full(wa.shape), _full(wf.shape), _full(wm.shape), _full(w_out.shape), _full((1, D_MODEL))],
        out_specs=[row(D_MODEL), row(D_MODEL), row(D_MODEL)],
        out_shape=[jax.ShapeDtypeStruct((s, D_MODEL), F32), jax.ShapeDtypeStruct((s, D_MODEL), BF16),
                   jax.ShapeDtypeStruct((s, D_MODEL), BF16)],
        compiler_params=_params(("parallel",)),
    )(x, oa, of, om, proj, b_gate, wa, wf, wm, w_out, g_mlp)


def _mlp_up(hm, w_up):
    s = hm.shape[0]
    tm, tn = min(1024, s), w_up.shape[2]

    def body(h_ref, w_ref, u_ref):
        r = jnp.maximum(_dot(h_ref[...], w_ref[0]), 0.0)
        u_ref[...] = (r * r).astype(BF16)

    return pl.pallas_call(
        body, name="mlp_up", grid=(s // tm, D_FF // tn),
        in_specs=[pl.BlockSpec((tm, D_MODEL), lambda i, j: (i, 0)), pl.BlockSpec((1, D_MODEL, tn), lambda i, j: (j, 0, 0))],
        out_specs=pl.BlockSpec((tm, tn), lambda i, j: (i, j)),
        out_shape=jax.ShapeDtypeStruct((s, D_FF), BF16),
        compiler_params=_params(("parallel", "parallel")),
    )(hm, w_up)


def _mlp_down_loss(u, w_down, x1, target):
    s = u.shape[0]
    tm = min(256, s)

    def body(u_ref, w_ref, x1_ref, t_ref, dy_ref, dyb_ref, loss_ref):
        i = pl.program_id(0)

        @pl.when(i == 0)
        def _():
            loss_ref[...] = jnp.zeros_like(loss_ref)

        y = x1_ref[...] + _dot(u_ref[...], w_ref[...])
        err = y - t_ref[...]
        dy = err * (1.0 / D_MODEL)
        dy_ref[...] = dy
        dyb_ref[...] = dy.astype(BF16)
        part = jnp.sum(jnp.sum(err * err, axis=-1, keepdims=True) * (1.0 / D_MODEL), axis=0, keepdims=True)
        loss_ref[...] += 0.5 * part

    row = pl.BlockSpec((tm, D_MODEL), lambda i: (i, 0))
    return pl.pallas_call(
        body, name="mlp_down_loss", grid=(s // tm,),
        in_specs=[pl.BlockSpec((tm, D_FF), lambda i: (i, 0)), _full(w_down.shape), row, row],
        out_specs=[row, row, _full((1, 1))],
        out_shape=[jax.ShapeDtypeStruct((s, D_MODEL), F32), jax.ShapeDtypeStruct((s, D_MODEL), BF16),
                   jax.ShapeDtypeStruct((1, 1), F32)],
        compiler_params=_params(("arbitrary",)),
    )(u, w_down, x1, target)


def _mlp_bwd_act(dy, w_down, u):
    s = dy.shape[0]
    tm, tn = min(1024, s), 1024

    def body(dy_ref, w_ref, u_ref, da_ref):
        du = _dot(dy_ref[...], w_ref[...], NT)
        da_ref[...] = (du * (2.0 * jnp.sqrt(u_ref[...].astype(F32)))).astype(BF16)

    return pl.pallas_call(
        body, name="mlp_bwd_act", grid=(D_FF // tn, s // tm),
        in_specs=[pl.BlockSpec((tm, D_MODEL), lambda j, i: (i, 0)), pl.BlockSpec((tn, D_MODEL), lambda j, i: (j, 0)),
                  pl.BlockSpec((tm, tn), lambda j, i: (i, j))],
        out_specs=pl.BlockSpec((tm, tn), lambda j, i: (i, j)),
        out_shape=jax.ShapeDtypeStruct((s, D_FF), BF16),
        compiler_params=_params(("parallel", "parallel")),
    )(dy, w_down, u)


def _rms_bwd(xv, g, dh, skip):
    r = lax.rsqrt(jnp.mean(xv * xv, axis=-1, keepdims=True) + EPS)
    n = xv * r
    dn = dh * g
    dx = skip + r * (dn - n * jnp.mean(dn * n, axis=-1, keepdims=True))
    return dx, jnp.sum(dh * n, axis=0, keepdims=True)


def _mlp_bwd_x(da, w_up, x1, dy, g_mlp):
    s = da.shape[0]
    tm = min(256, s)

    def body(da_ref, w_ref, x1_ref, dy_ref, g_ref, dx1_ref, dg_ref):
        i = pl.program_id(0)

        @pl.when(i == 0)
        def _():
            dg_ref[...] = jnp.zeros_like(dg_ref)

        tn = w_ref.shape[2]
        dhm = _dot(da_ref[:, 0:tn], w_ref[0], NT)
        for j in range(1, N_DEV):
            dhm = dhm + _dot(da_ref[:, j * tn:(j + 1) * tn], w_ref[j], NT)
        dx, dg = _rms_bwd(x1_ref[...], g_ref[...], dhm, dy_ref[...])
        dx1_ref[...] = dx
        dg_ref[...] += dg

    row = pl.BlockSpec((tm, D_MODEL), lambda i: (i, 0))
    return pl.pallas_call(
        body, name="mlp_bwd_x", grid=(s // tm,),
        in_specs=[pl.BlockSpec((tm, D_FF), lambda i: (i, 0)), _full(w_up.shape), row, row, _full((1, D_MODEL))],
        out_specs=[row, _full((1, D_MODEL))],
        out_shape=[jax.ShapeDtypeStruct((s, D_MODEL), F32), jax.ShapeDtypeStruct((1, D_MODEL), F32)],
        compiler_params=_params(("arbitrary",)),
    )(da, w_up, x1, dy, g_mlp)


def _merge_bwd(dx1, oa, of, om, proj, b_gate, wa, wf, wm, w_out):
    s = dx1.shape[0]
    tm = min(256, s)

    def body(dx1_ref, oa_ref, of_ref, om_ref, gl_ref, bg_ref, wa_ref, wf_ref, wm_ref, wo_ref,
             dp_ref, doa_ref, dof_ref, dom_ref, dya_ref, dyf_ref, dym_ref, dbg_ref):
        i = pl.program_id(0)

        @pl.when(i == 0)
        def _():
            dbg_ref[...] = jnp.zeros_like(dbg_ref)

        dmerged = _dot(dx1_ref[...].astype(BF16), wo_ref[...], NT)
        branches = ((oa_ref, wa_ref, doa_ref, dya_ref), (of_ref, wf_ref, dof_ref, dyf_ref), (om_ref, wm_ref, dom_ref, dym_ref))
        for b, (o_ref, w_ref, do_ref, dyb_ref) in enumerate(branches):
            cs = slice(b * D_MODEL, (b + 1) * D_MODEL)
            y = _dot(o_ref[...], w_ref[...])
            g = _sigmoid(gl_ref[:, cs].astype(F32) + bg_ref[:, cs])
            dz = (dmerged * y) * g * (1.0 - g)
            dp_ref[:, cs] = dz.astype(BF16)
            dbg_ref[:, cs] += jnp.sum(dz, axis=0, keepdims=True)
            dyb = (dmerged * g).astype(BF16)
            dyb_ref[...] = dyb
            do_ref[...] = _dot(dyb, w_ref[...], NT).astype(BF16)

    row = lambda w: pl.BlockSpec((tm, w), lambda i: (i, 0))
    sd = lambda w: jax.ShapeDtypeStruct((s, w), BF16)
    return pl.pallas_call(
        body, name="merge_bwd", grid=(s // tm,),
        in_specs=[row(D_MODEL), row(512), row(512), row(512), row(HALF_W), _full((1, HALF_W)),
                  _full(wa.shape), _full(wf.shape), _full(wm.shape), _full(w_out.shape)],
        out_specs=[row(HALF_W), row(512), row(512), row(512), row(D_MODEL), row(D_MODEL), row(D_MODEL), _full((1, HALF_W))],
        out_shape=[sd(PROJ_W), sd(512), sd(512), sd(512), sd(D_MODEL), sd(D_MODEL), sd(D_MODEL),
                   jax.ShapeDtypeStruct((1, HALF_W), F32)],
        compiler_params=_params(("arbitrary",)),
    )(dx1, oa, of, om, proj, b_gate, wa, wf, wm, w_out)


def _swa_bwd(qa, kp, vp, bias, sink, doa):
    s = qa.shape[0]
    nb = s // SWA_BLOCK

    def body(sink_ref, q_ref, kp_ref, vp_ref, bias_ref, do_ref, dq_ref, dkp_ref, dvp_ref, dbias_ref, dsink_ref, sk_acc):
        n = pl.program_id(0)

        @pl.when(n == 0)
        def _():
            dkp_ref[...] = jnp.zeros_like(dkp_ref)
            dvp_ref[...] = jnp.zeros_like(dvp_ref)
            dbias_ref[...] = jnp.zeros_like(dbias_ref)
            sk_acc[...] = jnp.zeros_like(sk_acc)

        start = pl.multiple_of(n * SWA_BLOCK, SWA_BLOCK)
        win = pl.ds(start, 2 * SWA_BLOCK)
        k2 = kp_ref[win, :]
        v2 = vp_ref[win, :]
        valid = _swa_valid(n)
        for kv in range(2):
            hs_kv = slice(kv * HEAD, (kv + 1) * HEAD)
            kh = k2[:, hs_kv]
            vh = v2[:, hs_kv]
            dk2 = jnp.zeros((2 * SWA_BLOCK, HEAD), F32)
            dv2 = jnp.zeros((2 * SWA_BLOCK, HEAD), F32)
            for g in range(4):
                h = kv * 4 + g
                hs = slice(h * HEAD, (h + 1) * HEAD)
                qh = q_ref[:, hs]
                doh = do_ref[:, hs]
                sc = _dot(qh, kh, NT) * 0.125 + bias_ref[h]
                sc = jnp.where(valid, sc, NEG)
                sk = sink_ref[h]
                mx = jnp.maximum(jnp.max(sc, axis=-1, keepdims=True), sk)
                p = jnp.exp(sc - mx)
                esk = jnp.exp(sk - mx)
                den = jnp.sum(p, axis=-1, keepdims=True) + esk
                p = p / den
                dp = _dot(doh, vh, NT)
                delta = jnp.sum(p * dp, axis=-1, keepdims=True)
                ds = p * (dp - delta)
                sk_acc[:, h:h + 1] += -(esk / den) * delta
                dbias_ref[h] += ds
                dsb = (ds * 0.125).astype(BF16)
                dq_ref[:, hs] = _dot(dsb, kh)
                dk2 = dk2 + _dot(dsb, qh, TN)
                dv2 = dv2 + _dot(p.astype(BF16), doh, TN)
            dkp_ref[win, hs_kv] += dk2
            dvp_ref[win, hs_kv] += dv2

        @pl.when(n == nb - 1)
        def _():
            dsink_ref[...] = jnp.sum(sk_acc[...], axis=0, keepdims=True)

    return pl.pallas_call(
        body, name="swa_bwd", grid=(nb,),
        in_specs=[pl.BlockSpec(memory_space=pltpu.SMEM),
                  pl.BlockSpec((SWA_BLOCK, 512), lambda n: (n, 0)),
                  _full(kp.shape), _full(vp.shape), _full(bias.shape),
                  pl.BlockSpec((SWA_BLOCK, 512), lambda n: (n, 0))],
        out_specs=[pl.BlockSpec((SWA_BLOCK, 512), lambda n: (n, 0)), _full(kp.shape), _full(vp.shape),
                   _full(bias.shape), _full((1, 128))],
        out_shape=[jax.ShapeDtypeStruct((s, 512), F32), jax.ShapeDtypeStruct(kp.shape, F32),
                   jax.ShapeDtypeStruct(vp.shape, F32), jax.ShapeDtypeStruct(bias.shape, F32),
                   jax.ShapeDtypeStruct((1, 128), F32)],
        scratch_shapes=[pltpu.VMEM((SWA_BLOCK, 128), F32)],
        compiler_params=_params(("arbitrary",)),
    )(sink, qa, kp, vp, bias, doa)


def _fox_bwd(q, k, v, do, o, cc4, lse4):
    s = q.shape[0]
    t = min(FOX_BWD_TK, s)
    tq = min(FOX_BWD_TQ, s)
    nq = s // t
    nqt = s // tq

    def body(q_ref, k_ref, v_ref, do_ref, o_ref, cc_ref, lse_ref,
             dqt_ref, dk_ref, dv_ref, dck_ref, dcq_ref, delta_ref, dk0, dk1, dv0, dv1, ds0, ds1):
        j = pl.program_id(1)

        @pl.when(j == 0)
        def _():
            dqt_ref[...] = jnp.zeros_like(dqt_ref)
            dcq_ref[...] = jnp.zeros_like(dcq_ref)
            lane8 = lax.broadcasted_iota(jnp.int32, (8, 128), 1)
            row8 = lax.broadcasted_iota(jnp.int32, (8, 128), 0)
            sel = jnp.where((lane8 // HEAD) == row8, 1.0, 0.0).astype(F32)

            def dl(i, c):
                rows = pl.ds(pl.multiple_of(i * tq, tq), tq)
                pr = do_ref[rows, :].astype(F32) * o_ref[rows, :].astype(F32)
                delta_ref[:, rows] = lax.dot_general(sel, pr, NT, precision=lax.Precision.HIGHEST,
                                                     preferred_element_type=F32)
                return c

            lax.fori_loop(0, nqt, dl, 0)

        kj = k_ref[...]
        vj = v_ref[...]
        ks = pl.ds(pl.multiple_of(j * t, t), t)
        kt = (kj.astype(F32) * 0.125).T.astype(BF16)
        ke = [jnp.where(_head_mask(e), kj, jnp.zeros_like(kj)) for e in range(2)]
        ve = [jnp.where(_head_mask(e), vj, jnp.zeros_like(vj)) for e in range(2)]
        kte = [jnp.where(_head_rows(e), kt, jnp.zeros_like(kt)) for e in range(2)]
        ck = [cc_ref[0, ks, e:e + 1] for e in range(2)]
        accs = ((dk0, dv0, ds0), (dk1, dv1, ds1))
        for refs in accs:
            for r in refs:
                r[...] = jnp.zeros_like(r)
        i_first = (j * t) // tq
        krow = lax.broadcasted_iota(jnp.int32, (t, tq), 0) + j * t
        qcol = lax.broadcasted_iota(jnp.int32, (t, tq), 1) + i_first * tq

        def step(i, c, masked):
            rows = pl.ds(pl.multiple_of(i * tq, tq), tq)
            qs = q_ref[rows, :] * jnp.asarray(0.125, BF16)
            doi = do_ref[rows, :]
            for e in range(2):
                dk_acc, dv_acc, ds_acc = accs[e]
                st = _dot(ke[e], qs, NT) - ck[e]
                if masked:
                    st = jnp.where(krow <= qcol, st, NEG)
                pt = jnp.exp(st - lse_ref[0, e:e + 1, rows])
                dpt = _dot(ve[e], doi, NT)
                dst = pt * (dpt - delta_ref[e:e + 1, rows])
                dsb = dst.astype(BF16)
                dv_acc[...] += _dot(pt.astype(BF16), doi)
                dk_acc[...] += _dot(dsb, qs)
                dqt_ref[:, rows] += _dot(kte[e], dsb)
                ds_acc[...] += dst
                dcq_ref[0, e:e + 1, rows] += jnp.sum(dst, axis=0, keepdims=True)
            return c

        step(i_first, 0, True)
        lax.fori_loop(i_first + 1, nqt, functools.partial(step, masked=False), 0)
        m0 = _head_mask(0)
        dk_ref[...] = jnp.where(m0, dk0[...], dk1[...])
        dv_ref[...] = jnp.where(m0, dv0[...], dv1[...])
        lane = lax.broadcasted_iota(jnp.int32, (t, 128), 1)
        c0 = jnp.sum(ds0[...], axis=-1, keepdims=True)
        c1 = jnp.sum(ds1[...], axis=-1, keepdims=True)
        dck_ref[0] = jnp.where(lane == 0, c0, jnp.where(lane == 1, c1, 0.0))

    res = lambda: pl.BlockSpec((s, 128), lambda hp, j: (0, hp))
    blk = lambda: pl.BlockSpec((t, 128), lambda hp, j: (j, hp))
    return pl.pallas_call(
        body, name="fox_bwd", grid=(4, nq),
        in_specs=[res(), blk(), blk(), res(), res(), pl.BlockSpec((1, s, 128), lambda hp, j: (hp, 0, 0)),
                  pl.BlockSpec((1, 8, s), lambda hp, j: (hp, 0, 0))],
        out_specs=[pl.BlockSpec((128, s), lambda hp, j: (hp, 0)), blk(), blk(),
                   pl.BlockSpec((1, t, 128), lambda hp, j: (hp, j, 0)),
                   pl.BlockSpec((1, 8, s), lambda hp, j: (hp, 0, 0))],
        out_shape=[jax.ShapeDtypeStruct((512, s), F32), jax.ShapeDtypeStruct((s, 512), F32),
                   jax.ShapeDtypeStruct((s, 512), F32), jax.ShapeDtypeStruct((4, s, 128), F32),
                   jax.ShapeDtypeStruct((4, 8, s), F32)],
        scratch_shapes=[pltpu.VMEM((8, s), F32)] + [pltpu.VMEM((t, 128), F32)] * 4 + [pltpu.VMEM((t, tq), F32)] * 2,
        compiler_params=_params(("arbitrary", "arbitrary")),
    )(q, k, v, do, o, cc4, lse4)


def _mem_bwd(qm, mk, mv, dom):
    s = qm.shape[0]
    tq = min(512, s)

    def body(q_ref, mk_ref, mv_ref, do_ref, dq_ref, dmk_ref, dmv_ref):
        i = pl.program_id(0)

        @pl.when(i == 0)
        def _():
            dmk_ref[...] = jnp.zeros_like(dmk_ref)
            dmv_ref[...] = jnp.zeros_like(dmv_ref)

        for h in range(MEM_HEADS):
            hs = slice(h * 128, (h + 1) * 128)
            qh = q_ref[:, hs]
            doh = do_ref[:, hs]
            sc = _dot(qh, mk_ref[:, hs], NT) * MEM_SCALE
            mx = jnp.max(sc, axis=-1, keepdims=True)
            p = jnp.exp(sc - mx)
            p = p / jnp.sum(p, axis=-1, keepdims=True)
            dp = _dot(doh, mv_ref[:, hs], NT)
            ds = p * (dp - jnp.sum(p * dp, axis=-1, keepdims=True))
            dsb = (ds * MEM_SCALE).astype(BF16)
            dq_ref[:, hs] = _dot(dsb, mk_ref[:, hs])
            dmk_ref[:, hs] += _dot(dsb, qh, TN)
            dmv_ref[:, hs] += _dot(p.astype(BF16), doh, TN)

    return pl.pallas_call(
        body, name="mem_bwd", grid=(s // tq,),
        in_specs=[pl.BlockSpec((tq, 512), lambda i: (i, 0)), _full(mk.shape), _full(mv.shape),
                  pl.BlockSpec((tq, 512), lambda i: (i, 0))],
        out_specs=[pl.BlockSpec((tq, 512), lambda i: (i, 0)), _full(mk.shape), _full(mv.shape)],
        out_shape=[jax.ShapeDtypeStruct((s, 512), F32), jax.ShapeDtypeStruct(mk.shape, F32),
                   jax.ShapeDtypeStruct(mv.shape, F32)],
        compiler_params=_params(("arbitrary",)),
    )(qm, mk, mv, dom)


def _memkv_bwd(dmk, dmv, kv_raw, kn_mem, mem, g_mem, mem_n, w_kv):
    def body(dmk_ref, dmv_ref, kv_ref, kn_ref, mem_ref, g_ref, mn_ref, w_ref, dw_ref, dkn_ref, dg_ref, dkv_ref):
        dkn = jnp.zeros((1, 128), F32)
        for h in range(MEM_HEADS):
            hs = slice(h * 128, (h + 1) * 128)
            v = kv_ref[:, hs]
            r = lax.rsqrt(jnp.mean(v * v, axis=-1, keepdims=True) + EPS)
            n = v * r
            dn = dmk_ref[:, hs]
            dkn = dkn + jnp.sum(dn * n, axis=0, keepdims=True)
            dng = dn * kn_ref[...]
            dkv_ref[:, hs] = (r * (dng - n * jnp.mean(dng * n, axis=-1, keepdims=True))).astype(BF16)
        dkv_ref[:, 512:1024] = dmv_ref[...].astype(BF16)
        dkn_ref[...] = dkn
        dkv = dkv_ref[...]
        dw_ref[...] = _dot(mn_ref[...], dkv, TN).astype(BF16)
        dmn = _dot(dkv, w_ref[...], NT)
        xv = mem_ref[...]
        r = lax.rsqrt(jnp.mean(xv * xv, axis=-1, keepdims=True) + EPS)
        dg_ref[...] = jnp.sum(dmn * (xv * r), axis=0, keepdims=True)

    m = mem.shape[0]
    return pl.pallas_call(
        body, name="memkv_bwd",
        out_shape=[jax.ShapeDtypeStruct((D_MODEL, 1024), BF16), jax.ShapeDtypeStruct((1, 128), F32),
                   jax.ShapeDtypeStruct((1, D_MODEL), F32)],
        scratch_shapes=[pltpu.VMEM((m, 1024), BF16)],
        compiler_params=pltpu.CompilerParams(vmem_limit_bytes=VMEM_LIMIT),
    )(dmk, dmv, kv_raw, kn_mem, mem, g_mem, mem_n, w_kv)


def _fox_gate_bwd(dc, proj, b_forget128):
    s = dc.shape[0]
    tm = min(512, s)
    nt = s // tm

    def body(dc_ref, p_ref, b_ref, dfl_ref, db_ref, carry_ref):
        i = pl.program_id(0)

        @pl.when(i == 0)
        def _():
            carry_ref[...] = jnp.zeros_like(carry_ref)
            db_ref[...] = jnp.zeros_like(db_ref)

        dcv = dc_ref[...]
        dlogf = jnp.dot(_tri(tm, False), dcv, precision=lax.Precision.HIGHEST, preferred_element_type=F32) + carry_ref[...]
        carry_ref[...] += jnp.sum(dcv, axis=0, keepdims=True)
        z = p_ref[...] + b_ref[...]
        dfl = dlogf * (1.0 / (1.0 + jnp.exp(z)))
        dfl_ref[...] = dfl.astype(BF16)
        db_ref[...] += jnp.sum(dfl, axis=0, keepdims=True)

    return pl.pallas_call(
        body, name="fox_gate_bwd", grid=(nt,),
        in_specs=[pl.BlockSpec((tm, 128), lambda i: (nt - 1 - i, 0)),
                  pl.BlockSpec((tm, 128), lambda i: (nt - 1 - i, 0)), _full((1, 128))],
        out_specs=[pl.BlockSpec((tm, 128), lambda i: (nt - 1 - i, 0)), _full((1, 128))],
        out_shape=[jax.ShapeDtypeStruct((s, 128), BF16), jax.ShapeDtypeStruct((1, 128), F32)],
        scratch_shapes=[pltpu.VMEM((1, 128), F32)],
        compiler_params=_params(("arbitrary",)),
    )(dc, proj, b_forget128)


def _proj_pre_bwd(dproj, proj, dqf, dkf, dvf, dqm, dqa, dka, dva, dfl, gq_fox, gk_fox, gq_mem, gq_swa, gk_swa):
    s = proj.shape[0]
    tm = min(256, s)

    def body(dp_in, p_ref, dqf_ref, dkf_ref, dvf_ref, dqm_ref, dqa_ref, dka_ref, dva_ref, dfl_ref,
             gqf, gkf, gqm, gqa, gka, dp_ref, dgn_ref):
        i = pl.program_id(0)

        @pl.when(i == 0)
        def _():
            dgn_ref[...] = jnp.zeros_like(dgn_ref)

        def norm_bwd(off, width, hd, g_ref, dn_ref, slot):
            acc = jnp.zeros((1, 128), F32)
            for b in range(width // 128):
                v = p_ref[:, off + b * 128: off + (b + 1) * 128].astype(F32)
                r = lax.rsqrt(_group_mean(v * v, hd) + EPS)
                n = v * r
                dn = dn_ref[:, b * 128:(b + 1) * 128]
                acc = acc + jnp.sum(dn * n, axis=0, keepdims=True)
                dng = dn * g_ref[...]
                dp_ref[:, off + b * 128: off + (b + 1) * 128] = (r * (dng - n * _group_mean(dng * n, hd))).astype(BF16)
            dgn_ref[slot:slot + 1, :] += acc

        norm_bwd(H_QF, 512, HEAD, gqf, dqf_ref, 0)
        norm_bwd(H_KF, 512, HEAD, gkf, dkf_ref, 1)
        dp_ref[:, H_VF:H_VF + 512] = dvf_ref[...].astype(BF16)
        norm_bwd(H_QM, 512, MEM_HEAD, gqm, dqm_ref, 2)
        norm_bwd(H_QA, 512, HEAD, gqa, dqa_ref, 3)
        norm_bwd(H_KA, 128, HEAD, gka, dka_ref, 4)
        dp_ref[:, H_VA:H_VA + 128] = dva_ref[...].astype(BF16)
        dp_ref[:, H_FL:H_FL + 128] = dfl_ref[...]
        dp_ref[:, H_FL + 128:HALF_W] = jnp.zeros((tm, HALF_W - H_FL - 128), BF16)

    row = lambda w: pl.BlockSpec((tm, w), lambda i: (i, 0))
    g_spec = _full((1, 128))
    return pl.pallas_call(
        body, name="proj_pre_bwd", grid=(s // tm,),
        in_specs=[pl.BlockSpec(memory_space=pl.ANY), pl.BlockSpec((tm, HALF_W), lambda i: (i, 1)),
                  row(512), row(512), row(512), row(512), row(512), row(128), row(128), row(128),
                  g_spec, g_spec, g_spec, g_spec, g_spec],
        out_specs=[pl.BlockSpec((tm, HALF_W), lambda i: (i, 1)), _full((8, 128))],
        out_shape=[jax.ShapeDtypeStruct((s, PROJ_W), BF16), jax.ShapeDtypeStruct((8, 128), F32)],
        input_output_aliases={0: 0},
        compiler_params=_params(("arbitrary",)),
    )(dproj, proj, dqf, dkf, dvf, dqm, dqa, dka, dva, dfl, gq_fox, gk_fox, gq_mem, gq_swa, gk_swa)


def _in_bwd_x(dproj, w_in_p, x, g_mix, dx1):
    s = x.shape[0]
    tm = min(256, s)

    def body(dp_ref, w_ref, x_ref, g_ref, dx1_ref, gx_ref, dg_ref):
        i = pl.program_id(0)

        @pl.when(i == 0)
        def _():
            dg_ref[...] = jnp.zeros_like(dg_ref)

        dx, dg = _rms_bwd(x_ref[...], g_ref[...], _dot(dp_ref[...], w_ref[...], NT), dx1_ref[...])
        gx_ref[...] = dx
        dg_ref[...] += dg

    row = pl.BlockSpec((tm, D_MODEL), lambda i: (i, 0))
    return pl.pallas_call(
        body, name="in_bwd_x", grid=(s // tm,),
        in_specs=[pl.BlockSpec((tm, PROJ_W), lambda i: (i, 0)), _full(w_in_p.shape), row, _full((1, D_MODEL)), row],
        out_specs=[row, _full((1, D_MODEL))],
        out_shape=[jax.ShapeDtypeStruct((s, D_MODEL), F32), jax.ShapeDtypeStruct((1, D_MODEL), F32)],
        compiler_params=_params(("arbitrary",)),
    )(dproj, w_in_p, x, g_mix, dx1)


def _rel_bias_bwd(dbias, bucket):
    def body(db_ref, bk_ref, o_ref):
        bk = bk_ref[...]
        lane = lax.broadcasted_iota(jnp.int32, (1, 128), 1)
        for b in range(REL_BUCKETS):
            sel = bk == b
            acc = jnp.zeros((1, 128), F32)
            for h in range(SWA_HEADS):
                tot = jnp.sum(jnp.sum(jnp.where(sel, db_ref[h], 0.0), axis=-1, keepdims=True), axis=0, keepdims=True)
                acc = jnp.where(lane == h, tot, acc)
            o_ref[b:b + 1, :] = acc

    return pl.pallas_call(
        body, name="rel_bias_bwd",
        out_shape=jax.ShapeDtypeStruct((REL_BUCKETS, 128), F32),
        compiler_params=pltpu.CompilerParams(vmem_limit_bytes=VMEM_LIMIT),
    )(dbias, bucket)


def _my_place():
    return lax.axis_index("x"), lax.axis_index("y"), lax.axis_index("c")


def _peer(place, k):
    x, y, c = place
    return (1 - x if k & 4 else x, 1 - y if k & 2 else y, 1 - c if k & 1 else c)


def _index(place):
    x, y, c = place
    return 4 * x + 2 * y + c


HBM_SPEC = pl.BlockSpec(memory_space=pltpu.HBM)
SEM_SPEC = pl.BlockSpec(memory_space=pltpu.SEMAPHORE)
DATAFLOW = pltpu.SideEffectType.DATAFLOW_SIDE_EFFECTING


def _split_copy(src_ref, land_ref, send_sems, recv_sems, me, k, gather):
    peer = _peer(me, k)
    if gather:
        src, dst = src_ref, land_ref.at[_index(me)]
    else:
        src, dst = src_ref.at[_index(peer)], land_ref.at[k - 1]
    return pltpu.make_async_remote_copy(src_ref=src, dst_ref=dst, send_sem=send_sems.at[k - 1], recv_sem=recv_sems.at[k - 1],
                                        device_id=peer, device_id_type=MESH)


def _split_start(srcs, slots, gather, name, after=None):
    n = len(srcs)
    extra = [] if after is None else [after]

    def body(*refs):
        refs = refs[:2 * n] + refs[2 * n + len(extra):]
        src_refs, land_refs = refs[:n], refs[n:2 * n]
        send_sems, recv_sems, token = refs[2 * n:3 * n], refs[3 * n:4 * n], refs[-1]
        me = _my_place()
        for w in range(n):
            for k in range(1, N_DEV):
                _split_copy(src_refs[w], land_refs[w], send_sems[w], recv_sems[w], me, k, gather).start()
        token[...] = jnp.zeros_like(token)

    lands = [lax.empty((slots,) + (a.shape if gather else a.shape[1:]), a.dtype) for a in srcs]
    sems = [pltpu.SemaphoreType.DMA((N_DEV - 1,))] * (2 * n)
    hbm = [pltpu.HBM(a.shape, a.dtype) for a in list(srcs) + lands]
    outs = pl.pallas_call(
        body, name=name,
        out_shape=(*sems, *hbm, jax.ShapeDtypeStruct((8, 128), F32)),
        in_specs=(HBM_SPEC,) * (2 * n) + (pl.BlockSpec(memory_space=pl.ANY),) * len(extra),
        out_specs=(SEM_SPEC,) * (2 * n) + (HBM_SPEC,) * (2 * n) + (pl.BlockSpec(memory_space=pltpu.VMEM),),
        input_output_aliases={i: 2 * n + i for i in range(2 * n)},
        compiler_params=pltpu.CompilerParams(has_side_effects=DATAFLOW),
    )(*[pltpu.with_memory_space_constraint(a, pltpu.HBM) for a in list(srcs) + lands], *extra)
    return list(outs[:n]), list(outs[n:2 * n]), list(outs[2 * n:3 * n]), list(outs[3 * n:4 * n]), outs[-1]


def _split_wait(started, w, after, gather, name):
    send_sems, recv_sems, srcs, lands, _ = started

    def body(src_ref, land_ref, send_sems, recv_sems, after_ref, src_out, land_out):
        me = _my_place()
        for k in range(1, N_DEV):
            cp = _split_copy(src_ref, land_ref, send_sems, recv_sems, me, k, gather)
            cp.wait_send()
            cp.wait_recv()

    return pl.pallas_call(
        body, name=name,
        out_shape=(pltpu.HBM(srcs[w].shape, srcs[w].dtype), pltpu.HBM(lands[w].shape, lands[w].dtype)),
        in_specs=(HBM_SPEC, HBM_SPEC, SEM_SPEC, SEM_SPEC, pl.BlockSpec(memory_space=pl.ANY)),
        out_specs=(HBM_SPEC, HBM_SPEC), input_output_aliases={0: 0, 1: 1},
        compiler_params=pltpu.CompilerParams(has_side_effects=DATAFLOW),
    )(srcs[w], lands[w], send_sems[w], recv_sems[w], after)


def _adam_math(w, g, m, v):
    m2 = ADAM_B1 * m + (1.0 - ADAM_B1) * g
    v2 = ADAM_B2 * v + (1.0 - ADAM_B2) * (g * g)
    m_hat = m2 / (1.0 - ADAM_B1 ** ADAM_STEP)
    v_hat = v2 / (1.0 - ADAM_B2 ** ADAM_STEP)
    delta = -ADAM_LR * (m_hat / (jnp.sqrt(v_hat) + ADAM_EPS) + ADAM_WD * w)
    return delta, m2, v2


def _adamw(own, land, w, m, v, name):
    a, b = w.shape
    bp = own.shape[1]
    ta = min(128, a)

    def body(o_ref, p_ref, w_ref, m_ref, v_ref, g_ref, d_ref, m2_ref, v2_ref):
        g = o_ref[:, 0:b].astype(F32)
        for k in range(N_DEV - 1):
            g = g + p_ref[k, :, 0:b].astype(F32)
        delta, m2, v2 = _adam_math(w_ref[...], g, m_ref[...], v_ref[...])
        g_ref[...] = g
        d_ref[...] = delta
        m2_ref[...] = m2
        v2_ref[...] = v2

    blk = pl.BlockSpec((ta, b), lambda i: (i, 0))
    sd = jax.ShapeDtypeStruct((a, b), F32)
    return pl.pallas_call(
        body, name=name, grid=(a // ta,),
        in_specs=[pl.BlockSpec((ta, bp), lambda i: (i, 0)), pl.BlockSpec((N_DEV - 1, ta, bp), lambda i: (0, i, 0)), blk, blk, blk],
        out_specs=[blk, blk, blk, blk], out_shape=[sd, sd, sd, sd],
        compiler_params=_params(("parallel",)),
    )(own, land, w, m, v)


def _bucket_table():
    t_loc = jnp.arange(SWA_BLOCK)[:, None] + SWA_BLOCK
    s_loc = jnp.arange(2 * SWA_BLOCK)[None, :]
    dist = t_loc - s_loc
    max_exact = REL_BUCKETS // 2
    d = jnp.maximum(dist, 0)
    df = jnp.maximum(d, 1).astype(F32)
    large = max_exact + (jnp.log(df / max_exact) / math.log(REL_MAX_DIST / max_exact) * (REL_BUCKETS - max_exact)).astype(jnp.int32)
    large = jnp.minimum(large, REL_BUCKETS - 1)
    bucket = jnp.where(d < max_exact, d, large)
    band = (dist >= 0) & (dist < SWA_BLOCK)
    return bucket, band


def _tile2(g):
    return jnp.concatenate([g, g], axis=1) if g.shape[1] == HEAD else g


SHARD_W = 737
SHARD_WP = 768
IN_WIDTH = N_DEV * SHARD_W
SEGMENTS = ((GL0, 2824, 3072), (QF0, 768, 512), (KF0, 1280, 512), (VF0, 1792, 512), (QM0, 2312, 512),
            (QA0, 0, 512), (KA0, 512, 128), (VA0, 640, 128), (FL0, 2304, 8))


def _lane_plan(sources):
    plan = []
    for t in range(len(sources) // 128):
        groups = {}
        for lane in range(128):
            src = sources[128 * t + lane]
            if src is not None:
                slab, col = src
                groups.setdefault((slab, col // 128, (lane - col) % 128), []).append(lane)
        tile = []
        for key, lanes in groups.items():
            assert lanes == list(range(lanes[0], lanes[-1] + 1))
            tile.append((key, lanes[0], lanes[-1] + 1))
        plan.append(tile)
    return plan


def _assemble(tile_plan, load, rows):
    lane = lax.broadcasted_iota(jnp.int32, (1, 128), 1)
    out = jnp.zeros((rows, 128), F32)
    for (slab, st, roll), lo, hi in tile_plan:
        v = load(slab, st)
        if roll:
            v = pltpu.roll(v, roll, 1)
        out = v if (lo, hi) == (0, 128) else jnp.where((lane >= lo) & (lane < hi), v, out)
    return out


def _w_in_from_shards(land, own):
    ref_col = [None] * PROJ_W
    for p0, r0, n in SEGMENTS:
        for i in range(n):
            ref_col[p0 + i] = divmod(r0 + i, SHARD_W)
    plan = _lane_plan(ref_col)
    d_model = own.shape[0]
    tm = 256

    def body(land_ref, own_ref, o_ref):
        me = _index(_my_place())

        def load(slab, st):
            cols = slice(st * 128, (st + 1) * 128)
            return jnp.where(me == slab, own_ref[:, cols], land_ref[slab, :, cols]).astype(F32)

        for t, tile_plan in enumerate(plan):
            o_ref[:, t * 128:(t + 1) * 128] = _assemble(tile_plan, load, tm).astype(BF16)

    return pl.pallas_call(
        body, name="w_in_from_shards", grid=(d_model // tm,),
        in_specs=[pl.BlockSpec((N_DEV, tm, SHARD_WP), lambda i: (0, i, 0)), pl.BlockSpec((tm, SHARD_WP), lambda i: (i, 0))],
        out_specs=pl.BlockSpec((tm, PROJ_W), lambda i: (i, 0)),
        out_shape=jax.ShapeDtypeStruct((d_model, PROJ_W), BF16),
        compiler_params=_params(("parallel",)),
    )(land, own)


def _dw_in_to_parts(dwp):
    padded_col = [None] * IN_WIDTH
    for p0, r0, n in SEGMENTS:
        for i in range(n):
            padded_col[r0 + i] = p0 + i
    sources = []
    for d in range(N_DEV):
        sources += [(0, padded_col[SHARD_W * d + c]) if c < SHARD_W else None for c in range(SHARD_WP)]
    plan = _lane_plan(sources)
    d_model = dwp.shape[0]
    tm = 256
    tiles = SHARD_WP // 128

    def body(dw_ref, o_ref):
        load = lambda slab, st: dw_ref[:, st * 128:(st + 1) * 128].astype(F32)
        for t, tile_plan in enumerate(plan):
            d, c = divmod(t, tiles)
            o_ref[d, :, c * 128:(c + 1) * 128] = _assemble(tile_plan, load, tm).astype(BF16)

    return pl.pallas_call(
        body, name="dw_in_to_parts", grid=(d_model // tm,),
        in_specs=[pl.BlockSpec((tm, PROJ_W), lambda i: (i, 0))],
        out_specs=pl.BlockSpec((N_DEV, tm, SHARD_WP), lambda i: (0, i, 0)),
        out_shape=jax.ShapeDtypeStruct((N_DEV, d_model, SHARD_WP), BF16),
        compiler_params=_params(("parallel",)),
    )(dwp)


def _cast_shards(shards):
    names = list(shards)

    def body(*refs):
        for src, dst in zip(refs[:len(names)], refs[len(names):]):
            if dst.shape != src.shape:
                dst[...] = jnp.zeros(dst.shape, BF16)
                dst[:, 0:src.shape[1]] = src[...].astype(BF16)
            else:
                dst[...] = src[...].astype(BF16)

    out_shape = [jax.ShapeDtypeStruct((shards[n].shape[0], SHARD_WP if n == "w_in" else shards[n].shape[1]), BF16)
                 for n in names]
    outs = pl.pallas_call(body, name="cast_shards", out_shape=out_shape,
                          compiler_params=pltpu.CompilerParams(vmem_limit_bytes=VMEM_LIMIT))(*[shards[n] for n in names])
    return dict(zip(names, outs))


def _tie(x, *tokens):
    for t in tokens:
        if t is not None:
            x = x + t[0:1, 0:1]
    return x


def _local_step(x, mem, target, p, getw, emit, deps=()):
    s = x.shape[0]
    bucket, band = _bucket_table()
    bucket_m = jnp.where(band, bucket, -1).astype(jnp.int32)
    bias = _bias_table(p["rel_bias"], bucket_m)
    gqf, gkf, gqa, gka = _tile2(p["qn_fox"]), _tile2(p["kn_fox"]), _tile2(p["qn_swa"]), _tile2(p["kn_swa"])
    gqm = p["qn_mem"]
    bf128 = jnp.pad(p["b_forget"], ((0, 0), (0, 120)))
    sink = p["sink_swa"].reshape(8)

    h = _rms_fwd(x, p["g_mix"], "rms_mix", deps)
    w_in = getw("w_in", h)
    proj = _mm(h, w_in, "nn", BF16, 512, 1536, 1024, "proj")
    fl = _mm(h, w_in[:, FL0:FL0 + 128], "nn", F32, 512, 128, 1024, "proj_fl")
    qf, kf, vf, qm, qa, ka, va = _proj_post(proj, gqf, gkf, gqm, gqa, gka)
    cc4 = _fox_gate_fwd(fl, bf128)
    w_kv = getw("w_mem_kv", cc4)
    mem_n, kv_raw, mk, mv = _memkv_fwd(mem, p["g_mem"], w_kv, p["kn_mem"])
    kp = jnp.pad(ka, ((SWA_BLOCK, 0), (0, 0)))
    vp = jnp.pad(va, ((SWA_BLOCK, 0), (0, 0)))
    oa = _swa_fwd(qa, kp, vp, bias, sink)
    of, lse4 = _fox_fwd(qf, kf, jnp.transpose(vf), cc4)
    om = _mem_fwd(qm, mk, mv)
    wa, wf, wm, w_out = getw("w_o_swa", oa), getw("w_o_fox", oa), getw("w_o_mem", oa), getw("w_out", oa)
    x1, hm, merged = _merge_fwd(x, oa, of, om, proj, p["b_gate"], wa, wf, wm, w_out, p["g_mlp"])
    w_up = getw("w_mlp_up", of)
    u = _mlp_up(hm, w_up)
    w_down = getw("w_mlp_down", hm)
    dy, dy_b, loss = _mlp_down_loss(u, w_down, x1, target)

    da = _mlp_bwd_act(dy_b, w_down, u)
    t_down = emit({"w_mlp_down": _mm(u, dy_b, "tn", BF16, 1024, 1024, 512, "dw_down")})
    dx1, dg_mlp = _mlp_bwd_x(da, w_up, x1, dy, _tie(p["g_mlp"], t_down))
    t_up = emit({"w_mlp_up": _mm(hm, da, "tn", BF16, 1024, 1024, 512, "dw_up", column_chunks=True)})
    dproj, doa, dof, dom, dya, dyf, dym, db_gate = _merge_bwd(
        dx1, oa, of, om, proj, _tie(p["b_gate"], t_up), wa, wf, wm, w_out)
    t_o = emit({"w_out": _mm(merged, dx1, "tn", BF16, 512, 1024, 512, "dw_out"),
                "w_o_swa": _mm(oa, dya, "tn", BF16, 512, 1024, 512, "dw_o_swa"),
                "w_o_fox": _mm(of, dyf, "tn", BF16, 512, 1024, 512, "dw_o_fox"),
                "w_o_mem": _mm(om, dym, "tn", BF16, 512, 1024, 512, "dw_o_mem")})

    dqm, dmk, dmv = _mem_bwd(qm, mk, mv, dom)
    dw_kv, dkn_mem, dg_mem = _memkv_bwd(dmk, dmv, kv_raw, _tie(p["kn_mem"], t_o), mem, p["g_mem"], mem_n, w_kv)
    t_kv = emit({"w_mem_kv": dw_kv})
    dqa, dkp, dvp, dbias, dsink = _swa_bwd(qa, kp, vp, bias, _tie(p["sink_swa"], t_kv).reshape(8), doa)
    dqf_t, dkf, dvf, dck4, dcq4 = _fox_bwd(qf, kf, vf, dof, of, cc4, lse4)
    dqf = jnp.transpose(dqf_t)

    dcq = jnp.transpose(dcq4[:, 0:2, :], (2, 0, 1)).reshape(s, 8)
    dck = jnp.transpose(dck4[:, :, 0:2], (1, 0, 2)).reshape(s, 8)
    dc = jnp.pad(dcq - dck, ((0, 0), (0, 120)))
    dfl, db_forget = _fox_gate_bwd(dc, fl, bf128)

    dproj, dgn = _proj_pre_bwd(dproj, proj, dqf, dkf, dvf, dqm, dqa, dkp[SWA_BLOCK:], dvp[SWA_BLOCK:], dfl,
                               gqf, gkf, gqm, gqa, gka)
    t_in = emit({"w_in": _mm(h, dproj, "tn", BF16, 1024, 3072, 512, "dw_in")})
    grad_x, dg_mix = _in_bwd_x(dproj, w_in, x, _tie(p["g_mix"], t_in), dx1)
    d_rel = _rel_bias_bwd(dbias, bucket_m)

    fold = lambda r: dgn[r:r + 1, 0:HEAD] + dgn[r:r + 1, HEAD:128]
    small = {
        "g_mix": dg_mix, "b_gate": db_gate, "b_forget": db_forget[:, 0:8],
        "qn_swa": fold(3), "kn_swa": fold(4), "sink_swa": dsink[:, 0:8], "rel_bias": d_rel[:, 0:8],
        "qn_fox": fold(0), "kn_fox": fold(1), "g_mem": dg_mem, "qn_mem": dgn[2:3, :], "kn_mem": dkn_mem,
        "g_mlp": dg_mlp,
    }
    return loss, grad_x, small


SMALL = ("g_mix", "b_gate", "b_forget", "qn_swa", "kn_swa", "sink_swa", "rel_bias", "qn_fox", "kn_fox", "g_mem",
         "qn_mem", "kn_mem", "g_mlp")
BIG = ("w_in", "w_mem_kv", "w_o_swa", "w_o_fox", "w_o_mem", "w_out", "w_mlp_up", "w_mlp_down")
COL_SHARDED = ("w_in", "w_o_swa", "w_o_fox", "w_o_mem", "w_mlp_up")
WEIGHTS = ("g_mix", "w_in", "b_gate", "b_forget", "qn_swa", "kn_swa", "sink_swa", "rel_bias", "qn_fox", "kn_fox", "g_mem",
           "w_mem_kv", "qn_mem", "kn_mem", "w_o_swa", "w_o_fox", "w_o_mem", "w_out", "g_mlp", "w_mlp_up", "w_mlp_down")
SMALL_USED = 6928
SMALL_PAD = 7168


def _gathered_to_full(name, g):
    if name in COL_SHARDED:
        return jnp.transpose(g, (1, 0, 2)).reshape(g.shape[1], N_DEV * g.shape[2])
    return g.reshape(N_DEV * g.shape[1], g.shape[2])


def _full_to_parts(name, full, b):
    if name in COL_SHARDED:
        return jnp.transpose(full.reshape(full.shape[0], N_DEV, b), (1, 0, 2)).astype(BF16)
    return full.reshape(N_DEV, full.shape[0] // N_DEV, full.shape[1]).astype(BF16)


def _pack_small(d, loss=None):
    flat = jnp.concatenate([d[n].reshape(-1) for n in SMALL])
    assert flat.shape[0] == SMALL_USED
    if loss is not None:
        flat = jnp.concatenate([flat, loss.reshape(-1)])
    return jnp.pad(flat, (0, SMALL_PAD - flat.shape[0])).reshape(8, SMALL_PAD // 8)


def _unpack_small(packed, like):
    flat = packed.reshape(-1)
    out, off = {}, 0
    for n in SMALL:
        size = like[n].size
        out[n] = flat[off:off + size].reshape(like[n].shape)
        off += size
    return out


def _adamw_small(parts, w, m, v):
    def body(p_ref, w_ref, m_ref, v_ref, g_ref, d_ref, m2_ref, v2_ref):
        g = p_ref[0]
        for k in range(1, N_DEV):
            g = g + p_ref[k]
        delta, m2, v2 = _adam_math(w_ref[...], g, m_ref[...], v_ref[...])
        g_ref[...] = g
        d_ref[...] = delta
        m2_ref[...] = m2
        v2_ref[...] = v2

    sd = jax.ShapeDtypeStruct(w.shape, F32)
    return pl.pallas_call(body, name="adamw_small", out_shape=[sd, sd, sd, sd])(parts, w, m, v)


def kernel(x, mem, g_mix, w_in, b_gate, b_forget, qn_swa, kn_swa, sink_swa, rel_bias, qn_fox, kn_fox, g_mem, w_mem_kv, qn_mem, kn_mem, w_o_swa, w_o_fox, w_o_mem, w_out, g_mlp, w_mlp_up, w_mlp_down, loss_target, m_g_mix, m_w_in, m_b_gate, m_b_forget, m_qn_swa, m_kn_swa, m_sink_swa, m_rel_bias, m_qn_fox, m_kn_fox, m_g_mem, m_w_mem_kv, m_qn_mem, m_kn_mem, m_w_o_swa, m_w_o_fox, m_w_o_mem, m_w_out, m_g_mlp, m_w_mlp_up, m_w_mlp_down, v_g_mix, v_w_in, v_b_gate, v_b_forget, v_qn_swa, v_kn_swa, v_sink_swa, v_rel_bias, v_qn_fox, v_kn_fox, v_g_mem, v_w_mem_kv, v_qn_mem, v_kn_mem, v_w_o_swa, v_w_o_fox, v_w_o_mem, v_w_out, v_g_mlp, v_w_mlp_up, v_w_mlp_down):
    wts = dict(g_mix=g_mix, w_in=w_in, b_gate=b_gate, b_forget=b_forget, qn_swa=qn_swa, kn_swa=kn_swa, sink_swa=sink_swa,
               rel_bias=rel_bias, qn_fox=qn_fox, kn_fox=kn_fox, g_mem=g_mem, w_mem_kv=w_mem_kv, qn_mem=qn_mem, kn_mem=kn_mem,
               w_o_swa=w_o_swa, w_o_fox=w_o_fox, w_o_mem=w_o_mem, w_out=w_out, g_mlp=g_mlp, w_mlp_up=w_mlp_up,
               w_mlp_down=w_mlp_down)
    mom = dict(g_mix=m_g_mix, w_in=m_w_in, b_gate=m_b_gate, b_forget=m_b_forget, qn_swa=m_qn_swa, kn_swa=m_kn_swa,
               sink_swa=m_sink_swa, rel_bias=m_rel_bias, qn_fox=m_qn_fox, kn_fox=m_kn_fox, g_mem=m_g_mem, w_mem_kv=m_w_mem_kv,
               qn_mem=m_qn_mem, kn_mem=m_kn_mem, w_o_swa=m_w_o_swa, w_o_fox=m_w_o_fox, w_o_mem=m_w_o_mem, w_out=m_w_out,
               g_mlp=m_g_mlp, w_mlp_up=m_w_mlp_up, w_mlp_down=m_w_mlp_down)
    var = dict(g_mix=v_g_mix, w_in=v_w_in, b_gate=v_b_gate, b_forget=v_b_forget, qn_swa=v_qn_swa, kn_swa=v_kn_swa,
               sink_swa=v_sink_swa, rel_bias=v_rel_bias, qn_fox=v_qn_fox, kn_fox=v_kn_fox, g_mem=v_g_mem, w_mem_kv=v_w_mem_kv,
               qn_mem=v_qn_mem, kn_mem=v_kn_mem, w_o_swa=v_w_o_swa, w_o_fox=v_w_o_fox, w_o_mem=v_w_o_mem, w_out=v_w_out,
               g_mlp=v_g_mlp, w_mlp_up=v_w_mlp_up, w_mlp_down=v_w_mlp_down)

    me = _index(_my_place())
    dev = lax.broadcasted_iota(jnp.int32, (N_DEV, 1, 1), 0)

    shards = _cast_shards({n: wts[n][0] for n in BIG})
    first = _split_start([shards["w_in"]], N_DEV, True, "ag_start_w_in")
    later = {}
    full = {}

    def getw(n, after):
        if n == "w_in" and n not in full:
            _, land = _split_wait(first, 0, after, True, "ag_wait_w_in")
            later["started"] = _split_start([shards[m] for m in BIG[1:]], N_DEV, True, "ag_start_rest", after=land)
            full[n] = _w_in_from_shards(land, shards[n])
        elif n not in full:
            _, land = _split_wait(later["started"], BIG[1:].index(n), after, True, "ag_wait_" + n)
            w = jnp.where(dev == me, shards[n][None], land)
            full[n] = w if n == "w_mlp_up" else _gathered_to_full(n, w)
        return full[n]

    exchanges = {}

    def emit(grads_by_name):
        parts = []
        for n, grad in grads_by_name.items():
            if n == "w_in":
                parts.append(_dw_in_to_parts(grad))
            else:
                parts.append(grad if n == "w_mlp_up" else _full_to_parts(n, grad, wts[n].shape[2]))
        started = _split_start(parts, N_DEV - 1, False, "rs_start_" + next(iter(grads_by_name)))
        for w, n in enumerate(grads_by_name):
            exchanges[n] = (started, w)
        return started[4]

    small_p = {n: wts[n] for n in SMALL}
    loss, grad_x, small_g = _local_step(x[0], mem[0], loss_target[0], small_p, getw, emit, (first[4],))

    packed = _pack_small(small_g, loss)
    small_gather = _split_start([packed], N_DEV, True, "ag_start_small")

    grads, delta, new_m, new_v = {}, {}, {}, {}

    def update(n, after):
        parts, land = _split_wait(*exchanges[n], after, False, "rs_wait_" + n)
        own = lax.dynamic_index_in_dim(parts, me, 0, keepdims=False)
        g, d, m2, v2 = _adamw(own, land, wts[n][0], mom[n][0], var[n][0], "adamw_" + n)
        grads[n], delta[n], new_m[n], new_v[n] = g[None], d[None], m2[None], v2[None]
        return d

    after = small_gather[4]
    for n in exchanges:
        if n != "w_in":
            after = update(n, after)

    _, land = _split_wait(small_gather, 0, after, True, "ag_wait_small")
    gathered = jnp.where(dev == me, packed[None], land)
    g, d, m2, v2 = _adamw_small(gathered, _pack_small(small_p), _pack_small({n: mom[n] for n in SMALL}),
                                _pack_small({n: var[n] for n in SMALL}))
    for dst, flat in ((grads, g), (delta, d), (new_m, m2), (new_v, v2)):
        dst.update(_unpack_small(flat, small_p))
    total = g.reshape(-1)[SMALL_USED]
    update("w_in", d)

    return (total, grad_x[None], *[grads[n] for n in WEIGHTS], *[delta[n] for n in WEIGHTS],
            *[new_m[n] for n in WEIGHTS], *[new_v[n] for n in WEIGHTS])
```

```python
import functools
import math

import jax
import jax.numpy as jnp
from jax import lax
from jax.experimental import pallas as pl
from jax.experimental.pallas import tpu as pltpu

F32 = jnp.float32
BF16 = jnp.bfloat16

D_MODEL = 1024
N_MEM = 256
D_FF = 4096
HEAD = 64
SWA_HEADS = 8
SWA_BLOCK = 128
MEM_HEADS = 4
MEM_HEAD = 128
EPS = 1e-6
NEG = -1e30
REL_BUCKETS = 32
REL_MAX_DIST = 128

ADAM_LR = 0.001
ADAM_B1 = 0.9
ADAM_B2 = 0.999
ADAM_EPS = 1e-08
ADAM_WD = 0.01
ADAM_STEP = 10

GL0, QF0, KF0, VF0, QM0, QA0, KA0, VA0, FL0 = 0, 3072, 3584, 4096, 4608, 5120, 5632, 5760, 5888
PROJ_W = 6144
HALF_W = 3072
H_QF, H_KF, H_VF, H_QM, H_QA, H_KA, H_VA, H_FL = 0, 512, 1024, 1536, 2048, 2560, 2688, 2816

VMEM_LIMIT = 56 * 1024 * 1024
N_DEV = 8
MESH = pl.DeviceIdType.MESH

NN = (((1,), (0,)), ((), ()))
NT = (((1,), (1,)), ((), ()))
TN = (((0,), (0,)), ((), ()))


def _dot(a, b, dims=NN):
    return lax.dot_general(a, b, dims, preferred_element_type=F32)


def _params(sem):
    return pltpu.CompilerParams(dimension_semantics=sem, vmem_limit_bytes=VMEM_LIMIT)


def _full(shape):
    nd = len(shape)
    return pl.BlockSpec(shape, lambda *_: (0,) * nd)


def _sigmoid(z):
    return 1.0 / (1.0 + jnp.exp(-z))


def _group_mean(v, hd):
    if hd == 128:
        return jnp.mean(v, axis=-1, keepdims=True)
    lane = lax.broadcasted_iota(jnp.int32, v.shape, 1)
    lo = lane < HEAD
    s_lo = jnp.sum(jnp.where(lo, v, 0.0), axis=-1, keepdims=True)
    s_hi = jnp.sum(jnp.where(lo, 0.0, v), axis=-1, keepdims=True)
    return jnp.where(lo, s_lo, s_hi) * (1.0 / HEAD)


def _mm(a, b, mode, out_dtype, tm, tn, tk, name, column_chunks=False):
    if mode == "nn":
        m, k = a.shape
        n = b.shape[1]
    elif mode == "nt":
        m, k = a.shape
        n = b.shape[0]
    else:
        k, m = a.shape
        n = b.shape[1]
    tm, tn, tk = min(tm, m), min(tn, n), min(tk, k)
    nk = k // tk
    chunk = n // N_DEV
    per_tile = tn // chunk if column_chunks else 1
    dims = {"nn": NN, "nt": NT, "tn": TN}[mode]
    a_spec = pl.BlockSpec((tk, tm), lambda j, i, kk: (kk, i)) if mode == "tn" else pl.BlockSpec((tm, tk), lambda j, i, kk: (i, kk))
    b_spec = pl.BlockSpec((tn, tk), lambda j, i, kk: (j, kk)) if mode == "nt" else pl.BlockSpec((tk, tn), lambda j, i, kk: (kk, j))

    def body(a_ref, b_ref, o_ref, *acc):
        prod = _dot(a_ref[...].astype(BF16), b_ref[...].astype(BF16), dims)

        def write(res):
            if column_chunks:
                for c in range(per_tile):
                    o_ref[c] = res[:, c * chunk:(c + 1) * chunk].astype(o_ref.dtype)
            else:
                o_ref[...] = res.astype(o_ref.dtype)

        if nk == 1:
            write(prod)
        else:
            acc_ref, = acc
            kk = pl.program_id(2)

            @pl.when(kk == 0)
            def _():
                acc_ref[...] = prod

            @pl.when(kk > 0)
            def _():
                acc_ref[...] += prod

            @pl.when(kk == nk - 1)
            def _():
                write(acc_ref[...])

    return pl.pallas_call(
        body, name=name, grid=(n // tn, m // tm, nk),
        in_specs=[a_spec, b_spec],
        out_specs=(pl.BlockSpec((per_tile, tm, chunk), lambda j, i, kk: (j, i, 0)) if column_chunks
                   else pl.BlockSpec((tm, tn), lambda j, i, kk: (i, j))),
        out_shape=jax.ShapeDtypeStruct((N_DEV, m, chunk) if column_chunks else (m, n), out_dtype),
        scratch_shapes=[pltpu.VMEM((tm, tn), F32)] if nk > 1 else [],
        compiler_params=_params(("parallel", "parallel", "arbitrary")),
    )(a, b)


def _rms_fwd(x, g, name, deps=()):
    s, d = x.shape
    tm = min(512, s)

    def body(x_ref, g_ref, *rest):
        h_ref = rest[len(deps)]
        xv = x_ref[...]
        r = lax.rsqrt(jnp.mean(xv * xv, axis=-1, keepdims=True) + EPS)
        h_ref[...] = (xv * r * g_ref[...]).astype(BF16)

    return pl.pallas_call(
        body, name=name, grid=(s // tm,),
        in_specs=[pl.BlockSpec((tm, d), lambda i: (i, 0)), _full((1, d))] + [pl.BlockSpec(memory_space=pl.ANY)] * len(deps),
        out_specs=pl.BlockSpec((tm, d), lambda i: (i, 0)),
        out_shape=jax.ShapeDtypeStruct((s, d), BF16),
        compiler_params=_params(("parallel",)),
    )(x, g, *deps)


def _proj_post(proj, gq_fox, gk_fox, gq_mem, gq_swa, gk_swa):
    s = proj.shape[0]
    tm = min(256, s)

    def body(p_ref, gqf, gkf, gqm, gqa, gka, qf_ref, kf_ref, vf_ref, qm_ref, qa_ref, ka_ref, va_ref):
        def norm(off, width, hd, g_ref, o_ref):
            for b in range(width // 128):
                v = p_ref[:, off + b * 128: off + (b + 1) * 128].astype(F32)
                r = lax.rsqrt(_group_mean(v * v, hd) + EPS)
                o_ref[:, b * 128:(b + 1) * 128] = (v * r * g_ref[...]).astype(BF16)

        norm(H_QF, 512, HEAD, gqf, qf_ref)
        norm(H_KF, 512, HEAD, gkf, kf_ref)
        vf_ref[...] = p_ref[:, H_VF:H_VF + 512].astype(BF16)
        norm(H_QM, 512, MEM_HEAD, gqm, qm_ref)
        norm(H_QA, 512, HEAD, gqa, qa_ref)
        norm(H_KA, 128, HEAD, gka, ka_ref)
        va_ref[...] = p_ref[:, H_VA:H_VA + 128].astype(BF16)

    g_spec = _full((1, 128))
    o512 = pl.BlockSpec((tm, 512), lambda i: (i, 0))
    o128 = pl.BlockSpec((tm, 128), lambda i: (i, 0))
    s512 = jax.ShapeDtypeStruct((s, 512), BF16)
    s128 = jax.ShapeDtypeStruct((s, 128), BF16)
    return pl.pallas_call(
        body, name="proj_post", grid=(s // tm,),
        in_specs=[pl.BlockSpec((tm, HALF_W), lambda i: (i, 1)), g_spec, g_spec, g_spec, g_spec, g_spec],
        out_specs=[o512, o512, o512, o512, o512, o128, o128],
        out_shape=[s512, s512, s512, s512, s512, s128, s128],
        compiler_params=_params(("parallel",)),
    )(proj, gq_fox, gk_fox, gq_mem, gq_swa, gk_swa)


def _tri(n, lower):
    r = lax.broadcasted_iota(jnp.int32, (n, n), 0)
    c = lax.broadcasted_iota(jnp.int32, (n, n), 1)
    return jnp.where((c <= r) if lower else (c >= r), 1.0, 0.0).astype(F32)


def _fox_gate_fwd(proj, b_forget128):
    s = proj.shape[0]
    tm = min(512, s)

    def body(p_ref, b_ref, cc_ref, carry_ref):
        i = pl.program_id(0)

        @pl.when(i == 0)
        def _():
            carry_ref[...] = jnp.zeros_like(carry_ref)

        z = p_ref[...] + b_ref[...]
        logf = jnp.minimum(z, 0.0) - jnp.log(1.0 + jnp.exp(-jnp.abs(z)))
        c = jnp.dot(_tri(tm, True), logf, precision=lax.Precision.HIGHEST, preferred_element_type=F32) + carry_ref[...]
        carry_ref[...] = c[tm - 1:tm, :]
        for hp in range(4):
            cc_ref[hp] = c if hp == 0 else pltpu.roll(c, 128 - 2 * hp, 1)

    return pl.pallas_call(
        body, name="fox_gate_fwd", grid=(s // tm,),
        in_specs=[pl.BlockSpec((tm, 128), lambda i: (i, 0)), _full((1, 128))],
        out_specs=pl.BlockSpec((4, tm, 128), lambda i: (0, i, 0)),
        out_shape=jax.ShapeDtypeStruct((4, s, 128), F32),
        scratch_shapes=[pltpu.VMEM((1, 128), F32)],
        compiler_params=_params(("arbitrary",)),
    )(proj, b_forget128)


def _memkv_fwd(mem, g_mem, w_kv, kn_mem):
    m = mem.shape[0]

    def body(mem_ref, g_ref, w_ref, kn_ref, memn_ref, kv_ref, mk_ref, mv_ref):
        xv = mem_ref[...]
        r = lax.rsqrt(jnp.mean(xv * xv, axis=-1, keepdims=True) + EPS)
        mn = (xv * r * g_ref[...]).astype(BF16)
        memn_ref[...] = mn
        kv = _dot(mn, w_ref[...])
        kv_ref[...] = kv
        for h in range(MEM_HEADS):
            v = kv[:, h * 128:(h + 1) * 128]
            rr = lax.rsqrt(jnp.mean(v * v, axis=-1, keepdims=True) + EPS)
            mk_ref[:, h * 128:(h + 1) * 128] = (v * rr * kn_ref[...]).astype(BF16)
        mv_ref[...] = kv[:, 512:1024].astype(BF16)

    return pl.pallas_call(
        body, name="memkv_fwd",
        out_shape=[jax.ShapeDtypeStruct((m, D_MODEL), BF16), jax.ShapeDtypeStruct((m, 1024), F32),
                   jax.ShapeDtypeStruct((m, 512), BF16), jax.ShapeDtypeStruct((m, 512), BF16)],
        compiler_params=pltpu.CompilerParams(vmem_limit_bytes=VMEM_LIMIT),
    )(mem, g_mem, w_kv, kn_mem)


def _bias_table(rel_bias, bucket):
    def body(rb_ref, bk_ref, o_ref):
        bk = bk_ref[...]
        for h in range(SWA_HEADS):
            acc = jnp.zeros(bk.shape, F32)
            for b in range(REL_BUCKETS):
                acc = jnp.where(bk == b, rb_ref[b, h], acc)
            o_ref[h] = acc

    return pl.pallas_call(
        body, name="bias_table",
        in_specs=[pl.BlockSpec(memory_space=pltpu.SMEM), pl.BlockSpec(memory_space=pltpu.VMEM)],
        out_shape=jax.ShapeDtypeStruct((SWA_HEADS,) + bucket.shape, F32),
    )(rel_bias, bucket)


def _swa_valid(n):
    row = lax.broadcasted_iota(jnp.int32, (SWA_BLOCK, 2 * SWA_BLOCK), 0)
    col = lax.broadcasted_iota(jnp.int32, (SWA_BLOCK, 2 * SWA_BLOCK), 1)
    dist = row + SWA_BLOCK - col
    return (dist >= 0) & (dist < SWA_BLOCK) & ((col >= SWA_BLOCK) | (n > 0))


def _swa_fwd(qa, kp, vp, bias, sink):
    s = qa.shape[0]
    nb = s // SWA_BLOCK

    def body(sink_ref, q_ref, kp_ref, vp_ref, bias_ref, o_ref):
        n = pl.program_id(0)
        start = pl.multiple_of(n * SWA_BLOCK, SWA_BLOCK)
        k2 = kp_ref[pl.ds(start, 2 * SWA_BLOCK), :]
        v2 = vp_ref[pl.ds(start, 2 * SWA_BLOCK), :]
        valid = _swa_valid(n)
        heads = range(SWA_HEADS)
        hs = lambda h: slice(h * HEAD, (h + 1) * HEAD)
        sc = [jnp.where(valid, _dot(q_ref[:, hs(h)], k2[:, hs(h // 4)], NT) * 0.125 + bias_ref[h], NEG) for h in heads]
        pn = []
        for h in heads:
            sk = sink_ref[h]
            mx = jnp.maximum(jnp.max(sc[h], axis=-1, keepdims=True), sk)
            p = jnp.exp(sc[h] - mx)
            den = jnp.sum(p, axis=-1, keepdims=True) + jnp.exp(sk - mx)
            pn.append((p / den).astype(BF16))
        outs = [_dot(pn[h], v2[:, hs(h // 4)]).astype(BF16) for h in heads]
        for h in heads:
            o_ref[:, hs(h)] = outs[h]

    return pl.pallas_call(
        body, name="swa_fwd", grid=(nb,),
        in_specs=[pl.BlockSpec(memory_space=pltpu.SMEM),
                  pl.BlockSpec((SWA_BLOCK, 512), lambda n: (n, 0)),
                  _full(kp.shape), _full(vp.shape), _full(bias.shape)],
        out_specs=pl.BlockSpec((SWA_BLOCK, 512), lambda n: (n, 0)),
        out_shape=jax.ShapeDtypeStruct((s, 512), BF16),
        compiler_params=_params(("parallel",)),
    )(sink, qa, kp, vp, bias)


def _head_mask(e):
    lane = lax.broadcasted_iota(jnp.int32, (1, 128), 1)
    return (lane >= e * HEAD) & (lane < (e + 1) * HEAD)


FOX_FWD_TQ, FOX_FWD_TK = 1024, 1024
FOX_BWD_TK, FOX_BWD_TQ = 256, 512


def _head_rows(e):
    row = lax.broadcasted_iota(jnp.int32, (128, 1), 0)
    return (row >= e * HEAD) & (row < (e + 1) * HEAD)


def _fox_fwd(q, k, v_t, cc4):
    s = q.shape[0]
    t = min(FOX_FWD_TQ, s)
    tk = min(FOX_FWD_TK, s)
    nq = s // t

    def body(q_ref, k_ref, vt_ref, cc_ref, o_ref, lse_ref):
        i = pl.program_id(1)
        qs = q_ref[...] * jnp.asarray(0.125, BF16)
        qe = [jnp.where(_head_mask(e), qs, jnp.zeros_like(qs)) for e in range(2)]
        n_full = (i * t) // tk
        krow = lax.broadcasted_iota(jnp.int32, (tk, t), 0) + n_full * tk
        qcol = lax.broadcasted_iota(jnp.int32, (tk, t), 1) + i * t

        def step(j, carry, masked):
            ks = pl.ds(pl.multiple_of(j * tk, tk), tk)
            kj = k_ref[ks, :]
            vtj = vt_ref[:, ks]
            out = []
            for e in range(2):
                m, acc = carry[2 * e], carry[2 * e + 1]
                st = _dot(kj, qe[e], NT) - cc_ref[0, ks, e:e + 1]
                if masked:
                    st = jnp.where(krow <= qcol, st, NEG)
                m_new = jnp.maximum(m, jnp.max(st, axis=0, keepdims=True))
                alpha = jnp.exp(m - m_new)
                pt = jnp.exp(st - m_new).astype(BF16)
                vte = jnp.where(_head_rows(e), vtj, jnp.ones_like(vtj))
                out += [m_new, alpha * acc + _dot(vte, pt)]
            return tuple(out)

        init = (jnp.full((1, t), NEG, F32), jnp.zeros((128, t), F32)) * 2
        carry = lax.fori_loop(0, n_full, functools.partial(step, masked=False), init)
        m0, a0, m1, a1 = step(n_full, carry, True)
        l0 = a0[HEAD:HEAD + 1, :]
        l1 = a1[0:1, :]
        o_t = jnp.where(_head_rows(0), a0 / l0, a1 / l1)
        o_ref[...] = o_t.T.astype(BF16)
        r8 = lax.broadcasted_iota(jnp.int32, (8, t), 0)
        lse_ref[0] = jnp.where(r8 == 0, m0 + jnp.log(l0), jnp.where(r8 == 1, m1 + jnp.log(l1), 0.0))

    return pl.pallas_call(
        body, name="fox_fwd", grid=(4, nq),
        in_specs=[pl.BlockSpec((t, 128), lambda hp, i: (i, hp)),
                  pl.BlockSpec((s, 128), lambda hp, i: (0, hp)),
                  pl.BlockSpec((128, s), lambda hp, i: (hp, 0)),
                  pl.BlockSpec((1, s, 128), lambda hp, i: (hp, 0, 0))],
        out_specs=[pl.BlockSpec((t, 128), lambda hp, i: (i, hp)),
                   pl.BlockSpec((1, 8, t), lambda hp, i: (hp, 0, i))],
        out_shape=[jax.ShapeDtypeStruct((s, 512), BF16), jax.ShapeDtypeStruct((4, 8, s), F32)],
        compiler_params=_params(("parallel", "parallel")),
    )(q, k, v_t, cc4)


MEM_SCALE = MEM_HEAD ** -0.5


def _mem_fwd(qm, mk, mv):
    s = qm.shape[0]
    tq = min(512, s)

    def body(q_ref, mk_ref, mv_ref, o_ref):
        for h in range(MEM_HEADS):
            hs = slice(h * 128, (h + 1) * 128)
            sc = _dot(q_ref[:, hs], mk_ref[:, hs], NT) * MEM_SCALE
            mx = jnp.max(sc, axis=-1, keepdims=True)
            p = jnp.exp(sc - mx)
            p = p / jnp.sum(p, axis=-1, keepdims=True)
            o_ref[:, hs] = _dot(p.astype(BF16), mv_ref[:, hs]).astype(BF16)

    return pl.pallas_call(
        body, name="mem_fwd", grid=(s // tq,),
        in_specs=[pl.BlockSpec((tq, 512), lambda i: (i, 0)), _full(mk.shape), _full(mv.shape)],
        out_specs=pl.BlockSpec((tq, 512), lambda i: (i, 0)),
        out_shape=jax.ShapeDtypeStruct((s, 512), BF16),
        compiler_params=_params(("parallel",)),
    )(qm, mk, mv)


def _merge_fwd(x, oa, of, om, proj, b_gate, wa, wf, wm, w_out, g_mlp):
    s = x.shape[0]
    tm = min(256, s)

    def body(x_ref, oa_ref, of_ref, om_ref, gl_ref, bg_ref, wa_ref, wf_ref, wm_ref, wo_ref, g_ref, x1_ref, hm_ref, mg_ref):
        merged = None
        for b, (o_ref, w_ref) in enumerate(((oa_ref, wa_ref), (of_ref, wf_ref), (om_ref, wm_ref))):
            cs = slice(b * D_MODEL, (b + 1) * D_MODEL)
            y = _dot(o_ref[...], w_ref[...])
            t = _sigmoid(gl_ref[:, cs].astype(F32) + bg_ref[:, cs]) * y
            merged = t if merged is None else merged + t
        mb = merged.astype(BF16)
        mg_ref[...] = mb
        x1 = x_ref[...] + _dot(mb, wo_ref[...])
        x1_ref[...] = x1
        r = lax.rsqrt(jnp.mean(x1 * x1, axis=-1, keepdims=True) + EPS)
        hm_ref[...] = (x1 * r * g_ref[...]).astype(BF16)

    row = lambda w: pl.BlockSpec((tm, w), lambda i: (i, 0))
    return pl.pallas_call(
        body, name="merge_fwd", grid=(s // tm,),
        in_specs=[row(D_MODEL), row(512), row(512), row(512), row(HALF_W), _full((1, HALF_W)),
                  _full(wa.shape), _full(wf.shape), _full(wm.shape), _full(w_out.shape), _full((1, D_MODEL))],
        out_specs=[row(D_MODEL), row(D_MODEL), row(D_MODEL)],
        out_shape=[jax.ShapeDtypeStruct((s, D_MODEL), F32), jax.ShapeDtypeStruct((s, D_MODEL), BF16),
                   jax.ShapeDtypeStruct((s, D_MODEL), BF16)],
        compiler_params=_params(("parallel",)),
    )(x, oa, of, om, proj, b_gate, wa, wf, wm, w_out, g_mlp)


def _mlp_up(hm, w_up):
    s = hm.shape[0]
    tm, tn = min(1024, s), w_up.shape[2]

    def body(h_ref, w_ref, u_ref):
        r = jnp.maximum(_dot(h_ref[...], w_ref[0]), 0.0)
        u_ref[...] = (r * r).astype(BF16)

    return pl.pallas_call(
        body, name="mlp_up", grid=(s // tm, D_FF // tn),
        in_specs=[pl.BlockSpec((tm, D_MODEL), lambda i, j: (i, 0)), pl.BlockSpec((1, D_MODEL, tn), lambda i, j: (j, 0, 0))],
        out_specs=pl.BlockSpec((tm, tn), lambda i, j: (i, j)),
        out_shape=jax.ShapeDtypeStruct((s, D_FF), BF16),
        compiler_params=_params(("parallel", "parallel")),
    )(hm, w_up)


def _mlp_down_loss(u, w_down, x1, target):
    s = u.shape[0]
    tm = min(256, s)

    def body(u_ref, w_ref, x1_ref, t_ref, dy_ref, dyb_ref, loss_ref):
        i = pl.program_id(0)

        @pl.when(i == 0)
        def _():
            loss_ref[...] = jnp.zeros_like(loss_ref)

        y = x1_ref[...] + _dot(u_ref[...], w_ref[...])
        err = y - t_ref[...]
        dy = err * (1.0 / D_MODEL)
        dy_ref[...] = dy
        dyb_ref[...] = dy.astype(BF16)
        part = jnp.sum(jnp.sum(err * err, axis=-1, keepdims=True) * (1.0 / D_MODEL), axis=0, keepdims=True)
        loss_ref[...] += 0.5 * part

    row = pl.BlockSpec((tm, D_MODEL), lambda i: (i, 0))
    return pl.pallas_call(
        body, name="mlp_down_loss", grid=(s // tm,),
        in_specs=[pl.BlockSpec((tm, D_FF), lambda i: (i, 0)), _full(w_down.shape), row, row],
        out_specs=[row, row, _full((1, 1))],
        out_shape=[jax.ShapeDtypeStruct((s, D_MODEL), F32), jax.ShapeDtypeStruct((s, D_MODEL), BF16),
                   jax.ShapeDtypeStruct((1, 1), F32)],
        compiler_params=_params(("arbitrary",)),
    )(u, w_down, x1, target)


def _mlp_bwd_act(dy, w_down, u):
    s = dy.shape[0]
    tm, tn = min(1024, s), 1024

    def body(dy_ref, w_ref, u_ref, da_ref):
        du = _dot(dy_ref[...], w_ref[...], NT)
        da_ref[...] = (du * (2.0 * jnp.sqrt(u_ref[...].astype(F32)))).astype(BF16)

    return pl.pallas_call(
        body, name="mlp_bwd_act", grid=(D_FF // tn, s // tm),
        in_specs=[pl.BlockSpec((tm, D_MODEL), lambda j, i: (i, 0)), pl.BlockSpec((tn, D_MODEL), lambda j, i: (j, 0)),
                  pl.BlockSpec((tm, tn), lambda j, i: (i, j))],
        out_specs=pl.BlockSpec((tm, tn), lambda j, i: (i, j)),
        out_shape=jax.ShapeDtypeStruct((s, D_FF), BF16),
        compiler_params=_params(("parallel", "parallel")),
    )(dy, w_down, u)


def _rms_bwd(xv, g, dh, skip):
    r = lax.rsqrt(jnp.mean(xv * xv, axis=-1, keepdims=True) + EPS)
    n = xv * r
    dn = dh * g
    dx = skip + r * (dn - n * jnp.mean(dn * n, axis=-1, keepdims=True))
    return dx, jnp.sum(dh * n, axis=0, keepdims=True)


def _mlp_bwd_x(da, w_up, x1, dy, g_mlp):
    s = da.shape[0]
    tm = min(256, s)

    def body(da_ref, w_ref, x1_ref, dy_ref, g_ref, dx1_ref, dg_ref):
        i = pl.program_id(0)

        @pl.when(i == 0)
        def _():
            dg_ref[...] = jnp.zeros_like(dg_ref)

        tn = w_ref.shape[2]
        dhm = _dot(da_ref[:, 0:tn], w_ref[0], NT)
        for j in range(1, N_DEV):
            dhm = dhm + _dot(da_ref[:, j * tn:(j + 1) * tn], w_ref[j], NT)
        dx, dg = _rms_bwd(x1_ref[...], g_ref[...], dhm, dy_ref[...])
        dx1_ref[...] = dx
        dg_ref[...] += dg

    row = pl.BlockSpec((tm, D_MODEL), lambda i: (i, 0))
    return pl.pallas_call(
        body, name="mlp_bwd_x", grid=(s // tm,),
        in_specs=[pl.BlockSpec((tm, D_FF), lambda i: (i, 0)), _full(w_up.shape), row, row, _full((1, D_MODEL))],
        out_specs=[row, _full((1, D_MODEL))],
        out_shape=[jax.ShapeDtypeStruct((s, D_MODEL), F32), jax.ShapeDtypeStruct((1, D_MODEL), F32)],
        compiler_params=_params(("arbitrary",)),
    )(da, w_up, x1, dy, g_mlp)


def _merge_bwd(dx1, oa, of, om, proj, b_gate, wa, wf, wm, w_out):
    s = dx1.shape[0]
    tm = min(256, s)

    def body(dx1_ref, oa_ref, of_ref, om_ref, gl_ref, bg_ref, wa_ref, wf_ref, wm_ref, wo_ref,
             dp_ref, doa_ref, dof_ref, dom_ref, dya_ref, dyf_ref, dym_ref, dbg_ref):
        i = pl.program_id(0)

        @pl.when(i == 0)
        def _():
            dbg_ref[...] = jnp.zeros_like(dbg_ref)

        dmerged = _dot(dx1_ref[...].astype(BF16), wo_ref[...], NT)
        branches = ((oa_ref, wa_ref, doa_ref, dya_ref), (of_ref, wf_ref, dof_ref, dyf_ref), (om_ref, wm_ref, dom_ref, dym_ref))
        for b, (o_ref, w_ref, do_ref, dyb_ref) in enumerate(branches):
            cs = slice(b * D_MODEL, (b + 1) * D_MODEL)
            y = _dot(o_ref[...], w_ref[...])
            g = _sigmoid(gl_ref[:, cs].astype(F32) + bg_ref[:, cs])
            dz = (dmerged * y) * g * (1.0 - g)
            dp_ref[:, cs] = dz.astype(BF16)
            dbg_ref[:, cs] += jnp.sum(dz, axis=0, keepdims=True)
            dyb = (dmerged * g).astype(BF16)
            dyb_ref[...] = dyb
            do_ref[...] = _dot(dyb, w_ref[...], NT).astype(BF16)

    row = lambda w: pl.BlockSpec((tm, w), lambda i: (i, 0))
    sd = lambda w: jax.ShapeDtypeStruct((s, w), BF16)
    return pl.pallas_call(
        body, name="merge_bwd", grid=(s // tm,),
        in_specs=[row(D_MODEL), row(512), row(512), row(512), row(HALF_W), _full((1, HALF_W)),
                  _full(wa.shape), _full(wf.shape), _full(wm.shape), _full(w_out.shape)],
        out_specs=[row(HALF_W), row(512), row(512), row(512), row(D_MODEL), row(D_MODEL), row(D_MODEL), _full((1, HALF_W))],
        out_shape=[sd(PROJ_W), sd(512), sd(512), sd(512), sd(D_MODEL), sd(D_MODEL), sd(D_MODEL),
                   jax.ShapeDtypeStruct((1, HALF_W), F32)],
        compiler_params=_params(("arbitrary",)),
    )(dx1, oa, of, om, proj, b_gate, wa, wf, wm, w_out)


def _swa_valid_t(n):
    key = lax.broadcasted_iota(jnp.int32, (2 * SWA_BLOCK, SWA_BLOCK), 0)
    qry = lax.broadcasted_iota(jnp.int32, (2 * SWA_BLOCK, SWA_BLOCK), 1)
    dist = qry + SWA_BLOCK - key
    return (dist >= 0) & (dist < SWA_BLOCK) & ((key >= SWA_BLOCK) | (n > 0))


def _swa_bwd(qa, kp, vp, bias_t, sink, doa):
    s = qa.shape[0]
    nb = s // SWA_BLOCK

    def body(sink_ref, q_ref, kp_ref, vp_ref, bias_ref, do_ref, dq_ref, dkp_ref, dvp_ref, dbias_ref, dsink_ref, sk_acc):
        n = pl.program_id(0)

        @pl.when(n == 0)
        def _():
            dkp_ref[...] = jnp.zeros_like(dkp_ref)
            dvp_ref[...] = jnp.zeros_like(dvp_ref)
            dbias_ref[...] = jnp.zeros_like(dbias_ref)
            sk_acc[...] = jnp.zeros_like(sk_acc)

        start = pl.multiple_of(n * SWA_BLOCK, SWA_BLOCK)
        win = pl.ds(start, 2 * SWA_BLOCK)
        k2 = kp_ref[win, :]
        v2 = vp_ref[win, :]
        valid = _swa_valid_t(n)
        heads = range(SWA_HEADS)
        hs = lambda h: slice(h * HEAD, (h + 1) * HEAD)
        scale = jnp.asarray(0.125, BF16)
        q = [q_ref[:, hs(h)] for h in heads]
        do = [do_ref[:, hs(h)] for h in heads]
        kk = [k2[:, hs(kv)] for kv in range(2)]
        vv = [v2[:, hs(kv)] for kv in range(2)]
        kt = [(kk[kv].astype(F32) * 0.125).T.astype(BF16) for kv in range(2)]
        st = [jnp.where(valid, _dot(kk[h // 4], q[h], NT) * 0.125 + bias_ref[h], NEG) for h in heads]
        dpt = [_dot(vv[h // 4], do[h], NT) for h in heads]
        pt, dst = [], []
        for h in heads:
            sk = sink_ref[h]
            mx = jnp.maximum(jnp.max(st[h], axis=0, keepdims=True), sk)
            p = jnp.exp(st[h] - mx)
            esk = jnp.exp(sk - mx)
            den = jnp.sum(p, axis=0, keepdims=True) + esk
            p = p / den
            delta = jnp.sum(p * dpt[h], axis=0, keepdims=True)
            d = p * (dpt[h] - delta)
            sk_acc[h:h + 1, :] += -(esk / den) * delta
            dbias_ref[h] += d
            pt.append(p.astype(BF16))
            dst.append(d.astype(BF16))
        dq_t = [_dot(kt[h // 4], dst[h]) for h in heads]
        dq_ref[...] = jnp.concatenate(dq_t, axis=0).T
        for kv in range(2):
            group = range(4 * kv, 4 * kv + 4)
            dk = [_dot(dst[h], q[h] * scale) for h in group]
            dv = [_dot(pt[h], do[h]) for h in group]
            dkp_ref[win, hs(kv)] += (dk[0] + dk[1]) + (dk[2] + dk[3])
            dvp_ref[win, hs(kv)] += (dv[0] + dv[1]) + (dv[2] + dv[3])

        @pl.when(n == nb - 1)
        def _():
            dsink_ref[...] = jnp.broadcast_to(jnp.sum(sk_acc[...], axis=1, keepdims=True), dsink_ref.shape)

    return pl.pallas_call(
        body, name="swa_bwd", grid=(nb,),
        in_specs=[pl.BlockSpec(memory_space=pltpu.SMEM),
                  pl.BlockSpec((SWA_BLOCK, 512), lambda n: (n, 0)),
                  _full(kp.shape), _full(vp.shape), _full(bias_t.shape),
                  pl.BlockSpec((SWA_BLOCK, 512), lambda n: (n, 0))],
        out_specs=[pl.BlockSpec((SWA_BLOCK, 512), lambda n: (n, 0)), _full(kp.shape), _full(vp.shape),
                   _full(bias_t.shape), _full((SWA_HEADS, 128))],
        out_shape=[jax.ShapeDtypeStruct((s, 512), F32), jax.ShapeDtypeStruct(kp.shape, F32),
                   jax.ShapeDtypeStruct(vp.shape, F32), jax.ShapeDtypeStruct(bias_t.shape, F32),
                   jax.ShapeDtypeStruct((SWA_HEADS, 128), F32)],
        scratch_shapes=[pltpu.VMEM((SWA_HEADS, 128), F32)],
        compiler_params=_params(("arbitrary",)),
    )(sink, qa, kp, vp, bias_t, doa)


def _fox_bwd(q, k, v, do, o, cc4, lse4):
    s = q.shape[0]
    t = min(FOX_BWD_TK, s)
    tq = min(FOX_BWD_TQ, s)
    nq = s // t
    nqt = s // tq

    def body(q_ref, k_ref, v_ref, do_ref, o_ref, cc_ref, lse_ref,
             dqt_ref, dk_ref, dv_ref, dck_ref, dcq_ref, delta_ref, dk0, dk1, dv0, dv1, ds0, ds1):
        j = pl.program_id(1)

        @pl.when(j == 0)
        def _():
            dqt_ref[...] = jnp.zeros_like(dqt_ref)
            dcq_ref[...] = jnp.zeros_like(dcq_ref)
            lane8 = lax.broadcasted_iota(jnp.int32, (8, 128), 1)
            row8 = lax.broadcasted_iota(jnp.int32, (8, 128), 0)
            sel = jnp.where((lane8 // HEAD) == row8, 1.0, 0.0).astype(F32)

            def dl(i, c):
                rows = pl.ds(pl.multiple_of(i * tq, tq), tq)
                pr = do_ref[rows, :].astype(F32) * o_ref[rows, :].astype(F32)
                delta_ref[:, rows] = lax.dot_general(sel, pr, NT, precision=lax.Precision.HIGHEST,
                                                     preferred_element_type=F32)
                return c

            lax.fori_loop(0, nqt, dl, 0)

        kj = k_ref[...]
        vj = v_ref[...]
        ks = pl.ds(pl.multiple_of(j * t, t), t)
        kt = (kj.astype(F32) * 0.125).T.astype(BF16)
        ke = [jnp.where(_head_mask(e), kj, jnp.zeros_like(kj)) for e in range(2)]
        ve = [jnp.where(_head_mask(e), vj, jnp.zeros_like(vj)) for e in range(2)]
        kte = [jnp.where(_head_rows(e), kt, jnp.zeros_like(kt)) for e in range(2)]
        ck = [cc_ref[0, ks, e:e + 1] for e in range(2)]
        accs = ((dk0, dv0, ds0), (dk1, dv1, ds1))
        for refs in accs:
            for r in refs:
                r[...] = jnp.zeros_like(r)
        i_first = (j * t) // tq
        krow = lax.broadcasted_iota(jnp.int32, (t, tq), 0) + j * t
        qcol = lax.broadcasted_iota(jnp.int32, (t, tq), 1) + i_first * tq

        def step(i, c, masked):
            rows = pl.ds(pl.multiple_of(i * tq, tq), tq)
            qs = q_ref[rows, :] * jnp.asarray(0.125, BF16)
            doi = do_ref[rows, :]
            for e in range(2):
                dk_acc, dv_acc, ds_acc = accs[e]
                st = _dot(ke[e], qs, NT) - ck[e]
                if masked:
                    st = jnp.where(krow <= qcol, st, NEG)
                pt = jnp.exp(st - lse_ref[0, e:e + 1, rows])
                dpt = _dot(ve[e], doi, NT)
                dst = pt * (dpt - delta_ref[e:e + 1, rows])
                dsb = dst.astype(BF16)
                dv_acc[...] += _dot(pt.astype(BF16), doi)
                dk_acc[...] += _dot(dsb, qs)
                dqt_ref[:, rows] += _dot(kte[e], dsb)
                ds_acc[...] += dst
                dcq_ref[0, e:e + 1, rows] += jnp.sum(dst, axis=0, keepdims=True)
            return c

        step(i_first, 0, True)
        lax.fori_loop(i_first + 1, nqt, functools.partial(step, masked=False), 0)
        m0 = _head_mask(0)
        dk_ref[...] = jnp.where(m0, dk0[...], dk1[...])
        dv_ref[...] = jnp.where(m0, dv0[...], dv1[...])
        lane = lax.broadcasted_iota(jnp.int32, (t, 128), 1)
        c0 = jnp.sum(ds0[...], axis=-1, keepdims=True)
        c1 = jnp.sum(ds1[...], axis=-1, keepdims=True)
        dck_ref[0] = jnp.where(lane == 0, c0, jnp.where(lane == 1, c1, 0.0))

    res = lambda: pl.BlockSpec((s, 128), lambda hp, j: (0, hp))
    blk = lambda: pl.BlockSpec((t, 128), lambda hp, j: (j, hp))
    return pl.pallas_call(
        body, name="fox_bwd", grid=(4, nq),
        in_specs=[res(), blk(), blk(), res(), res(), pl.BlockSpec((1, s, 128), lambda hp, j: (hp, 0, 0)),
                  pl.BlockSpec((1, 8, s), lambda hp, j: (hp, 0, 0))],
        out_specs=[pl.BlockSpec((128, s), lambda hp, j: (hp, 0)), blk(), blk(),
                   pl.BlockSpec((1, t, 128), lambda hp, j: (hp, j, 0)),
                   pl.BlockSpec((1, 8, s), lambda hp, j: (hp, 0, 0))],
        out_shape=[jax.ShapeDtypeStruct((512, s), F32), jax.ShapeDtypeStruct((s, 512), F32),
                   jax.ShapeDtypeStruct((s, 512), F32), jax.ShapeDtypeStruct((4, s, 128), F32),
                   jax.ShapeDtypeStruct((4, 8, s), F32)],
        scratch_shapes=[pltpu.VMEM((8, s), F32)] + [pltpu.VMEM((t, 128), F32)] * 4 + [pltpu.VMEM((t, tq), F32)] * 2,
        compiler_params=_params(("arbitrary", "arbitrary")),
    )(q, k, v, do, o, cc4, lse4)


def _mem_bwd(qm, mk, mv, dom):
    s = qm.shape[0]
    tq = min(512, s)

    def body(q_ref, mk_ref, mv_ref, do_ref, dq_ref, dmk_ref, dmv_ref):
        i = pl.program_id(0)

        @pl.when(i == 0)
        def _():
            dmk_ref[...] = jnp.zeros_like(dmk_ref)
            dmv_ref[...] = jnp.zeros_like(dmv_ref)

        for h in range(MEM_HEADS):
            hs = slice(h * 128, (h + 1) * 128)
            qh = q_ref[:, hs]
            doh = do_ref[:, hs]
            sc = _dot(qh, mk_ref[:, hs], NT) * MEM_SCALE
            mx = jnp.max(sc, axis=-1, keepdims=True)
            p = jnp.exp(sc - mx)
            p = p / jnp.sum(p, axis=-1, keepdims=True)
            dp = _dot(doh, mv_ref[:, hs], NT)
            ds = p * (dp - jnp.sum(p * dp, axis=-1, keepdims=True))
            dsb = (ds * MEM_SCALE).astype(BF16)
            dq_ref[:, hs] = _dot(dsb, mk_ref[:, hs])
            dmk_ref[:, hs] += _dot(dsb, qh, TN)
            dmv_ref[:, hs] += _dot(p.astype(BF16), doh, TN)

    return pl.pallas_call(
        body, name="mem_bwd", grid=(s // tq,),
        in_specs=[pl.BlockSpec((tq, 512), lambda i: (i, 0)), _full(mk.shape), _full(mv.shape),
                  pl.BlockSpec((tq, 512), lambda i: (i, 0))],
        out_specs=[pl.BlockSpec((tq, 512), lambda i: (i, 0)), _full(mk.shape), _full(mv.shape)],
        out_shape=[jax.ShapeDtypeStruct((s, 512), F32), jax.ShapeDtypeStruct(mk.shape, F32),
                   jax.ShapeDtypeStruct(mv.shape, F32)],
        compiler_params=_params(("arbitrary",)),
    )(qm, mk, mv, dom)


def _memkv_bwd(dmk, dmv, kv_raw, kn_mem, mem, g_mem, mem_n, w_kv):
    def body(dmk_ref, dmv_ref, kv_ref, kn_ref, mem_ref, g_ref, mn_ref, w_ref, dw_ref, dkn_ref, dg_ref, dkv_ref):
        dkn = jnp.zeros((1, 128), F32)
        for h in range(MEM_HEADS):
            hs = slice(h * 128, (h + 1) * 128)
            v = kv_ref[:, hs]
            r = lax.rsqrt(jnp.mean(v * v, axis=-1, keepdims=True) + EPS)
            n = v * r
            dn = dmk_ref[:, hs]
            dkn = dkn + jnp.sum(dn * n, axis=0, keepdims=True)
            dng = dn * kn_ref[...]
            dkv_ref[:, hs] = (r * (dng - n * jnp.mean(dng * n, axis=-1, keepdims=True))).astype(BF16)
        dkv_ref[:, 512:1024] = dmv_ref[...].astype(BF16)
        dkn_ref[...] = dkn
        dkv = dkv_ref[...]
        dw_ref[...] = _dot(mn_ref[...], dkv, TN).astype(BF16)
        dmn = _dot(dkv, w_ref[...], NT)
        xv = mem_ref[...]
        r = lax.rsqrt(jnp.mean(xv * xv, axis=-1, keepdims=True) + EPS)
        dg_ref[...] = jnp.sum(dmn * (xv * r), axis=0, keepdims=True)

    m = mem.shape[0]
    return pl.pallas_call(
        body, name="memkv_bwd",
        out_shape=[jax.ShapeDtypeStruct((D_MODEL, 1024), BF16), jax.ShapeDtypeStruct((1, 128), F32),
                   jax.ShapeDtypeStruct((1, D_MODEL), F32)],
        scratch_shapes=[pltpu.VMEM((m, 1024), BF16)],
        compiler_params=pltpu.CompilerParams(vmem_limit_bytes=VMEM_LIMIT),
    )(dmk, dmv, kv_raw, kn_mem, mem, g_mem, mem_n, w_kv)


def _fox_gate_bwd(dc, proj, b_forget128):
    s = dc.shape[0]
    tm = min(512, s)
    nt = s // tm

    def body(dc_ref, p_ref, b_ref, dfl_ref, db_ref, carry_ref):
        i = pl.program_id(0)

        @pl.when(i == 0)
        def _():
            carry_ref[...] = jnp.zeros_like(carry_ref)
            db_ref[...] = jnp.zeros_like(db_ref)

        dcv = dc_ref[...]
        dlogf = jnp.dot(_tri(tm, False), dcv, precision=lax.Precision.HIGHEST, preferred_element_type=F32) + carry_ref[...]
        carry_ref[...] += jnp.sum(dcv, axis=0, keepdims=True)
        z = p_ref[...] + b_ref[...]
        dfl = dlogf * (1.0 / (1.0 + jnp.exp(z)))
        dfl_ref[...] = dfl.astype(BF16)
        db_ref[...] += jnp.sum(dfl, axis=0, keepdims=True)

    return pl.pallas_call(
        body, name="fox_gate_bwd", grid=(nt,),
        in_specs=[pl.BlockSpec((tm, 128), lambda i: (nt - 1 - i, 0)),
                  pl.BlockSpec((tm, 128), lambda i: (nt - 1 - i, 0)), _full((1, 128))],
        out_specs=[pl.BlockSpec((tm, 128), lambda i: (nt - 1 - i, 0)), _full((1, 128))],
        out_shape=[jax.ShapeDtypeStruct((s, 128), BF16), jax.ShapeDtypeStruct((1, 128), F32)],
        scratch_shapes=[pltpu.VMEM((1, 128), F32)],
        compiler_params=_params(("arbitrary",)),
    )(dc, proj, b_forget128)


def _proj_pre_bwd(dproj, proj, dqf, dkf, dvf, dqm, dqa, dka, dva, dfl, gq_fox, gk_fox, gq_mem, gq_swa, gk_swa):
    s = proj.shape[0]
    tm = min(256, s)

    def body(dp_in, p_ref, dqf_ref, dkf_ref, dvf_ref, dqm_ref, dqa_ref, dka_ref, dva_ref, dfl_ref,
             gqf, gkf, gqm, gqa, gka, dp_ref, dgn_ref):
        i = pl.program_id(0)

        @pl.when(i == 0)
        def _():
            dgn_ref[...] = jnp.zeros_like(dgn_ref)

        def norm_bwd(off, width, hd, g_ref, dn_ref, slot):
            acc = jnp.zeros((1, 128), F32)
            for b in range(width // 128):
                v = p_ref[:, off + b * 128: off + (b + 1) * 128].astype(F32)
                r = lax.rsqrt(_group_mean(v * v, hd) + EPS)
                n = v * r
                dn = dn_ref[:, b * 128:(b + 1) * 128]
                acc = acc + jnp.sum(dn * n, axis=0, keepdims=True)
                dng = dn * g_ref[...]
                dp_ref[:, off + b * 128: off + (b + 1) * 128] = (r * (dng - n * _group_mean(dng * n, hd))).astype(BF16)
            dgn_ref[slot:slot + 1, :] += acc

        norm_bwd(H_QF, 512, HEAD, gqf, dqf_ref, 0)
        norm_bwd(H_KF, 512, HEAD, gkf, dkf_ref, 1)
        dp_ref[:, H_VF:H_VF + 512] = dvf_ref[...].astype(BF16)
        norm_bwd(H_QM, 512, MEM_HEAD, gqm, dqm_ref, 2)
        norm_bwd(H_QA, 512, HEAD, gqa, dqa_ref, 3)
        norm_bwd(H_KA, 128, HEAD, gka, dka_ref, 4)
        dp_ref[:, H_VA:H_VA + 128] = dva_ref[...].astype(BF16)
        dp_ref[:, H_FL:H_FL + 128] = dfl_ref[...]
        dp_ref[:, H_FL + 128:HALF_W] = jnp.zeros((tm, HALF_W - H_FL - 128), BF16)

    row = lambda w: pl.BlockSpec((tm, w), lambda i: (i, 0))
    g_spec = _full((1, 128))
    return pl.pallas_call(
        body, name="proj_pre_bwd", grid=(s // tm,),
        in_specs=[pl.BlockSpec(memory_space=pl.ANY), pl.BlockSpec((tm, HALF_W), lambda i: (i, 1)),
                  row(512), row(512), row(512), row(512), row(512), row(128), row(128), row(128),
                  g_spec, g_spec, g_spec, g_spec, g_spec],
        out_specs=[pl.BlockSpec((tm, HALF_W), lambda i: (i, 1)), _full((8, 128))],
        out_shape=[jax.ShapeDtypeStruct((s, PROJ_W), BF16), jax.ShapeDtypeStruct((8, 128), F32)],
        input_output_aliases={0: 0},
        compiler_params=_params(("arbitrary",)),
    )(dproj, proj, dqf, dkf, dvf, dqm, dqa, dka, dva, dfl, gq_fox, gk_fox, gq_mem, gq_swa, gk_swa)


def _in_bwd_x(dproj, w_in_p, x, g_mix, dx1):
    s = x.shape[0]
    tm = min(256, s)

    def body(dp_ref, w_ref, x_ref, g_ref, dx1_ref, gx_ref, dg_ref):
        i = pl.program_id(0)

        @pl.when(i == 0)
        def _():
            dg_ref[...] = jnp.zeros_like(dg_ref)

        dx, dg = _rms_bwd(x_ref[...], g_ref[...], _dot(dp_ref[...], w_ref[...], NT), dx1_ref[...])
        gx_ref[...] = dx
        dg_ref[...] += dg

    row = pl.BlockSpec((tm, D_MODEL), lambda i: (i, 0))
    return pl.pallas_call(
        body, name="in_bwd_x", grid=(s // tm,),
        in_specs=[pl.BlockSpec((tm, PROJ_W), lambda i: (i, 0)), _full(w_in_p.shape), row, _full((1, D_MODEL)), row],
        out_specs=[row, _full((1, D_MODEL))],
        out_shape=[jax.ShapeDtypeStruct((s, D_MODEL), F32), jax.ShapeDtypeStruct((1, D_MODEL), F32)],
        compiler_params=_params(("arbitrary",)),
    )(dproj, w_in_p, x, g_mix, dx1)


def _rel_bias_bwd(dbias, bucket):
    def body(db_ref, bk_ref, o_ref):
        bk = bk_ref[...]
        lane = lax.broadcasted_iota(jnp.int32, (1, 128), 1)
        for b in range(REL_BUCKETS):
            sel = bk == b
            acc = jnp.zeros((1, 128), F32)
            for h in range(SWA_HEADS):
                tot = jnp.sum(jnp.sum(jnp.where(sel, db_ref[h], 0.0), axis=-1, keepdims=True), axis=0, keepdims=True)
                acc = jnp.where(lane == h, tot, acc)
            o_ref[b:b + 1, :] = acc

    return pl.pallas_call(
        body, name="rel_bias_bwd",
        out_shape=jax.ShapeDtypeStruct((REL_BUCKETS, 128), F32),
        compiler_params=pltpu.CompilerParams(vmem_limit_bytes=VMEM_LIMIT),
    )(dbias, bucket)


def _my_place():
    return lax.axis_index("x"), lax.axis_index("y"), lax.axis_index("c")


def _peer(place, k):
    x, y, c = place
    return (1 - x if k & 4 else x, 1 - y if k & 2 else y, 1 - c if k & 1 else c)


def _index(place):
    x, y, c = place
    return 4 * x + 2 * y + c


HBM_SPEC = pl.BlockSpec(memory_space=pltpu.HBM)
SEM_SPEC = pl.BlockSpec(memory_space=pltpu.SEMAPHORE)
DATAFLOW = pltpu.SideEffectType.DATAFLOW_SIDE_EFFECTING


def _split_copy(src_ref, land_ref, send_sems, recv_sems, me, k, gather):
    peer = _peer(me, k)
    if gather:
        src, dst = src_ref, land_ref.at[_index(me)]
    else:
        src, dst = src_ref.at[_index(peer)], land_ref.at[k - 1]
    return pltpu.make_async_remote_copy(src_ref=src, dst_ref=dst, send_sem=send_sems.at[k - 1], recv_sem=recv_sems.at[k - 1],
                                        device_id=peer, device_id_type=MESH)


def _split_start(srcs, slots, gather, name, after=None):
    n = len(srcs)
    extra = [] if after is None else [after]

    def body(*refs):
        refs = refs[:2 * n] + refs[2 * n + len(extra):]
        src_refs, land_refs = refs[:n], refs[n:2 * n]
        send_sems, recv_sems, token = refs[2 * n:3 * n], refs[3 * n:4 * n], refs[-1]
        me = _my_place()
        for w in range(n):
            for k in range(1, N_DEV):
                _split_copy(src_refs[w], land_refs[w], send_sems[w], recv_sems[w], me, k, gather).start()
        token[...] = jnp.zeros_like(token)

    lands = [lax.empty((slots,) + (a.shape if gather else a.shape[1:]), a.dtype) for a in srcs]
    sems = [pltpu.SemaphoreType.DMA((N_DEV - 1,))] * (2 * n)
    hbm = [pltpu.HBM(a.shape, a.dtype) for a in list(srcs) + lands]
    outs = pl.pallas_call(
        body, name=name,
        out_shape=(*sems, *hbm, jax.ShapeDtypeStruct((8, 128), F32)),
        in_specs=(HBM_SPEC,) * (2 * n) + (pl.BlockSpec(memory_space=pl.ANY),) * len(extra),
        out_specs=(SEM_SPEC,) * (2 * n) + (HBM_SPEC,) * (2 * n) + (pl.BlockSpec(memory_space=pltpu.VMEM),),
        input_output_aliases={i: 2 * n + i for i in range(2 * n)},
        compiler_params=pltpu.CompilerParams(has_side_effects=DATAFLOW),
    )(*[pltpu.with_memory_space_constraint(a, pltpu.HBM) for a in list(srcs) + lands], *extra)
    return list(outs[:n]), list(outs[n:2 * n]), list(outs[2 * n:3 * n]), list(outs[3 * n:4 * n]), outs[-1]


def _split_wait(started, w, after, gather, name):
    send_sems, recv_sems, srcs, lands, _ = started

    def body(src_ref, land_ref, send_sems, recv_sems, after_ref, src_out, land_out):
        me = _my_place()
        for k in range(1, N_DEV):
            cp = _split_copy(src_ref, land_ref, send_sems, recv_sems, me, k, gather)
            cp.wait_send()
            cp.wait_recv()

    return pl.pallas_call(
        body, name=name,
        out_shape=(pltpu.HBM(srcs[w].shape, srcs[w].dtype), pltpu.HBM(lands[w].shape, lands[w].dtype)),
        in_specs=(HBM_SPEC, HBM_SPEC, SEM_SPEC, SEM_SPEC, pl.BlockSpec(memory_space=pl.ANY)),
        out_specs=(HBM_SPEC, HBM_SPEC), input_output_aliases={0: 0, 1: 1},
        compiler_params=pltpu.CompilerParams(has_side_effects=DATAFLOW),
    )(srcs[w], lands[w], send_sems[w], recv_sems[w], after)


def _adam_math(w, g, m, v):
    m2 = ADAM_B1 * m + (1.0 - ADAM_B1) * g
    v2 = ADAM_B2 * v + (1.0 - ADAM_B2) * (g * g)
    m_hat = m2 / (1.0 - ADAM_B1 ** ADAM_STEP)
    v_hat = v2 / (1.0 - ADAM_B2 ** ADAM_STEP)
    delta = -ADAM_LR * (m_hat / (jnp.sqrt(v_hat) + ADAM_EPS) + ADAM_WD * w)
    return delta, m2, v2


def _adamw(own, land, w, m, v, name):
    a, b = w.shape
    bp = own.shape[1]
    ta = min(128, a)

    def body(o_ref, p_ref, w_ref, m_ref, v_ref, g_ref, d_ref, m2_ref, v2_ref):
        g = o_ref[:, 0:b].astype(F32)
        for k in range(N_DEV - 1):
            g = g + p_ref[k, :, 0:b].astype(F32)
        delta, m2, v2 = _adam_math(w_ref[...], g, m_ref[...], v_ref[...])
        g_ref[...] = g
        d_ref[...] = delta
        m2_ref[...] = m2
        v2_ref[...] = v2

    blk = pl.BlockSpec((ta, b), lambda i: (i, 0))
    sd = jax.ShapeDtypeStruct((a, b), F32)
    return pl.pallas_call(
        body, name=name, grid=(a // ta,),
        in_specs=[pl.BlockSpec((ta, bp), lambda i: (i, 0)), pl.BlockSpec((N_DEV - 1, ta, bp), lambda i: (0, i, 0)), blk, blk, blk],
        out_specs=[blk, blk, blk, blk], out_shape=[sd, sd, sd, sd],
        compiler_params=_params(("parallel",)),
    )(own, land, w, m, v)


def _bucket_table():
    t_loc = jnp.arange(SWA_BLOCK)[:, None] + SWA_BLOCK
    s_loc = jnp.arange(2 * SWA_BLOCK)[None, :]
    dist = t_loc - s_loc
    max_exact = REL_BUCKETS // 2
    d = jnp.maximum(dist, 0)
    df = jnp.maximum(d, 1).astype(F32)
    large = max_exact + (jnp.log(df / max_exact) / math.log(REL_MAX_DIST / max_exact) * (REL_BUCKETS - max_exact)).astype(jnp.int32)
    large = jnp.minimum(large, REL_BUCKETS - 1)
    bucket = jnp.where(d < max_exact, d, large)
    band = (dist >= 0) & (dist < SWA_BLOCK)
    return bucket, band


def _tile2(g):
    return jnp.concatenate([g, g], axis=1) if g.shape[1] == HEAD else g


SHARD_W = 737
SHARD_WP = 768
IN_WIDTH = N_DEV * SHARD_W
SEGMENTS = ((GL0, 2824, 3072), (QF0, 768, 512), (KF0, 1280, 512), (VF0, 1792, 512), (QM0, 2312, 512),
            (QA0, 0, 512), (KA0, 512, 128), (VA0, 640, 128), (FL0, 2304, 8))


def _lane_plan(sources):
    plan = []
    for t in range(len(sources) // 128):
        groups = {}
        for lane in range(128):
            src = sources[128 * t + lane]
            if src is not None:
                slab, col = src
                groups.setdefault((slab, col // 128, (lane - col) % 128), []).append(lane)
        tile = []
        for key, lanes in groups.items():
            assert lanes == list(range(lanes[0], lanes[-1] + 1))
            tile.append((key, lanes[0], lanes[-1] + 1))
        plan.append(tile)
    return plan


def _assemble(tile_plan, load, rows):
    lane = lax.broadcasted_iota(jnp.int32, (1, 128), 1)
    out = jnp.zeros((rows, 128), F32)
    for (slab, st, roll), lo, hi in tile_plan:
        v = load(slab, st)
        if roll:
            v = pltpu.roll(v, roll, 1)
        out = v if (lo, hi) == (0, 128) else jnp.where((lane >= lo) & (lane < hi), v, out)
    return out


def _w_in_from_shards(land, own):
    ref_col = [None] * PROJ_W
    for p0, r0, n in SEGMENTS:
        for i in range(n):
            ref_col[p0 + i] = divmod(r0 + i, SHARD_W)
    plan = _lane_plan(ref_col)
    d_model = own.shape[0]
    tm = 256

    def body(land_ref, own_ref, o_ref):
        me = _index(_my_place())

        def load(slab, st):
            cols = slice(st * 128, (st + 1) * 128)
            return jnp.where(me == slab, own_ref[:, cols], land_ref[slab, :, cols]).astype(F32)

        for t, tile_plan in enumerate(plan):
            o_ref[:, t * 128:(t + 1) * 128] = _assemble(tile_plan, load, tm).astype(BF16)

    return pl.pallas_call(
        body, name="w_in_from_shards", grid=(d_model // tm,),
        in_specs=[pl.BlockSpec((N_DEV, tm, SHARD_WP), lambda i: (0, i, 0)), pl.BlockSpec((tm, SHARD_WP), lambda i: (i, 0))],
        out_specs=pl.BlockSpec((tm, PROJ_W), lambda i: (i, 0)),
        out_shape=jax.ShapeDtypeStruct((d_model, PROJ_W), BF16),
        compiler_params=_params(("parallel",)),
    )(land, own)


def _dw_in_to_parts(dwp):
    padded_col = [None] * IN_WIDTH
    for p0, r0, n in SEGMENTS:
        for i in range(n):
            padded_col[r0 + i] = p0 + i
    sources = []
    for d in range(N_DEV):
        sources += [(0, padded_col[SHARD_W * d + c]) if c < SHARD_W else None for c in range(SHARD_WP)]
    plan = _lane_plan(sources)
    d_model = dwp.shape[0]
    tm = 256
    tiles = SHARD_WP // 128

    def body(dw_ref, o_ref):
        load = lambda slab, st: dw_ref[:, st * 128:(st + 1) * 128].astype(F32)
        for t, tile_plan in enumerate(plan):
            d, c = divmod(t, tiles)
            o_ref[d, :, c * 128:(c + 1) * 128] = _assemble(tile_plan, load, tm).astype(BF16)

    return pl.pallas_call(
        body, name="dw_in_to_parts", grid=(d_model // tm,),
        in_specs=[pl.BlockSpec((tm, PROJ_W), lambda i: (i, 0))],
        out_specs=pl.BlockSpec((N_DEV, tm, SHARD_WP), lambda i: (0, i, 0)),
        out_shape=jax.ShapeDtypeStruct((N_DEV, d_model, SHARD_WP), BF16),
        compiler_params=_params(("parallel",)),
    )(dwp)


def _cast_shards(shards):
    names = list(shards)

    def body(*refs):
        for src, dst in zip(refs[:len(names)], refs[len(names):]):
            if dst.shape != src.shape:
                dst[...] = jnp.zeros(dst.shape, BF16)
                dst[:, 0:src.shape[1]] = src[...].astype(BF16)
            else:
                dst[...] = src[...].astype(BF16)

    out_shape = [jax.ShapeDtypeStruct((shards[n].shape[0], SHARD_WP if n == "w_in" else shards[n].shape[1]), BF16)
                 for n in names]
    outs = pl.pallas_call(body, name="cast_shards", out_shape=out_shape,
                          compiler_params=pltpu.CompilerParams(vmem_limit_bytes=VMEM_LIMIT))(*[shards[n] for n in names])
    return dict(zip(names, outs))


def _tie(x, *tokens):
    for t in tokens:
        if t is not None:
            x = x + t[0:1, 0:1]
    return x


def _local_step(x, mem, target, p, getw, emit, deps=()):
    s = x.shape[0]
    bucket, band = _bucket_table()
    bucket_m = jnp.where(band, bucket, -1).astype(jnp.int32)
    bias = _bias_table(p["rel_bias"], bucket_m)
    bucket_t = jnp.transpose(bucket_m)
    bias_t = _bias_table(p["rel_bias"], bucket_t)
    gqf, gkf, gqa, gka = _tile2(p["qn_fox"]), _tile2(p["kn_fox"]), _tile2(p["qn_swa"]), _tile2(p["kn_swa"])
    gqm = p["qn_mem"]
    bf128 = jnp.pad(p["b_forget"], ((0, 0), (0, 120)))
    sink = p["sink_swa"].reshape(8)

    h = _rms_fwd(x, p["g_mix"], "rms_mix", deps)
    w_in = getw("w_in", h)
    proj = _mm(h, w_in, "nn", BF16, 512, 1536, 1024, "proj")
    fl = _mm(h, w_in[:, FL0:FL0 + 128], "nn", F32, 512, 128, 1024, "proj_fl")
    qf, kf, vf, qm, qa, ka, va = _proj_post(proj, gqf, gkf, gqm, gqa, gka)
    cc4 = _fox_gate_fwd(fl, bf128)
    w_kv = getw("w_mem_kv", cc4)
    mem_n, kv_raw, mk, mv = _memkv_fwd(mem, p["g_mem"], w_kv, p["kn_mem"])
    kp = jnp.pad(ka, ((SWA_BLOCK, 0), (0, 0)))
    vp = jnp.pad(va, ((SWA_BLOCK, 0), (0, 0)))
    oa = _swa_fwd(qa, kp, vp, bias, sink)
    of, lse4 = _fox_fwd(qf, kf, jnp.transpose(vf), cc4)
    om = _mem_fwd(qm, mk, mv)
    wa, wf, wm, w_out = getw("w_o_swa", oa), getw("w_o_fox", oa), getw("w_o_mem", oa), getw("w_out", oa)
    x1, hm, merged = _merge_fwd(x, oa, of, om, proj, p["b_gate"], wa, wf, wm, w_out, p["g_mlp"])
    w_up = getw("w_mlp_up", of)
    u = _mlp_up(hm, w_up)
    w_down = getw("w_mlp_down", hm)
    dy, dy_b, loss = _mlp_down_loss(u, w_down, x1, target)

    da = _mlp_bwd_act(dy_b, w_down, u)
    t_down = emit({"w_mlp_down": _mm(u, dy_b, "tn", BF16, 1024, 1024, 512, "dw_down")})
    dx1, dg_mlp = _mlp_bwd_x(da, w_up, x1, dy, _tie(p["g_mlp"], t_down))
    t_up = emit({"w_mlp_up": _mm(hm, da, "tn", BF16, 1024, 1024, 512, "dw_up", column_chunks=True)})
    dproj, doa, dof, dom, dya, dyf, dym, db_gate = _merge_bwd(
        dx1, oa, of, om, proj, _tie(p["b_gate"], t_up), wa, wf, wm, w_out)
    t_o = emit({"w_out": _mm(merged, dx1, "tn", BF16, 512, 1024, 512, "dw_out"),
                "w_o_swa": _mm(oa, dya, "tn", BF16, 512, 1024, 512, "dw_o_swa"),
                "w_o_fox": _mm(of, dyf, "tn", BF16, 512, 1024, 512, "dw_o_fox"),
                "w_o_mem": _mm(om, dym, "tn", BF16, 512, 1024, 512, "dw_o_mem")})

    dqm, dmk, dmv = _mem_bwd(qm, mk, mv, dom)
    dw_kv, dkn_mem, dg_mem = _memkv_bwd(dmk, dmv, kv_raw, _tie(p["kn_mem"], t_o), mem, p["g_mem"], mem_n, w_kv)
    t_kv = emit({"w_mem_kv": dw_kv})
    dqa, dkp, dvp, dbias, dsink = _swa_bwd(qa, kp, vp, bias_t, _tie(p["sink_swa"], t_kv).reshape(8), doa)
    dqf_t, dkf, dvf, dck4, dcq4 = _fox_bwd(qf, kf, vf, dof, of, cc4, lse4)
    dqf = jnp.transpose(dqf_t)

    dcq = jnp.transpose(dcq4[:, 0:2, :], (2, 0, 1)).reshape(s, 8)
    dck = jnp.transpose(dck4[:, :, 0:2], (1, 0, 2)).reshape(s, 8)
    dc = jnp.pad(dcq - dck, ((0, 0), (0, 120)))
    dfl, db_forget = _fox_gate_bwd(dc, fl, bf128)

    dproj, dgn = _proj_pre_bwd(dproj, proj, dqf, dkf, dvf, dqm, dqa, dkp[SWA_BLOCK:], dvp[SWA_BLOCK:], dfl,
                               gqf, gkf, gqm, gqa, gka)
    t_in = emit({"w_in": _mm(h, dproj, "tn", BF16, 1024, 3072, 512, "dw_in")})
    grad_x, dg_mix = _in_bwd_x(dproj, w_in, x, _tie(p["g_mix"], t_in), dx1)
    d_rel = _rel_bias_bwd(dbias, bucket_t)

    fold = lambda r: dgn[r:r + 1, 0:HEAD] + dgn[r:r + 1, HEAD:128]
    small = {
        "g_mix": dg_mix, "b_gate": db_gate, "b_forget": db_forget[:, 0:8],
        "qn_swa": fold(3), "kn_swa": fold(4), "sink_swa": dsink[:, 0].reshape(1, 8), "rel_bias": d_rel[:, 0:8],
        "qn_fox": fold(0), "kn_fox": fold(1), "g_mem": dg_mem, "qn_mem": dgn[2:3, :], "kn_mem": dkn_mem,
        "g_mlp": dg_mlp,
    }
    return loss, grad_x, small


SMALL = ("g_mix", "b_gate", "b_forget", "qn_swa", "kn_swa", "sink_swa", "rel_bias", "qn_fox", "kn_fox", "g_mem",
         "qn_mem", "kn_mem", "g_mlp")
BIG = ("w_in", "w_mem_kv", "w_o_swa", "w_o_fox", "w_o_mem", "w_out", "w_mlp_up", "w_mlp_down")
COL_SHARDED = ("w_in", "w_o_swa", "w_o_fox", "w_o_mem", "w_mlp_up")
WEIGHTS = ("g_mix", "w_in", "b_gate", "b_forget", "qn_swa", "kn_swa", "sink_swa", "rel_bias", "qn_fox", "kn_fox", "g_mem",
           "w_mem_kv", "qn_mem", "kn_mem", "w_o_swa", "w_o_fox", "w_o_mem", "w_out", "g_mlp", "w_mlp_up", "w_mlp_down")
SMALL_USED = 6928
SMALL_PAD = 7168


def _gathered_to_full(name, g):
    if name in COL_SHARDED:
        return jnp.transpose(g, (1, 0, 2)).reshape(g.shape[1], N_DEV * g.shape[2])
    return g.reshape(N_DEV * g.shape[1], g.shape[2])


def _full_to_parts(name, full, b):
    if name in COL_SHARDED:
        return jnp.transpose(full.reshape(full.shape[0], N_DEV, b), (1, 0, 2)).astype(BF16)
    return full.reshape(N_DEV, full.shape[0] // N_DEV, full.shape[1]).astype(BF16)


def _pack_small(d, loss=None):
    flat = jnp.concatenate([d[n].reshape(-1) for n in SMALL])
    assert flat.shape[0] == SMALL_USED
    if loss is not None:
        flat = jnp.concatenate([flat, loss.reshape(-1)])
    return jnp.pad(flat, (0, SMALL_PAD - flat.shape[0])).reshape(8, SMALL_PAD // 8)


def _unpack_small(packed, like):
    flat = packed.reshape(-1)
    out, off = {}, 0
    for n in SMALL:
        size = like[n].size
        out[n] = flat[off:off + size].reshape(like[n].shape)
        off += size
    return out


def _adamw_small(parts, w, m, v):
    def body(p_ref, w_ref, m_ref, v_ref, g_ref, d_ref, m2_ref, v2_ref):
        g = p_ref[0]
        for k in range(1, N_DEV):
            g = g + p_ref[k]
        delta, m2, v2 = _adam_math(w_ref[...], g, m_ref[...], v_ref[...])
        g_ref[...] = g
        d_ref[...] = delta
        m2_ref[...] = m2
        v2_ref[...] = v2

    sd = jax.ShapeDtypeStruct(w.shape, F32)
    return pl.pallas_call(body, name="adamw_small", out_shape=[sd, sd, sd, sd])(parts, w, m, v)


def kernel(x, mem, g_mix, w_in, b_gate, b_forget, qn_swa, kn_swa, sink_swa, rel_bias, qn_fox, kn_fox, g_mem, w_mem_kv, qn_mem, kn_mem, w_o_swa, w_o_fox, w_o_mem, w_out, g_mlp, w_mlp_up, w_mlp_down, loss_target, m_g_mix, m_w_in, m_b_gate, m_b_forget, m_qn_swa, m_kn_swa, m_sink_swa, m_rel_bias, m_qn_fox, m_kn_fox, m_g_mem, m_w_mem_kv, m_qn_mem, m_kn_mem, m_w_o_swa, m_w_o_fox, m_w_o_mem, m_w_out, m_g_mlp, m_w_mlp_up, m_w_mlp_down, v_g_mix, v_w_in, v_b_gate, v_b_forget, v_qn_swa, v_kn_swa, v_sink_swa, v_rel_bias, v_qn_fox, v_kn_fox, v_g_mem, v_w_mem_kv, v_qn_mem, v_kn_mem, v_w_o_swa, v_w_o_fox, v_w_o_mem, v_w_out, v_g_mlp, v_w_mlp_up, v_w_mlp_down):
    wts = dict(g_mix=g_mix, w_in=w_in, b_gate=b_gate, b_forget=b_forget, qn_swa=qn_swa, kn_swa=kn_swa, sink_swa=sink_swa,
               rel_bias=rel_bias, qn_fox=qn_fox, kn_fox=kn_fox, g_mem=g_mem, w_mem_kv=w_mem_kv, qn_mem=qn_mem, kn_mem=kn_mem,
               w_o_swa=w_o_swa, w_o_fox=w_o_fox, w_o_mem=w_o_mem, w_out=w_out, g_mlp=g_mlp, w_mlp_up=w_mlp_up,
               w_mlp_down=w_mlp_down)
    mom = dict(g_mix=m_g_mix, w_in=m_w_in, b_gate=m_b_gate, b_forget=m_b_forget, qn_swa=m_qn_swa, kn_swa=m_kn_swa,
               sink_swa=m_sink_swa, rel_bias=m_rel_bias, qn_fox=m_qn_fox, kn_fox=m_kn_fox, g_mem=m_g_mem, w_mem_kv=m_w_mem_kv,
               qn_mem=m_qn_mem, kn_mem=m_kn_mem, w_o_swa=m_w_o_swa, w_o_fox=m_w_o_fox, w_o_mem=m_w_o_mem, w_out=m_w_out,
               g_mlp=m_g_mlp, w_mlp_up=m_w_mlp_up, w_mlp_down=m_w_mlp_down)
    var = dict(g_mix=v_g_mix, w_in=v_w_in, b_gate=v_b_gate, b_forget=v_b_forget, qn_swa=v_qn_swa, kn_swa=v_kn_swa,
               sink_swa=v_sink_swa, rel_bias=v_rel_bias, qn_fox=v_qn_fox, kn_fox=v_kn_fox, g_mem=v_g_mem, w_mem_kv=v_w_mem_kv,
               qn_mem=v_qn_mem, kn_mem=v_kn_mem, w_o_swa=v_w_o_swa, w_o_fox=v_w_o_fox, w_o_mem=v_w_o_mem, w_out=v_w_out,
               g_mlp=v_g_mlp, w_mlp_up=v_w_mlp_up, w_mlp_down=v_w_mlp_down)

    me = _index(_my_place())
    dev = lax.broadcasted_iota(jnp.int32, (N_DEV, 1, 1), 0)

    shards = _cast_shards({n: wts[n][0] for n in BIG})
    gather = _split_start([shards[n] for n in BIG], N_DEV, True, "ag_start")
    full = {}

    def getw(n, after):
        if n not in full:
            _, land = _split_wait(gather, BIG.index(n), after, True, "ag_wait_" + n)
            if n == "w_in":
                full[n] = _w_in_from_shards(land, shards[n])
            else:
                w = jnp.where(dev == me, shards[n][None], land)
                full[n] = w if n == "w_mlp_up" else _gathered_to_full(n, w)
        return full[n]

    exchanges = {}

    def emit(grads_by_name):
        parts = []
        for n, grad in grads_by_name.items():
            if n == "w_in":
                parts.append(_dw_in_to_parts(grad))
            else:
                parts.append(grad if n == "w_mlp_up" else _full_to_parts(n, grad, wts[n].shape[2]))
        started = _split_start(parts, N_DEV - 1, False, "rs_start_" + next(iter(grads_by_name)))
        for w, n in enumerate(grads_by_name):
            exchanges[n] = (started, w)
        return started[4]

    small_p = {n: wts[n] for n in SMALL}
    loss, grad_x, small_g = _local_step(x[0], mem[0], loss_target[0], small_p, getw, emit, (gather[4],))

    packed = _pack_small(small_g, loss)
    small_gather = _split_start([packed], N_DEV, True, "ag_start_small")

    grads, delta, new_m, new_v = {}, {}, {}, {}

    def update(n, after):
        parts, land = _split_wait(*exchanges[n], after, False, "rs_wait_" + n)
        own = lax.dynamic_index_in_dim(parts, me, 0, keepdims=False)
        g, d, m2, v2 = _adamw(own, land, wts[n][0], mom[n][0], var[n][0], "adamw_" + n)
        grads[n], delta[n], new_m[n], new_v[n] = g[None], d[None], m2[None], v2[None]
        return d

    after = small_gather[4]
    for n in exchanges:
        if n != "w_in":
            after = update(n, after)

    _, land = _split_wait(small_gather, 0, after, True, "ag_wait_small")
    gathered = jnp.where(dev == me, packed[None], land)
    g, d, m2, v2 = _adamw_small(gathered, _pack_small(small_p), _pack_small({n: mom[n] for n in SMALL}),
                                _pack_small({n: var[n] for n in SMALL}))
    for dst, flat in ((grads, g), (delta, d), (new_m, m2), (new_v, v2)):
        dst.update(_unpack_small(flat, small_p))
    total = g.reshape(-1)[SMALL_USED]
    update("w_in", d)

    return (total, grad_x[None], *[grads[n] for n in WEIGHTS], *[delta[n] for n in WEIGHTS],
            *[new_m[n] for n in WEIGHTS], *[new_v[n] for n in WEIGHTS])
```

```python
import functools
import math

import jax
import jax.numpy as jnp
from jax import lax
from jax.experimental import pallas as pl
from jax.experimental.pallas import tpu as pltpu

F32 = jnp.float32
BF16 = jnp.bfloat16

D_MODEL = 1024
N_MEM = 256
D_FF = 4096
HEAD = 64
SWA_HEADS = 8
SWA_BLOCK = 128
MEM_HEADS = 4
MEM_HEAD = 128
EPS = 1e-6
NEG = -1e30
REL_BUCKETS = 32
REL_MAX_DIST = 128

ADAM_LR = 0.001
ADAM_B1 = 0.9
ADAM_B2 = 0.999
ADAM_EPS = 1e-08
ADAM_WD = 0.01
ADAM_STEP = 10

GL0, QF0, KF0, VF0, QM0, QA0, KA0, VA0, FL0 = 0, 3072, 3584, 4096, 4608, 5120, 5632, 5760, 5888
PROJ_W = 6144
HALF_W = 3072
H_QF, H_KF, H_VF, H_QM, H_QA, H_KA, H_VA, H_FL = 0, 512, 1024, 1536, 2048, 2560, 2688, 2816

VMEM_LIMIT = 56 * 1024 * 1024
N_DEV = 8
MESH = pl.DeviceIdType.MESH

NN = (((1,), (0,)), ((), ()))
NT = (((1,), (1,)), ((), ()))
TN = (((0,), (0,)), ((), ()))


def _dot(a, b, dims=NN):
    return lax.dot_general(a, b, dims, preferred_element_type=F32)


def _params(sem):
    return pltpu.CompilerParams(dimension_semantics=sem, vmem_limit_bytes=VMEM_LIMIT)


def _full(shape):
    nd = len(shape)
    return pl.BlockSpec(shape, lambda *_: (0,) * nd)


def _sigmoid(z):
    return 1.0 / (1.0 + jnp.exp(-z))


def _group_mean(v, hd):
    if hd == 128:
        return jnp.mean(v, axis=-1, keepdims=True)
    lane = lax.broadcasted_iota(jnp.int32, v.shape, 1)
    lo = lane < HEAD
    s_lo = jnp.sum(jnp.where(lo, v, 0.0), axis=-1, keepdims=True)
    s_hi = jnp.sum(jnp.where(lo, 0.0, v), axis=-1, keepdims=True)
    return jnp.where(lo, s_lo, s_hi) * (1.0 / HEAD)


def _mm(a, b, mode, out_dtype, tm, tn, tk, name, column_chunks=False):
    if mode == "nn":
        m, k = a.shape
        n = b.shape[1]
    elif mode == "nt":
        m, k = a.shape
        n = b.shape[0]
    else:
        k, m = a.shape
        n = b.shape[1]
    tm, tn, tk = min(tm, m), min(tn, n), min(tk, k)
    nk = k // tk
    chunk = n // N_DEV
    per_tile = tn // chunk if column_chunks else 1
    dims = {"nn": NN, "nt": NT, "tn": TN}[mode]
    a_spec = pl.BlockSpec((tk, tm), lambda j, i, kk: (kk, i)) if mode == "tn" else pl.BlockSpec((tm, tk), lambda j, i, kk: (i, kk))
    b_spec = pl.BlockSpec((tn, tk), lambda j, i, kk: (j, kk)) if mode == "nt" else pl.BlockSpec((tk, tn), lambda j, i, kk: (kk, j))

    def body(a_ref, b_ref, o_ref, *acc):
        prod = _dot(a_ref[...].astype(BF16), b_ref[...].astype(BF16), dims)

        def write(res):
            if column_chunks:
                for c in range(per_tile):
                    o_ref[c] = res[:, c * chunk:(c + 1) * chunk].astype(o_ref.dtype)
            else:
                o_ref[...] = res.astype(o_ref.dtype)

        if nk == 1:
            write(prod)
        else:
            acc_ref, = acc
            kk = pl.program_id(2)

            @pl.when(kk == 0)
            def _():
                acc_ref[...] = prod

            @pl.when(kk > 0)
            def _():
                acc_ref[...] += prod

            @pl.when(kk == nk - 1)
            def _():
                write(acc_ref[...])

    return pl.pallas_call(
        body, name=name, grid=(n // tn, m // tm, nk),
        in_specs=[a_spec, b_spec],
        out_specs=(pl.BlockSpec((per_tile, tm, chunk), lambda j, i, kk: (j, i, 0)) if column_chunks
                   else pl.BlockSpec((tm, tn), lambda j, i, kk: (i, j))),
        out_shape=jax.ShapeDtypeStruct((N_DEV, m, chunk) if column_chunks else (m, n), out_dtype),
        scratch_shapes=[pltpu.VMEM((tm, tn), F32)] if nk > 1 else [],
        compiler_params=_params(("parallel", "parallel", "arbitrary")),
    )(a, b)


def _rms_fwd(x, g, name, deps=()):
    s, d = x.shape
    tm = min(512, s)

    def body(x_ref, g_ref, *rest):
        h_ref = rest[len(deps)]
        xv = x_ref[...]
        r = lax.rsqrt(jnp.mean(xv * xv, axis=-1, keepdims=True) + EPS)
        h_ref[...] = (xv * r * g_ref[...]).astype(BF16)

    return pl.pallas_call(
        body, name=name, grid=(s // tm,),
        in_specs=[pl.BlockSpec((tm, d), lambda i: (i, 0)), _full((1, d))] + [pl.BlockSpec(memory_space=pl.ANY)] * len(deps),
        out_specs=pl.BlockSpec((tm, d), lambda i: (i, 0)),
        out_shape=jax.ShapeDtypeStruct((s, d), BF16),
        compiler_params=_params(("parallel",)),
    )(x, g, *deps)


def _proj_post(proj, gq_fox, gk_fox, gq_mem, gq_swa, gk_swa):
    s = proj.shape[0]
    tm = min(256, s)

    def body(p_ref, gqf, gkf, gqm, gqa, gka, qf_ref, kf_ref, vf_ref, qm_ref, qa_ref, ka_ref, va_ref):
        def norm(off, width, hd, g_ref, o_ref):
            for b in range(width // 128):
                v = p_ref[:, off + b * 128: off + (b + 1) * 128].astype(F32)
                r = lax.rsqrt(_group_mean(v * v, hd) + EPS)
                o_ref[:, b * 128:(b + 1) * 128] = (v * r * g_ref[...]).astype(BF16)

        norm(H_QF, 512, HEAD, gqf, qf_ref)
        norm(H_KF, 512, HEAD, gkf, kf_ref)
        vf_ref[...] = p_ref[:, H_VF:H_VF + 512].astype(BF16)
        norm(H_QM, 512, MEM_HEAD, gqm, qm_ref)
        norm(H_QA, 512, HEAD, gqa, qa_ref)
        norm(H_KA, 128, HEAD, gka, ka_ref)
        va_ref[...] = p_ref[:, H_VA:H_VA + 128].astype(BF16)

    g_spec = _full((1, 128))
    o512 = pl.BlockSpec((tm, 512), lambda i: (i, 0))
    o128 = pl.BlockSpec((tm, 128), lambda i: (i, 0))
    s512 = jax.ShapeDtypeStruct((s, 512), BF16)
    s128 = jax.ShapeDtypeStruct((s, 128), BF16)
    return pl.pallas_call(
        body, name="proj_post", grid=(s // tm,),
        in_specs=[pl.BlockSpec((tm, HALF_W), lambda i: (i, 1)), g_spec, g_spec, g_spec, g_spec, g_spec],
        out_specs=[o512, o512, o512, o512, o512, o128, o128],
        out_shape=[s512, s512, s512, s512, s512, s128, s128],
        compiler_params=_params(("parallel",)),
    )(proj, gq_fox, gk_fox, gq_mem, gq_swa, gk_swa)


def _tri(n, lower):
    r = lax.broadcasted_iota(jnp.int32, (n, n), 0)
    c = lax.broadcasted_iota(jnp.int32, (n, n), 1)
    return jnp.where((c <= r) if lower else (c >= r), 1.0, 0.0).astype(F32)


def _fox_gate_fwd(proj, b_forget128):
    s = proj.shape[0]
    tm = min(512, s)

    def body(p_ref, b_ref, cc_ref, carry_ref):
        i = pl.program_id(0)

        @pl.when(i == 0)
        def _():
            carry_ref[...] = jnp.zeros_like(carry_ref)

        z = p_ref[...] + b_ref[...]
        logf = jnp.minimum(z, 0.0) - jnp.log(1.0 + jnp.exp(-jnp.abs(z)))
        c = jnp.dot(_tri(tm, True), logf, precision=lax.Precision.HIGHEST, preferred_element_type=F32) + carry_ref[...]
        carry_ref[...] = c[tm - 1:tm, :]
        for hp in range(4):
            cc_ref[hp] = c if hp == 0 else pltpu.roll(c, 128 - 2 * hp, 1)

    return pl.pallas_call(
        body, name="fox_gate_fwd", grid=(s // tm,),
        in_specs=[pl.BlockSpec((tm, 128), lambda i: (i, 0)), _full((1, 128))],
        out_specs=pl.BlockSpec((4, tm, 128), lambda i: (0, i, 0)),
        out_shape=jax.ShapeDtypeStruct((4, s, 128), F32),
        scratch_shapes=[pltpu.VMEM((1, 128), F32)],
        compiler_params=_params(("arbitrary",)),
    )(proj, b_forget128)


def _memkv_fwd(mem, g_mem, w_kv, kn_mem):
    m = mem.shape[0]

    def body(mem_ref, g_ref, w_ref, kn_ref, memn_ref, kv_ref, mk_ref, mv_ref):
        xv = mem_ref[...]
        r = lax.rsqrt(jnp.mean(xv * xv, axis=-1, keepdims=True) + EPS)
        mn = (xv * r * g_ref[...]).astype(BF16)
        memn_ref[...] = mn
        kv = _dot(mn, w_ref[...])
        kv_ref[...] = kv
        for h in range(MEM_HEADS):
            v = kv[:, h * 128:(h + 1) * 128]
            rr = lax.rsqrt(jnp.mean(v * v, axis=-1, keepdims=True) + EPS)
            mk_ref[:, h * 128:(h + 1) * 128] = (v * rr * kn_ref[...]).astype(BF16)
        mv_ref[...] = kv[:, 512:1024].astype(BF16)

    return pl.pallas_call(
        body, name="memkv_fwd",
        out_shape=[jax.ShapeDtypeStruct((m, D_MODEL), BF16), jax.ShapeDtypeStruct((m, 1024), F32),
                   jax.ShapeDtypeStruct((m, 512), BF16), jax.ShapeDtypeStruct((m, 512), BF16)],
        compiler_params=pltpu.CompilerParams(vmem_limit_bytes=VMEM_LIMIT),
    )(mem, g_mem, w_kv, kn_mem)


def _bias_table(rel_bias, bucket):
    def body(rb_ref, bk_ref, o_ref):
        bk = bk_ref[...]
        for h in range(SWA_HEADS):
            acc = jnp.zeros(bk.shape, F32)
            for b in range(REL_BUCKETS):
                acc = jnp.where(bk == b, rb_ref[b, h], acc)
            o_ref[h] = acc

    return pl.pallas_call(
        body, name="bias_table",
        in_specs=[pl.BlockSpec(memory_space=pltpu.SMEM), pl.BlockSpec(memory_space=pltpu.VMEM)],
        out_shape=jax.ShapeDtypeStruct((SWA_HEADS,) + bucket.shape, F32),
    )(rel_bias, bucket)


def _swa_valid(n):
    row = lax.broadcasted_iota(jnp.int32, (SWA_BLOCK, 2 * SWA_BLOCK), 0)
    col = lax.broadcasted_iota(jnp.int32, (SWA_BLOCK, 2 * SWA_BLOCK), 1)
    dist = row + SWA_BLOCK - col
    return (dist >= 0) & (dist < SWA_BLOCK) & ((col >= SWA_BLOCK) | (n > 0))


def _swa_fwd(qa, kp, vp, bias, sink):
    s = qa.shape[0]
    nb = s // SWA_BLOCK

    def body(sink_ref, q_ref, kp_ref, vp_ref, bias_ref, o_ref):
        n = pl.program_id(0)
        start = pl.multiple_of(n * SWA_BLOCK, SWA_BLOCK)
        k2 = kp_ref[pl.ds(start, 2 * SWA_BLOCK), :]
        v2 = vp_ref[pl.ds(start, 2 * SWA_BLOCK), :]
        valid = _swa_valid(n)
        heads = range(SWA_HEADS)
        hs = lambda h: slice(h * HEAD, (h + 1) * HEAD)
        sc = [jnp.where(valid, _dot(q_ref[:, hs(h)], k2[:, hs(h // 4)], NT) * 0.125 + bias_ref[h], NEG) for h in heads]
        pn = []
        for h in heads:
            sk = sink_ref[h]
            mx = jnp.maximum(jnp.max(sc[h], axis=-1, keepdims=True), sk)
            p = jnp.exp(sc[h] - mx)
            den = jnp.sum(p, axis=-1, keepdims=True) + jnp.exp(sk - mx)
            pn.append((p / den).astype(BF16))
        outs = [_dot(pn[h], v2[:, hs(h // 4)]).astype(BF16) for h in heads]
        for h in heads:
            o_ref[:, hs(h)] = outs[h]

    return pl.pallas_call(
        body, name="swa_fwd", grid=(nb,),
        in_specs=[pl.BlockSpec(memory_space=pltpu.SMEM),
                  pl.BlockSpec((SWA_BLOCK, 512), lambda n: (n, 0)),
                  _full(kp.shape), _full(vp.shape), _full(bias.shape)],
        out_specs=pl.BlockSpec((SWA_BLOCK, 512), lambda n: (n, 0)),
        out_shape=jax.ShapeDtypeStruct((s, 512), BF16),
        compiler_params=_params(("parallel",)),
    )(sink, qa, kp, vp, bias)


def _head_mask(e):
    lane = lax.broadcasted_iota(jnp.int32, (1, 128), 1)
    return (lane >= e * HEAD) & (lane < (e + 1) * HEAD)


FOX_FWD_TQ, FOX_FWD_TK = 1024, 1024
FOX_BWD_TK, FOX_BWD_TQ = 256, 512


def _head_rows(e):
    row = lax.broadcasted_iota(jnp.int32, (128, 1), 0)
    return (row >= e * HEAD) & (row < (e + 1) * HEAD)


def _fox_fwd(q, k, v_t, cc4):
    s = q.shape[0]
    t = min(FOX_FWD_TQ, s)
    tk = min(FOX_FWD_TK, s)
    nq = s // t

    def body(q_ref, k_ref, vt_ref, cc_ref, o_ref, lse_ref):
        i = pl.program_id(1)
        qs = q_ref[...] * jnp.asarray(0.125, BF16)
        qe = [jnp.where(_head_mask(e), qs, jnp.zeros_like(qs)) for e in range(2)]
        n_full = (i * t) // tk
        krow = lax.broadcasted_iota(jnp.int32, (tk, t), 0) + n_full * tk
        qcol = lax.broadcasted_iota(jnp.int32, (tk, t), 1) + i * t

        def step(j, carry, masked):
            ks = pl.ds(pl.multiple_of(j * tk, tk), tk)
            kj = k_ref[ks, :]
            vtj = vt_ref[:, ks]
            out = []
            for e in range(2):
                m, acc = carry[2 * e], carry[2 * e + 1]
                st = _dot(kj, qe[e], NT) - cc_ref[0, ks, e:e + 1]
                if masked:
                    st = jnp.where(krow <= qcol, st, NEG)
                m_new = jnp.maximum(m, jnp.max(st, axis=0, keepdims=True))
                alpha = jnp.exp(m - m_new)
                pt = jnp.exp(st - m_new).astype(BF16)
                vte = jnp.where(_head_rows(e), vtj, jnp.ones_like(vtj))
                out += [m_new, alpha * acc + _dot(vte, pt)]
            return tuple(out)

        init = (jnp.full((1, t), NEG, F32), jnp.zeros((128, t), F32)) * 2
        carry = lax.fori_loop(0, n_full, functools.partial(step, masked=False), init)
        m0, a0, m1, a1 = step(n_full, carry, True)
        l0 = a0[HEAD:HEAD + 1, :]
        l1 = a1[0:1, :]
        o_t = jnp.where(_head_rows(0), a0 / l0, a1 / l1)
        o_ref[...] = o_t.T.astype(BF16)
        r8 = lax.broadcasted_iota(jnp.int32, (8, t), 0)
        lse_ref[0] = jnp.where(r8 == 0, m0 + jnp.log(l0), jnp.where(r8 == 1, m1 + jnp.log(l1), 0.0))

    return pl.pallas_call(
        body, name="fox_fwd", grid=(4, nq),
        in_specs=[pl.BlockSpec((t, 128), lambda hp, i: (i, hp)),
                  pl.BlockSpec((s, 128), lambda hp, i: (0, hp)),
                  pl.BlockSpec((128, s), lambda hp, i: (hp, 0)),
                  pl.BlockSpec((1, s, 128), lambda hp, i: (hp, 0, 0))],
        out_specs=[pl.BlockSpec((t, 128), lambda hp, i: (i, hp)),
                   pl.BlockSpec((1, 8, t), lambda hp, i: (hp, 0, i))],
        out_shape=[jax.ShapeDtypeStruct((s, 512), BF16), jax.ShapeDtypeStruct((4, 8, s), F32)],
        compiler_params=_params(("parallel", "parallel")),
    )(q, k, v_t, cc4)


MEM_SCALE = MEM_HEAD ** -0.5


def _mem_fwd(qm, mk, mv):
    s = qm.shape[0]
    tq = min(512, s)

    def body(q_ref, mk_ref, mv_ref, o_ref):
        for h in range(MEM_HEADS):
            hs = slice(h * 128, (h + 1) * 128)
            sc = _dot(q_ref[:, hs], mk_ref[:, hs], NT) * MEM_SCALE
            mx = jnp.max(sc, axis=-1, keepdims=True)
            p = jnp.exp(sc - mx)
            p = p / jnp.sum(p, axis=-1, keepdims=True)
            o_ref[:, hs] = _dot(p.astype(BF16), mv_ref[:, hs]).astype(BF16)

    return pl.pallas_call(
        body, name="mem_fwd", grid=(s // tq,),
        in_specs=[pl.BlockSpec((tq, 512), lambda i: (i, 0)), _full(mk.shape), _full(mv.shape)],
        out_specs=pl.BlockSpec((tq, 512), lambda i: (i, 0)),
        out_shape=jax.ShapeDtypeStruct((s, 512), BF16),
        compiler_params=_params(("parallel",)),
    )(qm, mk, mv)


def _merge_fwd(x, oa, of, om, proj, b_gate, wa, wf, wm, w_out, g_mlp):
    s = x.shape[0]
    tm = min(256, s)

    def body(x_ref, oa_ref, of_ref, om_ref, gl_ref, bg_ref, wa_ref, wf_ref, wm_ref, wo_ref, g_ref, x1_ref, hm_ref, mg_ref):
        merged = None
        for b, (o_ref, w_ref) in enumerate(((oa_ref, wa_ref), (of_ref, wf_ref), (om_ref, wm_ref))):
            cs = slice(b * D_MODEL, (b + 1) * D_MODEL)
            y = _dot(o_ref[...], w_ref[...])
            t = _sigmoid(gl_ref[:, cs].astype(F32) + bg_ref[:, cs]) * y
            merged = t if merged is None else merged + t
        mb = merged.astype(BF16)
        mg_ref[...] = mb
        x1 = x_ref[...] + _dot(mb, wo_ref[...])
        x1_ref[...] = x1
        r = lax.rsqrt(jnp.mean(x1 * x1, axis=-1, keepdims=True) + EPS)
        hm_ref[...] = (x1 * r * g_ref[...]).astype(BF16)

    row = lambda w: pl.BlockSpec((tm, w), lambda i: (i, 0))
    return pl.pallas_call(
        body, name="merge_fwd", grid=(s // tm,),
        in_specs=[row(D_MODEL), row(512), row(512), row(512), row(HALF_W), _full((1, HALF_W)),
                  _full(wa.shape), _full(wf.shape), _full(wm.shape), _full(w_out.shape), _full((1, D_MODEL))],
        out_specs=[row(D_MODEL), row(D_MODEL), row(D_MODEL)],
        out_shape=[jax.ShapeDtypeStruct((s, D_MODEL), F32), jax.ShapeDtypeStruct((s, D_MODEL), BF16),
                   jax.ShapeDtypeStruct((s, D_MODEL), BF16)],
        compiler_params=_params(("parallel",)),
    )(x, oa, of, om, proj, b_gate, wa, wf, wm, w_out, g_mlp)


def _mlp_up(hm, w_up):
    s = hm.shape[0]
    tm, tn = min(1024, s), w_up.shape[2]

    def body(h_ref, w_ref, u_ref):
        r = jnp.maximum(_dot(h_ref[...], w_ref[0]), 0.0)
        u_ref[...] = (r * r).astype(BF16)

    return pl.pallas_call(
        body, name="mlp_up", grid=(s // tm, D_FF // tn),
        in_specs=[pl.BlockSpec((tm, D_MODEL), lambda i, j: (i, 0)), pl.BlockSpec((1, D_MODEL, tn), lambda i, j: (j, 0, 0))],
        out_specs=pl.BlockSpec((tm, tn), lambda i, j: (i, j)),
        out_shape=jax.ShapeDtypeStruct((s, D_FF), BF16),
        compiler_params=_params(("parallel", "parallel")),
    )(hm, w_up)


def _mlp_down_loss(u, w_down, x1, target):
    s = u.shape[0]
    tm = min(256, s)

    def body(u_ref, w_ref, x1_ref, t_ref, dy_ref, dyb_ref, loss_ref):
        i = pl.program_id(0)

        @pl.when(i == 0)
        def _():
            loss_ref[...] = jnp.zeros_like(loss_ref)

        y = x1_ref[...] + _dot(u_ref[...], w_ref[...])
        err = y - t_ref[...]
        dy = err * (1.0 / D_MODEL)
        dy_ref[...] = dy
        dyb_ref[...] = dy.astype(BF16)
        part = jnp.sum(jnp.sum(err * err, axis=-1, keepdims=True) * (1.0 / D_MODEL), axis=0, keepdims=True)
        loss_ref[...] += 0.5 * part

    row = pl.BlockSpec((tm, D_MODEL), lambda i: (i, 0))
    return pl.pallas_call(
        body, name="mlp_down_loss", grid=(s // tm,),
        in_specs=[pl.BlockSpec((tm, D_FF), lambda i: (i, 0)), _full(w_down.shape), row, row],
        out_specs=[row, row, _full((1, 1))],
        out_shape=[jax.ShapeDtypeStruct((s, D_MODEL), F32), jax.ShapeDtypeStruct((s, D_MODEL), BF16),
                   jax.ShapeDtypeStruct((1, 1), F32)],
        compiler_params=_params(("arbitrary",)),
    )(u, w_down, x1, target)


def _mlp_bwd_act(dy, w_down, u):
    s = dy.shape[0]
    tm, tn = min(1024, s), 1024

    def body(dy_ref, w_ref, u_ref, da_ref):
        du = _dot(dy_ref[...], w_ref[...], NT)
        da_ref[...] = (du * (2.0 * jnp.sqrt(u_ref[...].astype(F32)))).astype(BF16)

    return pl.pallas_call(
        body, name="mlp_bwd_act", grid=(D_FF // tn, s // tm),
        in_specs=[pl.BlockSpec((tm, D_MODEL), lambda j, i: (i, 0)), pl.BlockSpec((tn, D_MODEL), lambda j, i: (j, 0)),
                  pl.BlockSpec((tm, tn), lambda j, i: (i, j))],
        out_specs=pl.BlockSpec((tm, tn), lambda j, i: (i, j)),
        out_shape=jax.ShapeDtypeStruct((s, D_FF), BF16),
        compiler_params=_params(("parallel", "parallel")),
    )(dy, w_down, u)


def _rms_bwd(xv, g, dh, skip):
    r = lax.rsqrt(jnp.mean(xv * xv, axis=-1, keepdims=True) + EPS)
    n = xv * r
    dn = dh * g
    dx = skip + r * (dn - n * jnp.mean(dn * n, axis=-1, keepdims=True))
    return dx, jnp.sum(dh * n, axis=0, keepdims=True)


def _mlp_bwd_x(da, w_up, x1, dy, g_mlp):
    s = da.shape[0]
    tm = min(256, s)

    def body(da_ref, w_ref, x1_ref, dy_ref, g_ref, dx1_ref, dg_ref):
        i = pl.program_id(0)

        @pl.when(i == 0)
        def _():
            dg_ref[...] = jnp.zeros_like(dg_ref)

        tn = w_ref.shape[2]
        dhm = _dot(da_ref[:, 0:tn], w_ref[0], NT)
        for j in range(1, N_DEV):
            dhm = dhm + _dot(da_ref[:, j * tn:(j + 1) * tn], w_ref[j], NT)
        dx, dg = _rms_bwd(x1_ref[...], g_ref[...], dhm, dy_ref[...])
        dx1_ref[...] = dx
        dg_ref[...] += dg

    row = pl.BlockSpec((tm, D_MODEL), lambda i: (i, 0))
    return pl.pallas_call(
        body, name="mlp_bwd_x", grid=(s // tm,),
        in_specs=[pl.BlockSpec((tm, D_FF), lambda i: (i, 0)), _full(w_up.shape), row, row, _full((1, D_MODEL))],
        out_specs=[row, _full((1, D_MODEL))],
        out_shape=[jax.ShapeDtypeStruct((s, D_MODEL), F32), jax.ShapeDtypeStruct((1, D_MODEL), F32)],
        compiler_params=_params(("arbitrary",)),
    )(da, w_up, x1, dy, g_mlp)


def _merge_bwd(dx1, oa, of, om, proj, b_gate, wa, wf, wm, w_out):
    s = dx1.shape[0]
    tm = min(256, s)

    def body(dx1_ref, oa_ref, of_ref, om_ref, gl_ref, bg_ref, wa_ref, wf_ref, wm_ref, wo_ref,
             dp_ref, doa_ref, dof_ref, dom_ref, dya_ref, dyf_ref, dym_ref, dbg_ref):
        i = pl.program_id(0)

        @pl.when(i == 0)
        def _():
            dbg_ref[...] = jnp.zeros_like(dbg_ref)

        dmerged = _dot(dx1_ref[...].astype(BF16), wo_ref[...], NT)
        branches = ((oa_ref, wa_ref, doa_ref, dya_ref), (of_ref, wf_ref, dof_ref, dyf_ref), (om_ref, wm_ref, dom_ref, dym_ref))
        for b, (o_ref, w_ref, do_ref, dyb_ref) in enumerate(branches):
            cs = slice(b * D_MODEL, (b + 1) * D_MODEL)
            y = _dot(o_ref[...], w_ref[...])
            g = _sigmoid(gl_ref[:, cs].astype(F32) + bg_ref[:, cs])
            dz = (dmerged * y) * g * (1.0 - g)
            dp_ref[:, cs] = dz.astype(BF16)
            dbg_ref[:, cs] += jnp.sum(dz, axis=0, keepdims=True)
            dyb = (dmerged * g).astype(BF16)
            dyb_ref[...] = dyb
            do_ref[...] = _dot(dyb, w_ref[...], NT).astype(BF16)

    row = lambda w: pl.BlockSpec((tm, w), lambda i: (i, 0))
    sd = lambda w: jax.ShapeDtypeStruct((s, w), BF16)
    return pl.pallas_call(
        body, name="merge_bwd", grid=(s // tm,),
        in_specs=[row(D_MODEL), row(512), row(512), row(512), row(HALF_W), _full((1, HALF_W)),
                  _full(wa.shape), _full(wf.shape), _full(wm.shape), _full(w_out.shape)],
        out_specs=[row(HALF_W), row(512), row(512), row(512), row(D_MODEL), row(D_MODEL), row(D_MODEL), _full((1, HALF_W))],
        out_shape=[sd(PROJ_W), sd(512), sd(512), sd(512), sd(D_MODEL), sd(D_MODEL), sd(D_MODEL),
                   jax.ShapeDtypeStruct((1, HALF_W), F32)],
        compiler_params=_params(("arbitrary",)),
    )(dx1, oa, of, om, proj, b_gate, wa, wf, wm, w_out)


def _swa_valid_t(n):
    key = lax.broadcasted_iota(jnp.int32, (2 * SWA_BLOCK, SWA_BLOCK), 0)
    qry = lax.broadcasted_iota(jnp.int32, (2 * SWA_BLOCK, SWA_BLOCK), 1)
    dist = qry + SWA_BLOCK - key
    return (dist >= 0) & (dist < SWA_BLOCK) & ((key >= SWA_BLOCK) | (n > 0))


def _swa_bwd(qa, kp, vp, bias_t, sink, doa):
    s = qa.shape[0]
    nb = s // SWA_BLOCK

    def body(sink_ref, q_ref, kp_ref, vp_ref, bias_ref, do_ref, dq_ref, dkp_ref, dvp_ref, dbias_ref, dsink_ref, sk_acc):
        n = pl.program_id(0)

        @pl.when(n == 0)
        def _():
            dkp_ref[...] = jnp.zeros_like(dkp_ref)
            dvp_ref[...] = jnp.zeros_like(dvp_ref)
            dbias_ref[...] = jnp.zeros_like(dbias_ref)
            sk_acc[...] = jnp.zeros_like(sk_acc)

        start = pl.multiple_of(n * SWA_BLOCK, SWA_BLOCK)
        win = pl.ds(start, 2 * SWA_BLOCK)
        k2 = kp_ref[win, :]
        v2 = vp_ref[win, :]
        valid = _swa_valid_t(n)
        heads = range(SWA_HEADS)
        hs = lambda h: slice(h * HEAD, (h + 1) * HEAD)
        scale = jnp.asarray(0.125, BF16)
        q = [q_ref[:, hs(h)] for h in heads]
        do = [do_ref[:, hs(h)] for h in heads]
        kk = [k2[:, hs(kv)] for kv in range(2)]
        vv = [v2[:, hs(kv)] for kv in range(2)]
        kt = [(kk[kv].astype(F32) * 0.125).T.astype(BF16) for kv in range(2)]
        st = [jnp.where(valid, _dot(kk[h // 4], q[h], NT) * 0.125 + bias_ref[h], NEG) for h in heads]
        dpt = [_dot(vv[h // 4], do[h], NT) for h in heads]
        pt, dst = [], []
        for h in heads:
            sk = sink_ref[h]
            mx = jnp.maximum(jnp.max(st[h], axis=0, keepdims=True), sk)
            p = jnp.exp(st[h] - mx)
            esk = jnp.exp(sk - mx)
            den = jnp.sum(p, axis=0, keepdims=True) + esk
            p = p / den
            delta = jnp.sum(p * dpt[h], axis=0, keepdims=True)
            d = p * (dpt[h] - delta)
            sk_acc[h:h + 1, :] += -(esk / den) * delta
            dbias_ref[h] += d
            pt.append(p.astype(BF16))
            dst.append(d.astype(BF16))
        dq_t = [_dot(kt[h // 4], dst[h]) for h in heads]
        dq_ref[...] = jnp.concatenate(dq_t, axis=0).T
        for kv in range(2):
            group = range(4 * kv, 4 * kv + 4)
            dk = [_dot(dst[h], q[h] * scale) for h in group]
            dv = [_dot(pt[h], do[h]) for h in group]
            dkp_ref[win, hs(kv)] += (dk[0] + dk[1]) + (dk[2] + dk[3])
            dvp_ref[win, hs(kv)] += (dv[0] + dv[1]) + (dv[2] + dv[3])

        @pl.when(n == nb - 1)
        def _():
            dsink_ref[...] = jnp.broadcast_to(jnp.sum(sk_acc[...], axis=1, keepdims=True), dsink_ref.shape)

    return pl.pallas_call(
        body, name="swa_bwd", grid=(nb,),
        in_specs=[pl.BlockSpec(memory_space=pltpu.SMEM),
                  pl.BlockSpec((SWA_BLOCK, 512), lambda n: (n, 0)),
                  _full(kp.shape), _full(vp.shape), _full(bias_t.shape),
                  pl.BlockSpec((SWA_BLOCK, 512), lambda n: (n, 0))],
        out_specs=[pl.BlockSpec((SWA_BLOCK, 512), lambda n: (n, 0)), _full(kp.shape), _full(vp.shape),
                   _full(bias_t.shape), _full((SWA_HEADS, 128))],
        out_shape=[jax.ShapeDtypeStruct((s, 512), F32), jax.ShapeDtypeStruct(kp.shape, F32),
                   jax.ShapeDtypeStruct(vp.shape, F32), jax.ShapeDtypeStruct(bias_t.shape, F32),
                   jax.ShapeDtypeStruct((SWA_HEADS, 128), F32)],
        scratch_shapes=[pltpu.VMEM((SWA_HEADS, 128), F32)],
        compiler_params=_params(("arbitrary",)),
    )(sink, qa, kp, vp, bias_t, doa)


def _fox_bwd(q, k, v, do, o, cc4, lse4):
    s = q.shape[0]
    t = min(FOX_BWD_TK, s)
    tq = min(FOX_BWD_TQ, s)
    nq = s // t
    nqt = s // tq

    def body(q_ref, k_ref, v_ref, do_ref, o_ref, cc_ref, lse_ref,
             dqt_ref, dk_ref, dv_ref, dck_ref, dcq_ref, delta_ref, dk0, dk1, dv0, dv1, ds0, ds1):
        j = pl.program_id(1)

        @pl.when(j == 0)
        def _():
            dqt_ref[...] = jnp.zeros_like(dqt_ref)
            dcq_ref[...] = jnp.zeros_like(dcq_ref)
            lane8 = lax.broadcasted_iota(jnp.int32, (8, 128), 1)
            row8 = lax.broadcasted_iota(jnp.int32, (8, 128), 0)
            sel = jnp.where((lane8 // HEAD) == row8, 1.0, 0.0).astype(F32)

            def dl(i, c):
                rows = pl.ds(pl.multiple_of(i * tq, tq), tq)
                pr = do_ref[rows, :].astype(F32) * o_ref[rows, :].astype(F32)
                delta_ref[:, rows] = lax.dot_general(sel, pr, NT, precision=lax.Precision.HIGHEST,
                                                     preferred_element_type=F32)
                return c

            lax.fori_loop(0, nqt, dl, 0)

        kj = k_ref[...]
        vj = v_ref[...]
        ks = pl.ds(pl.multiple_of(j * t, t), t)
        kt = (kj.astype(F32) * 0.125).T.astype(BF16)
        ke = [jnp.where(_head_mask(e), kj, jnp.zeros_like(kj)) for e in range(2)]
        ve = [jnp.where(_head_mask(e), vj, jnp.zeros_like(vj)) for e in range(2)]
        kte = [jnp.where(_head_rows(e), kt, jnp.zeros_like(kt)) for e in range(2)]
        ck = [cc_ref[0, ks, e:e + 1] for e in range(2)]
        accs = ((dk0, dv0, ds0), (dk1, dv1, ds1))
        for refs in accs:
            for r in refs:
                r[...] = jnp.zeros_like(r)
        i_first = (j * t) // tq
        krow = lax.broadcasted_iota(jnp.int32, (t, tq), 0) + j * t
        qcol = lax.broadcasted_iota(jnp.int32, (t, tq), 1) + i_first * tq

        def step(i, c, masked):
            rows = pl.ds(pl.multiple_of(i * tq, tq), tq)
            qs = q_ref[rows, :] * jnp.asarray(0.125, BF16)
            doi = do_ref[rows, :]
            for e in range(2):
                dk_acc, dv_acc, ds_acc = accs[e]
                st = _dot(ke[e], qs, NT) - ck[e]
                if masked:
                    st = jnp.where(krow <= qcol, st, NEG)
                pt = jnp.exp(st - lse_ref[0, e:e + 1, rows])
                dpt = _dot(ve[e], doi, NT)
                dst = pt * (dpt - delta_ref[e:e + 1, rows])
                dsb = dst.astype(BF16)
                dv_acc[...] += _dot(pt.astype(BF16), doi)
                dk_acc[...] += _dot(dsb, qs)
                dqt_ref[:, rows] += _dot(kte[e], dsb)
                ds_acc[...] += dst
                dcq_ref[0, e:e + 1, rows] += jnp.sum(dst, axis=0, keepdims=True)
            return c

        step(i_first, 0, True)
        lax.fori_loop(i_first + 1, nqt, functools.partial(step, masked=False), 0)
        m0 = _head_mask(0)
        dk_ref[...] = jnp.where(m0, dk0[...], dk1[...])
        dv_ref[...] = jnp.where(m0, dv0[...], dv1[...])
        lane = lax.broadcasted_iota(jnp.int32, (t, 128), 1)
        c0 = jnp.sum(ds0[...], axis=-1, keepdims=True)
        c1 = jnp.sum(ds1[...], axis=-1, keepdims=True)
        dck_ref[0] = jnp.where(lane == 0, c0, jnp.where(lane == 1, c1, 0.0))

    res = lambda: pl.BlockSpec((s, 128), lambda hp, j: (0, hp))
    blk = lambda: pl.BlockSpec((t, 128), lambda hp, j: (j, hp))
    return pl.pallas_call(
        body, name="fox_bwd", grid=(4, nq),
        in_specs=[res(), blk(), blk(), res(), res(), pl.BlockSpec((1, s, 128), lambda hp, j: (hp, 0, 0)),
                  pl.BlockSpec((1, 8, s), lambda hp, j: (hp, 0, 0))],
        out_specs=[pl.BlockSpec((128, s), lambda hp, j: (hp, 0)), blk(), blk(),
                   pl.BlockSpec((1, t, 128), lambda hp, j: (hp, j, 0)),
                   pl.BlockSpec((1, 8, s), lambda hp, j: (hp, 0, 0))],
        out_shape=[jax.ShapeDtypeStruct((512, s), F32), jax.ShapeDtypeStruct((s, 512), F32),
                   jax.ShapeDtypeStruct((s, 512), F32), jax.ShapeDtypeStruct((4, s, 128), F32),
                   jax.ShapeDtypeStruct((4, 8, s), F32)],
        scratch_shapes=[pltpu.VMEM((8, s), F32)] + [pltpu.VMEM((t, 128), F32)] * 4 + [pltpu.VMEM((t, tq), F32)] * 2,
        compiler_params=_params(("arbitrary", "arbitrary")),
    )(q, k, v, do, o, cc4, lse4)


def _mem_bwd(qm, mk, mv, dom):
    s = qm.shape[0]
    tq = min(512, s)

    def body(q_ref, mk_ref, mv_ref, do_ref, dq_ref, dmk_ref, dmv_ref):
        i = pl.program_id(0)

        @pl.when(i == 0)
        def _():
            dmk_ref[...] = jnp.zeros_like(dmk_ref)
            dmv_ref[...] = jnp.zeros_like(dmv_ref)

        for h in range(MEM_HEADS):
            hs = slice(h * 128, (h + 1) * 128)
            qh = q_ref[:, hs]
            doh = do_ref[:, hs]
            sc = _dot(qh, mk_ref[:, hs], NT) * MEM_SCALE
            mx = jnp.max(sc, axis=-1, keepdims=True)
            p = jnp.exp(sc - mx)
            p = p / jnp.sum(p, axis=-1, keepdims=True)
            dp = _dot(doh, mv_ref[:, hs], NT)
            ds = p * (dp - jnp.sum(p * dp, axis=-1, keepdims=True))
            dsb = (ds * MEM_SCALE).astype(BF16)
            dq_ref[:, hs] = _dot(dsb, mk_ref[:, hs])
            dmk_ref[:, hs] += _dot(dsb, qh, TN)
            dmv_ref[:, hs] += _dot(p.astype(BF16), doh, TN)

    return pl.pallas_call(
        body, name="mem_bwd", grid=(s // tq,),
        in_specs=[pl.BlockSpec((tq, 512), lambda i: (i, 0)), _full(mk.shape), _full(mv.shape),
                  pl.BlockSpec((tq, 512), lambda i: (i, 0))],
        out_specs=[pl.BlockSpec((tq, 512), lambda i: (i, 0)), _full(mk.shape), _full(mv.shape)],
        out_shape=[jax.ShapeDtypeStruct((s, 512), F32), jax.ShapeDtypeStruct(mk.shape, F32),
                   jax.ShapeDtypeStruct(mv.shape, F32)],
        compiler_params=_params(("arbitrary",)),
    )(qm, mk, mv, dom)


def _memkv_bwd(dmk, dmv, kv_raw, kn_mem, mem, g_mem, mem_n, w_kv):
    def body(dmk_ref, dmv_ref, kv_ref, kn_ref, mem_ref, g_ref, mn_ref, w_ref, dw_ref, dkn_ref, dg_ref, dkv_ref):
        dkn = jnp.zeros((1, 128), F32)
        for h in range(MEM_HEADS):
            hs = slice(h * 128, (h + 1) * 128)
            v = kv_ref[:, hs]
            r = lax.rsqrt(jnp.mean(v * v, axis=-1, keepdims=True) + EPS)
            n = v * r
            dn = dmk_ref[:, hs]
            dkn = dkn + jnp.sum(dn * n, axis=0, keepdims=True)
            dng = dn * kn_ref[...]
            dkv_ref[:, hs] = (r * (dng - n * jnp.mean(dng * n, axis=-1, keepdims=True))).astype(BF16)
        dkv_ref[:, 512:1024] = dmv_ref[...].astype(BF16)
        dkn_ref[...] = dkn
        dkv = dkv_ref[...]
        dw_ref[...] = _dot(mn_ref[...], dkv, TN).astype(BF16)
        dmn = _dot(dkv, w_ref[...], NT)
        xv = mem_ref[...]
        r = lax.rsqrt(jnp.mean(xv * xv, axis=-1, keepdims=True) + EPS)
        dg_ref[...] = jnp.sum(dmn * (xv * r), axis=0, keepdims=True)

    m = mem.shape[0]
    return pl.pallas_call(
        body, name="memkv_bwd",
        out_shape=[jax.ShapeDtypeStruct((D_MODEL, 1024), BF16), jax.ShapeDtypeStruct((1, 128), F32),
                   jax.ShapeDtypeStruct((1, D_MODEL), F32)],
        scratch_shapes=[pltpu.VMEM((m, 1024), BF16)],
        compiler_params=pltpu.CompilerParams(vmem_limit_bytes=VMEM_LIMIT),
    )(dmk, dmv, kv_raw, kn_mem, mem, g_mem, mem_n, w_kv)


def _fox_gate_bwd(dc, proj, b_forget128):
    s = dc.shape[0]
    tm = min(512, s)
    nt = s // tm

    def body(dc_ref, p_ref, b_ref, dfl_ref, db_ref, carry_ref):
        i = pl.program_id(0)

        @pl.when(i == 0)
        def _():
            carry_ref[...] = jnp.zeros_like(carry_ref)
            db_ref[...] = jnp.zeros_like(db_ref)

        dcv = dc_ref[...]
        dlogf = jnp.dot(_tri(tm, False), dcv, precision=lax.Precision.HIGHEST, preferred_element_type=F32) + carry_ref[...]
        carry_ref[...] += jnp.sum(dcv, axis=0, keepdims=True)
        z = p_ref[...] + b_ref[...]
        dfl = dlogf * (1.0 / (1.0 + jnp.exp(z)))
        dfl_ref[...] = dfl.astype(BF16)
        db_ref[...] += jnp.sum(dfl, axis=0, keepdims=True)

    return pl.pallas_call(
        body, name="fox_gate_bwd", grid=(nt,),
        in_specs=[pl.BlockSpec((tm, 128), lambda i: (nt - 1 - i, 0)),
                  pl.BlockSpec((tm, 128), lambda i: (nt - 1 - i, 0)), _full((1, 128))],
        out_specs=[pl.BlockSpec((tm, 128), lambda i: (nt - 1 - i, 0)), _full((1, 128))],
        out_shape=[jax.ShapeDtypeStruct((s, 128), BF16), jax.ShapeDtypeStruct((1, 128), F32)],
        scratch_shapes=[pltpu.VMEM((1, 128), F32)],
        compiler_params=_params(("arbitrary",)),
    )(dc, proj, b_forget128)


def _proj_pre_bwd(dproj, proj, dqf, dkf, dvf, dqm, dqa, dka, dva, dfl, gq_fox, gk_fox, gq_mem, gq_swa, gk_swa):
    s = proj.shape[0]
    tm = min(256, s)

    def body(dp_in, p_ref, dqf_ref, dkf_ref, dvf_ref, dqm_ref, dqa_ref, dka_ref, dva_ref, dfl_ref,
             gqf, gkf, gqm, gqa, gka, dp_ref, dgn_ref):
        i = pl.program_id(0)

        @pl.when(i == 0)
        def _():
            dgn_ref[...] = jnp.zeros_like(dgn_ref)

        def norm_bwd(off, width, hd, g_ref, dn_ref, slot):
            acc = jnp.zeros((1, 128), F32)
            for b in range(width // 128):
                v = p_ref[:, off + b * 128: off + (b + 1) * 128].astype(F32)
                r = lax.rsqrt(_group_mean(v * v, hd) + EPS)
                n = v * r
                dn = dn_ref[:, b * 128:(b + 1) * 128]
                acc = acc + jnp.sum(dn * n, axis=0, keepdims=True)
                dng = dn * g_ref[...]
                dp_ref[:, off + b * 128: off + (b + 1) * 128] = (r * (dng - n * _group_mean(dng * n, hd))).astype(BF16)
            dgn_ref[slot:slot + 1, :] += acc

        norm_bwd(H_QF, 512, HEAD, gqf, dqf_ref, 0)
        norm_bwd(H_KF, 512, HEAD, gkf, dkf_ref, 1)
        dp_ref[:, H_VF:H_VF + 512] = dvf_ref[...].astype(BF16)
        norm_bwd(H_QM, 512, MEM_HEAD, gqm, dqm_ref, 2)
        norm_bwd(H_QA, 512, HEAD, gqa, dqa_ref, 3)
        norm_bwd(H_KA, 128, HEAD, gka, dka_ref, 4)
        dp_ref[:, H_VA:H_VA + 128] = dva_ref[...].astype(BF16)
        dp_ref[:, H_FL:H_FL + 128] = dfl_ref[...]
        dp_ref[:, H_FL + 128:HALF_W] = jnp.zeros((tm, HALF_W - H_FL - 128), BF16)

    row = lambda w: pl.BlockSpec((tm, w), lambda i: (i, 0))
    g_spec = _full((1, 128))
    return pl.pallas_call(
        body, name="proj_pre_bwd", grid=(s // tm,),
        in_specs=[pl.BlockSpec(memory_space=pl.ANY), pl.BlockSpec((tm, HALF_W), lambda i: (i, 1)),
                  row(512), row(512), row(512), row(512), row(512), row(128), row(128), row(128),
                  g_spec, g_spec, g_spec, g_spec, g_spec],
        out_specs=[pl.BlockSpec((tm, HALF_W), lambda i: (i, 1)), _full((8, 128))],
        out_shape=[jax.ShapeDtypeStruct((s, PROJ_W), BF16), jax.ShapeDtypeStruct((8, 128), F32)],
        input_output_aliases={0: 0},
        compiler_params=_params(("arbitrary",)),
    )(dproj, proj, dqf, dkf, dvf, dqm, dqa, dka, dva, dfl, gq_fox, gk_fox, gq_mem, gq_swa, gk_swa)


def _in_bwd_x(dproj, w_in_p, x, g_mix, dx1):
    s = x.shape[0]
    tm = min(256, s)

    def body(dp_ref, w_ref, x_ref, g_ref, dx1_ref, gx_ref, dg_ref):
        i = pl.program_id(0)

        @pl.when(i == 0)
        def _():
            dg_ref[...] = jnp.zeros_like(dg_ref)

        dx, dg = _rms_bwd(x_ref[...], g_ref[...], _dot(dp_ref[...], w_ref[...], NT), dx1_ref[...])
        gx_ref[...] = dx
        dg_ref[...] += dg

    row = pl.BlockSpec((tm, D_MODEL), lambda i: (i, 0))
    return pl.pallas_call(
        body, name="in_bwd_x", grid=(s // tm,),
        in_specs=[pl.BlockSpec((tm, PROJ_W), lambda i: (i, 0)), _full(w_in_p.shape), row, _full((1, D_MODEL)), row],
        out_specs=[row, _full((1, D_MODEL))],
        out_shape=[jax.ShapeDtypeStruct((s, D_MODEL), F32), jax.ShapeDtypeStruct((1, D_MODEL), F32)],
        compiler_params=_params(("arbitrary",)),
    )(dproj, w_in_p, x, g_mix, dx1)


def _rel_bias_bwd(dbias, bucket):
    def body(db_ref, bk_ref, o_ref):
        bk = bk_ref[...]
        lane = lax.broadcasted_iota(jnp.int32, (1, 128), 1)
        for b in range(REL_BUCKETS):
            sel = bk == b
            acc = jnp.zeros((1, 128), F32)
            for h in range(SWA_HEADS):
                tot = jnp.sum(jnp.sum(jnp.where(sel, db_ref[h], 0.0), axis=-1, keepdims=True), axis=0, keepdims=True)
                acc = jnp.where(lane == h, tot, acc)
            o_ref[b:b + 1, :] = acc

    return pl.pallas_call(
        body, name="rel_bias_bwd",
        out_shape=jax.ShapeDtypeStruct((REL_BUCKETS, 128), F32),
        compiler_params=pltpu.CompilerParams(vmem_limit_bytes=VMEM_LIMIT),
    )(dbias, bucket)


def _my_place():
    return lax.axis_index("x"), lax.axis_index("y"), lax.axis_index("c")


def _peer(place, k):
    x, y, c = place
    return (1 - x if k & 4 else x, 1 - y if k & 2 else y, 1 - c if k & 1 else c)


def _index(place):
    x, y, c = place
    return 4 * x + 2 * y + c


HBM_SPEC = pl.BlockSpec(memory_space=pltpu.HBM)
SEM_SPEC = pl.BlockSpec(memory_space=pltpu.SEMAPHORE)
DATAFLOW = pltpu.SideEffectType.DATAFLOW_SIDE_EFFECTING


ALL_PEERS = tuple(range(1, N_DEV))
SAME_CORE = (2, 4, 6)
OWN = N_DEV - 1


def _split_copy(src_ref, land_ref, send_sems, recv_sems, me, k, gather):
    peer = _peer(me, k)
    if gather:
        src, dst = src_ref, land_ref.at[_index(me)]
    else:
        src, dst = src_ref.at[_index(peer)], land_ref.at[k - 1]
    return pltpu.make_async_remote_copy(src_ref=src, dst_ref=dst, send_sem=send_sems.at[k - 1], recv_sem=recv_sems.at[k - 1],
                                        device_id=peer, device_id_type=MESH)


def _own_copy(src_ref, land_ref, recv_sems, me, gather):
    if gather:
        src, dst = src_ref, land_ref.at[_index(me)]
    else:
        src, dst = src_ref.at[_index(me)], land_ref.at[OWN]
    return pltpu.make_async_copy(src, dst, recv_sems.at[OWN])


def _split_start(srcs, gather, name, peers=ALL_PEERS):
    n = len(srcs)

    def body(*refs):
        src_refs, land_refs = refs[:n], refs[n:2 * n]
        send_sems, recv_sems, token = refs[2 * n:3 * n], refs[3 * n:4 * n], refs[-1]
        me = _my_place()
        for w in range(n):
            for k in peers:
                _split_copy(src_refs[w], land_refs[w], send_sems[w], recv_sems[w], me, k, gather).start()
            _own_copy(src_refs[w], land_refs[w], recv_sems[w], me, gather).start()
        token[...] = jnp.zeros_like(token)

    lands = [lax.empty((N_DEV,) + (a.shape if gather else a.shape[1:]), a.dtype) for a in srcs]
    sems = [pltpu.SemaphoreType.DMA((N_DEV,))] * (2 * n)
    hbm = [pltpu.HBM(a.shape, a.dtype) for a in list(srcs) + lands]
    outs = pl.pallas_call(
        body, name=name,
        out_shape=(*sems, *hbm, jax.ShapeDtypeStruct((8, 128), F32)),
        in_specs=(HBM_SPEC,) * (2 * n),
        out_specs=(SEM_SPEC,) * (2 * n) + (HBM_SPEC,) * (2 * n) + (pl.BlockSpec(memory_space=pltpu.VMEM),),
        input_output_aliases={i: 2 * n + i for i in range(2 * n)},
        compiler_params=pltpu.CompilerParams(has_side_effects=DATAFLOW),
    )(*[pltpu.with_memory_space_constraint(a, pltpu.HBM) for a in list(srcs) + lands])
    return list(outs[:n]), list(outs[n:2 * n]), list(outs[2 * n:3 * n]), list(outs[3 * n:4 * n]), outs[-1]


def _split_wait(started, w, after, gather, name):
    send_sems, recv_sems, srcs, lands, _ = started

    def body(src_ref, land_ref, send_sems, recv_sems, after_ref, src_out, land_out):
        me = _my_place()
        for k in ALL_PEERS:
            cp = _split_copy(src_ref, land_ref, send_sems, recv_sems, me, k, gather)
            cp.wait_send()
            cp.wait_recv()
        _own_copy(src_ref, land_ref, recv_sems, me, gather).wait()

    return pl.pallas_call(
        body, name=name,
        out_shape=(pltpu.HBM(srcs[w].shape, srcs[w].dtype), pltpu.HBM(lands[w].shape, lands[w].dtype)),
        in_specs=(HBM_SPEC, HBM_SPEC, SEM_SPEC, SEM_SPEC, pl.BlockSpec(memory_space=pl.ANY)),
        out_specs=(HBM_SPEC, HBM_SPEC), input_output_aliases={0: 0, 1: 1},
        compiler_params=pltpu.CompilerParams(has_side_effects=DATAFLOW),
    )(srcs[w], lands[w], send_sems[w], recv_sems[w], after)[1]


def _forward_copy(land_ref, send_sems, recv_sems, me, j, incoming):
    sibling = _peer(me, 1)
    rows = land_ref.at[_index(_peer(sibling if incoming else me, SAME_CORE[j]))]
    return pltpu.make_async_remote_copy(src_ref=rows, dst_ref=rows, send_sem=send_sems.at[j], recv_sem=recv_sems.at[j],
                                        device_id=sibling, device_id_type=MESH)


def _forward_start(started, after, name):
    send_a, recv_a, srcs, lands, _ = started

    def body(src_ref, land_ref, send_a, recv_a, after_ref, send_b, recv_b, src_out, land_out):
        me = _my_place()
        for j, k in enumerate(SAME_CORE):
            _split_copy(src_ref, land_ref, send_a, recv_a, me, k, True).wait_recv()
            _forward_copy(land_ref, send_b, recv_b, me, j, False).start()

    sems = pltpu.SemaphoreType.DMA((len(SAME_CORE),))
    return pl.pallas_call(
        body, name=name,
        out_shape=(sems, sems, pltpu.HBM(srcs[0].shape, srcs[0].dtype), pltpu.HBM(lands[0].shape, lands[0].dtype)),
        in_specs=(HBM_SPEC, HBM_SPEC, SEM_SPEC, SEM_SPEC, pl.BlockSpec(memory_space=pl.ANY)),
        out_specs=(SEM_SPEC, SEM_SPEC, HBM_SPEC, HBM_SPEC), input_output_aliases={0: 2, 1: 3},
        compiler_params=pltpu.CompilerParams(has_side_effects=DATAFLOW),
    )(srcs[0], lands[0], send_a[0], recv_a[0], after)


def _forward_wait(started, forwarded, name):
    send_a, recv_a, _, _, _ = started
    send_b, recv_b, src, land = forwarded

    def body(src_ref, land_ref, send_a, recv_a, send_b, recv_b, src_out, land_out):
        me = _my_place()
        _own_copy(src_ref, land_ref, recv_a, me, True).wait()
        for k in (1,) + SAME_CORE:
            _split_copy(src_ref, land_ref, send_a, recv_a, me, k, True).wait_send()
        _split_copy(src_ref, land_ref, send_a, recv_a, me, 1, True).wait_recv()
        for j in range(len(SAME_CORE)):
            _forward_copy(land_ref, send_b, recv_b, me, j, False).wait_send()
            _forward_copy(land_ref, send_b, recv_b, me, j, True).wait_recv()

    return pl.pallas_call(
        body, name=name,
        out_shape=(pltpu.HBM(src.shape, src.dtype), pltpu.HBM(land.shape, land.dtype)),
        in_specs=(HBM_SPEC, HBM_SPEC, SEM_SPEC, SEM_SPEC, SEM_SPEC, SEM_SPEC),
        out_specs=(HBM_SPEC, HBM_SPEC), input_output_aliases={0: 0, 1: 1},
        compiler_params=pltpu.CompilerParams(has_side_effects=DATAFLOW),
    )(src, land, send_a[0], recv_a[0], send_b, recv_b)[1]


def _adam_math(w, g, m, v):
    m2 = ADAM_B1 * m + (1.0 - ADAM_B1) * g
    v2 = ADAM_B2 * v + (1.0 - ADAM_B2) * (g * g)
    m_hat = m2 / (1.0 - ADAM_B1 ** ADAM_STEP)
    v_hat = v2 / (1.0 - ADAM_B2 ** ADAM_STEP)
    delta = -ADAM_LR * (m_hat / (jnp.sqrt(v_hat) + ADAM_EPS) + ADAM_WD * w)
    return delta, m2, v2


def _adamw(land, w, m, v, name):
    a, b = w.shape
    bp = land.shape[2]
    ta = min(128, a)

    def body(p_ref, w_ref, m_ref, v_ref, g_ref, d_ref, m2_ref, v2_ref):
        g = p_ref[0, :, 0:b].astype(F32)
        for k in range(1, N_DEV):
            g = g + p_ref[k, :, 0:b].astype(F32)
        delta, m2, v2 = _adam_math(w_ref[...], g, m_ref[...], v_ref[...])
        g_ref[...] = g
        d_ref[...] = delta
        m2_ref[...] = m2
        v2_ref[...] = v2

    blk = pl.BlockSpec((ta, b), lambda i: (i, 0))
    sd = jax.ShapeDtypeStruct((a, b), F32)
    return pl.pallas_call(
        body, name=name, grid=(a // ta,),
        in_specs=[pl.BlockSpec((N_DEV, ta, bp), lambda i: (0, i, 0)), blk, blk, blk],
        out_specs=[blk, blk, blk, blk], out_shape=[sd, sd, sd, sd],
        compiler_params=_params(("parallel",)),
    )(land, w, m, v)


def _bucket_table():
    t_loc = jnp.arange(SWA_BLOCK)[:, None] + SWA_BLOCK
    s_loc = jnp.arange(2 * SWA_BLOCK)[None, :]
    dist = t_loc - s_loc
    max_exact = REL_BUCKETS // 2
    d = jnp.maximum(dist, 0)
    df = jnp.maximum(d, 1).astype(F32)
    large = max_exact + (jnp.log(df / max_exact) / math.log(REL_MAX_DIST / max_exact) * (REL_BUCKETS - max_exact)).astype(jnp.int32)
    large = jnp.minimum(large, REL_BUCKETS - 1)
    bucket = jnp.where(d < max_exact, d, large)
    band = (dist >= 0) & (dist < SWA_BLOCK)
    return bucket, band


def _tile2(g):
    return jnp.concatenate([g, g], axis=1) if g.shape[1] == HEAD else g


SHARD_W = 737
SHARD_WP = 768
IN_WIDTH = N_DEV * SHARD_W
SEGMENTS = ((GL0, 2824, 3072), (QF0, 768, 512), (KF0, 1280, 512), (VF0, 1792, 512), (QM0, 2312, 512),
            (QA0, 0, 512), (KA0, 512, 128), (VA0, 640, 128), (FL0, 2304, 8))


def _lane_plan(sources):
    plan = []
    for t in range(len(sources) // 128):
        groups = {}
        for lane in range(128):
            src = sources[128 * t + lane]
            if src is not None:
                slab, col = src
                groups.setdefault((slab, col // 128, (lane - col) % 128), []).append(lane)
        tile = []
        for key, lanes in groups.items():
            assert lanes == list(range(lanes[0], lanes[-1] + 1))
            tile.append((key, lanes[0], lanes[-1] + 1))
        plan.append(tile)
    return plan


def _assemble(tile_plan, load, rows):
    lane = lax.broadcasted_iota(jnp.int32, (1, 128), 1)
    out = jnp.zeros((rows, 128), F32)
    for (slab, st, roll), lo, hi in tile_plan:
        v = load(slab, st)
        if roll:
            v = pltpu.roll(v, roll, 1)
        out = v if (lo, hi) == (0, 128) else jnp.where((lane >= lo) & (lane < hi), v, out)
    return out


def _w_in_from_shards(land):
    ref_col = [None] * PROJ_W
    for p0, r0, n in SEGMENTS:
        for i in range(n):
            ref_col[p0 + i] = divmod(r0 + i, SHARD_W)
    plan = _lane_plan(ref_col)
    d_model = land.shape[1]
    tm = 256

    def body(land_ref, o_ref):
        load = lambda slab, st: land_ref[slab, :, st * 128:(st + 1) * 128].astype(F32)
        for t, tile_plan in enumerate(plan):
            o_ref[:, t * 128:(t + 1) * 128] = _assemble(tile_plan, load, tm).astype(BF16)

    return pl.pallas_call(
        body, name="w_in_from_shards", grid=(d_model // tm,),
        in_specs=[pl.BlockSpec((N_DEV, tm, SHARD_WP), lambda i: (0, i, 0))],
        out_specs=pl.BlockSpec((tm, PROJ_W), lambda i: (i, 0)),
        out_shape=jax.ShapeDtypeStruct((d_model, PROJ_W), BF16),
        compiler_params=_params(("parallel",)),
    )(land)


def _dw_in_to_parts(dwp):
    padded_col = [None] * IN_WIDTH
    for p0, r0, n in SEGMENTS:
        for i in range(n):
            padded_col[r0 + i] = p0 + i
    sources = []
    for d in range(N_DEV):
        sources += [(0, padded_col[SHARD_W * d + c]) if c < SHARD_W else None for c in range(SHARD_WP)]
    plan = _lane_plan(sources)
    d_model = dwp.shape[0]
    tm = 256
    tiles = SHARD_WP // 128

    def body(dw_ref, o_ref):
        load = lambda slab, st: dw_ref[:, st * 128:(st + 1) * 128].astype(F32)
        for t, tile_plan in enumerate(plan):
            d, c = divmod(t, tiles)
            o_ref[d, :, c * 128:(c + 1) * 128] = _assemble(tile_plan, load, tm).astype(BF16)

    return pl.pallas_call(
        body, name="dw_in_to_parts", grid=(d_model // tm,),
        in_specs=[pl.BlockSpec((tm, PROJ_W), lambda i: (i, 0))],
        out_specs=pl.BlockSpec((N_DEV, tm, SHARD_WP), lambda i: (0, i, 0)),
        out_shape=jax.ShapeDtypeStruct((N_DEV, d_model, SHARD_WP), BF16),
        compiler_params=_params(("parallel",)),
    )(dwp)


def _cast_shards(shards):
    names = list(shards)

    def body(*refs):
        for src, dst in zip(refs[:len(names)], refs[len(names):]):
            if dst.shape != src.shape:
                dst[...] = jnp.zeros(dst.shape, BF16)
                dst[:, 0:src.shape[1]] = src[...].astype(BF16)
            else:
                dst[...] = src[...].astype(BF16)

    out_shape = [jax.ShapeDtypeStruct((shards[n].shape[0], SHARD_WP if n == "w_in" else shards[n].shape[1]), BF16)
                 for n in names]
    outs = pl.pallas_call(body, name="cast_shards", out_shape=out_shape,
                          compiler_params=pltpu.CompilerParams(vmem_limit_bytes=VMEM_LIMIT))(*[shards[n] for n in names])
    return dict(zip(names, outs))


def _tie(x, *tokens):
    for t in tokens:
        if t is not None:
            x = x + t[0:1, 0:1]
    return x


def _local_step(x, mem, target, p, getw, emit, deps=()):
    s = x.shape[0]
    bucket, band = _bucket_table()
    bucket_m = jnp.where(band, bucket, -1).astype(jnp.int32)
    bias = _bias_table(p["rel_bias"], bucket_m)
    bucket_t = jnp.transpose(bucket_m)
    bias_t = _bias_table(p["rel_bias"], bucket_t)
    gqf, gkf, gqa, gka = _tile2(p["qn_fox"]), _tile2(p["kn_fox"]), _tile2(p["qn_swa"]), _tile2(p["kn_swa"])
    gqm = p["qn_mem"]
    bf128 = jnp.pad(p["b_forget"], ((0, 0), (0, 120)))
    sink = p["sink_swa"].reshape(8)

    h = _rms_fwd(x, p["g_mix"], "rms_mix", deps)
    w_in = getw("w_in", h)
    proj = _mm(h, w_in, "nn", BF16, 512, 1536, 1024, "proj")
    fl = _mm(h, w_in[:, FL0:FL0 + 128], "nn", F32, 512, 128, 1024, "proj_fl")
    qf, kf, vf, qm, qa, ka, va = _proj_post(proj, gqf, gkf, gqm, gqa, gka)
    cc4 = _fox_gate_fwd(fl, bf128)
    w_kv = getw("w_mem_kv", cc4)
    mem_n, kv_raw, mk, mv = _memkv_fwd(mem, p["g_mem"], w_kv, p["kn_mem"])
    kp = jnp.pad(ka, ((SWA_BLOCK, 0), (0, 0)))
    vp = jnp.pad(va, ((SWA_BLOCK, 0), (0, 0)))
    oa = _swa_fwd(qa, kp, vp, bias, sink)
    of, lse4 = _fox_fwd(qf, kf, jnp.transpose(vf), cc4)
    om = _mem_fwd(qm, mk, mv)
    wa, wf, wm, w_out = getw("w_o_swa", oa), getw("w_o_fox", oa), getw("w_o_mem", oa), getw("w_out", oa)
    x1, hm, merged = _merge_fwd(x, oa, of, om, proj, p["b_gate"], wa, wf, wm, w_out, p["g_mlp"])
    w_up = getw("w_mlp_up", of)
    u = _mlp_up(hm, w_up)
    w_down = getw("w_mlp_down", hm)
    dy, dy_b, loss = _mlp_down_loss(u, w_down, x1, target)

    da = _mlp_bwd_act(dy_b, w_down, u)
    t_down = emit({"w_mlp_down": _mm(u, dy_b, "tn", BF16, 1024, 1024, 512, "dw_down")})
    dx1, dg_mlp = _mlp_bwd_x(da, w_up, x1, dy, _tie(p["g_mlp"], t_down))
    t_up = emit({"w_mlp_up": _mm(hm, da, "tn", BF16, 1024, 1024, 512, "dw_up", column_chunks=True)})
    dproj, doa, dof, dom, dya, dyf, dym, db_gate = _merge_bwd(
        dx1, oa, of, om, proj, _tie(p["b_gate"], t_up), wa, wf, wm, w_out)
    t_o = emit({"w_out": _mm(merged, dx1, "tn", BF16, 512, 1024, 512, "dw_out"),
                "w_o_swa": _mm(oa, dya, "tn", BF16, 512, 1024, 512, "dw_o_swa"),
                "w_o_fox": _mm(of, dyf, "tn", BF16, 512, 1024, 512, "dw_o_fox"),
                "w_o_mem": _mm(om, dym, "tn", BF16, 512, 1024, 512, "dw_o_mem")})

    dqm, dmk, dmv = _mem_bwd(qm, mk, mv, dom)
    dw_kv, dkn_mem, dg_mem = _memkv_bwd(dmk, dmv, kv_raw, _tie(p["kn_mem"], t_o), mem, p["g_mem"], mem_n, w_kv)
    t_kv = emit({"w_mem_kv": dw_kv})
    dqa, dkp, dvp, dbias, dsink = _swa_bwd(qa, kp, vp, bias_t, _tie(p["sink_swa"], t_kv).reshape(8), doa)
    dqf_t, dkf, dvf, dck4, dcq4 = _fox_bwd(qf, kf, vf, dof, of, cc4, lse4)
    dqf = jnp.transpose(dqf_t)

    dcq = jnp.transpose(dcq4[:, 0:2, :], (2, 0, 1)).reshape(s, 8)
    dck = jnp.transpose(dck4[:, :, 0:2], (1, 0, 2)).reshape(s, 8)
    dc = jnp.pad(dcq - dck, ((0, 0), (0, 120)))
    dfl, db_forget = _fox_gate_bwd(dc, fl, bf128)

    dproj, dgn = _proj_pre_bwd(dproj, proj, dqf, dkf, dvf, dqm, dqa, dkp[SWA_BLOCK:], dvp[SWA_BLOCK:], dfl,
                               gqf, gkf, gqm, gqa, gka)
    t_in = emit({"w_in": _mm(h, dproj, "tn", BF16, 1024, 3072, 512, "dw_in")})
    grad_x, dg_mix = _in_bwd_x(dproj, w_in, x, _tie(p["g_mix"], t_in), dx1)
    d_rel = _rel_bias_bwd(dbias, bucket_t)

    fold = lambda r: dgn[r:r + 1, 0:HEAD] + dgn[r:r + 1, HEAD:128]
    small = {
        "g_mix": dg_mix, "b_gate": db_gate, "b_forget": db_forget[:, 0:8],
        "qn_swa": fold(3), "kn_swa": fold(4), "sink_swa": dsink[:, 0].reshape(1, 8), "rel_bias": d_rel[:, 0:8],
        "qn_fox": fold(0), "kn_fox": fold(1), "g_mem": dg_mem, "qn_mem": dgn[2:3, :], "kn_mem": dkn_mem,
        "g_mlp": dg_mlp,
    }
    return loss, grad_x, small


SMALL = ("g_mix", "b_gate", "b_forget", "qn_swa", "kn_swa", "sink_swa", "rel_bias", "qn_fox", "kn_fox", "g_mem",
         "qn_mem", "kn_mem", "g_mlp")
BIG = ("w_in", "w_mem_kv", "w_o_swa", "w_o_fox", "w_o_mem", "w_out", "w_mlp_up", "w_mlp_down")
COL_SHARDED = ("w_in", "w_o_swa", "w_o_fox", "w_o_mem", "w_mlp_up")
WEIGHTS = ("g_mix", "w_in", "b_gate", "b_forget", "qn_swa", "kn_swa", "sink_swa", "rel_bias", "qn_fox", "kn_fox", "g_mem",
           "w_mem_kv", "qn_mem", "kn_mem", "w_o_swa", "w_o_fox", "w_o_mem", "w_out", "g_mlp", "w_mlp_up", "w_mlp_down")
SMALL_USED = 6928
SMALL_PAD = 7168


def _gathered_to_full(name, g):
    if name in COL_SHARDED:
        return jnp.transpose(g, (1, 0, 2)).reshape(g.shape[1], N_DEV * g.shape[2])
    return g.reshape(N_DEV * g.shape[1], g.shape[2])


def _full_to_parts(name, full, b):
    if name in COL_SHARDED:
        return jnp.transpose(full.reshape(full.shape[0], N_DEV, b), (1, 0, 2)).astype(BF16)
    return full.reshape(N_DEV, full.shape[0] // N_DEV, full.shape[1]).astype(BF16)


def _pack_small(d, loss=None):
    flat = jnp.concatenate([d[n].reshape(-1) for n in SMALL])
    assert flat.shape[0] == SMALL_USED
    if loss is not None:
        flat = jnp.concatenate([flat, loss.reshape(-1)])
    return jnp.pad(flat, (0, SMALL_PAD - flat.shape[0])).reshape(8, SMALL_PAD // 8)


def _unpack_small(packed, like):
    flat = packed.reshape(-1)
    out, off = {}, 0
    for n in SMALL:
        size = like[n].size
        out[n] = flat[off:off + size].reshape(like[n].shape)
        off += size
    return out


def _adamw_small(parts, w, m, v):
    def body(p_ref, w_ref, m_ref, v_ref, g_ref, d_ref, m2_ref, v2_ref):
        g = p_ref[0]
        for k in range(1, N_DEV):
            g = g + p_ref[k]
        delta, m2, v2 = _adam_math(w_ref[...], g, m_ref[...], v_ref[...])
        g_ref[...] = g
        d_ref[...] = delta
        m2_ref[...] = m2
        v2_ref[...] = v2

    sd = jax.ShapeDtypeStruct(w.shape, F32)
    return pl.pallas_call(body, name="adamw_small", out_shape=[sd, sd, sd, sd])(parts, w, m, v)


def kernel(x, mem, g_mix, w_in, b_gate, b_forget, qn_swa, kn_swa, sink_swa, rel_bias, qn_fox, kn_fox, g_mem, w_mem_kv, qn_mem, kn_mem, w_o_swa, w_o_fox, w_o_mem, w_out, g_mlp, w_mlp_up, w_mlp_down, loss_target, m_g_mix, m_w_in, m_b_gate, m_b_forget, m_qn_swa, m_kn_swa, m_sink_swa, m_rel_bias, m_qn_fox, m_kn_fox, m_g_mem, m_w_mem_kv, m_qn_mem, m_kn_mem, m_w_o_swa, m_w_o_fox, m_w_o_mem, m_w_out, m_g_mlp, m_w_mlp_up, m_w_mlp_down, v_g_mix, v_w_in, v_b_gate, v_b_forget, v_qn_swa, v_kn_swa, v_sink_swa, v_rel_bias, v_qn_fox, v_kn_fox, v_g_mem, v_w_mem_kv, v_qn_mem, v_kn_mem, v_w_o_swa, v_w_o_fox, v_w_o_mem, v_w_out, v_g_mlp, v_w_mlp_up, v_w_mlp_down):
    wts = dict(g_mix=g_mix, w_in=w_in, b_gate=b_gate, b_forget=b_forget, qn_swa=qn_swa, kn_swa=kn_swa, sink_swa=sink_swa,
               rel_bias=rel_bias, qn_fox=qn_fox, kn_fox=kn_fox, g_mem=g_mem, w_mem_kv=w_mem_kv, qn_mem=qn_mem, kn_mem=kn_mem,
               w_o_swa=w_o_swa, w_o_fox=w_o_fox, w_o_mem=w_o_mem, w_out=w_out, g_mlp=g_mlp, w_mlp_up=w_mlp_up,
               w_mlp_down=w_mlp_down)
    mom = dict(g_mix=m_g_mix, w_in=m_w_in, b_gate=m_b_gate, b_forget=m_b_forget, qn_swa=m_qn_swa, kn_swa=m_kn_swa,
               sink_swa=m_sink_swa, rel_bias=m_rel_bias, qn_fox=m_qn_fox, kn_fox=m_kn_fox, g_mem=m_g_mem, w_mem_kv=m_w_mem_kv,
               qn_mem=m_qn_mem, kn_mem=m_kn_mem, w_o_swa=m_w_o_swa, w_o_fox=m_w_o_fox, w_o_mem=m_w_o_mem, w_out=m_w_out,
               g_mlp=m_g_mlp, w_mlp_up=m_w_mlp_up, w_mlp_down=m_w_mlp_down)
    var = dict(g_mix=v_g_mix, w_in=v_w_in, b_gate=v_b_gate, b_forget=v_b_forget, qn_swa=v_qn_swa, kn_swa=v_kn_swa,
               sink_swa=v_sink_swa, rel_bias=v_rel_bias, qn_fox=v_qn_fox, kn_fox=v_kn_fox, g_mem=v_g_mem, w_mem_kv=v_w_mem_kv,
               qn_mem=v_qn_mem, kn_mem=v_kn_mem, w_o_swa=v_w_o_swa, w_o_fox=v_w_o_fox, w_o_mem=v_w_o_mem, w_out=v_w_out,
               g_mlp=v_g_mlp, w_mlp_up=v_w_mlp_up, w_mlp_down=v_w_mlp_down)

    shards = _cast_shards({n: wts[n][0] for n in BIG})
    first = _split_start([shards["w_in"]], True, "ag_start_w_in", peers=(1,) + SAME_CORE)
    rest = _split_start([shards[n] for n in BIG[1:]], True, "ag_start_rest")
    full = {}

    def getw(n, after):
        if n == "w_in" and n not in full:
            forwarded = _forward_start(first, after, "ag_forward_w_in")
            full[n] = _w_in_from_shards(_forward_wait(first, forwarded, "ag_wait_w_in"))
        elif n not in full:
            land = _split_wait(rest, BIG[1:].index(n), after, True, "ag_wait_" + n)
            full[n] = land if n == "w_mlp_up" else _gathered_to_full(n, land)
        return full[n]

    exchanges = {}

    def emit(grads_by_name):
        parts = []
        for n, grad in grads_by_name.items():
            if n == "w_in":
                parts.append(_dw_in_to_parts(grad))
            else:
                parts.append(grad if n == "w_mlp_up" else _full_to_parts(n, grad, wts[n].shape[2]))
        started = _split_start(parts, False, "rs_start_" + next(iter(grads_by_name)))
        for w, n in enumerate(grads_by_name):
            exchanges[n] = (started, w)
        return started[4]

    small_p = {n: wts[n] for n in SMALL}
    loss, grad_x, small_g = _local_step(x[0], mem[0], loss_target[0], small_p, getw, emit, (first[4], rest[4]))

    packed = _pack_small(small_g, loss)
    small_gather = _split_start([packed], True, "ag_start_small")

    grads, delta, new_m, new_v = {}, {}, {}, {}

    def update(n, after):
        land = _split_wait(*exchanges[n], after, False, "rs_wait_" + n)
        g, d, m2, v2 = _adamw(land, wts[n][0], mom[n][0], var[n][0], "adamw_" + n)
        grads[n], delta[n], new_m[n], new_v[n] = g[None], d[None], m2[None], v2[None]
        return d

    after = small_gather[4]
    for n in exchanges:
        if n != "w_in":
            after = update(n, after)

    gathered = _split_wait(small_gather, 0, after, True, "ag_wait_small")
    g, d, m2, v2 = _adamw_small(gathered, _pack_small(small_p), _pack_small({n: mom[n] for n in SMALL}),
                                _pack_small({n: var[n] for n in SMALL}))
    for dst, flat in ((grads, g), (delta, d), (new_m, m2), (new_v, v2)):
        dst.update(_unpack_small(flat, small_p))
    total = g.reshape(-1)[SMALL_USED]
    update("w_in", d)

    return (total, grad_x[None], *[grads[n] for n in WEIGHTS], *[delta[n] for n in WEIGHTS],
            *[new_m[n] for n in WEIGHTS], *[new_v[n] for n in WEIGHTS])
```

```python
import functools
import math

import jax
import jax.numpy as jnp
from jax import lax
from jax.experimental import pallas as pl
from jax.experimental.pallas import tpu as pltpu

F32 = jnp.float32
BF16 = jnp.bfloat16

D_MODEL = 1024
N_MEM = 256
D_FF = 4096
HEAD = 64
SWA_HEADS = 8
SWA_BLOCK = 128
MEM_HEADS = 4
MEM_HEAD = 128
EPS = 1e-6
NEG = -1e30
REL_BUCKETS = 32
REL_MAX_DIST = 128

ADAM_LR = 0.001
ADAM_B1 = 0.9
ADAM_B2 = 0.999
ADAM_EPS = 1e-08
ADAM_WD = 0.01
ADAM_STEP = 10

GL0, QF0, KF0, VF0, QM0, QA0, KA0, VA0, FL0 = 0, 3072, 3584, 4096, 4608, 5120, 5632, 5760, 5888
PROJ_W = 6144
HALF_W = 3072
H_QF, H_KF, H_VF, H_QM, H_QA, H_KA, H_VA, H_FL = 0, 512, 1024, 1536, 2048, 2560, 2688, 2816

VMEM_LIMIT = 56 * 1024 * 1024
N_DEV = 8
MESH = pl.DeviceIdType.MESH

NN = (((1,), (0,)), ((), ()))
NT = (((1,), (1,)), ((), ()))
TN = (((0,), (0,)), ((), ()))


def _dot(a, b, dims=NN):
    return lax.dot_general(a, b, dims, preferred_element_type=F32)


def _params(sem):
    return pltpu.CompilerParams(dimension_semantics=sem, vmem_limit_bytes=VMEM_LIMIT)


def _full(shape):
    nd = len(shape)
    return pl.BlockSpec(shape, lambda *_: (0,) * nd)


def _sigmoid(z):
    return 1.0 / (1.0 + jnp.exp(-z))


def _group_mean(v, hd):
    if hd == 128:
        return jnp.mean(v, axis=-1, keepdims=True)
    lane = lax.broadcasted_iota(jnp.int32, v.shape, 1)
    lo = lane < HEAD
    s_lo = jnp.sum(jnp.where(lo, v, 0.0), axis=-1, keepdims=True)
    s_hi = jnp.sum(jnp.where(lo, 0.0, v), axis=-1, keepdims=True)
    return jnp.where(lo, s_lo, s_hi) * (1.0 / HEAD)


def _mm(a, b, mode, out_dtype, tm, tn, tk, name, column_chunks=False):
    if mode == "nn":
        m, k = a.shape
        n = b.shape[1]
    elif mode == "nt":
        m, k = a.shape
        n = b.shape[0]
    else:
        k, m = a.shape
        n = b.shape[1]
    tm, tn, tk = min(tm, m), min(tn, n), min(tk, k)
    nk = k // tk
    chunk = n // N_DEV
    per_tile = tn // chunk if column_chunks else 1
    dims = {"nn": NN, "nt": NT, "tn": TN}[mode]
    a_spec = pl.BlockSpec((tk, tm), lambda j, i, kk: (kk, i)) if mode == "tn" else pl.BlockSpec((tm, tk), lambda j, i, kk: (i, kk))
    b_spec = pl.BlockSpec((tn, tk), lambda j, i, kk: (j, kk)) if mode == "nt" else pl.BlockSpec((tk, tn), lambda j, i, kk: (kk, j))

    def body(a_ref, b_ref, o_ref, *acc):
        prod = _dot(a_ref[...].astype(BF16), b_ref[...].astype(BF16), dims)

        def write(res):
            if column_chunks:
                for c in range(per_tile):
                    o_ref[c] = res[:, c * chunk:(c + 1) * chunk].astype(o_ref.dtype)
            else:
                o_ref[...] = res.astype(o_ref.dtype)

        if nk == 1:
            write(prod)
        else:
            acc_ref, = acc
            kk = pl.program_id(2)

            @pl.when(kk == 0)
            def _():
                acc_ref[...] = prod

            @pl.when(kk > 0)
            def _():
                acc_ref[...] += prod

            @pl.when(kk == nk - 1)
            def _():
                write(acc_ref[...])

    return pl.pallas_call(
        body, name=name, grid=(n // tn, m // tm, nk),
        in_specs=[a_spec, b_spec],
        out_specs=(pl.BlockSpec((per_tile, tm, chunk), lambda j, i, kk: (j, i, 0)) if column_chunks
                   else pl.BlockSpec((tm, tn), lambda j, i, kk: (i, j))),
        out_shape=jax.ShapeDtypeStruct((N_DEV, m, chunk) if column_chunks else (m, n), out_dtype),
        scratch_shapes=[pltpu.VMEM((tm, tn), F32)] if nk > 1 else [],
        compiler_params=_params(("parallel", "parallel", "arbitrary")),
    )(a, b)


def _rms_fwd(x, g, name, deps=()):
    s, d = x.shape
    tm = min(512, s)

    def body(x_ref, g_ref, *rest):
        h_ref = rest[len(deps)]
        xv = x_ref[...]
        r = lax.rsqrt(jnp.mean(xv * xv, axis=-1, keepdims=True) + EPS)
        h_ref[...] = (xv * r * g_ref[...]).astype(BF16)

    return pl.pallas_call(
        body, name=name, grid=(s // tm,),
        in_specs=[pl.BlockSpec((tm, d), lambda i: (i, 0)), _full((1, d))] + [pl.BlockSpec(memory_space=pl.ANY)] * len(deps),
        out_specs=pl.BlockSpec((tm, d), lambda i: (i, 0)),
        out_shape=jax.ShapeDtypeStruct((s, d), BF16),
        compiler_params=_params(("parallel",)),
    )(x, g, *deps)


def _proj_post(proj, gq_fox, gk_fox, gq_mem, gq_swa, gk_swa):
    s = proj.shape[0]
    tm = min(256, s)

    def body(p_ref, gqf, gkf, gqm, gqa, gka, qf_ref, kf_ref, vf_ref, qm_ref, qa_ref, ka_ref, va_ref):
        def norm(off, width, hd, g_ref, o_ref):
            for b in range(width // 128):
                v = p_ref[:, off + b * 128: off + (b + 1) * 128].astype(F32)
                r = lax.rsqrt(_group_mean(v * v, hd) + EPS)
                o_ref[:, b * 128:(b + 1) * 128] = (v * r * g_ref[...]).astype(BF16)

        norm(H_QF, 512, HEAD, gqf, qf_ref)
        norm(H_KF, 512, HEAD, gkf, kf_ref)
        vf_ref[...] = p_ref[:, H_VF:H_VF + 512].astype(BF16)
        norm(H_QM, 512, MEM_HEAD, gqm, qm_ref)
        norm(H_QA, 512, HEAD, gqa, qa_ref)
        norm(H_KA, 128, HEAD, gka, ka_ref)
        va_ref[...] = p_ref[:, H_VA:H_VA + 128].astype(BF16)

    g_spec = _full((1, 128))
    o512 = pl.BlockSpec((tm, 512), lambda i: (i, 0))
    o128 = pl.BlockSpec((tm, 128), lambda i: (i, 0))
    s512 = jax.ShapeDtypeStruct((s, 512), BF16)
    s128 = jax.ShapeDtypeStruct((s, 128), BF16)
    return pl.pallas_call(
        body, name="proj_post", grid=(s // tm,),
        in_specs=[pl.BlockSpec((tm, HALF_W), lambda i: (i, 1)), g_spec, g_spec, g_spec, g_spec, g_spec],
        out_specs=[o512, o512, o512, o512, o512, o128, o128],
        out_shape=[s512, s512, s512, s512, s512, s128, s128],
        compiler_params=_params(("parallel",)),
    )(proj, gq_fox, gk_fox, gq_mem, gq_swa, gk_swa)


def _tri(n, lower):
    r = lax.broadcasted_iota(jnp.int32, (n, n), 0)
    c = lax.broadcasted_iota(jnp.int32, (n, n), 1)
    return jnp.where((c <= r) if lower else (c >= r), 1.0, 0.0).astype(F32)


def _fox_gate_fwd(proj, b_forget128):
    s = proj.shape[0]
    tm = min(512, s)

    def body(p_ref, b_ref, cc_ref, carry_ref):
        i = pl.program_id(0)

        @pl.when(i == 0)
        def _():
            carry_ref[...] = jnp.zeros_like(carry_ref)

        z = p_ref[...] + b_ref[...]
        logf = jnp.minimum(z, 0.0) - jnp.log(1.0 + jnp.exp(-jnp.abs(z)))
        c = jnp.dot(_tri(tm, True), logf, precision=lax.Precision.HIGHEST, preferred_element_type=F32) + carry_ref[...]
        carry_ref[...] = c[tm - 1:tm, :]
        for hp in range(4):
            cc_ref[hp] = c if hp == 0 else pltpu.roll(c, 128 - 2 * hp, 1)

    return pl.pallas_call(
        body, name="fox_gate_fwd", grid=(s // tm,),
        in_specs=[pl.BlockSpec((tm, 128), lambda i: (i, 0)), _full((1, 128))],
        out_specs=pl.BlockSpec((4, tm, 128), lambda i: (0, i, 0)),
        out_shape=jax.ShapeDtypeStruct((4, s, 128), F32),
        scratch_shapes=[pltpu.VMEM((1, 128), F32)],
        compiler_params=_params(("arbitrary",)),
    )(proj, b_forget128)


def _memkv_fwd(mem, g_mem, w_kv, kn_mem):
    m = mem.shape[0]

    def body(mem_ref, g_ref, w_ref, kn_ref, memn_ref, kv_ref, mk_ref, mv_ref):
        xv = mem_ref[...]
        r = lax.rsqrt(jnp.mean(xv * xv, axis=-1, keepdims=True) + EPS)
        mn = (xv * r * g_ref[...]).astype(BF16)
        memn_ref[...] = mn
        kv = _dot(mn, w_ref[...])
        kv_ref[...] = kv
        for h in range(MEM_HEADS):
            v = kv[:, h * 128:(h + 1) * 128]
            rr = lax.rsqrt(jnp.mean(v * v, axis=-1, keepdims=True) + EPS)
            mk_ref[:, h * 128:(h + 1) * 128] = (v * rr * kn_ref[...]).astype(BF16)
        mv_ref[...] = kv[:, 512:1024].astype(BF16)

    return pl.pallas_call(
        body, name="memkv_fwd",
        out_shape=[jax.ShapeDtypeStruct((m, D_MODEL), BF16), jax.ShapeDtypeStruct((m, 1024), F32),
                   jax.ShapeDtypeStruct((m, 512), BF16), jax.ShapeDtypeStruct((m, 512), BF16)],
        compiler_params=pltpu.CompilerParams(vmem_limit_bytes=VMEM_LIMIT),
    )(mem, g_mem, w_kv, kn_mem)


def _bias_table(rel_bias, bucket):
    def body(rb_ref, bk_ref, o_ref):
        bk = bk_ref[...]
        for h in range(SWA_HEADS):
            acc = jnp.zeros(bk.shape, F32)
            for b in range(REL_BUCKETS):
                acc = jnp.where(bk == b, rb_ref[b, h], acc)
            o_ref[h] = acc

    return pl.pallas_call(
        body, name="bias_table",
        in_specs=[pl.BlockSpec(memory_space=pltpu.SMEM), pl.BlockSpec(memory_space=pltpu.VMEM)],
        out_shape=jax.ShapeDtypeStruct((SWA_HEADS,) + bucket.shape, F32),
    )(rel_bias, bucket)


def _swa_valid(n):
    row = lax.broadcasted_iota(jnp.int32, (SWA_BLOCK, 2 * SWA_BLOCK), 0)
    col = lax.broadcasted_iota(jnp.int32, (SWA_BLOCK, 2 * SWA_BLOCK), 1)
    dist = row + SWA_BLOCK - col
    return (dist >= 0) & (dist < SWA_BLOCK) & ((col >= SWA_BLOCK) | (n > 0))


def _swa_fwd(qa, kp, vp, bias, sink):
    s = qa.shape[0]
    nb = s // SWA_BLOCK

    def body(sink_ref, q_ref, kp_ref, vp_ref, bias_ref, o_ref):
        n = pl.program_id(0)
        start = pl.multiple_of(n * SWA_BLOCK, SWA_BLOCK)
        k2 = kp_ref[pl.ds(start, 2 * SWA_BLOCK), :]
        v2 = vp_ref[pl.ds(start, 2 * SWA_BLOCK), :]
        valid = _swa_valid(n)
        heads = range(SWA_HEADS)
        hs = lambda h: slice(h * HEAD, (h + 1) * HEAD)
        sc = [jnp.where(valid, _dot(q_ref[:, hs(h)], k2[:, hs(h // 4)], NT) * 0.125 + bias_ref[h], NEG) for h in heads]
        pn = []
        for h in heads:
            sk = sink_ref[h]
            mx = jnp.maximum(jnp.max(sc[h], axis=-1, keepdims=True), sk)
            p = jnp.exp(sc[h] - mx)
            den = jnp.sum(p, axis=-1, keepdims=True) + jnp.exp(sk - mx)
            pn.append((p / den).astype(BF16))
        outs = [_dot(pn[h], v2[:, hs(h // 4)]).astype(BF16) for h in heads]
        for h in heads:
            o_ref[:, hs(h)] = outs[h]

    return pl.pallas_call(
        body, name="swa_fwd", grid=(nb,),
        in_specs=[pl.BlockSpec(memory_space=pltpu.SMEM),
                  pl.BlockSpec((SWA_BLOCK, 512), lambda n: (n, 0)),
                  _full(kp.shape), _full(vp.shape), _full(bias.shape)],
        out_specs=pl.BlockSpec((SWA_BLOCK, 512), lambda n: (n, 0)),
        out_shape=jax.ShapeDtypeStruct((s, 512), BF16),
        compiler_params=_params(("parallel",)),
    )(sink, qa, kp, vp, bias)


def _head_mask(e):
    lane = lax.broadcasted_iota(jnp.int32, (1, 128), 1)
    return (lane >= e * HEAD) & (lane < (e + 1) * HEAD)


FOX_FWD_TQ, FOX_FWD_TK = 1024, 1024
FOX_BWD_TK, FOX_BWD_TQ = 256, 512


def _head_rows(e):
    row = lax.broadcasted_iota(jnp.int32, (128, 1), 0)
    return (row >= e * HEAD) & (row < (e + 1) * HEAD)


def _fox_fwd(q, k, v_t, cc4):
    s = q.shape[0]
    t = min(FOX_FWD_TQ, s)
    tk = min(FOX_FWD_TK, s)
    nq = s // t

    def body(q_ref, k_ref, vt_ref, cc_ref, o_ref, lse_ref):
        i = pl.program_id(1)
        qs = q_ref[...] * jnp.asarray(0.125, BF16)
        qe = [jnp.where(_head_mask(e), qs, jnp.zeros_like(qs)) for e in range(2)]
        n_full = (i * t) // tk
        krow = lax.broadcasted_iota(jnp.int32, (tk, t), 0) + n_full * tk
        qcol = lax.broadcasted_iota(jnp.int32, (tk, t), 1) + i * t

        def step(j, carry, masked):
            ks = pl.ds(pl.multiple_of(j * tk, tk), tk)
            kj = k_ref[ks, :]
            vtj = vt_ref[:, ks]
            out = []
            for e in range(2):
                m, acc = carry[2 * e], carry[2 * e + 1]
                st = _dot(kj, qe[e], NT) - cc_ref[0, ks, e:e + 1]
                if masked:
                    st = jnp.where(krow <= qcol, st, NEG)
                m_new = jnp.maximum(m, jnp.max(st, axis=0, keepdims=True))
                alpha = jnp.exp(m - m_new)
                pt = jnp.exp(st - m_new).astype(BF16)
                vte = jnp.where(_head_rows(e), vtj, jnp.ones_like(vtj))
                out += [m_new, alpha * acc + _dot(vte, pt)]
            return tuple(out)

        init = (jnp.full((1, t), NEG, F32), jnp.zeros((128, t), F32)) * 2
        carry = lax.fori_loop(0, n_full, functools.partial(step, masked=False), init)
        m0, a0, m1, a1 = step(n_full, carry, True)
        l0 = a0[HEAD:HEAD + 1, :]
        l1 = a1[0:1, :]
        o_t = jnp.where(_head_rows(0), a0 / l0, a1 / l1)
        o_ref[...] = o_t.T.astype(BF16)
        r8 = lax.broadcasted_iota(jnp.int32, (8, t), 0)
        lse_ref[0] = jnp.where(r8 == 0, m0 + jnp.log(l0), jnp.where(r8 == 1, m1 + jnp.log(l1), 0.0))

    return pl.pallas_call(
        body, name="fox_fwd", grid=(4, nq),
        in_specs=[pl.BlockSpec((t, 128), lambda hp, i: (i, hp)),
                  pl.BlockSpec((s, 128), lambda hp, i: (0, hp)),
                  pl.BlockSpec((128, s), lambda hp, i: (hp, 0)),
                  pl.BlockSpec((1, s, 128), lambda hp, i: (hp, 0, 0))],
        out_specs=[pl.BlockSpec((t, 128), lambda hp, i: (i, hp)),
                   pl.BlockSpec((1, 8, t), lambda hp, i: (hp, 0, i))],
        out_shape=[jax.ShapeDtypeStruct((s, 512), BF16), jax.ShapeDtypeStruct((4, 8, s), F32)],
        compiler_params=_params(("parallel", "parallel")),
    )(q, k, v_t, cc4)


MEM_SCALE = MEM_HEAD ** -0.5


def _mem_fwd(qm, mk, mv):
    s = qm.shape[0]
    tq = min(512, s)

    def body(q_ref, mk_ref, mv_ref, o_ref):
        for h in range(MEM_HEADS):
            hs = slice(h * 128, (h + 1) * 128)
            sc = _dot(q_ref[:, hs], mk_ref[:, hs], NT) * MEM_SCALE
            mx = jnp.max(sc, axis=-1, keepdims=True)
            p = jnp.exp(sc - mx)
            p = p / jnp.sum(p, axis=-1, keepdims=True)
            o_ref[:, hs] = _dot(p.astype(BF16), mv_ref[:, hs]).astype(BF16)

    return pl.pallas_call(
        body, name="mem_fwd", grid=(s // tq,),
        in_specs=[pl.BlockSpec((tq, 512), lambda i: (i, 0)), _full(mk.shape), _full(mv.shape)],
        out_specs=pl.BlockSpec((tq, 512), lambda i: (i, 0)),
        out_shape=jax.ShapeDtypeStruct((s, 512), BF16),
        compiler_params=_params(("parallel",)),
    )(qm, mk, mv)


def _merge_fwd(x, oa, of, om, proj, b_gate, wa, wf, wm, w_out, g_mlp):
    s = x.shape[0]
    tm = min(256, s)

    def body(x_ref, oa_ref, of_ref, om_ref, gl_ref, bg_ref, wa_ref, wf_ref, wm_ref, wo_ref, g_ref, x1_ref, hm_ref, mg_ref):
        merged = None
        for b, (o_ref, w_ref) in enumerate(((oa_ref, wa_ref), (of_ref, wf_ref), (om_ref, wm_ref))):
            cs = slice(b * D_MODEL, (b + 1) * D_MODEL)
            y = _dot(o_ref[...], w_ref[...])
            t = _sigmoid(gl_ref[:, cs].astype(F32) + bg_ref[:, cs]) * y
            merged = t if merged is None else merged + t
        mb = merged.astype(BF16)
        mg_ref[...] = mb
        x1 = x_ref[...] + _dot(mb, wo_ref[...])
        x1_ref[...] = x1
        r = lax.rsqrt(jnp.mean(x1 * x1, axis=-1, keepdims=True) + EPS)
        hm_ref[...] = (x1 * r * g_ref[...]).astype(BF16)

    row = lambda w: pl.BlockSpec((tm, w), lambda i: (i, 0))
    return pl.pallas_call(
        body, name="merge_fwd", grid=(s // tm,),
        in_specs=[row(D_MODEL), row(512), row(512), row(512), row(HALF_W), _full((1, HALF_W)),
                  _full(wa.shape), _full(wf.shape), _full(wm.shape), _full(w_out.shape), _full((1, D_MODEL))],
        out_specs=[row(D_MODEL), row(D_MODEL), row(D_MODEL)],
        out_shape=[jax.ShapeDtypeStruct((s, D_MODEL), F32), jax.ShapeDtypeStruct((s, D_MODEL), BF16),
                   jax.ShapeDtypeStruct((s, D_MODEL), BF16)],
        compiler_params=_params(("parallel",)),
    )(x, oa, of, om, proj, b_gate, wa, wf, wm, w_out, g_mlp)


def _mlp_up(hm, w_up):
    s = hm.shape[0]
    tm, tn = min(1024, s), w_up.shape[2]

    def body(h_ref, w_ref, u_ref):
        r = jnp.maximum(_dot(h_ref[...], w_ref[0]), 0.0)
        u_ref[...] = (r * r).astype(BF16)

    return pl.pallas_call(
        body, name="mlp_up", grid=(s // tm, D_FF // tn),
        in_specs=[pl.BlockSpec((tm, D_MODEL), lambda i, j: (i, 0)), pl.BlockSpec((1, D_MODEL, tn), lambda i, j: (j, 0, 0))],
        out_specs=pl.BlockSpec((tm, tn), lambda i, j: (i, j)),
        out_shape=jax.ShapeDtypeStruct((s, D_FF), BF16),
        compiler_params=_params(("parallel", "parallel")),
    )(hm, w_up)


def _mlp_down_loss(u, w_down, x1, target):
    s = u.shape[0]
    tm = min(256, s)

    def body(u_ref, w_ref, x1_ref, t_ref, dy_ref, dyb_ref, loss_ref):
        i = pl.program_id(0)

        @pl.when(i == 0)
        def _():
            loss_ref[...] = jnp.zeros_like(loss_ref)

        y = x1_ref[...] + _dot(u_ref[...], w_ref[...])
        err = y - t_ref[...]
        dy = err * (1.0 / D_MODEL)
        dy_ref[...] = dy
        dyb_ref[...] = dy.astype(BF16)
        part = jnp.sum(jnp.sum(err * err, axis=-1, keepdims=True) * (1.0 / D_MODEL), axis=0, keepdims=True)
        loss_ref[...] += 0.5 * part

    row = pl.BlockSpec((tm, D_MODEL), lambda i: (i, 0))
    return pl.pallas_call(
        body, name="mlp_down_loss", grid=(s // tm,),
        in_specs=[pl.BlockSpec((tm, D_FF), lambda i: (i, 0)), _full(w_down.shape), row, row],
        out_specs=[row, row, _full((1, 1))],
        out_shape=[jax.ShapeDtypeStruct((s, D_MODEL), F32), jax.ShapeDtypeStruct((s, D_MODEL), BF16),
                   jax.ShapeDtypeStruct((1, 1), F32)],
        compiler_params=_params(("arbitrary",)),
    )(u, w_down, x1, target)


def _mlp_bwd_act(dy, w_down, u):
    s = dy.shape[0]
    tm, tn = min(1024, s), 1024

    def body(dy_ref, w_ref, u_ref, da_ref):
        du = _dot(dy_ref[...], w_ref[...], NT)
        da_ref[...] = (du * (2.0 * jnp.sqrt(u_ref[...].astype(F32)))).astype(BF16)

    return pl.pallas_call(
        body, name="mlp_bwd_act", grid=(D_FF // tn, s // tm),
        in_specs=[pl.BlockSpec((tm, D_MODEL), lambda j, i: (i, 0)), pl.BlockSpec((tn, D_MODEL), lambda j, i: (j, 0)),
                  pl.BlockSpec((tm, tn), lambda j, i: (i, j))],
        out_specs=pl.BlockSpec((tm, tn), lambda j, i: (i, j)),
        out_shape=jax.ShapeDtypeStruct((s, D_FF), BF16),
        compiler_params=_params(("parallel", "parallel")),
    )(dy, w_down, u)


def _rms_bwd(xv, g, dh, skip):
    r = lax.rsqrt(jnp.mean(xv * xv, axis=-1, keepdims=True) + EPS)
    n = xv * r
    dn = dh * g
    dx = skip + r * (dn - n * jnp.mean(dn * n, axis=-1, keepdims=True))
    return dx, jnp.sum(dh * n, axis=0, keepdims=True)


def _mlp_bwd_x(da, w_up, x1, dy, g_mlp):
    s = da.shape[0]
    tm = min(256, s)

    def body(da_ref, w_ref, x1_ref, dy_ref, g_ref, dx1_ref, dg_ref):
        i = pl.program_id(0)

        @pl.when(i == 0)
        def _():
            dg_ref[...] = jnp.zeros_like(dg_ref)

        tn = w_ref.shape[2]
        dhm = _dot(da_ref[:, 0:tn], w_ref[0], NT)
        for j in range(1, N_DEV):
            dhm = dhm + _dot(da_ref[:, j * tn:(j + 1) * tn], w_ref[j], NT)
        dx, dg = _rms_bwd(x1_ref[...], g_ref[...], dhm, dy_ref[...])
        dx1_ref[...] = dx
        dg_ref[...] += dg

    row = pl.BlockSpec((tm, D_MODEL), lambda i: (i, 0))
    return pl.pallas_call(
        body, name="mlp_bwd_x", grid=(s // tm,),
        in_specs=[pl.BlockSpec((tm, D_FF), lambda i: (i, 0)), _full(w_up.shape), row, row, _full((1, D_MODEL))],
        out_specs=[row, _full((1, D_MODEL))],
        out_shape=[jax.ShapeDtypeStruct((s, D_MODEL), F32), jax.ShapeDtypeStruct((1, D_MODEL), F32)],
        compiler_params=_params(("arbitrary",)),
    )(da, w_up, x1, dy, g_mlp)


def _merge_bwd(dx1, oa, of, om, proj, b_gate, wa, wf, wm, w_out):
    s = dx1.shape[0]
    tm = min(256, s)

    def body(dx1_ref, oa_ref, of_ref, om_ref, gl_ref, bg_ref, wa_ref, wf_ref, wm_ref, wo_ref,
             dp_ref, doa_ref, dof_ref, dom_ref, dya_ref, dyf_ref, dym_ref, dbg_ref):
        i = pl.program_id(0)

        @pl.when(i == 0)
        def _():
            dbg_ref[...] = jnp.zeros_like(dbg_ref)

        dmerged = _dot(dx1_ref[...].astype(BF16), wo_ref[...], NT)
        branches = ((oa_ref, wa_ref, doa_ref, dya_ref), (of_ref, wf_ref, dof_ref, dyf_ref), (om_ref, wm_ref, dom_ref, dym_ref))
        for b, (o_ref, w_ref, do_ref, dyb_ref) in enumerate(branches):
            cs = slice(b * D_MODEL, (b + 1) * D_MODEL)
            y = _dot(o_ref[...], w_ref[...])
            g = _sigmoid(gl_ref[:, cs].astype(F32) + bg_ref[:, cs])
            dz = (dmerged * y) * g * (1.0 - g)
            dp_ref[:, cs] = dz.astype(BF16)
            dbg_ref[:, cs] += jnp.sum(dz, axis=0, keepdims=True)
            dyb = (dmerged * g).astype(BF16)
            dyb_ref[...] = dyb
            do_ref[...] = _dot(dyb, w_ref[...], NT).astype(BF16)

    row = lambda w: pl.BlockSpec((tm, w), lambda i: (i, 0))
    sd = lambda w: jax.ShapeDtypeStruct((s, w), BF16)
    return pl.pallas_call(
        body, name="merge_bwd", grid=(s // tm,),
        in_specs=[row(D_MODEL), row(512), row(512), row(512), row(HALF_W), _full((1, HALF_W)),
                  _full(wa.shape), _full(wf.shape), _full(wm.shape), _full(w_out.shape)],
        out_specs=[row(HALF_W), row(512), row(512), row(512), row(D_MODEL), row(D_MODEL), row(D_MODEL), _full((1, HALF_W))],
        out_shape=[sd(PROJ_W), sd(512), sd(512), sd(512), sd(D_MODEL), sd(D_MODEL), sd(D_MODEL),
                   jax.ShapeDtypeStruct((1, HALF_W), F32)],
        compiler_params=_params(("arbitrary",)),
    )(dx1, oa, of, om, proj, b_gate, wa, wf, wm, w_out)


def _swa_valid_t(n):
    key = lax.broadcasted_iota(jnp.int32, (2 * SWA_BLOCK, SWA_BLOCK), 0)
    qry = lax.broadcasted_iota(jnp.int32, (2 * SWA_BLOCK, SWA_BLOCK), 1)
    dist = qry + SWA_BLOCK - key
    return (dist >= 0) & (dist < SWA_BLOCK) & ((key >= SWA_BLOCK) | (n > 0))


def _swa_bwd(qa, kp, vp, bias_t, sink, doa):
    s = qa.shape[0]
    nb = s // SWA_BLOCK

    def body(sink_ref, q_ref, kp_ref, vp_ref, bias_ref, do_ref, dq_ref, dkp_ref, dvp_ref, dbias_ref, dsink_ref, sk_acc):
        n = pl.program_id(0)

        @pl.when(n == 0)
        def _():
            dkp_ref[...] = jnp.zeros_like(dkp_ref)
            dvp_ref[...] = jnp.zeros_like(dvp_ref)
            dbias_ref[...] = jnp.zeros_like(dbias_ref)
            sk_acc[...] = jnp.zeros_like(sk_acc)

        start = pl.multiple_of(n * SWA_BLOCK, SWA_BLOCK)
        win = pl.ds(start, 2 * SWA_BLOCK)
        k2 = kp_ref[win, :]
        v2 = vp_ref[win, :]
        valid = _swa_valid_t(n)
        heads = range(SWA_HEADS)
        hs = lambda h: slice(h * HEAD, (h + 1) * HEAD)
        scale = jnp.asarray(0.125, BF16)
        q = [q_ref[:, hs(h)] for h in heads]
        do = [do_ref[:, hs(h)] for h in heads]
        kk = [k2[:, hs(kv)] for kv in range(2)]
        vv = [v2[:, hs(kv)] for kv in range(2)]
        kt = [(kk[kv].astype(F32) * 0.125).T.astype(BF16) for kv in range(2)]
        st = [jnp.where(valid, _dot(kk[h // 4], q[h], NT) * 0.125 + bias_ref[h], NEG) for h in heads]
        dpt = [_dot(vv[h // 4], do[h], NT) for h in heads]
        pt, dst = [], []
        for h in heads:
            sk = sink_ref[h]
            mx = jnp.maximum(jnp.max(st[h], axis=0, keepdims=True), sk)
            p = jnp.exp(st[h] - mx)
            esk = jnp.exp(sk - mx)
            den = jnp.sum(p, axis=0, keepdims=True) + esk
            p = p / den
            delta = jnp.sum(p * dpt[h], axis=0, keepdims=True)
            d = p * (dpt[h] - delta)
            sk_acc[h:h + 1, :] += -(esk / den) * delta
            dbias_ref[h] += d
            pt.append(p.astype(BF16))
            dst.append(d.astype(BF16))
        dq_t = [_dot(kt[h // 4], dst[h]) for h in heads]
        dq_ref[...] = jnp.concatenate(dq_t, axis=0).T
        for kv in range(2):
            group = range(4 * kv, 4 * kv + 4)
            dk = [_dot(dst[h], q[h] * scale) for h in group]
            dv = [_dot(pt[h], do[h]) for h in group]
            dkp_ref[win, hs(kv)] += (dk[0] + dk[1]) + (dk[2] + dk[3])
            dvp_ref[win, hs(kv)] += (dv[0] + dv[1]) + (dv[2] + dv[3])

        @pl.when(n == nb - 1)
        def _():
            dsink_ref[...] = jnp.broadcast_to(jnp.sum(sk_acc[...], axis=1, keepdims=True), dsink_ref.shape)

    return pl.pallas_call(
        body, name="swa_bwd", grid=(nb,),
        in_specs=[pl.BlockSpec(memory_space=pltpu.SMEM),
                  pl.BlockSpec((SWA_BLOCK, 512), lambda n: (n, 0)),
                  _full(kp.shape), _full(vp.shape), _full(bias_t.shape),
                  pl.BlockSpec((SWA_BLOCK, 512), lambda n: (n, 0))],
        out_specs=[pl.BlockSpec((SWA_BLOCK, 512), lambda n: (n, 0)), _full(kp.shape), _full(vp.shape),
                   _full(bias_t.shape), _full((SWA_HEADS, 128))],
        out_shape=[jax.ShapeDtypeStruct((s, 512), F32), jax.ShapeDtypeStruct(kp.shape, F32),
                   jax.ShapeDtypeStruct(vp.shape, F32), jax.ShapeDtypeStruct(bias_t.shape, F32),
                   jax.ShapeDtypeStruct((SWA_HEADS, 128), F32)],
        scratch_shapes=[pltpu.VMEM((SWA_HEADS, 128), F32)],
        compiler_params=_params(("arbitrary",)),
    )(sink, qa, kp, vp, bias_t, doa)


def _fox_bwd(q, k, v, do, o, cc4, lse4):
    s = q.shape[0]
    t = min(FOX_BWD_TK, s)
    tq = min(FOX_BWD_TQ, s)
    nq = s // t
    nqt = s // tq

    def body(q_ref, k_ref, v_ref, do_ref, o_ref, cc_ref, lse_ref,
             dqt_ref, dk_ref, dv_ref, dck_ref, dcq_ref, delta_ref, dk0, dk1, dv0, dv1, ds0, ds1):
        j = pl.program_id(1)

        @pl.when(j == 0)
        def _():
            dqt_ref[...] = jnp.zeros_like(dqt_ref)
            dcq_ref[...] = jnp.zeros_like(dcq_ref)
            lane8 = lax.broadcasted_iota(jnp.int32, (8, 128), 1)
            row8 = lax.broadcasted_iota(jnp.int32, (8, 128), 0)
            sel = jnp.where((lane8 // HEAD) == row8, 1.0, 0.0).astype(F32)

            def dl(i, c):
                rows = pl.ds(pl.multiple_of(i * tq, tq), tq)
                pr = do_ref[rows, :].astype(F32) * o_ref[rows, :].astype(F32)
                delta_ref[:, rows] = lax.dot_general(sel, pr, NT, precision=lax.Precision.HIGHEST,
                                                     preferred_element_type=F32)
                return c

            lax.fori_loop(0, nqt, dl, 0)

        kj = k_ref[...]
        vj = v_ref[...]
        ks = pl.ds(pl.multiple_of(j * t, t), t)
        kt = (kj.astype(F32) * 0.125).T.astype(BF16)
        ke = [jnp.where(_head_mask(e), kj, jnp.zeros_like(kj)) for e in range(2)]
        ve = [jnp.where(_head_mask(e), vj, jnp.zeros_like(vj)) for e in range(2)]
        kte = [jnp.where(_head_rows(e), kt, jnp.zeros_like(kt)) for e in range(2)]
        ck = [cc_ref[0, ks, e:e + 1] for e in range(2)]
        accs = ((dk0, dv0, ds0), (dk1, dv1, ds1))
        for refs in accs:
            for r in refs:
                r[...] = jnp.zeros_like(r)
        i_first = (j * t) // tq
        krow = lax.broadcasted_iota(jnp.int32, (t, tq), 0) + j * t
        qcol = lax.broadcasted_iota(jnp.int32, (t, tq), 1) + i_first * tq

        def step(i, c, masked):
            rows = pl.ds(pl.multiple_of(i * tq, tq), tq)
            qs = q_ref[rows, :] * jnp.asarray(0.125, BF16)
            doi = do_ref[rows, :]
            for e in range(2):
                dk_acc, dv_acc, ds_acc = accs[e]
                st = _dot(ke[e], qs, NT) - ck[e]
                if masked:
                    st = jnp.where(krow <= qcol, st, NEG)
                pt = jnp.exp(st - lse_ref[0, e:e + 1, rows])
                dpt = _dot(ve[e], doi, NT)
                dst = pt * (dpt - delta_ref[e:e + 1, rows])
                dsb = dst.astype(BF16)
                dv_acc[...] += _dot(pt.astype(BF16), doi)
                dk_acc[...] += _dot(dsb, qs)
                dqt_ref[:, rows] += _dot(kte[e], dsb)
                ds_acc[...] += dst
                dcq_ref[0, e:e + 1, rows] += jnp.sum(dst, axis=0, keepdims=True)
            return c

        step(i_first, 0, True)
        lax.fori_loop(i_first + 1, nqt, functools.partial(step, masked=False), 0)
        m0 = _head_mask(0)
        dk_ref[...] = jnp.where(m0, dk0[...], dk1[...])
        dv_ref[...] = jnp.where(m0, dv0[...], dv1[...])
        lane = lax.broadcasted_iota(jnp.int32, (t, 128), 1)
        c0 = jnp.sum(ds0[...], axis=-1, keepdims=True)
        c1 = jnp.sum(ds1[...], axis=-1, keepdims=True)
        dck_ref[0] = jnp.where(lane == 0, c0, jnp.where(lane == 1, c1, 0.0))

    res = lambda: pl.BlockSpec((s, 128), lambda hp, j: (0, hp))
    blk = lambda: pl.BlockSpec((t, 128), lambda hp, j: (j, hp))
    return pl.pallas_call(
        body, name="fox_bwd", grid=(4, nq),
        in_specs=[res(), blk(), blk(), res(), res(), pl.BlockSpec((1, s, 128), lambda hp, j: (hp, 0, 0)),
                  pl.BlockSpec((1, 8, s), lambda hp, j: (hp, 0, 0))],
        out_specs=[pl.BlockSpec((128, s), lambda hp, j: (hp, 0)), blk(), blk(),
                   pl.BlockSpec((1, t, 128), lambda hp, j: (hp, j, 0)),
                   pl.BlockSpec((1, 8, s), lambda hp, j: (hp, 0, 0))],
        out_shape=[jax.ShapeDtypeStruct((512, s), F32), jax.ShapeDtypeStruct((s, 512), F32),
                   jax.ShapeDtypeStruct((s, 512), F32), jax.ShapeDtypeStruct((4, s, 128), F32),
                   jax.ShapeDtypeStruct((4, 8, s), F32)],
        scratch_shapes=[pltpu.VMEM((8, s), F32)] + [pltpu.VMEM((t, 128), F32)] * 4 + [pltpu.VMEM((t, tq), F32)] * 2,
        compiler_params=_params(("arbitrary", "arbitrary")),
    )(q, k, v, do, o, cc4, lse4)


def _mem_bwd(qm, mk, mv, dom):
    s = qm.shape[0]
    tq = min(512, s)

    def body(q_ref, mk_ref, mv_ref, do_ref, dq_ref, dmk_ref, dmv_ref):
        i = pl.program_id(0)

        @pl.when(i == 0)
        def _():
            dmk_ref[...] = jnp.zeros_like(dmk_ref)
            dmv_ref[...] = jnp.zeros_like(dmv_ref)

        for h in range(MEM_HEADS):
            hs = slice(h * 128, (h + 1) * 128)
            qh = q_ref[:, hs]
            doh = do_ref[:, hs]
            sc = _dot(qh, mk_ref[:, hs], NT) * MEM_SCALE
            mx = jnp.max(sc, axis=-1, keepdims=True)
            p = jnp.exp(sc - mx)
            p = p / jnp.sum(p, axis=-1, keepdims=True)
            dp = _dot(doh, mv_ref[:, hs], NT)
            ds = p * (dp - jnp.sum(p * dp, axis=-1, keepdims=True))
            dsb = (ds * MEM_SCALE).astype(BF16)
            dq_ref[:, hs] = _dot(dsb, mk_ref[:, hs])
            dmk_ref[:, hs] += _dot(dsb, qh, TN)
            dmv_ref[:, hs] += _dot(p.astype(BF16), doh, TN)

    return pl.pallas_call(
        body, name="mem_bwd", grid=(s // tq,),
        in_specs=[pl.BlockSpec((tq, 512), lambda i: (i, 0)), _full(mk.shape), _full(mv.shape),
                  pl.BlockSpec((tq, 512), lambda i: (i, 0))],
        out_specs=[pl.BlockSpec((tq, 512), lambda i: (i, 0)), _full(mk.shape), _full(mv.shape)],
        out_shape=[jax.ShapeDtypeStruct((s, 512), F32), jax.ShapeDtypeStruct(mk.shape, F32),
                   jax.ShapeDtypeStruct(mv.shape, F32)],
        compiler_params=_params(("arbitrary",)),
    )(qm, mk, mv, dom)


def _memkv_bwd(dmk, dmv, kv_raw, kn_mem, mem, g_mem, mem_n, w_kv):
    def body(dmk_ref, dmv_ref, kv_ref, kn_ref, mem_ref, g_ref, mn_ref, w_ref, dw_ref, dkn_ref, dg_ref, dkv_ref):
        dkn = jnp.zeros((1, 128), F32)
        for h in range(MEM_HEADS):
            hs = slice(h * 128, (h + 1) * 128)
            v = kv_ref[:, hs]
            r = lax.rsqrt(jnp.mean(v * v, axis=-1, keepdims=True) + EPS)
            n = v * r
            dn = dmk_ref[:, hs]
            dkn = dkn + jnp.sum(dn * n, axis=0, keepdims=True)
            dng = dn * kn_ref[...]
            dkv_ref[:, hs] = (r * (dng - n * jnp.mean(dng * n, axis=-1, keepdims=True))).astype(BF16)
        dkv_ref[:, 512:1024] = dmv_ref[...].astype(BF16)
        dkn_ref[...] = dkn
        dkv = dkv_ref[...]
        dw_ref[...] = _dot(mn_ref[...], dkv, TN).astype(BF16)
        dmn = _dot(dkv, w_ref[...], NT)
        xv = mem_ref[...]
        r = lax.rsqrt(jnp.mean(xv * xv, axis=-1, keepdims=True) + EPS)
        dg_ref[...] = jnp.sum(dmn * (xv * r), axis=0, keepdims=True)

    m = mem.shape[0]
    return pl.pallas_call(
        body, name="memkv_bwd",
        out_shape=[jax.ShapeDtypeStruct((D_MODEL, 1024), BF16), jax.ShapeDtypeStruct((1, 128), F32),
                   jax.ShapeDtypeStruct((1, D_MODEL), F32)],
        scratch_shapes=[pltpu.VMEM((m, 1024), BF16)],
        compiler_params=pltpu.CompilerParams(vmem_limit_bytes=VMEM_LIMIT),
    )(dmk, dmv, kv_raw, kn_mem, mem, g_mem, mem_n, w_kv)


def _fox_gate_bwd(dc, proj, b_forget128):
    s = dc.shape[0]
    tm = min(512, s)
    nt = s // tm

    def body(dc_ref, p_ref, b_ref, dfl_ref, db_ref, carry_ref):
        i = pl.program_id(0)

        @pl.when(i == 0)
        def _():
            carry_ref[...] = jnp.zeros_like(carry_ref)
            db_ref[...] = jnp.zeros_like(db_ref)

        dcv = dc_ref[...]
        dlogf = jnp.dot(_tri(tm, False), dcv, precision=lax.Precision.HIGHEST, preferred_element_type=F32) + carry_ref[...]
        carry_ref[...] += jnp.sum(dcv, axis=0, keepdims=True)
        z = p_ref[...] + b_ref[...]
        dfl = dlogf * (1.0 / (1.0 + jnp.exp(z)))
        dfl_ref[...] = dfl.astype(BF16)
        db_ref[...] += jnp.sum(dfl, axis=0, keepdims=True)

    return pl.pallas_call(
        body, name="fox_gate_bwd", grid=(nt,),
        in_specs=[pl.BlockSpec((tm, 128), lambda i: (nt - 1 - i, 0)),
                  pl.BlockSpec((tm, 128), lambda i: (nt - 1 - i, 0)), _full((1, 128))],
        out_specs=[pl.BlockSpec((tm, 128), lambda i: (nt - 1 - i, 0)), _full((1, 128))],
        out_shape=[jax.ShapeDtypeStruct((s, 128), BF16), jax.ShapeDtypeStruct((1, 128), F32)],
        scratch_shapes=[pltpu.VMEM((1, 128), F32)],
        compiler_params=_params(("arbitrary",)),
    )(dc, proj, b_forget128)


def _proj_pre_bwd(dproj, proj, dqf, dkf, dvf, dqm, dqa, dka, dva, dfl, gq_fox, gk_fox, gq_mem, gq_swa, gk_swa):
    s = proj.shape[0]
    tm = min(256, s)

    def body(dp_in, p_ref, dqf_ref, dkf_ref, dvf_ref, dqm_ref, dqa_ref, dka_ref, dva_ref, dfl_ref,
             gqf, gkf, gqm, gqa, gka, dp_ref, dgn_ref):
        i = pl.program_id(0)

        @pl.when(i == 0)
        def _():
            dgn_ref[...] = jnp.zeros_like(dgn_ref)

        def norm_bwd(off, width, hd, g_ref, dn_ref, slot):
            acc = jnp.zeros((1, 128), F32)
            for b in range(width // 128):
                v = p_ref[:, off + b * 128: off + (b + 1) * 128].astype(F32)
                r = lax.rsqrt(_group_mean(v * v, hd) + EPS)
                n = v * r
                dn = dn_ref[:, b * 128:(b + 1) * 128]
                acc = acc + jnp.sum(dn * n, axis=0, keepdims=True)
                dng = dn * g_ref[...]
                dp_ref[:, off + b * 128: off + (b + 1) * 128] = (r * (dng - n * _group_mean(dng * n, hd))).astype(BF16)
            dgn_ref[slot:slot + 1, :] += acc

        norm_bwd(H_QF, 512, HEAD, gqf, dqf_ref, 0)
        norm_bwd(H_KF, 512, HEAD, gkf, dkf_ref, 1)
        dp_ref[:, H_VF:H_VF + 512] = dvf_ref[...].astype(BF16)
        norm_bwd(H_QM, 512, MEM_HEAD, gqm, dqm_ref, 2)
        norm_bwd(H_QA, 512, HEAD, gqa, dqa_ref, 3)
        norm_bwd(H_KA, 128, HEAD, gka, dka_ref, 4)
        dp_ref[:, H_VA:H_VA + 128] = dva_ref[...].astype(BF16)
        dp_ref[:, H_FL:H_FL + 128] = dfl_ref[...]
        dp_ref[:, H_FL + 128:HALF_W] = jnp.zeros((tm, HALF_W - H_FL - 128), BF16)

    row = lambda w: pl.BlockSpec((tm, w), lambda i: (i, 0))
    g_spec = _full((1, 128))
    return pl.pallas_call(
        body, name="proj_pre_bwd", grid=(s // tm,),
        in_specs=[pl.BlockSpec(memory_space=pl.ANY), pl.BlockSpec((tm, HALF_W), lambda i: (i, 1)),
                  row(512), row(512), row(512), row(512), row(512), row(128), row(128), row(128),
                  g_spec, g_spec, g_spec, g_spec, g_spec],
        out_specs=[pl.BlockSpec((tm, HALF_W), lambda i: (i, 1)), _full((8, 128))],
        out_shape=[jax.ShapeDtypeStruct((s, PROJ_W), BF16), jax.ShapeDtypeStruct((8, 128), F32)],
        input_output_aliases={0: 0},
        compiler_params=_params(("arbitrary",)),
    )(dproj, proj, dqf, dkf, dvf, dqm, dqa, dka, dva, dfl, gq_fox, gk_fox, gq_mem, gq_swa, gk_swa)


def _in_bwd_x(dproj, w_in_p, x, g_mix, dx1):
    s = x.shape[0]
    tm = min(256, s)

    def body(dp_ref, w_ref, x_ref, g_ref, dx1_ref, gx_ref, dg_ref):
        i = pl.program_id(0)

        @pl.when(i == 0)
        def _():
            dg_ref[...] = jnp.zeros_like(dg_ref)

        dx, dg = _rms_bwd(x_ref[...], g_ref[...], _dot(dp_ref[...], w_ref[...], NT), dx1_ref[...])
        gx_ref[...] = dx
        dg_ref[...] += dg

    row = pl.BlockSpec((tm, D_MODEL), lambda i: (i, 0))
    return pl.pallas_call(
        body, name="in_bwd_x", grid=(s // tm,),
        in_specs=[pl.BlockSpec((tm, PROJ_W), lambda i: (i, 0)), _full(w_in_p.shape), row, _full((1, D_MODEL)), row],
        out_specs=[row, _full((1, D_MODEL))],
        out_shape=[jax.ShapeDtypeStruct((s, D_MODEL), F32), jax.ShapeDtypeStruct((1, D_MODEL), F32)],
        compiler_params=_params(("arbitrary",)),
    )(dproj, w_in_p, x, g_mix, dx1)


def _rel_bias_bwd(dbias, bucket):
    def body(db_ref, bk_ref, o_ref):
        bk = bk_ref[...]
        lane = lax.broadcasted_iota(jnp.int32, (1, 128), 1)
        for b in range(REL_BUCKETS):
            sel = bk == b
            acc = jnp.zeros((1, 128), F32)
            for h in range(SWA_HEADS):
                tot = jnp.sum(jnp.sum(jnp.where(sel, db_ref[h], 0.0), axis=-1, keepdims=True), axis=0, keepdims=True)
                acc = jnp.where(lane == h, tot, acc)
            o_ref[b:b + 1, :] = acc

    return pl.pallas_call(
        body, name="rel_bias_bwd",
        out_shape=jax.ShapeDtypeStruct((REL_BUCKETS, 128), F32),
        compiler_params=pltpu.CompilerParams(vmem_limit_bytes=VMEM_LIMIT),
    )(dbias, bucket)


def _my_place():
    return lax.axis_index("x"), lax.axis_index("y"), lax.axis_index("c")


def _peer(place, k):
    x, y, c = place
    return (1 - x if k & 4 else x, 1 - y if k & 2 else y, 1 - c if k & 1 else c)


def _index(place):
    x, y, c = place
    return 4 * x + 2 * y + c


HBM_SPEC = pl.BlockSpec(memory_space=pltpu.HBM)
SEM_SPEC = pl.BlockSpec(memory_space=pltpu.SEMAPHORE)
DATAFLOW = pltpu.SideEffectType.DATAFLOW_SIDE_EFFECTING


ALL_PEERS = tuple(range(1, N_DEV))
SAME_CORE = (2, 4, 6)
OWN = N_DEV - 1


def _split_copy(src_ref, land_ref, send_sems, recv_sems, me, k, gather):
    peer = _peer(me, k)
    if gather:
        src, dst = src_ref, land_ref.at[_index(me)]
    else:
        src, dst = src_ref.at[_index(peer)], land_ref.at[k - 1]
    return pltpu.make_async_remote_copy(src_ref=src, dst_ref=dst, send_sem=send_sems.at[k - 1], recv_sem=recv_sems.at[k - 1],
                                        device_id=peer, device_id_type=MESH)


def _own_copy(src_ref, land_ref, recv_sems, me, gather):
    if gather:
        src, dst = src_ref, land_ref.at[_index(me)]
    else:
        src, dst = src_ref.at[_index(me)], land_ref.at[OWN]
    return pltpu.make_async_copy(src, dst, recv_sems.at[OWN])


def _split_start(srcs, gather, name, peers=ALL_PEERS, after=None):
    n = len(srcs)
    extra = [] if after is None else [after]

    def body(*refs):
        refs = refs[:2 * n] + refs[2 * n + len(extra):]
        src_refs, land_refs = refs[:n], refs[n:2 * n]
        send_sems, recv_sems, token = refs[2 * n:3 * n], refs[3 * n:4 * n], refs[-1]
        me = _my_place()
        for w in range(n):
            for k in peers:
                _split_copy(src_refs[w], land_refs[w], send_sems[w], recv_sems[w], me, k, gather).start()
            _own_copy(src_refs[w], land_refs[w], recv_sems[w], me, gather).start()
        token[...] = jnp.zeros_like(token)

    lands = [lax.empty((N_DEV,) + (a.shape if gather else a.shape[1:]), a.dtype) for a in srcs]
    sems = [pltpu.SemaphoreType.DMA((N_DEV,))] * (2 * n)
    hbm = [pltpu.HBM(a.shape, a.dtype) for a in list(srcs) + lands]
    outs = pl.pallas_call(
        body, name=name,
        out_shape=(*sems, *hbm, jax.ShapeDtypeStruct((8, 128), F32)),
        in_specs=(HBM_SPEC,) * (2 * n) + (pl.BlockSpec(memory_space=pl.ANY),) * len(extra),
        out_specs=(SEM_SPEC,) * (2 * n) + (HBM_SPEC,) * (2 * n) + (pl.BlockSpec(memory_space=pltpu.VMEM),),
        input_output_aliases={i: 2 * n + i for i in range(2 * n)},
        compiler_params=pltpu.CompilerParams(has_side_effects=DATAFLOW),
    )(*[pltpu.with_memory_space_constraint(a, pltpu.HBM) for a in list(srcs) + lands], *extra)
    return list(outs[:n]), list(outs[n:2 * n]), list(outs[2 * n:3 * n]), list(outs[3 * n:4 * n]), outs[-1]


def _split_wait(started, w, after, gather, name):
    send_sems, recv_sems, srcs, lands, _ = started

    def body(src_ref, land_ref, send_sems, recv_sems, after_ref, src_out, land_out):
        me = _my_place()
        for k in ALL_PEERS:
            cp = _split_copy(src_ref, land_ref, send_sems, recv_sems, me, k, gather)
            cp.wait_send()
            cp.wait_recv()
        _own_copy(src_ref, land_ref, recv_sems, me, gather).wait()

    return pl.pallas_call(
        body, name=name,
        out_shape=(pltpu.HBM(srcs[w].shape, srcs[w].dtype), pltpu.HBM(lands[w].shape, lands[w].dtype)),
        in_specs=(HBM_SPEC, HBM_SPEC, SEM_SPEC, SEM_SPEC, pl.BlockSpec(memory_space=pl.ANY)),
        out_specs=(HBM_SPEC, HBM_SPEC), input_output_aliases={0: 0, 1: 1},
        compiler_params=pltpu.CompilerParams(has_side_effects=DATAFLOW),
    )(srcs[w], lands[w], send_sems[w], recv_sems[w], after)[1]


def _forward_copy(land_ref, send_sems, recv_sems, me, j, incoming):
    sibling = _peer(me, 1)
    rows = land_ref.at[_index(_peer(sibling if incoming else me, SAME_CORE[j]))]
    return pltpu.make_async_remote_copy(src_ref=rows, dst_ref=rows, send_sem=send_sems.at[j], recv_sem=recv_sems.at[j],
                                        device_id=sibling, device_id_type=MESH)


def _forward_start(started, after, name):
    send_a, recv_a, srcs, lands, _ = started

    def body(src_ref, land_ref, send_a, recv_a, after_ref, send_b, recv_b, src_out, land_out):
        me = _my_place()
        for j, k in enumerate(SAME_CORE):
            _split_copy(src_ref, land_ref, send_a, recv_a, me, k, True).wait_recv()
            _forward_copy(land_ref, send_b, recv_b, me, j, False).start()

    sems = pltpu.SemaphoreType.DMA((len(SAME_CORE),))
    return pl.pallas_call(
        body, name=name,
        out_shape=(sems, sems, pltpu.HBM(srcs[0].shape, srcs[0].dtype), pltpu.HBM(lands[0].shape, lands[0].dtype)),
        in_specs=(HBM_SPEC, HBM_SPEC, SEM_SPEC, SEM_SPEC, pl.BlockSpec(memory_space=pl.ANY)),
        out_specs=(SEM_SPEC, SEM_SPEC, HBM_SPEC, HBM_SPEC), input_output_aliases={0: 2, 1: 3},
        compiler_params=pltpu.CompilerParams(has_side_effects=DATAFLOW),
    )(srcs[0], lands[0], send_a[0], recv_a[0], after)


def _forward_wait(started, forwarded, name):
    send_a, recv_a, _, _, _ = started
    send_b, recv_b, src, land = forwarded

    def body(src_ref, land_ref, send_a, recv_a, send_b, recv_b, src_out, land_out):
        me = _my_place()
        _own_copy(src_ref, land_ref, recv_a, me, True).wait()
        for k in (1,) + SAME_CORE:
            _split_copy(src_ref, land_ref, send_a, recv_a, me, k, True).wait_send()
        _split_copy(src_ref, land_ref, send_a, recv_a, me, 1, True).wait_recv()
        for j in range(len(SAME_CORE)):
            _forward_copy(land_ref, send_b, recv_b, me, j, False).wait_send()
            _forward_copy(land_ref, send_b, recv_b, me, j, True).wait_recv()

    return pl.pallas_call(
        body, name=name,
        out_shape=(pltpu.HBM(src.shape, src.dtype), pltpu.HBM(land.shape, land.dtype)),
        in_specs=(HBM_SPEC, HBM_SPEC, SEM_SPEC, SEM_SPEC, SEM_SPEC, SEM_SPEC),
        out_specs=(HBM_SPEC, HBM_SPEC), input_output_aliases={0: 0, 1: 1},
        compiler_params=pltpu.CompilerParams(has_side_effects=DATAFLOW),
    )(src, land, send_a[0], recv_a[0], send_b, recv_b)[1]


def _adam_math(w, g, m, v):
    m2 = ADAM_B1 * m + (1.0 - ADAM_B1) * g
    v2 = ADAM_B2 * v + (1.0 - ADAM_B2) * (g * g)
    m_hat = m2 / (1.0 - ADAM_B1 ** ADAM_STEP)
    v_hat = v2 / (1.0 - ADAM_B2 ** ADAM_STEP)
    delta = -ADAM_LR * (m_hat / (jnp.sqrt(v_hat) + ADAM_EPS) + ADAM_WD * w)
    return delta, m2, v2


def _adamw(land, w, m, v, name):
    a, b = w.shape
    bp = land.shape[2]
    ta = min(128, a)

    def body(p_ref, w_ref, m_ref, v_ref, g_ref, d_ref, m2_ref, v2_ref):
        g = p_ref[0, :, 0:b].astype(F32)
        for k in range(1, N_DEV):
            g = g + p_ref[k, :, 0:b].astype(F32)
        delta, m2, v2 = _adam_math(w_ref[...], g, m_ref[...], v_ref[...])
        g_ref[...] = g
        d_ref[...] = delta
        m2_ref[...] = m2
        v2_ref[...] = v2

    blk = pl.BlockSpec((ta, b), lambda i: (i, 0))
    sd = jax.ShapeDtypeStruct((a, b), F32)
    return pl.pallas_call(
        body, name=name, grid=(a // ta,),
        in_specs=[pl.BlockSpec((N_DEV, ta, bp), lambda i: (0, i, 0)), blk, blk, blk],
        out_specs=[blk, blk, blk, blk], out_shape=[sd, sd, sd, sd],
        compiler_params=_params(("parallel",)),
    )(land, w, m, v)


def _bucket_table():
    t_loc = jnp.arange(SWA_BLOCK)[:, None] + SWA_BLOCK
    s_loc = jnp.arange(2 * SWA_BLOCK)[None, :]
    dist = t_loc - s_loc
    max_exact = REL_BUCKETS // 2
    d = jnp.maximum(dist, 0)
    df = jnp.maximum(d, 1).astype(F32)
    large = max_exact + (jnp.log(df / max_exact) / math.log(REL_MAX_DIST / max_exact) * (REL_BUCKETS - max_exact)).astype(jnp.int32)
    large = jnp.minimum(large, REL_BUCKETS - 1)
    bucket = jnp.where(d < max_exact, d, large)
    band = (dist >= 0) & (dist < SWA_BLOCK)
    return bucket, band


def _tile2(g):
    return jnp.concatenate([g, g], axis=1) if g.shape[1] == HEAD else g


SHARD_W = 737
SHARD_WP = 768
IN_WIDTH = N_DEV * SHARD_W
SEGMENTS = ((GL0, 2824, 3072), (QF0, 768, 512), (KF0, 1280, 512), (VF0, 1792, 512), (QM0, 2312, 512),
            (QA0, 0, 512), (KA0, 512, 128), (VA0, 640, 128), (FL0, 2304, 8))


def _lane_plan(sources):
    plan = []
    for t in range(len(sources) // 128):
        groups = {}
        for lane in range(128):
            src = sources[128 * t + lane]
            if src is not None:
                slab, col = src
                groups.setdefault((slab, col // 128, (lane - col) % 128), []).append(lane)
        tile = []
        for key, lanes in groups.items():
            assert lanes == list(range(lanes[0], lanes[-1] + 1))
            tile.append((key, lanes[0], lanes[-1] + 1))
        plan.append(tile)
    return plan


def _assemble(tile_plan, load, rows):
    lane = lax.broadcasted_iota(jnp.int32, (1, 128), 1)
    out = jnp.zeros((rows, 128), F32)
    for (slab, st, roll), lo, hi in tile_plan:
        v = load(slab, st)
        if roll:
            v = pltpu.roll(v, roll, 1)
        out = v if (lo, hi) == (0, 128) else jnp.where((lane >= lo) & (lane < hi), v, out)
    return out


def _w_in_from_shards(land):
    ref_col = [None] * PROJ_W
    for p0, r0, n in SEGMENTS:
        for i in range(n):
            ref_col[p0 + i] = divmod(r0 + i, SHARD_W)
    plan = _lane_plan(ref_col)
    d_model = land.shape[1]
    tm = 256

    def body(land_ref, o_ref):
        load = lambda slab, st: land_ref[slab, :, st * 128:(st + 1) * 128].astype(F32)
        for t, tile_plan in enumerate(plan):
            o_ref[:, t * 128:(t + 1) * 128] = _assemble(tile_plan, load, tm).astype(BF16)

    return pl.pallas_call(
        body, name="w_in_from_shards", grid=(d_model // tm,),
        in_specs=[pl.BlockSpec((N_DEV, tm, SHARD_WP), lambda i: (0, i, 0))],
        out_specs=pl.BlockSpec((tm, PROJ_W), lambda i: (i, 0)),
        out_shape=jax.ShapeDtypeStruct((d_model, PROJ_W), BF16),
        compiler_params=_params(("parallel",)),
    )(land)


def _dw_in_to_parts(dwp):
    padded_col = [None] * IN_WIDTH
    for p0, r0, n in SEGMENTS:
        for i in range(n):
            padded_col[r0 + i] = p0 + i
    sources = []
    for d in range(N_DEV):
        sources += [(0, padded_col[SHARD_W * d + c]) if c < SHARD_W else None for c in range(SHARD_WP)]
    plan = _lane_plan(sources)
    d_model = dwp.shape[0]
    tm = 256
    tiles = SHARD_WP // 128

    def body(dw_ref, o_ref):
        load = lambda slab, st: dw_ref[:, st * 128:(st + 1) * 128].astype(F32)
        for t, tile_plan in enumerate(plan):
            d, c = divmod(t, tiles)
            o_ref[d, :, c * 128:(c + 1) * 128] = _assemble(tile_plan, load, tm).astype(BF16)

    return pl.pallas_call(
        body, name="dw_in_to_parts", grid=(d_model // tm,),
        in_specs=[pl.BlockSpec((tm, PROJ_W), lambda i: (i, 0))],
        out_specs=pl.BlockSpec((N_DEV, tm, SHARD_WP), lambda i: (0, i, 0)),
        out_shape=jax.ShapeDtypeStruct((N_DEV, d_model, SHARD_WP), BF16),
        compiler_params=_params(("parallel",)),
    )(dwp)


def _cast_shards(shards):
    names = list(shards)

    def body(*refs):
        for src, dst in zip(refs[:len(names)], refs[len(names):]):
            if dst.shape != src.shape:
                dst[...] = jnp.zeros(dst.shape, BF16)
                dst[:, 0:src.shape[1]] = src[...].astype(BF16)
            else:
                dst[...] = src[...].astype(BF16)

    out_shape = [jax.ShapeDtypeStruct((shards[n].shape[0], SHARD_WP if n == "w_in" else shards[n].shape[1]), BF16)
                 for n in names]
    outs = pl.pallas_call(body, name="cast_shards", out_shape=out_shape,
                          compiler_params=pltpu.CompilerParams(vmem_limit_bytes=VMEM_LIMIT))(*[shards[n] for n in names])
    return dict(zip(names, outs))


def _tie(x, *tokens):
    for t in tokens:
        if t is not None:
            x = x + t[0:1, 0:1]
    return x


def _local_step(x, mem, target, p, getw, emit, deps=()):
    s = x.shape[0]
    bucket, band = _bucket_table()
    bucket_m = jnp.where(band, bucket, -1).astype(jnp.int32)
    bias = _bias_table(p["rel_bias"], bucket_m)
    bucket_t = jnp.transpose(bucket_m)
    bias_t = _bias_table(p["rel_bias"], bucket_t)
    gqf, gkf, gqa, gka = _tile2(p["qn_fox"]), _tile2(p["kn_fox"]), _tile2(p["qn_swa"]), _tile2(p["kn_swa"])
    gqm = p["qn_mem"]
    bf128 = jnp.pad(p["b_forget"], ((0, 0), (0, 120)))
    sink = p["sink_swa"].reshape(8)

    h = _rms_fwd(x, p["g_mix"], "rms_mix", deps)
    w_in = getw("w_in", h)
    proj = _mm(h, w_in, "nn", BF16, 512, 1536, 1024, "proj")
    fl = _mm(h, w_in[:, FL0:FL0 + 128], "nn", F32, 512, 128, 1024, "proj_fl")
    qf, kf, vf, qm, qa, ka, va = _proj_post(proj, gqf, gkf, gqm, gqa, gka)
    cc4 = _fox_gate_fwd(fl, bf128)
    w_kv = getw("w_mem_kv", cc4)
    mem_n, kv_raw, mk, mv = _memkv_fwd(mem, p["g_mem"], w_kv, p["kn_mem"])
    kp = jnp.pad(ka, ((SWA_BLOCK, 0), (0, 0)))
    vp = jnp.pad(va, ((SWA_BLOCK, 0), (0, 0)))
    oa = _swa_fwd(qa, kp, vp, bias, sink)
    of, lse4 = _fox_fwd(qf, kf, jnp.transpose(vf), cc4)
    om = _mem_fwd(qm, mk, mv)
    wa, wf, wm, w_out = getw("w_o_swa", oa), getw("w_o_fox", oa), getw("w_o_mem", oa), getw("w_out", oa)
    x1, hm, merged = _merge_fwd(x, oa, of, om, proj, p["b_gate"], wa, wf, wm, w_out, p["g_mlp"])
    w_up = getw("w_mlp_up", of)
    u = _mlp_up(hm, w_up)
    w_down = getw("w_mlp_down", hm)
    dy, dy_b, loss = _mlp_down_loss(u, w_down, x1, target)

    da = _mlp_bwd_act(dy_b, w_down, u)
    t_down = emit({"w_mlp_down": _mm(u, dy_b, "tn", BF16, 1024, 1024, 512, "dw_down")})
    dx1, dg_mlp = _mlp_bwd_x(da, w_up, x1, dy, _tie(p["g_mlp"], t_down))
    t_up = emit({"w_mlp_up": _mm(hm, da, "tn", BF16, 1024, 1024, 512, "dw_up", column_chunks=True)})
    dproj, doa, dof, dom, dya, dyf, dym, db_gate = _merge_bwd(
        dx1, oa, of, om, proj, _tie(p["b_gate"], t_up), wa, wf, wm, w_out)
    t_o = emit({"w_out": _mm(merged, dx1, "tn", BF16, 512, 1024, 512, "dw_out"),
                "w_o_swa": _mm(oa, dya, "tn", BF16, 512, 1024, 512, "dw_o_swa"),
                "w_o_fox": _mm(of, dyf, "tn", BF16, 512, 1024, 512, "dw_o_fox"),
                "w_o_mem": _mm(om, dym, "tn", BF16, 512, 1024, 512, "dw_o_mem")})

    dqm, dmk, dmv = _mem_bwd(qm, mk, mv, dom)
    dw_kv, dkn_mem, dg_mem = _memkv_bwd(dmk, dmv, kv_raw, _tie(p["kn_mem"], t_o), mem, p["g_mem"], mem_n, w_kv)
    t_kv = emit({"w_mem_kv": dw_kv})
    dqa, dkp, dvp, dbias, dsink = _swa_bwd(qa, kp, vp, bias_t, _tie(p["sink_swa"], t_kv).reshape(8), doa)
    dqf_t, dkf, dvf, dck4, dcq4 = _fox_bwd(qf, kf, vf, dof, of, cc4, lse4)
    dqf = jnp.transpose(dqf_t)

    dcq = jnp.transpose(dcq4[:, 0:2, :], (2, 0, 1)).reshape(s, 8)
    dck = jnp.transpose(dck4[:, :, 0:2], (1, 0, 2)).reshape(s, 8)
    dc = jnp.pad(dcq - dck, ((0, 0), (0, 120)))
    dfl, db_forget = _fox_gate_bwd(dc, fl, bf128)

    dproj, dgn = _proj_pre_bwd(dproj, proj, dqf, dkf, dvf, dqm, dqa, dkp[SWA_BLOCK:], dvp[SWA_BLOCK:], dfl,
                               gqf, gkf, gqm, gqa, gka)
    t_in = emit({"w_in": _mm(h, dproj, "tn", BF16, 1024, 3072, 512, "dw_in")})
    grad_x, dg_mix = _in_bwd_x(dproj, w_in, x, _tie(p["g_mix"], t_in), dx1)
    d_rel = _rel_bias_bwd(dbias, bucket_t)

    fold = lambda r: dgn[r:r + 1, 0:HEAD] + dgn[r:r + 1, HEAD:128]
    small = {
        "g_mix": dg_mix, "b_gate": db_gate, "b_forget": db_forget[:, 0:8],
        "qn_swa": fold(3), "kn_swa": fold(4), "sink_swa": dsink[:, 0].reshape(1, 8), "rel_bias": d_rel[:, 0:8],
        "qn_fox": fold(0), "kn_fox": fold(1), "g_mem": dg_mem, "qn_mem": dgn[2:3, :], "kn_mem": dkn_mem,
        "g_mlp": dg_mlp,
    }
    return loss, grad_x, small


SMALL = ("g_mix", "b_gate", "b_forget", "qn_swa", "kn_swa", "sink_swa", "rel_bias", "qn_fox", "kn_fox", "g_mem",
         "qn_mem", "kn_mem", "g_mlp")
BIG = ("w_in", "w_mem_kv", "w_o_swa", "w_o_fox", "w_o_mem", "w_out", "w_mlp_up", "w_mlp_down")
COL_SHARDED = ("w_in", "w_o_swa", "w_o_fox", "w_o_mem", "w_mlp_up")
WEIGHTS = ("g_mix", "w_in", "b_gate", "b_forget", "qn_swa", "kn_swa", "sink_swa", "rel_bias", "qn_fox", "kn_fox", "g_mem",
           "w_mem_kv", "qn_mem", "kn_mem", "w_o_swa", "w_o_fox", "w_o_mem", "w_out", "g_mlp", "w_mlp_up", "w_mlp_down")
SMALL_USED = 6928
SMALL_PAD = 7168


def _gathered_to_full(name, g):
    if name in COL_SHARDED:
        return jnp.transpose(g, (1, 0, 2)).reshape(g.shape[1], N_DEV * g.shape[2])
    return g.reshape(N_DEV * g.shape[1], g.shape[2])


def _full_to_parts(name, full, b):
    if name in COL_SHARDED:
        return jnp.transpose(full.reshape(full.shape[0], N_DEV, b), (1, 0, 2)).astype(BF16)
    return full.reshape(N_DEV, full.shape[0] // N_DEV, full.shape[1]).astype(BF16)


def _pack_small(d, loss=None):
    flat = jnp.concatenate([d[n].reshape(-1) for n in SMALL])
    assert flat.shape[0] == SMALL_USED
    if loss is not None:
        flat = jnp.concatenate([flat, loss.reshape(-1)])
    return jnp.pad(flat, (0, SMALL_PAD - flat.shape[0])).reshape(8, SMALL_PAD // 8)


def _unpack_small(packed, like):
    flat = packed.reshape(-1)
    out, off = {}, 0
    for n in SMALL:
        size = like[n].size
        out[n] = flat[off:off + size].reshape(like[n].shape)
        off += size
    return out


def _adamw_small(parts, w, m, v):
    def body(p_ref, w_ref, m_ref, v_ref, g_ref, d_ref, m2_ref, v2_ref):
        g = p_ref[0]
        for k in range(1, N_DEV):
            g = g + p_ref[k]
        delta, m2, v2 = _adam_math(w_ref[...], g, m_ref[...], v_ref[...])
        g_ref[...] = g
        d_ref[...] = delta
        m2_ref[...] = m2
        v2_ref[...] = v2

    sd = jax.ShapeDtypeStruct(w.shape, F32)
    return pl.pallas_call(body, name="adamw_small", out_shape=[sd, sd, sd, sd])(parts, w, m, v)


def kernel(x, mem, g_mix, w_in, b_gate, b_forget, qn_swa, kn_swa, sink_swa, rel_bias, qn_fox, kn_fox, g_mem, w_mem_kv, qn_mem, kn_mem, w_o_swa, w_o_fox, w_o_mem, w_out, g_mlp, w_mlp_up, w_mlp_down, loss_target, m_g_mix, m_w_in, m_b_gate, m_b_forget, m_qn_swa, m_kn_swa, m_sink_swa, m_rel_bias, m_qn_fox, m_kn_fox, m_g_mem, m_w_mem_kv, m_qn_mem, m_kn_mem, m_w_o_swa, m_w_o_fox, m_w_o_mem, m_w_out, m_g_mlp, m_w_mlp_up, m_w_mlp_down, v_g_mix, v_w_in, v_b_gate, v_b_forget, v_qn_swa, v_kn_swa, v_sink_swa, v_rel_bias, v_qn_fox, v_kn_fox, v_g_mem, v_w_mem_kv, v_qn_mem, v_kn_mem, v_w_o_swa, v_w_o_fox, v_w_o_mem, v_w_out, v_g_mlp, v_w_mlp_up, v_w_mlp_down):
    wts = dict(g_mix=g_mix, w_in=w_in, b_gate=b_gate, b_forget=b_forget, qn_swa=qn_swa, kn_swa=kn_swa, sink_swa=sink_swa,
               rel_bias=rel_bias, qn_fox=qn_fox, kn_fox=kn_fox, g_mem=g_mem, w_mem_kv=w_mem_kv, qn_mem=qn_mem, kn_mem=kn_mem,
               w_o_swa=w_o_swa, w_o_fox=w_o_fox, w_o_mem=w_o_mem, w_out=w_out, g_mlp=g_mlp, w_mlp_up=w_mlp_up,
               w_mlp_down=w_mlp_down)
    mom = dict(g_mix=m_g_mix, w_in=m_w_in, b_gate=m_b_gate, b_forget=m_b_forget, qn_swa=m_qn_swa, kn_swa=m_kn_swa,
               sink_swa=m_sink_swa, rel_bias=m_rel_bias, qn_fox=m_qn_fox, kn_fox=m_kn_fox, g_mem=m_g_mem, w_mem_kv=m_w_mem_kv,
               qn_mem=m_qn_mem, kn_mem=m_kn_mem, w_o_swa=m_w_o_swa, w_o_fox=m_w_o_fox, w_o_mem=m_w_o_mem, w_out=m_w_out,
               g_mlp=m_g_mlp, w_mlp_up=m_w_mlp_up, w_mlp_down=m_w_mlp_down)
    var = dict(g_mix=v_g_mix, w_in=v_w_in, b_gate=v_b_gate, b_forget=v_b_forget, qn_swa=v_qn_swa, kn_swa=v_kn_swa,
               sink_swa=v_sink_swa, rel_bias=v_rel_bias, qn_fox=v_qn_fox, kn_fox=v_kn_fox, g_mem=v_g_mem, w_mem_kv=v_w_mem_kv,
               qn_mem=v_qn_mem, kn_mem=v_kn_mem, w_o_swa=v_w_o_swa, w_o_fox=v_w_o_fox, w_o_mem=v_w_o_mem, w_out=v_w_out,
               g_mlp=v_g_mlp, w_mlp_up=v_w_mlp_up, w_mlp_down=v_w_mlp_down)

    shards = _cast_shards({n: wts[n][0] for n in BIG})
    first = _split_start([shards["w_in"]], True, "ag_start_w_in", peers=(1,) + SAME_CORE)
    rest = _split_start([shards[n] for n in BIG[1:]], True, "ag_start_rest", after=first[4])
    full = {}

    def getw(n, after):
        if n == "w_in" and n not in full:
            forwarded = _forward_start(first, after, "ag_forward_w_in")
            full[n] = _w_in_from_shards(_forward_wait(first, forwarded, "ag_wait_w_in"))
        elif n not in full:
            land = _split_wait(rest, BIG[1:].index(n), after, True, "ag_wait_" + n)
            full[n] = land if n == "w_mlp_up" else _gathered_to_full(n, land)
        return full[n]

    exchanges = {}

    def emit(grads_by_name):
        parts = []
        for n, grad in grads_by_name.items():
            if n == "w_in":
                parts.append(_dw_in_to_parts(grad))
            else:
                parts.append(grad if n == "w_mlp_up" else _full_to_parts(n, grad, wts[n].shape[2]))
        started = _split_start(parts, False, "rs_start_" + next(iter(grads_by_name)))
        for w, n in enumerate(grads_by_name):
            exchanges[n] = (started, w)
        return started[4]

    small_p = {n: wts[n] for n in SMALL}
    loss, grad_x, small_g = _local_step(x[0], mem[0], loss_target[0], small_p, getw, emit, (first[4], rest[4]))

    packed = _pack_small(small_g, loss)
    small_gather = _split_start([packed], True, "ag_start_small")

    grads, delta, new_m, new_v = {}, {}, {}, {}

    def update(n, after):
        land = _split_wait(*exchanges[n], after, False, "rs_wait_" + n)
        g, d, m2, v2 = _adamw(land, wts[n][0], mom[n][0], var[n][0], "adamw_" + n)
        grads[n], delta[n], new_m[n], new_v[n] = g[None], d[None], m2[None], v2[None]
        return d

    after = small_gather[4]
    for n in exchanges:
        if n != "w_in":
            after = update(n, after)

    gathered = _split_wait(small_gather, 0, after, True, "ag_wait_small")
    g, d, m2, v2 = _adamw_small(gathered, _pack_small(small_p), _pack_small({n: mom[n] for n in SMALL}),
                                _pack_small({n: var[n] for n in SMALL}))
    for dst, flat in ((grads, g), (delta, d), (new_m, m2), (new_v, v2)):
        dst.update(_unpack_small(flat, small_p))
    total = g.reshape(-1)[SMALL_USED]
    update("w_in", d)

    return (total, grad_x[None], *[grads[n] for n in WEIGHTS], *[delta[n] for n in WEIGHTS],
            *[new_m[n] for n in WEIGHTS], *[new_v[n] for n in WEIGHTS])
```

```python
import functools
import math

import jax
import jax.numpy as jnp
from jax import lax
from jax.experimental import pallas as pl
from jax.experimental.pallas import tpu as pltpu

F32 = jnp.float32
BF16 = jnp.bfloat16

D_MODEL = 1024
N_MEM = 256
D_FF = 4096
HEAD = 64
SWA_HEADS = 8
SWA_BLOCK = 128
MEM_HEADS = 4
MEM_HEAD = 128
EPS = 1e-6
NEG = -1e30
REL_BUCKETS = 32
REL_MAX_DIST = 128

ADAM_LR = 0.001
ADAM_B1 = 0.9
ADAM_B2 = 0.999
ADAM_EPS = 1e-08
ADAM_WD = 0.01
ADAM_STEP = 10

GL0, QF0, KF0, VF0, QM0, QA0, KA0, VA0, FL0 = 0, 3072, 3584, 4096, 4608, 5120, 5632, 5760, 5888
PROJ_W = 6144
HALF_W = 3072
H_QF, H_KF, H_VF, H_QM, H_QA, H_KA, H_VA, H_FL = 0, 512, 1024, 1536, 2048, 2560, 2688, 2816

VMEM_LIMIT = 56 * 1024 * 1024
N_DEV = 8
MESH = pl.DeviceIdType.MESH

NN = (((1,), (0,)), ((), ()))
NT = (((1,), (1,)), ((), ()))
TN = (((0,), (0,)), ((), ()))


def _dot(a, b, dims=NN):
    return lax.dot_general(a, b, dims, preferred_element_type=F32)


def _params(sem):
    return pltpu.CompilerParams(dimension_semantics=sem, vmem_limit_bytes=VMEM_LIMIT)


def _full(shape):
    nd = len(shape)
    return pl.BlockSpec(shape, lambda *_: (0,) * nd)


def _sigmoid(z):
    return 1.0 / (1.0 + jnp.exp(-z))


def _group_mean(v, hd):
    if hd == 128:
        return jnp.mean(v, axis=-1, keepdims=True)
    lane = lax.broadcasted_iota(jnp.int32, v.shape, 1)
    lo = lane < HEAD
    s_lo = jnp.sum(jnp.where(lo, v, 0.0), axis=-1, keepdims=True)
    s_hi = jnp.sum(jnp.where(lo, 0.0, v), axis=-1, keepdims=True)
    return jnp.where(lo, s_lo, s_hi) * (1.0 / HEAD)


def _mm(a, b, mode, out_dtype, tm, tn, tk, name, column_chunks=False):
    if mode == "nn":
        m, k = a.shape
        n = b.shape[1]
    elif mode == "nt":
        m, k = a.shape
        n = b.shape[0]
    else:
        k, m = a.shape
        n = b.shape[1]
    tm, tn, tk = min(tm, m), min(tn, n), min(tk, k)
    nk = k // tk
    chunk = n // N_DEV
    per_tile = tn // chunk if column_chunks else 1
    dims = {"nn": NN, "nt": NT, "tn": TN}[mode]
    a_spec = pl.BlockSpec((tk, tm), lambda j, i, kk: (kk, i)) if mode == "tn" else pl.BlockSpec((tm, tk), lambda j, i, kk: (i, kk))
    b_spec = pl.BlockSpec((tn, tk), lambda j, i, kk: (j, kk)) if mode == "nt" else pl.BlockSpec((tk, tn), lambda j, i, kk: (kk, j))

    def body(a_ref, b_ref, o_ref, *acc):
        prod = _dot(a_ref[...].astype(BF16), b_ref[...].astype(BF16), dims)

        def write(res):
            if column_chunks:
                for c in range(per_tile):
                    o_ref[c] = res[:, c * chunk:(c + 1) * chunk].astype(o_ref.dtype)
            else:
                o_ref[...] = res.astype(o_ref.dtype)

        if nk == 1:
            write(prod)
        else:
            acc_ref, = acc
            kk = pl.program_id(2)

            @pl.when(kk == 0)
            def _():
                acc_ref[...] = prod

            @pl.when(kk > 0)
            def _():
                acc_ref[...] += prod

            @pl.when(kk == nk - 1)
            def _():
                write(acc_ref[...])

    return pl.pallas_call(
        body, name=name, grid=(n // tn, m // tm, nk),
        in_specs=[a_spec, b_spec],
        out_specs=(pl.BlockSpec((per_tile, tm, chunk), lambda j, i, kk: (j, i, 0)) if column_chunks
                   else pl.BlockSpec((tm, tn), lambda j, i, kk: (i, j))),
        out_shape=jax.ShapeDtypeStruct((N_DEV, m, chunk) if column_chunks else (m, n), out_dtype),
        scratch_shapes=[pltpu.VMEM((tm, tn), F32)] if nk > 1 else [],
        compiler_params=_params(("parallel", "parallel", "arbitrary")),
    )(a, b)


def _rms_fwd(x, g, name, deps=()):
    s, d = x.shape
    tm = min(512, s)

    def body(x_ref, g_ref, *rest):
        h_ref = rest[len(deps)]
        xv = x_ref[...]
        r = lax.rsqrt(jnp.mean(xv * xv, axis=-1, keepdims=True) + EPS)
        h_ref[...] = (xv * r * g_ref[...]).astype(BF16)

    return pl.pallas_call(
        body, name=name, grid=(s // tm,),
        in_specs=[pl.BlockSpec((tm, d), lambda i: (i, 0)), _full((1, d))] + [pl.BlockSpec(memory_space=pl.ANY)] * len(deps),
        out_specs=pl.BlockSpec((tm, d), lambda i: (i, 0)),
        out_shape=jax.ShapeDtypeStruct((s, d), BF16),
        compiler_params=_params(("parallel",)),
    )(x, g, *deps)


def _proj_post(proj, gq_fox, gk_fox, gq_mem, gq_swa, gk_swa):
    s = proj.shape[0]
    tm = min(256, s)

    def body(p_ref, gqf, gkf, gqm, gqa, gka, qf_ref, kf_ref, vf_ref, qm_ref, qa_ref, ka_ref, va_ref):
        def norm(off, width, hd, g_ref, o_ref):
            for b in range(width // 128):
                v = p_ref[:, off + b * 128: off + (b + 1) * 128].astype(F32)
                r = lax.rsqrt(_group_mean(v * v, hd) + EPS)
                o_ref[:, b * 128:(b + 1) * 128] = (v * r * g_ref[...]).astype(BF16)

        norm(H_QF, 512, HEAD, gqf, qf_ref)
        norm(H_KF, 512, HEAD, gkf, kf_ref)
        vf_ref[...] = p_ref[:, H_VF:H_VF + 512].astype(BF16)
        norm(H_QM, 512, MEM_HEAD, gqm, qm_ref)
        norm(H_QA, 512, HEAD, gqa, qa_ref)
        norm(H_KA, 128, HEAD, gka, ka_ref)
        va_ref[...] = p_ref[:, H_VA:H_VA + 128].astype(BF16)

    g_spec = _full((1, 128))
    o512 = pl.BlockSpec((tm, 512), lambda i: (i, 0))
    o128 = pl.BlockSpec((tm, 128), lambda i: (i, 0))
    s512 = jax.ShapeDtypeStruct((s, 512), BF16)
    s128 = jax.ShapeDtypeStruct((s, 128), BF16)
    return pl.pallas_call(
        body, name="proj_post", grid=(s // tm,),
        in_specs=[pl.BlockSpec((tm, HALF_W), lambda i: (i, 1)), g_spec, g_spec, g_spec, g_spec, g_spec],
        out_specs=[o512, o512, o512, o512, o512, o128, o128],
        out_shape=[s512, s512, s512, s512, s512, s128, s128],
        compiler_params=_params(("parallel",)),
    )(proj, gq_fox, gk_fox, gq_mem, gq_swa, gk_swa)


def _tri(n, lower):
    r = lax.broadcasted_iota(jnp.int32, (n, n), 0)
    c = lax.broadcasted_iota(jnp.int32, (n, n), 1)
    return jnp.where((c <= r) if lower else (c >= r), 1.0, 0.0).astype(F32)


def _fox_gate_fwd(proj, b_forget128):
    s = proj.shape[0]
    tm = min(512, s)

    def body(p_ref, b_ref, cc_ref, carry_ref):
        i = pl.program_id(0)

        @pl.when(i == 0)
        def _():
            carry_ref[...] = jnp.zeros_like(carry_ref)

        z = p_ref[...] + b_ref[...]
        logf = jnp.minimum(z, 0.0) - jnp.log(1.0 + jnp.exp(-jnp.abs(z)))
        c = jnp.dot(_tri(tm, True), logf, precision=lax.Precision.HIGHEST, preferred_element_type=F32) + carry_ref[...]
        carry_ref[...] = c[tm - 1:tm, :]
        for hp in range(4):
            cc_ref[hp] = c if hp == 0 else pltpu.roll(c, 128 - 2 * hp, 1)

    return pl.pallas_call(
        body, name="fox_gate_fwd", grid=(s // tm,),
        in_specs=[pl.BlockSpec((tm, 128), lambda i: (i, 0)), _full((1, 128))],
        out_specs=pl.BlockSpec((4, tm, 128), lambda i: (0, i, 0)),
        out_shape=jax.ShapeDtypeStruct((4, s, 128), F32),
        scratch_shapes=[pltpu.VMEM((1, 128), F32)],
        compiler_params=_params(("arbitrary",)),
    )(proj, b_forget128)


def _memkv_fwd(mem, g_mem, w_kv, kn_mem):
    m = mem.shape[0]

    def body(mem_ref, g_ref, w_ref, kn_ref, memn_ref, kv_ref, mk_ref, mv_ref):
        xv = mem_ref[...]
        r = lax.rsqrt(jnp.mean(xv * xv, axis=-1, keepdims=True) + EPS)
        mn = (xv * r * g_ref[...]).astype(BF16)
        memn_ref[...] = mn
        kv = _dot(mn, w_ref[...])
        kv_ref[...] = kv
        for h in range(MEM_HEADS):
            v = kv[:, h * 128:(h + 1) * 128]
            rr = lax.rsqrt(jnp.mean(v * v, axis=-1, keepdims=True) + EPS)
            mk_ref[:, h * 128:(h + 1) * 128] = (v * rr * kn_ref[...]).astype(BF16)
        mv_ref[...] = kv[:, 512:1024].astype(BF16)

    return pl.pallas_call(
        body, name="memkv_fwd",
        out_shape=[jax.ShapeDtypeStruct((m, D_MODEL), BF16), jax.ShapeDtypeStruct((m, 1024), F32),
                   jax.ShapeDtypeStruct((m, 512), BF16), jax.ShapeDtypeStruct((m, 512), BF16)],
        compiler_params=pltpu.CompilerParams(vmem_limit_bytes=VMEM_LIMIT),
    )(mem, g_mem, w_kv, kn_mem)


def _bias_table(rel_bias, bucket):
    def body(rb_ref, bk_ref, o_ref):
        bk = bk_ref[...]
        for h in range(SWA_HEADS):
            acc = jnp.zeros(bk.shape, F32)
            for b in range(REL_BUCKETS):
                acc = jnp.where(bk == b, rb_ref[b, h], acc)
            o_ref[h] = acc

    return pl.pallas_call(
        body, name="bias_table",
        in_specs=[pl.BlockSpec(memory_space=pltpu.SMEM), pl.BlockSpec(memory_space=pltpu.VMEM)],
        out_shape=jax.ShapeDtypeStruct((SWA_HEADS,) + bucket.shape, F32),
    )(rel_bias, bucket)


def _swa_valid(n):
    row = lax.broadcasted_iota(jnp.int32, (SWA_BLOCK, 2 * SWA_BLOCK), 0)
    col = lax.broadcasted_iota(jnp.int32, (SWA_BLOCK, 2 * SWA_BLOCK), 1)
    dist = row + SWA_BLOCK - col
    return (dist >= 0) & (dist < SWA_BLOCK) & ((col >= SWA_BLOCK) | (n > 0))


def _swa_fwd(qa, kp, vp, bias, sink):
    s = qa.shape[0]
    nb = s // SWA_BLOCK

    def body(sink_ref, q_ref, kp_ref, vp_ref, bias_ref, o_ref):
        n = pl.program_id(0)
        start = pl.multiple_of(n * SWA_BLOCK, SWA_BLOCK)
        k2 = kp_ref[pl.ds(start, 2 * SWA_BLOCK), :]
        v2 = vp_ref[pl.ds(start, 2 * SWA_BLOCK), :]
        valid = _swa_valid(n)
        heads = range(SWA_HEADS)
        hs = lambda h: slice(h * HEAD, (h + 1) * HEAD)
        sc = [jnp.where(valid, _dot(q_ref[:, hs(h)], k2[:, hs(h // 4)], NT) * 0.125 + bias_ref[h], NEG) for h in heads]
        pn = []
        for h in heads:
            sk = sink_ref[h]
            mx = jnp.maximum(jnp.max(sc[h], axis=-1, keepdims=True), sk)
            p = jnp.exp(sc[h] - mx)
            den = jnp.sum(p, axis=-1, keepdims=True) + jnp.exp(sk - mx)
            pn.append((p / den).astype(BF16))
        outs = [_dot(pn[h], v2[:, hs(h // 4)]).astype(BF16) for h in heads]
        for h in heads:
            o_ref[:, hs(h)] = outs[h]

    return pl.pallas_call(
        body, name="swa_fwd", grid=(nb,),
        in_specs=[pl.BlockSpec(memory_space=pltpu.SMEM),
                  pl.BlockSpec((SWA_BLOCK, 512), lambda n: (n, 0)),
                  _full(kp.shape), _full(vp.shape), _full(bias.shape)],
        out_specs=pl.BlockSpec((SWA_BLOCK, 512), lambda n: (n, 0)),
        out_shape=jax.ShapeDtypeStruct((s, 512), BF16),
        compiler_params=_params(("parallel",)),
    )(sink, qa, kp, vp, bias)


def _head_mask(e):
    lane = lax.broadcasted_iota(jnp.int32, (1, 128), 1)
    return (lane >= e * HEAD) & (lane < (e + 1) * HEAD)


FOX_FWD_TQ, FOX_FWD_TK = 1024, 1024
FOX_BWD_TK, FOX_BWD_TQ = 512, 512


def _head_rows(e):
    row = lax.broadcasted_iota(jnp.int32, (128, 1), 0)
    return (row >= e * HEAD) & (row < (e + 1) * HEAD)


def _fox_fwd(q, k, v_t, cc4):
    s = q.shape[0]
    t = min(FOX_FWD_TQ, s)
    tk = min(FOX_FWD_TK, s)
    nq = s // t

    def body(q_ref, k_ref, vt_ref, cc_ref, o_ref, lse_ref):
        i = pl.program_id(1)
        qs = q_ref[...] * jnp.asarray(0.125, BF16)
        qe = [jnp.where(_head_mask(e), qs, jnp.zeros_like(qs)) for e in range(2)]
        n_full = (i * t) // tk
        krow = lax.broadcasted_iota(jnp.int32, (tk, t), 0) + n_full * tk
        qcol = lax.broadcasted_iota(jnp.int32, (tk, t), 1) + i * t

        def step(j, carry, masked):
            ks = pl.ds(pl.multiple_of(j * tk, tk), tk)
            kj = k_ref[ks, :]
            vtj = vt_ref[:, ks]
            out = []
            for e in range(2):
                m, acc = carry[2 * e], carry[2 * e + 1]
                st = _dot(kj, qe[e], NT) - cc_ref[0, ks, e:e + 1]
                if masked:
                    st = jnp.where(krow <= qcol, st, NEG)
                m_new = jnp.maximum(m, jnp.max(st, axis=0, keepdims=True))
                alpha = jnp.exp(m - m_new)
                pt = jnp.exp(st - m_new).astype(BF16)
                vte = jnp.where(_head_rows(e), vtj, jnp.ones_like(vtj))
                out += [m_new, alpha * acc + _dot(vte, pt)]
            return tuple(out)

        init = (jnp.full((1, t), NEG, F32), jnp.zeros((128, t), F32)) * 2
        carry = lax.fori_loop(0, n_full, functools.partial(step, masked=False), init)
        m0, a0, m1, a1 = step(n_full, carry, True)
        l0 = a0[HEAD:HEAD + 1, :]
        l1 = a1[0:1, :]
        o_t = jnp.where(_head_rows(0), a0 / l0, a1 / l1)
        o_ref[...] = o_t.T.astype(BF16)
        r8 = lax.broadcasted_iota(jnp.int32, (8, t), 0)
        lse_ref[0] = jnp.where(r8 == 0, m0 + jnp.log(l0), jnp.where(r8 == 1, m1 + jnp.log(l1), 0.0))

    return pl.pallas_call(
        body, name="fox_fwd", grid=(4, nq),
        in_specs=[pl.BlockSpec((t, 128), lambda hp, i: (i, hp)),
                  pl.BlockSpec((s, 128), lambda hp, i: (0, hp)),
                  pl.BlockSpec((128, s), lambda hp, i: (hp, 0)),
                  pl.BlockSpec((1, s, 128), lambda hp, i: (hp, 0, 0))],
        out_specs=[pl.BlockSpec((t, 128), lambda hp, i: (i, hp)),
                   pl.BlockSpec((1, 8, t), lambda hp, i: (hp, 0, i))],
        out_shape=[jax.ShapeDtypeStruct((s, 512), BF16), jax.ShapeDtypeStruct((4, 8, s), F32)],
        compiler_params=_params(("parallel", "parallel")),
    )(q, k, v_t, cc4)


MEM_SCALE = MEM_HEAD ** -0.5


def _mem_fwd(qm, mk, mv):
    s = qm.shape[0]
    tq = min(512, s)

    def body(q_ref, mk_ref, mv_ref, o_ref):
        for h in range(MEM_HEADS):
            hs = slice(h * 128, (h + 1) * 128)
            sc = _dot(q_ref[:, hs], mk_ref[:, hs], NT) * MEM_SCALE
            mx = jnp.max(sc, axis=-1, keepdims=True)
            p = jnp.exp(sc - mx)
            p = p / jnp.sum(p, axis=-1, keepdims=True)
            o_ref[:, hs] = _dot(p.astype(BF16), mv_ref[:, hs]).astype(BF16)

    return pl.pallas_call(
        body, name="mem_fwd", grid=(s // tq,),
        in_specs=[pl.BlockSpec((tq, 512), lambda i: (i, 0)), _full(mk.shape), _full(mv.shape)],
        out_specs=pl.BlockSpec((tq, 512), lambda i: (i, 0)),
        out_shape=jax.ShapeDtypeStruct((s, 512), BF16),
        compiler_params=_params(("parallel",)),
    )(qm, mk, mv)


def _merge_fwd(x, oa, of, om, proj, b_gate, wa, wf, wm, w_out, g_mlp):
    s = x.shape[0]
    tm = min(256, s)

    def body(x_ref, oa_ref, of_ref, om_ref, gl_ref, bg_ref, wa_ref, wf_ref, wm_ref, wo_ref, g_ref, x1_ref, hm_ref, mg_ref):
        merged = None
        for b, (o_ref, w_ref) in enumerate(((oa_ref, wa_ref), (of_ref, wf_ref), (om_ref, wm_ref))):
            cs = slice(b * D_MODEL, (b + 1) * D_MODEL)
            y = _dot(o_ref[...], w_ref[...])
            t = _sigmoid(gl_ref[:, cs].astype(F32) + bg_ref[:, cs]) * y
            merged = t if merged is None else merged + t
        mb = merged.astype(BF16)
        mg_ref[...] = mb
        x1 = x_ref[...] + _dot(mb, wo_ref[...])
        x1_ref[...] = x1
        r = lax.rsqrt(jnp.mean(x1 * x1, axis=-1, keepdims=True) + EPS)
        hm_ref[...] = (x1 * r * g_ref[...]).astype(BF16)

    row = lambda w: pl.BlockSpec((tm, w), lambda i: (i, 0))
    return pl.pallas_call(
        body, name="merge_fwd", grid=(s // tm,),
        in_specs=[row(D_MODEL), row(512), row(512), row(512), row(HALF_W), _full((1, HALF_W)),
                  _full(wa.shape), _full(wf.shape), _full(wm.shape), _full(w_out.shape), _full((1, D_MODEL))],
        out_specs=[row(D_MODEL), row(D_MODEL), row(D_MODEL)],
        out_shape=[jax.ShapeDtypeStruct((s, D_MODEL), F32), jax.ShapeDtypeStruct((s, D_MODEL), BF16),
                   jax.ShapeDtypeStruct((s, D_MODEL), BF16)],
        compiler_params=_params(("parallel",)),
    )(x, oa, of, om, proj, b_gate, wa, wf, wm, w_out, g_mlp)


def _mlp_up(hm, w_up):
    s = hm.shape[0]
    tm, tn = min(1024, s), w_up.shape[2]

    def body(h_ref, w_ref, u_ref):
        r = jnp.maximum(_dot(h_ref[...], w_ref[0]), 0.0)
        u_ref[...] = (r * r).astype(BF16)

    return pl.pallas_call(
        body, name="mlp_up", grid=(s // tm, D_FF // tn),
        in_specs=[pl.BlockSpec((tm, D_MODEL), lambda i, j: (i, 0)), pl.BlockSpec((1, D_MODEL, tn), lambda i, j: (j, 0, 0))],
        out_specs=pl.BlockSpec((tm, tn), lambda i, j: (i, j)),
        out_shape=jax.ShapeDtypeStruct((s, D_FF), BF16),
        compiler_params=_params(("parallel", "parallel")),
    )(hm, w_up)


def _mlp_down_loss(u, w_down, x1, target):
    s = u.shape[0]
    tm = min(256, s)

    def body(u_ref, w_ref, x1_ref, t_ref, dy_ref, dyb_ref, loss_ref):
        i = pl.program_id(0)

        @pl.when(i == 0)
        def _():
            loss_ref[...] = jnp.zeros_like(loss_ref)

        y = x1_ref[...] + _dot(u_ref[...], w_ref[...])
        err = y - t_ref[...]
        dy = err * (1.0 / D_MODEL)
        dy_ref[...] = dy
        dyb_ref[...] = dy.astype(BF16)
        part = jnp.sum(jnp.sum(err * err, axis=-1, keepdims=True) * (1.0 / D_MODEL), axis=0, keepdims=True)
        loss_ref[...] += 0.5 * part

    row = pl.BlockSpec((tm, D_MODEL), lambda i: (i, 0))
    return pl.pallas_call(
        body, name="mlp_down_loss", grid=(s // tm,),
        in_specs=[pl.BlockSpec((tm, D_FF), lambda i: (i, 0)), _full(w_down.shape), row, row],
        out_specs=[row, row, _full((1, 1))],
        out_shape=[jax.ShapeDtypeStruct((s, D_MODEL), F32), jax.ShapeDtypeStruct((s, D_MODEL), BF16),
                   jax.ShapeDtypeStruct((1, 1), F32)],
        compiler_params=_params(("arbitrary",)),
    )(u, w_down, x1, target)


def _mlp_bwd_act(dy, w_down, u):
    s = dy.shape[0]
    tm, tn = min(1024, s), 1024

    def body(dy_ref, w_ref, u_ref, da_ref):
        du = _dot(dy_ref[...], w_ref[...], NT)
        da_ref[...] = (du * (2.0 * jnp.sqrt(u_ref[...].astype(F32)))).astype(BF16)

    return pl.pallas_call(
        body, name="mlp_bwd_act", grid=(D_FF // tn, s // tm),
        in_specs=[pl.BlockSpec((tm, D_MODEL), lambda j, i: (i, 0)), pl.BlockSpec((tn, D_MODEL), lambda j, i: (j, 0)),
                  pl.BlockSpec((tm, tn), lambda j, i: (i, j))],
        out_specs=pl.BlockSpec((tm, tn), lambda j, i: (i, j)),
        out_shape=jax.ShapeDtypeStruct((s, D_FF), BF16),
        compiler_params=_params(("parallel", "parallel")),
    )(dy, w_down, u)


def _rms_bwd(xv, g, dh, skip):
    r = lax.rsqrt(jnp.mean(xv * xv, axis=-1, keepdims=True) + EPS)
    n = xv * r
    dn = dh * g
    dx = skip + r * (dn - n * jnp.mean(dn * n, axis=-1, keepdims=True))
    return dx, jnp.sum(dh * n, axis=0, keepdims=True)


def _mlp_bwd_x(da, w_up, x1, dy, g_mlp):
    s = da.shape[0]
    tm = min(256, s)

    def body(da_ref, w_ref, x1_ref, dy_ref, g_ref, dx1_ref, dg_ref):
        i = pl.program_id(0)

        @pl.when(i == 0)
        def _():
            dg_ref[...] = jnp.zeros_like(dg_ref)

        tn = w_ref.shape[2]
        dhm = _dot(da_ref[:, 0:tn], w_ref[0], NT)
        for j in range(1, N_DEV):
            dhm = dhm + _dot(da_ref[:, j * tn:(j + 1) * tn], w_ref[j], NT)
        dx, dg = _rms_bwd(x1_ref[...], g_ref[...], dhm, dy_ref[...])
        dx1_ref[...] = dx
        dg_ref[...] += dg

    row = pl.BlockSpec((tm, D_MODEL), lambda i: (i, 0))
    return pl.pallas_call(
        body, name="mlp_bwd_x", grid=(s // tm,),
        in_specs=[pl.BlockSpec((tm, D_FF), lambda i: (i, 0)), _full(w_up.shape), row, row, _full((1, D_MODEL))],
        out_specs=[row, _full((1, D_MODEL))],
        out_shape=[jax.ShapeDtypeStruct((s, D_MODEL), F32), jax.ShapeDtypeStruct((1, D_MODEL), F32)],
        compiler_params=_params(("arbitrary",)),
    )(da, w_up, x1, dy, g_mlp)


def _merge_bwd(dx1, oa, of, om, proj, b_gate, wa, wf, wm, w_out):
    s = dx1.shape[0]
    tm = min(256, s)

    def body(dx1_ref, oa_ref, of_ref, om_ref, gl_ref, bg_ref, wa_ref, wf_ref, wm_ref, wo_ref,
             dp_ref, doa_ref, dof_ref, dom_ref, dya_ref, dyf_ref, dym_ref, dbg_ref):
        i = pl.program_id(0)

        @pl.when(i == 0)
        def _():
            dbg_ref[...] = jnp.zeros_like(dbg_ref)

        dmerged = _dot(dx1_ref[...].astype(BF16), wo_ref[...], NT)
        branches = ((oa_ref, wa_ref, doa_ref, dya_ref), (of_ref, wf_ref, dof_ref, dyf_ref), (om_ref, wm_ref, dom_ref, dym_ref))
        for b, (o_ref, w_ref, do_ref, dyb_ref) in enumerate(branches):
            cs = slice(b * D_MODEL, (b + 1) * D_MODEL)
            y = _dot(o_ref[...], w_ref[...])
            g = _sigmoid(gl_ref[:, cs].astype(F32) + bg_ref[:, cs])
            dz = (dmerged * y) * g * (1.0 - g)
            dp_ref[:, cs] = dz.astype(BF16)
            dbg_ref[:, cs] += jnp.sum(dz, axis=0, keepdims=True)
            dyb = (dmerged * g).astype(BF16)
            dyb_ref[...] = dyb
            do_ref[...] = _dot(dyb, w_ref[...], NT).astype(BF16)

    row = lambda w: pl.BlockSpec((tm, w), lambda i: (i, 0))
    sd = lambda w: jax.ShapeDtypeStruct((s, w), BF16)
    return pl.pallas_call(
        body, name="merge_bwd", grid=(s // tm,),
        in_specs=[row(D_MODEL), row(512), row(512), row(512), row(HALF_W), _full((1, HALF_W)),
                  _full(wa.shape), _full(wf.shape), _full(wm.shape), _full(w_out.shape)],
        out_specs=[row(HALF_W), row(512), row(512), row(512), row(D_MODEL), row(D_MODEL), row(D_MODEL), _full((1, HALF_W))],
        out_shape=[sd(PROJ_W), sd(512), sd(512), sd(512), sd(D_MODEL), sd(D_MODEL), sd(D_MODEL),
                   jax.ShapeDtypeStruct((1, HALF_W), F32)],
        compiler_params=_params(("arbitrary",)),
    )(dx1, oa, of, om, proj, b_gate, wa, wf, wm, w_out)


def _swa_valid_t(n):
    key = lax.broadcasted_iota(jnp.int32, (2 * SWA_BLOCK, SWA_BLOCK), 0)
    qry = lax.broadcasted_iota(jnp.int32, (2 * SWA_BLOCK, SWA_BLOCK), 1)
    dist = qry + SWA_BLOCK - key
    return (dist >= 0) & (dist < SWA_BLOCK) & ((key >= SWA_BLOCK) | (n > 0))


def _swa_bwd(qa, kp, vp, bias_t, sink, doa):
    s = qa.shape[0]
    nb = s // SWA_BLOCK

    def body(sink_ref, q_ref, kp_ref, vp_ref, bias_ref, do_ref, dq_ref, dkp_ref, dvp_ref, dbias_ref, dsink_ref, sk_acc):
        n = pl.program_id(0)

        @pl.when(n == 0)
        def _():
            dkp_ref[...] = jnp.zeros_like(dkp_ref)
            dvp_ref[...] = jnp.zeros_like(dvp_ref)
            dbias_ref[...] = jnp.zeros_like(dbias_ref)
            sk_acc[...] = jnp.zeros_like(sk_acc)

        start = pl.multiple_of(n * SWA_BLOCK, SWA_BLOCK)
        win = pl.ds(start, 2 * SWA_BLOCK)
        k2 = kp_ref[win, :]
        v2 = vp_ref[win, :]
        valid = _swa_valid_t(n)
        heads = range(SWA_HEADS)
        hs = lambda h: slice(h * HEAD, (h + 1) * HEAD)
        scale = jnp.asarray(0.125, BF16)
        q = [q_ref[:, hs(h)] for h in heads]
        do = [do_ref[:, hs(h)] for h in heads]
        kk = [k2[:, hs(kv)] for kv in range(2)]
        vv = [v2[:, hs(kv)] for kv in range(2)]
        kt = [(kk[kv].astype(F32) * 0.125).T.astype(BF16) for kv in range(2)]
        st = [jnp.where(valid, _dot(kk[h // 4], q[h], NT) * 0.125 + bias_ref[h], NEG) for h in heads]
        dpt = [_dot(vv[h // 4], do[h], NT) for h in heads]
        pt, dst = [], []
        for h in heads:
            sk = sink_ref[h]
            mx = jnp.maximum(jnp.max(st[h], axis=0, keepdims=True), sk)
            p = jnp.exp(st[h] - mx)
            esk = jnp.exp(sk - mx)
            den = jnp.sum(p, axis=0, keepdims=True) + esk
            p = p / den
            delta = jnp.sum(p * dpt[h], axis=0, keepdims=True)
            d = p * (dpt[h] - delta)
            sk_acc[h:h + 1, :] += -(esk / den) * delta
            dbias_ref[h] += d
            pt.append(p.astype(BF16))
            dst.append(d.astype(BF16))
        dq_t = [_dot(kt[h // 4], dst[h]) for h in heads]
        dq_ref[...] = jnp.concatenate(dq_t, axis=0).T
        for kv in range(2):
            group = range(4 * kv, 4 * kv + 4)
            dk = [_dot(dst[h], q[h] * scale) for h in group]
            dv = [_dot(pt[h], do[h]) for h in group]
            dkp_ref[win, hs(kv)] += (dk[0] + dk[1]) + (dk[2] + dk[3])
            dvp_ref[win, hs(kv)] += (dv[0] + dv[1]) + (dv[2] + dv[3])

        @pl.when(n == nb - 1)
        def _():
            dsink_ref[...] = jnp.broadcast_to(jnp.sum(sk_acc[...], axis=1, keepdims=True), dsink_ref.shape)

    return pl.pallas_call(
        body, name="swa_bwd", grid=(nb,),
        in_specs=[pl.BlockSpec(memory_space=pltpu.SMEM),
                  pl.BlockSpec((SWA_BLOCK, 512), lambda n: (n, 0)),
                  _full(kp.shape), _full(vp.shape), _full(bias_t.shape),
                  pl.BlockSpec((SWA_BLOCK, 512), lambda n: (n, 0))],
        out_specs=[pl.BlockSpec((SWA_BLOCK, 512), lambda n: (n, 0)), _full(kp.shape), _full(vp.shape),
                   _full(bias_t.shape), _full((SWA_HEADS, 128))],
        out_shape=[jax.ShapeDtypeStruct((s, 512), F32), jax.ShapeDtypeStruct(kp.shape, F32),
                   jax.ShapeDtypeStruct(vp.shape, F32), jax.ShapeDtypeStruct(bias_t.shape, F32),
                   jax.ShapeDtypeStruct((SWA_HEADS, 128), F32)],
        scratch_shapes=[pltpu.VMEM((SWA_HEADS, 128), F32)],
        compiler_params=_params(("arbitrary",)),
    )(sink, qa, kp, vp, bias_t, doa)


def _fox_bwd(qt, k, v, dot, ot, cc4, lse4):
    s = k.shape[0]
    t = min(FOX_BWD_TK, s)
    tq = min(FOX_BWD_TQ, s)
    nq = s // t
    nqt = s // tq

    def body(qt_ref, k_ref, v_ref, dot_ref, ot_ref, cc_ref, lse_ref,
             dqt_ref, dk_ref, dv_ref, dck_ref, dcq_ref, delta_ref, dk0, dk1, dv0, dv1, ds0, ds1):
        j = pl.program_id(1)

        @pl.when(j == 0)
        def _():
            dqt_ref[...] = jnp.zeros_like(dqt_ref)
            dcq_ref[...] = jnp.zeros_like(dcq_ref)
            r8 = lax.broadcasted_iota(jnp.int32, (8, tq), 0)

            def dl(i, c):
                cols = pl.ds(pl.multiple_of(i * tq, tq), tq)
                pr = dot_ref[:, cols].astype(F32) * ot_ref[:, cols].astype(F32)
                d0 = jnp.sum(jnp.where(_head_rows(0), pr, 0.0), axis=0, keepdims=True)
                d1 = jnp.sum(jnp.where(_head_rows(1), pr, 0.0), axis=0, keepdims=True)
                delta_ref[:, cols] = jnp.where(r8 == 0, d0, jnp.where(r8 == 1, d1, 0.0))
                return c

            lax.fori_loop(0, nqt, dl, 0)

        kj = k_ref[...]
        vj = v_ref[...]
        ks = pl.ds(pl.multiple_of(j * t, t), t)
        kt = (kj.astype(F32) * 0.125).T.astype(BF16)
        ke = [jnp.where(_head_mask(e), kj, jnp.zeros_like(kj)) for e in range(2)]
        ve = [jnp.where(_head_mask(e), vj, jnp.zeros_like(vj)) for e in range(2)]
        kte = [jnp.where(_head_rows(e), kt, jnp.zeros_like(kt)) for e in range(2)]
        ck = [cc_ref[0, ks, e:e + 1] for e in range(2)]
        accs = ((dk0, dv0, ds0), (dk1, dv1, ds1))
        for refs in accs:
            for r in refs:
                r[...] = jnp.zeros_like(r)
        i_first = (j * t) // tq
        krow = lax.broadcasted_iota(jnp.int32, (t, tq), 0) + j * t
        qcol = lax.broadcasted_iota(jnp.int32, (t, tq), 1) + i_first * tq

        def step(i, c, masked):
            cols = pl.ds(pl.multiple_of(i * tq, tq), tq)
            qti = qt_ref[:, cols]
            doti = dot_ref[:, cols]
            for e in range(2):
                dkt_acc, dvt_acc, ds_acc = accs[e]
                st = _dot(ke[e], qti) - ck[e]
                if masked:
                    st = jnp.where(krow <= qcol, st, NEG)
                pt = jnp.exp(st - lse_ref[0, e:e + 1, cols])
                dpt = _dot(ve[e], doti)
                dst = pt * (dpt - delta_ref[e:e + 1, cols])
                dsb = dst.astype(BF16)
                dvt_acc[...] += _dot(doti, pt.astype(BF16), NT)
                dkt_acc[...] += _dot(qti, dsb, NT)
                dqt_ref[:, cols] += _dot(kte[e], dsb)
                ds_acc[...] += dst
                dcq_ref[0, e:e + 1, cols] += jnp.sum(dst, axis=0, keepdims=True)
            return c

        step(i_first, 0, True)
        lax.fori_loop(i_first + 1, nqt, functools.partial(step, masked=False), 0)
        r0 = _head_rows(0)
        dk_ref[...] = jnp.where(r0, dk0[...], dk1[...]).T
        dv_ref[...] = jnp.where(r0, dv0[...], dv1[...]).T
        lane = lax.broadcasted_iota(jnp.int32, (t, 128), 1)
        c0 = jnp.sum(ds0[...], axis=-1, keepdims=True)
        c1 = jnp.sum(ds1[...], axis=-1, keepdims=True)
        dck_ref[0] = jnp.where(lane == 0, c0, jnp.where(lane == 1, c1, 0.0))

    res_t = lambda: pl.BlockSpec((128, s), lambda hp, j: (hp, 0))
    blk = lambda: pl.BlockSpec((t, 128), lambda hp, j: (j, hp))
    return pl.pallas_call(
        body, name="fox_bwd", grid=(4, nq),
        in_specs=[res_t(), blk(), blk(), res_t(), res_t(), pl.BlockSpec((1, s, 128), lambda hp, j: (hp, 0, 0)),
                  pl.BlockSpec((1, 8, s), lambda hp, j: (hp, 0, 0))],
        out_specs=[res_t(), blk(), blk(),
                   pl.BlockSpec((1, t, 128), lambda hp, j: (hp, j, 0)),
                   pl.BlockSpec((1, 8, s), lambda hp, j: (hp, 0, 0))],
        out_shape=[jax.ShapeDtypeStruct((512, s), F32), jax.ShapeDtypeStruct((s, 512), F32),
                   jax.ShapeDtypeStruct((s, 512), F32), jax.ShapeDtypeStruct((4, s, 128), F32),
                   jax.ShapeDtypeStruct((4, 8, s), F32)],
        scratch_shapes=[pltpu.VMEM((8, s), F32)] + [pltpu.VMEM((128, t), F32)] * 4 + [pltpu.VMEM((t, tq), F32)] * 2,
        compiler_params=_params(("arbitrary", "arbitrary")),
    )(qt, k, v, dot, ot, cc4, lse4)


def _mem_bwd(qm, mk, mv, dom):
    s = qm.shape[0]
    tq = min(512, s)

    def body(q_ref, mk_ref, mv_ref, do_ref, dq_ref, dmk_ref, dmv_ref):
        i = pl.program_id(0)

        @pl.when(i == 0)
        def _():
            dmk_ref[...] = jnp.zeros_like(dmk_ref)
            dmv_ref[...] = jnp.zeros_like(dmv_ref)

        for h in range(MEM_HEADS):
            hs = slice(h * 128, (h + 1) * 128)
            qh = q_ref[:, hs]
            doh = do_ref[:, hs]
            sc = _dot(qh, mk_ref[:, hs], NT) * MEM_SCALE
            mx = jnp.max(sc, axis=-1, keepdims=True)
            p = jnp.exp(sc - mx)
            p = p / jnp.sum(p, axis=-1, keepdims=True)
            dp = _dot(doh, mv_ref[:, hs], NT)
            ds = p * (dp - jnp.sum(p * dp, axis=-1, keepdims=True))
            dsb = (ds * MEM_SCALE).astype(BF16)
            dq_ref[:, hs] = _dot(dsb, mk_ref[:, hs])
            dmk_ref[:, hs] += _dot(dsb, qh, TN)
            dmv_ref[:, hs] += _dot(p.astype(BF16), doh, TN)

    return pl.pallas_call(
        body, name="mem_bwd", grid=(s // tq,),
        in_specs=[pl.BlockSpec((tq, 512), lambda i: (i, 0)), _full(mk.shape), _full(mv.shape),
                  pl.BlockSpec((tq, 512), lambda i: (i, 0))],
        out_specs=[pl.BlockSpec((tq, 512), lambda i: (i, 0)), _full(mk.shape), _full(mv.shape)],
        out_shape=[jax.ShapeDtypeStruct((s, 512), F32), jax.ShapeDtypeStruct(mk.shape, F32),
                   jax.ShapeDtypeStruct(mv.shape, F32)],
        compiler_params=_params(("arbitrary",)),
    )(qm, mk, mv, dom)


def _memkv_bwd(dmk, dmv, kv_raw, kn_mem, mem, g_mem, mem_n, w_kv):
    def body(dmk_ref, dmv_ref, kv_ref, kn_ref, mem_ref, g_ref, mn_ref, w_ref, dw_ref, dkn_ref, dg_ref, dkv_ref):
        dkn = jnp.zeros((1, 128), F32)
        for h in range(MEM_HEADS):
            hs = slice(h * 128, (h + 1) * 128)
            v = kv_ref[:, hs]
            r = lax.rsqrt(jnp.mean(v * v, axis=-1, keepdims=True) + EPS)
            n = v * r
            dn = dmk_ref[:, hs]
            dkn = dkn + jnp.sum(dn * n, axis=0, keepdims=True)
            dng = dn * kn_ref[...]
            dkv_ref[:, hs] = (r * (dng - n * jnp.mean(dng * n, axis=-1, keepdims=True))).astype(BF16)
        dkv_ref[:, 512:1024] = dmv_ref[...].astype(BF16)
        dkn_ref[...] = dkn
        dkv = dkv_ref[...]
        dw_ref[...] = _dot(mn_ref[...], dkv, TN).astype(BF16)
        dmn = _dot(dkv, w_ref[...], NT)
        xv = mem_ref[...]
        r = lax.rsqrt(jnp.mean(xv * xv, axis=-1, keepdims=True) + EPS)
        dg_ref[...] = jnp.sum(dmn * (xv * r), axis=0, keepdims=True)

    m = mem.shape[0]
    return pl.pallas_call(
        body, name="memkv_bwd",
        out_shape=[jax.ShapeDtypeStruct((D_MODEL, 1024), BF16), jax.ShapeDtypeStruct((1, 128), F32),
                   jax.ShapeDtypeStruct((1, D_MODEL), F32)],
        scratch_shapes=[pltpu.VMEM((m, 1024), BF16)],
        compiler_params=pltpu.CompilerParams(vmem_limit_bytes=VMEM_LIMIT),
    )(dmk, dmv, kv_raw, kn_mem, mem, g_mem, mem_n, w_kv)


def _fox_gate_bwd(dc, proj, b_forget128):
    s = dc.shape[0]
    tm = min(512, s)
    nt = s // tm

    def body(dc_ref, p_ref, b_ref, dfl_ref, db_ref, carry_ref):
        i = pl.program_id(0)

        @pl.when(i == 0)
        def _():
            carry_ref[...] = jnp.zeros_like(carry_ref)
            db_ref[...] = jnp.zeros_like(db_ref)

        dcv = dc_ref[...]
        dlogf = jnp.dot(_tri(tm, False), dcv, precision=lax.Precision.HIGHEST, preferred_element_type=F32) + carry_ref[...]
        carry_ref[...] += jnp.sum(dcv, axis=0, keepdims=True)
        z = p_ref[...] + b_ref[...]
        dfl = dlogf * (1.0 / (1.0 + jnp.exp(z)))
        dfl_ref[...] = dfl.astype(BF16)
        db_ref[...] += jnp.sum(dfl, axis=0, keepdims=True)

    return pl.pallas_call(
        body, name="fox_gate_bwd", grid=(nt,),
        in_specs=[pl.BlockSpec((tm, 128), lambda i: (nt - 1 - i, 0)),
                  pl.BlockSpec((tm, 128), lambda i: (nt - 1 - i, 0)), _full((1, 128))],
        out_specs=[pl.BlockSpec((tm, 128), lambda i: (nt - 1 - i, 0)), _full((1, 128))],
        out_shape=[jax.ShapeDtypeStruct((s, 128), BF16), jax.ShapeDtypeStruct((1, 128), F32)],
        scratch_shapes=[pltpu.VMEM((1, 128), F32)],
        compiler_params=_params(("arbitrary",)),
    )(dc, proj, b_forget128)


def _proj_pre_bwd(dproj, proj, dqf, dkf, dvf, dqm, dqa, dka, dva, dfl, gq_fox, gk_fox, gq_mem, gq_swa, gk_swa):
    s = proj.shape[0]
    tm = min(256, s)

    def body(dp_in, p_ref, dqf_ref, dkf_ref, dvf_ref, dqm_ref, dqa_ref, dka_ref, dva_ref, dfl_ref,
             gqf, gkf, gqm, gqa, gka, dp_ref, dgn_ref):
        i = pl.program_id(0)

        @pl.when(i == 0)
        def _():
            dgn_ref[...] = jnp.zeros_like(dgn_ref)

        def norm_bwd(off, width, hd, g_ref, dn_ref, slot):
            acc = jnp.zeros((1, 128), F32)
            for b in range(width // 128):
                v = p_ref[:, off + b * 128: off + (b + 1) * 128].astype(F32)
                r = lax.rsqrt(_group_mean(v * v, hd) + EPS)
                n = v * r
                dn = dn_ref[:, b * 128:(b + 1) * 128]
                acc = acc + jnp.sum(dn * n, axis=0, keepdims=True)
                dng = dn * g_ref[...]
                dp_ref[:, off + b * 128: off + (b + 1) * 128] = (r * (dng - n * _group_mean(dng * n, hd))).astype(BF16)
            dgn_ref[slot:slot + 1, :] += acc

        norm_bwd(H_QF, 512, HEAD, gqf, dqf_ref, 0)
        norm_bwd(H_KF, 512, HEAD, gkf, dkf_ref, 1)
        dp_ref[:, H_VF:H_VF + 512] = dvf_ref[...].astype(BF16)
        norm_bwd(H_QM, 512, MEM_HEAD, gqm, dqm_ref, 2)
        norm_bwd(H_QA, 512, HEAD, gqa, dqa_ref, 3)
        norm_bwd(H_KA, 128, HEAD, gka, dka_ref, 4)
        dp_ref[:, H_VA:H_VA + 128] = dva_ref[...].astype(BF16)
        dp_ref[:, H_FL:H_FL + 128] = dfl_ref[...]
        dp_ref[:, H_FL + 128:HALF_W] = jnp.zeros((tm, HALF_W - H_FL - 128), BF16)

    row = lambda w: pl.BlockSpec((tm, w), lambda i: (i, 0))
    g_spec = _full((1, 128))
    return pl.pallas_call(
        body, name="proj_pre_bwd", grid=(s // tm,),
        in_specs=[pl.BlockSpec(memory_space=pl.ANY), pl.BlockSpec((tm, HALF_W), lambda i: (i, 1)),
                  row(512), row(512), row(512), row(512), row(512), row(128), row(128), row(128),
                  g_spec, g_spec, g_spec, g_spec, g_spec],
        out_specs=[pl.BlockSpec((tm, HALF_W), lambda i: (i, 1)), _full((8, 128))],
        out_shape=[jax.ShapeDtypeStruct((s, PROJ_W), BF16), jax.ShapeDtypeStruct((8, 128), F32)],
        input_output_aliases={0: 0},
        compiler_params=_params(("arbitrary",)),
    )(dproj, proj, dqf, dkf, dvf, dqm, dqa, dka, dva, dfl, gq_fox, gk_fox, gq_mem, gq_swa, gk_swa)


def _in_bwd_x(dproj, w_in_p, x, g_mix, dx1):
    s = x.shape[0]
    tm = min(256, s)

    def body(dp_ref, w_ref, x_ref, g_ref, dx1_ref, gx_ref, dg_ref):
        i = pl.program_id(0)

        @pl.when(i == 0)
        def _():
            dg_ref[...] = jnp.zeros_like(dg_ref)

        dx, dg = _rms_bwd(x_ref[...], g_ref[...], _dot(dp_ref[...], w_ref[...], NT), dx1_ref[...])
        gx_ref[...] = dx
        dg_ref[...] += dg

    row = pl.BlockSpec((tm, D_MODEL), lambda i: (i, 0))
    return pl.pallas_call(
        body, name="in_bwd_x", grid=(s // tm,),
        in_specs=[pl.BlockSpec((tm, PROJ_W), lambda i: (i, 0)), _full(w_in_p.shape), row, _full((1, D_MODEL)), row],
        out_specs=[row, _full((1, D_MODEL))],
        out_shape=[jax.ShapeDtypeStruct((s, D_MODEL), F32), jax.ShapeDtypeStruct((1, D_MODEL), F32)],
        compiler_params=_params(("arbitrary",)),
    )(dproj, w_in_p, x, g_mix, dx1)


def _rel_bias_bwd(dbias, bucket):
    def body(db_ref, bk_ref, o_ref):
        bk = bk_ref[...]
        lane = lax.broadcasted_iota(jnp.int32, (1, 128), 1)
        for b in range(REL_BUCKETS):
            sel = bk == b
            acc = jnp.zeros((1, 128), F32)
            for h in range(SWA_HEADS):
                tot = jnp.sum(jnp.sum(jnp.where(sel, db_ref[h], 0.0), axis=-1, keepdims=True), axis=0, keepdims=True)
                acc = jnp.where(lane == h, tot, acc)
            o_ref[b:b + 1, :] = acc

    return pl.pallas_call(
        body, name="rel_bias_bwd",
        out_shape=jax.ShapeDtypeStruct((REL_BUCKETS, 128), F32),
        compiler_params=pltpu.CompilerParams(vmem_limit_bytes=VMEM_LIMIT),
    )(dbias, bucket)


def _my_place():
    return lax.axis_index("x"), lax.axis_index("y"), lax.axis_index("c")


def _peer(place, k):
    x, y, c = place
    return (1 - x if k & 4 else x, 1 - y if k & 2 else y, 1 - c if k & 1 else c)


def _index(place):
    x, y, c = place
    return 4 * x + 2 * y + c


HBM_SPEC = pl.BlockSpec(memory_space=pltpu.HBM)
SEM_SPEC = pl.BlockSpec(memory_space=pltpu.SEMAPHORE)
DATAFLOW = pltpu.SideEffectType.DATAFLOW_SIDE_EFFECTING


ALL_PEERS = tuple(range(1, N_DEV))
SAME_CORE = (2, 4, 6)
OWN = N_DEV - 1


def _split_copy(src_ref, land_ref, send_sems, recv_sems, me, k, gather):
    peer = _peer(me, k)
    if gather:
        src, dst = src_ref, land_ref.at[_index(me)]
    else:
        src, dst = src_ref.at[_index(peer)], land_ref.at[k - 1]
    return pltpu.make_async_remote_copy(src_ref=src, dst_ref=dst, send_sem=send_sems.at[k - 1], recv_sem=recv_sems.at[k - 1],
                                        device_id=peer, device_id_type=MESH)


def _own_copy(src_ref, land_ref, recv_sems, me, gather):
    if gather:
        src, dst = src_ref, land_ref.at[_index(me)]
    else:
        src, dst = src_ref.at[_index(me)], land_ref.at[OWN]
    return pltpu.make_async_copy(src, dst, recv_sems.at[OWN])


def _split_start(srcs, gather, name, peers=ALL_PEERS, after=None):
    n = len(srcs)
    extra = [] if after is None else [after]

    def body(*refs):
        refs = refs[:2 * n] + refs[2 * n + len(extra):]
        src_refs, land_refs = refs[:n], refs[n:2 * n]
        send_sems, recv_sems, token = refs[2 * n:3 * n], refs[3 * n:4 * n], refs[-1]
        me = _my_place()
        for w in range(n):
            for k in peers:
                _split_copy(src_refs[w], land_refs[w], send_sems[w], recv_sems[w], me, k, gather).start()
            _own_copy(src_refs[w], land_refs[w], recv_sems[w], me, gather).start()
        token[...] = jnp.zeros_like(token)

    lands = [lax.empty((N_DEV,) + (a.shape if gather else a.shape[1:]), a.dtype) for a in srcs]
    sems = [pltpu.SemaphoreType.DMA((N_DEV,))] * (2 * n)
    hbm = [pltpu.HBM(a.shape, a.dtype) for a in list(srcs) + lands]
    outs = pl.pallas_call(
        body, name=name,
        out_shape=(*sems, *hbm, jax.ShapeDtypeStruct((8, 128), F32)),
        in_specs=(HBM_SPEC,) * (2 * n) + (pl.BlockSpec(memory_space=pl.ANY),) * len(extra),
        out_specs=(SEM_SPEC,) * (2 * n) + (HBM_SPEC,) * (2 * n) + (pl.BlockSpec(memory_space=pltpu.VMEM),),
        input_output_aliases={i: 2 * n + i for i in range(2 * n)},
        compiler_params=pltpu.CompilerParams(has_side_effects=DATAFLOW),
    )(*[pltpu.with_memory_space_constraint(a, pltpu.HBM) for a in list(srcs) + lands], *extra)
    return list(outs[:n]), list(outs[n:2 * n]), list(outs[2 * n:3 * n]), list(outs[3 * n:4 * n]), outs[-1]


def _split_wait(started, w, after, gather, name):
    send_sems, recv_sems, srcs, lands, _ = started

    def body(src_ref, land_ref, send_sems, recv_sems, after_ref, src_out, land_out):
        me = _my_place()
        for k in ALL_PEERS:
            cp = _split_copy(src_ref, land_ref, send_sems, recv_sems, me, k, gather)
            cp.wait_send()
            cp.wait_recv()
        _own_copy(src_ref, land_ref, recv_sems, me, gather).wait()

    return pl.pallas_call(
        body, name=name,
        out_shape=(pltpu.HBM(srcs[w].shape, srcs[w].dtype), pltpu.HBM(lands[w].shape, lands[w].dtype)),
        in_specs=(HBM_SPEC, HBM_SPEC, SEM_SPEC, SEM_SPEC, pl.BlockSpec(memory_space=pl.ANY)),
        out_specs=(HBM_SPEC, HBM_SPEC), input_output_aliases={0: 0, 1: 1},
        compiler_params=pltpu.CompilerParams(has_side_effects=DATAFLOW),
    )(srcs[w], lands[w], send_sems[w], recv_sems[w], after)[1]


def _forward_copy(land_ref, send_sems, recv_sems, me, j, incoming):
    sibling = _peer(me, 1)
    rows = land_ref.at[_index(_peer(sibling if incoming else me, SAME_CORE[j]))]
    return pltpu.make_async_remote_copy(src_ref=rows, dst_ref=rows, send_sem=send_sems.at[j], recv_sem=recv_sems.at[j],
                                        device_id=sibling, device_id_type=MESH)


def _forward_start(started, after, name):
    send_a, recv_a, srcs, lands, _ = started

    def body(src_ref, land_ref, send_a, recv_a, after_ref, send_b, recv_b, src_out, land_out):
        me = _my_place()
        for j, k in enumerate(SAME_CORE):
            _split_copy(src_ref, land_ref, send_a, recv_a, me, k, True).wait_recv()
            _forward_copy(land_ref, send_b, recv_b, me, j, False).start()

    sems = pltpu.SemaphoreType.DMA((len(SAME_CORE),))
    return pl.pallas_call(
        body, name=name,
        out_shape=(sems, sems, pltpu.HBM(srcs[0].shape, srcs[0].dtype), pltpu.HBM(lands[0].shape, lands[0].dtype)),
        in_specs=(HBM_SPEC, HBM_SPEC, SEM_SPEC, SEM_SPEC, pl.BlockSpec(memory_space=pl.ANY)),
        out_specs=(SEM_SPEC, SEM_SPEC, HBM_SPEC, HBM_SPEC), input_output_aliases={0: 2, 1: 3},
        compiler_params=pltpu.CompilerParams(has_side_effects=DATAFLOW),
    )(srcs[0], lands[0], send_a[0], recv_a[0], after)


def _forward_wait(started, forwarded, name):
    send_a, recv_a, _, _, _ = started
    send_b, recv_b, src, land = forwarded

    def body(src_ref, land_ref, send_a, recv_a, send_b, recv_b, src_out, land_out):
        me = _my_place()
        _own_copy(src_ref, land_ref, recv_a, me, True).wait()
        for k in (1,) + SAME_CORE:
            _split_copy(src_ref, land_ref, send_a, recv_a, me, k, True).wait_send()
        _split_copy(src_ref, land_ref, send_a, recv_a, me, 1, True).wait_recv()
        for j in range(len(SAME_CORE)):
            _forward_copy(land_ref, send_b, recv_b, me, j, False).wait_send()
            _forward_copy(land_ref, send_b, recv_b, me, j, True).wait_recv()

    return pl.pallas_call(
        body, name=name,
        out_shape=(pltpu.HBM(src.shape, src.dtype), pltpu.HBM(land.shape, land.dtype)),
        in_specs=(HBM_SPEC, HBM_SPEC, SEM_SPEC, SEM_SPEC, SEM_SPEC, SEM_SPEC),
        out_specs=(HBM_SPEC, HBM_SPEC), input_output_aliases={0: 0, 1: 1},
        compiler_params=pltpu.CompilerParams(has_side_effects=DATAFLOW),
    )(src, land, send_a[0], recv_a[0], send_b, recv_b)[1]


def _adam_math(w, g, m, v):
    m2 = ADAM_B1 * m + (1.0 - ADAM_B1) * g
    v2 = ADAM_B2 * v + (1.0 - ADAM_B2) * (g * g)
    m_hat = m2 / (1.0 - ADAM_B1 ** ADAM_STEP)
    v_hat = v2 / (1.0 - ADAM_B2 ** ADAM_STEP)
    delta = -ADAM_LR * (m_hat / (jnp.sqrt(v_hat) + ADAM_EPS) + ADAM_WD * w)
    return delta, m2, v2


def _adamw(land, w, m, v, name):
    a, b = w.shape
    bp = land.shape[2]
    ta = min(128, a)

    def body(p_ref, w_ref, m_ref, v_ref, g_ref, d_ref, m2_ref, v2_ref):
        g = p_ref[0, :, 0:b].astype(F32)
        for k in range(1, N_DEV):
            g = g + p_ref[k, :, 0:b].astype(F32)
        delta, m2, v2 = _adam_math(w_ref[...], g, m_ref[...], v_ref[...])
        g_ref[...] = g
        d_ref[...] = delta
        m2_ref[...] = m2
        v2_ref[...] = v2

    blk = pl.BlockSpec((ta, b), lambda i: (i, 0))
    sd = jax.ShapeDtypeStruct((a, b), F32)
    return pl.pallas_call(
        body, name=name, grid=(a // ta,),
        in_specs=[pl.BlockSpec((N_DEV, ta, bp), lambda i: (0, i, 0)), blk, blk, blk],
        out_specs=[blk, blk, blk, blk], out_shape=[sd, sd, sd, sd],
        compiler_params=_params(("parallel",)),
    )(land, w, m, v)


def _bucket_table():
    t_loc = jnp.arange(SWA_BLOCK)[:, None] + SWA_BLOCK
    s_loc = jnp.arange(2 * SWA_BLOCK)[None, :]
    dist = t_loc - s_loc
    max_exact = REL_BUCKETS // 2
    d = jnp.maximum(dist, 0)
    df = jnp.maximum(d, 1).astype(F32)
    large = max_exact + (jnp.log(df / max_exact) / math.log(REL_MAX_DIST / max_exact) * (REL_BUCKETS - max_exact)).astype(jnp.int32)
    large = jnp.minimum(large, REL_BUCKETS - 1)
    bucket = jnp.where(d < max_exact, d, large)
    band = (dist >= 0) & (dist < SWA_BLOCK)
    return bucket, band


def _tile2(g):
    return jnp.concatenate([g, g], axis=1) if g.shape[1] == HEAD else g


SHARD_W = 737
SHARD_WP = 768
IN_WIDTH = N_DEV * SHARD_W
SEGMENTS = ((GL0, 2824, 3072), (QF0, 768, 512), (KF0, 1280, 512), (VF0, 1792, 512), (QM0, 2312, 512),
            (QA0, 0, 512), (KA0, 512, 128), (VA0, 640, 128), (FL0, 2304, 8))


def _lane_plan(sources):
    plan = []
    for t in range(len(sources) // 128):
        groups = {}
        for lane in range(128):
            src = sources[128 * t + lane]
            if src is not None:
                slab, col = src
                groups.setdefault((slab, col // 128, (lane - col) % 128), []).append(lane)
        tile = []
        for key, lanes in groups.items():
            assert lanes == list(range(lanes[0], lanes[-1] + 1))
            tile.append((key, lanes[0], lanes[-1] + 1))
        plan.append(tile)
    return plan


def _assemble(tile_plan, load, rows):
    lane = lax.broadcasted_iota(jnp.int32, (1, 128), 1)
    out = jnp.zeros((rows, 128), F32)
    for (slab, st, roll), lo, hi in tile_plan:
        v = load(slab, st)
        if roll:
            v = pltpu.roll(v, roll, 1)
        out = v if (lo, hi) == (0, 128) else jnp.where((lane >= lo) & (lane < hi), v, out)
    return out


def _w_in_from_shards(land):
    ref_col = [None] * PROJ_W
    for p0, r0, n in SEGMENTS:
        for i in range(n):
            ref_col[p0 + i] = divmod(r0 + i, SHARD_W)
    plan = _lane_plan(ref_col)
    d_model = land.shape[1]
    tm = 256

    def body(land_ref, o_ref):
        load = lambda slab, st: land_ref[slab, :, st * 128:(st + 1) * 128].astype(F32)
        for t, tile_plan in enumerate(plan):
            o_ref[:, t * 128:(t + 1) * 128] = _assemble(tile_plan, load, tm).astype(BF16)

    return pl.pallas_call(
        body, name="w_in_from_shards", grid=(d_model // tm,),
        in_specs=[pl.BlockSpec((N_DEV, tm, SHARD_WP), lambda i: (0, i, 0))],
        out_specs=pl.BlockSpec((tm, PROJ_W), lambda i: (i, 0)),
        out_shape=jax.ShapeDtypeStruct((d_model, PROJ_W), BF16),
        compiler_params=_params(("parallel",)),
    )(land)


def _dw_in_to_parts(dwp):
    padded_col = [None] * IN_WIDTH
    for p0, r0, n in SEGMENTS:
        for i in range(n):
            padded_col[r0 + i] = p0 + i
    sources = []
    for d in range(N_DEV):
        sources += [(0, padded_col[SHARD_W * d + c]) if c < SHARD_W else None for c in range(SHARD_WP)]
    plan = _lane_plan(sources)
    d_model = dwp.shape[0]
    tm = 256
    tiles = SHARD_WP // 128

    def body(dw_ref, o_ref):
        load = lambda slab, st: dw_ref[:, st * 128:(st + 1) * 128].astype(F32)
        for t, tile_plan in enumerate(plan):
            d, c = divmod(t, tiles)
            o_ref[d, :, c * 128:(c + 1) * 128] = _assemble(tile_plan, load, tm).astype(BF16)

    return pl.pallas_call(
        body, name="dw_in_to_parts", grid=(d_model // tm,),
        in_specs=[pl.BlockSpec((tm, PROJ_W), lambda i: (i, 0))],
        out_specs=pl.BlockSpec((N_DEV, tm, SHARD_WP), lambda i: (0, i, 0)),
        out_shape=jax.ShapeDtypeStruct((N_DEV, d_model, SHARD_WP), BF16),
        compiler_params=_params(("parallel",)),
    )(dwp)


def _cast_shards(shards):
    names = list(shards)

    def body(*refs):
        for src, dst in zip(refs[:len(names)], refs[len(names):]):
            if dst.shape != src.shape:
                dst[...] = jnp.zeros(dst.shape, BF16)
                dst[:, 0:src.shape[1]] = src[...].astype(BF16)
            else:
                dst[...] = src[...].astype(BF16)

    out_shape = [jax.ShapeDtypeStruct((shards[n].shape[0], SHARD_WP if n == "w_in" else shards[n].shape[1]), BF16)
                 for n in names]
    outs = pl.pallas_call(body, name="cast_shards", out_shape=out_shape,
                          compiler_params=pltpu.CompilerParams(vmem_limit_bytes=VMEM_LIMIT))(*[shards[n] for n in names])
    return dict(zip(names, outs))


def _tie(x, *tokens):
    for t in tokens:
        if t is not None:
            x = x + t[0:1, 0:1]
    return x


def _local_step(x, mem, target, p, getw, emit, deps=()):
    s = x.shape[0]
    bucket, band = _bucket_table()
    bucket_m = jnp.where(band, bucket, -1).astype(jnp.int32)
    bias = _bias_table(p["rel_bias"], bucket_m)
    bucket_t = jnp.transpose(bucket_m)
    bias_t = _bias_table(p["rel_bias"], bucket_t)
    gqf, gkf, gqa, gka = _tile2(p["qn_fox"]), _tile2(p["kn_fox"]), _tile2(p["qn_swa"]), _tile2(p["kn_swa"])
    gqm = p["qn_mem"]
    bf128 = jnp.pad(p["b_forget"], ((0, 0), (0, 120)))
    sink = p["sink_swa"].reshape(8)

    h = _rms_fwd(x, p["g_mix"], "rms_mix", deps)
    w_in = getw("w_in", h)
    proj = _mm(h, w_in, "nn", BF16, 512, 1536, 1024, "proj")
    fl = _mm(h, w_in[:, FL0:FL0 + 128], "nn", F32, 512, 128, 1024, "proj_fl")
    qf, kf, vf, qm, qa, ka, va = _proj_post(proj, gqf, gkf, gqm, gqa, gka)
    cc4 = _fox_gate_fwd(fl, bf128)
    w_kv = getw("w_mem_kv", cc4)
    mem_n, kv_raw, mk, mv = _memkv_fwd(mem, p["g_mem"], w_kv, p["kn_mem"])
    kp = jnp.pad(ka, ((SWA_BLOCK, 0), (0, 0)))
    vp = jnp.pad(va, ((SWA_BLOCK, 0), (0, 0)))
    oa = _swa_fwd(qa, kp, vp, bias, sink)
    of, lse4 = _fox_fwd(qf, kf, jnp.transpose(vf), cc4)
    om = _mem_fwd(qm, mk, mv)
    wa, wf, wm, w_out = getw("w_o_swa", oa), getw("w_o_fox", oa), getw("w_o_mem", oa), getw("w_out", oa)
    x1, hm, merged = _merge_fwd(x, oa, of, om, proj, p["b_gate"], wa, wf, wm, w_out, p["g_mlp"])
    w_up = getw("w_mlp_up", of)
    u = _mlp_up(hm, w_up)
    w_down = getw("w_mlp_down", hm)
    dy, dy_b, loss = _mlp_down_loss(u, w_down, x1, target)

    da = _mlp_bwd_act(dy_b, w_down, u)
    t_down = emit({"w_mlp_down": _mm(u, dy_b, "tn", BF16, 1024, 1024, 2048, "dw_down")})
    dx1, dg_mlp = _mlp_bwd_x(da, w_up, x1, dy, _tie(p["g_mlp"], t_down))
    t_up = emit({"w_mlp_up": _mm(hm, da, "tn", BF16, 1024, 1024, 2048, "dw_up", column_chunks=True)})
    dproj, doa, dof, dom, dya, dyf, dym, db_gate = _merge_bwd(
        dx1, oa, of, om, proj, _tie(p["b_gate"], t_up), wa, wf, wm, w_out)
    t_o = emit({"w_out": _mm(merged, dx1, "tn", BF16, 1024, 1024, 2048, "dw_out"),
                "w_o_swa": _mm(oa, dya, "tn", BF16, 512, 1024, 2048, "dw_o_swa"),
                "w_o_fox": _mm(of, dyf, "tn", BF16, 512, 1024, 2048, "dw_o_fox"),
                "w_o_mem": _mm(om, dym, "tn", BF16, 512, 1024, 2048, "dw_o_mem")})

    dqm, dmk, dmv = _mem_bwd(qm, mk, mv, dom)
    dw_kv, dkn_mem, dg_mem = _memkv_bwd(dmk, dmv, kv_raw, _tie(p["kn_mem"], t_o), mem, p["g_mem"], mem_n, w_kv)
    t_kv = emit({"w_mem_kv": dw_kv})
    dqa, dkp, dvp, dbias, dsink = _swa_bwd(qa, kp, vp, bias_t, _tie(p["sink_swa"], t_kv).reshape(8), doa)
    qf_t = jnp.transpose(qf * jnp.asarray(0.125, BF16))
    dqf_t, dkf, dvf, dck4, dcq4 = _fox_bwd(qf_t, kf, vf, jnp.transpose(dof), jnp.transpose(of), cc4, lse4)
    dqf = jnp.transpose(dqf_t)

    dcq = jnp.transpose(dcq4[:, 0:2, :], (2, 0, 1)).reshape(s, 8)
    dck = jnp.transpose(dck4[:, :, 0:2], (1, 0, 2)).reshape(s, 8)
    dc = jnp.pad(dcq - dck, ((0, 0), (0, 120)))
    dfl, db_forget = _fox_gate_bwd(dc, fl, bf128)

    dproj, dgn = _proj_pre_bwd(dproj, proj, dqf, dkf, dvf, dqm, dqa, dkp[SWA_BLOCK:], dvp[SWA_BLOCK:], dfl,
                               gqf, gkf, gqm, gqa, gka)
    t_in = emit({"w_in": _mm(h, dproj, "tn", BF16, 1024, 3072, 1024, "dw_in")})
    grad_x, dg_mix = _in_bwd_x(dproj, w_in, x, _tie(p["g_mix"], t_in), dx1)
    d_rel = _rel_bias_bwd(dbias, bucket_t)

    fold = lambda r: dgn[r:r + 1, 0:HEAD] + dgn[r:r + 1, HEAD:128]
    small = {
        "g_mix": dg_mix, "b_gate": db_gate, "b_forget": db_forget[:, 0:8],
        "qn_swa": fold(3), "kn_swa": fold(4), "sink_swa": dsink[:, 0].reshape(1, 8), "rel_bias": d_rel[:, 0:8],
        "qn_fox": fold(0), "kn_fox": fold(1), "g_mem": dg_mem, "qn_mem": dgn[2:3, :], "kn_mem": dkn_mem,
        "g_mlp": dg_mlp,
    }
    return loss, grad_x, small


SMALL = ("g_mix", "b_gate", "b_forget", "qn_swa", "kn_swa", "sink_swa", "rel_bias", "qn_fox", "kn_fox", "g_mem",
         "qn_mem", "kn_mem", "g_mlp")
BIG = ("w_in", "w_mem_kv", "w_o_swa", "w_o_fox", "w_o_mem", "w_out", "w_mlp_up", "w_mlp_down")
COL_SHARDED = ("w_in", "w_o_swa", "w_o_fox", "w_o_mem", "w_mlp_up")
WEIGHTS = ("g_mix", "w_in", "b_gate", "b_forget", "qn_swa", "kn_swa", "sink_swa", "rel_bias", "qn_fox", "kn_fox", "g_mem",
           "w_mem_kv", "qn_mem", "kn_mem", "w_o_swa", "w_o_fox", "w_o_mem", "w_out", "g_mlp", "w_mlp_up", "w_mlp_down")
SMALL_USED = 6928
SMALL_PAD = 7168


def _gathered_to_full(name, g):
    if name in COL_SHARDED:
        return jnp.transpose(g, (1, 0, 2)).reshape(g.shape[1], N_DEV * g.shape[2])
    return g.reshape(N_DEV * g.shape[1], g.shape[2])


def _full_to_parts(name, full, b):
    if name in COL_SHARDED:
        return jnp.transpose(full.reshape(full.shape[0], N_DEV, b), (1, 0, 2)).astype(BF16)
    return full.reshape(N_DEV, full.shape[0] // N_DEV, full.shape[1]).astype(BF16)


def _pack_small(d, loss=None):
    flat = jnp.concatenate([d[n].reshape(-1) for n in SMALL])
    assert flat.shape[0] == SMALL_USED
    if loss is not None:
        flat = jnp.concatenate([flat, loss.reshape(-1)])
    return jnp.pad(flat, (0, SMALL_PAD - flat.shape[0])).reshape(8, SMALL_PAD // 8)


def _unpack_small(packed, like):
    flat = packed.reshape(-1)
    out, off = {}, 0
    for n in SMALL:
        size = like[n].size
        out[n] = flat[off:off + size].reshape(like[n].shape)
        off += size
    return out


def _adamw_small(parts, w, m, v):
    def body(p_ref, w_ref, m_ref, v_ref, g_ref, d_ref, m2_ref, v2_ref):
        g = p_ref[0]
        for k in range(1, N_DEV):
            g = g + p_ref[k]
        delta, m2, v2 = _adam_math(w_ref[...], g, m_ref[...], v_ref[...])
        g_ref[...] = g
        d_ref[...] = delta
        m2_ref[...] = m2
        v2_ref[...] = v2

    sd = jax.ShapeDtypeStruct(w.shape, F32)
    return pl.pallas_call(body, name="adamw_small", out_shape=[sd, sd, sd, sd])(parts, w, m, v)


def kernel(x, mem, g_mix, w_in, b_gate, b_forget, qn_swa, kn_swa, sink_swa, rel_bias, qn_fox, kn_fox, g_mem, w_mem_kv, qn_mem, kn_mem, w_o_swa, w_o_fox, w_o_mem, w_out, g_mlp, w_mlp_up, w_mlp_down, loss_target, m_g_mix, m_w_in, m_b_gate, m_b_forget, m_qn_swa, m_kn_swa, m_sink_swa, m_rel_bias, m_qn_fox, m_kn_fox, m_g_mem, m_w_mem_kv, m_qn_mem, m_kn_mem, m_w_o_swa, m_w_o_fox, m_w_o_mem, m_w_out, m_g_mlp, m_w_mlp_up, m_w_mlp_down, v_g_mix, v_w_in, v_b_gate, v_b_forget, v_qn_swa, v_kn_swa, v_sink_swa, v_rel_bias, v_qn_fox, v_kn_fox, v_g_mem, v_w_mem_kv, v_qn_mem, v_kn_mem, v_w_o_swa, v_w_o_fox, v_w_o_mem, v_w_out, v_g_mlp, v_w_mlp_up, v_w_mlp_down):
    wts = dict(g_mix=g_mix, w_in=w_in, b_gate=b_gate, b_forget=b_forget, qn_swa=qn_swa, kn_swa=kn_swa, sink_swa=sink_swa,
               rel_bias=rel_bias, qn_fox=qn_fox, kn_fox=kn_fox, g_mem=g_mem, w_mem_kv=w_mem_kv, qn_mem=qn_mem, kn_mem=kn_mem,
               w_o_swa=w_o_swa, w_o_fox=w_o_fox, w_o_mem=w_o_mem, w_out=w_out, g_mlp=g_mlp, w_mlp_up=w_mlp_up,
               w_mlp_down=w_mlp_down)
    mom = dict(g_mix=m_g_mix, w_in=m_w_in, b_gate=m_b_gate, b_forget=m_b_forget, qn_swa=m_qn_swa, kn_swa=m_kn_swa,
               sink_swa=m_sink_swa, rel_bias=m_rel_bias, qn_fox=m_qn_fox, kn_fox=m_kn_fox, g_mem=m_g_mem, w_mem_kv=m_w_mem_kv,
               qn_mem=m_qn_mem, kn_mem=m_kn_mem, w_o_swa=m_w_o_swa, w_o_fox=m_w_o_fox, w_o_mem=m_w_o_mem, w_out=m_w_out,
               g_mlp=m_g_mlp, w_mlp_up=m_w_mlp_up, w_mlp_down=m_w_mlp_down)
    var = dict(g_mix=v_g_mix, w_in=v_w_in, b_gate=v_b_gate, b_forget=v_b_forget, qn_swa=v_qn_swa, kn_swa=v_kn_swa,
               sink_swa=v_sink_swa, rel_bias=v_rel_bias, qn_fox=v_qn_fox, kn_fox=v_kn_fox, g_mem=v_g_mem, w_mem_kv=v_w_mem_kv,
               qn_mem=v_qn_mem, kn_mem=v_kn_mem, w_o_swa=v_w_o_swa, w_o_fox=v_w_o_fox, w_o_mem=v_w_o_mem, w_out=v_w_out,
               g_mlp=v_g_mlp, w_mlp_up=v_w_mlp_up, w_mlp_down=v_w_mlp_down)

    shards = _cast_shards({n: wts[n][0] for n in BIG})
    first = _split_start([shards["w_in"]], True, "ag_start_w_in", peers=(1,) + SAME_CORE)
    rest = _split_start([shards[n] for n in BIG[1:]], True, "ag_start_rest", after=first[4])
    full = {}

    def getw(n, after):
        if n == "w_in" and n not in full:
            forwarded = _forward_start(first, after, "ag_forward_w_in")
            full[n] = _w_in_from_shards(_forward_wait(first, forwarded, "ag_wait_w_in"))
        elif n not in full:
            land = _split_wait(rest, BIG[1:].index(n), after, True, "ag_wait_" + n)
            full[n] = land if n == "w_mlp_up" else _gathered_to_full(n, land)
        return full[n]

    exchanges = {}

    def emit(grads_by_name):
        parts = []
        for n, grad in grads_by_name.items():
            if n == "w_in":
                parts.append(_dw_in_to_parts(grad))
            else:
                parts.append(grad if n == "w_mlp_up" else _full_to_parts(n, grad, wts[n].shape[2]))
        started = _split_start(parts, False, "rs_start_" + next(iter(grads_by_name)))
        for w, n in enumerate(grads_by_name):
            exchanges[n] = (started, w)
        return started[4]

    small_p = {n: wts[n] for n in SMALL}
    loss, grad_x, small_g = _local_step(x[0], mem[0], loss_target[0], small_p, getw, emit, (first[4], rest[4]))

    packed = _pack_small(small_g, loss)
    small_gather = _split_start([packed], True, "ag_start_small")

    grads, delta, new_m, new_v = {}, {}, {}, {}

    def update(n, after):
        land = _split_wait(*exchanges[n], after, False, "rs_wait_" + n)
        g, d, m2, v2 = _adamw(land, wts[n][0], mom[n][0], var[n][0], "adamw_" + n)
        grads[n], delta[n], new_m[n], new_v[n] = g[None], d[None], m2[None], v2[None]
        return d

    after = small_gather[4]
    for n in exchanges:
        if n != "w_in":
            after = update(n, after)

    gathered = _split_wait(small_gather, 0, after, True, "ag_wait_small")
    g, d, m2, v2 = _adamw_small(gathered, _pack_small(small_p), _pack_small({n: mom[n] for n in SMALL}),
                                _pack_small({n: var[n] for n in SMALL}))
    for dst, flat in ((grads, g), (delta, d), (new_m, m2), (new_v, v2)):
        dst.update(_unpack_small(flat, small_p))
    total = g.reshape(-1)[SMALL_USED]
    update("w_in", d)

    return (total, grad_x[None], *[grads[n] for n in WEIGHTS], *[delta[n] for n in WEIGHTS],
            *[new_m[n] for n in WEIGHTS], *[new_v[n] for n in WEIGHTS])
```

```python
import functools
import math

import jax
import jax.numpy as jnp
from jax import lax
from jax.experimental import pallas as pl
from jax.experimental.pallas import tpu as pltpu

F32 = jnp.float32
BF16 = jnp.bfloat16

D_MODEL = 1024
N_MEM = 256
D_FF = 4096
HEAD = 64
SWA_HEADS = 8
SWA_BLOCK = 128
MEM_HEADS = 4
MEM_HEAD = 128
EPS = 1e-6
NEG = -1e30
REL_BUCKETS = 32
REL_MAX_DIST = 128

ADAM_LR = 0.001
ADAM_B1 = 0.9
ADAM_B2 = 0.999
ADAM_EPS = 1e-08
ADAM_WD = 0.01
ADAM_STEP = 10

GL0, QF0, KF0, VF0, QM0, QA0, KA0, VA0, FL0 = 0, 3072, 3584, 4096, 4608, 5120, 5632, 5760, 5888
PROJ_W = 6144
HALF_W = 3072
H_QF, H_KF, H_VF, H_QM, H_QA, H_KA, H_VA, H_FL = 0, 512, 1024, 1536, 2048, 2560, 2688, 2816

VMEM_LIMIT = 56 * 1024 * 1024
N_DEV = 8
MESH = pl.DeviceIdType.MESH

NN = (((1,), (0,)), ((), ()))
NT = (((1,), (1,)), ((), ()))
TN = (((0,), (0,)), ((), ()))


def _dot(a, b, dims=NN):
    return lax.dot_general(a, b, dims, preferred_element_type=F32)


def _params(sem):
    return pltpu.CompilerParams(dimension_semantics=sem, vmem_limit_bytes=VMEM_LIMIT)


def _full(shape):
    nd = len(shape)
    return pl.BlockSpec(shape, lambda *_: (0,) * nd)


def _sigmoid(z):
    return 1.0 / (1.0 + jnp.exp(-z))


def _group_mean(v, hd):
    if hd == 128:
        return jnp.mean(v, axis=-1, keepdims=True)
    lane = lax.broadcasted_iota(jnp.int32, v.shape, 1)
    lo = lane < HEAD
    s_lo = jnp.sum(jnp.where(lo, v, 0.0), axis=-1, keepdims=True)
    s_hi = jnp.sum(jnp.where(lo, 0.0, v), axis=-1, keepdims=True)
    return jnp.where(lo, s_lo, s_hi) * (1.0 / HEAD)


def _mm(a, b, mode, out_dtype, tm, tn, tk, name, column_chunks=False):
    if mode == "nn":
        m, k = a.shape
        n = b.shape[1]
    elif mode == "nt":
        m, k = a.shape
        n = b.shape[0]
    else:
        k, m = a.shape
        n = b.shape[1]
    tm, tn, tk = min(tm, m), min(tn, n), min(tk, k)
    nk = k // tk
    chunk = n // N_DEV
    per_tile = tn // chunk if column_chunks else 1
    dims = {"nn": NN, "nt": NT, "tn": TN}[mode]
    a_spec = pl.BlockSpec((tk, tm), lambda j, i, kk: (kk, i)) if mode == "tn" else pl.BlockSpec((tm, tk), lambda j, i, kk: (i, kk))
    b_spec = pl.BlockSpec((tn, tk), lambda j, i, kk: (j, kk)) if mode == "nt" else pl.BlockSpec((tk, tn), lambda j, i, kk: (kk, j))

    def body(a_ref, b_ref, o_ref, *acc):
        prod = _dot(a_ref[...].astype(BF16), b_ref[...].astype(BF16), dims)

        def write(res):
            if column_chunks:
                for c in range(per_tile):
                    o_ref[c] = res[:, c * chunk:(c + 1) * chunk].astype(o_ref.dtype)
            else:
                o_ref[...] = res.astype(o_ref.dtype)

        if nk == 1:
            write(prod)
        else:
            acc_ref, = acc
            kk = pl.program_id(2)

            @pl.when(kk == 0)
            def _():
                acc_ref[...] = prod

            @pl.when(kk > 0)
            def _():
                acc_ref[...] += prod

            @pl.when(kk == nk - 1)
            def _():
                write(acc_ref[...])

    return pl.pallas_call(
        body, name=name, grid=(n // tn, m // tm, nk),
        in_specs=[a_spec, b_spec],
        out_specs=(pl.BlockSpec((per_tile, tm, chunk), lambda j, i, kk: (j, i, 0)) if column_chunks
                   else pl.BlockSpec((tm, tn), lambda j, i, kk: (i, j))),
        out_shape=jax.ShapeDtypeStruct((N_DEV, m, chunk) if column_chunks else (m, n), out_dtype),
        scratch_shapes=[pltpu.VMEM((tm, tn), F32)] if nk > 1 else [],
        compiler_params=_params(("parallel", "parallel", "arbitrary")),
    )(a, b)


def _rms_fwd(x, g, name, deps=()):
    s, d = x.shape
    tm = min(512, s)

    def body(x_ref, g_ref, *rest):
        h_ref = rest[len(deps)]
        xv = x_ref[...]
        r = lax.rsqrt(jnp.mean(xv * xv, axis=-1, keepdims=True) + EPS)
        h_ref[...] = (xv * r * g_ref[...]).astype(BF16)

    return pl.pallas_call(
        body, name=name, grid=(s // tm,),
        in_specs=[pl.BlockSpec((tm, d), lambda i: (i, 0)), _full((1, d))] + [pl.BlockSpec(memory_space=pl.ANY)] * len(deps),
        out_specs=pl.BlockSpec((tm, d), lambda i: (i, 0)),
        out_shape=jax.ShapeDtypeStruct((s, d), BF16),
        compiler_params=_params(("parallel",)),
    )(x, g, *deps)


def _proj_post(proj, gq_fox, gk_fox, gq_mem, gq_swa, gk_swa):
    s = proj.shape[0]
    tm = min(256, s)

    def body(p_ref, gqf, gkf, gqm, gqa, gka, qf_ref, kf_ref, vf_ref, qm_ref, qa_ref, ka_ref, va_ref):
        def norm(off, width, hd, g_ref, o_ref):
            for b in range(width // 128):
                v = p_ref[:, off + b * 128: off + (b + 1) * 128].astype(F32)
                r = lax.rsqrt(_group_mean(v * v, hd) + EPS)
                o_ref[:, b * 128:(b + 1) * 128] = (v * r * g_ref[...]).astype(BF16)

        norm(H_QF, 512, HEAD, gqf, qf_ref)
        norm(H_KF, 512, HEAD, gkf, kf_ref)
        vf_ref[...] = p_ref[:, H_VF:H_VF + 512].astype(BF16)
        norm(H_QM, 512, MEM_HEAD, gqm, qm_ref)
        norm(H_QA, 512, HEAD, gqa, qa_ref)
        norm(H_KA, 128, HEAD, gka, ka_ref)
        va_ref[...] = p_ref[:, H_VA:H_VA + 128].astype(BF16)

    g_spec = _full((1, 128))
    o512 = pl.BlockSpec((tm, 512), lambda i: (i, 0))
    o128 = pl.BlockSpec((tm, 128), lambda i: (i, 0))
    s512 = jax.ShapeDtypeStruct((s, 512), BF16)
    s128 = jax.ShapeDtypeStruct((s, 128), BF16)
    return pl.pallas_call(
        body, name="proj_post", grid=(s // tm,),
        in_specs=[pl.BlockSpec((tm, HALF_W), lambda i: (i, 1)), g_spec, g_spec, g_spec, g_spec, g_spec],
        out_specs=[o512, o512, o512, o512, o512, o128, o128],
        out_shape=[s512, s512, s512, s512, s512, s128, s128],
        compiler_params=_params(("parallel",)),
    )(proj, gq_fox, gk_fox, gq_mem, gq_swa, gk_swa)


def _tri(n, lower):
    r = lax.broadcasted_iota(jnp.int32, (n, n), 0)
    c = lax.broadcasted_iota(jnp.int32, (n, n), 1)
    return jnp.where((c <= r) if lower else (c >= r), 1.0, 0.0).astype(F32)


def _fox_gate_fwd(proj, b_forget128):
    s = proj.shape[0]
    tm = min(512, s)

    def body(p_ref, b_ref, cc_ref, carry_ref):
        i = pl.program_id(0)

        @pl.when(i == 0)
        def _():
            carry_ref[...] = jnp.zeros_like(carry_ref)

        z = p_ref[...] + b_ref[...]
        logf = jnp.minimum(z, 0.0) - jnp.log(1.0 + jnp.exp(-jnp.abs(z)))
        c = jnp.dot(_tri(tm, True), logf, precision=lax.Precision.HIGHEST, preferred_element_type=F32) + carry_ref[...]
        carry_ref[...] = c[tm - 1:tm, :]
        for hp in range(4):
            cc_ref[hp] = c if hp == 0 else pltpu.roll(c, 128 - 2 * hp, 1)

    return pl.pallas_call(
        body, name="fox_gate_fwd", grid=(s // tm,),
        in_specs=[pl.BlockSpec((tm, 128), lambda i: (i, 0)), _full((1, 128))],
        out_specs=pl.BlockSpec((4, tm, 128), lambda i: (0, i, 0)),
        out_shape=jax.ShapeDtypeStruct((4, s, 128), F32),
        scratch_shapes=[pltpu.VMEM((1, 128), F32)],
        compiler_params=_params(("arbitrary",)),
    )(proj, b_forget128)


def _memkv_fwd(mem, g_mem, w_kv, kn_mem):
    m = mem.shape[0]

    def body(mem_ref, g_ref, w_ref, kn_ref, memn_ref, kv_ref, mk_ref, mv_ref):
        xv = mem_ref[...]
        r = lax.rsqrt(jnp.mean(xv * xv, axis=-1, keepdims=True) + EPS)
        mn = (xv * r * g_ref[...]).astype(BF16)
        memn_ref[...] = mn
        kv = _dot(mn, w_ref[...])
        kv_ref[...] = kv
        for h in range(MEM_HEADS):
            v = kv[:, h * 128:(h + 1) * 128]
            rr = lax.rsqrt(jnp.mean(v * v, axis=-1, keepdims=True) + EPS)
            mk_ref[:, h * 128:(h + 1) * 128] = (v * rr * kn_ref[...]).astype(BF16)
        mv_ref[...] = kv[:, 512:1024].astype(BF16)

    return pl.pallas_call(
        body, name="memkv_fwd",
        out_shape=[jax.ShapeDtypeStruct((m, D_MODEL), BF16), jax.ShapeDtypeStruct((m, 1024), F32),
                   jax.ShapeDtypeStruct((m, 512), BF16), jax.ShapeDtypeStruct((m, 512), BF16)],
        compiler_params=pltpu.CompilerParams(vmem_limit_bytes=VMEM_LIMIT),
    )(mem, g_mem, w_kv, kn_mem)


def _bias_table(rel_bias, bucket):
    def body(rb_ref, bk_ref, o_ref):
        bk = bk_ref[...]
        for h in range(SWA_HEADS):
            acc = jnp.zeros(bk.shape, F32)
            for b in range(REL_BUCKETS):
                acc = jnp.where(bk == b, rb_ref[b, h], acc)
            o_ref[h] = acc

    return pl.pallas_call(
        body, name="bias_table",
        in_specs=[pl.BlockSpec(memory_space=pltpu.SMEM), pl.BlockSpec(memory_space=pltpu.VMEM)],
        out_shape=jax.ShapeDtypeStruct((SWA_HEADS,) + bucket.shape, F32),
    )(rel_bias, bucket)


def _swa_valid(n):
    row = lax.broadcasted_iota(jnp.int32, (SWA_BLOCK, 2 * SWA_BLOCK), 0)
    col = lax.broadcasted_iota(jnp.int32, (SWA_BLOCK, 2 * SWA_BLOCK), 1)
    dist = row + SWA_BLOCK - col
    return (dist >= 0) & (dist < SWA_BLOCK) & ((col >= SWA_BLOCK) | (n > 0))


def _swa_fwd(qa, kp, vp, bias, sink):
    s = qa.shape[0]
    nb = s // SWA_BLOCK

    def body(sink_ref, q_ref, kp_ref, vp_ref, bias_ref, o_ref):
        n = pl.program_id(0)
        start = pl.multiple_of(n * SWA_BLOCK, SWA_BLOCK)
        k2 = kp_ref[pl.ds(start, 2 * SWA_BLOCK), :]
        v2 = vp_ref[pl.ds(start, 2 * SWA_BLOCK), :]
        valid = _swa_valid(n)
        heads = range(SWA_HEADS)
        hs = lambda h: slice(h * HEAD, (h + 1) * HEAD)
        sc = [jnp.where(valid, _dot(q_ref[:, hs(h)], k2[:, hs(h // 4)], NT) * 0.125 + bias_ref[h], NEG) for h in heads]
        pn = []
        for h in heads:
            sk = sink_ref[h]
            mx = jnp.maximum(jnp.max(sc[h], axis=-1, keepdims=True), sk)
            p = jnp.exp(sc[h] - mx)
            den = jnp.sum(p, axis=-1, keepdims=True) + jnp.exp(sk - mx)
            pn.append((p / den).astype(BF16))
        outs = [_dot(pn[h], v2[:, hs(h // 4)]).astype(BF16) for h in heads]
        for h in heads:
            o_ref[:, hs(h)] = outs[h]

    return pl.pallas_call(
        body, name="swa_fwd", grid=(nb,),
        in_specs=[pl.BlockSpec(memory_space=pltpu.SMEM),
                  pl.BlockSpec((SWA_BLOCK, 512), lambda n: (n, 0)),
                  _full(kp.shape), _full(vp.shape), _full(bias.shape)],
        out_specs=pl.BlockSpec((SWA_BLOCK, 512), lambda n: (n, 0)),
        out_shape=jax.ShapeDtypeStruct((s, 512), BF16),
        compiler_params=_params(("parallel",)),
    )(sink, qa, kp, vp, bias)


def _head_mask(e):
    lane = lax.broadcasted_iota(jnp.int32, (1, 128), 1)
    return (lane >= e * HEAD) & (lane < (e + 1) * HEAD)


FOX_FWD_TQ, FOX_FWD_TK = 1024, 1024
FOX_BWD_TK, FOX_BWD_TQ = 512, 512


def _head_rows(e):
    row = lax.broadcasted_iota(jnp.int32, (128, 1), 0)
    return (row >= e * HEAD) & (row < (e + 1) * HEAD)


def _fox_fwd(q, k, v_t, cc4):
    s = q.shape[0]
    t = min(FOX_FWD_TQ, s)
    tk = min(FOX_FWD_TK, s)
    nq = s // t

    def body(q_ref, k_ref, vt_ref, cc_ref, o_ref, lse_ref):
        i = pl.program_id(1)
        qs = q_ref[...] * jnp.asarray(0.125, BF16)
        qe = [jnp.where(_head_mask(e), qs, jnp.zeros_like(qs)) for e in range(2)]
        n_full = (i * t) // tk
        krow = lax.broadcasted_iota(jnp.int32, (tk, t), 0) + n_full * tk
        qcol = lax.broadcasted_iota(jnp.int32, (tk, t), 1) + i * t

        def step(j, carry, masked):
            ks = pl.ds(pl.multiple_of(j * tk, tk), tk)
            kj = k_ref[ks, :]
            vtj = vt_ref[:, ks]
            out = []
            for e in range(2):
                m, acc = carry[2 * e], carry[2 * e + 1]
                st = _dot(kj, qe[e], NT) - cc_ref[0, ks, e:e + 1]
                if masked:
                    st = jnp.where(krow <= qcol, st, NEG)
                m_new = jnp.maximum(m, jnp.max(st, axis=0, keepdims=True))
                alpha = jnp.exp(m - m_new)
                pt = jnp.exp(st - m_new).astype(BF16)
                vte = jnp.where(_head_rows(e), vtj, jnp.ones_like(vtj))
                out += [m_new, alpha * acc + _dot(vte, pt)]
            return tuple(out)

        init = (jnp.full((1, t), NEG, F32), jnp.zeros((128, t), F32)) * 2
        carry = lax.fori_loop(0, n_full, functools.partial(step, masked=False), init)
        m0, a0, m1, a1 = step(n_full, carry, True)
        l0 = a0[HEAD:HEAD + 1, :]
        l1 = a1[0:1, :]
        o_t = jnp.where(_head_rows(0), a0 / l0, a1 / l1)
        o_ref[...] = o_t.T.astype(BF16)
        r8 = lax.broadcasted_iota(jnp.int32, (8, t), 0)
        lse_ref[0] = jnp.where(r8 == 0, m0 + jnp.log(l0), jnp.where(r8 == 1, m1 + jnp.log(l1), 0.0))

    return pl.pallas_call(
        body, name="fox_fwd", grid=(4, nq),
        in_specs=[pl.BlockSpec((t, 128), lambda hp, i: (i, hp)),
                  pl.BlockSpec((s, 128), lambda hp, i: (0, hp)),
                  pl.BlockSpec((128, s), lambda hp, i: (hp, 0)),
                  pl.BlockSpec((1, s, 128), lambda hp, i: (hp, 0, 0))],
        out_specs=[pl.BlockSpec((t, 128), lambda hp, i: (i, hp)),
                   pl.BlockSpec((1, 8, t), lambda hp, i: (hp, 0, i))],
        out_shape=[jax.ShapeDtypeStruct((s, 512), BF16), jax.ShapeDtypeStruct((4, 8, s), F32)],
        compiler_params=_params(("parallel", "parallel")),
    )(q, k, v_t, cc4)


MEM_SCALE = MEM_HEAD ** -0.5


def _mem_fwd(qm, mk, mv):
    s = qm.shape[0]
    tq = min(512, s)

    def body(q_ref, mk_ref, mv_ref, o_ref):
        for h in range(MEM_HEADS):
            hs = slice(h * 128, (h + 1) * 128)
            sc = _dot(q_ref[:, hs], mk_ref[:, hs], NT) * MEM_SCALE
            mx = jnp.max(sc, axis=-1, keepdims=True)
            p = jnp.exp(sc - mx)
            p = p / jnp.sum(p, axis=-1, keepdims=True)
            o_ref[:, hs] = _dot(p.astype(BF16), mv_ref[:, hs]).astype(BF16)

    return pl.pallas_call(
        body, name="mem_fwd", grid=(s // tq,),
        in_specs=[pl.BlockSpec((tq, 512), lambda i: (i, 0)), _full(mk.shape), _full(mv.shape)],
        out_specs=pl.BlockSpec((tq, 512), lambda i: (i, 0)),
        out_shape=jax.ShapeDtypeStruct((s, 512), BF16),
        compiler_params=_params(("parallel",)),
    )(qm, mk, mv)


def _merge_fwd(x, oa, of, om, proj, b_gate, wa, wf, wm, w_out, g_mlp):
    s = x.shape[0]
    tm = min(256, s)

    def body(x_ref, oa_ref, of_ref, om_ref, gl_ref, bg_ref, wa_ref, wf_ref, wm_ref, wo_ref, g_ref, x1_ref, hm_ref, mg_ref):
        merged = None
        for b, (o_ref, w_ref) in enumerate(((oa_ref, wa_ref), (of_ref, wf_ref), (om_ref, wm_ref))):
            cs = slice(b * D_MODEL, (b + 1) * D_MODEL)
            y = _dot(o_ref[...], w_ref[...])
            t = _sigmoid(gl_ref[:, cs].astype(F32) + bg_ref[:, cs]) * y
            merged = t if merged is None else merged + t
        mb = merged.astype(BF16)
        mg_ref[...] = mb
        x1 = x_ref[...] + _dot(mb, wo_ref[...])
        x1_ref[...] = x1
        r = lax.rsqrt(jnp.mean(x1 * x1, axis=-1, keepdims=True) + EPS)
        hm_ref[...] = (x1 * r * g_ref[...]).astype(BF16)

    row = lambda w: pl.BlockSpec((tm, w), lambda i: (i, 0))
    return pl.pallas_call(
        body, name="merge_fwd", grid=(s // tm,),
        in_specs=[row(D_MODEL), row(512), row(512), row(512), row(HALF_W), _full((1, HALF_W)),
                  _full(wa.shape), _full(wf.shape), _full(wm.shape), _full(w_out.shape), _full((1, D_MODEL))],
        out_specs=[row(D_MODEL), row(D_MODEL), row(D_MODEL)],
        out_shape=[jax.ShapeDtypeStruct((s, D_MODEL), F32), jax.ShapeDtypeStruct((s, D_MODEL), BF16),
                   jax.ShapeDtypeStruct((s, D_MODEL), BF16)],
        compiler_params=_params(("parallel",)),
    )(x, oa, of, om, proj, b_gate, wa, wf, wm, w_out, g_mlp)


def _mlp_up(hm, w_up):
    s = hm.shape[0]
    tm, tn = min(1024, s), w_up.shape[2]

    def body(h_ref, w_ref, u_ref):
        r = jnp.maximum(_dot(h_ref[...], w_ref[0]), 0.0)
        u_ref[...] = (r * r).astype(BF16)

    return pl.pallas_call(
        body, name="mlp_up", grid=(s // tm, D_FF // tn),
        in_specs=[pl.BlockSpec((tm, D_MODEL), lambda i, j: (i, 0)), pl.BlockSpec((1, D_MODEL, tn), lambda i, j: (j, 0, 0))],
        out_specs=pl.BlockSpec((tm, tn), lambda i, j: (i, j)),
        out_shape=jax.ShapeDtypeStruct((s, D_FF), BF16),
        compiler_params=_params(("parallel", "parallel")),
    )(hm, w_up)


def _mlp_down_loss(u, w_down, x1, target):
    s = u.shape[0]
    tm = min(256, s)

    def body(u_ref, w_ref, x1_ref, t_ref, dy_ref, dyb_ref, loss_ref):
        i = pl.program_id(0)

        @pl.when(i == 0)
        def _():
            loss_ref[...] = jnp.zeros_like(loss_ref)

        y = x1_ref[...] + _dot(u_ref[...], w_ref[...])
        err = y - t_ref[...]
        dy = err * (1.0 / D_MODEL)
        dy_ref[...] = dy
        dyb_ref[...] = dy.astype(BF16)
        part = jnp.sum(jnp.sum(err * err, axis=-1, keepdims=True) * (1.0 / D_MODEL), axis=0, keepdims=True)
        loss_ref[...] += 0.5 * part

    row = pl.BlockSpec((tm, D_MODEL), lambda i: (i, 0))
    return pl.pallas_call(
        body, name="mlp_down_loss", grid=(s // tm,),
        in_specs=[pl.BlockSpec((tm, D_FF), lambda i: (i, 0)), _full(w_down.shape), row, row],
        out_specs=[row, row, _full((1, 1))],
        out_shape=[jax.ShapeDtypeStruct((s, D_MODEL), F32), jax.ShapeDtypeStruct((s, D_MODEL), BF16),
                   jax.ShapeDtypeStruct((1, 1), F32)],
        compiler_params=_params(("arbitrary",)),
    )(u, w_down, x1, target)


def _mlp_bwd_act(dy, w_down, u):
    s = dy.shape[0]
    tm, tn = min(1024, s), 1024

    def body(dy_ref, w_ref, u_ref, da_ref):
        du = _dot(dy_ref[...], w_ref[...], NT)
        da_ref[...] = (du * (2.0 * jnp.sqrt(u_ref[...].astype(F32)))).astype(BF16)

    return pl.pallas_call(
        body, name="mlp_bwd_act", grid=(D_FF // tn, s // tm),
        in_specs=[pl.BlockSpec((tm, D_MODEL), lambda j, i: (i, 0)), pl.BlockSpec((tn, D_MODEL), lambda j, i: (j, 0)),
                  pl.BlockSpec((tm, tn), lambda j, i: (i, j))],
        out_specs=pl.BlockSpec((tm, tn), lambda j, i: (i, j)),
        out_shape=jax.ShapeDtypeStruct((s, D_FF), BF16),
        compiler_params=_params(("parallel", "parallel")),
    )(dy, w_down, u)


def _rms_bwd(xv, g, dh, skip):
    r = lax.rsqrt(jnp.mean(xv * xv, axis=-1, keepdims=True) + EPS)
    n = xv * r
    dn = dh * g
    dx = skip + r * (dn - n * jnp.mean(dn * n, axis=-1, keepdims=True))
    return dx, jnp.sum(dh * n, axis=0, keepdims=True)


def _mlp_bwd_x(da, w_up, x1, dy, g_mlp):
    s = da.shape[0]
    tm = min(256, s)

    def body(da_ref, w_ref, x1_ref, dy_ref, g_ref, dx1_ref, dg_ref):
        i = pl.program_id(0)

        @pl.when(i == 0)
        def _():
            dg_ref[...] = jnp.zeros_like(dg_ref)

        tn = w_ref.shape[2]
        dhm = _dot(da_ref[:, 0:tn], w_ref[0], NT)
        for j in range(1, N_DEV):
            dhm = dhm + _dot(da_ref[:, j * tn:(j + 1) * tn], w_ref[j], NT)
        dx, dg = _rms_bwd(x1_ref[...], g_ref[...], dhm, dy_ref[...])
        dx1_ref[...] = dx
        dg_ref[...] += dg

    row = pl.BlockSpec((tm, D_MODEL), lambda i: (i, 0))
    return pl.pallas_call(
        body, name="mlp_bwd_x", grid=(s // tm,),
        in_specs=[pl.BlockSpec((tm, D_FF), lambda i: (i, 0)), _full(w_up.shape), row, row, _full((1, D_MODEL))],
        out_specs=[row, _full((1, D_MODEL))],
        out_shape=[jax.ShapeDtypeStruct((s, D_MODEL), F32), jax.ShapeDtypeStruct((1, D_MODEL), F32)],
        compiler_params=_params(("arbitrary",)),
    )(da, w_up, x1, dy, g_mlp)


def _merge_bwd(dx1, oa, of, om, proj, b_gate, wa, wf, wm, w_out):
    s = dx1.shape[0]
    tm = min(256, s)

    def body(dx1_ref, oa_ref, of_ref, om_ref, gl_ref, bg_ref, wa_ref, wf_ref, wm_ref, wo_ref,
             dp_ref, doa_ref, dof_ref, dom_ref, dya_ref, dyf_ref, dym_ref, dbg_ref):
        i = pl.program_id(0)

        @pl.when(i == 0)
        def _():
            dbg_ref[...] = jnp.zeros_like(dbg_ref)

        dmerged = _dot(dx1_ref[...].astype(BF16), wo_ref[...], NT)
        branches = ((oa_ref, wa_ref, doa_ref, dya_ref), (of_ref, wf_ref, dof_ref, dyf_ref), (om_ref, wm_ref, dom_ref, dym_ref))
        for b, (o_ref, w_ref, do_ref, dyb_ref) in enumerate(branches):
            cs = slice(b * D_MODEL, (b + 1) * D_MODEL)
            y = _dot(o_ref[...], w_ref[...])
            g = _sigmoid(gl_ref[:, cs].astype(F32) + bg_ref[:, cs])
            dz = (dmerged * y) * g * (1.0 - g)
            dp_ref[:, cs] = dz.astype(BF16)
            dbg_ref[:, cs] += jnp.sum(dz, axis=0, keepdims=True)
            dyb = (dmerged * g).astype(BF16)
            dyb_ref[...] = dyb
            do_ref[...] = _dot(dyb, w_ref[...], NT).astype(BF16)

    row = lambda w: pl.BlockSpec((tm, w), lambda i: (i, 0))
    sd = lambda w: jax.ShapeDtypeStruct((s, w), BF16)
    return pl.pallas_call(
        body, name="merge_bwd", grid=(s // tm,),
        in_specs=[row(D_MODEL), row(512), row(512), row(512), row(HALF_W), _full((1, HALF_W)),
                  _full(wa.shape), _full(wf.shape), _full(wm.shape), _full(w_out.shape)],
        out_specs=[row(HALF_W), row(512), row(512), row(512), row(D_MODEL), row(D_MODEL), row(D_MODEL), _full((1, HALF_W))],
        out_shape=[sd(PROJ_W), sd(512), sd(512), sd(512), sd(D_MODEL), sd(D_MODEL), sd(D_MODEL),
                   jax.ShapeDtypeStruct((1, HALF_W), F32)],
        compiler_params=_params(("arbitrary",)),
    )(dx1, oa, of, om, proj, b_gate, wa, wf, wm, w_out)


def _swa_valid_t(n):
    key = lax.broadcasted_iota(jnp.int32, (2 * SWA_BLOCK, SWA_BLOCK), 0)
    qry = lax.broadcasted_iota(jnp.int32, (2 * SWA_BLOCK, SWA_BLOCK), 1)
    dist = qry + SWA_BLOCK - key
    return (dist >= 0) & (dist < SWA_BLOCK) & ((key >= SWA_BLOCK) | (n > 0))


def _swa_bwd(qa, kp, vp, bias_t, sink, doa):
    s = qa.shape[0]
    nb = s // SWA_BLOCK

    def body(sink_ref, q_ref, kp_ref, vp_ref, bias_ref, do_ref, dq_ref, dkp_ref, dvp_ref, dbias_ref, dsink_ref, sk_acc):
        n = pl.program_id(0)

        @pl.when(n == 0)
        def _():
            dkp_ref[...] = jnp.zeros_like(dkp_ref)
            dvp_ref[...] = jnp.zeros_like(dvp_ref)
            dbias_ref[...] = jnp.zeros_like(dbias_ref)
            sk_acc[...] = jnp.zeros_like(sk_acc)

        start = pl.multiple_of(n * SWA_BLOCK, SWA_BLOCK)
        win = pl.ds(start, 2 * SWA_BLOCK)
        k2 = kp_ref[win, :]
        v2 = vp_ref[win, :]
        valid = _swa_valid_t(n)
        heads = range(SWA_HEADS)
        hs = lambda h: slice(h * HEAD, (h + 1) * HEAD)
        scale = jnp.asarray(0.125, BF16)
        q = [q_ref[:, hs(h)] for h in heads]
        do = [do_ref[:, hs(h)] for h in heads]
        kk = [k2[:, hs(kv)] for kv in range(2)]
        vv = [v2[:, hs(kv)] for kv in range(2)]
        kt = [(kk[kv].astype(F32) * 0.125).T.astype(BF16) for kv in range(2)]
        st = [jnp.where(valid, _dot(kk[h // 4], q[h], NT) * 0.125 + bias_ref[h], NEG) for h in heads]
        dpt = [_dot(vv[h // 4], do[h], NT) for h in heads]
        pt, dst = [], []
        for h in heads:
            sk = sink_ref[h]
            mx = jnp.maximum(jnp.max(st[h], axis=0, keepdims=True), sk)
            p = jnp.exp(st[h] - mx)
            esk = jnp.exp(sk - mx)
            den = jnp.sum(p, axis=0, keepdims=True) + esk
            p = p / den
            delta = jnp.sum(p * dpt[h], axis=0, keepdims=True)
            d = p * (dpt[h] - delta)
            sk_acc[h:h + 1, :] += -(esk / den) * delta
            dbias_ref[h] += d
            pt.append(p.astype(BF16))
            dst.append(d.astype(BF16))
        dq_t = [_dot(kt[h // 4], dst[h]) for h in heads]
        dq_ref[...] = jnp.concatenate(dq_t, axis=0).T
        for kv in range(2):
            group = range(4 * kv, 4 * kv + 4)
            dk = [_dot(dst[h], q[h] * scale) for h in group]
            dv = [_dot(pt[h], do[h]) for h in group]
            dkp_ref[win, hs(kv)] += (dk[0] + dk[1]) + (dk[2] + dk[3])
            dvp_ref[win, hs(kv)] += (dv[0] + dv[1]) + (dv[2] + dv[3])

        @pl.when(n == nb - 1)
        def _():
            dsink_ref[...] = jnp.broadcast_to(jnp.sum(sk_acc[...], axis=1, keepdims=True), dsink_ref.shape)

    return pl.pallas_call(
        body, name="swa_bwd", grid=(nb,),
        in_specs=[pl.BlockSpec(memory_space=pltpu.SMEM),
                  pl.BlockSpec((SWA_BLOCK, 512), lambda n: (n, 0)),
                  _full(kp.shape), _full(vp.shape), _full(bias_t.shape),
                  pl.BlockSpec((SWA_BLOCK, 512), lambda n: (n, 0))],
        out_specs=[pl.BlockSpec((SWA_BLOCK, 512), lambda n: (n, 0)), _full(kp.shape), _full(vp.shape),
                   _full(bias_t.shape), _full((SWA_HEADS, 128))],
        out_shape=[jax.ShapeDtypeStruct((s, 512), F32), jax.ShapeDtypeStruct(kp.shape, F32),
                   jax.ShapeDtypeStruct(vp.shape, F32), jax.ShapeDtypeStruct(bias_t.shape, F32),
                   jax.ShapeDtypeStruct((SWA_HEADS, 128), F32)],
        scratch_shapes=[pltpu.VMEM((SWA_HEADS, 128), F32)],
        compiler_params=_params(("arbitrary",)),
    )(sink, qa, kp, vp, bias_t, doa)


def _fox_bwd(qt, k, v, dot, ot, cc4, lse4):
    s = k.shape[0]
    t = min(FOX_BWD_TK, s)
    tq = min(FOX_BWD_TQ, s)
    nq = s // t
    nqt = s // tq

    def body(qt_ref, k_ref, v_ref, dot_ref, ot_ref, cc_ref, lse_ref,
             dqt_ref, dk_ref, dv_ref, dck_ref, dcq_ref, delta_ref, dk0, dk1, dv0, dv1, ds0, ds1):
        j = pl.program_id(1)

        @pl.when(j == 0)
        def _():
            dqt_ref[...] = jnp.zeros_like(dqt_ref)
            dcq_ref[...] = jnp.zeros_like(dcq_ref)
            r8 = lax.broadcasted_iota(jnp.int32, (8, tq), 0)

            def dl(i, c):
                cols = pl.ds(pl.multiple_of(i * tq, tq), tq)
                pr = dot_ref[:, cols].astype(F32) * ot_ref[:, cols].astype(F32)
                d0 = jnp.sum(jnp.where(_head_rows(0), pr, 0.0), axis=0, keepdims=True)
                d1 = jnp.sum(jnp.where(_head_rows(1), pr, 0.0), axis=0, keepdims=True)
                delta_ref[:, cols] = jnp.where(r8 == 0, d0, jnp.where(r8 == 1, d1, 0.0))
                return c

            lax.fori_loop(0, nqt, dl, 0)

        kj = k_ref[...]
        vj = v_ref[...]
        ks = pl.ds(pl.multiple_of(j * t, t), t)
        kt = (kj.astype(F32) * 0.125).T.astype(BF16)
        ke = [jnp.where(_head_mask(e), kj, jnp.zeros_like(kj)) for e in range(2)]
        ve = [jnp.where(_head_mask(e), vj, jnp.zeros_like(vj)) for e in range(2)]
        kte = [jnp.where(_head_rows(e), kt, jnp.zeros_like(kt)) for e in range(2)]
        ck = [cc_ref[0, ks, e:e + 1] for e in range(2)]
        accs = ((dk0, dv0, ds0), (dk1, dv1, ds1))
        for refs in accs:
            for r in refs:
                r[...] = jnp.zeros_like(r)
        i_first = (j * t) // tq
        krow = lax.broadcasted_iota(jnp.int32, (t, tq), 0) + j * t
        qcol = lax.broadcasted_iota(jnp.int32, (t, tq), 1) + i_first * tq

        def step(i, c, masked):
            cols = pl.ds(pl.multiple_of(i * tq, tq), tq)
            qti = qt_ref[:, cols]
            doti = dot_ref[:, cols]
            for e in range(2):
                dkt_acc, dvt_acc, ds_acc = accs[e]
                st = _dot(ke[e], qti) - ck[e]
                if masked:
                    st = jnp.where(krow <= qcol, st, NEG)
                pt = jnp.exp(st - lse_ref[0, e:e + 1, cols])
                dpt = _dot(ve[e], doti)
                dst = pt * (dpt - delta_ref[e:e + 1, cols])
                dsb = dst.astype(BF16)
                dvt_acc[...] += _dot(doti, pt.astype(BF16), NT)
                dkt_acc[...] += _dot(qti, dsb, NT)
                dqt_ref[:, cols] += _dot(kte[e], dsb)
                ds_acc[...] += dst
                dcq_ref[0, e:e + 1, cols] += jnp.sum(dst, axis=0, keepdims=True)
            return c

        step(i_first, 0, True)
        lax.fori_loop(i_first + 1, nqt, functools.partial(step, masked=False), 0)
        r0 = _head_rows(0)
        dk_ref[...] = jnp.where(r0, dk0[...], dk1[...]).T
        dv_ref[...] = jnp.where(r0, dv0[...], dv1[...]).T
        lane = lax.broadcasted_iota(jnp.int32, (t, 128), 1)
        c0 = jnp.sum(ds0[...], axis=-1, keepdims=True)
        c1 = jnp.sum(ds1[...], axis=-1, keepdims=True)
        dck_ref[0] = jnp.where(lane == 0, c0, jnp.where(lane == 1, c1, 0.0))

    res_t = lambda: pl.BlockSpec((128, s), lambda hp, j: (hp, 0))
    blk = lambda: pl.BlockSpec((t, 128), lambda hp, j: (j, hp))
    return pl.pallas_call(
        body, name="fox_bwd", grid=(4, nq),
        in_specs=[res_t(), blk(), blk(), res_t(), res_t(), pl.BlockSpec((1, s, 128), lambda hp, j: (hp, 0, 0)),
                  pl.BlockSpec((1, 8, s), lambda hp, j: (hp, 0, 0))],
        out_specs=[res_t(), blk(), blk(),
                   pl.BlockSpec((1, t, 128), lambda hp, j: (hp, j, 0)),
                   pl.BlockSpec((1, 8, s), lambda hp, j: (hp, 0, 0))],
        out_shape=[jax.ShapeDtypeStruct((512, s), F32), jax.ShapeDtypeStruct((s, 512), F32),
                   jax.ShapeDtypeStruct((s, 512), F32), jax.ShapeDtypeStruct((4, s, 128), F32),
                   jax.ShapeDtypeStruct((4, 8, s), F32)],
        scratch_shapes=[pltpu.VMEM((8, s), F32)] + [pltpu.VMEM((128, t), F32)] * 4 + [pltpu.VMEM((t, tq), F32)] * 2,
        compiler_params=_params(("arbitrary", "arbitrary")),
    )(qt, k, v, dot, ot, cc4, lse4)


def _mem_bwd(qm, mk, mv, dom):
    s = qm.shape[0]
    tq = min(512, s)

    def body(q_ref, mk_ref, mv_ref, do_ref, dq_ref, dmk_ref, dmv_ref):
        i = pl.program_id(0)

        @pl.when(i == 0)
        def _():
            dmk_ref[...] = jnp.zeros_like(dmk_ref)
            dmv_ref[...] = jnp.zeros_like(dmv_ref)

        for h in range(MEM_HEADS):
            hs = slice(h * 128, (h + 1) * 128)
            qh = q_ref[:, hs]
            doh = do_ref[:, hs]
            sc = _dot(qh, mk_ref[:, hs], NT) * MEM_SCALE
            mx = jnp.max(sc, axis=-1, keepdims=True)
            p = jnp.exp(sc - mx)
            p = p / jnp.sum(p, axis=-1, keepdims=True)
            dp = _dot(doh, mv_ref[:, hs], NT)
            ds = p * (dp - jnp.sum(p * dp, axis=-1, keepdims=True))
            dsb = (ds * MEM_SCALE).astype(BF16)
            dq_ref[:, hs] = _dot(dsb, mk_ref[:, hs])
            dmk_ref[:, hs] += _dot(dsb, qh, TN)
            dmv_ref[:, hs] += _dot(p.astype(BF16), doh, TN)

    return pl.pallas_call(
        body, name="mem_bwd", grid=(s // tq,),
        in_specs=[pl.BlockSpec((tq, 512), lambda i: (i, 0)), _full(mk.shape), _full(mv.shape),
                  pl.BlockSpec((tq, 512), lambda i: (i, 0))],
        out_specs=[pl.BlockSpec((tq, 512), lambda i: (i, 0)), _full(mk.shape), _full(mv.shape)],
        out_shape=[jax.ShapeDtypeStruct((s, 512), F32), jax.ShapeDtypeStruct(mk.shape, F32),
                   jax.ShapeDtypeStruct(mv.shape, F32)],
        compiler_params=_params(("arbitrary",)),
    )(qm, mk, mv, dom)


def _memkv_bwd(dmk, dmv, kv_raw, kn_mem, mem, g_mem, mem_n, w_kv):
    def body(dmk_ref, dmv_ref, kv_ref, kn_ref, mem_ref, g_ref, mn_ref, w_ref, dw_ref, dkn_ref, dg_ref, dkv_ref):
        dkn = jnp.zeros((1, 128), F32)
        for h in range(MEM_HEADS):
            hs = slice(h * 128, (h + 1) * 128)
            v = kv_ref[:, hs]
            r = lax.rsqrt(jnp.mean(v * v, axis=-1, keepdims=True) + EPS)
            n = v * r
            dn = dmk_ref[:, hs]
            dkn = dkn + jnp.sum(dn * n, axis=0, keepdims=True)
            dng = dn * kn_ref[...]
            dkv_ref[:, hs] = (r * (dng - n * jnp.mean(dng * n, axis=-1, keepdims=True))).astype(BF16)
        dkv_ref[:, 512:1024] = dmv_ref[...].astype(BF16)
        dkn_ref[...] = dkn
        dkv = dkv_ref[...]
        dw_ref[...] = _dot(mn_ref[...], dkv, TN).astype(BF16)
        dmn = _dot(dkv, w_ref[...], NT)
        xv = mem_ref[...]
        r = lax.rsqrt(jnp.mean(xv * xv, axis=-1, keepdims=True) + EPS)
        dg_ref[...] = jnp.sum(dmn * (xv * r), axis=0, keepdims=True)

    m = mem.shape[0]
    return pl.pallas_call(
        body, name="memkv_bwd",
        out_shape=[jax.ShapeDtypeStruct((D_MODEL, 1024), BF16), jax.ShapeDtypeStruct((1, 128), F32),
                   jax.ShapeDtypeStruct((1, D_MODEL), F32)],
        scratch_shapes=[pltpu.VMEM((m, 1024), BF16)],
        compiler_params=pltpu.CompilerParams(vmem_limit_bytes=VMEM_LIMIT),
    )(dmk, dmv, kv_raw, kn_mem, mem, g_mem, mem_n, w_kv)


def _fox_gate_bwd(dc, proj, b_forget128):
    s = dc.shape[0]
    tm = min(512, s)
    nt = s // tm

    def body(dc_ref, p_ref, b_ref, dfl_ref, db_ref, carry_ref):
        i = pl.program_id(0)

        @pl.when(i == 0)
        def _():
            carry_ref[...] = jnp.zeros_like(carry_ref)
            db_ref[...] = jnp.zeros_like(db_ref)

        dcv = dc_ref[...]
        dlogf = jnp.dot(_tri(tm, False), dcv, precision=lax.Precision.HIGHEST, preferred_element_type=F32) + carry_ref[...]
        carry_ref[...] += jnp.sum(dcv, axis=0, keepdims=True)
        z = p_ref[...] + b_ref[...]
        dfl = dlogf * (1.0 / (1.0 + jnp.exp(z)))
        dfl_ref[...] = dfl.astype(BF16)
        db_ref[...] += jnp.sum(dfl, axis=0, keepdims=True)

    return pl.pallas_call(
        body, name="fox_gate_bwd", grid=(nt,),
        in_specs=[pl.BlockSpec((tm, 128), lambda i: (nt - 1 - i, 0)),
                  pl.BlockSpec((tm, 128), lambda i: (nt - 1 - i, 0)), _full((1, 128))],
        out_specs=[pl.BlockSpec((tm, 128), lambda i: (nt - 1 - i, 0)), _full((1, 128))],
        out_shape=[jax.ShapeDtypeStruct((s, 128), BF16), jax.ShapeDtypeStruct((1, 128), F32)],
        scratch_shapes=[pltpu.VMEM((1, 128), F32)],
        compiler_params=_params(("arbitrary",)),
    )(dc, proj, b_forget128)


def _proj_pre_bwd(dproj, proj, dqf, dkf, dvf, dqm, dqa, dka, dva, dfl, gq_fox, gk_fox, gq_mem, gq_swa, gk_swa):
    s = proj.shape[0]
    tm = min(256, s)

    def body(dp_in, p_ref, dqf_ref, dkf_ref, dvf_ref, dqm_ref, dqa_ref, dka_ref, dva_ref, dfl_ref,
             gqf, gkf, gqm, gqa, gka, dp_ref, dgn_ref):
        i = pl.program_id(0)

        @pl.when(i == 0)
        def _():
            dgn_ref[...] = jnp.zeros_like(dgn_ref)

        def norm_bwd(off, width, hd, g_ref, dn_ref, slot):
            acc = jnp.zeros((1, 128), F32)
            for b in range(width // 128):
                v = p_ref[:, off + b * 128: off + (b + 1) * 128].astype(F32)
                r = lax.rsqrt(_group_mean(v * v, hd) + EPS)
                n = v * r
                dn = dn_ref[:, b * 128:(b + 1) * 128]
                acc = acc + jnp.sum(dn * n, axis=0, keepdims=True)
                dng = dn * g_ref[...]
                dp_ref[:, off + b * 128: off + (b + 1) * 128] = (r * (dng - n * _group_mean(dng * n, hd))).astype(BF16)
            dgn_ref[slot:slot + 1, :] += acc

        norm_bwd(H_QF, 512, HEAD, gqf, dqf_ref, 0)
        norm_bwd(H_KF, 512, HEAD, gkf, dkf_ref, 1)
        dp_ref[:, H_VF:H_VF + 512] = dvf_ref[...].astype(BF16)
        norm_bwd(H_QM, 512, MEM_HEAD, gqm, dqm_ref, 2)
        norm_bwd(H_QA, 512, HEAD, gqa, dqa_ref, 3)
        norm_bwd(H_KA, 128, HEAD, gka, dka_ref, 4)
        dp_ref[:, H_VA:H_VA + 128] = dva_ref[...].astype(BF16)
        dp_ref[:, H_FL:H_FL + 128] = dfl_ref[...]
        dp_ref[:, H_FL + 128:HALF_W] = jnp.zeros((tm, HALF_W - H_FL - 128), BF16)

    row = lambda w: pl.BlockSpec((tm, w), lambda i: (i, 0))
    g_spec = _full((1, 128))
    return pl.pallas_call(
        body, name="proj_pre_bwd", grid=(s // tm,),
        in_specs=[pl.BlockSpec(memory_space=pl.ANY), pl.BlockSpec((tm, HALF_W), lambda i: (i, 1)),
                  row(512), row(512), row(512), row(512), row(512), row(128), row(128), row(128),
                  g_spec, g_spec, g_spec, g_spec, g_spec],
        out_specs=[pl.BlockSpec((tm, HALF_W), lambda i: (i, 1)), _full((8, 128))],
        out_shape=[jax.ShapeDtypeStruct((s, PROJ_W), BF16), jax.ShapeDtypeStruct((8, 128), F32)],
        input_output_aliases={0: 0},
        compiler_params=_params(("arbitrary",)),
    )(dproj, proj, dqf, dkf, dvf, dqm, dqa, dka, dva, dfl, gq_fox, gk_fox, gq_mem, gq_swa, gk_swa)


def _in_bwd_x(dproj, w_in_p, x, g_mix, dx1):
    s = x.shape[0]
    tm = min(256, s)

    def body(dp_ref, w_ref, x_ref, g_ref, dx1_ref, gx_ref, dg_ref):
        i = pl.program_id(0)

        @pl.when(i == 0)
        def _():
            dg_ref[...] = jnp.zeros_like(dg_ref)

        dx, dg = _rms_bwd(x_ref[...], g_ref[...], _dot(dp_ref[...], w_ref[...], NT), dx1_ref[...])
        gx_ref[...] = dx
        dg_ref[...] += dg

    row = pl.BlockSpec((tm, D_MODEL), lambda i: (i, 0))
    return pl.pallas_call(
        body, name="in_bwd_x", grid=(s // tm,),
        in_specs=[pl.BlockSpec((tm, PROJ_W), lambda i: (i, 0)), _full(w_in_p.shape), row, _full((1, D_MODEL)), row],
        out_specs=[row, _full((1, D_MODEL))],
        out_shape=[jax.ShapeDtypeStruct((s, D_MODEL), F32), jax.ShapeDtypeStruct((1, D_MODEL), F32)],
        compiler_params=_params(("arbitrary",)),
    )(dproj, w_in_p, x, g_mix, dx1)


def _rel_bias_bwd(dbias, bucket):
    def body(db_ref, bk_ref, o_ref):
        bk = bk_ref[...]
        lane = lax.broadcasted_iota(jnp.int32, (1, 128), 1)
        for b in range(REL_BUCKETS):
            sel = bk == b
            acc = jnp.zeros((1, 128), F32)
            for h in range(SWA_HEADS):
                tot = jnp.sum(jnp.sum(jnp.where(sel, db_ref[h], 0.0), axis=-1, keepdims=True), axis=0, keepdims=True)
                acc = jnp.where(lane == h, tot, acc)
            o_ref[:, b * 128:(b + 1) * 128] = acc

    return pl.pallas_call(
        body, name="rel_bias_bwd",
        out_shape=jax.ShapeDtypeStruct((1, REL_BUCKETS * 128), F32),
        compiler_params=pltpu.CompilerParams(vmem_limit_bytes=VMEM_LIMIT),
    )(dbias, bucket)


def _my_place():
    return lax.axis_index("x"), lax.axis_index("y"), lax.axis_index("c")


def _peer(place, k):
    x, y, c = place
    return (1 - x if k & 4 else x, 1 - y if k & 2 else y, 1 - c if k & 1 else c)


def _index(place):
    x, y, c = place
    return 4 * x + 2 * y + c


HBM_SPEC = pl.BlockSpec(memory_space=pltpu.HBM)
SEM_SPEC = pl.BlockSpec(memory_space=pltpu.SEMAPHORE)
DATAFLOW = pltpu.SideEffectType.DATAFLOW_SIDE_EFFECTING


ALL_PEERS = tuple(range(1, N_DEV))
SAME_CORE = (2, 4, 6)
OWN = N_DEV - 1


def _split_copy(src_ref, land_ref, send_sems, recv_sems, me, k, gather):
    peer = _peer(me, k)
    if gather:
        src, dst = src_ref, land_ref.at[_index(me)]
    else:
        src, dst = src_ref.at[_index(peer)], land_ref.at[k - 1]
    return pltpu.make_async_remote_copy(src_ref=src, dst_ref=dst, send_sem=send_sems.at[k - 1], recv_sem=recv_sems.at[k - 1],
                                        device_id=peer, device_id_type=MESH)


def _own_copy(src_ref, land_ref, recv_sems, me, gather):
    if gather:
        src, dst = src_ref, land_ref.at[_index(me)]
    else:
        src, dst = src_ref.at[_index(me)], land_ref.at[OWN]
    return pltpu.make_async_copy(src, dst, recv_sems.at[OWN])


def _split_start(srcs, gather, name, peers=ALL_PEERS, after=None):
    n = len(srcs)
    extra = [] if after is None else [after]

    def body(*refs):
        refs = refs[:2 * n] + refs[2 * n + len(extra):]
        src_refs, land_refs = refs[:n], refs[n:2 * n]
        send_sems, recv_sems, token = refs[2 * n:3 * n], refs[3 * n:4 * n], refs[-1]
        me = _my_place()
        for w in range(n):
            for k in peers:
                _split_copy(src_refs[w], land_refs[w], send_sems[w], recv_sems[w], me, k, gather).start()
            _own_copy(src_refs[w], land_refs[w], recv_sems[w], me, gather).start()
        token[...] = jnp.zeros_like(token)

    lands = [lax.empty((N_DEV,) + (a.shape if gather else a.shape[1:]), a.dtype) for a in srcs]
    sems = [pltpu.SemaphoreType.DMA((N_DEV,))] * (2 * n)
    hbm = [pltpu.HBM(a.shape, a.dtype) for a in list(srcs) + lands]
    outs = pl.pallas_call(
        body, name=name,
        out_shape=(*sems, *hbm, jax.ShapeDtypeStruct((8, 128), F32)),
        in_specs=(HBM_SPEC,) * (2 * n) + (pl.BlockSpec(memory_space=pl.ANY),) * len(extra),
        out_specs=(SEM_SPEC,) * (2 * n) + (HBM_SPEC,) * (2 * n) + (pl.BlockSpec(memory_space=pltpu.VMEM),),
        input_output_aliases={i: 2 * n + i for i in range(2 * n)},
        compiler_params=pltpu.CompilerParams(has_side_effects=DATAFLOW),
    )(*[pltpu.with_memory_space_constraint(a, pltpu.HBM) for a in list(srcs) + lands], *extra)
    return list(outs[:n]), list(outs[n:2 * n]), list(outs[2 * n:3 * n]), list(outs[3 * n:4 * n]), outs[-1]


def _split_wait(started, w, after, gather, name):
    send_sems, recv_sems, srcs, lands, _ = started

    def body(src_ref, land_ref, send_sems, recv_sems, after_ref, src_out, land_out):
        me = _my_place()
        for k in ALL_PEERS:
            cp = _split_copy(src_ref, land_ref, send_sems, recv_sems, me, k, gather)
            cp.wait_send()
            cp.wait_recv()
        _own_copy(src_ref, land_ref, recv_sems, me, gather).wait()

    return pl.pallas_call(
        body, name=name,
        out_shape=(pltpu.HBM(srcs[w].shape, srcs[w].dtype), pltpu.HBM(lands[w].shape, lands[w].dtype)),
        in_specs=(HBM_SPEC, HBM_SPEC, SEM_SPEC, SEM_SPEC, pl.BlockSpec(memory_space=pl.ANY)),
        out_specs=(HBM_SPEC, HBM_SPEC), input_output_aliases={0: 0, 1: 1},
        compiler_params=pltpu.CompilerParams(has_side_effects=DATAFLOW),
    )(srcs[w], lands[w], send_sems[w], recv_sems[w], after)[1]


def _forward_copy(land_ref, send_sems, recv_sems, me, j, incoming):
    sibling = _peer(me, 1)
    rows = land_ref.at[_index(_peer(sibling if incoming else me, SAME_CORE[j]))]
    return pltpu.make_async_remote_copy(src_ref=rows, dst_ref=rows, send_sem=send_sems.at[j], recv_sem=recv_sems.at[j],
                                        device_id=sibling, device_id_type=MESH)


def _forward_start(started, after, name):
    send_a, recv_a, srcs, lands, _ = started

    def body(src_ref, land_ref, send_a, recv_a, after_ref, send_b, recv_b, src_out, land_out):
        me = _my_place()
        for j, k in enumerate(SAME_CORE):
            _split_copy(src_ref, land_ref, send_a, recv_a, me, k, True).wait_recv()
            _forward_copy(land_ref, send_b, recv_b, me, j, False).start()

    sems = pltpu.SemaphoreType.DMA((len(SAME_CORE),))
    return pl.pallas_call(
        body, name=name,
        out_shape=(sems, sems, pltpu.HBM(srcs[0].shape, srcs[0].dtype), pltpu.HBM(lands[0].shape, lands[0].dtype)),
        in_specs=(HBM_SPEC, HBM_SPEC, SEM_SPEC, SEM_SPEC, pl.BlockSpec(memory_space=pl.ANY)),
        out_specs=(SEM_SPEC, SEM_SPEC, HBM_SPEC, HBM_SPEC), input_output_aliases={0: 2, 1: 3},
        compiler_params=pltpu.CompilerParams(has_side_effects=DATAFLOW),
    )(srcs[0], lands[0], send_a[0], recv_a[0], after)


def _forward_wait(started, forwarded, name):
    send_a, recv_a, _, _, _ = started
    send_b, recv_b, src, land = forwarded

    def body(src_ref, land_ref, send_a, recv_a, send_b, recv_b, src_out, land_out):
        me = _my_place()
        _own_copy(src_ref, land_ref, recv_a, me, True).wait()
        for k in (1,) + SAME_CORE:
            _split_copy(src_ref, land_ref, send_a, recv_a, me, k, True).wait_send()
        _split_copy(src_ref, land_ref, send_a, recv_a, me, 1, True).wait_recv()
        for j in range(len(SAME_CORE)):
            _forward_copy(land_ref, send_b, recv_b, me, j, False).wait_send()
            _forward_copy(land_ref, send_b, recv_b, me, j, True).wait_recv()

    return pl.pallas_call(
        body, name=name,
        out_shape=(pltpu.HBM(src.shape, src.dtype), pltpu.HBM(land.shape, land.dtype)),
        in_specs=(HBM_SPEC, HBM_SPEC, SEM_SPEC, SEM_SPEC, SEM_SPEC, SEM_SPEC),
        out_specs=(HBM_SPEC, HBM_SPEC), input_output_aliases={0: 0, 1: 1},
        compiler_params=pltpu.CompilerParams(has_side_effects=DATAFLOW),
    )(src, land, send_a[0], recv_a[0], send_b, recv_b)[1]


def _adam_math(w, g, m, v):
    m2 = ADAM_B1 * m + (1.0 - ADAM_B1) * g
    v2 = ADAM_B2 * v + (1.0 - ADAM_B2) * (g * g)
    m_hat = m2 / (1.0 - ADAM_B1 ** ADAM_STEP)
    v_hat = v2 / (1.0 - ADAM_B2 ** ADAM_STEP)
    delta = -ADAM_LR * (m_hat / (jnp.sqrt(v_hat) + ADAM_EPS) + ADAM_WD * w)
    return delta, m2, v2


def _adamw(land, w, m, v, name):
    a, b = w.shape
    bp = land.shape[2]
    ta = min(128, a)

    def body(p_ref, w_ref, m_ref, v_ref, g_ref, d_ref, m2_ref, v2_ref):
        g = p_ref[0, :, 0:b].astype(F32)
        for k in range(1, N_DEV):
            g = g + p_ref[k, :, 0:b].astype(F32)
        delta, m2, v2 = _adam_math(w_ref[...], g, m_ref[...], v_ref[...])
        g_ref[...] = g
        d_ref[...] = delta
        m2_ref[...] = m2
        v2_ref[...] = v2

    blk = pl.BlockSpec((ta, b), lambda i: (i, 0))
    sd = jax.ShapeDtypeStruct((a, b), F32)
    return pl.pallas_call(
        body, name=name, grid=(a // ta,),
        in_specs=[pl.BlockSpec((N_DEV, ta, bp), lambda i: (0, i, 0)), blk, blk, blk],
        out_specs=[blk, blk, blk, blk], out_shape=[sd, sd, sd, sd],
        compiler_params=_params(("parallel",)),
    )(land, w, m, v)


def _bucket_table():
    t_loc = jnp.arange(SWA_BLOCK)[:, None] + SWA_BLOCK
    s_loc = jnp.arange(2 * SWA_BLOCK)[None, :]
    dist = t_loc - s_loc
    max_exact = REL_BUCKETS // 2
    d = jnp.maximum(dist, 0)
    df = jnp.maximum(d, 1).astype(F32)
    large = max_exact + (jnp.log(df / max_exact) / math.log(REL_MAX_DIST / max_exact) * (REL_BUCKETS - max_exact)).astype(jnp.int32)
    large = jnp.minimum(large, REL_BUCKETS - 1)
    bucket = jnp.where(d < max_exact, d, large)
    band = (dist >= 0) & (dist < SWA_BLOCK)
    return bucket, band


def _tile2(g):
    return jnp.concatenate([g, g], axis=1) if g.shape[1] == HEAD else g


SHARD_W = 737
SHARD_WP = 768
IN_WIDTH = N_DEV * SHARD_W
SEGMENTS = ((GL0, 2824, 3072), (QF0, 768, 512), (KF0, 1280, 512), (VF0, 1792, 512), (QM0, 2312, 512),
            (QA0, 0, 512), (KA0, 512, 128), (VA0, 640, 128), (FL0, 2304, 8))


def _lane_plan(sources):
    plan = []
    for t in range(len(sources) // 128):
        groups = {}
        for lane in range(128):
            src = sources[128 * t + lane]
            if src is not None:
                slab, col = src
                groups.setdefault((slab, col // 128, (lane - col) % 128), []).append(lane)
        tile = []
        for key, lanes in groups.items():
            assert lanes == list(range(lanes[0], lanes[-1] + 1))
            tile.append((key, lanes[0], lanes[-1] + 1))
        plan.append(tile)
    return plan


def _assemble(tile_plan, load, rows):
    lane = lax.broadcasted_iota(jnp.int32, (1, 128), 1)
    out = jnp.zeros((rows, 128), F32)
    for (slab, st, roll), lo, hi in tile_plan:
        v = load(slab, st)
        if roll:
            v = pltpu.roll(v, roll, 1)
        out = v if (lo, hi) == (0, 128) else jnp.where((lane >= lo) & (lane < hi), v, out)
    return out


def _w_in_from_shards(land):
    ref_col = [None] * PROJ_W
    for p0, r0, n in SEGMENTS:
        for i in range(n):
            ref_col[p0 + i] = divmod(r0 + i, SHARD_W)
    plan = _lane_plan(ref_col)
    d_model = land.shape[1]
    tm = 256

    def body(land_ref, o_ref):
        load = lambda slab, st: land_ref[slab, :, st * 128:(st + 1) * 128].astype(F32)
        for t, tile_plan in enumerate(plan):
            o_ref[:, t * 128:(t + 1) * 128] = _assemble(tile_plan, load, tm).astype(BF16)

    return pl.pallas_call(
        body, name="w_in_from_shards", grid=(d_model // tm,),
        in_specs=[pl.BlockSpec((N_DEV, tm, SHARD_WP), lambda i: (0, i, 0))],
        out_specs=pl.BlockSpec((tm, PROJ_W), lambda i: (i, 0)),
        out_shape=jax.ShapeDtypeStruct((d_model, PROJ_W), BF16),
        compiler_params=_params(("parallel",)),
    )(land)


def _dw_in_to_parts(dwp):
    padded_col = [None] * IN_WIDTH
    for p0, r0, n in SEGMENTS:
        for i in range(n):
            padded_col[r0 + i] = p0 + i
    sources = []
    for d in range(N_DEV):
        sources += [(0, padded_col[SHARD_W * d + c]) if c < SHARD_W else None for c in range(SHARD_WP)]
    plan = _lane_plan(sources)
    d_model = dwp.shape[0]
    tm = 256
    tiles = SHARD_WP // 128

    def body(dw_ref, o_ref):
        load = lambda slab, st: dw_ref[:, st * 128:(st + 1) * 128].astype(F32)
        for t, tile_plan in enumerate(plan):
            d, c = divmod(t, tiles)
            o_ref[d, :, c * 128:(c + 1) * 128] = _assemble(tile_plan, load, tm).astype(BF16)

    return pl.pallas_call(
        body, name="dw_in_to_parts", grid=(d_model // tm,),
        in_specs=[pl.BlockSpec((tm, PROJ_W), lambda i: (i, 0))],
        out_specs=pl.BlockSpec((N_DEV, tm, SHARD_WP), lambda i: (0, i, 0)),
        out_shape=jax.ShapeDtypeStruct((N_DEV, d_model, SHARD_WP), BF16),
        compiler_params=_params(("parallel",)),
    )(dwp)


def _cast_shards(shards):
    names = list(shards)

    def body(*refs):
        for src, dst in zip(refs[:len(names)], refs[len(names):]):
            if dst.shape != src.shape:
                dst[...] = jnp.zeros(dst.shape, BF16)
                dst[:, 0:src.shape[1]] = src[...].astype(BF16)
            else:
                dst[...] = src[...].astype(BF16)

    out_shape = [jax.ShapeDtypeStruct((shards[n].shape[0], SHARD_WP if n == "w_in" else shards[n].shape[1]), BF16)
                 for n in names]
    outs = pl.pallas_call(body, name="cast_shards", out_shape=out_shape,
                          compiler_params=pltpu.CompilerParams(vmem_limit_bytes=VMEM_LIMIT))(*[shards[n] for n in names])
    return dict(zip(names, outs))


def _tie(x, *tokens):
    for t in tokens:
        if t is not None:
            x = x + t[0:1, 0:1]
    return x


def _local_step(x, mem, target, p, getw, emit, deps=()):
    s = x.shape[0]
    bucket, band = _bucket_table()
    bucket_m = jnp.where(band, bucket, -1).astype(jnp.int32)
    bias = _bias_table(p["rel_bias"], bucket_m)
    bucket_t = jnp.transpose(bucket_m)
    bias_t = _bias_table(p["rel_bias"], bucket_t)
    gqf, gkf, gqa, gka = _tile2(p["qn_fox"]), _tile2(p["kn_fox"]), _tile2(p["qn_swa"]), _tile2(p["kn_swa"])
    gqm = p["qn_mem"]
    bf128 = jnp.pad(p["b_forget"], ((0, 0), (0, 120)))
    sink = p["sink_swa"].reshape(8)

    h = _rms_fwd(x, p["g_mix"], "rms_mix", deps)
    w_in = getw("w_in", h)
    proj = _mm(h, w_in, "nn", BF16, 512, 1536, 1024, "proj")
    fl = _mm(h, w_in[:, FL0:FL0 + 128], "nn", F32, 512, 128, 1024, "proj_fl")
    qf, kf, vf, qm, qa, ka, va = _proj_post(proj, gqf, gkf, gqm, gqa, gka)
    cc4 = _fox_gate_fwd(fl, bf128)
    w_kv = getw("w_mem_kv", cc4)
    mem_n, kv_raw, mk, mv = _memkv_fwd(mem, p["g_mem"], w_kv, p["kn_mem"])
    kp = jnp.pad(ka, ((SWA_BLOCK, 0), (0, 0)))
    vp = jnp.pad(va, ((SWA_BLOCK, 0), (0, 0)))
    oa = _swa_fwd(qa, kp, vp, bias, sink)
    of, lse4 = _fox_fwd(qf, kf, jnp.transpose(vf), cc4)
    om = _mem_fwd(qm, mk, mv)
    wa, wf, wm, w_out = getw("w_o_swa", oa), getw("w_o_fox", oa), getw("w_o_mem", oa), getw("w_out", oa)
    x1, hm, merged = _merge_fwd(x, oa, of, om, proj, p["b_gate"], wa, wf, wm, w_out, p["g_mlp"])
    w_up = getw("w_mlp_up", of)
    u = _mlp_up(hm, w_up)
    w_down = getw("w_mlp_down", hm)
    dy, dy_b, loss = _mlp_down_loss(u, w_down, x1, target)

    da = _mlp_bwd_act(dy_b, w_down, u)
    t_down = emit({"w_mlp_down": _mm(u, dy_b, "tn", BF16, 1024, 1024, 2048, "dw_down")})
    dx1, dg_mlp = _mlp_bwd_x(da, w_up, x1, dy, _tie(p["g_mlp"], t_down))
    t_up = emit({"w_mlp_up": _mm(hm, da, "tn", BF16, 1024, 1024, 2048, "dw_up", column_chunks=True)})
    dproj, doa, dof, dom, dya, dyf, dym, db_gate = _merge_bwd(
        dx1, oa, of, om, proj, _tie(p["b_gate"], t_up), wa, wf, wm, w_out)
    t_o = emit({"w_out": _mm(merged, dx1, "tn", BF16, 1024, 1024, 2048, "dw_out"),
                "w_o_swa": _mm(oa, dya, "tn", BF16, 512, 1024, 2048, "dw_o_swa"),
                "w_o_fox": _mm(of, dyf, "tn", BF16, 512, 1024, 2048, "dw_o_fox"),
                "w_o_mem": _mm(om, dym, "tn", BF16, 512, 1024, 2048, "dw_o_mem")})

    dqm, dmk, dmv = _mem_bwd(qm, mk, mv, dom)
    dw_kv, dkn_mem, dg_mem = _memkv_bwd(dmk, dmv, kv_raw, _tie(p["kn_mem"], t_o), mem, p["g_mem"], mem_n, w_kv)
    t_kv = emit({"w_mem_kv": dw_kv})
    dqa, dkp, dvp, dbias, dsink = _swa_bwd(qa, kp, vp, bias_t, _tie(p["sink_swa"], t_kv).reshape(8), doa)
    qf_t = jnp.transpose(qf * jnp.asarray(0.125, BF16))
    dqf_t, dkf, dvf, dck4, dcq4 = _fox_bwd(qf_t, kf, vf, jnp.transpose(dof), jnp.transpose(of), cc4, lse4)
    dqf = jnp.transpose(dqf_t)

    dcq = jnp.transpose(dcq4[:, 0:2, :], (2, 0, 1)).reshape(s, 8)
    dck = jnp.transpose(dck4[:, :, 0:2], (1, 0, 2)).reshape(s, 8)
    dc = jnp.pad(dcq - dck, ((0, 0), (0, 120)))
    dfl, db_forget = _fox_gate_bwd(dc, fl, bf128)

    dproj, dgn = _proj_pre_bwd(dproj, proj, dqf, dkf, dvf, dqm, dqa, dkp[SWA_BLOCK:], dvp[SWA_BLOCK:], dfl,
                               gqf, gkf, gqm, gqa, gka)
    t_in = emit({"w_in": _mm(h, dproj, "tn", BF16, 1024, 3072, 1024, "dw_in")})
    grad_x, dg_mix = _in_bwd_x(dproj, w_in, x, _tie(p["g_mix"], t_in), dx1)
    d_rel = _rel_bias_bwd(dbias, bucket_t)

    fold = lambda r: dgn[r:r + 1, 0:HEAD] + dgn[r:r + 1, HEAD:128]
    small = {
        "g_mix": dg_mix, "b_gate": db_gate, "b_forget": db_forget[:, 0:8],
        "qn_swa": fold(3), "kn_swa": fold(4), "sink_swa": dsink[:, 0].reshape(1, 8), "rel_bias": d_rel,
        "qn_fox": fold(0), "kn_fox": fold(1), "g_mem": dg_mem, "qn_mem": dgn[2:3, :], "kn_mem": dkn_mem,
        "g_mlp": dg_mlp,
    }
    return loss, grad_x, small


SMALL = ("g_mix", "b_gate", "b_forget", "qn_swa", "kn_swa", "sink_swa", "rel_bias", "qn_fox", "kn_fox", "g_mem",
         "qn_mem", "kn_mem", "g_mlp")
BIG = ("w_in", "w_mem_kv", "w_o_swa", "w_o_fox", "w_o_mem", "w_out", "w_mlp_up", "w_mlp_down")
COL_SHARDED = ("w_in", "w_o_swa", "w_o_fox", "w_o_mem", "w_mlp_up")
WEIGHTS = ("g_mix", "w_in", "b_gate", "b_forget", "qn_swa", "kn_swa", "sink_swa", "rel_bias", "qn_fox", "kn_fox", "g_mem",
           "w_mem_kv", "qn_mem", "kn_mem", "w_o_swa", "w_o_fox", "w_o_mem", "w_out", "g_mlp", "w_mlp_up", "w_mlp_down")
SMALL_SLOTS = (("g_mix", 1024), ("b_gate", 3072), ("b_forget", 128), ("qn_swa", 128), ("kn_swa", 128), ("sink_swa", 128),
               ("rel_bias", REL_BUCKETS * 128), ("qn_fox", 128), ("kn_fox", 128), ("g_mem", 1024), ("qn_mem", 128),
               ("kn_mem", 128), ("g_mlp", 1024), ("loss", 128))
SMALL_OFF = {n: sum(w for _, w in SMALL_SLOTS[:i]) for i, (n, _) in enumerate(SMALL_SLOTS)}
SMALL_ROW = sum(w for _, w in SMALL_SLOTS)


def _gathered_to_full(name, g):
    if name in COL_SHARDED:
        return jnp.transpose(g, (1, 0, 2)).reshape(g.shape[1], N_DEV * g.shape[2])
    return g.reshape(N_DEV * g.shape[1], g.shape[2])


def _full_to_parts(name, full, b):
    if name in COL_SHARDED:
        return jnp.transpose(full.reshape(full.shape[0], N_DEV, b), (1, 0, 2)).astype(BF16)
    return full.reshape(N_DEV, full.shape[0] // N_DEV, full.shape[1]).astype(BF16)


def _pack_small(grads, loss):
    pieces = []
    for n, width in SMALL_SLOTS:
        a = loss.reshape(1, 1) if n == "loss" else grads[n].reshape(1, -1)
        pieces.append(jnp.pad(a, ((0, 0), (0, width - a.shape[1]))))
    return jnp.concatenate(pieces, axis=1)


def _adamw_small(gathered, w, m, v):
    names = list(SMALL)

    def body(*refs):
        p_ref = refs[0]
        ins = refs[1:1 + 3 * len(names)]
        outs = refs[1 + 3 * len(names):]
        g_all = p_ref[0]
        for k in range(1, N_DEV):
            g_all = g_all + p_ref[k]
        for i, n in enumerate(names):
            w_ref, m_ref, v_ref = ins[3 * i:3 * i + 3]
            out = outs[4 * i:4 * i + 4]
            rows, cols = w_ref.shape
            for r in range(rows):
                off = SMALL_OFF[n] + 128 * r
                g = g_all[:, off:off + cols]
                rs = slice(r, r + 1)
                res = (g,) + _adam_math(w_ref[rs, :], g, m_ref[rs, :], v_ref[rs, :])
                for o_ref, val in zip(out, res):
                    o_ref[rs, :] = val
        outs[-1][...] = g_all[:, SMALL_OFF["loss"]:SMALL_OFF["loss"] + 128]

    args = [gathered]
    out_shape = []
    for n in names:
        args += [w[n], m[n], v[n]]
        out_shape += [jax.ShapeDtypeStruct(w[n].shape, F32)] * 4
    out_shape.append(jax.ShapeDtypeStruct((1, 128), F32))
    outs = pl.pallas_call(body, name="adamw_small", out_shape=out_shape)(*args)
    return {n: outs[4 * i:4 * i + 4] for i, n in enumerate(names)}, outs[-1]


def kernel(x, mem, g_mix, w_in, b_gate, b_forget, qn_swa, kn_swa, sink_swa, rel_bias, qn_fox, kn_fox, g_mem, w_mem_kv, qn_mem, kn_mem, w_o_swa, w_o_fox, w_o_mem, w_out, g_mlp, w_mlp_up, w_mlp_down, loss_target, m_g_mix, m_w_in, m_b_gate, m_b_forget, m_qn_swa, m_kn_swa, m_sink_swa, m_rel_bias, m_qn_fox, m_kn_fox, m_g_mem, m_w_mem_kv, m_qn_mem, m_kn_mem, m_w_o_swa, m_w_o_fox, m_w_o_mem, m_w_out, m_g_mlp, m_w_mlp_up, m_w_mlp_down, v_g_mix, v_w_in, v_b_gate, v_b_forget, v_qn_swa, v_kn_swa, v_sink_swa, v_rel_bias, v_qn_fox, v_kn_fox, v_g_mem, v_w_mem_kv, v_qn_mem, v_kn_mem, v_w_o_swa, v_w_o_fox, v_w_o_mem, v_w_out, v_g_mlp, v_w_mlp_up, v_w_mlp_down):
    wts = dict(g_mix=g_mix, w_in=w_in, b_gate=b_gate, b_forget=b_forget, qn_swa=qn_swa, kn_swa=kn_swa, sink_swa=sink_swa,
               rel_bias=rel_bias, qn_fox=qn_fox, kn_fox=kn_fox, g_mem=g_mem, w_mem_kv=w_mem_kv, qn_mem=qn_mem, kn_mem=kn_mem,
               w_o_swa=w_o_swa, w_o_fox=w_o_fox, w_o_mem=w_o_mem, w_out=w_out, g_mlp=g_mlp, w_mlp_up=w_mlp_up,
               w_mlp_down=w_mlp_down)
    mom = dict(g_mix=m_g_mix, w_in=m_w_in, b_gate=m_b_gate, b_forget=m_b_forget, qn_swa=m_qn_swa, kn_swa=m_kn_swa,
               sink_swa=m_sink_swa, rel_bias=m_rel_bias, qn_fox=m_qn_fox, kn_fox=m_kn_fox, g_mem=m_g_mem, w_mem_kv=m_w_mem_kv,
               qn_mem=m_qn_mem, kn_mem=m_kn_mem, w_o_swa=m_w_o_swa, w_o_fox=m_w_o_fox, w_o_mem=m_w_o_mem, w_out=m_w_out,
               g_mlp=m_g_mlp, w_mlp_up=m_w_mlp_up, w_mlp_down=m_w_mlp_down)
    var = dict(g_mix=v_g_mix, w_in=v_w_in, b_gate=v_b_gate, b_forget=v_b_forget, qn_swa=v_qn_swa, kn_swa=v_kn_swa,
               sink_swa=v_sink_swa, rel_bias=v_rel_bias, qn_fox=v_qn_fox, kn_fox=v_kn_fox, g_mem=v_g_mem, w_mem_kv=v_w_mem_kv,
               qn_mem=v_qn_mem, kn_mem=v_kn_mem, w_o_swa=v_w_o_swa, w_o_fox=v_w_o_fox, w_o_mem=v_w_o_mem, w_out=v_w_out,
               g_mlp=v_g_mlp, w_mlp_up=v_w_mlp_up, w_mlp_down=v_w_mlp_down)

    shards = _cast_shards({n: wts[n][0] for n in BIG})
    first = _split_start([shards["w_in"]], True, "ag_start_w_in", peers=(1,) + SAME_CORE)
    rest = _split_start([shards[n] for n in BIG[1:]], True, "ag_start_rest", after=first[4])
    full = {}

    def getw(n, after):
        if n == "w_in" and n not in full:
            forwarded = _forward_start(first, after, "ag_forward_w_in")
            full[n] = _w_in_from_shards(_forward_wait(first, forwarded, "ag_wait_w_in"))
        elif n not in full:
            land = _split_wait(rest, BIG[1:].index(n), after, True, "ag_wait_" + n)
            full[n] = land if n == "w_mlp_up" else _gathered_to_full(n, land)
        return full[n]

    exchanges = {}

    def emit(grads_by_name):
        parts = []
        for n, grad in grads_by_name.items():
            if n == "w_in":
                parts.append(_dw_in_to_parts(grad))
            else:
                parts.append(grad if n == "w_mlp_up" else _full_to_parts(n, grad, wts[n].shape[2]))
        started = _split_start(parts, False, "rs_start_" + next(iter(grads_by_name)))
        for w, n in enumerate(grads_by_name):
            exchanges[n] = (started, w)
        return started[4]

    small_p = {n: wts[n] for n in SMALL}
    loss, grad_x, small_g = _local_step(x[0], mem[0], loss_target[0], small_p, getw, emit, (first[4], rest[4]))

    packed = _pack_small(small_g, loss)
    small_gather = _split_start([packed], True, "ag_start_small")

    grads, delta, new_m, new_v = {}, {}, {}, {}

    def update(n, after):
        land = _split_wait(*exchanges[n], after, False, "rs_wait_" + n)
        g, d, m2, v2 = _adamw(land, wts[n][0], mom[n][0], var[n][0], "adamw_" + n)
        grads[n], delta[n], new_m[n], new_v[n] = g[None], d[None], m2[None], v2[None]
        return d

    after = small_gather[4]
    for n in exchanges:
        if n != "w_in":
            after = update(n, after)

    gathered = _split_wait(small_gather, 0, after, True, "ag_wait_small")
    small_out, total = _adamw_small(gathered, small_p, mom, var)
    for name, (g, d, m2, v2) in small_out.items():
        grads[name], delta[name], new_m[name], new_v[name] = g, d, m2, v2
    update("w_in", total)

    return (total[0, 0], grad_x[None], *[grads[n] for n in WEIGHTS], *[delta[n] for n in WEIGHTS],
            *[new_m[n] for n in WEIGHTS], *[new_v[n] for n in WEIGHTS])
```

```python
import functools
import math

import jax
import jax.numpy as jnp
from jax import lax
from jax.experimental import pallas as pl
from jax.experimental.pallas import tpu as pltpu

F32 = jnp.float32
BF16 = jnp.bfloat16

D_MODEL = 1024
N_MEM = 256
D_FF = 4096
HEAD = 64
SWA_HEADS = 8
SWA_BLOCK = 128
MEM_HEADS = 4
MEM_HEAD = 128
EPS = 1e-6
NEG = -1e30
REL_BUCKETS = 32
REL_MAX_DIST = 128

ADAM_LR = 0.001
ADAM_B1 = 0.9
ADAM_B2 = 0.999
ADAM_EPS = 1e-08
ADAM_WD = 0.01
ADAM_STEP = 10

GL0, QF0, KF0, VF0, QM0, QA0, KA0, VA0, FL0 = 0, 3072, 3584, 4096, 4608, 5120, 5632, 5760, 5888
PROJ_W = 6144
HALF_W = 3072
H_QF, H_KF, H_VF, H_QM, H_QA, H_KA, H_VA, H_FL = 0, 512, 1024, 1536, 2048, 2560, 2688, 2816

VMEM_LIMIT = 56 * 1024 * 1024
N_DEV = 8
MESH = pl.DeviceIdType.MESH

NN = (((1,), (0,)), ((), ()))
NT = (((1,), (1,)), ((), ()))
TN = (((0,), (0,)), ((), ()))


def _dot(a, b, dims=NN):
    return lax.dot_general(a, b, dims, preferred_element_type=F32)


def _params(sem):
    return pltpu.CompilerParams(dimension_semantics=sem, vmem_limit_bytes=VMEM_LIMIT)


def _full(shape):
    nd = len(shape)
    return pl.BlockSpec(shape, lambda *_: (0,) * nd)


def _sigmoid(z):
    return 1.0 / (1.0 + jnp.exp(-z))


def _group_mean(v, hd):
    if hd == 128:
        return jnp.mean(v, axis=-1, keepdims=True)
    lane = lax.broadcasted_iota(jnp.int32, v.shape, 1)
    lo = lane < HEAD
    s_lo = jnp.sum(jnp.where(lo, v, 0.0), axis=-1, keepdims=True)
    s_hi = jnp.sum(jnp.where(lo, 0.0, v), axis=-1, keepdims=True)
    return jnp.where(lo, s_lo, s_hi) * (1.0 / HEAD)


def _mm(a, b, mode, out_dtype, tm, tn, tk, name, column_chunks=False):
    if mode == "nn":
        m, k = a.shape
        n = b.shape[1]
    elif mode == "nt":
        m, k = a.shape
        n = b.shape[0]
    else:
        k, m = a.shape
        n = b.shape[1]
    tm, tn, tk = min(tm, m), min(tn, n), min(tk, k)
    nk = k // tk
    chunk = n // N_DEV
    per_tile = tn // chunk if column_chunks else 1
    dims = {"nn": NN, "nt": NT, "tn": TN}[mode]
    a_spec = pl.BlockSpec((tk, tm), lambda j, i, kk: (kk, i)) if mode == "tn" else pl.BlockSpec((tm, tk), lambda j, i, kk: (i, kk))
    b_spec = pl.BlockSpec((tn, tk), lambda j, i, kk: (j, kk)) if mode == "nt" else pl.BlockSpec((tk, tn), lambda j, i, kk: (kk, j))

    def body(a_ref, b_ref, o_ref, *acc):
        prod = _dot(a_ref[...].astype(BF16), b_ref[...].astype(BF16), dims)

        def write(res):
            if column_chunks:
                for c in range(per_tile):
                    o_ref[c] = res[:, c * chunk:(c + 1) * chunk].astype(o_ref.dtype)
            else:
                o_ref[...] = res.astype(o_ref.dtype)

        if nk == 1:
            write(prod)
        else:
            acc_ref, = acc
            kk = pl.program_id(2)

            @pl.when(kk == 0)
            def _():
                acc_ref[...] = prod

            @pl.when(kk > 0)
            def _():
                acc_ref[...] += prod

            @pl.when(kk == nk - 1)
            def _():
                write(acc_ref[...])

    return pl.pallas_call(
        body, name=name, grid=(n // tn, m // tm, nk),
        in_specs=[a_spec, b_spec],
        out_specs=(pl.BlockSpec((per_tile, tm, chunk), lambda j, i, kk: (j, i, 0)) if column_chunks
                   else pl.BlockSpec((tm, tn), lambda j, i, kk: (i, j))),
        out_shape=jax.ShapeDtypeStruct((N_DEV, m, chunk) if column_chunks else (m, n), out_dtype),
        scratch_shapes=[pltpu.VMEM((tm, tn), F32)] if nk > 1 else [],
        compiler_params=_params(("parallel", "parallel", "arbitrary")),
    )(a, b)


def _rms_fwd(x, g, name, deps=()):
    s, d = x.shape
    tm = min(512, s)

    def body(x_ref, g_ref, *rest):
        h_ref = rest[len(deps)]
        xv = x_ref[...]
        r = lax.rsqrt(jnp.mean(xv * xv, axis=-1, keepdims=True) + EPS)
        h_ref[...] = (xv * r * g_ref[...]).astype(BF16)

    return pl.pallas_call(
        body, name=name, grid=(s // tm,),
        in_specs=[pl.BlockSpec((tm, d), lambda i: (i, 0)), _full((1, d))] + [pl.BlockSpec(memory_space=pl.ANY)] * len(deps),
        out_specs=pl.BlockSpec((tm, d), lambda i: (i, 0)),
        out_shape=jax.ShapeDtypeStruct((s, d), BF16),
        compiler_params=_params(("parallel",)),
    )(x, g, *deps)


def _proj_post(proj, gq_fox, gk_fox, gq_mem, gq_swa, gk_swa):
    s = proj.shape[0]
    tm = min(256, s)

    def body(p_ref, gqf, gkf, gqm, gqa, gka, qf_ref, kf_ref, vf_ref, qm_ref, qa_ref, ka_ref, va_ref, qft_ref, vft_ref):
        def norm(off, width, hd, g_ref, o_ref, scaled_t_ref=None):
            for b in range(width // 128):
                v = p_ref[:, off + b * 128: off + (b + 1) * 128].astype(F32)
                r = lax.rsqrt(_group_mean(v * v, hd) + EPS)
                vn = (v * r * g_ref[...]).astype(BF16)
                o_ref[:, b * 128:(b + 1) * 128] = vn
                if scaled_t_ref is not None:
                    scaled_t_ref[b * 128:(b + 1) * 128, :] = (vn.astype(F32) * 0.125).T.astype(BF16)

        norm(H_QF, 512, HEAD, gqf, qf_ref, qft_ref)
        norm(H_KF, 512, HEAD, gkf, kf_ref)
        vf_ref[...] = p_ref[:, H_VF:H_VF + 512].astype(BF16)
        for b in range(4):
            vft_ref[b * 128:(b + 1) * 128, :] = p_ref[:, H_VF + b * 128:H_VF + (b + 1) * 128].astype(F32).T.astype(BF16)
        norm(H_QM, 512, MEM_HEAD, gqm, qm_ref)
        norm(H_QA, 512, HEAD, gqa, qa_ref)
        norm(H_KA, 128, HEAD, gka, ka_ref)
        va_ref[...] = p_ref[:, H_VA:H_VA + 128].astype(BF16)

    g_spec = _full((1, 128))
    o512 = pl.BlockSpec((tm, 512), lambda i: (i, 0))
    o128 = pl.BlockSpec((tm, 128), lambda i: (i, 0))
    s512 = jax.ShapeDtypeStruct((s, 512), BF16)
    s128 = jax.ShapeDtypeStruct((s, 128), BF16)
    return pl.pallas_call(
        body, name="proj_post", grid=(s // tm,),
        in_specs=[pl.BlockSpec((tm, HALF_W), lambda i: (i, 1)), g_spec, g_spec, g_spec, g_spec, g_spec],
        out_specs=[o512, o512, o512, o512, o512, o128, o128] + [pl.BlockSpec((512, tm), lambda i: (0, i))] * 2,
        out_shape=[s512, s512, s512, s512, s512, s128, s128] + [jax.ShapeDtypeStruct((512, s), BF16)] * 2,
        compiler_params=_params(("parallel",)),
    )(proj, gq_fox, gk_fox, gq_mem, gq_swa, gk_swa)


def _tri(n, lower):
    r = lax.broadcasted_iota(jnp.int32, (n, n), 0)
    c = lax.broadcasted_iota(jnp.int32, (n, n), 1)
    return jnp.where((c <= r) if lower else (c >= r), 1.0, 0.0).astype(F32)


def _fox_gate_fwd(proj, b_forget128):
    s = proj.shape[0]
    tm = min(512, s)

    def body(p_ref, b_ref, cc_ref, carry_ref):
        i = pl.program_id(0)

        @pl.when(i == 0)
        def _():
            carry_ref[...] = jnp.zeros_like(carry_ref)

        z = p_ref[...] + b_ref[...]
        logf = jnp.minimum(z, 0.0) - jnp.log(1.0 + jnp.exp(-jnp.abs(z)))
        c = jnp.dot(_tri(tm, True), logf, precision=lax.Precision.HIGHEST, preferred_element_type=F32) + carry_ref[...]
        carry_ref[...] = c[tm - 1:tm, :]
        for hp in range(4):
            cc_ref[hp] = c if hp == 0 else pltpu.roll(c, 128 - 2 * hp, 1)

    return pl.pallas_call(
        body, name="fox_gate_fwd", grid=(s // tm,),
        in_specs=[pl.BlockSpec((tm, 128), lambda i: (i, 0)), _full((1, 128))],
        out_specs=pl.BlockSpec((4, tm, 128), lambda i: (0, i, 0)),
        out_shape=jax.ShapeDtypeStruct((4, s, 128), F32),
        scratch_shapes=[pltpu.VMEM((1, 128), F32)],
        compiler_params=_params(("arbitrary",)),
    )(proj, b_forget128)


def _memkv_fwd(mem, g_mem, w_kv, kn_mem):
    m = mem.shape[0]

    def body(mem_ref, g_ref, w_ref, kn_ref, memn_ref, kv_ref, mk_ref, mv_ref):
        xv = mem_ref[...]
        r = lax.rsqrt(jnp.mean(xv * xv, axis=-1, keepdims=True) + EPS)
        mn = (xv * r * g_ref[...]).astype(BF16)
        memn_ref[...] = mn
        kv = _dot(mn, w_ref[...])
        kv_ref[...] = kv
        for h in range(MEM_HEADS):
            v = kv[:, h * 128:(h + 1) * 128]
            rr = lax.rsqrt(jnp.mean(v * v, axis=-1, keepdims=True) + EPS)
            mk_ref[:, h * 128:(h + 1) * 128] = (v * rr * kn_ref[...]).astype(BF16)
        mv_ref[...] = kv[:, 512:1024].astype(BF16)

    return pl.pallas_call(
        body, name="memkv_fwd",
        out_shape=[jax.ShapeDtypeStruct((m, D_MODEL), BF16), jax.ShapeDtypeStruct((m, 1024), F32),
                   jax.ShapeDtypeStruct((m, 512), BF16), jax.ShapeDtypeStruct((m, 512), BF16)],
        compiler_params=pltpu.CompilerParams(vmem_limit_bytes=VMEM_LIMIT),
    )(mem, g_mem, w_kv, kn_mem)


def _bias_table(rel_bias, bucket):
    def body(rb_ref, bk_ref, o_ref):
        bk = bk_ref[...]
        for h in range(SWA_HEADS):
            acc = jnp.zeros(bk.shape, F32)
            for b in range(REL_BUCKETS):
                acc = jnp.where(bk == b, rb_ref[b, h], acc)
            o_ref[h] = acc

    return pl.pallas_call(
        body, name="bias_table",
        in_specs=[pl.BlockSpec(memory_space=pltpu.SMEM), pl.BlockSpec(memory_space=pltpu.VMEM)],
        out_shape=jax.ShapeDtypeStruct((SWA_HEADS,) + bucket.shape, F32),
    )(rel_bias, bucket)


def _swa_valid(n):
    row = lax.broadcasted_iota(jnp.int32, (SWA_BLOCK, 2 * SWA_BLOCK), 0)
    col = lax.broadcasted_iota(jnp.int32, (SWA_BLOCK, 2 * SWA_BLOCK), 1)
    dist = row + SWA_BLOCK - col
    return (dist >= 0) & (dist < SWA_BLOCK) & ((col >= SWA_BLOCK) | (n > 0))


def _swa_fwd(qa, kp, vp, bias, sink):
    s = qa.shape[0]
    nb = s // SWA_BLOCK

    def body(sink_ref, q_ref, kp_ref, vp_ref, bias_ref, o_ref):
        n = pl.program_id(0)
        start = pl.multiple_of(n * SWA_BLOCK, SWA_BLOCK)
        k2 = kp_ref[pl.ds(start, 2 * SWA_BLOCK), :]
        v2 = vp_ref[pl.ds(start, 2 * SWA_BLOCK), :]
        valid = _swa_valid(n)
        heads = range(SWA_HEADS)
        hs = lambda h: slice(h * HEAD, (h + 1) * HEAD)
        sc = [jnp.where(valid, _dot(q_ref[:, hs(h)], k2[:, hs(h // 4)], NT) * 0.125 + bias_ref[h], NEG) for h in heads]
        pn = []
        for h in heads:
            sk = sink_ref[h]
            mx = jnp.maximum(jnp.max(sc[h], axis=-1, keepdims=True), sk)
            p = jnp.exp(sc[h] - mx)
            den = jnp.sum(p, axis=-1, keepdims=True) + jnp.exp(sk - mx)
            pn.append((p / den).astype(BF16))
        outs = [_dot(pn[h], v2[:, hs(h // 4)]).astype(BF16) for h in heads]
        for h in heads:
            o_ref[:, hs(h)] = outs[h]

    return pl.pallas_call(
        body, name="swa_fwd", grid=(nb,),
        in_specs=[pl.BlockSpec(memory_space=pltpu.SMEM),
                  pl.BlockSpec((SWA_BLOCK, 512), lambda n: (n, 0)),
                  _full(kp.shape), _full(vp.shape), _full(bias.shape)],
        out_specs=pl.BlockSpec((SWA_BLOCK, 512), lambda n: (n, 0)),
        out_shape=jax.ShapeDtypeStruct((s, 512), BF16),
        compiler_params=_params(("parallel",)),
    )(sink, qa, kp, vp, bias)


def _head_mask(e):
    lane = lax.broadcasted_iota(jnp.int32, (1, 128), 1)
    return (lane >= e * HEAD) & (lane < (e + 1) * HEAD)


FOX_FWD_TQ, FOX_FWD_TK = 1024, 1024
FOX_BWD_TK, FOX_BWD_TQ = 512, 512


def _head_rows(e):
    row = lax.broadcasted_iota(jnp.int32, (128, 1), 0)
    return (row >= e * HEAD) & (row < (e + 1) * HEAD)


def _fox_fwd(q, k, v_t, cc4):
    s = q.shape[0]
    t = min(FOX_FWD_TQ, s)
    tk = min(FOX_FWD_TK, s)
    nq = s // t

    def body(q_ref, k_ref, vt_ref, cc_ref, o_ref, lse_ref, ot_ref):
        i = pl.program_id(1)
        qs = q_ref[...] * jnp.asarray(0.125, BF16)
        qe = [jnp.where(_head_mask(e), qs, jnp.zeros_like(qs)) for e in range(2)]
        n_full = (i * t) // tk
        krow = lax.broadcasted_iota(jnp.int32, (tk, t), 0) + n_full * tk
        qcol = lax.broadcasted_iota(jnp.int32, (tk, t), 1) + i * t

        def step(j, carry, masked):
            ks = pl.ds(pl.multiple_of(j * tk, tk), tk)
            kj = k_ref[ks, :]
            vtj = vt_ref[:, ks]
            out = []
            for e in range(2):
                m, acc = carry[2 * e], carry[2 * e + 1]
                st = _dot(kj, qe[e], NT) - cc_ref[0, ks, e:e + 1]
                if masked:
                    st = jnp.where(krow <= qcol, st, NEG)
                m_new = jnp.maximum(m, jnp.max(st, axis=0, keepdims=True))
                alpha = jnp.exp(m - m_new)
                pt = jnp.exp(st - m_new).astype(BF16)
                vte = jnp.where(_head_rows(e), vtj, jnp.ones_like(vtj))
                out += [m_new, alpha * acc + _dot(vte, pt)]
            return tuple(out)

        init = (jnp.full((1, t), NEG, F32), jnp.zeros((128, t), F32)) * 2
        carry = lax.fori_loop(0, n_full, functools.partial(step, masked=False), init)
        m0, a0, m1, a1 = step(n_full, carry, True)
        l0 = a0[HEAD:HEAD + 1, :]
        l1 = a1[0:1, :]
        o_t = jnp.where(_head_rows(0), a0 / l0, a1 / l1)
        o_ref[...] = o_t.T.astype(BF16)
        ot_ref[...] = o_t.astype(BF16)
        r8 = lax.broadcasted_iota(jnp.int32, (8, t), 0)
        lse_ref[0] = jnp.where(r8 == 0, m0 + jnp.log(l0), jnp.where(r8 == 1, m1 + jnp.log(l1), 0.0))

    return pl.pallas_call(
        body, name="fox_fwd", grid=(4, nq),
        in_specs=[pl.BlockSpec((t, 128), lambda hp, i: (i, hp)),
                  pl.BlockSpec((s, 128), lambda hp, i: (0, hp)),
                  pl.BlockSpec((128, s), lambda hp, i: (hp, 0)),
                  pl.BlockSpec((1, s, 128), lambda hp, i: (hp, 0, 0))],
        out_specs=[pl.BlockSpec((t, 128), lambda hp, i: (i, hp)),
                   pl.BlockSpec((1, 8, t), lambda hp, i: (hp, 0, i)),
                   pl.BlockSpec((128, t), lambda hp, i: (hp, i))],
        out_shape=[jax.ShapeDtypeStruct((s, 512), BF16), jax.ShapeDtypeStruct((4, 8, s), F32),
                   jax.ShapeDtypeStruct((512, s), BF16)],
        compiler_params=_params(("parallel", "parallel")),
    )(q, k, v_t, cc4)


MEM_SCALE = MEM_HEAD ** -0.5


def _mem_fwd(qm, mk, mv):
    s = qm.shape[0]
    tq = min(512, s)

    def body(q_ref, mk_ref, mv_ref, o_ref):
        for h in range(MEM_HEADS):
            hs = slice(h * 128, (h + 1) * 128)
            sc = _dot(q_ref[:, hs], mk_ref[:, hs], NT) * MEM_SCALE
            mx = jnp.max(sc, axis=-1, keepdims=True)
            p = jnp.exp(sc - mx)
            p = p / jnp.sum(p, axis=-1, keepdims=True)
            o_ref[:, hs] = _dot(p.astype(BF16), mv_ref[:, hs]).astype(BF16)

    return pl.pallas_call(
        body, name="mem_fwd", grid=(s // tq,),
        in_specs=[pl.BlockSpec((tq, 512), lambda i: (i, 0)), _full(mk.shape), _full(mv.shape)],
        out_specs=pl.BlockSpec((tq, 512), lambda i: (i, 0)),
        out_shape=jax.ShapeDtypeStruct((s, 512), BF16),
        compiler_params=_params(("parallel",)),
    )(qm, mk, mv)


def _merge_fwd(x, oa, of, om, proj, b_gate, wa, wf, wm, w_out, g_mlp):
    s = x.shape[0]
    tm = min(256, s)

    def body(x_ref, oa_ref, of_ref, om_ref, gl_ref, bg_ref, wa_ref, wf_ref, wm_ref, wo_ref, g_ref, x1_ref, hm_ref, mg_ref):
        merged = None
        for b, (o_ref, w_ref) in enumerate(((oa_ref, wa_ref), (of_ref, wf_ref), (om_ref, wm_ref))):
            cs = slice(b * D_MODEL, (b + 1) * D_MODEL)
            y = _dot(o_ref[...], w_ref[...])
            t = _sigmoid(gl_ref[:, cs].astype(F32) + bg_ref[:, cs]) * y
            merged = t if merged is None else merged + t
        mb = merged.astype(BF16)
        mg_ref[...] = mb
        x1 = x_ref[...] + _dot(mb, wo_ref[...])
        x1_ref[...] = x1
        r = lax.rsqrt(jnp.mean(x1 * x1, axis=-1, keepdims=True) + EPS)
        hm_ref[...] = (x1 * r * g_ref[...]).astype(BF16)

    row = lambda w: pl.BlockSpec((tm, w), lambda i: (i, 0))
    return pl.pallas_call(
        body, name="merge_fwd", grid=(s // tm,),
        in_specs=[row(D_MODEL), row(512), row(512), row(512), row(HALF_W), _full((1, HALF_W)),
                  _full(wa.shape), _full(wf.shape), _full(wm.shape), _full(w_out.shape), _full((1, D_MODEL))],
        out_specs=[row(D_MODEL), row(D_MODEL), row(D_MODEL)],
        out_shape=[jax.ShapeDtypeStruct((s, D_MODEL), F32), jax.ShapeDtypeStruct((s, D_MODEL), BF16),
                   jax.ShapeDtypeStruct((s, D_MODEL), BF16)],
        compiler_params=_params(("parallel",)),
    )(x, oa, of, om, proj, b_gate, wa, wf, wm, w_out, g_mlp)


def _mlp_up(hm, w_up):
    s = hm.shape[0]
    tm, tn = min(1024, s), w_up.shape[2]

    def body(h_ref, w_ref, u_ref):
        r = jnp.maximum(_dot(h_ref[...], w_ref[0]), 0.0)
        u_ref[...] = (r * r).astype(BF16)

    return pl.pallas_call(
        body, name="mlp_up", grid=(s // tm, D_FF // tn),
        in_specs=[pl.BlockSpec((tm, D_MODEL), lambda i, j: (i, 0)), pl.BlockSpec((1, D_MODEL, tn), lambda i, j: (j, 0, 0))],
        out_specs=pl.BlockSpec((tm, tn), lambda i, j: (i, j)),
        out_shape=jax.ShapeDtypeStruct((s, D_FF), BF16),
        compiler_params=_params(("parallel", "parallel")),
    )(hm, w_up)


def _mlp_down_loss(u, w_down, x1, target):
    s = u.shape[0]
    tm = min(256, s)

    def body(u_ref, w_ref, x1_ref, t_ref, dy_ref, dyb_ref, loss_ref):
        i = pl.program_id(0)

        @pl.when(i == 0)
        def _():
            loss_ref[...] = jnp.zeros_like(loss_ref)

        y = x1_ref[...] + _dot(u_ref[...], w_ref[...])
        err = y - t_ref[...]
        dy = err * (1.0 / D_MODEL)
        dy_ref[...] = dy
        dyb_ref[...] = dy.astype(BF16)
        part = jnp.sum(jnp.sum(err * err, axis=-1, keepdims=True) * (1.0 / D_MODEL), axis=0, keepdims=True)
        loss_ref[...] += 0.5 * part

    row = pl.BlockSpec((tm, D_MODEL), lambda i: (i, 0))
    return pl.pallas_call(
        body, name="mlp_down_loss", grid=(s // tm,),
        in_specs=[pl.BlockSpec((tm, D_FF), lambda i: (i, 0)), _full(w_down.shape), row, row],
        out_specs=[row, row, _full((1, 1))],
        out_shape=[jax.ShapeDtypeStruct((s, D_MODEL), F32), jax.ShapeDtypeStruct((s, D_MODEL), BF16),
                   jax.ShapeDtypeStruct((1, 1), F32)],
        compiler_params=_params(("arbitrary",)),
    )(u, w_down, x1, target)


def _mlp_bwd_act(dy, w_down, u):
    s = dy.shape[0]
    tm, tn = min(1024, s), 1024

    def body(dy_ref, w_ref, u_ref, da_ref):
        du = _dot(dy_ref[...], w_ref[...], NT)
        da_ref[...] = (du * (2.0 * jnp.sqrt(u_ref[...].astype(F32)))).astype(BF16)

    return pl.pallas_call(
        body, name="mlp_bwd_act", grid=(D_FF // tn, s // tm),
        in_specs=[pl.BlockSpec((tm, D_MODEL), lambda j, i: (i, 0)), pl.BlockSpec((tn, D_MODEL), lambda j, i: (j, 0)),
                  pl.BlockSpec((tm, tn), lambda j, i: (i, j))],
        out_specs=pl.BlockSpec((tm, tn), lambda j, i: (i, j)),
        out_shape=jax.ShapeDtypeStruct((s, D_FF), BF16),
        compiler_params=_params(("parallel", "parallel")),
    )(dy, w_down, u)


def _rms_bwd(xv, g, dh, skip):
    r = lax.rsqrt(jnp.mean(xv * xv, axis=-1, keepdims=True) + EPS)
    n = xv * r
    dn = dh * g
    dx = skip + r * (dn - n * jnp.mean(dn * n, axis=-1, keepdims=True))
    return dx, jnp.sum(dh * n, axis=0, keepdims=True)


def _mlp_bwd_x(da, w_up, x1, dy, g_mlp):
    s = da.shape[0]
    tm = min(256, s)

    def body(da_ref, w_ref, x1_ref, dy_ref, g_ref, dx1_ref, dg_ref):
        i = pl.program_id(0)

        @pl.when(i == 0)
        def _():
            dg_ref[...] = jnp.zeros_like(dg_ref)

        tn = w_ref.shape[2]
        dhm = _dot(da_ref[:, 0:tn], w_ref[0], NT)
        for j in range(1, N_DEV):
            dhm = dhm + _dot(da_ref[:, j * tn:(j + 1) * tn], w_ref[j], NT)
        dx, dg = _rms_bwd(x1_ref[...], g_ref[...], dhm, dy_ref[...])
        dx1_ref[...] = dx
        dg_ref[...] += dg

    row = pl.BlockSpec((tm, D_MODEL), lambda i: (i, 0))
    return pl.pallas_call(
        body, name="mlp_bwd_x", grid=(s // tm,),
        in_specs=[pl.BlockSpec((tm, D_FF), lambda i: (i, 0)), _full(w_up.shape), row, row, _full((1, D_MODEL))],
        out_specs=[row, _full((1, D_MODEL))],
        out_shape=[jax.ShapeDtypeStruct((s, D_MODEL), F32), jax.ShapeDtypeStruct((1, D_MODEL), F32)],
        compiler_params=_params(("arbitrary",)),
    )(da, w_up, x1, dy, g_mlp)


def _merge_bwd(dx1, oa, of, om, proj, b_gate, wa, wf, wm, w_out):
    s = dx1.shape[0]
    tm = min(256, s)

    def body(dx1_ref, oa_ref, of_ref, om_ref, gl_ref, bg_ref, wa_ref, wf_ref, wm_ref, wo_ref,
             dp_ref, doa_ref, dof_ref, dom_ref, dya_ref, dyf_ref, dym_ref, dbg_ref):
        i = pl.program_id(0)

        @pl.when(i == 0)
        def _():
            dbg_ref[...] = jnp.zeros_like(dbg_ref)

        dmerged = _dot(dx1_ref[...].astype(BF16), wo_ref[...], NT)
        branches = ((oa_ref, wa_ref, doa_ref, dya_ref), (of_ref, wf_ref, dof_ref, dyf_ref), (om_ref, wm_ref, dom_ref, dym_ref))
        for b, (o_ref, w_ref, do_ref, dyb_ref) in enumerate(branches):
            cs = slice(b * D_MODEL, (b + 1) * D_MODEL)
            y = _dot(o_ref[...], w_ref[...])
            g = _sigmoid(gl_ref[:, cs].astype(F32) + bg_ref[:, cs])
            dz = (dmerged * y) * g * (1.0 - g)
            dp_ref[:, cs] = dz.astype(BF16)
            dbg_ref[:, cs] += jnp.sum(dz, axis=0, keepdims=True)
            dyb = (dmerged * g).astype(BF16)
            dyb_ref[...] = dyb
            do = _dot(dyb, w_ref[...], NT)
            do_ref[...] = (do.T if b == 1 else do).astype(BF16)

    row = lambda w: pl.BlockSpec((tm, w), lambda i: (i, 0))
    sd = lambda w: jax.ShapeDtypeStruct((s, w), BF16)
    return pl.pallas_call(
        body, name="merge_bwd", grid=(s // tm,),
        in_specs=[row(D_MODEL), row(512), row(512), row(512), row(HALF_W), _full((1, HALF_W)),
                  _full(wa.shape), _full(wf.shape), _full(wm.shape), _full(w_out.shape)],
        out_specs=[row(HALF_W), row(512), pl.BlockSpec((512, tm), lambda i: (0, i)), row(512),
                   row(D_MODEL), row(D_MODEL), row(D_MODEL), _full((1, HALF_W))],
        out_shape=[sd(PROJ_W), sd(512), jax.ShapeDtypeStruct((512, s), BF16), sd(512), sd(D_MODEL), sd(D_MODEL), sd(D_MODEL),
                   jax.ShapeDtypeStruct((1, HALF_W), F32)],
        compiler_params=_params(("arbitrary",)),
    )(dx1, oa, of, om, proj, b_gate, wa, wf, wm, w_out)


def _swa_valid_t(n):
    key = lax.broadcasted_iota(jnp.int32, (2 * SWA_BLOCK, SWA_BLOCK), 0)
    qry = lax.broadcasted_iota(jnp.int32, (2 * SWA_BLOCK, SWA_BLOCK), 1)
    dist = qry + SWA_BLOCK - key
    return (dist >= 0) & (dist < SWA_BLOCK) & ((key >= SWA_BLOCK) | (n > 0))


def _swa_bwd(qa, kp, vp, bias_t, sink, doa):
    s = qa.shape[0]
    nb = s // SWA_BLOCK

    def body(sink_ref, q_ref, kp_ref, vp_ref, bias_ref, do_ref, dq_ref, dkp_ref, dvp_ref, dbias_ref, dsink_ref, sk_acc):
        n = pl.program_id(0)

        @pl.when(n == 0)
        def _():
            dkp_ref[...] = jnp.zeros_like(dkp_ref)
            dvp_ref[...] = jnp.zeros_like(dvp_ref)
            dbias_ref[...] = jnp.zeros_like(dbias_ref)
            sk_acc[...] = jnp.zeros_like(sk_acc)

        start = pl.multiple_of(n * SWA_BLOCK, SWA_BLOCK)
        win = pl.ds(start, 2 * SWA_BLOCK)
        k2 = kp_ref[win, :]
        v2 = vp_ref[win, :]
        valid = _swa_valid_t(n)
        heads = range(SWA_HEADS)
        hs = lambda h: slice(h * HEAD, (h + 1) * HEAD)
        scale = jnp.asarray(0.125, BF16)
        q = [q_ref[:, hs(h)] for h in heads]
        do = [do_ref[:, hs(h)] for h in heads]
        kk = [k2[:, hs(kv)] for kv in range(2)]
        vv = [v2[:, hs(kv)] for kv in range(2)]
        kt = [(kk[kv].astype(F32) * 0.125).T.astype(BF16) for kv in range(2)]
        st = [jnp.where(valid, _dot(kk[h // 4], q[h], NT) * 0.125 + bias_ref[h], NEG) for h in heads]
        dpt = [_dot(vv[h // 4], do[h], NT) for h in heads]
        pt, dst = [], []
        for h in heads:
            sk = sink_ref[h]
            mx = jnp.maximum(jnp.max(st[h], axis=0, keepdims=True), sk)
            p = jnp.exp(st[h] - mx)
            esk = jnp.exp(sk - mx)
            den = jnp.sum(p, axis=0, keepdims=True) + esk
            p = p / den
            delta = jnp.sum(p * dpt[h], axis=0, keepdims=True)
            d = p * (dpt[h] - delta)
            sk_acc[h:h + 1, :] += -(esk / den) * delta
            dbias_ref[h] += d
            pt.append(p.astype(BF16))
            dst.append(d.astype(BF16))
        dq_t = [_dot(kt[h // 4], dst[h]) for h in heads]
        dq_ref[...] = jnp.concatenate(dq_t, axis=0).T
        for kv in range(2):
            group = range(4 * kv, 4 * kv + 4)
            dk = [_dot(dst[h], q[h] * scale) for h in group]
            dv = [_dot(pt[h], do[h]) for h in group]
            dkp_ref[win, hs(kv)] += (dk[0] + dk[1]) + (dk[2] + dk[3])
            dvp_ref[win, hs(kv)] += (dv[0] + dv[1]) + (dv[2] + dv[3])

        @pl.when(n == nb - 1)
        def _():
            dsink_ref[...] = jnp.broadcast_to(jnp.sum(sk_acc[...], axis=1, keepdims=True), dsink_ref.shape)

    return pl.pallas_call(
        body, name="swa_bwd", grid=(nb,),
        in_specs=[pl.BlockSpec(memory_space=pltpu.SMEM),
                  pl.BlockSpec((SWA_BLOCK, 512), lambda n: (n, 0)),
                  _full(kp.shape), _full(vp.shape), _full(bias_t.shape),
                  pl.BlockSpec((SWA_BLOCK, 512), lambda n: (n, 0))],
        out_specs=[pl.BlockSpec((SWA_BLOCK, 512), lambda n: (n, 0)), _full(kp.shape), _full(vp.shape),
                   _full(bias_t.shape), _full((SWA_HEADS, 128))],
        out_shape=[jax.ShapeDtypeStruct((s, 512), F32), jax.ShapeDtypeStruct(kp.shape, F32),
                   jax.ShapeDtypeStruct(vp.shape, F32), jax.ShapeDtypeStruct(bias_t.shape, F32),
                   jax.ShapeDtypeStruct((SWA_HEADS, 128), F32)],
        scratch_shapes=[pltpu.VMEM((SWA_HEADS, 128), F32)],
        compiler_params=_params(("arbitrary",)),
    )(sink, qa, kp, vp, bias_t, doa)


def _fox_bwd(qt, k, v, dot, ot, cc4, lse4):
    s = k.shape[0]
    t = min(FOX_BWD_TK, s)
    tq = min(FOX_BWD_TQ, s)
    nq = s // t
    nqt = s // tq

    def body(qt_ref, k_ref, v_ref, dot_ref, ot_ref, cc_ref, lse_ref,
             dqt_ref, dk_ref, dv_ref, dck_ref, dcq_ref, delta_ref, dk0, dk1, dv0, dv1, ds0, ds1):
        j = pl.program_id(1)

        @pl.when(j == 0)
        def _():
            dqt_ref[...] = jnp.zeros_like(dqt_ref)
            dcq_ref[...] = jnp.zeros_like(dcq_ref)
            r8 = lax.broadcasted_iota(jnp.int32, (8, tq), 0)

            def dl(i, c):
                cols = pl.ds(pl.multiple_of(i * tq, tq), tq)
                pr = dot_ref[:, cols].astype(F32) * ot_ref[:, cols].astype(F32)
                d0 = jnp.sum(jnp.where(_head_rows(0), pr, 0.0), axis=0, keepdims=True)
                d1 = jnp.sum(jnp.where(_head_rows(1), pr, 0.0), axis=0, keepdims=True)
                delta_ref[:, cols] = jnp.where(r8 == 0, d0, jnp.where(r8 == 1, d1, 0.0))
                return c

            lax.fori_loop(0, nqt, dl, 0)

        kj = k_ref[...]
        vj = v_ref[...]
        ks = pl.ds(pl.multiple_of(j * t, t), t)
        kt = (kj.astype(F32) * 0.125).T.astype(BF16)
        ke = [jnp.where(_head_mask(e), kj, jnp.zeros_like(kj)) for e in range(2)]
        ve = [jnp.where(_head_mask(e), vj, jnp.zeros_like(vj)) for e in range(2)]
        kte = [jnp.where(_head_rows(e), kt, jnp.zeros_like(kt)) for e in range(2)]
        ck = [cc_ref[0, ks, e:e + 1] for e in range(2)]
        accs = ((dk0, dv0, ds0), (dk1, dv1, ds1))
        for refs in accs:
            for r in refs:
                r[...] = jnp.zeros_like(r)
        i_first = (j * t) // tq
        krow = lax.broadcasted_iota(jnp.int32, (t, tq), 0) + j * t
        qcol = lax.broadcasted_iota(jnp.int32, (t, tq), 1) + i_first * tq

        def step(i, c, masked):
            cols = pl.ds(pl.multiple_of(i * tq, tq), tq)
            qti = qt_ref[:, cols]
            doti = dot_ref[:, cols]
            for e in range(2):
                dkt_acc, dvt_acc, ds_acc = accs[e]
                st = _dot(ke[e], qti) - ck[e]
                if masked:
                    st = jnp.where(krow <= qcol, st, NEG)
                pt = jnp.exp(st - lse_ref[0, e:e + 1, cols])
                dpt = _dot(ve[e], doti)
                dst = pt * (dpt - delta_ref[e:e + 1, cols])
                dsb = dst.astype(BF16)
                dvt_acc[...] += _dot(doti, pt.astype(BF16), NT)
                dkt_acc[...] += _dot(qti, dsb, NT)
                dqt_ref[:, cols] += _dot(kte[e], dsb)
                ds_acc[...] += dst
                dcq_ref[0, e:e + 1, cols] += jnp.sum(dst, axis=0, keepdims=True)
            return c

        step(i_first, 0, True)
        lax.fori_loop(i_first + 1, nqt, functools.partial(step, masked=False), 0)
        r0 = _head_rows(0)
        dk_ref[...] = jnp.where(r0, dk0[...], dk1[...]).T
        dv_ref[...] = jnp.where(r0, dv0[...], dv1[...]).T
        lane = lax.broadcasted_iota(jnp.int32, (t, 128), 1)
        c0 = jnp.sum(ds0[...], axis=-1, keepdims=True)
        c1 = jnp.sum(ds1[...], axis=-1, keepdims=True)
        dck_ref[0] = jnp.where(lane == 0, c0, jnp.where(lane == 1, c1, 0.0))

    res_t = lambda: pl.BlockSpec((128, s), lambda hp, j: (hp, 0))
    blk = lambda: pl.BlockSpec((t, 128), lambda hp, j: (j, hp))
    return pl.pallas_call(
        body, name="fox_bwd", grid=(4, nq),
        in_specs=[res_t(), blk(), blk(), res_t(), res_t(), pl.BlockSpec((1, s, 128), lambda hp, j: (hp, 0, 0)),
                  pl.BlockSpec((1, 8, s), lambda hp, j: (hp, 0, 0))],
        out_specs=[res_t(), blk(), blk(),
                   pl.BlockSpec((1, t, 128), lambda hp, j: (hp, j, 0)),
                   pl.BlockSpec((1, 8, s), lambda hp, j: (hp, 0, 0))],
        out_shape=[jax.ShapeDtypeStruct((512, s), F32), jax.ShapeDtypeStruct((s, 512), F32),
                   jax.ShapeDtypeStruct((s, 512), F32), jax.ShapeDtypeStruct((4, s, 128), F32),
                   jax.ShapeDtypeStruct((4, 8, s), F32)],
        scratch_shapes=[pltpu.VMEM((8, s), F32)] + [pltpu.VMEM((128, t), F32)] * 4 + [pltpu.VMEM((t, tq), F32)] * 2,
        compiler_params=_params(("arbitrary", "arbitrary")),
    )(qt, k, v, dot, ot, cc4, lse4)


def _mem_bwd(qm, mk, mv, dom):
    s = qm.shape[0]
    tq = min(512, s)

    def body(q_ref, mk_ref, mv_ref, do_ref, dq_ref, dmk_ref, dmv_ref):
        i = pl.program_id(0)

        @pl.when(i == 0)
        def _():
            dmk_ref[...] = jnp.zeros_like(dmk_ref)
            dmv_ref[...] = jnp.zeros_like(dmv_ref)

        for h in range(MEM_HEADS):
            hs = slice(h * 128, (h + 1) * 128)
            qh = q_ref[:, hs]
            doh = do_ref[:, hs]
            sc = _dot(qh, mk_ref[:, hs], NT) * MEM_SCALE
            mx = jnp.max(sc, axis=-1, keepdims=True)
            p = jnp.exp(sc - mx)
            p = p / jnp.sum(p, axis=-1, keepdims=True)
            dp = _dot(doh, mv_ref[:, hs], NT)
            ds = p * (dp - jnp.sum(p * dp, axis=-1, keepdims=True))
            dsb = (ds * MEM_SCALE).astype(BF16)
            dq_ref[:, hs] = _dot(dsb, mk_ref[:, hs])
            dmk_ref[:, hs] += _dot(dsb, qh, TN)
            dmv_ref[:, hs] += _dot(p.astype(BF16), doh, TN)

    return pl.pallas_call(
        body, name="mem_bwd", grid=(s // tq,),
        in_specs=[pl.BlockSpec((tq, 512), lambda i: (i, 0)), _full(mk.shape), _full(mv.shape),
                  pl.BlockSpec((tq, 512), lambda i: (i, 0))],
        out_specs=[pl.BlockSpec((tq, 512), lambda i: (i, 0)), _full(mk.shape), _full(mv.shape)],
        out_shape=[jax.ShapeDtypeStruct((s, 512), F32), jax.ShapeDtypeStruct(mk.shape, F32),
                   jax.ShapeDtypeStruct(mv.shape, F32)],
        compiler_params=_params(("arbitrary",)),
    )(qm, mk, mv, dom)


def _memkv_bwd(dmk, dmv, kv_raw, kn_mem, mem, g_mem, mem_n, w_kv):
    def body(dmk_ref, dmv_ref, kv_ref, kn_ref, mem_ref, g_ref, mn_ref, w_ref, dw_ref, dkn_ref, dg_ref, dkv_ref):
        dkn = jnp.zeros((1, 128), F32)
        for h in range(MEM_HEADS):
            hs = slice(h * 128, (h + 1) * 128)
            v = kv_ref[:, hs]
            r = lax.rsqrt(jnp.mean(v * v, axis=-1, keepdims=True) + EPS)
            n = v * r
            dn = dmk_ref[:, hs]
            dkn = dkn + jnp.sum(dn * n, axis=0, keepdims=True)
            dng = dn * kn_ref[...]
            dkv_ref[:, hs] = (r * (dng - n * jnp.mean(dng * n, axis=-1, keepdims=True))).astype(BF16)
        dkv_ref[:, 512:1024] = dmv_ref[...].astype(BF16)
        dkn_ref[...] = dkn
        dkv = dkv_ref[...]
        dw_ref[...] = _dot(mn_ref[...], dkv, TN).astype(BF16)
        dmn = _dot(dkv, w_ref[...], NT)
        xv = mem_ref[...]
        r = lax.rsqrt(jnp.mean(xv * xv, axis=-1, keepdims=True) + EPS)
        dg_ref[...] = jnp.sum(dmn * (xv * r), axis=0, keepdims=True)

    m = mem.shape[0]
    return pl.pallas_call(
        body, name="memkv_bwd",
        out_shape=[jax.ShapeDtypeStruct((D_MODEL, 1024), BF16), jax.ShapeDtypeStruct((1, 128), F32),
                   jax.ShapeDtypeStruct((1, D_MODEL), F32)],
        scratch_shapes=[pltpu.VMEM((m, 1024), BF16)],
        compiler_params=pltpu.CompilerParams(vmem_limit_bytes=VMEM_LIMIT),
    )(dmk, dmv, kv_raw, kn_mem, mem, g_mem, mem_n, w_kv)


def _fox_gate_bwd(dc, proj, b_forget128):
    s = dc.shape[0]
    tm = min(512, s)
    nt = s // tm

    def body(dc_ref, p_ref, b_ref, dfl_ref, db_ref, carry_ref):
        i = pl.program_id(0)

        @pl.when(i == 0)
        def _():
            carry_ref[...] = jnp.zeros_like(carry_ref)
            db_ref[...] = jnp.zeros_like(db_ref)

        dcv = dc_ref[...]
        dlogf = jnp.dot(_tri(tm, False), dcv, precision=lax.Precision.HIGHEST, preferred_element_type=F32) + carry_ref[...]
        carry_ref[...] += jnp.sum(dcv, axis=0, keepdims=True)
        z = p_ref[...] + b_ref[...]
        dfl = dlogf * (1.0 / (1.0 + jnp.exp(z)))
        dfl_ref[...] = dfl.astype(BF16)
        db_ref[...] += jnp.sum(dfl, axis=0, keepdims=True)

    return pl.pallas_call(
        body, name="fox_gate_bwd", grid=(nt,),
        in_specs=[pl.BlockSpec((tm, 128), lambda i: (nt - 1 - i, 0)),
                  pl.BlockSpec((tm, 128), lambda i: (nt - 1 - i, 0)), _full((1, 128))],
        out_specs=[pl.BlockSpec((tm, 128), lambda i: (nt - 1 - i, 0)), _full((1, 128))],
        out_shape=[jax.ShapeDtypeStruct((s, 128), BF16), jax.ShapeDtypeStruct((1, 128), F32)],
        scratch_shapes=[pltpu.VMEM((1, 128), F32)],
        compiler_params=_params(("arbitrary",)),
    )(dc, proj, b_forget128)


def _proj_pre_bwd(dproj, proj, dqf, dkf, dvf, dqm, dqa, dka, dva, dfl, gq_fox, gk_fox, gq_mem, gq_swa, gk_swa):
    s = proj.shape[0]
    tm = min(256, s)

    def body(dp_in, p_ref, dqf_ref, dkf_ref, dvf_ref, dqm_ref, dqa_ref, dka_ref, dva_ref, dfl_ref,
             gqf, gkf, gqm, gqa, gka, dp_ref, dgn_ref):
        i = pl.program_id(0)

        @pl.when(i == 0)
        def _():
            dgn_ref[...] = jnp.zeros_like(dgn_ref)

        def norm_bwd(off, width, hd, g_ref, dn_ref, slot):
            acc = jnp.zeros((1, 128), F32)
            for b in range(width // 128):
                v = p_ref[:, off + b * 128: off + (b + 1) * 128].astype(F32)
                r = lax.rsqrt(_group_mean(v * v, hd) + EPS)
                n = v * r
                dn = dn_ref[b * 128:(b + 1) * 128, :].T if slot == 0 else dn_ref[:, b * 128:(b + 1) * 128]
                acc = acc + jnp.sum(dn * n, axis=0, keepdims=True)
                dng = dn * g_ref[...]
                dp_ref[:, off + b * 128: off + (b + 1) * 128] = (r * (dng - n * _group_mean(dng * n, hd))).astype(BF16)
            dgn_ref[slot:slot + 1, :] += acc

        norm_bwd(H_QF, 512, HEAD, gqf, dqf_ref, 0)
        norm_bwd(H_KF, 512, HEAD, gkf, dkf_ref, 1)
        dp_ref[:, H_VF:H_VF + 512] = dvf_ref[...].astype(BF16)
        norm_bwd(H_QM, 512, MEM_HEAD, gqm, dqm_ref, 2)
        norm_bwd(H_QA, 512, HEAD, gqa, dqa_ref, 3)
        norm_bwd(H_KA, 128, HEAD, gka, dka_ref, 4)
        dp_ref[:, H_VA:H_VA + 128] = dva_ref[...].astype(BF16)
        dp_ref[:, H_FL:H_FL + 128] = dfl_ref[...]
        dp_ref[:, H_FL + 128:HALF_W] = jnp.zeros((tm, HALF_W - H_FL - 128), BF16)

    row = lambda w: pl.BlockSpec((tm, w), lambda i: (i, 0))
    g_spec = _full((1, 128))
    return pl.pallas_call(
        body, name="proj_pre_bwd", grid=(s // tm,),
        in_specs=[pl.BlockSpec(memory_space=pl.ANY), pl.BlockSpec((tm, HALF_W), lambda i: (i, 1)),
                  pl.BlockSpec((512, tm), lambda i: (0, i)), row(512), row(512), row(512), row(512),
                  row(128), row(128), row(128), g_spec, g_spec, g_spec, g_spec, g_spec],
        out_specs=[pl.BlockSpec((tm, HALF_W), lambda i: (i, 1)), _full((8, 128))],
        out_shape=[jax.ShapeDtypeStruct((s, PROJ_W), BF16), jax.ShapeDtypeStruct((8, 128), F32)],
        input_output_aliases={0: 0},
        compiler_params=_params(("arbitrary",)),
    )(dproj, proj, dqf, dkf, dvf, dqm, dqa, dka, dva, dfl, gq_fox, gk_fox, gq_mem, gq_swa, gk_swa)


def _in_bwd_x(dproj, w_in_p, x, g_mix, dx1):
    s = x.shape[0]
    tm = min(256, s)

    def body(dp_ref, w_ref, x_ref, g_ref, dx1_ref, gx_ref, dg_ref):
        i = pl.program_id(0)

        @pl.when(i == 0)
        def _():
            dg_ref[...] = jnp.zeros_like(dg_ref)

        dx, dg = _rms_bwd(x_ref[...], g_ref[...], _dot(dp_ref[...], w_ref[...], NT), dx1_ref[...])
        gx_ref[...] = dx
        dg_ref[...] += dg

    row = pl.BlockSpec((tm, D_MODEL), lambda i: (i, 0))
    return pl.pallas_call(
        body, name="in_bwd_x", grid=(s // tm,),
        in_specs=[pl.BlockSpec((tm, PROJ_W), lambda i: (i, 0)), _full(w_in_p.shape), row, _full((1, D_MODEL)), row],
        out_specs=[row, _full((1, D_MODEL))],
        out_shape=[jax.ShapeDtypeStruct((s, D_MODEL), F32), jax.ShapeDtypeStruct((1, D_MODEL), F32)],
        compiler_params=_params(("arbitrary",)),
    )(dproj, w_in_p, x, g_mix, dx1)


def _rel_bias_bwd(dbias, bucket):
    def body(db_ref, bk_ref, o_ref):
        bk = bk_ref[...]
        lane = lax.broadcasted_iota(jnp.int32, (1, 128), 1)
        for b in range(REL_BUCKETS):
            sel = bk == b
            acc = jnp.zeros((1, 128), F32)
            for h in range(SWA_HEADS):
                tot = jnp.sum(jnp.sum(jnp.where(sel, db_ref[h], 0.0), axis=-1, keepdims=True), axis=0, keepdims=True)
                acc = jnp.where(lane == h, tot, acc)
            o_ref[:, b * 128:(b + 1) * 128] = acc

    return pl.pallas_call(
        body, name="rel_bias_bwd",
        out_shape=jax.ShapeDtypeStruct((1, REL_BUCKETS * 128), F32),
        compiler_params=pltpu.CompilerParams(vmem_limit_bytes=VMEM_LIMIT),
    )(dbias, bucket)


def _my_place():
    return lax.axis_index("x"), lax.axis_index("y"), lax.axis_index("c")


def _peer(place, k):
    x, y, c = place
    return (1 - x if k & 4 else x, 1 - y if k & 2 else y, 1 - c if k & 1 else c)


def _index(place):
    x, y, c = place
    return 4 * x + 2 * y + c


HBM_SPEC = pl.BlockSpec(memory_space=pltpu.HBM)
SEM_SPEC = pl.BlockSpec(memory_space=pltpu.SEMAPHORE)
DATAFLOW = pltpu.SideEffectType.DATAFLOW_SIDE_EFFECTING


ALL_PEERS = tuple(range(1, N_DEV))
SAME_CORE = (2, 4, 6)
OWN = N_DEV - 1


def _split_copy(src_ref, land_ref, send_sems, recv_sems, me, k, gather):
    peer = _peer(me, k)
    if gather:
        src, dst = src_ref, land_ref.at[_index(me)]
    else:
        src, dst = src_ref.at[_index(peer)], land_ref.at[k - 1]
    return pltpu.make_async_remote_copy(src_ref=src, dst_ref=dst, send_sem=send_sems.at[k - 1], recv_sem=recv_sems.at[k - 1],
                                        device_id=peer, device_id_type=MESH)


def _own_copy(src_ref, land_ref, recv_sems, me, gather):
    if gather:
        src, dst = src_ref, land_ref.at[_index(me)]
    else:
        src, dst = src_ref.at[_index(me)], land_ref.at[OWN]
    return pltpu.make_async_copy(src, dst, recv_sems.at[OWN])


def _split_start(srcs, gather, name, peers=ALL_PEERS, after=None):
    n = len(srcs)
    extra = [] if after is None else [after]

    def body(*refs):
        refs = refs[:2 * n] + refs[2 * n + len(extra):]
        src_refs, land_refs = refs[:n], refs[n:2 * n]
        send_sems, recv_sems, token = refs[2 * n:3 * n], refs[3 * n:4 * n], refs[-1]
        me = _my_place()
        for w in range(n):
            for k in peers:
                _split_copy(src_refs[w], land_refs[w], send_sems[w], recv_sems[w], me, k, gather).start()
            _own_copy(src_refs[w], land_refs[w], recv_sems[w], me, gather).start()
        token[...] = jnp.zeros_like(token)

    lands = [lax.empty((N_DEV,) + (a.shape if gather else a.shape[1:]), a.dtype) for a in srcs]
    sems = [pltpu.SemaphoreType.DMA((N_DEV,))] * (2 * n)
    hbm = [pltpu.HBM(a.shape, a.dtype) for a in list(srcs) + lands]
    outs = pl.pallas_call(
        body, name=name,
        out_shape=(*sems, *hbm, jax.ShapeDtypeStruct((8, 128), F32)),
        in_specs=(HBM_SPEC,) * (2 * n) + (pl.BlockSpec(memory_space=pl.ANY),) * len(extra),
        out_specs=(SEM_SPEC,) * (2 * n) + (HBM_SPEC,) * (2 * n) + (pl.BlockSpec(memory_space=pltpu.VMEM),),
        input_output_aliases={i: 2 * n + i for i in range(2 * n)},
        compiler_params=pltpu.CompilerParams(has_side_effects=DATAFLOW),
    )(*[pltpu.with_memory_space_constraint(a, pltpu.HBM) for a in list(srcs) + lands], *extra)
    return list(outs[:n]), list(outs[n:2 * n]), list(outs[2 * n:3 * n]), list(outs[3 * n:4 * n]), outs[-1]


def _split_wait(started, w, after, gather, name):
    send_sems, recv_sems, srcs, lands, _ = started

    def body(src_ref, land_ref, send_sems, recv_sems, after_ref, src_out, land_out):
        me = _my_place()
        for k in ALL_PEERS:
            cp = _split_copy(src_ref, land_ref, send_sems, recv_sems, me, k, gather)
            cp.wait_send()
            cp.wait_recv()
        _own_copy(src_ref, land_ref, recv_sems, me, gather).wait()

    return pl.pallas_call(
        body, name=name,
        out_shape=(pltpu.HBM(srcs[w].shape, srcs[w].dtype), pltpu.HBM(lands[w].shape, lands[w].dtype)),
        in_specs=(HBM_SPEC, HBM_SPEC, SEM_SPEC, SEM_SPEC, pl.BlockSpec(memory_space=pl.ANY)),
        out_specs=(HBM_SPEC, HBM_SPEC), input_output_aliases={0: 0, 1: 1},
        compiler_params=pltpu.CompilerParams(has_side_effects=DATAFLOW),
    )(srcs[w], lands[w], send_sems[w], recv_sems[w], after)[1]


def _forward_copy(land_ref, send_sems, recv_sems, me, j, incoming):
    sibling = _peer(me, 1)
    rows = land_ref.at[_index(_peer(sibling if incoming else me, SAME_CORE[j]))]
    return pltpu.make_async_remote_copy(src_ref=rows, dst_ref=rows, send_sem=send_sems.at[j], recv_sem=recv_sems.at[j],
                                        device_id=sibling, device_id_type=MESH)


def _forward_start(started, after, name):
    send_a, recv_a, srcs, lands, _ = started

    def body(src_ref, land_ref, send_a, recv_a, after_ref, send_b, recv_b, src_out, land_out):
        me = _my_place()
        for j, k in enumerate(SAME_CORE):
            _split_copy(src_ref, land_ref, send_a, recv_a, me, k, True).wait_recv()
            _forward_copy(land_ref, send_b, recv_b, me, j, False).start()

    sems = pltpu.SemaphoreType.DMA((len(SAME_CORE),))
    return pl.pallas_call(
        body, name=name,
        out_shape=(sems, sems, pltpu.HBM(srcs[0].shape, srcs[0].dtype), pltpu.HBM(lands[0].shape, lands[0].dtype)),
        in_specs=(HBM_SPEC, HBM_SPEC, SEM_SPEC, SEM_SPEC, pl.BlockSpec(memory_space=pl.ANY)),
        out_specs=(SEM_SPEC, SEM_SPEC, HBM_SPEC, HBM_SPEC), input_output_aliases={0: 2, 1: 3},
        compiler_params=pltpu.CompilerParams(has_side_effects=DATAFLOW),
    )(srcs[0], lands[0], send_a[0], recv_a[0], after)


def _forward_wait(started, forwarded, name):
    send_a, recv_a, _, _, _ = started
    send_b, recv_b, src, land = forwarded

    def body(src_ref, land_ref, send_a, recv_a, send_b, recv_b, src_out, land_out):
        me = _my_place()
        _own_copy(src_ref, land_ref, recv_a, me, True).wait()
        for k in (1,) + SAME_CORE:
            _split_copy(src_ref, land_ref, send_a, recv_a, me, k, True).wait_send()
        _split_copy(src_ref, land_ref, send_a, recv_a, me, 1, True).wait_recv()
        for j in range(len(SAME_CORE)):
            _forward_copy(land_ref, send_b, recv_b, me, j, False).wait_send()
            _forward_copy(land_ref, send_b, recv_b, me, j, True).wait_recv()

    return pl.pallas_call(
        body, name=name,
        out_shape=(pltpu.HBM(src.shape, src.dtype), pltpu.HBM(land.shape, land.dtype)),
        in_specs=(HBM_SPEC, HBM_SPEC, SEM_SPEC, SEM_SPEC, SEM_SPEC, SEM_SPEC),
        out_specs=(HBM_SPEC, HBM_SPEC), input_output_aliases={0: 0, 1: 1},
        compiler_params=pltpu.CompilerParams(has_side_effects=DATAFLOW),
    )(src, land, send_a[0], recv_a[0], send_b, recv_b)[1]


def _adam_math(w, g, m, v):
    m2 = ADAM_B1 * m + (1.0 - ADAM_B1) * g
    v2 = ADAM_B2 * v + (1.0 - ADAM_B2) * (g * g)
    m_hat = m2 / (1.0 - ADAM_B1 ** ADAM_STEP)
    v_hat = v2 / (1.0 - ADAM_B2 ** ADAM_STEP)
    delta = -ADAM_LR * (m_hat / (jnp.sqrt(v_hat) + ADAM_EPS) + ADAM_WD * w)
    return delta, m2, v2


def _adamw(land, w, m, v, name):
    a, b = w.shape
    bp = land.shape[2]
    ta = min(128, a)

    def body(p_ref, w_ref, m_ref, v_ref, g_ref, d_ref, m2_ref, v2_ref):
        g = p_ref[0, :, 0:b].astype(F32)
        for k in range(1, N_DEV):
            g = g + p_ref[k, :, 0:b].astype(F32)
        delta, m2, v2 = _adam_math(w_ref[...], g, m_ref[...], v_ref[...])
        g_ref[...] = g
        d_ref[...] = delta
        m2_ref[...] = m2
        v2_ref[...] = v2

    blk = pl.BlockSpec((ta, b), lambda i: (i, 0))
    sd = jax.ShapeDtypeStruct((a, b), F32)
    return pl.pallas_call(
        body, name=name, grid=(a // ta,),
        in_specs=[pl.BlockSpec((N_DEV, ta, bp), lambda i: (0, i, 0)), blk, blk, blk],
        out_specs=[blk, blk, blk, blk], out_shape=[sd, sd, sd, sd],
        compiler_params=_params(("parallel",)),
    )(land, w, m, v)


def _bucket_table():
    t_loc = jnp.arange(SWA_BLOCK)[:, None] + SWA_BLOCK
    s_loc = jnp.arange(2 * SWA_BLOCK)[None, :]
    dist = t_loc - s_loc
    max_exact = REL_BUCKETS // 2
    d = jnp.maximum(dist, 0)
    df = jnp.maximum(d, 1).astype(F32)
    large = max_exact + (jnp.log(df / max_exact) / math.log(REL_MAX_DIST / max_exact) * (REL_BUCKETS - max_exact)).astype(jnp.int32)
    large = jnp.minimum(large, REL_BUCKETS - 1)
    bucket = jnp.where(d < max_exact, d, large)
    band = (dist >= 0) & (dist < SWA_BLOCK)
    return bucket, band


def _tile2(g):
    return jnp.concatenate([g, g], axis=1) if g.shape[1] == HEAD else g


SHARD_W = 737
SHARD_WP = 768
IN_WIDTH = N_DEV * SHARD_W
SEGMENTS = ((GL0, 2824, 3072), (QF0, 768, 512), (KF0, 1280, 512), (VF0, 1792, 512), (QM0, 2312, 512),
            (QA0, 0, 512), (KA0, 512, 128), (VA0, 640, 128), (FL0, 2304, 8))


def _lane_plan(sources):
    plan = []
    for t in range(len(sources) // 128):
        groups = {}
        for lane in range(128):
            src = sources[128 * t + lane]
            if src is not None:
                slab, col = src
                groups.setdefault((slab, col // 128, (lane - col) % 128), []).append(lane)
        tile = []
        for key, lanes in groups.items():
            assert lanes == list(range(lanes[0], lanes[-1] + 1))
            tile.append((key, lanes[0], lanes[-1] + 1))
        plan.append(tile)
    return plan


def _assemble(tile_plan, load, rows):
    lane = lax.broadcasted_iota(jnp.int32, (1, 128), 1)
    out = jnp.zeros((rows, 128), F32)
    for (slab, st, roll), lo, hi in tile_plan:
        v = load(slab, st)
        if roll:
            v = pltpu.roll(v, roll, 1)
        out = v if (lo, hi) == (0, 128) else jnp.where((lane >= lo) & (lane < hi), v, out)
    return out


def _w_in_from_shards(land):
    ref_col = [None] * PROJ_W
    for p0, r0, n in SEGMENTS:
        for i in range(n):
            ref_col[p0 + i] = divmod(r0 + i, SHARD_W)
    plan = _lane_plan(ref_col)
    d_model = land.shape[1]
    tm = 256

    def body(land_ref, o_ref):
        load = lambda slab, st: land_ref[slab, :, st * 128:(st + 1) * 128].astype(F32)
        for t, tile_plan in enumerate(plan):
            o_ref[:, t * 128:(t + 1) * 128] = _assemble(tile_plan, load, tm).astype(BF16)

    return pl.pallas_call(
        body, name="w_in_from_shards", grid=(d_model // tm,),
        in_specs=[pl.BlockSpec((N_DEV, tm, SHARD_WP), lambda i: (0, i, 0))],
        out_specs=pl.BlockSpec((tm, PROJ_W), lambda i: (i, 0)),
        out_shape=jax.ShapeDtypeStruct((d_model, PROJ_W), BF16),
        compiler_params=_params(("parallel",)),
    )(land)


def _dw_in_to_parts(dwp):
    padded_col = [None] * IN_WIDTH
    for p0, r0, n in SEGMENTS:
        for i in range(n):
            padded_col[r0 + i] = p0 + i
    sources = []
    for d in range(N_DEV):
        sources += [(0, padded_col[SHARD_W * d + c]) if c < SHARD_W else None for c in range(SHARD_WP)]
    plan = _lane_plan(sources)
    d_model = dwp.shape[0]
    tm = 256
    tiles = SHARD_WP // 128

    def body(dw_ref, o_ref):
        load = lambda slab, st: dw_ref[:, st * 128:(st + 1) * 128].astype(F32)
        for t, tile_plan in enumerate(plan):
            d, c = divmod(t, tiles)
            o_ref[d, :, c * 128:(c + 1) * 128] = _assemble(tile_plan, load, tm).astype(BF16)

    return pl.pallas_call(
        body, name="dw_in_to_parts", grid=(d_model // tm,),
        in_specs=[pl.BlockSpec((tm, PROJ_W), lambda i: (i, 0))],
        out_specs=pl.BlockSpec((N_DEV, tm, SHARD_WP), lambda i: (0, i, 0)),
        out_shape=jax.ShapeDtypeStruct((N_DEV, d_model, SHARD_WP), BF16),
        compiler_params=_params(("parallel",)),
    )(dwp)


def _cast_shards(shards):
    names = list(shards)

    def body(*refs):
        for src, dst in zip(refs[:len(names)], refs[len(names):]):
            if dst.shape != src.shape:
                dst[...] = jnp.zeros(dst.shape, BF16)
                dst[:, 0:src.shape[1]] = src[...].astype(BF16)
            else:
                dst[...] = src[...].astype(BF16)

    out_shape = [jax.ShapeDtypeStruct((shards[n].shape[0], SHARD_WP if n == "w_in" else shards[n].shape[1]), BF16)
                 for n in names]
    outs = pl.pallas_call(body, name="cast_shards", out_shape=out_shape,
                          compiler_params=pltpu.CompilerParams(vmem_limit_bytes=VMEM_LIMIT))(*[shards[n] for n in names])
    return dict(zip(names, outs))


def _tie(x, *tokens):
    for t in tokens:
        if t is not None:
            x = x + t[0:1, 0:1]
    return x


def _local_step(x, mem, target, p, getw, emit, deps=()):
    s = x.shape[0]
    bucket, band = _bucket_table()
    bucket_m = jnp.where(band, bucket, -1).astype(jnp.int32)
    bias = _bias_table(p["rel_bias"], bucket_m)
    bucket_t = jnp.transpose(bucket_m)
    bias_t = _bias_table(p["rel_bias"], bucket_t)
    gqf, gkf, gqa, gka = _tile2(p["qn_fox"]), _tile2(p["kn_fox"]), _tile2(p["qn_swa"]), _tile2(p["kn_swa"])
    gqm = p["qn_mem"]
    bf128 = jnp.pad(p["b_forget"], ((0, 0), (0, 120)))
    sink = p["sink_swa"].reshape(8)

    h = _rms_fwd(x, p["g_mix"], "rms_mix", deps)
    w_in = getw("w_in", h)
    proj = _mm(h, w_in, "nn", BF16, 512, 1536, 1024, "proj")
    fl = _mm(h, w_in[:, FL0:FL0 + 128], "nn", F32, 512, 128, 1024, "proj_fl")
    qf, kf, vf, qm, qa, ka, va, qf_t, vf_t = _proj_post(proj, gqf, gkf, gqm, gqa, gka)
    cc4 = _fox_gate_fwd(fl, bf128)
    w_kv = getw("w_mem_kv", cc4)
    mem_n, kv_raw, mk, mv = _memkv_fwd(mem, p["g_mem"], w_kv, p["kn_mem"])
    kp = jnp.pad(ka, ((SWA_BLOCK, 0), (0, 0)))
    vp = jnp.pad(va, ((SWA_BLOCK, 0), (0, 0)))
    oa = _swa_fwd(qa, kp, vp, bias, sink)
    of, lse4, of_t = _fox_fwd(qf, kf, vf_t, cc4)
    om = _mem_fwd(qm, mk, mv)
    wa, wf, wm, w_out = getw("w_o_swa", oa), getw("w_o_fox", oa), getw("w_o_mem", oa), getw("w_out", oa)
    x1, hm, merged = _merge_fwd(x, oa, of, om, proj, p["b_gate"], wa, wf, wm, w_out, p["g_mlp"])
    w_up = getw("w_mlp_up", of)
    u = _mlp_up(hm, w_up)
    w_down = getw("w_mlp_down", hm)
    dy, dy_b, loss = _mlp_down_loss(u, w_down, x1, target)

    da = _mlp_bwd_act(dy_b, w_down, u)
    t_down = emit({"w_mlp_down": _mm(u, dy_b, "tn", BF16, 1024, 1024, 2048, "dw_down")})
    dx1, dg_mlp = _mlp_bwd_x(da, w_up, x1, dy, _tie(p["g_mlp"], t_down))
    t_up = emit({"w_mlp_up": _mm(hm, da, "tn", BF16, 1024, 1024, 2048, "dw_up", column_chunks=True)})
    dproj, doa, dof_t, dom, dya, dyf, dym, db_gate = _merge_bwd(
        dx1, oa, of, om, proj, _tie(p["b_gate"], t_up), wa, wf, wm, w_out)
    t_o = emit({"w_out": _mm(merged, dx1, "tn", BF16, 1024, 1024, 2048, "dw_out"),
                "w_o_swa": _mm(oa, dya, "tn", BF16, 512, 1024, 2048, "dw_o_swa"),
                "w_o_fox": _mm(of, dyf, "tn", BF16, 512, 1024, 2048, "dw_o_fox"),
                "w_o_mem": _mm(om, dym, "tn", BF16, 512, 1024, 2048, "dw_o_mem")})

    dqm, dmk, dmv = _mem_bwd(qm, mk, mv, dom)
    dw_kv, dkn_mem, dg_mem = _memkv_bwd(dmk, dmv, kv_raw, _tie(p["kn_mem"], t_o), mem, p["g_mem"], mem_n, w_kv)
    t_kv = emit({"w_mem_kv": dw_kv})
    dqa, dkp, dvp, dbias, dsink = _swa_bwd(qa, kp, vp, bias_t, _tie(p["sink_swa"], t_kv).reshape(8), doa)
    dqf_t, dkf, dvf, dck4, dcq4 = _fox_bwd(qf_t, kf, vf, dof_t, of_t, cc4, lse4)

    dcq = jnp.transpose(dcq4[:, 0:2, :], (2, 0, 1)).reshape(s, 8)
    dck = jnp.transpose(dck4[:, :, 0:2], (1, 0, 2)).reshape(s, 8)
    dc = jnp.pad(dcq - dck, ((0, 0), (0, 120)))
    dfl, db_forget = _fox_gate_bwd(dc, fl, bf128)

    dproj, dgn = _proj_pre_bwd(dproj, proj, dqf_t, dkf, dvf, dqm, dqa, dkp[SWA_BLOCK:], dvp[SWA_BLOCK:], dfl,
                               gqf, gkf, gqm, gqa, gka)
    t_in = emit({"w_in": _mm(h, dproj, "tn", BF16, 1024, 3072, 1024, "dw_in")})
    grad_x, dg_mix = _in_bwd_x(dproj, w_in, x, _tie(p["g_mix"], t_in), dx1)
    d_rel = _rel_bias_bwd(dbias, bucket_t)

    fold = lambda r: dgn[r:r + 1, 0:HEAD] + dgn[r:r + 1, HEAD:128]
    small = {
        "g_mix": dg_mix, "b_gate": db_gate, "b_forget": db_forget[:, 0:8],
        "qn_swa": fold(3), "kn_swa": fold(4), "sink_swa": dsink[:, 0].reshape(1, 8), "rel_bias": d_rel,
        "qn_fox": fold(0), "kn_fox": fold(1), "g_mem": dg_mem, "qn_mem": dgn[2:3, :], "kn_mem": dkn_mem,
        "g_mlp": dg_mlp,
    }
    return loss, grad_x, small


SMALL = ("g_mix", "b_gate", "b_forget", "qn_swa", "kn_swa", "sink_swa", "rel_bias", "qn_fox", "kn_fox", "g_mem",
         "qn_mem", "kn_mem", "g_mlp")
BIG = ("w_in", "w_mem_kv", "w_o_swa", "w_o_fox", "w_o_mem", "w_out", "w_mlp_up", "w_mlp_down")
COL_SHARDED = ("w_in", "w_o_swa", "w_o_fox", "w_o_mem", "w_mlp_up")
WEIGHTS = ("g_mix", "w_in", "b_gate", "b_forget", "qn_swa", "kn_swa", "sink_swa", "rel_bias", "qn_fox", "kn_fox", "g_mem",
           "w_mem_kv", "qn_mem", "kn_mem", "w_o_swa", "w_o_fox", "w_o_mem", "w_out", "g_mlp", "w_mlp_up", "w_mlp_down")
SMALL_SLOTS = (("g_mix", 1024), ("b_gate", 3072), ("b_forget", 128), ("qn_swa", 128), ("kn_swa", 128), ("sink_swa", 128),
               ("rel_bias", REL_BUCKETS * 128), ("qn_fox", 128), ("kn_fox", 128), ("g_mem", 1024), ("qn_mem", 128),
               ("kn_mem", 128), ("g_mlp", 1024), ("loss", 128))
SMALL_OFF = {n: sum(w for _, w in SMALL_SLOTS[:i]) for i, (n, _) in enumerate(SMALL_SLOTS)}
SMALL_ROW = sum(w for _, w in SMALL_SLOTS)


def _gathered_to_full(name, g):
    if name in COL_SHARDED:
        return jnp.transpose(g, (1, 0, 2)).reshape(g.shape[1], N_DEV * g.shape[2])
    return g.reshape(N_DEV * g.shape[1], g.shape[2])


def _full_to_parts(name, full, b):
    if name in COL_SHARDED:
        return jnp.transpose(full.reshape(full.shape[0], N_DEV, b), (1, 0, 2)).astype(BF16)
    return full.reshape(N_DEV, full.shape[0] // N_DEV, full.shape[1]).astype(BF16)


def _pack_small(grads, loss):
    pieces = []
    for n, width in SMALL_SLOTS:
        a = loss.reshape(1, 1) if n == "loss" else grads[n].reshape(1, -1)
        pieces.append(jnp.pad(a, ((0, 0), (0, width - a.shape[1]))))
    return jnp.concatenate(pieces, axis=1)


def _adamw_small(gathered, w, m, v):
    names = list(SMALL)

    def body(*refs):
        p_ref = refs[0]
        ins = refs[1:1 + 3 * len(names)]
        outs = refs[1 + 3 * len(names):]
        g_all = p_ref[0]
        for k in range(1, N_DEV):
            g_all = g_all + p_ref[k]
        for i, n in enumerate(names):
            w_ref, m_ref, v_ref = ins[3 * i:3 * i + 3]
            out = outs[4 * i:4 * i + 4]
            rows, cols = w_ref.shape
            for r in range(rows):
                off = SMALL_OFF[n] + 128 * r
                g = g_all[:, off:off + cols]
                rs = slice(r, r + 1)
                res = (g,) + _adam_math(w_ref[rs, :], g, m_ref[rs, :], v_ref[rs, :])
                for o_ref, val in zip(out, res):
                    o_ref[rs, :] = val
        outs[-1][...] = g_all[:, SMALL_OFF["loss"]:SMALL_OFF["loss"] + 128]

    args = [gathered]
    out_shape = []
    for n in names:
        args += [w[n], m[n], v[n]]
        out_shape += [jax.ShapeDtypeStruct(w[n].shape, F32)] * 4
    out_shape.append(jax.ShapeDtypeStruct((1, 128), F32))
    outs = pl.pallas_call(body, name="adamw_small", out_shape=out_shape)(*args)
    return {n: outs[4 * i:4 * i + 4] for i, n in enumerate(names)}, outs[-1]


def kernel(x, mem, g_mix, w_in, b_gate, b_forget, qn_swa, kn_swa, sink_swa, rel_bias, qn_fox, kn_fox, g_mem, w_mem_kv, qn_mem, kn_mem, w_o_swa, w_o_fox, w_o_mem, w_out, g_mlp, w_mlp_up, w_mlp_down, loss_target, m_g_mix, m_w_in, m_b_gate, m_b_forget, m_qn_swa, m_kn_swa, m_sink_swa, m_rel_bias, m_qn_fox, m_kn_fox, m_g_mem, m_w_mem_kv, m_qn_mem, m_kn_mem, m_w_o_swa, m_w_o_fox, m_w_o_mem, m_w_out, m_g_mlp, m_w_mlp_up, m_w_mlp_down, v_g_mix, v_w_in, v_b_gate, v_b_forget, v_qn_swa, v_kn_swa, v_sink_swa, v_rel_bias, v_qn_fox, v_kn_fox, v_g_mem, v_w_mem_kv, v_qn_mem, v_kn_mem, v_w_o_swa, v_w_o_fox, v_w_o_mem, v_w_out, v_g_mlp, v_w_mlp_up, v_w_mlp_down):
    wts = dict(g_mix=g_mix, w_in=w_in, b_gate=b_gate, b_forget=b_forget, qn_swa=qn_swa, kn_swa=kn_swa, sink_swa=sink_swa,
               rel_bias=rel_bias, qn_fox=qn_fox, kn_fox=kn_fox, g_mem=g_mem, w_mem_kv=w_mem_kv, qn_mem=qn_mem, kn_mem=kn_mem,
               w_o_swa=w_o_swa, w_o_fox=w_o_fox, w_o_mem=w_o_mem, w_out=w_out, g_mlp=g_mlp, w_mlp_up=w_mlp_up,
               w_mlp_down=w_mlp_down)
    mom = dict(g_mix=m_g_mix, w_in=m_w_in, b_gate=m_b_gate, b_forget=m_b_forget, qn_swa=m_qn_swa, kn_swa=m_kn_swa,
               sink_swa=m_sink_swa, rel_bias=m_rel_bias, qn_fox=m_qn_fox, kn_fox=m_kn_fox, g_mem=m_g_mem, w_mem_kv=m_w_mem_kv,
               qn_mem=m_qn_mem, kn_mem=m_kn_mem, w_o_swa=m_w_o_swa, w_o_fox=m_w_o_fox, w_o_mem=m_w_o_mem, w_out=m_w_out,
               g_mlp=m_g_mlp, w_mlp_up=m_w_mlp_up, w_mlp_down=m_w_mlp_down)
    var = dict(g_mix=v_g_mix, w_in=v_w_in, b_gate=v_b_gate, b_forget=v_b_forget, qn_swa=v_qn_swa, kn_swa=v_kn_swa,
               sink_swa=v_sink_swa, rel_bias=v_rel_bias, qn_fox=v_qn_fox, kn_fox=v_kn_fox, g_mem=v_g_mem, w_mem_kv=v_w_mem_kv,
               qn_mem=v_qn_mem, kn_mem=v_kn_mem, w_o_swa=v_w_o_swa, w_o_fox=v_w_o_fox, w_o_mem=v_w_o_mem, w_out=v_w_out,
               g_mlp=v_g_mlp, w_mlp_up=v_w_mlp_up, w_mlp_down=v_w_mlp_down)

    shards = _cast_shards({n: wts[n][0] for n in BIG})
    first = _split_start([shards["w_in"]], True, "ag_start_w_in", peers=(1,) + SAME_CORE)
    rest = _split_start([shards[n] for n in BIG[1:]], True, "ag_start_rest", after=first[4])
    full = {}

    def getw(n, after):
        if n == "w_in" and n not in full:
            forwarded = _forward_start(first, after, "ag_forward_w_in")
            full[n] = _w_in_from_shards(_forward_wait(first, forwarded, "ag_wait_w_in"))
        elif n not in full:
            land = _split_wait(rest, BIG[1:].index(n), after, True, "ag_wait_" + n)
            full[n] = land if n == "w_mlp_up" else _gathered_to_full(n, land)
        return full[n]

    exchanges = {}

    def emit(grads_by_name):
        parts = []
        for n, grad in grads_by_name.items():
            if n == "w_in":
                parts.append(_dw_in_to_parts(grad))
            else:
                parts.append(grad if n == "w_mlp_up" else _full_to_parts(n, grad, wts[n].shape[2]))
        started = _split_start(parts, False, "rs_start_" + next(iter(grads_by_name)))
        for w, n in enumerate(grads_by_name):
            exchanges[n] = (started, w)
        return started[4]

    small_p = {n: wts[n] for n in SMALL}
    loss, grad_x, small_g = _local_step(x[0], mem[0], loss_target[0], small_p, getw, emit, (first[4], rest[4]))

    packed = _pack_small(small_g, loss)
    small_gather = _split_start([packed], True, "ag_start_small")

    grads, delta, new_m, new_v = {}, {}, {}, {}

    def update(n, after):
        land = _split_wait(*exchanges[n], after, False, "rs_wait_" + n)
        g, d, m2, v2 = _adamw(land, wts[n][0], mom[n][0], var[n][0], "adamw_" + n)
        grads[n], delta[n], new_m[n], new_v[n] = g[None], d[None], m2[None], v2[None]
        return d

    after = small_gather[4]
    for n in exchanges:
        if n != "w_in":
            after = update(n, after)

    gathered = _split_wait(small_gather, 0, after, True, "ag_wait_small")
    small_out, total = _adamw_small(gathered, small_p, mom, var)
    for name, (g, d, m2, v2) in small_out.items():
        grads[name], delta[name], new_m[name], new_v[name] = g, d, m2, v2
    update("w_in", total)

    return (total[0, 0], grad_x[None], *[grads[n] for n in WEIGHTS], *[delta[n] for n in WEIGHTS],
            *[new_m[n] for n in WEIGHTS], *[new_v[n] for n in WEIGHTS])
```

```python
import functools
import math

import jax
import jax.numpy as jnp
from jax import lax
from jax.experimental import pallas as pl
from jax.experimental.pallas import tpu as pltpu

F32 = jnp.float32
BF16 = jnp.bfloat16

D_MODEL = 1024
N_MEM = 256
D_FF = 4096
HEAD = 64
SWA_HEADS = 8
SWA_BLOCK = 128
MEM_HEADS = 4
MEM_HEAD = 128
EPS = 1e-6
NEG = -1e30
REL_BUCKETS = 32
REL_MAX_DIST = 128

ADAM_LR = 0.001
ADAM_B1 = 0.9
ADAM_B2 = 0.999
ADAM_EPS = 1e-08
ADAM_WD = 0.01
ADAM_STEP = 10

GL0, QF0, KF0, VF0, QM0, QA0, KA0, VA0, FL0 = 0, 3072, 3584, 4096, 4608, 5120, 5632, 5760, 5888
PROJ_W = 6144
HALF_W = 3072
H_QF, H_KF, H_VF, H_QM, H_QA, H_KA, H_VA, H_FL = 0, 512, 1024, 1536, 2048, 2560, 2688, 2816

VMEM_LIMIT = 56 * 1024 * 1024
N_DEV = 8
MESH = pl.DeviceIdType.MESH

NN = (((1,), (0,)), ((), ()))
NT = (((1,), (1,)), ((), ()))
TN = (((0,), (0,)), ((), ()))


def _dot(a, b, dims=NN):
    return lax.dot_general(a, b, dims, preferred_element_type=F32)


def _params(sem):
    return pltpu.CompilerParams(dimension_semantics=sem, vmem_limit_bytes=VMEM_LIMIT)


def _full(shape):
    nd = len(shape)
    return pl.BlockSpec(shape, lambda *_: (0,) * nd)


def _sigmoid(z):
    return 1.0 / (1.0 + jnp.exp(-z))


def _group_mean(v, hd):
    if hd == 128:
        return jnp.mean(v, axis=-1, keepdims=True)
    lane = lax.broadcasted_iota(jnp.int32, v.shape, 1)
    lo = lane < HEAD
    s_lo = jnp.sum(jnp.where(lo, v, 0.0), axis=-1, keepdims=True)
    s_hi = jnp.sum(jnp.where(lo, 0.0, v), axis=-1, keepdims=True)
    return jnp.where(lo, s_lo, s_hi) * (1.0 / HEAD)


def _mm(a, b, mode, out_dtype, tm, tn, tk, name, column_chunks=False):
    if mode == "nn":
        m, k = a.shape
        n = b.shape[1]
    elif mode == "nt":
        m, k = a.shape
        n = b.shape[0]
    else:
        k, m = a.shape
        n = b.shape[1]
    tm, tn, tk = min(tm, m), min(tn, n), min(tk, k)
    nk = k // tk
    chunk = n // N_DEV
    per_tile = tn // chunk if column_chunks else 1
    dims = {"nn": NN, "nt": NT, "tn": TN}[mode]
    a_spec = pl.BlockSpec((tk, tm), lambda j, i, kk: (kk, i)) if mode == "tn" else pl.BlockSpec((tm, tk), lambda j, i, kk: (i, kk))
    b_spec = pl.BlockSpec((tn, tk), lambda j, i, kk: (j, kk)) if mode == "nt" else pl.BlockSpec((tk, tn), lambda j, i, kk: (kk, j))

    def body(a_ref, b_ref, o_ref, *acc):
        prod = _dot(a_ref[...].astype(BF16), b_ref[...].astype(BF16), dims)

        def write(res):
            if column_chunks:
                for c in range(per_tile):
                    o_ref[c] = res[:, c * chunk:(c + 1) * chunk].astype(o_ref.dtype)
            else:
                o_ref[...] = res.astype(o_ref.dtype)

        if nk == 1:
            write(prod)
        else:
            acc_ref, = acc
            kk = pl.program_id(2)

            @pl.when(kk == 0)
            def _():
                acc_ref[...] = prod

            @pl.when(kk > 0)
            def _():
                acc_ref[...] += prod

            @pl.when(kk == nk - 1)
            def _():
                write(acc_ref[...])

    return pl.pallas_call(
        body, name=name, grid=(n // tn, m // tm, nk),
        in_specs=[a_spec, b_spec],
        out_specs=(pl.BlockSpec((per_tile, tm, chunk), lambda j, i, kk: (j, i, 0)) if column_chunks
                   else pl.BlockSpec((tm, tn), lambda j, i, kk: (i, j))),
        out_shape=jax.ShapeDtypeStruct((N_DEV, m, chunk) if column_chunks else (m, n), out_dtype),
        scratch_shapes=[pltpu.VMEM((tm, tn), F32)] if nk > 1 else [],
        compiler_params=_params(("parallel", "parallel", "arbitrary")),
    )(a, b)


def _rms_fwd(x, g, name, deps=()):
    s, d = x.shape
    tm = min(512, s)

    def body(x_ref, g_ref, *rest):
        h_ref = rest[len(deps)]
        xv = x_ref[...]
        r = lax.rsqrt(jnp.mean(xv * xv, axis=-1, keepdims=True) + EPS)
        h_ref[...] = (xv * r * g_ref[...]).astype(BF16)

    return pl.pallas_call(
        body, name=name, grid=(s // tm,),
        in_specs=[pl.BlockSpec((tm, d), lambda i: (i, 0)), _full((1, d))] + [pl.BlockSpec(memory_space=pl.ANY)] * len(deps),
        out_specs=pl.BlockSpec((tm, d), lambda i: (i, 0)),
        out_shape=jax.ShapeDtypeStruct((s, d), BF16),
        compiler_params=_params(("parallel",)),
    )(x, g, *deps)


def _proj_post(proj, gq_fox, gk_fox, gq_mem, gq_swa, gk_swa):
    s = proj.shape[0]
    tm = min(256, s)

    def body(p_ref, gqf, gkf, gqm, gqa, gka, qf_ref, kf_ref, vf_ref, qm_ref, qa_ref, ka_ref, va_ref, qft_ref, vft_ref):
        def norm(off, width, hd, g_ref, o_ref, scaled_t_ref=None):
            for b in range(width // 128):
                v = p_ref[:, off + b * 128: off + (b + 1) * 128].astype(F32)
                r = lax.rsqrt(_group_mean(v * v, hd) + EPS)
                vn = (v * r * g_ref[...]).astype(BF16)
                o_ref[:, b * 128:(b + 1) * 128] = vn
                if scaled_t_ref is not None:
                    scaled_t_ref[b * 128:(b + 1) * 128, :] = (vn.astype(F32) * 0.125).T.astype(BF16)

        norm(H_QF, 512, HEAD, gqf, qf_ref, qft_ref)
        norm(H_KF, 512, HEAD, gkf, kf_ref)
        vf_ref[...] = p_ref[:, H_VF:H_VF + 512].astype(BF16)
        for b in range(4):
            vft_ref[b * 128:(b + 1) * 128, :] = p_ref[:, H_VF + b * 128:H_VF + (b + 1) * 128].astype(F32).T.astype(BF16)
        norm(H_QM, 512, MEM_HEAD, gqm, qm_ref)
        norm(H_QA, 512, HEAD, gqa, qa_ref)
        norm(H_KA, 128, HEAD, gka, ka_ref)
        va_ref[...] = p_ref[:, H_VA:H_VA + 128].astype(BF16)

    g_spec = _full((1, 128))
    o512 = pl.BlockSpec((tm, 512), lambda i: (i, 0))
    o128 = pl.BlockSpec((tm, 128), lambda i: (i, 0))
    s512 = jax.ShapeDtypeStruct((s, 512), BF16)
    s128 = jax.ShapeDtypeStruct((s, 128), BF16)
    return pl.pallas_call(
        body, name="proj_post", grid=(s // tm,),
        in_specs=[pl.BlockSpec((tm, HALF_W), lambda i: (i, 1)), g_spec, g_spec, g_spec, g_spec, g_spec],
        out_specs=[o512, o512, o512, o512, o512, o128, o128] + [pl.BlockSpec((512, tm), lambda i: (0, i))] * 2,
        out_shape=[s512, s512, s512, s512, s512, s128, s128] + [jax.ShapeDtypeStruct((512, s), BF16)] * 2,
        compiler_params=_params(("parallel",)),
    )(proj, gq_fox, gk_fox, gq_mem, gq_swa, gk_swa)


def _tri(n, lower):
    r = lax.broadcasted_iota(jnp.int32, (n, n), 0)
    c = lax.broadcasted_iota(jnp.int32, (n, n), 1)
    return jnp.where((c <= r) if lower else (c >= r), 1.0, 0.0).astype(F32)


def _fox_gate_fwd(proj, b_forget128):
    s = proj.shape[0]
    tm = min(512, s)

    def body(p_ref, b_ref, cc_ref, ca_ref, carry_ref):
        i = pl.program_id(0)

        @pl.when(i == 0)
        def _():
            carry_ref[...] = jnp.zeros_like(carry_ref)

        z = p_ref[...] + b_ref[...]
        logf = jnp.minimum(z, 0.0) - jnp.log(1.0 + jnp.exp(-jnp.abs(z)))
        c = jnp.dot(_tri(tm, True), logf, precision=lax.Precision.HIGHEST, preferred_element_type=F32) + carry_ref[...]
        carry_ref[...] = c[tm - 1:tm, :]
        lane = lax.broadcasted_iota(jnp.int32, (tm, 128), 1)
        for hp in range(4):
            cc_ref[hp] = c if hp == 0 else pltpu.roll(c, 128 - 2 * hp, 1)
            aug = jnp.zeros((tm, 128), F32)
            for e in range(2):
                rest = jnp.broadcast_to(c[:, 2 * hp + e:2 * hp + e + 1], (tm, 128))
                for part in range(3):
                    piece = rest.astype(BF16).astype(F32)
                    aug = jnp.where(lane == HEAD * (1 - e) + part, piece, aug)
                    rest = rest - piece
            ca_ref[hp] = aug.astype(BF16)

    return pl.pallas_call(
        body, name="fox_gate_fwd", grid=(s // tm,),
        in_specs=[pl.BlockSpec((tm, 128), lambda i: (i, 0)), _full((1, 128))],
        out_specs=[pl.BlockSpec((4, tm, 128), lambda i: (0, i, 0))] * 2,
        out_shape=[jax.ShapeDtypeStruct((4, s, 128), F32), jax.ShapeDtypeStruct((4, s, 128), BF16)],
        scratch_shapes=[pltpu.VMEM((1, 128), F32)],
        compiler_params=_params(("arbitrary",)),
    )(proj, b_forget128)


def _memkv_fwd(mem, g_mem, w_kv, kn_mem):
    m = mem.shape[0]

    def body(mem_ref, g_ref, w_ref, kn_ref, memn_ref, kv_ref, mk_ref, mv_ref):
        xv = mem_ref[...]
        r = lax.rsqrt(jnp.mean(xv * xv, axis=-1, keepdims=True) + EPS)
        mn = (xv * r * g_ref[...]).astype(BF16)
        memn_ref[...] = mn
        kv = _dot(mn, w_ref[...])
        kv_ref[...] = kv
        for h in range(MEM_HEADS):
            v = kv[:, h * 128:(h + 1) * 128]
            rr = lax.rsqrt(jnp.mean(v * v, axis=-1, keepdims=True) + EPS)
            mk_ref[:, h * 128:(h + 1) * 128] = (v * rr * kn_ref[...]).astype(BF16)
        mv_ref[...] = kv[:, 512:1024].astype(BF16)

    return pl.pallas_call(
        body, name="memkv_fwd",
        out_shape=[jax.ShapeDtypeStruct((m, D_MODEL), BF16), jax.ShapeDtypeStruct((m, 1024), F32),
                   jax.ShapeDtypeStruct((m, 512), BF16), jax.ShapeDtypeStruct((m, 512), BF16)],
        compiler_params=pltpu.CompilerParams(vmem_limit_bytes=VMEM_LIMIT),
    )(mem, g_mem, w_kv, kn_mem)


def _bias_table(rel_bias, bucket):
    def body(rb_ref, bk_ref, o_ref):
        bk = bk_ref[...]
        for h in range(SWA_HEADS):
            acc = jnp.zeros(bk.shape, F32)
            for b in range(REL_BUCKETS):
                acc = jnp.where(bk == b, rb_ref[b, h], acc)
            o_ref[h] = acc

    return pl.pallas_call(
        body, name="bias_table",
        in_specs=[pl.BlockSpec(memory_space=pltpu.SMEM), pl.BlockSpec(memory_space=pltpu.VMEM)],
        out_shape=jax.ShapeDtypeStruct((SWA_HEADS,) + bucket.shape, F32),
    )(rel_bias, bucket)


def _swa_valid(n):
    row = lax.broadcasted_iota(jnp.int32, (SWA_BLOCK, 2 * SWA_BLOCK), 0)
    col = lax.broadcasted_iota(jnp.int32, (SWA_BLOCK, 2 * SWA_BLOCK), 1)
    dist = row + SWA_BLOCK - col
    return (dist >= 0) & (dist < SWA_BLOCK) & ((col >= SWA_BLOCK) | (n > 0))


def _swa_fwd(qa, kp, vp, bias, sink):
    s = qa.shape[0]
    nb = s // SWA_BLOCK

    def body(sink_ref, q_ref, kp_ref, vp_ref, bias_ref, o_ref):
        n = pl.program_id(0)
        start = pl.multiple_of(n * SWA_BLOCK, SWA_BLOCK)
        k2 = kp_ref[pl.ds(start, 2 * SWA_BLOCK), :]
        v2 = vp_ref[pl.ds(start, 2 * SWA_BLOCK), :]
        valid = _swa_valid(n)
        heads = range(SWA_HEADS)
        hs = lambda h: slice(h * HEAD, (h + 1) * HEAD)
        sc = [jnp.where(valid, _dot(q_ref[:, hs(h)], k2[:, hs(h // 4)], NT) * 0.125 + bias_ref[h], NEG) for h in heads]
        pn = []
        for h in heads:
            sk = sink_ref[h]
            mx = jnp.maximum(jnp.max(sc[h], axis=-1, keepdims=True), sk)
            p = jnp.exp(sc[h] - mx)
            den = jnp.sum(p, axis=-1, keepdims=True) + jnp.exp(sk - mx)
            pn.append((p / den).astype(BF16))
        outs = [_dot(pn[h], v2[:, hs(h // 4)]).astype(BF16) for h in heads]
        for h in heads:
            o_ref[:, hs(h)] = outs[h]

    return pl.pallas_call(
        body, name="swa_fwd", grid=(nb,),
        in_specs=[pl.BlockSpec(memory_space=pltpu.SMEM),
                  pl.BlockSpec((SWA_BLOCK, 512), lambda n: (n, 0)),
                  _full(kp.shape), _full(vp.shape), _full(bias.shape)],
        out_specs=pl.BlockSpec((SWA_BLOCK, 512), lambda n: (n, 0)),
        out_shape=jax.ShapeDtypeStruct((s, 512), BF16),
        compiler_params=_params(("parallel",)),
    )(sink, qa, kp, vp, bias)


def _head_mask(e):
    lane = lax.broadcasted_iota(jnp.int32, (1, 128), 1)
    return (lane >= e * HEAD) & (lane < (e + 1) * HEAD)


FOX_FWD_TQ, FOX_FWD_TK = 1024, 1024
FOX_BWD_TK, FOX_BWD_TQ = 512, 512


def _head_rows(e):
    row = lax.broadcasted_iota(jnp.int32, (128, 1), 0)
    return (row >= e * HEAD) & (row < (e + 1) * HEAD)


def _fox_fwd(q, k, v_t, ca4):
    s = q.shape[0]
    t = min(FOX_FWD_TQ, s)
    tk = min(FOX_FWD_TK, s)
    nq = s // t

    def body(q_ref, k_ref, vt_ref, ca_ref, o_ref, lse_ref, ot_ref):
        i = pl.program_id(1)
        qs = q_ref[...] * jnp.asarray(0.125, BF16)
        lane = lax.broadcasted_iota(jnp.int32, (1, 128), 1)
        minus = [jnp.where((lane >= HEAD * (1 - e)) & (lane < HEAD * (1 - e) + 3), -1.0, 0.0).astype(BF16) for e in range(2)]
        qe = [jnp.where(_head_mask(e), qs, jnp.broadcast_to(minus[e], qs.shape)) for e in range(2)]
        n_full = (i * t) // tk
        krow = lax.broadcasted_iota(jnp.int32, (tk, t), 0) + n_full * tk
        qcol = lax.broadcasted_iota(jnp.int32, (tk, t), 1) + i * t

        def step(j, carry, masked):
            ks = pl.ds(pl.multiple_of(j * tk, tk), tk)
            kj = k_ref[ks, :]
            caj = ca_ref[0, ks, :]
            vtj = vt_ref[:, ks]
            out = []
            for e in range(2):
                m, acc = carry[2 * e], carry[2 * e + 1]
                st = _dot(jnp.where(_head_mask(e), kj, caj), qe[e], NT)
                if masked:
                    st = jnp.where(krow <= qcol, st, NEG)
                m_new = jnp.maximum(m, jnp.max(st, axis=0, keepdims=True))
                alpha = jnp.exp(m - m_new)
                pt = jnp.exp(st - m_new).astype(BF16)
                vte = jnp.where(_head_rows(e), vtj, jnp.ones_like(vtj))
                out += [m_new, alpha * acc + _dot(vte, pt)]
            return tuple(out)

        init = (jnp.full((1, t), NEG, F32), jnp.zeros((128, t), F32)) * 2
        carry = lax.fori_loop(0, n_full, functools.partial(step, masked=False), init)
        m0, a0, m1, a1 = step(n_full, carry, True)
        l0 = a0[HEAD:HEAD + 1, :]
        l1 = a1[0:1, :]
        o_t = jnp.where(_head_rows(0), a0 / l0, a1 / l1)
        o_ref[...] = o_t.T.astype(BF16)
        ot_ref[...] = o_t.astype(BF16)
        r8 = lax.broadcasted_iota(jnp.int32, (8, t), 0)
        lse_ref[0] = jnp.where(r8 == 0, m0 + jnp.log(l0), jnp.where(r8 == 1, m1 + jnp.log(l1), 0.0))

    return pl.pallas_call(
        body, name="fox_fwd", grid=(4, nq),
        in_specs=[pl.BlockSpec((t, 128), lambda hp, i: (i, hp)),
                  pl.BlockSpec((s, 128), lambda hp, i: (0, hp)),
                  pl.BlockSpec((128, s), lambda hp, i: (hp, 0)),
                  pl.BlockSpec((1, s, 128), lambda hp, i: (hp, 0, 0))],
        out_specs=[pl.BlockSpec((t, 128), lambda hp, i: (i, hp)),
                   pl.BlockSpec((1, 8, t), lambda hp, i: (hp, 0, i)),
                   pl.BlockSpec((128, t), lambda hp, i: (hp, i))],
        out_shape=[jax.ShapeDtypeStruct((s, 512), BF16), jax.ShapeDtypeStruct((4, 8, s), F32),
                   jax.ShapeDtypeStruct((512, s), BF16)],
        compiler_params=_params(("parallel", "parallel")),
    )(q, k, v_t, ca4)


MEM_SCALE = MEM_HEAD ** -0.5


def _mem_fwd(qm, mk, mv):
    s = qm.shape[0]
    tq = min(512, s)

    def body(q_ref, mk_ref, mv_ref, o_ref):
        for h in range(MEM_HEADS):
            hs = slice(h * 128, (h + 1) * 128)
            sc = _dot(q_ref[:, hs], mk_ref[:, hs], NT) * MEM_SCALE
            mx = jnp.max(sc, axis=-1, keepdims=True)
            p = jnp.exp(sc - mx)
            p = p / jnp.sum(p, axis=-1, keepdims=True)
            o_ref[:, hs] = _dot(p.astype(BF16), mv_ref[:, hs]).astype(BF16)

    return pl.pallas_call(
        body, name="mem_fwd", grid=(s // tq,),
        in_specs=[pl.BlockSpec((tq, 512), lambda i: (i, 0)), _full(mk.shape), _full(mv.shape)],
        out_specs=pl.BlockSpec((tq, 512), lambda i: (i, 0)),
        out_shape=jax.ShapeDtypeStruct((s, 512), BF16),
        compiler_params=_params(("parallel",)),
    )(qm, mk, mv)


def _merge_fwd(x, oa, of, om, proj, b_gate, wa, wf, wm, w_out, g_mlp):
    s = x.shape[0]
    tm = min(256, s)

    def body(x_ref, oa_ref, of_ref, om_ref, gl_ref, bg_ref, wa_ref, wf_ref, wm_ref, wo_ref, g_ref, x1_ref, hm_ref, mg_ref):
        merged = None
        for b, (o_ref, w_ref) in enumerate(((oa_ref, wa_ref), (of_ref, wf_ref), (om_ref, wm_ref))):
            cs = slice(b * D_MODEL, (b + 1) * D_MODEL)
            y = _dot(o_ref[...], w_ref[...])
            t = _sigmoid(gl_ref[:, cs].astype(F32) + bg_ref[:, cs]) * y
            merged = t if merged is None else merged + t
        mb = merged.astype(BF16)
        mg_ref[...] = mb
        x1 = x_ref[...] + _dot(mb, wo_ref[...])
        x1_ref[...] = x1
        r = lax.rsqrt(jnp.mean(x1 * x1, axis=-1, keepdims=True) + EPS)
        hm_ref[...] = (x1 * r * g_ref[...]).astype(BF16)

    row = lambda w: pl.BlockSpec((tm, w), lambda i: (i, 0))
    return pl.pallas_call(
        body, name="merge_fwd", grid=(s // tm,),
        in_specs=[row(D_MODEL), row(512), row(512), row(512), row(HALF_W), _full((1, HALF_W)),
                  _full(wa.shape), _full(wf.shape), _full(wm.shape), _full(w_out.shape), _full((1, D_MODEL))],
        out_specs=[row(D_MODEL), row(D_MODEL), row(D_MODEL)],
        out_shape=[jax.ShapeDtypeStruct((s, D_MODEL), F32), jax.ShapeDtypeStruct((s, D_MODEL), BF16),
                   jax.ShapeDtypeStruct((s, D_MODEL), BF16)],
        compiler_params=_params(("parallel",)),
    )(x, oa, of, om, proj, b_gate, wa, wf, wm, w_out, g_mlp)


def _mlp_up(hm, w_up):
    s = hm.shape[0]
    tm, tn = min(1024, s), w_up.shape[2]

    def body(h_ref, w_ref, u_ref):
        r = jnp.maximum(_dot(h_ref[...], w_ref[0]), 0.0)
        u_ref[...] = (r * r).astype(BF16)

    return pl.pallas_call(
        body, name="mlp_up", grid=(s // tm, D_FF // tn),
        in_specs=[pl.BlockSpec((tm, D_MODEL), lambda i, j: (i, 0)), pl.BlockSpec((1, D_MODEL, tn), lambda i, j: (j, 0, 0))],
        out_specs=pl.BlockSpec((tm, tn), lambda i, j: (i, j)),
        out_shape=jax.ShapeDtypeStruct((s, D_FF), BF16),
        compiler_params=_params(("parallel", "parallel")),
    )(hm, w_up)


def _mlp_down_loss(u, w_down, x1, target):
    s = u.shape[0]
    tm = min(256, s)

    def body(u_ref, w_ref, x1_ref, t_ref, dy_ref, dyb_ref, loss_ref):
        i = pl.program_id(0)

        @pl.when(i == 0)
        def _():
            loss_ref[...] = jnp.zeros_like(loss_ref)

        y = x1_ref[...] + _dot(u_ref[...], w_ref[...])
        err = y - t_ref[...]
        dy = err * (1.0 / D_MODEL)
        dy_ref[...] = dy
        dyb_ref[...] = dy.astype(BF16)
        part = jnp.sum(jnp.sum(err * err, axis=-1, keepdims=True) * (1.0 / D_MODEL), axis=0, keepdims=True)
        loss_ref[...] += 0.5 * part

    row = pl.BlockSpec((tm, D_MODEL), lambda i: (i, 0))
    return pl.pallas_call(
        body, name="mlp_down_loss", grid=(s // tm,),
        in_specs=[pl.BlockSpec((tm, D_FF), lambda i: (i, 0)), _full(w_down.shape), row, row],
        out_specs=[row, row, _full((1, 1))],
        out_shape=[jax.ShapeDtypeStruct((s, D_MODEL), F32), jax.ShapeDtypeStruct((s, D_MODEL), BF16),
                   jax.ShapeDtypeStruct((1, 1), F32)],
        compiler_params=_params(("arbitrary",)),
    )(u, w_down, x1, target)


def _mlp_bwd_act(dy, w_down, u):
    s = dy.shape[0]
    tm, tn = min(1024, s), 1024

    def body(dy_ref, w_ref, u_ref, da_ref):
        du = _dot(dy_ref[...], w_ref[...], NT)
        da_ref[...] = (du * (2.0 * jnp.sqrt(u_ref[...].astype(F32)))).astype(BF16)

    return pl.pallas_call(
        body, name="mlp_bwd_act", grid=(D_FF // tn, s // tm),
        in_specs=[pl.BlockSpec((tm, D_MODEL), lambda j, i: (i, 0)), pl.BlockSpec((tn, D_MODEL), lambda j, i: (j, 0)),
                  pl.BlockSpec((tm, tn), lambda j, i: (i, j))],
        out_specs=pl.BlockSpec((tm, tn), lambda j, i: (i, j)),
        out_shape=jax.ShapeDtypeStruct((s, D_FF), BF16),
        compiler_params=_params(("parallel", "parallel")),
    )(dy, w_down, u)


def _rms_bwd(xv, g, dh, skip):
    r = lax.rsqrt(jnp.mean(xv * xv, axis=-1, keepdims=True) + EPS)
    n = xv * r
    dn = dh * g
    dx = skip + r * (dn - n * jnp.mean(dn * n, axis=-1, keepdims=True))
    return dx, jnp.sum(dh * n, axis=0, keepdims=True)


def _mlp_bwd_x(da, w_up, x1, dy, g_mlp):
    s = da.shape[0]
    tm = min(256, s)

    def body(da_ref, w_ref, x1_ref, dy_ref, g_ref, dx1_ref, dg_ref):
        i = pl.program_id(0)

        @pl.when(i == 0)
        def _():
            dg_ref[...] = jnp.zeros_like(dg_ref)

        tn = w_ref.shape[2]
        dhm = _dot(da_ref[:, 0:tn], w_ref[0], NT)
        for j in range(1, N_DEV):
            dhm = dhm + _dot(da_ref[:, j * tn:(j + 1) * tn], w_ref[j], NT)
        dx, dg = _rms_bwd(x1_ref[...], g_ref[...], dhm, dy_ref[...])
        dx1_ref[...] = dx
        dg_ref[...] += dg

    row = pl.BlockSpec((tm, D_MODEL), lambda i: (i, 0))
    return pl.pallas_call(
        body, name="mlp_bwd_x", grid=(s // tm,),
        in_specs=[pl.BlockSpec((tm, D_FF), lambda i: (i, 0)), _full(w_up.shape), row, row, _full((1, D_MODEL))],
        out_specs=[row, _full((1, D_MODEL))],
        out_shape=[jax.ShapeDtypeStruct((s, D_MODEL), F32), jax.ShapeDtypeStruct((1, D_MODEL), F32)],
        compiler_params=_params(("arbitrary",)),
    )(da, w_up, x1, dy, g_mlp)


def _merge_bwd(dx1, oa, of, om, proj, b_gate, wa, wf, wm, w_out):
    s = dx1.shape[0]
    tm = min(256, s)

    def body(dx1_ref, oa_ref, of_ref, om_ref, gl_ref, bg_ref, wa_ref, wf_ref, wm_ref, wo_ref,
             dp_ref, doa_ref, dof_ref, dom_ref, dya_ref, dyf_ref, dym_ref, dbg_ref):
        i = pl.program_id(0)

        @pl.when(i == 0)
        def _():
            dbg_ref[...] = jnp.zeros_like(dbg_ref)

        dmerged = _dot(dx1_ref[...].astype(BF16), wo_ref[...], NT)
        branches = ((oa_ref, wa_ref, doa_ref, dya_ref), (of_ref, wf_ref, dof_ref, dyf_ref), (om_ref, wm_ref, dom_ref, dym_ref))
        for b, (o_ref, w_ref, do_ref, dyb_ref) in enumerate(branches):
            cs = slice(b * D_MODEL, (b + 1) * D_MODEL)
            y = _dot(o_ref[...], w_ref[...])
            g = _sigmoid(gl_ref[:, cs].astype(F32) + bg_ref[:, cs])
            dz = (dmerged * y) * g * (1.0 - g)
            dp_ref[:, cs] = dz.astype(BF16)
            dbg_ref[:, cs] += jnp.sum(dz, axis=0, keepdims=True)
            dyb = (dmerged * g).astype(BF16)
            dyb_ref[...] = dyb
            do = _dot(dyb, w_ref[...], NT)
            do_ref[...] = (do.T if b == 1 else do).astype(BF16)

    row = lambda w: pl.BlockSpec((tm, w), lambda i: (i, 0))
    sd = lambda w: jax.ShapeDtypeStruct((s, w), BF16)
    return pl.pallas_call(
        body, name="merge_bwd", grid=(s // tm,),
        in_specs=[row(D_MODEL), row(512), row(512), row(512), row(HALF_W), _full((1, HALF_W)),
                  _full(wa.shape), _full(wf.shape), _full(wm.shape), _full(w_out.shape)],
        out_specs=[row(HALF_W), row(512), pl.BlockSpec((512, tm), lambda i: (0, i)), row(512),
                   row(D_MODEL), row(D_MODEL), row(D_MODEL), _full((1, HALF_W))],
        out_shape=[sd(PROJ_W), sd(512), jax.ShapeDtypeStruct((512, s), BF16), sd(512), sd(D_MODEL), sd(D_MODEL), sd(D_MODEL),
                   jax.ShapeDtypeStruct((1, HALF_W), F32)],
        compiler_params=_params(("arbitrary",)),
    )(dx1, oa, of, om, proj, b_gate, wa, wf, wm, w_out)


def _swa_valid_t(n):
    key = lax.broadcasted_iota(jnp.int32, (2 * SWA_BLOCK, SWA_BLOCK), 0)
    qry = lax.broadcasted_iota(jnp.int32, (2 * SWA_BLOCK, SWA_BLOCK), 1)
    dist = qry + SWA_BLOCK - key
    return (dist >= 0) & (dist < SWA_BLOCK) & ((key >= SWA_BLOCK) | (n > 0))


def _swa_bwd(qa, kp, vp, bias_t, sink, doa):
    s = qa.shape[0]
    nb = s // SWA_BLOCK

    def body(sink_ref, q_ref, kp_ref, vp_ref, bias_ref, do_ref, dq_ref, dkp_ref, dvp_ref, dbias_ref, dsink_ref, sk_acc):
        n = pl.program_id(0)

        @pl.when(n == 0)
        def _():
            dkp_ref[...] = jnp.zeros_like(dkp_ref)
            dvp_ref[...] = jnp.zeros_like(dvp_ref)
            dbias_ref[...] = jnp.zeros_like(dbias_ref)
            sk_acc[...] = jnp.zeros_like(sk_acc)

        start = pl.multiple_of(n * SWA_BLOCK, SWA_BLOCK)
        win = pl.ds(start, 2 * SWA_BLOCK)
        k2 = kp_ref[win, :]
        v2 = vp_ref[win, :]
        valid = _swa_valid_t(n)
        heads = range(SWA_HEADS)
        hs = lambda h: slice(h * HEAD, (h + 1) * HEAD)
        scale = jnp.asarray(0.125, BF16)
        q = [q_ref[:, hs(h)] for h in heads]
        do = [do_ref[:, hs(h)] for h in heads]
        kk = [k2[:, hs(kv)] for kv in range(2)]
        vv = [v2[:, hs(kv)] for kv in range(2)]
        kt = [(kk[kv].astype(F32) * 0.125).T.astype(BF16) for kv in range(2)]
        st = [jnp.where(valid, _dot(kk[h // 4], q[h], NT) * 0.125 + bias_ref[h], NEG) for h in heads]
        dpt = [_dot(vv[h // 4], do[h], NT) for h in heads]
        pt, dst = [], []
        for h in heads:
            sk = sink_ref[h]
            mx = jnp.maximum(jnp.max(st[h], axis=0, keepdims=True), sk)
            p = jnp.exp(st[h] - mx)
            esk = jnp.exp(sk - mx)
            den = jnp.sum(p, axis=0, keepdims=True) + esk
            p = p / den
            delta = jnp.sum(p * dpt[h], axis=0, keepdims=True)
            d = p * (dpt[h] - delta)
            sk_acc[h:h + 1, :] += -(esk / den) * delta
            dbias_ref[h] += d
            pt.append(p.astype(BF16))
            dst.append(d.astype(BF16))
        dq_t = [_dot(kt[h // 4], dst[h]) for h in heads]
        dq_ref[...] = jnp.concatenate(dq_t, axis=0).T.astype(BF16)
        for kv in range(2):
            group = range(4 * kv, 4 * kv + 4)
            dk = [_dot(dst[h], q[h] * scale) for h in group]
            dv = [_dot(pt[h], do[h]) for h in group]
            dkp_ref[win, hs(kv)] += (dk[0] + dk[1]) + (dk[2] + dk[3])
            dvp_ref[win, hs(kv)] += (dv[0] + dv[1]) + (dv[2] + dv[3])

        @pl.when(n == nb - 1)
        def _():
            dsink_ref[...] = jnp.broadcast_to(jnp.sum(sk_acc[...], axis=1, keepdims=True), dsink_ref.shape)

    return pl.pallas_call(
        body, name="swa_bwd", grid=(nb,),
        in_specs=[pl.BlockSpec(memory_space=pltpu.SMEM),
                  pl.BlockSpec((SWA_BLOCK, 512), lambda n: (n, 0)),
                  _full(kp.shape), _full(vp.shape), _full(bias_t.shape),
                  pl.BlockSpec((SWA_BLOCK, 512), lambda n: (n, 0))],
        out_specs=[pl.BlockSpec((SWA_BLOCK, 512), lambda n: (n, 0)), _full(kp.shape), _full(vp.shape),
                   _full(bias_t.shape), _full((SWA_HEADS, 128))],
        out_shape=[jax.ShapeDtypeStruct((s, 512), BF16), jax.ShapeDtypeStruct(kp.shape, F32),
                   jax.ShapeDtypeStruct(vp.shape, F32), jax.ShapeDtypeStruct(bias_t.shape, F32),
                   jax.ShapeDtypeStruct((SWA_HEADS, 128), F32)],
        scratch_shapes=[pltpu.VMEM((SWA_HEADS, 128), F32)],
        compiler_params=_params(("arbitrary",)),
    )(sink, qa, kp, vp, bias_t, doa)


def _fox_bwd(qt, k, v, dot, ot, cc4, lse4):
    s = k.shape[0]
    t = min(FOX_BWD_TK, s)
    tq = min(FOX_BWD_TQ, s)
    nq = s // t
    nqt = s // tq

    def body(qt_ref, k_ref, v_ref, dot_ref, ot_ref, cc_ref, lse_ref,
             dqt_ref, dk_ref, dv_ref, dck_ref, dcq_ref, delta_ref, dk0, dk1, dv0, dv1, ds0, ds1):
        j = pl.program_id(1)

        @pl.when(j == 0)
        def _():
            dqt_ref[...] = jnp.zeros_like(dqt_ref)
            dcq_ref[...] = jnp.zeros_like(dcq_ref)
            r8 = lax.broadcasted_iota(jnp.int32, (8, tq), 0)

            def dl(i, c):
                cols = pl.ds(pl.multiple_of(i * tq, tq), tq)
                pr = dot_ref[:, cols].astype(F32) * ot_ref[:, cols].astype(F32)
                d0 = jnp.sum(jnp.where(_head_rows(0), pr, 0.0), axis=0, keepdims=True)
                d1 = jnp.sum(jnp.where(_head_rows(1), pr, 0.0), axis=0, keepdims=True)
                delta_ref[:, cols] = jnp.where(r8 == 0, d0, jnp.where(r8 == 1, d1, 0.0))
                return c

            lax.fori_loop(0, nqt, dl, 0)

        kj = k_ref[...]
        vj = v_ref[...]
        ks = pl.ds(pl.multiple_of(j * t, t), t)
        kt = (kj.astype(F32) * 0.125).T.astype(BF16)
        ke = [jnp.where(_head_mask(e), kj, jnp.zeros_like(kj)) for e in range(2)]
        ve = [jnp.where(_head_mask(e), vj, jnp.zeros_like(vj)) for e in range(2)]
        kte = [jnp.where(_head_rows(e), kt, jnp.zeros_like(kt)) for e in range(2)]
        ck = [cc_ref[0, ks, e:e + 1] for e in range(2)]
        accs = ((dk0, dv0, ds0), (dk1, dv1, ds1))
        for refs in accs:
            for r in refs:
                r[...] = jnp.zeros_like(r)
        i_first = (j * t) // tq
        krow = lax.broadcasted_iota(jnp.int32, (t, tq), 0) + j * t
        qcol = lax.broadcasted_iota(jnp.int32, (t, tq), 1) + i_first * tq

        def step(i, c, masked):
            cols = pl.ds(pl.multiple_of(i * tq, tq), tq)
            qti = qt_ref[:, cols]
            doti = dot_ref[:, cols]
            for e in range(2):
                dkt_acc, dvt_acc, ds_acc = accs[e]
                st = _dot(ke[e], qti) - ck[e]
                if masked:
                    st = jnp.where(krow <= qcol, st, NEG)
                pt = jnp.exp(st - lse_ref[0, e:e + 1, cols])
                dpt = _dot(ve[e], doti)
                dst = pt * (dpt - delta_ref[e:e + 1, cols])
                dsb = dst.astype(BF16)
                dvt_acc[...] += _dot(doti, pt.astype(BF16), NT)
                dkt_acc[...] += _dot(qti, dsb, NT)
                dqt_ref[:, cols] += _dot(kte[e], dsb)
                ds_acc[...] += dst
                dcq_ref[0, e:e + 1, cols] += jnp.sum(dst, axis=0, keepdims=True)
            return c

        step(i_first, 0, True)
        lax.fori_loop(i_first + 1, nqt, functools.partial(step, masked=False), 0)
        r0 = _head_rows(0)
        dk_ref[...] = jnp.where(r0, dk0[...], dk1[...]).T.astype(BF16)
        dv_ref[...] = jnp.where(r0, dv0[...], dv1[...]).T.astype(BF16)
        lane = lax.broadcasted_iota(jnp.int32, (t, 128), 1)
        c0 = jnp.sum(ds0[...], axis=-1, keepdims=True)
        c1 = jnp.sum(ds1[...], axis=-1, keepdims=True)
        dck_ref[0] = jnp.where(lane == 0, c0, jnp.where(lane == 1, c1, 0.0))

    res_t = lambda: pl.BlockSpec((128, s), lambda hp, j: (hp, 0))
    blk = lambda: pl.BlockSpec((t, 128), lambda hp, j: (j, hp))
    return pl.pallas_call(
        body, name="fox_bwd", grid=(4, nq),
        in_specs=[res_t(), blk(), blk(), res_t(), res_t(), pl.BlockSpec((1, s, 128), lambda hp, j: (hp, 0, 0)),
                  pl.BlockSpec((1, 8, s), lambda hp, j: (hp, 0, 0))],
        out_specs=[res_t(), blk(), blk(),
                   pl.BlockSpec((1, t, 128), lambda hp, j: (hp, j, 0)),
                   pl.BlockSpec((1, 8, s), lambda hp, j: (hp, 0, 0))],
        out_shape=[jax.ShapeDtypeStruct((512, s), F32), jax.ShapeDtypeStruct((s, 512), BF16),
                   jax.ShapeDtypeStruct((s, 512), BF16), jax.ShapeDtypeStruct((4, s, 128), F32),
                   jax.ShapeDtypeStruct((4, 8, s), F32)],
        scratch_shapes=[pltpu.VMEM((8, s), F32)] + [pltpu.VMEM((128, t), F32)] * 4 + [pltpu.VMEM((t, tq), F32)] * 2,
        compiler_params=_params(("arbitrary", "arbitrary")),
    )(qt, k, v, dot, ot, cc4, lse4)


def _mem_bwd(qm, mk, mv, dom):
    s = qm.shape[0]
    tq = min(512, s)

    def body(q_ref, mk_ref, mv_ref, do_ref, dq_ref, dmk_ref, dmv_ref):
        i = pl.program_id(0)

        @pl.when(i == 0)
        def _():
            dmk_ref[...] = jnp.zeros_like(dmk_ref)
            dmv_ref[...] = jnp.zeros_like(dmv_ref)

        heads = range(MEM_HEADS)
        hs = lambda h: slice(h * 128, (h + 1) * 128)
        sc = [_dot(q_ref[:, hs(h)], mk_ref[:, hs(h)], NT) * MEM_SCALE for h in heads]
        dp = [_dot(do_ref[:, hs(h)], mv_ref[:, hs(h)], NT) for h in heads]
        pb, dsb = [], []
        for h in heads:
            p = jnp.exp(sc[h] - jnp.max(sc[h], axis=-1, keepdims=True))
            p = p / jnp.sum(p, axis=-1, keepdims=True)
            ds = p * (dp[h] - jnp.sum(p * dp[h], axis=-1, keepdims=True))
            pb.append(p.astype(BF16))
            dsb.append((ds * MEM_SCALE).astype(BF16))
        dq = [_dot(dsb[h], mk_ref[:, hs(h)]).astype(BF16) for h in heads]
        dmk = [_dot(dsb[h], q_ref[:, hs(h)], TN) for h in heads]
        dmv = [_dot(pb[h], do_ref[:, hs(h)], TN) for h in heads]
        for h in heads:
            dq_ref[:, hs(h)] = dq[h]
            dmk_ref[:, hs(h)] += dmk[h]
            dmv_ref[:, hs(h)] += dmv[h]

    return pl.pallas_call(
        body, name="mem_bwd", grid=(s // tq,),
        in_specs=[pl.BlockSpec((tq, 512), lambda i: (i, 0)), _full(mk.shape), _full(mv.shape),
                  pl.BlockSpec((tq, 512), lambda i: (i, 0))],
        out_specs=[pl.BlockSpec((tq, 512), lambda i: (i, 0)), _full(mk.shape), _full(mv.shape)],
        out_shape=[jax.ShapeDtypeStruct((s, 512), BF16), jax.ShapeDtypeStruct(mk.shape, F32),
                   jax.ShapeDtypeStruct(mv.shape, F32)],
        compiler_params=_params(("arbitrary",)),
    )(qm, mk, mv, dom)


def _memkv_bwd(dmk, dmv, kv_raw, kn_mem, mem, g_mem, mem_n, w_kv):
    def body(dmk_ref, dmv_ref, kv_ref, kn_ref, mem_ref, g_ref, mn_ref, w_ref, dw_ref, dkn_ref, dg_ref, dkv_ref):
        dkn = jnp.zeros((1, 128), F32)
        for h in range(MEM_HEADS):
            hs = slice(h * 128, (h + 1) * 128)
            v = kv_ref[:, hs]
            r = lax.rsqrt(jnp.mean(v * v, axis=-1, keepdims=True) + EPS)
            n = v * r
            dn = dmk_ref[:, hs]
            dkn = dkn + jnp.sum(dn * n, axis=0, keepdims=True)
            dng = dn * kn_ref[...]
            dkv_ref[:, hs] = (r * (dng - n * jnp.mean(dng * n, axis=-1, keepdims=True))).astype(BF16)
        dkv_ref[:, 512:1024] = dmv_ref[...].astype(BF16)
        dkn_ref[...] = dkn
        dkv = dkv_ref[...]
        dw_ref[...] = _dot(mn_ref[...], dkv, TN).astype(BF16)
        dmn = _dot(dkv, w_ref[...], NT)
        xv = mem_ref[...]
        r = lax.rsqrt(jnp.mean(xv * xv, axis=-1, keepdims=True) + EPS)
        dg_ref[...] = jnp.sum(dmn * (xv * r), axis=0, keepdims=True)

    m = mem.shape[0]
    return pl.pallas_call(
        body, name="memkv_bwd",
        out_shape=[jax.ShapeDtypeStruct((D_MODEL, 1024), BF16), jax.ShapeDtypeStruct((1, 128), F32),
                   jax.ShapeDtypeStruct((1, D_MODEL), F32)],
        scratch_shapes=[pltpu.VMEM((m, 1024), BF16)],
        compiler_params=pltpu.CompilerParams(vmem_limit_bytes=VMEM_LIMIT),
    )(dmk, dmv, kv_raw, kn_mem, mem, g_mem, mem_n, w_kv)


def _fox_gate_bwd(dc, proj, b_forget128):
    s = dc.shape[0]
    tm = min(512, s)
    nt = s // tm

    def body(dc_ref, p_ref, b_ref, dfl_ref, db_ref, carry_ref):
        i = pl.program_id(0)

        @pl.when(i == 0)
        def _():
            carry_ref[...] = jnp.zeros_like(carry_ref)
            db_ref[...] = jnp.zeros_like(db_ref)

        dcv = dc_ref[...]
        dlogf = jnp.dot(_tri(tm, False), dcv, precision=lax.Precision.HIGHEST, preferred_element_type=F32) + carry_ref[...]
        carry_ref[...] += jnp.sum(dcv, axis=0, keepdims=True)
        z = p_ref[...] + b_ref[...]
        dfl = dlogf * (1.0 / (1.0 + jnp.exp(z)))
        dfl_ref[...] = dfl.astype(BF16)
        db_ref[...] += jnp.sum(dfl, axis=0, keepdims=True)

    return pl.pallas_call(
        body, name="fox_gate_bwd", grid=(nt,),
        in_specs=[pl.BlockSpec((tm, 128), lambda i: (nt - 1 - i, 0)),
                  pl.BlockSpec((tm, 128), lambda i: (nt - 1 - i, 0)), _full((1, 128))],
        out_specs=[pl.BlockSpec((tm, 128), lambda i: (nt - 1 - i, 0)), _full((1, 128))],
        out_shape=[jax.ShapeDtypeStruct((s, 128), BF16), jax.ShapeDtypeStruct((1, 128), F32)],
        scratch_shapes=[pltpu.VMEM((1, 128), F32)],
        compiler_params=_params(("arbitrary",)),
    )(dc, proj, b_forget128)


def _proj_pre_bwd(dproj, proj, dqf, dkf, dvf, dqm, dqa, dka, dva, dfl, gq_fox, gk_fox, gq_mem, gq_swa, gk_swa):
    s = proj.shape[0]
    tm = min(256, s)

    def body(dp_in, p_ref, dqf_ref, dkf_ref, dvf_ref, dqm_ref, dqa_ref, dka_ref, dva_ref, dfl_ref,
             gqf, gkf, gqm, gqa, gka, dp_ref, dgn_ref):
        i = pl.program_id(0)

        @pl.when(i == 0)
        def _():
            dgn_ref[...] = jnp.zeros_like(dgn_ref)

        def norm_bwd(off, width, hd, g_ref, dn_ref, slot):
            acc = jnp.zeros((1, 128), F32)
            for b in range(width // 128):
                v = p_ref[:, off + b * 128: off + (b + 1) * 128].astype(F32)
                r = lax.rsqrt(_group_mean(v * v, hd) + EPS)
                n = v * r
                dn = dn_ref[b * 128:(b + 1) * 128, :].T if slot == 0 else dn_ref[:, b * 128:(b + 1) * 128].astype(F32)
                acc = acc + jnp.sum(dn * n, axis=0, keepdims=True)
                dng = dn * g_ref[...]
                dp_ref[:, off + b * 128: off + (b + 1) * 128] = (r * (dng - n * _group_mean(dng * n, hd))).astype(BF16)
            dgn_ref[slot:slot + 1, :] += acc

        norm_bwd(H_QF, 512, HEAD, gqf, dqf_ref, 0)
        norm_bwd(H_KF, 512, HEAD, gkf, dkf_ref, 1)
        dp_ref[:, H_VF:H_VF + 512] = dvf_ref[...].astype(BF16)
        norm_bwd(H_QM, 512, MEM_HEAD, gqm, dqm_ref, 2)
        norm_bwd(H_QA, 512, HEAD, gqa, dqa_ref, 3)
        norm_bwd(H_KA, 128, HEAD, gka, dka_ref, 4)
        dp_ref[:, H_VA:H_VA + 128] = dva_ref[...].astype(BF16)
        dp_ref[:, H_FL:H_FL + 128] = dfl_ref[...]
        dp_ref[:, H_FL + 128:HALF_W] = jnp.zeros((tm, HALF_W - H_FL - 128), BF16)

    row = lambda w: pl.BlockSpec((tm, w), lambda i: (i, 0))
    g_spec = _full((1, 128))
    return pl.pallas_call(
        body, name="proj_pre_bwd", grid=(s // tm,),
        in_specs=[pl.BlockSpec(memory_space=pl.ANY), pl.BlockSpec((tm, HALF_W), lambda i: (i, 1)),
                  pl.BlockSpec((512, tm), lambda i: (0, i)), row(512), row(512), row(512), row(512),
                  row(128), row(128), row(128), g_spec, g_spec, g_spec, g_spec, g_spec],
        out_specs=[pl.BlockSpec((tm, HALF_W), lambda i: (i, 1)), _full((8, 128))],
        out_shape=[jax.ShapeDtypeStruct((s, PROJ_W), BF16), jax.ShapeDtypeStruct((8, 128), F32)],
        input_output_aliases={0: 0},
        compiler_params=_params(("arbitrary",)),
    )(dproj, proj, dqf, dkf, dvf, dqm, dqa, dka, dva, dfl, gq_fox, gk_fox, gq_mem, gq_swa, gk_swa)


def _in_bwd_x(dproj, w_in_p, x, g_mix, dx1):
    s = x.shape[0]
    tm = min(256, s)

    def body(dp_ref, w_ref, x_ref, g_ref, dx1_ref, gx_ref, dg_ref):
        i = pl.program_id(0)

        @pl.when(i == 0)
        def _():
            dg_ref[...] = jnp.zeros_like(dg_ref)

        dx, dg = _rms_bwd(x_ref[...], g_ref[...], _dot(dp_ref[...], w_ref[...], NT), dx1_ref[...])
        gx_ref[...] = dx
        dg_ref[...] += dg

    row = pl.BlockSpec((tm, D_MODEL), lambda i: (i, 0))
    return pl.pallas_call(
        body, name="in_bwd_x", grid=(s // tm,),
        in_specs=[pl.BlockSpec((tm, PROJ_W), lambda i: (i, 0)), _full(w_in_p.shape), row, _full((1, D_MODEL)), row],
        out_specs=[row, _full((1, D_MODEL))],
        out_shape=[jax.ShapeDtypeStruct((s, D_MODEL), F32), jax.ShapeDtypeStruct((1, D_MODEL), F32)],
        compiler_params=_params(("arbitrary",)),
    )(dproj, w_in_p, x, g_mix, dx1)


def _rel_bias_bwd(dbias, bucket):
    def body(db_ref, bk_ref, o_ref):
        bk = bk_ref[...]
        lane = lax.broadcasted_iota(jnp.int32, (1, 128), 1)
        for b in range(REL_BUCKETS):
            sel = bk == b
            acc = jnp.zeros((1, 128), F32)
            for h in range(SWA_HEADS):
                tot = jnp.sum(jnp.sum(jnp.where(sel, db_ref[h], 0.0), axis=-1, keepdims=True), axis=0, keepdims=True)
                acc = jnp.where(lane == h, tot, acc)
            o_ref[:, b * 128:(b + 1) * 128] = acc

    return pl.pallas_call(
        body, name="rel_bias_bwd",
        out_shape=jax.ShapeDtypeStruct((1, REL_BUCKETS * 128), F32),
        compiler_params=pltpu.CompilerParams(vmem_limit_bytes=VMEM_LIMIT),
    )(dbias, bucket)


def _my_place():
    return lax.axis_index("x"), lax.axis_index("y"), lax.axis_index("c")


def _peer(place, k):
    x, y, c = place
    return (1 - x if k & 4 else x, 1 - y if k & 2 else y, 1 - c if k & 1 else c)


def _index(place):
    x, y, c = place
    return 4 * x + 2 * y + c


HBM_SPEC = pl.BlockSpec(memory_space=pltpu.HBM)
SEM_SPEC = pl.BlockSpec(memory_space=pltpu.SEMAPHORE)
DATAFLOW = pltpu.SideEffectType.DATAFLOW_SIDE_EFFECTING


ALL_PEERS = tuple(range(1, N_DEV))
SAME_CORE = (2, 4, 6)
OWN = N_DEV - 1


def _split_copy(src_ref, land_ref, send_sems, recv_sems, me, k, gather):
    peer = _peer(me, k)
    if gather:
        src, dst = src_ref, land_ref.at[_index(me)]
    else:
        src, dst = src_ref.at[_index(peer)], land_ref.at[k - 1]
    return pltpu.make_async_remote_copy(src_ref=src, dst_ref=dst, send_sem=send_sems.at[k - 1], recv_sem=recv_sems.at[k - 1],
                                        device_id=peer, device_id_type=MESH)


def _own_copy(src_ref, land_ref, recv_sems, me, gather):
    if gather:
        src, dst = src_ref, land_ref.at[_index(me)]
    else:
        src, dst = src_ref.at[_index(me)], land_ref.at[OWN]
    return pltpu.make_async_copy(src, dst, recv_sems.at[OWN])


def _split_start(srcs, gather, name, peers=ALL_PEERS, after=None):
    n = len(srcs)
    extra = [] if after is None else [after]

    def body(*refs):
        refs = refs[:2 * n] + refs[2 * n + len(extra):]
        src_refs, land_refs = refs[:n], refs[n:2 * n]
        send_sems, recv_sems, token = refs[2 * n:3 * n], refs[3 * n:4 * n], refs[-1]
        me = _my_place()
        for w in range(n):
            for k in peers:
                _split_copy(src_refs[w], land_refs[w], send_sems[w], recv_sems[w], me, k, gather).start()
            _own_copy(src_refs[w], land_refs[w], recv_sems[w], me, gather).start()
        token[...] = jnp.zeros_like(token)

    lands = [lax.empty((N_DEV,) + (a.shape if gather else a.shape[1:]), a.dtype) for a in srcs]
    sems = [pltpu.SemaphoreType.DMA((N_DEV,))] * (2 * n)
    hbm = [pltpu.HBM(a.shape, a.dtype) for a in list(srcs) + lands]
    outs = pl.pallas_call(
        body, name=name,
        out_shape=(*sems, *hbm, jax.ShapeDtypeStruct((8, 128), F32)),
        in_specs=(HBM_SPEC,) * (2 * n) + (pl.BlockSpec(memory_space=pl.ANY),) * len(extra),
        out_specs=(SEM_SPEC,) * (2 * n) + (HBM_SPEC,) * (2 * n) + (pl.BlockSpec(memory_space=pltpu.VMEM),),
        input_output_aliases={i: 2 * n + i for i in range(2 * n)},
        compiler_params=pltpu.CompilerParams(has_side_effects=DATAFLOW),
    )(*[pltpu.with_memory_space_constraint(a, pltpu.HBM) for a in list(srcs) + lands], *extra)
    return list(outs[:n]), list(outs[n:2 * n]), list(outs[2 * n:3 * n]), list(outs[3 * n:4 * n]), outs[-1]


def _split_wait(started, w, after, gather, name):
    send_sems, recv_sems, srcs, lands, _ = started

    def body(src_ref, land_ref, send_sems, recv_sems, after_ref, src_out, land_out):
        me = _my_place()
        for k in ALL_PEERS:
            cp = _split_copy(src_ref, land_ref, send_sems, recv_sems, me, k, gather)
            cp.wait_send()
            cp.wait_recv()
        _own_copy(src_ref, land_ref, recv_sems, me, gather).wait()

    return pl.pallas_call(
        body, name=name,
        out_shape=(pltpu.HBM(srcs[w].shape, srcs[w].dtype), pltpu.HBM(lands[w].shape, lands[w].dtype)),
        in_specs=(HBM_SPEC, HBM_SPEC, SEM_SPEC, SEM_SPEC, pl.BlockSpec(memory_space=pl.ANY)),
        out_specs=(HBM_SPEC, HBM_SPEC), input_output_aliases={0: 0, 1: 1},
        compiler_params=pltpu.CompilerParams(has_side_effects=DATAFLOW),
    )(srcs[w], lands[w], send_sems[w], recv_sems[w], after)[1]


def _forward_copy(land_ref, send_sems, recv_sems, me, j, incoming):
    sibling = _peer(me, 1)
    rows = land_ref.at[_index(_peer(sibling if incoming else me, SAME_CORE[j]))]
    return pltpu.make_async_remote_copy(src_ref=rows, dst_ref=rows, send_sem=send_sems.at[j], recv_sem=recv_sems.at[j],
                                        device_id=sibling, device_id_type=MESH)


def _forward_start(started, after, name):
    send_a, recv_a, srcs, lands, _ = started

    def body(src_ref, land_ref, send_a, recv_a, after_ref, send_b, recv_b, src_out, land_out):
        me = _my_place()
        for j, k in enumerate(SAME_CORE):
            _split_copy(src_ref, land_ref, send_a, recv_a, me, k, True).wait_recv()
            _forward_copy(land_ref, send_b, recv_b, me, j, False).start()

    sems = pltpu.SemaphoreType.DMA((len(SAME_CORE),))
    return pl.pallas_call(
        body, name=name,
        out_shape=(sems, sems, pltpu.HBM(srcs[0].shape, srcs[0].dtype), pltpu.HBM(lands[0].shape, lands[0].dtype)),
        in_specs=(HBM_SPEC, HBM_SPEC, SEM_SPEC, SEM_SPEC, pl.BlockSpec(memory_space=pl.ANY)),
        out_specs=(SEM_SPEC, SEM_SPEC, HBM_SPEC, HBM_SPEC), input_output_aliases={0: 2, 1: 3},
        compiler_params=pltpu.CompilerParams(has_side_effects=DATAFLOW),
    )(srcs[0], lands[0], send_a[0], recv_a[0], after)


def _forward_wait(started, forwarded, name):
    send_a, recv_a, _, _, _ = started
    send_b, recv_b, src, land = forwarded

    def body(src_ref, land_ref, send_a, recv_a, send_b, recv_b, src_out, land_out):
        me = _my_place()
        _own_copy(src_ref, land_ref, recv_a, me, True).wait()
        for k in (1,) + SAME_CORE:
            _split_copy(src_ref, land_ref, send_a, recv_a, me, k, True).wait_send()
        _split_copy(src_ref, land_ref, send_a, recv_a, me, 1, True).wait_recv()
        for j in range(len(SAME_CORE)):
            _forward_copy(land_ref, send_b, recv_b, me, j, False).wait_send()
            _forward_copy(land_ref, send_b, recv_b, me, j, True).wait_recv()

    return pl.pallas_call(
        body, name=name,
        out_shape=(pltpu.HBM(src.shape, src.dtype), pltpu.HBM(land.shape, land.dtype)),
        in_specs=(HBM_SPEC, HBM_SPEC, SEM_SPEC, SEM_SPEC, SEM_SPEC, SEM_SPEC),
        out_specs=(HBM_SPEC, HBM_SPEC), input_output_aliases={0: 0, 1: 1},
        compiler_params=pltpu.CompilerParams(has_side_effects=DATAFLOW),
    )(src, land, send_a[0], recv_a[0], send_b, recv_b)[1]


def _adam_math(w, g, m, v):
    m2 = ADAM_B1 * m + (1.0 - ADAM_B1) * g
    v2 = ADAM_B2 * v + (1.0 - ADAM_B2) * (g * g)
    m_hat = m2 / (1.0 - ADAM_B1 ** ADAM_STEP)
    v_hat = v2 / (1.0 - ADAM_B2 ** ADAM_STEP)
    delta = -ADAM_LR * (m_hat / (jnp.sqrt(v_hat) + ADAM_EPS) + ADAM_WD * w)
    return delta, m2, v2


def _adamw(land, w, m, v, name):
    a, b = w.shape
    bp = land.shape[2]
    ta = min(128, a)

    def body(p_ref, w_ref, m_ref, v_ref, g_ref, d_ref, m2_ref, v2_ref):
        g = p_ref[0, :, 0:b].astype(F32)
        for k in range(1, N_DEV):
            g = g + p_ref[k, :, 0:b].astype(F32)
        delta, m2, v2 = _adam_math(w_ref[...], g, m_ref[...], v_ref[...])
        g_ref[...] = g
        d_ref[...] = delta
        m2_ref[...] = m2
        v2_ref[...] = v2

    blk = pl.BlockSpec((ta, b), lambda i: (i, 0))
    sd = jax.ShapeDtypeStruct((a, b), F32)
    return pl.pallas_call(
        body, name=name, grid=(a // ta,),
        in_specs=[pl.BlockSpec((N_DEV, ta, bp), lambda i: (0, i, 0)), blk, blk, blk],
        out_specs=[blk, blk, blk, blk], out_shape=[sd, sd, sd, sd],
        compiler_params=_params(("parallel",)),
    )(land, w, m, v)


def _bucket_table():
    t_loc = jnp.arange(SWA_BLOCK)[:, None] + SWA_BLOCK
    s_loc = jnp.arange(2 * SWA_BLOCK)[None, :]
    dist = t_loc - s_loc
    max_exact = REL_BUCKETS // 2
    d = jnp.maximum(dist, 0)
    df = jnp.maximum(d, 1).astype(F32)
    large = max_exact + (jnp.log(df / max_exact) / math.log(REL_MAX_DIST / max_exact) * (REL_BUCKETS - max_exact)).astype(jnp.int32)
    large = jnp.minimum(large, REL_BUCKETS - 1)
    bucket = jnp.where(d < max_exact, d, large)
    band = (dist >= 0) & (dist < SWA_BLOCK)
    return bucket, band


def _tile2(g):
    return jnp.concatenate([g, g], axis=1) if g.shape[1] == HEAD else g


SHARD_W = 737
SHARD_WP = 768
IN_WIDTH = N_DEV * SHARD_W
SEGMENTS = ((GL0, 2824, 3072), (QF0, 768, 512), (KF0, 1280, 512), (VF0, 1792, 512), (QM0, 2312, 512),
            (QA0, 0, 512), (KA0, 512, 128), (VA0, 640, 128), (FL0, 2304, 8))


def _lane_plan(sources):
    plan = []
    for t in range(len(sources) // 128):
        groups = {}
        for lane in range(128):
            src = sources[128 * t + lane]
            if src is not None:
                slab, col = src
                groups.setdefault((slab, col // 128, (lane - col) % 128), []).append(lane)
        tile = []
        for key, lanes in groups.items():
            assert lanes == list(range(lanes[0], lanes[-1] + 1))
            tile.append((key, lanes[0], lanes[-1] + 1))
        plan.append(tile)
    return plan


def _assemble(tile_plan, load, rows):
    lane = lax.broadcasted_iota(jnp.int32, (1, 128), 1)
    out = jnp.zeros((rows, 128), F32)
    for (slab, st, roll), lo, hi in tile_plan:
        v = load(slab, st)
        if roll:
            v = pltpu.roll(v, roll, 1)
        out = v if (lo, hi) == (0, 128) else jnp.where((lane >= lo) & (lane < hi), v, out)
    return out


def _w_in_from_shards(land):
    ref_col = [None] * PROJ_W
    for p0, r0, n in SEGMENTS:
        for i in range(n):
            ref_col[p0 + i] = divmod(r0 + i, SHARD_W)
    plan = _lane_plan(ref_col)
    d_model = land.shape[1]
    tm = 256

    def body(land_ref, o_ref):
        load = lambda slab, st: land_ref[slab, :, st * 128:(st + 1) * 128].astype(F32)
        for t, tile_plan in enumerate(plan):
            o_ref[:, t * 128:(t + 1) * 128] = _assemble(tile_plan, load, tm).astype(BF16)

    return pl.pallas_call(
        body, name="w_in_from_shards", grid=(d_model // tm,),
        in_specs=[pl.BlockSpec((N_DEV, tm, SHARD_WP), lambda i: (0, i, 0))],
        out_specs=pl.BlockSpec((tm, PROJ_W), lambda i: (i, 0)),
        out_shape=jax.ShapeDtypeStruct((d_model, PROJ_W), BF16),
        compiler_params=_params(("parallel",)),
    )(land)


def _dw_in_to_parts(dwp):
    padded_col = [None] * IN_WIDTH
    for p0, r0, n in SEGMENTS:
        for i in range(n):
            padded_col[r0 + i] = p0 + i
    sources = []
    for d in range(N_DEV):
        sources += [(0, padded_col[SHARD_W * d + c]) if c < SHARD_W else None for c in range(SHARD_WP)]
    plan = _lane_plan(sources)
    d_model = dwp.shape[0]
    tm = 256
    tiles = SHARD_WP // 128

    def body(dw_ref, o_ref):
        load = lambda slab, st: dw_ref[:, st * 128:(st + 1) * 128].astype(F32)
        for t, tile_plan in enumerate(plan):
            d, c = divmod(t, tiles)
            o_ref[d, :, c * 128:(c + 1) * 128] = _assemble(tile_plan, load, tm).astype(BF16)

    return pl.pallas_call(
        body, name="dw_in_to_parts", grid=(d_model // tm,),
        in_specs=[pl.BlockSpec((tm, PROJ_W), lambda i: (i, 0))],
        out_specs=pl.BlockSpec((N_DEV, tm, SHARD_WP), lambda i: (0, i, 0)),
        out_shape=jax.ShapeDtypeStruct((N_DEV, d_model, SHARD_WP), BF16),
        compiler_params=_params(("parallel",)),
    )(dwp)


def _cast_shards(shards):
    names = list(shards)

    def body(*refs):
        for src, dst in zip(refs[:len(names)], refs[len(names):]):
            if dst.shape != src.shape:
                dst[...] = jnp.zeros(dst.shape, BF16)
                dst[:, 0:src.shape[1]] = src[...].astype(BF16)
            else:
                dst[...] = src[...].astype(BF16)

    out_shape = [jax.ShapeDtypeStruct((shards[n].shape[0], SHARD_WP if n == "w_in" else shards[n].shape[1]), BF16)
                 for n in names]
    outs = pl.pallas_call(body, name="cast_shards", out_shape=out_shape,
                          compiler_params=pltpu.CompilerParams(vmem_limit_bytes=VMEM_LIMIT))(*[shards[n] for n in names])
    return dict(zip(names, outs))


def _tie(x, *tokens):
    for t in tokens:
        if t is not None:
            x = x + t[0:1, 0:1]
    return x


def _local_step(x, mem, target, p, getw, emit, deps=()):
    s = x.shape[0]
    bucket, band = _bucket_table()
    bucket_m = jnp.where(band, bucket, -1).astype(jnp.int32)
    bias = _bias_table(p["rel_bias"], bucket_m)
    bucket_t = jnp.transpose(bucket_m)
    bias_t = _bias_table(p["rel_bias"], bucket_t)
    gqf, gkf, gqa, gka = _tile2(p["qn_fox"]), _tile2(p["kn_fox"]), _tile2(p["qn_swa"]), _tile2(p["kn_swa"])
    gqm = p["qn_mem"]
    bf128 = jnp.pad(p["b_forget"], ((0, 0), (0, 120)))
    sink = p["sink_swa"].reshape(8)

    h = _rms_fwd(x, p["g_mix"], "rms_mix", deps)
    w_in = getw("w_in", h)
    proj = _mm(h, w_in, "nn", BF16, 512, 1536, 1024, "proj")
    fl = _mm(h, w_in[:, FL0:FL0 + 128], "nn", F32, 512, 128, 1024, "proj_fl")
    qf, kf, vf, qm, qa, ka, va, qf_t, vf_t = _proj_post(proj, gqf, gkf, gqm, gqa, gka)
    cc4, ca4 = _fox_gate_fwd(fl, bf128)
    w_kv = getw("w_mem_kv", cc4)
    mem_n, kv_raw, mk, mv = _memkv_fwd(mem, p["g_mem"], w_kv, p["kn_mem"])
    kp = jnp.pad(ka, ((SWA_BLOCK, 0), (0, 0)))
    vp = jnp.pad(va, ((SWA_BLOCK, 0), (0, 0)))
    oa = _swa_fwd(qa, kp, vp, bias, sink)
    of, lse4, of_t = _fox_fwd(qf, kf, vf_t, ca4)
    om = _mem_fwd(qm, mk, mv)
    wa, wf, wm, w_out = getw("w_o_swa", oa), getw("w_o_fox", oa), getw("w_o_mem", oa), getw("w_out", oa)
    x1, hm, merged = _merge_fwd(x, oa, of, om, proj, p["b_gate"], wa, wf, wm, w_out, p["g_mlp"])
    w_up = getw("w_mlp_up", of)
    u = _mlp_up(hm, w_up)
    w_down = getw("w_mlp_down", hm)
    dy, dy_b, loss = _mlp_down_loss(u, w_down, x1, target)

    da = _mlp_bwd_act(dy_b, w_down, u)
    t_down = emit({"w_mlp_down": _mm(u, dy_b, "tn", BF16, 1024, 1024, 2048, "dw_down")})
    dx1, dg_mlp = _mlp_bwd_x(da, w_up, x1, dy, _tie(p["g_mlp"], t_down))
    t_up = emit({"w_mlp_up": _mm(hm, da, "tn", BF16, 1024, 1024, 2048, "dw_up", column_chunks=True)})
    dproj, doa, dof_t, dom, dya, dyf, dym, db_gate = _merge_bwd(
        dx1, oa, of, om, proj, _tie(p["b_gate"], t_up), wa, wf, wm, w_out)
    t_o = emit({"w_out": _mm(merged, dx1, "tn", BF16, 1024, 1024, 2048, "dw_out"),
                "w_o_swa": _mm(oa, dya, "tn", BF16, 512, 1024, 2048, "dw_o_swa"),
                "w_o_fox": _mm(of, dyf, "tn", BF16, 512, 1024, 2048, "dw_o_fox"),
                "w_o_mem": _mm(om, dym, "tn", BF16, 512, 1024, 2048, "dw_o_mem")})

    dqm, dmk, dmv = _mem_bwd(qm, mk, mv, dom)
    dw_kv, dkn_mem, dg_mem = _memkv_bwd(dmk, dmv, kv_raw, _tie(p["kn_mem"], t_o), mem, p["g_mem"], mem_n, w_kv)
    t_kv = emit({"w_mem_kv": dw_kv})
    dqa, dkp, dvp, dbias, dsink = _swa_bwd(qa, kp, vp, bias_t, _tie(p["sink_swa"], t_kv).reshape(8), doa)
    dqf_t, dkf, dvf, dck4, dcq4 = _fox_bwd(qf_t, kf, vf, dof_t, of_t, cc4, lse4)

    dcq = jnp.transpose(dcq4[:, 0:2, :], (2, 0, 1)).reshape(s, 8)
    dck = jnp.transpose(dck4[:, :, 0:2], (1, 0, 2)).reshape(s, 8)
    dc = jnp.pad(dcq - dck, ((0, 0), (0, 120)))
    dfl, db_forget = _fox_gate_bwd(dc, fl, bf128)

    dproj, dgn = _proj_pre_bwd(dproj, proj, dqf_t, dkf, dvf, dqm, dqa, dkp[SWA_BLOCK:], dvp[SWA_BLOCK:], dfl,
                               gqf, gkf, gqm, gqa, gka)
    t_in = emit({"w_in": _mm(h, dproj, "tn", BF16, 1024, 3072, 1024, "dw_in")})
    grad_x, dg_mix = _in_bwd_x(dproj, w_in, x, _tie(p["g_mix"], t_in), dx1)
    d_rel = _rel_bias_bwd(dbias, bucket_t)

    fold = lambda r: dgn[r:r + 1, 0:HEAD] + dgn[r:r + 1, HEAD:128]
    small = {
        "g_mix": dg_mix, "b_gate": db_gate, "b_forget": db_forget[:, 0:8],
        "qn_swa": fold(3), "kn_swa": fold(4), "sink_swa": dsink[:, 0].reshape(1, 8), "rel_bias": d_rel,
        "qn_fox": fold(0), "kn_fox": fold(1), "g_mem": dg_mem, "qn_mem": dgn[2:3, :], "kn_mem": dkn_mem,
        "g_mlp": dg_mlp,
    }
    return loss, grad_x, small


SMALL = ("g_mix", "b_gate", "b_forget", "qn_swa", "kn_swa", "sink_swa", "rel_bias", "qn_fox", "kn_fox", "g_mem",
         "qn_mem", "kn_mem", "g_mlp")
BIG = ("w_in", "w_mem_kv", "w_o_swa", "w_o_fox", "w_o_mem", "w_out", "w_mlp_up", "w_mlp_down")
COL_SHARDED = ("w_in", "w_o_swa", "w_o_fox", "w_o_mem", "w_mlp_up")
WEIGHTS = ("g_mix", "w_in", "b_gate", "b_forget", "qn_swa", "kn_swa", "sink_swa", "rel_bias", "qn_fox", "kn_fox", "g_mem",
           "w_mem_kv", "qn_mem", "kn_mem", "w_o_swa", "w_o_fox", "w_o_mem", "w_out", "g_mlp", "w_mlp_up", "w_mlp_down")
SMALL_SLOTS = (("g_mix", 1024), ("b_gate", 3072), ("b_forget", 128), ("qn_swa", 128), ("kn_swa", 128), ("sink_swa", 128),
               ("rel_bias", REL_BUCKETS * 128), ("qn_fox", 128), ("kn_fox", 128), ("g_mem", 1024), ("qn_mem", 128),
               ("kn_mem", 128), ("g_mlp", 1024), ("loss", 128))
SMALL_OFF = {n: sum(w for _, w in SMALL_SLOTS[:i]) for i, (n, _) in enumerate(SMALL_SLOTS)}
SMALL_ROW = sum(w for _, w in SMALL_SLOTS)


def _gathered_to_full(name, g):
    if name in COL_SHARDED:
        return jnp.transpose(g, (1, 0, 2)).reshape(g.shape[1], N_DEV * g.shape[2])
    return g.reshape(N_DEV * g.shape[1], g.shape[2])


def _full_to_parts(name, full, b):
    if name in COL_SHARDED:
        return jnp.transpose(full.reshape(full.shape[0], N_DEV, b), (1, 0, 2)).astype(BF16)
    return full.reshape(N_DEV, full.shape[0] // N_DEV, full.shape[1]).astype(BF16)


def _pack_small(grads, loss):
    pieces = []
    for n, width in SMALL_SLOTS:
        a = loss.reshape(1, 1) if n == "loss" else grads[n].reshape(1, -1)
        pieces.append(jnp.pad(a, ((0, 0), (0, width - a.shape[1]))))
    return jnp.concatenate(pieces, axis=1)


def _adamw_small(gathered, w, m, v):
    names = list(SMALL)

    def body(*refs):
        p_ref = refs[0]
        ins = refs[1:1 + 3 * len(names)]
        outs = refs[1 + 3 * len(names):]
        g_all = p_ref[0]
        for k in range(1, N_DEV):
            g_all = g_all + p_ref[k]
        for i, n in enumerate(names):
            w_ref, m_ref, v_ref = ins[3 * i:3 * i + 3]
            out = outs[4 * i:4 * i + 4]
            rows, cols = w_ref.shape
            for r in range(rows):
                off = SMALL_OFF[n] + 128 * r
                g = g_all[:, off:off + cols]
                rs = slice(r, r + 1)
                res = (g,) + _adam_math(w_ref[rs, :], g, m_ref[rs, :], v_ref[rs, :])
                for o_ref, val in zip(out, res):
                    o_ref[rs, :] = val
        outs[-1][...] = g_all[:, SMALL_OFF["loss"]:SMALL_OFF["loss"] + 128]

    args = [gathered]
    out_shape = []
    for n in names:
        args += [w[n], m[n], v[n]]
        out_shape += [jax.ShapeDtypeStruct(w[n].shape, F32)] * 4
    out_shape.append(jax.ShapeDtypeStruct((1, 128), F32))
    outs = pl.pallas_call(body, name="adamw_small", out_shape=out_shape)(*args)
    return {n: outs[4 * i:4 * i + 4] for i, n in enumerate(names)}, outs[-1]


def kernel(x, mem, g_mix, w_in, b_gate, b_forget, qn_swa, kn_swa, sink_swa, rel_bias, qn_fox, kn_fox, g_mem, w_mem_kv, qn_mem, kn_mem, w_o_swa, w_o_fox, w_o_mem, w_out, g_mlp, w_mlp_up, w_mlp_down, loss_target, m_g_mix, m_w_in, m_b_gate, m_b_forget, m_qn_swa, m_kn_swa, m_sink_swa, m_rel_bias, m_qn_fox, m_kn_fox, m_g_mem, m_w_mem_kv, m_qn_mem, m_kn_mem, m_w_o_swa, m_w_o_fox, m_w_o_mem, m_w_out, m_g_mlp, m_w_mlp_up, m_w_mlp_down, v_g_mix, v_w_in, v_b_gate, v_b_forget, v_qn_swa, v_kn_swa, v_sink_swa, v_rel_bias, v_qn_fox, v_kn_fox, v_g_mem, v_w_mem_kv, v_qn_mem, v_kn_mem, v_w_o_swa, v_w_o_fox, v_w_o_mem, v_w_out, v_g_mlp, v_w_mlp_up, v_w_mlp_down):
    wts = dict(g_mix=g_mix, w_in=w_in, b_gate=b_gate, b_forget=b_forget, qn_swa=qn_swa, kn_swa=kn_swa, sink_swa=sink_swa,
               rel_bias=rel_bias, qn_fox=qn_fox, kn_fox=kn_fox, g_mem=g_mem, w_mem_kv=w_mem_kv, qn_mem=qn_mem, kn_mem=kn_mem,
               w_o_swa=w_o_swa, w_o_fox=w_o_fox, w_o_mem=w_o_mem, w_out=w_out, g_mlp=g_mlp, w_mlp_up=w_mlp_up,
               w_mlp_down=w_mlp_down)
    mom = dict(g_mix=m_g_mix, w_in=m_w_in, b_gate=m_b_gate, b_forget=m_b_forget, qn_swa=m_qn_swa, kn_swa=m_kn_swa,
               sink_swa=m_sink_swa, rel_bias=m_rel_bias, qn_fox=m_qn_fox, kn_fox=m_kn_fox, g_mem=m_g_mem, w_mem_kv=m_w_mem_kv,
               qn_mem=m_qn_mem, kn_mem=m_kn_mem, w_o_swa=m_w_o_swa, w_o_fox=m_w_o_fox, w_o_mem=m_w_o_mem, w_out=m_w_out,
               g_mlp=m_g_mlp, w_mlp_up=m_w_mlp_up, w_mlp_down=m_w_mlp_down)
    var = dict(g_mix=v_g_mix, w_in=v_w_in, b_gate=v_b_gate, b_forget=v_b_forget, qn_swa=v_qn_swa, kn_swa=v_kn_swa,
               sink_swa=v_sink_swa, rel_bias=v_rel_bias, qn_fox=v_qn_fox, kn_fox=v_kn_fox, g_mem=v_g_mem, w_mem_kv=v_w_mem_kv,
               qn_mem=v_qn_mem, kn_mem=v_kn_mem, w_o_swa=v_w_o_swa, w_o_fox=v_w_o_fox, w_o_mem=v_w_o_mem, w_out=v_w_out,
               g_mlp=v_g_mlp, w_mlp_up=v_w_mlp_up, w_mlp_down=v_w_mlp_down)

    shards = _cast_shards({n: wts[n][0] for n in BIG})
    first = _split_start([shards["w_in"]], True, "ag_start_w_in", peers=(1,) + SAME_CORE)
    rest = _split_start([shards[n] for n in BIG[1:]], True, "ag_start_rest", after=first[4])
    full = {}

    def getw(n, after):
        if n == "w_in" and n not in full:
            forwarded = _forward_start(first, after, "ag_forward_w_in")
            full[n] = _w_in_from_shards(_forward_wait(first, forwarded, "ag_wait_w_in"))
        elif n not in full:
            land = _split_wait(rest, BIG[1:].index(n), after, True, "ag_wait_" + n)
            full[n] = land if n == "w_mlp_up" else _gathered_to_full(n, land)
        return full[n]

    exchanges = {}

    def emit(grads_by_name):
        parts = []
        for n, grad in grads_by_name.items():
            if n == "w_in":
                parts.append(_dw_in_to_parts(grad))
            else:
                parts.append(grad if n == "w_mlp_up" else _full_to_parts(n, grad, wts[n].shape[2]))
        started = _split_start(parts, False, "rs_start_" + next(iter(grads_by_name)))
        for w, n in enumerate(grads_by_name):
            exchanges[n] = (started, w)
        return started[4]

    small_p = {n: wts[n] for n in SMALL}
    loss, grad_x, small_g = _local_step(x[0], mem[0], loss_target[0], small_p, getw, emit, (first[4], rest[4]))

    packed = _pack_small(small_g, loss)
    small_gather = _split_start([packed], True, "ag_start_small")

    grads, delta, new_m, new_v = {}, {}, {}, {}

    def update(n, after):
        land = _split_wait(*exchanges[n], after, False, "rs_wait_" + n)
        g, d, m2, v2 = _adamw(land, wts[n][0], mom[n][0], var[n][0], "adamw_" + n)
        grads[n], delta[n], new_m[n], new_v[n] = g[None], d[None], m2[None], v2[None]
        return d

    after = small_gather[4]
    for n in exchanges:
        if n != "w_in":
            after = update(n, after)

    gathered = _split_wait(small_gather, 0, after, True, "ag_wait_small")
    small_out, total = _adamw_small(gathered, small_p, mom, var)
    for name, (g, d, m2, v2) in small_out.items():
        grads[name], delta[name], new_m[name], new_v[name] = g, d, m2, v2
    update("w_in", total)

    return (total[0, 0], grad_x[None], *[grads[n] for n in WEIGHTS], *[delta[n] for n in WEIGHTS],
            *[new_m[n] for n in WEIGHTS], *[new_v[n] for n in WEIGHTS])
```

```python
import math

import jax
import jax.numpy as jnp
from jax import lax
from jax.experimental import pallas as pl
from jax.experimental.pallas import tpu as pltpu

F32 = jnp.float32
BF16 = jnp.bfloat16

D_MODEL = 1024
N_MEM = 256
D_FF = 4096
HEAD = 64
SWA_HEADS = 8
SWA_BLOCK = 128
MEM_HEADS = 4
MEM_HEAD = 128
EPS = 1e-6
NEG = -1e30
REL_BUCKETS = 32
REL_MAX_DIST = 128

ADAM_LR = 0.001
ADAM_B1 = 0.9
ADAM_B2 = 0.999
ADAM_EPS = 1e-08
ADAM_WD = 0.01
ADAM_STEP = 10

GL0, QF0, KF0, VF0, QM0, QA0, KA0, VA0, FL0 = 0, 3072, 3584, 4096, 4608, 5120, 5632, 5760, 5888
PROJ_W = 6144
HALF_W = 3072
H_QF, H_KF, H_VF, H_QM, H_QA, H_KA, H_VA, H_FL = 0, 512, 1024, 1536, 2048, 2560, 2688, 2816

VMEM_LIMIT = 56 * 1024 * 1024
N_DEV = 8
MESH = pl.DeviceIdType.MESH

NN = (((1,), (0,)), ((), ()))
NT = (((1,), (1,)), ((), ()))
TN = (((0,), (0,)), ((), ()))


def _dot(a, b, dims=NN):
    return lax.dot_general(a, b, dims, preferred_element_type=F32)


def _params(sem):
    return pltpu.CompilerParams(dimension_semantics=sem, vmem_limit_bytes=VMEM_LIMIT)


def _full(shape):
    nd = len(shape)
    return pl.BlockSpec(shape, lambda *_: (0,) * nd)


def _sigmoid(z):
    return 1.0 / (1.0 + jnp.exp(-z))


def _group_mean(v, hd):
    if hd == 128:
        return jnp.mean(v, axis=-1, keepdims=True)
    lane = lax.broadcasted_iota(jnp.int32, v.shape, 1)
    lo = lane < HEAD
    s_lo = jnp.sum(jnp.where(lo, v, 0.0), axis=-1, keepdims=True)
    s_hi = jnp.sum(jnp.where(lo, 0.0, v), axis=-1, keepdims=True)
    return jnp.where(lo, s_lo, s_hi) * (1.0 / HEAD)


def _mm(a, b, mode, out_dtype, tm, tn, tk, name, column_chunks=False):
    if mode == "nn":
        m, k = a.shape
        n = b.shape[1]
    elif mode == "nt":
        m, k = a.shape
        n = b.shape[0]
    else:
        k, m = a.shape
        n = b.shape[1]
    tm, tn, tk = min(tm, m), min(tn, n), min(tk, k)
    nk = k // tk
    chunk = n // N_DEV
    per_tile = tn // chunk if column_chunks else 1
    dims = {"nn": NN, "nt": NT, "tn": TN}[mode]
    a_spec = pl.BlockSpec((tk, tm), lambda j, i, kk: (kk, i)) if mode == "tn" else pl.BlockSpec((tm, tk), lambda j, i, kk: (i, kk))
    b_spec = pl.BlockSpec((tn, tk), lambda j, i, kk: (j, kk)) if mode == "nt" else pl.BlockSpec((tk, tn), lambda j, i, kk: (kk, j))

    def body(a_ref, b_ref, o_ref, *acc):
        prod = _dot(a_ref[...].astype(BF16), b_ref[...].astype(BF16), dims)

        def write(res):
            if column_chunks:
                for c in range(per_tile):
                    o_ref[c] = res[:, c * chunk:(c + 1) * chunk].astype(o_ref.dtype)
            else:
                o_ref[...] = res.astype(o_ref.dtype)

        if nk == 1:
            write(prod)
        else:
            acc_ref, = acc
            kk = pl.program_id(2)

            @pl.when(kk == 0)
            def _():
                acc_ref[...] = prod

            @pl.when(kk > 0)
            def _():
                acc_ref[...] += prod

            @pl.when(kk == nk - 1)
            def _():
                write(acc_ref[...])

    return pl.pallas_call(
        body, name=name, grid=(n // tn, m // tm, nk),
        in_specs=[a_spec, b_spec],
        out_specs=(pl.BlockSpec((per_tile, tm, chunk), lambda j, i, kk: (j, i, 0)) if column_chunks
                   else pl.BlockSpec((tm, tn), lambda j, i, kk: (i, j))),
        out_shape=jax.ShapeDtypeStruct((N_DEV, m, chunk) if column_chunks else (m, n), out_dtype),
        scratch_shapes=[pltpu.VMEM((tm, tn), F32)] if nk > 1 else [],
        compiler_params=_params(("parallel", "parallel", "arbitrary")),
    )(a, b)


def _rms_fwd(x, g, name, deps=()):
    s, d = x.shape
    tm = min(512, s)

    def body(x_ref, g_ref, *rest):
        h_ref = rest[len(deps)]
        xv = x_ref[...]
        r = lax.rsqrt(jnp.mean(xv * xv, axis=-1, keepdims=True) + EPS)
        h_ref[...] = (xv * r * g_ref[...]).astype(BF16)

    return pl.pallas_call(
        body, name=name, grid=(s // tm,),
        in_specs=[pl.BlockSpec((tm, d), lambda i: (i, 0)), _full((1, d))] + [pl.BlockSpec(memory_space=pl.ANY)] * len(deps),
        out_specs=pl.BlockSpec((tm, d), lambda i: (i, 0)),
        out_shape=jax.ShapeDtypeStruct((s, d), BF16),
        compiler_params=_params(("parallel",)),
    )(x, g, *deps)


def _proj_post(proj, gq_fox, gk_fox, gq_mem, gq_swa, gk_swa):
    s = proj.shape[0]
    tm = min(256, s)

    def body(p_ref, gqf, gkf, gqm, gqa, gka, qf_ref, kf_ref, vf_ref, qm_ref, qa_ref, ka_ref, va_ref, qft_ref, vft_ref):
        def norm(off, width, hd, g_ref, o_ref, scaled_t_ref=None):
            for b in range(width // 128):
                v = p_ref[:, off + b * 128: off + (b + 1) * 128].astype(F32)
                r = lax.rsqrt(_group_mean(v * v, hd) + EPS)
                vn = (v * r * g_ref[...]).astype(BF16)
                o_ref[:, b * 128:(b + 1) * 128] = vn
                if scaled_t_ref is not None:
                    scaled_t_ref[b * 128:(b + 1) * 128, :] = (vn.astype(F32) * 0.125).T.astype(BF16)

        norm(H_QF, 512, HEAD, gqf, qf_ref, qft_ref)
        norm(H_KF, 512, HEAD, gkf, kf_ref)
        vf_ref[...] = p_ref[:, H_VF:H_VF + 512].astype(BF16)
        for b in range(4):
            vft_ref[b * 128:(b + 1) * 128, :] = p_ref[:, H_VF + b * 128:H_VF + (b + 1) * 128].astype(F32).T.astype(BF16)
        norm(H_QM, 512, MEM_HEAD, gqm, qm_ref)
        norm(H_QA, 512, HEAD, gqa, qa_ref)
        norm(H_KA, 128, HEAD, gka, ka_ref)
        va_ref[...] = p_ref[:, H_VA:H_VA + 128].astype(BF16)

    g_spec = _full((1, 128))
    o512 = pl.BlockSpec((tm, 512), lambda i: (i, 0))
    o128 = pl.BlockSpec((tm, 128), lambda i: (i, 0))
    s512 = jax.ShapeDtypeStruct((s, 512), BF16)
    s128 = jax.ShapeDtypeStruct((s, 128), BF16)
    return pl.pallas_call(
        body, name="proj_post", grid=(s // tm,),
        in_specs=[pl.BlockSpec((tm, HALF_W), lambda i: (i, 1)), g_spec, g_spec, g_spec, g_spec, g_spec],
        out_specs=[o512, o512, o512, o512, o512, o128, o128] + [pl.BlockSpec((512, tm), lambda i: (0, i))] * 2,
        out_shape=[s512, s512, s512, s512, s512, s128, s128] + [jax.ShapeDtypeStruct((512, s), BF16)] * 2,
        compiler_params=_params(("parallel",)),
    )(proj, gq_fox, gk_fox, gq_mem, gq_swa, gk_swa)


def _tri(n, lower):
    r = lax.broadcasted_iota(jnp.int32, (n, n), 0)
    c = lax.broadcasted_iota(jnp.int32, (n, n), 1)
    return jnp.where((c <= r) if lower else (c >= r), 1.0, 0.0).astype(F32)


def _fox_gate_fwd(proj, b_forget128):
    s = proj.shape[0]
    tm = min(512, s)

    def body(p_ref, b_ref, cc_ref, ca_ref, carry_ref):
        i = pl.program_id(0)

        @pl.when(i == 0)
        def _():
            carry_ref[...] = jnp.zeros_like(carry_ref)

        z = p_ref[...] + b_ref[...]
        logf = jnp.minimum(z, 0.0) - jnp.log(1.0 + jnp.exp(-jnp.abs(z)))
        c = jnp.dot(_tri(tm, True), logf, precision=lax.Precision.HIGHEST, preferred_element_type=F32) + carry_ref[...]
        carry_ref[...] = c[tm - 1:tm, :]
        lane = lax.broadcasted_iota(jnp.int32, (tm, 128), 1)
        for hp in range(4):
            cc_ref[hp] = c if hp == 0 else pltpu.roll(c, 128 - 2 * hp, 1)
            aug = jnp.zeros((tm, 128), F32)
            for e in range(2):
                rest = jnp.broadcast_to(c[:, 2 * hp + e:2 * hp + e + 1], (tm, 128))
                for part in range(3):
                    piece = rest.astype(BF16).astype(F32)
                    aug = jnp.where(lane == HEAD * (1 - e) + part, piece, aug)
                    rest = rest - piece
            ca_ref[hp] = aug.astype(BF16)

    return pl.pallas_call(
        body, name="fox_gate_fwd", grid=(s // tm,),
        in_specs=[pl.BlockSpec((tm, 128), lambda i: (i, 0)), _full((1, 128))],
        out_specs=[pl.BlockSpec((4, tm, 128), lambda i: (0, i, 0))] * 2,
        out_shape=[jax.ShapeDtypeStruct((4, s, 128), F32), jax.ShapeDtypeStruct((4, s, 128), BF16)],
        scratch_shapes=[pltpu.VMEM((1, 128), F32)],
        compiler_params=_params(("arbitrary",)),
    )(proj, b_forget128)


def _memkv_fwd(mem, g_mem, w_kv, kn_mem):
    m = mem.shape[0]

    def body(mem_ref, g_ref, w_ref, kn_ref, memn_ref, kv_ref, mk_ref, mv_ref):
        xv = mem_ref[...]
        r = lax.rsqrt(jnp.mean(xv * xv, axis=-1, keepdims=True) + EPS)
        mn = (xv * r * g_ref[...]).astype(BF16)
        memn_ref[...] = mn
        kv = _dot(mn, w_ref[...])
        kv_ref[...] = kv
        for h in range(MEM_HEADS):
            v = kv[:, h * 128:(h + 1) * 128]
            rr = lax.rsqrt(jnp.mean(v * v, axis=-1, keepdims=True) + EPS)
            mk_ref[:, h * 128:(h + 1) * 128] = (v * rr * kn_ref[...]).astype(BF16)
        mv_ref[...] = kv[:, 512:1024].astype(BF16)

    return pl.pallas_call(
        body, name="memkv_fwd",
        out_shape=[jax.ShapeDtypeStruct((m, D_MODEL), BF16), jax.ShapeDtypeStruct((m, 1024), F32),
                   jax.ShapeDtypeStruct((m, 512), BF16), jax.ShapeDtypeStruct((m, 512), BF16)],
        compiler_params=pltpu.CompilerParams(vmem_limit_bytes=VMEM_LIMIT),
    )(mem, g_mem, w_kv, kn_mem)


def _bias_table(rel_bias, bucket):
    def body(rb_ref, bk_ref, o_ref):
        bk = bk_ref[...]
        for h in range(SWA_HEADS):
            acc = jnp.zeros(bk.shape, F32)
            for b in range(REL_BUCKETS):
                acc = jnp.where(bk == b, rb_ref[b, h], acc)
            o_ref[h] = acc

    return pl.pallas_call(
        body, name="bias_table",
        in_specs=[pl.BlockSpec(memory_space=pltpu.SMEM), pl.BlockSpec(memory_space=pltpu.VMEM)],
        out_shape=jax.ShapeDtypeStruct((SWA_HEADS,) + bucket.shape, F32),
    )(rel_bias, bucket)


def _swa_valid(n):
    row = lax.broadcasted_iota(jnp.int32, (SWA_BLOCK, 2 * SWA_BLOCK), 0)
    col = lax.broadcasted_iota(jnp.int32, (SWA_BLOCK, 2 * SWA_BLOCK), 1)
    dist = row + SWA_BLOCK - col
    return (dist >= 0) & (dist < SWA_BLOCK) & ((col >= SWA_BLOCK) | (n > 0))


def _swa_fwd(qa, kp, vp, bias, sink):
    s = qa.shape[0]
    nb = s // SWA_BLOCK

    def body(sink_ref, q_ref, kp_ref, vp_ref, bias_ref, o_ref):
        n = pl.program_id(0)
        start = pl.multiple_of(n * SWA_BLOCK, SWA_BLOCK)
        k2 = kp_ref[pl.ds(start, 2 * SWA_BLOCK), :]
        v2 = vp_ref[pl.ds(start, 2 * SWA_BLOCK), :]
        valid = _swa_valid(n)
        heads = range(SWA_HEADS)
        hs = lambda h: slice(h * HEAD, (h + 1) * HEAD)
        sc = [jnp.where(valid, _dot(q_ref[:, hs(h)], k2[:, hs(h // 4)], NT) * 0.125 + bias_ref[h], NEG) for h in heads]
        pn = []
        for h in heads:
            sk = sink_ref[h]
            mx = jnp.maximum(jnp.max(sc[h], axis=-1, keepdims=True), sk)
            p = jnp.exp(sc[h] - mx)
            den = jnp.sum(p, axis=-1, keepdims=True) + jnp.exp(sk - mx)
            pn.append((p / den).astype(BF16))
        outs = [_dot(pn[h], v2[:, hs(h // 4)]).astype(BF16) for h in heads]
        for h in heads:
            o_ref[:, hs(h)] = outs[h]

    return pl.pallas_call(
        body, name="swa_fwd", grid=(nb,),
        in_specs=[pl.BlockSpec(memory_space=pltpu.SMEM),
                  pl.BlockSpec((SWA_BLOCK, 512), lambda n: (n, 0)),
                  _full(kp.shape), _full(vp.shape), _full(bias.shape)],
        out_specs=pl.BlockSpec((SWA_BLOCK, 512), lambda n: (n, 0)),
        out_shape=jax.ShapeDtypeStruct((s, 512), BF16),
        compiler_params=_params(("parallel",)),
    )(sink, qa, kp, vp, bias)


def _head_mask(e):
    lane = lax.broadcasted_iota(jnp.int32, (1, 128), 1)
    return (lane >= e * HEAD) & (lane < (e + 1) * HEAD)


FOX_FWD_T = 1024
FOX_BWD_T = 512


def _head_rows(e):
    row = lax.broadcasted_iota(jnp.int32, (128, 1), 0)
    return (row >= e * HEAD) & (row < (e + 1) * HEAD)


def _fox_fwd(q, k, v_t, ca4):
    s = q.shape[0]
    t = min(FOX_FWD_T, s)
    nq = s // t

    def body(q_ref, k_ref, vt_ref, ca_ref, o_ref, lse_ref, ot_ref):
        i = pl.program_id(1)
        qs = q_ref[...] * jnp.asarray(0.125, BF16)
        lane = lax.broadcasted_iota(jnp.int32, (1, 128), 1)
        minus = [jnp.where((lane >= HEAD * (1 - e)) & (lane < HEAD * (1 - e) + 3), -1.0, 0.0).astype(BF16) for e in range(2)]
        qe = [jnp.where(_head_mask(e), qs, jnp.broadcast_to(minus[e], qs.shape)) for e in range(2)]

        def block(carry, key0, nkeys, q0, nqs, masked):
            ks = pl.ds(pl.multiple_of(key0, 128), nkeys)
            kj = k_ref[ks, :]
            caj = ca_ref[0, ks, :]
            vtj = vt_ref[:, ks]
            out = []
            for e in range(2):
                m_all, acc_all = carry[2 * e], carry[2 * e + 1]
                m, acc = m_all[:, q0:q0 + nqs], acc_all[:, q0:q0 + nqs]
                st = _dot(jnp.where(_head_mask(e), kj, caj), qe[e][q0:q0 + nqs, :], NT)
                if masked:
                    krow = lax.broadcasted_iota(jnp.int32, (nkeys, nqs), 0) + key0
                    qcol = lax.broadcasted_iota(jnp.int32, (nkeys, nqs), 1) + (i * t + q0)
                    st = jnp.where(krow <= qcol, st, NEG)
                m_new = jnp.maximum(m, jnp.max(st, axis=0, keepdims=True))
                alpha = jnp.exp(m - m_new)
                pt = jnp.exp(st - m_new).astype(BF16)
                vte = jnp.where(_head_rows(e), vtj, jnp.ones_like(vtj))
                acc_new = alpha * acc + _dot(vte, pt)
                if nqs < t:
                    m_new = jnp.concatenate([m_all[:, :q0], m_new], axis=1)
                    acc_new = jnp.concatenate([acc_all[:, :q0], acc_new], axis=1)
                out += [m_new, acc_new]
            return tuple(out)

        half = t // 2
        init = (jnp.full((1, t), NEG, F32), jnp.zeros((128, t), F32)) * 2
        carry = lax.fori_loop(0, i, lambda j, c: block(c, j * t, t, 0, t, False), init)
        carry = block(carry, i * t, half, 0, t, True)
        m0, a0, m1, a1 = block(carry, i * t + half, half, half, half, True)
        l0 = a0[HEAD:HEAD + 1, :]
        l1 = a1[0:1, :]
        o_t = jnp.where(_head_rows(0), a0 / l0, a1 / l1)
        o_ref[...] = o_t.T.astype(BF16)
        ot_ref[...] = o_t.astype(BF16)
        r8 = lax.broadcasted_iota(jnp.int32, (8, t), 0)
        lse_ref[0] = jnp.where(r8 == 0, m0 + jnp.log(l0), jnp.where(r8 == 1, m1 + jnp.log(l1), 0.0))

    return pl.pallas_call(
        body, name="fox_fwd", grid=(4, nq),
        in_specs=[pl.BlockSpec((t, 128), lambda hp, i: (i, hp)),
                  pl.BlockSpec((s, 128), lambda hp, i: (0, hp)),
                  pl.BlockSpec((128, s), lambda hp, i: (hp, 0)),
                  pl.BlockSpec((1, s, 128), lambda hp, i: (hp, 0, 0))],
        out_specs=[pl.BlockSpec((t, 128), lambda hp, i: (i, hp)),
                   pl.BlockSpec((1, 8, t), lambda hp, i: (hp, 0, i)),
                   pl.BlockSpec((128, t), lambda hp, i: (hp, i))],
        out_shape=[jax.ShapeDtypeStruct((s, 512), BF16), jax.ShapeDtypeStruct((4, 8, s), F32),
                   jax.ShapeDtypeStruct((512, s), BF16)],
        compiler_params=_params(("parallel", "parallel")),
    )(q, k, v_t, ca4)


MEM_SCALE = MEM_HEAD ** -0.5


def _mem_fwd(qm, mk, mv):
    s = qm.shape[0]
    tq = min(512, s)

    def body(q_ref, mk_ref, mv_ref, o_ref):
        for h in range(MEM_HEADS):
            hs = slice(h * 128, (h + 1) * 128)
            sc = _dot(q_ref[:, hs], mk_ref[:, hs], NT) * MEM_SCALE
            mx = jnp.max(sc, axis=-1, keepdims=True)
            p = jnp.exp(sc - mx)
            p = p / jnp.sum(p, axis=-1, keepdims=True)
            o_ref[:, hs] = _dot(p.astype(BF16), mv_ref[:, hs]).astype(BF16)

    return pl.pallas_call(
        body, name="mem_fwd", grid=(s // tq,),
        in_specs=[pl.BlockSpec((tq, 512), lambda i: (i, 0)), _full(mk.shape), _full(mv.shape)],
        out_specs=pl.BlockSpec((tq, 512), lambda i: (i, 0)),
        out_shape=jax.ShapeDtypeStruct((s, 512), BF16),
        compiler_params=_params(("parallel",)),
    )(qm, mk, mv)


def _merge_fwd(x, oa, of, om, proj, b_gate, wa, wf, wm, w_out, g_mlp):
    s = x.shape[0]
    tm = min(256, s)

    def body(x_ref, oa_ref, of_ref, om_ref, gl_ref, bg_ref, wa_ref, wf_ref, wm_ref, wo_ref, g_ref, x1_ref, hm_ref, mg_ref):
        merged = None
        for b, (o_ref, w_ref) in enumerate(((oa_ref, wa_ref), (of_ref, wf_ref), (om_ref, wm_ref))):
            cs = slice(b * D_MODEL, (b + 1) * D_MODEL)
            y = _dot(o_ref[...], w_ref[...])
            t = _sigmoid(gl_ref[:, cs].astype(F32) + bg_ref[:, cs]) * y
            merged = t if merged is None else merged + t
        mb = merged.astype(BF16)
        mg_ref[...] = mb
        x1 = x_ref[...] + _dot(mb, wo_ref[...])
        x1_ref[...] = x1
        r = lax.rsqrt(jnp.mean(x1 * x1, axis=-1, keepdims=True) + EPS)
        hm_ref[...] = (x1 * r * g_ref[...]).astype(BF16)

    row = lambda w: pl.BlockSpec((tm, w), lambda i: (i, 0))
    return pl.pallas_call(
        body, name="merge_fwd", grid=(s // tm,),
        in_specs=[row(D_MODEL), row(512), row(512), row(512), row(HALF_W), _full((1, HALF_W)),
                  _full(wa.shape), _full(wf.shape), _full(wm.shape), _full(w_out.shape), _full((1, D_MODEL))],
        out_specs=[row(D_MODEL), row(D_MODEL), row(D_MODEL)],
        out_shape=[jax.ShapeDtypeStruct((s, D_MODEL), F32), jax.ShapeDtypeStruct((s, D_MODEL), BF16),
                   jax.ShapeDtypeStruct((s, D_MODEL), BF16)],
        compiler_params=_params(("parallel",)),
    )(x, oa, of, om, proj, b_gate, wa, wf, wm, w_out, g_mlp)


def _mlp_up(hm, w_up):
    s = hm.shape[0]
    tm, tn = min(1024, s), w_up.shape[2]

    def body(h_ref, w_ref, u_ref):
        r = jnp.maximum(_dot(h_ref[...], w_ref[0]), 0.0)
        u_ref[...] = (r * r).astype(BF16)

    return pl.pallas_call(
        body, name="mlp_up", grid=(s // tm, D_FF // tn),
        in_specs=[pl.BlockSpec((tm, D_MODEL), lambda i, j: (i, 0)), pl.BlockSpec((1, D_MODEL, tn), lambda i, j: (j, 0, 0))],
        out_specs=pl.BlockSpec((tm, tn), lambda i, j: (i, j)),
        out_shape=jax.ShapeDtypeStruct((s, D_FF), BF16),
        compiler_params=_params(("parallel", "parallel")),
    )(hm, w_up)


def _mlp_down_loss(u, w_down, x1, target):
    s = u.shape[0]
    tm = min(256, s)

    def body(u_ref, w_ref, x1_ref, t_ref, dy_ref, dyb_ref, loss_ref):
        i = pl.program_id(0)

        @pl.when(i == 0)
        def _():
            loss_ref[...] = jnp.zeros_like(loss_ref)

        y = x1_ref[...] + _dot(u_ref[...], w_ref[...])
        err = y - t_ref[...]
        dy = err * (1.0 / D_MODEL)
        dy_ref[...] = dy
        dyb_ref[...] = dy.astype(BF16)
        part = jnp.sum(jnp.sum(err * err, axis=-1, keepdims=True) * (1.0 / D_MODEL), axis=0, keepdims=True)
        loss_ref[...] += 0.5 * part

    row = pl.BlockSpec((tm, D_MODEL), lambda i: (i, 0))
    return pl.pallas_call(
        body, name="mlp_down_loss", grid=(s // tm,),
        in_specs=[pl.BlockSpec((tm, D_FF), lambda i: (i, 0)), _full(w_down.shape), row, row],
        out_specs=[row, row, _full((1, 1))],
        out_shape=[jax.ShapeDtypeStruct((s, D_MODEL), F32), jax.ShapeDtypeStruct((s, D_MODEL), BF16),
                   jax.ShapeDtypeStruct((1, 1), F32)],
        compiler_params=_params(("arbitrary",)),
    )(u, w_down, x1, target)


def _mlp_bwd_act(dy, w_down, u):
    s = dy.shape[0]
    tm, tn = min(1024, s), 1024

    def body(dy_ref, w_ref, u_ref, da_ref):
        du = _dot(dy_ref[...], w_ref[...], NT)
        da_ref[...] = (du * (2.0 * jnp.sqrt(u_ref[...].astype(F32)))).astype(BF16)

    return pl.pallas_call(
        body, name="mlp_bwd_act", grid=(D_FF // tn, s // tm),
        in_specs=[pl.BlockSpec((tm, D_MODEL), lambda j, i: (i, 0)), pl.BlockSpec((tn, D_MODEL), lambda j, i: (j, 0)),
                  pl.BlockSpec((tm, tn), lambda j, i: (i, j))],
        out_specs=pl.BlockSpec((tm, tn), lambda j, i: (i, j)),
        out_shape=jax.ShapeDtypeStruct((s, D_FF), BF16),
        compiler_params=_params(("parallel", "parallel")),
    )(dy, w_down, u)


def _rms_bwd(xv, g, dh, skip):
    r = lax.rsqrt(jnp.mean(xv * xv, axis=-1, keepdims=True) + EPS)
    n = xv * r
    dn = dh * g
    dx = skip + r * (dn - n * jnp.mean(dn * n, axis=-1, keepdims=True))
    return dx, jnp.sum(dh * n, axis=0, keepdims=True)


def _mlp_bwd_x(da, w_up, x1, dy, g_mlp):
    s = da.shape[0]
    tm = min(256, s)

    def body(da_ref, w_ref, x1_ref, dy_ref, g_ref, dx1_ref, dg_ref):
        i = pl.program_id(0)

        @pl.when(i == 0)
        def _():
            dg_ref[...] = jnp.zeros_like(dg_ref)

        tn = w_ref.shape[2]
        dhm = _dot(da_ref[:, 0:tn], w_ref[0], NT)
        for j in range(1, N_DEV):
            dhm = dhm + _dot(da_ref[:, j * tn:(j + 1) * tn], w_ref[j], NT)
        dx, dg = _rms_bwd(x1_ref[...], g_ref[...], dhm, dy_ref[...])
        dx1_ref[...] = dx
        dg_ref[...] += dg

    row = pl.BlockSpec((tm, D_MODEL), lambda i: (i, 0))
    return pl.pallas_call(
        body, name="mlp_bwd_x", grid=(s // tm,),
        in_specs=[pl.BlockSpec((tm, D_FF), lambda i: (i, 0)), _full(w_up.shape), row, row, _full((1, D_MODEL))],
        out_specs=[row, _full((1, D_MODEL))],
        out_shape=[jax.ShapeDtypeStruct((s, D_MODEL), F32), jax.ShapeDtypeStruct((1, D_MODEL), F32)],
        compiler_params=_params(("arbitrary",)),
    )(da, w_up, x1, dy, g_mlp)


def _merge_bwd(dx1, oa, of, om, proj, b_gate, wa, wf, wm, w_out):
    s = dx1.shape[0]
    tm = min(256, s)

    def body(dx1_ref, oa_ref, of_ref, om_ref, gl_ref, bg_ref, wa_ref, wf_ref, wm_ref, wo_ref,
             dp_ref, doa_ref, dof_ref, dom_ref, dya_ref, dyf_ref, dym_ref, dbg_ref):
        i = pl.program_id(0)

        @pl.when(i == 0)
        def _():
            dbg_ref[...] = jnp.zeros_like(dbg_ref)

        dmerged = _dot(dx1_ref[...].astype(BF16), wo_ref[...], NT)
        branches = ((oa_ref, wa_ref, doa_ref, dya_ref), (of_ref, wf_ref, dof_ref, dyf_ref), (om_ref, wm_ref, dom_ref, dym_ref))
        for b, (o_ref, w_ref, do_ref, dyb_ref) in enumerate(branches):
            cs = slice(b * D_MODEL, (b + 1) * D_MODEL)
            y = _dot(o_ref[...], w_ref[...])
            g = _sigmoid(gl_ref[:, cs].astype(F32) + bg_ref[:, cs])
            dz = (dmerged * y) * g * (1.0 - g)
            dp_ref[:, cs] = dz.astype(BF16)
            dbg_ref[:, cs] += jnp.sum(dz, axis=0, keepdims=True)
            dyb = (dmerged * g).astype(BF16)
            dyb_ref[...] = dyb
            do = _dot(dyb, w_ref[...], NT)
            do_ref[...] = (do.T if b == 1 else do).astype(BF16)

    row = lambda w: pl.BlockSpec((tm, w), lambda i: (i, 0))
    sd = lambda w: jax.ShapeDtypeStruct((s, w), BF16)
    return pl.pallas_call(
        body, name="merge_bwd", grid=(s // tm,),
        in_specs=[row(D_MODEL), row(512), row(512), row(512), row(HALF_W), _full((1, HALF_W)),
                  _full(wa.shape), _full(wf.shape), _full(wm.shape), _full(w_out.shape)],
        out_specs=[row(HALF_W), row(512), pl.BlockSpec((512, tm), lambda i: (0, i)), row(512),
                   row(D_MODEL), row(D_MODEL), row(D_MODEL), _full((1, HALF_W))],
        out_shape=[sd(PROJ_W), sd(512), jax.ShapeDtypeStruct((512, s), BF16), sd(512), sd(D_MODEL), sd(D_MODEL), sd(D_MODEL),
                   jax.ShapeDtypeStruct((1, HALF_W), F32)],
        compiler_params=_params(("arbitrary",)),
    )(dx1, oa, of, om, proj, b_gate, wa, wf, wm, w_out)


def _swa_valid_t(n):
    key = lax.broadcasted_iota(jnp.int32, (2 * SWA_BLOCK, SWA_BLOCK), 0)
    qry = lax.broadcasted_iota(jnp.int32, (2 * SWA_BLOCK, SWA_BLOCK), 1)
    dist = qry + SWA_BLOCK - key
    return (dist >= 0) & (dist < SWA_BLOCK) & ((key >= SWA_BLOCK) | (n > 0))


def _swa_bwd(qa, kp, vp, bias_t, sink, doa):
    s = qa.shape[0]
    nb = s // SWA_BLOCK

    def body(sink_ref, q_ref, kp_ref, vp_ref, bias_ref, do_ref, dq_ref, dkp_ref, dvp_ref, dbias_ref, dsink_ref, sk_acc):
        n = pl.program_id(0)

        @pl.when(n == 0)
        def _():
            dkp_ref[...] = jnp.zeros_like(dkp_ref)
            dvp_ref[...] = jnp.zeros_like(dvp_ref)
            dbias_ref[...] = jnp.zeros_like(dbias_ref)
            sk_acc[...] = jnp.zeros_like(sk_acc)

        start = pl.multiple_of(n * SWA_BLOCK, SWA_BLOCK)
        win = pl.ds(start, 2 * SWA_BLOCK)
        k2 = kp_ref[win, :]
        v2 = vp_ref[win, :]
        valid = _swa_valid_t(n)
        heads = range(SWA_HEADS)
        hs = lambda h: slice(h * HEAD, (h + 1) * HEAD)
        scale = jnp.asarray(0.125, BF16)
        q = [q_ref[:, hs(h)] for h in heads]
        do = [do_ref[:, hs(h)] for h in heads]
        kk = [k2[:, hs(kv)] for kv in range(2)]
        vv = [v2[:, hs(kv)] for kv in range(2)]
        kt = [(kk[kv].astype(F32) * 0.125).T.astype(BF16) for kv in range(2)]
        st = [jnp.where(valid, _dot(kk[h // 4], q[h], NT) * 0.125 + bias_ref[h], NEG) for h in heads]
        dpt = [_dot(vv[h // 4], do[h], NT) for h in heads]
        pt, dst = [], []
        for h in heads:
            sk = sink_ref[h]
            mx = jnp.maximum(jnp.max(st[h], axis=0, keepdims=True), sk)
            p = jnp.exp(st[h] - mx)
            esk = jnp.exp(sk - mx)
            den = jnp.sum(p, axis=0, keepdims=True) + esk
            p = p / den
            delta = jnp.sum(p * dpt[h], axis=0, keepdims=True)
            d = p * (dpt[h] - delta)
            sk_acc[h:h + 1, :] += -(esk / den) * delta
            dbias_ref[h] += d
            pt.append(p.astype(BF16))
            dst.append(d.astype(BF16))
        dq_t = [_dot(kt[h // 4], dst[h]) for h in heads]
        dq_ref[...] = jnp.concatenate(dq_t, axis=0).T.astype(BF16)
        for kv in range(2):
            group = range(4 * kv, 4 * kv + 4)
            dk = [_dot(dst[h], q[h] * scale) for h in group]
            dv = [_dot(pt[h], do[h]) for h in group]
            dkp_ref[win, hs(kv)] += (dk[0] + dk[1]) + (dk[2] + dk[3])
            dvp_ref[win, hs(kv)] += (dv[0] + dv[1]) + (dv[2] + dv[3])

        @pl.when(n == nb - 1)
        def _():
            dsink_ref[...] = jnp.broadcast_to(jnp.sum(sk_acc[...], axis=1, keepdims=True), dsink_ref.shape)

    return pl.pallas_call(
        body, name="swa_bwd", grid=(nb,),
        in_specs=[pl.BlockSpec(memory_space=pltpu.SMEM),
                  pl.BlockSpec((SWA_BLOCK, 512), lambda n: (n, 0)),
                  _full(kp.shape), _full(vp.shape), _full(bias_t.shape),
                  pl.BlockSpec((SWA_BLOCK, 512), lambda n: (n, 0))],
        out_specs=[pl.BlockSpec((SWA_BLOCK, 512), lambda n: (n, 0)), _full(kp.shape), _full(vp.shape),
                   _full(bias_t.shape), _full((SWA_HEADS, 128))],
        out_shape=[jax.ShapeDtypeStruct((s, 512), BF16), jax.ShapeDtypeStruct(kp.shape, F32),
                   jax.ShapeDtypeStruct(vp.shape, F32), jax.ShapeDtypeStruct(bias_t.shape, F32),
                   jax.ShapeDtypeStruct((SWA_HEADS, 128), F32)],
        scratch_shapes=[pltpu.VMEM((SWA_HEADS, 128), F32)],
        compiler_params=_params(("arbitrary",)),
    )(sink, qa, kp, vp, bias_t, doa)


def _fox_bwd(qt, k, v, dot, ot, cc4, lse4):
    s = k.shape[0]
    t = min(FOX_BWD_T, s)
    nq = s // t

    def body(qt_ref, k_ref, v_ref, dot_ref, ot_ref, cc_ref, lse_ref,
             dqt_ref, dk_ref, dv_ref, dck_ref, dcq_ref, delta_ref, dk0, dk1, dv0, dv1, ds0, ds1):
        j = pl.program_id(1)

        @pl.when(j == 0)
        def _():
            dqt_ref[...] = jnp.zeros_like(dqt_ref)
            dcq_ref[...] = jnp.zeros_like(dcq_ref)
            r8 = lax.broadcasted_iota(jnp.int32, (8, t), 0)

            def dl(i, c):
                cols = pl.ds(pl.multiple_of(i * t, t), t)
                pr = dot_ref[:, cols].astype(F32) * ot_ref[:, cols].astype(F32)
                d0 = jnp.sum(jnp.where(_head_rows(0), pr, 0.0), axis=0, keepdims=True)
                d1 = jnp.sum(jnp.where(_head_rows(1), pr, 0.0), axis=0, keepdims=True)
                delta_ref[:, cols] = jnp.where(r8 == 0, d0, jnp.where(r8 == 1, d1, 0.0))
                return c

            lax.fori_loop(0, nq, dl, 0)

        kj = k_ref[...]
        vj = v_ref[...]
        ks = pl.ds(pl.multiple_of(j * t, t), t)
        kt = (kj.astype(F32) * 0.125).T.astype(BF16)
        ke = [jnp.where(_head_mask(e), kj, jnp.zeros_like(kj)) for e in range(2)]
        ve = [jnp.where(_head_mask(e), vj, jnp.zeros_like(vj)) for e in range(2)]
        kte = [jnp.where(_head_rows(e), kt, jnp.zeros_like(kt)) for e in range(2)]
        ck = [cc_ref[0, ks, e:e + 1] for e in range(2)]
        accs = ((dk0, dv0, ds0), (dk1, dv1, ds1))
        for refs in accs:
            for r in refs:
                r[...] = jnp.zeros_like(r)

        def block(q0, nqs, k0, nks, masked):
            cols = pl.ds(pl.multiple_of(q0, 128), nqs)
            rows = slice(k0, k0 + nks)
            qti = qt_ref[:, cols]
            doti = dot_ref[:, cols]
            for e in range(2):
                dkt_acc, dvt_acc, ds_acc = accs[e]
                st = _dot(ke[e][rows, :], qti) - ck[e][rows, :]
                if masked:
                    krow = lax.broadcasted_iota(jnp.int32, (nks, nqs), 0) + (j * t + k0)
                    qcol = lax.broadcasted_iota(jnp.int32, (nks, nqs), 1) + q0
                    st = jnp.where(krow <= qcol, st, NEG)
                pt = jnp.exp(st - lse_ref[0, e:e + 1, cols])
                dpt = _dot(ve[e][rows, :], doti)
                dst = pt * (dpt - delta_ref[e:e + 1, cols])
                dsb = dst.astype(BF16)
                dvt_acc[:, rows] += _dot(doti, pt.astype(BF16), NT)
                dkt_acc[:, rows] += _dot(qti, dsb, NT)
                dqt_ref[:, cols] += _dot(kte[e][:, rows], dsb)
                ds_acc[rows, 0:nqs] += dst
                dcq_ref[0, e:e + 1, cols] += jnp.sum(dst, axis=0, keepdims=True)

        half = t // 2
        block(j * t, half, 0, half, True)
        block(j * t + half, half, 0, t, True)

        def rest(i, c):
            block(i * t, t, 0, t, False)
            return c

        lax.fori_loop(j + 1, nq, rest, 0)
        r0 = _head_rows(0)
        dk_ref[...] = jnp.where(r0, dk0[...], dk1[...]).T.astype(BF16)
        dv_ref[...] = jnp.where(r0, dv0[...], dv1[...]).T.astype(BF16)
        lane = lax.broadcasted_iota(jnp.int32, (t, 128), 1)
        c0 = jnp.sum(ds0[...], axis=-1, keepdims=True)
        c1 = jnp.sum(ds1[...], axis=-1, keepdims=True)
        dck_ref[0] = jnp.where(lane == 0, c0, jnp.where(lane == 1, c1, 0.0))

    res_t = lambda: pl.BlockSpec((128, s), lambda hp, j: (hp, 0))
    blk = lambda: pl.BlockSpec((t, 128), lambda hp, j: (j, hp))
    return pl.pallas_call(
        body, name="fox_bwd", grid=(4, nq),
        in_specs=[res_t(), blk(), blk(), res_t(), res_t(), pl.BlockSpec((1, s, 128), lambda hp, j: (hp, 0, 0)),
                  pl.BlockSpec((1, 8, s), lambda hp, j: (hp, 0, 0))],
        out_specs=[res_t(), blk(), blk(),
                   pl.BlockSpec((1, t, 128), lambda hp, j: (hp, j, 0)),
                   pl.BlockSpec((1, 8, s), lambda hp, j: (hp, 0, 0))],
        out_shape=[jax.ShapeDtypeStruct((512, s), F32), jax.ShapeDtypeStruct((s, 512), BF16),
                   jax.ShapeDtypeStruct((s, 512), BF16), jax.ShapeDtypeStruct((4, s, 128), F32),
                   jax.ShapeDtypeStruct((4, 8, s), F32)],
        scratch_shapes=[pltpu.VMEM((8, s), F32)] + [pltpu.VMEM((128, t), F32)] * 4 + [pltpu.VMEM((t, t), F32)] * 2,
        compiler_params=_params(("arbitrary", "arbitrary")),
    )(qt, k, v, dot, ot, cc4, lse4)


def _mem_bwd(qm, mk, mv, dom):
    s = qm.shape[0]
    tq = min(512, s)

    def body(q_ref, mk_ref, mv_ref, do_ref, dq_ref, dmk_ref, dmv_ref):
        i = pl.program_id(0)

        @pl.when(i == 0)
        def _():
            dmk_ref[...] = jnp.zeros_like(dmk_ref)
            dmv_ref[...] = jnp.zeros_like(dmv_ref)

        heads = range(MEM_HEADS)
        hs = lambda h: slice(h * 128, (h + 1) * 128)
        sc = [_dot(q_ref[:, hs(h)], mk_ref[:, hs(h)], NT) * MEM_SCALE for h in heads]
        dp = [_dot(do_ref[:, hs(h)], mv_ref[:, hs(h)], NT) for h in heads]
        pb, dsb = [], []
        for h in heads:
            p = jnp.exp(sc[h] - jnp.max(sc[h], axis=-1, keepdims=True))
            p = p / jnp.sum(p, axis=-1, keepdims=True)
            ds = p * (dp[h] - jnp.sum(p * dp[h], axis=-1, keepdims=True))
            pb.append(p.astype(BF16))
            dsb.append((ds * MEM_SCALE).astype(BF16))
        dq = [_dot(dsb[h], mk_ref[:, hs(h)]).astype(BF16) for h in heads]
        dmk = [_dot(dsb[h], q_ref[:, hs(h)], TN) for h in heads]
        dmv = [_dot(pb[h], do_ref[:, hs(h)], TN) for h in heads]
        for h in heads:
            dq_ref[:, hs(h)] = dq[h]
            dmk_ref[:, hs(h)] += dmk[h]
            dmv_ref[:, hs(h)] += dmv[h]

    return pl.pallas_call(
        body, name="mem_bwd", grid=(s // tq,),
        in_specs=[pl.BlockSpec((tq, 512), lambda i: (i, 0)), _full(mk.shape), _full(mv.shape),
                  pl.BlockSpec((tq, 512), lambda i: (i, 0))],
        out_specs=[pl.BlockSpec((tq, 512), lambda i: (i, 0)), _full(mk.shape), _full(mv.shape)],
        out_shape=[jax.ShapeDtypeStruct((s, 512), BF16), jax.ShapeDtypeStruct(mk.shape, F32),
                   jax.ShapeDtypeStruct(mv.shape, F32)],
        compiler_params=_params(("arbitrary",)),
    )(qm, mk, mv, dom)


def _memkv_bwd(dmk, dmv, kv_raw, kn_mem, mem, g_mem, mem_n, w_kv):
    def body(dmk_ref, dmv_ref, kv_ref, kn_ref, mem_ref, g_ref, mn_ref, w_ref, dw_ref, dkn_ref, dg_ref, dkv_ref):
        dkn = jnp.zeros((1, 128), F32)
        for h in range(MEM_HEADS):
            hs = slice(h * 128, (h + 1) * 128)
            v = kv_ref[:, hs]
            r = lax.rsqrt(jnp.mean(v * v, axis=-1, keepdims=True) + EPS)
            n = v * r
            dn = dmk_ref[:, hs]
            dkn = dkn + jnp.sum(dn * n, axis=0, keepdims=True)
            dng = dn * kn_ref[...]
            dkv_ref[:, hs] = (r * (dng - n * jnp.mean(dng * n, axis=-1, keepdims=True))).astype(BF16)
        dkv_ref[:, 512:1024] = dmv_ref[...].astype(BF16)
        dkn_ref[...] = dkn
        dkv = dkv_ref[...]
        dw_ref[...] = _dot(mn_ref[...], dkv, TN).astype(BF16)
        dmn = _dot(dkv, w_ref[...], NT)
        xv = mem_ref[...]
        r = lax.rsqrt(jnp.mean(xv * xv, axis=-1, keepdims=True) + EPS)
        dg_ref[...] = jnp.sum(dmn * (xv * r), axis=0, keepdims=True)

    m = mem.shape[0]
    return pl.pallas_call(
        body, name="memkv_bwd",
        out_shape=[jax.ShapeDtypeStruct((D_MODEL, 1024), BF16), jax.ShapeDtypeStruct((1, 128), F32),
                   jax.ShapeDtypeStruct((1, D_MODEL), F32)],
        scratch_shapes=[pltpu.VMEM((m, 1024), BF16)],
        compiler_params=pltpu.CompilerParams(vmem_limit_bytes=VMEM_LIMIT),
    )(dmk, dmv, kv_raw, kn_mem, mem, g_mem, mem_n, w_kv)


def _fox_gate_bwd(dc, proj, b_forget128):
    s = dc.shape[0]
    tm = min(512, s)
    nt = s // tm

    def body(dc_ref, p_ref, b_ref, dfl_ref, db_ref, carry_ref):
        i = pl.program_id(0)

        @pl.when(i == 0)
        def _():
            carry_ref[...] = jnp.zeros_like(carry_ref)
            db_ref[...] = jnp.zeros_like(db_ref)

        dcv = dc_ref[...]
        dlogf = jnp.dot(_tri(tm, False), dcv, precision=lax.Precision.HIGHEST, preferred_element_type=F32) + carry_ref[...]
        carry_ref[...] += jnp.sum(dcv, axis=0, keepdims=True)
        z = p_ref[...] + b_ref[...]
        dfl = dlogf * (1.0 / (1.0 + jnp.exp(z)))
        dfl_ref[...] = dfl.astype(BF16)
        db_ref[...] += jnp.sum(dfl, axis=0, keepdims=True)

    return pl.pallas_call(
        body, name="fox_gate_bwd", grid=(nt,),
        in_specs=[pl.BlockSpec((tm, 128), lambda i: (nt - 1 - i, 0)),
                  pl.BlockSpec((tm, 128), lambda i: (nt - 1 - i, 0)), _full((1, 128))],
        out_specs=[pl.BlockSpec((tm, 128), lambda i: (nt - 1 - i, 0)), _full((1, 128))],
        out_shape=[jax.ShapeDtypeStruct((s, 128), BF16), jax.ShapeDtypeStruct((1, 128), F32)],
        scratch_shapes=[pltpu.VMEM((1, 128), F32)],
        compiler_params=_params(("arbitrary",)),
    )(dc, proj, b_forget128)


def _proj_pre_bwd(dproj, proj, dqf, dkf, dvf, dqm, dqa, dka, dva, dfl, gq_fox, gk_fox, gq_mem, gq_swa, gk_swa):
    s = proj.shape[0]
    tm = min(256, s)

    def body(dp_in, p_ref, dqf_ref, dkf_ref, dvf_ref, dqm_ref, dqa_ref, dka_ref, dva_ref, dfl_ref,
             gqf, gkf, gqm, gqa, gka, dp_ref, dgn_ref):
        i = pl.program_id(0)

        @pl.when(i == 0)
        def _():
            dgn_ref[...] = jnp.zeros_like(dgn_ref)

        def norm_bwd(off, width, hd, g_ref, dn_ref, slot):
            acc = jnp.zeros((1, 128), F32)
            for b in range(width // 128):
                v = p_ref[:, off + b * 128: off + (b + 1) * 128].astype(F32)
                r = lax.rsqrt(_group_mean(v * v, hd) + EPS)
                n = v * r
                dn = dn_ref[b * 128:(b + 1) * 128, :].T if slot == 0 else dn_ref[:, b * 128:(b + 1) * 128].astype(F32)
                acc = acc + jnp.sum(dn * n, axis=0, keepdims=True)
                dng = dn * g_ref[...]
                dp_ref[:, off + b * 128: off + (b + 1) * 128] = (r * (dng - n * _group_mean(dng * n, hd))).astype(BF16)
            dgn_ref[slot:slot + 1, :] += acc

        norm_bwd(H_QF, 512, HEAD, gqf, dqf_ref, 0)
        norm_bwd(H_KF, 512, HEAD, gkf, dkf_ref, 1)
        dp_ref[:, H_VF:H_VF + 512] = dvf_ref[...].astype(BF16)
        norm_bwd(H_QM, 512, MEM_HEAD, gqm, dqm_ref, 2)
        norm_bwd(H_QA, 512, HEAD, gqa, dqa_ref, 3)
        norm_bwd(H_KA, 128, HEAD, gka, dka_ref, 4)
        dp_ref[:, H_VA:H_VA + 128] = dva_ref[...].astype(BF16)
        dp_ref[:, H_FL:H_FL + 128] = dfl_ref[...]
        dp_ref[:, H_FL + 128:HALF_W] = jnp.zeros((tm, HALF_W - H_FL - 128), BF16)

    row = lambda w: pl.BlockSpec((tm, w), lambda i: (i, 0))
    g_spec = _full((1, 128))
    return pl.pallas_call(
        body, name="proj_pre_bwd", grid=(s // tm,),
        in_specs=[pl.BlockSpec(memory_space=pl.ANY), pl.BlockSpec((tm, HALF_W), lambda i: (i, 1)),
                  pl.BlockSpec((512, tm), lambda i: (0, i)), row(512), row(512), row(512), row(512),
                  row(128), row(128), row(128), g_spec, g_spec, g_spec, g_spec, g_spec],
        out_specs=[pl.BlockSpec((tm, HALF_W), lambda i: (i, 1)), _full((8, 128))],
        out_shape=[jax.ShapeDtypeStruct((s, PROJ_W), BF16), jax.ShapeDtypeStruct((8, 128), F32)],
        input_output_aliases={0: 0},
        compiler_params=_params(("arbitrary",)),
    )(dproj, proj, dqf, dkf, dvf, dqm, dqa, dka, dva, dfl, gq_fox, gk_fox, gq_mem, gq_swa, gk_swa)


def _in_bwd_x(dproj, w_in_p, x, g_mix, dx1):
    s = x.shape[0]
    tm = min(256, s)

    def body(dp_ref, w_ref, x_ref, g_ref, dx1_ref, gx_ref, dg_ref):
        i = pl.program_id(0)

        @pl.when(i == 0)
        def _():
            dg_ref[...] = jnp.zeros_like(dg_ref)

        dx, dg = _rms_bwd(x_ref[...], g_ref[...], _dot(dp_ref[...], w_ref[...], NT), dx1_ref[...])
        gx_ref[...] = dx
        dg_ref[...] += dg

    row = pl.BlockSpec((tm, D_MODEL), lambda i: (i, 0))
    return pl.pallas_call(
        body, name="in_bwd_x", grid=(s // tm,),
        in_specs=[pl.BlockSpec((tm, PROJ_W), lambda i: (i, 0)), _full(w_in_p.shape), row, _full((1, D_MODEL)), row],
        out_specs=[row, _full((1, D_MODEL))],
        out_shape=[jax.ShapeDtypeStruct((s, D_MODEL), F32), jax.ShapeDtypeStruct((1, D_MODEL), F32)],
        compiler_params=_params(("arbitrary",)),
    )(dproj, w_in_p, x, g_mix, dx1)


def _rel_bias_bwd(dbias, bucket):
    def body(db_ref, bk_ref, o_ref):
        bk = bk_ref[...]
        lane = lax.broadcasted_iota(jnp.int32, (1, 128), 1)
        for b in range(REL_BUCKETS):
            sel = bk == b
            acc = jnp.zeros((1, 128), F32)
            for h in range(SWA_HEADS):
                tot = jnp.sum(jnp.sum(jnp.where(sel, db_ref[h], 0.0), axis=-1, keepdims=True), axis=0, keepdims=True)
                acc = jnp.where(lane == h, tot, acc)
            o_ref[:, b * 128:(b + 1) * 128] = acc

    return pl.pallas_call(
        body, name="rel_bias_bwd",
        out_shape=jax.ShapeDtypeStruct((1, REL_BUCKETS * 128), F32),
        compiler_params=pltpu.CompilerParams(vmem_limit_bytes=VMEM_LIMIT),
    )(dbias, bucket)


def _my_place():
    return lax.axis_index("x"), lax.axis_index("y"), lax.axis_index("c")


def _peer(place, k):
    x, y, c = place
    return (1 - x if k & 4 else x, 1 - y if k & 2 else y, 1 - c if k & 1 else c)


def _index(place):
    x, y, c = place
    return 4 * x + 2 * y + c


HBM_SPEC = pl.BlockSpec(memory_space=pltpu.HBM)
SEM_SPEC = pl.BlockSpec(memory_space=pltpu.SEMAPHORE)
DATAFLOW = pltpu.SideEffectType.DATAFLOW_SIDE_EFFECTING


ALL_PEERS = tuple(range(1, N_DEV))
SAME_CORE = (2, 4, 6)
OWN = N_DEV - 1


def _split_copy(src_ref, land_ref, send_sems, recv_sems, me, k, gather):
    peer = _peer(me, k)
    if gather:
        src, dst = src_ref, land_ref.at[_index(me)]
    else:
        src, dst = src_ref.at[_index(peer)], land_ref.at[k - 1]
    return pltpu.make_async_remote_copy(src_ref=src, dst_ref=dst, send_sem=send_sems.at[k - 1], recv_sem=recv_sems.at[k - 1],
                                        device_id=peer, device_id_type=MESH)


def _own_copy(src_ref, land_ref, recv_sems, me, gather):
    if gather:
        src, dst = src_ref, land_ref.at[_index(me)]
    else:
        src, dst = src_ref.at[_index(me)], land_ref.at[OWN]
    return pltpu.make_async_copy(src, dst, recv_sems.at[OWN])


def _split_start(srcs, gather, name, peers=ALL_PEERS, after=None):
    n = len(srcs)
    extra = [] if after is None else [after]

    def body(*refs):
        refs = refs[:2 * n] + refs[2 * n + len(extra):]
        src_refs, land_refs = refs[:n], refs[n:2 * n]
        send_sems, recv_sems, token = refs[2 * n:3 * n], refs[3 * n:4 * n], refs[-1]
        me = _my_place()
        for w in range(n):
            for k in peers:
                _split_copy(src_refs[w], land_refs[w], send_sems[w], recv_sems[w], me, k, gather).start()
            _own_copy(src_refs[w], land_refs[w], recv_sems[w], me, gather).start()
        token[...] = jnp.zeros_like(token)

    lands = [lax.empty((N_DEV,) + (a.shape if gather else a.shape[1:]), a.dtype) for a in srcs]
    sems = [pltpu.SemaphoreType.DMA((N_DEV,))] * (2 * n)
    hbm = [pltpu.HBM(a.shape, a.dtype) for a in list(srcs) + lands]
    outs = pl.pallas_call(
        body, name=name,
        out_shape=(*sems, *hbm, jax.ShapeDtypeStruct((8, 128), F32)),
        in_specs=(HBM_SPEC,) * (2 * n) + (pl.BlockSpec(memory_space=pl.ANY),) * len(extra),
        out_specs=(SEM_SPEC,) * (2 * n) + (HBM_SPEC,) * (2 * n) + (pl.BlockSpec(memory_space=pltpu.VMEM),),
        input_output_aliases={i: 2 * n + i for i in range(2 * n)},
        compiler_params=pltpu.CompilerParams(has_side_effects=DATAFLOW),
    )(*[pltpu.with_memory_space_constraint(a, pltpu.HBM) for a in list(srcs) + lands], *extra)
    return list(outs[:n]), list(outs[n:2 * n]), list(outs[2 * n:3 * n]), list(outs[3 * n:4 * n]), outs[-1]


def _split_wait(started, w, after, gather, name):
    send_sems, recv_sems, srcs, lands, _ = started

    def body(src_ref, land_ref, send_sems, recv_sems, after_ref, src_out, land_out):
        me = _my_place()
        for k in ALL_PEERS:
            cp = _split_copy(src_ref, land_ref, send_sems, recv_sems, me, k, gather)
            cp.wait_send()
            cp.wait_recv()
        _own_copy(src_ref, land_ref, recv_sems, me, gather).wait()

    return pl.pallas_call(
        body, name=name,
        out_shape=(pltpu.HBM(srcs[w].shape, srcs[w].dtype), pltpu.HBM(lands[w].shape, lands[w].dtype)),
        in_specs=(HBM_SPEC, HBM_SPEC, SEM_SPEC, SEM_SPEC, pl.BlockSpec(memory_space=pl.ANY)),
        out_specs=(HBM_SPEC, HBM_SPEC), input_output_aliases={0: 0, 1: 1},
        compiler_params=pltpu.CompilerParams(has_side_effects=DATAFLOW),
    )(srcs[w], lands[w], send_sems[w], recv_sems[w], after)[1]


def _forward_copy(land_ref, send_sems, recv_sems, me, j, incoming):
    sibling = _peer(me, 1)
    rows = land_ref.at[_index(_peer(sibling if incoming else me, SAME_CORE[j]))]
    return pltpu.make_async_remote_copy(src_ref=rows, dst_ref=rows, send_sem=send_sems.at[j], recv_sem=recv_sems.at[j],
                                        device_id=sibling, device_id_type=MESH)


def _forward_start(started, after, name):
    send_a, recv_a, srcs, lands, _ = started

    def body(src_ref, land_ref, send_a, recv_a, after_ref, send_b, recv_b, src_out, land_out):
        me = _my_place()
        for j, k in enumerate(SAME_CORE):
            _split_copy(src_ref, land_ref, send_a, recv_a, me, k, True).wait_recv()
            _forward_copy(land_ref, send_b, recv_b, me, j, False).start()

    sems = pltpu.SemaphoreType.DMA((len(SAME_CORE),))
    return pl.pallas_call(
        body, name=name,
        out_shape=(sems, sems, pltpu.HBM(srcs[0].shape, srcs[0].dtype), pltpu.HBM(lands[0].shape, lands[0].dtype)),
        in_specs=(HBM_SPEC, HBM_SPEC, SEM_SPEC, SEM_SPEC, pl.BlockSpec(memory_space=pl.ANY)),
        out_specs=(SEM_SPEC, SEM_SPEC, HBM_SPEC, HBM_SPEC), input_output_aliases={0: 2, 1: 3},
        compiler_params=pltpu.CompilerParams(has_side_effects=DATAFLOW),
    )(srcs[0], lands[0], send_a[0], recv_a[0], after)


def _forward_wait(started, forwarded, name):
    send_a, recv_a, _, _, _ = started
    send_b, recv_b, src, land = forwarded

    def body(src_ref, land_ref, send_a, recv_a, send_b, recv_b, src_out, land_out):
        me = _my_place()
        _own_copy(src_ref, land_ref, recv_a, me, True).wait()
        for k in (1,) + SAME_CORE:
            _split_copy(src_ref, land_ref, send_a, recv_a, me, k, True).wait_send()
        _split_copy(src_ref, land_ref, send_a, recv_a, me, 1, True).wait_recv()
        for j in range(len(SAME_CORE)):
            _forward_copy(land_ref, send_b, recv_b, me, j, False).wait_send()
            _forward_copy(land_ref, send_b, recv_b, me, j, True).wait_recv()

    return pl.pallas_call(
        body, name=name,
        out_shape=(pltpu.HBM(src.shape, src.dtype), pltpu.HBM(land.shape, land.dtype)),
        in_specs=(HBM_SPEC, HBM_SPEC, SEM_SPEC, SEM_SPEC, SEM_SPEC, SEM_SPEC),
        out_specs=(HBM_SPEC, HBM_SPEC), input_output_aliases={0: 0, 1: 1},
        compiler_params=pltpu.CompilerParams(has_side_effects=DATAFLOW),
    )(src, land, send_a[0], recv_a[0], send_b, recv_b)[1]


def _adam_math(w, g, m, v):
    m2 = ADAM_B1 * m + (1.0 - ADAM_B1) * g
    v2 = ADAM_B2 * v + (1.0 - ADAM_B2) * (g * g)
    m_hat = m2 / (1.0 - ADAM_B1 ** ADAM_STEP)
    v_hat = v2 / (1.0 - ADAM_B2 ** ADAM_STEP)
    delta = -ADAM_LR * (m_hat / (jnp.sqrt(v_hat) + ADAM_EPS) + ADAM_WD * w)
    return delta, m2, v2


def _adamw(land, w, m, v, name):
    a, b = w.shape
    bp = land.shape[2]
    ta = min(128, a)

    def body(p_ref, w_ref, m_ref, v_ref, g_ref, d_ref, m2_ref, v2_ref):
        g = p_ref[0, :, 0:b].astype(F32)
        for k in range(1, N_DEV):
            g = g + p_ref[k, :, 0:b].astype(F32)
        delta, m2, v2 = _adam_math(w_ref[...], g, m_ref[...], v_ref[...])
        g_ref[...] = g
        d_ref[...] = delta
        m2_ref[...] = m2
        v2_ref[...] = v2

    blk = pl.BlockSpec((ta, b), lambda i: (i, 0))
    sd = jax.ShapeDtypeStruct((a, b), F32)
    return pl.pallas_call(
        body, name=name, grid=(a // ta,),
        in_specs=[pl.BlockSpec((N_DEV, ta, bp), lambda i: (0, i, 0)), blk, blk, blk],
        out_specs=[blk, blk, blk, blk], out_shape=[sd, sd, sd, sd],
        compiler_params=_params(("parallel",)),
    )(land, w, m, v)


def _bucket_table():
    t_loc = jnp.arange(SWA_BLOCK)[:, None] + SWA_BLOCK
    s_loc = jnp.arange(2 * SWA_BLOCK)[None, :]
    dist = t_loc - s_loc
    max_exact = REL_BUCKETS // 2
    d = jnp.maximum(dist, 0)
    df = jnp.maximum(d, 1).astype(F32)
    large = max_exact + (jnp.log(df / max_exact) / math.log(REL_MAX_DIST / max_exact) * (REL_BUCKETS - max_exact)).astype(jnp.int32)
    large = jnp.minimum(large, REL_BUCKETS - 1)
    bucket = jnp.where(d < max_exact, d, large)
    band = (dist >= 0) & (dist < SWA_BLOCK)
    return bucket, band


def _tile2(g):
    return jnp.concatenate([g, g], axis=1) if g.shape[1] == HEAD else g


SHARD_W = 737
SHARD_WP = 768
IN_WIDTH = N_DEV * SHARD_W
SEGMENTS = ((GL0, 2824, 3072), (QF0, 768, 512), (KF0, 1280, 512), (VF0, 1792, 512), (QM0, 2312, 512),
            (QA0, 0, 512), (KA0, 512, 128), (VA0, 640, 128), (FL0, 2304, 8))


def _lane_plan(sources):
    plan = []
    for t in range(len(sources) // 128):
        groups = {}
        for lane in range(128):
            src = sources[128 * t + lane]
            if src is not None:
                slab, col = src
                groups.setdefault((slab, col // 128, (lane - col) % 128), []).append(lane)
        tile = []
        for key, lanes in groups.items():
            assert lanes == list(range(lanes[0], lanes[-1] + 1))
            tile.append((key, lanes[0], lanes[-1] + 1))
        plan.append(tile)
    return plan


def _assemble(tile_plan, load, rows):
    lane = lax.broadcasted_iota(jnp.int32, (1, 128), 1)
    out = jnp.zeros((rows, 128), F32)
    for (slab, st, roll), lo, hi in tile_plan:
        v = load(slab, st)
        if roll:
            v = pltpu.roll(v, roll, 1)
        out = v if (lo, hi) == (0, 128) else jnp.where((lane >= lo) & (lane < hi), v, out)
    return out


def _w_in_from_shards(land):
    ref_col = [None] * PROJ_W
    for p0, r0, n in SEGMENTS:
        for i in range(n):
            ref_col[p0 + i] = divmod(r0 + i, SHARD_W)
    plan = _lane_plan(ref_col)
    d_model = land.shape[1]
    tm = 256

    def body(land_ref, o_ref):
        load = lambda slab, st: land_ref[slab, :, st * 128:(st + 1) * 128].astype(F32)
        for t, tile_plan in enumerate(plan):
            o_ref[:, t * 128:(t + 1) * 128] = _assemble(tile_plan, load, tm).astype(BF16)

    return pl.pallas_call(
        body, name="w_in_from_shards", grid=(d_model // tm,),
        in_specs=[pl.BlockSpec((N_DEV, tm, SHARD_WP), lambda i: (0, i, 0))],
        out_specs=pl.BlockSpec((tm, PROJ_W), lambda i: (i, 0)),
        out_shape=jax.ShapeDtypeStruct((d_model, PROJ_W), BF16),
        compiler_params=_params(("parallel",)),
    )(land)


def _dw_in_to_parts(dwp):
    padded_col = [None] * IN_WIDTH
    for p0, r0, n in SEGMENTS:
        for i in range(n):
            padded_col[r0 + i] = p0 + i
    sources = []
    for d in range(N_DEV):
        sources += [(0, padded_col[SHARD_W * d + c]) if c < SHARD_W else None for c in range(SHARD_WP)]
    plan = _lane_plan(sources)
    d_model = dwp.shape[0]
    tm = 256
    tiles = SHARD_WP // 128

    def body(dw_ref, o_ref):
        load = lambda slab, st: dw_ref[:, st * 128:(st + 1) * 128].astype(F32)
        for t, tile_plan in enumerate(plan):
            d, c = divmod(t, tiles)
            o_ref[d, :, c * 128:(c + 1) * 128] = _assemble(tile_plan, load, tm).astype(BF16)

    return pl.pallas_call(
        body, name="dw_in_to_parts", grid=(d_model // tm,),
        in_specs=[pl.BlockSpec((tm, PROJ_W), lambda i: (i, 0))],
        out_specs=pl.BlockSpec((N_DEV, tm, SHARD_WP), lambda i: (0, i, 0)),
        out_shape=jax.ShapeDtypeStruct((N_DEV, d_model, SHARD_WP), BF16),
        compiler_params=_params(("parallel",)),
    )(dwp)


def _cast_shards(shards):
    names = list(shards)

    def body(*refs):
        for src, dst in zip(refs[:len(names)], refs[len(names):]):
            if dst.shape != src.shape:
                dst[...] = jnp.zeros(dst.shape, BF16)
                dst[:, 0:src.shape[1]] = src[...].astype(BF16)
            else:
                dst[...] = src[...].astype(BF16)

    out_shape = [jax.ShapeDtypeStruct((shards[n].shape[0], SHARD_WP if n == "w_in" else shards[n].shape[1]), BF16)
                 for n in names]
    outs = pl.pallas_call(body, name="cast_shards", out_shape=out_shape,
                          compiler_params=pltpu.CompilerParams(vmem_limit_bytes=VMEM_LIMIT))(*[shards[n] for n in names])
    return dict(zip(names, outs))


def _tie(x, *tokens):
    for t in tokens:
        if t is not None:
            x = x + t[0:1, 0:1]
    return x


def _local_step(x, mem, target, p, getw, emit, deps=()):
    s = x.shape[0]
    bucket, band = _bucket_table()
    bucket_m = jnp.where(band, bucket, -1).astype(jnp.int32)
    bias = _bias_table(p["rel_bias"], bucket_m)
    bucket_t = jnp.transpose(bucket_m)
    bias_t = _bias_table(p["rel_bias"], bucket_t)
    gqf, gkf, gqa, gka = _tile2(p["qn_fox"]), _tile2(p["kn_fox"]), _tile2(p["qn_swa"]), _tile2(p["kn_swa"])
    gqm = p["qn_mem"]
    bf128 = jnp.pad(p["b_forget"], ((0, 0), (0, 120)))
    sink = p["sink_swa"].reshape(8)

    h = _rms_fwd(x, p["g_mix"], "rms_mix", deps)
    w_in = getw("w_in", h)
    proj = _mm(h, w_in, "nn", BF16, 512, 1536, 1024, "proj")
    fl = _mm(h, w_in[:, FL0:FL0 + 128], "nn", F32, 512, 128, 1024, "proj_fl")
    qf, kf, vf, qm, qa, ka, va, qf_t, vf_t = _proj_post(proj, gqf, gkf, gqm, gqa, gka)
    cc4, ca4 = _fox_gate_fwd(fl, bf128)
    w_kv = getw("w_mem_kv", cc4)
    mem_n, kv_raw, mk, mv = _memkv_fwd(mem, p["g_mem"], w_kv, p["kn_mem"])
    kp = jnp.pad(ka, ((SWA_BLOCK, 0), (0, 0)))
    vp = jnp.pad(va, ((SWA_BLOCK, 0), (0, 0)))
    oa = _swa_fwd(qa, kp, vp, bias, sink)
    of, lse4, of_t = _fox_fwd(qf, kf, vf_t, ca4)
    om = _mem_fwd(qm, mk, mv)
    wa, wf, wm, w_out = getw("w_o_swa", oa), getw("w_o_fox", oa), getw("w_o_mem", oa), getw("w_out", oa)
    x1, hm, merged = _merge_fwd(x, oa, of, om, proj, p["b_gate"], wa, wf, wm, w_out, p["g_mlp"])
    w_up = getw("w_mlp_up", of)
    u = _mlp_up(hm, w_up)
    w_down = getw("w_mlp_down", hm)
    dy, dy_b, loss = _mlp_down_loss(u, w_down, x1, target)

    da = _mlp_bwd_act(dy_b, w_down, u)
    t_down = emit({"w_mlp_down": _mm(u, dy_b, "tn", BF16, 1024, 1024, 2048, "dw_down")})
    dx1, dg_mlp = _mlp_bwd_x(da, w_up, x1, dy, _tie(p["g_mlp"], t_down))
    t_up = emit({"w_mlp_up": _mm(hm, da, "tn", BF16, 1024, 1024, 2048, "dw_up", column_chunks=True)})
    dproj, doa, dof_t, dom, dya, dyf, dym, db_gate = _merge_bwd(
        dx1, oa, of, om, proj, _tie(p["b_gate"], t_up), wa, wf, wm, w_out)
    t_o = emit({"w_out": _mm(merged, dx1, "tn", BF16, 1024, 1024, 2048, "dw_out"),
                "w_o_swa": _mm(oa, dya, "tn", BF16, 512, 1024, 2048, "dw_o_swa"),
                "w_o_fox": _mm(of, dyf, "tn", BF16, 512, 1024, 2048, "dw_o_fox"),
                "w_o_mem": _mm(om, dym, "tn", BF16, 512, 1024, 2048, "dw_o_mem")})

    dqm, dmk, dmv = _mem_bwd(qm, mk, mv, dom)
    dw_kv, dkn_mem, dg_mem = _memkv_bwd(dmk, dmv, kv_raw, _tie(p["kn_mem"], t_o), mem, p["g_mem"], mem_n, w_kv)
    t_kv = emit({"w_mem_kv": dw_kv})
    dqa, dkp, dvp, dbias, dsink = _swa_bwd(qa, kp, vp, bias_t, _tie(p["sink_swa"], t_kv).reshape(8), doa)
    dqf_t, dkf, dvf, dck4, dcq4 = _fox_bwd(qf_t, kf, vf, dof_t, of_t, cc4, lse4)

    dcq = jnp.transpose(dcq4[:, 0:2, :], (2, 0, 1)).reshape(s, 8)
    dck = jnp.transpose(dck4[:, :, 0:2], (1, 0, 2)).reshape(s, 8)
    dc = jnp.pad(dcq - dck, ((0, 0), (0, 120)))
    dfl, db_forget = _fox_gate_bwd(dc, fl, bf128)

    dproj, dgn = _proj_pre_bwd(dproj, proj, dqf_t, dkf, dvf, dqm, dqa, dkp[SWA_BLOCK:], dvp[SWA_BLOCK:], dfl,
                               gqf, gkf, gqm, gqa, gka)
    t_in = emit({"w_in": _mm(h, dproj, "tn", BF16, 1024, 3072, 1024, "dw_in")})
    grad_x, dg_mix = _in_bwd_x(dproj, w_in, x, _tie(p["g_mix"], t_in), dx1)
    d_rel = _rel_bias_bwd(dbias, bucket_t)

    fold = lambda r: dgn[r:r + 1, 0:HEAD] + dgn[r:r + 1, HEAD:128]
    small = {
        "g_mix": dg_mix, "b_gate": db_gate, "b_forget": db_forget[:, 0:8],
        "qn_swa": fold(3), "kn_swa": fold(4), "sink_swa": dsink[:, 0].reshape(1, 8), "rel_bias": d_rel,
        "qn_fox": fold(0), "kn_fox": fold(1), "g_mem": dg_mem, "qn_mem": dgn[2:3, :], "kn_mem": dkn_mem,
        "g_mlp": dg_mlp,
    }
    return loss, grad_x, small


SMALL = ("g_mix", "b_gate", "b_forget", "qn_swa", "kn_swa", "sink_swa", "rel_bias", "qn_fox", "kn_fox", "g_mem",
         "qn_mem", "kn_mem", "g_mlp")
BIG = ("w_in", "w_mem_kv", "w_o_swa", "w_o_fox", "w_o_mem", "w_out", "w_mlp_up", "w_mlp_down")
COL_SHARDED = ("w_in", "w_o_swa", "w_o_fox", "w_o_mem", "w_mlp_up")
WEIGHTS = ("g_mix", "w_in", "b_gate", "b_forget", "qn_swa", "kn_swa", "sink_swa", "rel_bias", "qn_fox", "kn_fox", "g_mem",
           "w_mem_kv", "qn_mem", "kn_mem", "w_o_swa", "w_o_fox", "w_o_mem", "w_out", "g_mlp", "w_mlp_up", "w_mlp_down")
SMALL_SLOTS = (("g_mix", 1024), ("b_gate", 3072), ("b_forget", 128), ("qn_swa", 128), ("kn_swa", 128), ("sink_swa", 128),
               ("rel_bias", REL_BUCKETS * 128), ("qn_fox", 128), ("kn_fox", 128), ("g_mem", 1024), ("qn_mem", 128),
               ("kn_mem", 128), ("g_mlp", 1024), ("loss", 128))
SMALL_OFF = {n: sum(w for _, w in SMALL_SLOTS[:i]) for i, (n, _) in enumerate(SMALL_SLOTS)}
SMALL_ROW = sum(w for _, w in SMALL_SLOTS)


def _gathered_to_full(name, g):
    if name in COL_SHARDED:
        return jnp.transpose(g, (1, 0, 2)).reshape(g.shape[1], N_DEV * g.shape[2])
    return g.reshape(N_DEV * g.shape[1], g.shape[2])


def _full_to_parts(name, full, b):
    if name in COL_SHARDED:
        return jnp.transpose(full.reshape(full.shape[0], N_DEV, b), (1, 0, 2)).astype(BF16)
    return full.reshape(N_DEV, full.shape[0] // N_DEV, full.shape[1]).astype(BF16)


def _pack_small(grads, loss):
    pieces = []
    for n, width in SMALL_SLOTS:
        a = loss.reshape(1, 1) if n == "loss" else grads[n].reshape(1, -1)
        pieces.append(jnp.pad(a, ((0, 0), (0, width - a.shape[1]))))
    return jnp.concatenate(pieces, axis=1)


def _adamw_small(gathered, w, m, v):
    names = list(SMALL)

    def body(*refs):
        p_ref = refs[0]
        ins = refs[1:1 + 3 * len(names)]
        outs = refs[1 + 3 * len(names):]
        g_all = p_ref[0]
        for k in range(1, N_DEV):
            g_all = g_all + p_ref[k]
        for i, n in enumerate(names):
            w_ref, m_ref, v_ref = ins[3 * i:3 * i + 3]
            out = outs[4 * i:4 * i + 4]
            rows, cols = w_ref.shape
            for r in range(rows):
                off = SMALL_OFF[n] + 128 * r
                g = g_all[:, off:off + cols]
                rs = slice(r, r + 1)
                res = (g,) + _adam_math(w_ref[rs, :], g, m_ref[rs, :], v_ref[rs, :])
                for o_ref, val in zip(out, res):
                    o_ref[rs, :] = val
        outs[-1][...] = g_all[:, SMALL_OFF["loss"]:SMALL_OFF["loss"] + 128]

    args = [gathered]
    out_shape = []
    for n in names:
        args += [w[n], m[n], v[n]]
        out_shape += [jax.ShapeDtypeStruct(w[n].shape, F32)] * 4
    out_shape.append(jax.ShapeDtypeStruct((1, 128), F32))
    outs = pl.pallas_call(body, name="adamw_small", out_shape=out_shape)(*args)
    return {n: outs[4 * i:4 * i + 4] for i, n in enumerate(names)}, outs[-1]


def kernel(x, mem, g_mix, w_in, b_gate, b_forget, qn_swa, kn_swa, sink_swa, rel_bias, qn_fox, kn_fox, g_mem, w_mem_kv, qn_mem, kn_mem, w_o_swa, w_o_fox, w_o_mem, w_out, g_mlp, w_mlp_up, w_mlp_down, loss_target, m_g_mix, m_w_in, m_b_gate, m_b_forget, m_qn_swa, m_kn_swa, m_sink_swa, m_rel_bias, m_qn_fox, m_kn_fox, m_g_mem, m_w_mem_kv, m_qn_mem, m_kn_mem, m_w_o_swa, m_w_o_fox, m_w_o_mem, m_w_out, m_g_mlp, m_w_mlp_up, m_w_mlp_down, v_g_mix, v_w_in, v_b_gate, v_b_forget, v_qn_swa, v_kn_swa, v_sink_swa, v_rel_bias, v_qn_fox, v_kn_fox, v_g_mem, v_w_mem_kv, v_qn_mem, v_kn_mem, v_w_o_swa, v_w_o_fox, v_w_o_mem, v_w_out, v_g_mlp, v_w_mlp_up, v_w_mlp_down):
    wts = dict(g_mix=g_mix, w_in=w_in, b_gate=b_gate, b_forget=b_forget, qn_swa=qn_swa, kn_swa=kn_swa, sink_swa=sink_swa,
               rel_bias=rel_bias, qn_fox=qn_fox, kn_fox=kn_fox, g_mem=g_mem, w_mem_kv=w_mem_kv, qn_mem=qn_mem, kn_mem=kn_mem,
               w_o_swa=w_o_swa, w_o_fox=w_o_fox, w_o_mem=w_o_mem, w_out=w_out, g_mlp=g_mlp, w_mlp_up=w_mlp_up,
               w_mlp_down=w_mlp_down)
    mom = dict(g_mix=m_g_mix, w_in=m_w_in, b_gate=m_b_gate, b_forget=m_b_forget, qn_swa=m_qn_swa, kn_swa=m_kn_swa,
               sink_swa=m_sink_swa, rel_bias=m_rel_bias, qn_fox=m_qn_fox, kn_fox=m_kn_fox, g_mem=m_g_mem, w_mem_kv=m_w_mem_kv,
               qn_mem=m_qn_mem, kn_mem=m_kn_mem, w_o_swa=m_w_o_swa, w_o_fox=m_w_o_fox, w_o_mem=m_w_o_mem, w_out=m_w_out,
               g_mlp=m_g_mlp, w_mlp_up=m_w_mlp_up, w_mlp_down=m_w_mlp_down)
    var = dict(g_mix=v_g_mix, w_in=v_w_in, b_gate=v_b_gate, b_forget=v_b_forget, qn_swa=v_qn_swa, kn_swa=v_kn_swa,
               sink_swa=v_sink_swa, rel_bias=v_rel_bias, qn_fox=v_qn_fox, kn_fox=v_kn_fox, g_mem=v_g_mem, w_mem_kv=v_w_mem_kv,
               qn_mem=v_qn_mem, kn_mem=v_kn_mem, w_o_swa=v_w_o_swa, w_o_fox=v_w_o_fox, w_o_mem=v_w_o_mem, w_out=v_w_out,
               g_mlp=v_g_mlp, w_mlp_up=v_w_mlp_up, w_mlp_down=v_w_mlp_down)

    shards = _cast_shards({n: wts[n][0] for n in BIG})
    first = _split_start([shards["w_in"]], True, "ag_start_w_in", peers=(1,) + SAME_CORE)
    rest = _split_start([shards[n] for n in BIG[1:]], True, "ag_start_rest", after=first[4])
    full = {}

    def getw(n, after):
        if n == "w_in" and n not in full:
            forwarded = _forward_start(first, after, "ag_forward_w_in")
            full[n] = _w_in_from_shards(_forward_wait(first, forwarded, "ag_wait_w_in"))
        elif n not in full:
            land = _split_wait(rest, BIG[1:].index(n), after, True, "ag_wait_" + n)
            full[n] = land if n == "w_mlp_up" else _gathered_to_full(n, land)
        return full[n]

    exchanges = {}

    def emit(grads_by_name):
        parts = []
        for n, grad in grads_by_name.items():
            if n == "w_in":
                parts.append(_dw_in_to_parts(grad))
            else:
                parts.append(grad if n == "w_mlp_up" else _full_to_parts(n, grad, wts[n].shape[2]))
        started = _split_start(parts, False, "rs_start_" + next(iter(grads_by_name)))
        for w, n in enumerate(grads_by_name):
            exchanges[n] = (started, w)
        return started[4]

    small_p = {n: wts[n] for n in SMALL}
    loss, grad_x, small_g = _local_step(x[0], mem[0], loss_target[0], small_p, getw, emit, (first[4], rest[4]))

    packed = _pack_small(small_g, loss)
    small_gather = _split_start([packed], True, "ag_start_small")

    grads, delta, new_m, new_v = {}, {}, {}, {}

    def update(n, after):
        land = _split_wait(*exchanges[n], after, False, "rs_wait_" + n)
        g, d, m2, v2 = _adamw(land, wts[n][0], mom[n][0], var[n][0], "adamw_" + n)
        grads[n], delta[n], new_m[n], new_v[n] = g[None], d[None], m2[None], v2[None]
        return d

    after = small_gather[4]
    for n in exchanges:
        if n != "w_in":
            after = update(n, after)

    gathered = _split_wait(small_gather, 0, after, True, "ag_wait_small")
    small_out, total = _adamw_small(gathered, small_p, mom, var)
    for name, (g, d, m2, v2) in small_out.items():
        grads[name], delta[name], new_m[name], new_v[name] = g, d, m2, v2
    update("w_in", total)

    return (total[0, 0], grad_x[None], *[grads[n] for n in WEIGHTS], *[delta[n] for n in WEIGHTS],
            *[new_m[n] for n in WEIGHTS], *[new_v[n] for n in WEIGHTS])
```

```python
import math

import jax
import jax.numpy as jnp
from jax import lax
from jax.experimental import pallas as pl
from jax.experimental.pallas import tpu as pltpu

F32 = jnp.float32
BF16 = jnp.bfloat16

D_MODEL = 1024
N_MEM = 256
D_FF = 4096
HEAD = 64
SWA_HEADS = 8
SWA_BLOCK = 128
MEM_HEADS = 4
MEM_HEAD = 128
EPS = 1e-6
NEG = -1e30
REL_BUCKETS = 32
REL_MAX_DIST = 128

ADAM_LR = 0.001
ADAM_B1 = 0.9
ADAM_B2 = 0.999
ADAM_EPS = 1e-08
ADAM_WD = 0.01
ADAM_STEP = 10

GL0, QF0, KF0, VF0, QM0, QA0, KA0, VA0, FL0 = 0, 3072, 3584, 4096, 4608, 5120, 5632, 5760, 5888
PROJ_W = 6144
HALF_W = 3072
H_QF, H_KF, H_VF, H_QM, H_QA, H_KA, H_VA, H_FL = 0, 512, 1024, 1536, 2048, 2560, 2688, 2816

VMEM_LIMIT = 56 * 1024 * 1024
N_DEV = 8
MESH = pl.DeviceIdType.MESH

NN = (((1,), (0,)), ((), ()))
NT = (((1,), (1,)), ((), ()))
TN = (((0,), (0,)), ((), ()))


def _dot(a, b, dims=NN):
    return lax.dot_general(a, b, dims, preferred_element_type=F32)


def _params(sem):
    return pltpu.CompilerParams(dimension_semantics=sem, vmem_limit_bytes=VMEM_LIMIT)


def _full(shape):
    nd = len(shape)
    return pl.BlockSpec(shape, lambda *_: (0,) * nd)


def _sigmoid(z):
    return 1.0 / (1.0 + jnp.exp(-z))


def _group_mean(v, hd):
    if hd == 128:
        return jnp.mean(v, axis=-1, keepdims=True)
    lane = lax.broadcasted_iota(jnp.int32, v.shape, 1)
    lo = lane < HEAD
    s_lo = jnp.sum(jnp.where(lo, v, 0.0), axis=-1, keepdims=True)
    s_hi = jnp.sum(jnp.where(lo, 0.0, v), axis=-1, keepdims=True)
    return jnp.where(lo, s_lo, s_hi) * (1.0 / HEAD)


def _mm(a, b, mode, out_dtype, tm, tn, tk, name, column_chunks=False):
    if mode == "nn":
        m, k = a.shape
        n = b.shape[1]
    elif mode == "nt":
        m, k = a.shape
        n = b.shape[0]
    else:
        k, m = a.shape
        n = b.shape[1]
    tm, tn, tk = min(tm, m), min(tn, n), min(tk, k)
    nk = k // tk
    chunk = n // N_DEV
    per_tile = tn // chunk if column_chunks else 1
    dims = {"nn": NN, "nt": NT, "tn": TN}[mode]
    a_spec = pl.BlockSpec((tk, tm), lambda j, i, kk: (kk, i)) if mode == "tn" else pl.BlockSpec((tm, tk), lambda j, i, kk: (i, kk))
    b_spec = pl.BlockSpec((tn, tk), lambda j, i, kk: (j, kk)) if mode == "nt" else pl.BlockSpec((tk, tn), lambda j, i, kk: (kk, j))

    def body(a_ref, b_ref, o_ref, *acc):
        prod = _dot(a_ref[...].astype(BF16), b_ref[...].astype(BF16), dims)

        def write(res):
            if column_chunks:
                for c in range(per_tile):
                    o_ref[c] = res[:, c * chunk:(c + 1) * chunk].astype(o_ref.dtype)
            else:
                o_ref[...] = res.astype(o_ref.dtype)

        if nk == 1:
            write(prod)
        else:
            acc_ref, = acc
            kk = pl.program_id(2)

            @pl.when(kk == 0)
            def _():
                acc_ref[...] = prod

            @pl.when(kk > 0)
            def _():
                acc_ref[...] += prod

            @pl.when(kk == nk - 1)
            def _():
                write(acc_ref[...])

    return pl.pallas_call(
        body, name=name, grid=(n // tn, m // tm, nk),
        in_specs=[a_spec, b_spec],
        out_specs=(pl.BlockSpec((per_tile, tm, chunk), lambda j, i, kk: (j, i, 0)) if column_chunks
                   else pl.BlockSpec((tm, tn), lambda j, i, kk: (i, j))),
        out_shape=jax.ShapeDtypeStruct((N_DEV, m, chunk) if column_chunks else (m, n), out_dtype),
        scratch_shapes=[pltpu.VMEM((tm, tn), F32)] if nk > 1 else [],
        compiler_params=_params(("parallel", "parallel", "arbitrary")),
    )(a, b)


def _rms_fwd(x, g, name, deps=()):
    s, d = x.shape
    tm = min(512, s)

    def body(x_ref, g_ref, *rest):
        h_ref = rest[len(deps)]
        xv = x_ref[...]
        r = lax.rsqrt(jnp.mean(xv * xv, axis=-1, keepdims=True) + EPS)
        h_ref[...] = (xv * r * g_ref[...]).astype(BF16)

    return pl.pallas_call(
        body, name=name, grid=(s // tm,),
        in_specs=[pl.BlockSpec((tm, d), lambda i: (i, 0)), _full((1, d))] + [pl.BlockSpec(memory_space=pl.ANY)] * len(deps),
        out_specs=pl.BlockSpec((tm, d), lambda i: (i, 0)),
        out_shape=jax.ShapeDtypeStruct((s, d), BF16),
        compiler_params=_params(("parallel",)),
    )(x, g, *deps)


def _proj_post(proj, gq_fox, gk_fox, gq_mem, gq_swa, gk_swa):
    s = proj.shape[0]
    tm = min(256, s)

    def body(p_ref, gqf, gkf, gqm, gqa, gka, qf_ref, kf_ref, vf_ref, qm_ref, qa_ref, ka_ref, va_ref, qft_ref, vft_ref):
        def norm(off, width, hd, g_ref, o_ref, scaled_t_ref=None):
            for b in range(width // 128):
                v = p_ref[:, off + b * 128: off + (b + 1) * 128].astype(F32)
                r = lax.rsqrt(_group_mean(v * v, hd) + EPS)
                vn = (v * r * g_ref[...]).astype(BF16)
                o_ref[:, b * 128:(b + 1) * 128] = vn
                if scaled_t_ref is not None:
                    scaled_t_ref[b * 128:(b + 1) * 128, :] = (vn.astype(F32) * 0.125).T.astype(BF16)

        norm(H_QF, 512, HEAD, gqf, qf_ref, qft_ref)
        norm(H_KF, 512, HEAD, gkf, kf_ref)
        vf_ref[...] = p_ref[:, H_VF:H_VF + 512].astype(BF16)
        for b in range(4):
            vft_ref[b * 128:(b + 1) * 128, :] = p_ref[:, H_VF + b * 128:H_VF + (b + 1) * 128].astype(F32).T.astype(BF16)
        norm(H_QM, 512, MEM_HEAD, gqm, qm_ref)
        norm(H_QA, 512, HEAD, gqa, qa_ref)
        norm(H_KA, 128, HEAD, gka, ka_ref)
        va_ref[...] = p_ref[:, H_VA:H_VA + 128].astype(BF16)

    g_spec = _full((1, 128))
    o512 = pl.BlockSpec((tm, 512), lambda i: (i, 0))
    o128 = pl.BlockSpec((tm, 128), lambda i: (i, 0))
    s512 = jax.ShapeDtypeStruct((s, 512), BF16)
    s128 = jax.ShapeDtypeStruct((s, 128), BF16)
    return pl.pallas_call(
        body, name="proj_post", grid=(s // tm,),
        in_specs=[pl.BlockSpec((tm, HALF_W), lambda i: (i, 1)), g_spec, g_spec, g_spec, g_spec, g_spec],
        out_specs=[o512, o512, o512, o512, o512, o128, o128] + [pl.BlockSpec((512, tm), lambda i: (0, i))] * 2,
        out_shape=[s512, s512, s512, s512, s512, s128, s128] + [jax.ShapeDtypeStruct((512, s), BF16)] * 2,
        compiler_params=_params(("parallel",)),
    )(proj, gq_fox, gk_fox, gq_mem, gq_swa, gk_swa)


def _tri(n, lower):
    r = lax.broadcasted_iota(jnp.int32, (n, n), 0)
    c = lax.broadcasted_iota(jnp.int32, (n, n), 1)
    return jnp.where((c <= r) if lower else (c >= r), 1.0, 0.0).astype(F32)


def _fox_gate_fwd(proj, b_forget128):
    s = proj.shape[0]
    tm = min(512, s)

    def body(p_ref, b_ref, cc_ref, ca_ref, carry_ref):
        i = pl.program_id(0)

        @pl.when(i == 0)
        def _():
            carry_ref[...] = jnp.zeros_like(carry_ref)

        z = p_ref[...] + b_ref[...]
        logf = jnp.minimum(z, 0.0) - jnp.log(1.0 + jnp.exp(-jnp.abs(z)))
        c = jnp.dot(_tri(tm, True), logf, precision=lax.Precision.HIGHEST, preferred_element_type=F32) + carry_ref[...]
        carry_ref[...] = c[tm - 1:tm, :]
        lane = lax.broadcasted_iota(jnp.int32, (tm, 128), 1)
        for hp in range(4):
            cc_ref[hp] = c if hp == 0 else pltpu.roll(c, 128 - 2 * hp, 1)
            aug = jnp.zeros((tm, 128), F32)
            for e in range(2):
                rest = jnp.broadcast_to(c[:, 2 * hp + e:2 * hp + e + 1], (tm, 128))
                for part in range(3):
                    piece = rest.astype(BF16).astype(F32)
                    aug = jnp.where(lane == HEAD * (1 - e) + part, piece, aug)
                    rest = rest - piece
            ca_ref[hp] = aug.astype(BF16)

    return pl.pallas_call(
        body, name="fox_gate_fwd", grid=(s // tm,),
        in_specs=[pl.BlockSpec((tm, 128), lambda i: (i, 0)), _full((1, 128))],
        out_specs=[pl.BlockSpec((4, tm, 128), lambda i: (0, i, 0))] * 2,
        out_shape=[jax.ShapeDtypeStruct((4, s, 128), F32), jax.ShapeDtypeStruct((4, s, 128), BF16)],
        scratch_shapes=[pltpu.VMEM((1, 128), F32)],
        compiler_params=_params(("arbitrary",)),
    )(proj, b_forget128)


def _memkv_fwd(mem, g_mem, w_kv, kn_mem):
    m = mem.shape[0]

    def body(mem_ref, g_ref, w_ref, kn_ref, memn_ref, kv_ref, mk_ref, mv_ref):
        xv = mem_ref[...]
        r = lax.rsqrt(jnp.mean(xv * xv, axis=-1, keepdims=True) + EPS)
        mn = (xv * r * g_ref[...]).astype(BF16)
        memn_ref[...] = mn
        kv = _dot(mn, w_ref[...])
        kv_ref[...] = kv
        for h in range(MEM_HEADS):
            v = kv[:, h * 128:(h + 1) * 128]
            rr = lax.rsqrt(jnp.mean(v * v, axis=-1, keepdims=True) + EPS)
            mk_ref[:, h * 128:(h + 1) * 128] = (v * rr * kn_ref[...]).astype(BF16)
        mv_ref[...] = kv[:, 512:1024].astype(BF16)

    return pl.pallas_call(
        body, name="memkv_fwd",
        out_shape=[jax.ShapeDtypeStruct((m, D_MODEL), BF16), jax.ShapeDtypeStruct((m, 1024), F32),
                   jax.ShapeDtypeStruct((m, 512), BF16), jax.ShapeDtypeStruct((m, 512), BF16)],
        compiler_params=pltpu.CompilerParams(vmem_limit_bytes=VMEM_LIMIT),
    )(mem, g_mem, w_kv, kn_mem)


def _bias_table(rel_bias, bucket):
    def body(rb_ref, bk_ref, o_ref):
        bk = bk_ref[...]
        for h in range(SWA_HEADS):
            acc = jnp.zeros(bk.shape, F32)
            for b in range(REL_BUCKETS):
                acc = jnp.where(bk == b, rb_ref[b, h], acc)
            o_ref[h] = acc

    return pl.pallas_call(
        body, name="bias_table",
        in_specs=[pl.BlockSpec(memory_space=pltpu.SMEM), pl.BlockSpec(memory_space=pltpu.VMEM)],
        out_shape=jax.ShapeDtypeStruct((SWA_HEADS,) + bucket.shape, F32),
    )(rel_bias, bucket)


def _swa_valid(n):
    row = lax.broadcasted_iota(jnp.int32, (SWA_BLOCK, 2 * SWA_BLOCK), 0)
    col = lax.broadcasted_iota(jnp.int32, (SWA_BLOCK, 2 * SWA_BLOCK), 1)
    dist = row + SWA_BLOCK - col
    return (dist >= 0) & (dist < SWA_BLOCK) & ((col >= SWA_BLOCK) | (n > 0))


def _swa_fwd(qa, kp, vp, bias, sink):
    s = qa.shape[0]
    nb = s // SWA_BLOCK

    def body(sink_ref, q_ref, kp_ref, vp_ref, bias_ref, o_ref):
        n = pl.program_id(0)
        start = pl.multiple_of(n * SWA_BLOCK, SWA_BLOCK)
        k2 = kp_ref[pl.ds(start, 2 * SWA_BLOCK), :]
        v2 = vp_ref[pl.ds(start, 2 * SWA_BLOCK), :]
        valid = _swa_valid(n)
        heads = range(SWA_HEADS)
        hs = lambda h: slice(h * HEAD, (h + 1) * HEAD)
        sc = [jnp.where(valid, _dot(q_ref[:, hs(h)], k2[:, hs(h // 4)], NT) * 0.125 + bias_ref[h], NEG) for h in heads]
        pn = []
        for h in heads:
            sk = sink_ref[h]
            mx = jnp.maximum(jnp.max(sc[h], axis=-1, keepdims=True), sk)
            p = jnp.exp(sc[h] - mx)
            den = jnp.sum(p, axis=-1, keepdims=True) + jnp.exp(sk - mx)
            pn.append((p / den).astype(BF16))
        outs = [_dot(pn[h], v2[:, hs(h // 4)]).astype(BF16) for h in heads]
        for h in heads:
            o_ref[:, hs(h)] = outs[h]

    return pl.pallas_call(
        body, name="swa_fwd", grid=(nb,),
        in_specs=[pl.BlockSpec(memory_space=pltpu.SMEM),
                  pl.BlockSpec((SWA_BLOCK, 512), lambda n: (n, 0)),
                  _full(kp.shape), _full(vp.shape), _full(bias.shape)],
        out_specs=pl.BlockSpec((SWA_BLOCK, 512), lambda n: (n, 0)),
        out_shape=jax.ShapeDtypeStruct((s, 512), BF16),
        compiler_params=_params(("parallel",)),
    )(sink, qa, kp, vp, bias)


def _head_mask(e):
    lane = lax.broadcasted_iota(jnp.int32, (1, 128), 1)
    return (lane >= e * HEAD) & (lane < (e + 1) * HEAD)


FOX_FWD_T = 1024
FOX_BWD_T = 512


def _head_rows(e):
    row = lax.broadcasted_iota(jnp.int32, (128, 1), 0)
    return (row >= e * HEAD) & (row < (e + 1) * HEAD)


def _fox_fwd(q, k, v_t, ca4):
    s = q.shape[0]
    t = min(FOX_FWD_T, s)
    nq = s // t

    def body(q_ref, k_ref, vt_ref, ca_ref, o_ref, lse_ref, ot_ref):
        i = pl.program_id(1)
        qs = q_ref[...] * jnp.asarray(0.125, BF16)
        lane = lax.broadcasted_iota(jnp.int32, (1, 128), 1)
        minus = [jnp.where((lane >= HEAD * (1 - e)) & (lane < HEAD * (1 - e) + 3), -1.0, 0.0).astype(BF16) for e in range(2)]
        qe = [jnp.where(_head_mask(e), qs, jnp.broadcast_to(minus[e], qs.shape)) for e in range(2)]

        def block(carry, key0, nkeys, q0, nqs, masked):
            ks = pl.ds(pl.multiple_of(key0, 128), nkeys)
            kj = k_ref[ks, :]
            caj = ca_ref[0, ks, :]
            vtj = vt_ref[:, ks]
            out = []
            for e in range(2):
                m_all, acc_all = carry[2 * e], carry[2 * e + 1]
                m, acc = m_all[:, q0:q0 + nqs], acc_all[:, q0:q0 + nqs]
                st = _dot(jnp.where(_head_mask(e), kj, caj), qe[e][q0:q0 + nqs, :], NT)
                if masked:
                    krow = lax.broadcasted_iota(jnp.int32, (nkeys, nqs), 0) + key0
                    qcol = lax.broadcasted_iota(jnp.int32, (nkeys, nqs), 1) + (i * t + q0)
                    st = jnp.where(krow <= qcol, st, NEG)
                m_new = jnp.maximum(m, jnp.max(st, axis=0, keepdims=True))
                alpha = jnp.exp(m - m_new)
                pt = jnp.exp(st - m_new).astype(BF16)
                vte = jnp.where(_head_rows(e), vtj, jnp.ones_like(vtj))
                acc_new = alpha * acc + _dot(vte, pt)
                if nqs < t:
                    m_new = jnp.concatenate([m_all[:, :q0], m_new], axis=1)
                    acc_new = jnp.concatenate([acc_all[:, :q0], acc_new], axis=1)
                out += [m_new, acc_new]
            return tuple(out)

        half = t // 2
        init = (jnp.full((1, t), NEG, F32), jnp.zeros((128, t), F32)) * 2
        carry = lax.fori_loop(0, i, lambda j, c: block(c, j * t, t, 0, t, False), init)
        carry = block(carry, i * t, half, 0, t, True)
        m0, a0, m1, a1 = block(carry, i * t + half, half, half, half, True)
        l0 = a0[HEAD:HEAD + 1, :]
        l1 = a1[0:1, :]
        o_t = jnp.where(_head_rows(0), a0 / l0, a1 / l1)
        o_ref[...] = o_t.T.astype(BF16)
        ot_ref[...] = o_t.astype(BF16)
        r8 = lax.broadcasted_iota(jnp.int32, (8, t), 0)
        lse_ref[0] = jnp.where(r8 == 0, m0 + jnp.log(l0), jnp.where(r8 == 1, m1 + jnp.log(l1), 0.0))

    return pl.pallas_call(
        body, name="fox_fwd", grid=(4, nq),
        in_specs=[pl.BlockSpec((t, 128), lambda hp, i: (i, hp)),
                  pl.BlockSpec((s, 128), lambda hp, i: (0, hp)),
                  pl.BlockSpec((128, s), lambda hp, i: (hp, 0)),
                  pl.BlockSpec((1, s, 128), lambda hp, i: (hp, 0, 0))],
        out_specs=[pl.BlockSpec((t, 128), lambda hp, i: (i, hp)),
                   pl.BlockSpec((1, 8, t), lambda hp, i: (hp, 0, i)),
                   pl.BlockSpec((128, t), lambda hp, i: (hp, i))],
        out_shape=[jax.ShapeDtypeStruct((s, 512), BF16), jax.ShapeDtypeStruct((4, 8, s), F32),
                   jax.ShapeDtypeStruct((512, s), BF16)],
        compiler_params=_params(("parallel", "parallel")),
    )(q, k, v_t, ca4)


MEM_SCALE = MEM_HEAD ** -0.5


def _mem_fwd(qm, mk, mv):
    s = qm.shape[0]
    tq = min(512, s)

    def body(q_ref, mk_ref, mv_ref, o_ref):
        for h in range(MEM_HEADS):
            hs = slice(h * 128, (h + 1) * 128)
            sc = _dot(q_ref[:, hs], mk_ref[:, hs], NT) * MEM_SCALE
            mx = jnp.max(sc, axis=-1, keepdims=True)
            p = jnp.exp(sc - mx)
            p = p / jnp.sum(p, axis=-1, keepdims=True)
            o_ref[:, hs] = _dot(p.astype(BF16), mv_ref[:, hs]).astype(BF16)

    return pl.pallas_call(
        body, name="mem_fwd", grid=(s // tq,),
        in_specs=[pl.BlockSpec((tq, 512), lambda i: (i, 0)), _full(mk.shape), _full(mv.shape)],
        out_specs=pl.BlockSpec((tq, 512), lambda i: (i, 0)),
        out_shape=jax.ShapeDtypeStruct((s, 512), BF16),
        compiler_params=_params(("parallel",)),
    )(qm, mk, mv)


def _merge_fwd(x, oa, of, om, proj, b_gate, wa, wf, wm, w_out, g_mlp):
    s = x.shape[0]
    tm = min(256, s)

    def body(x_ref, oa_ref, of_ref, om_ref, gl_ref, bg_ref, wa_ref, wf_ref, wm_ref, wo_ref, g_ref, x1_ref, hm_ref, mg_ref):
        merged = None
        for b, (o_ref, w_ref) in enumerate(((oa_ref, wa_ref), (of_ref, wf_ref), (om_ref, wm_ref))):
            cs = slice(b * D_MODEL, (b + 1) * D_MODEL)
            y = _dot(o_ref[...], w_ref[...])
            t = _sigmoid(gl_ref[:, cs].astype(F32) + bg_ref[:, cs]) * y
            merged = t if merged is None else merged + t
        mb = merged.astype(BF16)
        mg_ref[...] = mb
        x1 = x_ref[...] + _dot(mb, wo_ref[...])
        x1_ref[...] = x1
        r = lax.rsqrt(jnp.mean(x1 * x1, axis=-1, keepdims=True) + EPS)
        hm_ref[...] = (x1 * r * g_ref[...]).astype(BF16)

    row = lambda w: pl.BlockSpec((tm, w), lambda i: (i, 0))
    return pl.pallas_call(
        body, name="merge_fwd", grid=(s // tm,),
        in_specs=[row(D_MODEL), row(512), row(512), row(512), row(HALF_W), _full((1, HALF_W)),
                  _full(wa.shape), _full(wf.shape), _full(wm.shape), _full(w_out.shape), _full((1, D_MODEL))],
        out_specs=[row(D_MODEL), row(D_MODEL), row(D_MODEL)],
        out_shape=[jax.ShapeDtypeStruct((s, D_MODEL), F32), jax.ShapeDtypeStruct((s, D_MODEL), BF16),
                   jax.ShapeDtypeStruct((s, D_MODEL), BF16)],
        compiler_params=_params(("parallel",)),
    )(x, oa, of, om, proj, b_gate, wa, wf, wm, w_out, g_mlp)


def _mlp_up(hm, w_up):
    s = hm.shape[0]
    tm, tn = min(1024, s), w_up.shape[2]

    def body(h_ref, w_ref, u_ref):
        r = jnp.maximum(_dot(h_ref[...], w_ref[0]), 0.0)
        u_ref[...] = (r * r).astype(BF16)

    return pl.pallas_call(
        body, name="mlp_up", grid=(s // tm, D_FF // tn),
        in_specs=[pl.BlockSpec((tm, D_MODEL), lambda i, j: (i, 0)), pl.BlockSpec((1, D_MODEL, tn), lambda i, j: (j, 0, 0))],
        out_specs=pl.BlockSpec((tm, tn), lambda i, j: (i, j)),
        out_shape=jax.ShapeDtypeStruct((s, D_FF), BF16),
        compiler_params=_params(("parallel", "parallel")),
    )(hm, w_up)


def _mlp_down_loss(u, w_down, x1, target):
    s = u.shape[0]
    tm = min(256, s)

    def body(u_ref, w_ref, x1_ref, t_ref, dy_ref, dyb_ref, loss_ref):
        i = pl.program_id(0)

        @pl.when(i == 0)
        def _():
            loss_ref[...] = jnp.zeros_like(loss_ref)

        y = x1_ref[...] + _dot(u_ref[...], w_ref[...])
        err = y - t_ref[...]
        dy = err * (1.0 / D_MODEL)
        dy_ref[...] = dy
        dyb_ref[...] = dy.astype(BF16)
        part = jnp.sum(jnp.sum(err * err, axis=-1, keepdims=True) * (1.0 / D_MODEL), axis=0, keepdims=True)
        loss_ref[...] += 0.5 * part

    row = pl.BlockSpec((tm, D_MODEL), lambda i: (i, 0))
    return pl.pallas_call(
        body, name="mlp_down_loss", grid=(s // tm,),
        in_specs=[pl.BlockSpec((tm, D_FF), lambda i: (i, 0)), _full(w_down.shape), row, row],
        out_specs=[row, row, _full((1, 1))],
        out_shape=[jax.ShapeDtypeStruct((s, D_MODEL), F32), jax.ShapeDtypeStruct((s, D_MODEL), BF16),
                   jax.ShapeDtypeStruct((1, 1), F32)],
        compiler_params=_params(("arbitrary",)),
    )(u, w_down, x1, target)


def _mlp_bwd_act(dy, w_down, u):
    s = dy.shape[0]
    tm, tn = min(1024, s), 1024

    def body(dy_ref, w_ref, u_ref, da_ref):
        du = _dot(dy_ref[...], w_ref[...], NT)
        da_ref[...] = (du * (2.0 * jnp.sqrt(u_ref[...].astype(F32)))).astype(BF16)

    return pl.pallas_call(
        body, name="mlp_bwd_act", grid=(D_FF // tn, s // tm),
        in_specs=[pl.BlockSpec((tm, D_MODEL), lambda j, i: (i, 0)), pl.BlockSpec((tn, D_MODEL), lambda j, i: (j, 0)),
                  pl.BlockSpec((tm, tn), lambda j, i: (i, j))],
        out_specs=pl.BlockSpec((tm, tn), lambda j, i: (i, j)),
        out_shape=jax.ShapeDtypeStruct((s, D_FF), BF16),
        compiler_params=_params(("parallel", "parallel")),
    )(dy, w_down, u)


def _rms_bwd(xv, g, dh, skip):
    r = lax.rsqrt(jnp.mean(xv * xv, axis=-1, keepdims=True) + EPS)
    n = xv * r
    dn = dh * g
    dx = skip + r * (dn - n * jnp.mean(dn * n, axis=-1, keepdims=True))
    return dx, jnp.sum(dh * n, axis=0, keepdims=True)


def _mlp_bwd_x(da, w_up, x1, dy, g_mlp):
    s = da.shape[0]
    tm = min(256, s)

    def body(da_ref, w_ref, x1_ref, dy_ref, g_ref, dx1_ref, dg_ref):
        i = pl.program_id(0)

        @pl.when(i == 0)
        def _():
            dg_ref[...] = jnp.zeros_like(dg_ref)

        tn = w_ref.shape[2]
        dhm = _dot(da_ref[:, 0:tn], w_ref[0], NT)
        for j in range(1, N_DEV):
            dhm = dhm + _dot(da_ref[:, j * tn:(j + 1) * tn], w_ref[j], NT)
        dx, dg = _rms_bwd(x1_ref[...], g_ref[...], dhm, dy_ref[...])
        dx1_ref[...] = dx
        dg_ref[...] += dg

    row = pl.BlockSpec((tm, D_MODEL), lambda i: (i, 0))
    return pl.pallas_call(
        body, name="mlp_bwd_x", grid=(s // tm,),
        in_specs=[pl.BlockSpec((tm, D_FF), lambda i: (i, 0)), _full(w_up.shape), row, row, _full((1, D_MODEL))],
        out_specs=[row, _full((1, D_MODEL))],
        out_shape=[jax.ShapeDtypeStruct((s, D_MODEL), F32), jax.ShapeDtypeStruct((1, D_MODEL), F32)],
        compiler_params=_params(("arbitrary",)),
    )(da, w_up, x1, dy, g_mlp)


def _merge_bwd(dx1, oa, of, om, proj, b_gate, wa, wf, wm, w_out):
    s = dx1.shape[0]
    tm = min(256, s)

    def body(dx1_ref, oa_ref, of_ref, om_ref, gl_ref, bg_ref, wa_ref, wf_ref, wm_ref, wo_ref,
             dp_ref, doa_ref, dof_ref, dom_ref, dya_ref, dyf_ref, dym_ref, dbg_ref):
        i = pl.program_id(0)

        @pl.when(i == 0)
        def _():
            dbg_ref[...] = jnp.zeros_like(dbg_ref)

        dmerged = _dot(dx1_ref[...].astype(BF16), wo_ref[...], NT)
        branches = ((oa_ref, wa_ref, doa_ref, dya_ref), (of_ref, wf_ref, dof_ref, dyf_ref), (om_ref, wm_ref, dom_ref, dym_ref))
        for b, (o_ref, w_ref, do_ref, dyb_ref) in enumerate(branches):
            cs = slice(b * D_MODEL, (b + 1) * D_MODEL)
            y = _dot(o_ref[...], w_ref[...])
            g = _sigmoid(gl_ref[:, cs].astype(F32) + bg_ref[:, cs])
            dz = (dmerged * y) * g * (1.0 - g)
            dp_ref[:, cs] = dz.astype(BF16)
            dbg_ref[:, cs] += jnp.sum(dz, axis=0, keepdims=True)
            dyb = (dmerged * g).astype(BF16)
            dyb_ref[...] = dyb
            do = _dot(dyb, w_ref[...], NT)
            do_ref[...] = (do.T if b == 1 else do).astype(BF16)

    row = lambda w: pl.BlockSpec((tm, w), lambda i: (i, 0))
    sd = lambda w: jax.ShapeDtypeStruct((s, w), BF16)
    return pl.pallas_call(
        body, name="merge_bwd", grid=(s // tm,),
        in_specs=[row(D_MODEL), row(512), row(512), row(512), row(HALF_W), _full((1, HALF_W)),
                  _full(wa.shape), _full(wf.shape), _full(wm.shape), _full(w_out.shape)],
        out_specs=[row(HALF_W), row(512), pl.BlockSpec((512, tm), lambda i: (0, i)), row(512),
                   row(D_MODEL), row(D_MODEL), row(D_MODEL), _full((1, HALF_W))],
        out_shape=[sd(PROJ_W), sd(512), jax.ShapeDtypeStruct((512, s), BF16), sd(512), sd(D_MODEL), sd(D_MODEL), sd(D_MODEL),
                   jax.ShapeDtypeStruct((1, HALF_W), F32)],
        compiler_params=_params(("arbitrary",)),
    )(dx1, oa, of, om, proj, b_gate, wa, wf, wm, w_out)


def _swa_valid_t(n):
    key = lax.broadcasted_iota(jnp.int32, (2 * SWA_BLOCK, SWA_BLOCK), 0)
    qry = lax.broadcasted_iota(jnp.int32, (2 * SWA_BLOCK, SWA_BLOCK), 1)
    dist = qry + SWA_BLOCK - key
    return (dist >= 0) & (dist < SWA_BLOCK) & ((key >= SWA_BLOCK) | (n > 0))


def _swa_bwd(qa, kp, vp, bias_t, sink, doa):
    s = qa.shape[0]
    nb = s // SWA_BLOCK

    def body(sink_ref, q_ref, kp_ref, vp_ref, bias_ref, do_ref, dq_ref, dkp_ref, dvp_ref, dbias_ref, dsink_ref, sk_acc):
        n = pl.program_id(0)

        @pl.when(n == 0)
        def _():
            dkp_ref[...] = jnp.zeros_like(dkp_ref)
            dvp_ref[...] = jnp.zeros_like(dvp_ref)
            dbias_ref[...] = jnp.zeros_like(dbias_ref)
            sk_acc[...] = jnp.zeros_like(sk_acc)

        start = pl.multiple_of(n * SWA_BLOCK, SWA_BLOCK)
        win = pl.ds(start, 2 * SWA_BLOCK)
        k2 = kp_ref[win, :]
        v2 = vp_ref[win, :]
        valid = _swa_valid_t(n)
        heads = range(SWA_HEADS)
        hs = lambda h: slice(h * HEAD, (h + 1) * HEAD)
        scale = jnp.asarray(0.125, BF16)
        q = [q_ref[:, hs(h)] for h in heads]
        do = [do_ref[:, hs(h)] for h in heads]
        kk = [k2[:, hs(kv)] for kv in range(2)]
        vv = [v2[:, hs(kv)] for kv in range(2)]
        kt = [(kk[kv].astype(F32) * 0.125).T.astype(BF16) for kv in range(2)]
        st = [jnp.where(valid, _dot(kk[h // 4], q[h], NT) * 0.125 + bias_ref[h], NEG) for h in heads]
        dpt = [_dot(vv[h // 4], do[h], NT) for h in heads]
        pt, dst = [], []
        for h in heads:
            sk = sink_ref[h]
            mx = jnp.maximum(jnp.max(st[h], axis=0, keepdims=True), sk)
            p = jnp.exp(st[h] - mx)
            esk = jnp.exp(sk - mx)
            den = jnp.sum(p, axis=0, keepdims=True) + esk
            p = p / den
            delta = jnp.sum(p * dpt[h], axis=0, keepdims=True)
            d = p * (dpt[h] - delta)
            sk_acc[h:h + 1, :] += -(esk / den) * delta
            dbias_ref[h] += d
            pt.append(p.astype(BF16))
            dst.append(d.astype(BF16))
        dq_t = [_dot(kt[h // 4], dst[h]) for h in heads]
        dq_ref[...] = jnp.concatenate(dq_t, axis=0).T.astype(BF16)
        for kv in range(2):
            group = range(4 * kv, 4 * kv + 4)
            dk = [_dot(dst[h], q[h] * scale) for h in group]
            dv = [_dot(pt[h], do[h]) for h in group]
            dkp_ref[win, hs(kv)] += (dk[0] + dk[1]) + (dk[2] + dk[3])
            dvp_ref[win, hs(kv)] += (dv[0] + dv[1]) + (dv[2] + dv[3])

        @pl.when(n == nb - 1)
        def _():
            dsink_ref[...] = jnp.broadcast_to(jnp.sum(sk_acc[...], axis=1, keepdims=True), dsink_ref.shape)

    return pl.pallas_call(
        body, name="swa_bwd", grid=(nb,),
        in_specs=[pl.BlockSpec(memory_space=pltpu.SMEM),
                  pl.BlockSpec((SWA_BLOCK, 512), lambda n: (n, 0)),
                  _full(kp.shape), _full(vp.shape), _full(bias_t.shape),
                  pl.BlockSpec((SWA_BLOCK, 512), lambda n: (n, 0))],
        out_specs=[pl.BlockSpec((SWA_BLOCK, 512), lambda n: (n, 0)), _full(kp.shape), _full(vp.shape),
                   _full(bias_t.shape), _full((SWA_HEADS, 128))],
        out_shape=[jax.ShapeDtypeStruct((s, 512), BF16), jax.ShapeDtypeStruct(kp.shape, F32),
                   jax.ShapeDtypeStruct(vp.shape, F32), jax.ShapeDtypeStruct(bias_t.shape, F32),
                   jax.ShapeDtypeStruct((SWA_HEADS, 128), F32)],
        scratch_shapes=[pltpu.VMEM((SWA_HEADS, 128), F32)],
        compiler_params=_params(("arbitrary",)),
    )(sink, qa, kp, vp, bias_t, doa)


def _fox_bwd(qt, k, v, dot, ot, cc4, lse4):
    s = k.shape[0]
    t = min(FOX_BWD_T, s)
    nq = s // t

    def body(qt_ref, k_ref, v_ref, dot_ref, ot_ref, cc_ref, lse_ref,
             dqt_ref, dk_ref, dv_ref, dck_ref, dcq_ref, delta_ref, dk0, dk1, dv0, dv1, ds0, ds1):
        j = pl.program_id(1)

        @pl.when(j == 0)
        def _():
            dqt_ref[...] = jnp.zeros_like(dqt_ref)
            dcq_ref[...] = jnp.zeros_like(dcq_ref)
            r8 = lax.broadcasted_iota(jnp.int32, (8, t), 0)

            def dl(i, c):
                cols = pl.ds(pl.multiple_of(i * t, t), t)
                pr = dot_ref[:, cols].astype(F32) * ot_ref[:, cols].astype(F32)
                d0 = jnp.sum(jnp.where(_head_rows(0), pr, 0.0), axis=0, keepdims=True)
                d1 = jnp.sum(jnp.where(_head_rows(1), pr, 0.0), axis=0, keepdims=True)
                delta_ref[:, cols] = jnp.where(r8 == 0, d0, jnp.where(r8 == 1, d1, 0.0))
                return c

            lax.fori_loop(0, nq, dl, 0)

        kj = k_ref[...]
        vj = v_ref[...]
        ks = pl.ds(pl.multiple_of(j * t, t), t)
        kt = (kj.astype(F32) * 0.125).T.astype(BF16)
        ke = [jnp.where(_head_mask(e), kj, jnp.zeros_like(kj)) for e in range(2)]
        ve = [jnp.where(_head_mask(e), vj, jnp.zeros_like(vj)) for e in range(2)]
        kte = [jnp.where(_head_rows(e), kt, jnp.zeros_like(kt)) for e in range(2)]
        ck = [cc_ref[0, ks, e:e + 1] for e in range(2)]
        accs = ((dk0, dv0, ds0), (dk1, dv1, ds1))
        for refs in accs:
            for r in refs:
                r[...] = jnp.zeros_like(r)

        def block(q0, nqs, k0, nks, masked):
            cols = pl.ds(pl.multiple_of(q0, 128), nqs)
            rows = slice(k0, k0 + nks)
            qti = qt_ref[:, cols]
            doti = dot_ref[:, cols]
            for e in range(2):
                dkt_acc, dvt_acc, ds_acc = accs[e]
                st = _dot(ke[e][rows, :], qti) - ck[e][rows, :]
                if masked:
                    krow = lax.broadcasted_iota(jnp.int32, (nks, nqs), 0) + (j * t + k0)
                    qcol = lax.broadcasted_iota(jnp.int32, (nks, nqs), 1) + q0
                    st = jnp.where(krow <= qcol, st, NEG)
                pt = jnp.exp(st - lse_ref[0, e:e + 1, cols])
                dpt = _dot(ve[e][rows, :], doti)
                dst = pt * (dpt - delta_ref[e:e + 1, cols])
                dsb = dst.astype(BF16)
                dvt_acc[:, rows] += _dot(doti, pt.astype(BF16), NT)
                dkt_acc[:, rows] += _dot(qti, dsb, NT)
                dqt_ref[:, cols] += _dot(kte[e][:, rows], dsb)
                ds_acc[rows, 0:nqs] += dst
                dcq_ref[0, e:e + 1, cols] += jnp.sum(dst, axis=0, keepdims=True)

        half = t // 2
        block(j * t, half, 0, half, True)
        block(j * t + half, half, 0, t, True)

        def rest(i, c):
            block(i * t, t, 0, t, False)
            return c

        lax.fori_loop(j + 1, nq, rest, 0)
        r0 = _head_rows(0)
        dk_ref[...] = jnp.where(r0, dk0[...], dk1[...]).T.astype(BF16)
        dv_ref[...] = jnp.where(r0, dv0[...], dv1[...]).T.astype(BF16)
        ones = jnp.ones((8, t), F32)
        sums = [lax.dot_general(ones, acc[...], NT, precision=lax.Precision.HIGHEST, preferred_element_type=F32)
                for acc in (ds0, ds1)]
        r8 = lax.broadcasted_iota(jnp.int32, (8, t), 0)
        dck_ref[0] = jnp.where(r8 == 0, sums[0], jnp.where(r8 == 1, sums[1], 0.0))

    res_t = lambda: pl.BlockSpec((128, s), lambda hp, j: (hp, 0))
    blk = lambda: pl.BlockSpec((t, 128), lambda hp, j: (j, hp))
    return pl.pallas_call(
        body, name="fox_bwd", grid=(4, nq),
        in_specs=[res_t(), blk(), blk(), res_t(), res_t(), pl.BlockSpec((1, s, 128), lambda hp, j: (hp, 0, 0)),
                  pl.BlockSpec((1, 8, s), lambda hp, j: (hp, 0, 0))],
        out_specs=[res_t(), blk(), blk(),
                   pl.BlockSpec((1, 8, t), lambda hp, j: (hp, 0, j)),
                   pl.BlockSpec((1, 8, s), lambda hp, j: (hp, 0, 0))],
        out_shape=[jax.ShapeDtypeStruct((512, s), F32), jax.ShapeDtypeStruct((s, 512), BF16),
                   jax.ShapeDtypeStruct((s, 512), BF16), jax.ShapeDtypeStruct((4, 8, s), F32),
                   jax.ShapeDtypeStruct((4, 8, s), F32)],
        scratch_shapes=[pltpu.VMEM((8, s), F32)] + [pltpu.VMEM((128, t), F32)] * 4 + [pltpu.VMEM((t, t), F32)] * 2,
        compiler_params=_params(("arbitrary", "arbitrary")),
    )(qt, k, v, dot, ot, cc4, lse4)


def _mem_bwd(qm, mk, mv, dom):
    s = qm.shape[0]
    tq = min(512, s)

    def body(q_ref, mk_ref, mv_ref, do_ref, dq_ref, dmk_ref, dmv_ref):
        i = pl.program_id(0)

        @pl.when(i == 0)
        def _():
            dmk_ref[...] = jnp.zeros_like(dmk_ref)
            dmv_ref[...] = jnp.zeros_like(dmv_ref)

        heads = range(MEM_HEADS)
        hs = lambda h: slice(h * 128, (h + 1) * 128)
        sc = [_dot(q_ref[:, hs(h)], mk_ref[:, hs(h)], NT) * MEM_SCALE for h in heads]
        dp = [_dot(do_ref[:, hs(h)], mv_ref[:, hs(h)], NT) for h in heads]
        pb, dsb = [], []
        for h in heads:
            p = jnp.exp(sc[h] - jnp.max(sc[h], axis=-1, keepdims=True))
            p = p / jnp.sum(p, axis=-1, keepdims=True)
            ds = p * (dp[h] - jnp.sum(p * dp[h], axis=-1, keepdims=True))
            pb.append(p.astype(BF16))
            dsb.append((ds * MEM_SCALE).astype(BF16))
        dq = [_dot(dsb[h], mk_ref[:, hs(h)]).astype(BF16) for h in heads]
        dmk = [_dot(dsb[h], q_ref[:, hs(h)], TN) for h in heads]
        dmv = [_dot(pb[h], do_ref[:, hs(h)], TN) for h in heads]
        for h in heads:
            dq_ref[:, hs(h)] = dq[h]
            dmk_ref[:, hs(h)] += dmk[h]
            dmv_ref[:, hs(h)] += dmv[h]

    return pl.pallas_call(
        body, name="mem_bwd", grid=(s // tq,),
        in_specs=[pl.BlockSpec((tq, 512), lambda i: (i, 0)), _full(mk.shape), _full(mv.shape),
                  pl.BlockSpec((tq, 512), lambda i: (i, 0))],
        out_specs=[pl.BlockSpec((tq, 512), lambda i: (i, 0)), _full(mk.shape), _full(mv.shape)],
        out_shape=[jax.ShapeDtypeStruct((s, 512), BF16), jax.ShapeDtypeStruct(mk.shape, F32),
                   jax.ShapeDtypeStruct(mv.shape, F32)],
        compiler_params=_params(("arbitrary",)),
    )(qm, mk, mv, dom)


def _memkv_bwd(dmk, dmv, kv_raw, kn_mem, mem, g_mem, mem_n, w_kv):
    def body(dmk_ref, dmv_ref, kv_ref, kn_ref, mem_ref, g_ref, mn_ref, w_ref, dw_ref, dkn_ref, dg_ref, dkv_ref):
        dkn = jnp.zeros((1, 128), F32)
        for h in range(MEM_HEADS):
            hs = slice(h * 128, (h + 1) * 128)
            v = kv_ref[:, hs]
            r = lax.rsqrt(jnp.mean(v * v, axis=-1, keepdims=True) + EPS)
            n = v * r
            dn = dmk_ref[:, hs]
            dkn = dkn + jnp.sum(dn * n, axis=0, keepdims=True)
            dng = dn * kn_ref[...]
            dkv_ref[:, hs] = (r * (dng - n * jnp.mean(dng * n, axis=-1, keepdims=True))).astype(BF16)
        dkv_ref[:, 512:1024] = dmv_ref[...].astype(BF16)
        dkn_ref[...] = dkn
        dkv = dkv_ref[...]
        dw_ref[...] = _dot(mn_ref[...], dkv, TN).astype(BF16)
        dmn = _dot(dkv, w_ref[...], NT)
        xv = mem_ref[...]
        r = lax.rsqrt(jnp.mean(xv * xv, axis=-1, keepdims=True) + EPS)
        dg_ref[...] = jnp.sum(dmn * (xv * r), axis=0, keepdims=True)

    m = mem.shape[0]
    return pl.pallas_call(
        body, name="memkv_bwd",
        out_shape=[jax.ShapeDtypeStruct((D_MODEL, 1024), BF16), jax.ShapeDtypeStruct((1, 128), F32),
                   jax.ShapeDtypeStruct((1, D_MODEL), F32)],
        scratch_shapes=[pltpu.VMEM((m, 1024), BF16)],
        compiler_params=pltpu.CompilerParams(vmem_limit_bytes=VMEM_LIMIT),
    )(dmk, dmv, kv_raw, kn_mem, mem, g_mem, mem_n, w_kv)


def _fox_gate_bwd(dc, proj, b_forget128):
    s = dc.shape[0]
    tm = min(512, s)
    nt = s // tm

    def body(dc_ref, p_ref, b_ref, dfl_ref, db_ref, carry_ref):
        i = pl.program_id(0)

        @pl.when(i == 0)
        def _():
            carry_ref[...] = jnp.zeros_like(carry_ref)
            db_ref[...] = jnp.zeros_like(db_ref)

        dcv = dc_ref[...]
        dlogf = jnp.dot(_tri(tm, False), dcv, precision=lax.Precision.HIGHEST, preferred_element_type=F32) + carry_ref[...]
        carry_ref[...] += jnp.sum(dcv, axis=0, keepdims=True)
        z = p_ref[...] + b_ref[...]
        dfl = dlogf * (1.0 / (1.0 + jnp.exp(z)))
        dfl_ref[...] = dfl.astype(BF16)
        db_ref[...] += jnp.sum(dfl, axis=0, keepdims=True)

    return pl.pallas_call(
        body, name="fox_gate_bwd", grid=(nt,),
        in_specs=[pl.BlockSpec((tm, 128), lambda i: (nt - 1 - i, 0)),
                  pl.BlockSpec((tm, 128), lambda i: (nt - 1 - i, 0)), _full((1, 128))],
        out_specs=[pl.BlockSpec((tm, 128), lambda i: (nt - 1 - i, 0)), _full((1, 128))],
        out_shape=[jax.ShapeDtypeStruct((s, 128), BF16), jax.ShapeDtypeStruct((1, 128), F32)],
        scratch_shapes=[pltpu.VMEM((1, 128), F32)],
        compiler_params=_params(("arbitrary",)),
    )(dc, proj, b_forget128)


def _proj_pre_bwd(dproj, proj, dqf, dkf, dvf, dqm, dqa, dka, dva, dfl, gq_fox, gk_fox, gq_mem, gq_swa, gk_swa):
    s = proj.shape[0]
    tm = min(256, s)

    def body(dp_in, p_ref, dqf_ref, dkf_ref, dvf_ref, dqm_ref, dqa_ref, dka_ref, dva_ref, dfl_ref,
             gqf, gkf, gqm, gqa, gka, dp_ref, dgn_ref):
        i = pl.program_id(0)

        @pl.when(i == 0)
        def _():
            dgn_ref[...] = jnp.zeros_like(dgn_ref)

        def norm_bwd(off, width, hd, g_ref, dn_ref, slot):
            acc = jnp.zeros((1, 128), F32)
            for b in range(width // 128):
                v = p_ref[:, off + b * 128: off + (b + 1) * 128].astype(F32)
                r = lax.rsqrt(_group_mean(v * v, hd) + EPS)
                n = v * r
                dn = dn_ref[b * 128:(b + 1) * 128, :].T if slot == 0 else dn_ref[:, b * 128:(b + 1) * 128].astype(F32)
                acc = acc + jnp.sum(dn * n, axis=0, keepdims=True)
                dng = dn * g_ref[...]
                dp_ref[:, off + b * 128: off + (b + 1) * 128] = (r * (dng - n * _group_mean(dng * n, hd))).astype(BF16)
            dgn_ref[slot:slot + 1, :] += acc

        norm_bwd(H_QF, 512, HEAD, gqf, dqf_ref, 0)
        norm_bwd(H_KF, 512, HEAD, gkf, dkf_ref, 1)
        dp_ref[:, H_VF:H_VF + 512] = dvf_ref[...].astype(BF16)
        norm_bwd(H_QM, 512, MEM_HEAD, gqm, dqm_ref, 2)
        norm_bwd(H_QA, 512, HEAD, gqa, dqa_ref, 3)
        norm_bwd(H_KA, 128, HEAD, gka, dka_ref, 4)
        dp_ref[:, H_VA:H_VA + 128] = dva_ref[...].astype(BF16)
        dp_ref[:, H_FL:H_FL + 128] = dfl_ref[...]
        dp_ref[:, H_FL + 128:HALF_W] = jnp.zeros((tm, HALF_W - H_FL - 128), BF16)

    row = lambda w: pl.BlockSpec((tm, w), lambda i: (i, 0))
    g_spec = _full((1, 128))
    return pl.pallas_call(
        body, name="proj_pre_bwd", grid=(s // tm,),
        in_specs=[pl.BlockSpec(memory_space=pl.ANY), pl.BlockSpec((tm, HALF_W), lambda i: (i, 1)),
                  pl.BlockSpec((512, tm), lambda i: (0, i)), row(512), row(512), row(512), row(512),
                  row(128), row(128), row(128), g_spec, g_spec, g_spec, g_spec, g_spec],
        out_specs=[pl.BlockSpec((tm, HALF_W), lambda i: (i, 1)), _full((8, 128))],
        out_shape=[jax.ShapeDtypeStruct((s, PROJ_W), BF16), jax.ShapeDtypeStruct((8, 128), F32)],
        input_output_aliases={0: 0},
        compiler_params=_params(("arbitrary",)),
    )(dproj, proj, dqf, dkf, dvf, dqm, dqa, dka, dva, dfl, gq_fox, gk_fox, gq_mem, gq_swa, gk_swa)


def _in_bwd_x(dproj, w_in_p, x, g_mix, dx1):
    s = x.shape[0]
    tm = min(256, s)

    def body(dp_ref, w_ref, x_ref, g_ref, dx1_ref, gx_ref, dg_ref):
        i = pl.program_id(0)

        @pl.when(i == 0)
        def _():
            dg_ref[...] = jnp.zeros_like(dg_ref)

        dx, dg = _rms_bwd(x_ref[...], g_ref[...], _dot(dp_ref[...], w_ref[...], NT), dx1_ref[...])
        gx_ref[...] = dx
        dg_ref[...] += dg

    row = pl.BlockSpec((tm, D_MODEL), lambda i: (i, 0))
    return pl.pallas_call(
        body, name="in_bwd_x", grid=(s // tm,),
        in_specs=[pl.BlockSpec((tm, PROJ_W), lambda i: (i, 0)), _full(w_in_p.shape), row, _full((1, D_MODEL)), row],
        out_specs=[row, _full((1, D_MODEL))],
        out_shape=[jax.ShapeDtypeStruct((s, D_MODEL), F32), jax.ShapeDtypeStruct((1, D_MODEL), F32)],
        compiler_params=_params(("arbitrary",)),
    )(dproj, w_in_p, x, g_mix, dx1)


def _rel_bias_bwd(dbias, bucket):
    def body(db_ref, bk_ref, o_ref):
        bk = bk_ref[...]
        lane = lax.broadcasted_iota(jnp.int32, (1, 128), 1)
        for b in range(REL_BUCKETS):
            sel = bk == b
            acc = jnp.zeros((1, 128), F32)
            for h in range(SWA_HEADS):
                tot = jnp.sum(jnp.sum(jnp.where(sel, db_ref[h], 0.0), axis=-1, keepdims=True), axis=0, keepdims=True)
                acc = jnp.where(lane == h, tot, acc)
            o_ref[:, b * 128:(b + 1) * 128] = acc

    return pl.pallas_call(
        body, name="rel_bias_bwd",
        out_shape=jax.ShapeDtypeStruct((1, REL_BUCKETS * 128), F32),
        compiler_params=pltpu.CompilerParams(vmem_limit_bytes=VMEM_LIMIT),
    )(dbias, bucket)


def _my_place():
    return lax.axis_index("x"), lax.axis_index("y"), lax.axis_index("c")


def _peer(place, k):
    x, y, c = place
    return (1 - x if k & 4 else x, 1 - y if k & 2 else y, 1 - c if k & 1 else c)


def _index(place):
    x, y, c = place
    return 4 * x + 2 * y + c


HBM_SPEC = pl.BlockSpec(memory_space=pltpu.HBM)
SEM_SPEC = pl.BlockSpec(memory_space=pltpu.SEMAPHORE)
DATAFLOW = pltpu.SideEffectType.DATAFLOW_SIDE_EFFECTING


ALL_PEERS = tuple(range(1, N_DEV))
SAME_CORE = (2, 4, 6)
OWN = N_DEV - 1


def _split_copy(src_ref, land_ref, send_sems, recv_sems, me, k, gather):
    peer = _peer(me, k)
    if gather:
        src, dst = src_ref, land_ref.at[_index(me)]
    else:
        src, dst = src_ref.at[_index(peer)], land_ref.at[k - 1]
    return pltpu.make_async_remote_copy(src_ref=src, dst_ref=dst, send_sem=send_sems.at[k - 1], recv_sem=recv_sems.at[k - 1],
                                        device_id=peer, device_id_type=MESH)


def _own_copy(src_ref, land_ref, recv_sems, me, gather):
    if gather:
        src, dst = src_ref, land_ref.at[_index(me)]
    else:
        src, dst = src_ref.at[_index(me)], land_ref.at[OWN]
    return pltpu.make_async_copy(src, dst, recv_sems.at[OWN])


def _split_start(srcs, gather, name, peers=ALL_PEERS, after=None):
    n = len(srcs)
    extra = [] if after is None else [after]

    def body(*refs):
        refs = refs[:2 * n] + refs[2 * n + len(extra):]
        src_refs, land_refs = refs[:n], refs[n:2 * n]
        send_sems, recv_sems, token = refs[2 * n:3 * n], refs[3 * n:4 * n], refs[-1]
        me = _my_place()
        for w in range(n):
            for k in peers:
                _split_copy(src_refs[w], land_refs[w], send_sems[w], recv_sems[w], me, k, gather).start()
            _own_copy(src_refs[w], land_refs[w], recv_sems[w], me, gather).start()
        token[...] = jnp.zeros_like(token)

    lands = [lax.empty((N_DEV,) + (a.shape if gather else a.shape[1:]), a.dtype) for a in srcs]
    sems = [pltpu.SemaphoreType.DMA((N_DEV,))] * (2 * n)
    hbm = [pltpu.HBM(a.shape, a.dtype) for a in list(srcs) + lands]
    outs = pl.pallas_call(
        body, name=name,
        out_shape=(*sems, *hbm, jax.ShapeDtypeStruct((8, 128), F32)),
        in_specs=(HBM_SPEC,) * (2 * n) + (pl.BlockSpec(memory_space=pl.ANY),) * len(extra),
        out_specs=(SEM_SPEC,) * (2 * n) + (HBM_SPEC,) * (2 * n) + (pl.BlockSpec(memory_space=pltpu.VMEM),),
        input_output_aliases={i: 2 * n + i for i in range(2 * n)},
        compiler_params=pltpu.CompilerParams(has_side_effects=DATAFLOW),
    )(*[pltpu.with_memory_space_constraint(a, pltpu.HBM) for a in list(srcs) + lands], *extra)
    return list(outs[:n]), list(outs[n:2 * n]), list(outs[2 * n:3 * n]), list(outs[3 * n:4 * n]), outs[-1]


def _split_wait(started, w, after, gather, name):
    send_sems, recv_sems, srcs, lands, _ = started

    def body(src_ref, land_ref, send_sems, recv_sems, after_ref, src_out, land_out):
        me = _my_place()
        for k in ALL_PEERS:
            cp = _split_copy(src_ref, land_ref, send_sems, recv_sems, me, k, gather)
            cp.wait_send()
            cp.wait_recv()
        _own_copy(src_ref, land_ref, recv_sems, me, gather).wait()

    return pl.pallas_call(
        body, name=name,
        out_shape=(pltpu.HBM(srcs[w].shape, srcs[w].dtype), pltpu.HBM(lands[w].shape, lands[w].dtype)),
        in_specs=(HBM_SPEC, HBM_SPEC, SEM_SPEC, SEM_SPEC, pl.BlockSpec(memory_space=pl.ANY)),
        out_specs=(HBM_SPEC, HBM_SPEC), input_output_aliases={0: 0, 1: 1},
        compiler_params=pltpu.CompilerParams(has_side_effects=DATAFLOW),
    )(srcs[w], lands[w], send_sems[w], recv_sems[w], after)[1]


def _forward_copy(land_ref, send_sems, recv_sems, me, j, incoming):
    sibling = _peer(me, 1)
    rows = land_ref.at[_index(_peer(sibling if incoming else me, SAME_CORE[j]))]
    return pltpu.make_async_remote_copy(src_ref=rows, dst_ref=rows, send_sem=send_sems.at[j], recv_sem=recv_sems.at[j],
                                        device_id=sibling, device_id_type=MESH)


def _forward_start(started, after, name):
    send_a, recv_a, srcs, lands, _ = started

    def body(src_ref, land_ref, send_a, recv_a, after_ref, send_b, recv_b, src_out, land_out):
        me = _my_place()
        for j, k in enumerate(SAME_CORE):
            _split_copy(src_ref, land_ref, send_a, recv_a, me, k, True).wait_recv()
            _forward_copy(land_ref, send_b, recv_b, me, j, False).start()

    sems = pltpu.SemaphoreType.DMA((len(SAME_CORE),))
    return pl.pallas_call(
        body, name=name,
        out_shape=(sems, sems, pltpu.HBM(srcs[0].shape, srcs[0].dtype), pltpu.HBM(lands[0].shape, lands[0].dtype)),
        in_specs=(HBM_SPEC, HBM_SPEC, SEM_SPEC, SEM_SPEC, pl.BlockSpec(memory_space=pl.ANY)),
        out_specs=(SEM_SPEC, SEM_SPEC, HBM_SPEC, HBM_SPEC), input_output_aliases={0: 2, 1: 3},
        compiler_params=pltpu.CompilerParams(has_side_effects=DATAFLOW),
    )(srcs[0], lands[0], send_a[0], recv_a[0], after)


def _forward_wait(started, forwarded, name):
    send_a, recv_a, _, _, _ = started
    send_b, recv_b, src, land = forwarded

    def body(src_ref, land_ref, send_a, recv_a, send_b, recv_b, src_out, land_out):
        me = _my_place()
        _own_copy(src_ref, land_ref, recv_a, me, True).wait()
        for k in (1,) + SAME_CORE:
            _split_copy(src_ref, land_ref, send_a, recv_a, me, k, True).wait_send()
        _split_copy(src_ref, land_ref, send_a, recv_a, me, 1, True).wait_recv()
        for j in range(len(SAME_CORE)):
            _forward_copy(land_ref, send_b, recv_b, me, j, False).wait_send()
            _forward_copy(land_ref, send_b, recv_b, me, j, True).wait_recv()

    return pl.pallas_call(
        body, name=name,
        out_shape=(pltpu.HBM(src.shape, src.dtype), pltpu.HBM(land.shape, land.dtype)),
        in_specs=(HBM_SPEC, HBM_SPEC, SEM_SPEC, SEM_SPEC, SEM_SPEC, SEM_SPEC),
        out_specs=(HBM_SPEC, HBM_SPEC), input_output_aliases={0: 0, 1: 1},
        compiler_params=pltpu.CompilerParams(has_side_effects=DATAFLOW),
    )(src, land, send_a[0], recv_a[0], send_b, recv_b)[1]


def _adam_math(w, g, m, v):
    m2 = ADAM_B1 * m + (1.0 - ADAM_B1) * g
    v2 = ADAM_B2 * v + (1.0 - ADAM_B2) * (g * g)
    m_hat = m2 / (1.0 - ADAM_B1 ** ADAM_STEP)
    v_hat = v2 / (1.0 - ADAM_B2 ** ADAM_STEP)
    delta = -ADAM_LR * (m_hat / (jnp.sqrt(v_hat) + ADAM_EPS) + ADAM_WD * w)
    return delta, m2, v2


def _adamw(land, w, m, v, name):
    a, b = w.shape
    bp = land.shape[2]
    ta = min(128, a)

    def body(p_ref, w_ref, m_ref, v_ref, g_ref, d_ref, m2_ref, v2_ref):
        g = p_ref[0, :, 0:b].astype(F32)
        for k in range(1, N_DEV):
            g = g + p_ref[k, :, 0:b].astype(F32)
        delta, m2, v2 = _adam_math(w_ref[...], g, m_ref[...], v_ref[...])
        g_ref[...] = g
        d_ref[...] = delta
        m2_ref[...] = m2
        v2_ref[...] = v2

    blk = pl.BlockSpec((ta, b), lambda i: (i, 0))
    sd = jax.ShapeDtypeStruct((a, b), F32)
    return pl.pallas_call(
        body, name=name, grid=(a // ta,),
        in_specs=[pl.BlockSpec((N_DEV, ta, bp), lambda i: (0, i, 0)), blk, blk, blk],
        out_specs=[blk, blk, blk, blk], out_shape=[sd, sd, sd, sd],
        compiler_params=_params(("parallel",)),
    )(land, w, m, v)


def _bucket_table():
    t_loc = jnp.arange(SWA_BLOCK)[:, None] + SWA_BLOCK
    s_loc = jnp.arange(2 * SWA_BLOCK)[None, :]
    dist = t_loc - s_loc
    max_exact = REL_BUCKETS // 2
    d = jnp.maximum(dist, 0)
    df = jnp.maximum(d, 1).astype(F32)
    large = max_exact + (jnp.log(df / max_exact) / math.log(REL_MAX_DIST / max_exact) * (REL_BUCKETS - max_exact)).astype(jnp.int32)
    large = jnp.minimum(large, REL_BUCKETS - 1)
    bucket = jnp.where(d < max_exact, d, large)
    band = (dist >= 0) & (dist < SWA_BLOCK)
    return bucket, band


def _tile2(g):
    return jnp.concatenate([g, g], axis=1) if g.shape[1] == HEAD else g


SHARD_W = 737
SHARD_WP = 768
IN_WIDTH = N_DEV * SHARD_W
SEGMENTS = ((GL0, 2824, 3072), (QF0, 768, 512), (KF0, 1280, 512), (VF0, 1792, 512), (QM0, 2312, 512),
            (QA0, 0, 512), (KA0, 512, 128), (VA0, 640, 128), (FL0, 2304, 8))


def _lane_plan(sources):
    plan = []
    for t in range(len(sources) // 128):
        groups = {}
        for lane in range(128):
            src = sources[128 * t + lane]
            if src is not None:
                slab, col = src
                groups.setdefault((slab, col // 128, (lane - col) % 128), []).append(lane)
        tile = []
        for key, lanes in groups.items():
            assert lanes == list(range(lanes[0], lanes[-1] + 1))
            tile.append((key, lanes[0], lanes[-1] + 1))
        plan.append(tile)
    return plan


def _assemble(tile_plan, load, rows):
    lane = lax.broadcasted_iota(jnp.int32, (1, 128), 1)
    out = jnp.zeros((rows, 128), F32)
    for (slab, st, roll), lo, hi in tile_plan:
        v = load(slab, st)
        if roll:
            v = pltpu.roll(v, roll, 1)
        out = v if (lo, hi) == (0, 128) else jnp.where((lane >= lo) & (lane < hi), v, out)
    return out


def _w_in_from_shards(land):
    ref_col = [None] * PROJ_W
    for p0, r0, n in SEGMENTS:
        for i in range(n):
            ref_col[p0 + i] = divmod(r0 + i, SHARD_W)
    plan = _lane_plan(ref_col)
    d_model = land.shape[1]
    tm = 256

    def body(land_ref, o_ref):
        load = lambda slab, st: land_ref[slab, :, st * 128:(st + 1) * 128].astype(F32)
        for t, tile_plan in enumerate(plan):
            o_ref[:, t * 128:(t + 1) * 128] = _assemble(tile_plan, load, tm).astype(BF16)

    return pl.pallas_call(
        body, name="w_in_from_shards", grid=(d_model // tm,),
        in_specs=[pl.BlockSpec((N_DEV, tm, SHARD_WP), lambda i: (0, i, 0))],
        out_specs=pl.BlockSpec((tm, PROJ_W), lambda i: (i, 0)),
        out_shape=jax.ShapeDtypeStruct((d_model, PROJ_W), BF16),
        compiler_params=_params(("parallel",)),
    )(land)


def _dw_in_to_parts(dwp):
    padded_col = [None] * IN_WIDTH
    for p0, r0, n in SEGMENTS:
        for i in range(n):
            padded_col[r0 + i] = p0 + i
    sources = []
    for d in range(N_DEV):
        sources += [(0, padded_col[SHARD_W * d + c]) if c < SHARD_W else None for c in range(SHARD_WP)]
    plan = _lane_plan(sources)
    d_model = dwp.shape[0]
    tm = 256
    tiles = SHARD_WP // 128

    def body(dw_ref, o_ref):
        load = lambda slab, st: dw_ref[:, st * 128:(st + 1) * 128].astype(F32)
        for t, tile_plan in enumerate(plan):
            d, c = divmod(t, tiles)
            o_ref[d, :, c * 128:(c + 1) * 128] = _assemble(tile_plan, load, tm).astype(BF16)

    return pl.pallas_call(
        body, name="dw_in_to_parts", grid=(d_model // tm,),
        in_specs=[pl.BlockSpec((tm, PROJ_W), lambda i: (i, 0))],
        out_specs=pl.BlockSpec((N_DEV, tm, SHARD_WP), lambda i: (0, i, 0)),
        out_shape=jax.ShapeDtypeStruct((N_DEV, d_model, SHARD_WP), BF16),
        compiler_params=_params(("parallel",)),
    )(dwp)


def _cast_shards(shards):
    names = list(shards)

    def body(*refs):
        for src, dst in zip(refs[:len(names)], refs[len(names):]):
            if dst.shape != src.shape:
                dst[...] = jnp.zeros(dst.shape, BF16)
                dst[:, 0:src.shape[1]] = src[...].astype(BF16)
            else:
                dst[...] = src[...].astype(BF16)

    out_shape = [jax.ShapeDtypeStruct((shards[n].shape[0], SHARD_WP if n == "w_in" else shards[n].shape[1]), BF16)
                 for n in names]
    outs = pl.pallas_call(body, name="cast_shards", out_shape=out_shape,
                          compiler_params=pltpu.CompilerParams(vmem_limit_bytes=VMEM_LIMIT))(*[shards[n] for n in names])
    return dict(zip(names, outs))


def _tie(x, *tokens):
    for t in tokens:
        if t is not None:
            x = x + t[0:1, 0:1]
    return x


def _local_step(x, mem, target, p, getw, emit, deps=()):
    s = x.shape[0]
    bucket, band = _bucket_table()
    bucket_m = jnp.where(band, bucket, -1).astype(jnp.int32)
    bias = _bias_table(p["rel_bias"], bucket_m)
    bucket_t = jnp.transpose(bucket_m)
    bias_t = _bias_table(p["rel_bias"], bucket_t)
    gqf, gkf, gqa, gka = _tile2(p["qn_fox"]), _tile2(p["kn_fox"]), _tile2(p["qn_swa"]), _tile2(p["kn_swa"])
    gqm = p["qn_mem"]
    bf128 = jnp.pad(p["b_forget"], ((0, 0), (0, 120)))
    sink = p["sink_swa"].reshape(8)

    h = _rms_fwd(x, p["g_mix"], "rms_mix", tuple(deps) + (bias, bias_t))
    w_in = getw("w_in", h)
    proj = _mm(h, w_in, "nn", BF16, 512, 1536, 1024, "proj")
    fl = _mm(h, w_in[:, FL0:FL0 + 128], "nn", F32, 512, 128, 1024, "proj_fl")
    qf, kf, vf, qm, qa, ka, va, qf_t, vf_t = _proj_post(proj, gqf, gkf, gqm, gqa, gka)
    cc4, ca4 = _fox_gate_fwd(fl, bf128)
    w_kv = getw("w_mem_kv", cc4)
    mem_n, kv_raw, mk, mv = _memkv_fwd(mem, p["g_mem"], w_kv, p["kn_mem"])
    kp = jnp.pad(ka, ((SWA_BLOCK, 0), (0, 0)))
    vp = jnp.pad(va, ((SWA_BLOCK, 0), (0, 0)))
    oa = _swa_fwd(qa, kp, vp, bias, sink)
    of, lse4, of_t = _fox_fwd(qf, kf, vf_t, ca4)
    om = _mem_fwd(qm, mk, mv)
    wa, wf, wm, w_out = getw("w_o_swa", oa), getw("w_o_fox", oa), getw("w_o_mem", oa), getw("w_out", oa)
    x1, hm, merged = _merge_fwd(x, oa, of, om, proj, p["b_gate"], wa, wf, wm, w_out, p["g_mlp"])
    w_up = getw("w_mlp_up", of)
    u = _mlp_up(hm, w_up)
    w_down = getw("w_mlp_down", hm)
    dy, dy_b, loss = _mlp_down_loss(u, w_down, x1, target)

    da = _mlp_bwd_act(dy_b, w_down, u)
    t_down = emit({"w_mlp_down": _mm(u, dy_b, "tn", BF16, 1024, 1024, 2048, "dw_down")})
    dx1, dg_mlp = _mlp_bwd_x(da, w_up, x1, dy, _tie(p["g_mlp"], t_down))
    t_up = emit({"w_mlp_up": _mm(hm, da, "tn", BF16, 1024, 1024, 2048, "dw_up", column_chunks=True)})
    dproj, doa, dof_t, dom, dya, dyf, dym, db_gate = _merge_bwd(
        dx1, oa, of, om, proj, _tie(p["b_gate"], t_up), wa, wf, wm, w_out)
    t_o = emit({"w_out": _mm(merged, dx1, "tn", BF16, 1024, 1024, 2048, "dw_out"),
                "w_o_swa": _mm(oa, dya, "tn", BF16, 512, 1024, 2048, "dw_o_swa"),
                "w_o_fox": _mm(of, dyf, "tn", BF16, 512, 1024, 2048, "dw_o_fox"),
                "w_o_mem": _mm(om, dym, "tn", BF16, 512, 1024, 2048, "dw_o_mem")})

    dqm, dmk, dmv = _mem_bwd(qm, mk, mv, dom)
    dw_kv, dkn_mem, dg_mem = _memkv_bwd(dmk, dmv, kv_raw, _tie(p["kn_mem"], t_o), mem, p["g_mem"], mem_n, w_kv)
    t_kv = emit({"w_mem_kv": dw_kv})
    dqa, dkp, dvp, dbias, dsink = _swa_bwd(qa, kp, vp, bias_t, _tie(p["sink_swa"], t_kv).reshape(8), doa)
    dqf_t, dkf, dvf, dck4, dcq4 = _fox_bwd(qf_t, kf, vf, dof_t, of_t, cc4, lse4)

    dcq = jnp.transpose(dcq4[:, 0:2, :], (2, 0, 1)).reshape(s, 8)
    dck = jnp.transpose(dck4[:, 0:2, :], (2, 0, 1)).reshape(s, 8)
    dc = jnp.pad(dcq - dck, ((0, 0), (0, 120)))
    dfl, db_forget = _fox_gate_bwd(dc, fl, bf128)

    dproj, dgn = _proj_pre_bwd(dproj, proj, dqf_t, dkf, dvf, dqm, dqa, dkp[SWA_BLOCK:], dvp[SWA_BLOCK:], dfl,
                               gqf, gkf, gqm, gqa, gka)
    t_in = emit({"w_in": _mm(h, dproj, "tn", BF16, 1024, 3072, 1024, "dw_in")})
    grad_x, dg_mix = _in_bwd_x(dproj, w_in, x, _tie(p["g_mix"], t_in), dx1)
    d_rel = _rel_bias_bwd(dbias, bucket_t)

    fold = lambda r: dgn[r:r + 1, 0:HEAD] + dgn[r:r + 1, HEAD:128]
    small = {
        "g_mix": dg_mix, "b_gate": db_gate, "b_forget": db_forget[:, 0:8],
        "qn_swa": fold(3), "kn_swa": fold(4), "sink_swa": dsink[:, 0].reshape(1, 8), "rel_bias": d_rel,
        "qn_fox": fold(0), "kn_fox": fold(1), "g_mem": dg_mem, "qn_mem": dgn[2:3, :], "kn_mem": dkn_mem,
        "g_mlp": dg_mlp,
    }
    return loss, grad_x, small


SMALL = ("g_mix", "b_gate", "b_forget", "qn_swa", "kn_swa", "sink_swa", "rel_bias", "qn_fox", "kn_fox", "g_mem",
         "qn_mem", "kn_mem", "g_mlp")
BIG = ("w_in", "w_mem_kv", "w_o_swa", "w_o_fox", "w_o_mem", "w_out", "w_mlp_up", "w_mlp_down")
COL_SHARDED = ("w_in", "w_o_swa", "w_o_fox", "w_o_mem", "w_mlp_up")
WEIGHTS = ("g_mix", "w_in", "b_gate", "b_forget", "qn_swa", "kn_swa", "sink_swa", "rel_bias", "qn_fox", "kn_fox", "g_mem",
           "w_mem_kv", "qn_mem", "kn_mem", "w_o_swa", "w_o_fox", "w_o_mem", "w_out", "g_mlp", "w_mlp_up", "w_mlp_down")
SMALL_SLOTS = (("g_mix", 1024), ("b_gate", 3072), ("b_forget", 128), ("qn_swa", 128), ("kn_swa", 128), ("sink_swa", 128),
               ("rel_bias", REL_BUCKETS * 128), ("qn_fox", 128), ("kn_fox", 128), ("g_mem", 1024), ("qn_mem", 128),
               ("kn_mem", 128), ("g_mlp", 1024), ("loss", 128))
SMALL_OFF = {n: sum(w for _, w in SMALL_SLOTS[:i]) for i, (n, _) in enumerate(SMALL_SLOTS)}
SMALL_ROW = sum(w for _, w in SMALL_SLOTS)


def _gathered_to_full(name, g):
    if name in COL_SHARDED:
        return jnp.transpose(g, (1, 0, 2)).reshape(g.shape[1], N_DEV * g.shape[2])
    return g.reshape(N_DEV * g.shape[1], g.shape[2])


def _full_to_parts(name, full, b):
    if name in COL_SHARDED:
        return jnp.transpose(full.reshape(full.shape[0], N_DEV, b), (1, 0, 2)).astype(BF16)
    return full.reshape(N_DEV, full.shape[0] // N_DEV, full.shape[1]).astype(BF16)


def _pack_small(grads, loss):
    pieces = []
    for n, width in SMALL_SLOTS:
        a = loss.reshape(1, 1) if n == "loss" else grads[n].reshape(1, -1)
        pieces.append(jnp.pad(a, ((0, 0), (0, width - a.shape[1]))))
    return jnp.concatenate(pieces, axis=1)


def _adamw_small(gathered, w, m, v):
    names = list(SMALL)

    def body(*refs):
        p_ref = refs[0]
        ins = refs[1:1 + 3 * len(names)]
        outs = refs[1 + 3 * len(names):]
        g_all = p_ref[0]
        for k in range(1, N_DEV):
            g_all = g_all + p_ref[k]
        for i, n in enumerate(names):
            w_ref, m_ref, v_ref = ins[3 * i:3 * i + 3]
            out = outs[4 * i:4 * i + 4]
            rows, cols = w_ref.shape
            for r in range(rows):
                off = SMALL_OFF[n] + 128 * r
                g = g_all[:, off:off + cols]
                rs = slice(r, r + 1)
                res = (g,) + _adam_math(w_ref[rs, :], g, m_ref[rs, :], v_ref[rs, :])
                for o_ref, val in zip(out, res):
                    o_ref[rs, :] = val
        outs[-1][...] = g_all[:, SMALL_OFF["loss"]:SMALL_OFF["loss"] + 128]

    args = [gathered]
    out_shape = []
    for n in names:
        args += [w[n], m[n], v[n]]
        out_shape += [jax.ShapeDtypeStruct(w[n].shape, F32)] * 4
    out_shape.append(jax.ShapeDtypeStruct((1, 128), F32))
    outs = pl.pallas_call(body, name="adamw_small", out_shape=out_shape)(*args)
    return {n: outs[4 * i:4 * i + 4] for i, n in enumerate(names)}, outs[-1]


def kernel(x, mem, g_mix, w_in, b_gate, b_forget, qn_swa, kn_swa, sink_swa, rel_bias, qn_fox, kn_fox, g_mem, w_mem_kv, qn_mem, kn_mem, w_o_swa, w_o_fox, w_o_mem, w_out, g_mlp, w_mlp_up, w_mlp_down, loss_target, m_g_mix, m_w_in, m_b_gate, m_b_forget, m_qn_swa, m_kn_swa, m_sink_swa, m_rel_bias, m_qn_fox, m_kn_fox, m_g_mem, m_w_mem_kv, m_qn_mem, m_kn_mem, m_w_o_swa, m_w_o_fox, m_w_o_mem, m_w_out, m_g_mlp, m_w_mlp_up, m_w_mlp_down, v_g_mix, v_w_in, v_b_gate, v_b_forget, v_qn_swa, v_kn_swa, v_sink_swa, v_rel_bias, v_qn_fox, v_kn_fox, v_g_mem, v_w_mem_kv, v_qn_mem, v_kn_mem, v_w_o_swa, v_w_o_fox, v_w_o_mem, v_w_out, v_g_mlp, v_w_mlp_up, v_w_mlp_down):
    wts = dict(g_mix=g_mix, w_in=w_in, b_gate=b_gate, b_forget=b_forget, qn_swa=qn_swa, kn_swa=kn_swa, sink_swa=sink_swa,
               rel_bias=rel_bias, qn_fox=qn_fox, kn_fox=kn_fox, g_mem=g_mem, w_mem_kv=w_mem_kv, qn_mem=qn_mem, kn_mem=kn_mem,
               w_o_swa=w_o_swa, w_o_fox=w_o_fox, w_o_mem=w_o_mem, w_out=w_out, g_mlp=g_mlp, w_mlp_up=w_mlp_up,
               w_mlp_down=w_mlp_down)
    mom = dict(g_mix=m_g_mix, w_in=m_w_in, b_gate=m_b_gate, b_forget=m_b_forget, qn_swa=m_qn_swa, kn_swa=m_kn_swa,
               sink_swa=m_sink_swa, rel_bias=m_rel_bias, qn_fox=m_qn_fox, kn_fox=m_kn_fox, g_mem=m_g_mem, w_mem_kv=m_w_mem_kv,
               qn_mem=m_qn_mem, kn_mem=m_kn_mem, w_o_swa=m_w_o_swa, w_o_fox=m_w_o_fox, w_o_mem=m_w_o_mem, w_out=m_w_out,
               g_mlp=m_g_mlp, w_mlp_up=m_w_mlp_up, w_mlp_down=m_w_mlp_down)
    var = dict(g_mix=v_g_mix, w_in=v_w_in, b_gate=v_b_gate, b_forget=v_b_forget, qn_swa=v_qn_swa, kn_swa=v_kn_swa,
               sink_swa=v_sink_swa, rel_bias=v_rel_bias, qn_fox=v_qn_fox, kn_fox=v_kn_fox, g_mem=v_g_mem, w_mem_kv=v_w_mem_kv,
               qn_mem=v_qn_mem, kn_mem=v_kn_mem, w_o_swa=v_w_o_swa, w_o_fox=v_w_o_fox, w_o_mem=v_w_o_mem, w_out=v_w_out,
               g_mlp=v_g_mlp, w_mlp_up=v_w_mlp_up, w_mlp_down=v_w_mlp_down)

    shards = _cast_shards({n: wts[n][0] for n in BIG})
    first = _split_start([shards["w_in"]], True, "ag_start_w_in", peers=(1,) + SAME_CORE)
    rest = _split_start([shards[n] for n in BIG[1:]], True, "ag_start_rest", after=first[4])
    full = {}

    def getw(n, after):
        if n == "w_in" and n not in full:
            forwarded = _forward_start(first, after, "ag_forward_w_in")
            full[n] = _w_in_from_shards(_forward_wait(first, forwarded, "ag_wait_w_in"))
        elif n not in full:
            land = _split_wait(rest, BIG[1:].index(n), after, True, "ag_wait_" + n)
            full[n] = land if n == "w_mlp_up" else _gathered_to_full(n, land)
        return full[n]

    exchanges = {}

    def emit(grads_by_name):
        parts = []
        for n, grad in grads_by_name.items():
            if n == "w_in":
                parts.append(_dw_in_to_parts(grad))
            else:
                parts.append(grad if n == "w_mlp_up" else _full_to_parts(n, grad, wts[n].shape[2]))
        started = _split_start(parts, False, "rs_start_" + next(iter(grads_by_name)))
        for w, n in enumerate(grads_by_name):
            exchanges[n] = (started, w)
        return started[4]

    small_p = {n: wts[n] for n in SMALL}
    loss, grad_x, small_g = _local_step(x[0], mem[0], loss_target[0], small_p, getw, emit, (first[4], rest[4]))

    packed = _pack_small(small_g, loss)
    small_gather = _split_start([packed], True, "ag_start_small")

    grads, delta, new_m, new_v = {}, {}, {}, {}

    def update(n, after):
        land = _split_wait(*exchanges[n], after, False, "rs_wait_" + n)
        g, d, m2, v2 = _adamw(land, wts[n][0], mom[n][0], var[n][0], "adamw_" + n)
        grads[n], delta[n], new_m[n], new_v[n] = g[None], d[None], m2[None], v2[None]
        return d

    after = small_gather[4]
    for n in exchanges:
        if n != "w_in":
            after = update(n, after)

    gathered = _split_wait(small_gather, 0, after, True, "ag_wait_small")
    small_out, total = _adamw_small(gathered, small_p, mom, var)
    for name, (g, d, m2, v2) in small_out.items():
        grads[name], delta[name], new_m[name], new_v[name] = g, d, m2, v2
    update("w_in", total)

    return (total[0, 0], grad_x[None], *[grads[n] for n in WEIGHTS], *[delta[n] for n in WEIGHTS],
            *[new_m[n] for n in WEIGHTS], *[new_v[n] for n in WEIGHTS])
```

```python
import math

import jax
import jax.numpy as jnp
from jax import lax
from jax.experimental import pallas as pl
from jax.experimental.pallas import tpu as pltpu

F32 = jnp.float32
BF16 = jnp.bfloat16

D_MODEL = 1024
N_MEM = 256
D_FF = 4096
HEAD = 64
SWA_HEADS = 8
SWA_BLOCK = 128
MEM_HEADS = 4
MEM_HEAD = 128
EPS = 1e-6
NEG = -1e30
REL_BUCKETS = 32
REL_MAX_DIST = 128

ADAM_LR = 0.001
ADAM_B1 = 0.9
ADAM_B2 = 0.999
ADAM_EPS = 1e-08
ADAM_WD = 0.01
ADAM_STEP = 10

GL0, QF0, KF0, VF0, QM0, QA0, KA0, VA0, FL0 = 0, 3072, 3584, 4096, 4608, 5120, 5632, 5760, 5888
PROJ_W = 6144
HALF_W = 3072
H_QF, H_KF, H_VF, H_QM, H_QA, H_KA, H_VA, H_FL = 0, 512, 1024, 1536, 2048, 2560, 2688, 2816

VMEM_LIMIT = 56 * 1024 * 1024
N_DEV = 8
MESH = pl.DeviceIdType.MESH

NN = (((1,), (0,)), ((), ()))
NT = (((1,), (1,)), ((), ()))
TN = (((0,), (0,)), ((), ()))


def _dot(a, b, dims=NN):
    return lax.dot_general(a, b, dims, preferred_element_type=F32)


def _params(sem):
    return pltpu.CompilerParams(dimension_semantics=sem, vmem_limit_bytes=VMEM_LIMIT)


def _full(shape):
    nd = len(shape)
    return pl.BlockSpec(shape, lambda *_: (0,) * nd)


def _sigmoid(z):
    return 1.0 / (1.0 + jnp.exp(-z))


def _group_mean(v, hd):
    if hd == 128:
        return jnp.mean(v, axis=-1, keepdims=True)
    lane = lax.broadcasted_iota(jnp.int32, v.shape, 1)
    lo = lane < HEAD
    s_lo = jnp.sum(jnp.where(lo, v, 0.0), axis=-1, keepdims=True)
    s_hi = jnp.sum(jnp.where(lo, 0.0, v), axis=-1, keepdims=True)
    return jnp.where(lo, s_lo, s_hi) * (1.0 / HEAD)


def _mm(a, b, mode, out_dtype, tm, tn, tk, name, column_chunks=False):
    if mode == "nn":
        m, k = a.shape
        n = b.shape[1]
    elif mode == "nt":
        m, k = a.shape
        n = b.shape[0]
    else:
        k, m = a.shape
        n = b.shape[1]
    tm, tn, tk = min(tm, m), min(tn, n), min(tk, k)
    nk = k // tk
    chunk = n // N_DEV
    per_tile = tn // chunk if column_chunks else 1
    dims = {"nn": NN, "nt": NT, "tn": TN}[mode]
    a_spec = pl.BlockSpec((tk, tm), lambda j, i, kk: (kk, i)) if mode == "tn" else pl.BlockSpec((tm, tk), lambda j, i, kk: (i, kk))
    b_spec = pl.BlockSpec((tn, tk), lambda j, i, kk: (j, kk)) if mode == "nt" else pl.BlockSpec((tk, tn), lambda j, i, kk: (kk, j))

    def body(a_ref, b_ref, o_ref, *acc):
        prod = _dot(a_ref[...].astype(BF16), b_ref[...].astype(BF16), dims)

        def write(res):
            if column_chunks:
                for c in range(per_tile):
                    o_ref[c] = res[:, c * chunk:(c + 1) * chunk].astype(o_ref.dtype)
            else:
                o_ref[...] = res.astype(o_ref.dtype)

        if nk == 1:
            write(prod)
        else:
            acc_ref, = acc
            kk = pl.program_id(2)

            @pl.when(kk == 0)
            def _():
                acc_ref[...] = prod

            @pl.when(kk > 0)
            def _():
                acc_ref[...] += prod

            @pl.when(kk == nk - 1)
            def _():
                write(acc_ref[...])

    return pl.pallas_call(
        body, name=name, grid=(n // tn, m // tm, nk),
        in_specs=[a_spec, b_spec],
        out_specs=(pl.BlockSpec((per_tile, tm, chunk), lambda j, i, kk: (j, i, 0)) if column_chunks
                   else pl.BlockSpec((tm, tn), lambda j, i, kk: (i, j))),
        out_shape=jax.ShapeDtypeStruct((N_DEV, m, chunk) if column_chunks else (m, n), out_dtype),
        scratch_shapes=[pltpu.VMEM((tm, tn), F32)] if nk > 1 else [],
        compiler_params=_params(("parallel", "parallel", "arbitrary")),
    )(a, b)


def _rms_fwd(x, g, name, deps=()):
    s, d = x.shape
    tm = min(512, s)

    def body(x_ref, g_ref, *rest):
        h_ref = rest[len(deps)]
        xv = x_ref[...]
        r = lax.rsqrt(jnp.mean(xv * xv, axis=-1, keepdims=True) + EPS)
        h_ref[...] = (xv * r * g_ref[...]).astype(BF16)

    return pl.pallas_call(
        body, name=name, grid=(s // tm,),
        in_specs=[pl.BlockSpec((tm, d), lambda i: (i, 0)), _full((1, d))] + [pl.BlockSpec(memory_space=pl.ANY)] * len(deps),
        out_specs=pl.BlockSpec((tm, d), lambda i: (i, 0)),
        out_shape=jax.ShapeDtypeStruct((s, d), BF16),
        compiler_params=_params(("parallel",)),
    )(x, g, *deps)


def _proj_post(proj, gq_fox, gk_fox, gq_mem, gq_swa, gk_swa):
    s = proj.shape[0]
    tm = min(256, s)

    def body(p_ref, gqf, gkf, gqm, gqa, gka, qf_ref, kf_ref, vf_ref, qm_ref, qa_ref, ka_ref, va_ref, qft_ref, vft_ref):
        def norm(off, width, hd, g_ref, o_ref, scaled_t_ref=None):
            for b in range(width // 128):
                v = p_ref[:, off + b * 128: off + (b + 1) * 128].astype(F32)
                r = lax.rsqrt(_group_mean(v * v, hd) + EPS)
                vn = (v * r * g_ref[...]).astype(BF16)
                o_ref[:, b * 128:(b + 1) * 128] = vn
                if scaled_t_ref is not None:
                    scaled_t_ref[b * 128:(b + 1) * 128, :] = (vn.astype(F32) * 0.125).T.astype(BF16)

        norm(H_QF, 512, HEAD, gqf, qf_ref, qft_ref)
        norm(H_KF, 512, HEAD, gkf, kf_ref)
        vf_ref[...] = p_ref[:, H_VF:H_VF + 512].astype(BF16)
        for b in range(4):
            vft_ref[b * 128:(b + 1) * 128, :] = p_ref[:, H_VF + b * 128:H_VF + (b + 1) * 128].astype(F32).T.astype(BF16)
        norm(H_QM, 512, MEM_HEAD, gqm, qm_ref)
        norm(H_QA, 512, HEAD, gqa, qa_ref)
        norm(H_KA, 128, HEAD, gka, ka_ref)
        va_ref[...] = p_ref[:, H_VA:H_VA + 128].astype(BF16)

    g_spec = _full((1, 128))
    o512 = pl.BlockSpec((tm, 512), lambda i: (i, 0))
    o128 = pl.BlockSpec((tm, 128), lambda i: (i, 0))
    s512 = jax.ShapeDtypeStruct((s, 512), BF16)
    s128 = jax.ShapeDtypeStruct((s, 128), BF16)
    return pl.pallas_call(
        body, name="proj_post", grid=(s // tm,),
        in_specs=[pl.BlockSpec((tm, HALF_W), lambda i: (i, 1)), g_spec, g_spec, g_spec, g_spec, g_spec],
        out_specs=[o512, o512, o512, o512, o512, o128, o128] + [pl.BlockSpec((512, tm), lambda i: (0, i))] * 2,
        out_shape=[s512, s512, s512, s512, s512, s128, s128] + [jax.ShapeDtypeStruct((512, s), BF16)] * 2,
        compiler_params=_params(("parallel",)),
    )(proj, gq_fox, gk_fox, gq_mem, gq_swa, gk_swa)


def _tri(n, lower):
    r = lax.broadcasted_iota(jnp.int32, (n, n), 0)
    c = lax.broadcasted_iota(jnp.int32, (n, n), 1)
    return jnp.where((c <= r) if lower else (c >= r), 1.0, 0.0).astype(F32)


def _fox_gate_fwd(proj, b_forget128):
    s = proj.shape[0]
    tm = min(512, s)

    def body(p_ref, b_ref, cc_ref, ca_ref, carry_ref):
        i = pl.program_id(0)

        @pl.when(i == 0)
        def _():
            carry_ref[...] = jnp.zeros_like(carry_ref)

        z = p_ref[...] + b_ref[...]
        logf = jnp.minimum(z, 0.0) - jnp.log(1.0 + jnp.exp(-jnp.abs(z)))
        c = jnp.dot(_tri(tm, True), logf, precision=lax.Precision.HIGHEST, preferred_element_type=F32) + carry_ref[...]
        carry_ref[...] = c[tm - 1:tm, :]
        lane = lax.broadcasted_iota(jnp.int32, (tm, 128), 1)
        for hp in range(4):
            cc_ref[hp] = c if hp == 0 else pltpu.roll(c, 128 - 2 * hp, 1)
            aug = jnp.zeros((tm, 128), F32)
            for e in range(2):
                rest = jnp.broadcast_to(c[:, 2 * hp + e:2 * hp + e + 1], (tm, 128))
                for part in range(3):
                    piece = rest.astype(BF16).astype(F32)
                    aug = jnp.where(lane == HEAD * (1 - e) + part, piece, aug)
                    rest = rest - piece
            ca_ref[hp] = aug.astype(BF16)

    return pl.pallas_call(
        body, name="fox_gate_fwd", grid=(s // tm,),
        in_specs=[pl.BlockSpec((tm, 128), lambda i: (i, 0)), _full((1, 128))],
        out_specs=[pl.BlockSpec((4, tm, 128), lambda i: (0, i, 0))] * 2,
        out_shape=[jax.ShapeDtypeStruct((4, s, 128), F32), jax.ShapeDtypeStruct((4, s, 128), BF16)],
        scratch_shapes=[pltpu.VMEM((1, 128), F32)],
        compiler_params=_params(("arbitrary",)),
    )(proj, b_forget128)


def _memkv_fwd(mem, g_mem, w_kv, kn_mem):
    m = mem.shape[0]

    def body(mem_ref, g_ref, w_ref, kn_ref, memn_ref, kv_ref, mk_ref, mv_ref):
        xv = mem_ref[...]
        r = lax.rsqrt(jnp.mean(xv * xv, axis=-1, keepdims=True) + EPS)
        mn = (xv * r * g_ref[...]).astype(BF16)
        memn_ref[...] = mn
        kv = _dot(mn, w_ref[...])
        kv_ref[...] = kv
        for h in range(MEM_HEADS):
            v = kv[:, h * 128:(h + 1) * 128]
            rr = lax.rsqrt(jnp.mean(v * v, axis=-1, keepdims=True) + EPS)
            mk_ref[:, h * 128:(h + 1) * 128] = (v * rr * kn_ref[...]).astype(BF16)
        mv_ref[...] = kv[:, 512:1024].astype(BF16)

    return pl.pallas_call(
        body, name="memkv_fwd",
        out_shape=[jax.ShapeDtypeStruct((m, D_MODEL), BF16), jax.ShapeDtypeStruct((m, 1024), F32),
                   jax.ShapeDtypeStruct((m, 512), BF16), jax.ShapeDtypeStruct((m, 512), BF16)],
        compiler_params=pltpu.CompilerParams(vmem_limit_bytes=VMEM_LIMIT),
    )(mem, g_mem, w_kv, kn_mem)


def _bias_table(rel_bias, bucket):
    def body(rb_ref, bk_ref, o_ref):
        bk = bk_ref[...]
        for h in range(SWA_HEADS):
            acc = jnp.zeros(bk.shape, F32)
            for b in range(REL_BUCKETS):
                acc = jnp.where(bk == b, rb_ref[b, h], acc)
            o_ref[h] = acc

    return pl.pallas_call(
        body, name="bias_table",
        in_specs=[pl.BlockSpec(memory_space=pltpu.SMEM), pl.BlockSpec(memory_space=pltpu.VMEM)],
        out_shape=jax.ShapeDtypeStruct((SWA_HEADS,) + bucket.shape, F32),
    )(rel_bias, bucket)


def _swa_valid(n):
    row = lax.broadcasted_iota(jnp.int32, (SWA_BLOCK, 2 * SWA_BLOCK), 0)
    col = lax.broadcasted_iota(jnp.int32, (SWA_BLOCK, 2 * SWA_BLOCK), 1)
    dist = row + SWA_BLOCK - col
    return (dist >= 0) & (dist < SWA_BLOCK) & ((col >= SWA_BLOCK) | (n > 0))


def _swa_fwd(qa, kp, vp, bias, sink):
    s = qa.shape[0]
    nb = s // SWA_BLOCK

    def body(sink_ref, q_ref, kp_ref, vp_ref, bias_ref, o_ref):
        n = pl.program_id(0)
        start = pl.multiple_of(n * SWA_BLOCK, SWA_BLOCK)
        k2 = kp_ref[pl.ds(start, 2 * SWA_BLOCK), :]
        v2 = vp_ref[pl.ds(start, 2 * SWA_BLOCK), :]
        valid = _swa_valid(n)
        heads = range(SWA_HEADS)
        hs = lambda h: slice(h * HEAD, (h + 1) * HEAD)
        sc = [jnp.where(valid, _dot(q_ref[:, hs(h)], k2[:, hs(h // 4)], NT) * 0.125 + bias_ref[h], NEG) for h in heads]
        pn = []
        for h in heads:
            sk = sink_ref[h]
            mx = jnp.maximum(jnp.max(sc[h], axis=-1, keepdims=True), sk)
            p = jnp.exp(sc[h] - mx)
            den = jnp.sum(p, axis=-1, keepdims=True) + jnp.exp(sk - mx)
            pn.append((p / den).astype(BF16))
        outs = [_dot(pn[h], v2[:, hs(h // 4)]).astype(BF16) for h in heads]
        for h in heads:
            o_ref[:, hs(h)] = outs[h]

    return pl.pallas_call(
        body, name="swa_fwd", grid=(nb,),
        in_specs=[pl.BlockSpec(memory_space=pltpu.SMEM),
                  pl.BlockSpec((SWA_BLOCK, 512), lambda n: (n, 0)),
                  _full(kp.shape), _full(vp.shape), _full(bias.shape)],
        out_specs=pl.BlockSpec((SWA_BLOCK, 512), lambda n: (n, 0)),
        out_shape=jax.ShapeDtypeStruct((s, 512), BF16),
        compiler_params=_params(("parallel",)),
    )(sink, qa, kp, vp, bias)


def _head_mask(e):
    lane = lax.broadcasted_iota(jnp.int32, (1, 128), 1)
    return (lane >= e * HEAD) & (lane < (e + 1) * HEAD)


FOX_FWD_T = 1024
FOX_BWD_T = 512


def _head_rows(e):
    row = lax.broadcasted_iota(jnp.int32, (128, 1), 0)
    return (row >= e * HEAD) & (row < (e + 1) * HEAD)


def _fox_fwd(q, k, v_t, ca4):
    s = q.shape[0]
    t = min(FOX_FWD_T, s)
    nq = s // t

    def body(q_ref, k_ref, vt_ref, ca_ref, o_ref, lse_ref, ot_ref):
        i = pl.program_id(1)
        qs = q_ref[...] * jnp.asarray(0.125, BF16)
        lane = lax.broadcasted_iota(jnp.int32, (1, 128), 1)
        minus = [jnp.where((lane >= HEAD * (1 - e)) & (lane < HEAD * (1 - e) + 3), -1.0, 0.0).astype(BF16) for e in range(2)]
        qe = [jnp.where(_head_mask(e), qs, jnp.broadcast_to(minus[e], qs.shape)) for e in range(2)]

        def block(carry, key0, nkeys, q0, nqs, masked):
            ks = pl.ds(pl.multiple_of(key0, 128), nkeys)
            kj = k_ref[ks, :]
            caj = ca_ref[0, ks, :]
            vtj = vt_ref[:, ks]
            out = []
            for e in range(2):
                m_all, acc_all = carry[2 * e], carry[2 * e + 1]
                m, acc = m_all[:, q0:q0 + nqs], acc_all[:, q0:q0 + nqs]
                st = _dot(jnp.where(_head_mask(e), kj, caj), qe[e][q0:q0 + nqs, :], NT)
                if masked:
                    krow = lax.broadcasted_iota(jnp.int32, (nkeys, nqs), 0) + key0
                    qcol = lax.broadcasted_iota(jnp.int32, (nkeys, nqs), 1) + (i * t + q0)
                    st = jnp.where(krow <= qcol, st, NEG)
                m_new = jnp.maximum(m, jnp.max(st, axis=0, keepdims=True))
                alpha = jnp.exp(m - m_new)
                pt = jnp.exp(st - m_new).astype(BF16)
                vte = jnp.where(_head_rows(e), vtj, jnp.ones_like(vtj))
                acc_new = alpha * acc + _dot(vte, pt)
                if nqs < t:
                    m_new = jnp.concatenate([m_all[:, :q0], m_new], axis=1)
                    acc_new = jnp.concatenate([acc_all[:, :q0], acc_new], axis=1)
                out += [m_new, acc_new]
            return tuple(out)

        half = t // 2
        init = (jnp.full((1, t), NEG, F32), jnp.zeros((128, t), F32)) * 2
        carry = lax.fori_loop(0, i, lambda j, c: block(c, j * t, t, 0, t, False), init)
        carry = block(carry, i * t, half, 0, t, True)
        m0, a0, m1, a1 = block(carry, i * t + half, half, half, half, True)
        l0 = a0[HEAD:HEAD + 1, :]
        l1 = a1[0:1, :]
        o_t = jnp.where(_head_rows(0), a0 / l0, a1 / l1)
        o_ref[...] = o_t.T.astype(BF16)
        ot_ref[...] = o_t.astype(BF16)
        r8 = lax.broadcasted_iota(jnp.int32, (8, t), 0)
        lse_ref[0] = jnp.where(r8 == 0, m0 + jnp.log(l0), jnp.where(r8 == 1, m1 + jnp.log(l1), 0.0))

    return pl.pallas_call(
        body, name="fox_fwd", grid=(4, nq),
        in_specs=[pl.BlockSpec((t, 128), lambda hp, i: (i, hp)),
                  pl.BlockSpec((s, 128), lambda hp, i: (0, hp)),
                  pl.BlockSpec((128, s), lambda hp, i: (hp, 0)),
                  pl.BlockSpec((1, s, 128), lambda hp, i: (hp, 0, 0))],
        out_specs=[pl.BlockSpec((t, 128), lambda hp, i: (i, hp)),
                   pl.BlockSpec((1, 8, t), lambda hp, i: (hp, 0, i)),
                   pl.BlockSpec((128, t), lambda hp, i: (hp, i))],
        out_shape=[jax.ShapeDtypeStruct((s, 512), BF16), jax.ShapeDtypeStruct((4, 8, s), F32),
                   jax.ShapeDtypeStruct((512, s), BF16)],
        compiler_params=_params(("parallel", "parallel")),
    )(q, k, v_t, ca4)


MEM_SCALE = MEM_HEAD ** -0.5


def _mem_fwd(qm, mk, mv):
    s = qm.shape[0]
    tq = min(512, s)

    def body(q_ref, mk_ref, mv_ref, o_ref):
        for h in range(MEM_HEADS):
            hs = slice(h * 128, (h + 1) * 128)
            sc = _dot(q_ref[:, hs], mk_ref[:, hs], NT) * MEM_SCALE
            mx = jnp.max(sc, axis=-1, keepdims=True)
            p = jnp.exp(sc - mx)
            p = p / jnp.sum(p, axis=-1, keepdims=True)
            o_ref[:, hs] = _dot(p.astype(BF16), mv_ref[:, hs]).astype(BF16)

    return pl.pallas_call(
        body, name="mem_fwd", grid=(s // tq,),
        in_specs=[pl.BlockSpec((tq, 512), lambda i: (i, 0)), _full(mk.shape), _full(mv.shape)],
        out_specs=pl.BlockSpec((tq, 512), lambda i: (i, 0)),
        out_shape=jax.ShapeDtypeStruct((s, 512), BF16),
        compiler_params=_params(("parallel",)),
    )(qm, mk, mv)


def _merge_fwd(x, oa, of, om, proj, b_gate, wa, wf, wm, w_out, g_mlp):
    s = x.shape[0]
    tm = min(256, s)

    def body(x_ref, oa_ref, of_ref, om_ref, gl_ref, bg_ref, wa_ref, wf_ref, wm_ref, wo_ref, g_ref, x1_ref, hm_ref, mg_ref):
        merged = None
        for b, (o_ref, w_ref) in enumerate(((oa_ref, wa_ref), (of_ref, wf_ref), (om_ref, wm_ref))):
            cs = slice(b * D_MODEL, (b + 1) * D_MODEL)
            y = _dot(o_ref[...], w_ref[...])
            t = _sigmoid(gl_ref[:, cs].astype(F32) + bg_ref[:, cs]) * y
            merged = t if merged is None else merged + t
        mb = merged.astype(BF16)
        mg_ref[...] = mb
        x1 = x_ref[...] + _dot(mb, wo_ref[...])
        x1_ref[...] = x1
        r = lax.rsqrt(jnp.mean(x1 * x1, axis=-1, keepdims=True) + EPS)
        hm_ref[...] = (x1 * r * g_ref[...]).astype(BF16)

    row = lambda w: pl.BlockSpec((tm, w), lambda i: (i, 0))
    return pl.pallas_call(
        body, name="merge_fwd", grid=(s // tm,),
        in_specs=[row(D_MODEL), row(512), row(512), row(512), row(HALF_W), _full((1, HALF_W)),
                  _full(wa.shape), _full(wf.shape), _full(wm.shape), _full(w_out.shape), _full((1, D_MODEL))],
        out_specs=[row(D_MODEL), row(D_MODEL), row(D_MODEL)],
        out_shape=[jax.ShapeDtypeStruct((s, D_MODEL), F32), jax.ShapeDtypeStruct((s, D_MODEL), BF16),
                   jax.ShapeDtypeStruct((s, D_MODEL), BF16)],
        compiler_params=_params(("parallel",)),
    )(x, oa, of, om, proj, b_gate, wa, wf, wm, w_out, g_mlp)


def _mlp_up(hm, w_up):
    s = hm.shape[0]
    tm, tn = min(1024, s), w_up.shape[2]

    def body(h_ref, w_ref, u_ref):
        r = jnp.maximum(_dot(h_ref[...], w_ref[0]), 0.0)
        u_ref[...] = (r * r).astype(BF16)

    return pl.pallas_call(
        body, name="mlp_up", grid=(s // tm, D_FF // tn),
        in_specs=[pl.BlockSpec((tm, D_MODEL), lambda i, j: (i, 0)), pl.BlockSpec((1, D_MODEL, tn), lambda i, j: (j, 0, 0))],
        out_specs=pl.BlockSpec((tm, tn), lambda i, j: (i, j)),
        out_shape=jax.ShapeDtypeStruct((s, D_FF), BF16),
        compiler_params=_params(("parallel", "parallel")),
    )(hm, w_up)


def _mlp_down_loss(u, w_down, x1, target):
    s = u.shape[0]
    tm = min(256, s)

    def body(u_ref, w_ref, x1_ref, t_ref, dy_ref, dyb_ref, loss_ref):
        i = pl.program_id(0)

        @pl.when(i == 0)
        def _():
            loss_ref[...] = jnp.zeros_like(loss_ref)

        y = x1_ref[...] + _dot(u_ref[...], w_ref[...])
        err = y - t_ref[...]
        dy = err * (1.0 / D_MODEL)
        dy_ref[...] = dy
        dyb_ref[...] = dy.astype(BF16)
        part = jnp.sum(jnp.sum(err * err, axis=-1, keepdims=True) * (1.0 / D_MODEL), axis=0, keepdims=True)
        loss_ref[...] += 0.5 * part

    row = pl.BlockSpec((tm, D_MODEL), lambda i: (i, 0))
    return pl.pallas_call(
        body, name="mlp_down_loss", grid=(s // tm,),
        in_specs=[pl.BlockSpec((tm, D_FF), lambda i: (i, 0)), _full(w_down.shape), row, row],
        out_specs=[row, row, _full((1, 1))],
        out_shape=[jax.ShapeDtypeStruct((s, D_MODEL), F32), jax.ShapeDtypeStruct((s, D_MODEL), BF16),
                   jax.ShapeDtypeStruct((1, 1), F32)],
        compiler_params=_params(("arbitrary",)),
    )(u, w_down, x1, target)


def _mlp_bwd_act(dy, w_down, u):
    s = dy.shape[0]
    tm, tn = min(1024, s), 1024

    def body(dy_ref, w_ref, u_ref, da_ref):
        du = _dot(dy_ref[...], w_ref[...], NT)
        da_ref[...] = (du * (2.0 * jnp.sqrt(u_ref[...].astype(F32)))).astype(BF16)

    return pl.pallas_call(
        body, name="mlp_bwd_act", grid=(D_FF // tn, s // tm),
        in_specs=[pl.BlockSpec((tm, D_MODEL), lambda j, i: (i, 0)), pl.BlockSpec((tn, D_MODEL), lambda j, i: (j, 0)),
                  pl.BlockSpec((tm, tn), lambda j, i: (i, j))],
        out_specs=pl.BlockSpec((tm, tn), lambda j, i: (i, j)),
        out_shape=jax.ShapeDtypeStruct((s, D_FF), BF16),
        compiler_params=_params(("parallel", "parallel")),
    )(dy, w_down, u)


def _rms_bwd(xv, g, dh, skip):
    r = lax.rsqrt(jnp.mean(xv * xv, axis=-1, keepdims=True) + EPS)
    n = xv * r
    dn = dh * g
    dx = skip + r * (dn - n * jnp.mean(dn * n, axis=-1, keepdims=True))
    return dx, jnp.sum(dh * n, axis=0, keepdims=True)


def _mlp_bwd_x(da, w_up, x1, dy, g_mlp):
    s = da.shape[0]
    tm = min(256, s)

    def body(da_ref, w_ref, x1_ref, dy_ref, g_ref, dx1_ref, dg_ref):
        i = pl.program_id(0)

        @pl.when(i == 0)
        def _():
            dg_ref[...] = jnp.zeros_like(dg_ref)

        tn = w_ref.shape[2]
        dhm = _dot(da_ref[:, 0:tn], w_ref[0], NT)
        for j in range(1, N_DEV):
            dhm = dhm + _dot(da_ref[:, j * tn:(j + 1) * tn], w_ref[j], NT)
        dx, dg = _rms_bwd(x1_ref[...], g_ref[...], dhm, dy_ref[...])
        dx1_ref[...] = dx
        dg_ref[...] += dg

    row = pl.BlockSpec((tm, D_MODEL), lambda i: (i, 0))
    return pl.pallas_call(
        body, name="mlp_bwd_x", grid=(s // tm,),
        in_specs=[pl.BlockSpec((tm, D_FF), lambda i: (i, 0)), _full(w_up.shape), row, row, _full((1, D_MODEL))],
        out_specs=[row, _full((1, D_MODEL))],
        out_shape=[jax.ShapeDtypeStruct((s, D_MODEL), F32), jax.ShapeDtypeStruct((1, D_MODEL), F32)],
        compiler_params=_params(("arbitrary",)),
    )(da, w_up, x1, dy, g_mlp)


def _merge_bwd(dx1, oa, of, om, proj, b_gate, wa, wf, wm, w_out):
    s = dx1.shape[0]
    tm = min(256, s)

    def body(dx1_ref, oa_ref, of_ref, om_ref, gl_ref, bg_ref, wa_ref, wf_ref, wm_ref, wo_ref,
             dp_ref, doa_ref, dof_ref, dom_ref, dya_ref, dyf_ref, dym_ref, dbg_ref):
        i = pl.program_id(0)

        @pl.when(i == 0)
        def _():
            dbg_ref[...] = jnp.zeros_like(dbg_ref)

        dmerged = _dot(dx1_ref[...].astype(BF16), wo_ref[...], NT)
        branches = ((oa_ref, wa_ref, doa_ref, dya_ref), (of_ref, wf_ref, dof_ref, dyf_ref), (om_ref, wm_ref, dom_ref, dym_ref))
        for b, (o_ref, w_ref, do_ref, dyb_ref) in enumerate(branches):
            cs = slice(b * D_MODEL, (b + 1) * D_MODEL)
            y = _dot(o_ref[...], w_ref[...])
            g = _sigmoid(gl_ref[:, cs].astype(F32) + bg_ref[:, cs])
            dz = (dmerged * y) * g * (1.0 - g)
            dp_ref[:, cs] = dz.astype(BF16)
            dbg_ref[:, cs] += jnp.sum(dz, axis=0, keepdims=True)
            dyb = (dmerged * g).astype(BF16)
            dyb_ref[...] = dyb
            do = _dot(dyb, w_ref[...], NT)
            do_ref[...] = (do.T if b == 1 else do).astype(BF16)

    row = lambda w: pl.BlockSpec((tm, w), lambda i: (i, 0))
    sd = lambda w: jax.ShapeDtypeStruct((s, w), BF16)
    return pl.pallas_call(
        body, name="merge_bwd", grid=(s // tm,),
        in_specs=[row(D_MODEL), row(512), row(512), row(512), row(HALF_W), _full((1, HALF_W)),
                  _full(wa.shape), _full(wf.shape), _full(wm.shape), _full(w_out.shape)],
        out_specs=[row(HALF_W), row(512), pl.BlockSpec((512, tm), lambda i: (0, i)), row(512),
                   row(D_MODEL), row(D_MODEL), row(D_MODEL), _full((1, HALF_W))],
        out_shape=[sd(PROJ_W), sd(512), jax.ShapeDtypeStruct((512, s), BF16), sd(512), sd(D_MODEL), sd(D_MODEL), sd(D_MODEL),
                   jax.ShapeDtypeStruct((1, HALF_W), F32)],
        compiler_params=_params(("arbitrary",)),
    )(dx1, oa, of, om, proj, b_gate, wa, wf, wm, w_out)


def _swa_valid_t(n):
    key = lax.broadcasted_iota(jnp.int32, (2 * SWA_BLOCK, SWA_BLOCK), 0)
    qry = lax.broadcasted_iota(jnp.int32, (2 * SWA_BLOCK, SWA_BLOCK), 1)
    dist = qry + SWA_BLOCK - key
    return (dist >= 0) & (dist < SWA_BLOCK) & ((key >= SWA_BLOCK) | (n > 0))


def _swa_bwd(qa, kp, vp, bias_t, sink, doa):
    s = qa.shape[0]
    nb = s // SWA_BLOCK

    def body(sink_ref, q_ref, kp_ref, vp_ref, bias_ref, do_ref, dq_ref, dkp_ref, dvp_ref, dbias_ref, dsink_ref, sk_acc):
        n = pl.program_id(0)

        @pl.when(n == 0)
        def _():
            dkp_ref[...] = jnp.zeros_like(dkp_ref)
            dvp_ref[...] = jnp.zeros_like(dvp_ref)
            dbias_ref[...] = jnp.zeros_like(dbias_ref)
            sk_acc[...] = jnp.zeros_like(sk_acc)

        start = pl.multiple_of(n * SWA_BLOCK, SWA_BLOCK)
        win = pl.ds(start, 2 * SWA_BLOCK)
        k2 = kp_ref[win, :]
        v2 = vp_ref[win, :]
        valid = _swa_valid_t(n)
        heads = range(SWA_HEADS)
        hs = lambda h: slice(h * HEAD, (h + 1) * HEAD)
        scale = jnp.asarray(0.125, BF16)
        q = [q_ref[:, hs(h)] for h in heads]
        do = [do_ref[:, hs(h)] for h in heads]
        kk = [k2[:, hs(kv)] for kv in range(2)]
        vv = [v2[:, hs(kv)] for kv in range(2)]
        kt = [(kk[kv].astype(F32) * 0.125).T.astype(BF16) for kv in range(2)]
        st = [jnp.where(valid, _dot(kk[h // 4], q[h], NT) * 0.125 + bias_ref[h], NEG) for h in heads]
        dpt = [_dot(vv[h // 4], do[h], NT) for h in heads]
        pt, dst = [], []
        for h in heads:
            sk = sink_ref[h]
            mx = jnp.maximum(jnp.max(st[h], axis=0, keepdims=True), sk)
            p = jnp.exp(st[h] - mx)
            esk = jnp.exp(sk - mx)
            den = jnp.sum(p, axis=0, keepdims=True) + esk
            p = p / den
            delta = jnp.sum(p * dpt[h], axis=0, keepdims=True)
            d = p * (dpt[h] - delta)
            sk_acc[h:h + 1, :] += -(esk / den) * delta
            dbias_ref[h] += d
            pt.append(p.astype(BF16))
            dst.append(d.astype(BF16))
        dq_t = [_dot(kt[h // 4], dst[h]) for h in heads]
        dq_ref[...] = jnp.concatenate(dq_t, axis=0).T.astype(BF16)
        for kv in range(2):
            group = range(4 * kv, 4 * kv + 4)
            dk = [_dot(dst[h], q[h] * scale) for h in group]
            dv = [_dot(pt[h], do[h]) for h in group]
            dkp_ref[win, hs(kv)] += (dk[0] + dk[1]) + (dk[2] + dk[3])
            dvp_ref[win, hs(kv)] += (dv[0] + dv[1]) + (dv[2] + dv[3])

        @pl.when(n == nb - 1)
        def _():
            dsink_ref[...] = jnp.broadcast_to(jnp.sum(sk_acc[...], axis=1, keepdims=True), dsink_ref.shape)

    return pl.pallas_call(
        body, name="swa_bwd", grid=(nb,),
        in_specs=[pl.BlockSpec(memory_space=pltpu.SMEM),
                  pl.BlockSpec((SWA_BLOCK, 512), lambda n: (n, 0)),
                  _full(kp.shape), _full(vp.shape), _full(bias_t.shape),
                  pl.BlockSpec((SWA_BLOCK, 512), lambda n: (n, 0))],
        out_specs=[pl.BlockSpec((SWA_BLOCK, 512), lambda n: (n, 0)), _full(kp.shape), _full(vp.shape),
                   _full(bias_t.shape), _full((SWA_HEADS, 128))],
        out_shape=[jax.ShapeDtypeStruct((s, 512), BF16), jax.ShapeDtypeStruct(kp.shape, F32),
                   jax.ShapeDtypeStruct(vp.shape, F32), jax.ShapeDtypeStruct(bias_t.shape, F32),
                   jax.ShapeDtypeStruct((SWA_HEADS, 128), F32)],
        scratch_shapes=[pltpu.VMEM((SWA_HEADS, 128), F32)],
        compiler_params=_params(("arbitrary",)),
    )(sink, qa, kp, vp, bias_t, doa)


def _fox_bwd(qt, k, v, dot, ot, cc4, lse4):
    s = k.shape[0]
    t = min(FOX_BWD_T, s)
    nq = s // t

    def body(qt_ref, k_ref, v_ref, dot_ref, ot_ref, cc_ref, lse_ref,
             dqt_ref, dk_ref, dv_ref, dck_ref, dcq_ref, delta_ref, dk0, dk1, dv0, dv1, ds0, ds1):
        j = pl.program_id(1)

        @pl.when(j == 0)
        def _():
            dqt_ref[...] = jnp.zeros_like(dqt_ref)
            dcq_ref[...] = jnp.zeros_like(dcq_ref)
            r8 = lax.broadcasted_iota(jnp.int32, (8, t), 0)

            def dl(i, c):
                cols = pl.ds(pl.multiple_of(i * t, t), t)
                pr = dot_ref[:, cols].astype(F32) * ot_ref[:, cols].astype(F32)
                d0 = jnp.sum(jnp.where(_head_rows(0), pr, 0.0), axis=0, keepdims=True)
                d1 = jnp.sum(jnp.where(_head_rows(1), pr, 0.0), axis=0, keepdims=True)
                delta_ref[:, cols] = jnp.where(r8 == 0, d0, jnp.where(r8 == 1, d1, 0.0))
                return c

            lax.fori_loop(0, nq, dl, 0)

        kj = k_ref[...]
        vj = v_ref[...]
        ks = pl.ds(pl.multiple_of(j * t, t), t)
        kt = (kj.astype(F32) * 0.125).T.astype(BF16)
        ke = [jnp.where(_head_mask(e), kj, jnp.zeros_like(kj)) for e in range(2)]
        ve = [jnp.where(_head_mask(e), vj, jnp.zeros_like(vj)) for e in range(2)]
        kte = [jnp.where(_head_rows(e), kt, jnp.zeros_like(kt)) for e in range(2)]
        ck = [cc_ref[0, ks, e:e + 1] for e in range(2)]
        accs = ((dk0, dv0, ds0), (dk1, dv1, ds1))
        for refs in accs:
            for r in refs:
                r[...] = jnp.zeros_like(r)

        def block(q0, nqs, k0, nks, masked):
            cols = pl.ds(pl.multiple_of(q0, 128), nqs)
            rows = slice(k0, k0 + nks)
            qti = qt_ref[:, cols]
            doti = dot_ref[:, cols]
            for e in range(2):
                dkt_acc, dvt_acc, ds_acc = accs[e]
                st = _dot(ke[e][rows, :], qti) - ck[e][rows, :]
                if masked:
                    krow = lax.broadcasted_iota(jnp.int32, (nks, nqs), 0) + (j * t + k0)
                    qcol = lax.broadcasted_iota(jnp.int32, (nks, nqs), 1) + q0
                    st = jnp.where(krow <= qcol, st, NEG)
                pt = jnp.exp(st - lse_ref[0, e:e + 1, cols])
                dpt = _dot(ve[e][rows, :], doti)
                dst = pt * (dpt - delta_ref[e:e + 1, cols])
                dsb = dst.astype(BF16)
                dvt_acc[:, rows] += _dot(doti, pt.astype(BF16), NT)
                dkt_acc[:, rows] += _dot(qti, dsb, NT)
                dqt_ref[:, cols] += _dot(kte[e][:, rows], dsb)
                ds_acc[rows, 0:nqs] += dst
                dcq_ref[0, e:e + 1, cols] += jnp.sum(dst, axis=0, keepdims=True)

        half = t // 2
        block(j * t, half, 0, half, True)
        block(j * t + half, half, 0, t, True)

        def rest(i, c):
            block(i * t, t, 0, t, False)
            return c

        lax.fori_loop(j + 1, nq, rest, 0)
        r0 = _head_rows(0)
        dk_ref[...] = jnp.where(r0, dk0[...], dk1[...]).T.astype(BF16)
        dv_ref[...] = jnp.where(r0, dv0[...], dv1[...]).T.astype(BF16)
        lane = lax.broadcasted_iota(jnp.int32, (t, 128), 1)
        c0 = jnp.sum(ds0[...], axis=-1, keepdims=True)
        c1 = jnp.sum(ds1[...], axis=-1, keepdims=True)
        dck_ref[0] = jnp.where(lane == 0, c0, jnp.where(lane == 1, c1, 0.0))

    res_t = lambda: pl.BlockSpec((128, s), lambda hp, j: (hp, 0))
    blk = lambda: pl.BlockSpec((t, 128), lambda hp, j: (j, hp))
    return pl.pallas_call(
        body, name="fox_bwd", grid=(4, nq),
        in_specs=[res_t(), blk(), blk(), res_t(), res_t(), pl.BlockSpec((1, s, 128), lambda hp, j: (hp, 0, 0)),
                  pl.BlockSpec((1, 8, s), lambda hp, j: (hp, 0, 0))],
        out_specs=[res_t(), blk(), blk(),
                   pl.BlockSpec((1, t, 128), lambda hp, j: (hp, j, 0)),
                   pl.BlockSpec((1, 8, s), lambda hp, j: (hp, 0, 0))],
        out_shape=[jax.ShapeDtypeStruct((512, s), F32), jax.ShapeDtypeStruct((s, 512), BF16),
                   jax.ShapeDtypeStruct((s, 512), BF16), jax.ShapeDtypeStruct((4, s, 128), F32),
                   jax.ShapeDtypeStruct((4, 8, s), F32)],
        scratch_shapes=[pltpu.VMEM((8, s), F32)] + [pltpu.VMEM((128, t), F32)] * 4 + [pltpu.VMEM((t, t), F32)] * 2,
        compiler_params=_params(("arbitrary", "arbitrary")),
    )(qt, k, v, dot, ot, cc4, lse4)


def _mem_bwd(qm, mk, mv, dom):
    s = qm.shape[0]
    tq = min(512, s)

    def body(q_ref, mk_ref, mv_ref, do_ref, dq_ref, dmk_ref, dmv_ref):
        i = pl.program_id(0)

        @pl.when(i == 0)
        def _():
            dmk_ref[...] = jnp.zeros_like(dmk_ref)
            dmv_ref[...] = jnp.zeros_like(dmv_ref)

        heads = range(MEM_HEADS)
        hs = lambda h: slice(h * 128, (h + 1) * 128)
        sc = [_dot(q_ref[:, hs(h)], mk_ref[:, hs(h)], NT) * MEM_SCALE for h in heads]
        dp = [_dot(do_ref[:, hs(h)], mv_ref[:, hs(h)], NT) for h in heads]
        pb, dsb = [], []
        for h in heads:
            p = jnp.exp(sc[h] - jnp.max(sc[h], axis=-1, keepdims=True))
            p = p / jnp.sum(p, axis=-1, keepdims=True)
            ds = p * (dp[h] - jnp.sum(p * dp[h], axis=-1, keepdims=True))
            pb.append(p.astype(BF16))
            dsb.append((ds * MEM_SCALE).astype(BF16))
        dq = [_dot(dsb[h], mk_ref[:, hs(h)]).astype(BF16) for h in heads]
        dmk = [_dot(dsb[h], q_ref[:, hs(h)], TN) for h in heads]
        dmv = [_dot(pb[h], do_ref[:, hs(h)], TN) for h in heads]
        for h in heads:
            dq_ref[:, hs(h)] = dq[h]
            dmk_ref[:, hs(h)] += dmk[h]
            dmv_ref[:, hs(h)] += dmv[h]

    return pl.pallas_call(
        body, name="mem_bwd", grid=(s // tq,),
        in_specs=[pl.BlockSpec((tq, 512), lambda i: (i, 0)), _full(mk.shape), _full(mv.shape),
                  pl.BlockSpec((tq, 512), lambda i: (i, 0))],
        out_specs=[pl.BlockSpec((tq, 512), lambda i: (i, 0)), _full(mk.shape), _full(mv.shape)],
        out_shape=[jax.ShapeDtypeStruct((s, 512), BF16), jax.ShapeDtypeStruct(mk.shape, F32),
                   jax.ShapeDtypeStruct(mv.shape, F32)],
        compiler_params=_params(("arbitrary",)),
    )(qm, mk, mv, dom)


def _memkv_bwd(dmk, dmv, kv_raw, kn_mem, mem, g_mem, mem_n, w_kv):
    def body(dmk_ref, dmv_ref, kv_ref, kn_ref, mem_ref, g_ref, mn_ref, w_ref, dw_ref, dkn_ref, dg_ref, dkv_ref):
        dkn = jnp.zeros((1, 128), F32)
        for h in range(MEM_HEADS):
            hs = slice(h * 128, (h + 1) * 128)
            v = kv_ref[:, hs]
            r = lax.rsqrt(jnp.mean(v * v, axis=-1, keepdims=True) + EPS)
            n = v * r
            dn = dmk_ref[:, hs]
            dkn = dkn + jnp.sum(dn * n, axis=0, keepdims=True)
            dng = dn * kn_ref[...]
            dkv_ref[:, hs] = (r * (dng - n * jnp.mean(dng * n, axis=-1, keepdims=True))).astype(BF16)
        dkv_ref[:, 512:1024] = dmv_ref[...].astype(BF16)
        dkn_ref[...] = dkn
        dkv = dkv_ref[...]
        dw_ref[...] = _dot(mn_ref[...], dkv, TN).astype(BF16)
        dmn = _dot(dkv, w_ref[...], NT)
        xv = mem_ref[...]
        r = lax.rsqrt(jnp.mean(xv * xv, axis=-1, keepdims=True) + EPS)
        dg_ref[...] = jnp.sum(dmn * (xv * r), axis=0, keepdims=True)

    m = mem.shape[0]
    return pl.pallas_call(
        body, name="memkv_bwd",
        out_shape=[jax.ShapeDtypeStruct((D_MODEL, 1024), BF16), jax.ShapeDtypeStruct((1, 128), F32),
                   jax.ShapeDtypeStruct((1, D_MODEL), F32)],
        scratch_shapes=[pltpu.VMEM((m, 1024), BF16)],
        compiler_params=pltpu.CompilerParams(vmem_limit_bytes=VMEM_LIMIT),
    )(dmk, dmv, kv_raw, kn_mem, mem, g_mem, mem_n, w_kv)


def _fox_gate_bwd(dcq4, dck4, proj, b_forget128):
    s = dck4.shape[1]
    tm = min(512, s)
    nt = s // tm

    def body(dcq_ref, dck_ref, p_ref, b_ref, dfl_ref, db_ref, carry_ref):
        i = pl.program_id(0)

        @pl.when(i == 0)
        def _():
            carry_ref[...] = jnp.zeros_like(carry_ref)
            db_ref[...] = jnp.zeros_like(db_ref)

        dcv = jnp.zeros((tm, 128), F32)
        for hp in range(4):
            by_query = jnp.concatenate([dcq_ref[hp], jnp.zeros((120, tm), F32)], axis=0).T
            d = by_query - dck_ref[hp]
            dcv = dcv + (d if hp == 0 else pltpu.roll(d, 2 * hp, 1))
        dlogf = jnp.dot(_tri(tm, False), dcv, precision=lax.Precision.HIGHEST, preferred_element_type=F32) + carry_ref[...]
        carry_ref[...] += jnp.sum(dcv, axis=0, keepdims=True)
        z = p_ref[...] + b_ref[...]
        dfl = dlogf * (1.0 / (1.0 + jnp.exp(z)))
        dfl_ref[...] = dfl.astype(BF16)
        db_ref[...] += jnp.sum(dfl, axis=0, keepdims=True)

    return pl.pallas_call(
        body, name="fox_gate_bwd", grid=(nt,),
        in_specs=[pl.BlockSpec((4, 8, tm), lambda i: (0, 0, nt - 1 - i)),
                  pl.BlockSpec((4, tm, 128), lambda i: (0, nt - 1 - i, 0)),
                  pl.BlockSpec((tm, 128), lambda i: (nt - 1 - i, 0)), _full((1, 128))],
        out_specs=[pl.BlockSpec((tm, 128), lambda i: (nt - 1 - i, 0)), _full((1, 128))],
        out_shape=[jax.ShapeDtypeStruct((s, 128), BF16), jax.ShapeDtypeStruct((1, 128), F32)],
        scratch_shapes=[pltpu.VMEM((1, 128), F32)],
        compiler_params=_params(("arbitrary",)),
    )(dcq4, dck4, proj, b_forget128)


def _proj_pre_bwd(dproj, proj, dqf, dkf, dvf, dqm, dqa, dka, dva, dfl, gq_fox, gk_fox, gq_mem, gq_swa, gk_swa):
    s = proj.shape[0]
    tm = min(256, s)

    def body(dp_in, p_ref, dqf_ref, dkf_ref, dvf_ref, dqm_ref, dqa_ref, dka_ref, dva_ref, dfl_ref,
             gqf, gkf, gqm, gqa, gka, dp_ref, dgn_ref):
        i = pl.program_id(0)

        @pl.when(i == 0)
        def _():
            dgn_ref[...] = jnp.zeros_like(dgn_ref)

        def norm_bwd(off, width, hd, g_ref, dn_ref, slot):
            acc = jnp.zeros((1, 128), F32)
            for b in range(width // 128):
                v = p_ref[:, off + b * 128: off + (b + 1) * 128].astype(F32)
                r = lax.rsqrt(_group_mean(v * v, hd) + EPS)
                n = v * r
                dn = dn_ref[b * 128:(b + 1) * 128, :].T if slot == 0 else dn_ref[:, b * 128:(b + 1) * 128].astype(F32)
                acc = acc + jnp.sum(dn * n, axis=0, keepdims=True)
                dng = dn * g_ref[...]
                dp_ref[:, off + b * 128: off + (b + 1) * 128] = (r * (dng - n * _group_mean(dng * n, hd))).astype(BF16)
            dgn_ref[slot:slot + 1, :] += acc

        norm_bwd(H_QF, 512, HEAD, gqf, dqf_ref, 0)
        norm_bwd(H_KF, 512, HEAD, gkf, dkf_ref, 1)
        dp_ref[:, H_VF:H_VF + 512] = dvf_ref[...].astype(BF16)
        norm_bwd(H_QM, 512, MEM_HEAD, gqm, dqm_ref, 2)
        norm_bwd(H_QA, 512, HEAD, gqa, dqa_ref, 3)
        norm_bwd(H_KA, 128, HEAD, gka, dka_ref, 4)
        dp_ref[:, H_VA:H_VA + 128] = dva_ref[...].astype(BF16)
        dp_ref[:, H_FL:H_FL + 128] = dfl_ref[...]
        dp_ref[:, H_FL + 128:HALF_W] = jnp.zeros((tm, HALF_W - H_FL - 128), BF16)

    row = lambda w: pl.BlockSpec((tm, w), lambda i: (i, 0))
    g_spec = _full((1, 128))
    return pl.pallas_call(
        body, name="proj_pre_bwd", grid=(s // tm,),
        in_specs=[pl.BlockSpec(memory_space=pl.ANY), pl.BlockSpec((tm, HALF_W), lambda i: (i, 1)),
                  pl.BlockSpec((512, tm), lambda i: (0, i)), row(512), row(512), row(512), row(512),
                  row(128), row(128), row(128), g_spec, g_spec, g_spec, g_spec, g_spec],
        out_specs=[pl.BlockSpec((tm, HALF_W), lambda i: (i, 1)), _full((8, 128))],
        out_shape=[jax.ShapeDtypeStruct((s, PROJ_W), BF16), jax.ShapeDtypeStruct((8, 128), F32)],
        input_output_aliases={0: 0},
        compiler_params=_params(("arbitrary",)),
    )(dproj, proj, dqf, dkf, dvf, dqm, dqa, dka, dva, dfl, gq_fox, gk_fox, gq_mem, gq_swa, gk_swa)


def _in_bwd_x(dproj, w_in_p, x, g_mix, dx1):
    s = x.shape[0]
    tm = min(256, s)

    def body(dp_ref, w_ref, x_ref, g_ref, dx1_ref, gx_ref, dg_ref):
        i = pl.program_id(0)

        @pl.when(i == 0)
        def _():
            dg_ref[...] = jnp.zeros_like(dg_ref)

        dx, dg = _rms_bwd(x_ref[...], g_ref[...], _dot(dp_ref[...], w_ref[...], NT), dx1_ref[...])
        gx_ref[...] = dx
        dg_ref[...] += dg

    row = pl.BlockSpec((tm, D_MODEL), lambda i: (i, 0))
    return pl.pallas_call(
        body, name="in_bwd_x", grid=(s // tm,),
        in_specs=[pl.BlockSpec((tm, PROJ_W), lambda i: (i, 0)), _full(w_in_p.shape), row, _full((1, D_MODEL)), row],
        out_specs=[row, _full((1, D_MODEL))],
        out_shape=[jax.ShapeDtypeStruct((s, D_MODEL), F32), jax.ShapeDtypeStruct((1, D_MODEL), F32)],
        compiler_params=_params(("arbitrary",)),
    )(dproj, w_in_p, x, g_mix, dx1)


def _rel_bias_bwd(dbias, bucket):
    def body(db_ref, bk_ref, o_ref):
        bk = bk_ref[...]
        lane = lax.broadcasted_iota(jnp.int32, (1, 128), 1)
        for b in range(REL_BUCKETS):
            sel = bk == b
            acc = jnp.zeros((1, 128), F32)
            for h in range(SWA_HEADS):
                tot = jnp.sum(jnp.sum(jnp.where(sel, db_ref[h], 0.0), axis=-1, keepdims=True), axis=0, keepdims=True)
                acc = jnp.where(lane == h, tot, acc)
            o_ref[:, b * 128:(b + 1) * 128] = acc

    return pl.pallas_call(
        body, name="rel_bias_bwd",
        out_shape=jax.ShapeDtypeStruct((1, REL_BUCKETS * 128), F32),
        compiler_params=pltpu.CompilerParams(vmem_limit_bytes=VMEM_LIMIT),
    )(dbias, bucket)


def _my_place():
    return lax.axis_index("x"), lax.axis_index("y"), lax.axis_index("c")


def _peer(place, k):
    x, y, c = place
    return (1 - x if k & 4 else x, 1 - y if k & 2 else y, 1 - c if k & 1 else c)


def _index(place):
    x, y, c = place
    return 4 * x + 2 * y + c


HBM_SPEC = pl.BlockSpec(memory_space=pltpu.HBM)
SEM_SPEC = pl.BlockSpec(memory_space=pltpu.SEMAPHORE)
DATAFLOW = pltpu.SideEffectType.DATAFLOW_SIDE_EFFECTING


ALL_PEERS = tuple(range(1, N_DEV))
SAME_CORE = (2, 4, 6)
OWN = N_DEV - 1


def _split_copy(src_ref, land_ref, send_sems, recv_sems, me, k, gather):
    peer = _peer(me, k)
    if gather:
        src, dst = src_ref, land_ref.at[_index(me)]
    else:
        src, dst = src_ref.at[_index(peer)], land_ref.at[k - 1]
    return pltpu.make_async_remote_copy(src_ref=src, dst_ref=dst, send_sem=send_sems.at[k - 1], recv_sem=recv_sems.at[k - 1],
                                        device_id=peer, device_id_type=MESH)


def _own_copy(src_ref, land_ref, recv_sems, me, gather):
    if gather:
        src, dst = src_ref, land_ref.at[_index(me)]
    else:
        src, dst = src_ref.at[_index(me)], land_ref.at[OWN]
    return pltpu.make_async_copy(src, dst, recv_sems.at[OWN])


def _split_start(srcs, gather, name, peers=ALL_PEERS, after=None):
    n = len(srcs)
    extra = [] if after is None else [after]

    def body(*refs):
        refs = refs[:2 * n] + refs[2 * n + len(extra):]
        src_refs, land_refs = refs[:n], refs[n:2 * n]
        send_sems, recv_sems, token = refs[2 * n:3 * n], refs[3 * n:4 * n], refs[-1]
        me = _my_place()
        for w in range(n):
            for k in peers:
                _split_copy(src_refs[w], land_refs[w], send_sems[w], recv_sems[w], me, k, gather).start()
            _own_copy(src_refs[w], land_refs[w], recv_sems[w], me, gather).start()
        token[...] = jnp.zeros_like(token)

    lands = [lax.empty((N_DEV,) + (a.shape if gather else a.shape[1:]), a.dtype) for a in srcs]
    sems = [pltpu.SemaphoreType.DMA((N_DEV,))] * (2 * n)
    hbm = [pltpu.HBM(a.shape, a.dtype) for a in list(srcs) + lands]
    outs = pl.pallas_call(
        body, name=name,
        out_shape=(*sems, *hbm, jax.ShapeDtypeStruct((8, 128), F32)),
        in_specs=(HBM_SPEC,) * (2 * n) + (pl.BlockSpec(memory_space=pl.ANY),) * len(extra),
        out_specs=(SEM_SPEC,) * (2 * n) + (HBM_SPEC,) * (2 * n) + (pl.BlockSpec(memory_space=pltpu.VMEM),),
        input_output_aliases={i: 2 * n + i for i in range(2 * n)},
        compiler_params=pltpu.CompilerParams(has_side_effects=DATAFLOW),
    )(*[pltpu.with_memory_space_constraint(a, pltpu.HBM) for a in list(srcs) + lands], *extra)
    return list(outs[:n]), list(outs[n:2 * n]), list(outs[2 * n:3 * n]), list(outs[3 * n:4 * n]), outs[-1]


def _split_wait(started, w, after, gather, name):
    send_sems, recv_sems, srcs, lands, _ = started

    def body(src_ref, land_ref, send_sems, recv_sems, after_ref, src_out, land_out):
        me = _my_place()
        for k in ALL_PEERS:
            cp = _split_copy(src_ref, land_ref, send_sems, recv_sems, me, k, gather)
            cp.wait_send()
            cp.wait_recv()
        _own_copy(src_ref, land_ref, recv_sems, me, gather).wait()

    return pl.pallas_call(
        body, name=name,
        out_shape=(pltpu.HBM(srcs[w].shape, srcs[w].dtype), pltpu.HBM(lands[w].shape, lands[w].dtype)),
        in_specs=(HBM_SPEC, HBM_SPEC, SEM_SPEC, SEM_SPEC, pl.BlockSpec(memory_space=pl.ANY)),
        out_specs=(HBM_SPEC, HBM_SPEC), input_output_aliases={0: 0, 1: 1},
        compiler_params=pltpu.CompilerParams(has_side_effects=DATAFLOW),
    )(srcs[w], lands[w], send_sems[w], recv_sems[w], after)[1]


def _forward_copy(land_ref, send_sems, recv_sems, me, j, incoming):
    sibling = _peer(me, 1)
    rows = land_ref.at[_index(_peer(sibling if incoming else me, SAME_CORE[j]))]
    return pltpu.make_async_remote_copy(src_ref=rows, dst_ref=rows, send_sem=send_sems.at[j], recv_sem=recv_sems.at[j],
                                        device_id=sibling, device_id_type=MESH)


def _forward_start(started, after, name):
    send_a, recv_a, srcs, lands, _ = started

    def body(src_ref, land_ref, send_a, recv_a, after_ref, send_b, recv_b, src_out, land_out):
        me = _my_place()
        for j, k in enumerate(SAME_CORE):
            _split_copy(src_ref, land_ref, send_a, recv_a, me, k, True).wait_recv()
            _forward_copy(land_ref, send_b, recv_b, me, j, False).start()

    sems = pltpu.SemaphoreType.DMA((len(SAME_CORE),))
    return pl.pallas_call(
        body, name=name,
        out_shape=(sems, sems, pltpu.HBM(srcs[0].shape, srcs[0].dtype), pltpu.HBM(lands[0].shape, lands[0].dtype)),
        in_specs=(HBM_SPEC, HBM_SPEC, SEM_SPEC, SEM_SPEC, pl.BlockSpec(memory_space=pl.ANY)),
        out_specs=(SEM_SPEC, SEM_SPEC, HBM_SPEC, HBM_SPEC), input_output_aliases={0: 2, 1: 3},
        compiler_params=pltpu.CompilerParams(has_side_effects=DATAFLOW),
    )(srcs[0], lands[0], send_a[0], recv_a[0], after)


def _forward_wait(started, forwarded, name):
    send_a, recv_a, _, _, _ = started
    send_b, recv_b, src, land = forwarded

    def body(src_ref, land_ref, send_a, recv_a, send_b, recv_b, src_out, land_out):
        me = _my_place()
        _own_copy(src_ref, land_ref, recv_a, me, True).wait()
        for k in (1,) + SAME_CORE:
            _split_copy(src_ref, land_ref, send_a, recv_a, me, k, True).wait_send()
        _split_copy(src_ref, land_ref, send_a, recv_a, me, 1, True).wait_recv()
        for j in range(len(SAME_CORE)):
            _forward_copy(land_ref, send_b, recv_b, me, j, False).wait_send()
            _forward_copy(land_ref, send_b, recv_b, me, j, True).wait_recv()

    return pl.pallas_call(
        body, name=name,
        out_shape=(pltpu.HBM(src.shape, src.dtype), pltpu.HBM(land.shape, land.dtype)),
        in_specs=(HBM_SPEC, HBM_SPEC, SEM_SPEC, SEM_SPEC, SEM_SPEC, SEM_SPEC),
        out_specs=(HBM_SPEC, HBM_SPEC), input_output_aliases={0: 0, 1: 1},
        compiler_params=pltpu.CompilerParams(has_side_effects=DATAFLOW),
    )(src, land, send_a[0], recv_a[0], send_b, recv_b)[1]


def _adam_math(w, g, m, v):
    m2 = ADAM_B1 * m + (1.0 - ADAM_B1) * g
    v2 = ADAM_B2 * v + (1.0 - ADAM_B2) * (g * g)
    m_hat = m2 / (1.0 - ADAM_B1 ** ADAM_STEP)
    v_hat = v2 / (1.0 - ADAM_B2 ** ADAM_STEP)
    delta = -ADAM_LR * (m_hat / (jnp.sqrt(v_hat) + ADAM_EPS) + ADAM_WD * w)
    return delta, m2, v2


def _adamw(land, w, m, v, name):
    a, b = w.shape
    bp = land.shape[2]
    ta = min(128, a)

    def body(p_ref, w_ref, m_ref, v_ref, g_ref, d_ref, m2_ref, v2_ref):
        g = p_ref[0, :, 0:b].astype(F32)
        for k in range(1, N_DEV):
            g = g + p_ref[k, :, 0:b].astype(F32)
        delta, m2, v2 = _adam_math(w_ref[...], g, m_ref[...], v_ref[...])
        g_ref[...] = g
        d_ref[...] = delta
        m2_ref[...] = m2
        v2_ref[...] = v2

    blk = pl.BlockSpec((ta, b), lambda i: (i, 0))
    sd = jax.ShapeDtypeStruct((a, b), F32)
    return pl.pallas_call(
        body, name=name, grid=(a // ta,),
        in_specs=[pl.BlockSpec((N_DEV, ta, bp), lambda i: (0, i, 0)), blk, blk, blk],
        out_specs=[blk, blk, blk, blk], out_shape=[sd, sd, sd, sd],
        compiler_params=_params(("parallel",)),
    )(land, w, m, v)


def _bucket_table():
    t_loc = jnp.arange(SWA_BLOCK)[:, None] + SWA_BLOCK
    s_loc = jnp.arange(2 * SWA_BLOCK)[None, :]
    dist = t_loc - s_loc
    max_exact = REL_BUCKETS // 2
    d = jnp.maximum(dist, 0)
    df = jnp.maximum(d, 1).astype(F32)
    large = max_exact + (jnp.log(df / max_exact) / math.log(REL_MAX_DIST / max_exact) * (REL_BUCKETS - max_exact)).astype(jnp.int32)
    large = jnp.minimum(large, REL_BUCKETS - 1)
    bucket = jnp.where(d < max_exact, d, large)
    band = (dist >= 0) & (dist < SWA_BLOCK)
    return bucket, band


def _tile2(g):
    return jnp.concatenate([g, g], axis=1) if g.shape[1] == HEAD else g


SHARD_W = 737
SHARD_WP = 768
IN_WIDTH = N_DEV * SHARD_W
SEGMENTS = ((GL0, 2824, 3072), (QF0, 768, 512), (KF0, 1280, 512), (VF0, 1792, 512), (QM0, 2312, 512),
            (QA0, 0, 512), (KA0, 512, 128), (VA0, 640, 128), (FL0, 2304, 8))


def _lane_plan(sources):
    plan = []
    for t in range(len(sources) // 128):
        groups = {}
        for lane in range(128):
            src = sources[128 * t + lane]
            if src is not None:
                slab, col = src
                groups.setdefault((slab, col // 128, (lane - col) % 128), []).append(lane)
        tile = []
        for key, lanes in groups.items():
            assert lanes == list(range(lanes[0], lanes[-1] + 1))
            tile.append((key, lanes[0], lanes[-1] + 1))
        plan.append(tile)
    return plan


def _assemble(tile_plan, load, rows):
    lane = lax.broadcasted_iota(jnp.int32, (1, 128), 1)
    out = jnp.zeros((rows, 128), F32)
    for (slab, st, roll), lo, hi in tile_plan:
        v = load(slab, st)
        if roll:
            v = pltpu.roll(v, roll, 1)
        out = v if (lo, hi) == (0, 128) else jnp.where((lane >= lo) & (lane < hi), v, out)
    return out


def _w_in_from_shards(land):
    ref_col = [None] * PROJ_W
    for p0, r0, n in SEGMENTS:
        for i in range(n):
            ref_col[p0 + i] = divmod(r0 + i, SHARD_W)
    plan = _lane_plan(ref_col)
    d_model = land.shape[1]
    tm = 256

    def body(land_ref, o_ref):
        load = lambda slab, st: land_ref[slab, :, st * 128:(st + 1) * 128].astype(F32)
        for t, tile_plan in enumerate(plan):
            o_ref[:, t * 128:(t + 1) * 128] = _assemble(tile_plan, load, tm).astype(BF16)

    return pl.pallas_call(
        body, name="w_in_from_shards", grid=(d_model // tm,),
        in_specs=[pl.BlockSpec((N_DEV, tm, SHARD_WP), lambda i: (0, i, 0))],
        out_specs=pl.BlockSpec((tm, PROJ_W), lambda i: (i, 0)),
        out_shape=jax.ShapeDtypeStruct((d_model, PROJ_W), BF16),
        compiler_params=_params(("parallel",)),
    )(land)


def _dw_in_to_parts(dwp):
    padded_col = [None] * IN_WIDTH
    for p0, r0, n in SEGMENTS:
        for i in range(n):
            padded_col[r0 + i] = p0 + i
    sources = []
    for d in range(N_DEV):
        sources += [(0, padded_col[SHARD_W * d + c]) if c < SHARD_W else None for c in range(SHARD_WP)]
    plan = _lane_plan(sources)
    d_model = dwp.shape[0]
    tm = 256
    tiles = SHARD_WP // 128

    def body(dw_ref, o_ref):
        load = lambda slab, st: dw_ref[:, st * 128:(st + 1) * 128].astype(F32)
        for t, tile_plan in enumerate(plan):
            d, c = divmod(t, tiles)
            o_ref[d, :, c * 128:(c + 1) * 128] = _assemble(tile_plan, load, tm).astype(BF16)

    return pl.pallas_call(
        body, name="dw_in_to_parts", grid=(d_model // tm,),
        in_specs=[pl.BlockSpec((tm, PROJ_W), lambda i: (i, 0))],
        out_specs=pl.BlockSpec((N_DEV, tm, SHARD_WP), lambda i: (0, i, 0)),
        out_shape=jax.ShapeDtypeStruct((N_DEV, d_model, SHARD_WP), BF16),
        compiler_params=_params(("parallel",)),
    )(dwp)


def _cast_shards(shards):
    names = list(shards)

    def body(*refs):
        for src, dst in zip(refs[:len(names)], refs[len(names):]):
            if dst.shape != src.shape:
                dst[...] = jnp.zeros(dst.shape, BF16)
                dst[:, 0:src.shape[1]] = src[...].astype(BF16)
            else:
                dst[...] = src[...].astype(BF16)

    out_shape = [jax.ShapeDtypeStruct((shards[n].shape[0], SHARD_WP if n == "w_in" else shards[n].shape[1]), BF16)
                 for n in names]
    outs = pl.pallas_call(body, name="cast_shards", out_shape=out_shape,
                          compiler_params=pltpu.CompilerParams(vmem_limit_bytes=VMEM_LIMIT))(*[shards[n] for n in names])
    return dict(zip(names, outs))


def _tie(x, *tokens):
    for t in tokens:
        if t is not None:
            x = x + t[0:1, 0:1]
    return x


def _local_step(x, mem, target, p, getw, emit, deps=()):
    s = x.shape[0]
    bucket, band = _bucket_table()
    bucket_m = jnp.where(band, bucket, -1).astype(jnp.int32)
    bias = _bias_table(p["rel_bias"], bucket_m)
    bucket_t = jnp.transpose(bucket_m)
    bias_t = _bias_table(p["rel_bias"], bucket_t)
    gqf, gkf, gqa, gka = _tile2(p["qn_fox"]), _tile2(p["kn_fox"]), _tile2(p["qn_swa"]), _tile2(p["kn_swa"])
    gqm = p["qn_mem"]
    bf128 = jnp.pad(p["b_forget"], ((0, 0), (0, 120)))
    sink = p["sink_swa"].reshape(8)

    h = _rms_fwd(x, p["g_mix"], "rms_mix", tuple(deps) + (bias, bias_t))
    w_in = getw("w_in", h)
    proj = _mm(h, w_in, "nn", BF16, 512, 1536, 1024, "proj")
    fl = _mm(h, w_in[:, FL0:FL0 + 128], "nn", F32, 512, 128, 1024, "proj_fl")
    qf, kf, vf, qm, qa, ka, va, qf_t, vf_t = _proj_post(proj, gqf, gkf, gqm, gqa, gka)
    cc4, ca4 = _fox_gate_fwd(fl, bf128)
    w_kv = getw("w_mem_kv", cc4)
    mem_n, kv_raw, mk, mv = _memkv_fwd(mem, p["g_mem"], w_kv, p["kn_mem"])
    kp = jnp.pad(ka, ((SWA_BLOCK, 0), (0, 0)))
    vp = jnp.pad(va, ((SWA_BLOCK, 0), (0, 0)))
    oa = _swa_fwd(qa, kp, vp, bias, sink)
    of, lse4, of_t = _fox_fwd(qf, kf, vf_t, ca4)
    om = _mem_fwd(qm, mk, mv)
    wa, wf, wm, w_out = getw("w_o_swa", oa), getw("w_o_fox", oa), getw("w_o_mem", oa), getw("w_out", oa)
    x1, hm, merged = _merge_fwd(x, oa, of, om, proj, p["b_gate"], wa, wf, wm, w_out, p["g_mlp"])
    w_up = getw("w_mlp_up", of)
    u = _mlp_up(hm, w_up)
    w_down = getw("w_mlp_down", hm)
    dy, dy_b, loss = _mlp_down_loss(u, w_down, x1, target)

    da = _mlp_bwd_act(dy_b, w_down, u)
    t_down = emit({"w_mlp_down": _mm(u, dy_b, "tn", BF16, 1024, 1024, 2048, "dw_down")})
    dx1, dg_mlp = _mlp_bwd_x(da, w_up, x1, dy, _tie(p["g_mlp"], t_down))
    t_up = emit({"w_mlp_up": _mm(hm, da, "tn", BF16, 1024, 1024, 2048, "dw_up", column_chunks=True)})
    dproj, doa, dof_t, dom, dya, dyf, dym, db_gate = _merge_bwd(
        dx1, oa, of, om, proj, _tie(p["b_gate"], t_up), wa, wf, wm, w_out)
    t_o = emit({"w_out": _mm(merged, dx1, "tn", BF16, 1024, 1024, 2048, "dw_out"),
                "w_o_swa": _mm(oa, dya, "tn", BF16, 512, 1024, 2048, "dw_o_swa"),
                "w_o_fox": _mm(of, dyf, "tn", BF16, 512, 1024, 2048, "dw_o_fox"),
                "w_o_mem": _mm(om, dym, "tn", BF16, 512, 1024, 2048, "dw_o_mem")})

    dqm, dmk, dmv = _mem_bwd(qm, mk, mv, dom)
    dw_kv, dkn_mem, dg_mem = _memkv_bwd(dmk, dmv, kv_raw, _tie(p["kn_mem"], t_o), mem, p["g_mem"], mem_n, w_kv)
    t_kv = emit({"w_mem_kv": dw_kv})
    dqa, dkp, dvp, dbias, dsink = _swa_bwd(qa, kp, vp, bias_t, _tie(p["sink_swa"], t_kv).reshape(8), doa)
    dqf_t, dkf, dvf, dck4, dcq4 = _fox_bwd(qf_t, kf, vf, dof_t, of_t, cc4, lse4)

    dfl, db_forget = _fox_gate_bwd(dcq4, dck4, fl, bf128)

    dproj, dgn = _proj_pre_bwd(dproj, proj, dqf_t, dkf, dvf, dqm, dqa, dkp[SWA_BLOCK:], dvp[SWA_BLOCK:], dfl,
                               gqf, gkf, gqm, gqa, gka)
    t_in = emit({"w_in": _mm(h, dproj, "tn", BF16, 1024, 3072, 1024, "dw_in")})
    grad_x, dg_mix = _in_bwd_x(dproj, w_in, x, _tie(p["g_mix"], t_in), dx1)
    d_rel = _rel_bias_bwd(dbias, bucket_t)

    fold = lambda r: dgn[r:r + 1, 0:HEAD] + dgn[r:r + 1, HEAD:128]
    small = {
        "g_mix": dg_mix, "b_gate": db_gate, "b_forget": db_forget[:, 0:8],
        "qn_swa": fold(3), "kn_swa": fold(4), "sink_swa": dsink[:, 0].reshape(1, 8), "rel_bias": d_rel,
        "qn_fox": fold(0), "kn_fox": fold(1), "g_mem": dg_mem, "qn_mem": dgn[2:3, :], "kn_mem": dkn_mem,
        "g_mlp": dg_mlp,
    }
    return loss, grad_x, small


SMALL = ("g_mix", "b_gate", "b_forget", "qn_swa", "kn_swa", "sink_swa", "rel_bias", "qn_fox", "kn_fox", "g_mem",
         "qn_mem", "kn_mem", "g_mlp")
BIG = ("w_in", "w_mem_kv", "w_o_swa", "w_o_fox", "w_o_mem", "w_out", "w_mlp_up", "w_mlp_down")
COL_SHARDED = ("w_in", "w_o_swa", "w_o_fox", "w_o_mem", "w_mlp_up")
WEIGHTS = ("g_mix", "w_in", "b_gate", "b_forget", "qn_swa", "kn_swa", "sink_swa", "rel_bias", "qn_fox", "kn_fox", "g_mem",
           "w_mem_kv", "qn_mem", "kn_mem", "w_o_swa", "w_o_fox", "w_o_mem", "w_out", "g_mlp", "w_mlp_up", "w_mlp_down")
SMALL_SLOTS = (("g_mix", 1024), ("b_gate", 3072), ("b_forget", 128), ("qn_swa", 128), ("kn_swa", 128), ("sink_swa", 128),
               ("rel_bias", REL_BUCKETS * 128), ("qn_fox", 128), ("kn_fox", 128), ("g_mem", 1024), ("qn_mem", 128),
               ("kn_mem", 128), ("g_mlp", 1024), ("loss", 128))
SMALL_OFF = {n: sum(w for _, w in SMALL_SLOTS[:i]) for i, (n, _) in enumerate(SMALL_SLOTS)}
SMALL_ROW = sum(w for _, w in SMALL_SLOTS)


def _gathered_to_full(name, g):
    if name in COL_SHARDED:
        return jnp.transpose(g, (1, 0, 2)).reshape(g.shape[1], N_DEV * g.shape[2])
    return g.reshape(N_DEV * g.shape[1], g.shape[2])


def _full_to_parts(name, full, b):
    if name in COL_SHARDED:
        return jnp.transpose(full.reshape(full.shape[0], N_DEV, b), (1, 0, 2)).astype(BF16)
    return full.reshape(N_DEV, full.shape[0] // N_DEV, full.shape[1]).astype(BF16)


def _pack_small(grads, loss):
    pieces = []
    for n, width in SMALL_SLOTS:
        a = loss.reshape(1, 1) if n == "loss" else grads[n].reshape(1, -1)
        pieces.append(jnp.pad(a, ((0, 0), (0, width - a.shape[1]))))
    return jnp.concatenate(pieces, axis=1)


def _adamw_small(gathered, w, m, v):
    names = list(SMALL)

    def body(*refs):
        p_ref = refs[0]
        ins = refs[1:1 + 3 * len(names)]
        outs = refs[1 + 3 * len(names):]
        g_all = p_ref[0]
        for k in range(1, N_DEV):
            g_all = g_all + p_ref[k]
        for i, n in enumerate(names):
            w_ref, m_ref, v_ref = ins[3 * i:3 * i + 3]
            out = outs[4 * i:4 * i + 4]
            rows, cols = w_ref.shape
            for r in range(rows):
                off = SMALL_OFF[n] + 128 * r
                g = g_all[:, off:off + cols]
                rs = slice(r, r + 1)
                res = (g,) + _adam_math(w_ref[rs, :], g, m_ref[rs, :], v_ref[rs, :])
                for o_ref, val in zip(out, res):
                    o_ref[rs, :] = val
        outs[-1][...] = g_all[:, SMALL_OFF["loss"]:SMALL_OFF["loss"] + 128]

    args = [gathered]
    out_shape = []
    for n in names:
        args += [w[n], m[n], v[n]]
        out_shape += [jax.ShapeDtypeStruct(w[n].shape, F32)] * 4
    out_shape.append(jax.ShapeDtypeStruct((1, 128), F32))
    outs = pl.pallas_call(body, name="adamw_small", out_shape=out_shape)(*args)
    return {n: outs[4 * i:4 * i + 4] for i, n in enumerate(names)}, outs[-1]


def kernel(x, mem, g_mix, w_in, b_gate, b_forget, qn_swa, kn_swa, sink_swa, rel_bias, qn_fox, kn_fox, g_mem, w_mem_kv, qn_mem, kn_mem, w_o_swa, w_o_fox, w_o_mem, w_out, g_mlp, w_mlp_up, w_mlp_down, loss_target, m_g_mix, m_w_in, m_b_gate, m_b_forget, m_qn_swa, m_kn_swa, m_sink_swa, m_rel_bias, m_qn_fox, m_kn_fox, m_g_mem, m_w_mem_kv, m_qn_mem, m_kn_mem, m_w_o_swa, m_w_o_fox, m_w_o_mem, m_w_out, m_g_mlp, m_w_mlp_up, m_w_mlp_down, v_g_mix, v_w_in, v_b_gate, v_b_forget, v_qn_swa, v_kn_swa, v_sink_swa, v_rel_bias, v_qn_fox, v_kn_fox, v_g_mem, v_w_mem_kv, v_qn_mem, v_kn_mem, v_w_o_swa, v_w_o_fox, v_w_o_mem, v_w_out, v_g_mlp, v_w_mlp_up, v_w_mlp_down):
    wts = dict(g_mix=g_mix, w_in=w_in, b_gate=b_gate, b_forget=b_forget, qn_swa=qn_swa, kn_swa=kn_swa, sink_swa=sink_swa,
               rel_bias=rel_bias, qn_fox=qn_fox, kn_fox=kn_fox, g_mem=g_mem, w_mem_kv=w_mem_kv, qn_mem=qn_mem, kn_mem=kn_mem,
               w_o_swa=w_o_swa, w_o_fox=w_o_fox, w_o_mem=w_o_mem, w_out=w_out, g_mlp=g_mlp, w_mlp_up=w_mlp_up,
               w_mlp_down=w_mlp_down)
    mom = dict(g_mix=m_g_mix, w_in=m_w_in, b_gate=m_b_gate, b_forget=m_b_forget, qn_swa=m_qn_swa, kn_swa=m_kn_swa,
               sink_swa=m_sink_swa, rel_bias=m_rel_bias, qn_fox=m_qn_fox, kn_fox=m_kn_fox, g_mem=m_g_mem, w_mem_kv=m_w_mem_kv,
               qn_mem=m_qn_mem, kn_mem=m_kn_mem, w_o_swa=m_w_o_swa, w_o_fox=m_w_o_fox, w_o_mem=m_w_o_mem, w_out=m_w_out,
               g_mlp=m_g_mlp, w_mlp_up=m_w_mlp_up, w_mlp_down=m_w_mlp_down)
    var = dict(g_mix=v_g_mix, w_in=v_w_in, b_gate=v_b_gate, b_forget=v_b_forget, qn_swa=v_qn_swa, kn_swa=v_kn_swa,
               sink_swa=v_sink_swa, rel_bias=v_rel_bias, qn_fox=v_qn_fox, kn_fox=v_kn_fox, g_mem=v_g_mem, w_mem_kv=v_w_mem_kv,
               qn_mem=v_qn_mem, kn_mem=v_kn_mem, w_o_swa=v_w_o_swa, w_o_fox=v_w_o_fox, w_o_mem=v_w_o_mem, w_out=v_w_out,
               g_mlp=v_g_mlp, w_mlp_up=v_w_mlp_up, w_mlp_down=v_w_mlp_down)

    shards = _cast_shards({n: wts[n][0] for n in BIG})
    first = _split_start([shards["w_in"]], True, "ag_start_w_in", peers=(1,) + SAME_CORE)
    rest = _split_start([shards[n] for n in BIG[1:]], True, "ag_start_rest", after=first[4])
    full = {}

    def getw(n, after):
        if n == "w_in" and n not in full:
            forwarded = _forward_start(first, after, "ag_forward_w_in")
            full[n] = _w_in_from_shards(_forward_wait(first, forwarded, "ag_wait_w_in"))
        elif n not in full:
            land = _split_wait(rest, BIG[1:].index(n), after, True, "ag_wait_" + n)
            full[n] = land if n == "w_mlp_up" else _gathered_to_full(n, land)
        return full[n]

    exchanges = {}

    def emit(grads_by_name):
        parts = []
        for n, grad in grads_by_name.items():
            if n == "w_in":
                parts.append(_dw_in_to_parts(grad))
            else:
                parts.append(grad if n == "w_mlp_up" else _full_to_parts(n, grad, wts[n].shape[2]))
        started = _split_start(parts, False, "rs_start_" + next(iter(grads_by_name)))
        for w, n in enumerate(grads_by_name):
            exchanges[n] = (started, w)
        return started[4]

    small_p = {n: wts[n] for n in SMALL}
    loss, grad_x, small_g = _local_step(x[0], mem[0], loss_target[0], small_p, getw, emit, (first[4], rest[4]))

    packed = _pack_small(small_g, loss)
    small_gather = _split_start([packed], True, "ag_start_small")

    grads, delta, new_m, new_v = {}, {}, {}, {}

    def update(n, after):
        land = _split_wait(*exchanges[n], after, False, "rs_wait_" + n)
        g, d, m2, v2 = _adamw(land, wts[n][0], mom[n][0], var[n][0], "adamw_" + n)
        grads[n], delta[n], new_m[n], new_v[n] = g[None], d[None], m2[None], v2[None]
        return d

    after = small_gather[4]
    for n in exchanges:
        if n != "w_in":
            after = update(n, after)

    gathered = _split_wait(small_gather, 0, after, True, "ag_wait_small")
    small_out, total = _adamw_small(gathered, small_p, mom, var)
    for name, (g, d, m2, v2) in small_out.items():
        grads[name], delta[name], new_m[name], new_v[name] = g, d, m2, v2
    update("w_in", total)

    return (total[0, 0], grad_x[None], *[grads[n] for n in WEIGHTS], *[delta[n] for n in WEIGHTS],
            *[new_m[n] for n in WEIGHTS], *[new_v[n] for n in WEIGHTS])
```

```python
import math

import jax
import jax.numpy as jnp
from jax import lax
from jax.experimental import pallas as pl
from jax.experimental.pallas import tpu as pltpu

F32 = jnp.float32
BF16 = jnp.bfloat16

D_MODEL = 1024
N_MEM = 256
D_FF = 4096
HEAD = 64
SWA_HEADS = 8
SWA_BLOCK = 128
MEM_HEADS = 4
MEM_HEAD = 128
EPS = 1e-6
NEG = -1e30
REL_BUCKETS = 32
REL_MAX_DIST = 128

ADAM_LR = 0.001
ADAM_B1 = 0.9
ADAM_B2 = 0.999
ADAM_EPS = 1e-08
ADAM_WD = 0.01
ADAM_STEP = 10

GL0, QF0, KF0, VF0, QM0, QA0, KA0, VA0, FL0 = 0, 3072, 3584, 4096, 4608, 5120, 5632, 5760, 5888
PROJ_W = 6144
HALF_W = 3072
H_QF, H_KF, H_VF, H_QM, H_QA, H_KA, H_VA, H_FL = 0, 512, 1024, 1536, 2048, 2560, 2688, 2816

VMEM_LIMIT = 56 * 1024 * 1024
N_DEV = 8
MESH = pl.DeviceIdType.MESH

NN = (((1,), (0,)), ((), ()))
NT = (((1,), (1,)), ((), ()))
TN = (((0,), (0,)), ((), ()))


def _dot(a, b, dims=NN):
    return lax.dot_general(a, b, dims, preferred_element_type=F32)


def _params(sem):
    return pltpu.CompilerParams(dimension_semantics=sem, vmem_limit_bytes=VMEM_LIMIT)


def _full(shape):
    nd = len(shape)
    return pl.BlockSpec(shape, lambda *_: (0,) * nd)


def _sigmoid(z):
    return 1.0 / (1.0 + jnp.exp(-z))


def _group_mean(v, hd):
    if hd == 128:
        return jnp.mean(v, axis=-1, keepdims=True)
    lane = lax.broadcasted_iota(jnp.int32, v.shape, 1)
    lo = lane < HEAD
    s_lo = jnp.sum(jnp.where(lo, v, 0.0), axis=-1, keepdims=True)
    s_hi = jnp.sum(jnp.where(lo, 0.0, v), axis=-1, keepdims=True)
    return jnp.where(lo, s_lo, s_hi) * (1.0 / HEAD)


def _mm(a, b, mode, out_dtype, tm, tn, tk, name, column_chunks=False):
    if mode == "nn":
        m, k = a.shape
        n = b.shape[1]
    elif mode == "nt":
        m, k = a.shape
        n = b.shape[0]
    else:
        k, m = a.shape
        n = b.shape[1]
    tm, tn, tk = min(tm, m), min(tn, n), min(tk, k)
    nk = k // tk
    chunk = n // N_DEV
    per_tile = tn // chunk if column_chunks else 1
    dims = {"nn": NN, "nt": NT, "tn": TN}[mode]
    a_spec = pl.BlockSpec((tk, tm), lambda j, i, kk: (kk, i)) if mode == "tn" else pl.BlockSpec((tm, tk), lambda j, i, kk: (i, kk))
    b_spec = pl.BlockSpec((tn, tk), lambda j, i, kk: (j, kk)) if mode == "nt" else pl.BlockSpec((tk, tn), lambda j, i, kk: (kk, j))

    def body(a_ref, b_ref, o_ref, *acc):
        prod = _dot(a_ref[...].astype(BF16), b_ref[...].astype(BF16), dims)

        def write(res):
            if column_chunks:
                for c in range(per_tile):
                    o_ref[c] = res[:, c * chunk:(c + 1) * chunk].astype(o_ref.dtype)
            else:
                o_ref[...] = res.astype(o_ref.dtype)

        if nk == 1:
            write(prod)
        else:
            acc_ref, = acc
            kk = pl.program_id(2)

            @pl.when(kk == 0)
            def _():
                acc_ref[...] = prod

            @pl.when(kk > 0)
            def _():
                acc_ref[...] += prod

            @pl.when(kk == nk - 1)
            def _():
                write(acc_ref[...])

    return pl.pallas_call(
        body, name=name, grid=(n // tn, m // tm, nk),
        in_specs=[a_spec, b_spec],
        out_specs=(pl.BlockSpec((per_tile, tm, chunk), lambda j, i, kk: (j, i, 0)) if column_chunks
                   else pl.BlockSpec((tm, tn), lambda j, i, kk: (i, j))),
        out_shape=jax.ShapeDtypeStruct((N_DEV, m, chunk) if column_chunks else (m, n), out_dtype),
        scratch_shapes=[pltpu.VMEM((tm, tn), F32)] if nk > 1 else [],
        compiler_params=_params(("parallel", "parallel", "arbitrary")),
    )(a, b)


def _rms_fwd(x, g, name, deps=()):
    s, d = x.shape
    tm = min(512, s)

    def body(x_ref, g_ref, *rest):
        h_ref = rest[len(deps)]
        xv = x_ref[...]
        r = lax.rsqrt(jnp.mean(xv * xv, axis=-1, keepdims=True) + EPS)
        h_ref[...] = (xv * r * g_ref[...]).astype(BF16)

    return pl.pallas_call(
        body, name=name, grid=(s // tm,),
        in_specs=[pl.BlockSpec((tm, d), lambda i: (i, 0)), _full((1, d))] + [pl.BlockSpec(memory_space=pl.ANY)] * len(deps),
        out_specs=pl.BlockSpec((tm, d), lambda i: (i, 0)),
        out_shape=jax.ShapeDtypeStruct((s, d), BF16),
        compiler_params=_params(("parallel",)),
    )(x, g, *deps)


def _proj_post(proj, gq_fox, gk_fox, gq_mem, gq_swa, gk_swa):
    s = proj.shape[0]
    tm = min(256, s)

    def body(p_ref, gqf, gkf, gqm, gqa, gka, qf_ref, kf_ref, vf_ref, qm_ref, qa_ref, ka_ref, va_ref, qft_ref, vft_ref):
        def norm(off, width, hd, g_ref, o_ref, scaled_t_ref=None):
            for b in range(width // 128):
                v = p_ref[:, off + b * 128: off + (b + 1) * 128].astype(F32)
                r = lax.rsqrt(_group_mean(v * v, hd) + EPS)
                vn = (v * r * g_ref[...]).astype(BF16)
                o_ref[:, b * 128:(b + 1) * 128] = vn
                if scaled_t_ref is not None:
                    scaled_t_ref[b * 128:(b + 1) * 128, :] = (vn.astype(F32) * 0.125).T.astype(BF16)

        norm(H_QF, 512, HEAD, gqf, qf_ref, qft_ref)
        norm(H_KF, 512, HEAD, gkf, kf_ref)
        vf_ref[...] = p_ref[:, H_VF:H_VF + 512].astype(BF16)
        for b in range(4):
            vft_ref[b * 128:(b + 1) * 128, :] = p_ref[:, H_VF + b * 128:H_VF + (b + 1) * 128].astype(F32).T.astype(BF16)
        norm(H_QM, 512, MEM_HEAD, gqm, qm_ref)
        norm(H_QA, 512, HEAD, gqa, qa_ref)
        norm(H_KA, 128, HEAD, gka, ka_ref)
        va_ref[...] = p_ref[:, H_VA:H_VA + 128].astype(BF16)

    g_spec = _full((1, 128))
    o512 = pl.BlockSpec((tm, 512), lambda i: (i, 0))
    o128 = pl.BlockSpec((tm, 128), lambda i: (i, 0))
    s512 = jax.ShapeDtypeStruct((s, 512), BF16)
    s128 = jax.ShapeDtypeStruct((s, 128), BF16)
    return pl.pallas_call(
        body, name="proj_post", grid=(s // tm,),
        in_specs=[pl.BlockSpec((tm, HALF_W), lambda i: (i, 1)), g_spec, g_spec, g_spec, g_spec, g_spec],
        out_specs=[o512, o512, o512, o512, o512, o128, o128] + [pl.BlockSpec((512, tm), lambda i: (0, i))] * 2,
        out_shape=[s512, s512, s512, s512, s512, s128, s128] + [jax.ShapeDtypeStruct((512, s), BF16)] * 2,
        compiler_params=_params(("parallel",)),
    )(proj, gq_fox, gk_fox, gq_mem, gq_swa, gk_swa)


def _tri(n, lower):
    r = lax.broadcasted_iota(jnp.int32, (n, n), 0)
    c = lax.broadcasted_iota(jnp.int32, (n, n), 1)
    return jnp.where((c <= r) if lower else (c >= r), 1.0, 0.0).astype(F32)


def _fox_gate_fwd(proj, b_forget128):
    s = proj.shape[0]
    tm = min(512, s)

    def body(p_ref, b_ref, cc_ref, ca_ref, carry_ref):
        i = pl.program_id(0)

        @pl.when(i == 0)
        def _():
            carry_ref[...] = jnp.zeros_like(carry_ref)

        z = p_ref[...] + b_ref[...]
        logf = jnp.minimum(z, 0.0) - jnp.log(1.0 + jnp.exp(-jnp.abs(z)))
        c = jnp.dot(_tri(tm, True), logf, precision=lax.Precision.HIGHEST, preferred_element_type=F32) + carry_ref[...]
        carry_ref[...] = c[tm - 1:tm, :]
        lane = lax.broadcasted_iota(jnp.int32, (tm, 128), 1)
        for hp in range(4):
            cc_ref[hp] = c if hp == 0 else pltpu.roll(c, 128 - 2 * hp, 1)
            aug = jnp.zeros((tm, 128), F32)
            for e in range(2):
                rest = jnp.broadcast_to(c[:, 2 * hp + e:2 * hp + e + 1], (tm, 128))
                for part in range(3):
                    piece = rest.astype(BF16).astype(F32)
                    aug = jnp.where(lane == HEAD * (1 - e) + part, piece, aug)
                    rest = rest - piece
            ca_ref[hp] = aug.astype(BF16)

    return pl.pallas_call(
        body, name="fox_gate_fwd", grid=(s // tm,),
        in_specs=[pl.BlockSpec((tm, 128), lambda i: (i, 0)), _full((1, 128))],
        out_specs=[pl.BlockSpec((4, tm, 128), lambda i: (0, i, 0))] * 2,
        out_shape=[jax.ShapeDtypeStruct((4, s, 128), F32), jax.ShapeDtypeStruct((4, s, 128), BF16)],
        scratch_shapes=[pltpu.VMEM((1, 128), F32)],
        compiler_params=_params(("arbitrary",)),
    )(proj, b_forget128)


def _memkv_fwd(mem, g_mem, w_kv, kn_mem):
    m = mem.shape[0]

    def body(mem_ref, g_ref, w_ref, kn_ref, memn_ref, kv_ref, mk_ref, mv_ref):
        xv = mem_ref[...]
        r = lax.rsqrt(jnp.mean(xv * xv, axis=-1, keepdims=True) + EPS)
        mn = (xv * r * g_ref[...]).astype(BF16)
        memn_ref[...] = mn
        kv = _dot(mn, w_ref[...])
        kv_ref[...] = kv
        for h in range(MEM_HEADS):
            v = kv[:, h * 128:(h + 1) * 128]
            rr = lax.rsqrt(jnp.mean(v * v, axis=-1, keepdims=True) + EPS)
            mk_ref[:, h * 128:(h + 1) * 128] = (v * rr * kn_ref[...]).astype(BF16)
        mv_ref[...] = kv[:, 512:1024].astype(BF16)

    return pl.pallas_call(
        body, name="memkv_fwd",
        out_shape=[jax.ShapeDtypeStruct((m, D_MODEL), BF16), jax.ShapeDtypeStruct((m, 1024), F32),
                   jax.ShapeDtypeStruct((m, 512), BF16), jax.ShapeDtypeStruct((m, 512), BF16)],
        compiler_params=pltpu.CompilerParams(vmem_limit_bytes=VMEM_LIMIT),
    )(mem, g_mem, w_kv, kn_mem)


def _bias_table(rel_bias, bucket):
    def body(rb_ref, bk_ref, o_ref):
        bk = bk_ref[...]
        for h in range(SWA_HEADS):
            acc = jnp.zeros(bk.shape, F32)
            for b in range(REL_BUCKETS):
                acc = jnp.where(bk == b, rb_ref[b, h], acc)
            o_ref[h] = acc

    return pl.pallas_call(
        body, name="bias_table",
        in_specs=[pl.BlockSpec(memory_space=pltpu.SMEM), pl.BlockSpec(memory_space=pltpu.VMEM)],
        out_shape=jax.ShapeDtypeStruct((SWA_HEADS,) + bucket.shape, F32),
    )(rel_bias, bucket)


def _swa_valid(n):
    row = lax.broadcasted_iota(jnp.int32, (SWA_BLOCK, 2 * SWA_BLOCK), 0)
    col = lax.broadcasted_iota(jnp.int32, (SWA_BLOCK, 2 * SWA_BLOCK), 1)
    dist = row + SWA_BLOCK - col
    return (dist >= 0) & (dist < SWA_BLOCK) & ((col >= SWA_BLOCK) | (n > 0))


def _swa_fwd(qa, kp, vp, bias, sink):
    s = qa.shape[0]
    nb = s // SWA_BLOCK

    def body(sink_ref, q_ref, kp_ref, vp_ref, bias_ref, o_ref):
        n = pl.program_id(0)
        start = pl.multiple_of(n * SWA_BLOCK, SWA_BLOCK)
        k2 = kp_ref[pl.ds(start, 2 * SWA_BLOCK), :]
        v2 = vp_ref[pl.ds(start, 2 * SWA_BLOCK), :]
        valid = _swa_valid(n)
        heads = range(SWA_HEADS)
        hs = lambda h: slice(h * HEAD, (h + 1) * HEAD)
        sc = [jnp.where(valid, _dot(q_ref[:, hs(h)], k2[:, hs(h // 4)], NT) * 0.125 + bias_ref[h], NEG) for h in heads]
        pn = []
        for h in heads:
            sk = sink_ref[h]
            mx = jnp.maximum(jnp.max(sc[h], axis=-1, keepdims=True), sk)
            p = jnp.exp(sc[h] - mx)
            den = jnp.sum(p, axis=-1, keepdims=True) + jnp.exp(sk - mx)
            pn.append((p / den).astype(BF16))
        outs = [_dot(pn[h], v2[:, hs(h // 4)]).astype(BF16) for h in heads]
        for h in heads:
            o_ref[:, hs(h)] = outs[h]

    return pl.pallas_call(
        body, name="swa_fwd", grid=(nb,),
        in_specs=[pl.BlockSpec(memory_space=pltpu.SMEM),
                  pl.BlockSpec((SWA_BLOCK, 512), lambda n: (n, 0)),
                  _full(kp.shape), _full(vp.shape), _full(bias.shape)],
        out_specs=pl.BlockSpec((SWA_BLOCK, 512), lambda n: (n, 0)),
        out_shape=jax.ShapeDtypeStruct((s, 512), BF16),
        compiler_params=_params(("parallel",)),
    )(sink, qa, kp, vp, bias)


def _head_mask(e):
    lane = lax.broadcasted_iota(jnp.int32, (1, 128), 1)
    return (lane >= e * HEAD) & (lane < (e + 1) * HEAD)


FOX_FWD_T = 1024
FOX_BWD_T = 512


def _head_rows(e):
    row = lax.broadcasted_iota(jnp.int32, (128, 1), 0)
    return (row >= e * HEAD) & (row < (e + 1) * HEAD)


def _fox_fwd(q, k, v_t, ca4):
    s = q.shape[0]
    t = min(FOX_FWD_T, s)
    nq = s // t

    def body(q_ref, k_ref, vt_ref, ca_ref, o_ref, lse_ref, ot_ref):
        i = pl.program_id(1)
        qs = q_ref[...] * jnp.asarray(0.125, BF16)
        lane = lax.broadcasted_iota(jnp.int32, (1, 128), 1)
        minus = [jnp.where((lane >= HEAD * (1 - e)) & (lane < HEAD * (1 - e) + 3), -1.0, 0.0).astype(BF16) for e in range(2)]
        qe = [jnp.where(_head_mask(e), qs, jnp.broadcast_to(minus[e], qs.shape)) for e in range(2)]

        def block(carry, key0, nkeys, q0, nqs, masked):
            ks = pl.ds(pl.multiple_of(key0, 128), nkeys)
            kj = k_ref[ks, :]
            caj = ca_ref[0, ks, :]
            vtj = vt_ref[:, ks]
            out = []
            for e in range(2):
                m_all, acc_all = carry[2 * e], carry[2 * e + 1]
                m, acc = m_all[:, q0:q0 + nqs], acc_all[:, q0:q0 + nqs]
                st = _dot(jnp.where(_head_mask(e), kj, caj), qe[e][q0:q0 + nqs, :], NT)
                if masked:
                    krow = lax.broadcasted_iota(jnp.int32, (nkeys, nqs), 0) + key0
                    qcol = lax.broadcasted_iota(jnp.int32, (nkeys, nqs), 1) + (i * t + q0)
                    st = jnp.where(krow <= qcol, st, NEG)
                m_new = jnp.maximum(m, jnp.max(st, axis=0, keepdims=True))
                alpha = jnp.exp(m - m_new)
                pt = jnp.exp(st - m_new).astype(BF16)
                vte = jnp.where(_head_rows(e), vtj, jnp.ones_like(vtj))
                acc_new = alpha * acc + _dot(vte, pt)
                if nqs < t:
                    m_new = jnp.concatenate([m_all[:, :q0], m_new], axis=1)
                    acc_new = jnp.concatenate([acc_all[:, :q0], acc_new], axis=1)
                out += [m_new, acc_new]
            return tuple(out)

        half = t // 2
        init = (jnp.full((1, t), NEG, F32), jnp.zeros((128, t), F32)) * 2
        carry = lax.fori_loop(0, i, lambda j, c: block(c, j * t, t, 0, t, False), init)
        carry = block(carry, i * t, half, 0, t, True)
        m0, a0, m1, a1 = block(carry, i * t + half, half, half, half, True)
        l0 = a0[HEAD:HEAD + 1, :]
        l1 = a1[0:1, :]
        o_t = jnp.where(_head_rows(0), a0 / l0, a1 / l1)
        o_ref[...] = o_t.T.astype(BF16)
        ot_ref[...] = o_t.astype(BF16)
        r8 = lax.broadcasted_iota(jnp.int32, (8, t), 0)
        lse_ref[0] = jnp.where(r8 == 0, m0 + jnp.log(l0), jnp.where(r8 == 1, m1 + jnp.log(l1), 0.0))

    return pl.pallas_call(
        body, name="fox_fwd", grid=(4, nq),
        in_specs=[pl.BlockSpec((t, 128), lambda hp, i: (i, hp)),
                  pl.BlockSpec((s, 128), lambda hp, i: (0, hp)),
                  pl.BlockSpec((128, s), lambda hp, i: (hp, 0)),
                  pl.BlockSpec((1, s, 128), lambda hp, i: (hp, 0, 0))],
        out_specs=[pl.BlockSpec((t, 128), lambda hp, i: (i, hp)),
                   pl.BlockSpec((1, 8, t), lambda hp, i: (hp, 0, i)),
                   pl.BlockSpec((128, t), lambda hp, i: (hp, i))],
        out_shape=[jax.ShapeDtypeStruct((s, 512), BF16), jax.ShapeDtypeStruct((4, 8, s), F32),
                   jax.ShapeDtypeStruct((512, s), BF16)],
        compiler_params=_params(("parallel", "parallel")),
    )(q, k, v_t, ca4)


MEM_SCALE = MEM_HEAD ** -0.5


def _mem_fwd(qm, mk, mv):
    s = qm.shape[0]
    tq = min(512, s)

    def body(q_ref, mk_ref, mv_ref, o_ref):
        for h in range(MEM_HEADS):
            hs = slice(h * 128, (h + 1) * 128)
            sc = _dot(q_ref[:, hs], mk_ref[:, hs], NT) * MEM_SCALE
            mx = jnp.max(sc, axis=-1, keepdims=True)
            p = jnp.exp(sc - mx)
            p = p / jnp.sum(p, axis=-1, keepdims=True)
            o_ref[:, hs] = _dot(p.astype(BF16), mv_ref[:, hs]).astype(BF16)

    return pl.pallas_call(
        body, name="mem_fwd", grid=(s // tq,),
        in_specs=[pl.BlockSpec((tq, 512), lambda i: (i, 0)), _full(mk.shape), _full(mv.shape)],
        out_specs=pl.BlockSpec((tq, 512), lambda i: (i, 0)),
        out_shape=jax.ShapeDtypeStruct((s, 512), BF16),
        compiler_params=_params(("parallel",)),
    )(qm, mk, mv)


def _merge_fwd(x, oa, of, om, proj, b_gate, wa, wf, wm, w_out, g_mlp):
    s = x.shape[0]
    tm = min(256, s)

    def body(x_ref, oa_ref, of_ref, om_ref, gl_ref, bg_ref, wa_ref, wf_ref, wm_ref, wo_ref, g_ref, x1_ref, hm_ref, mg_ref):
        merged = None
        for b, (o_ref, w_ref) in enumerate(((oa_ref, wa_ref), (of_ref, wf_ref), (om_ref, wm_ref))):
            cs = slice(b * D_MODEL, (b + 1) * D_MODEL)
            y = _dot(o_ref[...], w_ref[...])
            t = _sigmoid(gl_ref[:, cs].astype(F32) + bg_ref[:, cs]) * y
            merged = t if merged is None else merged + t
        mb = merged.astype(BF16)
        mg_ref[...] = mb
        x1 = x_ref[...] + _dot(mb, wo_ref[...])
        x1_ref[...] = x1
        r = lax.rsqrt(jnp.mean(x1 * x1, axis=-1, keepdims=True) + EPS)
        hm_ref[...] = (x1 * r * g_ref[...]).astype(BF16)

    row = lambda w: pl.BlockSpec((tm, w), lambda i: (i, 0))
    return pl.pallas_call(
        body, name="merge_fwd", grid=(s // tm,),
        in_specs=[row(D_MODEL), row(512), row(512), row(512), row(HALF_W), _full((1, HALF_W)),
                  _full(wa.shape), _full(wf.shape), _full(wm.shape), _full(w_out.shape), _full((1, D_MODEL))],
        out_specs=[row(D_MODEL), row(D_MODEL), row(D_MODEL)],
        out_shape=[jax.ShapeDtypeStruct((s, D_MODEL), F32), jax.ShapeDtypeStruct((s, D_MODEL), BF16),
                   jax.ShapeDtypeStruct((s, D_MODEL), BF16)],
        compiler_params=_params(("parallel",)),
    )(x, oa, of, om, proj, b_gate, wa, wf, wm, w_out, g_mlp)


def _mlp_up(hm, w_up):
    s = hm.shape[0]
    tm, tn = min(1024, s), w_up.shape[2]

    def body(h_ref, w_ref, u_ref):
        r = jnp.maximum(_dot(h_ref[...], w_ref[0]), 0.0)
        u_ref[...] = (r * r).astype(BF16)

    return pl.pallas_call(
        body, name="mlp_up", grid=(s // tm, D_FF // tn),
        in_specs=[pl.BlockSpec((tm, D_MODEL), lambda i, j: (i, 0)), pl.BlockSpec((1, D_MODEL, tn), lambda i, j: (j, 0, 0))],
        out_specs=pl.BlockSpec((tm, tn), lambda i, j: (i, j)),
        out_shape=jax.ShapeDtypeStruct((s, D_FF), BF16),
        compiler_params=_params(("parallel", "parallel")),
    )(hm, w_up)


def _mlp_down_loss(u, w_down, x1, target):
    s = u.shape[0]
    tm = min(256, s)

    def body(u_ref, w_ref, x1_ref, t_ref, dy_ref, dyb_ref, loss_ref):
        i = pl.program_id(0)

        @pl.when(i == 0)
        def _():
            loss_ref[...] = jnp.zeros_like(loss_ref)

        y = x1_ref[...] + _dot(u_ref[...], w_ref[...])
        err = y - t_ref[...]
        dy = err * (1.0 / D_MODEL)
        dy_ref[...] = dy
        dyb_ref[...] = dy.astype(BF16)
        part = jnp.sum(jnp.sum(err * err, axis=-1, keepdims=True) * (1.0 / D_MODEL), axis=0, keepdims=True)
        loss_ref[...] += 0.5 * part

    row = pl.BlockSpec((tm, D_MODEL), lambda i: (i, 0))
    return pl.pallas_call(
        body, name="mlp_down_loss", grid=(s // tm,),
        in_specs=[pl.BlockSpec((tm, D_FF), lambda i: (i, 0)), _full(w_down.shape), row, row],
        out_specs=[row, row, _full((1, 1))],
        out_shape=[jax.ShapeDtypeStruct((s, D_MODEL), F32), jax.ShapeDtypeStruct((s, D_MODEL), BF16),
                   jax.ShapeDtypeStruct((1, 1), F32)],
        compiler_params=_params(("arbitrary",)),
    )(u, w_down, x1, target)


def _mlp_bwd_act(dy, w_down, u):
    s = dy.shape[0]
    tm, tn = min(1024, s), 1024

    def body(dy_ref, w_ref, u_ref, da_ref):
        du = _dot(dy_ref[...], w_ref[...], NT)
        da_ref[...] = (du * (2.0 * jnp.sqrt(u_ref[...].astype(F32)))).astype(BF16)

    return pl.pallas_call(
        body, name="mlp_bwd_act", grid=(D_FF // tn, s // tm),
        in_specs=[pl.BlockSpec((tm, D_MODEL), lambda j, i: (i, 0)), pl.BlockSpec((tn, D_MODEL), lambda j, i: (j, 0)),
                  pl.BlockSpec((tm, tn), lambda j, i: (i, j))],
        out_specs=pl.BlockSpec((tm, tn), lambda j, i: (i, j)),
        out_shape=jax.ShapeDtypeStruct((s, D_FF), BF16),
        compiler_params=_params(("parallel", "parallel")),
    )(dy, w_down, u)


def _rms_bwd(xv, g, dh, skip):
    r = lax.rsqrt(jnp.mean(xv * xv, axis=-1, keepdims=True) + EPS)
    n = xv * r
    dn = dh * g
    dx = skip + r * (dn - n * jnp.mean(dn * n, axis=-1, keepdims=True))
    return dx, jnp.sum(dh * n, axis=0, keepdims=True)


def _mlp_bwd_x(da, w_up, x1, dy, g_mlp):
    s = da.shape[0]
    tm = min(256, s)

    def body(da_ref, w_ref, x1_ref, dy_ref, g_ref, dx1_ref, dg_ref):
        i = pl.program_id(0)

        @pl.when(i == 0)
        def _():
            dg_ref[...] = jnp.zeros_like(dg_ref)

        tn = w_ref.shape[2]
        dhm = _dot(da_ref[:, 0:tn], w_ref[0], NT)
        for j in range(1, N_DEV):
            dhm = dhm + _dot(da_ref[:, j * tn:(j + 1) * tn], w_ref[j], NT)
        dx, dg = _rms_bwd(x1_ref[...], g_ref[...], dhm, dy_ref[...])
        dx1_ref[...] = dx
        dg_ref[...] += dg

    row = pl.BlockSpec((tm, D_MODEL), lambda i: (i, 0))
    return pl.pallas_call(
        body, name="mlp_bwd_x", grid=(s // tm,),
        in_specs=[pl.BlockSpec((tm, D_FF), lambda i: (i, 0)), _full(w_up.shape), row, row, _full((1, D_MODEL))],
        out_specs=[row, _full((1, D_MODEL))],
        out_shape=[jax.ShapeDtypeStruct((s, D_MODEL), F32), jax.ShapeDtypeStruct((1, D_MODEL), F32)],
        compiler_params=_params(("arbitrary",)),
    )(da, w_up, x1, dy, g_mlp)


def _merge_bwd(dx1, oa, of, om, proj, b_gate, wa, wf, wm, w_out):
    s = dx1.shape[0]
    tm = min(256, s)

    def body(dx1_ref, oa_ref, of_ref, om_ref, gl_ref, bg_ref, wa_ref, wf_ref, wm_ref, wo_ref,
             dp_ref, doa_ref, dof_ref, dom_ref, dya_ref, dyf_ref, dym_ref, dbg_ref):
        i = pl.program_id(0)

        @pl.when(i == 0)
        def _():
            dbg_ref[...] = jnp.zeros_like(dbg_ref)

        dmerged = _dot(dx1_ref[...].astype(BF16), wo_ref[...], NT)
        branches = ((oa_ref, wa_ref, doa_ref, dya_ref), (of_ref, wf_ref, dof_ref, dyf_ref), (om_ref, wm_ref, dom_ref, dym_ref))
        for b, (o_ref, w_ref, do_ref, dyb_ref) in enumerate(branches):
            cs = slice(b * D_MODEL, (b + 1) * D_MODEL)
            y = _dot(o_ref[...], w_ref[...])
            g = _sigmoid(gl_ref[:, cs].astype(F32) + bg_ref[:, cs])
            dz = (dmerged * y) * g * (1.0 - g)
            dp_ref[:, cs] = dz.astype(BF16)
            dbg_ref[:, cs] += jnp.sum(dz, axis=0, keepdims=True)
            dyb = (dmerged * g).astype(BF16)
            dyb_ref[...] = dyb
            do = _dot(dyb, w_ref[...], NT)
            do_ref[...] = (do.T if b == 1 else do).astype(BF16)

    row = lambda w: pl.BlockSpec((tm, w), lambda i: (i, 0))
    sd = lambda w: jax.ShapeDtypeStruct((s, w), BF16)
    return pl.pallas_call(
        body, name="merge_bwd", grid=(s // tm,),
        in_specs=[row(D_MODEL), row(512), row(512), row(512), row(HALF_W), _full((1, HALF_W)),
                  _full(wa.shape), _full(wf.shape), _full(wm.shape), _full(w_out.shape)],
        out_specs=[row(HALF_W), row(512), pl.BlockSpec((512, tm), lambda i: (0, i)), row(512),
                   row(D_MODEL), row(D_MODEL), row(D_MODEL), _full((1, HALF_W))],
        out_shape=[sd(PROJ_W), sd(512), jax.ShapeDtypeStruct((512, s), BF16), sd(512), sd(D_MODEL), sd(D_MODEL), sd(D_MODEL),
                   jax.ShapeDtypeStruct((1, HALF_W), F32)],
        compiler_params=_params(("arbitrary",)),
    )(dx1, oa, of, om, proj, b_gate, wa, wf, wm, w_out)


def _swa_valid_t(n):
    key = lax.broadcasted_iota(jnp.int32, (2 * SWA_BLOCK, SWA_BLOCK), 0)
    qry = lax.broadcasted_iota(jnp.int32, (2 * SWA_BLOCK, SWA_BLOCK), 1)
    dist = qry + SWA_BLOCK - key
    return (dist >= 0) & (dist < SWA_BLOCK) & ((key >= SWA_BLOCK) | (n > 0))


def _swa_bwd(qa, kp, vp, bias_t, sink, doa):
    s = qa.shape[0]
    nb = s // SWA_BLOCK

    def body(sink_ref, q_ref, kp_ref, vp_ref, bias_ref, do_ref, dq_ref, dkp_ref, dvp_ref, dbias_ref, dsink_ref, sk_acc):
        n = pl.program_id(0)

        @pl.when(n == 0)
        def _():
            dkp_ref[...] = jnp.zeros_like(dkp_ref)
            dvp_ref[...] = jnp.zeros_like(dvp_ref)
            dbias_ref[...] = jnp.zeros_like(dbias_ref)
            sk_acc[...] = jnp.zeros_like(sk_acc)

        start = pl.multiple_of(n * SWA_BLOCK, SWA_BLOCK)
        win = pl.ds(start, 2 * SWA_BLOCK)
        k2 = kp_ref[win, :]
        v2 = vp_ref[win, :]
        valid = _swa_valid_t(n)
        heads = range(SWA_HEADS)
        hs = lambda h: slice(h * HEAD, (h + 1) * HEAD)
        scale = jnp.asarray(0.125, BF16)
        q = [q_ref[:, hs(h)] for h in heads]
        do = [do_ref[:, hs(h)] for h in heads]
        kk = [k2[:, hs(kv)] for kv in range(2)]
        vv = [v2[:, hs(kv)] for kv in range(2)]
        kt = [(kk[kv].astype(F32) * 0.125).T.astype(BF16) for kv in range(2)]
        st = [jnp.where(valid, _dot(kk[h // 4], q[h], NT) * 0.125 + bias_ref[h], NEG) for h in heads]
        dpt = [_dot(vv[h // 4], do[h], NT) for h in heads]
        pt, dst = [], []
        for h in heads:
            sk = sink_ref[h]
            mx = jnp.maximum(jnp.max(st[h], axis=0, keepdims=True), sk)
            p = jnp.exp(st[h] - mx)
            esk = jnp.exp(sk - mx)
            den = jnp.sum(p, axis=0, keepdims=True) + esk
            p = p / den
            delta = jnp.sum(p * dpt[h], axis=0, keepdims=True)
            d = p * (dpt[h] - delta)
            sk_acc[h:h + 1, :] += -(esk / den) * delta
            dbias_ref[h] += d
            pt.append(p.astype(BF16))
            dst.append(d.astype(BF16))
        dq_t = [_dot(kt[h // 4], dst[h]) for h in heads]
        dq_ref[...] = jnp.concatenate(dq_t, axis=0).T.astype(BF16)
        for kv in range(2):
            group = range(4 * kv, 4 * kv + 4)
            dk = [_dot(dst[h], q[h] * scale) for h in group]
            dv = [_dot(pt[h], do[h]) for h in group]
            dkp_ref[win, hs(kv)] += (dk[0] + dk[1]) + (dk[2] + dk[3])
            dvp_ref[win, hs(kv)] += (dv[0] + dv[1]) + (dv[2] + dv[3])

        @pl.when(n == nb - 1)
        def _():
            dsink_ref[...] = jnp.broadcast_to(jnp.sum(sk_acc[...], axis=1, keepdims=True), dsink_ref.shape)

    return pl.pallas_call(
        body, name="swa_bwd", grid=(nb,),
        in_specs=[pl.BlockSpec(memory_space=pltpu.SMEM),
                  pl.BlockSpec((SWA_BLOCK, 512), lambda n: (n, 0)),
                  _full(kp.shape), _full(vp.shape), _full(bias_t.shape),
                  pl.BlockSpec((SWA_BLOCK, 512), lambda n: (n, 0))],
        out_specs=[pl.BlockSpec((SWA_BLOCK, 512), lambda n: (n, 0)), _full(kp.shape), _full(vp.shape),
                   _full(bias_t.shape), _full((SWA_HEADS, 128))],
        out_shape=[jax.ShapeDtypeStruct((s, 512), BF16), jax.ShapeDtypeStruct(kp.shape, F32),
                   jax.ShapeDtypeStruct(vp.shape, F32), jax.ShapeDtypeStruct(bias_t.shape, F32),
                   jax.ShapeDtypeStruct((SWA_HEADS, 128), F32)],
        scratch_shapes=[pltpu.VMEM((SWA_HEADS, 128), F32)],
        compiler_params=_params(("arbitrary",)),
    )(sink, qa, kp, vp, bias_t, doa)


def _fox_bwd(qt, k, v, dot, ot, cc4, lse4):
    s = k.shape[0]
    t = min(FOX_BWD_T, s)
    nq = s // t

    def body(qt_ref, k_ref, v_ref, dot_ref, ot_ref, cc_ref, lse_ref,
             dqt_ref, dk_ref, dv_ref, dck_ref, dcq_ref, delta_ref, dkt_acc, dvt_acc, ds0, ds1):
        j = pl.program_id(1)

        @pl.when(j == 0)
        def _():
            dqt_ref[...] = jnp.zeros_like(dqt_ref)
            dcq_ref[...] = jnp.zeros_like(dcq_ref)
            r8 = lax.broadcasted_iota(jnp.int32, (8, t), 0)

            def dl(i, c):
                cols = pl.ds(pl.multiple_of(i * t, t), t)
                pr = dot_ref[:, cols].astype(F32) * ot_ref[:, cols].astype(F32)
                d0 = jnp.sum(jnp.where(_head_rows(0), pr, 0.0), axis=0, keepdims=True)
                d1 = jnp.sum(jnp.where(_head_rows(1), pr, 0.0), axis=0, keepdims=True)
                delta_ref[:, cols] = jnp.where(r8 == 0, d0, jnp.where(r8 == 1, d1, 0.0))
                return c

            lax.fori_loop(0, nq, dl, 0)

        kj = k_ref[...]
        vj = v_ref[...]
        ks = pl.ds(pl.multiple_of(j * t, t), t)
        kt = (kj.astype(F32) * 0.125).T.astype(BF16)
        ke = [jnp.where(_head_mask(e), kj, jnp.zeros_like(kj)) for e in range(2)]
        ve = [jnp.where(_head_mask(e), vj, jnp.zeros_like(vj)) for e in range(2)]
        ck = [cc_ref[0, ks, e:e + 1] for e in range(2)]
        for r in (dkt_acc, dvt_acc, ds0, ds1):
            r[...] = jnp.zeros_like(r)

        def block(q0, nqs, k0, nks, masked):
            cols = pl.ds(pl.multiple_of(q0, 128), nqs)
            rows = slice(k0, k0 + nks)
            qti = qt_ref[:, cols]
            doti = dot_ref[:, cols]
            for e, ds_acc in enumerate((ds0, ds1)):
                dims = slice(e * HEAD, (e + 1) * HEAD)
                st = _dot(ke[e][rows, :], qti) - ck[e][rows, :]
                if masked:
                    krow = lax.broadcasted_iota(jnp.int32, (nks, nqs), 0) + (j * t + k0)
                    qcol = lax.broadcasted_iota(jnp.int32, (nks, nqs), 1) + q0
                    st = jnp.where(krow <= qcol, st, NEG)
                pt = jnp.exp(st - lse_ref[0, e:e + 1, cols])
                dpt = _dot(ve[e][rows, :], doti)
                dst = pt * (dpt - delta_ref[e:e + 1, cols])
                dsb = dst.astype(BF16)
                dvt_acc[dims, rows] += _dot(doti[dims, :], pt.astype(BF16), NT)
                dkt_acc[dims, rows] += _dot(qti[dims, :], dsb, NT)
                dqt_ref[dims, cols] += _dot(kt[dims, rows], dsb)
                ds_acc[rows, 0:nqs] += dst
                dcq_ref[0, e:e + 1, cols] += jnp.sum(dst, axis=0, keepdims=True)

        half = t // 2
        block(j * t, half, 0, half, True)
        block(j * t + half, half, 0, t, True)

        def rest(i, c):
            block(i * t, t, 0, t, False)
            return c

        lax.fori_loop(j + 1, nq, rest, 0)
        dk_ref[...] = dkt_acc[...].T.astype(BF16)
        dv_ref[...] = dvt_acc[...].T.astype(BF16)
        lane = lax.broadcasted_iota(jnp.int32, (t, 128), 1)
        c0 = jnp.sum(ds0[...], axis=-1, keepdims=True)
        c1 = jnp.sum(ds1[...], axis=-1, keepdims=True)
        dck_ref[0] = jnp.where(lane == 0, c0, jnp.where(lane == 1, c1, 0.0))

    res_t = lambda: pl.BlockSpec((128, s), lambda hp, j: (hp, 0))
    blk = lambda: pl.BlockSpec((t, 128), lambda hp, j: (j, hp))
    return pl.pallas_call(
        body, name="fox_bwd", grid=(4, nq),
        in_specs=[res_t(), blk(), blk(), res_t(), res_t(), pl.BlockSpec((1, s, 128), lambda hp, j: (hp, 0, 0)),
                  pl.BlockSpec((1, 8, s), lambda hp, j: (hp, 0, 0))],
        out_specs=[res_t(), blk(), blk(),
                   pl.BlockSpec((1, t, 128), lambda hp, j: (hp, j, 0)),
                   pl.BlockSpec((1, 8, s), lambda hp, j: (hp, 0, 0))],
        out_shape=[jax.ShapeDtypeStruct((512, s), F32), jax.ShapeDtypeStruct((s, 512), BF16),
                   jax.ShapeDtypeStruct((s, 512), BF16), jax.ShapeDtypeStruct((4, s, 128), F32),
                   jax.ShapeDtypeStruct((4, 8, s), F32)],
        scratch_shapes=[pltpu.VMEM((8, s), F32)] + [pltpu.VMEM((128, t), F32)] * 2 + [pltpu.VMEM((t, t), F32)] * 2,
        compiler_params=_params(("arbitrary", "arbitrary")),
    )(qt, k, v, dot, ot, cc4, lse4)


def _mem_bwd(qm, mk, mv, dom):
    s = qm.shape[0]
    tq = min(512, s)

    def body(q_ref, mk_ref, mv_ref, do_ref, dq_ref, dmk_ref, dmv_ref):
        i = pl.program_id(0)

        @pl.when(i == 0)
        def _():
            dmk_ref[...] = jnp.zeros_like(dmk_ref)
            dmv_ref[...] = jnp.zeros_like(dmv_ref)

        heads = range(MEM_HEADS)
        hs = lambda h: slice(h * 128, (h + 1) * 128)
        sc = [_dot(q_ref[:, hs(h)], mk_ref[:, hs(h)], NT) * MEM_SCALE for h in heads]
        dp = [_dot(do_ref[:, hs(h)], mv_ref[:, hs(h)], NT) for h in heads]
        pb, dsb = [], []
        for h in heads:
            p = jnp.exp(sc[h] - jnp.max(sc[h], axis=-1, keepdims=True))
            p = p / jnp.sum(p, axis=-1, keepdims=True)
            ds = p * (dp[h] - jnp.sum(p * dp[h], axis=-1, keepdims=True))
            pb.append(p.astype(BF16))
            dsb.append((ds * MEM_SCALE).astype(BF16))
        dq = [_dot(dsb[h], mk_ref[:, hs(h)]).astype(BF16) for h in heads]
        dmk = [_dot(dsb[h], q_ref[:, hs(h)], TN) for h in heads]
        dmv = [_dot(pb[h], do_ref[:, hs(h)], TN) for h in heads]
        for h in heads:
            dq_ref[:, hs(h)] = dq[h]
            dmk_ref[:, hs(h)] += dmk[h]
            dmv_ref[:, hs(h)] += dmv[h]

    return pl.pallas_call(
        body, name="mem_bwd", grid=(s // tq,),
        in_specs=[pl.BlockSpec((tq, 512), lambda i: (i, 0)), _full(mk.shape), _full(mv.shape),
                  pl.BlockSpec((tq, 512), lambda i: (i, 0))],
        out_specs=[pl.BlockSpec((tq, 512), lambda i: (i, 0)), _full(mk.shape), _full(mv.shape)],
        out_shape=[jax.ShapeDtypeStruct((s, 512), BF16), jax.ShapeDtypeStruct(mk.shape, F32),
                   jax.ShapeDtypeStruct(mv.shape, F32)],
        compiler_params=_params(("arbitrary",)),
    )(qm, mk, mv, dom)


def _memkv_bwd(dmk, dmv, kv_raw, kn_mem, mem, g_mem, mem_n, w_kv):
    def body(dmk_ref, dmv_ref, kv_ref, kn_ref, mem_ref, g_ref, mn_ref, w_ref, dw_ref, dkn_ref, dg_ref, dkv_ref):
        dkn = jnp.zeros((1, 128), F32)
        for h in range(MEM_HEADS):
            hs = slice(h * 128, (h + 1) * 128)
            v = kv_ref[:, hs]
            r = lax.rsqrt(jnp.mean(v * v, axis=-1, keepdims=True) + EPS)
            n = v * r
            dn = dmk_ref[:, hs]
            dkn = dkn + jnp.sum(dn * n, axis=0, keepdims=True)
            dng = dn * kn_ref[...]
            dkv_ref[:, hs] = (r * (dng - n * jnp.mean(dng * n, axis=-1, keepdims=True))).astype(BF16)
        dkv_ref[:, 512:1024] = dmv_ref[...].astype(BF16)
        dkn_ref[...] = dkn
        dkv = dkv_ref[...]
        dw_ref[...] = _dot(mn_ref[...], dkv, TN).astype(BF16)
        dmn = _dot(dkv, w_ref[...], NT)
        xv = mem_ref[...]
        r = lax.rsqrt(jnp.mean(xv * xv, axis=-1, keepdims=True) + EPS)
        dg_ref[...] = jnp.sum(dmn * (xv * r), axis=0, keepdims=True)

    m = mem.shape[0]
    return pl.pallas_call(
        body, name="memkv_bwd",
        out_shape=[jax.ShapeDtypeStruct((D_MODEL, 1024), BF16), jax.ShapeDtypeStruct((1, 128), F32),
                   jax.ShapeDtypeStruct((1, D_MODEL), F32)],
        scratch_shapes=[pltpu.VMEM((m, 1024), BF16)],
        compiler_params=pltpu.CompilerParams(vmem_limit_bytes=VMEM_LIMIT),
    )(dmk, dmv, kv_raw, kn_mem, mem, g_mem, mem_n, w_kv)


def _fox_gate_bwd(dcq4, dck4, proj, b_forget128):
    s = dck4.shape[1]
    tm = min(512, s)
    nt = s // tm

    def body(dcq_ref, dck_ref, p_ref, b_ref, dfl_ref, db_ref, carry_ref):
        i = pl.program_id(0)

        @pl.when(i == 0)
        def _():
            carry_ref[...] = jnp.zeros_like(carry_ref)
            db_ref[...] = jnp.zeros_like(db_ref)

        dcv = jnp.zeros((tm, 128), F32)
        for hp in range(4):
            by_query = jnp.concatenate([dcq_ref[hp], jnp.zeros((120, tm), F32)], axis=0).T
            d = by_query - dck_ref[hp]
            dcv = dcv + (d if hp == 0 else pltpu.roll(d, 2 * hp, 1))
        dlogf = jnp.dot(_tri(tm, False), dcv, precision=lax.Precision.HIGHEST, preferred_element_type=F32) + carry_ref[...]
        carry_ref[...] += jnp.sum(dcv, axis=0, keepdims=True)
        z = p_ref[...] + b_ref[...]
        dfl = dlogf * (1.0 / (1.0 + jnp.exp(z)))
        dfl_ref[...] = dfl.astype(BF16)
        db_ref[...] += jnp.sum(dfl, axis=0, keepdims=True)

    return pl.pallas_call(
        body, name="fox_gate_bwd", grid=(nt,),
        in_specs=[pl.BlockSpec((4, 8, tm), lambda i: (0, 0, nt - 1 - i)),
                  pl.BlockSpec((4, tm, 128), lambda i: (0, nt - 1 - i, 0)),
                  pl.BlockSpec((tm, 128), lambda i: (nt - 1 - i, 0)), _full((1, 128))],
        out_specs=[pl.BlockSpec((tm, 128), lambda i: (nt - 1 - i, 0)), _full((1, 128))],
        out_shape=[jax.ShapeDtypeStruct((s, 128), BF16), jax.ShapeDtypeStruct((1, 128), F32)],
        scratch_shapes=[pltpu.VMEM((1, 128), F32)],
        compiler_params=_params(("arbitrary",)),
    )(dcq4, dck4, proj, b_forget128)


def _proj_pre_bwd(dproj, proj, dqf, dkf, dvf, dqm, dqa, dka, dva, dfl, gq_fox, gk_fox, gq_mem, gq_swa, gk_swa):
    s = proj.shape[0]
    tm = min(256, s)

    def body(dp_in, p_ref, dqf_ref, dkf_ref, dvf_ref, dqm_ref, dqa_ref, dka_ref, dva_ref, dfl_ref,
             gqf, gkf, gqm, gqa, gka, dp_ref, dgn_ref):
        i = pl.program_id(0)

        @pl.when(i == 0)
        def _():
            dgn_ref[...] = jnp.zeros_like(dgn_ref)

        def norm_bwd(off, width, hd, g_ref, dn_ref, slot):
            acc = jnp.zeros((1, 128), F32)
            for b in range(width // 128):
                v = p_ref[:, off + b * 128: off + (b + 1) * 128].astype(F32)
                r = lax.rsqrt(_group_mean(v * v, hd) + EPS)
                n = v * r
                dn = dn_ref[b * 128:(b + 1) * 128, :].T if slot == 0 else dn_ref[:, b * 128:(b + 1) * 128].astype(F32)
                acc = acc + jnp.sum(dn * n, axis=0, keepdims=True)
                dng = dn * g_ref[...]
                dp_ref[:, off + b * 128: off + (b + 1) * 128] = (r * (dng - n * _group_mean(dng * n, hd))).astype(BF16)
            dgn_ref[slot:slot + 1, :] += acc

        norm_bwd(H_QF, 512, HEAD, gqf, dqf_ref, 0)
        norm_bwd(H_KF, 512, HEAD, gkf, dkf_ref, 1)
        dp_ref[:, H_VF:H_VF + 512] = dvf_ref[...].astype(BF16)
        norm_bwd(H_QM, 512, MEM_HEAD, gqm, dqm_ref, 2)
        norm_bwd(H_QA, 512, HEAD, gqa, dqa_ref, 3)
        norm_bwd(H_KA, 128, HEAD, gka, dka_ref, 4)
        dp_ref[:, H_VA:H_VA + 128] = dva_ref[...].astype(BF16)
        dp_ref[:, H_FL:H_FL + 128] = dfl_ref[...]
        dp_ref[:, H_FL + 128:HALF_W] = jnp.zeros((tm, HALF_W - H_FL - 128), BF16)

    row = lambda w: pl.BlockSpec((tm, w), lambda i: (i, 0))
    g_spec = _full((1, 128))
    return pl.pallas_call(
        body, name="proj_pre_bwd", grid=(s // tm,),
        in_specs=[pl.BlockSpec(memory_space=pl.ANY), pl.BlockSpec((tm, HALF_W), lambda i: (i, 1)),
                  pl.BlockSpec((512, tm), lambda i: (0, i)), row(512), row(512), row(512), row(512),
                  row(128), row(128), row(128), g_spec, g_spec, g_spec, g_spec, g_spec],
        out_specs=[pl.BlockSpec((tm, HALF_W), lambda i: (i, 1)), _full((8, 128))],
        out_shape=[jax.ShapeDtypeStruct((s, PROJ_W), BF16), jax.ShapeDtypeStruct((8, 128), F32)],
        input_output_aliases={0: 0},
        compiler_params=_params(("arbitrary",)),
    )(dproj, proj, dqf, dkf, dvf, dqm, dqa, dka, dva, dfl, gq_fox, gk_fox, gq_mem, gq_swa, gk_swa)


def _in_bwd_x(dproj, w_in_p, x, g_mix, dx1):
    s = x.shape[0]
    tm = min(256, s)

    def body(dp_ref, w_ref, x_ref, g_ref, dx1_ref, gx_ref, dg_ref):
        i = pl.program_id(0)

        @pl.when(i == 0)
        def _():
            dg_ref[...] = jnp.zeros_like(dg_ref)

        dx, dg = _rms_bwd(x_ref[...], g_ref[...], _dot(dp_ref[...], w_ref[...], NT), dx1_ref[...])
        gx_ref[...] = dx
        dg_ref[...] += dg

    row = pl.BlockSpec((tm, D_MODEL), lambda i: (i, 0))
    return pl.pallas_call(
        body, name="in_bwd_x", grid=(s // tm,),
        in_specs=[pl.BlockSpec((tm, PROJ_W), lambda i: (i, 0)), _full(w_in_p.shape), row, _full((1, D_MODEL)), row],
        out_specs=[row, _full((1, D_MODEL))],
        out_shape=[jax.ShapeDtypeStruct((s, D_MODEL), F32), jax.ShapeDtypeStruct((1, D_MODEL), F32)],
        compiler_params=_params(("arbitrary",)),
    )(dproj, w_in_p, x, g_mix, dx1)


def _rel_bias_bwd(dbias, bucket):
    def body(db_ref, bk_ref, o_ref):
        bk = bk_ref[...]
        lane = lax.broadcasted_iota(jnp.int32, (1, 128), 1)
        for b in range(REL_BUCKETS):
            sel = bk == b
            acc = jnp.zeros((1, 128), F32)
            for h in range(SWA_HEADS):
                tot = jnp.sum(jnp.sum(jnp.where(sel, db_ref[h], 0.0), axis=-1, keepdims=True), axis=0, keepdims=True)
                acc = jnp.where(lane == h, tot, acc)
            o_ref[:, b * 128:(b + 1) * 128] = acc

    return pl.pallas_call(
        body, name="rel_bias_bwd",
        out_shape=jax.ShapeDtypeStruct((1, REL_BUCKETS * 128), F32),
        compiler_params=pltpu.CompilerParams(vmem_limit_bytes=VMEM_LIMIT),
    )(dbias, bucket)


def _my_place():
    return lax.axis_index("x"), lax.axis_index("y"), lax.axis_index("c")


def _peer(place, k):
    x, y, c = place
    return (1 - x if k & 4 else x, 1 - y if k & 2 else y, 1 - c if k & 1 else c)


def _index(place):
    x, y, c = place
    return 4 * x + 2 * y + c


HBM_SPEC = pl.BlockSpec(memory_space=pltpu.HBM)
SEM_SPEC = pl.BlockSpec(memory_space=pltpu.SEMAPHORE)
DATAFLOW = pltpu.SideEffectType.DATAFLOW_SIDE_EFFECTING


ALL_PEERS = tuple(range(1, N_DEV))
SAME_CORE = (2, 4, 6)
OWN = N_DEV - 1


def _split_copy(src_ref, land_ref, send_sems, recv_sems, me, k, gather):
    peer = _peer(me, k)
    if gather:
        src, dst = src_ref, land_ref.at[_index(me)]
    else:
        src, dst = src_ref.at[_index(peer)], land_ref.at[k - 1]
    return pltpu.make_async_remote_copy(src_ref=src, dst_ref=dst, send_sem=send_sems.at[k - 1], recv_sem=recv_sems.at[k - 1],
                                        device_id=peer, device_id_type=MESH)


def _own_copy(src_ref, land_ref, recv_sems, me, gather):
    if gather:
        src, dst = src_ref, land_ref.at[_index(me)]
    else:
        src, dst = src_ref.at[_index(me)], land_ref.at[OWN]
    return pltpu.make_async_copy(src, dst, recv_sems.at[OWN])


def _split_start(srcs, gather, name, peers=ALL_PEERS, after=None):
    n = len(srcs)
    extra = [] if after is None else [after]

    def body(*refs):
        refs = refs[:2 * n] + refs[2 * n + len(extra):]
        src_refs, land_refs = refs[:n], refs[n:2 * n]
        send_sems, recv_sems, token = refs[2 * n:3 * n], refs[3 * n:4 * n], refs[-1]
        me = _my_place()
        for w in range(n):
            for k in peers:
                _split_copy(src_refs[w], land_refs[w], send_sems[w], recv_sems[w], me, k, gather).start()
            _own_copy(src_refs[w], land_refs[w], recv_sems[w], me, gather).start()
        token[...] = jnp.zeros_like(token)

    lands = [lax.empty((N_DEV,) + (a.shape if gather else a.shape[1:]), a.dtype) for a in srcs]
    sems = [pltpu.SemaphoreType.DMA((N_DEV,))] * (2 * n)
    hbm = [pltpu.HBM(a.shape, a.dtype) for a in list(srcs) + lands]
    outs = pl.pallas_call(
        body, name=name,
        out_shape=(*sems, *hbm, jax.ShapeDtypeStruct((8, 128), F32)),
        in_specs=(HBM_SPEC,) * (2 * n) + (pl.BlockSpec(memory_space=pl.ANY),) * len(extra),
        out_specs=(SEM_SPEC,) * (2 * n) + (HBM_SPEC,) * (2 * n) + (pl.BlockSpec(memory_space=pltpu.VMEM),),
        input_output_aliases={i: 2 * n + i for i in range(2 * n)},
        compiler_params=pltpu.CompilerParams(has_side_effects=DATAFLOW),
    )(*[pltpu.with_memory_space_constraint(a, pltpu.HBM) for a in list(srcs) + lands], *extra)
    return list(outs[:n]), list(outs[n:2 * n]), list(outs[2 * n:3 * n]), list(outs[3 * n:4 * n]), outs[-1]


def _split_wait(started, w, after, gather, name):
    send_sems, recv_sems, srcs, lands, _ = started

    def body(src_ref, land_ref, send_sems, recv_sems, after_ref, src_out, land_out):
        me = _my_place()
        for k in ALL_PEERS:
            cp = _split_copy(src_ref, land_ref, send_sems, recv_sems, me, k, gather)
            cp.wait_send()
            cp.wait_recv()
        _own_copy(src_ref, land_ref, recv_sems, me, gather).wait()

    return pl.pallas_call(
        body, name=name,
        out_shape=(pltpu.HBM(srcs[w].shape, srcs[w].dtype), pltpu.HBM(lands[w].shape, lands[w].dtype)),
        in_specs=(HBM_SPEC, HBM_SPEC, SEM_SPEC, SEM_SPEC, pl.BlockSpec(memory_space=pl.ANY)),
        out_specs=(HBM_SPEC, HBM_SPEC), input_output_aliases={0: 0, 1: 1},
        compiler_params=pltpu.CompilerParams(has_side_effects=DATAFLOW),
    )(srcs[w], lands[w], send_sems[w], recv_sems[w], after)[1]


def _forward_copy(land_ref, send_sems, recv_sems, me, j, incoming):
    sibling = _peer(me, 1)
    rows = land_ref.at[_index(_peer(sibling if incoming else me, SAME_CORE[j]))]
    return pltpu.make_async_remote_copy(src_ref=rows, dst_ref=rows, send_sem=send_sems.at[j], recv_sem=recv_sems.at[j],
                                        device_id=sibling, device_id_type=MESH)


def _forward_start(started, after, name):
    send_a, recv_a, srcs, lands, _ = started

    def body(src_ref, land_ref, send_a, recv_a, after_ref, send_b, recv_b, src_out, land_out):
        me = _my_place()
        for j, k in enumerate(SAME_CORE):
            _split_copy(src_ref, land_ref, send_a, recv_a, me, k, True).wait_recv()
            _forward_copy(land_ref, send_b, recv_b, me, j, False).start()

    sems = pltpu.SemaphoreType.DMA((len(SAME_CORE),))
    return pl.pallas_call(
        body, name=name,
        out_shape=(sems, sems, pltpu.HBM(srcs[0].shape, srcs[0].dtype), pltpu.HBM(lands[0].shape, lands[0].dtype)),
        in_specs=(HBM_SPEC, HBM_SPEC, SEM_SPEC, SEM_SPEC, pl.BlockSpec(memory_space=pl.ANY)),
        out_specs=(SEM_SPEC, SEM_SPEC, HBM_SPEC, HBM_SPEC), input_output_aliases={0: 2, 1: 3},
        compiler_params=pltpu.CompilerParams(has_side_effects=DATAFLOW),
    )(srcs[0], lands[0], send_a[0], recv_a[0], after)


def _forward_wait(started, forwarded, name):
    send_a, recv_a, _, _, _ = started
    send_b, recv_b, src, land = forwarded

    def body(src_ref, land_ref, send_a, recv_a, send_b, recv_b, src_out, land_out):
        me = _my_place()
        _own_copy(src_ref, land_ref, recv_a, me, True).wait()
        for k in (1,) + SAME_CORE:
            _split_copy(src_ref, land_ref, send_a, recv_a, me, k, True).wait_send()
        _split_copy(src_ref, land_ref, send_a, recv_a, me, 1, True).wait_recv()
        for j in range(len(SAME_CORE)):
            _forward_copy(land_ref, send_b, recv_b, me, j, False).wait_send()
            _forward_copy(land_ref, send_b, recv_b, me, j, True).wait_recv()

    return pl.pallas_call(
        body, name=name,
        out_shape=(pltpu.HBM(src.shape, src.dtype), pltpu.HBM(land.shape, land.dtype)),
        in_specs=(HBM_SPEC, HBM_SPEC, SEM_SPEC, SEM_SPEC, SEM_SPEC, SEM_SPEC),
        out_specs=(HBM_SPEC, HBM_SPEC), input_output_aliases={0: 0, 1: 1},
        compiler_params=pltpu.CompilerParams(has_side_effects=DATAFLOW),
    )(src, land, send_a[0], recv_a[0], send_b, recv_b)[1]


def _adam_math(w, g, m, v):
    m2 = ADAM_B1 * m + (1.0 - ADAM_B1) * g
    v2 = ADAM_B2 * v + (1.0 - ADAM_B2) * (g * g)
    m_hat = m2 / (1.0 - ADAM_B1 ** ADAM_STEP)
    v_hat = v2 / (1.0 - ADAM_B2 ** ADAM_STEP)
    delta = -ADAM_LR * (m_hat / (jnp.sqrt(v_hat) + ADAM_EPS) + ADAM_WD * w)
    return delta, m2, v2


def _adamw(land, w, m, v, name):
    a, b = w.shape
    bp = land.shape[2]
    ta = min(128, a)

    def body(p_ref, w_ref, m_ref, v_ref, g_ref, d_ref, m2_ref, v2_ref):
        g = p_ref[0, :, 0:b].astype(F32)
        for k in range(1, N_DEV):
            g = g + p_ref[k, :, 0:b].astype(F32)
        delta, m2, v2 = _adam_math(w_ref[...], g, m_ref[...], v_ref[...])
        g_ref[...] = g
        d_ref[...] = delta
        m2_ref[...] = m2
        v2_ref[...] = v2

    blk = pl.BlockSpec((ta, b), lambda i: (i, 0))
    sd = jax.ShapeDtypeStruct((a, b), F32)
    return pl.pallas_call(
        body, name=name, grid=(a // ta,),
        in_specs=[pl.BlockSpec((N_DEV, ta, bp), lambda i: (0, i, 0)), blk, blk, blk],
        out_specs=[blk, blk, blk, blk], out_shape=[sd, sd, sd, sd],
        compiler_params=_params(("parallel",)),
    )(land, w, m, v)


def _bucket_table():
    t_loc = jnp.arange(SWA_BLOCK)[:, None] + SWA_BLOCK
    s_loc = jnp.arange(2 * SWA_BLOCK)[None, :]
    dist = t_loc - s_loc
    max_exact = REL_BUCKETS // 2
    d = jnp.maximum(dist, 0)
    df = jnp.maximum(d, 1).astype(F32)
    large = max_exact + (jnp.log(df / max_exact) / math.log(REL_MAX_DIST / max_exact) * (REL_BUCKETS - max_exact)).astype(jnp.int32)
    large = jnp.minimum(large, REL_BUCKETS - 1)
    bucket = jnp.where(d < max_exact, d, large)
    band = (dist >= 0) & (dist < SWA_BLOCK)
    return bucket, band


def _tile2(g):
    return jnp.concatenate([g, g], axis=1) if g.shape[1] == HEAD else g


SHARD_W = 737
SHARD_WP = 768
IN_WIDTH = N_DEV * SHARD_W
SEGMENTS = ((GL0, 2824, 3072), (QF0, 768, 512), (KF0, 1280, 512), (VF0, 1792, 512), (QM0, 2312, 512),
            (QA0, 0, 512), (KA0, 512, 128), (VA0, 640, 128), (FL0, 2304, 8))


def _lane_plan(sources):
    plan = []
    for t in range(len(sources) // 128):
        groups = {}
        for lane in range(128):
            src = sources[128 * t + lane]
            if src is not None:
                slab, col = src
                groups.setdefault((slab, col // 128, (lane - col) % 128), []).append(lane)
        tile = []
        for key, lanes in groups.items():
            assert lanes == list(range(lanes[0], lanes[-1] + 1))
            tile.append((key, lanes[0], lanes[-1] + 1))
        plan.append(tile)
    return plan


def _assemble(tile_plan, load, rows):
    lane = lax.broadcasted_iota(jnp.int32, (1, 128), 1)
    out = jnp.zeros((rows, 128), F32)
    for (slab, st, roll), lo, hi in tile_plan:
        v = load(slab, st)
        if roll:
            v = pltpu.roll(v, roll, 1)
        out = v if (lo, hi) == (0, 128) else jnp.where((lane >= lo) & (lane < hi), v, out)
    return out


def _w_in_from_shards(land):
    ref_col = [None] * PROJ_W
    for p0, r0, n in SEGMENTS:
        for i in range(n):
            ref_col[p0 + i] = divmod(r0 + i, SHARD_W)
    plan = _lane_plan(ref_col)
    d_model = land.shape[1]
    tm = 256

    def body(land_ref, o_ref):
        load = lambda slab, st: land_ref[slab, :, st * 128:(st + 1) * 128].astype(F32)
        for t, tile_plan in enumerate(plan):
            o_ref[:, t * 128:(t + 1) * 128] = _assemble(tile_plan, load, tm).astype(BF16)

    return pl.pallas_call(
        body, name="w_in_from_shards", grid=(d_model // tm,),
        in_specs=[pl.BlockSpec((N_DEV, tm, SHARD_WP), lambda i: (0, i, 0))],
        out_specs=pl.BlockSpec((tm, PROJ_W), lambda i: (i, 0)),
        out_shape=jax.ShapeDtypeStruct((d_model, PROJ_W), BF16),
        compiler_params=_params(("parallel",)),
    )(land)


def _dw_in_to_parts(dwp):
    padded_col = [None] * IN_WIDTH
    for p0, r0, n in SEGMENTS:
        for i in range(n):
            padded_col[r0 + i] = p0 + i
    sources = []
    for d in range(N_DEV):
        sources += [(0, padded_col[SHARD_W * d + c]) if c < SHARD_W else None for c in range(SHARD_WP)]
    plan = _lane_plan(sources)
    d_model = dwp.shape[0]
    tm = 256
    tiles = SHARD_WP // 128

    def body(dw_ref, o_ref):
        load = lambda slab, st: dw_ref[:, st * 128:(st + 1) * 128].astype(F32)
        for t, tile_plan in enumerate(plan):
            d, c = divmod(t, tiles)
            o_ref[d, :, c * 128:(c + 1) * 128] = _assemble(tile_plan, load, tm).astype(BF16)

    return pl.pallas_call(
        body, name="dw_in_to_parts", grid=(d_model // tm,),
        in_specs=[pl.BlockSpec((tm, PROJ_W), lambda i: (i, 0))],
        out_specs=pl.BlockSpec((N_DEV, tm, SHARD_WP), lambda i: (0, i, 0)),
        out_shape=jax.ShapeDtypeStruct((N_DEV, d_model, SHARD_WP), BF16),
        compiler_params=_params(("parallel",)),
    )(dwp)


def _cast_shards(shards):
    names = list(shards)

    def body(*refs):
        for src, dst in zip(refs[:len(names)], refs[len(names):]):
            if dst.shape != src.shape:
                dst[...] = jnp.zeros(dst.shape, BF16)
                dst[:, 0:src.shape[1]] = src[...].astype(BF16)
            else:
                dst[...] = src[...].astype(BF16)

    out_shape = [jax.ShapeDtypeStruct((shards[n].shape[0], SHARD_WP if n == "w_in" else shards[n].shape[1]), BF16)
                 for n in names]
    outs = pl.pallas_call(body, name="cast_shards", out_shape=out_shape,
                          compiler_params=pltpu.CompilerParams(vmem_limit_bytes=VMEM_LIMIT))(*[shards[n] for n in names])
    return dict(zip(names, outs))


def _tie(x, *tokens):
    for t in tokens:
        if t is not None:
            x = x + t[0:1, 0:1]
    return x


def _local_step(x, mem, target, p, getw, emit, deps=()):
    s = x.shape[0]
    bucket, band = _bucket_table()
    bucket_m = jnp.where(band, bucket, -1).astype(jnp.int32)
    bias = _bias_table(p["rel_bias"], bucket_m)
    bucket_t = jnp.transpose(bucket_m)
    bias_t = _bias_table(p["rel_bias"], bucket_t)
    gqf, gkf, gqa, gka = _tile2(p["qn_fox"]), _tile2(p["kn_fox"]), _tile2(p["qn_swa"]), _tile2(p["kn_swa"])
    gqm = p["qn_mem"]
    bf128 = jnp.pad(p["b_forget"], ((0, 0), (0, 120)))
    sink = p["sink_swa"].reshape(8)

    h = _rms_fwd(x, p["g_mix"], "rms_mix", tuple(deps) + (bias, bias_t))
    w_in = getw("w_in", h)
    proj = _mm(h, w_in, "nn", BF16, 512, 1536, 1024, "proj")
    fl = _mm(h, w_in[:, FL0:FL0 + 128], "nn", F32, 512, 128, 1024, "proj_fl")
    qf, kf, vf, qm, qa, ka, va, qf_t, vf_t = _proj_post(proj, gqf, gkf, gqm, gqa, gka)
    cc4, ca4 = _fox_gate_fwd(fl, bf128)
    w_kv = getw("w_mem_kv", cc4)
    mem_n, kv_raw, mk, mv = _memkv_fwd(mem, p["g_mem"], w_kv, p["kn_mem"])
    kp = jnp.pad(ka, ((SWA_BLOCK, 0), (0, 0)))
    vp = jnp.pad(va, ((SWA_BLOCK, 0), (0, 0)))
    oa = _swa_fwd(qa, kp, vp, bias, sink)
    of, lse4, of_t = _fox_fwd(qf, kf, vf_t, ca4)
    om = _mem_fwd(qm, mk, mv)
    wa, wf, wm, w_out = getw("w_o_swa", oa), getw("w_o_fox", oa), getw("w_o_mem", oa), getw("w_out", oa)
    x1, hm, merged = _merge_fwd(x, oa, of, om, proj, p["b_gate"], wa, wf, wm, w_out, p["g_mlp"])
    w_up = getw("w_mlp_up", of)
    u = _mlp_up(hm, w_up)
    w_down = getw("w_mlp_down", hm)
    dy, dy_b, loss = _mlp_down_loss(u, w_down, x1, target)

    da = _mlp_bwd_act(dy_b, w_down, u)
    t_down = emit({"w_mlp_down": _mm(u, dy_b, "tn", BF16, 1024, 1024, 2048, "dw_down")})
    dx1, dg_mlp = _mlp_bwd_x(da, w_up, x1, dy, _tie(p["g_mlp"], t_down))
    t_up = emit({"w_mlp_up": _mm(hm, da, "tn", BF16, 1024, 1024, 2048, "dw_up", column_chunks=True)})
    dproj, doa, dof_t, dom, dya, dyf, dym, db_gate = _merge_bwd(
        dx1, oa, of, om, proj, _tie(p["b_gate"], t_up), wa, wf, wm, w_out)
    t_o = emit({"w_out": _mm(merged, dx1, "tn", BF16, 1024, 1024, 2048, "dw_out"),
                "w_o_swa": _mm(oa, dya, "tn", BF16, 512, 1024, 2048, "dw_o_swa"),
                "w_o_fox": _mm(of, dyf, "tn", BF16, 512, 1024, 2048, "dw_o_fox"),
                "w_o_mem": _mm(om, dym, "tn", BF16, 512, 1024, 2048, "dw_o_mem")})

    dqm, dmk, dmv = _mem_bwd(qm, mk, mv, dom)
    dw_kv, dkn_mem, dg_mem = _memkv_bwd(dmk, dmv, kv_raw, _tie(p["kn_mem"], t_o), mem, p["g_mem"], mem_n, w_kv)
    t_kv = emit({"w_mem_kv": dw_kv})
    dqa, dkp, dvp, dbias, dsink = _swa_bwd(qa, kp, vp, bias_t, _tie(p["sink_swa"], t_kv).reshape(8), doa)
    dqf_t, dkf, dvf, dck4, dcq4 = _fox_bwd(qf_t, kf, vf, dof_t, of_t, cc4, lse4)

    dfl, db_forget = _fox_gate_bwd(dcq4, dck4, fl, bf128)

    dproj, dgn = _proj_pre_bwd(dproj, proj, dqf_t, dkf, dvf, dqm, dqa, dkp[SWA_BLOCK:], dvp[SWA_BLOCK:], dfl,
                               gqf, gkf, gqm, gqa, gka)
    t_in = emit({"w_in": _mm(h, dproj, "tn", BF16, 1024, 3072, 1024, "dw_in")})
    grad_x, dg_mix = _in_bwd_x(dproj, w_in, x, _tie(p["g_mix"], t_in), dx1)
    d_rel = _rel_bias_bwd(dbias, bucket_t)

    fold = lambda r: dgn[r:r + 1, 0:HEAD] + dgn[r:r + 1, HEAD:128]
    small = {
        "g_mix": dg_mix, "b_gate": db_gate, "b_forget": db_forget[:, 0:8],
        "qn_swa": fold(3), "kn_swa": fold(4), "sink_swa": dsink[:, 0].reshape(1, 8), "rel_bias": d_rel,
        "qn_fox": fold(0), "kn_fox": fold(1), "g_mem": dg_mem, "qn_mem": dgn[2:3, :], "kn_mem": dkn_mem,
        "g_mlp": dg_mlp,
    }
    return loss, grad_x, small


SMALL = ("g_mix", "b_gate", "b_forget", "qn_swa", "kn_swa", "sink_swa", "rel_bias", "qn_fox", "kn_fox", "g_mem",
         "qn_mem", "kn_mem", "g_mlp")
BIG = ("w_in", "w_mem_kv", "w_o_swa", "w_o_fox", "w_o_mem", "w_out", "w_mlp_up", "w_mlp_down")
COL_SHARDED = ("w_in", "w_o_swa", "w_o_fox", "w_o_mem", "w_mlp_up")
WEIGHTS = ("g_mix", "w_in", "b_gate", "b_forget", "qn_swa", "kn_swa", "sink_swa", "rel_bias", "qn_fox", "kn_fox", "g_mem",
           "w_mem_kv", "qn_mem", "kn_mem", "w_o_swa", "w_o_fox", "w_o_mem", "w_out", "g_mlp", "w_mlp_up", "w_mlp_down")
SMALL_SLOTS = (("g_mix", 1024), ("b_gate", 3072), ("b_forget", 128), ("qn_swa", 128), ("kn_swa", 128), ("sink_swa", 128),
               ("rel_bias", REL_BUCKETS * 128), ("qn_fox", 128), ("kn_fox", 128), ("g_mem", 1024), ("qn_mem", 128),
               ("kn_mem", 128), ("g_mlp", 1024), ("loss", 128))
SMALL_OFF = {n: sum(w for _, w in SMALL_SLOTS[:i]) for i, (n, _) in enumerate(SMALL_SLOTS)}
SMALL_ROW = sum(w for _, w in SMALL_SLOTS)


def _gathered_to_full(name, g):
    if name in COL_SHARDED:
        return jnp.transpose(g, (1, 0, 2)).reshape(g.shape[1], N_DEV * g.shape[2])
    return g.reshape(N_DEV * g.shape[1], g.shape[2])


def _full_to_parts(name, full, b):
    if name in COL_SHARDED:
        return jnp.transpose(full.reshape(full.shape[0], N_DEV, b), (1, 0, 2)).astype(BF16)
    return full.reshape(N_DEV, full.shape[0] // N_DEV, full.shape[1]).astype(BF16)


def _pack_small(grads, loss):
    pieces = []
    for n, width in SMALL_SLOTS:
        a = loss.reshape(1, 1) if n == "loss" else grads[n].reshape(1, -1)
        pieces.append(jnp.pad(a, ((0, 0), (0, width - a.shape[1]))))
    return jnp.concatenate(pieces, axis=1)


def _adamw_small(gathered, w, m, v):
    names = list(SMALL)

    def body(*refs):
        p_ref = refs[0]
        ins = refs[1:1 + 3 * len(names)]
        outs = refs[1 + 3 * len(names):]
        g_all = p_ref[0]
        for k in range(1, N_DEV):
            g_all = g_all + p_ref[k]
        for i, n in enumerate(names):
            w_ref, m_ref, v_ref = ins[3 * i:3 * i + 3]
            out = outs[4 * i:4 * i + 4]
            rows, cols = w_ref.shape
            for r in range(rows):
                off = SMALL_OFF[n] + 128 * r
                g = g_all[:, off:off + cols]
                rs = slice(r, r + 1)
                res = (g,) + _adam_math(w_ref[rs, :], g, m_ref[rs, :], v_ref[rs, :])
                for o_ref, val in zip(out, res):
                    o_ref[rs, :] = val
        outs[-1][...] = g_all[:, SMALL_OFF["loss"]:SMALL_OFF["loss"] + 128]

    args = [gathered]
    out_shape = []
    for n in names:
        args += [w[n], m[n], v[n]]
        out_shape += [jax.ShapeDtypeStruct(w[n].shape, F32)] * 4
    out_shape.append(jax.ShapeDtypeStruct((1, 128), F32))
    outs = pl.pallas_call(body, name="adamw_small", out_shape=out_shape)(*args)
    return {n: outs[4 * i:4 * i + 4] for i, n in enumerate(names)}, outs[-1]


def kernel(x, mem, g_mix, w_in, b_gate, b_forget, qn_swa, kn_swa, sink_swa, rel_bias, qn_fox, kn_fox, g_mem, w_mem_kv, qn_mem, kn_mem, w_o_swa, w_o_fox, w_o_mem, w_out, g_mlp, w_mlp_up, w_mlp_down, loss_target, m_g_mix, m_w_in, m_b_gate, m_b_forget, m_qn_swa, m_kn_swa, m_sink_swa, m_rel_bias, m_qn_fox, m_kn_fox, m_g_mem, m_w_mem_kv, m_qn_mem, m_kn_mem, m_w_o_swa, m_w_o_fox, m_w_o_mem, m_w_out, m_g_mlp, m_w_mlp_up, m_w_mlp_down, v_g_mix, v_w_in, v_b_gate, v_b_forget, v_qn_swa, v_kn_swa, v_sink_swa, v_rel_bias, v_qn_fox, v_kn_fox, v_g_mem, v_w_mem_kv, v_qn_mem, v_kn_mem, v_w_o_swa, v_w_o_fox, v_w_o_mem, v_w_out, v_g_mlp, v_w_mlp_up, v_w_mlp_down):
    wts = dict(g_mix=g_mix, w_in=w_in, b_gate=b_gate, b_forget=b_forget, qn_swa=qn_swa, kn_swa=kn_swa, sink_swa=sink_swa,
               rel_bias=rel_bias, qn_fox=qn_fox, kn_fox=kn_fox, g_mem=g_mem, w_mem_kv=w_mem_kv, qn_mem=qn_mem, kn_mem=kn_mem,
               w_o_swa=w_o_swa, w_o_fox=w_o_fox, w_o_mem=w_o_mem, w_out=w_out, g_mlp=g_mlp, w_mlp_up=w_mlp_up,
               w_mlp_down=w_mlp_down)
    mom = dict(g_mix=m_g_mix, w_in=m_w_in, b_gate=m_b_gate, b_forget=m_b_forget, qn_swa=m_qn_swa, kn_swa=m_kn_swa,
               sink_swa=m_sink_swa, rel_bias=m_rel_bias, qn_fox=m_qn_fox, kn_fox=m_kn_fox, g_mem=m_g_mem, w_mem_kv=m_w_mem_kv,
               qn_mem=m_qn_mem, kn_mem=m_kn_mem, w_o_swa=m_w_o_swa, w_o_fox=m_w_o_fox, w_o_mem=m_w_o_mem, w_out=m_w_out,
               g_mlp=m_g_mlp, w_mlp_up=m_w_mlp_up, w_mlp_down=m_w_mlp_down)
    var = dict(g_mix=v_g_mix, w_in=v_w_in, b_gate=v_b_gate, b_forget=v_b_forget, qn_swa=v_qn_swa, kn_swa=v_kn_swa,
               sink_swa=v_sink_swa, rel_bias=v_rel_bias, qn_fox=v_qn_fox, kn_fox=v_kn_fox, g_mem=v_g_mem, w_mem_kv=v_w_mem_kv,
               qn_mem=v_qn_mem, kn_mem=v_kn_mem, w_o_swa=v_w_o_swa, w_o_fox=v_w_o_fox, w_o_mem=v_w_o_mem, w_out=v_w_out,
               g_mlp=v_g_mlp, w_mlp_up=v_w_mlp_up, w_mlp_down=v_w_mlp_down)

    shards = _cast_shards({n: wts[n][0] for n in BIG})
    first = _split_start([shards["w_in"]], True, "ag_start_w_in", peers=(1,) + SAME_CORE)
    rest = _split_start([shards[n] for n in BIG[1:]], True, "ag_start_rest", after=first[4])
    full = {}

    def getw(n, after):
        if n == "w_in" and n not in full:
            forwarded = _forward_start(first, after, "ag_forward_w_in")
            full[n] = _w_in_from_shards(_forward_wait(first, forwarded, "ag_wait_w_in"))
        elif n not in full:
            land = _split_wait(rest, BIG[1:].index(n), after, True, "ag_wait_" + n)
            full[n] = land if n == "w_mlp_up" else _gathered_to_full(n, land)
        return full[n]

    exchanges = {}

    def emit(grads_by_name):
        parts = []
        for n, grad in grads_by_name.items():
            if n == "w_in":
                parts.append(_dw_in_to_parts(grad))
            else:
                parts.append(grad if n == "w_mlp_up" else _full_to_parts(n, grad, wts[n].shape[2]))
        started = _split_start(parts, False, "rs_start_" + next(iter(grads_by_name)))
        for w, n in enumerate(grads_by_name):
            exchanges[n] = (started, w)
        return started[4]

    small_p = {n: wts[n] for n in SMALL}
    loss, grad_x, small_g = _local_step(x[0], mem[0], loss_target[0], small_p, getw, emit, (first[4], rest[4]))

    packed = _pack_small(small_g, loss)
    small_gather = _split_start([packed], True, "ag_start_small")

    grads, delta, new_m, new_v = {}, {}, {}, {}

    def update(n, after):
        land = _split_wait(*exchanges[n], after, False, "rs_wait_" + n)
        g, d, m2, v2 = _adamw(land, wts[n][0], mom[n][0], var[n][0], "adamw_" + n)
        grads[n], delta[n], new_m[n], new_v[n] = g[None], d[None], m2[None], v2[None]
        return d

    after = small_gather[4]
    for n in exchanges:
        if n != "w_in":
            after = update(n, after)

    gathered = _split_wait(small_gather, 0, after, True, "ag_wait_small")
    small_out, total = _adamw_small(gathered, small_p, mom, var)
    for name, (g, d, m2, v2) in small_out.items():
        grads[name], delta[name], new_m[name], new_v[name] = g, d, m2, v2
    update("w_in", total)

    return (total[0, 0], grad_x[None], *[grads[n] for n in WEIGHTS], *[delta[n] for n in WEIGHTS],
            *[new_m[n] for n in WEIGHTS], *[new_v[n] for n in WEIGHTS])
```

```python
import math

import jax
import jax.numpy as jnp
from jax import lax
from jax.experimental import pallas as pl
from jax.experimental.pallas import tpu as pltpu

F32 = jnp.float32
BF16 = jnp.bfloat16

D_MODEL = 1024
N_MEM = 256
D_FF = 4096
HEAD = 64
SWA_HEADS = 8
SWA_BLOCK = 128
MEM_HEADS = 4
MEM_HEAD = 128
EPS = 1e-6
NEG = -1e30
REL_BUCKETS = 32
REL_MAX_DIST = 128

ADAM_LR = 0.001
ADAM_B1 = 0.9
ADAM_B2 = 0.999
ADAM_EPS = 1e-08
ADAM_WD = 0.01
ADAM_STEP = 10

GL0, QF0, KF0, VF0, QM0, QA0, KA0, VA0, FL0 = 0, 3072, 3584, 4096, 4608, 5120, 5632, 5760, 5888
PROJ_W = 6144
HALF_W = 3072
H_QF, H_KF, H_VF, H_QM, H_QA, H_KA, H_VA, H_FL = 0, 512, 1024, 1536, 2048, 2560, 2688, 2816

VMEM_LIMIT = 56 * 1024 * 1024
N_DEV = 8
MESH = pl.DeviceIdType.MESH

NN = (((1,), (0,)), ((), ()))
NT = (((1,), (1,)), ((), ()))
TN = (((0,), (0,)), ((), ()))


def _dot(a, b, dims=NN):
    return lax.dot_general(a, b, dims, preferred_element_type=F32)


def _params(sem):
    return pltpu.CompilerParams(dimension_semantics=sem, vmem_limit_bytes=VMEM_LIMIT)


def _full(shape):
    nd = len(shape)
    return pl.BlockSpec(shape, lambda *_: (0,) * nd)


def _sigmoid(z):
    return 1.0 / (1.0 + jnp.exp(-z))


def _group_mean(v, hd):
    if hd == 128:
        return jnp.mean(v, axis=-1, keepdims=True)
    lane = lax.broadcasted_iota(jnp.int32, v.shape, 1)
    lo = lane < HEAD
    s_lo = jnp.sum(jnp.where(lo, v, 0.0), axis=-1, keepdims=True)
    s_hi = jnp.sum(jnp.where(lo, 0.0, v), axis=-1, keepdims=True)
    return jnp.where(lo, s_lo, s_hi) * (1.0 / HEAD)


def _mm(a, b, mode, out_dtype, tm, tn, tk, name, column_chunks=False):
    if mode == "nn":
        m, k = a.shape
        n = b.shape[1]
    elif mode == "nt":
        m, k = a.shape
        n = b.shape[0]
    else:
        k, m = a.shape
        n = b.shape[1]
    tm, tn, tk = min(tm, m), min(tn, n), min(tk, k)
    nk = k // tk
    chunk = n // N_DEV
    per_tile = tn // chunk if column_chunks else 1
    dims = {"nn": NN, "nt": NT, "tn": TN}[mode]
    a_spec = pl.BlockSpec((tk, tm), lambda j, i, kk: (kk, i)) if mode == "tn" else pl.BlockSpec((tm, tk), lambda j, i, kk: (i, kk))
    b_spec = pl.BlockSpec((tn, tk), lambda j, i, kk: (j, kk)) if mode == "nt" else pl.BlockSpec((tk, tn), lambda j, i, kk: (kk, j))

    def body(a_ref, b_ref, o_ref, *acc):
        prod = _dot(a_ref[...].astype(BF16), b_ref[...].astype(BF16), dims)

        def write(res):
            if column_chunks:
                for c in range(per_tile):
                    o_ref[c] = res[:, c * chunk:(c + 1) * chunk].astype(o_ref.dtype)
            else:
                o_ref[...] = res.astype(o_ref.dtype)

        if nk == 1:
            write(prod)
        else:
            acc_ref, = acc
            kk = pl.program_id(2)

            @pl.when(kk == 0)
            def _():
                acc_ref[...] = prod

            @pl.when(kk > 0)
            def _():
                acc_ref[...] += prod

            @pl.when(kk == nk - 1)
            def _():
                write(acc_ref[...])

    return pl.pallas_call(
        body, name=name, grid=(n // tn, m // tm, nk),
        in_specs=[a_spec, b_spec],
        out_specs=(pl.BlockSpec((per_tile, tm, chunk), lambda j, i, kk: (j, i, 0)) if column_chunks
                   else pl.BlockSpec((tm, tn), lambda j, i, kk: (i, j))),
        out_shape=jax.ShapeDtypeStruct((N_DEV, m, chunk) if column_chunks else (m, n), out_dtype),
        scratch_shapes=[pltpu.VMEM((tm, tn), F32)] if nk > 1 else [],
        compiler_params=_params(("parallel", "parallel", "arbitrary")),
    )(a, b)


def _mm_tn3(a_list, b_list, name):
    s, m = a_list[0].shape
    n = b_list[0].shape[1]
    tk = min(2048, s)
    nk = s // tk

    def body(*refs):
        a_refs, b_refs, o_refs, acc_refs = refs[0:3], refs[3:6], refs[6:9], refs[9:12]
        kk = pl.program_id(0)
        for a_ref, b_ref, o_ref, acc_ref in zip(a_refs, b_refs, o_refs, acc_refs):
            prod = _dot(a_ref[...], b_ref[...], TN)
            if nk == 1:
                o_ref[...] = prod.astype(o_ref.dtype)
                continue

            @pl.when(kk == 0)
            def _(acc_ref=acc_ref, prod=prod):
                acc_ref[...] = prod

            @pl.when(kk > 0)
            def _(acc_ref=acc_ref, prod=prod):
                acc_ref[...] += prod

            @pl.when(kk == nk - 1)
            def _(acc_ref=acc_ref, o_ref=o_ref):
                o_ref[...] = acc_ref[...].astype(o_ref.dtype)

    return pl.pallas_call(
        body, name=name, grid=(nk,),
        in_specs=[pl.BlockSpec((tk, m), lambda kk: (kk, 0))] * 3 + [pl.BlockSpec((tk, n), lambda kk: (kk, 0))] * 3,
        out_specs=[_full((m, n))] * 3,
        out_shape=[jax.ShapeDtypeStruct((m, n), BF16)] * 3,
        scratch_shapes=[pltpu.VMEM((m, n), F32)] * 3,
        compiler_params=_params(("arbitrary",)),
    )(*a_list, *b_list)


def _rms_fwd(x, g, name, deps=()):
    s, d = x.shape
    tm = min(512, s)

    def body(x_ref, g_ref, *rest):
        h_ref = rest[len(deps)]
        xv = x_ref[...]
        r = lax.rsqrt(jnp.mean(xv * xv, axis=-1, keepdims=True) + EPS)
        h_ref[...] = (xv * r * g_ref[...]).astype(BF16)

    return pl.pallas_call(
        body, name=name, grid=(s // tm,),
        in_specs=[pl.BlockSpec((tm, d), lambda i: (i, 0)), _full((1, d))] + [pl.BlockSpec(memory_space=pl.ANY)] * len(deps),
        out_specs=pl.BlockSpec((tm, d), lambda i: (i, 0)),
        out_shape=jax.ShapeDtypeStruct((s, d), BF16),
        compiler_params=_params(("parallel",)),
    )(x, g, *deps)


def _proj_post(proj, gq_fox, gk_fox, gq_mem, gq_swa, gk_swa):
    s = proj.shape[0]
    tm = min(256, s)

    def body(p_ref, gqf, gkf, gqm, gqa, gka, qf_ref, kf_ref, vf_ref, qm_ref, qa_ref, ka_ref, va_ref, qft_ref, vft_ref):
        def norm(off, width, hd, g_ref, o_ref, scaled_t_ref=None):
            for b in range(width // 128):
                v = p_ref[:, off + b * 128: off + (b + 1) * 128].astype(F32)
                r = lax.rsqrt(_group_mean(v * v, hd) + EPS)
                vn = (v * r * g_ref[...]).astype(BF16)
                o_ref[:, b * 128:(b + 1) * 128] = vn
                if scaled_t_ref is not None:
                    scaled_t_ref[b * 128:(b + 1) * 128, :] = (vn.astype(F32) * 0.125).T.astype(BF16)

        norm(H_QF, 512, HEAD, gqf, qf_ref, qft_ref)
        norm(H_KF, 512, HEAD, gkf, kf_ref)
        vf_ref[...] = p_ref[:, H_VF:H_VF + 512].astype(BF16)
        for b in range(4):
            vft_ref[b * 128:(b + 1) * 128, :] = p_ref[:, H_VF + b * 128:H_VF + (b + 1) * 128].astype(F32).T.astype(BF16)
        norm(H_QM, 512, MEM_HEAD, gqm, qm_ref)
        norm(H_QA, 512, HEAD, gqa, qa_ref)
        norm(H_KA, 128, HEAD, gka, ka_ref)
        va_ref[...] = p_ref[:, H_VA:H_VA + 128].astype(BF16)

    g_spec = _full((1, 128))
    o512 = pl.BlockSpec((tm, 512), lambda i: (i, 0))
    o128 = pl.BlockSpec((tm, 128), lambda i: (i, 0))
    s512 = jax.ShapeDtypeStruct((s, 512), BF16)
    s128 = jax.ShapeDtypeStruct((s, 128), BF16)
    return pl.pallas_call(
        body, name="proj_post", grid=(s // tm,),
        in_specs=[pl.BlockSpec((tm, HALF_W), lambda i: (i, 1)), g_spec, g_spec, g_spec, g_spec, g_spec],
        out_specs=[o512, o512, o512, o512, o512, o128, o128] + [pl.BlockSpec((512, tm), lambda i: (0, i))] * 2,
        out_shape=[s512, s512, s512, s512, s512, s128, s128] + [jax.ShapeDtypeStruct((512, s), BF16)] * 2,
        compiler_params=_params(("parallel",)),
    )(proj, gq_fox, gk_fox, gq_mem, gq_swa, gk_swa)


def _tri(n, lower):
    r = lax.broadcasted_iota(jnp.int32, (n, n), 0)
    c = lax.broadcasted_iota(jnp.int32, (n, n), 1)
    return jnp.where((c <= r) if lower else (c >= r), 1.0, 0.0).astype(F32)


def _fox_gate_fwd(proj, b_forget128):
    s = proj.shape[0]
    tm = min(512, s)

    def body(p_ref, b_ref, cc_ref, ca_ref, carry_ref):
        i = pl.program_id(0)

        @pl.when(i == 0)
        def _():
            carry_ref[...] = jnp.zeros_like(carry_ref)

        z = p_ref[...] + b_ref[...]
        logf = jnp.minimum(z, 0.0) - jnp.log(1.0 + jnp.exp(-jnp.abs(z)))
        c = jnp.dot(_tri(tm, True), logf, precision=lax.Precision.HIGHEST, preferred_element_type=F32) + carry_ref[...]
        carry_ref[...] = c[tm - 1:tm, :]
        lane = lax.broadcasted_iota(jnp.int32, (tm, 128), 1)
        for hp in range(4):
            cc_ref[hp] = c if hp == 0 else pltpu.roll(c, 128 - 2 * hp, 1)
            aug = jnp.zeros((tm, 128), F32)
            for e in range(2):
                rest = jnp.broadcast_to(c[:, 2 * hp + e:2 * hp + e + 1], (tm, 128))
                for part in range(3):
                    piece = rest.astype(BF16).astype(F32)
                    aug = jnp.where(lane == HEAD * (1 - e) + part, piece, aug)
                    rest = rest - piece
            ca_ref[hp] = aug.astype(BF16)

    return pl.pallas_call(
        body, name="fox_gate_fwd", grid=(s // tm,),
        in_specs=[pl.BlockSpec((tm, 128), lambda i: (i, 0)), _full((1, 128))],
        out_specs=[pl.BlockSpec((4, tm, 128), lambda i: (0, i, 0))] * 2,
        out_shape=[jax.ShapeDtypeStruct((4, s, 128), F32), jax.ShapeDtypeStruct((4, s, 128), BF16)],
        scratch_shapes=[pltpu.VMEM((1, 128), F32)],
        compiler_params=_params(("arbitrary",)),
    )(proj, b_forget128)


def _memkv_fwd(mem, g_mem, w_kv, kn_mem):
    m = mem.shape[0]

    def body(mem_ref, g_ref, w_ref, kn_ref, memn_ref, kv_ref, mk_ref, mv_ref):
        xv = mem_ref[...]
        r = lax.rsqrt(jnp.mean(xv * xv, axis=-1, keepdims=True) + EPS)
        mn = (xv * r * g_ref[...]).astype(BF16)
        memn_ref[...] = mn
        kv = _dot(mn, w_ref[...])
        kv_ref[...] = kv
        for h in range(MEM_HEADS):
            v = kv[:, h * 128:(h + 1) * 128]
            rr = lax.rsqrt(jnp.mean(v * v, axis=-1, keepdims=True) + EPS)
            mk_ref[:, h * 128:(h + 1) * 128] = (v * rr * kn_ref[...]).astype(BF16)
        mv_ref[...] = kv[:, 512:1024].astype(BF16)

    return pl.pallas_call(
        body, name="memkv_fwd",
        out_shape=[jax.ShapeDtypeStruct((m, D_MODEL), BF16), jax.ShapeDtypeStruct((m, 1024), F32),
                   jax.ShapeDtypeStruct((m, 512), BF16), jax.ShapeDtypeStruct((m, 512), BF16)],
        compiler_params=pltpu.CompilerParams(vmem_limit_bytes=VMEM_LIMIT),
    )(mem, g_mem, w_kv, kn_mem)


def _bias_table(rel_bias, bucket):
    def body(rb_ref, bk_ref, o_ref):
        bk = bk_ref[...]
        for h in range(SWA_HEADS):
            acc = jnp.zeros(bk.shape, F32)
            for b in range(REL_BUCKETS):
                acc = jnp.where(bk == b, rb_ref[b, h], acc)
            o_ref[h] = acc

    return pl.pallas_call(
        body, name="bias_table",
        in_specs=[pl.BlockSpec(memory_space=pltpu.SMEM), pl.BlockSpec(memory_space=pltpu.VMEM)],
        out_shape=jax.ShapeDtypeStruct((SWA_HEADS,) + bucket.shape, F32),
    )(rel_bias, bucket)


def _swa_valid(n):
    row = lax.broadcasted_iota(jnp.int32, (SWA_BLOCK, 2 * SWA_BLOCK), 0)
    col = lax.broadcasted_iota(jnp.int32, (SWA_BLOCK, 2 * SWA_BLOCK), 1)
    dist = row + SWA_BLOCK - col
    return (dist >= 0) & (dist < SWA_BLOCK) & ((col >= SWA_BLOCK) | (n > 0))


def _swa_fwd(qa, kp, vp, bias, sink):
    s = qa.shape[0]
    nb = s // SWA_BLOCK

    def body(sink_ref, q_ref, kp_ref, vp_ref, bias_ref, o_ref):
        n = pl.program_id(0)
        start = pl.multiple_of(n * SWA_BLOCK, SWA_BLOCK)
        k2 = kp_ref[pl.ds(start, 2 * SWA_BLOCK), :]
        v2 = vp_ref[pl.ds(start, 2 * SWA_BLOCK), :]
        valid = _swa_valid(n)
        heads = range(SWA_HEADS)
        hs = lambda h: slice(h * HEAD, (h + 1) * HEAD)
        sc = [jnp.where(valid, _dot(q_ref[:, hs(h)], k2[:, hs(h // 4)], NT) * 0.125 + bias_ref[h], NEG) for h in heads]
        pn = []
        for h in heads:
            sk = sink_ref[h]
            mx = jnp.maximum(jnp.max(sc[h], axis=-1, keepdims=True), sk)
            p = jnp.exp(sc[h] - mx)
            den = jnp.sum(p, axis=-1, keepdims=True) + jnp.exp(sk - mx)
            pn.append((p / den).astype(BF16))
        outs = [_dot(pn[h], v2[:, hs(h // 4)]).astype(BF16) for h in heads]
        for h in heads:
            o_ref[:, hs(h)] = outs[h]

    return pl.pallas_call(
        body, name="swa_fwd", grid=(nb,),
        in_specs=[pl.BlockSpec(memory_space=pltpu.SMEM),
                  pl.BlockSpec((SWA_BLOCK, 512), lambda n: (n, 0)),
                  _full(kp.shape), _full(vp.shape), _full(bias.shape)],
        out_specs=pl.BlockSpec((SWA_BLOCK, 512), lambda n: (n, 0)),
        out_shape=jax.ShapeDtypeStruct((s, 512), BF16),
        compiler_params=_params(("parallel",)),
    )(sink, qa, kp, vp, bias)


def _head_mask(e):
    lane = lax.broadcasted_iota(jnp.int32, (1, 128), 1)
    return (lane >= e * HEAD) & (lane < (e + 1) * HEAD)


FOX_FWD_T = 1024
FOX_BWD_T = 512


def _head_rows(e):
    row = lax.broadcasted_iota(jnp.int32, (128, 1), 0)
    return (row >= e * HEAD) & (row < (e + 1) * HEAD)


def _fox_fwd(q, k, v_t, ca4):
    s = q.shape[0]
    t = min(FOX_FWD_T, s)
    nq = s // t

    def body(q_ref, k_ref, vt_ref, ca_ref, o_ref, lse_ref, ot_ref):
        i = pl.program_id(1)
        qs = q_ref[...] * jnp.asarray(0.125, BF16)
        lane = lax.broadcasted_iota(jnp.int32, (1, 128), 1)
        minus = [jnp.where((lane >= HEAD * (1 - e)) & (lane < HEAD * (1 - e) + 3), -1.0, 0.0).astype(BF16) for e in range(2)]
        qe = [jnp.where(_head_mask(e), qs, jnp.broadcast_to(minus[e], qs.shape)) for e in range(2)]

        def block(carry, key0, nkeys, q0, nqs, masked):
            ks = pl.ds(pl.multiple_of(key0, 128), nkeys)
            kj = k_ref[ks, :]
            caj = ca_ref[0, ks, :]
            vtj = vt_ref[:, ks]
            out = []
            for e in range(2):
                m_all, acc_all = carry[2 * e], carry[2 * e + 1]
                m, acc = m_all[:, q0:q0 + nqs], acc_all[:, q0:q0 + nqs]
                st = _dot(jnp.where(_head_mask(e), kj, caj), qe[e][q0:q0 + nqs, :], NT)
                if masked:
                    krow = lax.broadcasted_iota(jnp.int32, (nkeys, nqs), 0) + key0
                    qcol = lax.broadcasted_iota(jnp.int32, (nkeys, nqs), 1) + (i * t + q0)
                    st = jnp.where(krow <= qcol, st, NEG)
                m_new = jnp.maximum(m, jnp.max(st, axis=0, keepdims=True))
                alpha = jnp.exp(m - m_new)
                pt = jnp.exp(st - m_new).astype(BF16)
                vte = jnp.where(_head_rows(e), vtj, jnp.ones_like(vtj))
                acc_new = alpha * acc + _dot(vte, pt)
                if nqs < t:
                    m_new = jnp.concatenate([m_all[:, :q0], m_new], axis=1)
                    acc_new = jnp.concatenate([acc_all[:, :q0], acc_new], axis=1)
                out += [m_new, acc_new]
            return tuple(out)

        half = t // 2
        init = (jnp.full((1, t), NEG, F32), jnp.zeros((128, t), F32)) * 2
        carry = lax.fori_loop(0, i, lambda j, c: block(c, j * t, t, 0, t, False), init)
        carry = block(carry, i * t, half, 0, t, True)
        m0, a0, m1, a1 = block(carry, i * t + half, half, half, half, True)
        l0 = a0[HEAD:HEAD + 1, :]
        l1 = a1[0:1, :]
        o_t = jnp.where(_head_rows(0), a0 / l0, a1 / l1)
        o_ref[...] = o_t.T.astype(BF16)
        ot_ref[...] = o_t.astype(BF16)
        r8 = lax.broadcasted_iota(jnp.int32, (8, t), 0)
        lse_ref[0] = jnp.where(r8 == 0, m0 + jnp.log(l0), jnp.where(r8 == 1, m1 + jnp.log(l1), 0.0))

    return pl.pallas_call(
        body, name="fox_fwd", grid=(4, nq),
        in_specs=[pl.BlockSpec((t, 128), lambda hp, i: (i, hp)),
                  pl.BlockSpec((s, 128), lambda hp, i: (0, hp)),
                  pl.BlockSpec((128, s), lambda hp, i: (hp, 0)),
                  pl.BlockSpec((1, s, 128), lambda hp, i: (hp, 0, 0))],
        out_specs=[pl.BlockSpec((t, 128), lambda hp, i: (i, hp)),
                   pl.BlockSpec((1, 8, t), lambda hp, i: (hp, 0, i)),
                   pl.BlockSpec((128, t), lambda hp, i: (hp, i))],
        out_shape=[jax.ShapeDtypeStruct((s, 512), BF16), jax.ShapeDtypeStruct((4, 8, s), F32),
                   jax.ShapeDtypeStruct((512, s), BF16)],
        compiler_params=_params(("parallel", "parallel")),
    )(q, k, v_t, ca4)


MEM_SCALE = MEM_HEAD ** -0.5


def _mem_fwd(qm, mk, mv):
    s = qm.shape[0]
    tq = min(512, s)

    def body(q_ref, mk_ref, mv_ref, o_ref):
        for h in range(MEM_HEADS):
            hs = slice(h * 128, (h + 1) * 128)
            sc = _dot(q_ref[:, hs], mk_ref[:, hs], NT) * MEM_SCALE
            mx = jnp.max(sc, axis=-1, keepdims=True)
            p = jnp.exp(sc - mx)
            p = p / jnp.sum(p, axis=-1, keepdims=True)
            o_ref[:, hs] = _dot(p.astype(BF16), mv_ref[:, hs]).astype(BF16)

    return pl.pallas_call(
        body, name="mem_fwd", grid=(s // tq,),
        in_specs=[pl.BlockSpec((tq, 512), lambda i: (i, 0)), _full(mk.shape), _full(mv.shape)],
        out_specs=pl.BlockSpec((tq, 512), lambda i: (i, 0)),
        out_shape=jax.ShapeDtypeStruct((s, 512), BF16),
        compiler_params=_params(("parallel",)),
    )(qm, mk, mv)


def _merge_fwd(x, oa, of, om, proj, b_gate, wa, wf, wm, w_out, g_mlp):
    s = x.shape[0]
    tm = min(256, s)

    def body(x_ref, oa_ref, of_ref, om_ref, gl_ref, bg_ref, wa_ref, wf_ref, wm_ref, wo_ref, g_ref, x1_ref, hm_ref, mg_ref):
        merged = None
        for b, (o_ref, w_ref) in enumerate(((oa_ref, wa_ref), (of_ref, wf_ref), (om_ref, wm_ref))):
            cs = slice(b * D_MODEL, (b + 1) * D_MODEL)
            y = _dot(o_ref[...], w_ref[...])
            t = _sigmoid(gl_ref[:, cs].astype(F32) + bg_ref[:, cs]) * y
            merged = t if merged is None else merged + t
        mb = merged.astype(BF16)
        mg_ref[...] = mb
        x1 = x_ref[...] + _dot(mb, wo_ref[...])
        x1_ref[...] = x1
        r = lax.rsqrt(jnp.mean(x1 * x1, axis=-1, keepdims=True) + EPS)
        hm_ref[...] = (x1 * r * g_ref[...]).astype(BF16)

    row = lambda w: pl.BlockSpec((tm, w), lambda i: (i, 0))
    return pl.pallas_call(
        body, name="merge_fwd", grid=(s // tm,),
        in_specs=[row(D_MODEL), row(512), row(512), row(512), row(HALF_W), _full((1, HALF_W)),
                  _full(wa.shape), _full(wf.shape), _full(wm.shape), _full(w_out.shape), _full((1, D_MODEL))],
        out_specs=[row(D_MODEL), row(D_MODEL), row(D_MODEL)],
        out_shape=[jax.ShapeDtypeStruct((s, D_MODEL), F32), jax.ShapeDtypeStruct((s, D_MODEL), BF16),
                   jax.ShapeDtypeStruct((s, D_MODEL), BF16)],
        compiler_params=_params(("parallel",)),
    )(x, oa, of, om, proj, b_gate, wa, wf, wm, w_out, g_mlp)


def _mlp_up(hm, w_up):
    s = hm.shape[0]
    tm, tn = min(1024, s), w_up.shape[2]

    def body(h_ref, w_ref, u_ref):
        r = jnp.maximum(_dot(h_ref[...], w_ref[0]), 0.0)
        u_ref[...] = (r * r).astype(BF16)

    return pl.pallas_call(
        body, name="mlp_up", grid=(s // tm, D_FF // tn),
        in_specs=[pl.BlockSpec((tm, D_MODEL), lambda i, j: (i, 0)), pl.BlockSpec((1, D_MODEL, tn), lambda i, j: (j, 0, 0))],
        out_specs=pl.BlockSpec((tm, tn), lambda i, j: (i, j)),
        out_shape=jax.ShapeDtypeStruct((s, D_FF), BF16),
        compiler_params=_params(("parallel", "parallel")),
    )(hm, w_up)


def _mlp_down_loss(u, w_down, x1, target):
    s = u.shape[0]
    tm = min(256, s)

    def body(u_ref, w_ref, x1_ref, t_ref, dy_ref, dyb_ref, loss_ref):
        i = pl.program_id(0)

        @pl.when(i == 0)
        def _():
            loss_ref[...] = jnp.zeros_like(loss_ref)

        y = x1_ref[...] + _dot(u_ref[...], w_ref[...])
        err = y - t_ref[...]
        dy = err * (1.0 / D_MODEL)
        dy_ref[...] = dy
        dyb_ref[...] = dy.astype(BF16)
        part = jnp.sum(jnp.sum(err * err, axis=-1, keepdims=True) * (1.0 / D_MODEL), axis=0, keepdims=True)
        loss_ref[...] += 0.5 * part

    row = pl.BlockSpec((tm, D_MODEL), lambda i: (i, 0))
    return pl.pallas_call(
        body, name="mlp_down_loss", grid=(s // tm,),
        in_specs=[pl.BlockSpec((tm, D_FF), lambda i: (i, 0)), _full(w_down.shape), row, row],
        out_specs=[row, row, _full((1, 1))],
        out_shape=[jax.ShapeDtypeStruct((s, D_MODEL), F32), jax.ShapeDtypeStruct((s, D_MODEL), BF16),
                   jax.ShapeDtypeStruct((1, 1), F32)],
        compiler_params=_params(("arbitrary",)),
    )(u, w_down, x1, target)


def _mlp_bwd_act(dy, w_down, u):
    s = dy.shape[0]
    tm, tn = min(1024, s), 1024

    def body(dy_ref, w_ref, u_ref, da_ref):
        du = _dot(dy_ref[...], w_ref[...], NT)
        da_ref[...] = (du * (2.0 * jnp.sqrt(u_ref[...].astype(F32)))).astype(BF16)

    return pl.pallas_call(
        body, name="mlp_bwd_act", grid=(D_FF // tn, s // tm),
        in_specs=[pl.BlockSpec((tm, D_MODEL), lambda j, i: (i, 0)), pl.BlockSpec((tn, D_MODEL), lambda j, i: (j, 0)),
                  pl.BlockSpec((tm, tn), lambda j, i: (i, j))],
        out_specs=pl.BlockSpec((tm, tn), lambda j, i: (i, j)),
        out_shape=jax.ShapeDtypeStruct((s, D_FF), BF16),
        compiler_params=_params(("parallel", "parallel")),
    )(dy, w_down, u)


def _rms_bwd(xv, g, dh, skip):
    r = lax.rsqrt(jnp.mean(xv * xv, axis=-1, keepdims=True) + EPS)
    n = xv * r
    dn = dh * g
    dx = skip + r * (dn - n * jnp.mean(dn * n, axis=-1, keepdims=True))
    return dx, jnp.sum(dh * n, axis=0, keepdims=True)


def _mlp_bwd_x(da, w_up, x1, dy, g_mlp):
    s = da.shape[0]
    tm = min(256, s)

    def body(da_ref, w_ref, x1_ref, dy_ref, g_ref, dx1_ref, dg_ref):
        i = pl.program_id(0)

        @pl.when(i == 0)
        def _():
            dg_ref[...] = jnp.zeros_like(dg_ref)

        tn = w_ref.shape[2]
        dhm = _dot(da_ref[:, 0:tn], w_ref[0], NT)
        for j in range(1, N_DEV):
            dhm = dhm + _dot(da_ref[:, j * tn:(j + 1) * tn], w_ref[j], NT)
        dx, dg = _rms_bwd(x1_ref[...], g_ref[...], dhm, dy_ref[...])
        dx1_ref[...] = dx
        dg_ref[...] += dg

    row = pl.BlockSpec((tm, D_MODEL), lambda i: (i, 0))
    return pl.pallas_call(
        body, name="mlp_bwd_x", grid=(s // tm,),
        in_specs=[pl.BlockSpec((tm, D_FF), lambda i: (i, 0)), _full(w_up.shape), row, row, _full((1, D_MODEL))],
        out_specs=[row, _full((1, D_MODEL))],
        out_shape=[jax.ShapeDtypeStruct((s, D_MODEL), F32), jax.ShapeDtypeStruct((1, D_MODEL), F32)],
        compiler_params=_params(("arbitrary",)),
    )(da, w_up, x1, dy, g_mlp)


def _merge_bwd(dx1, oa, of, om, proj, b_gate, wa, wf, wm, w_out):
    s = dx1.shape[0]
    tm = min(256, s)

    def body(dx1_ref, oa_ref, of_ref, om_ref, gl_ref, bg_ref, wa_ref, wf_ref, wm_ref, wo_ref,
             dp_ref, doa_ref, dof_ref, dom_ref, dya_ref, dyf_ref, dym_ref, dbg_ref):
        i = pl.program_id(0)

        @pl.when(i == 0)
        def _():
            dbg_ref[...] = jnp.zeros_like(dbg_ref)

        dmerged = _dot(dx1_ref[...].astype(BF16), wo_ref[...], NT)
        branches = ((oa_ref, wa_ref, doa_ref, dya_ref), (of_ref, wf_ref, dof_ref, dyf_ref), (om_ref, wm_ref, dom_ref, dym_ref))
        for b, (o_ref, w_ref, do_ref, dyb_ref) in enumerate(branches):
            cs = slice(b * D_MODEL, (b + 1) * D_MODEL)
            y = _dot(o_ref[...], w_ref[...])
            g = _sigmoid(gl_ref[:, cs].astype(F32) + bg_ref[:, cs])
            dz = (dmerged * y) * g * (1.0 - g)
            dp_ref[:, cs] = dz.astype(BF16)
            dbg_ref[:, cs] += jnp.sum(dz, axis=0, keepdims=True)
            dyb = (dmerged * g).astype(BF16)
            dyb_ref[...] = dyb
            do = _dot(dyb, w_ref[...], NT)
            do_ref[...] = (do.T if b == 1 else do).astype(BF16)

    row = lambda w: pl.BlockSpec((tm, w), lambda i: (i, 0))
    sd = lambda w: jax.ShapeDtypeStruct((s, w), BF16)
    return pl.pallas_call(
        body, name="merge_bwd", grid=(s // tm,),
        in_specs=[row(D_MODEL), row(512), row(512), row(512), row(HALF_W), _full((1, HALF_W)),
                  _full(wa.shape), _full(wf.shape), _full(wm.shape), _full(w_out.shape)],
        out_specs=[row(HALF_W), row(512), pl.BlockSpec((512, tm), lambda i: (0, i)), row(512),
                   row(D_MODEL), row(D_MODEL), row(D_MODEL), _full((1, HALF_W))],
        out_shape=[sd(PROJ_W), sd(512), jax.ShapeDtypeStruct((512, s), BF16), sd(512), sd(D_MODEL), sd(D_MODEL), sd(D_MODEL),
                   jax.ShapeDtypeStruct((1, HALF_W), F32)],
        compiler_params=_params(("arbitrary",)),
    )(dx1, oa, of, om, proj, b_gate, wa, wf, wm, w_out)


def _swa_valid_t(n):
    key = lax.broadcasted_iota(jnp.int32, (2 * SWA_BLOCK, SWA_BLOCK), 0)
    qry = lax.broadcasted_iota(jnp.int32, (2 * SWA_BLOCK, SWA_BLOCK), 1)
    dist = qry + SWA_BLOCK - key
    return (dist >= 0) & (dist < SWA_BLOCK) & ((key >= SWA_BLOCK) | (n > 0))


def _swa_bwd(qa, kp, vp, bias_t, sink, doa):
    s = qa.shape[0]
    nb = s // SWA_BLOCK

    def body(sink_ref, q_ref, kp_ref, vp_ref, bias_ref, do_ref, dq_ref, dkp_ref, dvp_ref, dbias_ref, dsink_ref, sk_acc):
        n = pl.program_id(0)

        @pl.when(n == 0)
        def _():
            dkp_ref[...] = jnp.zeros_like(dkp_ref)
            dvp_ref[...] = jnp.zeros_like(dvp_ref)
            dbias_ref[...] = jnp.zeros_like(dbias_ref)
            sk_acc[...] = jnp.zeros_like(sk_acc)

        start = pl.multiple_of(n * SWA_BLOCK, SWA_BLOCK)
        win = pl.ds(start, 2 * SWA_BLOCK)
        k2 = kp_ref[win, :]
        v2 = vp_ref[win, :]
        valid = _swa_valid_t(n)
        heads = range(SWA_HEADS)
        hs = lambda h: slice(h * HEAD, (h + 1) * HEAD)
        scale = jnp.asarray(0.125, BF16)
        q = [q_ref[:, hs(h)] for h in heads]
        do = [do_ref[:, hs(h)] for h in heads]
        kk = [k2[:, hs(kv)] for kv in range(2)]
        vv = [v2[:, hs(kv)] for kv in range(2)]
        kt = [(kk[kv].astype(F32) * 0.125).T.astype(BF16) for kv in range(2)]
        st = [jnp.where(valid, _dot(kk[h // 4], q[h], NT) * 0.125 + bias_ref[h], NEG) for h in heads]
        dpt = [_dot(vv[h // 4], do[h], NT) for h in heads]
        pt, dst = [], []
        for h in heads:
            sk = sink_ref[h]
            mx = jnp.maximum(jnp.max(st[h], axis=0, keepdims=True), sk)
            p = jnp.exp(st[h] - mx)
            esk = jnp.exp(sk - mx)
            den = jnp.sum(p, axis=0, keepdims=True) + esk
            p = p / den
            delta = jnp.sum(p * dpt[h], axis=0, keepdims=True)
            d = p * (dpt[h] - delta)
            sk_acc[h:h + 1, :] += -(esk / den) * delta
            dbias_ref[h] += d
            pt.append(p.astype(BF16))
            dst.append(d.astype(BF16))
        dq_t = [_dot(kt[h // 4], dst[h]) for h in heads]
        dq_ref[...] = jnp.concatenate(dq_t, axis=0).T.astype(BF16)
        for kv in range(2):
            group = range(4 * kv, 4 * kv + 4)
            dk = [_dot(dst[h], q[h] * scale) for h in group]
            dv = [_dot(pt[h], do[h]) for h in group]
            dkp_ref[win, hs(kv)] += (dk[0] + dk[1]) + (dk[2] + dk[3])
            dvp_ref[win, hs(kv)] += (dv[0] + dv[1]) + (dv[2] + dv[3])

        @pl.when(n == nb - 1)
        def _():
            dsink_ref[...] = jnp.broadcast_to(jnp.sum(sk_acc[...], axis=1, keepdims=True), dsink_ref.shape)

    return pl.pallas_call(
        body, name="swa_bwd", grid=(nb,),
        in_specs=[pl.BlockSpec(memory_space=pltpu.SMEM),
                  pl.BlockSpec((SWA_BLOCK, 512), lambda n: (n, 0)),
                  _full(kp.shape), _full(vp.shape), _full(bias_t.shape),
                  pl.BlockSpec((SWA_BLOCK, 512), lambda n: (n, 0))],
        out_specs=[pl.BlockSpec((SWA_BLOCK, 512), lambda n: (n, 0)), _full(kp.shape), _full(vp.shape),
                   _full(bias_t.shape), _full((SWA_HEADS, 128))],
        out_shape=[jax.ShapeDtypeStruct((s, 512), BF16), jax.ShapeDtypeStruct(kp.shape, F32),
                   jax.ShapeDtypeStruct(vp.shape, F32), jax.ShapeDtypeStruct(bias_t.shape, F32),
                   jax.ShapeDtypeStruct((SWA_HEADS, 128), F32)],
        scratch_shapes=[pltpu.VMEM((SWA_HEADS, 128), F32)],
        compiler_params=_params(("arbitrary",)),
    )(sink, qa, kp, vp, bias_t, doa)


def _fox_bwd(qt, k, v, dot, ot, cc4, lse4):
    s = k.shape[0]
    t = min(FOX_BWD_T, s)
    nq = s // t

    def body(qt_ref, k_ref, v_ref, dot_ref, ot_ref, cc_ref, lse_ref,
             dqt_ref, dk_ref, dv_ref, dck_ref, dcq_ref, delta_ref, dkt_acc, dvt_acc, ds0, ds1):
        j = pl.program_id(1)

        @pl.when(j == 0)
        def _():
            dqt_ref[...] = jnp.zeros_like(dqt_ref)
            dcq_ref[...] = jnp.zeros_like(dcq_ref)
            r8 = lax.broadcasted_iota(jnp.int32, (8, t), 0)

            def dl(i, c):
                cols = pl.ds(pl.multiple_of(i * t, t), t)
                pr = dot_ref[:, cols].astype(F32) * ot_ref[:, cols].astype(F32)
                d0 = jnp.sum(jnp.where(_head_rows(0), pr, 0.0), axis=0, keepdims=True)
                d1 = jnp.sum(jnp.where(_head_rows(1), pr, 0.0), axis=0, keepdims=True)
                delta_ref[:, cols] = jnp.where(r8 == 0, d0, jnp.where(r8 == 1, d1, 0.0))
                return c

            lax.fori_loop(0, nq, dl, 0)

        kj = k_ref[...]
        vj = v_ref[...]
        ks = pl.ds(pl.multiple_of(j * t, t), t)
        kt = (kj.astype(F32) * 0.125).T.astype(BF16)
        ke = [jnp.where(_head_mask(e), kj, jnp.zeros_like(kj)) for e in range(2)]
        ve = [jnp.where(_head_mask(e), vj, jnp.zeros_like(vj)) for e in range(2)]
        ck = [cc_ref[0, ks, e:e + 1] for e in range(2)]
        for r in (dkt_acc, dvt_acc, ds0, ds1):
            r[...] = jnp.zeros_like(r)

        def block(q0, nqs, k0, nks, masked):
            cols = pl.ds(pl.multiple_of(q0, 128), nqs)
            rows = slice(k0, k0 + nks)
            qti = qt_ref[:, cols]
            doti = dot_ref[:, cols]
            for e, ds_acc in enumerate((ds0, ds1)):
                dims = slice(e * HEAD, (e + 1) * HEAD)
                st = _dot(ke[e][rows, :], qti) - ck[e][rows, :]
                if masked:
                    krow = lax.broadcasted_iota(jnp.int32, (nks, nqs), 0) + (j * t + k0)
                    qcol = lax.broadcasted_iota(jnp.int32, (nks, nqs), 1) + q0
                    st = jnp.where(krow <= qcol, st, NEG)
                pt = jnp.exp(st - lse_ref[0, e:e + 1, cols])
                dpt = _dot(ve[e][rows, :], doti)
                dst = pt * (dpt - delta_ref[e:e + 1, cols])
                dsb = dst.astype(BF16)
                dvt_acc[dims, rows] += _dot(doti[dims, :], pt.astype(BF16), NT)
                dkt_acc[dims, rows] += _dot(qti[dims, :], dsb, NT)
                dqt_ref[dims, cols] += _dot(kt[dims, rows], dsb)
                ds_acc[rows, 0:nqs] += dst
                dcq_ref[0, e:e + 1, cols] += jnp.sum(dst, axis=0, keepdims=True)

        half = t // 2
        block(j * t, half, 0, half, True)
        block(j * t + half, half, 0, t, True)

        def rest(i, c):
            block(i * t, t, 0, t, False)
            return c

        lax.fori_loop(j + 1, nq, rest, 0)
        dk_ref[...] = dkt_acc[...].T.astype(BF16)
        dv_ref[...] = dvt_acc[...].T.astype(BF16)
        lane = lax.broadcasted_iota(jnp.int32, (t, 128), 1)
        c0 = jnp.sum(ds0[...], axis=-1, keepdims=True)
        c1 = jnp.sum(ds1[...], axis=-1, keepdims=True)
        dck_ref[0] = jnp.where(lane == 0, c0, jnp.where(lane == 1, c1, 0.0))

    res_t = lambda: pl.BlockSpec((128, s), lambda hp, j: (hp, 0))
    blk = lambda: pl.BlockSpec((t, 128), lambda hp, j: (j, hp))
    return pl.pallas_call(
        body, name="fox_bwd", grid=(4, nq),
        in_specs=[res_t(), blk(), blk(), res_t(), res_t(), pl.BlockSpec((1, s, 128), lambda hp, j: (hp, 0, 0)),
                  pl.BlockSpec((1, 8, s), lambda hp, j: (hp, 0, 0))],
        out_specs=[res_t(), blk(), blk(),
                   pl.BlockSpec((1, t, 128), lambda hp, j: (hp, j, 0)),
                   pl.BlockSpec((1, 8, s), lambda hp, j: (hp, 0, 0))],
        out_shape=[jax.ShapeDtypeStruct((512, s), F32), jax.ShapeDtypeStruct((s, 512), BF16),
                   jax.ShapeDtypeStruct((s, 512), BF16), jax.ShapeDtypeStruct((4, s, 128), F32),
                   jax.ShapeDtypeStruct((4, 8, s), F32)],
        scratch_shapes=[pltpu.VMEM((8, s), F32)] + [pltpu.VMEM((128, t), F32)] * 2 + [pltpu.VMEM((t, t), F32)] * 2,
        compiler_params=_params(("arbitrary", "arbitrary")),
    )(qt, k, v, dot, ot, cc4, lse4)


def _mem_bwd(qm, mk, mv, dom):
    s = qm.shape[0]
    tq = min(512, s)

    def body(q_ref, mk_ref, mv_ref, do_ref, dq_ref, dmk_ref, dmv_ref):
        i = pl.program_id(0)

        @pl.when(i == 0)
        def _():
            dmk_ref[...] = jnp.zeros_like(dmk_ref)
            dmv_ref[...] = jnp.zeros_like(dmv_ref)

        heads = range(MEM_HEADS)
        hs = lambda h: slice(h * 128, (h + 1) * 128)
        sc = [_dot(q_ref[:, hs(h)], mk_ref[:, hs(h)], NT) * MEM_SCALE for h in heads]
        dp = [_dot(do_ref[:, hs(h)], mv_ref[:, hs(h)], NT) for h in heads]
        pb, dsb = [], []
        for h in heads:
            p = jnp.exp(sc[h] - jnp.max(sc[h], axis=-1, keepdims=True))
            p = p / jnp.sum(p, axis=-1, keepdims=True)
            ds = p * (dp[h] - jnp.sum(p * dp[h], axis=-1, keepdims=True))
            pb.append(p.astype(BF16))
            dsb.append((ds * MEM_SCALE).astype(BF16))
        dq = [_dot(dsb[h], mk_ref[:, hs(h)]).astype(BF16) for h in heads]
        dmk = [_dot(dsb[h], q_ref[:, hs(h)], TN) for h in heads]
        dmv = [_dot(pb[h], do_ref[:, hs(h)], TN) for h in heads]
        for h in heads:
            dq_ref[:, hs(h)] = dq[h]
            dmk_ref[:, hs(h)] += dmk[h]
            dmv_ref[:, hs(h)] += dmv[h]

    return pl.pallas_call(
        body, name="mem_bwd", grid=(s // tq,),
        in_specs=[pl.BlockSpec((tq, 512), lambda i: (i, 0)), _full(mk.shape), _full(mv.shape),
                  pl.BlockSpec((tq, 512), lambda i: (i, 0))],
        out_specs=[pl.BlockSpec((tq, 512), lambda i: (i, 0)), _full(mk.shape), _full(mv.shape)],
        out_shape=[jax.ShapeDtypeStruct((s, 512), BF16), jax.ShapeDtypeStruct(mk.shape, F32),
                   jax.ShapeDtypeStruct(mv.shape, F32)],
        compiler_params=_params(("arbitrary",)),
    )(qm, mk, mv, dom)


def _memkv_bwd(dmk, dmv, kv_raw, kn_mem, mem, g_mem, mem_n, w_kv):
    def body(dmk_ref, dmv_ref, kv_ref, kn_ref, mem_ref, g_ref, mn_ref, w_ref, dw_ref, dkn_ref, dg_ref, dkv_ref):
        dkn = jnp.zeros((1, 128), F32)
        for h in range(MEM_HEADS):
            hs = slice(h * 128, (h + 1) * 128)
            v = kv_ref[:, hs]
            r = lax.rsqrt(jnp.mean(v * v, axis=-1, keepdims=True) + EPS)
            n = v * r
            dn = dmk_ref[:, hs]
            dkn = dkn + jnp.sum(dn * n, axis=0, keepdims=True)
            dng = dn * kn_ref[...]
            dkv_ref[:, hs] = (r * (dng - n * jnp.mean(dng * n, axis=-1, keepdims=True))).astype(BF16)
        dkv_ref[:, 512:1024] = dmv_ref[...].astype(BF16)
        dkn_ref[...] = dkn
        dkv = dkv_ref[...]
        dw_ref[...] = _dot(mn_ref[...], dkv, TN).astype(BF16)
        dmn = _dot(dkv, w_ref[...], NT)
        xv = mem_ref[...]
        r = lax.rsqrt(jnp.mean(xv * xv, axis=-1, keepdims=True) + EPS)
        dg_ref[...] = jnp.sum(dmn * (xv * r), axis=0, keepdims=True)

    m = mem.shape[0]
    return pl.pallas_call(
        body, name="memkv_bwd",
        out_shape=[jax.ShapeDtypeStruct((D_MODEL, 1024), BF16), jax.ShapeDtypeStruct((1, 128), F32),
                   jax.ShapeDtypeStruct((1, D_MODEL), F32)],
        scratch_shapes=[pltpu.VMEM((m, 1024), BF16)],
        compiler_params=pltpu.CompilerParams(vmem_limit_bytes=VMEM_LIMIT),
    )(dmk, dmv, kv_raw, kn_mem, mem, g_mem, mem_n, w_kv)


def _fox_gate_bwd(dcq4, dck4, proj, b_forget128):
    s = dck4.shape[1]
    tm = min(512, s)
    nt = s // tm

    def body(dcq_ref, dck_ref, p_ref, b_ref, dfl_ref, db_ref, carry_ref):
        i = pl.program_id(0)

        @pl.when(i == 0)
        def _():
            carry_ref[...] = jnp.zeros_like(carry_ref)
            db_ref[...] = jnp.zeros_like(db_ref)

        dcv = jnp.zeros((tm, 128), F32)
        for hp in range(4):
            by_query = jnp.concatenate([dcq_ref[hp], jnp.zeros((120, tm), F32)], axis=0).T
            d = by_query - dck_ref[hp]
            dcv = dcv + (d if hp == 0 else pltpu.roll(d, 2 * hp, 1))
        dlogf = jnp.dot(_tri(tm, False), dcv, precision=lax.Precision.HIGHEST, preferred_element_type=F32) + carry_ref[...]
        carry_ref[...] += jnp.sum(dcv, axis=0, keepdims=True)
        z = p_ref[...] + b_ref[...]
        dfl = dlogf * (1.0 / (1.0 + jnp.exp(z)))
        dfl_ref[...] = dfl.astype(BF16)
        db_ref[...] += jnp.sum(dfl, axis=0, keepdims=True)

    return pl.pallas_call(
        body, name="fox_gate_bwd", grid=(nt,),
        in_specs=[pl.BlockSpec((4, 8, tm), lambda i: (0, 0, nt - 1 - i)),
                  pl.BlockSpec((4, tm, 128), lambda i: (0, nt - 1 - i, 0)),
                  pl.BlockSpec((tm, 128), lambda i: (nt - 1 - i, 0)), _full((1, 128))],
        out_specs=[pl.BlockSpec((tm, 128), lambda i: (nt - 1 - i, 0)), _full((1, 128))],
        out_shape=[jax.ShapeDtypeStruct((s, 128), BF16), jax.ShapeDtypeStruct((1, 128), F32)],
        scratch_shapes=[pltpu.VMEM((1, 128), F32)],
        compiler_params=_params(("arbitrary",)),
    )(dcq4, dck4, proj, b_forget128)


def _proj_pre_bwd(dproj, proj, dqf, dkf, dvf, dqm, dqa, dka, dva, dfl, gq_fox, gk_fox, gq_mem, gq_swa, gk_swa):
    s = proj.shape[0]
    tm = min(256, s)

    def body(dp_in, p_ref, dqf_ref, dkf_ref, dvf_ref, dqm_ref, dqa_ref, dka_ref, dva_ref, dfl_ref,
             gqf, gkf, gqm, gqa, gka, dp_ref, dgn_ref):
        i = pl.program_id(0)

        @pl.when(i == 0)
        def _():
            dgn_ref[...] = jnp.zeros_like(dgn_ref)

        def norm_bwd(off, width, hd, g_ref, dn_ref, slot):
            acc = jnp.zeros((1, 128), F32)
            for b in range(width // 128):
                v = p_ref[:, off + b * 128: off + (b + 1) * 128].astype(F32)
                r = lax.rsqrt(_group_mean(v * v, hd) + EPS)
                n = v * r
                dn = dn_ref[b * 128:(b + 1) * 128, :].T if slot == 0 else dn_ref[:, b * 128:(b + 1) * 128].astype(F32)
                acc = acc + jnp.sum(dn * n, axis=0, keepdims=True)
                dng = dn * g_ref[...]
                dp_ref[:, off + b * 128: off + (b + 1) * 128] = (r * (dng - n * _group_mean(dng * n, hd))).astype(BF16)
            dgn_ref[slot:slot + 1, :] += acc

        norm_bwd(H_QF, 512, HEAD, gqf, dqf_ref, 0)
        norm_bwd(H_KF, 512, HEAD, gkf, dkf_ref, 1)
        dp_ref[:, H_VF:H_VF + 512] = dvf_ref[...].astype(BF16)
        norm_bwd(H_QM, 512, MEM_HEAD, gqm, dqm_ref, 2)
        norm_bwd(H_QA, 512, HEAD, gqa, dqa_ref, 3)
        norm_bwd(H_KA, 128, HEAD, gka, dka_ref, 4)
        dp_ref[:, H_VA:H_VA + 128] = dva_ref[...].astype(BF16)
        dp_ref[:, H_FL:H_FL + 128] = dfl_ref[...]
        dp_ref[:, H_FL + 128:HALF_W] = jnp.zeros((tm, HALF_W - H_FL - 128), BF16)

    row = lambda w: pl.BlockSpec((tm, w), lambda i: (i, 0))
    g_spec = _full((1, 128))
    return pl.pallas_call(
        body, name="proj_pre_bwd", grid=(s // tm,),
        in_specs=[pl.BlockSpec(memory_space=pl.ANY), pl.BlockSpec((tm, HALF_W), lambda i: (i, 1)),
                  pl.BlockSpec((512, tm), lambda i: (0, i)), row(512), row(512), row(512), row(512),
                  row(128), row(128), row(128), g_spec, g_spec, g_spec, g_spec, g_spec],
        out_specs=[pl.BlockSpec((tm, HALF_W), lambda i: (i, 1)), _full((8, 128))],
        out_shape=[jax.ShapeDtypeStruct((s, PROJ_W), BF16), jax.ShapeDtypeStruct((8, 128), F32)],
        input_output_aliases={0: 0},
        compiler_params=_params(("arbitrary",)),
    )(dproj, proj, dqf, dkf, dvf, dqm, dqa, dka, dva, dfl, gq_fox, gk_fox, gq_mem, gq_swa, gk_swa)


def _in_bwd_x(dproj, w_in_p, x, g_mix, dx1):
    s = x.shape[0]
    tm = min(256, s)

    def body(dp_ref, w_ref, x_ref, g_ref, dx1_ref, gx_ref, dg_ref):
        i = pl.program_id(0)

        @pl.when(i == 0)
        def _():
            dg_ref[...] = jnp.zeros_like(dg_ref)

        dx, dg = _rms_bwd(x_ref[...], g_ref[...], _dot(dp_ref[...], w_ref[...], NT), dx1_ref[...])
        gx_ref[...] = dx
        dg_ref[...] += dg

    row = pl.BlockSpec((tm, D_MODEL), lambda i: (i, 0))
    return pl.pallas_call(
        body, name="in_bwd_x", grid=(s // tm,),
        in_specs=[pl.BlockSpec((tm, PROJ_W), lambda i: (i, 0)), _full(w_in_p.shape), row, _full((1, D_MODEL)), row],
        out_specs=[row, _full((1, D_MODEL))],
        out_shape=[jax.ShapeDtypeStruct((s, D_MODEL), F32), jax.ShapeDtypeStruct((1, D_MODEL), F32)],
        compiler_params=_params(("arbitrary",)),
    )(dproj, w_in_p, x, g_mix, dx1)


def _rel_bias_bwd(dbias, bucket):
    def body(db_ref, bk_ref, o_ref):
        bk = bk_ref[...]
        lane = lax.broadcasted_iota(jnp.int32, (1, 128), 1)
        for b in range(REL_BUCKETS):
            sel = bk == b
            acc = jnp.zeros((1, 128), F32)
            for h in range(SWA_HEADS):
                tot = jnp.sum(jnp.sum(jnp.where(sel, db_ref[h], 0.0), axis=-1, keepdims=True), axis=0, keepdims=True)
                acc = jnp.where(lane == h, tot, acc)
            o_ref[:, b * 128:(b + 1) * 128] = acc

    return pl.pallas_call(
        body, name="rel_bias_bwd",
        out_shape=jax.ShapeDtypeStruct((1, REL_BUCKETS * 128), F32),
        compiler_params=pltpu.CompilerParams(vmem_limit_bytes=VMEM_LIMIT),
    )(dbias, bucket)


def _my_place():
    return lax.axis_index("x"), lax.axis_index("y"), lax.axis_index("c")


def _peer(place, k):
    x, y, c = place
    return (1 - x if k & 4 else x, 1 - y if k & 2 else y, 1 - c if k & 1 else c)


def _index(place):
    x, y, c = place
    return 4 * x + 2 * y + c


HBM_SPEC = pl.BlockSpec(memory_space=pltpu.HBM)
SEM_SPEC = pl.BlockSpec(memory_space=pltpu.SEMAPHORE)
DATAFLOW = pltpu.SideEffectType.DATAFLOW_SIDE_EFFECTING


ALL_PEERS = tuple(range(1, N_DEV))
SAME_CORE = (2, 4, 6)
OWN = N_DEV - 1


def _split_copy(src_ref, land_ref, send_sems, recv_sems, me, k, gather):
    peer = _peer(me, k)
    if gather:
        src, dst = src_ref, land_ref.at[_index(me)]
    else:
        src, dst = src_ref.at[_index(peer)], land_ref.at[k - 1]
    return pltpu.make_async_remote_copy(src_ref=src, dst_ref=dst, send_sem=send_sems.at[k - 1], recv_sem=recv_sems.at[k - 1],
                                        device_id=peer, device_id_type=MESH)


def _own_copy(src_ref, land_ref, recv_sems, me, gather):
    if gather:
        src, dst = src_ref, land_ref.at[_index(me)]
    else:
        src, dst = src_ref.at[_index(me)], land_ref.at[OWN]
    return pltpu.make_async_copy(src, dst, recv_sems.at[OWN])


def _split_start(srcs, gather, name, peers=ALL_PEERS, after=None):
    n = len(srcs)
    extra = [] if after is None else [after]

    def body(*refs):
        refs = refs[:2 * n] + refs[2 * n + len(extra):]
        src_refs, land_refs = refs[:n], refs[n:2 * n]
        send_sems, recv_sems, token = refs[2 * n:3 * n], refs[3 * n:4 * n], refs[-1]
        me = _my_place()
        for w in range(n):
            for k in peers:
                _split_copy(src_refs[w], land_refs[w], send_sems[w], recv_sems[w], me, k, gather).start()
            _own_copy(src_refs[w], land_refs[w], recv_sems[w], me, gather).start()
        token[...] = jnp.zeros_like(token)

    lands = [lax.empty((N_DEV,) + (a.shape if gather else a.shape[1:]), a.dtype) for a in srcs]
    sems = [pltpu.SemaphoreType.DMA((N_DEV,))] * (2 * n)
    hbm = [pltpu.HBM(a.shape, a.dtype) for a in list(srcs) + lands]
    outs = pl.pallas_call(
        body, name=name,
        out_shape=(*sems, *hbm, jax.ShapeDtypeStruct((8, 128), F32)),
        in_specs=(HBM_SPEC,) * (2 * n) + (pl.BlockSpec(memory_space=pl.ANY),) * len(extra),
        out_specs=(SEM_SPEC,) * (2 * n) + (HBM_SPEC,) * (2 * n) + (pl.BlockSpec(memory_space=pltpu.VMEM),),
        input_output_aliases={i: 2 * n + i for i in range(2 * n)},
        compiler_params=pltpu.CompilerParams(has_side_effects=DATAFLOW),
    )(*[pltpu.with_memory_space_constraint(a, pltpu.HBM) for a in list(srcs) + lands], *extra)
    return list(outs[:n]), list(outs[n:2 * n]), list(outs[2 * n:3 * n]), list(outs[3 * n:4 * n]), outs[-1]


def _split_wait(started, w, after, gather, name):
    send_sems, recv_sems, srcs, lands, _ = started

    def body(src_ref, land_ref, send_sems, recv_sems, after_ref, src_out, land_out):
        me = _my_place()
        for k in ALL_PEERS:
            cp = _split_copy(src_ref, land_ref, send_sems, recv_sems, me, k, gather)
            cp.wait_send()
            cp.wait_recv()
        _own_copy(src_ref, land_ref, recv_sems, me, gather).wait()

    return pl.pallas_call(
        body, name=name,
        out_shape=(pltpu.HBM(srcs[w].shape, srcs[w].dtype), pltpu.HBM(lands[w].shape, lands[w].dtype)),
        in_specs=(HBM_SPEC, HBM_SPEC, SEM_SPEC, SEM_SPEC, pl.BlockSpec(memory_space=pl.ANY)),
        out_specs=(HBM_SPEC, HBM_SPEC), input_output_aliases={0: 0, 1: 1},
        compiler_params=pltpu.CompilerParams(has_side_effects=DATAFLOW),
    )(srcs[w], lands[w], send_sems[w], recv_sems[w], after)[1]


def _forward_copy(land_ref, send_sems, recv_sems, me, j, incoming):
    sibling = _peer(me, 1)
    rows = land_ref.at[_index(_peer(sibling if incoming else me, SAME_CORE[j]))]
    return pltpu.make_async_remote_copy(src_ref=rows, dst_ref=rows, send_sem=send_sems.at[j], recv_sem=recv_sems.at[j],
                                        device_id=sibling, device_id_type=MESH)


def _forward_start(started, after, name):
    send_a, recv_a, srcs, lands, _ = started

    def body(src_ref, land_ref, send_a, recv_a, after_ref, send_b, recv_b, src_out, land_out):
        me = _my_place()
        for j, k in enumerate(SAME_CORE):
            _split_copy(src_ref, land_ref, send_a, recv_a, me, k, True).wait_recv()
            _forward_copy(land_ref, send_b, recv_b, me, j, False).start()

    sems = pltpu.SemaphoreType.DMA((len(SAME_CORE),))
    return pl.pallas_call(
        body, name=name,
        out_shape=(sems, sems, pltpu.HBM(srcs[0].shape, srcs[0].dtype), pltpu.HBM(lands[0].shape, lands[0].dtype)),
        in_specs=(HBM_SPEC, HBM_SPEC, SEM_SPEC, SEM_SPEC, pl.BlockSpec(memory_space=pl.ANY)),
        out_specs=(SEM_SPEC, SEM_SPEC, HBM_SPEC, HBM_SPEC), input_output_aliases={0: 2, 1: 3},
        compiler_params=pltpu.CompilerParams(has_side_effects=DATAFLOW),
    )(srcs[0], lands[0], send_a[0], recv_a[0], after)


def _forward_wait(started, forwarded, name):
    send_a, recv_a, _, _, _ = started
    send_b, recv_b, src, land = forwarded

    def body(src_ref, land_ref, send_a, recv_a, send_b, recv_b, src_out, land_out):
        me = _my_place()
        _own_copy(src_ref, land_ref, recv_a, me, True).wait()
        for k in (1,) + SAME_CORE:
            _split_copy(src_ref, land_ref, send_a, recv_a, me, k, True).wait_send()
        _split_copy(src_ref, land_ref, send_a, recv_a, me, 1, True).wait_recv()
        for j in range(len(SAME_CORE)):
            _forward_copy(land_ref, send_b, recv_b, me, j, False).wait_send()
            _forward_copy(land_ref, send_b, recv_b, me, j, True).wait_recv()

    return pl.pallas_call(
        body, name=name,
        out_shape=(pltpu.HBM(src.shape, src.dtype), pltpu.HBM(land.shape, land.dtype)),
        in_specs=(HBM_SPEC, HBM_SPEC, SEM_SPEC, SEM_SPEC, SEM_SPEC, SEM_SPEC),
        out_specs=(HBM_SPEC, HBM_SPEC), input_output_aliases={0: 0, 1: 1},
        compiler_params=pltpu.CompilerParams(has_side_effects=DATAFLOW),
    )(src, land, send_a[0], recv_a[0], send_b, recv_b)[1]


def _adam_math(w, g, m, v):
    m2 = ADAM_B1 * m + (1.0 - ADAM_B1) * g
    v2 = ADAM_B2 * v + (1.0 - ADAM_B2) * (g * g)
    m_hat = m2 / (1.0 - ADAM_B1 ** ADAM_STEP)
    v_hat = v2 / (1.0 - ADAM_B2 ** ADAM_STEP)
    delta = -ADAM_LR * (m_hat / (jnp.sqrt(v_hat) + ADAM_EPS) + ADAM_WD * w)
    return delta, m2, v2


def _adamw(land, w, m, v, name):
    a, b = w.shape
    bp = land.shape[2]
    ta = min(128, a)

    def body(p_ref, w_ref, m_ref, v_ref, g_ref, d_ref, m2_ref, v2_ref):
        g = p_ref[0, :, 0:b].astype(F32)
        for k in range(1, N_DEV):
            g = g + p_ref[k, :, 0:b].astype(F32)
        delta, m2, v2 = _adam_math(w_ref[...], g, m_ref[...], v_ref[...])
        g_ref[...] = g
        d_ref[...] = delta
        m2_ref[...] = m2
        v2_ref[...] = v2

    blk = pl.BlockSpec((ta, b), lambda i: (i, 0))
    sd = jax.ShapeDtypeStruct((a, b), F32)
    return pl.pallas_call(
        body, name=name, grid=(a // ta,),
        in_specs=[pl.BlockSpec((N_DEV, ta, bp), lambda i: (0, i, 0)), blk, blk, blk],
        out_specs=[blk, blk, blk, blk], out_shape=[sd, sd, sd, sd],
        compiler_params=_params(("parallel",)),
    )(land, w, m, v)


def _bucket_table():
    t_loc = jnp.arange(SWA_BLOCK)[:, None] + SWA_BLOCK
    s_loc = jnp.arange(2 * SWA_BLOCK)[None, :]
    dist = t_loc - s_loc
    max_exact = REL_BUCKETS // 2
    d = jnp.maximum(dist, 0)
    df = jnp.maximum(d, 1).astype(F32)
    large = max_exact + (jnp.log(df / max_exact) / math.log(REL_MAX_DIST / max_exact) * (REL_BUCKETS - max_exact)).astype(jnp.int32)
    large = jnp.minimum(large, REL_BUCKETS - 1)
    bucket = jnp.where(d < max_exact, d, large)
    band = (dist >= 0) & (dist < SWA_BLOCK)
    return bucket, band


def _tile2(g):
    return jnp.concatenate([g, g], axis=1) if g.shape[1] == HEAD else g


SHARD_W = 737
SHARD_WP = 768
IN_WIDTH = N_DEV * SHARD_W
SEGMENTS = ((GL0, 2824, 3072), (QF0, 768, 512), (KF0, 1280, 512), (VF0, 1792, 512), (QM0, 2312, 512),
            (QA0, 0, 512), (KA0, 512, 128), (VA0, 640, 128), (FL0, 2304, 8))


def _lane_plan(sources):
    plan = []
    for t in range(len(sources) // 128):
        groups = {}
        for lane in range(128):
            src = sources[128 * t + lane]
            if src is not None:
                slab, col = src
                groups.setdefault((slab, col // 128, (lane - col) % 128), []).append(lane)
        tile = []
        for key, lanes in groups.items():
            assert lanes == list(range(lanes[0], lanes[-1] + 1))
            tile.append((key, lanes[0], lanes[-1] + 1))
        plan.append(tile)
    return plan


def _assemble(tile_plan, load, rows):
    lane = lax.broadcasted_iota(jnp.int32, (1, 128), 1)
    out = jnp.zeros((rows, 128), F32)
    for (slab, st, roll), lo, hi in tile_plan:
        v = load(slab, st)
        if roll:
            v = pltpu.roll(v, roll, 1)
        out = v if (lo, hi) == (0, 128) else jnp.where((lane >= lo) & (lane < hi), v, out)
    return out


def _w_in_from_shards(land):
    ref_col = [None] * PROJ_W
    for p0, r0, n in SEGMENTS:
        for i in range(n):
            ref_col[p0 + i] = divmod(r0 + i, SHARD_W)
    plan = _lane_plan(ref_col)
    d_model = land.shape[1]
    tm = 256

    def body(land_ref, o_ref):
        load = lambda slab, st: land_ref[slab, :, st * 128:(st + 1) * 128].astype(F32)
        for t, tile_plan in enumerate(plan):
            o_ref[:, t * 128:(t + 1) * 128] = _assemble(tile_plan, load, tm).astype(BF16)

    return pl.pallas_call(
        body, name="w_in_from_shards", grid=(d_model // tm,),
        in_specs=[pl.BlockSpec((N_DEV, tm, SHARD_WP), lambda i: (0, i, 0))],
        out_specs=pl.BlockSpec((tm, PROJ_W), lambda i: (i, 0)),
        out_shape=jax.ShapeDtypeStruct((d_model, PROJ_W), BF16),
        compiler_params=_params(("parallel",)),
    )(land)


def _dw_in_to_parts(dwp):
    padded_col = [None] * IN_WIDTH
    for p0, r0, n in SEGMENTS:
        for i in range(n):
            padded_col[r0 + i] = p0 + i
    sources = []
    for d in range(N_DEV):
        sources += [(0, padded_col[SHARD_W * d + c]) if c < SHARD_W else None for c in range(SHARD_WP)]
    plan = _lane_plan(sources)
    d_model = dwp.shape[0]
    tm = 256
    tiles = SHARD_WP // 128

    def body(dw_ref, o_ref):
        load = lambda slab, st: dw_ref[:, st * 128:(st + 1) * 128].astype(F32)
        for t, tile_plan in enumerate(plan):
            d, c = divmod(t, tiles)
            o_ref[d, :, c * 128:(c + 1) * 128] = _assemble(tile_plan, load, tm).astype(BF16)

    return pl.pallas_call(
        body, name="dw_in_to_parts", grid=(d_model // tm,),
        in_specs=[pl.BlockSpec((tm, PROJ_W), lambda i: (i, 0))],
        out_specs=pl.BlockSpec((N_DEV, tm, SHARD_WP), lambda i: (0, i, 0)),
        out_shape=jax.ShapeDtypeStruct((N_DEV, d_model, SHARD_WP), BF16),
        compiler_params=_params(("parallel",)),
    )(dwp)


def _cast_shards(shards):
    names = list(shards)

    def body(*refs):
        for src, dst in zip(refs[:len(names)], refs[len(names):]):
            if dst.shape != src.shape:
                dst[...] = jnp.zeros(dst.shape, BF16)
                dst[:, 0:src.shape[1]] = src[...].astype(BF16)
            else:
                dst[...] = src[...].astype(BF16)

    out_shape = [jax.ShapeDtypeStruct((shards[n].shape[0], SHARD_WP if n == "w_in" else shards[n].shape[1]), BF16)
                 for n in names]
    outs = pl.pallas_call(body, name="cast_shards", out_shape=out_shape,
                          compiler_params=pltpu.CompilerParams(vmem_limit_bytes=VMEM_LIMIT))(*[shards[n] for n in names])
    return dict(zip(names, outs))


def _tie(x, *tokens):
    for t in tokens:
        if t is not None:
            x = x + t[0:1, 0:1]
    return x


def _local_step(x, mem, target, p, getw, emit, deps=()):
    s = x.shape[0]
    bucket, band = _bucket_table()
    bucket_m = jnp.where(band, bucket, -1).astype(jnp.int32)
    bias = _bias_table(p["rel_bias"], bucket_m)
    bucket_t = jnp.transpose(bucket_m)
    bias_t = _bias_table(p["rel_bias"], bucket_t)
    gqf, gkf, gqa, gka = _tile2(p["qn_fox"]), _tile2(p["kn_fox"]), _tile2(p["qn_swa"]), _tile2(p["kn_swa"])
    gqm = p["qn_mem"]
    bf128 = jnp.pad(p["b_forget"], ((0, 0), (0, 120)))
    sink = p["sink_swa"].reshape(8)

    h = _rms_fwd(x, p["g_mix"], "rms_mix", tuple(deps) + (bias, bias_t))
    w_in = getw("w_in", h)
    proj = _mm(h, w_in, "nn", BF16, 512, 1536, 1024, "proj")
    fl = _mm(h, w_in[:, FL0:FL0 + 128], "nn", F32, 512, 128, 1024, "proj_fl")
    qf, kf, vf, qm, qa, ka, va, qf_t, vf_t = _proj_post(proj, gqf, gkf, gqm, gqa, gka)
    cc4, ca4 = _fox_gate_fwd(fl, bf128)
    w_kv = getw("w_mem_kv", cc4)
    mem_n, kv_raw, mk, mv = _memkv_fwd(mem, p["g_mem"], w_kv, p["kn_mem"])
    kp = jnp.pad(ka, ((SWA_BLOCK, 0), (0, 0)))
    vp = jnp.pad(va, ((SWA_BLOCK, 0), (0, 0)))
    oa = _swa_fwd(qa, kp, vp, bias, sink)
    of, lse4, of_t = _fox_fwd(qf, kf, vf_t, ca4)
    om = _mem_fwd(qm, mk, mv)
    wa, wf, wm, w_out = getw("w_o_swa", oa), getw("w_o_fox", oa), getw("w_o_mem", oa), getw("w_out", oa)
    x1, hm, merged = _merge_fwd(x, oa, of, om, proj, p["b_gate"], wa, wf, wm, w_out, p["g_mlp"])
    w_up = getw("w_mlp_up", of)
    u = _mlp_up(hm, w_up)
    w_down = getw("w_mlp_down", hm)
    dy, dy_b, loss = _mlp_down_loss(u, w_down, x1, target)

    da = _mlp_bwd_act(dy_b, w_down, u)
    t_down = emit({"w_mlp_down": _mm(u, dy_b, "tn", BF16, 1024, 1024, 2048, "dw_down")})
    dx1, dg_mlp = _mlp_bwd_x(da, w_up, x1, dy, _tie(p["g_mlp"], t_down))
    t_up = emit({"w_mlp_up": _mm(hm, da, "tn", BF16, 1024, 1024, 2048, "dw_up", column_chunks=True)})
    dproj, doa, dof_t, dom, dya, dyf, dym, db_gate = _merge_bwd(
        dx1, oa, of, om, proj, _tie(p["b_gate"], t_up), wa, wf, wm, w_out)
    dw_oa, dw_of, dw_om = _mm_tn3([oa, of, om], [dya, dyf, dym], "dw_o")
    t_o = emit({"w_out": _mm(merged, dx1, "tn", BF16, 1024, 1024, 2048, "dw_out"),
                "w_o_swa": dw_oa, "w_o_fox": dw_of, "w_o_mem": dw_om})

    dqm, dmk, dmv = _mem_bwd(qm, mk, mv, dom)
    dw_kv, dkn_mem, dg_mem = _memkv_bwd(dmk, dmv, kv_raw, _tie(p["kn_mem"], t_o), mem, p["g_mem"], mem_n, w_kv)
    t_kv = emit({"w_mem_kv": dw_kv})
    dqa, dkp, dvp, dbias, dsink = _swa_bwd(qa, kp, vp, bias_t, _tie(p["sink_swa"], t_kv).reshape(8), doa)
    dqf_t, dkf, dvf, dck4, dcq4 = _fox_bwd(qf_t, kf, vf, dof_t, of_t, cc4, lse4)

    dfl, db_forget = _fox_gate_bwd(dcq4, dck4, fl, bf128)

    dproj, dgn = _proj_pre_bwd(dproj, proj, dqf_t, dkf, dvf, dqm, dqa, dkp[SWA_BLOCK:], dvp[SWA_BLOCK:], dfl,
                               gqf, gkf, gqm, gqa, gka)
    t_in = emit({"w_in": _mm(h, dproj, "tn", BF16, 1024, 3072, 1024, "dw_in")})
    grad_x, dg_mix = _in_bwd_x(dproj, w_in, x, _tie(p["g_mix"], t_in), dx1)
    d_rel = _rel_bias_bwd(dbias, bucket_t)

    fold = lambda r: dgn[r:r + 1, 0:HEAD] + dgn[r:r + 1, HEAD:128]
    small = {
        "g_mix": dg_mix, "b_gate": db_gate, "b_forget": db_forget[:, 0:8],
        "qn_swa": fold(3), "kn_swa": fold(4), "sink_swa": dsink[:, 0].reshape(1, 8), "rel_bias": d_rel,
        "qn_fox": fold(0), "kn_fox": fold(1), "g_mem": dg_mem, "qn_mem": dgn[2:3, :], "kn_mem": dkn_mem,
        "g_mlp": dg_mlp,
    }
    return loss, grad_x, small


SMALL = ("g_mix", "b_gate", "b_forget", "qn_swa", "kn_swa", "sink_swa", "rel_bias", "qn_fox", "kn_fox", "g_mem",
         "qn_mem", "kn_mem", "g_mlp")
BIG = ("w_in", "w_mem_kv", "w_o_swa", "w_o_fox", "w_o_mem", "w_out", "w_mlp_up", "w_mlp_down")
COL_SHARDED = ("w_in", "w_o_swa", "w_o_fox", "w_o_mem", "w_mlp_up")
WEIGHTS = ("g_mix", "w_in", "b_gate", "b_forget", "qn_swa", "kn_swa", "sink_swa", "rel_bias", "qn_fox", "kn_fox", "g_mem",
           "w_mem_kv", "qn_mem", "kn_mem", "w_o_swa", "w_o_fox", "w_o_mem", "w_out", "g_mlp", "w_mlp_up", "w_mlp_down")
SMALL_SLOTS = (("g_mix", 1024), ("b_gate", 3072), ("b_forget", 128), ("qn_swa", 128), ("kn_swa", 128), ("sink_swa", 128),
               ("rel_bias", REL_BUCKETS * 128), ("qn_fox", 128), ("kn_fox", 128), ("g_mem", 1024), ("qn_mem", 128),
               ("kn_mem", 128), ("g_mlp", 1024), ("loss", 128))
SMALL_OFF = {n: sum(w for _, w in SMALL_SLOTS[:i]) for i, (n, _) in enumerate(SMALL_SLOTS)}
SMALL_ROW = sum(w for _, w in SMALL_SLOTS)


def _gathered_to_full(name, g):
    if name in COL_SHARDED:
        return jnp.transpose(g, (1, 0, 2)).reshape(g.shape[1], N_DEV * g.shape[2])
    return g.reshape(N_DEV * g.shape[1], g.shape[2])


def _full_to_parts(name, full, b):
    if name in COL_SHARDED:
        return jnp.transpose(full.reshape(full.shape[0], N_DEV, b), (1, 0, 2)).astype(BF16)
    return full.reshape(N_DEV, full.shape[0] // N_DEV, full.shape[1]).astype(BF16)


def _pack_small(grads, loss):
    pieces = []
    for n, width in SMALL_SLOTS:
        a = loss.reshape(1, 1) if n == "loss" else grads[n].reshape(1, -1)
        pieces.append(jnp.pad(a, ((0, 0), (0, width - a.shape[1]))))
    return jnp.concatenate(pieces, axis=1)


def _adamw_small(gathered, w, m, v):
    names = list(SMALL)

    def body(*refs):
        p_ref = refs[0]
        ins = refs[1:1 + 3 * len(names)]
        outs = refs[1 + 3 * len(names):]
        g_all = p_ref[0]
        for k in range(1, N_DEV):
            g_all = g_all + p_ref[k]
        for i, n in enumerate(names):
            w_ref, m_ref, v_ref = ins[3 * i:3 * i + 3]
            out = outs[4 * i:4 * i + 4]
            rows, cols = w_ref.shape
            for r in range(rows):
                off = SMALL_OFF[n] + 128 * r
                g = g_all[:, off:off + cols]
                rs = slice(r, r + 1)
                res = (g,) + _adam_math(w_ref[rs, :], g, m_ref[rs, :], v_ref[rs, :])
                for o_ref, val in zip(out, res):
                    o_ref[rs, :] = val
        outs[-1][...] = g_all[:, SMALL_OFF["loss"]:SMALL_OFF["loss"] + 128]

    args = [gathered]
    out_shape = []
    for n in names:
        args += [w[n], m[n], v[n]]
        out_shape += [jax.ShapeDtypeStruct(w[n].shape, F32)] * 4
    out_shape.append(jax.ShapeDtypeStruct((1, 128), F32))
    outs = pl.pallas_call(body, name="adamw_small", out_shape=out_shape)(*args)
    return {n: outs[4 * i:4 * i + 4] for i, n in enumerate(names)}, outs[-1]


def kernel(x, mem, g_mix, w_in, b_gate, b_forget, qn_swa, kn_swa, sink_swa, rel_bias, qn_fox, kn_fox, g_mem, w_mem_kv, qn_mem, kn_mem, w_o_swa, w_o_fox, w_o_mem, w_out, g_mlp, w_mlp_up, w_mlp_down, loss_target, m_g_mix, m_w_in, m_b_gate, m_b_forget, m_qn_swa, m_kn_swa, m_sink_swa, m_rel_bias, m_qn_fox, m_kn_fox, m_g_mem, m_w_mem_kv, m_qn_mem, m_kn_mem, m_w_o_swa, m_w_o_fox, m_w_o_mem, m_w_out, m_g_mlp, m_w_mlp_up, m_w_mlp_down, v_g_mix, v_w_in, v_b_gate, v_b_forget, v_qn_swa, v_kn_swa, v_sink_swa, v_rel_bias, v_qn_fox, v_kn_fox, v_g_mem, v_w_mem_kv, v_qn_mem, v_kn_mem, v_w_o_swa, v_w_o_fox, v_w_o_mem, v_w_out, v_g_mlp, v_w_mlp_up, v_w_mlp_down):
    wts = dict(g_mix=g_mix, w_in=w_in, b_gate=b_gate, b_forget=b_forget, qn_swa=qn_swa, kn_swa=kn_swa, sink_swa=sink_swa,
               rel_bias=rel_bias, qn_fox=qn_fox, kn_fox=kn_fox, g_mem=g_mem, w_mem_kv=w_mem_kv, qn_mem=qn_mem, kn_mem=kn_mem,
               w_o_swa=w_o_swa, w_o_fox=w_o_fox, w_o_mem=w_o_mem, w_out=w_out, g_mlp=g_mlp, w_mlp_up=w_mlp_up,
               w_mlp_down=w_mlp_down)
    mom = dict(g_mix=m_g_mix, w_in=m_w_in, b_gate=m_b_gate, b_forget=m_b_forget, qn_swa=m_qn_swa, kn_swa=m_kn_swa,
               sink_swa=m_sink_swa, rel_bias=m_rel_bias, qn_fox=m_qn_fox, kn_fox=m_kn_fox, g_mem=m_g_mem, w_mem_kv=m_w_mem_kv,
               qn_mem=m_qn_mem, kn_mem=m_kn_mem, w_o_swa=m_w_o_swa, w_o_fox=m_w_o_fox, w_o_mem=m_w_o_mem, w_out=m_w_out,
               g_mlp=m_g_mlp, w_mlp_up=m_w_mlp_up, w_mlp_down=m_w_mlp_down)
    var = dict(g_mix=v_g_mix, w_in=v_w_in, b_gate=v_b_gate, b_forget=v_b_forget, qn_swa=v_qn_swa, kn_swa=v_kn_swa,
               sink_swa=v_sink_swa, rel_bias=v_rel_bias, qn_fox=v_qn_fox, kn_fox=v_kn_fox, g_mem=v_g_mem, w_mem_kv=v_w_mem_kv,
               qn_mem=v_qn_mem, kn_mem=v_kn_mem, w_o_swa=v_w_o_swa, w_o_fox=v_w_o_fox, w_o_mem=v_w_o_mem, w_out=v_w_out,
               g_mlp=v_g_mlp, w_mlp_up=v_w_mlp_up, w_mlp_down=v_w_mlp_down)

    shards = _cast_shards({n: wts[n][0] for n in BIG})
    first = _split_start([shards["w_in"]], True, "ag_start_w_in", peers=(1,) + SAME_CORE)
    rest = _split_start([shards[n] for n in BIG[1:]], True, "ag_start_rest", after=first[4])
    full = {}

    def getw(n, after):
        if n == "w_in" and n not in full:
            forwarded = _forward_start(first, after, "ag_forward_w_in")
            full[n] = _w_in_from_shards(_forward_wait(first, forwarded, "ag_wait_w_in"))
        elif n not in full:
            land = _split_wait(rest, BIG[1:].index(n), after, True, "ag_wait_" + n)
            full[n] = land if n == "w_mlp_up" else _gathered_to_full(n, land)
        return full[n]

    exchanges = {}

    def emit(grads_by_name):
        parts = []
        for n, grad in grads_by_name.items():
            if n == "w_in":
                parts.append(_dw_in_to_parts(grad))
            else:
                parts.append(grad if n == "w_mlp_up" else _full_to_parts(n, grad, wts[n].shape[2]))
        started = _split_start(parts, False, "rs_start_" + next(iter(grads_by_name)))
        for w, n in enumerate(grads_by_name):
            exchanges[n] = (started, w)
        return started[4]

    small_p = {n: wts[n] for n in SMALL}
    loss, grad_x, small_g = _local_step(x[0], mem[0], loss_target[0], small_p, getw, emit, (first[4], rest[4]))

    packed = _pack_small(small_g, loss)
    small_gather = _split_start([packed], True, "ag_start_small")

    grads, delta, new_m, new_v = {}, {}, {}, {}

    def update(n, after):
        land = _split_wait(*exchanges[n], after, False, "rs_wait_" + n)
        g, d, m2, v2 = _adamw(land, wts[n][0], mom[n][0], var[n][0], "adamw_" + n)
        grads[n], delta[n], new_m[n], new_v[n] = g[None], d[None], m2[None], v2[None]
        return d

    after = small_gather[4]
    for n in exchanges:
        if n != "w_in":
            after = update(n, after)

    gathered = _split_wait(small_gather, 0, after, True, "ag_wait_small")
    small_out, total = _adamw_small(gathered, small_p, mom, var)
    for name, (g, d, m2, v2) in small_out.items():
        grads[name], delta[name], new_m[name], new_v[name] = g, d, m2, v2
    update("w_in", total)

    return (total[0, 0], grad_x[None], *[grads[n] for n in WEIGHTS], *[delta[n] for n in WEIGHTS],
            *[new_m[n] for n in WEIGHTS], *[new_v[n] for n in WEIGHTS])
```

```python
import math

import jax
import jax.numpy as jnp
from jax import lax
from jax.experimental import pallas as pl
from jax.experimental.pallas import tpu as pltpu

F32 = jnp.float32
BF16 = jnp.bfloat16

D_MODEL = 1024
N_MEM = 256
D_FF = 4096
HEAD = 64
SWA_HEADS = 8
SWA_BLOCK = 128
MEM_HEADS = 4
MEM_HEAD = 128
EPS = 1e-6
NEG = -1e30
REL_BUCKETS = 32
REL_MAX_DIST = 128

ADAM_LR = 0.001
ADAM_B1 = 0.9
ADAM_B2 = 0.999
ADAM_EPS = 1e-08
ADAM_WD = 0.01
ADAM_STEP = 10

GL0, QF0, KF0, VF0, QM0, QA0, KA0, VA0, FL0 = 0, 3072, 3584, 4096, 4608, 5120, 5632, 5760, 5888
PROJ_W = 6144
HALF_W = 3072
H_QF, H_KF, H_VF, H_QM, H_QA, H_KA, H_VA, H_FL = 0, 512, 1024, 1536, 2048, 2560, 2688, 2816

VMEM_LIMIT = 56 * 1024 * 1024
N_DEV = 8
MESH = pl.DeviceIdType.MESH

NN = (((1,), (0,)), ((), ()))
NT = (((1,), (1,)), ((), ()))
TN = (((0,), (0,)), ((), ()))


def _dot(a, b, dims=NN):
    return lax.dot_general(a, b, dims, preferred_element_type=F32)


def _params(sem):
    return pltpu.CompilerParams(dimension_semantics=sem, vmem_limit_bytes=VMEM_LIMIT)


def _full(shape):
    nd = len(shape)
    return pl.BlockSpec(shape, lambda *_: (0,) * nd)


def _sigmoid(z):
    return 1.0 / (1.0 + jnp.exp(-z))


def _group_mean(v, hd):
    if hd == 128:
        return jnp.mean(v, axis=-1, keepdims=True)
    r = lax.broadcasted_iota(jnp.int32, (128, 128), 0) // HEAD
    c = lax.broadcasted_iota(jnp.int32, (128, 128), 1) // HEAD
    same_head = jnp.where(r == c, 1.0 / HEAD, 0.0).astype(BF16)
    total = None
    rest = v
    for _ in range(2):
        part = rest.astype(BF16)
        rest = rest - part.astype(F32)
        term = _dot(part, same_head)
        total = term if total is None else total + term
    return total


def _mm(a, b, mode, out_dtype, tm, tn, tk, name, column_chunks=False):
    if mode == "nn":
        m, k = a.shape
        n = b.shape[1]
    elif mode == "nt":
        m, k = a.shape
        n = b.shape[0]
    else:
        k, m = a.shape
        n = b.shape[1]
    tm, tn, tk = min(tm, m), min(tn, n), min(tk, k)
    nk = k // tk
    chunk = n // N_DEV
    per_tile = tn // chunk if column_chunks else 1
    dims = {"nn": NN, "nt": NT, "tn": TN}[mode]
    a_spec = pl.BlockSpec((tk, tm), lambda j, i, kk: (kk, i)) if mode == "tn" else pl.BlockSpec((tm, tk), lambda j, i, kk: (i, kk))
    b_spec = pl.BlockSpec((tn, tk), lambda j, i, kk: (j, kk)) if mode == "nt" else pl.BlockSpec((tk, tn), lambda j, i, kk: (kk, j))

    def body(a_ref, b_ref, o_ref, *acc):
        prod = _dot(a_ref[...].astype(BF16), b_ref[...].astype(BF16), dims)

        def write(res):
            if column_chunks:
                for c in range(per_tile):
                    o_ref[c] = res[:, c * chunk:(c + 1) * chunk].astype(o_ref.dtype)
            else:
                o_ref[...] = res.astype(o_ref.dtype)

        if nk == 1:
            write(prod)
        else:
            acc_ref, = acc
            kk = pl.program_id(2)

            @pl.when(kk == 0)
            def _():
                acc_ref[...] = prod

            @pl.when(kk > 0)
            def _():
                acc_ref[...] += prod

            @pl.when(kk == nk - 1)
            def _():
                write(acc_ref[...])

    return pl.pallas_call(
        body, name=name, grid=(n // tn, m // tm, nk),
        in_specs=[a_spec, b_spec],
        out_specs=(pl.BlockSpec((per_tile, tm, chunk), lambda j, i, kk: (j, i, 0)) if column_chunks
                   else pl.BlockSpec((tm, tn), lambda j, i, kk: (i, j))),
        out_shape=jax.ShapeDtypeStruct((N_DEV, m, chunk) if column_chunks else (m, n), out_dtype),
        scratch_shapes=[pltpu.VMEM((tm, tn), F32)] if nk > 1 else [],
        compiler_params=_params(("parallel", "parallel", "arbitrary")),
    )(a, b)


def _mm_tn3(a_list, b_list, name):
    s, m = a_list[0].shape
    n = b_list[0].shape[1]
    tk = min(2048, s)
    nk = s // tk

    def body(*refs):
        a_refs, b_refs, o_refs, acc_refs = refs[0:3], refs[3:6], refs[6:9], refs[9:12]
        kk = pl.program_id(0)
        for a_ref, b_ref, o_ref, acc_ref in zip(a_refs, b_refs, o_refs, acc_refs):
            prod = _dot(a_ref[...], b_ref[...], TN)
            if nk == 1:
                o_ref[...] = prod.astype(o_ref.dtype)
                continue

            @pl.when(kk == 0)
            def _(acc_ref=acc_ref, prod=prod):
                acc_ref[...] = prod

            @pl.when(kk > 0)
            def _(acc_ref=acc_ref, prod=prod):
                acc_ref[...] += prod

            @pl.when(kk == nk - 1)
            def _(acc_ref=acc_ref, o_ref=o_ref):
                o_ref[...] = acc_ref[...].astype(o_ref.dtype)

    return pl.pallas_call(
        body, name=name, grid=(nk,),
        in_specs=[pl.BlockSpec((tk, m), lambda kk: (kk, 0))] * 3 + [pl.BlockSpec((tk, n), lambda kk: (kk, 0))] * 3,
        out_specs=[_full((m, n))] * 3,
        out_shape=[jax.ShapeDtypeStruct((m, n), BF16)] * 3,
        scratch_shapes=[pltpu.VMEM((m, n), F32)] * 3,
        compiler_params=_params(("arbitrary",)),
    )(*a_list, *b_list)


def _rms_fwd(x, g, name, deps=()):
    s, d = x.shape
    tm = min(512, s)

    def body(x_ref, g_ref, *rest):
        h_ref = rest[len(deps)]
        xv = x_ref[...]
        r = lax.rsqrt(jnp.mean(xv * xv, axis=-1, keepdims=True) + EPS)
        h_ref[...] = (xv * r * g_ref[...]).astype(BF16)

    return pl.pallas_call(
        body, name=name, grid=(s // tm,),
        in_specs=[pl.BlockSpec((tm, d), lambda i: (i, 0)), _full((1, d))] + [pl.BlockSpec(memory_space=pl.ANY)] * len(deps),
        out_specs=pl.BlockSpec((tm, d), lambda i: (i, 0)),
        out_shape=jax.ShapeDtypeStruct((s, d), BF16),
        compiler_params=_params(("parallel",)),
    )(x, g, *deps)


def _proj_post(proj, gq_fox, gk_fox, gq_mem, gq_swa, gk_swa):
    s = proj.shape[0]
    tm = min(256, s)

    def body(p_ref, gqf, gkf, gqm, gqa, gka, qf_ref, kf_ref, vf_ref, qm_ref, qa_ref, ka_ref, va_ref, qft_ref, vft_ref):
        def norm(off, width, hd, g_ref, o_ref, scaled_t_ref=None):
            for b in range(width // 128):
                v = p_ref[:, off + b * 128: off + (b + 1) * 128].astype(F32)
                r = lax.rsqrt(_group_mean(v * v, hd) + EPS)
                vn = (v * r * g_ref[...]).astype(BF16)
                o_ref[:, b * 128:(b + 1) * 128] = vn
                if scaled_t_ref is not None:
                    scaled_t_ref[b * 128:(b + 1) * 128, :] = (vn.astype(F32) * 0.125).T.astype(BF16)

        norm(H_QF, 512, HEAD, gqf, qf_ref, qft_ref)
        norm(H_KF, 512, HEAD, gkf, kf_ref)
        vf_ref[...] = p_ref[:, H_VF:H_VF + 512].astype(BF16)
        for b in range(4):
            vft_ref[b * 128:(b + 1) * 128, :] = p_ref[:, H_VF + b * 128:H_VF + (b + 1) * 128].astype(F32).T.astype(BF16)
        norm(H_QM, 512, MEM_HEAD, gqm, qm_ref)
        norm(H_QA, 512, HEAD, gqa, qa_ref)
        norm(H_KA, 128, HEAD, gka, ka_ref)
        va_ref[...] = p_ref[:, H_VA:H_VA + 128].astype(BF16)

    g_spec = _full((1, 128))
    o512 = pl.BlockSpec((tm, 512), lambda i: (i, 0))
    o128 = pl.BlockSpec((tm, 128), lambda i: (i, 0))
    s512 = jax.ShapeDtypeStruct((s, 512), BF16)
    s128 = jax.ShapeDtypeStruct((s, 128), BF16)
    return pl.pallas_call(
        body, name="proj_post", grid=(s // tm,),
        in_specs=[pl.BlockSpec((tm, HALF_W), lambda i: (i, 1)), g_spec, g_spec, g_spec, g_spec, g_spec],
        out_specs=[o512, o512, o512, o512, o512, o128, o128] + [pl.BlockSpec((512, tm), lambda i: (0, i))] * 2,
        out_shape=[s512, s512, s512, s512, s512, s128, s128] + [jax.ShapeDtypeStruct((512, s), BF16)] * 2,
        compiler_params=_params(("parallel",)),
    )(proj, gq_fox, gk_fox, gq_mem, gq_swa, gk_swa)


def _tri(n, lower):
    r = lax.broadcasted_iota(jnp.int32, (n, n), 0)
    c = lax.broadcasted_iota(jnp.int32, (n, n), 1)
    return jnp.where((c <= r) if lower else (c >= r), 1.0, 0.0).astype(F32)


def _fox_gate_fwd(proj, b_forget128):
    s = proj.shape[0]
    tm = min(512, s)

    def body(p_ref, b_ref, cc_ref, ca_ref, carry_ref):
        i = pl.program_id(0)

        @pl.when(i == 0)
        def _():
            carry_ref[...] = jnp.zeros_like(carry_ref)

        z = p_ref[...] + b_ref[...]
        logf = jnp.minimum(z, 0.0) - jnp.log(1.0 + jnp.exp(-jnp.abs(z)))
        c = jnp.dot(_tri(tm, True), logf, precision=lax.Precision.HIGHEST, preferred_element_type=F32) + carry_ref[...]
        carry_ref[...] = c[tm - 1:tm, :]
        lane = lax.broadcasted_iota(jnp.int32, (tm, 128), 1)
        for hp in range(4):
            cc_ref[hp] = c if hp == 0 else pltpu.roll(c, 128 - 2 * hp, 1)
            aug = jnp.zeros((tm, 128), F32)
            for e in range(2):
                rest = jnp.broadcast_to(c[:, 2 * hp + e:2 * hp + e + 1], (tm, 128))
                for part in range(3):
                    piece = rest.astype(BF16).astype(F32)
                    aug = jnp.where(lane == HEAD * (1 - e) + part, piece, aug)
                    rest = rest - piece
            ca_ref[hp] = aug.astype(BF16)

    return pl.pallas_call(
        body, name="fox_gate_fwd", grid=(s // tm,),
        in_specs=[pl.BlockSpec((tm, 128), lambda i: (i, 0)), _full((1, 128))],
        out_specs=[pl.BlockSpec((4, tm, 128), lambda i: (0, i, 0))] * 2,
        out_shape=[jax.ShapeDtypeStruct((4, s, 128), F32), jax.ShapeDtypeStruct((4, s, 128), BF16)],
        scratch_shapes=[pltpu.VMEM((1, 128), F32)],
        compiler_params=_params(("arbitrary",)),
    )(proj, b_forget128)


def _memkv_fwd(mem, g_mem, w_kv, kn_mem):
    m = mem.shape[0]

    def body(mem_ref, g_ref, w_ref, kn_ref, memn_ref, kv_ref, mk_ref, mv_ref):
        xv = mem_ref[...]
        r = lax.rsqrt(jnp.mean(xv * xv, axis=-1, keepdims=True) + EPS)
        mn = (xv * r * g_ref[...]).astype(BF16)
        memn_ref[...] = mn
        kv = _dot(mn, w_ref[...])
        kv_ref[...] = kv
        for h in range(MEM_HEADS):
            v = kv[:, h * 128:(h + 1) * 128]
            rr = lax.rsqrt(jnp.mean(v * v, axis=-1, keepdims=True) + EPS)
            mk_ref[:, h * 128:(h + 1) * 128] = (v * rr * kn_ref[...]).astype(BF16)
        mv_ref[...] = kv[:, 512:1024].astype(BF16)

    return pl.pallas_call(
        body, name="memkv_fwd",
        out_shape=[jax.ShapeDtypeStruct((m, D_MODEL), BF16), jax.ShapeDtypeStruct((m, 1024), F32),
                   jax.ShapeDtypeStruct((m, 512), BF16), jax.ShapeDtypeStruct((m, 512), BF16)],
        compiler_params=pltpu.CompilerParams(vmem_limit_bytes=VMEM_LIMIT),
    )(mem, g_mem, w_kv, kn_mem)


def _bias_table(rel_bias, bucket):
    def body(rb_ref, bk_ref, o_ref):
        bk = bk_ref[...]
        for h in range(SWA_HEADS):
            acc = jnp.zeros(bk.shape, F32)
            for b in range(REL_BUCKETS):
                acc = jnp.where(bk == b, rb_ref[b, h], acc)
            o_ref[h] = acc

    return pl.pallas_call(
        body, name="bias_table",
        in_specs=[pl.BlockSpec(memory_space=pltpu.SMEM), pl.BlockSpec(memory_space=pltpu.VMEM)],
        out_shape=jax.ShapeDtypeStruct((SWA_HEADS,) + bucket.shape, F32),
    )(rel_bias, bucket)


def _swa_valid(n):
    row = lax.broadcasted_iota(jnp.int32, (SWA_BLOCK, 2 * SWA_BLOCK), 0)
    col = lax.broadcasted_iota(jnp.int32, (SWA_BLOCK, 2 * SWA_BLOCK), 1)
    dist = row + SWA_BLOCK - col
    return (dist >= 0) & (dist < SWA_BLOCK) & ((col >= SWA_BLOCK) | (n > 0))


def _swa_fwd(qa, kp, vp, bias, sink):
    s = qa.shape[0]
    nb = s // SWA_BLOCK

    def body(sink_ref, q_ref, kp_ref, vp_ref, bias_ref, o_ref):
        n = pl.program_id(0)
        start = pl.multiple_of(n * SWA_BLOCK, SWA_BLOCK)
        k2 = kp_ref[pl.ds(start, 2 * SWA_BLOCK), :]
        v2 = vp_ref[pl.ds(start, 2 * SWA_BLOCK), :]
        valid = _swa_valid(n)
        heads = range(SWA_HEADS)
        hs = lambda h: slice(h * HEAD, (h + 1) * HEAD)
        sc = [jnp.where(valid, _dot(q_ref[:, hs(h)], k2[:, hs(h // 4)], NT) * 0.125 + bias_ref[h], NEG) for h in heads]
        pn = []
        for h in heads:
            sk = sink_ref[h]
            mx = jnp.maximum(jnp.max(sc[h], axis=-1, keepdims=True), sk)
            p = jnp.exp(sc[h] - mx)
            den = jnp.sum(p, axis=-1, keepdims=True) + jnp.exp(sk - mx)
            pn.append((p / den).astype(BF16))
        outs = [_dot(pn[h], v2[:, hs(h // 4)]).astype(BF16) for h in heads]
        for h in heads:
            o_ref[:, hs(h)] = outs[h]

    return pl.pallas_call(
        body, name="swa_fwd", grid=(nb,),
        in_specs=[pl.BlockSpec(memory_space=pltpu.SMEM),
                  pl.BlockSpec((SWA_BLOCK, 512), lambda n: (n, 0)),
                  _full(kp.shape), _full(vp.shape), _full(bias.shape)],
        out_specs=pl.BlockSpec((SWA_BLOCK, 512), lambda n: (n, 0)),
        out_shape=jax.ShapeDtypeStruct((s, 512), BF16),
        compiler_params=_params(("parallel",)),
    )(sink, qa, kp, vp, bias)


def _head_mask(e):
    lane = lax.broadcasted_iota(jnp.int32, (1, 128), 1)
    return (lane >= e * HEAD) & (lane < (e + 1) * HEAD)


FOX_FWD_T = 1024
FOX_BWD_T = 512


def _head_rows(e):
    row = lax.broadcasted_iota(jnp.int32, (128, 1), 0)
    return (row >= e * HEAD) & (row < (e + 1) * HEAD)


def _fox_fwd(q, k, v_t, ca4):
    s = q.shape[0]
    t = min(FOX_FWD_T, s)
    nq = s // t

    def body(q_ref, k_ref, vt_ref, ca_ref, o_ref, lse_ref, ot_ref):
        i = pl.program_id(1)
        qs = q_ref[...] * jnp.asarray(0.125, BF16)
        lane = lax.broadcasted_iota(jnp.int32, (1, 128), 1)
        minus = [jnp.where((lane >= HEAD * (1 - e)) & (lane < HEAD * (1 - e) + 3), -1.0, 0.0).astype(BF16) for e in range(2)]
        qe = [jnp.where(_head_mask(e), qs, jnp.broadcast_to(minus[e], qs.shape)) for e in range(2)]

        def block(carry, key0, nkeys, q0, nqs, masked):
            ks = pl.ds(pl.multiple_of(key0, 128), nkeys)
            kj = k_ref[ks, :]
            caj = ca_ref[0, ks, :]
            vtj = vt_ref[:, ks]
            out = []
            for e in range(2):
                m_all, acc_all = carry[2 * e], carry[2 * e + 1]
                m, acc = m_all[:, q0:q0 + nqs], acc_all[:, q0:q0 + nqs]
                st = _dot(jnp.where(_head_mask(e), kj, caj), qe[e][q0:q0 + nqs, :], NT)
                if masked:
                    krow = lax.broadcasted_iota(jnp.int32, (nkeys, nqs), 0) + key0
                    qcol = lax.broadcasted_iota(jnp.int32, (nkeys, nqs), 1) + (i * t + q0)
                    st = jnp.where(krow <= qcol, st, NEG)
                m_new = jnp.maximum(m, jnp.max(st, axis=0, keepdims=True))
                alpha = jnp.exp(m - m_new)
                pt = jnp.exp(st - m_new).astype(BF16)
                vte = jnp.where(_head_rows(e), vtj, jnp.ones_like(vtj))
                acc_new = alpha * acc + _dot(vte, pt)
                if nqs < t:
                    m_new = jnp.concatenate([m_all[:, :q0], m_new], axis=1)
                    acc_new = jnp.concatenate([acc_all[:, :q0], acc_new], axis=1)
                out += [m_new, acc_new]
            return tuple(out)

        half = t // 2
        init = (jnp.full((1, t), NEG, F32), jnp.zeros((128, t), F32)) * 2
        carry = lax.fori_loop(0, i, lambda j, c: block(c, j * t, t, 0, t, False), init)
        carry = block(carry, i * t, half, 0, t, True)
        m0, a0, m1, a1 = block(carry, i * t + half, half, half, half, True)
        l0 = a0[HEAD:HEAD + 1, :]
        l1 = a1[0:1, :]
        o_t = jnp.where(_head_rows(0), a0 / l0, a1 / l1)
        o_ref[...] = o_t.T.astype(BF16)
        ot_ref[...] = o_t.astype(BF16)
        r8 = lax.broadcasted_iota(jnp.int32, (8, t), 0)
        lse_ref[0] = jnp.where(r8 == 0, m0 + jnp.log(l0), jnp.where(r8 == 1, m1 + jnp.log(l1), 0.0))

    return pl.pallas_call(
        body, name="fox_fwd", grid=(4, nq),
        in_specs=[pl.BlockSpec((t, 128), lambda hp, i: (i, hp)),
                  pl.BlockSpec((s, 128), lambda hp, i: (0, hp)),
                  pl.BlockSpec((128, s), lambda hp, i: (hp, 0)),
                  pl.BlockSpec((1, s, 128), lambda hp, i: (hp, 0, 0))],
        out_specs=[pl.BlockSpec((t, 128), lambda hp, i: (i, hp)),
                   pl.BlockSpec((1, 8, t), lambda hp, i: (hp, 0, i)),
                   pl.BlockSpec((128, t), lambda hp, i: (hp, i))],
        out_shape=[jax.ShapeDtypeStruct((s, 512), BF16), jax.ShapeDtypeStruct((4, 8, s), F32),
                   jax.ShapeDtypeStruct((512, s), BF16)],
        compiler_params=_params(("parallel", "parallel")),
    )(q, k, v_t, ca4)


MEM_SCALE = MEM_HEAD ** -0.5


def _mem_fwd(qm, mk, mv):
    s = qm.shape[0]
    tq = min(512, s)

    def body(q_ref, mk_ref, mv_ref, o_ref):
        for h in range(MEM_HEADS):
            hs = slice(h * 128, (h + 1) * 128)
            sc = _dot(q_ref[:, hs], mk_ref[:, hs], NT) * MEM_SCALE
            mx = jnp.max(sc, axis=-1, keepdims=True)
            p = jnp.exp(sc - mx)
            p = p / jnp.sum(p, axis=-1, keepdims=True)
            o_ref[:, hs] = _dot(p.astype(BF16), mv_ref[:, hs]).astype(BF16)

    return pl.pallas_call(
        body, name="mem_fwd", grid=(s // tq,),
        in_specs=[pl.BlockSpec((tq, 512), lambda i: (i, 0)), _full(mk.shape), _full(mv.shape)],
        out_specs=pl.BlockSpec((tq, 512), lambda i: (i, 0)),
        out_shape=jax.ShapeDtypeStruct((s, 512), BF16),
        compiler_params=_params(("parallel",)),
    )(qm, mk, mv)


def _merge_fwd(x, oa, of, om, proj, b_gate, wa, wf, wm, w_out, g_mlp):
    s = x.shape[0]
    tm = min(256, s)

    def body(x_ref, oa_ref, of_ref, om_ref, gl_ref, bg_ref, wa_ref, wf_ref, wm_ref, wo_ref, g_ref, x1_ref, hm_ref, mg_ref):
        merged = None
        for b, (o_ref, w_ref) in enumerate(((oa_ref, wa_ref), (of_ref, wf_ref), (om_ref, wm_ref))):
            cs = slice(b * D_MODEL, (b + 1) * D_MODEL)
            y = _dot(o_ref[...], w_ref[...])
            t = _sigmoid(gl_ref[:, cs].astype(F32) + bg_ref[:, cs]) * y
            merged = t if merged is None else merged + t
        mb = merged.astype(BF16)
        mg_ref[...] = mb
        x1 = x_ref[...] + _dot(mb, wo_ref[...])
        x1_ref[...] = x1
        r = lax.rsqrt(jnp.mean(x1 * x1, axis=-1, keepdims=True) + EPS)
        hm_ref[...] = (x1 * r * g_ref[...]).astype(BF16)

    row = lambda w: pl.BlockSpec((tm, w), lambda i: (i, 0))
    return pl.pallas_call(
        body, name="merge_fwd", grid=(s // tm,),
        in_specs=[row(D_MODEL), row(512), row(512), row(512), row(HALF_W), _full((1, HALF_W)),
                  _full(wa.shape), _full(wf.shape), _full(wm.shape), _full(w_out.shape), _full((1, D_MODEL))],
        out_specs=[row(D_MODEL), row(D_MODEL), row(D_MODEL)],
        out_shape=[jax.ShapeDtypeStruct((s, D_MODEL), F32), jax.ShapeDtypeStruct((s, D_MODEL), BF16),
                   jax.ShapeDtypeStruct((s, D_MODEL), BF16)],
        compiler_params=_params(("parallel",)),
    )(x, oa, of, om, proj, b_gate, wa, wf, wm, w_out, g_mlp)


def _mlp_up(hm, w_up):
    s = hm.shape[0]
    tm, tn = min(1024, s), w_up.shape[2]

    def body(h_ref, w_ref, u_ref):
        r = jnp.maximum(_dot(h_ref[...], w_ref[0]), 0.0)
        u_ref[...] = (r * r).astype(BF16)

    return pl.pallas_call(
        body, name="mlp_up", grid=(s // tm, D_FF // tn),
        in_specs=[pl.BlockSpec((tm, D_MODEL), lambda i, j: (i, 0)), pl.BlockSpec((1, D_MODEL, tn), lambda i, j: (j, 0, 0))],
        out_specs=pl.BlockSpec((tm, tn), lambda i, j: (i, j)),
        out_shape=jax.ShapeDtypeStruct((s, D_FF), BF16),
        compiler_params=_params(("parallel", "parallel")),
    )(hm, w_up)


def _mlp_down_loss(u, w_down, x1, target):
    s = u.shape[0]
    tm = min(256, s)

    def body(u_ref, w_ref, x1_ref, t_ref, dy_ref, dyb_ref, loss_ref):
        i = pl.program_id(0)

        @pl.when(i == 0)
        def _():
            loss_ref[...] = jnp.zeros_like(loss_ref)

        y = x1_ref[...] + _dot(u_ref[...], w_ref[...])
        err = y - t_ref[...]
        dy = err * (1.0 / D_MODEL)
        dy_ref[...] = dy
        dyb_ref[...] = dy.astype(BF16)
        part = jnp.sum(jnp.sum(err * err, axis=-1, keepdims=True) * (1.0 / D_MODEL), axis=0, keepdims=True)
        loss_ref[...] += 0.5 * part

    row = pl.BlockSpec((tm, D_MODEL), lambda i: (i, 0))
    return pl.pallas_call(
        body, name="mlp_down_loss", grid=(s // tm,),
        in_specs=[pl.BlockSpec((tm, D_FF), lambda i: (i, 0)), _full(w_down.shape), row, row],
        out_specs=[row, row, _full((1, 1))],
        out_shape=[jax.ShapeDtypeStruct((s, D_MODEL), F32), jax.ShapeDtypeStruct((s, D_MODEL), BF16),
                   jax.ShapeDtypeStruct((1, 1), F32)],
        compiler_params=_params(("arbitrary",)),
    )(u, w_down, x1, target)


def _mlp_bwd_act(dy, w_down, u):
    s = dy.shape[0]
    tm, tn = min(1024, s), 1024

    def body(dy_ref, w_ref, u_ref, da_ref):
        du = _dot(dy_ref[...], w_ref[...], NT)
        da_ref[...] = (du * (2.0 * jnp.sqrt(u_ref[...].astype(F32)))).astype(BF16)

    return pl.pallas_call(
        body, name="mlp_bwd_act", grid=(D_FF // tn, s // tm),
        in_specs=[pl.BlockSpec((tm, D_MODEL), lambda j, i: (i, 0)), pl.BlockSpec((tn, D_MODEL), lambda j, i: (j, 0)),
                  pl.BlockSpec((tm, tn), lambda j, i: (i, j))],
        out_specs=pl.BlockSpec((tm, tn), lambda j, i: (i, j)),
        out_shape=jax.ShapeDtypeStruct((s, D_FF), BF16),
        compiler_params=_params(("parallel", "parallel")),
    )(dy, w_down, u)


def _rms_bwd(xv, g, dh, skip):
    r = lax.rsqrt(jnp.mean(xv * xv, axis=-1, keepdims=True) + EPS)
    n = xv * r
    dn = dh * g
    dx = skip + r * (dn - n * jnp.mean(dn * n, axis=-1, keepdims=True))
    return dx, jnp.sum(dh * n, axis=0, keepdims=True)


def _mlp_bwd_x(da, w_up, x1, dy, g_mlp):
    s = da.shape[0]
    tm = min(256, s)

    def body(da_ref, w_ref, x1_ref, dy_ref, g_ref, dx1_ref, dg_ref):
        i = pl.program_id(0)

        @pl.when(i == 0)
        def _():
            dg_ref[...] = jnp.zeros_like(dg_ref)

        tn = w_ref.shape[2]
        dhm = _dot(da_ref[:, 0:tn], w_ref[0], NT)
        for j in range(1, N_DEV):
            dhm = dhm + _dot(da_ref[:, j * tn:(j + 1) * tn], w_ref[j], NT)
        dx, dg = _rms_bwd(x1_ref[...], g_ref[...], dhm, dy_ref[...])
        dx1_ref[...] = dx
        dg_ref[...] += dg

    row = pl.BlockSpec((tm, D_MODEL), lambda i: (i, 0))
    return pl.pallas_call(
        body, name="mlp_bwd_x", grid=(s // tm,),
        in_specs=[pl.BlockSpec((tm, D_FF), lambda i: (i, 0)), _full(w_up.shape), row, row, _full((1, D_MODEL))],
        out_specs=[row, _full((1, D_MODEL))],
        out_shape=[jax.ShapeDtypeStruct((s, D_MODEL), F32), jax.ShapeDtypeStruct((1, D_MODEL), F32)],
        compiler_params=_params(("arbitrary",)),
    )(da, w_up, x1, dy, g_mlp)


def _merge_bwd(dx1, oa, of, om, proj, b_gate, wa, wf, wm, w_out):
    s = dx1.shape[0]
    tm = min(256, s)

    def body(dx1_ref, oa_ref, of_ref, om_ref, gl_ref, bg_ref, wa_ref, wf_ref, wm_ref, wo_ref,
             dp_ref, doa_ref, dof_ref, dom_ref, dya_ref, dyf_ref, dym_ref, dbg_ref):
        i = pl.program_id(0)

        @pl.when(i == 0)
        def _():
            dbg_ref[...] = jnp.zeros_like(dbg_ref)

        dmerged = _dot(dx1_ref[...].astype(BF16), wo_ref[...], NT)
        branches = ((oa_ref, wa_ref, doa_ref, dya_ref), (of_ref, wf_ref, dof_ref, dyf_ref), (om_ref, wm_ref, dom_ref, dym_ref))
        for b, (o_ref, w_ref, do_ref, dyb_ref) in enumerate(branches):
            cs = slice(b * D_MODEL, (b + 1) * D_MODEL)
            y = _dot(o_ref[...], w_ref[...])
            g = _sigmoid(gl_ref[:, cs].astype(F32) + bg_ref[:, cs])
            dz = (dmerged * y) * g * (1.0 - g)
            dp_ref[:, cs] = dz.astype(BF16)
            dbg_ref[:, cs] += jnp.sum(dz, axis=0, keepdims=True)
            dyb = (dmerged * g).astype(BF16)
            dyb_ref[...] = dyb
            do = _dot(dyb, w_ref[...], NT)
            do_ref[...] = (do.T if b == 1 else do).astype(BF16)

    row = lambda w: pl.BlockSpec((tm, w), lambda i: (i, 0))
    sd = lambda w: jax.ShapeDtypeStruct((s, w), BF16)
    return pl.pallas_call(
        body, name="merge_bwd", grid=(s // tm,),
        in_specs=[row(D_MODEL), row(512), row(512), row(512), row(HALF_W), _full((1, HALF_W)),
                  _full(wa.shape), _full(wf.shape), _full(wm.shape), _full(w_out.shape)],
        out_specs=[row(HALF_W), row(512), pl.BlockSpec((512, tm), lambda i: (0, i)), row(512),
                   row(D_MODEL), row(D_MODEL), row(D_MODEL), _full((1, HALF_W))],
        out_shape=[sd(PROJ_W), sd(512), jax.ShapeDtypeStruct((512, s), BF16), sd(512), sd(D_MODEL), sd(D_MODEL), sd(D_MODEL),
                   jax.ShapeDtypeStruct((1, HALF_W), F32)],
        compiler_params=_params(("arbitrary",)),
    )(dx1, oa, of, om, proj, b_gate, wa, wf, wm, w_out)


def _swa_valid_t(n):
    key = lax.broadcasted_iota(jnp.int32, (2 * SWA_BLOCK, SWA_BLOCK), 0)
    qry = lax.broadcasted_iota(jnp.int32, (2 * SWA_BLOCK, SWA_BLOCK), 1)
    dist = qry + SWA_BLOCK - key
    return (dist >= 0) & (dist < SWA_BLOCK) & ((key >= SWA_BLOCK) | (n > 0))


def _swa_bwd(qa, kp, vp, bias_t, sink, doa):
    s = qa.shape[0]
    nb = s // SWA_BLOCK

    def body(sink_ref, q_ref, kp_ref, vp_ref, bias_ref, do_ref, dq_ref, dkp_ref, dvp_ref, dbias_ref, dsink_ref, sk_acc):
        n = pl.program_id(0)

        @pl.when(n == 0)
        def _():
            dkp_ref[...] = jnp.zeros_like(dkp_ref)
            dvp_ref[...] = jnp.zeros_like(dvp_ref)
            dbias_ref[...] = jnp.zeros_like(dbias_ref)
            sk_acc[...] = jnp.zeros_like(sk_acc)

        start = pl.multiple_of(n * SWA_BLOCK, SWA_BLOCK)
        win = pl.ds(start, 2 * SWA_BLOCK)
        k2 = kp_ref[win, :]
        v2 = vp_ref[win, :]
        valid = _swa_valid_t(n)
        heads = range(SWA_HEADS)
        hs = lambda h: slice(h * HEAD, (h + 1) * HEAD)
        scale = jnp.asarray(0.125, BF16)
        q = [q_ref[:, hs(h)] for h in heads]
        do = [do_ref[:, hs(h)] for h in heads]
        kk = [k2[:, hs(kv)] for kv in range(2)]
        vv = [v2[:, hs(kv)] for kv in range(2)]
        kt = [(kk[kv].astype(F32) * 0.125).T.astype(BF16) for kv in range(2)]
        st = [jnp.where(valid, _dot(kk[h // 4], q[h], NT) * 0.125 + bias_ref[h], NEG) for h in heads]
        dpt = [_dot(vv[h // 4], do[h], NT) for h in heads]
        pt, dst = [], []
        for h in heads:
            sk = sink_ref[h]
            mx = jnp.maximum(jnp.max(st[h], axis=0, keepdims=True), sk)
            p = jnp.exp(st[h] - mx)
            esk = jnp.exp(sk - mx)
            den = jnp.sum(p, axis=0, keepdims=True) + esk
            p = p / den
            delta = jnp.sum(p * dpt[h], axis=0, keepdims=True)
            d = p * (dpt[h] - delta)
            sk_acc[h:h + 1, :] += -(esk / den) * delta
            dbias_ref[h] += d
            pt.append(p.astype(BF16))
            dst.append(d.astype(BF16))
        dq_t = [_dot(kt[h // 4], dst[h]) for h in heads]
        dq_ref[...] = jnp.concatenate(dq_t, axis=0).T.astype(BF16)
        for kv in range(2):
            group = range(4 * kv, 4 * kv + 4)
            dk = [_dot(dst[h], q[h] * scale) for h in group]
            dv = [_dot(pt[h], do[h]) for h in group]
            dkp_ref[win, hs(kv)] += (dk[0] + dk[1]) + (dk[2] + dk[3])
            dvp_ref[win, hs(kv)] += (dv[0] + dv[1]) + (dv[2] + dv[3])

        @pl.when(n == nb - 1)
        def _():
            dsink_ref[...] = jnp.broadcast_to(jnp.sum(sk_acc[...], axis=1, keepdims=True), dsink_ref.shape)

    return pl.pallas_call(
        body, name="swa_bwd", grid=(nb,),
        in_specs=[pl.BlockSpec(memory_space=pltpu.SMEM),
                  pl.BlockSpec((SWA_BLOCK, 512), lambda n: (n, 0)),
                  _full(kp.shape), _full(vp.shape), _full(bias_t.shape),
                  pl.BlockSpec((SWA_BLOCK, 512), lambda n: (n, 0))],
        out_specs=[pl.BlockSpec((SWA_BLOCK, 512), lambda n: (n, 0)), _full(kp.shape), _full(vp.shape),
                   _full(bias_t.shape), _full((SWA_HEADS, 128))],
        out_shape=[jax.ShapeDtypeStruct((s, 512), BF16), jax.ShapeDtypeStruct(kp.shape, F32),
                   jax.ShapeDtypeStruct(vp.shape, F32), jax.ShapeDtypeStruct(bias_t.shape, F32),
                   jax.ShapeDtypeStruct((SWA_HEADS, 128), F32)],
        scratch_shapes=[pltpu.VMEM((SWA_HEADS, 128), F32)],
        compiler_params=_params(("arbitrary",)),
    )(sink, qa, kp, vp, bias_t, doa)


def _fox_bwd(qt, k, v, dot, ot, cc4, lse4):
    s = k.shape[0]
    t = min(FOX_BWD_T, s)
    nq = s // t

    def body(qt_ref, k_ref, v_ref, dot_ref, ot_ref, cc_ref, lse_ref,
             dqt_ref, dk_ref, dv_ref, dck_ref, dcq_ref, delta_ref, dkt_acc, dvt_acc, ds0, ds1):
        j = pl.program_id(1)

        @pl.when(j == 0)
        def _():
            dqt_ref[...] = jnp.zeros_like(dqt_ref)
            dcq_ref[...] = jnp.zeros_like(dcq_ref)
            r8 = lax.broadcasted_iota(jnp.int32, (8, t), 0)

            def dl(i, c):
                cols = pl.ds(pl.multiple_of(i * t, t), t)
                pr = dot_ref[:, cols].astype(F32) * ot_ref[:, cols].astype(F32)
                d0 = jnp.sum(jnp.where(_head_rows(0), pr, 0.0), axis=0, keepdims=True)
                d1 = jnp.sum(jnp.where(_head_rows(1), pr, 0.0), axis=0, keepdims=True)
                delta_ref[:, cols] = jnp.where(r8 == 0, d0, jnp.where(r8 == 1, d1, 0.0))
                return c

            lax.fori_loop(0, nq, dl, 0)

        kj = k_ref[...]
        vj = v_ref[...]
        ks = pl.ds(pl.multiple_of(j * t, t), t)
        kt = (kj.astype(F32) * 0.125).T.astype(BF16)
        ke = [jnp.where(_head_mask(e), kj, jnp.zeros_like(kj)) for e in range(2)]
        ve = [jnp.where(_head_mask(e), vj, jnp.zeros_like(vj)) for e in range(2)]
        ck = [cc_ref[0, ks, e:e + 1] for e in range(2)]
        for r in (dkt_acc, dvt_acc, ds0, ds1):
            r[...] = jnp.zeros_like(r)

        def block(q0, nqs, k0, nks, masked):
            cols = pl.ds(pl.multiple_of(q0, 128), nqs)
            rows = slice(k0, k0 + nks)
            qti = qt_ref[:, cols]
            doti = dot_ref[:, cols]
            for e, ds_acc in enumerate((ds0, ds1)):
                dims = slice(e * HEAD, (e + 1) * HEAD)
                st = _dot(ke[e][rows, :], qti) - ck[e][rows, :]
                if masked:
                    krow = lax.broadcasted_iota(jnp.int32, (nks, nqs), 0) + (j * t + k0)
                    qcol = lax.broadcasted_iota(jnp.int32, (nks, nqs), 1) + q0
                    st = jnp.where(krow <= qcol, st, NEG)
                pt = jnp.exp(st - lse_ref[0, e:e + 1, cols])
                dpt = _dot(ve[e][rows, :], doti)
                dst = pt * (dpt - delta_ref[e:e + 1, cols])
                dsb = dst.astype(BF16)
                dvt_acc[dims, rows] += _dot(doti[dims, :], pt.astype(BF16), NT)
                dkt_acc[dims, rows] += _dot(qti[dims, :], dsb, NT)
                dqt_ref[dims, cols] += _dot(kt[dims, rows], dsb)
                ds_acc[rows, 0:nqs] += dst
                dcq_ref[0, e:e + 1, cols] += jnp.sum(dst, axis=0, keepdims=True)

        half = t // 2
        block(j * t, half, 0, half, True)
        block(j * t + half, half, 0, t, True)

        def rest(i, c):
            block(i * t, t, 0, t, False)
            return c

        lax.fori_loop(j + 1, nq, rest, 0)
        dk_ref[...] = dkt_acc[...].T.astype(BF16)
        dv_ref[...] = dvt_acc[...].T.astype(BF16)
        lane = lax.broadcasted_iota(jnp.int32, (t, 128), 1)
        c0 = jnp.sum(ds0[...], axis=-1, keepdims=True)
        c1 = jnp.sum(ds1[...], axis=-1, keepdims=True)
        dck_ref[0] = jnp.where(lane == 0, c0, jnp.where(lane == 1, c1, 0.0))

    res_t = lambda: pl.BlockSpec((128, s), lambda hp, j: (hp, 0))
    blk = lambda: pl.BlockSpec((t, 128), lambda hp, j: (j, hp))
    return pl.pallas_call(
        body, name="fox_bwd", grid=(4, nq),
        in_specs=[res_t(), blk(), blk(), res_t(), res_t(), pl.BlockSpec((1, s, 128), lambda hp, j: (hp, 0, 0)),
                  pl.BlockSpec((1, 8, s), lambda hp, j: (hp, 0, 0))],
        out_specs=[res_t(), blk(), blk(),
                   pl.BlockSpec((1, t, 128), lambda hp, j: (hp, j, 0)),
                   pl.BlockSpec((1, 8, s), lambda hp, j: (hp, 0, 0))],
        out_shape=[jax.ShapeDtypeStruct((512, s), F32), jax.ShapeDtypeStruct((s, 512), BF16),
                   jax.ShapeDtypeStruct((s, 512), BF16), jax.ShapeDtypeStruct((4, s, 128), F32),
                   jax.ShapeDtypeStruct((4, 8, s), F32)],
        scratch_shapes=[pltpu.VMEM((8, s), F32)] + [pltpu.VMEM((128, t), F32)] * 2 + [pltpu.VMEM((t, t), F32)] * 2,
        compiler_params=_params(("arbitrary", "arbitrary")),
    )(qt, k, v, dot, ot, cc4, lse4)


def _mem_bwd(qm, mk, mv, dom):
    s = qm.shape[0]
    tq = min(512, s)

    def body(q_ref, mk_ref, mv_ref, do_ref, dq_ref, dmk_ref, dmv_ref):
        i = pl.program_id(0)

        @pl.when(i == 0)
        def _():
            dmk_ref[...] = jnp.zeros_like(dmk_ref)
            dmv_ref[...] = jnp.zeros_like(dmv_ref)

        heads = range(MEM_HEADS)
        hs = lambda h: slice(h * 128, (h + 1) * 128)
        sc = [_dot(q_ref[:, hs(h)], mk_ref[:, hs(h)], NT) * MEM_SCALE for h in heads]
        dp = [_dot(do_ref[:, hs(h)], mv_ref[:, hs(h)], NT) for h in heads]
        pb, dsb = [], []
        for h in heads:
            p = jnp.exp(sc[h] - jnp.max(sc[h], axis=-1, keepdims=True))
            p = p / jnp.sum(p, axis=-1, keepdims=True)
            ds = p * (dp[h] - jnp.sum(p * dp[h], axis=-1, keepdims=True))
            pb.append(p.astype(BF16))
            dsb.append((ds * MEM_SCALE).astype(BF16))
        dq = [_dot(dsb[h], mk_ref[:, hs(h)]).astype(BF16) for h in heads]
        dmk = [_dot(dsb[h], q_ref[:, hs(h)], TN) for h in heads]
        dmv = [_dot(pb[h], do_ref[:, hs(h)], TN) for h in heads]
        for h in heads:
            dq_ref[:, hs(h)] = dq[h]
            dmk_ref[:, hs(h)] += dmk[h]
            dmv_ref[:, hs(h)] += dmv[h]

    return pl.pallas_call(
        body, name="mem_bwd", grid=(s // tq,),
        in_specs=[pl.BlockSpec((tq, 512), lambda i: (i, 0)), _full(mk.shape), _full(mv.shape),
                  pl.BlockSpec((tq, 512), lambda i: (i, 0))],
        out_specs=[pl.BlockSpec((tq, 512), lambda i: (i, 0)), _full(mk.shape), _full(mv.shape)],
        out_shape=[jax.ShapeDtypeStruct((s, 512), BF16), jax.ShapeDtypeStruct(mk.shape, F32),
                   jax.ShapeDtypeStruct(mv.shape, F32)],
        compiler_params=_params(("arbitrary",)),
    )(qm, mk, mv, dom)


def _memkv_bwd(dmk, dmv, kv_raw, kn_mem, mem, g_mem, mem_n, w_kv):
    def body(dmk_ref, dmv_ref, kv_ref, kn_ref, mem_ref, g_ref, mn_ref, w_ref, dw_ref, dkn_ref, dg_ref, dkv_ref):
        dkn = jnp.zeros((1, 128), F32)
        for h in range(MEM_HEADS):
            hs = slice(h * 128, (h + 1) * 128)
            v = kv_ref[:, hs]
            r = lax.rsqrt(jnp.mean(v * v, axis=-1, keepdims=True) + EPS)
            n = v * r
            dn = dmk_ref[:, hs]
            dkn = dkn + jnp.sum(dn * n, axis=0, keepdims=True)
            dng = dn * kn_ref[...]
            dkv_ref[:, hs] = (r * (dng - n * jnp.mean(dng * n, axis=-1, keepdims=True))).astype(BF16)
        dkv_ref[:, 512:1024] = dmv_ref[...].astype(BF16)
        dkn_ref[...] = dkn
        dkv = dkv_ref[...]
        dw_ref[...] = _dot(mn_ref[...], dkv, TN).astype(BF16)
        dmn = _dot(dkv, w_ref[...], NT)
        xv = mem_ref[...]
        r = lax.rsqrt(jnp.mean(xv * xv, axis=-1, keepdims=True) + EPS)
        dg_ref[...] = jnp.sum(dmn * (xv * r), axis=0, keepdims=True)

    m = mem.shape[0]
    return pl.pallas_call(
        body, name="memkv_bwd",
        out_shape=[jax.ShapeDtypeStruct((D_MODEL, 1024), BF16), jax.ShapeDtypeStruct((1, 128), F32),
                   jax.ShapeDtypeStruct((1, D_MODEL), F32)],
        scratch_shapes=[pltpu.VMEM((m, 1024), BF16)],
        compiler_params=pltpu.CompilerParams(vmem_limit_bytes=VMEM_LIMIT),
    )(dmk, dmv, kv_raw, kn_mem, mem, g_mem, mem_n, w_kv)


def _fox_gate_bwd(dcq4, dck4, proj, b_forget128):
    s = dck4.shape[1]
    tm = min(512, s)
    nt = s // tm

    def body(dcq_ref, dck_ref, p_ref, b_ref, dfl_ref, db_ref, carry_ref):
        i = pl.program_id(0)

        @pl.when(i == 0)
        def _():
            carry_ref[...] = jnp.zeros_like(carry_ref)
            db_ref[...] = jnp.zeros_like(db_ref)

        dcv = jnp.zeros((tm, 128), F32)
        for hp in range(4):
            by_query = jnp.concatenate([dcq_ref[hp], jnp.zeros((120, tm), F32)], axis=0).T
            d = by_query - dck_ref[hp]
            dcv = dcv + (d if hp == 0 else pltpu.roll(d, 2 * hp, 1))
        dlogf = jnp.dot(_tri(tm, False), dcv, precision=lax.Precision.HIGHEST, preferred_element_type=F32) + carry_ref[...]
        carry_ref[...] += jnp.sum(dcv, axis=0, keepdims=True)
        z = p_ref[...] + b_ref[...]
        dfl = dlogf * (1.0 / (1.0 + jnp.exp(z)))
        dfl_ref[...] = dfl.astype(BF16)
        db_ref[...] += jnp.sum(dfl, axis=0, keepdims=True)

    return pl.pallas_call(
        body, name="fox_gate_bwd", grid=(nt,),
        in_specs=[pl.BlockSpec((4, 8, tm), lambda i: (0, 0, nt - 1 - i)),
                  pl.BlockSpec((4, tm, 128), lambda i: (0, nt - 1 - i, 0)),
                  pl.BlockSpec((tm, 128), lambda i: (nt - 1 - i, 0)), _full((1, 128))],
        out_specs=[pl.BlockSpec((tm, 128), lambda i: (nt - 1 - i, 0)), _full((1, 128))],
        out_shape=[jax.ShapeDtypeStruct((s, 128), BF16), jax.ShapeDtypeStruct((1, 128), F32)],
        scratch_shapes=[pltpu.VMEM((1, 128), F32)],
        compiler_params=_params(("arbitrary",)),
    )(dcq4, dck4, proj, b_forget128)


def _proj_pre_bwd(dproj, proj, dqf, dkf, dvf, dqm, dqa, dka, dva, dfl, gq_fox, gk_fox, gq_mem, gq_swa, gk_swa):
    s = proj.shape[0]
    tm = min(256, s)

    def body(dp_in, p_ref, dqf_ref, dkf_ref, dvf_ref, dqm_ref, dqa_ref, dka_ref, dva_ref, dfl_ref,
             gqf, gkf, gqm, gqa, gka, dp_ref, dgn_ref):
        i = pl.program_id(0)

        @pl.when(i == 0)
        def _():
            dgn_ref[...] = jnp.zeros_like(dgn_ref)

        def norm_bwd(off, width, hd, g_ref, dn_ref, slot):
            acc = jnp.zeros((1, 128), F32)
            for b in range(width // 128):
                v = p_ref[:, off + b * 128: off + (b + 1) * 128].astype(F32)
                r = lax.rsqrt(_group_mean(v * v, hd) + EPS)
                n = v * r
                dn = dn_ref[b * 128:(b + 1) * 128, :].T if slot == 0 else dn_ref[:, b * 128:(b + 1) * 128].astype(F32)
                acc = acc + jnp.sum(dn * n, axis=0, keepdims=True)
                dng = dn * g_ref[...]
                dp_ref[:, off + b * 128: off + (b + 1) * 128] = (r * (dng - n * _group_mean(dng * n, hd))).astype(BF16)
            dgn_ref[slot:slot + 1, :] += acc

        norm_bwd(H_QF, 512, HEAD, gqf, dqf_ref, 0)
        norm_bwd(H_KF, 512, HEAD, gkf, dkf_ref, 1)
        dp_ref[:, H_VF:H_VF + 512] = dvf_ref[...].astype(BF16)
        norm_bwd(H_QM, 512, MEM_HEAD, gqm, dqm_ref, 2)
        norm_bwd(H_QA, 512, HEAD, gqa, dqa_ref, 3)
        norm_bwd(H_KA, 128, HEAD, gka, dka_ref, 4)
        dp_ref[:, H_VA:H_VA + 128] = dva_ref[...].astype(BF16)
        dp_ref[:, H_FL:H_FL + 128] = dfl_ref[...]
        dp_ref[:, H_FL + 128:HALF_W] = jnp.zeros((tm, HALF_W - H_FL - 128), BF16)

    row = lambda w: pl.BlockSpec((tm, w), lambda i: (i, 0))
    g_spec = _full((1, 128))
    return pl.pallas_call(
        body, name="proj_pre_bwd", grid=(s // tm,),
        in_specs=[pl.BlockSpec(memory_space=pl.ANY), pl.BlockSpec((tm, HALF_W), lambda i: (i, 1)),
                  pl.BlockSpec((512, tm), lambda i: (0, i)), row(512), row(512), row(512), row(512),
                  row(128), row(128), row(128), g_spec, g_spec, g_spec, g_spec, g_spec],
        out_specs=[pl.BlockSpec((tm, HALF_W), lambda i: (i, 1)), _full((8, 128))],
        out_shape=[jax.ShapeDtypeStruct((s, PROJ_W), BF16), jax.ShapeDtypeStruct((8, 128), F32)],
        input_output_aliases={0: 0},
        compiler_params=_params(("arbitrary",)),
    )(dproj, proj, dqf, dkf, dvf, dqm, dqa, dka, dva, dfl, gq_fox, gk_fox, gq_mem, gq_swa, gk_swa)


def _in_bwd_x(dproj, w_in_p, x, g_mix, dx1):
    s = x.shape[0]
    tm = min(256, s)

    def body(dp_ref, w_ref, x_ref, g_ref, dx1_ref, gx_ref, dg_ref):
        i = pl.program_id(0)

        @pl.when(i == 0)
        def _():
            dg_ref[...] = jnp.zeros_like(dg_ref)

        dx, dg = _rms_bwd(x_ref[...], g_ref[...], _dot(dp_ref[...], w_ref[...], NT), dx1_ref[...])
        gx_ref[...] = dx
        dg_ref[...] += dg

    row = pl.BlockSpec((tm, D_MODEL), lambda i: (i, 0))
    return pl.pallas_call(
        body, name="in_bwd_x", grid=(s // tm,),
        in_specs=[pl.BlockSpec((tm, PROJ_W), lambda i: (i, 0)), _full(w_in_p.shape), row, _full((1, D_MODEL)), row],
        out_specs=[row, _full((1, D_MODEL))],
        out_shape=[jax.ShapeDtypeStruct((s, D_MODEL), F32), jax.ShapeDtypeStruct((1, D_MODEL), F32)],
        compiler_params=_params(("arbitrary",)),
    )(dproj, w_in_p, x, g_mix, dx1)


def _rel_bias_bwd(dbias, bucket):
    def body(db_ref, bk_ref, o_ref):
        bk = bk_ref[...]
        lane = lax.broadcasted_iota(jnp.int32, (1, 128), 1)
        for b in range(REL_BUCKETS):
            sel = bk == b
            acc = jnp.zeros((1, 128), F32)
            for h in range(SWA_HEADS):
                tot = jnp.sum(jnp.sum(jnp.where(sel, db_ref[h], 0.0), axis=-1, keepdims=True), axis=0, keepdims=True)
                acc = jnp.where(lane == h, tot, acc)
            o_ref[:, b * 128:(b + 1) * 128] = acc

    return pl.pallas_call(
        body, name="rel_bias_bwd",
        out_shape=jax.ShapeDtypeStruct((1, REL_BUCKETS * 128), F32),
        compiler_params=pltpu.CompilerParams(vmem_limit_bytes=VMEM_LIMIT),
    )(dbias, bucket)


def _my_place():
    return lax.axis_index("x"), lax.axis_index("y"), lax.axis_index("c")


def _peer(place, k):
    x, y, c = place
    return (1 - x if k & 4 else x, 1 - y if k & 2 else y, 1 - c if k & 1 else c)


def _index(place):
    x, y, c = place
    return 4 * x + 2 * y + c


HBM_SPEC = pl.BlockSpec(memory_space=pltpu.HBM)
SEM_SPEC = pl.BlockSpec(memory_space=pltpu.SEMAPHORE)
DATAFLOW = pltpu.SideEffectType.DATAFLOW_SIDE_EFFECTING


ALL_PEERS = tuple(range(1, N_DEV))
SAME_CORE = (2, 4, 6)
OWN = N_DEV - 1


def _split_copy(src_ref, land_ref, send_sems, recv_sems, me, k, gather):
    peer = _peer(me, k)
    if gather:
        src, dst = src_ref, land_ref.at[_index(me)]
    else:
        src, dst = src_ref.at[_index(peer)], land_ref.at[k - 1]
    return pltpu.make_async_remote_copy(src_ref=src, dst_ref=dst, send_sem=send_sems.at[k - 1], recv_sem=recv_sems.at[k - 1],
                                        device_id=peer, device_id_type=MESH)


def _own_copy(src_ref, land_ref, recv_sems, me, gather):
    if gather:
        src, dst = src_ref, land_ref.at[_index(me)]
    else:
        src, dst = src_ref.at[_index(me)], land_ref.at[OWN]
    return pltpu.make_async_copy(src, dst, recv_sems.at[OWN])


def _split_start(srcs, gather, name, peers=ALL_PEERS, after=None):
    n = len(srcs)
    extra = [] if after is None else [after]

    def body(*refs):
        refs = refs[:2 * n] + refs[2 * n + len(extra):]
        src_refs, land_refs = refs[:n], refs[n:2 * n]
        send_sems, recv_sems, token = refs[2 * n:3 * n], refs[3 * n:4 * n], refs[-1]
        me = _my_place()
        for w in range(n):
            for k in peers:
                _split_copy(src_refs[w], land_refs[w], send_sems[w], recv_sems[w], me, k, gather).start()
            _own_copy(src_refs[w], land_refs[w], recv_sems[w], me, gather).start()
        token[...] = jnp.zeros_like(token)

    lands = [lax.empty((N_DEV,) + (a.shape if gather else a.shape[1:]), a.dtype) for a in srcs]
    sems = [pltpu.SemaphoreType.DMA((N_DEV,))] * (2 * n)
    hbm = [pltpu.HBM(a.shape, a.dtype) for a in list(srcs) + lands]
    outs = pl.pallas_call(
        body, name=name,
        out_shape=(*sems, *hbm, jax.ShapeDtypeStruct((8, 128), F32)),
        in_specs=(HBM_SPEC,) * (2 * n) + (pl.BlockSpec(memory_space=pl.ANY),) * len(extra),
        out_specs=(SEM_SPEC,) * (2 * n) + (HBM_SPEC,) * (2 * n) + (pl.BlockSpec(memory_space=pltpu.VMEM),),
        input_output_aliases={i: 2 * n + i for i in range(2 * n)},
        compiler_params=pltpu.CompilerParams(has_side_effects=DATAFLOW),
    )(*[pltpu.with_memory_space_constraint(a, pltpu.HBM) for a in list(srcs) + lands], *extra)
    return list(outs[:n]), list(outs[n:2 * n]), list(outs[2 * n:3 * n]), list(outs[3 * n:4 * n]), outs[-1]


def _split_wait(started, w, after, gather, name):
    send_sems, recv_sems, srcs, lands, _ = started

    def body(src_ref, land_ref, send_sems, recv_sems, after_ref, src_out, land_out):
        me = _my_place()
        for k in ALL_PEERS:
            cp = _split_copy(src_ref, land_ref, send_sems, recv_sems, me, k, gather)
            cp.wait_send()
            cp.wait_recv()
        _own_copy(src_ref, land_ref, recv_sems, me, gather).wait()

    return pl.pallas_call(
        body, name=name,
        out_shape=(pltpu.HBM(srcs[w].shape, srcs[w].dtype), pltpu.HBM(lands[w].shape, lands[w].dtype)),
        in_specs=(HBM_SPEC, HBM_SPEC, SEM_SPEC, SEM_SPEC, pl.BlockSpec(memory_space=pl.ANY)),
        out_specs=(HBM_SPEC, HBM_SPEC), input_output_aliases={0: 0, 1: 1},
        compiler_params=pltpu.CompilerParams(has_side_effects=DATAFLOW),
    )(srcs[w], lands[w], send_sems[w], recv_sems[w], after)[1]


def _forward_copy(land_ref, send_sems, recv_sems, me, j, incoming):
    sibling = _peer(me, 1)
    rows = land_ref.at[_index(_peer(sibling if incoming else me, SAME_CORE[j]))]
    return pltpu.make_async_remote_copy(src_ref=rows, dst_ref=rows, send_sem=send_sems.at[j], recv_sem=recv_sems.at[j],
                                        device_id=sibling, device_id_type=MESH)


def _forward_start(started, after, name):
    send_a, recv_a, srcs, lands, _ = started

    def body(src_ref, land_ref, send_a, recv_a, after_ref, send_b, recv_b, src_out, land_out):
        me = _my_place()
        for j, k in enumerate(SAME_CORE):
            _split_copy(src_ref, land_ref, send_a, recv_a, me, k, True).wait_recv()
            _forward_copy(land_ref, send_b, recv_b, me, j, False).start()

    sems = pltpu.SemaphoreType.DMA((len(SAME_CORE),))
    return pl.pallas_call(
        body, name=name,
        out_shape=(sems, sems, pltpu.HBM(srcs[0].shape, srcs[0].dtype), pltpu.HBM(lands[0].shape, lands[0].dtype)),
        in_specs=(HBM_SPEC, HBM_SPEC, SEM_SPEC, SEM_SPEC, pl.BlockSpec(memory_space=pl.ANY)),
        out_specs=(SEM_SPEC, SEM_SPEC, HBM_SPEC, HBM_SPEC), input_output_aliases={0: 2, 1: 3},
        compiler_params=pltpu.CompilerParams(has_side_effects=DATAFLOW),
    )(srcs[0], lands[0], send_a[0], recv_a[0], after)


def _forward_wait(started, forwarded, name):
    send_a, recv_a, _, _, _ = started
    send_b, recv_b, src, land = forwarded

    def body(src_ref, land_ref, send_a, recv_a, send_b, recv_b, src_out, land_out):
        me = _my_place()
        _own_copy(src_ref, land_ref, recv_a, me, True).wait()
        for k in (1,) + SAME_CORE:
            _split_copy(src_ref, land_ref, send_a, recv_a, me, k, True).wait_send()
        _split_copy(src_ref, land_ref, send_a, recv_a, me, 1, True).wait_recv()
        for j in range(len(SAME_CORE)):
            _forward_copy(land_ref, send_b, recv_b, me, j, False).wait_send()
            _forward_copy(land_ref, send_b, recv_b, me, j, True).wait_recv()

    return pl.pallas_call(
        body, name=name,
        out_shape=(pltpu.HBM(src.shape, src.dtype), pltpu.HBM(land.shape, land.dtype)),
        in_specs=(HBM_SPEC, HBM_SPEC, SEM_SPEC, SEM_SPEC, SEM_SPEC, SEM_SPEC),
        out_specs=(HBM_SPEC, HBM_SPEC), input_output_aliases={0: 0, 1: 1},
        compiler_params=pltpu.CompilerParams(has_side_effects=DATAFLOW),
    )(src, land, send_a[0], recv_a[0], send_b, recv_b)[1]


def _adam_math(w, g, m, v):
    m2 = ADAM_B1 * m + (1.0 - ADAM_B1) * g
    v2 = ADAM_B2 * v + (1.0 - ADAM_B2) * (g * g)
    m_hat = m2 / (1.0 - ADAM_B1 ** ADAM_STEP)
    v_hat = v2 / (1.0 - ADAM_B2 ** ADAM_STEP)
    delta = -ADAM_LR * (m_hat / (jnp.sqrt(v_hat) + ADAM_EPS) + ADAM_WD * w)
    return delta, m2, v2


def _adamw(land, w, m, v, name):
    a, b = w.shape
    bp = land.shape[2]
    ta = min(128, a)

    def body(p_ref, w_ref, m_ref, v_ref, g_ref, d_ref, m2_ref, v2_ref):
        g = p_ref[0, :, 0:b].astype(F32)
        for k in range(1, N_DEV):
            g = g + p_ref[k, :, 0:b].astype(F32)
        delta, m2, v2 = _adam_math(w_ref[...], g, m_ref[...], v_ref[...])
        g_ref[...] = g
        d_ref[...] = delta
        m2_ref[...] = m2
        v2_ref[...] = v2

    blk = pl.BlockSpec((ta, b), lambda i: (i, 0))
    sd = jax.ShapeDtypeStruct((a, b), F32)
    return pl.pallas_call(
        body, name=name, grid=(a // ta,),
        in_specs=[pl.BlockSpec((N_DEV, ta, bp), lambda i: (0, i, 0)), blk, blk, blk],
        out_specs=[blk, blk, blk, blk], out_shape=[sd, sd, sd, sd],
        compiler_params=_params(("parallel",)),
    )(land, w, m, v)


def _bucket_table():
    t_loc = jnp.arange(SWA_BLOCK)[:, None] + SWA_BLOCK
    s_loc = jnp.arange(2 * SWA_BLOCK)[None, :]
    dist = t_loc - s_loc
    max_exact = REL_BUCKETS // 2
    d = jnp.maximum(dist, 0)
    df = jnp.maximum(d, 1).astype(F32)
    large = max_exact + (jnp.log(df / max_exact) / math.log(REL_MAX_DIST / max_exact) * (REL_BUCKETS - max_exact)).astype(jnp.int32)
    large = jnp.minimum(large, REL_BUCKETS - 1)
    bucket = jnp.where(d < max_exact, d, large)
    band = (dist >= 0) & (dist < SWA_BLOCK)
    return bucket, band


def _tile2(g):
    return jnp.concatenate([g, g], axis=1) if g.shape[1] == HEAD else g


SHARD_W = 737
SHARD_WP = 768
IN_WIDTH = N_DEV * SHARD_W
SEGMENTS = ((GL0, 2824, 3072), (QF0, 768, 512), (KF0, 1280, 512), (VF0, 1792, 512), (QM0, 2312, 512),
            (QA0, 0, 512), (KA0, 512, 128), (VA0, 640, 128), (FL0, 2304, 8))


def _lane_plan(sources):
    plan = []
    for t in range(len(sources) // 128):
        groups = {}
        for lane in range(128):
            src = sources[128 * t + lane]
            if src is not None:
                slab, col = src
                groups.setdefault((slab, col // 128, (lane - col) % 128), []).append(lane)
        tile = []
        for key, lanes in groups.items():
            assert lanes == list(range(lanes[0], lanes[-1] + 1))
            tile.append((key, lanes[0], lanes[-1] + 1))
        plan.append(tile)
    return plan


def _assemble(tile_plan, load, rows):
    lane = lax.broadcasted_iota(jnp.int32, (1, 128), 1)
    out = jnp.zeros((rows, 128), F32)
    for (slab, st, roll), lo, hi in tile_plan:
        v = load(slab, st)
        if roll:
            v = pltpu.roll(v, roll, 1)
        out = v if (lo, hi) == (0, 128) else jnp.where((lane >= lo) & (lane < hi), v, out)
    return out


def _w_in_from_shards(land):
    ref_col = [None] * PROJ_W
    for p0, r0, n in SEGMENTS:
        for i in range(n):
            ref_col[p0 + i] = divmod(r0 + i, SHARD_W)
    plan = _lane_plan(ref_col)
    d_model = land.shape[1]
    tm = 256

    def body(land_ref, o_ref):
        load = lambda slab, st: land_ref[slab, :, st * 128:(st + 1) * 128].astype(F32)
        for t, tile_plan in enumerate(plan):
            o_ref[:, t * 128:(t + 1) * 128] = _assemble(tile_plan, load, tm).astype(BF16)

    return pl.pallas_call(
        body, name="w_in_from_shards", grid=(d_model // tm,),
        in_specs=[pl.BlockSpec((N_DEV, tm, SHARD_WP), lambda i: (0, i, 0))],
        out_specs=pl.BlockSpec((tm, PROJ_W), lambda i: (i, 0)),
        out_shape=jax.ShapeDtypeStruct((d_model, PROJ_W), BF16),
        compiler_params=_params(("parallel",)),
    )(land)


def _dw_in_to_parts(dwp):
    padded_col = [None] * IN_WIDTH
    for p0, r0, n in SEGMENTS:
        for i in range(n):
            padded_col[r0 + i] = p0 + i
    sources = []
    for d in range(N_DEV):
        sources += [(0, padded_col[SHARD_W * d + c]) if c < SHARD_W else None for c in range(SHARD_WP)]
    plan = _lane_plan(sources)
    d_model = dwp.shape[0]
    tm = 256
    tiles = SHARD_WP // 128

    def body(dw_ref, o_ref):
        load = lambda slab, st: dw_ref[:, st * 128:(st + 1) * 128].astype(F32)
        for t, tile_plan in enumerate(plan):
            d, c = divmod(t, tiles)
            o_ref[d, :, c * 128:(c + 1) * 128] = _assemble(tile_plan, load, tm).astype(BF16)

    return pl.pallas_call(
        body, name="dw_in_to_parts", grid=(d_model // tm,),
        in_specs=[pl.BlockSpec((tm, PROJ_W), lambda i: (i, 0))],
        out_specs=pl.BlockSpec((N_DEV, tm, SHARD_WP), lambda i: (0, i, 0)),
        out_shape=jax.ShapeDtypeStruct((N_DEV, d_model, SHARD_WP), BF16),
        compiler_params=_params(("parallel",)),
    )(dwp)


def _cast_shards(shards):
    names = list(shards)

    def body(*refs):
        for src, dst in zip(refs[:len(names)], refs[len(names):]):
            if dst.shape != src.shape:
                dst[...] = jnp.zeros(dst.shape, BF16)
                dst[:, 0:src.shape[1]] = src[...].astype(BF16)
            else:
                dst[...] = src[...].astype(BF16)

    out_shape = [jax.ShapeDtypeStruct((shards[n].shape[0], SHARD_WP if n == "w_in" else shards[n].shape[1]), BF16)
                 for n in names]
    outs = pl.pallas_call(body, name="cast_shards", out_shape=out_shape,
                          compiler_params=pltpu.CompilerParams(vmem_limit_bytes=VMEM_LIMIT))(*[shards[n] for n in names])
    return dict(zip(names, outs))


def _tie(x, *tokens):
    for t in tokens:
        if t is not None:
            x = x + t[0:1, 0:1]
    return x


def _local_step(x, mem, target, p, getw, emit, deps=()):
    s = x.shape[0]
    bucket, band = _bucket_table()
    bucket_m = jnp.where(band, bucket, -1).astype(jnp.int32)
    bias = _bias_table(p["rel_bias"], bucket_m)
    bucket_t = jnp.transpose(bucket_m)
    bias_t = _bias_table(p["rel_bias"], bucket_t)
    gqf, gkf, gqa, gka = _tile2(p["qn_fox"]), _tile2(p["kn_fox"]), _tile2(p["qn_swa"]), _tile2(p["kn_swa"])
    gqm = p["qn_mem"]
    bf128 = jnp.pad(p["b_forget"], ((0, 0), (0, 120)))
    sink = p["sink_swa"].reshape(8)

    h = _rms_fwd(x, p["g_mix"], "rms_mix", tuple(deps) + (bias, bias_t))
    w_in = getw("w_in", h)
    proj = _mm(h, w_in, "nn", BF16, 512, 1536, 1024, "proj")
    fl = _mm(h, w_in[:, FL0:FL0 + 128], "nn", F32, 512, 128, 1024, "proj_fl")
    qf, kf, vf, qm, qa, ka, va, qf_t, vf_t = _proj_post(proj, gqf, gkf, gqm, gqa, gka)
    cc4, ca4 = _fox_gate_fwd(fl, bf128)
    w_kv = getw("w_mem_kv", cc4)
    mem_n, kv_raw, mk, mv = _memkv_fwd(mem, p["g_mem"], w_kv, p["kn_mem"])
    kp = jnp.pad(ka, ((SWA_BLOCK, 0), (0, 0)))
    vp = jnp.pad(va, ((SWA_BLOCK, 0), (0, 0)))
    oa = _swa_fwd(qa, kp, vp, bias, sink)
    of, lse4, of_t = _fox_fwd(qf, kf, vf_t, ca4)
    om = _mem_fwd(qm, mk, mv)
    wa, wf, wm, w_out = getw("w_o_swa", oa), getw("w_o_fox", oa), getw("w_o_mem", oa), getw("w_out", oa)
    x1, hm, merged = _merge_fwd(x, oa, of, om, proj, p["b_gate"], wa, wf, wm, w_out, p["g_mlp"])
    w_up = getw("w_mlp_up", of)
    u = _mlp_up(hm, w_up)
    w_down = getw("w_mlp_down", hm)
    dy, dy_b, loss = _mlp_down_loss(u, w_down, x1, target)

    da = _mlp_bwd_act(dy_b, w_down, u)
    t_down = emit({"w_mlp_down": _mm(u, dy_b, "tn", BF16, 1024, 1024, 2048, "dw_down")})
    dx1, dg_mlp = _mlp_bwd_x(da, w_up, x1, dy, _tie(p["g_mlp"], t_down))
    t_up = emit({"w_mlp_up": _mm(hm, da, "tn", BF16, 1024, 1024, 2048, "dw_up", column_chunks=True)})
    dproj, doa, dof_t, dom, dya, dyf, dym, db_gate = _merge_bwd(
        dx1, oa, of, om, proj, _tie(p["b_gate"], t_up), wa, wf, wm, w_out)
    dw_oa, dw_of, dw_om = _mm_tn3([oa, of, om], [dya, dyf, dym], "dw_o")
    t_o = emit({"w_out": _mm(merged, dx1, "tn", BF16, 1024, 1024, 2048, "dw_out"),
                "w_o_swa": dw_oa, "w_o_fox": dw_of, "w_o_mem": dw_om})

    dqm, dmk, dmv = _mem_bwd(qm, mk, mv, dom)
    dw_kv, dkn_mem, dg_mem = _memkv_bwd(dmk, dmv, kv_raw, _tie(p["kn_mem"], t_o), mem, p["g_mem"], mem_n, w_kv)
    t_kv = emit({"w_mem_kv": dw_kv})
    dqa, dkp, dvp, dbias, dsink = _swa_bwd(qa, kp, vp, bias_t, _tie(p["sink_swa"], t_kv).reshape(8), doa)
    dqf_t, dkf, dvf, dck4, dcq4 = _fox_bwd(qf_t, kf, vf, dof_t, of_t, cc4, lse4)

    dfl, db_forget = _fox_gate_bwd(dcq4, dck4, fl, bf128)

    dproj, dgn = _proj_pre_bwd(dproj, proj, dqf_t, dkf, dvf, dqm, dqa, dkp[SWA_BLOCK:], dvp[SWA_BLOCK:], dfl,
                               gqf, gkf, gqm, gqa, gka)
    t_in = emit({"w_in": _mm(h, dproj, "tn", BF16, 1024, 3072, 1024, "dw_in")})
    grad_x, dg_mix = _in_bwd_x(dproj, w_in, x, _tie(p["g_mix"], t_in), dx1)
    d_rel = _rel_bias_bwd(dbias, bucket_t)

    fold = lambda r: dgn[r:r + 1, 0:HEAD] + dgn[r:r + 1, HEAD:128]
    small = {
        "g_mix": dg_mix, "b_gate": db_gate, "b_forget": db_forget[:, 0:8],
        "qn_swa": fold(3), "kn_swa": fold(4), "sink_swa": dsink[:, 0].reshape(1, 8), "rel_bias": d_rel,
        "qn_fox": fold(0), "kn_fox": fold(1), "g_mem": dg_mem, "qn_mem": dgn[2:3, :], "kn_mem": dkn_mem,
        "g_mlp": dg_mlp,
    }
    return loss, grad_x, small


SMALL = ("g_mix", "b_gate", "b_forget", "qn_swa", "kn_swa", "sink_swa", "rel_bias", "qn_fox", "kn_fox", "g_mem",
         "qn_mem", "kn_mem", "g_mlp")
BIG = ("w_in", "w_mem_kv", "w_o_swa", "w_o_fox", "w_o_mem", "w_out", "w_mlp_up", "w_mlp_down")
COL_SHARDED = ("w_in", "w_o_swa", "w_o_fox", "w_o_mem", "w_mlp_up")
WEIGHTS = ("g_mix", "w_in", "b_gate", "b_forget", "qn_swa", "kn_swa", "sink_swa", "rel_bias", "qn_fox", "kn_fox", "g_mem",
           "w_mem_kv", "qn_mem", "kn_mem", "w_o_swa", "w_o_fox", "w_o_mem", "w_out", "g_mlp", "w_mlp_up", "w_mlp_down")
SMALL_SLOTS = (("g_mix", 1024), ("b_gate", 3072), ("b_forget", 128), ("qn_swa", 128), ("kn_swa", 128), ("sink_swa", 128),
               ("rel_bias", REL_BUCKETS * 128), ("qn_fox", 128), ("kn_fox", 128), ("g_mem", 1024), ("qn_mem", 128),
               ("kn_mem", 128), ("g_mlp", 1024), ("loss", 128))
SMALL_OFF = {n: sum(w for _, w in SMALL_SLOTS[:i]) for i, (n, _) in enumerate(SMALL_SLOTS)}
SMALL_ROW = sum(w for _, w in SMALL_SLOTS)


def _gathered_to_full(name, g):
    if name in COL_SHARDED:
        return jnp.transpose(g, (1, 0, 2)).reshape(g.shape[1], N_DEV * g.shape[2])
    return g.reshape(N_DEV * g.shape[1], g.shape[2])


def _full_to_parts(name, full, b):
    if name in COL_SHARDED:
        return jnp.transpose(full.reshape(full.shape[0], N_DEV, b), (1, 0, 2)).astype(BF16)
    return full.reshape(N_DEV, full.shape[0] // N_DEV, full.shape[1]).astype(BF16)


def _pack_small(grads, loss):
    pieces = []
    for n, width in SMALL_SLOTS:
        a = loss.reshape(1, 1) if n == "loss" else grads[n].reshape(1, -1)
        pieces.append(jnp.pad(a, ((0, 0), (0, width - a.shape[1]))))
    return jnp.concatenate(pieces, axis=1)


def _adamw_small(gathered, w, m, v):
    names = list(SMALL)

    def body(*refs):
        p_ref = refs[0]
        ins = refs[1:1 + 3 * len(names)]
        outs = refs[1 + 3 * len(names):]
        g_all = p_ref[0]
        for k in range(1, N_DEV):
            g_all = g_all + p_ref[k]
        for i, n in enumerate(names):
            w_ref, m_ref, v_ref = ins[3 * i:3 * i + 3]
            out = outs[4 * i:4 * i + 4]
            rows, cols = w_ref.shape
            for r in range(rows):
                off = SMALL_OFF[n] + 128 * r
                g = g_all[:, off:off + cols]
                rs = slice(r, r + 1)
                res = (g,) + _adam_math(w_ref[rs, :], g, m_ref[rs, :], v_ref[rs, :])
                for o_ref, val in zip(out, res):
                    o_ref[rs, :] = val
        outs[-1][...] = g_all[:, SMALL_OFF["loss"]:SMALL_OFF["loss"] + 128]

    args = [gathered]
    out_shape = []
    for n in names:
        args += [w[n], m[n], v[n]]
        out_shape += [jax.ShapeDtypeStruct(w[n].shape, F32)] * 4
    out_shape.append(jax.ShapeDtypeStruct((1, 128), F32))
    outs = pl.pallas_call(body, name="adamw_small", out_shape=out_shape)(*args)
    return {n: outs[4 * i:4 * i + 4] for i, n in enumerate(names)}, outs[-1]


def kernel(x, mem, g_mix, w_in, b_gate, b_forget, qn_swa, kn_swa, sink_swa, rel_bias, qn_fox, kn_fox, g_mem, w_mem_kv, qn_mem, kn_mem, w_o_swa, w_o_fox, w_o_mem, w_out, g_mlp, w_mlp_up, w_mlp_down, loss_target, m_g_mix, m_w_in, m_b_gate, m_b_forget, m_qn_swa, m_kn_swa, m_sink_swa, m_rel_bias, m_qn_fox, m_kn_fox, m_g_mem, m_w_mem_kv, m_qn_mem, m_kn_mem, m_w_o_swa, m_w_o_fox, m_w_o_mem, m_w_out, m_g_mlp, m_w_mlp_up, m_w_mlp_down, v_g_mix, v_w_in, v_b_gate, v_b_forget, v_qn_swa, v_kn_swa, v_sink_swa, v_rel_bias, v_qn_fox, v_kn_fox, v_g_mem, v_w_mem_kv, v_qn_mem, v_kn_mem, v_w_o_swa, v_w_o_fox, v_w_o_mem, v_w_out, v_g_mlp, v_w_mlp_up, v_w_mlp_down):
    wts = dict(g_mix=g_mix, w_in=w_in, b_gate=b_gate, b_forget=b_forget, qn_swa=qn_swa, kn_swa=kn_swa, sink_swa=sink_swa,
               rel_bias=rel_bias, qn_fox=qn_fox, kn_fox=kn_fox, g_mem=g_mem, w_mem_kv=w_mem_kv, qn_mem=qn_mem, kn_mem=kn_mem,
               w_o_swa=w_o_swa, w_o_fox=w_o_fox, w_o_mem=w_o_mem, w_out=w_out, g_mlp=g_mlp, w_mlp_up=w_mlp_up,
               w_mlp_down=w_mlp_down)
    mom = dict(g_mix=m_g_mix, w_in=m_w_in, b_gate=m_b_gate, b_forget=m_b_forget, qn_swa=m_qn_swa, kn_swa=m_kn_swa,
               sink_swa=m_sink_swa, rel_bias=m_rel_bias, qn_fox=m_qn_fox, kn_fox=m_kn_fox, g_mem=m_g_mem, w_mem_kv=m_w_mem_kv,
               qn_mem=m_qn_mem, kn_mem=m_kn_mem, w_o_swa=m_w_o_swa, w_o_fox=m_w_o_fox, w_o_mem=m_w_o_mem, w_out=m_w_out,
               g_mlp=m_g_mlp, w_mlp_up=m_w_mlp_up, w_mlp_down=m_w_mlp_down)
    var = dict(g_mix=v_g_mix, w_in=v_w_in, b_gate=v_b_gate, b_forget=v_b_forget, qn_swa=v_qn_swa, kn_swa=v_kn_swa,
               sink_swa=v_sink_swa, rel_bias=v_rel_bias, qn_fox=v_qn_fox, kn_fox=v_kn_fox, g_mem=v_g_mem, w_mem_kv=v_w_mem_kv,
               qn_mem=v_qn_mem, kn_mem=v_kn_mem, w_o_swa=v_w_o_swa, w_o_fox=v_w_o_fox, w_o_mem=v_w_o_mem, w_out=v_w_out,
               g_mlp=v_g_mlp, w_mlp_up=v_w_mlp_up, w_mlp_down=v_w_mlp_down)

    shards = _cast_shards({n: wts[n][0] for n in BIG})
    first = _split_start([shards["w_in"]], True, "ag_start_w_in", peers=(1,) + SAME_CORE)
    rest = _split_start([shards[n] for n in BIG[1:]], True, "ag_start_rest", after=first[4])
    full = {}

    def getw(n, after):
        if n == "w_in" and n not in full:
            forwarded = _forward_start(first, after, "ag_forward_w_in")
            full[n] = _w_in_from_shards(_forward_wait(first, forwarded, "ag_wait_w_in"))
        elif n not in full:
            land = _split_wait(rest, BIG[1:].index(n), after, True, "ag_wait_" + n)
            full[n] = land if n == "w_mlp_up" else _gathered_to_full(n, land)
        return full[n]

    exchanges = {}

    def emit(grads_by_name):
        parts = []
        for n, grad in grads_by_name.items():
            if n == "w_in":
                parts.append(_dw_in_to_parts(grad))
            else:
                parts.append(grad if n == "w_mlp_up" else _full_to_parts(n, grad, wts[n].shape[2]))
        started = _split_start(parts, False, "rs_start_" + next(iter(grads_by_name)))
        for w, n in enumerate(grads_by_name):
            exchanges[n] = (started, w)
        return started[4]

    small_p = {n: wts[n] for n in SMALL}
    loss, grad_x, small_g = _local_step(x[0], mem[0], loss_target[0], small_p, getw, emit, (first[4], rest[4]))

    packed = _pack_small(small_g, loss)
    small_gather = _split_start([packed], True, "ag_start_small")

    grads, delta, new_m, new_v = {}, {}, {}, {}

    def update(n, after):
        land = _split_wait(*exchanges[n], after, False, "rs_wait_" + n)
        g, d, m2, v2 = _adamw(land, wts[n][0], mom[n][0], var[n][0], "adamw_" + n)
        grads[n], delta[n], new_m[n], new_v[n] = g[None], d[None], m2[None], v2[None]
        return d

    after = small_gather[4]
    for n in exchanges:
        if n != "w_in":
            after = update(n, after)

    gathered = _split_wait(small_gather, 0, after, True, "ag_wait_small")
    small_out, total = _adamw_small(gathered, small_p, mom, var)
    for name, (g, d, m2, v2) in small_out.items():
        grads[name], delta[name], new_m[name], new_v[name] = g, d, m2, v2
    update("w_in", total)

    return (total[0, 0], grad_x[None], *[grads[n] for n in WEIGHTS], *[delta[n] for n in WEIGHTS],
            *[new_m[n] for n in WEIGHTS], *[new_v[n] for n in WEIGHTS])
```

```python
import math

import jax
import jax.numpy as jnp
from jax import lax
from jax.experimental import pallas as pl
from jax.experimental.pallas import tpu as pltpu

F32 = jnp.float32
BF16 = jnp.bfloat16

D_MODEL = 1024
N_MEM = 256
D_FF = 4096
HEAD = 64
SWA_HEADS = 8
SWA_BLOCK = 128
MEM_HEADS = 4
MEM_HEAD = 128
EPS = 1e-6
NEG = -1e30
REL_BUCKETS = 32
REL_MAX_DIST = 128

ADAM_LR = 0.001
ADAM_B1 = 0.9
ADAM_B2 = 0.999
ADAM_EPS = 1e-08
ADAM_WD = 0.01
ADAM_STEP = 10

GL0, QF0, KF0, VF0, QM0, QA0, KA0, VA0, FL0 = 0, 3072, 3584, 4096, 4608, 5120, 5632, 5760, 5888
PROJ_W = 6144
HALF_W = 3072
H_QF, H_KF, H_VF, H_QM, H_QA, H_KA, H_VA, H_FL = 0, 512, 1024, 1536, 2048, 2560, 2688, 2816

VMEM_LIMIT = 56 * 1024 * 1024
N_DEV = 8
MESH = pl.DeviceIdType.MESH

NN = (((1,), (0,)), ((), ()))
NT = (((1,), (1,)), ((), ()))
TN = (((0,), (0,)), ((), ()))


def _dot(a, b, dims=NN):
    return lax.dot_general(a, b, dims, preferred_element_type=F32)


def _params(sem):
    return pltpu.CompilerParams(dimension_semantics=sem, vmem_limit_bytes=VMEM_LIMIT)


def _full(shape):
    nd = len(shape)
    return pl.BlockSpec(shape, lambda *_: (0,) * nd)


def _sigmoid(z):
    return 1.0 / (1.0 + jnp.exp(-z))


def _group_mean(v, hd):
    if hd == 128:
        return jnp.mean(v, axis=-1, keepdims=True)
    r = lax.broadcasted_iota(jnp.int32, (128, 128), 0) // HEAD
    c = lax.broadcasted_iota(jnp.int32, (128, 128), 1) // HEAD
    same_head = jnp.where(r == c, 1.0 / HEAD, 0.0).astype(BF16)
    total = None
    rest = v
    for _ in range(2):
        part = rest.astype(BF16)
        rest = rest - part.astype(F32)
        term = _dot(part, same_head)
        total = term if total is None else total + term
    return total


def _mm(a, b, mode, out_dtype, tm, tn, tk, name, column_chunks=False):
    if mode == "nn":
        m, k = a.shape
        n = b.shape[1]
    elif mode == "nt":
        m, k = a.shape
        n = b.shape[0]
    else:
        k, m = a.shape
        n = b.shape[1]
    tm, tn, tk = min(tm, m), min(tn, n), min(tk, k)
    nk = k // tk
    chunk = n // N_DEV
    per_tile = tn // chunk if column_chunks else 1
    dims = {"nn": NN, "nt": NT, "tn": TN}[mode]
    a_spec = pl.BlockSpec((tk, tm), lambda j, i, kk: (kk, i)) if mode == "tn" else pl.BlockSpec((tm, tk), lambda j, i, kk: (i, kk))
    b_spec = pl.BlockSpec((tn, tk), lambda j, i, kk: (j, kk)) if mode == "nt" else pl.BlockSpec((tk, tn), lambda j, i, kk: (kk, j))

    def body(a_ref, b_ref, o_ref, *acc):
        prod = _dot(a_ref[...].astype(BF16), b_ref[...].astype(BF16), dims)

        def write(res):
            if column_chunks:
                for c in range(per_tile):
                    o_ref[c] = res[:, c * chunk:(c + 1) * chunk].astype(o_ref.dtype)
            else:
                o_ref[...] = res.astype(o_ref.dtype)

        if nk == 1:
            write(prod)
        else:
            acc_ref, = acc
            kk = pl.program_id(2)

            @pl.when(kk == 0)
            def _():
                acc_ref[...] = prod

            @pl.when(kk > 0)
            def _():
                acc_ref[...] += prod

            @pl.when(kk == nk - 1)
            def _():
                write(acc_ref[...])

    return pl.pallas_call(
        body, name=name, grid=(n // tn, m // tm, nk),
        in_specs=[a_spec, b_spec],
        out_specs=(pl.BlockSpec((per_tile, tm, chunk), lambda j, i, kk: (j, i, 0)) if column_chunks
                   else pl.BlockSpec((tm, tn), lambda j, i, kk: (i, j))),
        out_shape=jax.ShapeDtypeStruct((N_DEV, m, chunk) if column_chunks else (m, n), out_dtype),
        scratch_shapes=[pltpu.VMEM((tm, tn), F32)] if nk > 1 else [],
        compiler_params=_params(("parallel", "parallel", "arbitrary")),
    )(a, b)


def _mm_tn3(a_list, b_list, name):
    s, m = a_list[0].shape
    n = b_list[0].shape[1]
    tk = min(2048, s)
    nk = s // tk

    def body(*refs):
        a_refs, b_refs, o_refs, acc_refs = refs[0:3], refs[3:6], refs[6:9], refs[9:12]
        kk = pl.program_id(0)
        for a_ref, b_ref, o_ref, acc_ref in zip(a_refs, b_refs, o_refs, acc_refs):
            prod = _dot(a_ref[...], b_ref[...], TN)
            if nk == 1:
                o_ref[...] = prod.astype(o_ref.dtype)
                continue

            @pl.when(kk == 0)
            def _(acc_ref=acc_ref, prod=prod):
                acc_ref[...] = prod

            @pl.when(kk > 0)
            def _(acc_ref=acc_ref, prod=prod):
                acc_ref[...] += prod

            @pl.when(kk == nk - 1)
            def _(acc_ref=acc_ref, o_ref=o_ref):
                o_ref[...] = acc_ref[...].astype(o_ref.dtype)

    return pl.pallas_call(
        body, name=name, grid=(nk,),
        in_specs=[pl.BlockSpec((tk, m), lambda kk: (kk, 0))] * 3 + [pl.BlockSpec((tk, n), lambda kk: (kk, 0))] * 3,
        out_specs=[_full((m, n))] * 3,
        out_shape=[jax.ShapeDtypeStruct((m, n), BF16)] * 3,
        scratch_shapes=[pltpu.VMEM((m, n), F32)] * 3,
        compiler_params=_params(("arbitrary",)),
    )(*a_list, *b_list)


def _rms_fwd(x, g, name, deps=()):
    s, d = x.shape
    tm = min(512, s)

    def body(x_ref, g_ref, *rest):
        h_ref = rest[len(deps)]
        xv = x_ref[...]
        r = lax.rsqrt(jnp.mean(xv * xv, axis=-1, keepdims=True) + EPS)
        h_ref[...] = (xv * r * g_ref[...]).astype(BF16)

    return pl.pallas_call(
        body, name=name, grid=(s // tm,),
        in_specs=[pl.BlockSpec((tm, d), lambda i: (i, 0)), _full((1, d))] + [pl.BlockSpec(memory_space=pl.ANY)] * len(deps),
        out_specs=pl.BlockSpec((tm, d), lambda i: (i, 0)),
        out_shape=jax.ShapeDtypeStruct((s, d), BF16),
        compiler_params=_params(("parallel",)),
    )(x, g, *deps)


def _proj_post(proj, gq_fox, gk_fox, gq_mem, gq_swa, gk_swa):
    s = proj.shape[0]
    tm = min(512, s)

    def body(p_ref, gqf, gkf, gqm, gqa, gka, qf_ref, kf_ref, vf_ref, qm_ref, qa_ref, ka_ref, va_ref, qft_ref, vft_ref):
        def norm(off, width, hd, g_ref, o_ref, scaled_t_ref=None):
            for b in range(width // 128):
                v = p_ref[:, off + b * 128: off + (b + 1) * 128].astype(F32)
                r = lax.rsqrt(_group_mean(v * v, hd) + EPS)
                vn = (v * r * g_ref[...]).astype(BF16)
                o_ref[:, b * 128:(b + 1) * 128] = vn
                if scaled_t_ref is not None:
                    scaled_t_ref[b * 128:(b + 1) * 128, :] = (vn.astype(F32) * 0.125).T.astype(BF16)

        norm(H_QF, 512, HEAD, gqf, qf_ref, qft_ref)
        norm(H_KF, 512, HEAD, gkf, kf_ref)
        vf_ref[...] = p_ref[:, H_VF:H_VF + 512].astype(BF16)
        for b in range(4):
            vft_ref[b * 128:(b + 1) * 128, :] = p_ref[:, H_VF + b * 128:H_VF + (b + 1) * 128].astype(F32).T.astype(BF16)
        norm(H_QM, 512, MEM_HEAD, gqm, qm_ref)
        norm(H_QA, 512, HEAD, gqa, qa_ref)
        norm(H_KA, 128, HEAD, gka, ka_ref)
        va_ref[...] = p_ref[:, H_VA:H_VA + 128].astype(BF16)

    g_spec = _full((1, 128))
    o512 = pl.BlockSpec((tm, 512), lambda i: (i, 0))
    o128 = pl.BlockSpec((tm, 128), lambda i: (i, 0))
    s512 = jax.ShapeDtypeStruct((s, 512), BF16)
    s128 = jax.ShapeDtypeStruct((s, 128), BF16)
    return pl.pallas_call(
        body, name="proj_post", grid=(s // tm,),
        in_specs=[pl.BlockSpec((tm, HALF_W), lambda i: (i, 1)), g_spec, g_spec, g_spec, g_spec, g_spec],
        out_specs=[o512, o512, o512, o512, o512, o128, o128] + [pl.BlockSpec((512, tm), lambda i: (0, i))] * 2,
        out_shape=[s512, s512, s512, s512, s512, s128, s128] + [jax.ShapeDtypeStruct((512, s), BF16)] * 2,
        compiler_params=_params(("parallel",)),
    )(proj, gq_fox, gk_fox, gq_mem, gq_swa, gk_swa)


def _tri(n, lower):
    r = lax.broadcasted_iota(jnp.int32, (n, n), 0)
    c = lax.broadcasted_iota(jnp.int32, (n, n), 1)
    return jnp.where((c <= r) if lower else (c >= r), 1.0, 0.0).astype(F32)


def _fox_gate_fwd(proj, b_forget128):
    s = proj.shape[0]
    tm = min(512, s)

    def body(p_ref, b_ref, cc_ref, ca_ref, carry_ref):
        i = pl.program_id(0)

        @pl.when(i == 0)
        def _():
            carry_ref[...] = jnp.zeros_like(carry_ref)

        z = p_ref[...] + b_ref[...]
        logf = jnp.minimum(z, 0.0) - jnp.log(1.0 + jnp.exp(-jnp.abs(z)))
        c = jnp.dot(_tri(tm, True), logf, precision=lax.Precision.HIGHEST, preferred_element_type=F32) + carry_ref[...]
        carry_ref[...] = c[tm - 1:tm, :]
        lane = lax.broadcasted_iota(jnp.int32, (tm, 128), 1)
        for hp in range(4):
            cc_ref[hp] = c if hp == 0 else pltpu.roll(c, 128 - 2 * hp, 1)
            aug = jnp.zeros((tm, 128), F32)
            for e in range(2):
                rest = jnp.broadcast_to(c[:, 2 * hp + e:2 * hp + e + 1], (tm, 128))
                for part in range(3):
                    piece = rest.astype(BF16).astype(F32)
                    aug = jnp.where(lane == HEAD * (1 - e) + part, piece, aug)
                    rest = rest - piece
            ca_ref[hp] = aug.astype(BF16)

    return pl.pallas_call(
        body, name="fox_gate_fwd", grid=(s // tm,),
        in_specs=[pl.BlockSpec((tm, 128), lambda i: (i, 0)), _full((1, 128))],
        out_specs=[pl.BlockSpec((4, tm, 128), lambda i: (0, i, 0))] * 2,
        out_shape=[jax.ShapeDtypeStruct((4, s, 128), F32), jax.ShapeDtypeStruct((4, s, 128), BF16)],
        scratch_shapes=[pltpu.VMEM((1, 128), F32)],
        compiler_params=_params(("arbitrary",)),
    )(proj, b_forget128)


def _memkv_fwd(mem, g_mem, w_kv, kn_mem):
    m = mem.shape[0]

    def body(mem_ref, g_ref, w_ref, kn_ref, memn_ref, kv_ref, mk_ref, mv_ref):
        xv = mem_ref[...]
        r = lax.rsqrt(jnp.mean(xv * xv, axis=-1, keepdims=True) + EPS)
        mn = (xv * r * g_ref[...]).astype(BF16)
        memn_ref[...] = mn
        kv = _dot(mn, w_ref[...])
        kv_ref[...] = kv
        for h in range(MEM_HEADS):
            v = kv[:, h * 128:(h + 1) * 128]
            rr = lax.rsqrt(jnp.mean(v * v, axis=-1, keepdims=True) + EPS)
            mk_ref[:, h * 128:(h + 1) * 128] = (v * rr * kn_ref[...]).astype(BF16)
        mv_ref[...] = kv[:, 512:1024].astype(BF16)

    return pl.pallas_call(
        body, name="memkv_fwd",
        out_shape=[jax.ShapeDtypeStruct((m, D_MODEL), BF16), jax.ShapeDtypeStruct((m, 1024), F32),
                   jax.ShapeDtypeStruct((m, 512), BF16), jax.ShapeDtypeStruct((m, 512), BF16)],
        compiler_params=pltpu.CompilerParams(vmem_limit_bytes=VMEM_LIMIT),
    )(mem, g_mem, w_kv, kn_mem)


def _bias_table(rel_bias, bucket):
    def body(rb_ref, bk_ref, o_ref):
        bk = bk_ref[...]
        for h in range(SWA_HEADS):
            acc = jnp.zeros(bk.shape, F32)
            for b in range(REL_BUCKETS):
                acc = jnp.where(bk == b, rb_ref[b, h], acc)
            o_ref[h] = acc

    return pl.pallas_call(
        body, name="bias_table",
        in_specs=[pl.BlockSpec(memory_space=pltpu.SMEM), pl.BlockSpec(memory_space=pltpu.VMEM)],
        out_shape=jax.ShapeDtypeStruct((SWA_HEADS,) + bucket.shape, F32),
    )(rel_bias, bucket)


def _swa_valid(n):
    row = lax.broadcasted_iota(jnp.int32, (SWA_BLOCK, 2 * SWA_BLOCK), 0)
    col = lax.broadcasted_iota(jnp.int32, (SWA_BLOCK, 2 * SWA_BLOCK), 1)
    dist = row + SWA_BLOCK - col
    return (dist >= 0) & (dist < SWA_BLOCK) & ((col >= SWA_BLOCK) | (n > 0))


def _swa_fwd(qa, kp, vp, bias, sink):
    s = qa.shape[0]
    nb = s // SWA_BLOCK

    def body(sink_ref, q_ref, kp_ref, vp_ref, bias_ref, o_ref):
        n = pl.program_id(0)
        start = pl.multiple_of(n * SWA_BLOCK, SWA_BLOCK)
        k2 = kp_ref[pl.ds(start, 2 * SWA_BLOCK), :]
        v2 = vp_ref[pl.ds(start, 2 * SWA_BLOCK), :]
        valid = _swa_valid(n)
        heads = range(SWA_HEADS)
        hs = lambda h: slice(h * HEAD, (h + 1) * HEAD)
        sc = [jnp.where(valid, _dot(q_ref[:, hs(h)], k2[:, hs(h // 4)], NT) * 0.125 + bias_ref[h], NEG) for h in heads]
        pn = []
        for h in heads:
            sk = sink_ref[h]
            mx = jnp.maximum(jnp.max(sc[h], axis=-1, keepdims=True), sk)
            p = jnp.exp(sc[h] - mx)
            den = jnp.sum(p, axis=-1, keepdims=True) + jnp.exp(sk - mx)
            pn.append((p / den).astype(BF16))
        outs = [_dot(pn[h], v2[:, hs(h // 4)]).astype(BF16) for h in heads]
        for h in heads:
            o_ref[:, hs(h)] = outs[h]

    return pl.pallas_call(
        body, name="swa_fwd", grid=(nb,),
        in_specs=[pl.BlockSpec(memory_space=pltpu.SMEM),
                  pl.BlockSpec((SWA_BLOCK, 512), lambda n: (n, 0)),
                  _full(kp.shape), _full(vp.shape), _full(bias.shape)],
        out_specs=pl.BlockSpec((SWA_BLOCK, 512), lambda n: (n, 0)),
        out_shape=jax.ShapeDtypeStruct((s, 512), BF16),
        compiler_params=_params(("parallel",)),
    )(sink, qa, kp, vp, bias)


def _head_mask(e):
    lane = lax.broadcasted_iota(jnp.int32, (1, 128), 1)
    return (lane >= e * HEAD) & (lane < (e + 1) * HEAD)


FOX_FWD_T = 1024
FOX_BWD_T = 512


def _head_rows(e):
    row = lax.broadcasted_iota(jnp.int32, (128, 1), 0)
    return (row >= e * HEAD) & (row < (e + 1) * HEAD)


def _fox_fwd(q, k, v_t, ca4):
    s = q.shape[0]
    t = min(FOX_FWD_T, s)
    nq = s // t

    def body(q_ref, k_ref, vt_ref, ca_ref, o_ref, lse_ref, ot_ref):
        i = pl.program_id(1)
        qs = q_ref[...] * jnp.asarray(0.125, BF16)
        lane = lax.broadcasted_iota(jnp.int32, (1, 128), 1)
        minus = [jnp.where((lane >= HEAD * (1 - e)) & (lane < HEAD * (1 - e) + 3), -1.0, 0.0).astype(BF16) for e in range(2)]
        qe = [jnp.where(_head_mask(e), qs, jnp.broadcast_to(minus[e], qs.shape)) for e in range(2)]

        def block(carry, key0, nkeys, q0, nqs, masked):
            ks = pl.ds(pl.multiple_of(key0, 128), nkeys)
            kj = k_ref[ks, :]
            caj = ca_ref[0, ks, :]
            vtj = vt_ref[:, ks]
            out = []
            for e in range(2):
                m_all, acc_all = carry[2 * e], carry[2 * e + 1]
                m, acc = m_all[:, q0:q0 + nqs], acc_all[:, q0:q0 + nqs]
                st = _dot(jnp.where(_head_mask(e), kj, caj), qe[e][q0:q0 + nqs, :], NT)
                if masked:
                    krow = lax.broadcasted_iota(jnp.int32, (nkeys, nqs), 0) + key0
                    qcol = lax.broadcasted_iota(jnp.int32, (nkeys, nqs), 1) + (i * t + q0)
                    st = jnp.where(krow <= qcol, st, NEG)
                m_new = jnp.maximum(m, jnp.max(st, axis=0, keepdims=True))
                alpha = jnp.exp(m - m_new)
                pt = jnp.exp(st - m_new).astype(BF16)
                vte = jnp.where(_head_rows(e), vtj, jnp.ones_like(vtj))
                acc_new = alpha * acc + _dot(vte, pt)
                if nqs < t:
                    m_new = jnp.concatenate([m_all[:, :q0], m_new], axis=1)
                    acc_new = jnp.concatenate([acc_all[:, :q0], acc_new], axis=1)
                out += [m_new, acc_new]
            return tuple(out)

        half = t // 2
        init = (jnp.full((1, t), NEG, F32), jnp.zeros((128, t), F32)) * 2
        carry = lax.fori_loop(0, i, lambda j, c: block(c, j * t, t, 0, t, False), init)
        carry = block(carry, i * t, half, 0, t, True)
        m0, a0, m1, a1 = block(carry, i * t + half, half, half, half, True)
        l0 = a0[HEAD:HEAD + 1, :]
        l1 = a1[0:1, :]
        o_t = jnp.where(_head_rows(0), a0 / l0, a1 / l1)
        o_ref[...] = o_t.T.astype(BF16)
        ot_ref[...] = o_t.astype(BF16)
        r8 = lax.broadcasted_iota(jnp.int32, (8, t), 0)
        lse_ref[0] = jnp.where(r8 == 0, m0 + jnp.log(l0), jnp.where(r8 == 1, m1 + jnp.log(l1), 0.0))

    return pl.pallas_call(
        body, name="fox_fwd", grid=(4, nq),
        in_specs=[pl.BlockSpec((t, 128), lambda hp, i: (i, hp)),
                  pl.BlockSpec((s, 128), lambda hp, i: (0, hp)),
                  pl.BlockSpec((128, s), lambda hp, i: (hp, 0)),
                  pl.BlockSpec((1, s, 128), lambda hp, i: (hp, 0, 0))],
        out_specs=[pl.BlockSpec((t, 128), lambda hp, i: (i, hp)),
                   pl.BlockSpec((1, 8, t), lambda hp, i: (hp, 0, i)),
                   pl.BlockSpec((128, t), lambda hp, i: (hp, i))],
        out_shape=[jax.ShapeDtypeStruct((s, 512), BF16), jax.ShapeDtypeStruct((4, 8, s), F32),
                   jax.ShapeDtypeStruct((512, s), BF16)],
        compiler_params=_params(("parallel", "parallel")),
    )(q, k, v_t, ca4)


MEM_SCALE = MEM_HEAD ** -0.5


def _mem_fwd(qm, mk, mv):
    s = qm.shape[0]
    tq = min(512, s)

    def body(q_ref, mk_ref, mv_ref, o_ref):
        for h in range(MEM_HEADS):
            hs = slice(h * 128, (h + 1) * 128)
            sc = _dot(q_ref[:, hs], mk_ref[:, hs], NT) * MEM_SCALE
            mx = jnp.max(sc, axis=-1, keepdims=True)
            p = jnp.exp(sc - mx)
            p = p / jnp.sum(p, axis=-1, keepdims=True)
            o_ref[:, hs] = _dot(p.astype(BF16), mv_ref[:, hs]).astype(BF16)

    return pl.pallas_call(
        body, name="mem_fwd", grid=(s // tq,),
        in_specs=[pl.BlockSpec((tq, 512), lambda i: (i, 0)), _full(mk.shape), _full(mv.shape)],
        out_specs=pl.BlockSpec((tq, 512), lambda i: (i, 0)),
        out_shape=jax.ShapeDtypeStruct((s, 512), BF16),
        compiler_params=_params(("parallel",)),
    )(qm, mk, mv)


def _merge_fwd(x, oa, of, om, proj, b_gate, wa, wf, wm, w_out, g_mlp):
    s = x.shape[0]
    tm = min(512, s)

    def body(x_ref, oa_ref, of_ref, om_ref, gl_ref, bg_ref, wa_ref, wf_ref, wm_ref, wo_ref, g_ref, x1_ref, hm_ref, mg_ref):
        merged = None
        for b, (o_ref, w_ref) in enumerate(((oa_ref, wa_ref), (of_ref, wf_ref), (om_ref, wm_ref))):
            cs = slice(b * D_MODEL, (b + 1) * D_MODEL)
            y = _dot(o_ref[...], w_ref[...])
            t = _sigmoid(gl_ref[:, cs].astype(F32) + bg_ref[:, cs]) * y
            merged = t if merged is None else merged + t
        mb = merged.astype(BF16)
        mg_ref[...] = mb
        x1 = x_ref[...] + _dot(mb, wo_ref[...])
        x1_ref[...] = x1
        r = lax.rsqrt(jnp.mean(x1 * x1, axis=-1, keepdims=True) + EPS)
        hm_ref[...] = (x1 * r * g_ref[...]).astype(BF16)

    row = lambda w: pl.BlockSpec((tm, w), lambda i: (i, 0))
    return pl.pallas_call(
        body, name="merge_fwd", grid=(s // tm,),
        in_specs=[row(D_MODEL), row(512), row(512), row(512), row(HALF_W), _full((1, HALF_W)),
                  _full(wa.shape), _full(wf.shape), _full(wm.shape), _full(w_out.shape), _full((1, D_MODEL))],
        out_specs=[row(D_MODEL), row(D_MODEL), row(D_MODEL)],
        out_shape=[jax.ShapeDtypeStruct((s, D_MODEL), F32), jax.ShapeDtypeStruct((s, D_MODEL), BF16),
                   jax.ShapeDtypeStruct((s, D_MODEL), BF16)],
        compiler_params=_params(("parallel",)),
    )(x, oa, of, om, proj, b_gate, wa, wf, wm, w_out, g_mlp)


def _mlp_up(hm, w_up):
    s = hm.shape[0]
    tm, tn = min(1024, s), w_up.shape[2]

    def body(h_ref, w_ref, u_ref):
        r = jnp.maximum(_dot(h_ref[...], w_ref[0]), 0.0)
        u_ref[...] = (r * r).astype(BF16)

    return pl.pallas_call(
        body, name="mlp_up", grid=(s // tm, D_FF // tn),
        in_specs=[pl.BlockSpec((tm, D_MODEL), lambda i, j: (i, 0)), pl.BlockSpec((1, D_MODEL, tn), lambda i, j: (j, 0, 0))],
        out_specs=pl.BlockSpec((tm, tn), lambda i, j: (i, j)),
        out_shape=jax.ShapeDtypeStruct((s, D_FF), BF16),
        compiler_params=_params(("parallel", "parallel")),
    )(hm, w_up)


def _mlp_down_loss(u, w_down, x1, target):
    s = u.shape[0]
    tm = min(256, s)

    def body(u_ref, w_ref, x1_ref, t_ref, dy_ref, dyb_ref, loss_ref):
        i = pl.program_id(0)

        @pl.when(i == 0)
        def _():
            loss_ref[...] = jnp.zeros_like(loss_ref)

        y = x1_ref[...] + _dot(u_ref[...], w_ref[...])
        err = y - t_ref[...]
        dy = err * (1.0 / D_MODEL)
        dy_ref[...] = dy
        dyb_ref[...] = dy.astype(BF16)
        part = jnp.sum(jnp.sum(err * err, axis=-1, keepdims=True) * (1.0 / D_MODEL), axis=0, keepdims=True)
        loss_ref[...] += 0.5 * part

    row = pl.BlockSpec((tm, D_MODEL), lambda i: (i, 0))
    return pl.pallas_call(
        body, name="mlp_down_loss", grid=(s // tm,),
        in_specs=[pl.BlockSpec((tm, D_FF), lambda i: (i, 0)), _full(w_down.shape), row, row],
        out_specs=[row, row, _full((1, 1))],
        out_shape=[jax.ShapeDtypeStruct((s, D_MODEL), F32), jax.ShapeDtypeStruct((s, D_MODEL), BF16),
                   jax.ShapeDtypeStruct((1, 1), F32)],
        compiler_params=_params(("arbitrary",)),
    )(u, w_down, x1, target)


def _mlp_bwd_act(dy, w_down, u):
    s = dy.shape[0]
    tm, tn = min(1024, s), 1024

    def body(dy_ref, w_ref, u_ref, da_ref):
        du = _dot(dy_ref[...], w_ref[...], NT)
        da_ref[...] = (du * (2.0 * jnp.sqrt(u_ref[...].astype(F32)))).astype(BF16)

    return pl.pallas_call(
        body, name="mlp_bwd_act", grid=(D_FF // tn, s // tm),
        in_specs=[pl.BlockSpec((tm, D_MODEL), lambda j, i: (i, 0)), pl.BlockSpec((tn, D_MODEL), lambda j, i: (j, 0)),
                  pl.BlockSpec((tm, tn), lambda j, i: (i, j))],
        out_specs=pl.BlockSpec((tm, tn), lambda j, i: (i, j)),
        out_shape=jax.ShapeDtypeStruct((s, D_FF), BF16),
        compiler_params=_params(("parallel", "parallel")),
    )(dy, w_down, u)


def _rms_bwd(xv, g, dh, skip):
    r = lax.rsqrt(jnp.mean(xv * xv, axis=-1, keepdims=True) + EPS)
    n = xv * r
    dn = dh * g
    dx = skip + r * (dn - n * jnp.mean(dn * n, axis=-1, keepdims=True))
    return dx, jnp.sum(dh * n, axis=0, keepdims=True)


def _mlp_bwd_x(da, w_up, x1, dy, g_mlp):
    s = da.shape[0]
    tm = min(256, s)

    def body(da_ref, w_ref, x1_ref, dy_ref, g_ref, dx1_ref, dg_ref):
        i = pl.program_id(0)

        @pl.when(i == 0)
        def _():
            dg_ref[...] = jnp.zeros_like(dg_ref)

        tn = w_ref.shape[2]
        dhm = _dot(da_ref[:, 0:tn], w_ref[0], NT)
        for j in range(1, N_DEV):
            dhm = dhm + _dot(da_ref[:, j * tn:(j + 1) * tn], w_ref[j], NT)
        dx, dg = _rms_bwd(x1_ref[...], g_ref[...], dhm, dy_ref[...])
        dx1_ref[...] = dx
        dg_ref[...] += dg

    row = pl.BlockSpec((tm, D_MODEL), lambda i: (i, 0))
    return pl.pallas_call(
        body, name="mlp_bwd_x", grid=(s // tm,),
        in_specs=[pl.BlockSpec((tm, D_FF), lambda i: (i, 0)), _full(w_up.shape), row, row, _full((1, D_MODEL))],
        out_specs=[row, _full((1, D_MODEL))],
        out_shape=[jax.ShapeDtypeStruct((s, D_MODEL), F32), jax.ShapeDtypeStruct((1, D_MODEL), F32)],
        compiler_params=_params(("arbitrary",)),
    )(da, w_up, x1, dy, g_mlp)


def _merge_bwd(dx1, oa, of, om, proj, b_gate, wa, wf, wm, w_out):
    s = dx1.shape[0]
    tm = min(512, s)

    def body(dx1_ref, oa_ref, of_ref, om_ref, gl_ref, bg_ref, wa_ref, wf_ref, wm_ref, wo_ref,
             dp_ref, doa_ref, dof_ref, dom_ref, dya_ref, dyf_ref, dym_ref, dbg_ref):
        i = pl.program_id(0)

        @pl.when(i == 0)
        def _():
            dbg_ref[...] = jnp.zeros_like(dbg_ref)

        dmerged = _dot(dx1_ref[...].astype(BF16), wo_ref[...], NT)
        branches = ((oa_ref, wa_ref, doa_ref, dya_ref), (of_ref, wf_ref, dof_ref, dyf_ref), (om_ref, wm_ref, dom_ref, dym_ref))
        for b, (o_ref, w_ref, do_ref, dyb_ref) in enumerate(branches):
            cs = slice(b * D_MODEL, (b + 1) * D_MODEL)
            y = _dot(o_ref[...], w_ref[...])
            g = _sigmoid(gl_ref[:, cs].astype(F32) + bg_ref[:, cs])
            dz = (dmerged * y) * g * (1.0 - g)
            dp_ref[:, cs] = dz.astype(BF16)
            dbg_ref[:, cs] += jnp.sum(dz, axis=0, keepdims=True)
            dyb = (dmerged * g).astype(BF16)
            dyb_ref[...] = dyb
            do = _dot(dyb, w_ref[...], NT)
            do_ref[...] = (do.T if b == 1 else do).astype(BF16)

    row = lambda w: pl.BlockSpec((tm, w), lambda i: (i, 0))
    sd = lambda w: jax.ShapeDtypeStruct((s, w), BF16)
    return pl.pallas_call(
        body, name="merge_bwd", grid=(s // tm,),
        in_specs=[row(D_MODEL), row(512), row(512), row(512), row(HALF_W), _full((1, HALF_W)),
                  _full(wa.shape), _full(wf.shape), _full(wm.shape), _full(w_out.shape)],
        out_specs=[row(HALF_W), row(512), pl.BlockSpec((512, tm), lambda i: (0, i)), row(512),
                   row(D_MODEL), row(D_MODEL), row(D_MODEL), _full((1, HALF_W))],
        out_shape=[sd(PROJ_W), sd(512), jax.ShapeDtypeStruct((512, s), BF16), sd(512), sd(D_MODEL), sd(D_MODEL), sd(D_MODEL),
                   jax.ShapeDtypeStruct((1, HALF_W), F32)],
        compiler_params=_params(("arbitrary",)),
    )(dx1, oa, of, om, proj, b_gate, wa, wf, wm, w_out)


def _swa_valid_t(n):
    key = lax.broadcasted_iota(jnp.int32, (2 * SWA_BLOCK, SWA_BLOCK), 0)
    qry = lax.broadcasted_iota(jnp.int32, (2 * SWA_BLOCK, SWA_BLOCK), 1)
    dist = qry + SWA_BLOCK - key
    return (dist >= 0) & (dist < SWA_BLOCK) & ((key >= SWA_BLOCK) | (n > 0))


def _swa_bwd(qa, kp, vp, bias_t, sink, doa):
    s = qa.shape[0]
    nb = s // SWA_BLOCK

    def body(sink_ref, q_ref, kp_ref, vp_ref, bias_ref, do_ref, dq_ref, dkp_ref, dvp_ref, dbias_ref, dsink_ref, sk_acc):
        n = pl.program_id(0)

        @pl.when(n == 0)
        def _():
            dkp_ref[...] = jnp.zeros_like(dkp_ref)
            dvp_ref[...] = jnp.zeros_like(dvp_ref)
            dbias_ref[...] = jnp.zeros_like(dbias_ref)
            sk_acc[...] = jnp.zeros_like(sk_acc)

        start = pl.multiple_of(n * SWA_BLOCK, SWA_BLOCK)
        win = pl.ds(start, 2 * SWA_BLOCK)
        k2 = kp_ref[win, :]
        v2 = vp_ref[win, :]
        valid = _swa_valid_t(n)
        heads = range(SWA_HEADS)
        hs = lambda h: slice(h * HEAD, (h + 1) * HEAD)
        scale = jnp.asarray(0.125, BF16)
        q = [q_ref[:, hs(h)] for h in heads]
        do = [do_ref[:, hs(h)] for h in heads]
        kk = [k2[:, hs(kv)] for kv in range(2)]
        vv = [v2[:, hs(kv)] for kv in range(2)]
        kt = [(kk[kv].astype(F32) * 0.125).T.astype(BF16) for kv in range(2)]
        st = [jnp.where(valid, _dot(kk[h // 4], q[h], NT) * 0.125 + bias_ref[h], NEG) for h in heads]
        dpt = [_dot(vv[h // 4], do[h], NT) for h in heads]
        pt, dst = [], []
        for h in heads:
            sk = sink_ref[h]
            mx = jnp.maximum(jnp.max(st[h], axis=0, keepdims=True), sk)
            p = jnp.exp(st[h] - mx)
            esk = jnp.exp(sk - mx)
            den = jnp.sum(p, axis=0, keepdims=True) + esk
            p = p / den
            delta = jnp.sum(p * dpt[h], axis=0, keepdims=True)
            d = p * (dpt[h] - delta)
            sk_acc[h:h + 1, :] += -(esk / den) * delta
            dbias_ref[h] += d
            pt.append(p.astype(BF16))
            dst.append(d.astype(BF16))
        dq_t = [_dot(kt[h // 4], dst[h]) for h in heads]
        dq_ref[...] = jnp.concatenate(dq_t, axis=0).T.astype(BF16)
        for kv in range(2):
            group = range(4 * kv, 4 * kv + 4)
            dk = [_dot(dst[h], q[h] * scale) for h in group]
            dv = [_dot(pt[h], do[h]) for h in group]
            dkp_ref[win, hs(kv)] += (dk[0] + dk[1]) + (dk[2] + dk[3])
            dvp_ref[win, hs(kv)] += (dv[0] + dv[1]) + (dv[2] + dv[3])

        @pl.when(n == nb - 1)
        def _():
            dsink_ref[...] = jnp.broadcast_to(jnp.sum(sk_acc[...], axis=1, keepdims=True), dsink_ref.shape)

    return pl.pallas_call(
        body, name="swa_bwd", grid=(nb,),
        in_specs=[pl.BlockSpec(memory_space=pltpu.SMEM),
                  pl.BlockSpec((SWA_BLOCK, 512), lambda n: (n, 0)),
                  _full(kp.shape), _full(vp.shape), _full(bias_t.shape),
                  pl.BlockSpec((SWA_BLOCK, 512), lambda n: (n, 0))],
        out_specs=[pl.BlockSpec((SWA_BLOCK, 512), lambda n: (n, 0)), _full(kp.shape), _full(vp.shape),
                   _full(bias_t.shape), _full((SWA_HEADS, 128))],
        out_shape=[jax.ShapeDtypeStruct((s, 512), BF16), jax.ShapeDtypeStruct(kp.shape, F32),
                   jax.ShapeDtypeStruct(vp.shape, F32), jax.ShapeDtypeStruct(bias_t.shape, F32),
                   jax.ShapeDtypeStruct((SWA_HEADS, 128), F32)],
        scratch_shapes=[pltpu.VMEM((SWA_HEADS, 128), F32)],
        compiler_params=_params(("arbitrary",)),
    )(sink, qa, kp, vp, bias_t, doa)


def _fox_bwd(qt, k, v, dot, ot, cc4, lse4):
    s = k.shape[0]
    t = min(FOX_BWD_T, s)
    nq = s // t

    def body(qt_ref, k_ref, v_ref, dot_ref, ot_ref, cc_ref, lse_ref,
             dqt_ref, dk_ref, dv_ref, dck_ref, dcq_ref, delta_ref, dkt_acc, dvt_acc, ds0, ds1):
        j = pl.program_id(1)

        @pl.when(j == 0)
        def _():
            dqt_ref[...] = jnp.zeros_like(dqt_ref)
            dcq_ref[...] = jnp.zeros_like(dcq_ref)
            r8 = lax.broadcasted_iota(jnp.int32, (8, t), 0)

            def dl(i, c):
                cols = pl.ds(pl.multiple_of(i * t, t), t)
                pr = dot_ref[:, cols].astype(F32) * ot_ref[:, cols].astype(F32)
                d0 = jnp.sum(jnp.where(_head_rows(0), pr, 0.0), axis=0, keepdims=True)
                d1 = jnp.sum(jnp.where(_head_rows(1), pr, 0.0), axis=0, keepdims=True)
                delta_ref[:, cols] = jnp.where(r8 == 0, d0, jnp.where(r8 == 1, d1, 0.0))
                return c

            lax.fori_loop(0, nq, dl, 0)

        kj = k_ref[...]
        vj = v_ref[...]
        ks = pl.ds(pl.multiple_of(j * t, t), t)
        kt = (kj.astype(F32) * 0.125).T.astype(BF16)
        ke = [jnp.where(_head_mask(e), kj, jnp.zeros_like(kj)) for e in range(2)]
        ve = [jnp.where(_head_mask(e), vj, jnp.zeros_like(vj)) for e in range(2)]
        ck = [cc_ref[0, ks, e:e + 1] for e in range(2)]
        for r in (dkt_acc, dvt_acc, ds0, ds1):
            r[...] = jnp.zeros_like(r)

        def block(q0, nqs, k0, nks, masked):
            cols = pl.ds(pl.multiple_of(q0, 128), nqs)
            rows = slice(k0, k0 + nks)
            qti = qt_ref[:, cols]
            doti = dot_ref[:, cols]
            for e, ds_acc in enumerate((ds0, ds1)):
                dims = slice(e * HEAD, (e + 1) * HEAD)
                st = _dot(ke[e][rows, :], qti) - ck[e][rows, :]
                if masked:
                    krow = lax.broadcasted_iota(jnp.int32, (nks, nqs), 0) + (j * t + k0)
                    qcol = lax.broadcasted_iota(jnp.int32, (nks, nqs), 1) + q0
                    st = jnp.where(krow <= qcol, st, NEG)
                pt = jnp.exp(st - lse_ref[0, e:e + 1, cols])
                dpt = _dot(ve[e][rows, :], doti)
                dst = pt * (dpt - delta_ref[e:e + 1, cols])
                dsb = dst.astype(BF16)
                dvt_acc[dims, rows] += _dot(doti[dims, :], pt.astype(BF16), NT)
                dkt_acc[dims, rows] += _dot(qti[dims, :], dsb, NT)
                dqt_ref[dims, cols] += _dot(kt[dims, rows], dsb)
                ds_acc[rows, 0:nqs] += dst
                dcq_ref[0, e:e + 1, cols] += jnp.sum(dst, axis=0, keepdims=True)

        half = t // 2
        block(j * t, half, 0, half, True)
        block(j * t + half, half, 0, t, True)

        def rest(i, c):
            block(i * t, t, 0, t, False)
            return c

        lax.fori_loop(j + 1, nq, rest, 0)
        dk_ref[...] = dkt_acc[...].T.astype(BF16)
        dv_ref[...] = dvt_acc[...].T.astype(BF16)
        lane = lax.broadcasted_iota(jnp.int32, (t, 128), 1)
        c0 = jnp.sum(ds0[...], axis=-1, keepdims=True)
        c1 = jnp.sum(ds1[...], axis=-1, keepdims=True)
        dck_ref[0] = jnp.where(lane == 0, c0, jnp.where(lane == 1, c1, 0.0))

    res_t = lambda: pl.BlockSpec((128, s), lambda hp, j: (hp, 0))
    blk = lambda: pl.BlockSpec((t, 128), lambda hp, j: (j, hp))
    return pl.pallas_call(
        body, name="fox_bwd", grid=(4, nq),
        in_specs=[res_t(), blk(), blk(), res_t(), res_t(), pl.BlockSpec((1, s, 128), lambda hp, j: (hp, 0, 0)),
                  pl.BlockSpec((1, 8, s), lambda hp, j: (hp, 0, 0))],
        out_specs=[res_t(), blk(), blk(),
                   pl.BlockSpec((1, t, 128), lambda hp, j: (hp, j, 0)),
                   pl.BlockSpec((1, 8, s), lambda hp, j: (hp, 0, 0))],
        out_shape=[jax.ShapeDtypeStruct((512, s), F32), jax.ShapeDtypeStruct((s, 512), BF16),
                   jax.ShapeDtypeStruct((s, 512), BF16), jax.ShapeDtypeStruct((4, s, 128), F32),
                   jax.ShapeDtypeStruct((4, 8, s), F32)],
        scratch_shapes=[pltpu.VMEM((8, s), F32)] + [pltpu.VMEM((128, t), F32)] * 2 + [pltpu.VMEM((t, t), F32)] * 2,
        compiler_params=_params(("arbitrary", "arbitrary")),
    )(qt, k, v, dot, ot, cc4, lse4)


def _mem_bwd(qm, mk, mv, dom):
    s = qm.shape[0]
    tq = min(512, s)

    def body(q_ref, mk_ref, mv_ref, do_ref, dq_ref, dmk_ref, dmv_ref):
        i = pl.program_id(0)

        @pl.when(i == 0)
        def _():
            dmk_ref[...] = jnp.zeros_like(dmk_ref)
            dmv_ref[...] = jnp.zeros_like(dmv_ref)

        heads = range(MEM_HEADS)
        hs = lambda h: slice(h * 128, (h + 1) * 128)
        sc = [_dot(q_ref[:, hs(h)], mk_ref[:, hs(h)], NT) * MEM_SCALE for h in heads]
        dp = [_dot(do_ref[:, hs(h)], mv_ref[:, hs(h)], NT) for h in heads]
        pb, dsb = [], []
        for h in heads:
            p = jnp.exp(sc[h] - jnp.max(sc[h], axis=-1, keepdims=True))
            p = p / jnp.sum(p, axis=-1, keepdims=True)
            ds = p * (dp[h] - jnp.sum(p * dp[h], axis=-1, keepdims=True))
            pb.append(p.astype(BF16))
            dsb.append((ds * MEM_SCALE).astype(BF16))
        dq = [_dot(dsb[h], mk_ref[:, hs(h)]).astype(BF16) for h in heads]
        dmk = [_dot(dsb[h], q_ref[:, hs(h)], TN) for h in heads]
        dmv = [_dot(pb[h], do_ref[:, hs(h)], TN) for h in heads]
        for h in heads:
            dq_ref[:, hs(h)] = dq[h]
            dmk_ref[:, hs(h)] += dmk[h]
            dmv_ref[:, hs(h)] += dmv[h]

    return pl.pallas_call(
        body, name="mem_bwd", grid=(s // tq,),
        in_specs=[pl.BlockSpec((tq, 512), lambda i: (i, 0)), _full(mk.shape), _full(mv.shape),
                  pl.BlockSpec((tq, 512), lambda i: (i, 0))],
        out_specs=[pl.BlockSpec((tq, 512), lambda i: (i, 0)), _full(mk.shape), _full(mv.shape)],
        out_shape=[jax.ShapeDtypeStruct((s, 512), BF16), jax.ShapeDtypeStruct(mk.shape, F32),
                   jax.ShapeDtypeStruct(mv.shape, F32)],
        compiler_params=_params(("arbitrary",)),
    )(qm, mk, mv, dom)


def _memkv_bwd(dmk, dmv, kv_raw, kn_mem, mem, g_mem, mem_n, w_kv):
    def body(dmk_ref, dmv_ref, kv_ref, kn_ref, mem_ref, g_ref, mn_ref, w_ref, dw_ref, dkn_ref, dg_ref, dkv_ref):
        dkn = jnp.zeros((1, 128), F32)
        for h in range(MEM_HEADS):
            hs = slice(h * 128, (h + 1) * 128)
            v = kv_ref[:, hs]
            r = lax.rsqrt(jnp.mean(v * v, axis=-1, keepdims=True) + EPS)
            n = v * r
            dn = dmk_ref[:, hs]
            dkn = dkn + jnp.sum(dn * n, axis=0, keepdims=True)
            dng = dn * kn_ref[...]
            dkv_ref[:, hs] = (r * (dng - n * jnp.mean(dng * n, axis=-1, keepdims=True))).astype(BF16)
        dkv_ref[:, 512:1024] = dmv_ref[...].astype(BF16)
        dkn_ref[...] = dkn
        dkv = dkv_ref[...]
        dw_ref[...] = _dot(mn_ref[...], dkv, TN).astype(BF16)
        dmn = _dot(dkv, w_ref[...], NT)
        xv = mem_ref[...]
        r = lax.rsqrt(jnp.mean(xv * xv, axis=-1, keepdims=True) + EPS)
        dg_ref[...] = jnp.sum(dmn * (xv * r), axis=0, keepdims=True)

    m = mem.shape[0]
    return pl.pallas_call(
        body, name="memkv_bwd",
        out_shape=[jax.ShapeDtypeStruct((D_MODEL, 1024), BF16), jax.ShapeDtypeStruct((1, 128), F32),
                   jax.ShapeDtypeStruct((1, D_MODEL), F32)],
        scratch_shapes=[pltpu.VMEM((m, 1024), BF16)],
        compiler_params=pltpu.CompilerParams(vmem_limit_bytes=VMEM_LIMIT),
    )(dmk, dmv, kv_raw, kn_mem, mem, g_mem, mem_n, w_kv)


def _fox_gate_bwd(dcq4, dck4, proj, b_forget128):
    s = dck4.shape[1]
    tm = min(512, s)
    nt = s // tm

    def body(dcq_ref, dck_ref, p_ref, b_ref, dfl_ref, db_ref, carry_ref):
        i = pl.program_id(0)

        @pl.when(i == 0)
        def _():
            carry_ref[...] = jnp.zeros_like(carry_ref)
            db_ref[...] = jnp.zeros_like(db_ref)

        dcv = jnp.zeros((tm, 128), F32)
        for hp in range(4):
            by_query = jnp.concatenate([dcq_ref[hp], jnp.zeros((120, tm), F32)], axis=0).T
            d = by_query - dck_ref[hp]
            dcv = dcv + (d if hp == 0 else pltpu.roll(d, 2 * hp, 1))
        dlogf = jnp.dot(_tri(tm, False), dcv, precision=lax.Precision.HIGHEST, preferred_element_type=F32) + carry_ref[...]
        carry_ref[...] += jnp.sum(dcv, axis=0, keepdims=True)
        z = p_ref[...] + b_ref[...]
        dfl = dlogf * (1.0 / (1.0 + jnp.exp(z)))
        dfl_ref[...] = dfl.astype(BF16)
        db_ref[...] += jnp.sum(dfl, axis=0, keepdims=True)

    return pl.pallas_call(
        body, name="fox_gate_bwd", grid=(nt,),
        in_specs=[pl.BlockSpec((4, 8, tm), lambda i: (0, 0, nt - 1 - i)),
                  pl.BlockSpec((4, tm, 128), lambda i: (0, nt - 1 - i, 0)),
                  pl.BlockSpec((tm, 128), lambda i: (nt - 1 - i, 0)), _full((1, 128))],
        out_specs=[pl.BlockSpec((tm, 128), lambda i: (nt - 1 - i, 0)), _full((1, 128))],
        out_shape=[jax.ShapeDtypeStruct((s, 128), BF16), jax.ShapeDtypeStruct((1, 128), F32)],
        scratch_shapes=[pltpu.VMEM((1, 128), F32)],
        compiler_params=_params(("arbitrary",)),
    )(dcq4, dck4, proj, b_forget128)


def _proj_pre_bwd(dproj, proj, dqf, dkf, dvf, dqm, dqa, dka, dva, dfl, gq_fox, gk_fox, gq_mem, gq_swa, gk_swa):
    s = proj.shape[0]
    tm = min(256, s)

    def body(dp_in, p_ref, dqf_ref, dkf_ref, dvf_ref, dqm_ref, dqa_ref, dka_ref, dva_ref, dfl_ref,
             gqf, gkf, gqm, gqa, gka, dp_ref, dgn_ref):
        i = pl.program_id(0)

        @pl.when(i == 0)
        def _():
            dgn_ref[...] = jnp.zeros_like(dgn_ref)

        def norm_bwd(off, width, hd, g_ref, dn_ref, slot):
            acc = jnp.zeros((1, 128), F32)
            for b in range(width // 128):
                v = p_ref[:, off + b * 128: off + (b + 1) * 128].astype(F32)
                r = lax.rsqrt(_group_mean(v * v, hd) + EPS)
                n = v * r
                dn = dn_ref[b * 128:(b + 1) * 128, :].T if slot == 0 else dn_ref[:, b * 128:(b + 1) * 128].astype(F32)
                acc = acc + jnp.sum(dn * n, axis=0, keepdims=True)
                dng = dn * g_ref[...]
                dp_ref[:, off + b * 128: off + (b + 1) * 128] = (r * (dng - n * _group_mean(dng * n, hd))).astype(BF16)
            dgn_ref[slot:slot + 1, :] += acc

        norm_bwd(H_QF, 512, HEAD, gqf, dqf_ref, 0)
        norm_bwd(H_KF, 512, HEAD, gkf, dkf_ref, 1)
        dp_ref[:, H_VF:H_VF + 512] = dvf_ref[...].astype(BF16)
        norm_bwd(H_QM, 512, MEM_HEAD, gqm, dqm_ref, 2)
        norm_bwd(H_QA, 512, HEAD, gqa, dqa_ref, 3)
        norm_bwd(H_KA, 128, HEAD, gka, dka_ref, 4)
        dp_ref[:, H_VA:H_VA + 128] = dva_ref[...].astype(BF16)
        dp_ref[:, H_FL:H_FL + 128] = dfl_ref[...]
        dp_ref[:, H_FL + 128:HALF_W] = jnp.zeros((tm, HALF_W - H_FL - 128), BF16)

    row = lambda w: pl.BlockSpec((tm, w), lambda i: (i, 0))
    g_spec = _full((1, 128))
    return pl.pallas_call(
        body, name="proj_pre_bwd", grid=(s // tm,),
        in_specs=[pl.BlockSpec(memory_space=pl.ANY), pl.BlockSpec((tm, HALF_W), lambda i: (i, 1)),
                  pl.BlockSpec((512, tm), lambda i: (0, i)), row(512), row(512), row(512), row(512),
                  row(128), row(128), row(128), g_spec, g_spec, g_spec, g_spec, g_spec],
        out_specs=[pl.BlockSpec((tm, HALF_W), lambda i: (i, 1)), _full((8, 128))],
        out_shape=[jax.ShapeDtypeStruct((s, PROJ_W), BF16), jax.ShapeDtypeStruct((8, 128), F32)],
        input_output_aliases={0: 0},
        compiler_params=_params(("arbitrary",)),
    )(dproj, proj, dqf, dkf, dvf, dqm, dqa, dka, dva, dfl, gq_fox, gk_fox, gq_mem, gq_swa, gk_swa)


def _in_bwd_x(dproj, w_in_p, x, g_mix, dx1):
    s = x.shape[0]
    tm = min(256, s)

    def body(dp_ref, w_ref, x_ref, g_ref, dx1_ref, gx_ref, dg_ref):
        i = pl.program_id(0)

        @pl.when(i == 0)
        def _():
            dg_ref[...] = jnp.zeros_like(dg_ref)

        dx, dg = _rms_bwd(x_ref[...], g_ref[...], _dot(dp_ref[...], w_ref[...], NT), dx1_ref[...])
        gx_ref[...] = dx
        dg_ref[...] += dg

    row = pl.BlockSpec((tm, D_MODEL), lambda i: (i, 0))
    return pl.pallas_call(
        body, name="in_bwd_x", grid=(s // tm,),
        in_specs=[pl.BlockSpec((tm, PROJ_W), lambda i: (i, 0)), _full(w_in_p.shape), row, _full((1, D_MODEL)), row],
        out_specs=[row, _full((1, D_MODEL))],
        out_shape=[jax.ShapeDtypeStruct((s, D_MODEL), F32), jax.ShapeDtypeStruct((1, D_MODEL), F32)],
        compiler_params=_params(("arbitrary",)),
    )(dproj, w_in_p, x, g_mix, dx1)


def _rel_bias_bwd(dbias, bucket):
    def body(db_ref, bk_ref, o_ref):
        bk = bk_ref[...]
        lane = lax.broadcasted_iota(jnp.int32, (1, 128), 1)
        for b in range(REL_BUCKETS):
            sel = bk == b
            acc = jnp.zeros((1, 128), F32)
            for h in range(SWA_HEADS):
                tot = jnp.sum(jnp.sum(jnp.where(sel, db_ref[h], 0.0), axis=-1, keepdims=True), axis=0, keepdims=True)
                acc = jnp.where(lane == h, tot, acc)
            o_ref[:, b * 128:(b + 1) * 128] = acc

    return pl.pallas_call(
        body, name="rel_bias_bwd",
        out_shape=jax.ShapeDtypeStruct((1, REL_BUCKETS * 128), F32),
        compiler_params=pltpu.CompilerParams(vmem_limit_bytes=VMEM_LIMIT),
    )(dbias, bucket)


def _my_place():
    return lax.axis_index("x"), lax.axis_index("y"), lax.axis_index("c")


def _peer(place, k):
    x, y, c = place
    return (1 - x if k & 4 else x, 1 - y if k & 2 else y, 1 - c if k & 1 else c)


def _index(place):
    x, y, c = place
    return 4 * x + 2 * y + c


HBM_SPEC = pl.BlockSpec(memory_space=pltpu.HBM)
SEM_SPEC = pl.BlockSpec(memory_space=pltpu.SEMAPHORE)
DATAFLOW = pltpu.SideEffectType.DATAFLOW_SIDE_EFFECTING


ALL_PEERS = tuple(range(1, N_DEV))
SAME_CORE = (2, 4, 6)
OWN = N_DEV - 1


def _split_copy(src_ref, land_ref, send_sems, recv_sems, me, k, gather):
    peer = _peer(me, k)
    if gather:
        src, dst = src_ref, land_ref.at[_index(me)]
    else:
        src, dst = src_ref.at[_index(peer)], land_ref.at[k - 1]
    return pltpu.make_async_remote_copy(src_ref=src, dst_ref=dst, send_sem=send_sems.at[k - 1], recv_sem=recv_sems.at[k - 1],
                                        device_id=peer, device_id_type=MESH)


def _own_copy(src_ref, land_ref, recv_sems, me, gather):
    if gather:
        src, dst = src_ref, land_ref.at[_index(me)]
    else:
        src, dst = src_ref.at[_index(me)], land_ref.at[OWN]
    return pltpu.make_async_copy(src, dst, recv_sems.at[OWN])


def _split_start(srcs, gather, name, peers=ALL_PEERS, after=None):
    n = len(srcs)
    extra = [] if after is None else [after]

    def body(*refs):
        refs = refs[:2 * n] + refs[2 * n + len(extra):]
        src_refs, land_refs = refs[:n], refs[n:2 * n]
        send_sems, recv_sems, token = refs[2 * n:3 * n], refs[3 * n:4 * n], refs[-1]
        me = _my_place()
        for w in range(n):
            for k in peers:
                _split_copy(src_refs[w], land_refs[w], send_sems[w], recv_sems[w], me, k, gather).start()
            _own_copy(src_refs[w], land_refs[w], recv_sems[w], me, gather).start()
        token[...] = jnp.zeros_like(token)

    lands = [lax.empty((N_DEV,) + (a.shape if gather else a.shape[1:]), a.dtype) for a in srcs]
    sems = [pltpu.SemaphoreType.DMA((N_DEV,))] * (2 * n)
    hbm = [pltpu.HBM(a.shape, a.dtype) for a in list(srcs) + lands]
    outs = pl.pallas_call(
        body, name=name,
        out_shape=(*sems, *hbm, jax.ShapeDtypeStruct((8, 128), F32)),
        in_specs=(HBM_SPEC,) * (2 * n) + (pl.BlockSpec(memory_space=pl.ANY),) * len(extra),
        out_specs=(SEM_SPEC,) * (2 * n) + (HBM_SPEC,) * (2 * n) + (pl.BlockSpec(memory_space=pltpu.VMEM),),
        input_output_aliases={i: 2 * n + i for i in range(2 * n)},
        compiler_params=pltpu.CompilerParams(has_side_effects=DATAFLOW),
    )(*[pltpu.with_memory_space_constraint(a, pltpu.HBM) for a in list(srcs) + lands], *extra)
    return list(outs[:n]), list(outs[n:2 * n]), list(outs[2 * n:3 * n]), list(outs[3 * n:4 * n]), outs[-1]


def _split_wait(started, w, after, gather, name):
    send_sems, recv_sems, srcs, lands, _ = started

    def body(src_ref, land_ref, send_sems, recv_sems, after_ref, src_out, land_out):
        me = _my_place()
        for k in ALL_PEERS:
            cp = _split_copy(src_ref, land_ref, send_sems, recv_sems, me, k, gather)
            cp.wait_send()
            cp.wait_recv()
        _own_copy(src_ref, land_ref, recv_sems, me, gather).wait()

    return pl.pallas_call(
        body, name=name,
        out_shape=(pltpu.HBM(srcs[w].shape, srcs[w].dtype), pltpu.HBM(lands[w].shape, lands[w].dtype)),
        in_specs=(HBM_SPEC, HBM_SPEC, SEM_SPEC, SEM_SPEC, pl.BlockSpec(memory_space=pl.ANY)),
        out_specs=(HBM_SPEC, HBM_SPEC), input_output_aliases={0: 0, 1: 1},
        compiler_params=pltpu.CompilerParams(has_side_effects=DATAFLOW),
    )(srcs[w], lands[w], send_sems[w], recv_sems[w], after)[1]


def _forward_copy(land_ref, send_sems, recv_sems, me, j, incoming):
    sibling = _peer(me, 1)
    rows = land_ref.at[_index(_peer(sibling if incoming else me, SAME_CORE[j]))]
    return pltpu.make_async_remote_copy(src_ref=rows, dst_ref=rows, send_sem=send_sems.at[j], recv_sem=recv_sems.at[j],
                                        device_id=sibling, device_id_type=MESH)


def _forward_start(started, after, name):
    send_a, recv_a, srcs, lands, _ = started

    def body(src_ref, land_ref, send_a, recv_a, after_ref, send_b, recv_b, src_out, land_out):
        me = _my_place()
        for j, k in enumerate(SAME_CORE):
            _split_copy(src_ref, land_ref, send_a, recv_a, me, k, True).wait_recv()
            _forward_copy(land_ref, send_b, recv_b, me, j, False).start()

    sems = pltpu.SemaphoreType.DMA((len(SAME_CORE),))
    return pl.pallas_call(
        body, name=name,
        out_shape=(sems, sems, pltpu.HBM(srcs[0].shape, srcs[0].dtype), pltpu.HBM(lands[0].shape, lands[0].dtype)),
        in_specs=(HBM_SPEC, HBM_SPEC, SEM_SPEC, SEM_SPEC, pl.BlockSpec(memory_space=pl.ANY)),
        out_specs=(SEM_SPEC, SEM_SPEC, HBM_SPEC, HBM_SPEC), input_output_aliases={0: 2, 1: 3},
        compiler_params=pltpu.CompilerParams(has_side_effects=DATAFLOW),
    )(srcs[0], lands[0], send_a[0], recv_a[0], after)


def _forward_wait(started, forwarded, name):
    send_a, recv_a, _, _, _ = started
    send_b, recv_b, src, land = forwarded

    def body(src_ref, land_ref, send_a, recv_a, send_b, recv_b, src_out, land_out):
        me = _my_place()
        _own_copy(src_ref, land_ref, recv_a, me, True).wait()
        for k in (1,) + SAME_CORE:
            _split_copy(src_ref, land_ref, send_a, recv_a, me, k, True).wait_send()
        _split_copy(src_ref, land_ref, send_a, recv_a, me, 1, True).wait_recv()
        for j in range(len(SAME_CORE)):
            _forward_copy(land_ref, send_b, recv_b, me, j, False).wait_send()
            _forward_copy(land_ref, send_b, recv_b, me, j, True).wait_recv()

    return pl.pallas_call(
        body, name=name,
        out_shape=(pltpu.HBM(src.shape, src.dtype), pltpu.HBM(land.shape, land.dtype)),
        in_specs=(HBM_SPEC, HBM_SPEC, SEM_SPEC, SEM_SPEC, SEM_SPEC, SEM_SPEC),
        out_specs=(HBM_SPEC, HBM_SPEC), input_output_aliases={0: 0, 1: 1},
        compiler_params=pltpu.CompilerParams(has_side_effects=DATAFLOW),
    )(src, land, send_a[0], recv_a[0], send_b, recv_b)[1]


def _adam_math(w, g, m, v):
    m2 = ADAM_B1 * m + (1.0 - ADAM_B1) * g
    v2 = ADAM_B2 * v + (1.0 - ADAM_B2) * (g * g)
    m_hat = m2 / (1.0 - ADAM_B1 ** ADAM_STEP)
    v_hat = v2 / (1.0 - ADAM_B2 ** ADAM_STEP)
    delta = -ADAM_LR * (m_hat / (jnp.sqrt(v_hat) + ADAM_EPS) + ADAM_WD * w)
    return delta, m2, v2


def _adamw(land, w, m, v, name):
    a, b = w.shape
    bp = land.shape[2]
    ta = min(128, a)

    def body(p_ref, w_ref, m_ref, v_ref, g_ref, d_ref, m2_ref, v2_ref):
        g = p_ref[0, :, 0:b].astype(F32)
        for k in range(1, N_DEV):
            g = g + p_ref[k, :, 0:b].astype(F32)
        delta, m2, v2 = _adam_math(w_ref[...], g, m_ref[...], v_ref[...])
        g_ref[...] = g
        d_ref[...] = delta
        m2_ref[...] = m2
        v2_ref[...] = v2

    blk = pl.BlockSpec((ta, b), lambda i: (i, 0))
    sd = jax.ShapeDtypeStruct((a, b), F32)
    return pl.pallas_call(
        body, name=name, grid=(a // ta,),
        in_specs=[pl.BlockSpec((N_DEV, ta, bp), lambda i: (0, i, 0)), blk, blk, blk],
        out_specs=[blk, blk, blk, blk], out_shape=[sd, sd, sd, sd],
        compiler_params=_params(("parallel",)),
    )(land, w, m, v)


def _bucket_table():
    t_loc = jnp.arange(SWA_BLOCK)[:, None] + SWA_BLOCK
    s_loc = jnp.arange(2 * SWA_BLOCK)[None, :]
    dist = t_loc - s_loc
    max_exact = REL_BUCKETS // 2
    d = jnp.maximum(dist, 0)
    df = jnp.maximum(d, 1).astype(F32)
    large = max_exact + (jnp.log(df / max_exact) / math.log(REL_MAX_DIST / max_exact) * (REL_BUCKETS - max_exact)).astype(jnp.int32)
    large = jnp.minimum(large, REL_BUCKETS - 1)
    bucket = jnp.where(d < max_exact, d, large)
    band = (dist >= 0) & (dist < SWA_BLOCK)
    return bucket, band


def _tile2(g):
    return jnp.concatenate([g, g], axis=1) if g.shape[1] == HEAD else g


SHARD_W = 737
SHARD_WP = 768
IN_WIDTH = N_DEV * SHARD_W
SEGMENTS = ((GL0, 2824, 3072), (QF0, 768, 512), (KF0, 1280, 512), (VF0, 1792, 512), (QM0, 2312, 512),
            (QA0, 0, 512), (KA0, 512, 128), (VA0, 640, 128), (FL0, 2304, 8))


def _lane_plan(sources):
    plan = []
    for t in range(len(sources) // 128):
        groups = {}
        for lane in range(128):
            src = sources[128 * t + lane]
            if src is not None:
                slab, col = src
                groups.setdefault((slab, col // 128, (lane - col) % 128), []).append(lane)
        tile = []
        for key, lanes in groups.items():
            assert lanes == list(range(lanes[0], lanes[-1] + 1))
            tile.append((key, lanes[0], lanes[-1] + 1))
        plan.append(tile)
    return plan


def _assemble(tile_plan, load, rows):
    lane = lax.broadcasted_iota(jnp.int32, (1, 128), 1)
    out = jnp.zeros((rows, 128), F32)
    for (slab, st, roll), lo, hi in tile_plan:
        v = load(slab, st)
        if roll:
            v = pltpu.roll(v, roll, 1)
        out = v if (lo, hi) == (0, 128) else jnp.where((lane >= lo) & (lane < hi), v, out)
    return out


def _w_in_from_shards(land):
    ref_col = [None] * PROJ_W
    for p0, r0, n in SEGMENTS:
        for i in range(n):
            ref_col[p0 + i] = divmod(r0 + i, SHARD_W)
    plan = _lane_plan(ref_col)
    d_model = land.shape[1]
    tm = 256

    def body(land_ref, o_ref):
        load = lambda slab, st: land_ref[slab, :, st * 128:(st + 1) * 128].astype(F32)
        for t, tile_plan in enumerate(plan):
            o_ref[:, t * 128:(t + 1) * 128] = _assemble(tile_plan, load, tm).astype(BF16)

    return pl.pallas_call(
        body, name="w_in_from_shards", grid=(d_model // tm,),
        in_specs=[pl.BlockSpec((N_DEV, tm, SHARD_WP), lambda i: (0, i, 0))],
        out_specs=pl.BlockSpec((tm, PROJ_W), lambda i: (i, 0)),
        out_shape=jax.ShapeDtypeStruct((d_model, PROJ_W), BF16),
        compiler_params=_params(("parallel",)),
    )(land)


def _dw_in_to_parts(dwp):
    padded_col = [None] * IN_WIDTH
    for p0, r0, n in SEGMENTS:
        for i in range(n):
            padded_col[r0 + i] = p0 + i
    sources = []
    for d in range(N_DEV):
        sources += [(0, padded_col[SHARD_W * d + c]) if c < SHARD_W else None for c in range(SHARD_WP)]
    plan = _lane_plan(sources)
    d_model = dwp.shape[0]
    tm = 256
    tiles = SHARD_WP // 128

    def body(dw_ref, o_ref):
        load = lambda slab, st: dw_ref[:, st * 128:(st + 1) * 128].astype(F32)
        for t, tile_plan in enumerate(plan):
            d, c = divmod(t, tiles)
            o_ref[d, :, c * 128:(c + 1) * 128] = _assemble(tile_plan, load, tm).astype(BF16)

    return pl.pallas_call(
        body, name="dw_in_to_parts", grid=(d_model // tm,),
        in_specs=[pl.BlockSpec((tm, PROJ_W), lambda i: (i, 0))],
        out_specs=pl.BlockSpec((N_DEV, tm, SHARD_WP), lambda i: (0, i, 0)),
        out_shape=jax.ShapeDtypeStruct((N_DEV, d_model, SHARD_WP), BF16),
        compiler_params=_params(("parallel",)),
    )(dwp)


def _cast_shards(shards):
    names = list(shards)

    def body(*refs):
        for src, dst in zip(refs[:len(names)], refs[len(names):]):
            if dst.shape != src.shape:
                dst[...] = jnp.zeros(dst.shape, BF16)
                dst[:, 0:src.shape[1]] = src[...].astype(BF16)
            else:
                dst[...] = src[...].astype(BF16)

    out_shape = [jax.ShapeDtypeStruct((shards[n].shape[0], SHARD_WP if n == "w_in" else shards[n].shape[1]), BF16)
                 for n in names]
    outs = pl.pallas_call(body, name="cast_shards", out_shape=out_shape,
                          compiler_params=pltpu.CompilerParams(vmem_limit_bytes=VMEM_LIMIT))(*[shards[n] for n in names])
    return dict(zip(names, outs))


def _tie(x, *tokens):
    for t in tokens:
        if t is not None:
            x = x + t[0:1, 0:1]
    return x


def _local_step(x, mem, target, p, getw, emit, deps=()):
    s = x.shape[0]
    bucket, band = _bucket_table()
    bucket_m = jnp.where(band, bucket, -1).astype(jnp.int32)
    bias = _bias_table(p["rel_bias"], bucket_m)
    bucket_t = jnp.transpose(bucket_m)
    bias_t = _bias_table(p["rel_bias"], bucket_t)
    gqf, gkf, gqa, gka = _tile2(p["qn_fox"]), _tile2(p["kn_fox"]), _tile2(p["qn_swa"]), _tile2(p["kn_swa"])
    gqm = p["qn_mem"]
    bf128 = jnp.pad(p["b_forget"], ((0, 0), (0, 120)))
    sink = p["sink_swa"].reshape(8)

    h = _rms_fwd(x, p["g_mix"], "rms_mix", tuple(deps) + (bias, bias_t))
    w_in = getw("w_in", h)
    proj = _mm(h, w_in, "nn", BF16, 512, 1536, 1024, "proj")
    fl = _mm(h, w_in[:, FL0:FL0 + 128], "nn", F32, 512, 128, 1024, "proj_fl")
    qf, kf, vf, qm, qa, ka, va, qf_t, vf_t = _proj_post(proj, gqf, gkf, gqm, gqa, gka)
    cc4, ca4 = _fox_gate_fwd(fl, bf128)
    w_kv = getw("w_mem_kv", cc4)
    mem_n, kv_raw, mk, mv = _memkv_fwd(mem, p["g_mem"], w_kv, p["kn_mem"])
    kp = jnp.pad(ka, ((SWA_BLOCK, 0), (0, 0)))
    vp = jnp.pad(va, ((SWA_BLOCK, 0), (0, 0)))
    oa = _swa_fwd(qa, kp, vp, bias, sink)
    of, lse4, of_t = _fox_fwd(qf, kf, vf_t, ca4)
    om = _mem_fwd(qm, mk, mv)
    wa, wf, wm, w_out = getw("w_o_swa", oa), getw("w_o_fox", oa), getw("w_o_mem", oa), getw("w_out", oa)
    x1, hm, merged = _merge_fwd(x, oa, of, om, proj, p["b_gate"], wa, wf, wm, w_out, p["g_mlp"])
    w_up = getw("w_mlp_up", of)
    u = _mlp_up(hm, w_up)
    w_down = getw("w_mlp_down", hm)
    dy, dy_b, loss = _mlp_down_loss(u, w_down, x1, target)

    da = _mlp_bwd_act(dy_b, w_down, u)
    t_down = emit({"w_mlp_down": _mm(u, dy_b, "tn", BF16, 1024, 1024, 2048, "dw_down")})
    dx1, dg_mlp = _mlp_bwd_x(da, w_up, x1, dy, _tie(p["g_mlp"], t_down))
    t_up = emit({"w_mlp_up": _mm(hm, da, "tn", BF16, 1024, 1024, 2048, "dw_up", column_chunks=True)})
    dproj, doa, dof_t, dom, dya, dyf, dym, db_gate = _merge_bwd(
        dx1, oa, of, om, proj, _tie(p["b_gate"], t_up), wa, wf, wm, w_out)
    dw_oa, dw_of, dw_om = _mm_tn3([oa, of, om], [dya, dyf, dym], "dw_o")
    t_o = emit({"w_out": _mm(merged, dx1, "tn", BF16, 1024, 1024, 2048, "dw_out"),
                "w_o_swa": dw_oa, "w_o_fox": dw_of, "w_o_mem": dw_om})

    dqm, dmk, dmv = _mem_bwd(qm, mk, mv, dom)
    dw_kv, dkn_mem, dg_mem = _memkv_bwd(dmk, dmv, kv_raw, _tie(p["kn_mem"], t_o), mem, p["g_mem"], mem_n, w_kv)
    t_kv = emit({"w_mem_kv": dw_kv})
    dqa, dkp, dvp, dbias, dsink = _swa_bwd(qa, kp, vp, bias_t, _tie(p["sink_swa"], t_kv).reshape(8), doa)
    dqf_t, dkf, dvf, dck4, dcq4 = _fox_bwd(qf_t, kf, vf, dof_t, of_t, cc4, lse4)

    dfl, db_forget = _fox_gate_bwd(dcq4, dck4, fl, bf128)

    dproj, dgn = _proj_pre_bwd(dproj, proj, dqf_t, dkf, dvf, dqm, dqa, dkp[SWA_BLOCK:], dvp[SWA_BLOCK:], dfl,
                               gqf, gkf, gqm, gqa, gka)
    t_in = emit({"w_in": _mm(h, dproj, "tn", BF16, 1024, 3072, 1024, "dw_in")})
    grad_x, dg_mix = _in_bwd_x(dproj, w_in, x, _tie(p["g_mix"], t_in), dx1)
    d_rel = _rel_bias_bwd(dbias, bucket_t)

    fold = lambda r: dgn[r:r + 1, 0:HEAD] + dgn[r:r + 1, HEAD:128]
    small = {
        "g_mix": dg_mix, "b_gate": db_gate, "b_forget": db_forget[:, 0:8],
        "qn_swa": fold(3), "kn_swa": fold(4), "sink_swa": dsink[:, 0].reshape(1, 8), "rel_bias": d_rel,
        "qn_fox": fold(0), "kn_fox": fold(1), "g_mem": dg_mem, "qn_mem": dgn[2:3, :], "kn_mem": dkn_mem,
        "g_mlp": dg_mlp,
    }
    return loss, grad_x, small


SMALL = ("g_mix", "b_gate", "b_forget", "qn_swa", "kn_swa", "sink_swa", "rel_bias", "qn_fox", "kn_fox", "g_mem",
         "qn_mem", "kn_mem", "g_mlp")
BIG = ("w_in", "w_mem_kv", "w_o_swa", "w_o_fox", "w_o_mem", "w_out", "w_mlp_up", "w_mlp_down")
COL_SHARDED = ("w_in", "w_o_swa", "w_o_fox", "w_o_mem", "w_mlp_up")
WEIGHTS = ("g_mix", "w_in", "b_gate", "b_forget", "qn_swa", "kn_swa", "sink_swa", "rel_bias", "qn_fox", "kn_fox", "g_mem",
           "w_mem_kv", "qn_mem", "kn_mem", "w_o_swa", "w_o_fox", "w_o_mem", "w_out", "g_mlp", "w_mlp_up", "w_mlp_down")
SMALL_SLOTS = (("g_mix", 1024), ("b_gate", 3072), ("b_forget", 128), ("qn_swa", 128), ("kn_swa", 128), ("sink_swa", 128),
               ("rel_bias", REL_BUCKETS * 128), ("qn_fox", 128), ("kn_fox", 128), ("g_mem", 1024), ("qn_mem", 128),
               ("kn_mem", 128), ("g_mlp", 1024), ("loss", 128))
SMALL_OFF = {n: sum(w for _, w in SMALL_SLOTS[:i]) for i, (n, _) in enumerate(SMALL_SLOTS)}
SMALL_ROW = sum(w for _, w in SMALL_SLOTS)


def _gathered_to_full(name, g):
    if name in COL_SHARDED:
        return jnp.transpose(g, (1, 0, 2)).reshape(g.shape[1], N_DEV * g.shape[2])
    return g.reshape(N_DEV * g.shape[1], g.shape[2])


def _full_to_parts(name, full, b):
    if name in COL_SHARDED:
        return jnp.transpose(full.reshape(full.shape[0], N_DEV, b), (1, 0, 2)).astype(BF16)
    return full.reshape(N_DEV, full.shape[0] // N_DEV, full.shape[1]).astype(BF16)


def _pack_small(grads, loss):
    pieces = []
    for n, width in SMALL_SLOTS:
        a = loss.reshape(1, 1) if n == "loss" else grads[n].reshape(1, -1)
        pieces.append(jnp.pad(a, ((0, 0), (0, width - a.shape[1]))))
    return jnp.concatenate(pieces, axis=1)


def _adamw_small(gathered, w, m, v):
    names = list(SMALL)

    def body(*refs):
        p_ref = refs[0]
        ins = refs[1:1 + 3 * len(names)]
        outs = refs[1 + 3 * len(names):]
        g_all = p_ref[0]
        for k in range(1, N_DEV):
            g_all = g_all + p_ref[k]
        for i, n in enumerate(names):
            w_ref, m_ref, v_ref = ins[3 * i:3 * i + 3]
            out = outs[4 * i:4 * i + 4]
            rows, cols = w_ref.shape
            for r in range(rows):
                off = SMALL_OFF[n] + 128 * r
                g = g_all[:, off:off + cols]
                rs = slice(r, r + 1)
                res = (g,) + _adam_math(w_ref[rs, :], g, m_ref[rs, :], v_ref[rs, :])
                for o_ref, val in zip(out, res):
                    o_ref[rs, :] = val
        outs[-1][...] = g_all[:, SMALL_OFF["loss"]:SMALL_OFF["loss"] + 128]

    args = [gathered]
    out_shape = []
    for n in names:
        args += [w[n], m[n], v[n]]
        out_shape += [jax.ShapeDtypeStruct(w[n].shape, F32)] * 4
    out_shape.append(jax.ShapeDtypeStruct((1, 128), F32))
    outs = pl.pallas_call(body, name="adamw_small", out_shape=out_shape)(*args)
    return {n: outs[4 * i:4 * i + 4] for i, n in enumerate(names)}, outs[-1]


def kernel(x, mem, g_mix, w_in, b_gate, b_forget, qn_swa, kn_swa, sink_swa, rel_bias, qn_fox, kn_fox, g_mem, w_mem_kv, qn_mem, kn_mem, w_o_swa, w_o_fox, w_o_mem, w_out, g_mlp, w_mlp_up, w_mlp_down, loss_target, m_g_mix, m_w_in, m_b_gate, m_b_forget, m_qn_swa, m_kn_swa, m_sink_swa, m_rel_bias, m_qn_fox, m_kn_fox, m_g_mem, m_w_mem_kv, m_qn_mem, m_kn_mem, m_w_o_swa, m_w_o_fox, m_w_o_mem, m_w_out, m_g_mlp, m_w_mlp_up, m_w_mlp_down, v_g_mix, v_w_in, v_b_gate, v_b_forget, v_qn_swa, v_kn_swa, v_sink_swa, v_rel_bias, v_qn_fox, v_kn_fox, v_g_mem, v_w_mem_kv, v_qn_mem, v_kn_mem, v_w_o_swa, v_w_o_fox, v_w_o_mem, v_w_out, v_g_mlp, v_w_mlp_up, v_w_mlp_down):
    wts = dict(g_mix=g_mix, w_in=w_in, b_gate=b_gate, b_forget=b_forget, qn_swa=qn_swa, kn_swa=kn_swa, sink_swa=sink_swa,
               rel_bias=rel_bias, qn_fox=qn_fox, kn_fox=kn_fox, g_mem=g_mem, w_mem_kv=w_mem_kv, qn_mem=qn_mem, kn_mem=kn_mem,
               w_o_swa=w_o_swa, w_o_fox=w_o_fox, w_o_mem=w_o_mem, w_out=w_out, g_mlp=g_mlp, w_mlp_up=w_mlp_up,
               w_mlp_down=w_mlp_down)
    mom = dict(g_mix=m_g_mix, w_in=m_w_in, b_gate=m_b_gate, b_forget=m_b_forget, qn_swa=m_qn_swa, kn_swa=m_kn_swa,
               sink_swa=m_sink_swa, rel_bias=m_rel_bias, qn_fox=m_qn_fox, kn_fox=m_kn_fox, g_mem=m_g_mem, w_mem_kv=m_w_mem_kv,
               qn_mem=m_qn_mem, kn_mem=m_kn_mem, w_o_swa=m_w_o_swa, w_o_fox=m_w_o_fox, w_o_mem=m_w_o_mem, w_out=m_w_out,
               g_mlp=m_g_mlp, w_mlp_up=m_w_mlp_up, w_mlp_down=m_w_mlp_down)
    var = dict(g_mix=v_g_mix, w_in=v_w_in, b_gate=v_b_gate, b_forget=v_b_forget, qn_swa=v_qn_swa, kn_swa=v_kn_swa,
               sink_swa=v_sink_swa, rel_bias=v_rel_bias, qn_fox=v_qn_fox, kn_fox=v_kn_fox, g_mem=v_g_mem, w_mem_kv=v_w_mem_kv,
               qn_mem=v_qn_mem, kn_mem=v_kn_mem, w_o_swa=v_w_o_swa, w_o_fox=v_w_o_fox, w_o_mem=v_w_o_mem, w_out=v_w_out,
               g_mlp=v_g_mlp, w_mlp_up=v_w_mlp_up, w_mlp_down=v_w_mlp_down)

    shards = _cast_shards({n: wts[n][0] for n in BIG})
    first = _split_start([shards["w_in"]], True, "ag_start_w_in", peers=(1,) + SAME_CORE)
    rest = _split_start([shards[n] for n in BIG[1:]], True, "ag_start_rest", after=first[4])
    full = {}

    def getw(n, after):
        if n == "w_in" and n not in full:
            forwarded = _forward_start(first, after, "ag_forward_w_in")
            full[n] = _w_in_from_shards(_forward_wait(first, forwarded, "ag_wait_w_in"))
        elif n not in full:
            land = _split_wait(rest, BIG[1:].index(n), after, True, "ag_wait_" + n)
            full[n] = land if n == "w_mlp_up" else _gathered_to_full(n, land)
        return full[n]

    exchanges = {}

    def emit(grads_by_name):
        parts = []
        for n, grad in grads_by_name.items():
            if n == "w_in":
                parts.append(_dw_in_to_parts(grad))
            else:
                parts.append(grad if n == "w_mlp_up" else _full_to_parts(n, grad, wts[n].shape[2]))
        started = _split_start(parts, False, "rs_start_" + next(iter(grads_by_name)))
        for w, n in enumerate(grads_by_name):
            exchanges[n] = (started, w)
        return started[4]

    small_p = {n: wts[n] for n in SMALL}
    loss, grad_x, small_g = _local_step(x[0], mem[0], loss_target[0], small_p, getw, emit, (first[4], rest[4]))

    packed = _pack_small(small_g, loss)
    small_gather = _split_start([packed], True, "ag_start_small")

    grads, delta, new_m, new_v = {}, {}, {}, {}

    def update(n, after):
        land = _split_wait(*exchanges[n], after, False, "rs_wait_" + n)
        g, d, m2, v2 = _adamw(land, wts[n][0], mom[n][0], var[n][0], "adamw_" + n)
        grads[n], delta[n], new_m[n], new_v[n] = g[None], d[None], m2[None], v2[None]
        return d

    after = small_gather[4]
    for n in exchanges:
        if n != "w_in":
            after = update(n, after)

    gathered = _split_wait(small_gather, 0, after, True, "ag_wait_small")
    small_out, total = _adamw_small(gathered, small_p, mom, var)
    for name, (g, d, m2, v2) in small_out.items():
        grads[name], delta[name], new_m[name], new_v[name] = g, d, m2, v2
    update("w_in", total)

    return (total[0, 0], grad_x[None], *[grads[n] for n in WEIGHTS], *[delta[n] for n in WEIGHTS],
            *[new_m[n] for n in WEIGHTS], *[new_v[n] for n in WEIGHTS])
```

```python
import math

import jax
import jax.numpy as jnp
from jax import lax
from jax.experimental import pallas as pl
from jax.experimental.pallas import tpu as pltpu

F32 = jnp.float32
BF16 = jnp.bfloat16

D_MODEL = 1024
N_MEM = 256
D_FF = 4096
HEAD = 64
SWA_HEADS = 8
SWA_BLOCK = 128
MEM_HEADS = 4
MEM_HEAD = 128
EPS = 1e-6
NEG = -1e30
REL_BUCKETS = 32
REL_MAX_DIST = 128

ADAM_LR = 0.001
ADAM_B1 = 0.9
ADAM_B2 = 0.999
ADAM_EPS = 1e-08
ADAM_WD = 0.01
ADAM_STEP = 10

GL0, QF0, KF0, VF0, QM0, QA0, KA0, VA0, FL0 = 0, 3072, 3584, 4096, 4608, 5120, 5632, 5760, 5888
PROJ_W = 6144
HALF_W = 3072
H_QF, H_KF, H_VF, H_QM, H_QA, H_KA, H_VA, H_FL = 0, 512, 1024, 1536, 2048, 2560, 2688, 2816

VMEM_LIMIT = 56 * 1024 * 1024
N_DEV = 8
MESH = pl.DeviceIdType.MESH

NN = (((1,), (0,)), ((), ()))
NT = (((1,), (1,)), ((), ()))
TN = (((0,), (0,)), ((), ()))


def _dot(a, b, dims=NN):
    return lax.dot_general(a, b, dims, preferred_element_type=F32)


def _params(sem):
    return pltpu.CompilerParams(dimension_semantics=sem, vmem_limit_bytes=VMEM_LIMIT)


def _full(shape):
    nd = len(shape)
    return pl.BlockSpec(shape, lambda *_: (0,) * nd)


def _sigmoid(z):
    return 1.0 / (1.0 + jnp.exp(-z))


def _group_mean(v, hd):
    if hd == 128:
        return jnp.mean(v, axis=-1, keepdims=True)
    r = lax.broadcasted_iota(jnp.int32, (128, 128), 0) // HEAD
    c = lax.broadcasted_iota(jnp.int32, (128, 128), 1) // HEAD
    same_head = jnp.where(r == c, 1.0 / HEAD, 0.0).astype(BF16)
    total = None
    rest = v
    for _ in range(2):
        part = rest.astype(BF16)
        rest = rest - part.astype(F32)
        term = _dot(part, same_head)
        total = term if total is None else total + term
    return total


def _mm(a, b, mode, out_dtype, tm, tn, tk, name, column_chunks=False):
    if mode == "nn":
        m, k = a.shape
        n = b.shape[1]
    elif mode == "nt":
        m, k = a.shape
        n = b.shape[0]
    else:
        k, m = a.shape
        n = b.shape[1]
    tm, tn, tk = min(tm, m), min(tn, n), min(tk, k)
    nk = k // tk
    chunk = n // N_DEV
    per_tile = tn // chunk if column_chunks else 1
    dims = {"nn": NN, "nt": NT, "tn": TN}[mode]
    a_spec = pl.BlockSpec((tk, tm), lambda j, i, kk: (kk, i)) if mode == "tn" else pl.BlockSpec((tm, tk), lambda j, i, kk: (i, kk))
    b_spec = pl.BlockSpec((tn, tk), lambda j, i, kk: (j, kk)) if mode == "nt" else pl.BlockSpec((tk, tn), lambda j, i, kk: (kk, j))

    def body(a_ref, b_ref, o_ref, *acc):
        prod = _dot(a_ref[...].astype(BF16), b_ref[...].astype(BF16), dims)

        def write(res):
            if column_chunks:
                for c in range(per_tile):
                    o_ref[c] = res[:, c * chunk:(c + 1) * chunk].astype(o_ref.dtype)
            else:
                o_ref[...] = res.astype(o_ref.dtype)

        if nk == 1:
            write(prod)
        else:
            acc_ref, = acc
            kk = pl.program_id(2)

            @pl.when(kk == 0)
            def _():
                acc_ref[...] = prod

            @pl.when(kk > 0)
            def _():
                acc_ref[...] += prod

            @pl.when(kk == nk - 1)
            def _():
                write(acc_ref[...])

    return pl.pallas_call(
        body, name=name, grid=(n // tn, m // tm, nk),
        in_specs=[a_spec, b_spec],
        out_specs=(pl.BlockSpec((per_tile, tm, chunk), lambda j, i, kk: (j, i, 0)) if column_chunks
                   else pl.BlockSpec((tm, tn), lambda j, i, kk: (i, j))),
        out_shape=jax.ShapeDtypeStruct((N_DEV, m, chunk) if column_chunks else (m, n), out_dtype),
        scratch_shapes=[pltpu.VMEM((tm, tn), F32)] if nk > 1 else [],
        compiler_params=_params(("parallel", "parallel", "arbitrary")),
    )(a, b)


def _mm_tn3(a_list, b_list, name):
    s, m = a_list[0].shape
    n = b_list[0].shape[1]
    tk = min(2048, s)
    nk = s // tk

    def body(*refs):
        a_refs, b_refs, o_refs, acc_refs = refs[0:3], refs[3:6], refs[6:9], refs[9:12]
        kk = pl.program_id(0)
        for a_ref, b_ref, o_ref, acc_ref in zip(a_refs, b_refs, o_refs, acc_refs):
            prod = _dot(a_ref[...], b_ref[...], TN)
            if nk == 1:
                o_ref[...] = prod.astype(o_ref.dtype)
                continue

            @pl.when(kk == 0)
            def _(acc_ref=acc_ref, prod=prod):
                acc_ref[...] = prod

            @pl.when(kk > 0)
            def _(acc_ref=acc_ref, prod=prod):
                acc_ref[...] += prod

            @pl.when(kk == nk - 1)
            def _(acc_ref=acc_ref, o_ref=o_ref):
                o_ref[...] = acc_ref[...].astype(o_ref.dtype)

    return pl.pallas_call(
        body, name=name, grid=(nk,),
        in_specs=[pl.BlockSpec((tk, m), lambda kk: (kk, 0))] * 3 + [pl.BlockSpec((tk, n), lambda kk: (kk, 0))] * 3,
        out_specs=[_full((m, n))] * 3,
        out_shape=[jax.ShapeDtypeStruct((m, n), BF16)] * 3,
        scratch_shapes=[pltpu.VMEM((m, n), F32)] * 3,
        compiler_params=_params(("arbitrary",)),
    )(*a_list, *b_list)


def _rms_fwd(x, g, name, deps=()):
    s, d = x.shape
    tm = min(512, s)

    def body(x_ref, g_ref, *rest):
        h_ref = rest[len(deps)]
        xv = x_ref[...]
        r = lax.rsqrt(jnp.mean(xv * xv, axis=-1, keepdims=True) + EPS)
        h_ref[...] = (xv * r * g_ref[...]).astype(BF16)

    return pl.pallas_call(
        body, name=name, grid=(s // tm,),
        in_specs=[pl.BlockSpec((tm, d), lambda i: (i, 0)), _full((1, d))] + [pl.BlockSpec(memory_space=pl.ANY)] * len(deps),
        out_specs=pl.BlockSpec((tm, d), lambda i: (i, 0)),
        out_shape=jax.ShapeDtypeStruct((s, d), BF16),
        compiler_params=_params(("parallel",)),
    )(x, g, *deps)


def _proj_post(proj, gq_fox, gk_fox, gq_mem, gq_swa, gk_swa):
    s = proj.shape[0]
    tm = min(512, s)

    def body(p_ref, gqf, gkf, gqm, gqa, gka, qf_ref, kf_ref, vf_ref, qm_ref, qa_ref, ka_ref, va_ref, qft_ref, vft_ref):
        def norm(off, width, hd, g_ref, o_ref, scaled_t_ref=None):
            for b in range(width // 128):
                v = p_ref[:, off + b * 128: off + (b + 1) * 128].astype(F32)
                r = lax.rsqrt(_group_mean(v * v, hd) + EPS)
                vn = (v * r * g_ref[...]).astype(BF16)
                o_ref[:, b * 128:(b + 1) * 128] = vn
                if scaled_t_ref is not None:
                    scaled_t_ref[b * 128:(b + 1) * 128, :] = (vn.astype(F32) * 0.125).T.astype(BF16)

        norm(H_QF, 512, HEAD, gqf, qf_ref, qft_ref)
        norm(H_KF, 512, HEAD, gkf, kf_ref)
        vf_ref[...] = p_ref[:, H_VF:H_VF + 512].astype(BF16)
        for b in range(4):
            vft_ref[b * 128:(b + 1) * 128, :] = p_ref[:, H_VF + b * 128:H_VF + (b + 1) * 128].astype(F32).T.astype(BF16)
        norm(H_QM, 512, MEM_HEAD, gqm, qm_ref)
        norm(H_QA, 512, HEAD, gqa, qa_ref)
        norm(H_KA, 128, HEAD, gka, ka_ref)
        va_ref[...] = p_ref[:, H_VA:H_VA + 128].astype(BF16)

    g_spec = _full((1, 128))
    o512 = pl.BlockSpec((tm, 512), lambda i: (i, 0))
    o128 = pl.BlockSpec((tm, 128), lambda i: (i, 0))
    s512 = jax.ShapeDtypeStruct((s, 512), BF16)
    s128 = jax.ShapeDtypeStruct((s, 128), BF16)
    return pl.pallas_call(
        body, name="proj_post", grid=(s // tm,),
        in_specs=[pl.BlockSpec((tm, HALF_W), lambda i: (i, 1)), g_spec, g_spec, g_spec, g_spec, g_spec],
        out_specs=[o512, o512, o512, o512, o512, o128, o128] + [pl.BlockSpec((512, tm), lambda i: (0, i))] * 2,
        out_shape=[s512, s512, s512, s512, s512, s128, s128] + [jax.ShapeDtypeStruct((512, s), BF16)] * 2,
        compiler_params=_params(("parallel",)),
    )(proj, gq_fox, gk_fox, gq_mem, gq_swa, gk_swa)


def _tri(n, lower):
    r = lax.broadcasted_iota(jnp.int32, (n, n), 0)
    c = lax.broadcasted_iota(jnp.int32, (n, n), 1)
    return jnp.where((c <= r) if lower else (c >= r), 1.0, 0.0).astype(F32)


def _fox_gate_fwd(proj, b_forget128):
    s = proj.shape[0]
    tm = min(512, s)

    def body(p_ref, b_ref, cc_ref, ca_ref, carry_ref):
        i = pl.program_id(0)

        @pl.when(i == 0)
        def _():
            carry_ref[...] = jnp.zeros_like(carry_ref)

        z = p_ref[...] + b_ref[...]
        logf = jnp.minimum(z, 0.0) - jnp.log(1.0 + jnp.exp(-jnp.abs(z)))
        c = jnp.dot(_tri(tm, True), logf, precision=lax.Precision.HIGHEST, preferred_element_type=F32) + carry_ref[...]
        carry_ref[...] = c[tm - 1:tm, :]
        lane = lax.broadcasted_iota(jnp.int32, (tm, 128), 1)
        for hp in range(4):
            cc_ref[hp] = c if hp == 0 else pltpu.roll(c, 128 - 2 * hp, 1)
            aug = jnp.zeros((tm, 128), F32)
            for e in range(2):
                rest = jnp.broadcast_to(c[:, 2 * hp + e:2 * hp + e + 1], (tm, 128))
                for part in range(3):
                    piece = rest.astype(BF16).astype(F32)
                    aug = jnp.where(lane == HEAD * (1 - e) + part, piece, aug)
                    rest = rest - piece
            ca_ref[hp] = aug.astype(BF16)

    return pl.pallas_call(
        body, name="fox_gate_fwd", grid=(s // tm,),
        in_specs=[pl.BlockSpec((tm, 128), lambda i: (i, 0)), _full((1, 128))],
        out_specs=[pl.BlockSpec((4, tm, 128), lambda i: (0, i, 0))] * 2,
        out_shape=[jax.ShapeDtypeStruct((4, s, 128), F32), jax.ShapeDtypeStruct((4, s, 128), BF16)],
        scratch_shapes=[pltpu.VMEM((1, 128), F32)],
        compiler_params=_params(("arbitrary",)),
    )(proj, b_forget128)


def _memkv_fwd(mem, g_mem, w_kv, kn_mem):
    m = mem.shape[0]

    def body(mem_ref, g_ref, w_ref, kn_ref, memn_ref, kv_ref, mk_ref, mv_ref):
        xv = mem_ref[...]
        r = lax.rsqrt(jnp.mean(xv * xv, axis=-1, keepdims=True) + EPS)
        mn = (xv * r * g_ref[...]).astype(BF16)
        memn_ref[...] = mn
        kv = _dot(mn, w_ref[...])
        kv_ref[...] = kv
        for h in range(MEM_HEADS):
            v = kv[:, h * 128:(h + 1) * 128]
            rr = lax.rsqrt(jnp.mean(v * v, axis=-1, keepdims=True) + EPS)
            mk_ref[:, h * 128:(h + 1) * 128] = (v * rr * kn_ref[...]).astype(BF16)
        mv_ref[...] = kv[:, 512:1024].astype(BF16)

    return pl.pallas_call(
        body, name="memkv_fwd",
        out_shape=[jax.ShapeDtypeStruct((m, D_MODEL), BF16), jax.ShapeDtypeStruct((m, 1024), F32),
                   jax.ShapeDtypeStruct((m, 512), BF16), jax.ShapeDtypeStruct((m, 512), BF16)],
        compiler_params=pltpu.CompilerParams(vmem_limit_bytes=VMEM_LIMIT),
    )(mem, g_mem, w_kv, kn_mem)


def _bias_table(rel_bias, bucket):
    def body(rb_ref, bk_ref, o_ref):
        bk = bk_ref[...]
        for h in range(SWA_HEADS):
            acc = jnp.zeros(bk.shape, F32)
            for b in range(REL_BUCKETS):
                acc = jnp.where(bk == b, rb_ref[b, h], acc)
            o_ref[h] = acc

    return pl.pallas_call(
        body, name="bias_table",
        in_specs=[pl.BlockSpec(memory_space=pltpu.SMEM), pl.BlockSpec(memory_space=pltpu.VMEM)],
        out_shape=jax.ShapeDtypeStruct((SWA_HEADS,) + bucket.shape, F32),
    )(rel_bias, bucket)


def _swa_valid(n):
    row = lax.broadcasted_iota(jnp.int32, (SWA_BLOCK, 2 * SWA_BLOCK), 0)
    col = lax.broadcasted_iota(jnp.int32, (SWA_BLOCK, 2 * SWA_BLOCK), 1)
    dist = row + SWA_BLOCK - col
    return (dist >= 0) & (dist < SWA_BLOCK) & ((col >= SWA_BLOCK) | (n > 0))


def _swa_fwd(qa, kp, vp, bias, sink):
    s = qa.shape[0]
    nb = s // SWA_BLOCK

    def body(sink_ref, q_ref, kp_ref, vp_ref, bias_ref, o_ref):
        n = pl.program_id(0)
        start = pl.multiple_of(n * SWA_BLOCK, SWA_BLOCK)
        k2 = kp_ref[pl.ds(start, 2 * SWA_BLOCK), :]
        v2 = vp_ref[pl.ds(start, 2 * SWA_BLOCK), :]
        valid = _swa_valid(n)
        heads = range(SWA_HEADS)
        hs = lambda h: slice(h * HEAD, (h + 1) * HEAD)
        sc = [jnp.where(valid, _dot(q_ref[:, hs(h)], k2[:, hs(h // 4)], NT) * 0.125 + bias_ref[h], NEG) for h in heads]
        pn = []
        for h in heads:
            sk = sink_ref[h]
            mx = jnp.maximum(jnp.max(sc[h], axis=-1, keepdims=True), sk)
            p = jnp.exp(sc[h] - mx)
            den = jnp.sum(p, axis=-1, keepdims=True) + jnp.exp(sk - mx)
            pn.append((p / den).astype(BF16))
        outs = [_dot(pn[h], v2[:, hs(h // 4)]).astype(BF16) for h in heads]
        for h in heads:
            o_ref[:, hs(h)] = outs[h]

    return pl.pallas_call(
        body, name="swa_fwd", grid=(nb,),
        in_specs=[pl.BlockSpec(memory_space=pltpu.SMEM),
                  pl.BlockSpec((SWA_BLOCK, 512), lambda n: (n, 0)),
                  _full(kp.shape), _full(vp.shape), _full(bias.shape)],
        out_specs=pl.BlockSpec((SWA_BLOCK, 512), lambda n: (n, 0)),
        out_shape=jax.ShapeDtypeStruct((s, 512), BF16),
        compiler_params=_params(("parallel",)),
    )(sink, qa, kp, vp, bias)


def _head_mask(e):
    lane = lax.broadcasted_iota(jnp.int32, (1, 128), 1)
    return (lane >= e * HEAD) & (lane < (e + 1) * HEAD)


FOX_FWD_T = 1024
FOX_BWD_T = 512


def _head_rows(e):
    row = lax.broadcasted_iota(jnp.int32, (128, 1), 0)
    return (row >= e * HEAD) & (row < (e + 1) * HEAD)


def _fox_fwd(q, k, v_t, ca4):
    s = q.shape[0]
    t = min(FOX_FWD_T, s)
    nq = s // t

    def body(q_ref, k_ref, vt_ref, ca_ref, o_ref, lse_ref, ot_ref):
        i = pl.program_id(1)
        qs = q_ref[...] * jnp.asarray(0.125, BF16)
        lane = lax.broadcasted_iota(jnp.int32, (1, 128), 1)
        minus = [jnp.where((lane >= HEAD * (1 - e)) & (lane < HEAD * (1 - e) + 3), -1.0, 0.0).astype(BF16) for e in range(2)]
        qe = [jnp.where(_head_mask(e), qs, jnp.broadcast_to(minus[e], qs.shape)) for e in range(2)]

        def block(carry, key0, nkeys, q0, nqs, masked):
            ks = pl.ds(pl.multiple_of(key0, 128), nkeys)
            kj = k_ref[ks, :]
            caj = ca_ref[0, ks, :]
            vtj = vt_ref[:, ks]
            out = []
            for e in range(2):
                m_all, acc_all = carry[2 * e], carry[2 * e + 1]
                m, acc = m_all[:, q0:q0 + nqs], acc_all[:, q0:q0 + nqs]
                st = _dot(jnp.where(_head_mask(e), kj, caj), qe[e][q0:q0 + nqs, :], NT)
                if masked:
                    krow = lax.broadcasted_iota(jnp.int32, (nkeys, nqs), 0) + key0
                    qcol = lax.broadcasted_iota(jnp.int32, (nkeys, nqs), 1) + (i * t + q0)
                    st = jnp.where(krow <= qcol, st, NEG)
                m_new = jnp.maximum(m, jnp.max(st, axis=0, keepdims=True))
                alpha = jnp.exp(m - m_new)
                pt = jnp.exp(st - m_new).astype(BF16)
                vte = jnp.where(_head_rows(e), vtj, jnp.ones_like(vtj))
                acc_new = alpha * acc + _dot(vte, pt)
                if nqs < t:
                    m_new = jnp.concatenate([m_all[:, :q0], m_new], axis=1)
                    acc_new = jnp.concatenate([acc_all[:, :q0], acc_new], axis=1)
                out += [m_new, acc_new]
            return tuple(out)

        half = t // 2
        init = (jnp.full((1, t), NEG, F32), jnp.zeros((128, t), F32)) * 2
        carry = lax.fori_loop(0, i, lambda j, c: block(c, j * t, t, 0, t, False), init)
        carry = block(carry, i * t, half, 0, t, True)
        m0, a0, m1, a1 = block(carry, i * t + half, half, half, half, True)
        l0 = a0[HEAD:HEAD + 1, :]
        l1 = a1[0:1, :]
        o_t = jnp.where(_head_rows(0), a0 / l0, a1 / l1)
        o_ref[...] = o_t.T.astype(BF16)
        ot_ref[...] = o_t.astype(BF16)
        r8 = lax.broadcasted_iota(jnp.int32, (8, t), 0)
        lse_ref[0] = jnp.where(r8 == 0, m0 + jnp.log(l0), jnp.where(r8 == 1, m1 + jnp.log(l1), 0.0))

    return pl.pallas_call(
        body, name="fox_fwd", grid=(4, nq),
        in_specs=[pl.BlockSpec((t, 128), lambda hp, i: (i, hp)),
                  pl.BlockSpec((s, 128), lambda hp, i: (0, hp)),
                  pl.BlockSpec((128, s), lambda hp, i: (hp, 0)),
                  pl.BlockSpec((1, s, 128), lambda hp, i: (hp, 0, 0))],
        out_specs=[pl.BlockSpec((t, 128), lambda hp, i: (i, hp)),
                   pl.BlockSpec((1, 8, t), lambda hp, i: (hp, 0, i)),
                   pl.BlockSpec((128, t), lambda hp, i: (hp, i))],
        out_shape=[jax.ShapeDtypeStruct((s, 512), BF16), jax.ShapeDtypeStruct((4, 8, s), F32),
                   jax.ShapeDtypeStruct((512, s), BF16)],
        compiler_params=_params(("parallel", "parallel")),
    )(q, k, v_t, ca4)


MEM_SCALE = MEM_HEAD ** -0.5


def _mem_fwd(qm, mk, mv):
    s = qm.shape[0]
    tq = min(512, s)

    def body(q_ref, mk_ref, mv_ref, o_ref):
        for h in range(MEM_HEADS):
            hs = slice(h * 128, (h + 1) * 128)
            sc = _dot(q_ref[:, hs], mk_ref[:, hs], NT) * MEM_SCALE
            mx = jnp.max(sc, axis=-1, keepdims=True)
            p = jnp.exp(sc - mx)
            p = p / jnp.sum(p, axis=-1, keepdims=True)
            o_ref[:, hs] = _dot(p.astype(BF16), mv_ref[:, hs]).astype(BF16)

    return pl.pallas_call(
        body, name="mem_fwd", grid=(s // tq,),
        in_specs=[pl.BlockSpec((tq, 512), lambda i: (i, 0)), _full(mk.shape), _full(mv.shape)],
        out_specs=pl.BlockSpec((tq, 512), lambda i: (i, 0)),
        out_shape=jax.ShapeDtypeStruct((s, 512), BF16),
        compiler_params=_params(("parallel",)),
    )(qm, mk, mv)


def _merge_fwd(x, oa, of, om, proj, b_gate, wa, wf, wm, w_out, g_mlp):
    s = x.shape[0]
    tm = min(512, s)

    def body(x_ref, oa_ref, of_ref, om_ref, gl_ref, bg_ref, wa_ref, wf_ref, wm_ref, wo_ref, g_ref, x1_ref, hm_ref, mg_ref):
        merged = None
        for b, (o_ref, w_ref) in enumerate(((oa_ref, wa_ref), (of_ref, wf_ref), (om_ref, wm_ref))):
            cs = slice(b * D_MODEL, (b + 1) * D_MODEL)
            y = _dot(o_ref[...], w_ref[...])
            t = _sigmoid(gl_ref[:, cs].astype(F32) + bg_ref[:, cs]) * y
            merged = t if merged is None else merged + t
        mb = merged.astype(BF16)
        mg_ref[...] = mb
        x1 = x_ref[...] + _dot(mb, wo_ref[...])
        x1_ref[...] = x1
        r = lax.rsqrt(jnp.mean(x1 * x1, axis=-1, keepdims=True) + EPS)
        hm_ref[...] = (x1 * r * g_ref[...]).astype(BF16)

    row = lambda w: pl.BlockSpec((tm, w), lambda i: (i, 0))
    return pl.pallas_call(
        body, name="merge_fwd", grid=(s // tm,),
        in_specs=[row(D_MODEL), row(512), row(512), row(512), row(HALF_W), _full((1, HALF_W)),
                  _full(wa.shape), _full(wf.shape), _full(wm.shape), _full(w_out.shape), _full((1, D_MODEL))],
        out_specs=[row(D_MODEL), row(D_MODEL), row(D_MODEL)],
        out_shape=[jax.ShapeDtypeStruct((s, D_MODEL), F32), jax.ShapeDtypeStruct((s, D_MODEL), BF16),
                   jax.ShapeDtypeStruct((s, D_MODEL), BF16)],
        compiler_params=_params(("parallel",)),
    )(x, oa, of, om, proj, b_gate, wa, wf, wm, w_out, g_mlp)


def _mlp_up(hm, w_up):
    s = hm.shape[0]
    tm, tn = min(1024, s), w_up.shape[2]

    def body(h_ref, w_ref, u_ref):
        r = jnp.maximum(_dot(h_ref[...], w_ref[0]), 0.0)
        u_ref[...] = (r * r).astype(BF16)

    return pl.pallas_call(
        body, name="mlp_up", grid=(s // tm, D_FF // tn),
        in_specs=[pl.BlockSpec((tm, D_MODEL), lambda i, j: (i, 0)), pl.BlockSpec((1, D_MODEL, tn), lambda i, j: (j, 0, 0))],
        out_specs=pl.BlockSpec((tm, tn), lambda i, j: (i, j)),
        out_shape=jax.ShapeDtypeStruct((s, D_FF), BF16),
        compiler_params=_params(("parallel", "parallel")),
    )(hm, w_up)


def _mlp_down_loss(u, w_down, x1, target):
    s = u.shape[0]
    tm = min(512, s)

    def body(u_ref, w_ref, x1_ref, t_ref, dy_ref, dyb_ref, loss_ref):
        i = pl.program_id(0)

        @pl.when(i == 0)
        def _():
            loss_ref[...] = jnp.zeros_like(loss_ref)

        y = x1_ref[...] + _dot(u_ref[...], w_ref[...])
        err = y - t_ref[...]
        dy = err * (1.0 / D_MODEL)
        dy_ref[...] = dy
        dyb_ref[...] = dy.astype(BF16)
        part = jnp.sum(jnp.sum(err * err, axis=-1, keepdims=True) * (1.0 / D_MODEL), axis=0, keepdims=True)
        loss_ref[...] += 0.5 * part

    row = pl.BlockSpec((tm, D_MODEL), lambda i: (i, 0))
    return pl.pallas_call(
        body, name="mlp_down_loss", grid=(s // tm,),
        in_specs=[pl.BlockSpec((tm, D_FF), lambda i: (i, 0)), _full(w_down.shape), row, row],
        out_specs=[row, row, _full((1, 1))],
        out_shape=[jax.ShapeDtypeStruct((s, D_MODEL), F32), jax.ShapeDtypeStruct((s, D_MODEL), BF16),
                   jax.ShapeDtypeStruct((1, 1), F32)],
        compiler_params=_params(("arbitrary",)),
    )(u, w_down, x1, target)


def _mlp_bwd_act(dy, w_down, u):
    s = dy.shape[0]
    tm, tn = min(1024, s), 1024

    def body(dy_ref, w_ref, u_ref, da_ref):
        du = _dot(dy_ref[...], w_ref[...], NT)
        da_ref[...] = (du * (2.0 * jnp.sqrt(u_ref[...].astype(F32)))).astype(BF16)

    return pl.pallas_call(
        body, name="mlp_bwd_act", grid=(D_FF // tn, s // tm),
        in_specs=[pl.BlockSpec((tm, D_MODEL), lambda j, i: (i, 0)), pl.BlockSpec((tn, D_MODEL), lambda j, i: (j, 0)),
                  pl.BlockSpec((tm, tn), lambda j, i: (i, j))],
        out_specs=pl.BlockSpec((tm, tn), lambda j, i: (i, j)),
        out_shape=jax.ShapeDtypeStruct((s, D_FF), BF16),
        compiler_params=_params(("parallel", "parallel")),
    )(dy, w_down, u)


def _rms_bwd(xv, g, dh, skip):
    r = lax.rsqrt(jnp.mean(xv * xv, axis=-1, keepdims=True) + EPS)
    n = xv * r
    dn = dh * g
    dx = skip + r * (dn - n * jnp.mean(dn * n, axis=-1, keepdims=True))
    return dx, jnp.sum(dh * n, axis=0, keepdims=True)


def _mlp_bwd_x(da, w_up, x1, dy, g_mlp):
    s = da.shape[0]
    tm = min(512, s)

    def body(da_ref, w_ref, x1_ref, dy_ref, g_ref, dx1_ref, dg_ref):
        i = pl.program_id(0)

        @pl.when(i == 0)
        def _():
            dg_ref[...] = jnp.zeros_like(dg_ref)

        tn = w_ref.shape[2]
        dhm = _dot(da_ref[:, 0:tn], w_ref[0], NT)
        for j in range(1, N_DEV):
            dhm = dhm + _dot(da_ref[:, j * tn:(j + 1) * tn], w_ref[j], NT)
        dx, dg = _rms_bwd(x1_ref[...], g_ref[...], dhm, dy_ref[...])
        dx1_ref[...] = dx
        dg_ref[...] += dg

    row = pl.BlockSpec((tm, D_MODEL), lambda i: (i, 0))
    return pl.pallas_call(
        body, name="mlp_bwd_x", grid=(s // tm,),
        in_specs=[pl.BlockSpec((tm, D_FF), lambda i: (i, 0)), _full(w_up.shape), row, row, _full((1, D_MODEL))],
        out_specs=[row, _full((1, D_MODEL))],
        out_shape=[jax.ShapeDtypeStruct((s, D_MODEL), F32), jax.ShapeDtypeStruct((1, D_MODEL), F32)],
        compiler_params=_params(("arbitrary",)),
    )(da, w_up, x1, dy, g_mlp)


def _merge_bwd(dx1, oa, of, om, proj, b_gate, wa, wf, wm, w_out):
    s = dx1.shape[0]
    tm = min(512, s)

    def body(dx1_ref, oa_ref, of_ref, om_ref, gl_ref, bg_ref, wa_ref, wf_ref, wm_ref, wo_ref,
             dp_ref, doa_ref, dof_ref, dom_ref, dya_ref, dyf_ref, dym_ref, dbg_ref):
        i = pl.program_id(0)

        @pl.when(i == 0)
        def _():
            dbg_ref[...] = jnp.zeros_like(dbg_ref)

        dmerged = _dot(dx1_ref[...].astype(BF16), wo_ref[...], NT)
        branches = ((oa_ref, wa_ref, doa_ref, dya_ref), (of_ref, wf_ref, dof_ref, dyf_ref), (om_ref, wm_ref, dom_ref, dym_ref))
        for b, (o_ref, w_ref, do_ref, dyb_ref) in enumerate(branches):
            cs = slice(b * D_MODEL, (b + 1) * D_MODEL)
            y = _dot(o_ref[...], w_ref[...])
            g = _sigmoid(gl_ref[:, cs].astype(F32) + bg_ref[:, cs])
            dz = (dmerged * y) * g * (1.0 - g)
            dp_ref[:, cs] = dz.astype(BF16)
            dbg_ref[:, cs] += jnp.sum(dz, axis=0, keepdims=True)
            dyb = (dmerged * g).astype(BF16)
            dyb_ref[...] = dyb
            do = _dot(dyb, w_ref[...], NT)
            do_ref[...] = (do.T if b == 1 else do).astype(BF16)

    row = lambda w: pl.BlockSpec((tm, w), lambda i: (i, 0))
    sd = lambda w: jax.ShapeDtypeStruct((s, w), BF16)
    return pl.pallas_call(
        body, name="merge_bwd", grid=(s // tm,),
        in_specs=[row(D_MODEL), row(512), row(512), row(512), row(HALF_W), _full((1, HALF_W)),
                  _full(wa.shape), _full(wf.shape), _full(wm.shape), _full(w_out.shape)],
        out_specs=[row(HALF_W), row(512), pl.BlockSpec((512, tm), lambda i: (0, i)), row(512),
                   row(D_MODEL), row(D_MODEL), row(D_MODEL), _full((1, HALF_W))],
        out_shape=[sd(PROJ_W), sd(512), jax.ShapeDtypeStruct((512, s), BF16), sd(512), sd(D_MODEL), sd(D_MODEL), sd(D_MODEL),
                   jax.ShapeDtypeStruct((1, HALF_W), F32)],
        compiler_params=_params(("arbitrary",)),
    )(dx1, oa, of, om, proj, b_gate, wa, wf, wm, w_out)


def _swa_valid_t(n):
    key = lax.broadcasted_iota(jnp.int32, (2 * SWA_BLOCK, SWA_BLOCK), 0)
    qry = lax.broadcasted_iota(jnp.int32, (2 * SWA_BLOCK, SWA_BLOCK), 1)
    dist = qry + SWA_BLOCK - key
    return (dist >= 0) & (dist < SWA_BLOCK) & ((key >= SWA_BLOCK) | (n > 0))


def _swa_bwd(qa, kp, vp, bias_t, sink, doa):
    s = qa.shape[0]
    nb = s // SWA_BLOCK

    def body(sink_ref, q_ref, kp_ref, vp_ref, bias_ref, do_ref, dq_ref, dkp_ref, dvp_ref, dbias_ref, dsink_ref, sk_acc):
        n = pl.program_id(0)

        @pl.when(n == 0)
        def _():
            dkp_ref[...] = jnp.zeros_like(dkp_ref)
            dvp_ref[...] = jnp.zeros_like(dvp_ref)
            dbias_ref[...] = jnp.zeros_like(dbias_ref)
            sk_acc[...] = jnp.zeros_like(sk_acc)

        start = pl.multiple_of(n * SWA_BLOCK, SWA_BLOCK)
        win = pl.ds(start, 2 * SWA_BLOCK)
        k2 = kp_ref[win, :]
        v2 = vp_ref[win, :]
        valid = _swa_valid_t(n)
        heads = range(SWA_HEADS)
        hs = lambda h: slice(h * HEAD, (h + 1) * HEAD)
        scale = jnp.asarray(0.125, BF16)
        q = [q_ref[:, hs(h)] for h in heads]
        do = [do_ref[:, hs(h)] for h in heads]
        kk = [k2[:, hs(kv)] for kv in range(2)]
        vv = [v2[:, hs(kv)] for kv in range(2)]
        kt = [(kk[kv].astype(F32) * 0.125).T.astype(BF16) for kv in range(2)]
        st = [jnp.where(valid, _dot(kk[h // 4], q[h], NT) * 0.125 + bias_ref[h], NEG) for h in heads]
        dpt = [_dot(vv[h // 4], do[h], NT) for h in heads]
        pt, dst = [], []
        for h in heads:
            sk = sink_ref[h]
            mx = jnp.maximum(jnp.max(st[h], axis=0, keepdims=True), sk)
            p = jnp.exp(st[h] - mx)
            esk = jnp.exp(sk - mx)
            den = jnp.sum(p, axis=0, keepdims=True) + esk
            p = p / den
            delta = jnp.sum(p * dpt[h], axis=0, keepdims=True)
            d = p * (dpt[h] - delta)
            sk_acc[h:h + 1, :] += -(esk / den) * delta
            dbias_ref[h] += d
            pt.append(p.astype(BF16))
            dst.append(d.astype(BF16))
        dq_t = [_dot(kt[h // 4], dst[h]) for h in heads]
        dq_ref[...] = jnp.concatenate(dq_t, axis=0).T.astype(BF16)
        for kv in range(2):
            group = range(4 * kv, 4 * kv + 4)
            dk = [_dot(dst[h], q[h] * scale) for h in group]
            dv = [_dot(pt[h], do[h]) for h in group]
            dkp_ref[win, hs(kv)] += (dk[0] + dk[1]) + (dk[2] + dk[3])
            dvp_ref[win, hs(kv)] += (dv[0] + dv[1]) + (dv[2] + dv[3])

        @pl.when(n == nb - 1)
        def _():
            dsink_ref[...] = jnp.broadcast_to(jnp.sum(sk_acc[...], axis=1, keepdims=True), dsink_ref.shape)

    return pl.pallas_call(
        body, name="swa_bwd", grid=(nb,),
        in_specs=[pl.BlockSpec(memory_space=pltpu.SMEM),
                  pl.BlockSpec((SWA_BLOCK, 512), lambda n: (n, 0)),
                  _full(kp.shape), _full(vp.shape), _full(bias_t.shape),
                  pl.BlockSpec((SWA_BLOCK, 512), lambda n: (n, 0))],
        out_specs=[pl.BlockSpec((SWA_BLOCK, 512), lambda n: (n, 0)), _full(kp.shape), _full(vp.shape),
                   _full(bias_t.shape), _full((SWA_HEADS, 128))],
        out_shape=[jax.ShapeDtypeStruct((s, 512), BF16), jax.ShapeDtypeStruct(kp.shape, F32),
                   jax.ShapeDtypeStruct(vp.shape, F32), jax.ShapeDtypeStruct(bias_t.shape, F32),
                   jax.ShapeDtypeStruct((SWA_HEADS, 128), F32)],
        scratch_shapes=[pltpu.VMEM((SWA_HEADS, 128), F32)],
        compiler_params=_params(("arbitrary",)),
    )(sink, qa, kp, vp, bias_t, doa)


def _fox_bwd(qt, k, v, dot, ot, cc4, lse4):
    s = k.shape[0]
    t = min(FOX_BWD_T, s)
    nq = s // t

    def body(qt_ref, k_ref, v_ref, dot_ref, ot_ref, cc_ref, lse_ref,
             dqt_ref, dk_ref, dv_ref, dck_ref, dcq_ref, delta_ref, dkt_acc, dvt_acc, ds0, ds1):
        j = pl.program_id(1)

        @pl.when(j == 0)
        def _():
            dqt_ref[...] = jnp.zeros_like(dqt_ref)
            dcq_ref[...] = jnp.zeros_like(dcq_ref)
            r8 = lax.broadcasted_iota(jnp.int32, (8, t), 0)

            def dl(i, c):
                cols = pl.ds(pl.multiple_of(i * t, t), t)
                pr = dot_ref[:, cols].astype(F32) * ot_ref[:, cols].astype(F32)
                d0 = jnp.sum(jnp.where(_head_rows(0), pr, 0.0), axis=0, keepdims=True)
                d1 = jnp.sum(jnp.where(_head_rows(1), pr, 0.0), axis=0, keepdims=True)
                delta_ref[:, cols] = jnp.where(r8 == 0, d0, jnp.where(r8 == 1, d1, 0.0))
                return c

            lax.fori_loop(0, nq, dl, 0)

        kj = k_ref[...]
        vj = v_ref[...]
        ks = pl.ds(pl.multiple_of(j * t, t), t)
        kt = (kj.astype(F32) * 0.125).T.astype(BF16)
        ke = [jnp.where(_head_mask(e), kj, jnp.zeros_like(kj)) for e in range(2)]
        ve = [jnp.where(_head_mask(e), vj, jnp.zeros_like(vj)) for e in range(2)]
        ck = [cc_ref[0, ks, e:e + 1] for e in range(2)]
        for r in (dkt_acc, dvt_acc, ds0, ds1):
            r[...] = jnp.zeros_like(r)

        def block(q0, nqs, k0, nks, masked):
            cols = pl.ds(pl.multiple_of(q0, 128), nqs)
            rows = slice(k0, k0 + nks)
            qti = qt_ref[:, cols]
            doti = dot_ref[:, cols]
            for e, ds_acc in enumerate((ds0, ds1)):
                dims = slice(e * HEAD, (e + 1) * HEAD)
                st = _dot(ke[e][rows, :], qti) - ck[e][rows, :]
                if masked:
                    krow = lax.broadcasted_iota(jnp.int32, (nks, nqs), 0) + (j * t + k0)
                    qcol = lax.broadcasted_iota(jnp.int32, (nks, nqs), 1) + q0
                    st = jnp.where(krow <= qcol, st, NEG)
                pt = jnp.exp(st - lse_ref[0, e:e + 1, cols])
                dpt = _dot(ve[e][rows, :], doti)
                dst = pt * (dpt - delta_ref[e:e + 1, cols])
                dsb = dst.astype(BF16)
                dvt_acc[dims, rows] += _dot(doti[dims, :], pt.astype(BF16), NT)
                dkt_acc[dims, rows] += _dot(qti[dims, :], dsb, NT)
                dqt_ref[dims, cols] += _dot(kt[dims, rows], dsb)
                ds_acc[rows, 0:nqs] += dst
                dcq_ref[0, e:e + 1, cols] += jnp.sum(dst, axis=0, keepdims=True)

        half = t // 2
        block(j * t, half, 0, half, True)
        block(j * t + half, half, 0, t, True)

        def rest(i, c):
            block(i * t, t, 0, t, False)
            return c

        lax.fori_loop(j + 1, nq, rest, 0)
        dk_ref[...] = dkt_acc[...].T.astype(BF16)
        dv_ref[...] = dvt_acc[...].T.astype(BF16)
        lane = lax.broadcasted_iota(jnp.int32, (t, 128), 1)
        c0 = jnp.sum(ds0[...], axis=-1, keepdims=True)
        c1 = jnp.sum(ds1[...], axis=-1, keepdims=True)
        dck_ref[0] = jnp.where(lane == 0, c0, jnp.where(lane == 1, c1, 0.0))

    res_t = lambda: pl.BlockSpec((128, s), lambda hp, j: (hp, 0))
    blk = lambda: pl.BlockSpec((t, 128), lambda hp, j: (j, hp))
    return pl.pallas_call(
        body, name="fox_bwd", grid=(4, nq),
        in_specs=[res_t(), blk(), blk(), res_t(), res_t(), pl.BlockSpec((1, s, 128), lambda hp, j: (hp, 0, 0)),
                  pl.BlockSpec((1, 8, s), lambda hp, j: (hp, 0, 0))],
        out_specs=[res_t(), blk(), blk(),
                   pl.BlockSpec((1, t, 128), lambda hp, j: (hp, j, 0)),
                   pl.BlockSpec((1, 8, s), lambda hp, j: (hp, 0, 0))],
        out_shape=[jax.ShapeDtypeStruct((512, s), F32), jax.ShapeDtypeStruct((s, 512), BF16),
                   jax.ShapeDtypeStruct((s, 512), BF16), jax.ShapeDtypeStruct((4, s, 128), F32),
                   jax.ShapeDtypeStruct((4, 8, s), F32)],
        scratch_shapes=[pltpu.VMEM((8, s), F32)] + [pltpu.VMEM((128, t), F32)] * 2 + [pltpu.VMEM((t, t), F32)] * 2,
        compiler_params=_params(("arbitrary", "arbitrary")),
    )(qt, k, v, dot, ot, cc4, lse4)


def _mem_bwd(qm, mk, mv, dom):
    s = qm.shape[0]
    tq = min(512, s)

    def body(q_ref, mk_ref, mv_ref, do_ref, dq_ref, dmk_ref, dmv_ref):
        i = pl.program_id(0)

        @pl.when(i == 0)
        def _():
            dmk_ref[...] = jnp.zeros_like(dmk_ref)
            dmv_ref[...] = jnp.zeros_like(dmv_ref)

        heads = range(MEM_HEADS)
        hs = lambda h: slice(h * 128, (h + 1) * 128)
        sc = [_dot(q_ref[:, hs(h)], mk_ref[:, hs(h)], NT) * MEM_SCALE for h in heads]
        dp = [_dot(do_ref[:, hs(h)], mv_ref[:, hs(h)], NT) for h in heads]
        pb, dsb = [], []
        for h in heads:
            p = jnp.exp(sc[h] - jnp.max(sc[h], axis=-1, keepdims=True))
            p = p / jnp.sum(p, axis=-1, keepdims=True)
            ds = p * (dp[h] - jnp.sum(p * dp[h], axis=-1, keepdims=True))
            pb.append(p.astype(BF16))
            dsb.append((ds * MEM_SCALE).astype(BF16))
        dq = [_dot(dsb[h], mk_ref[:, hs(h)]).astype(BF16) for h in heads]
        dmk = [_dot(dsb[h], q_ref[:, hs(h)], TN) for h in heads]
        dmv = [_dot(pb[h], do_ref[:, hs(h)], TN) for h in heads]
        for h in heads:
            dq_ref[:, hs(h)] = dq[h]
            dmk_ref[:, hs(h)] += dmk[h]
            dmv_ref[:, hs(h)] += dmv[h]

    return pl.pallas_call(
        body, name="mem_bwd", grid=(s // tq,),
        in_specs=[pl.BlockSpec((tq, 512), lambda i: (i, 0)), _full(mk.shape), _full(mv.shape),
                  pl.BlockSpec((tq, 512), lambda i: (i, 0))],
        out_specs=[pl.BlockSpec((tq, 512), lambda i: (i, 0)), _full(mk.shape), _full(mv.shape)],
        out_shape=[jax.ShapeDtypeStruct((s, 512), BF16), jax.ShapeDtypeStruct(mk.shape, F32),
                   jax.ShapeDtypeStruct(mv.shape, F32)],
        compiler_params=_params(("arbitrary",)),
    )(qm, mk, mv, dom)


def _memkv_bwd(dmk, dmv, kv_raw, kn_mem, mem, g_mem, mem_n, w_kv):
    def body(dmk_ref, dmv_ref, kv_ref, kn_ref, mem_ref, g_ref, mn_ref, w_ref, dw_ref, dkn_ref, dg_ref, dkv_ref):
        dkn = jnp.zeros((1, 128), F32)
        for h in range(MEM_HEADS):
            hs = slice(h * 128, (h + 1) * 128)
            v = kv_ref[:, hs]
            r = lax.rsqrt(jnp.mean(v * v, axis=-1, keepdims=True) + EPS)
            n = v * r
            dn = dmk_ref[:, hs]
            dkn = dkn + jnp.sum(dn * n, axis=0, keepdims=True)
            dng = dn * kn_ref[...]
            dkv_ref[:, hs] = (r * (dng - n * jnp.mean(dng * n, axis=-1, keepdims=True))).astype(BF16)
        dkv_ref[:, 512:1024] = dmv_ref[...].astype(BF16)
        dkn_ref[...] = dkn
        dkv = dkv_ref[...]
        dw_ref[...] = _dot(mn_ref[...], dkv, TN).astype(BF16)
        dmn = _dot(dkv, w_ref[...], NT)
        xv = mem_ref[...]
        r = lax.rsqrt(jnp.mean(xv * xv, axis=-1, keepdims=True) + EPS)
        dg_ref[...] = jnp.sum(dmn * (xv * r), axis=0, keepdims=True)

    m = mem.shape[0]
    return pl.pallas_call(
        body, name="memkv_bwd",
        out_shape=[jax.ShapeDtypeStruct((D_MODEL, 1024), BF16), jax.ShapeDtypeStruct((1, 128), F32),
                   jax.ShapeDtypeStruct((1, D_MODEL), F32)],
        scratch_shapes=[pltpu.VMEM((m, 1024), BF16)],
        compiler_params=pltpu.CompilerParams(vmem_limit_bytes=VMEM_LIMIT),
    )(dmk, dmv, kv_raw, kn_mem, mem, g_mem, mem_n, w_kv)


def _fox_gate_bwd(dcq4, dck4, proj, b_forget128):
    s = dck4.shape[1]
    tm = min(512, s)
    nt = s // tm

    def body(dcq_ref, dck_ref, p_ref, b_ref, dfl_ref, db_ref, carry_ref):
        i = pl.program_id(0)

        @pl.when(i == 0)
        def _():
            carry_ref[...] = jnp.zeros_like(carry_ref)
            db_ref[...] = jnp.zeros_like(db_ref)

        dcv = jnp.zeros((tm, 128), F32)
        for hp in range(4):
            by_query = jnp.concatenate([dcq_ref[hp], jnp.zeros((120, tm), F32)], axis=0).T
            d = by_query - dck_ref[hp]
            dcv = dcv + (d if hp == 0 else pltpu.roll(d, 2 * hp, 1))
        dlogf = jnp.dot(_tri(tm, False), dcv, precision=lax.Precision.HIGHEST, preferred_element_type=F32) + carry_ref[...]
        carry_ref[...] += jnp.sum(dcv, axis=0, keepdims=True)
        z = p_ref[...] + b_ref[...]
        dfl = dlogf * (1.0 / (1.0 + jnp.exp(z)))
        dfl_ref[...] = dfl.astype(BF16)
        db_ref[...] += jnp.sum(dfl, axis=0, keepdims=True)

    return pl.pallas_call(
        body, name="fox_gate_bwd", grid=(nt,),
        in_specs=[pl.BlockSpec((4, 8, tm), lambda i: (0, 0, nt - 1 - i)),
                  pl.BlockSpec((4, tm, 128), lambda i: (0, nt - 1 - i, 0)),
                  pl.BlockSpec((tm, 128), lambda i: (nt - 1 - i, 0)), _full((1, 128))],
        out_specs=[pl.BlockSpec((tm, 128), lambda i: (nt - 1 - i, 0)), _full((1, 128))],
        out_shape=[jax.ShapeDtypeStruct((s, 128), BF16), jax.ShapeDtypeStruct((1, 128), F32)],
        scratch_shapes=[pltpu.VMEM((1, 128), F32)],
        compiler_params=_params(("arbitrary",)),
    )(dcq4, dck4, proj, b_forget128)


def _proj_pre_bwd(dproj, proj, dqf, dkf, dvf, dqm, dqa, dka, dva, dfl, gq_fox, gk_fox, gq_mem, gq_swa, gk_swa):
    s = proj.shape[0]
    tm = min(256, s)

    def body(dp_in, p_ref, dqf_ref, dkf_ref, dvf_ref, dqm_ref, dqa_ref, dka_ref, dva_ref, dfl_ref,
             gqf, gkf, gqm, gqa, gka, dp_ref, dgn_ref):
        i = pl.program_id(0)

        @pl.when(i == 0)
        def _():
            dgn_ref[...] = jnp.zeros_like(dgn_ref)

        def norm_bwd(off, width, hd, g_ref, dn_ref, slot):
            acc = jnp.zeros((1, 128), F32)
            for b in range(width // 128):
                v = p_ref[:, off + b * 128: off + (b + 1) * 128].astype(F32)
                r = lax.rsqrt(_group_mean(v * v, hd) + EPS)
                n = v * r
                dn = dn_ref[b * 128:(b + 1) * 128, :].T if slot == 0 else dn_ref[:, b * 128:(b + 1) * 128].astype(F32)
                acc = acc + jnp.sum(dn * n, axis=0, keepdims=True)
                dng = dn * g_ref[...]
                dp_ref[:, off + b * 128: off + (b + 1) * 128] = (r * (dng - n * _group_mean(dng * n, hd))).astype(BF16)
            dgn_ref[slot:slot + 1, :] += acc

        norm_bwd(H_QF, 512, HEAD, gqf, dqf_ref, 0)
        norm_bwd(H_KF, 512, HEAD, gkf, dkf_ref, 1)
        dp_ref[:, H_VF:H_VF + 512] = dvf_ref[...].astype(BF16)
        norm_bwd(H_QM, 512, MEM_HEAD, gqm, dqm_ref, 2)
        norm_bwd(H_QA, 512, HEAD, gqa, dqa_ref, 3)
        norm_bwd(H_KA, 128, HEAD, gka, dka_ref, 4)
        dp_ref[:, H_VA:H_VA + 128] = dva_ref[...].astype(BF16)
        dp_ref[:, H_FL:H_FL + 128] = dfl_ref[...]
        dp_ref[:, H_FL + 128:HALF_W] = jnp.zeros((tm, HALF_W - H_FL - 128), BF16)

    row = lambda w: pl.BlockSpec((tm, w), lambda i: (i, 0))
    g_spec = _full((1, 128))
    return pl.pallas_call(
        body, name="proj_pre_bwd", grid=(s // tm,),
        in_specs=[pl.BlockSpec(memory_space=pl.ANY), pl.BlockSpec((tm, HALF_W), lambda i: (i, 1)),
                  pl.BlockSpec((512, tm), lambda i: (0, i)), row(512), row(512), row(512), row(512),
                  row(128), row(128), row(128), g_spec, g_spec, g_spec, g_spec, g_spec],
        out_specs=[pl.BlockSpec((tm, HALF_W), lambda i: (i, 1)), _full((8, 128))],
        out_shape=[jax.ShapeDtypeStruct((s, PROJ_W), BF16), jax.ShapeDtypeStruct((8, 128), F32)],
        input_output_aliases={0: 0},
        compiler_params=_params(("arbitrary",)),
    )(dproj, proj, dqf, dkf, dvf, dqm, dqa, dka, dva, dfl, gq_fox, gk_fox, gq_mem, gq_swa, gk_swa)


def _in_bwd_x(dproj, w_in_p, x, g_mix, dx1):
    s = x.shape[0]
    tm = min(512, s)

    def body(dp_ref, w_ref, x_ref, g_ref, dx1_ref, gx_ref, dg_ref):
        i = pl.program_id(0)

        @pl.when(i == 0)
        def _():
            dg_ref[...] = jnp.zeros_like(dg_ref)

        dx, dg = _rms_bwd(x_ref[...], g_ref[...], _dot(dp_ref[...], w_ref[...], NT), dx1_ref[...])
        gx_ref[...] = dx
        dg_ref[...] += dg

    row = pl.BlockSpec((tm, D_MODEL), lambda i: (i, 0))
    return pl.pallas_call(
        body, name="in_bwd_x", grid=(s // tm,),
        in_specs=[pl.BlockSpec((tm, PROJ_W), lambda i: (i, 0)), _full(w_in_p.shape), row, _full((1, D_MODEL)), row],
        out_specs=[row, _full((1, D_MODEL))],
        out_shape=[jax.ShapeDtypeStruct((s, D_MODEL), F32), jax.ShapeDtypeStruct((1, D_MODEL), F32)],
        compiler_params=_params(("arbitrary",)),
    )(dproj, w_in_p, x, g_mix, dx1)


def _rel_bias_bwd(dbias, bucket):
    def body(db_ref, bk_ref, o_ref):
        bk = bk_ref[...]
        lane = lax.broadcasted_iota(jnp.int32, (1, 128), 1)
        for b in range(REL_BUCKETS):
            sel = bk == b
            acc = jnp.zeros((1, 128), F32)
            for h in range(SWA_HEADS):
                tot = jnp.sum(jnp.sum(jnp.where(sel, db_ref[h], 0.0), axis=-1, keepdims=True), axis=0, keepdims=True)
                acc = jnp.where(lane == h, tot, acc)
            o_ref[:, b * 128:(b + 1) * 128] = acc

    return pl.pallas_call(
        body, name="rel_bias_bwd",
        out_shape=jax.ShapeDtypeStruct((1, REL_BUCKETS * 128), F32),
        compiler_params=pltpu.CompilerParams(vmem_limit_bytes=VMEM_LIMIT),
    )(dbias, bucket)


def _my_place():
    return lax.axis_index("x"), lax.axis_index("y"), lax.axis_index("c")


def _peer(place, k):
    x, y, c = place
    return (1 - x if k & 4 else x, 1 - y if k & 2 else y, 1 - c if k & 1 else c)


def _index(place):
    x, y, c = place
    return 4 * x + 2 * y + c


HBM_SPEC = pl.BlockSpec(memory_space=pltpu.HBM)
SEM_SPEC = pl.BlockSpec(memory_space=pltpu.SEMAPHORE)
DATAFLOW = pltpu.SideEffectType.DATAFLOW_SIDE_EFFECTING


ALL_PEERS = tuple(range(1, N_DEV))
SAME_CORE = (2, 4, 6)
OWN = N_DEV - 1


def _split_copy(src_ref, land_ref, send_sems, recv_sems, me, k, gather):
    peer = _peer(me, k)
    if gather:
        src, dst = src_ref, land_ref.at[_index(me)]
    else:
        src, dst = src_ref.at[_index(peer)], land_ref.at[k - 1]
    return pltpu.make_async_remote_copy(src_ref=src, dst_ref=dst, send_sem=send_sems.at[k - 1], recv_sem=recv_sems.at[k - 1],
                                        device_id=peer, device_id_type=MESH)


def _own_copy(src_ref, land_ref, recv_sems, me, gather):
    if gather:
        src, dst = src_ref, land_ref.at[_index(me)]
    else:
        src, dst = src_ref.at[_index(me)], land_ref.at[OWN]
    return pltpu.make_async_copy(src, dst, recv_sems.at[OWN])


def _split_start(srcs, gather, name, peers=ALL_PEERS, after=None):
    n = len(srcs)
    extra = [] if after is None else [after]

    def body(*refs):
        refs = refs[:2 * n] + refs[2 * n + len(extra):]
        src_refs, land_refs = refs[:n], refs[n:2 * n]
        send_sems, recv_sems, token = refs[2 * n:3 * n], refs[3 * n:4 * n], refs[-1]
        me = _my_place()
        for w in range(n):
            for k in peers:
                _split_copy(src_refs[w], land_refs[w], send_sems[w], recv_sems[w], me, k, gather).start()
            _own_copy(src_refs[w], land_refs[w], recv_sems[w], me, gather).start()
        token[...] = jnp.zeros_like(token)

    lands = [lax.empty((N_DEV,) + (a.shape if gather else a.shape[1:]), a.dtype) for a in srcs]
    sems = [pltpu.SemaphoreType.DMA((N_DEV,))] * (2 * n)
    hbm = [pltpu.HBM(a.shape, a.dtype) for a in list(srcs) + lands]
    outs = pl.pallas_call(
        body, name=name,
        out_shape=(*sems, *hbm, jax.ShapeDtypeStruct((8, 128), F32)),
        in_specs=(HBM_SPEC,) * (2 * n) + (pl.BlockSpec(memory_space=pl.ANY),) * len(extra),
        out_specs=(SEM_SPEC,) * (2 * n) + (HBM_SPEC,) * (2 * n) + (pl.BlockSpec(memory_space=pltpu.VMEM),),
        input_output_aliases={i: 2 * n + i for i in range(2 * n)},
        compiler_params=pltpu.CompilerParams(has_side_effects=DATAFLOW),
    )(*[pltpu.with_memory_space_constraint(a, pltpu.HBM) for a in list(srcs) + lands], *extra)
    return list(outs[:n]), list(outs[n:2 * n]), list(outs[2 * n:3 * n]), list(outs[3 * n:4 * n]), outs[-1]


def _split_wait(started, w, after, gather, name):
    send_sems, recv_sems, srcs, lands, _ = started

    def body(src_ref, land_ref, send_sems, recv_sems, after_ref, src_out, land_out):
        me = _my_place()
        for k in ALL_PEERS:
            cp = _split_copy(src_ref, land_ref, send_sems, recv_sems, me, k, gather)
            cp.wait_send()
            cp.wait_recv()
        _own_copy(src_ref, land_ref, recv_sems, me, gather).wait()

    return pl.pallas_call(
        body, name=name,
        out_shape=(pltpu.HBM(srcs[w].shape, srcs[w].dtype), pltpu.HBM(lands[w].shape, lands[w].dtype)),
        in_specs=(HBM_SPEC, HBM_SPEC, SEM_SPEC, SEM_SPEC, pl.BlockSpec(memory_space=pl.ANY)),
        out_specs=(HBM_SPEC, HBM_SPEC), input_output_aliases={0: 0, 1: 1},
        compiler_params=pltpu.CompilerParams(has_side_effects=DATAFLOW),
    )(srcs[w], lands[w], send_sems[w], recv_sems[w], after)[1]


def _forward_copy(land_ref, send_sems, recv_sems, me, j, incoming):
    sibling = _peer(me, 1)
    rows = land_ref.at[_index(_peer(sibling if incoming else me, SAME_CORE[j]))]
    return pltpu.make_async_remote_copy(src_ref=rows, dst_ref=rows, send_sem=send_sems.at[j], recv_sem=recv_sems.at[j],
                                        device_id=sibling, device_id_type=MESH)


def _forward_start(started, after, name):
    send_a, recv_a, srcs, lands, _ = started

    def body(src_ref, land_ref, send_a, recv_a, after_ref, send_b, recv_b, src_out, land_out):
        me = _my_place()
        for j, k in enumerate(SAME_CORE):
            _split_copy(src_ref, land_ref, send_a, recv_a, me, k, True).wait_recv()
            _forward_copy(land_ref, send_b, recv_b, me, j, False).start()

    sems = pltpu.SemaphoreType.DMA((len(SAME_CORE),))
    return pl.pallas_call(
        body, name=name,
        out_shape=(sems, sems, pltpu.HBM(srcs[0].shape, srcs[0].dtype), pltpu.HBM(lands[0].shape, lands[0].dtype)),
        in_specs=(HBM_SPEC, HBM_SPEC, SEM_SPEC, SEM_SPEC, pl.BlockSpec(memory_space=pl.ANY)),
        out_specs=(SEM_SPEC, SEM_SPEC, HBM_SPEC, HBM_SPEC), input_output_aliases={0: 2, 1: 3},
        compiler_params=pltpu.CompilerParams(has_side_effects=DATAFLOW),
    )(srcs[0], lands[0], send_a[0], recv_a[0], after)


def _forward_wait(started, forwarded, name):
    send_a, recv_a, _, _, _ = started
    send_b, recv_b, src, land = forwarded

    def body(src_ref, land_ref, send_a, recv_a, send_b, recv_b, src_out, land_out):
        me = _my_place()
        _own_copy(src_ref, land_ref, recv_a, me, True).wait()
        for k in (1,) + SAME_CORE:
            _split_copy(src_ref, land_ref, send_a, recv_a, me, k, True).wait_send()
        _split_copy(src_ref, land_ref, send_a, recv_a, me, 1, True).wait_recv()
        for j in range(len(SAME_CORE)):
            _forward_copy(land_ref, send_b, recv_b, me, j, False).wait_send()
            _forward_copy(land_ref, send_b, recv_b, me, j, True).wait_recv()

    return pl.pallas_call(
        body, name=name,
        out_shape=(pltpu.HBM(src.shape, src.dtype), pltpu.HBM(land.shape, land.dtype)),
        in_specs=(HBM_SPEC, HBM_SPEC, SEM_SPEC, SEM_SPEC, SEM_SPEC, SEM_SPEC),
        out_specs=(HBM_SPEC, HBM_SPEC), input_output_aliases={0: 0, 1: 1},
        compiler_params=pltpu.CompilerParams(has_side_effects=DATAFLOW),
    )(src, land, send_a[0], recv_a[0], send_b, recv_b)[1]


def _adam_math(w, g, m, v):
    m2 = ADAM_B1 * m + (1.0 - ADAM_B1) * g
    v2 = ADAM_B2 * v + (1.0 - ADAM_B2) * (g * g)
    m_hat = m2 / (1.0 - ADAM_B1 ** ADAM_STEP)
    v_hat = v2 / (1.0 - ADAM_B2 ** ADAM_STEP)
    delta = -ADAM_LR * (m_hat / (jnp.sqrt(v_hat) + ADAM_EPS) + ADAM_WD * w)
    return delta, m2, v2


def _adamw(land, w, m, v, name):
    a, b = w.shape
    bp = land.shape[2]
    ta = min(128, a)

    def body(p_ref, w_ref, m_ref, v_ref, g_ref, d_ref, m2_ref, v2_ref):
        g = p_ref[0, :, 0:b].astype(F32)
        for k in range(1, N_DEV):
            g = g + p_ref[k, :, 0:b].astype(F32)
        delta, m2, v2 = _adam_math(w_ref[...], g, m_ref[...], v_ref[...])
        g_ref[...] = g
        d_ref[...] = delta
        m2_ref[...] = m2
        v2_ref[...] = v2

    blk = pl.BlockSpec((ta, b), lambda i: (i, 0))
    sd = jax.ShapeDtypeStruct((a, b), F32)
    return pl.pallas_call(
        body, name=name, grid=(a // ta,),
        in_specs=[pl.BlockSpec((N_DEV, ta, bp), lambda i: (0, i, 0)), blk, blk, blk],
        out_specs=[blk, blk, blk, blk], out_shape=[sd, sd, sd, sd],
        compiler_params=_params(("parallel",)),
    )(land, w, m, v)


def _bucket_table():
    t_loc = jnp.arange(SWA_BLOCK)[:, None] + SWA_BLOCK
    s_loc = jnp.arange(2 * SWA_BLOCK)[None, :]
    dist = t_loc - s_loc
    max_exact = REL_BUCKETS // 2
    d = jnp.maximum(dist, 0)
    df = jnp.maximum(d, 1).astype(F32)
    large = max_exact + (jnp.log(df / max_exact) / math.log(REL_MAX_DIST / max_exact) * (REL_BUCKETS - max_exact)).astype(jnp.int32)
    large = jnp.minimum(large, REL_BUCKETS - 1)
    bucket = jnp.where(d < max_exact, d, large)
    band = (dist >= 0) & (dist < SWA_BLOCK)
    return bucket, band


def _tile2(g):
    return jnp.concatenate([g, g], axis=1) if g.shape[1] == HEAD else g


SHARD_W = 737
SHARD_WP = 768
IN_WIDTH = N_DEV * SHARD_W
SEGMENTS = ((GL0, 2824, 3072), (QF0, 768, 512), (KF0, 1280, 512), (VF0, 1792, 512), (QM0, 2312, 512),
            (QA0, 0, 512), (KA0, 512, 128), (VA0, 640, 128), (FL0, 2304, 8))


def _lane_plan(sources):
    plan = []
    for t in range(len(sources) // 128):
        groups = {}
        for lane in range(128):
            src = sources[128 * t + lane]
            if src is not None:
                slab, col = src
                groups.setdefault((slab, col // 128, (lane - col) % 128), []).append(lane)
        tile = []
        for key, lanes in groups.items():
            assert lanes == list(range(lanes[0], lanes[-1] + 1))
            tile.append((key, lanes[0], lanes[-1] + 1))
        plan.append(tile)
    return plan


def _assemble(tile_plan, load, rows):
    lane = lax.broadcasted_iota(jnp.int32, (1, 128), 1)
    out = jnp.zeros((rows, 128), F32)
    for (slab, st, roll), lo, hi in tile_plan:
        v = load(slab, st)
        if roll:
            v = pltpu.roll(v, roll, 1)
        out = v if (lo, hi) == (0, 128) else jnp.where((lane >= lo) & (lane < hi), v, out)
    return out


def _w_in_from_shards(land):
    ref_col = [None] * PROJ_W
    for p0, r0, n in SEGMENTS:
        for i in range(n):
            ref_col[p0 + i] = divmod(r0 + i, SHARD_W)
    plan = _lane_plan(ref_col)
    d_model = land.shape[1]
    tm = 256

    def body(land_ref, o_ref):
        load = lambda slab, st: land_ref[slab, :, st * 128:(st + 1) * 128].astype(F32)
        for t, tile_plan in enumerate(plan):
            o_ref[:, t * 128:(t + 1) * 128] = _assemble(tile_plan, load, tm).astype(BF16)

    return pl.pallas_call(
        body, name="w_in_from_shards", grid=(d_model // tm,),
        in_specs=[pl.BlockSpec((N_DEV, tm, SHARD_WP), lambda i: (0, i, 0))],
        out_specs=pl.BlockSpec((tm, PROJ_W), lambda i: (i, 0)),
        out_shape=jax.ShapeDtypeStruct((d_model, PROJ_W), BF16),
        compiler_params=_params(("parallel",)),
    )(land)


def _dw_in_to_parts(dwp):
    padded_col = [None] * IN_WIDTH
    for p0, r0, n in SEGMENTS:
        for i in range(n):
            padded_col[r0 + i] = p0 + i
    sources = []
    for d in range(N_DEV):
        sources += [(0, padded_col[SHARD_W * d + c]) if c < SHARD_W else None for c in range(SHARD_WP)]
    plan = _lane_plan(sources)
    d_model = dwp.shape[0]
    tm = 256
    tiles = SHARD_WP // 128

    def body(dw_ref, o_ref):
        load = lambda slab, st: dw_ref[:, st * 128:(st + 1) * 128].astype(F32)
        for t, tile_plan in enumerate(plan):
            d, c = divmod(t, tiles)
            o_ref[d, :, c * 128:(c + 1) * 128] = _assemble(tile_plan, load, tm).astype(BF16)

    return pl.pallas_call(
        body, name="dw_in_to_parts", grid=(d_model // tm,),
        in_specs=[pl.BlockSpec((tm, PROJ_W), lambda i: (i, 0))],
        out_specs=pl.BlockSpec((N_DEV, tm, SHARD_WP), lambda i: (0, i, 0)),
        out_shape=jax.ShapeDtypeStruct((N_DEV, d_model, SHARD_WP), BF16),
        compiler_params=_params(("parallel",)),
    )(dwp)


def _cast_shards(shards):
    names = list(shards)

    def body(*refs):
        for src, dst in zip(refs[:len(names)], refs[len(names):]):
            if dst.shape != src.shape:
                dst[...] = jnp.zeros(dst.shape, BF16)
                dst[:, 0:src.shape[1]] = src[...].astype(BF16)
            else:
                dst[...] = src[...].astype(BF16)

    out_shape = [jax.ShapeDtypeStruct((shards[n].shape[0], SHARD_WP if n == "w_in" else shards[n].shape[1]), BF16)
                 for n in names]
    outs = pl.pallas_call(body, name="cast_shards", out_shape=out_shape,
                          compiler_params=pltpu.CompilerParams(vmem_limit_bytes=VMEM_LIMIT))(*[shards[n] for n in names])
    return dict(zip(names, outs))


def _tie(x, *tokens):
    for t in tokens:
        if t is not None:
            x = x + t[0:1, 0:1]
    return x


def _local_step(x, mem, target, p, getw, emit, deps=()):
    s = x.shape[0]
    bucket, band = _bucket_table()
    bucket_m = jnp.where(band, bucket, -1).astype(jnp.int32)
    bias = _bias_table(p["rel_bias"], bucket_m)
    bucket_t = jnp.transpose(bucket_m)
    bias_t = _bias_table(p["rel_bias"], bucket_t)
    gqf, gkf, gqa, gka = _tile2(p["qn_fox"]), _tile2(p["kn_fox"]), _tile2(p["qn_swa"]), _tile2(p["kn_swa"])
    gqm = p["qn_mem"]
    bf128 = jnp.pad(p["b_forget"], ((0, 0), (0, 120)))
    sink = p["sink_swa"].reshape(8)

    h = _rms_fwd(x, p["g_mix"], "rms_mix", tuple(deps) + (bias, bias_t))
    w_in = getw("w_in", h)
    proj = _mm(h, w_in, "nn", BF16, 512, 1536, 1024, "proj")
    fl = _mm(h, w_in[:, FL0:FL0 + 128], "nn", F32, 512, 128, 1024, "proj_fl")
    qf, kf, vf, qm, qa, ka, va, qf_t, vf_t = _proj_post(proj, gqf, gkf, gqm, gqa, gka)
    cc4, ca4 = _fox_gate_fwd(fl, bf128)
    w_kv = getw("w_mem_kv", cc4)
    mem_n, kv_raw, mk, mv = _memkv_fwd(mem, p["g_mem"], w_kv, p["kn_mem"])
    kp = jnp.pad(ka, ((SWA_BLOCK, 0), (0, 0)))
    vp = jnp.pad(va, ((SWA_BLOCK, 0), (0, 0)))
    oa = _swa_fwd(qa, kp, vp, bias, sink)
    of, lse4, of_t = _fox_fwd(qf, kf, vf_t, ca4)
    om = _mem_fwd(qm, mk, mv)
    wa, wf, wm, w_out = getw("w_o_swa", oa), getw("w_o_fox", oa), getw("w_o_mem", oa), getw("w_out", oa)
    x1, hm, merged = _merge_fwd(x, oa, of, om, proj, p["b_gate"], wa, wf, wm, w_out, p["g_mlp"])
    w_up = getw("w_mlp_up", of)
    u = _mlp_up(hm, w_up)
    w_down = getw("w_mlp_down", hm)
    dy, dy_b, loss = _mlp_down_loss(u, w_down, x1, target)

    da = _mlp_bwd_act(dy_b, w_down, u)
    t_down = emit({"w_mlp_down": _mm(u, dy_b, "tn", BF16, 1024, 1024, 2048, "dw_down")})
    dx1, dg_mlp = _mlp_bwd_x(da, w_up, x1, dy, _tie(p["g_mlp"], t_down))
    t_up = emit({"w_mlp_up": _mm(hm, da, "tn", BF16, 1024, 1024, 2048, "dw_up", column_chunks=True)})
    dproj, doa, dof_t, dom, dya, dyf, dym, db_gate = _merge_bwd(
        dx1, oa, of, om, proj, _tie(p["b_gate"], t_up), wa, wf, wm, w_out)
    dw_oa, dw_of, dw_om = _mm_tn3([oa, of, om], [dya, dyf, dym], "dw_o")
    t_o = emit({"w_out": _mm(merged, dx1, "tn", BF16, 1024, 1024, 2048, "dw_out"),
                "w_o_swa": dw_oa, "w_o_fox": dw_of, "w_o_mem": dw_om})

    dqm, dmk, dmv = _mem_bwd(qm, mk, mv, dom)
    dw_kv, dkn_mem, dg_mem = _memkv_bwd(dmk, dmv, kv_raw, _tie(p["kn_mem"], t_o), mem, p["g_mem"], mem_n, w_kv)
    t_kv = emit({"w_mem_kv": dw_kv})
    dqa, dkp, dvp, dbias, dsink = _swa_bwd(qa, kp, vp, bias_t, _tie(p["sink_swa"], t_kv).reshape(8), doa)
    dqf_t, dkf, dvf, dck4, dcq4 = _fox_bwd(qf_t, kf, vf, dof_t, of_t, cc4, lse4)

    dfl, db_forget = _fox_gate_bwd(dcq4, dck4, fl, bf128)

    dproj, dgn = _proj_pre_bwd(dproj, proj, dqf_t, dkf, dvf, dqm, dqa, dkp[SWA_BLOCK:], dvp[SWA_BLOCK:], dfl,
                               gqf, gkf, gqm, gqa, gka)
    t_in = emit({"w_in": _mm(h, dproj, "tn", BF16, 1024, 3072, 1024, "dw_in")})
    grad_x, dg_mix = _in_bwd_x(dproj, w_in, x, _tie(p["g_mix"], t_in), dx1)
    d_rel = _rel_bias_bwd(dbias, bucket_t)

    fold = lambda r: dgn[r:r + 1, 0:HEAD] + dgn[r:r + 1, HEAD:128]
    small = {
        "g_mix": dg_mix, "b_gate": db_gate, "b_forget": db_forget[:, 0:8],
        "qn_swa": fold(3), "kn_swa": fold(4), "sink_swa": dsink[:, 0].reshape(1, 8), "rel_bias": d_rel,
        "qn_fox": fold(0), "kn_fox": fold(1), "g_mem": dg_mem, "qn_mem": dgn[2:3, :], "kn_mem": dkn_mem,
        "g_mlp": dg_mlp,
    }
    return loss, grad_x, small


SMALL = ("g_mix", "b_gate", "b_forget", "qn_swa", "kn_swa", "sink_swa", "rel_bias", "qn_fox", "kn_fox", "g_mem",
         "qn_mem", "kn_mem", "g_mlp")
BIG = ("w_in", "w_mem_kv", "w_o_swa", "w_o_fox", "w_o_mem", "w_out", "w_mlp_up", "w_mlp_down")
COL_SHARDED = ("w_in", "w_o_swa", "w_o_fox", "w_o_mem", "w_mlp_up")
WEIGHTS = ("g_mix", "w_in", "b_gate", "b_forget", "qn_swa", "kn_swa", "sink_swa", "rel_bias", "qn_fox", "kn_fox", "g_mem",
           "w_mem_kv", "qn_mem", "kn_mem", "w_o_swa", "w_o_fox", "w_o_mem", "w_out", "g_mlp", "w_mlp_up", "w_mlp_down")
SMALL_SLOTS = (("g_mix", 1024), ("b_gate", 3072), ("b_forget", 128), ("qn_swa", 128), ("kn_swa", 128), ("sink_swa", 128),
               ("rel_bias", REL_BUCKETS * 128), ("qn_fox", 128), ("kn_fox", 128), ("g_mem", 1024), ("qn_mem", 128),
               ("kn_mem", 128), ("g_mlp", 1024), ("loss", 128))
SMALL_OFF = {n: sum(w for _, w in SMALL_SLOTS[:i]) for i, (n, _) in enumerate(SMALL_SLOTS)}
SMALL_ROW = sum(w for _, w in SMALL_SLOTS)


def _gathered_to_full(name, g):
    if name in COL_SHARDED:
        return jnp.transpose(g, (1, 0, 2)).reshape(g.shape[1], N_DEV * g.shape[2])
    return g.reshape(N_DEV * g.shape[1], g.shape[2])


def _full_to_parts(name, full, b):
    if name in COL_SHARDED:
        return jnp.transpose(full.reshape(full.shape[0], N_DEV, b), (1, 0, 2)).astype(BF16)
    return full.reshape(N_DEV, full.shape[0] // N_DEV, full.shape[1]).astype(BF16)


def _pack_small(grads, loss):
    pieces = []
    for n, width in SMALL_SLOTS:
        a = loss.reshape(1, 1) if n == "loss" else grads[n].reshape(1, -1)
        pieces.append(jnp.pad(a, ((0, 0), (0, width - a.shape[1]))))
    return jnp.concatenate(pieces, axis=1)


def _adamw_small(gathered, w, m, v):
    names = list(SMALL)

    def body(*refs):
        p_ref = refs[0]
        ins = refs[1:1 + 3 * len(names)]
        outs = refs[1 + 3 * len(names):]
        g_all = p_ref[0]
        for k in range(1, N_DEV):
            g_all = g_all + p_ref[k]
        for i, n in enumerate(names):
            w_ref, m_ref, v_ref = ins[3 * i:3 * i + 3]
            out = outs[4 * i:4 * i + 4]
            rows, cols = w_ref.shape
            for r in range(rows):
                off = SMALL_OFF[n] + 128 * r
                g = g_all[:, off:off + cols]
                rs = slice(r, r + 1)
                res = (g,) + _adam_math(w_ref[rs, :], g, m_ref[rs, :], v_ref[rs, :])
                for o_ref, val in zip(out, res):
                    o_ref[rs, :] = val
        outs[-1][...] = g_all[:, SMALL_OFF["loss"]:SMALL_OFF["loss"] + 128]

    args = [gathered]
    out_shape = []
    for n in names:
        args += [w[n], m[n], v[n]]
        out_shape += [jax.ShapeDtypeStruct(w[n].shape, F32)] * 4
    out_shape.append(jax.ShapeDtypeStruct((1, 128), F32))
    outs = pl.pallas_call(body, name="adamw_small", out_shape=out_shape)(*args)
    return {n: outs[4 * i:4 * i + 4] for i, n in enumerate(names)}, outs[-1]


def kernel(x, mem, g_mix, w_in, b_gate, b_forget, qn_swa, kn_swa, sink_swa, rel_bias, qn_fox, kn_fox, g_mem, w_mem_kv, qn_mem, kn_mem, w_o_swa, w_o_fox, w_o_mem, w_out, g_mlp, w_mlp_up, w_mlp_down, loss_target, m_g_mix, m_w_in, m_b_gate, m_b_forget, m_qn_swa, m_kn_swa, m_sink_swa, m_rel_bias, m_qn_fox, m_kn_fox, m_g_mem, m_w_mem_kv, m_qn_mem, m_kn_mem, m_w_o_swa, m_w_o_fox, m_w_o_mem, m_w_out, m_g_mlp, m_w_mlp_up, m_w_mlp_down, v_g_mix, v_w_in, v_b_gate, v_b_forget, v_qn_swa, v_kn_swa, v_sink_swa, v_rel_bias, v_qn_fox, v_kn_fox, v_g_mem, v_w_mem_kv, v_qn_mem, v_kn_mem, v_w_o_swa, v_w_o_fox, v_w_o_mem, v_w_out, v_g_mlp, v_w_mlp_up, v_w_mlp_down):
    wts = dict(g_mix=g_mix, w_in=w_in, b_gate=b_gate, b_forget=b_forget, qn_swa=qn_swa, kn_swa=kn_swa, sink_swa=sink_swa,
               rel_bias=rel_bias, qn_fox=qn_fox, kn_fox=kn_fox, g_mem=g_mem, w_mem_kv=w_mem_kv, qn_mem=qn_mem, kn_mem=kn_mem,
               w_o_swa=w_o_swa, w_o_fox=w_o_fox, w_o_mem=w_o_mem, w_out=w_out, g_mlp=g_mlp, w_mlp_up=w_mlp_up,
               w_mlp_down=w_mlp_down)
    mom = dict(g_mix=m_g_mix, w_in=m_w_in, b_gate=m_b_gate, b_forget=m_b_forget, qn_swa=m_qn_swa, kn_swa=m_kn_swa,
               sink_swa=m_sink_swa, rel_bias=m_rel_bias, qn_fox=m_qn_fox, kn_fox=m_kn_fox, g_mem=m_g_mem, w_mem_kv=m_w_mem_kv,
               qn_mem=m_qn_mem, kn_mem=m_kn_mem, w_o_swa=m_w_o_swa, w_o_fox=m_w_o_fox, w_o_mem=m_w_o_mem, w_out=m_w_out,
               g_mlp=m_g_mlp, w_mlp_up=m_w_mlp_up, w_mlp_down=m_w_mlp_down)
    var = dict(g_mix=v_g_mix, w_in=v_w_in, b_gate=v_b_gate, b_forget=v_b_forget, qn_swa=v_qn_swa, kn_swa=v_kn_swa,
               sink_swa=v_sink_swa, rel_bias=v_rel_bias, qn_fox=v_qn_fox, kn_fox=v_kn_fox, g_mem=v_g_mem, w_mem_kv=v_w_mem_kv,
               qn_mem=v_qn_mem, kn_mem=v_kn_mem, w_o_swa=v_w_o_swa, w_o_fox=v_w_o_fox, w_o_mem=v_w_o_mem, w_out=v_w_out,
               g_mlp=v_g_mlp, w_mlp_up=v_w_mlp_up, w_mlp_down=v_w_mlp_down)

    shards = _cast_shards({n: wts[n][0] for n in BIG})
    first = _split_start([shards["w_in"]], True, "ag_start_w_in", peers=(1,) + SAME_CORE)
    rest = _split_start([shards[n] for n in BIG[1:]], True, "ag_start_rest", after=first[4])
    full = {}

    def getw(n, after):
        if n == "w_in" and n not in full:
            forwarded = _forward_start(first, after, "ag_forward_w_in")
            full[n] = _w_in_from_shards(_forward_wait(first, forwarded, "ag_wait_w_in"))
        elif n not in full:
            land = _split_wait(rest, BIG[1:].index(n), after, True, "ag_wait_" + n)
            full[n] = land if n == "w_mlp_up" else _gathered_to_full(n, land)
        return full[n]

    exchanges = {}

    def emit(grads_by_name):
        parts = []
        for n, grad in grads_by_name.items():
            if n == "w_in":
                parts.append(_dw_in_to_parts(grad))
            else:
                parts.append(grad if n == "w_mlp_up" else _full_to_parts(n, grad, wts[n].shape[2]))
        started = _split_start(parts, False, "rs_start_" + next(iter(grads_by_name)))
        for w, n in enumerate(grads_by_name):
            exchanges[n] = (started, w)
        return started[4]

    small_p = {n: wts[n] for n in SMALL}
    loss, grad_x, small_g = _local_step(x[0], mem[0], loss_target[0], small_p, getw, emit, (first[4], rest[4]))

    packed = _pack_small(small_g, loss)
    small_gather = _split_start([packed], True, "ag_start_small")

    grads, delta, new_m, new_v = {}, {}, {}, {}

    def update(n, after):
        land = _split_wait(*exchanges[n], after, False, "rs_wait_" + n)
        g, d, m2, v2 = _adamw(land, wts[n][0], mom[n][0], var[n][0], "adamw_" + n)
        grads[n], delta[n], new_m[n], new_v[n] = g[None], d[None], m2[None], v2[None]
        return d

    after = small_gather[4]
    for n in exchanges:
        if n != "w_in":
            after = update(n, after)

    gathered = _split_wait(small_gather, 0, after, True, "ag_wait_small")
    small_out, total = _adamw_small(gathered, small_p, mom, var)
    for name, (g, d, m2, v2) in small_out.items():
        grads[name], delta[name], new_m[name], new_v[name] = g, d, m2, v2
    update("w_in", total)

    return (total[0, 0], grad_x[None], *[grads[n] for n in WEIGHTS], *[delta[n] for n in WEIGHTS],
            *[new_m[n] for n in WEIGHTS], *[new_v[n] for n in WEIGHTS])
```

```python
import math

import jax
import jax.numpy as jnp
from jax import lax
from jax.experimental import pallas as pl
from jax.experimental.pallas import tpu as pltpu

F32 = jnp.float32
BF16 = jnp.bfloat16

D_MODEL = 1024
N_MEM = 256
D_FF = 4096
HEAD = 64
SWA_HEADS = 8
SWA_BLOCK = 128
MEM_HEADS = 4
MEM_HEAD = 128
EPS = 1e-6
NEG = -1e30
REL_BUCKETS = 32
REL_MAX_DIST = 128

ADAM_LR = 0.001
ADAM_B1 = 0.9
ADAM_B2 = 0.999
ADAM_EPS = 1e-08
ADAM_WD = 0.01
ADAM_STEP = 10

GL0, QF0, KF0, VF0, QM0, QA0, KA0, VA0, FL0 = 0, 3072, 3584, 4096, 4608, 5120, 5632, 5760, 5888
PROJ_W = 6144
HALF_W = 3072
H_QF, H_KF, H_VF, H_QM, H_QA, H_KA, H_VA, H_FL = 0, 512, 1024, 1536, 2048, 2560, 2688, 2816

VMEM_LIMIT = 56 * 1024 * 1024
N_DEV = 8
MESH = pl.DeviceIdType.MESH

NN = (((1,), (0,)), ((), ()))
NT = (((1,), (1,)), ((), ()))
TN = (((0,), (0,)), ((), ()))


def _dot(a, b, dims=NN):
    return lax.dot_general(a, b, dims, preferred_element_type=F32)


def _params(sem):
    return pltpu.CompilerParams(dimension_semantics=sem, vmem_limit_bytes=VMEM_LIMIT)


def _full(shape):
    nd = len(shape)
    return pl.BlockSpec(shape, lambda *_: (0,) * nd)


def _sigmoid(z):
    return 1.0 / (1.0 + jnp.exp(-z))


def _group_mean(v, hd):
    if hd == 128:
        return jnp.mean(v, axis=-1, keepdims=True)
    r = lax.broadcasted_iota(jnp.int32, (128, 128), 0) // HEAD
    c = lax.broadcasted_iota(jnp.int32, (128, 128), 1) // HEAD
    same_head = jnp.where(r == c, 1.0 / HEAD, 0.0).astype(BF16)
    total = None
    rest = v
    for _ in range(2):
        part = rest.astype(BF16)
        rest = rest - part.astype(F32)
        term = _dot(part, same_head)
        total = term if total is None else total + term
    return total


def _mm(a, b, mode, out_dtype, tm, tn, tk, name, column_chunks=False):
    if mode == "nn":
        m, k = a.shape
        n = b.shape[1]
    elif mode == "nt":
        m, k = a.shape
        n = b.shape[0]
    else:
        k, m = a.shape
        n = b.shape[1]
    tm, tn, tk = min(tm, m), min(tn, n), min(tk, k)
    nk = k // tk
    chunk = n // N_DEV
    per_tile = tn // chunk if column_chunks else 1
    dims = {"nn": NN, "nt": NT, "tn": TN}[mode]
    a_spec = pl.BlockSpec((tk, tm), lambda j, i, kk: (kk, i)) if mode == "tn" else pl.BlockSpec((tm, tk), lambda j, i, kk: (i, kk))
    b_spec = pl.BlockSpec((tn, tk), lambda j, i, kk: (j, kk)) if mode == "nt" else pl.BlockSpec((tk, tn), lambda j, i, kk: (kk, j))

    def body(a_ref, b_ref, o_ref, *acc):
        prod = _dot(a_ref[...].astype(BF16), b_ref[...].astype(BF16), dims)

        def write(res):
            if column_chunks:
                for c in range(per_tile):
                    o_ref[c] = res[:, c * chunk:(c + 1) * chunk].astype(o_ref.dtype)
            else:
                o_ref[...] = res.astype(o_ref.dtype)

        if nk == 1:
            write(prod)
        else:
            acc_ref, = acc
            kk = pl.program_id(2)

            @pl.when(kk == 0)
            def _():
                acc_ref[...] = prod

            @pl.when(kk > 0)
            def _():
                acc_ref[...] += prod

            @pl.when(kk == nk - 1)
            def _():
                write(acc_ref[...])

    return pl.pallas_call(
        body, name=name, grid=(n // tn, m // tm, nk),
        in_specs=[a_spec, b_spec],
        out_specs=(pl.BlockSpec((per_tile, tm, chunk), lambda j, i, kk: (j, i, 0)) if column_chunks
                   else pl.BlockSpec((tm, tn), lambda j, i, kk: (i, j))),
        out_shape=jax.ShapeDtypeStruct((N_DEV, m, chunk) if column_chunks else (m, n), out_dtype),
        scratch_shapes=[pltpu.VMEM((tm, tn), F32)] if nk > 1 else [],
        compiler_params=_params(("parallel", "parallel", "arbitrary")),
    )(a, b)


def _mm_tn3(a_list, b_list, name):
    s, m = a_list[0].shape
    n = b_list[0].shape[1]
    tk = min(2048, s)
    nk = s // tk

    def body(*refs):
        a_refs, b_refs, o_refs, acc_refs = refs[0:3], refs[3:6], refs[6:9], refs[9:12]
        kk = pl.program_id(0)
        for a_ref, b_ref, o_ref, acc_ref in zip(a_refs, b_refs, o_refs, acc_refs):
            prod = _dot(a_ref[...], b_ref[...], TN)
            if nk == 1:
                o_ref[...] = prod.astype(o_ref.dtype)
                continue

            @pl.when(kk == 0)
            def _(acc_ref=acc_ref, prod=prod):
                acc_ref[...] = prod

            @pl.when(kk > 0)
            def _(acc_ref=acc_ref, prod=prod):
                acc_ref[...] += prod

            @pl.when(kk == nk - 1)
            def _(acc_ref=acc_ref, o_ref=o_ref):
                o_ref[...] = acc_ref[...].astype(o_ref.dtype)

    return pl.pallas_call(
        body, name=name, grid=(nk,),
        in_specs=[pl.BlockSpec((tk, m), lambda kk: (kk, 0))] * 3 + [pl.BlockSpec((tk, n), lambda kk: (kk, 0))] * 3,
        out_specs=[_full((m, n))] * 3,
        out_shape=[jax.ShapeDtypeStruct((m, n), BF16)] * 3,
        scratch_shapes=[pltpu.VMEM((m, n), F32)] * 3,
        compiler_params=_params(("arbitrary",)),
    )(*a_list, *b_list)


def _rms_fwd(x, g, name, deps=()):
    s, d = x.shape
    tm = min(512, s)

    def body(x_ref, g_ref, *rest):
        h_ref = rest[len(deps)]
        xv = x_ref[...]
        r = lax.rsqrt(jnp.mean(xv * xv, axis=-1, keepdims=True) + EPS)
        h_ref[...] = (xv * r * g_ref[...]).astype(BF16)

    return pl.pallas_call(
        body, name=name, grid=(s // tm,),
        in_specs=[pl.BlockSpec((tm, d), lambda i: (i, 0)), _full((1, d))] + [pl.BlockSpec(memory_space=pl.ANY)] * len(deps),
        out_specs=pl.BlockSpec((tm, d), lambda i: (i, 0)),
        out_shape=jax.ShapeDtypeStruct((s, d), BF16),
        compiler_params=_params(("parallel",)),
    )(x, g, *deps)


def _proj_post(proj, gq_fox, gk_fox, gq_mem, gq_swa, gk_swa):
    s = proj.shape[0]
    tm = min(512, s)

    def body(p_ref, gqf, gkf, gqm, gqa, gka, qf_ref, kf_ref, vf_ref, qm_ref, qa_ref, ka_ref, va_ref, qft_ref, vft_ref):
        def norm(off, width, hd, g_ref, o_ref, scaled_t_ref=None):
            for b in range(width // 128):
                v = p_ref[:, off + b * 128: off + (b + 1) * 128].astype(F32)
                r = lax.rsqrt(_group_mean(v * v, hd) + EPS)
                vn = (v * r * g_ref[...]).astype(BF16)
                o_ref[:, b * 128:(b + 1) * 128] = vn
                if scaled_t_ref is not None:
                    scaled_t_ref[b * 128:(b + 1) * 128, :] = (vn.astype(F32) * 0.125).T.astype(BF16)

        norm(H_QF, 512, HEAD, gqf, qf_ref, qft_ref)
        norm(H_KF, 512, HEAD, gkf, kf_ref)
        vf_ref[...] = p_ref[:, H_VF:H_VF + 512].astype(BF16)
        for b in range(4):
            vft_ref[b * 128:(b + 1) * 128, :] = p_ref[:, H_VF + b * 128:H_VF + (b + 1) * 128].astype(F32).T.astype(BF16)
        norm(H_QM, 512, MEM_HEAD, gqm, qm_ref)
        norm(H_QA, 512, HEAD, gqa, qa_ref)
        norm(H_KA, 128, HEAD, gka, ka_ref)
        va_ref[...] = p_ref[:, H_VA:H_VA + 128].astype(BF16)

    g_spec = _full((1, 128))
    o512 = pl.BlockSpec((tm, 512), lambda i: (i, 0))
    o128 = pl.BlockSpec((tm, 128), lambda i: (i, 0))
    s512 = jax.ShapeDtypeStruct((s, 512), BF16)
    s128 = jax.ShapeDtypeStruct((s, 128), BF16)
    return pl.pallas_call(
        body, name="proj_post", grid=(s // tm,),
        in_specs=[pl.BlockSpec((tm, HALF_W), lambda i: (i, 1)), g_spec, g_spec, g_spec, g_spec, g_spec],
        out_specs=[o512, o512, o512, o512, o512, o128, o128] + [pl.BlockSpec((512, tm), lambda i: (0, i))] * 2,
        out_shape=[s512, s512, s512, s512, s512, s128, s128] + [jax.ShapeDtypeStruct((512, s), BF16)] * 2,
        compiler_params=_params(("parallel",)),
    )(proj, gq_fox, gk_fox, gq_mem, gq_swa, gk_swa)


def _tri(n, lower):
    r = lax.broadcasted_iota(jnp.int32, (n, n), 0)
    c = lax.broadcasted_iota(jnp.int32, (n, n), 1)
    return jnp.where((c <= r) if lower else (c >= r), 1.0, 0.0).astype(F32)


def _fox_gate_fwd(proj, b_forget128):
    s = proj.shape[0]
    tm = min(512, s)

    def body(p_ref, b_ref, cc_ref, ca_ref, carry_ref):
        i = pl.program_id(0)

        @pl.when(i == 0)
        def _():
            carry_ref[...] = jnp.zeros_like(carry_ref)

        z = p_ref[...] + b_ref[...]
        logf = jnp.minimum(z, 0.0) - jnp.log(1.0 + jnp.exp(-jnp.abs(z)))
        c = jnp.dot(_tri(tm, True), logf, precision=lax.Precision.HIGHEST, preferred_element_type=F32) + carry_ref[...]
        carry_ref[...] = c[tm - 1:tm, :]
        lane = lax.broadcasted_iota(jnp.int32, (tm, 128), 1)
        for hp in range(4):
            cc_ref[hp] = c if hp == 0 else pltpu.roll(c, 128 - 2 * hp, 1)
            aug = jnp.zeros((tm, 128), F32)
            for e in range(2):
                rest = jnp.broadcast_to(c[:, 2 * hp + e:2 * hp + e + 1], (tm, 128))
                for part in range(3):
                    piece = rest.astype(BF16).astype(F32)
                    aug = jnp.where(lane == HEAD * (1 - e) + part, piece, aug)
                    rest = rest - piece
            ca_ref[hp] = aug.astype(BF16)

    return pl.pallas_call(
        body, name="fox_gate_fwd", grid=(s // tm,),
        in_specs=[pl.BlockSpec((tm, 128), lambda i: (i, 0)), _full((1, 128))],
        out_specs=[pl.BlockSpec((4, tm, 128), lambda i: (0, i, 0))] * 2,
        out_shape=[jax.ShapeDtypeStruct((4, s, 128), F32), jax.ShapeDtypeStruct((4, s, 128), BF16)],
        scratch_shapes=[pltpu.VMEM((1, 128), F32)],
        compiler_params=_params(("arbitrary",)),
    )(proj, b_forget128)


def _memkv_fwd(mem, g_mem, w_kv, kn_mem):
    m = mem.shape[0]

    def body(mem_ref, g_ref, w_ref, kn_ref, memn_ref, kv_ref, mk_ref, mv_ref):
        xv = mem_ref[...]
        r = lax.rsqrt(jnp.mean(xv * xv, axis=-1, keepdims=True) + EPS)
        mn = (xv * r * g_ref[...]).astype(BF16)
        memn_ref[...] = mn
        kv = _dot(mn, w_ref[...])
        kv_ref[...] = kv
        for h in range(MEM_HEADS):
            v = kv[:, h * 128:(h + 1) * 128]
            rr = lax.rsqrt(jnp.mean(v * v, axis=-1, keepdims=True) + EPS)
            mk_ref[:, h * 128:(h + 1) * 128] = (v * rr * kn_ref[...]).astype(BF16)
        mv_ref[...] = kv[:, 512:1024].astype(BF16)

    return pl.pallas_call(
        body, name="memkv_fwd",
        out_shape=[jax.ShapeDtypeStruct((m, D_MODEL), BF16), jax.ShapeDtypeStruct((m, 1024), F32),
                   jax.ShapeDtypeStruct((m, 512), BF16), jax.ShapeDtypeStruct((m, 512), BF16)],
        compiler_params=pltpu.CompilerParams(vmem_limit_bytes=VMEM_LIMIT),
    )(mem, g_mem, w_kv, kn_mem)


def _bias_table(rel_bias, bucket):
    def body(rb_ref, bk_ref, o_ref):
        bk = bk_ref[...]
        for h in range(SWA_HEADS):
            acc = jnp.zeros(bk.shape, F32)
            for b in range(REL_BUCKETS):
                acc = jnp.where(bk == b, rb_ref[b, h], acc)
            o_ref[h] = acc

    return pl.pallas_call(
        body, name="bias_table",
        in_specs=[pl.BlockSpec(memory_space=pltpu.SMEM), pl.BlockSpec(memory_space=pltpu.VMEM)],
        out_shape=jax.ShapeDtypeStruct((SWA_HEADS,) + bucket.shape, F32),
    )(rel_bias, bucket)


def _swa_valid(n):
    row = lax.broadcasted_iota(jnp.int32, (SWA_BLOCK, 2 * SWA_BLOCK), 0)
    col = lax.broadcasted_iota(jnp.int32, (SWA_BLOCK, 2 * SWA_BLOCK), 1)
    dist = row + SWA_BLOCK - col
    return (dist >= 0) & (dist < SWA_BLOCK) & ((col >= SWA_BLOCK) | (n > 0))


def _swa_fwd(qa, kp, vp, bias, sink):
    s = qa.shape[0]
    nb = s // SWA_BLOCK

    def body(sink_ref, q_ref, kp_ref, vp_ref, bias_ref, o_ref):
        n = pl.program_id(0)
        start = pl.multiple_of(n * SWA_BLOCK, SWA_BLOCK)
        k2 = kp_ref[pl.ds(start, 2 * SWA_BLOCK), :]
        v2 = vp_ref[pl.ds(start, 2 * SWA_BLOCK), :]
        valid = _swa_valid(n)
        heads = range(SWA_HEADS)
        hs = lambda h: slice(h * HEAD, (h + 1) * HEAD)
        sc = [jnp.where(valid, _dot(q_ref[:, hs(h)], k2[:, hs(h // 4)], NT) * 0.125 + bias_ref[h], NEG) for h in heads]
        pn = []
        for h in heads:
            sk = sink_ref[h]
            mx = jnp.maximum(jnp.max(sc[h], axis=-1, keepdims=True), sk)
            p = jnp.exp(sc[h] - mx)
            den = jnp.sum(p, axis=-1, keepdims=True) + jnp.exp(sk - mx)
            pn.append((p / den).astype(BF16))
        outs = [_dot(pn[h], v2[:, hs(h // 4)]).astype(BF16) for h in heads]
        for h in heads:
            o_ref[:, hs(h)] = outs[h]

    return pl.pallas_call(
        body, name="swa_fwd", grid=(nb,),
        in_specs=[pl.BlockSpec(memory_space=pltpu.SMEM),
                  pl.BlockSpec((SWA_BLOCK, 512), lambda n: (n, 0)),
                  _full(kp.shape), _full(vp.shape), _full(bias.shape)],
        out_specs=pl.BlockSpec((SWA_BLOCK, 512), lambda n: (n, 0)),
        out_shape=jax.ShapeDtypeStruct((s, 512), BF16),
        compiler_params=_params(("parallel",)),
    )(sink, qa, kp, vp, bias)


def _head_mask(e):
    lane = lax.broadcasted_iota(jnp.int32, (1, 128), 1)
    return (lane >= e * HEAD) & (lane < (e + 1) * HEAD)


FOX_FWD_T = 1024
FOX_BWD_T = 512


def _head_rows(e):
    row = lax.broadcasted_iota(jnp.int32, (128, 1), 0)
    return (row >= e * HEAD) & (row < (e + 1) * HEAD)


def _fox_fwd(q, k, v_t, ca4):
    s = q.shape[0]
    t = min(FOX_FWD_T, s)
    nq = s // t

    def body(q_ref, k_ref, vt_ref, ca_ref, o_ref, lse_ref, ot_ref):
        i = pl.program_id(1)
        qs = q_ref[...] * jnp.asarray(0.125, BF16)
        lane = lax.broadcasted_iota(jnp.int32, (1, 128), 1)
        minus = [jnp.where((lane >= HEAD * (1 - e)) & (lane < HEAD * (1 - e) + 3), -1.0, 0.0).astype(BF16) for e in range(2)]
        qe = [jnp.where(_head_mask(e), qs, jnp.broadcast_to(minus[e], qs.shape)) for e in range(2)]

        def block(carry, key0, nkeys, q0, nqs, masked):
            ks = pl.ds(pl.multiple_of(key0, 128), nkeys)
            kj = k_ref[ks, :]
            caj = ca_ref[0, ks, :]
            vtj = vt_ref[:, ks]
            out = []
            for e in range(2):
                m_all, acc_all = carry[2 * e], carry[2 * e + 1]
                m, acc = m_all[:, q0:q0 + nqs], acc_all[:, q0:q0 + nqs]
                st = _dot(jnp.where(_head_mask(e), kj, caj), qe[e][q0:q0 + nqs, :], NT)
                if masked:
                    krow = lax.broadcasted_iota(jnp.int32, (nkeys, nqs), 0) + key0
                    qcol = lax.broadcasted_iota(jnp.int32, (nkeys, nqs), 1) + (i * t + q0)
                    st = jnp.where(krow <= qcol, st, NEG)
                m_new = jnp.maximum(m, jnp.max(st, axis=0, keepdims=True))
                alpha = jnp.exp(m - m_new)
                pt = jnp.exp(st - m_new).astype(BF16)
                vte = jnp.where(_head_rows(e), vtj, jnp.ones_like(vtj))
                acc_new = alpha * acc + _dot(vte, pt)
                if nqs < t:
                    m_new = jnp.concatenate([m_all[:, :q0], m_new], axis=1)
                    acc_new = jnp.concatenate([acc_all[:, :q0], acc_new], axis=1)
                out += [m_new, acc_new]
            return tuple(out)

        half = t // 2
        init = (jnp.full((1, t), NEG, F32), jnp.zeros((128, t), F32)) * 2
        carry = lax.fori_loop(0, i, lambda j, c: block(c, j * t, t, 0, t, False), init)
        carry = block(carry, i * t, half, 0, t, True)
        m0, a0, m1, a1 = block(carry, i * t + half, half, half, half, True)
        l0 = a0[HEAD:HEAD + 1, :]
        l1 = a1[0:1, :]
        o_t = jnp.where(_head_rows(0), a0 / l0, a1 / l1)
        o_ref[...] = o_t.T.astype(BF16)
        ot_ref[...] = o_t.astype(BF16)
        r8 = lax.broadcasted_iota(jnp.int32, (8, t), 0)
        lse_ref[0] = jnp.where(r8 == 0, m0 + jnp.log(l0), jnp.where(r8 == 1, m1 + jnp.log(l1), 0.0))

    return pl.pallas_call(
        body, name="fox_fwd", grid=(4, nq),
        in_specs=[pl.BlockSpec((t, 128), lambda hp, i: (i, hp)),
                  pl.BlockSpec((s, 128), lambda hp, i: (0, hp)),
                  pl.BlockSpec((128, s), lambda hp, i: (hp, 0)),
                  pl.BlockSpec((1, s, 128), lambda hp, i: (hp, 0, 0))],
        out_specs=[pl.BlockSpec((t, 128), lambda hp, i: (i, hp)),
                   pl.BlockSpec((1, 8, t), lambda hp, i: (hp, 0, i)),
                   pl.BlockSpec((128, t), lambda hp, i: (hp, i))],
        out_shape=[jax.ShapeDtypeStruct((s, 512), BF16), jax.ShapeDtypeStruct((4, 8, s), F32),
                   jax.ShapeDtypeStruct((512, s), BF16)],
        compiler_params=_params(("parallel", "parallel")),
    )(q, k, v_t, ca4)


MEM_SCALE = MEM_HEAD ** -0.5


def _mem_fwd(qm, mk, mv):
    s = qm.shape[0]
    tq = min(512, s)

    def body(q_ref, mk_ref, mv_ref, o_ref):
        for h in range(MEM_HEADS):
            hs = slice(h * 128, (h + 1) * 128)
            sc = _dot(q_ref[:, hs], mk_ref[:, hs], NT) * MEM_SCALE
            mx = jnp.max(sc, axis=-1, keepdims=True)
            p = jnp.exp(sc - mx)
            p = p / jnp.sum(p, axis=-1, keepdims=True)
            o_ref[:, hs] = _dot(p.astype(BF16), mv_ref[:, hs]).astype(BF16)

    return pl.pallas_call(
        body, name="mem_fwd", grid=(s // tq,),
        in_specs=[pl.BlockSpec((tq, 512), lambda i: (i, 0)), _full(mk.shape), _full(mv.shape)],
        out_specs=pl.BlockSpec((tq, 512), lambda i: (i, 0)),
        out_shape=jax.ShapeDtypeStruct((s, 512), BF16),
        compiler_params=_params(("parallel",)),
    )(qm, mk, mv)


def _merge_fwd(x, oa, of, om, proj, b_gate, wa, wf, wm, w_out, g_mlp):
    s = x.shape[0]
    tm = min(512, s)

    def body(x_ref, oa_ref, of_ref, om_ref, gl_ref, bg_ref, wa_ref, wf_ref, wm_ref, wo_ref, g_ref, x1_ref, hm_ref, mg_ref):
        merged = None
        for b, (o_ref, w_ref) in enumerate(((oa_ref, wa_ref), (of_ref, wf_ref), (om_ref, wm_ref))):
            cs = slice(b * D_MODEL, (b + 1) * D_MODEL)
            y = _dot(o_ref[...], w_ref[...])
            t = _sigmoid(gl_ref[:, cs].astype(F32) + bg_ref[:, cs]) * y
            merged = t if merged is None else merged + t
        mb = merged.astype(BF16)
        mg_ref[...] = mb
        x1 = x_ref[...] + _dot(mb, wo_ref[...])
        x1_ref[...] = x1
        r = lax.rsqrt(jnp.mean(x1 * x1, axis=-1, keepdims=True) + EPS)
        hm_ref[...] = (x1 * r * g_ref[...]).astype(BF16)

    row = lambda w: pl.BlockSpec((tm, w), lambda i: (i, 0))
    return pl.pallas_call(
        body, name="merge_fwd", grid=(s // tm,),
        in_specs=[row(D_MODEL), row(512), row(512), row(512), row(HALF_W), _full((1, HALF_W)),
                  _full(wa.shape), _full(wf.shape), _full(wm.shape), _full(w_out.shape), _full((1, D_MODEL))],
        out_specs=[row(D_MODEL), row(D_MODEL), row(D_MODEL)],
        out_shape=[jax.ShapeDtypeStruct((s, D_MODEL), F32), jax.ShapeDtypeStruct((s, D_MODEL), BF16),
                   jax.ShapeDtypeStruct((s, D_MODEL), BF16)],
        compiler_params=_params(("parallel",)),
    )(x, oa, of, om, proj, b_gate, wa, wf, wm, w_out, g_mlp)


def _mlp_up(hm, w_up):
    s = hm.shape[0]
    tm, tn = min(1024, s), w_up.shape[2]

    def body(h_ref, w_ref, u_ref):
        r = jnp.maximum(_dot(h_ref[...], w_ref[0]), 0.0)
        u_ref[...] = (r * r).astype(BF16)

    return pl.pallas_call(
        body, name="mlp_up", grid=(s // tm, D_FF // tn),
        in_specs=[pl.BlockSpec((tm, D_MODEL), lambda i, j: (i, 0)), pl.BlockSpec((1, D_MODEL, tn), lambda i, j: (j, 0, 0))],
        out_specs=pl.BlockSpec((tm, tn), lambda i, j: (i, j)),
        out_shape=jax.ShapeDtypeStruct((s, D_FF), BF16),
        compiler_params=_params(("parallel", "parallel")),
    )(hm, w_up)


def _mlp_down_loss(u, w_down, x1, target):
    s = u.shape[0]
    tm = min(256, s)

    def body(u_ref, w_ref, x1_ref, t_ref, dy_ref, dyb_ref, loss_ref):
        i = pl.program_id(0)

        @pl.when(i == 0)
        def _():
            loss_ref[...] = jnp.zeros_like(loss_ref)

        y = x1_ref[...] + _dot(u_ref[...], w_ref[...])
        err = y - t_ref[...]
        dy = err * (1.0 / D_MODEL)
        dy_ref[...] = dy
        dyb_ref[...] = dy.astype(BF16)
        part = jnp.sum(jnp.sum(err * err, axis=-1, keepdims=True) * (1.0 / D_MODEL), axis=0, keepdims=True)
        loss_ref[...] += 0.5 * part

    row = pl.BlockSpec((tm, D_MODEL), lambda i: (i, 0))
    return pl.pallas_call(
        body, name="mlp_down_loss", grid=(s // tm,),
        in_specs=[pl.BlockSpec((tm, D_FF), lambda i: (i, 0)), _full(w_down.shape), row, row],
        out_specs=[row, row, _full((1, 1))],
        out_shape=[jax.ShapeDtypeStruct((s, D_MODEL), F32), jax.ShapeDtypeStruct((s, D_MODEL), BF16),
                   jax.ShapeDtypeStruct((1, 1), F32)],
        compiler_params=_params(("arbitrary",)),
    )(u, w_down, x1, target)


def _mlp_bwd_act(dy, w_down, u):
    s = dy.shape[0]
    tm, tn = min(1024, s), 1024

    def body(dy_ref, w_ref, u_ref, da_ref):
        du = _dot(dy_ref[...], w_ref[...], NT)
        da_ref[...] = (du * (2.0 * jnp.sqrt(u_ref[...].astype(F32)))).astype(BF16)

    return pl.pallas_call(
        body, name="mlp_bwd_act", grid=(D_FF // tn, s // tm),
        in_specs=[pl.BlockSpec((tm, D_MODEL), lambda j, i: (i, 0)), pl.BlockSpec((tn, D_MODEL), lambda j, i: (j, 0)),
                  pl.BlockSpec((tm, tn), lambda j, i: (i, j))],
        out_specs=pl.BlockSpec((tm, tn), lambda j, i: (i, j)),
        out_shape=jax.ShapeDtypeStruct((s, D_FF), BF16),
        compiler_params=_params(("parallel", "parallel")),
    )(dy, w_down, u)


def _rms_bwd(xv, g, dh, skip):
    r = lax.rsqrt(jnp.mean(xv * xv, axis=-1, keepdims=True) + EPS)
    n = xv * r
    dn = dh * g
    dx = skip + r * (dn - n * jnp.mean(dn * n, axis=-1, keepdims=True))
    return dx, jnp.sum(dh * n, axis=0, keepdims=True)


def _mlp_bwd_x(da, w_up, x1, dy, g_mlp):
    s = da.shape[0]
    tm = min(256, s)

    def body(da_ref, w_ref, x1_ref, dy_ref, g_ref, dx1_ref, dg_ref):
        i = pl.program_id(0)

        @pl.when(i == 0)
        def _():
            dg_ref[...] = jnp.zeros_like(dg_ref)

        tn = w_ref.shape[2]
        dhm = _dot(da_ref[:, 0:tn], w_ref[0], NT)
        for j in range(1, N_DEV):
            dhm = dhm + _dot(da_ref[:, j * tn:(j + 1) * tn], w_ref[j], NT)
        dx, dg = _rms_bwd(x1_ref[...], g_ref[...], dhm, dy_ref[...])
        dx1_ref[...] = dx
        dg_ref[...] += dg

    row = pl.BlockSpec((tm, D_MODEL), lambda i: (i, 0))
    return pl.pallas_call(
        body, name="mlp_bwd_x", grid=(s // tm,),
        in_specs=[pl.BlockSpec((tm, D_FF), lambda i: (i, 0)), _full(w_up.shape), row, row, _full((1, D_MODEL))],
        out_specs=[row, _full((1, D_MODEL))],
        out_shape=[jax.ShapeDtypeStruct((s, D_MODEL), F32), jax.ShapeDtypeStruct((1, D_MODEL), F32)],
        compiler_params=_params(("arbitrary",)),
    )(da, w_up, x1, dy, g_mlp)


def _merge_bwd(dx1, oa, of, om, proj, b_gate, wa, wf, wm, w_out):
    s = dx1.shape[0]
    tm = min(512, s)

    def body(dx1_ref, oa_ref, of_ref, om_ref, gl_ref, bg_ref, wa_ref, wf_ref, wm_ref, wo_ref,
             dp_ref, doa_ref, dof_ref, dom_ref, dya_ref, dyf_ref, dym_ref, dbg_ref):
        i = pl.program_id(0)

        @pl.when(i == 0)
        def _():
            dbg_ref[...] = jnp.zeros_like(dbg_ref)

        dmerged = _dot(dx1_ref[...].astype(BF16), wo_ref[...], NT)
        branches = ((oa_ref, wa_ref, doa_ref, dya_ref), (of_ref, wf_ref, dof_ref, dyf_ref), (om_ref, wm_ref, dom_ref, dym_ref))
        for b, (o_ref, w_ref, do_ref, dyb_ref) in enumerate(branches):
            cs = slice(b * D_MODEL, (b + 1) * D_MODEL)
            y = _dot(o_ref[...], w_ref[...])
            g = _sigmoid(gl_ref[:, cs].astype(F32) + bg_ref[:, cs])
            dz = (dmerged * y) * g * (1.0 - g)
            dp_ref[:, cs] = dz.astype(BF16)
            dbg_ref[:, cs] += jnp.sum(dz, axis=0, keepdims=True)
            dyb = (dmerged * g).astype(BF16)
            dyb_ref[...] = dyb
            do = _dot(dyb, w_ref[...], NT)
            do_ref[...] = (do.T if b == 1 else do).astype(BF16)

    row = lambda w: pl.BlockSpec((tm, w), lambda i: (i, 0))
    sd = lambda w: jax.ShapeDtypeStruct((s, w), BF16)
    return pl.pallas_call(
        body, name="merge_bwd", grid=(s // tm,),
        in_specs=[row(D_MODEL), row(512), row(512), row(512), row(HALF_W), _full((1, HALF_W)),
                  _full(wa.shape), _full(wf.shape), _full(wm.shape), _full(w_out.shape)],
        out_specs=[row(HALF_W), row(512), pl.BlockSpec((512, tm), lambda i: (0, i)), row(512),
                   row(D_MODEL), row(D_MODEL), row(D_MODEL), _full((1, HALF_W))],
        out_shape=[sd(PROJ_W), sd(512), jax.ShapeDtypeStruct((512, s), BF16), sd(512), sd(D_MODEL), sd(D_MODEL), sd(D_MODEL),
                   jax.ShapeDtypeStruct((1, HALF_W), F32)],
        compiler_params=_params(("arbitrary",)),
    )(dx1, oa, of, om, proj, b_gate, wa, wf, wm, w_out)


def _swa_valid_t(n):
    key = lax.broadcasted_iota(jnp.int32, (2 * SWA_BLOCK, SWA_BLOCK), 0)
    qry = lax.broadcasted_iota(jnp.int32, (2 * SWA_BLOCK, SWA_BLOCK), 1)
    dist = qry + SWA_BLOCK - key
    return (dist >= 0) & (dist < SWA_BLOCK) & ((key >= SWA_BLOCK) | (n > 0))


def _swa_bwd(qa, kp, vp, bias_t, sink, doa):
    s = qa.shape[0]
    nb = s // SWA_BLOCK

    def body(sink_ref, q_ref, kp_ref, vp_ref, bias_ref, do_ref, dq_ref, dkp_ref, dvp_ref, dbias_ref, dsink_ref, sk_acc):
        n = pl.program_id(0)

        @pl.when(n == 0)
        def _():
            dkp_ref[...] = jnp.zeros_like(dkp_ref)
            dvp_ref[...] = jnp.zeros_like(dvp_ref)
            dbias_ref[...] = jnp.zeros_like(dbias_ref)
            sk_acc[...] = jnp.zeros_like(sk_acc)

        start = pl.multiple_of(n * SWA_BLOCK, SWA_BLOCK)
        win = pl.ds(start, 2 * SWA_BLOCK)
        k2 = kp_ref[win, :]
        v2 = vp_ref[win, :]
        valid = _swa_valid_t(n)
        heads = range(SWA_HEADS)
        hs = lambda h: slice(h * HEAD, (h + 1) * HEAD)
        scale = jnp.asarray(0.125, BF16)
        q = [q_ref[:, hs(h)] for h in heads]
        do = [do_ref[:, hs(h)] for h in heads]
        kk = [k2[:, hs(kv)] for kv in range(2)]
        vv = [v2[:, hs(kv)] for kv in range(2)]
        kt = [(kk[kv].astype(F32) * 0.125).T.astype(BF16) for kv in range(2)]
        st = [jnp.where(valid, _dot(kk[h // 4], q[h], NT) * 0.125 + bias_ref[h], NEG) for h in heads]
        dpt = [_dot(vv[h // 4], do[h], NT) for h in heads]
        pt, dst = [], []
        for h in heads:
            sk = sink_ref[h]
            mx = jnp.maximum(jnp.max(st[h], axis=0, keepdims=True), sk)
            p = jnp.exp(st[h] - mx)
            esk = jnp.exp(sk - mx)
            den = jnp.sum(p, axis=0, keepdims=True) + esk
            p = p / den
            delta = jnp.sum(p * dpt[h], axis=0, keepdims=True)
            d = p * (dpt[h] - delta)
            sk_acc[h:h + 1, :] += -(esk / den) * delta
            dbias_ref[h] += d
            pt.append(p.astype(BF16))
            dst.append(d.astype(BF16))
        dq_t = [_dot(kt[h // 4], dst[h]) for h in heads]
        dq_ref[...] = jnp.concatenate(dq_t, axis=0).T.astype(BF16)
        for kv in range(2):
            group = range(4 * kv, 4 * kv + 4)
            dk = [_dot(dst[h], q[h] * scale) for h in group]
            dv = [_dot(pt[h], do[h]) for h in group]
            dkp_ref[win, hs(kv)] += (dk[0] + dk[1]) + (dk[2] + dk[3])
            dvp_ref[win, hs(kv)] += (dv[0] + dv[1]) + (dv[2] + dv[3])

        @pl.when(n == nb - 1)
        def _():
            dsink_ref[...] = jnp.broadcast_to(jnp.sum(sk_acc[...], axis=1, keepdims=True), dsink_ref.shape)

    return pl.pallas_call(
        body, name="swa_bwd", grid=(nb,),
        in_specs=[pl.BlockSpec(memory_space=pltpu.SMEM),
                  pl.BlockSpec((SWA_BLOCK, 512), lambda n: (n, 0)),
                  _full(kp.shape), _full(vp.shape), _full(bias_t.shape),
                  pl.BlockSpec((SWA_BLOCK, 512), lambda n: (n, 0))],
        out_specs=[pl.BlockSpec((SWA_BLOCK, 512), lambda n: (n, 0)), _full(kp.shape), _full(vp.shape),
                   _full(bias_t.shape), _full((SWA_HEADS, 128))],
        out_shape=[jax.ShapeDtypeStruct((s, 512), BF16), jax.ShapeDtypeStruct(kp.shape, F32),
                   jax.ShapeDtypeStruct(vp.shape, F32), jax.ShapeDtypeStruct(bias_t.shape, F32),
                   jax.ShapeDtypeStruct((SWA_HEADS, 128), F32)],
        scratch_shapes=[pltpu.VMEM((SWA_HEADS, 128), F32)],
        compiler_params=_params(("arbitrary",)),
    )(sink, qa, kp, vp, bias_t, doa)


def _fox_bwd(qt, k, v, dot, ot, cc4, lse4):
    s = k.shape[0]
    t = min(FOX_BWD_T, s)
    nq = s // t

    def body(qt_ref, k_ref, v_ref, dot_ref, ot_ref, cc_ref, lse_ref,
             dqt_ref, dk_ref, dv_ref, dck_ref, dcq_ref, delta_ref, dkt_acc, dvt_acc, ds0, ds1):
        j = pl.program_id(1)

        @pl.when(j == 0)
        def _():
            dqt_ref[...] = jnp.zeros_like(dqt_ref)
            dcq_ref[...] = jnp.zeros_like(dcq_ref)
            r8 = lax.broadcasted_iota(jnp.int32, (8, t), 0)

            def dl(i, c):
                cols = pl.ds(pl.multiple_of(i * t, t), t)
                pr = dot_ref[:, cols].astype(F32) * ot_ref[:, cols].astype(F32)
                d0 = jnp.sum(jnp.where(_head_rows(0), pr, 0.0), axis=0, keepdims=True)
                d1 = jnp.sum(jnp.where(_head_rows(1), pr, 0.0), axis=0, keepdims=True)
                delta_ref[:, cols] = jnp.where(r8 == 0, d0, jnp.where(r8 == 1, d1, 0.0))
                return c

            lax.fori_loop(0, nq, dl, 0)

        kj = k_ref[...]
        vj = v_ref[...]
        ks = pl.ds(pl.multiple_of(j * t, t), t)
        kt = (kj.astype(F32) * 0.125).T.astype(BF16)
        ke = [jnp.where(_head_mask(e), kj, jnp.zeros_like(kj)) for e in range(2)]
        ve = [jnp.where(_head_mask(e), vj, jnp.zeros_like(vj)) for e in range(2)]
        ck = [cc_ref[0, ks, e:e + 1] for e in range(2)]
        for r in (dkt_acc, dvt_acc, ds0, ds1):
            r[...] = jnp.zeros_like(r)

        def block(q0, nqs, k0, nks, masked):
            cols = pl.ds(pl.multiple_of(q0, 128), nqs)
            rows = slice(k0, k0 + nks)
            qti = qt_ref[:, cols]
            doti = dot_ref[:, cols]
            for e, ds_acc in enumerate((ds0, ds1)):
                dims = slice(e * HEAD, (e + 1) * HEAD)
                st = _dot(ke[e][rows, :], qti) - ck[e][rows, :]
                if masked:
                    krow = lax.broadcasted_iota(jnp.int32, (nks, nqs), 0) + (j * t + k0)
                    qcol = lax.broadcasted_iota(jnp.int32, (nks, nqs), 1) + q0
                    st = jnp.where(krow <= qcol, st, NEG)
                pt = jnp.exp(st - lse_ref[0, e:e + 1, cols])
                dpt = _dot(ve[e][rows, :], doti)
                dst = pt * (dpt - delta_ref[e:e + 1, cols])
                dsb = dst.astype(BF16)
                dvt_acc[dims, rows] += _dot(doti[dims, :], pt.astype(BF16), NT)
                dkt_acc[dims, rows] += _dot(qti[dims, :], dsb, NT)
                dqt_ref[dims, cols] += _dot(kt[dims, rows], dsb)
                ds_acc[rows, 0:nqs] += dst
                dcq_ref[0, e:e + 1, cols] += jnp.sum(dst, axis=0, keepdims=True)

        half = t // 2
        block(j * t, half, 0, half, True)
        block(j * t + half, half, 0, t, True)

        def rest(i, c):
            block(i * t, t, 0, t, False)
            return c

        lax.fori_loop(j + 1, nq, rest, 0)
        dk_ref[...] = dkt_acc[...].T.astype(BF16)
        dv_ref[...] = dvt_acc[...].T.astype(BF16)
        lane = lax.broadcasted_iota(jnp.int32, (t, 128), 1)
        c0 = jnp.sum(ds0[...], axis=-1, keepdims=True)
        c1 = jnp.sum(ds1[...], axis=-1, keepdims=True)
        dck_ref[0] = jnp.where(lane == 0, c0, jnp.where(lane == 1, c1, 0.0))

    res_t = lambda: pl.BlockSpec((128, s), lambda hp, j: (hp, 0))
    blk = lambda: pl.BlockSpec((t, 128), lambda hp, j: (j, hp))
    return pl.pallas_call(
        body, name="fox_bwd", grid=(4, nq),
        in_specs=[res_t(), blk(), blk(), res_t(), res_t(), pl.BlockSpec((1, s, 128), lambda hp, j: (hp, 0, 0)),
                  pl.BlockSpec((1, 8, s), lambda hp, j: (hp, 0, 0))],
        out_specs=[res_t(), blk(), blk(),
                   pl.BlockSpec((1, t, 128), lambda hp, j: (hp, j, 0)),
                   pl.BlockSpec((1, 8, s), lambda hp, j: (hp, 0, 0))],
        out_shape=[jax.ShapeDtypeStruct((512, s), F32), jax.ShapeDtypeStruct((s, 512), BF16),
                   jax.ShapeDtypeStruct((s, 512), BF16), jax.ShapeDtypeStruct((4, s, 128), F32),
                   jax.ShapeDtypeStruct((4, 8, s), F32)],
        scratch_shapes=[pltpu.VMEM((8, s), F32)] + [pltpu.VMEM((128, t), F32)] * 2 + [pltpu.VMEM((t, t), F32)] * 2,
        compiler_params=_params(("arbitrary", "arbitrary")),
    )(qt, k, v, dot, ot, cc4, lse4)


def _mem_bwd(qm, mk, mv, dom):
    s = qm.shape[0]
    tq = min(512, s)

    def body(q_ref, mk_ref, mv_ref, do_ref, dq_ref, dmk_ref, dmv_ref):
        i = pl.program_id(0)

        @pl.when(i == 0)
        def _():
            dmk_ref[...] = jnp.zeros_like(dmk_ref)
            dmv_ref[...] = jnp.zeros_like(dmv_ref)

        heads = range(MEM_HEADS)
        hs = lambda h: slice(h * 128, (h + 1) * 128)
        sc = [_dot(q_ref[:, hs(h)], mk_ref[:, hs(h)], NT) * MEM_SCALE for h in heads]
        dp = [_dot(do_ref[:, hs(h)], mv_ref[:, hs(h)], NT) for h in heads]
        pb, dsb = [], []
        for h in heads:
            p = jnp.exp(sc[h] - jnp.max(sc[h], axis=-1, keepdims=True))
            p = p / jnp.sum(p, axis=-1, keepdims=True)
            ds = p * (dp[h] - jnp.sum(p * dp[h], axis=-1, keepdims=True))
            pb.append(p.astype(BF16))
            dsb.append((ds * MEM_SCALE).astype(BF16))
        dq = [_dot(dsb[h], mk_ref[:, hs(h)]).astype(BF16) for h in heads]
        dmk = [_dot(dsb[h], q_ref[:, hs(h)], TN) for h in heads]
        dmv = [_dot(pb[h], do_ref[:, hs(h)], TN) for h in heads]
        for h in heads:
            dq_ref[:, hs(h)] = dq[h]
            dmk_ref[:, hs(h)] += dmk[h]
            dmv_ref[:, hs(h)] += dmv[h]

    return pl.pallas_call(
        body, name="mem_bwd", grid=(s // tq,),
        in_specs=[pl.BlockSpec((tq, 512), lambda i: (i, 0)), _full(mk.shape), _full(mv.shape),
                  pl.BlockSpec((tq, 512), lambda i: (i, 0))],
        out_specs=[pl.BlockSpec((tq, 512), lambda i: (i, 0)), _full(mk.shape), _full(mv.shape)],
        out_shape=[jax.ShapeDtypeStruct((s, 512), BF16), jax.ShapeDtypeStruct(mk.shape, F32),
                   jax.ShapeDtypeStruct(mv.shape, F32)],
        compiler_params=_params(("arbitrary",)),
    )(qm, mk, mv, dom)


def _memkv_bwd(dmk, dmv, kv_raw, kn_mem, mem, g_mem, mem_n, w_kv):
    def body(dmk_ref, dmv_ref, kv_ref, kn_ref, mem_ref, g_ref, mn_ref, w_ref, dw_ref, dkn_ref, dg_ref, dkv_ref):
        dkn = jnp.zeros((1, 128), F32)
        for h in range(MEM_HEADS):
            hs = slice(h * 128, (h + 1) * 128)
            v = kv_ref[:, hs]
            r = lax.rsqrt(jnp.mean(v * v, axis=-1, keepdims=True) + EPS)
            n = v * r
            dn = dmk_ref[:, hs]
            dkn = dkn + jnp.sum(dn * n, axis=0, keepdims=True)
            dng = dn * kn_ref[...]
            dkv_ref[:, hs] = (r * (dng - n * jnp.mean(dng * n, axis=-1, keepdims=True))).astype(BF16)
        dkv_ref[:, 512:1024] = dmv_ref[...].astype(BF16)
        dkn_ref[...] = dkn
        dkv = dkv_ref[...]
        dw_ref[...] = _dot(mn_ref[...], dkv, TN).astype(BF16)
        dmn = _dot(dkv, w_ref[...], NT)
        xv = mem_ref[...]
        r = lax.rsqrt(jnp.mean(xv * xv, axis=-1, keepdims=True) + EPS)
        dg_ref[...] = jnp.sum(dmn * (xv * r), axis=0, keepdims=True)

    m = mem.shape[0]
    return pl.pallas_call(
        body, name="memkv_bwd",
        out_shape=[jax.ShapeDtypeStruct((D_MODEL, 1024), BF16), jax.ShapeDtypeStruct((1, 128), F32),
                   jax.ShapeDtypeStruct((1, D_MODEL), F32)],
        scratch_shapes=[pltpu.VMEM((m, 1024), BF16)],
        compiler_params=pltpu.CompilerParams(vmem_limit_bytes=VMEM_LIMIT),
    )(dmk, dmv, kv_raw, kn_mem, mem, g_mem, mem_n, w_kv)


def _fox_gate_bwd(dcq4, dck4, proj, b_forget128):
    s = dck4.shape[1]
    tm = min(512, s)
    nt = s // tm

    def body(dcq_ref, dck_ref, p_ref, b_ref, dfl_ref, db_ref, carry_ref):
        i = pl.program_id(0)

        @pl.when(i == 0)
        def _():
            carry_ref[...] = jnp.zeros_like(carry_ref)
            db_ref[...] = jnp.zeros_like(db_ref)

        dcv = jnp.zeros((tm, 128), F32)
        for hp in range(4):
            by_query = jnp.concatenate([dcq_ref[hp], jnp.zeros((120, tm), F32)], axis=0).T
            d = by_query - dck_ref[hp]
            dcv = dcv + (d if hp == 0 else pltpu.roll(d, 2 * hp, 1))
        dlogf = jnp.dot(_tri(tm, False), dcv, precision=lax.Precision.HIGHEST, preferred_element_type=F32) + carry_ref[...]
        carry_ref[...] += jnp.sum(dcv, axis=0, keepdims=True)
        z = p_ref[...] + b_ref[...]
        dfl = dlogf * (1.0 / (1.0 + jnp.exp(z)))
        dfl_ref[...] = dfl.astype(BF16)
        db_ref[...] += jnp.sum(dfl, axis=0, keepdims=True)

    return pl.pallas_call(
        body, name="fox_gate_bwd", grid=(nt,),
        in_specs=[pl.BlockSpec((4, 8, tm), lambda i: (0, 0, nt - 1 - i)),
                  pl.BlockSpec((4, tm, 128), lambda i: (0, nt - 1 - i, 0)),
                  pl.BlockSpec((tm, 128), lambda i: (nt - 1 - i, 0)), _full((1, 128))],
        out_specs=[pl.BlockSpec((tm, 128), lambda i: (nt - 1 - i, 0)), _full((1, 128))],
        out_shape=[jax.ShapeDtypeStruct((s, 128), BF16), jax.ShapeDtypeStruct((1, 128), F32)],
        scratch_shapes=[pltpu.VMEM((1, 128), F32)],
        compiler_params=_params(("arbitrary",)),
    )(dcq4, dck4, proj, b_forget128)


def _proj_pre_bwd(dproj, proj, dqf, dkf, dvf, dqm, dqa, dka, dva, dfl, gq_fox, gk_fox, gq_mem, gq_swa, gk_swa):
    s = proj.shape[0]
    tm = min(512, s)

    def body(dp_in, p_ref, dqf_ref, dkf_ref, dvf_ref, dqm_ref, dqa_ref, dka_ref, dva_ref, dfl_ref,
             gqf, gkf, gqm, gqa, gka, dp_ref, dgn_ref):
        i = pl.program_id(0)

        @pl.when(i == 0)
        def _():
            dgn_ref[...] = jnp.zeros_like(dgn_ref)

        def norm_bwd(off, width, hd, g_ref, dn_ref, slot):
            acc = jnp.zeros((1, 128), F32)
            for b in range(width // 128):
                v = p_ref[:, off + b * 128: off + (b + 1) * 128].astype(F32)
                r = lax.rsqrt(_group_mean(v * v, hd) + EPS)
                n = v * r
                dn = dn_ref[b * 128:(b + 1) * 128, :].T if slot == 0 else dn_ref[:, b * 128:(b + 1) * 128].astype(F32)
                acc = acc + jnp.sum(dn * n, axis=0, keepdims=True)
                dng = dn * g_ref[...]
                dp_ref[:, off + b * 128: off + (b + 1) * 128] = (r * (dng - n * _group_mean(dng * n, hd))).astype(BF16)
            dgn_ref[slot:slot + 1, :] += acc

        norm_bwd(H_QF, 512, HEAD, gqf, dqf_ref, 0)
        norm_bwd(H_KF, 512, HEAD, gkf, dkf_ref, 1)
        dp_ref[:, H_VF:H_VF + 512] = dvf_ref[...].astype(BF16)
        norm_bwd(H_QM, 512, MEM_HEAD, gqm, dqm_ref, 2)
        norm_bwd(H_QA, 512, HEAD, gqa, dqa_ref, 3)
        norm_bwd(H_KA, 128, HEAD, gka, dka_ref, 4)
        dp_ref[:, H_VA:H_VA + 128] = dva_ref[...].astype(BF16)
        dp_ref[:, H_FL:H_FL + 128] = dfl_ref[...]
        dp_ref[:, H_FL + 128:HALF_W] = jnp.zeros((tm, HALF_W - H_FL - 128), BF16)

    row = lambda w: pl.BlockSpec((tm, w), lambda i: (i, 0))
    g_spec = _full((1, 128))
    return pl.pallas_call(
        body, name="proj_pre_bwd", grid=(s // tm,),
        in_specs=[pl.BlockSpec(memory_space=pl.ANY), pl.BlockSpec((tm, HALF_W), lambda i: (i, 1)),
                  pl.BlockSpec((512, tm), lambda i: (0, i)), row(512), row(512), row(512), row(512),
                  row(128), row(128), row(128), g_spec, g_spec, g_spec, g_spec, g_spec],
        out_specs=[pl.BlockSpec((tm, HALF_W), lambda i: (i, 1)), _full((8, 128))],
        out_shape=[jax.ShapeDtypeStruct((s, PROJ_W), BF16), jax.ShapeDtypeStruct((8, 128), F32)],
        input_output_aliases={0: 0},
        compiler_params=_params(("arbitrary",)),
    )(dproj, proj, dqf, dkf, dvf, dqm, dqa, dka, dva, dfl, gq_fox, gk_fox, gq_mem, gq_swa, gk_swa)


def _in_bwd_x(dproj, w_in_p, x, g_mix, dx1):
    s = x.shape[0]
    tm = min(256, s)

    def body(dp_ref, w_ref, x_ref, g_ref, dx1_ref, gx_ref, dg_ref):
        i = pl.program_id(0)

        @pl.when(i == 0)
        def _():
            dg_ref[...] = jnp.zeros_like(dg_ref)

        dx, dg = _rms_bwd(x_ref[...], g_ref[...], _dot(dp_ref[...], w_ref[...], NT), dx1_ref[...])
        gx_ref[...] = dx
        dg_ref[...] += dg

    row = pl.BlockSpec((tm, D_MODEL), lambda i: (i, 0))
    return pl.pallas_call(
        body, name="in_bwd_x", grid=(s // tm,),
        in_specs=[pl.BlockSpec((tm, PROJ_W), lambda i: (i, 0)), _full(w_in_p.shape), row, _full((1, D_MODEL)), row],
        out_specs=[row, _full((1, D_MODEL))],
        out_shape=[jax.ShapeDtypeStruct((s, D_MODEL), F32), jax.ShapeDtypeStruct((1, D_MODEL), F32)],
        compiler_params=_params(("arbitrary",)),
    )(dproj, w_in_p, x, g_mix, dx1)


def _rel_bias_bwd(dbias, bucket):
    def body(db_ref, bk_ref, o_ref):
        bk = bk_ref[...]
        lane = lax.broadcasted_iota(jnp.int32, (1, 128), 1)
        for b in range(REL_BUCKETS):
            sel = bk == b
            acc = jnp.zeros((1, 128), F32)
            for h in range(SWA_HEADS):
                tot = jnp.sum(jnp.sum(jnp.where(sel, db_ref[h], 0.0), axis=0, keepdims=True), axis=-1, keepdims=True)
                acc = jnp.where(lane == h, tot, acc)
            o_ref[:, b * 128:(b + 1) * 128] = acc

    return pl.pallas_call(
        body, name="rel_bias_bwd",
        out_shape=jax.ShapeDtypeStruct((1, REL_BUCKETS * 128), F32),
        compiler_params=pltpu.CompilerParams(vmem_limit_bytes=VMEM_LIMIT),
    )(dbias, bucket)


def _my_place():
    return lax.axis_index("x"), lax.axis_index("y"), lax.axis_index("c")


def _peer(place, k):
    x, y, c = place
    return (1 - x if k & 4 else x, 1 - y if k & 2 else y, 1 - c if k & 1 else c)


def _index(place):
    x, y, c = place
    return 4 * x + 2 * y + c


HBM_SPEC = pl.BlockSpec(memory_space=pltpu.HBM)
SEM_SPEC = pl.BlockSpec(memory_space=pltpu.SEMAPHORE)
DATAFLOW = pltpu.SideEffectType.DATAFLOW_SIDE_EFFECTING


ALL_PEERS = tuple(range(1, N_DEV))
SAME_CORE = (2, 4, 6)
OWN = N_DEV - 1


def _split_copy(src_ref, land_ref, send_sems, recv_sems, me, k, gather):
    peer = _peer(me, k)
    if gather:
        src, dst = src_ref, land_ref.at[_index(me)]
    else:
        src, dst = src_ref.at[_index(peer)], land_ref.at[k - 1]
    return pltpu.make_async_remote_copy(src_ref=src, dst_ref=dst, send_sem=send_sems.at[k - 1], recv_sem=recv_sems.at[k - 1],
                                        device_id=peer, device_id_type=MESH)


def _own_copy(src_ref, land_ref, recv_sems, me, gather):
    if gather:
        src, dst = src_ref, land_ref.at[_index(me)]
    else:
        src, dst = src_ref.at[_index(me)], land_ref.at[OWN]
    return pltpu.make_async_copy(src, dst, recv_sems.at[OWN])


def _split_start(srcs, gather, name, peers=ALL_PEERS, after=None):
    n = len(srcs)
    extra = [] if after is None else [after]

    def body(*refs):
        refs = refs[:2 * n] + refs[2 * n + len(extra):]
        src_refs, land_refs = refs[:n], refs[n:2 * n]
        send_sems, recv_sems, token = refs[2 * n:3 * n], refs[3 * n:4 * n], refs[-1]
        me = _my_place()
        for w in range(n):
            for k in peers:
                _split_copy(src_refs[w], land_refs[w], send_sems[w], recv_sems[w], me, k, gather).start()
            _own_copy(src_refs[w], land_refs[w], recv_sems[w], me, gather).start()
        token[...] = jnp.zeros_like(token)

    lands = [lax.empty((N_DEV,) + (a.shape if gather else a.shape[1:]), a.dtype) for a in srcs]
    sems = [pltpu.SemaphoreType.DMA((N_DEV,))] * (2 * n)
    hbm = [pltpu.HBM(a.shape, a.dtype) for a in list(srcs) + lands]
    outs = pl.pallas_call(
        body, name=name,
        out_shape=(*sems, *hbm, jax.ShapeDtypeStruct((8, 128), F32)),
        in_specs=(HBM_SPEC,) * (2 * n) + (pl.BlockSpec(memory_space=pl.ANY),) * len(extra),
        out_specs=(SEM_SPEC,) * (2 * n) + (HBM_SPEC,) * (2 * n) + (pl.BlockSpec(memory_space=pltpu.VMEM),),
        input_output_aliases={i: 2 * n + i for i in range(2 * n)},
        compiler_params=pltpu.CompilerParams(has_side_effects=DATAFLOW),
    )(*[pltpu.with_memory_space_constraint(a, pltpu.HBM) for a in list(srcs) + lands], *extra)
    return list(outs[:n]), list(outs[n:2 * n]), list(outs[2 * n:3 * n]), list(outs[3 * n:4 * n]), outs[-1]


def _split_wait(started, w, after, gather, name):
    send_sems, recv_sems, srcs, lands, _ = started

    def body(src_ref, land_ref, send_sems, recv_sems, after_ref, src_out, land_out):
        me = _my_place()
        for k in ALL_PEERS:
            cp = _split_copy(src_ref, land_ref, send_sems, recv_sems, me, k, gather)
            cp.wait_send()
            cp.wait_recv()
        _own_copy(src_ref, land_ref, recv_sems, me, gather).wait()

    return pl.pallas_call(
        body, name=name,
        out_shape=(pltpu.HBM(srcs[w].shape, srcs[w].dtype), pltpu.HBM(lands[w].shape, lands[w].dtype)),
        in_specs=(HBM_SPEC, HBM_SPEC, SEM_SPEC, SEM_SPEC, pl.BlockSpec(memory_space=pl.ANY)),
        out_specs=(HBM_SPEC, HBM_SPEC), input_output_aliases={0: 0, 1: 1},
        compiler_params=pltpu.CompilerParams(has_side_effects=DATAFLOW),
    )(srcs[w], lands[w], send_sems[w], recv_sems[w], after)[1]


def _forward_copy(land_ref, send_sems, recv_sems, me, j, incoming):
    sibling = _peer(me, 1)
    rows = land_ref.at[_index(_peer(sibling if incoming else me, SAME_CORE[j]))]
    return pltpu.make_async_remote_copy(src_ref=rows, dst_ref=rows, send_sem=send_sems.at[j], recv_sem=recv_sems.at[j],
                                        device_id=sibling, device_id_type=MESH)


def _forward_start(started, after, name):
    send_a, recv_a, srcs, lands, _ = started

    def body(src_ref, land_ref, send_a, recv_a, after_ref, send_b, recv_b, src_out, land_out):
        me = _my_place()
        for j, k in enumerate(SAME_CORE):
            _split_copy(src_ref, land_ref, send_a, recv_a, me, k, True).wait_recv()
            _forward_copy(land_ref, send_b, recv_b, me, j, False).start()

    sems = pltpu.SemaphoreType.DMA((len(SAME_CORE),))
    return pl.pallas_call(
        body, name=name,
        out_shape=(sems, sems, pltpu.HBM(srcs[0].shape, srcs[0].dtype), pltpu.HBM(lands[0].shape, lands[0].dtype)),
        in_specs=(HBM_SPEC, HBM_SPEC, SEM_SPEC, SEM_SPEC, pl.BlockSpec(memory_space=pl.ANY)),
        out_specs=(SEM_SPEC, SEM_SPEC, HBM_SPEC, HBM_SPEC), input_output_aliases={0: 2, 1: 3},
        compiler_params=pltpu.CompilerParams(has_side_effects=DATAFLOW),
    )(srcs[0], lands[0], send_a[0], recv_a[0], after)


def _forward_wait(started, forwarded, name):
    send_a, recv_a, _, _, _ = started
    send_b, recv_b, src, land = forwarded

    def body(src_ref, land_ref, send_a, recv_a, send_b, recv_b, src_out, land_out):
        me = _my_place()
        _own_copy(src_ref, land_ref, recv_a, me, True).wait()
        for k in (1,) + SAME_CORE:
            _split_copy(src_ref, land_ref, send_a, recv_a, me, k, True).wait_send()
        _split_copy(src_ref, land_ref, send_a, recv_a, me, 1, True).wait_recv()
        for j in range(len(SAME_CORE)):
            _forward_copy(land_ref, send_b, recv_b, me, j, False).wait_send()
            _forward_copy(land_ref, send_b, recv_b, me, j, True).wait_recv()

    return pl.pallas_call(
        body, name=name,
        out_shape=(pltpu.HBM(src.shape, src.dtype), pltpu.HBM(land.shape, land.dtype)),
        in_specs=(HBM_SPEC, HBM_SPEC, SEM_SPEC, SEM_SPEC, SEM_SPEC, SEM_SPEC),
        out_specs=(HBM_SPEC, HBM_SPEC), input_output_aliases={0: 0, 1: 1},
        compiler_params=pltpu.CompilerParams(has_side_effects=DATAFLOW),
    )(src, land, send_a[0], recv_a[0], send_b, recv_b)[1]


def _adam_math(w, g, m, v):
    m2 = ADAM_B1 * m + (1.0 - ADAM_B1) * g
    v2 = ADAM_B2 * v + (1.0 - ADAM_B2) * (g * g)
    m_hat = m2 / (1.0 - ADAM_B1 ** ADAM_STEP)
    v_hat = v2 / (1.0 - ADAM_B2 ** ADAM_STEP)
    delta = -ADAM_LR * (m_hat / (jnp.sqrt(v_hat) + ADAM_EPS) + ADAM_WD * w)
    return delta, m2, v2


def _adamw(land, w, m, v, name):
    a, b = w.shape
    bp = land.shape[2]
    ta = min(128, a)

    def body(p_ref, w_ref, m_ref, v_ref, g_ref, d_ref, m2_ref, v2_ref):
        g = p_ref[0, :, 0:b].astype(F32)
        for k in range(1, N_DEV):
            g = g + p_ref[k, :, 0:b].astype(F32)
        delta, m2, v2 = _adam_math(w_ref[...], g, m_ref[...], v_ref[...])
        g_ref[...] = g
        d_ref[...] = delta
        m2_ref[...] = m2
        v2_ref[...] = v2

    blk = pl.BlockSpec((ta, b), lambda i: (i, 0))
    sd = jax.ShapeDtypeStruct((a, b), F32)
    return pl.pallas_call(
        body, name=name, grid=(a // ta,),
        in_specs=[pl.BlockSpec((N_DEV, ta, bp), lambda i: (0, i, 0)), blk, blk, blk],
        out_specs=[blk, blk, blk, blk], out_shape=[sd, sd, sd, sd],
        compiler_params=_params(("parallel",)),
    )(land, w, m, v)


def _bucket_table():
    t_loc = jnp.arange(SWA_BLOCK)[:, None] + SWA_BLOCK
    s_loc = jnp.arange(2 * SWA_BLOCK)[None, :]
    dist = t_loc - s_loc
    max_exact = REL_BUCKETS // 2
    d = jnp.maximum(dist, 0)
    df = jnp.maximum(d, 1).astype(F32)
    large = max_exact + (jnp.log(df / max_exact) / math.log(REL_MAX_DIST / max_exact) * (REL_BUCKETS - max_exact)).astype(jnp.int32)
    large = jnp.minimum(large, REL_BUCKETS - 1)
    bucket = jnp.where(d < max_exact, d, large)
    band = (dist >= 0) & (dist < SWA_BLOCK)
    return bucket, band


def _tile2(g):
    return jnp.concatenate([g, g], axis=1) if g.shape[1] == HEAD else g


SHARD_W = 737
SHARD_WP = 768
IN_WIDTH = N_DEV * SHARD_W
SEGMENTS = ((GL0, 2824, 3072), (QF0, 768, 512), (KF0, 1280, 512), (VF0, 1792, 512), (QM0, 2312, 512),
            (QA0, 0, 512), (KA0, 512, 128), (VA0, 640, 128), (FL0, 2304, 8))


def _lane_plan(sources):
    plan = []
    for t in range(len(sources) // 128):
        groups = {}
        for lane in range(128):
            src = sources[128 * t + lane]
            if src is not None:
                slab, col = src
                groups.setdefault((slab, col // 128, (lane - col) % 128), []).append(lane)
        tile = []
        for key, lanes in groups.items():
            assert lanes == list(range(lanes[0], lanes[-1] + 1))
            tile.append((key, lanes[0], lanes[-1] + 1))
        plan.append(tile)
    return plan


def _assemble(tile_plan, load, rows):
    lane = lax.broadcasted_iota(jnp.int32, (1, 128), 1)
    out = jnp.zeros((rows, 128), F32)
    for (slab, st, roll), lo, hi in tile_plan:
        v = load(slab, st)
        if roll:
            v = pltpu.roll(v, roll, 1)
        out = v if (lo, hi) == (0, 128) else jnp.where((lane >= lo) & (lane < hi), v, out)
    return out


def _w_in_from_shards(land):
    ref_col = [None] * PROJ_W
    for p0, r0, n in SEGMENTS:
        for i in range(n):
            ref_col[p0 + i] = divmod(r0 + i, SHARD_W)
    plan = _lane_plan(ref_col)
    d_model = land.shape[1]
    tm = 256

    def body(land_ref, o_ref):
        load = lambda slab, st: land_ref[slab, :, st * 128:(st + 1) * 128].astype(F32)
        for t, tile_plan in enumerate(plan):
            o_ref[:, t * 128:(t + 1) * 128] = _assemble(tile_plan, load, tm).astype(BF16)

    return pl.pallas_call(
        body, name="w_in_from_shards", grid=(d_model // tm,),
        in_specs=[pl.BlockSpec((N_DEV, tm, SHARD_WP), lambda i: (0, i, 0))],
        out_specs=pl.BlockSpec((tm, PROJ_W), lambda i: (i, 0)),
        out_shape=jax.ShapeDtypeStruct((d_model, PROJ_W), BF16),
        compiler_params=_params(("parallel",)),
    )(land)


def _dw_in_to_parts(dwp):
    padded_col = [None] * IN_WIDTH
    for p0, r0, n in SEGMENTS:
        for i in range(n):
            padded_col[r0 + i] = p0 + i
    sources = []
    for d in range(N_DEV):
        sources += [(0, padded_col[SHARD_W * d + c]) if c < SHARD_W else None for c in range(SHARD_WP)]
    plan = _lane_plan(sources)
    d_model = dwp.shape[0]
    tm = 256
    tiles = SHARD_WP // 128

    def body(dw_ref, o_ref):
        load = lambda slab, st: dw_ref[:, st * 128:(st + 1) * 128].astype(F32)
        for t, tile_plan in enumerate(plan):
            d, c = divmod(t, tiles)
            o_ref[d, :, c * 128:(c + 1) * 128] = _assemble(tile_plan, load, tm).astype(BF16)

    return pl.pallas_call(
        body, name="dw_in_to_parts", grid=(d_model // tm,),
        in_specs=[pl.BlockSpec((tm, PROJ_W), lambda i: (i, 0))],
        out_specs=pl.BlockSpec((N_DEV, tm, SHARD_WP), lambda i: (0, i, 0)),
        out_shape=jax.ShapeDtypeStruct((N_DEV, d_model, SHARD_WP), BF16),
        compiler_params=_params(("parallel",)),
    )(dwp)


def _cast_shards(shards):
    names = list(shards)

    def body(*refs):
        for src, dst in zip(refs[:len(names)], refs[len(names):]):
            if dst.shape != src.shape:
                dst[...] = jnp.zeros(dst.shape, BF16)
                dst[:, 0:src.shape[1]] = src[...].astype(BF16)
            else:
                dst[...] = src[...].astype(BF16)

    out_shape = [jax.ShapeDtypeStruct((shards[n].shape[0], SHARD_WP if n == "w_in" else shards[n].shape[1]), BF16)
                 for n in names]
    outs = pl.pallas_call(body, name="cast_shards", out_shape=out_shape,
                          compiler_params=pltpu.CompilerParams(vmem_limit_bytes=VMEM_LIMIT))(*[shards[n] for n in names])
    return dict(zip(names, outs))


def _tie(x, *tokens):
    for t in tokens:
        if t is not None:
            x = x + t[0:1, 0:1]
    return x


def _local_step(x, mem, target, p, getw, emit, deps=()):
    s = x.shape[0]
    bucket, band = _bucket_table()
    bucket_m = jnp.where(band, bucket, -1).astype(jnp.int32)
    bias = _bias_table(p["rel_bias"], bucket_m)
    bucket_t = jnp.transpose(bucket_m)
    bias_t = _bias_table(p["rel_bias"], bucket_t)
    gqf, gkf, gqa, gka = _tile2(p["qn_fox"]), _tile2(p["kn_fox"]), _tile2(p["qn_swa"]), _tile2(p["kn_swa"])
    gqm = p["qn_mem"]
    bf128 = jnp.pad(p["b_forget"], ((0, 0), (0, 120)))
    sink = p["sink_swa"].reshape(8)

    h = _rms_fwd(x, p["g_mix"], "rms_mix", tuple(deps) + (bias, bias_t))
    w_in = getw("w_in", h)
    proj = _mm(h, w_in, "nn", BF16, 512, 1536, 1024, "proj")
    fl = _mm(h, w_in[:, FL0:FL0 + 128], "nn", F32, 512, 128, 1024, "proj_fl")
    qf, kf, vf, qm, qa, ka, va, qf_t, vf_t = _proj_post(proj, gqf, gkf, gqm, gqa, gka)
    cc4, ca4 = _fox_gate_fwd(fl, bf128)
    w_kv = getw("w_mem_kv", cc4)
    mem_n, kv_raw, mk, mv = _memkv_fwd(mem, p["g_mem"], w_kv, p["kn_mem"])
    kp = jnp.pad(ka, ((SWA_BLOCK, 0), (0, 0)))
    vp = jnp.pad(va, ((SWA_BLOCK, 0), (0, 0)))
    oa = _swa_fwd(qa, kp, vp, bias, sink)
    of, lse4, of_t = _fox_fwd(qf, kf, vf_t, ca4)
    om = _mem_fwd(qm, mk, mv)
    wa, wf, wm, w_out = getw("w_o_swa", oa), getw("w_o_fox", oa), getw("w_o_mem", oa), getw("w_out", oa)
    x1, hm, merged = _merge_fwd(x, oa, of, om, proj, p["b_gate"], wa, wf, wm, w_out, p["g_mlp"])
    w_up = getw("w_mlp_up", of)
    u = _mlp_up(hm, w_up)
    w_down = getw("w_mlp_down", hm)
    dy, dy_b, loss = _mlp_down_loss(u, w_down, x1, target)

    da = _mlp_bwd_act(dy_b, w_down, u)
    t_down = emit({"w_mlp_down": _mm(u, dy_b, "tn", BF16, 1024, 1024, 2048, "dw_down")})
    dx1, dg_mlp = _mlp_bwd_x(da, w_up, x1, dy, _tie(p["g_mlp"], t_down))
    t_up = emit({"w_mlp_up": _mm(hm, da, "tn", BF16, 1024, 1024, 2048, "dw_up", column_chunks=True)})
    dproj, doa, dof_t, dom, dya, dyf, dym, db_gate = _merge_bwd(
        dx1, oa, of, om, proj, _tie(p["b_gate"], t_up), wa, wf, wm, w_out)
    dw_oa, dw_of, dw_om = _mm_tn3([oa, of, om], [dya, dyf, dym], "dw_o")
    t_o = emit({"w_out": _mm(merged, dx1, "tn", BF16, 1024, 1024, 2048, "dw_out"),
                "w_o_swa": dw_oa, "w_o_fox": dw_of, "w_o_mem": dw_om})

    dqm, dmk, dmv = _mem_bwd(qm, mk, mv, dom)
    dw_kv, dkn_mem, dg_mem = _memkv_bwd(dmk, dmv, kv_raw, _tie(p["kn_mem"], t_o), mem, p["g_mem"], mem_n, w_kv)
    t_kv = emit({"w_mem_kv": dw_kv})
    dqa, dkp, dvp, dbias, dsink = _swa_bwd(qa, kp, vp, bias_t, _tie(p["sink_swa"], t_kv).reshape(8), doa)
    dqf_t, dkf, dvf, dck4, dcq4 = _fox_bwd(qf_t, kf, vf, dof_t, of_t, cc4, lse4)

    dfl, db_forget = _fox_gate_bwd(dcq4, dck4, fl, bf128)

    dproj, dgn = _proj_pre_bwd(dproj, proj, dqf_t, dkf, dvf, dqm, dqa, dkp[SWA_BLOCK:], dvp[SWA_BLOCK:], dfl,
                               gqf, gkf, gqm, gqa, gka)
    t_in = emit({"w_in": _mm(h, dproj, "tn", BF16, 1024, 3072, 1024, "dw_in")})
    grad_x, dg_mix = _in_bwd_x(dproj, w_in, x, _tie(p["g_mix"], t_in), dx1)
    d_rel = _rel_bias_bwd(dbias, bucket_t)

    fold = lambda r: dgn[r:r + 1, 0:HEAD] + dgn[r:r + 1, HEAD:128]
    small = {
        "g_mix": dg_mix, "b_gate": db_gate, "b_forget": db_forget[:, 0:8],
        "qn_swa": fold(3), "kn_swa": fold(4), "sink_swa": dsink[:, 0].reshape(1, 8), "rel_bias": d_rel,
        "qn_fox": fold(0), "kn_fox": fold(1), "g_mem": dg_mem, "qn_mem": dgn[2:3, :], "kn_mem": dkn_mem,
        "g_mlp": dg_mlp,
    }
    return loss, grad_x, small


SMALL = ("g_mix", "b_gate", "b_forget", "qn_swa", "kn_swa", "sink_swa", "rel_bias", "qn_fox", "kn_fox", "g_mem",
         "qn_mem", "kn_mem", "g_mlp")
BIG = ("w_in", "w_mem_kv", "w_o_swa", "w_o_fox", "w_o_mem", "w_out", "w_mlp_up", "w_mlp_down")
COL_SHARDED = ("w_in", "w_o_swa", "w_o_fox", "w_o_mem", "w_mlp_up")
WEIGHTS = ("g_mix", "w_in", "b_gate", "b_forget", "qn_swa", "kn_swa", "sink_swa", "rel_bias", "qn_fox", "kn_fox", "g_mem",
           "w_mem_kv", "qn_mem", "kn_mem", "w_o_swa", "w_o_fox", "w_o_mem", "w_out", "g_mlp", "w_mlp_up", "w_mlp_down")
SMALL_SLOTS = (("g_mix", 1024), ("b_gate", 3072), ("b_forget", 128), ("qn_swa", 128), ("kn_swa", 128), ("sink_swa", 128),
               ("rel_bias", REL_BUCKETS * 128), ("qn_fox", 128), ("kn_fox", 128), ("g_mem", 1024), ("qn_mem", 128),
               ("kn_mem", 128), ("g_mlp", 1024), ("loss", 128))
SMALL_OFF = {n: sum(w for _, w in SMALL_SLOTS[:i]) for i, (n, _) in enumerate(SMALL_SLOTS)}
SMALL_ROW = sum(w for _, w in SMALL_SLOTS)


def _gathered_to_full(name, g):
    if name in COL_SHARDED:
        return jnp.transpose(g, (1, 0, 2)).reshape(g.shape[1], N_DEV * g.shape[2])
    return g.reshape(N_DEV * g.shape[1], g.shape[2])


def _full_to_parts(name, full, b):
    if name in COL_SHARDED:
        return jnp.transpose(full.reshape(full.shape[0], N_DEV, b), (1, 0, 2)).astype(BF16)
    return full.reshape(N_DEV, full.shape[0] // N_DEV, full.shape[1]).astype(BF16)


def _pack_small(grads, loss):
    pieces = []
    for n, width in SMALL_SLOTS:
        a = loss.reshape(1, 1) if n == "loss" else grads[n].reshape(1, -1)
        pieces.append(jnp.pad(a, ((0, 0), (0, width - a.shape[1]))))
    return jnp.concatenate(pieces, axis=1)


def _adamw_small(gathered, w, m, v):
    names = list(SMALL)

    def body(*refs):
        p_ref = refs[0]
        ins = refs[1:1 + 3 * len(names)]
        outs = refs[1 + 3 * len(names):]
        g_all = p_ref[0]
        for k in range(1, N_DEV):
            g_all = g_all + p_ref[k]
        for i, n in enumerate(names):
            w_ref, m_ref, v_ref = ins[3 * i:3 * i + 3]
            out = outs[4 * i:4 * i + 4]
            rows, cols = w_ref.shape
            for r in range(rows):
                off = SMALL_OFF[n] + 128 * r
                g = g_all[:, off:off + cols]
                rs = slice(r, r + 1)
                res = (g,) + _adam_math(w_ref[rs, :], g, m_ref[rs, :], v_ref[rs, :])
                for o_ref, val in zip(out, res):
                    o_ref[rs, :] = val
        outs[-1][...] = g_all[:, SMALL_OFF["loss"]:SMALL_OFF["loss"] + 128]

    args = [gathered]
    out_shape = []
    for n in names:
        args += [w[n], m[n], v[n]]
        out_shape += [jax.ShapeDtypeStruct(w[n].shape, F32)] * 4
    out_shape.append(jax.ShapeDtypeStruct((1, 128), F32))
    outs = pl.pallas_call(body, name="adamw_small", out_shape=out_shape)(*args)
    return {n: outs[4 * i:4 * i + 4] for i, n in enumerate(names)}, outs[-1]


def kernel(x, mem, g_mix, w_in, b_gate, b_forget, qn_swa, kn_swa, sink_swa, rel_bias, qn_fox, kn_fox, g_mem, w_mem_kv, qn_mem, kn_mem, w_o_swa, w_o_fox, w_o_mem, w_out, g_mlp, w_mlp_up, w_mlp_down, loss_target, m_g_mix, m_w_in, m_b_gate, m_b_forget, m_qn_swa, m_kn_swa, m_sink_swa, m_rel_bias, m_qn_fox, m_kn_fox, m_g_mem, m_w_mem_kv, m_qn_mem, m_kn_mem, m_w_o_swa, m_w_o_fox, m_w_o_mem, m_w_out, m_g_mlp, m_w_mlp_up, m_w_mlp_down, v_g_mix, v_w_in, v_b_gate, v_b_forget, v_qn_swa, v_kn_swa, v_sink_swa, v_rel_bias, v_qn_fox, v_kn_fox, v_g_mem, v_w_mem_kv, v_qn_mem, v_kn_mem, v_w_o_swa, v_w_o_fox, v_w_o_mem, v_w_out, v_g_mlp, v_w_mlp_up, v_w_mlp_down):
    wts = dict(g_mix=g_mix, w_in=w_in, b_gate=b_gate, b_forget=b_forget, qn_swa=qn_swa, kn_swa=kn_swa, sink_swa=sink_swa,
               rel_bias=rel_bias, qn_fox=qn_fox, kn_fox=kn_fox, g_mem=g_mem, w_mem_kv=w_mem_kv, qn_mem=qn_mem, kn_mem=kn_mem,
               w_o_swa=w_o_swa, w_o_fox=w_o_fox, w_o_mem=w_o_mem, w_out=w_out, g_mlp=g_mlp, w_mlp_up=w_mlp_up,
               w_mlp_down=w_mlp_down)
    mom = dict(g_mix=m_g_mix, w_in=m_w_in, b_gate=m_b_gate, b_forget=m_b_forget, qn_swa=m_qn_swa, kn_swa=m_kn_swa,
               sink_swa=m_sink_swa, rel_bias=m_rel_bias, qn_fox=m_qn_fox, kn_fox=m_kn_fox, g_mem=m_g_mem, w_mem_kv=m_w_mem_kv,
               qn_mem=m_qn_mem, kn_mem=m_kn_mem, w_o_swa=m_w_o_swa, w_o_fox=m_w_o_fox, w_o_mem=m_w_o_mem, w_out=m_w_out,
               g_mlp=m_g_mlp, w_mlp_up=m_w_mlp_up, w_mlp_down=m_w_mlp_down)
    var = dict(g_mix=v_g_mix, w_in=v_w_in, b_gate=v_b_gate, b_forget=v_b_forget, qn_swa=v_qn_swa, kn_swa=v_kn_swa,
               sink_swa=v_sink_swa, rel_bias=v_rel_bias, qn_fox=v_qn_fox, kn_fox=v_kn_fox, g_mem=v_g_mem, w_mem_kv=v_w_mem_kv,
               qn_mem=v_qn_mem, kn_mem=v_kn_mem, w_o_swa=v_w_o_swa, w_o_fox=v_w_o_fox, w_o_mem=v_w_o_mem, w_out=v_w_out,
               g_mlp=v_g_mlp, w_mlp_up=v_w_mlp_up, w_mlp_down=v_w_mlp_down)

    shards = _cast_shards({n: wts[n][0] for n in BIG})
    first = _split_start([shards["w_in"]], True, "ag_start_w_in", peers=(1,) + SAME_CORE)
    rest = _split_start([shards[n] for n in BIG[1:]], True, "ag_start_rest", after=first[4])
    full = {}

    def getw(n, after):
        if n == "w_in" and n not in full:
            forwarded = _forward_start(first, after, "ag_forward_w_in")
            full[n] = _w_in_from_shards(_forward_wait(first, forwarded, "ag_wait_w_in"))
        elif n not in full:
            land = _split_wait(rest, BIG[1:].index(n), after, True, "ag_wait_" + n)
            full[n] = land if n == "w_mlp_up" else _gathered_to_full(n, land)
        return full[n]

    exchanges = {}

    def emit(grads_by_name):
        parts = []
        for n, grad in grads_by_name.items():
            if n == "w_in":
                parts.append(_dw_in_to_parts(grad))
            else:
                parts.append(grad if n == "w_mlp_up" else _full_to_parts(n, grad, wts[n].shape[2]))
        started = _split_start(parts, False, "rs_start_" + next(iter(grads_by_name)))
        for w, n in enumerate(grads_by_name):
            exchanges[n] = (started, w)
        return started[4]

    small_p = {n: wts[n] for n in SMALL}
    loss, grad_x, small_g = _local_step(x[0], mem[0], loss_target[0], small_p, getw, emit, (first[4], rest[4]))

    packed = _pack_small(small_g, loss)
    small_gather = _split_start([packed], True, "ag_start_small")

    grads, delta, new_m, new_v = {}, {}, {}, {}

    def update(n, after):
        land = _split_wait(*exchanges[n], after, False, "rs_wait_" + n)
        g, d, m2, v2 = _adamw(land, wts[n][0], mom[n][0], var[n][0], "adamw_" + n)
        grads[n], delta[n], new_m[n], new_v[n] = g[None], d[None], m2[None], v2[None]
        return d

    after = small_gather[4]
    for n in exchanges:
        if n != "w_in":
            after = update(n, after)

    gathered = _split_wait(small_gather, 0, after, True, "ag_wait_small")
    small_out, total = _adamw_small(gathered, small_p, mom, var)
    for name, (g, d, m2, v2) in small_out.items():
        grads[name], delta[name], new_m[name], new_v[name] = g, d, m2, v2
    update("w_in", total)

    return (total[0, 0], grad_x[None], *[grads[n] for n in WEIGHTS], *[delta[n] for n in WEIGHTS],
            *[new_m[n] for n in WEIGHTS], *[new_v[n] for n in WEIGHTS])
```

```python
import functools
import math

import jax
import jax.numpy as jnp
from jax import lax
from jax.experimental import pallas as pl
from jax.experimental.pallas import tpu as pltpu

F32 = jnp.float32
BF16 = jnp.bfloat16

D_MODEL = 1024
N_MEM = 256
D_FF = 4096
HEAD = 64
SWA_HEADS = 8
SWA_BLOCK = 128
MEM_HEADS = 4
MEM_HEAD = 128
EPS = 1e-6
NEG = -1e30
REL_BUCKETS = 32
REL_MAX_DIST = 128

ADAM_LR = 0.001
ADAM_B1 = 0.9
ADAM_B2 = 0.999
ADAM_EPS = 1e-08
ADAM_WD = 0.01
ADAM_STEP = 10

GL0, QF0, KF0, VF0, QM0, QA0, KA0, VA0, FL0 = 0, 3072, 3584, 4096, 4608, 5120, 5632, 5760, 5888
PROJ_W = 6144
HALF_W = 3072
H_QF, H_KF, H_VF, H_QM, H_QA, H_KA, H_VA, H_FL = 0, 512, 1024, 1536, 2048, 2560, 2688, 2816

VMEM_LIMIT = 56 * 1024 * 1024
N_DEV = 8
MESH = pl.DeviceIdType.MESH

NN = (((1,), (0,)), ((), ()))
NT = (((1,), (1,)), ((), ()))
TN = (((0,), (0,)), ((), ()))


def _dot(a, b, dims=NN):
    return lax.dot_general(a, b, dims, preferred_element_type=F32)


def _params(sem):
    return pltpu.CompilerParams(dimension_semantics=sem, vmem_limit_bytes=VMEM_LIMIT)


def _full(shape):
    nd = len(shape)
    return pl.BlockSpec(shape, lambda *_: (0,) * nd)


def _sigmoid(z):
    return 1.0 / (1.0 + jnp.exp(-z))


def _group_mean(v, hd):
    if hd == 128:
        return jnp.mean(v, axis=-1, keepdims=True)
    r = lax.broadcasted_iota(jnp.int32, (128, 128), 0) // HEAD
    c = lax.broadcasted_iota(jnp.int32, (128, 128), 1) // HEAD
    same_head = jnp.where(r == c, 1.0 / HEAD, 0.0).astype(BF16)
    total = None
    rest = v
    for _ in range(2):
        part = rest.astype(BF16)
        rest = rest - part.astype(F32)
        term = _dot(part, same_head)
        total = term if total is None else total + term
    return total


def _mm(a, b, mode, out_dtype, tm, tn, tk, name, column_chunks=False, m_part=(0, 1), after=None):
    if mode == "nn":
        m, k = a.shape
        n = b.shape[1]
    elif mode == "nt":
        m, k = a.shape
        n = b.shape[0]
    else:
        k, m = a.shape
        n = b.shape[1]
    assert mode == "tn" or m_part == (0, 1)
    m //= m_part[1]
    tm, tn, tk = min(tm, m), min(tn, n), min(tk, k)
    m0 = m_part[0] * (m // tm)
    extra = [] if after is None else [after]
    nk = k // tk
    chunk = n // N_DEV
    per_tile = tn // chunk if column_chunks else 1
    dims = {"nn": NN, "nt": NT, "tn": TN}[mode]
    a_spec = pl.BlockSpec((tk, tm), lambda j, i, kk: (kk, m0 + i)) if mode == "tn" else pl.BlockSpec((tm, tk), lambda j, i, kk: (i, kk))
    b_spec = pl.BlockSpec((tn, tk), lambda j, i, kk: (j, kk)) if mode == "nt" else pl.BlockSpec((tk, tn), lambda j, i, kk: (kk, j))

    def body(a_ref, b_ref, *rest):
        o_ref, *acc = rest[len(extra):]
        prod = _dot(a_ref[...].astype(BF16), b_ref[...].astype(BF16), dims)

        def write(res):
            if column_chunks:
                for c in range(per_tile):
                    o_ref[c] = res[:, c * chunk:(c + 1) * chunk].astype(o_ref.dtype)
            else:
                o_ref[...] = res.astype(o_ref.dtype)

        if nk == 1:
            write(prod)
        else:
            acc_ref, = acc
            kk = pl.program_id(2)

            @pl.when(kk == 0)
            def _():
                acc_ref[...] = prod

            @pl.when(kk > 0)
            def _():
                acc_ref[...] += prod

            @pl.when(kk == nk - 1)
            def _():
                write(acc_ref[...])

    return pl.pallas_call(
        body, name=name, grid=(n // tn, m // tm, nk),
        in_specs=[a_spec, b_spec] + [pl.BlockSpec(memory_space=pl.ANY)] * len(extra),
        out_specs=(pl.BlockSpec((per_tile, tm, chunk), lambda j, i, kk: (j, i, 0)) if column_chunks
                   else pl.BlockSpec((tm, tn), lambda j, i, kk: (i, j))),
        out_shape=jax.ShapeDtypeStruct((N_DEV, m, chunk) if column_chunks else (m, n), out_dtype),
        scratch_shapes=[pltpu.VMEM((tm, tn), F32)] if nk > 1 else [],
        compiler_params=_params(("parallel", "parallel", "arbitrary")),
    )(a, b, *extra)


def _mm_tn3(a_list, b_list, name):
    s, m = a_list[0].shape
    n = b_list[0].shape[1]
    tk = min(2048, s)
    nk = s // tk

    def body(*refs):
        a_refs, b_refs, o_refs, acc_refs = refs[0:3], refs[3:6], refs[6:9], refs[9:12]
        kk = pl.program_id(0)
        for a_ref, b_ref, o_ref, acc_ref in zip(a_refs, b_refs, o_refs, acc_refs):
            prod = _dot(a_ref[...], b_ref[...], TN)
            if nk == 1:
                o_ref[...] = prod.astype(o_ref.dtype)
                continue

            @pl.when(kk == 0)
            def _(acc_ref=acc_ref, prod=prod):
                acc_ref[...] = prod

            @pl.when(kk > 0)
            def _(acc_ref=acc_ref, prod=prod):
                acc_ref[...] += prod

            @pl.when(kk == nk - 1)
            def _(acc_ref=acc_ref, o_ref=o_ref):
                o_ref[...] = acc_ref[...].astype(o_ref.dtype)

    return pl.pallas_call(
        body, name=name, grid=(nk,),
        in_specs=[pl.BlockSpec((tk, m), lambda kk: (kk, 0))] * 3 + [pl.BlockSpec((tk, n), lambda kk: (kk, 0))] * 3,
        out_specs=[_full((m, n))] * 3,
        out_shape=[jax.ShapeDtypeStruct((m, n), BF16)] * 3,
        scratch_shapes=[pltpu.VMEM((m, n), F32)] * 3,
        compiler_params=_params(("arbitrary",)),
    )(*a_list, *b_list)


def _rms_fwd(x, g, name, deps=()):
    s, d = x.shape
    tm = min(512, s)

    def body(x_ref, g_ref, *rest):
        h_ref = rest[len(deps)]
        xv = x_ref[...]
        r = lax.rsqrt(jnp.mean(xv * xv, axis=-1, keepdims=True) + EPS)
        h_ref[...] = (xv * r * g_ref[...]).astype(BF16)

    return pl.pallas_call(
        body, name=name, grid=(s // tm,),
        in_specs=[pl.BlockSpec((tm, d), lambda i: (i, 0)), _full((1, d))] + [pl.BlockSpec(memory_space=pl.ANY)] * len(deps),
        out_specs=pl.BlockSpec((tm, d), lambda i: (i, 0)),
        out_shape=jax.ShapeDtypeStruct((s, d), BF16),
        compiler_params=_params(("parallel",)),
    )(x, g, *deps)


def _proj_post(proj, gq_fox, gk_fox, gq_mem, gq_swa, gk_swa):
    s = proj.shape[0]
    tm = min(512, s)

    def body(p_ref, gqf, gkf, gqm, gqa, gka, qf_ref, kf_ref, vf_ref, qm_ref, qa_ref, ka_ref, va_ref, qft_ref, vft_ref):
        def norm(off, width, hd, g_ref, o_ref, scaled_t_ref=None):
            for b in range(width // 128):
                v = p_ref[:, off + b * 128: off + (b + 1) * 128].astype(F32)
                r = lax.rsqrt(_group_mean(v * v, hd) + EPS)
                vn = (v * r * g_ref[...]).astype(BF16)
                o_ref[:, b * 128:(b + 1) * 128] = vn
                if scaled_t_ref is not None:
                    scaled_t_ref[b * 128:(b + 1) * 128, :] = (vn.astype(F32) * 0.125).T.astype(BF16)

        norm(H_QF, 512, HEAD, gqf, qf_ref, qft_ref)
        norm(H_KF, 512, HEAD, gkf, kf_ref)
        vf_ref[...] = p_ref[:, H_VF:H_VF + 512].astype(BF16)
        for b in range(4):
            vft_ref[b * 128:(b + 1) * 128, :] = p_ref[:, H_VF + b * 128:H_VF + (b + 1) * 128].astype(F32).T.astype(BF16)
        norm(H_QM, 512, MEM_HEAD, gqm, qm_ref)
        norm(H_QA, 512, HEAD, gqa, qa_ref)
        norm(H_KA, 128, HEAD, gka, ka_ref)
        va_ref[...] = p_ref[:, H_VA:H_VA + 128].astype(BF16)

    g_spec = _full((1, 128))
    o512 = pl.BlockSpec((tm, 512), lambda i: (i, 0))
    o128 = pl.BlockSpec((tm, 128), lambda i: (i, 0))
    s512 = jax.ShapeDtypeStruct((s, 512), BF16)
    s128 = jax.ShapeDtypeStruct((s, 128), BF16)
    return pl.pallas_call(
        body, name="proj_post", grid=(s // tm,),
        in_specs=[pl.BlockSpec((tm, HALF_W), lambda i: (i, 1)), g_spec, g_spec, g_spec, g_spec, g_spec],
        out_specs=[o512, o512, o512, o512, o512, o128, o128] + [pl.BlockSpec((512, tm), lambda i: (0, i))] * 2,
        out_shape=[s512, s512, s512, s512, s512, s128, s128] + [jax.ShapeDtypeStruct((512, s), BF16)] * 2,
        compiler_params=_params(("parallel",)),
    )(proj, gq_fox, gk_fox, gq_mem, gq_swa, gk_swa)


def _tri(n, lower):
    r = lax.broadcasted_iota(jnp.int32, (n, n), 0)
    c = lax.broadcasted_iota(jnp.int32, (n, n), 1)
    return jnp.where((c <= r) if lower else (c >= r), 1.0, 0.0).astype(F32)


def _fox_gate_fwd(proj, b_forget128):
    s = proj.shape[0]
    tm = min(512, s)

    def body(p_ref, b_ref, cc_ref, ca_ref, carry_ref):
        i = pl.program_id(0)

        @pl.when(i == 0)
        def _():
            carry_ref[...] = jnp.zeros_like(carry_ref)

        z = p_ref[...] + b_ref[...]
        logf = jnp.minimum(z, 0.0) - jnp.log(1.0 + jnp.exp(-jnp.abs(z)))
        c = jnp.dot(_tri(tm, True), logf, precision=lax.Precision.HIGHEST, preferred_element_type=F32) + carry_ref[...]
        carry_ref[...] = c[tm - 1:tm, :]
        lane = lax.broadcasted_iota(jnp.int32, (tm, 128), 1)
        for hp in range(4):
            cc_ref[hp] = c if hp == 0 else pltpu.roll(c, 128 - 2 * hp, 1)
            aug = jnp.zeros((tm, 128), F32)
            for e in range(2):
                rest = jnp.broadcast_to(c[:, 2 * hp + e:2 * hp + e + 1], (tm, 128))
                for part in range(3):
                    piece = rest.astype(BF16).astype(F32)
                    aug = jnp.where(lane == HEAD * (1 - e) + part, piece, aug)
                    rest = rest - piece
            ca_ref[hp] = aug.astype(BF16)

    return pl.pallas_call(
        body, name="fox_gate_fwd", grid=(s // tm,),
        in_specs=[pl.BlockSpec((tm, 128), lambda i: (i, 0)), _full((1, 128))],
        out_specs=[pl.BlockSpec((4, tm, 128), lambda i: (0, i, 0))] * 2,
        out_shape=[jax.ShapeDtypeStruct((4, s, 128), F32), jax.ShapeDtypeStruct((4, s, 128), BF16)],
        scratch_shapes=[pltpu.VMEM((1, 128), F32)],
        compiler_params=_params(("arbitrary",)),
    )(proj, b_forget128)


def _memkv_fwd(mem, g_mem, w_kv, kn_mem):
    m = mem.shape[0]

    def body(mem_ref, g_ref, w_ref, kn_ref, memn_ref, kv_ref, mk_ref, mv_ref):
        xv = mem_ref[...]
        r = lax.rsqrt(jnp.mean(xv * xv, axis=-1, keepdims=True) + EPS)
        mn = (xv * r * g_ref[...]).astype(BF16)
        memn_ref[...] = mn
        kv = _dot(mn, w_ref[...])
        kv_ref[...] = kv
        for h in range(MEM_HEADS):
            v = kv[:, h * 128:(h + 1) * 128]
            rr = lax.rsqrt(jnp.mean(v * v, axis=-1, keepdims=True) + EPS)
            mk_ref[:, h * 128:(h + 1) * 128] = (v * rr * kn_ref[...]).astype(BF16)
        mv_ref[...] = kv[:, 512:1024].astype(BF16)

    return pl.pallas_call(
        body, name="memkv_fwd",
        out_shape=[jax.ShapeDtypeStruct((m, D_MODEL), BF16), jax.ShapeDtypeStruct((m, 1024), F32),
                   jax.ShapeDtypeStruct((m, 512), BF16), jax.ShapeDtypeStruct((m, 512), BF16)],
        compiler_params=pltpu.CompilerParams(vmem_limit_bytes=VMEM_LIMIT),
    )(mem, g_mem, w_kv, kn_mem)


def _bias_table(rel_bias, bucket):
    def body(rb_ref, bk_ref, o_ref):
        bk = bk_ref[...]
        for h in range(SWA_HEADS):
            acc = jnp.zeros(bk.shape, F32)
            for b in range(REL_BUCKETS):
                acc = jnp.where(bk == b, rb_ref[b, h], acc)
            o_ref[h] = acc

    return pl.pallas_call(
        body, name="bias_table",
        in_specs=[pl.BlockSpec(memory_space=pltpu.SMEM), pl.BlockSpec(memory_space=pltpu.VMEM)],
        out_shape=jax.ShapeDtypeStruct((SWA_HEADS,) + bucket.shape, F32),
    )(rel_bias, bucket)


def _swa_valid(n):
    row = lax.broadcasted_iota(jnp.int32, (SWA_BLOCK, 2 * SWA_BLOCK), 0)
    col = lax.broadcasted_iota(jnp.int32, (SWA_BLOCK, 2 * SWA_BLOCK), 1)
    dist = row + SWA_BLOCK - col
    return (dist >= 0) & (dist < SWA_BLOCK) & ((col >= SWA_BLOCK) | (n > 0))


def _swa_fwd(qa, kp, vp, bias, sink):
    s = qa.shape[0]
    nb = s // SWA_BLOCK

    def body(sink_ref, q_ref, kp_ref, vp_ref, bias_ref, o_ref):
        n = pl.program_id(0)
        start = pl.multiple_of(n * SWA_BLOCK, SWA_BLOCK)
        k2 = kp_ref[pl.ds(start, 2 * SWA_BLOCK), :]
        v2 = vp_ref[pl.ds(start, 2 * SWA_BLOCK), :]
        valid = _swa_valid(n)
        heads = range(SWA_HEADS)
        hs = lambda h: slice(h * HEAD, (h + 1) * HEAD)
        sc = [jnp.where(valid, _dot(q_ref[:, hs(h)], k2[:, hs(h // 4)], NT) * 0.125 + bias_ref[h], NEG) for h in heads]
        pn = []
        for h in heads:
            sk = sink_ref[h]
            mx = jnp.maximum(jnp.max(sc[h], axis=-1, keepdims=True), sk)
            p = jnp.exp(sc[h] - mx)
            den = jnp.sum(p, axis=-1, keepdims=True) + jnp.exp(sk - mx)
            pn.append((p / den).astype(BF16))
        outs = [_dot(pn[h], v2[:, hs(h // 4)]).astype(BF16) for h in heads]
        for h in heads:
            o_ref[:, hs(h)] = outs[h]

    return pl.pallas_call(
        body, name="swa_fwd", grid=(nb,),
        in_specs=[pl.BlockSpec(memory_space=pltpu.SMEM),
                  pl.BlockSpec((SWA_BLOCK, 512), lambda n: (n, 0)),
                  _full(kp.shape), _full(vp.shape), _full(bias.shape)],
        out_specs=pl.BlockSpec((SWA_BLOCK, 512), lambda n: (n, 0)),
        out_shape=jax.ShapeDtypeStruct((s, 512), BF16),
        compiler_params=_params(("parallel",)),
    )(sink, qa, kp, vp, bias)


def _head_mask(e):
    lane = lax.broadcasted_iota(jnp.int32, (1, 128), 1)
    return (lane >= e * HEAD) & (lane < (e + 1) * HEAD)


FOX_FWD_T = 1024
FOX_BWD_T = 512


def _head_rows(e):
    row = lax.broadcasted_iota(jnp.int32, (128, 1), 0)
    return (row >= e * HEAD) & (row < (e + 1) * HEAD)


def _fox_fwd(q, k, v_t, ca4):
    s = q.shape[0]
    t = min(FOX_FWD_T, s)
    nq = s // t

    def body(q_ref, k_ref, vt_ref, ca_ref, o_ref, lse_ref, ot_ref):
        i = pl.program_id(1)
        qs = q_ref[...] * jnp.asarray(0.125, BF16)
        lane = lax.broadcasted_iota(jnp.int32, (1, 128), 1)
        minus = [jnp.where((lane >= HEAD * (1 - e)) & (lane < HEAD * (1 - e) + 3), -1.0, 0.0).astype(BF16) for e in range(2)]
        qe = [jnp.where(_head_mask(e), qs, jnp.broadcast_to(minus[e], qs.shape)) for e in range(2)]

        def block(carry, key0, nkeys, q0, nqs, masked):
            ks = pl.ds(pl.multiple_of(key0, 128), nkeys)
            kj = k_ref[ks, :]
            caj = ca_ref[0, ks, :]
            vtj = vt_ref[:, ks]
            out = []
            for e in range(2):
                m_all, acc_all = carry[2 * e], carry[2 * e + 1]
                m, acc = m_all[:, q0:q0 + nqs], acc_all[:, q0:q0 + nqs]
                st = _dot(jnp.where(_head_mask(e), kj, caj), qe[e][q0:q0 + nqs, :], NT)
                if masked:
                    krow = lax.broadcasted_iota(jnp.int32, (nkeys, nqs), 0) + key0
                    qcol = lax.broadcasted_iota(jnp.int32, (nkeys, nqs), 1) + (i * t + q0)
                    st = jnp.where(krow <= qcol, st, NEG)
                m_new = jnp.maximum(m, jnp.max(st, axis=0, keepdims=True))
                alpha = jnp.exp(m - m_new)
                pt = jnp.exp(st - m_new).astype(BF16)
                vte = jnp.where(_head_rows(e), vtj, jnp.ones_like(vtj))
                acc_new = alpha * acc + _dot(vte, pt)
                if nqs < t:
                    m_new = jnp.concatenate([m_all[:, :q0], m_new], axis=1)
                    acc_new = jnp.concatenate([acc_all[:, :q0], acc_new], axis=1)
                out += [m_new, acc_new]
            return tuple(out)

        half = t // 2
        init = (jnp.full((1, t), NEG, F32), jnp.zeros((128, t), F32)) * 2
        carry = lax.fori_loop(0, i, lambda j, c: block(c, j * t, t, 0, t, False), init)
        carry = block(carry, i * t, half, 0, t, True)
        m0, a0, m1, a1 = block(carry, i * t + half, half, half, half, True)
        l0 = a0[HEAD:HEAD + 1, :]
        l1 = a1[0:1, :]
        o_t = jnp.where(_head_rows(0), a0 / l0, a1 / l1)
        o_ref[...] = o_t.T.astype(BF16)
        ot_ref[...] = o_t.astype(BF16)
        r8 = lax.broadcasted_iota(jnp.int32, (8, t), 0)
        lse_ref[0] = jnp.where(r8 == 0, m0 + jnp.log(l0), jnp.where(r8 == 1, m1 + jnp.log(l1), 0.0))

    return pl.pallas_call(
        body, name="fox_fwd", grid=(4, nq),
        in_specs=[pl.BlockSpec((t, 128), lambda hp, i: (i, hp)),
                  pl.BlockSpec((s, 128), lambda hp, i: (0, hp)),
                  pl.BlockSpec((128, s), lambda hp, i: (hp, 0)),
                  pl.BlockSpec((1, s, 128), lambda hp, i: (hp, 0, 0))],
        out_specs=[pl.BlockSpec((t, 128), lambda hp, i: (i, hp)),
                   pl.BlockSpec((1, 8, t), lambda hp, i: (hp, 0, i)),
                   pl.BlockSpec((128, t), lambda hp, i: (hp, i))],
        out_shape=[jax.ShapeDtypeStruct((s, 512), BF16), jax.ShapeDtypeStruct((4, 8, s), F32),
                   jax.ShapeDtypeStruct((512, s), BF16)],
        compiler_params=_params(("parallel", "parallel")),
    )(q, k, v_t, ca4)


MEM_SCALE = MEM_HEAD ** -0.5


def _mem_fwd(qm, mk, mv):
    s = qm.shape[0]
    tq = min(512, s)

    def body(q_ref, mk_ref, mv_ref, o_ref):
        for h in range(MEM_HEADS):
            hs = slice(h * 128, (h + 1) * 128)
            sc = _dot(q_ref[:, hs], mk_ref[:, hs], NT) * MEM_SCALE
            mx = jnp.max(sc, axis=-1, keepdims=True)
            p = jnp.exp(sc - mx)
            p = p / jnp.sum(p, axis=-1, keepdims=True)
            o_ref[:, hs] = _dot(p.astype(BF16), mv_ref[:, hs]).astype(BF16)

    return pl.pallas_call(
        body, name="mem_fwd", grid=(s // tq,),
        in_specs=[pl.BlockSpec((tq, 512), lambda i: (i, 0)), _full(mk.shape), _full(mv.shape)],
        out_specs=pl.BlockSpec((tq, 512), lambda i: (i, 0)),
        out_shape=jax.ShapeDtypeStruct((s, 512), BF16),
        compiler_params=_params(("parallel",)),
    )(qm, mk, mv)


def _merge_fwd(x, oa, of, om, proj, b_gate, wa, wf, wm, w_out, g_mlp):
    s = x.shape[0]
    tm = min(512, s)

    def body(x_ref, oa_ref, of_ref, om_ref, gl_ref, bg_ref, wa_ref, wf_ref, wm_ref, wo_ref, g_ref, x1_ref, hm_ref, mg_ref):
        merged = None
        for b, (o_ref, w_ref) in enumerate(((oa_ref, wa_ref), (of_ref, wf_ref), (om_ref, wm_ref))):
            cs = slice(b * D_MODEL, (b + 1) * D_MODEL)
            y = _dot(o_ref[...], w_ref[...])
            t = _sigmoid(gl_ref[:, cs].astype(F32) + bg_ref[:, cs]) * y
            merged = t if merged is None else merged + t
        mb = merged.astype(BF16)
        mg_ref[...] = mb
        x1 = x_ref[...] + _dot(mb, wo_ref[...])
        x1_ref[...] = x1
        r = lax.rsqrt(jnp.mean(x1 * x1, axis=-1, keepdims=True) + EPS)
        hm_ref[...] = (x1 * r * g_ref[...]).astype(BF16)

    row = lambda w: pl.BlockSpec((tm, w), lambda i: (i, 0))
    return pl.pallas_call(
        body, name="merge_fwd", grid=(s // tm,),
        in_specs=[row(D_MODEL), row(512), row(512), row(512), row(HALF_W), _full((1, HALF_W)),
                  _full(wa.shape), _full(wf.shape), _full(wm.shape), _full(w_out.shape), _full((1, D_MODEL))],
        out_specs=[row(D_MODEL), row(D_MODEL), row(D_MODEL)],
        out_shape=[jax.ShapeDtypeStruct((s, D_MODEL), F32), jax.ShapeDtypeStruct((s, D_MODEL), BF16),
                   jax.ShapeDtypeStruct((s, D_MODEL), BF16)],
        compiler_params=_params(("parallel",)),
    )(x, oa, of, om, proj, b_gate, wa, wf, wm, w_out, g_mlp)


def _mlp_up(hm, w_up):
    s = hm.shape[0]
    tm, tn = min(1024, s), w_up.shape[2]

    def body(h_ref, w_ref, u_ref):
        r = jnp.maximum(_dot(h_ref[...], w_ref[0]), 0.0)
        u_ref[...] = (r * r).astype(BF16)

    return pl.pallas_call(
        body, name="mlp_up", grid=(s // tm, D_FF // tn),
        in_specs=[pl.BlockSpec((tm, D_MODEL), lambda i, j: (i, 0)), pl.BlockSpec((1, D_MODEL, tn), lambda i, j: (j, 0, 0))],
        out_specs=pl.BlockSpec((tm, tn), lambda i, j: (i, j)),
        out_shape=jax.ShapeDtypeStruct((s, D_FF), BF16),
        compiler_params=_params(("parallel", "parallel")),
    )(hm, w_up)


def _mlp_down_loss(u, w_down, x1, target):
    s = u.shape[0]
    tm = min(256, s)

    def body(u_ref, w_ref, x1_ref, t_ref, dy_ref, dyb_ref, loss_ref):
        i = pl.program_id(0)

        @pl.when(i == 0)
        def _():
            loss_ref[...] = jnp.zeros_like(loss_ref)

        y = x1_ref[...] + _dot(u_ref[...], w_ref[...])
        err = y - t_ref[...]
        dy = err * (1.0 / D_MODEL)
        dy_ref[...] = dy
        dyb_ref[...] = dy.astype(BF16)
        part = jnp.sum(jnp.sum(err * err, axis=-1, keepdims=True) * (1.0 / D_MODEL), axis=0, keepdims=True)
        loss_ref[...] += 0.5 * part

    row = pl.BlockSpec((tm, D_MODEL), lambda i: (i, 0))
    return pl.pallas_call(
        body, name="mlp_down_loss", grid=(s // tm,),
        in_specs=[pl.BlockSpec((tm, D_FF), lambda i: (i, 0)), _full(w_down.shape), row, row],
        out_specs=[row, row, _full((1, 1))],
        out_shape=[jax.ShapeDtypeStruct((s, D_MODEL), F32), jax.ShapeDtypeStruct((s, D_MODEL), BF16),
                   jax.ShapeDtypeStruct((1, 1), F32)],
        compiler_params=_params(("arbitrary",)),
    )(u, w_down, x1, target)


def _mlp_bwd_act(dy, w_down, u):
    s = dy.shape[0]
    tm, tn = min(1024, s), 1024

    def body(dy_ref, w_ref, u_ref, da_ref):
        du = _dot(dy_ref[...], w_ref[...], NT)
        da_ref[...] = (du * (2.0 * jnp.sqrt(u_ref[...].astype(F32)))).astype(BF16)

    return pl.pallas_call(
        body, name="mlp_bwd_act", grid=(D_FF // tn, s // tm),
        in_specs=[pl.BlockSpec((tm, D_MODEL), lambda j, i: (i, 0)), pl.BlockSpec((tn, D_MODEL), lambda j, i: (j, 0)),
                  pl.BlockSpec((tm, tn), lambda j, i: (i, j))],
        out_specs=pl.BlockSpec((tm, tn), lambda j, i: (i, j)),
        out_shape=jax.ShapeDtypeStruct((s, D_FF), BF16),
        compiler_params=_params(("parallel", "parallel")),
    )(dy, w_down, u)


def _rms_bwd(xv, g, dh, skip):
    r = lax.rsqrt(jnp.mean(xv * xv, axis=-1, keepdims=True) + EPS)
    n = xv * r
    dn = dh * g
    dx = skip + r * (dn - n * jnp.mean(dn * n, axis=-1, keepdims=True))
    return dx, jnp.sum(dh * n, axis=0, keepdims=True)


def _mlp_bwd_x(da, w_up, x1, dy, g_mlp):
    s = da.shape[0]
    tm = min(256, s)

    def body(da_ref, w_ref, x1_ref, dy_ref, g_ref, dx1_ref, dg_ref):
        i = pl.program_id(0)

        @pl.when(i == 0)
        def _():
            dg_ref[...] = jnp.zeros_like(dg_ref)

        tn = w_ref.shape[2]
        dhm = _dot(da_ref[:, 0:tn], w_ref[0], NT)
        for j in range(1, N_DEV):
            dhm = dhm + _dot(da_ref[:, j * tn:(j + 1) * tn], w_ref[j], NT)
        dx, dg = _rms_bwd(x1_ref[...], g_ref[...], dhm, dy_ref[...])
        dx1_ref[...] = dx
        dg_ref[...] += dg

    row = pl.BlockSpec((tm, D_MODEL), lambda i: (i, 0))
    return pl.pallas_call(
        body, name="mlp_bwd_x", grid=(s // tm,),
        in_specs=[pl.BlockSpec((tm, D_FF), lambda i: (i, 0)), _full(w_up.shape), row, row, _full((1, D_MODEL))],
        out_specs=[row, _full((1, D_MODEL))],
        out_shape=[jax.ShapeDtypeStruct((s, D_MODEL), F32), jax.ShapeDtypeStruct((1, D_MODEL), F32)],
        compiler_params=_params(("arbitrary",)),
    )(da, w_up, x1, dy, g_mlp)


def _merge_bwd(dx1, oa, of, om, proj, b_gate, wa, wf, wm, w_out):
    s = dx1.shape[0]
    tm = min(512, s)

    def body(dx1_ref, oa_ref, of_ref, om_ref, gl_ref, bg_ref, wa_ref, wf_ref, wm_ref, wo_ref,
             dp_ref, doa_ref, dof_ref, dom_ref, dya_ref, dyf_ref, dym_ref, dbg_ref):
        i = pl.program_id(0)

        @pl.when(i == 0)
        def _():
            dbg_ref[...] = jnp.zeros_like(dbg_ref)

        dmerged = _dot(dx1_ref[...].astype(BF16), wo_ref[...], NT)
        branches = ((oa_ref, wa_ref, doa_ref, dya_ref), (of_ref, wf_ref, dof_ref, dyf_ref), (om_ref, wm_ref, dom_ref, dym_ref))
        for b, (o_ref, w_ref, do_ref, dyb_ref) in enumerate(branches):
            cs = slice(b * D_MODEL, (b + 1) * D_MODEL)
            y = _dot(o_ref[...], w_ref[...])
            g = _sigmoid(gl_ref[:, cs].astype(F32) + bg_ref[:, cs])
            dz = (dmerged * y) * g * (1.0 - g)
            dp_ref[:, cs] = dz.astype(BF16)
            dbg_ref[:, cs] += jnp.sum(dz, axis=0, keepdims=True)
            dyb = (dmerged * g).astype(BF16)
            dyb_ref[...] = dyb
            do = _dot(dyb, w_ref[...], NT)
            do_ref[...] = (do.T if b == 1 else do).astype(BF16)

    row = lambda w: pl.BlockSpec((tm, w), lambda i: (i, 0))
    sd = lambda w: jax.ShapeDtypeStruct((s, w), BF16)
    return pl.pallas_call(
        body, name="merge_bwd", grid=(s // tm,),
        in_specs=[row(D_MODEL), row(512), row(512), row(512), row(HALF_W), _full((1, HALF_W)),
                  _full(wa.shape), _full(wf.shape), _full(wm.shape), _full(w_out.shape)],
        out_specs=[row(HALF_W), row(512), pl.BlockSpec((512, tm), lambda i: (0, i)), row(512),
                   row(D_MODEL), row(D_MODEL), row(D_MODEL), _full((1, HALF_W))],
        out_shape=[sd(PROJ_W), sd(512), jax.ShapeDtypeStruct((512, s), BF16), sd(512), sd(D_MODEL), sd(D_MODEL), sd(D_MODEL),
                   jax.ShapeDtypeStruct((1, HALF_W), F32)],
        compiler_params=_params(("arbitrary",)),
    )(dx1, oa, of, om, proj, b_gate, wa, wf, wm, w_out)


def _swa_valid_t(n):
    key = lax.broadcasted_iota(jnp.int32, (2 * SWA_BLOCK, SWA_BLOCK), 0)
    qry = lax.broadcasted_iota(jnp.int32, (2 * SWA_BLOCK, SWA_BLOCK), 1)
    dist = qry + SWA_BLOCK - key
    return (dist >= 0) & (dist < SWA_BLOCK) & ((key >= SWA_BLOCK) | (n > 0))


def _swa_bwd(qa, kp, vp, bias_t, sink, doa):
    s = qa.shape[0]
    nb = s // SWA_BLOCK

    def body(sink_ref, q_ref, kp_ref, vp_ref, bias_ref, do_ref, dq_ref, dkp_ref, dvp_ref, dbias_ref, dsink_ref, sk_acc):
        n = pl.program_id(0)

        @pl.when(n == 0)
        def _():
            dkp_ref[...] = jnp.zeros_like(dkp_ref)
            dvp_ref[...] = jnp.zeros_like(dvp_ref)
            dbias_ref[...] = jnp.zeros_like(dbias_ref)
            sk_acc[...] = jnp.zeros_like(sk_acc)

        start = pl.multiple_of(n * SWA_BLOCK, SWA_BLOCK)
        win = pl.ds(start, 2 * SWA_BLOCK)
        k2 = kp_ref[win, :]
        v2 = vp_ref[win, :]
        valid = _swa_valid_t(n)
        heads = range(SWA_HEADS)
        hs = lambda h: slice(h * HEAD, (h + 1) * HEAD)
        scale = jnp.asarray(0.125, BF16)
        q = [q_ref[:, hs(h)] for h in heads]
        do = [do_ref[:, hs(h)] for h in heads]
        kk = [k2[:, hs(kv)] for kv in range(2)]
        vv = [v2[:, hs(kv)] for kv in range(2)]
        kt = [(kk[kv].astype(F32) * 0.125).T.astype(BF16) for kv in range(2)]
        st = [jnp.where(valid, _dot(kk[h // 4], q[h], NT) * 0.125 + bias_ref[h], NEG) for h in heads]
        dpt = [_dot(vv[h // 4], do[h], NT) for h in heads]
        pt, dst = [], []
        for h in heads:
            sk = sink_ref[h]
            mx = jnp.maximum(jnp.max(st[h], axis=0, keepdims=True), sk)
            p = jnp.exp(st[h] - mx)
            esk = jnp.exp(sk - mx)
            den = jnp.sum(p, axis=0, keepdims=True) + esk
            p = p / den
            delta = jnp.sum(p * dpt[h], axis=0, keepdims=True)
            d = p * (dpt[h] - delta)
            sk_acc[h:h + 1, :] += -(esk / den) * delta
            dbias_ref[h] += d
            pt.append(p.astype(BF16))
            dst.append(d.astype(BF16))
        dq_t = [_dot(kt[h // 4], dst[h]) for h in heads]
        dq_ref[...] = jnp.concatenate(dq_t, axis=0).T.astype(BF16)
        for kv in range(2):
            group = range(4 * kv, 4 * kv + 4)
            dk = [_dot(dst[h], q[h] * scale) for h in group]
            dv = [_dot(pt[h], do[h]) for h in group]
            dkp_ref[win, hs(kv)] += (dk[0] + dk[1]) + (dk[2] + dk[3])
            dvp_ref[win, hs(kv)] += (dv[0] + dv[1]) + (dv[2] + dv[3])

        @pl.when(n == nb - 1)
        def _():
            dsink_ref[...] = jnp.broadcast_to(jnp.sum(sk_acc[...], axis=1, keepdims=True), dsink_ref.shape)

    return pl.pallas_call(
        body, name="swa_bwd", grid=(nb,),
        in_specs=[pl.BlockSpec(memory_space=pltpu.SMEM),
                  pl.BlockSpec((SWA_BLOCK, 512), lambda n: (n, 0)),
                  _full(kp.shape), _full(vp.shape), _full(bias_t.shape),
                  pl.BlockSpec((SWA_BLOCK, 512), lambda n: (n, 0))],
        out_specs=[pl.BlockSpec((SWA_BLOCK, 512), lambda n: (n, 0)), _full(kp.shape), _full(vp.shape),
                   _full(bias_t.shape), _full((SWA_HEADS, 128))],
        out_shape=[jax.ShapeDtypeStruct((s, 512), BF16), jax.ShapeDtypeStruct(kp.shape, F32),
                   jax.ShapeDtypeStruct(vp.shape, F32), jax.ShapeDtypeStruct(bias_t.shape, F32),
                   jax.ShapeDtypeStruct((SWA_HEADS, 128), F32)],
        scratch_shapes=[pltpu.VMEM((SWA_HEADS, 128), F32)],
        compiler_params=_params(("arbitrary",)),
    )(sink, qa, kp, vp, bias_t, doa)


def _fox_bwd(qt, k, v, dot, ot, cc4, lse4):
    s = k.shape[0]
    t = min(FOX_BWD_T, s)
    nq = s // t

    def body(qt_ref, k_ref, v_ref, dot_ref, ot_ref, cc_ref, lse_ref,
             dqt_ref, dk_ref, dv_ref, dck_ref, dcq_ref, delta_ref, dkt_acc, dvt_acc, ds0, ds1):
        j = pl.program_id(1)

        @pl.when(j == 0)
        def _():
            dqt_ref[...] = jnp.zeros_like(dqt_ref)
            dcq_ref[...] = jnp.zeros_like(dcq_ref)
            r8 = lax.broadcasted_iota(jnp.int32, (8, t), 0)

            def dl(i, c):
                cols = pl.ds(pl.multiple_of(i * t, t), t)
                pr = dot_ref[:, cols].astype(F32) * ot_ref[:, cols].astype(F32)
                d0 = jnp.sum(jnp.where(_head_rows(0), pr, 0.0), axis=0, keepdims=True)
                d1 = jnp.sum(jnp.where(_head_rows(1), pr, 0.0), axis=0, keepdims=True)
                delta_ref[:, cols] = jnp.where(r8 == 0, d0, jnp.where(r8 == 1, d1, 0.0))
                return c

            lax.fori_loop(0, nq, dl, 0)

        kj = k_ref[...]
        vj = v_ref[...]
        ks = pl.ds(pl.multiple_of(j * t, t), t)
        kt = (kj.astype(F32) * 0.125).T.astype(BF16)
        ke = [jnp.where(_head_mask(e), kj, jnp.zeros_like(kj)) for e in range(2)]
        ve = [jnp.where(_head_mask(e), vj, jnp.zeros_like(vj)) for e in range(2)]
        ck = [cc_ref[0, ks, e:e + 1] for e in range(2)]
        for r in (dkt_acc, dvt_acc, ds0, ds1):
            r[...] = jnp.zeros_like(r)

        def block(q0, nqs, k0, nks, masked):
            cols = pl.ds(pl.multiple_of(q0, 128), nqs)
            rows = slice(k0, k0 + nks)
            qti = qt_ref[:, cols]
            doti = dot_ref[:, cols]
            for e, ds_acc in enumerate((ds0, ds1)):
                dims = slice(e * HEAD, (e + 1) * HEAD)
                st = _dot(ke[e][rows, :], qti) - ck[e][rows, :]
                if masked:
                    krow = lax.broadcasted_iota(jnp.int32, (nks, nqs), 0) + (j * t + k0)
                    qcol = lax.broadcasted_iota(jnp.int32, (nks, nqs), 1) + q0
                    st = jnp.where(krow <= qcol, st, NEG)
                pt = jnp.exp(st - lse_ref[0, e:e + 1, cols])
                dpt = _dot(ve[e][rows, :], doti)
                dst = pt * (dpt - delta_ref[e:e + 1, cols])
                dsb = dst.astype(BF16)
                dvt_acc[dims, rows] += _dot(doti[dims, :], pt.astype(BF16), NT)
                dkt_acc[dims, rows] += _dot(qti[dims, :], dsb, NT)
                dqt_ref[dims, cols] += _dot(kt[dims, rows], dsb)
                ds_acc[rows, 0:nqs] += dst
                dcq_ref[0, e:e + 1, cols] += jnp.sum(dst, axis=0, keepdims=True)

        half = t // 2
        block(j * t, half, 0, half, True)
        block(j * t + half, half, 0, t, True)

        def rest(i, c):
            block(i * t, t, 0, t, False)
            return c

        lax.fori_loop(j + 1, nq, rest, 0)
        dk_ref[...] = dkt_acc[...].T.astype(BF16)
        dv_ref[...] = dvt_acc[...].T.astype(BF16)
        lane = lax.broadcasted_iota(jnp.int32, (t, 128), 1)
        c0 = jnp.sum(ds0[...], axis=-1, keepdims=True)
        c1 = jnp.sum(ds1[...], axis=-1, keepdims=True)
        dck_ref[0] = jnp.where(lane == 0, c0, jnp.where(lane == 1, c1, 0.0))

    res_t = lambda: pl.BlockSpec((128, s), lambda hp, j: (hp, 0))
    blk = lambda: pl.BlockSpec((t, 128), lambda hp, j: (j, hp))
    return pl.pallas_call(
        body, name="fox_bwd", grid=(4, nq),
        in_specs=[res_t(), blk(), blk(), res_t(), res_t(), pl.BlockSpec((1, s, 128), lambda hp, j: (hp, 0, 0)),
                  pl.BlockSpec((1, 8, s), lambda hp, j: (hp, 0, 0))],
        out_specs=[res_t(), blk(), blk(),
                   pl.BlockSpec((1, t, 128), lambda hp, j: (hp, j, 0)),
                   pl.BlockSpec((1, 8, s), lambda hp, j: (hp, 0, 0))],
        out_shape=[jax.ShapeDtypeStruct((512, s), F32), jax.ShapeDtypeStruct((s, 512), BF16),
                   jax.ShapeDtypeStruct((s, 512), BF16), jax.ShapeDtypeStruct((4, s, 128), F32),
                   jax.ShapeDtypeStruct((4, 8, s), F32)],
        scratch_shapes=[pltpu.VMEM((8, s), F32)] + [pltpu.VMEM((128, t), F32)] * 2 + [pltpu.VMEM((t, t), F32)] * 2,
        compiler_params=_params(("arbitrary", "arbitrary")),
    )(qt, k, v, dot, ot, cc4, lse4)


def _mem_bwd(qm, mk, mv, dom):
    s = qm.shape[0]
    tq = min(512, s)

    def body(q_ref, mk_ref, mv_ref, do_ref, dq_ref, dmk_ref, dmv_ref):
        i = pl.program_id(0)

        @pl.when(i == 0)
        def _():
            dmk_ref[...] = jnp.zeros_like(dmk_ref)
            dmv_ref[...] = jnp.zeros_like(dmv_ref)

        heads = range(MEM_HEADS)
        hs = lambda h: slice(h * 128, (h + 1) * 128)
        sc = [_dot(q_ref[:, hs(h)], mk_ref[:, hs(h)], NT) * MEM_SCALE for h in heads]
        dp = [_dot(do_ref[:, hs(h)], mv_ref[:, hs(h)], NT) for h in heads]
        pb, dsb = [], []
        for h in heads:
            p = jnp.exp(sc[h] - jnp.max(sc[h], axis=-1, keepdims=True))
            p = p / jnp.sum(p, axis=-1, keepdims=True)
            ds = p * (dp[h] - jnp.sum(p * dp[h], axis=-1, keepdims=True))
            pb.append(p.astype(BF16))
            dsb.append((ds * MEM_SCALE).astype(BF16))
        dq = [_dot(dsb[h], mk_ref[:, hs(h)]).astype(BF16) for h in heads]
        dmk = [_dot(dsb[h], q_ref[:, hs(h)], TN) for h in heads]
        dmv = [_dot(pb[h], do_ref[:, hs(h)], TN) for h in heads]
        for h in heads:
            dq_ref[:, hs(h)] = dq[h]
            dmk_ref[:, hs(h)] += dmk[h]
            dmv_ref[:, hs(h)] += dmv[h]

    return pl.pallas_call(
        body, name="mem_bwd", grid=(s // tq,),
        in_specs=[pl.BlockSpec((tq, 512), lambda i: (i, 0)), _full(mk.shape), _full(mv.shape),
                  pl.BlockSpec((tq, 512), lambda i: (i, 0))],
        out_specs=[pl.BlockSpec((tq, 512), lambda i: (i, 0)), _full(mk.shape), _full(mv.shape)],
        out_shape=[jax.ShapeDtypeStruct((s, 512), BF16), jax.ShapeDtypeStruct(mk.shape, F32),
                   jax.ShapeDtypeStruct(mv.shape, F32)],
        compiler_params=_params(("arbitrary",)),
    )(qm, mk, mv, dom)


def _memkv_bwd(dmk, dmv, kv_raw, kn_mem, mem, g_mem, mem_n, w_kv):
    def body(dmk_ref, dmv_ref, kv_ref, kn_ref, mem_ref, g_ref, mn_ref, w_ref, dw_ref, dkn_ref, dg_ref, dkv_ref):
        dkn = jnp.zeros((1, 128), F32)
        for h in range(MEM_HEADS):
            hs = slice(h * 128, (h + 1) * 128)
            v = kv_ref[:, hs]
            r = lax.rsqrt(jnp.mean(v * v, axis=-1, keepdims=True) + EPS)
            n = v * r
            dn = dmk_ref[:, hs]
            dkn = dkn + jnp.sum(dn * n, axis=0, keepdims=True)
            dng = dn * kn_ref[...]
            dkv_ref[:, hs] = (r * (dng - n * jnp.mean(dng * n, axis=-1, keepdims=True))).astype(BF16)
        dkv_ref[:, 512:1024] = dmv_ref[...].astype(BF16)
        dkn_ref[...] = dkn
        dkv = dkv_ref[...]
        dw_ref[...] = _dot(mn_ref[...], dkv, TN).astype(BF16)
        dmn = _dot(dkv, w_ref[...], NT)
        xv = mem_ref[...]
        r = lax.rsqrt(jnp.mean(xv * xv, axis=-1, keepdims=True) + EPS)
        dg_ref[...] = jnp.sum(dmn * (xv * r), axis=0, keepdims=True)

    m = mem.shape[0]
    return pl.pallas_call(
        body, name="memkv_bwd",
        out_shape=[jax.ShapeDtypeStruct((D_MODEL, 1024), BF16), jax.ShapeDtypeStruct((1, 128), F32),
                   jax.ShapeDtypeStruct((1, D_MODEL), F32)],
        scratch_shapes=[pltpu.VMEM((m, 1024), BF16)],
        compiler_params=pltpu.CompilerParams(vmem_limit_bytes=VMEM_LIMIT),
    )(dmk, dmv, kv_raw, kn_mem, mem, g_mem, mem_n, w_kv)


def _fox_gate_bwd(dcq4, dck4, proj, b_forget128):
    s = dck4.shape[1]
    tm = min(512, s)
    nt = s // tm

    def body(dcq_ref, dck_ref, p_ref, b_ref, dfl_ref, db_ref, carry_ref):
        i = pl.program_id(0)

        @pl.when(i == 0)
        def _():
            carry_ref[...] = jnp.zeros_like(carry_ref)
            db_ref[...] = jnp.zeros_like(db_ref)

        dcv = jnp.zeros((tm, 128), F32)
        for hp in range(4):
            by_query = jnp.concatenate([dcq_ref[hp], jnp.zeros((120, tm), F32)], axis=0).T
            d = by_query - dck_ref[hp]
            dcv = dcv + (d if hp == 0 else pltpu.roll(d, 2 * hp, 1))
        dlogf = jnp.dot(_tri(tm, False), dcv, precision=lax.Precision.HIGHEST, preferred_element_type=F32) + carry_ref[...]
        carry_ref[...] += jnp.sum(dcv, axis=0, keepdims=True)
        z = p_ref[...] + b_ref[...]
        dfl = dlogf * (1.0 / (1.0 + jnp.exp(z)))
        dfl_ref[...] = dfl.astype(BF16)
        db_ref[...] += jnp.sum(dfl, axis=0, keepdims=True)

    return pl.pallas_call(
        body, name="fox_gate_bwd", grid=(nt,),
        in_specs=[pl.BlockSpec((4, 8, tm), lambda i: (0, 0, nt - 1 - i)),
                  pl.BlockSpec((4, tm, 128), lambda i: (0, nt - 1 - i, 0)),
                  pl.BlockSpec((tm, 128), lambda i: (nt - 1 - i, 0)), _full((1, 128))],
        out_specs=[pl.BlockSpec((tm, 128), lambda i: (nt - 1 - i, 0)), _full((1, 128))],
        out_shape=[jax.ShapeDtypeStruct((s, 128), BF16), jax.ShapeDtypeStruct((1, 128), F32)],
        scratch_shapes=[pltpu.VMEM((1, 128), F32)],
        compiler_params=_params(("arbitrary",)),
    )(dcq4, dck4, proj, b_forget128)


def _proj_pre_bwd(dproj, proj, dqf, dkf, dvf, dqm, dqa, dka, dva, dfl, gq_fox, gk_fox, gq_mem, gq_swa, gk_swa):
    s = proj.shape[0]
    tm = min(256, s)

    def body(dp_in, p_ref, dqf_ref, dkf_ref, dvf_ref, dqm_ref, dqa_ref, dka_ref, dva_ref, dfl_ref,
             gqf, gkf, gqm, gqa, gka, dp_ref, dgn_ref):
        i = pl.program_id(0)

        @pl.when(i == 0)
        def _():
            dgn_ref[...] = jnp.zeros_like(dgn_ref)

        def norm_bwd(off, width, hd, g_ref, dn_ref, slot):
            acc = jnp.zeros((1, 128), F32)
            for b in range(width // 128):
                v = p_ref[:, off + b * 128: off + (b + 1) * 128].astype(F32)
                r = lax.rsqrt(_group_mean(v * v, hd) + EPS)
                n = v * r
                dn = dn_ref[b * 128:(b + 1) * 128, :].T if slot == 0 else dn_ref[:, b * 128:(b + 1) * 128].astype(F32)
                acc = acc + jnp.sum(dn * n, axis=0, keepdims=True)
                dng = dn * g_ref[...]
                dp_ref[:, off + b * 128: off + (b + 1) * 128] = (r * (dng - n * _group_mean(dng * n, hd))).astype(BF16)
            dgn_ref[slot:slot + 1, :] += acc

        norm_bwd(H_QF, 512, HEAD, gqf, dqf_ref, 0)
        norm_bwd(H_KF, 512, HEAD, gkf, dkf_ref, 1)
        dp_ref[:, H_VF:H_VF + 512] = dvf_ref[...].astype(BF16)
        norm_bwd(H_QM, 512, MEM_HEAD, gqm, dqm_ref, 2)
        norm_bwd(H_QA, 512, HEAD, gqa, dqa_ref, 3)
        norm_bwd(H_KA, 128, HEAD, gka, dka_ref, 4)
        dp_ref[:, H_VA:H_VA + 128] = dva_ref[...].astype(BF16)
        dp_ref[:, H_FL:H_FL + 128] = dfl_ref[...]
        dp_ref[:, H_FL + 128:HALF_W] = jnp.zeros((tm, HALF_W - H_FL - 128), BF16)

    row = lambda w: pl.BlockSpec((tm, w), lambda i: (i, 0))
    g_spec = _full((1, 128))
    return pl.pallas_call(
        body, name="proj_pre_bwd", grid=(s // tm,),
        in_specs=[pl.BlockSpec(memory_space=pl.ANY), pl.BlockSpec((tm, HALF_W), lambda i: (i, 1)),
                  pl.BlockSpec((512, tm), lambda i: (0, i)), row(512), row(512), row(512), row(512),
                  row(128), row(128), row(128), g_spec, g_spec, g_spec, g_spec, g_spec],
        out_specs=[pl.BlockSpec((tm, HALF_W), lambda i: (i, 1)), _full((8, 128))],
        out_shape=[jax.ShapeDtypeStruct((s, PROJ_W), BF16), jax.ShapeDtypeStruct((8, 128), F32)],
        input_output_aliases={0: 0},
        compiler_params=_params(("arbitrary",)),
    )(dproj, proj, dqf, dkf, dvf, dqm, dqa, dka, dva, dfl, gq_fox, gk_fox, gq_mem, gq_swa, gk_swa)


def _in_bwd_x(dproj, w_in_p, x, g_mix, dx1):
    s = x.shape[0]
    tm = min(256, s)

    def body(dp_ref, w_ref, x_ref, g_ref, dx1_ref, gx_ref, dg_ref):
        i = pl.program_id(0)

        @pl.when(i == 0)
        def _():
            dg_ref[...] = jnp.zeros_like(dg_ref)

        dx, dg = _rms_bwd(x_ref[...], g_ref[...], _dot(dp_ref[...], w_ref[...], NT), dx1_ref[...])
        gx_ref[...] = dx
        dg_ref[...] += dg

    row = pl.BlockSpec((tm, D_MODEL), lambda i: (i, 0))
    return pl.pallas_call(
        body, name="in_bwd_x", grid=(s // tm,),
        in_specs=[pl.BlockSpec((tm, PROJ_W), lambda i: (i, 0)), _full(w_in_p.shape), row, _full((1, D_MODEL)), row],
        out_specs=[row, _full((1, D_MODEL))],
        out_shape=[jax.ShapeDtypeStruct((s, D_MODEL), F32), jax.ShapeDtypeStruct((1, D_MODEL), F32)],
        compiler_params=_params(("arbitrary",)),
    )(dproj, w_in_p, x, g_mix, dx1)


def _rel_bias_bwd(dbias, bucket):
    def body(db_ref, bk_ref, o_ref):
        bk = bk_ref[...]
        lane = lax.broadcasted_iota(jnp.int32, (1, 128), 1)
        for b in range(REL_BUCKETS):
            sel = bk == b
            acc = jnp.zeros((1, 128), F32)
            for h in range(SWA_HEADS):
                tot = jnp.sum(jnp.sum(jnp.where(sel, db_ref[h], 0.0), axis=-1, keepdims=True), axis=0, keepdims=True)
                acc = jnp.where(lane == h, tot, acc)
            o_ref[:, b * 128:(b + 1) * 128] = acc

    return pl.pallas_call(
        body, name="rel_bias_bwd",
        out_shape=jax.ShapeDtypeStruct((1, REL_BUCKETS * 128), F32),
        compiler_params=pltpu.CompilerParams(vmem_limit_bytes=VMEM_LIMIT),
    )(dbias, bucket)


def _my_place():
    return lax.axis_index("x"), lax.axis_index("y"), lax.axis_index("c")


def _peer(place, k):
    x, y, c = place
    return (1 - x if k & 4 else x, 1 - y if k & 2 else y, 1 - c if k & 1 else c)


def _index(place):
    x, y, c = place
    return 4 * x + 2 * y + c


HBM_SPEC = pl.BlockSpec(memory_space=pltpu.HBM)
SEM_SPEC = pl.BlockSpec(memory_space=pltpu.SEMAPHORE)
DATAFLOW = pltpu.SideEffectType.DATAFLOW_SIDE_EFFECTING


ALL_PEERS = tuple(range(1, N_DEV))
SAME_CORE = (2, 4, 6)
OWN = N_DEV - 1


def _split_copy(src_ref, land_ref, send_sems, recv_sems, me, k, gather):
    peer = _peer(me, k)
    if gather:
        src, dst = src_ref, land_ref.at[_index(me)]
    else:
        src, dst = src_ref.at[_index(peer)], land_ref.at[k - 1]
    return pltpu.make_async_remote_copy(src_ref=src, dst_ref=dst, send_sem=send_sems.at[k - 1], recv_sem=recv_sems.at[k - 1],
                                        device_id=peer, device_id_type=MESH)


def _own_copy(src_ref, land_ref, recv_sems, me, gather):
    if gather:
        src, dst = src_ref, land_ref.at[_index(me)]
    else:
        src, dst = src_ref.at[_index(me)], land_ref.at[OWN]
    return pltpu.make_async_copy(src, dst, recv_sems.at[OWN])


def _split_start(srcs, gather, name, peers=ALL_PEERS, after=None):
    n = len(srcs)
    extra = [] if after is None else [after]

    def body(*refs):
        refs = refs[:2 * n] + refs[2 * n + len(extra):]
        src_refs, land_refs = refs[:n], refs[n:2 * n]
        send_sems, recv_sems, token = refs[2 * n:3 * n], refs[3 * n:4 * n], refs[-1]
        me = _my_place()
        for w in range(n):
            for k in peers:
                _split_copy(src_refs[w], land_refs[w], send_sems[w], recv_sems[w], me, k, gather).start()
            _own_copy(src_refs[w], land_refs[w], recv_sems[w], me, gather).start()
        token[...] = jnp.zeros_like(token)

    lands = [lax.empty((N_DEV,) + (a.shape if gather else a.shape[1:]), a.dtype) for a in srcs]
    sems = [pltpu.SemaphoreType.DMA((N_DEV,))] * (2 * n)
    hbm = [pltpu.HBM(a.shape, a.dtype) for a in list(srcs) + lands]
    outs = pl.pallas_call(
        body, name=name,
        out_shape=(*sems, *hbm, jax.ShapeDtypeStruct((8, 128), F32)),
        in_specs=(HBM_SPEC,) * (2 * n) + (pl.BlockSpec(memory_space=pl.ANY),) * len(extra),
        out_specs=(SEM_SPEC,) * (2 * n) + (HBM_SPEC,) * (2 * n) + (pl.BlockSpec(memory_space=pltpu.VMEM),),
        input_output_aliases={i: 2 * n + i for i in range(2 * n)},
        compiler_params=pltpu.CompilerParams(has_side_effects=DATAFLOW),
    )(*[pltpu.with_memory_space_constraint(a, pltpu.HBM) for a in list(srcs) + lands], *extra)
    return list(outs[:n]), list(outs[n:2 * n]), list(outs[2 * n:3 * n]), list(outs[3 * n:4 * n]), outs[-1]


def _split_wait(started, w, after, gather, name):
    send_sems, recv_sems, srcs, lands, _ = started

    def body(src_ref, land_ref, send_sems, recv_sems, after_ref, src_out, land_out):
        me = _my_place()
        for k in ALL_PEERS:
            cp = _split_copy(src_ref, land_ref, send_sems, recv_sems, me, k, gather)
            cp.wait_send()
            cp.wait_recv()
        _own_copy(src_ref, land_ref, recv_sems, me, gather).wait()

    return pl.pallas_call(
        body, name=name,
        out_shape=(pltpu.HBM(srcs[w].shape, srcs[w].dtype), pltpu.HBM(lands[w].shape, lands[w].dtype)),
        in_specs=(HBM_SPEC, HBM_SPEC, SEM_SPEC, SEM_SPEC, pl.BlockSpec(memory_space=pl.ANY)),
        out_specs=(HBM_SPEC, HBM_SPEC), input_output_aliases={0: 0, 1: 1},
        compiler_params=pltpu.CompilerParams(has_side_effects=DATAFLOW),
    )(srcs[w], lands[w], send_sems[w], recv_sems[w], after)[1]


def _forward_copy(land_ref, send_sems, recv_sems, me, j, incoming):
    sibling = _peer(me, 1)
    rows = land_ref.at[_index(_peer(sibling if incoming else me, SAME_CORE[j]))]
    return pltpu.make_async_remote_copy(src_ref=rows, dst_ref=rows, send_sem=send_sems.at[j], recv_sem=recv_sems.at[j],
                                        device_id=sibling, device_id_type=MESH)


def _forward_start(started, after, name):
    send_a, recv_a, srcs, lands, _ = started

    def body(src_ref, land_ref, send_a, recv_a, after_ref, send_b, recv_b, src_out, land_out):
        me = _my_place()
        for j, k in enumerate(SAME_CORE):
            _split_copy(src_ref, land_ref, send_a, recv_a, me, k, True).wait_recv()
            _forward_copy(land_ref, send_b, recv_b, me, j, False).start()

    sems = pltpu.SemaphoreType.DMA((len(SAME_CORE),))
    return pl.pallas_call(
        body, name=name,
        out_shape=(sems, sems, pltpu.HBM(srcs[0].shape, srcs[0].dtype), pltpu.HBM(lands[0].shape, lands[0].dtype)),
        in_specs=(HBM_SPEC, HBM_SPEC, SEM_SPEC, SEM_SPEC, pl.BlockSpec(memory_space=pl.ANY)),
        out_specs=(SEM_SPEC, SEM_SPEC, HBM_SPEC, HBM_SPEC), input_output_aliases={0: 2, 1: 3},
        compiler_params=pltpu.CompilerParams(has_side_effects=DATAFLOW),
    )(srcs[0], lands[0], send_a[0], recv_a[0], after)


def _forward_wait(started, forwarded, name):
    send_a, recv_a, _, _, _ = started
    send_b, recv_b, src, land = forwarded

    def body(src_ref, land_ref, send_a, recv_a, send_b, recv_b, src_out, land_out):
        me = _my_place()
        _own_copy(src_ref, land_ref, recv_a, me, True).wait()
        for k in (1,) + SAME_CORE:
            _split_copy(src_ref, land_ref, send_a, recv_a, me, k, True).wait_send()
        _split_copy(src_ref, land_ref, send_a, recv_a, me, 1, True).wait_recv()
        for j in range(len(SAME_CORE)):
            _forward_copy(land_ref, send_b, recv_b, me, j, False).wait_send()
            _forward_copy(land_ref, send_b, recv_b, me, j, True).wait_recv()

    return pl.pallas_call(
        body, name=name,
        out_shape=(pltpu.HBM(src.shape, src.dtype), pltpu.HBM(land.shape, land.dtype)),
        in_specs=(HBM_SPEC, HBM_SPEC, SEM_SPEC, SEM_SPEC, SEM_SPEC, SEM_SPEC),
        out_specs=(HBM_SPEC, HBM_SPEC), input_output_aliases={0: 0, 1: 1},
        compiler_params=pltpu.CompilerParams(has_side_effects=DATAFLOW),
    )(src, land, send_a[0], recv_a[0], send_b, recv_b)[1]


def _adam_math(w, g, m, v):
    m2 = ADAM_B1 * m + (1.0 - ADAM_B1) * g
    v2 = ADAM_B2 * v + (1.0 - ADAM_B2) * (g * g)
    m_hat = m2 / (1.0 - ADAM_B1 ** ADAM_STEP)
    v_hat = v2 / (1.0 - ADAM_B2 ** ADAM_STEP)
    delta = -ADAM_LR * (m_hat / (jnp.sqrt(v_hat) + ADAM_EPS) + ADAM_WD * w)
    return delta, m2, v2


def _adamw(lands, w, m, v, name):
    a, b = w.shape
    bp = lands[0].shape[2]
    ta = min(128, a)
    per = a // len(lands) // ta

    def body(*refs):
        p_refs = refs[:len(lands)]
        w_ref, m_ref, v_ref, g_ref, d_ref, m2_ref, v2_ref = refs[len(lands):]
        i = pl.program_id(0)

        def run(p_ref):
            g = p_ref[0, :, 0:b].astype(F32)
            for k in range(1, N_DEV):
                g = g + p_ref[k, :, 0:b].astype(F32)
            delta, m2, v2 = _adam_math(w_ref[...], g, m_ref[...], v_ref[...])
            g_ref[...] = g
            d_ref[...] = delta
            m2_ref[...] = m2
            v2_ref[...] = v2

        for part, p_ref in enumerate(p_refs):
            pl.when((i >= part * per) & (i < (part + 1) * per))(functools.partial(run, p_ref))

    land_spec = lambda part: pl.BlockSpec((N_DEV, ta, bp), lambda i: (0, jnp.clip(i - part * per, 0, per - 1), 0))
    blk = pl.BlockSpec((ta, b), lambda i: (i, 0))
    sd = jax.ShapeDtypeStruct((a, b), F32)
    return pl.pallas_call(
        body, name=name, grid=(a // ta,),
        in_specs=[land_spec(part) for part in range(len(lands))] + [blk, blk, blk],
        out_specs=[blk, blk, blk, blk], out_shape=[sd, sd, sd, sd],
        compiler_params=_params(("parallel",)),
    )(*lands, w, m, v)


def _bucket_table():
    t_loc = jnp.arange(SWA_BLOCK)[:, None] + SWA_BLOCK
    s_loc = jnp.arange(2 * SWA_BLOCK)[None, :]
    dist = t_loc - s_loc
    max_exact = REL_BUCKETS // 2
    d = jnp.maximum(dist, 0)
    df = jnp.maximum(d, 1).astype(F32)
    large = max_exact + (jnp.log(df / max_exact) / math.log(REL_MAX_DIST / max_exact) * (REL_BUCKETS - max_exact)).astype(jnp.int32)
    large = jnp.minimum(large, REL_BUCKETS - 1)
    bucket = jnp.where(d < max_exact, d, large)
    band = (dist >= 0) & (dist < SWA_BLOCK)
    return bucket, band


def _tile2(g):
    return jnp.concatenate([g, g], axis=1) if g.shape[1] == HEAD else g


SHARD_W = 737
SHARD_WP = 768
IN_WIDTH = N_DEV * SHARD_W
SEGMENTS = ((GL0, 2824, 3072), (QF0, 768, 512), (KF0, 1280, 512), (VF0, 1792, 512), (QM0, 2312, 512),
            (QA0, 0, 512), (KA0, 512, 128), (VA0, 640, 128), (FL0, 2304, 8))


def _lane_plan(sources):
    plan = []
    for t in range(len(sources) // 128):
        groups = {}
        for lane in range(128):
            src = sources[128 * t + lane]
            if src is not None:
                slab, col = src
                groups.setdefault((slab, col // 128, (lane - col) % 128), []).append(lane)
        tile = []
        for key, lanes in groups.items():
            assert lanes == list(range(lanes[0], lanes[-1] + 1))
            tile.append((key, lanes[0], lanes[-1] + 1))
        plan.append(tile)
    return plan


def _assemble(tile_plan, load, rows):
    lane = lax.broadcasted_iota(jnp.int32, (1, 128), 1)
    out = jnp.zeros((rows, 128), F32)
    for (slab, st, roll), lo, hi in tile_plan:
        v = load(slab, st)
        if roll:
            v = pltpu.roll(v, roll, 1)
        out = v if (lo, hi) == (0, 128) else jnp.where((lane >= lo) & (lane < hi), v, out)
    return out


def _w_in_from_shards(land):
    ref_col = [None] * PROJ_W
    for p0, r0, n in SEGMENTS:
        for i in range(n):
            ref_col[p0 + i] = divmod(r0 + i, SHARD_W)
    plan = _lane_plan(ref_col)
    d_model = land.shape[1]
    tm = 256

    def body(land_ref, o_ref):
        load = lambda slab, st: land_ref[slab, :, st * 128:(st + 1) * 128].astype(F32)
        for t, tile_plan in enumerate(plan):
            o_ref[:, t * 128:(t + 1) * 128] = _assemble(tile_plan, load, tm).astype(BF16)

    return pl.pallas_call(
        body, name="w_in_from_shards", grid=(d_model // tm,),
        in_specs=[pl.BlockSpec((N_DEV, tm, SHARD_WP), lambda i: (0, i, 0))],
        out_specs=pl.BlockSpec((tm, PROJ_W), lambda i: (i, 0)),
        out_shape=jax.ShapeDtypeStruct((d_model, PROJ_W), BF16),
        compiler_params=_params(("parallel",)),
    )(land)


def _dw_in_to_parts(dwp, name):
    padded_col = [None] * IN_WIDTH
    for p0, r0, n in SEGMENTS:
        for i in range(n):
            padded_col[r0 + i] = p0 + i
    sources = []
    for d in range(N_DEV):
        sources += [(0, padded_col[SHARD_W * d + c]) if c < SHARD_W else None for c in range(SHARD_WP)]
    plan = _lane_plan(sources)
    d_model = dwp.shape[0]
    tm = 256
    tiles = SHARD_WP // 128

    def body(dw_ref, o_ref):
        load = lambda slab, st: dw_ref[:, st * 128:(st + 1) * 128].astype(F32)
        for t, tile_plan in enumerate(plan):
            d, c = divmod(t, tiles)
            o_ref[d, :, c * 128:(c + 1) * 128] = _assemble(tile_plan, load, tm).astype(BF16)

    return pl.pallas_call(
        body, name=name, grid=(d_model // tm,),
        in_specs=[pl.BlockSpec((tm, PROJ_W), lambda i: (i, 0))],
        out_specs=pl.BlockSpec((N_DEV, tm, SHARD_WP), lambda i: (0, i, 0)),
        out_shape=jax.ShapeDtypeStruct((N_DEV, d_model, SHARD_WP), BF16),
        compiler_params=_params(("parallel",)),
    )(dwp)


def _cast_shards(shards):
    names = list(shards)

    def body(*refs):
        for src, dst in zip(refs[:len(names)], refs[len(names):]):
            if dst.shape != src.shape:
                dst[...] = jnp.zeros(dst.shape, BF16)
                dst[:, 0:src.shape[1]] = src[...].astype(BF16)
            else:
                dst[...] = src[...].astype(BF16)

    out_shape = [jax.ShapeDtypeStruct((shards[n].shape[0], SHARD_WP if n == "w_in" else shards[n].shape[1]), BF16)
                 for n in names]
    outs = pl.pallas_call(body, name="cast_shards", out_shape=out_shape,
                          compiler_params=pltpu.CompilerParams(vmem_limit_bytes=VMEM_LIMIT))(*[shards[n] for n in names])
    return dict(zip(names, outs))


def _tie(x, *tokens):
    for t in tokens:
        if t is not None:
            x = x + t[0:1, 0:1]
    return x


def _local_step(x, mem, target, p, getw, emit, deps=()):
    s = x.shape[0]
    bucket, band = _bucket_table()
    bucket_m = jnp.where(band, bucket, -1).astype(jnp.int32)
    bias = _bias_table(p["rel_bias"], bucket_m)
    bucket_t = jnp.transpose(bucket_m)
    bias_t = _bias_table(p["rel_bias"], bucket_t)
    gqf, gkf, gqa, gka = _tile2(p["qn_fox"]), _tile2(p["kn_fox"]), _tile2(p["qn_swa"]), _tile2(p["kn_swa"])
    gqm = p["qn_mem"]
    bf128 = jnp.pad(p["b_forget"], ((0, 0), (0, 120)))
    sink = p["sink_swa"].reshape(8)

    h = _rms_fwd(x, p["g_mix"], "rms_mix", tuple(deps) + (bias, bias_t))
    w_in = getw("w_in", h)
    proj = _mm(h, w_in, "nn", BF16, 512, 1536, 1024, "proj")
    fl = _mm(h, w_in[:, FL0:FL0 + 128], "nn", F32, 512, 128, 1024, "proj_fl")
    qf, kf, vf, qm, qa, ka, va, qf_t, vf_t = _proj_post(proj, gqf, gkf, gqm, gqa, gka)
    cc4, ca4 = _fox_gate_fwd(fl, bf128)
    w_kv = getw("w_mem_kv", cc4)
    mem_n, kv_raw, mk, mv = _memkv_fwd(mem, p["g_mem"], w_kv, p["kn_mem"])
    kp = jnp.pad(ka, ((SWA_BLOCK, 0), (0, 0)))
    vp = jnp.pad(va, ((SWA_BLOCK, 0), (0, 0)))
    oa = _swa_fwd(qa, kp, vp, bias, sink)
    of, lse4, of_t = _fox_fwd(qf, kf, vf_t, ca4)
    om = _mem_fwd(qm, mk, mv)
    wa, wf, wm, w_out = getw("w_o_swa", oa), getw("w_o_fox", oa), getw("w_o_mem", oa), getw("w_out", oa)
    x1, hm, merged = _merge_fwd(x, oa, of, om, proj, p["b_gate"], wa, wf, wm, w_out, p["g_mlp"])
    w_up = getw("w_mlp_up", of)
    u = _mlp_up(hm, w_up)
    w_down = getw("w_mlp_down", hm)
    dy, dy_b, loss = _mlp_down_loss(u, w_down, x1, target)

    da = _mlp_bwd_act(dy_b, w_down, u)
    t_down = emit({"w_mlp_down": _mm(u, dy_b, "tn", BF16, 1024, 1024, 2048, "dw_down")})
    dx1, dg_mlp = _mlp_bwd_x(da, w_up, x1, dy, _tie(p["g_mlp"], t_down))
    t_up = emit({"w_mlp_up": _mm(hm, da, "tn", BF16, 1024, 1024, 2048, "dw_up", column_chunks=True)})
    dproj, doa, dof_t, dom, dya, dyf, dym, db_gate = _merge_bwd(
        dx1, oa, of, om, proj, _tie(p["b_gate"], t_up), wa, wf, wm, w_out)
    dw_oa, dw_of, dw_om = _mm_tn3([oa, of, om], [dya, dyf, dym], "dw_o")
    t_o = emit({"w_out": _mm(merged, dx1, "tn", BF16, 1024, 1024, 2048, "dw_out"),
                "w_o_swa": dw_oa, "w_o_fox": dw_of, "w_o_mem": dw_om})

    dqm, dmk, dmv = _mem_bwd(qm, mk, mv, dom)
    dw_kv, dkn_mem, dg_mem = _memkv_bwd(dmk, dmv, kv_raw, _tie(p["kn_mem"], t_o), mem, p["g_mem"], mem_n, w_kv)
    t_kv = emit({"w_mem_kv": dw_kv})
    dqa, dkp, dvp, dbias, dsink = _swa_bwd(qa, kp, vp, bias_t, _tie(p["sink_swa"], t_kv).reshape(8), doa)
    dqf_t, dkf, dvf, dck4, dcq4 = _fox_bwd(qf_t, kf, vf, dof_t, of_t, cc4, lse4)

    dfl, db_forget = _fox_gate_bwd(dcq4, dck4, fl, bf128)

    dproj, dgn = _proj_pre_bwd(dproj, proj, dqf_t, dkf, dvf, dqm, dqa, dkp[SWA_BLOCK:], dvp[SWA_BLOCK:], dfl,
                               gqf, gkf, gqm, gqa, gka)
    t_in = emit({"w_in_a": _mm(h, dproj, "tn", BF16, 512, 3072, 1024, "dw_in_a", m_part=(0, 2))})
    t_in = emit({"w_in_b": _mm(h, dproj, "tn", BF16, 512, 3072, 1024, "dw_in_b", m_part=(1, 2), after=t_in)})
    grad_x, dg_mix = _in_bwd_x(dproj, w_in, x, _tie(p["g_mix"], t_in), dx1)
    d_rel = _rel_bias_bwd(dbias, bucket_t)

    fold = lambda r: dgn[r:r + 1, 0:HEAD] + dgn[r:r + 1, HEAD:128]
    small = {
        "g_mix": dg_mix, "b_gate": db_gate, "b_forget": db_forget[:, 0:8],
        "qn_swa": fold(3), "kn_swa": fold(4), "sink_swa": dsink[:, 0].reshape(1, 8), "rel_bias": d_rel,
        "qn_fox": fold(0), "kn_fox": fold(1), "g_mem": dg_mem, "qn_mem": dgn[2:3, :], "kn_mem": dkn_mem,
        "g_mlp": dg_mlp,
    }
    return loss, grad_x, small


SMALL = ("g_mix", "b_gate", "b_forget", "qn_swa", "kn_swa", "sink_swa", "rel_bias", "qn_fox", "kn_fox", "g_mem",
         "qn_mem", "kn_mem", "g_mlp")
BIG = ("w_in", "w_mem_kv", "w_o_swa", "w_o_fox", "w_o_mem", "w_out", "w_mlp_up", "w_mlp_down")
COL_SHARDED = ("w_in", "w_o_swa", "w_o_fox", "w_o_mem", "w_mlp_up")
WEIGHTS = ("g_mix", "w_in", "b_gate", "b_forget", "qn_swa", "kn_swa", "sink_swa", "rel_bias", "qn_fox", "kn_fox", "g_mem",
           "w_mem_kv", "qn_mem", "kn_mem", "w_o_swa", "w_o_fox", "w_o_mem", "w_out", "g_mlp", "w_mlp_up", "w_mlp_down")
SMALL_SLOTS = (("g_mix", 1024), ("b_gate", 3072), ("b_forget", 128), ("qn_swa", 128), ("kn_swa", 128), ("sink_swa", 128),
               ("rel_bias", REL_BUCKETS * 128), ("qn_fox", 128), ("kn_fox", 128), ("g_mem", 1024), ("qn_mem", 128),
               ("kn_mem", 128), ("g_mlp", 1024), ("loss", 128))
SMALL_OFF = {n: sum(w for _, w in SMALL_SLOTS[:i]) for i, (n, _) in enumerate(SMALL_SLOTS)}
SMALL_ROW = sum(w for _, w in SMALL_SLOTS)


def _gathered_to_full(name, g):
    if name in COL_SHARDED:
        return jnp.transpose(g, (1, 0, 2)).reshape(g.shape[1], N_DEV * g.shape[2])
    return g.reshape(N_DEV * g.shape[1], g.shape[2])


def _full_to_parts(name, full, b):
    if name in COL_SHARDED:
        return jnp.transpose(full.reshape(full.shape[0], N_DEV, b), (1, 0, 2)).astype(BF16)
    return full.reshape(N_DEV, full.shape[0] // N_DEV, full.shape[1]).astype(BF16)


def _pack_small(grads, loss):
    pieces = []
    for n, width in SMALL_SLOTS:
        a = loss.reshape(1, 1) if n == "loss" else grads[n].reshape(1, -1)
        pieces.append(jnp.pad(a, ((0, 0), (0, width - a.shape[1]))))
    return jnp.concatenate(pieces, axis=1)


def _adamw_small(gathered, w, m, v):
    names = list(SMALL)

    def body(*refs):
        p_ref = refs[0]
        ins = refs[1:1 + 3 * len(names)]
        outs = refs[1 + 3 * len(names):]
        g_all = p_ref[0]
        for k in range(1, N_DEV):
            g_all = g_all + p_ref[k]
        for i, n in enumerate(names):
            w_ref, m_ref, v_ref = ins[3 * i:3 * i + 3]
            out = outs[4 * i:4 * i + 4]
            rows, cols = w_ref.shape
            for r in range(rows):
                off = SMALL_OFF[n] + 128 * r
                g = g_all[:, off:off + cols]
                rs = slice(r, r + 1)
                res = (g,) + _adam_math(w_ref[rs, :], g, m_ref[rs, :], v_ref[rs, :])
                for o_ref, val in zip(out, res):
                    o_ref[rs, :] = val
        outs[-1][...] = g_all[:, SMALL_OFF["loss"]:SMALL_OFF["loss"] + 128]

    args = [gathered]
    out_shape = []
    for n in names:
        args += [w[n], m[n], v[n]]
        out_shape += [jax.ShapeDtypeStruct(w[n].shape, F32)] * 4
    out_shape.append(jax.ShapeDtypeStruct((1, 128), F32))
    outs = pl.pallas_call(body, name="adamw_small", out_shape=out_shape)(*args)
    return {n: outs[4 * i:4 * i + 4] for i, n in enumerate(names)}, outs[-1]


def kernel(x, mem, g_mix, w_in, b_gate, b_forget, qn_swa, kn_swa, sink_swa, rel_bias, qn_fox, kn_fox, g_mem, w_mem_kv, qn_mem, kn_mem, w_o_swa, w_o_fox, w_o_mem, w_out, g_mlp, w_mlp_up, w_mlp_down, loss_target, m_g_mix, m_w_in, m_b_gate, m_b_forget, m_qn_swa, m_kn_swa, m_sink_swa, m_rel_bias, m_qn_fox, m_kn_fox, m_g_mem, m_w_mem_kv, m_qn_mem, m_kn_mem, m_w_o_swa, m_w_o_fox, m_w_o_mem, m_w_out, m_g_mlp, m_w_mlp_up, m_w_mlp_down, v_g_mix, v_w_in, v_b_gate, v_b_forget, v_qn_swa, v_kn_swa, v_sink_swa, v_rel_bias, v_qn_fox, v_kn_fox, v_g_mem, v_w_mem_kv, v_qn_mem, v_kn_mem, v_w_o_swa, v_w_o_fox, v_w_o_mem, v_w_out, v_g_mlp, v_w_mlp_up, v_w_mlp_down):
    wts = dict(g_mix=g_mix, w_in=w_in, b_gate=b_gate, b_forget=b_forget, qn_swa=qn_swa, kn_swa=kn_swa, sink_swa=sink_swa,
               rel_bias=rel_bias, qn_fox=qn_fox, kn_fox=kn_fox, g_mem=g_mem, w_mem_kv=w_mem_kv, qn_mem=qn_mem, kn_mem=kn_mem,
               w_o_swa=w_o_swa, w_o_fox=w_o_fox, w_o_mem=w_o_mem, w_out=w_out, g_mlp=g_mlp, w_mlp_up=w_mlp_up,
               w_mlp_down=w_mlp_down)
    mom = dict(g_mix=m_g_mix, w_in=m_w_in, b_gate=m_b_gate, b_forget=m_b_forget, qn_swa=m_qn_swa, kn_swa=m_kn_swa,
               sink_swa=m_sink_swa, rel_bias=m_rel_bias, qn_fox=m_qn_fox, kn_fox=m_kn_fox, g_mem=m_g_mem, w_mem_kv=m_w_mem_kv,
               qn_mem=m_qn_mem, kn_mem=m_kn_mem, w_o_swa=m_w_o_swa, w_o_fox=m_w_o_fox, w_o_mem=m_w_o_mem, w_out=m_w_out,
               g_mlp=m_g_mlp, w_mlp_up=m_w_mlp_up, w_mlp_down=m_w_mlp_down)
    var = dict(g_mix=v_g_mix, w_in=v_w_in, b_gate=v_b_gate, b_forget=v_b_forget, qn_swa=v_qn_swa, kn_swa=v_kn_swa,
               sink_swa=v_sink_swa, rel_bias=v_rel_bias, qn_fox=v_qn_fox, kn_fox=v_kn_fox, g_mem=v_g_mem, w_mem_kv=v_w_mem_kv,
               qn_mem=v_qn_mem, kn_mem=v_kn_mem, w_o_swa=v_w_o_swa, w_o_fox=v_w_o_fox, w_o_mem=v_w_o_mem, w_out=v_w_out,
               g_mlp=v_g_mlp, w_mlp_up=v_w_mlp_up, w_mlp_down=v_w_mlp_down)

    shards = _cast_shards({n: wts[n][0] for n in BIG})
    first = _split_start([shards["w_in"]], True, "ag_start_w_in", peers=(1,) + SAME_CORE)
    rest = _split_start([shards[n] for n in BIG[1:]], True, "ag_start_rest", after=first[4])
    full = {}

    def getw(n, after):
        if n == "w_in" and n not in full:
            forwarded = _forward_start(first, after, "ag_forward_w_in")
            full[n] = _w_in_from_shards(_forward_wait(first, forwarded, "ag_wait_w_in"))
        elif n not in full:
            land = _split_wait(rest, BIG[1:].index(n), after, True, "ag_wait_" + n)
            full[n] = land if n == "w_mlp_up" else _gathered_to_full(n, land)
        return full[n]

    exchanges = {}

    def emit(grads_by_name):
        parts = []
        for n, grad in grads_by_name.items():
            if n.startswith("w_in"):
                parts.append(_dw_in_to_parts(grad, "d" + n + "_to_parts"))
            else:
                parts.append(grad if n == "w_mlp_up" else _full_to_parts(n, grad, wts[n].shape[2]))
        started = _split_start(parts, False, "rs_start_" + next(iter(grads_by_name)))
        for w, n in enumerate(grads_by_name):
            exchanges[n] = (started, w)
        return started[4]

    small_p = {n: wts[n] for n in SMALL}
    loss, grad_x, small_g = _local_step(x[0], mem[0], loss_target[0], small_p, getw, emit, (first[4], rest[4]))

    packed = _pack_small(small_g, loss)
    small_gather = _split_start([packed], True, "ag_start_small")

    grads, delta, new_m, new_v = {}, {}, {}, {}

    def update(n, after):
        lands = [_split_wait(*exchanges[e], after, False, "rs_wait_" + e) for e in exchanges if e.startswith(n)]
        g, d, m2, v2 = _adamw(lands, wts[n][0], mom[n][0], var[n][0], "adamw_" + n)
        grads[n], delta[n], new_m[n], new_v[n] = g[None], d[None], m2[None], v2[None]
        return d

    after = small_gather[4]
    for n in exchanges:
        if not n.startswith("w_in"):
            after = update(n, after)

    gathered = _split_wait(small_gather, 0, after, True, "ag_wait_small")
    small_out, total = _adamw_small(gathered, small_p, mom, var)
    for name, (g, d, m2, v2) in small_out.items():
        grads[name], delta[name], new_m[name], new_v[name] = g, d, m2, v2
    update("w_in", total)

    return (total[0, 0], grad_x[None], *[grads[n] for n in WEIGHTS], *[delta[n] for n in WEIGHTS],
            *[new_m[n] for n in WEIGHTS], *[new_v[n] for n in WEIGHTS])
```

```python
import functools
import math

import jax
import jax.numpy as jnp
from jax import lax
from jax.experimental import pallas as pl
from jax.experimental.pallas import tpu as pltpu

F32 = jnp.float32
BF16 = jnp.bfloat16

D_MODEL = 1024
N_MEM = 256
D_FF = 4096
HEAD = 64
SWA_HEADS = 8
SWA_BLOCK = 128
MEM_HEADS = 4
MEM_HEAD = 128
EPS = 1e-6
NEG = -1e30
REL_BUCKETS = 32
REL_MAX_DIST = 128

ADAM_LR = 0.001
ADAM_B1 = 0.9
ADAM_B2 = 0.999
ADAM_EPS = 1e-08
ADAM_WD = 0.01
ADAM_STEP = 10

GL0, QF0, KF0, VF0, QM0, QA0, KA0, VA0, FL0 = 0, 3072, 3584, 4096, 4608, 5120, 5632, 5760, 5888
PROJ_W = 6144
HALF_W = 3072
H_QF, H_KF, H_VF, H_QM, H_QA, H_KA, H_VA, H_FL = 0, 512, 1024, 1536, 2048, 2560, 2688, 2816

VMEM_LIMIT = 56 * 1024 * 1024
N_DEV = 8
MESH = pl.DeviceIdType.MESH

NN = (((1,), (0,)), ((), ()))
NT = (((1,), (1,)), ((), ()))
TN = (((0,), (0,)), ((), ()))


def _dot(a, b, dims=NN):
    return lax.dot_general(a, b, dims, preferred_element_type=F32)


def _params(sem):
    return pltpu.CompilerParams(dimension_semantics=sem, vmem_limit_bytes=VMEM_LIMIT)


def _full(shape):
    nd = len(shape)
    return pl.BlockSpec(shape, lambda *_: (0,) * nd)


def _sigmoid(z):
    return 1.0 / (1.0 + jnp.exp(-z))


def _group_mean(v, hd):
    if hd == 128:
        return jnp.mean(v, axis=-1, keepdims=True)
    r = lax.broadcasted_iota(jnp.int32, (128, 128), 0) // HEAD
    c = lax.broadcasted_iota(jnp.int32, (128, 128), 1) // HEAD
    same_head = jnp.where(r == c, 1.0 / HEAD, 0.0).astype(BF16)
    total = None
    rest = v
    for _ in range(2):
        part = rest.astype(BF16)
        rest = rest - part.astype(F32)
        term = _dot(part, same_head)
        total = term if total is None else total + term
    return total


def _mm(a, b, mode, out_dtype, tm, tn, tk, name, column_chunks=False, m_part=(0, 1), after=None):
    if mode == "nn":
        m, k = a.shape
        n = b.shape[1]
    elif mode == "nt":
        m, k = a.shape
        n = b.shape[0]
    else:
        k, m = a.shape
        n = b.shape[1]
    assert mode == "tn" or m_part == (0, 1)
    m //= m_part[1]
    tm, tn, tk = min(tm, m), min(tn, n), min(tk, k)
    m0 = m_part[0] * (m // tm)
    extra = [] if after is None else [after]
    nk = k // tk
    chunk = n // N_DEV
    per_tile = tn // chunk if column_chunks else 1
    dims = {"nn": NN, "nt": NT, "tn": TN}[mode]
    a_spec = pl.BlockSpec((tk, tm), lambda j, i, kk: (kk, m0 + i)) if mode == "tn" else pl.BlockSpec((tm, tk), lambda j, i, kk: (i, kk))
    b_spec = pl.BlockSpec((tn, tk), lambda j, i, kk: (j, kk)) if mode == "nt" else pl.BlockSpec((tk, tn), lambda j, i, kk: (kk, j))

    def body(a_ref, b_ref, *rest):
        o_ref, *acc = rest[len(extra):]
        prod = _dot(a_ref[...].astype(BF16), b_ref[...].astype(BF16), dims)

        def write(res):
            if column_chunks:
                for c in range(per_tile):
                    o_ref[c] = res[:, c * chunk:(c + 1) * chunk].astype(o_ref.dtype)
            else:
                o_ref[...] = res.astype(o_ref.dtype)

        if nk == 1:
            write(prod)
        else:
            acc_ref, = acc
            kk = pl.program_id(2)

            @pl.when(kk == 0)
            def _():
                acc_ref[...] = prod

            @pl.when(kk > 0)
            def _():
                acc_ref[...] += prod

            @pl.when(kk == nk - 1)
            def _():
                write(acc_ref[...])

    return pl.pallas_call(
        body, name=name, grid=(n // tn, m // tm, nk),
        in_specs=[a_spec, b_spec] + [pl.BlockSpec(memory_space=pl.ANY)] * len(extra),
        out_specs=(pl.BlockSpec((per_tile, tm, chunk), lambda j, i, kk: (j, i, 0)) if column_chunks
                   else pl.BlockSpec((tm, tn), lambda j, i, kk: (i, j))),
        out_shape=jax.ShapeDtypeStruct((N_DEV, m, chunk) if column_chunks else (m, n), out_dtype),
        scratch_shapes=[pltpu.VMEM((tm, tn), F32)] if nk > 1 else [],
        compiler_params=_params(("parallel", "parallel", "arbitrary")),
    )(a, b, *extra)


def _mm_tn3(a_list, b_list, name):
    s, m = a_list[0].shape
    n = b_list[0].shape[1]
    tk = min(2048, s)
    nk = s // tk

    def body(*refs):
        a_refs, b_refs, o_refs, acc_refs = refs[0:3], refs[3:6], refs[6:9], refs[9:12]
        kk = pl.program_id(0)
        for a_ref, b_ref, o_ref, acc_ref in zip(a_refs, b_refs, o_refs, acc_refs):
            prod = _dot(a_ref[...], b_ref[...], TN)
            if nk == 1:
                o_ref[...] = prod.astype(o_ref.dtype)
                continue

            @pl.when(kk == 0)
            def _(acc_ref=acc_ref, prod=prod):
                acc_ref[...] = prod

            @pl.when(kk > 0)
            def _(acc_ref=acc_ref, prod=prod):
                acc_ref[...] += prod

            @pl.when(kk == nk - 1)
            def _(acc_ref=acc_ref, o_ref=o_ref):
                o_ref[...] = acc_ref[...].astype(o_ref.dtype)

    return pl.pallas_call(
        body, name=name, grid=(nk,),
        in_specs=[pl.BlockSpec((tk, m), lambda kk: (kk, 0))] * 3 + [pl.BlockSpec((tk, n), lambda kk: (kk, 0))] * 3,
        out_specs=[_full((m, n))] * 3,
        out_shape=[jax.ShapeDtypeStruct((m, n), BF16)] * 3,
        scratch_shapes=[pltpu.VMEM((m, n), F32)] * 3,
        compiler_params=_params(("arbitrary",)),
    )(*a_list, *b_list)


def _rms_fwd(x, g, name, deps=()):
    s, d = x.shape
    tm = min(512, s)

    def body(x_ref, g_ref, *rest):
        h_ref = rest[len(deps)]
        xv = x_ref[...]
        r = lax.rsqrt(jnp.mean(xv * xv, axis=-1, keepdims=True) + EPS)
        h_ref[...] = (xv * r * g_ref[...]).astype(BF16)

    return pl.pallas_call(
        body, name=name, grid=(s // tm,),
        in_specs=[pl.BlockSpec((tm, d), lambda i: (i, 0)), _full((1, d))] + [pl.BlockSpec(memory_space=pl.ANY)] * len(deps),
        out_specs=pl.BlockSpec((tm, d), lambda i: (i, 0)),
        out_shape=jax.ShapeDtypeStruct((s, d), BF16),
        compiler_params=_params(("parallel",)),
    )(x, g, *deps)


def _proj_post(proj, gq_fox, gk_fox, gq_mem, gq_swa, gk_swa):
    s = proj.shape[0]
    tm = min(512, s)

    def body(p_ref, gqf, gkf, gqm, gqa, gka, qf_ref, kf_ref, vf_ref, qm_ref, qa_ref, ka_ref, va_ref, qft_ref, vft_ref):
        def norm(off, width, hd, g_ref, o_ref, scaled_t_ref=None):
            for b in range(width // 128):
                v = p_ref[:, off + b * 128: off + (b + 1) * 128].astype(F32)
                r = lax.rsqrt(_group_mean(v * v, hd) + EPS)
                vn = (v * r * g_ref[...]).astype(BF16)
                o_ref[:, b * 128:(b + 1) * 128] = vn
                if scaled_t_ref is not None:
                    scaled_t_ref[b * 128:(b + 1) * 128, :] = (vn.astype(F32) * 0.125).T.astype(BF16)

        norm(H_QF, 512, HEAD, gqf, qf_ref, qft_ref)
        norm(H_KF, 512, HEAD, gkf, kf_ref)
        vf_ref[...] = p_ref[:, H_VF:H_VF + 512].astype(BF16)
        for b in range(4):
            vft_ref[b * 128:(b + 1) * 128, :] = p_ref[:, H_VF + b * 128:H_VF + (b + 1) * 128].astype(F32).T.astype(BF16)
        norm(H_QM, 512, MEM_HEAD, gqm, qm_ref)
        norm(H_QA, 512, HEAD, gqa, qa_ref)
        norm(H_KA, 128, HEAD, gka, ka_ref)
        va_ref[...] = p_ref[:, H_VA:H_VA + 128].astype(BF16)

    g_spec = _full((1, 128))
    o512 = pl.BlockSpec((tm, 512), lambda i: (i, 0))
    o128 = pl.BlockSpec((tm, 128), lambda i: (i, 0))
    s512 = jax.ShapeDtypeStruct((s, 512), BF16)
    s128 = jax.ShapeDtypeStruct((s, 128), BF16)
    return pl.pallas_call(
        body, name="proj_post", grid=(s // tm,),
        in_specs=[pl.BlockSpec((tm, HALF_W), lambda i: (i, 1)), g_spec, g_spec, g_spec, g_spec, g_spec],
        out_specs=[o512, o512, o512, o512, o512, o128, o128] + [pl.BlockSpec((512, tm), lambda i: (0, i))] * 2,
        out_shape=[s512, s512, s512, s512, s512, s128, s128] + [jax.ShapeDtypeStruct((512, s), BF16)] * 2,
        compiler_params=_params(("parallel",)),
    )(proj, gq_fox, gk_fox, gq_mem, gq_swa, gk_swa)


def _tri(n, lower):
    r = lax.broadcasted_iota(jnp.int32, (n, n), 0)
    c = lax.broadcasted_iota(jnp.int32, (n, n), 1)
    return jnp.where((c <= r) if lower else (c >= r), 1.0, 0.0).astype(F32)


def _fox_gate_fwd(proj, b_forget128):
    s = proj.shape[0]
    tm = min(512, s)

    def body(p_ref, b_ref, cc_ref, ca_ref, carry_ref):
        i = pl.program_id(0)

        @pl.when(i == 0)
        def _():
            carry_ref[...] = jnp.zeros_like(carry_ref)

        z = p_ref[...] + b_ref[...]
        logf = jnp.minimum(z, 0.0) - jnp.log(1.0 + jnp.exp(-jnp.abs(z)))
        c = jnp.dot(_tri(tm, True), logf, precision=lax.Precision.HIGHEST, preferred_element_type=F32) + carry_ref[...]
        carry_ref[...] = c[tm - 1:tm, :]
        lane = lax.broadcasted_iota(jnp.int32, (tm, 128), 1)
        for hp in range(4):
            cc_ref[hp] = c if hp == 0 else pltpu.roll(c, 128 - 2 * hp, 1)
            aug = jnp.zeros((tm, 128), F32)
            for e in range(2):
                rest = jnp.broadcast_to(c[:, 2 * hp + e:2 * hp + e + 1], (tm, 128))
                for part in range(3):
                    piece = rest.astype(BF16).astype(F32)
                    aug = jnp.where(lane == HEAD * (1 - e) + part, piece, aug)
                    rest = rest - piece
            ca_ref[hp] = aug.astype(BF16)

    return pl.pallas_call(
        body, name="fox_gate_fwd", grid=(s // tm,),
        in_specs=[pl.BlockSpec((tm, 128), lambda i: (i, 0)), _full((1, 128))],
        out_specs=[pl.BlockSpec((4, tm, 128), lambda i: (0, i, 0))] * 2,
        out_shape=[jax.ShapeDtypeStruct((4, s, 128), F32), jax.ShapeDtypeStruct((4, s, 128), BF16)],
        scratch_shapes=[pltpu.VMEM((1, 128), F32)],
        compiler_params=_params(("arbitrary",)),
    )(proj, b_forget128)


def _memkv_fwd(mem, g_mem, w_kv, kn_mem):
    m = mem.shape[0]

    def body(mem_ref, g_ref, w_ref, kn_ref, memn_ref, kv_ref, mk_ref, mv_ref):
        xv = mem_ref[...]
        r = lax.rsqrt(jnp.mean(xv * xv, axis=-1, keepdims=True) + EPS)
        mn = (xv * r * g_ref[...]).astype(BF16)
        memn_ref[...] = mn
        kv = _dot(mn, w_ref[...])
        kv_ref[...] = kv
        for h in range(MEM_HEADS):
            v = kv[:, h * 128:(h + 1) * 128]
            rr = lax.rsqrt(jnp.mean(v * v, axis=-1, keepdims=True) + EPS)
            mk_ref[:, h * 128:(h + 1) * 128] = (v * rr * kn_ref[...]).astype(BF16)
        mv_ref[...] = kv[:, 512:1024].astype(BF16)

    return pl.pallas_call(
        body, name="memkv_fwd",
        out_shape=[jax.ShapeDtypeStruct((m, D_MODEL), BF16), jax.ShapeDtypeStruct((m, 1024), F32),
                   jax.ShapeDtypeStruct((m, 512), BF16), jax.ShapeDtypeStruct((m, 512), BF16)],
        compiler_params=pltpu.CompilerParams(vmem_limit_bytes=VMEM_LIMIT),
    )(mem, g_mem, w_kv, kn_mem)


def _bias_table(rel_bias, bucket):
    def body(rb_ref, bk_ref, o_ref):
        bk = bk_ref[...]
        for h in range(SWA_HEADS):
            acc = jnp.zeros(bk.shape, F32)
            for b in range(REL_BUCKETS):
                acc = jnp.where(bk == b, rb_ref[b, h], acc)
            o_ref[h] = acc

    return pl.pallas_call(
        body, name="bias_table",
        in_specs=[pl.BlockSpec(memory_space=pltpu.SMEM), pl.BlockSpec(memory_space=pltpu.VMEM)],
        out_shape=jax.ShapeDtypeStruct((SWA_HEADS,) + bucket.shape, F32),
    )(rel_bias, bucket)


def _swa_valid(n):
    row = lax.broadcasted_iota(jnp.int32, (SWA_BLOCK, 2 * SWA_BLOCK), 0)
    col = lax.broadcasted_iota(jnp.int32, (SWA_BLOCK, 2 * SWA_BLOCK), 1)
    dist = row + SWA_BLOCK - col
    return (dist >= 0) & (dist < SWA_BLOCK) & ((col >= SWA_BLOCK) | (n > 0))


def _swa_fwd(qa, kp, vp, bias, sink):
    s = qa.shape[0]
    nb = s // SWA_BLOCK

    def body(sink_ref, q_ref, kp_ref, vp_ref, bias_ref, o_ref):
        n = pl.program_id(0)
        start = pl.multiple_of(n * SWA_BLOCK, SWA_BLOCK)
        k2 = kp_ref[pl.ds(start, 2 * SWA_BLOCK), :]
        v2 = vp_ref[pl.ds(start, 2 * SWA_BLOCK), :]
        valid = _swa_valid(n)
        heads = range(SWA_HEADS)
        hs = lambda h: slice(h * HEAD, (h + 1) * HEAD)
        sc = [jnp.where(valid, _dot(q_ref[:, hs(h)], k2[:, hs(h // 4)], NT) * 0.125 + bias_ref[h], NEG) for h in heads]
        pn = []
        for h in heads:
            sk = sink_ref[h]
            mx = jnp.maximum(jnp.max(sc[h], axis=-1, keepdims=True), sk)
            p = jnp.exp(sc[h] - mx)
            den = jnp.sum(p, axis=-1, keepdims=True) + jnp.exp(sk - mx)
            pn.append((p / den).astype(BF16))
        outs = [_dot(pn[h], v2[:, hs(h // 4)]).astype(BF16) for h in heads]
        for h in heads:
            o_ref[:, hs(h)] = outs[h]

    return pl.pallas_call(
        body, name="swa_fwd", grid=(nb,),
        in_specs=[pl.BlockSpec(memory_space=pltpu.SMEM),
                  pl.BlockSpec((SWA_BLOCK, 512), lambda n: (n, 0)),
                  _full(kp.shape), _full(vp.shape), _full(bias.shape)],
        out_specs=pl.BlockSpec((SWA_BLOCK, 512), lambda n: (n, 0)),
        out_shape=jax.ShapeDtypeStruct((s, 512), BF16),
        compiler_params=_params(("parallel",)),
    )(sink, qa, kp, vp, bias)


def _head_mask(e):
    lane = lax.broadcasted_iota(jnp.int32, (1, 128), 1)
    return (lane >= e * HEAD) & (lane < (e + 1) * HEAD)


FOX_FWD_T = 1024
FOX_BWD_T = 512


def _head_rows(e):
    row = lax.broadcasted_iota(jnp.int32, (128, 1), 0)
    return (row >= e * HEAD) & (row < (e + 1) * HEAD)


def _fox_fwd(q, k, v_t, ca4):
    s = q.shape[0]
    t = min(FOX_FWD_T, s)
    nq = s // t

    def body(q_ref, k_ref, vt_ref, ca_ref, o_ref, lse_ref, ot_ref):
        i = pl.program_id(1)
        qs = q_ref[...] * jnp.asarray(0.125, BF16)
        lane = lax.broadcasted_iota(jnp.int32, (1, 128), 1)
        minus = [jnp.where((lane >= HEAD * (1 - e)) & (lane < HEAD * (1 - e) + 3), -1.0, 0.0).astype(BF16) for e in range(2)]
        qe = [jnp.where(_head_mask(e), qs, jnp.broadcast_to(minus[e], qs.shape)) for e in range(2)]

        def block(carry, key0, nkeys, q0, nqs, masked):
            ks = pl.ds(pl.multiple_of(key0, 128), nkeys)
            kj = k_ref[ks, :]
            caj = ca_ref[0, ks, :]
            vtj = vt_ref[:, ks]
            out = []
            for e in range(2):
                m_all, acc_all = carry[2 * e], carry[2 * e + 1]
                m, acc = m_all[:, q0:q0 + nqs], acc_all[:, q0:q0 + nqs]
                st = _dot(jnp.where(_head_mask(e), kj, caj), qe[e][q0:q0 + nqs, :], NT)
                if masked:
                    krow = lax.broadcasted_iota(jnp.int32, (nkeys, nqs), 0) + key0
                    qcol = lax.broadcasted_iota(jnp.int32, (nkeys, nqs), 1) + (i * t + q0)
                    st = jnp.where(krow <= qcol, st, NEG)
                m_new = jnp.maximum(m, jnp.max(st, axis=0, keepdims=True))
                alpha = jnp.exp(m - m_new)
                pt = jnp.exp(st - m_new).astype(BF16)
                vte = jnp.where(_head_rows(e), vtj, jnp.ones_like(vtj))
                acc_new = alpha * acc + _dot(vte, pt)
                if nqs < t:
                    m_new = jnp.concatenate([m_all[:, :q0], m_new], axis=1)
                    acc_new = jnp.concatenate([acc_all[:, :q0], acc_new], axis=1)
                out += [m_new, acc_new]
            return tuple(out)

        half = t // 2
        init = (jnp.full((1, t), NEG, F32), jnp.zeros((128, t), F32)) * 2
        carry = lax.fori_loop(0, i, lambda j, c: block(c, j * t, t, 0, t, False), init)
        carry = block(carry, i * t, half, 0, t, True)
        m0, a0, m1, a1 = block(carry, i * t + half, half, half, half, True)
        l0 = a0[HEAD:HEAD + 1, :]
        l1 = a1[0:1, :]
        o_t = jnp.where(_head_rows(0), a0 / l0, a1 / l1)
        o_ref[...] = o_t.T.astype(BF16)
        ot_ref[...] = o_t.astype(BF16)
        r8 = lax.broadcasted_iota(jnp.int32, (8, t), 0)
        lse_ref[0] = jnp.where(r8 == 0, m0 + jnp.log(l0), jnp.where(r8 == 1, m1 + jnp.log(l1), 0.0))

    return pl.pallas_call(
        body, name="fox_fwd", grid=(4, nq),
        in_specs=[pl.BlockSpec((t, 128), lambda hp, i: (i, hp)),
                  pl.BlockSpec((s, 128), lambda hp, i: (0, hp)),
                  pl.BlockSpec((128, s), lambda hp, i: (hp, 0)),
                  pl.BlockSpec((1, s, 128), lambda hp, i: (hp, 0, 0))],
        out_specs=[pl.BlockSpec((t, 128), lambda hp, i: (i, hp)),
                   pl.BlockSpec((1, 8, t), lambda hp, i: (hp, 0, i)),
                   pl.BlockSpec((128, t), lambda hp, i: (hp, i))],
        out_shape=[jax.ShapeDtypeStruct((s, 512), BF16), jax.ShapeDtypeStruct((4, 8, s), F32),
                   jax.ShapeDtypeStruct((512, s), BF16)],
        compiler_params=_params(("parallel", "parallel")),
    )(q, k, v_t, ca4)


MEM_SCALE = MEM_HEAD ** -0.5


def _mem_fwd(qm, mk, mv):
    s = qm.shape[0]
    tq = min(512, s)

    def body(q_ref, mk_ref, mv_ref, o_ref):
        for h in range(MEM_HEADS):
            hs = slice(h * 128, (h + 1) * 128)
            sc = _dot(q_ref[:, hs], mk_ref[:, hs], NT) * MEM_SCALE
            mx = jnp.max(sc, axis=-1, keepdims=True)
            p = jnp.exp(sc - mx)
            p = p / jnp.sum(p, axis=-1, keepdims=True)
            o_ref[:, hs] = _dot(p.astype(BF16), mv_ref[:, hs]).astype(BF16)

    return pl.pallas_call(
        body, name="mem_fwd", grid=(s // tq,),
        in_specs=[pl.BlockSpec((tq, 512), lambda i: (i, 0)), _full(mk.shape), _full(mv.shape)],
        out_specs=pl.BlockSpec((tq, 512), lambda i: (i, 0)),
        out_shape=jax.ShapeDtypeStruct((s, 512), BF16),
        compiler_params=_params(("parallel",)),
    )(qm, mk, mv)


def _merge_fwd(x, oa, of, om, proj, b_gate, wa, wf, wm, w_out, g_mlp):
    s = x.shape[0]
    tm = min(512, s)

    def body(x_ref, oa_ref, of_ref, om_ref, gl_ref, bg_ref, wa_ref, wf_ref, wm_ref, wo_ref, g_ref, x1_ref, hm_ref, mg_ref):
        merged = None
        for b, (o_ref, w_ref) in enumerate(((oa_ref, wa_ref), (of_ref, wf_ref), (om_ref, wm_ref))):
            cs = slice(b * D_MODEL, (b + 1) * D_MODEL)
            y = _dot(o_ref[...], w_ref[...])
            t = _sigmoid(gl_ref[:, cs].astype(F32) + bg_ref[:, cs]) * y
            merged = t if merged is None else merged + t
        mb = merged.astype(BF16)
        mg_ref[...] = mb
        x1 = x_ref[...] + _dot(mb, wo_ref[...])
        x1_ref[...] = x1
        r = lax.rsqrt(jnp.mean(x1 * x1, axis=-1, keepdims=True) + EPS)
        hm_ref[...] = (x1 * r * g_ref[...]).astype(BF16)

    row = lambda w: pl.BlockSpec((tm, w), lambda i: (i, 0))
    return pl.pallas_call(
        body, name="merge_fwd", grid=(s // tm,),
        in_specs=[row(D_MODEL), row(512), row(512), row(512), row(HALF_W), _full((1, HALF_W)),
                  _full(wa.shape), _full(wf.shape), _full(wm.shape), _full(w_out.shape), _full((1, D_MODEL))],
        out_specs=[row(D_MODEL), row(D_MODEL), row(D_MODEL)],
        out_shape=[jax.ShapeDtypeStruct((s, D_MODEL), F32), jax.ShapeDtypeStruct((s, D_MODEL), BF16),
                   jax.ShapeDtypeStruct((s, D_MODEL), BF16)],
        compiler_params=_params(("parallel",)),
    )(x, oa, of, om, proj, b_gate, wa, wf, wm, w_out, g_mlp)


def _mlp_up(hm, w_up):
    s = hm.shape[0]
    tm, tn = min(1024, s), w_up.shape[2]

    def body(h_ref, w_ref, u_ref):
        r = jnp.maximum(_dot(h_ref[...], w_ref[0]), 0.0)
        u_ref[...] = (r * r).astype(BF16)

    return pl.pallas_call(
        body, name="mlp_up", grid=(s // tm, D_FF // tn),
        in_specs=[pl.BlockSpec((tm, D_MODEL), lambda i, j: (i, 0)), pl.BlockSpec((1, D_MODEL, tn), lambda i, j: (j, 0, 0))],
        out_specs=pl.BlockSpec((tm, tn), lambda i, j: (i, j)),
        out_shape=jax.ShapeDtypeStruct((s, D_FF), BF16),
        compiler_params=_params(("parallel", "parallel")),
    )(hm, w_up)


def _mlp_down_loss(u, w_down, x1, target):
    s = u.shape[0]
    tm = min(256, s)

    def body(u_ref, w_ref, x1_ref, t_ref, dy_ref, dyb_ref, loss_ref):
        i = pl.program_id(0)

        @pl.when(i == 0)
        def _():
            loss_ref[...] = jnp.zeros_like(loss_ref)

        y = x1_ref[...] + _dot(u_ref[...], w_ref[...])
        err = y - t_ref[...]
        dy = err * (1.0 / D_MODEL)
        dy_ref[...] = dy
        dyb_ref[...] = dy.astype(BF16)
        part = jnp.sum(jnp.sum(err * err, axis=-1, keepdims=True) * (1.0 / D_MODEL), axis=0, keepdims=True)
        loss_ref[...] += 0.5 * part

    row = pl.BlockSpec((tm, D_MODEL), lambda i: (i, 0))
    return pl.pallas_call(
        body, name="mlp_down_loss", grid=(s // tm,),
        in_specs=[pl.BlockSpec((tm, D_FF), lambda i: (i, 0)), _full(w_down.shape), row, row],
        out_specs=[row, row, _full((1, 1))],
        out_shape=[jax.ShapeDtypeStruct((s, D_MODEL), F32), jax.ShapeDtypeStruct((s, D_MODEL), BF16),
                   jax.ShapeDtypeStruct((1, 1), F32)],
        compiler_params=_params(("arbitrary",)),
    )(u, w_down, x1, target)


def _mlp_bwd_act(dy, w_down, u):
    s = dy.shape[0]
    tm, tn = min(1024, s), 1024

    def body(dy_ref, w_ref, u_ref, da_ref):
        du = _dot(dy_ref[...], w_ref[...], NT)
        da_ref[...] = (du * (2.0 * jnp.sqrt(u_ref[...].astype(F32)))).astype(BF16)

    return pl.pallas_call(
        body, name="mlp_bwd_act", grid=(D_FF // tn, s // tm),
        in_specs=[pl.BlockSpec((tm, D_MODEL), lambda j, i: (i, 0)), pl.BlockSpec((tn, D_MODEL), lambda j, i: (j, 0)),
                  pl.BlockSpec((tm, tn), lambda j, i: (i, j))],
        out_specs=pl.BlockSpec((tm, tn), lambda j, i: (i, j)),
        out_shape=jax.ShapeDtypeStruct((s, D_FF), BF16),
        compiler_params=_params(("parallel", "parallel")),
    )(dy, w_down, u)


def _rms_bwd(xv, g, dh, skip):
    r = lax.rsqrt(jnp.mean(xv * xv, axis=-1, keepdims=True) + EPS)
    n = xv * r
    dn = dh * g
    dx = skip + r * (dn - n * jnp.mean(dn * n, axis=-1, keepdims=True))
    return dx, jnp.sum(dh * n, axis=0, keepdims=True)


def _mlp_bwd_x(da, w_up, x1, dy, g_mlp):
    s = da.shape[0]
    tm = min(256, s)

    def body(da_ref, w_ref, x1_ref, dy_ref, g_ref, dx1_ref, dg_ref):
        i = pl.program_id(0)

        @pl.when(i == 0)
        def _():
            dg_ref[...] = jnp.zeros_like(dg_ref)

        tn = w_ref.shape[2]
        dhm = _dot(da_ref[:, 0:tn], w_ref[0], NT)
        for j in range(1, N_DEV):
            dhm = dhm + _dot(da_ref[:, j * tn:(j + 1) * tn], w_ref[j], NT)
        dx, dg = _rms_bwd(x1_ref[...], g_ref[...], dhm, dy_ref[...])
        dx1_ref[...] = dx
        dg_ref[...] += dg

    row = pl.BlockSpec((tm, D_MODEL), lambda i: (i, 0))
    return pl.pallas_call(
        body, name="mlp_bwd_x", grid=(s // tm,),
        in_specs=[pl.BlockSpec((tm, D_FF), lambda i: (i, 0)), _full(w_up.shape), row, row, _full((1, D_MODEL))],
        out_specs=[row, _full((1, D_MODEL))],
        out_shape=[jax.ShapeDtypeStruct((s, D_MODEL), F32), jax.ShapeDtypeStruct((1, D_MODEL), F32)],
        compiler_params=_params(("arbitrary",)),
    )(da, w_up, x1, dy, g_mlp)


def _merge_bwd(dx1, oa, of, om, proj, b_gate, wa, wf, wm, w_out):
    s = dx1.shape[0]
    tm = min(512, s)

    def body(dx1_ref, oa_ref, of_ref, om_ref, gl_ref, bg_ref, wa_ref, wf_ref, wm_ref, wo_ref,
             dp_ref, doa_ref, dof_ref, dom_ref, dya_ref, dyf_ref, dym_ref, dbg_ref):
        i = pl.program_id(0)

        @pl.when(i == 0)
        def _():
            dbg_ref[...] = jnp.zeros_like(dbg_ref)

        dmerged = _dot(dx1_ref[...].astype(BF16), wo_ref[...], NT)
        branches = ((oa_ref, wa_ref, doa_ref, dya_ref), (of_ref, wf_ref, dof_ref, dyf_ref), (om_ref, wm_ref, dom_ref, dym_ref))
        for b, (o_ref, w_ref, do_ref, dyb_ref) in enumerate(branches):
            cs = slice(b * D_MODEL, (b + 1) * D_MODEL)
            y = _dot(o_ref[...], w_ref[...])
            g = _sigmoid(gl_ref[:, cs].astype(F32) + bg_ref[:, cs])
            dz = (dmerged * y) * g * (1.0 - g)
            dp_ref[:, cs] = dz.astype(BF16)
            dbg_ref[:, cs] += jnp.sum(dz, axis=0, keepdims=True)
            dyb = (dmerged * g).astype(BF16)
            dyb_ref[...] = dyb
            do = _dot(dyb, w_ref[...], NT)
            do_ref[...] = (do.T if b == 1 else do).astype(BF16)

    row = lambda w: pl.BlockSpec((tm, w), lambda i: (i, 0))
    sd = lambda w: jax.ShapeDtypeStruct((s, w), BF16)
    return pl.pallas_call(
        body, name="merge_bwd", grid=(s // tm,),
        in_specs=[row(D_MODEL), row(512), row(512), row(512), row(HALF_W), _full((1, HALF_W)),
                  _full(wa.shape), _full(wf.shape), _full(wm.shape), _full(w_out.shape)],
        out_specs=[row(HALF_W), row(512), pl.BlockSpec((512, tm), lambda i: (0, i)), row(512),
                   row(D_MODEL), row(D_MODEL), row(D_MODEL), _full((1, HALF_W))],
        out_shape=[sd(PROJ_W), sd(512), jax.ShapeDtypeStruct((512, s), BF16), sd(512), sd(D_MODEL), sd(D_MODEL), sd(D_MODEL),
                   jax.ShapeDtypeStruct((1, HALF_W), F32)],
        compiler_params=_params(("arbitrary",)),
    )(dx1, oa, of, om, proj, b_gate, wa, wf, wm, w_out)


def _swa_valid_t(n):
    key = lax.broadcasted_iota(jnp.int32, (2 * SWA_BLOCK, SWA_BLOCK), 0)
    qry = lax.broadcasted_iota(jnp.int32, (2 * SWA_BLOCK, SWA_BLOCK), 1)
    dist = qry + SWA_BLOCK - key
    return (dist >= 0) & (dist < SWA_BLOCK) & ((key >= SWA_BLOCK) | (n > 0))


def _swa_bwd(qa, kp, vp, bias_t, sink, doa):
    s = qa.shape[0]
    nb = s // SWA_BLOCK

    def body(sink_ref, q_ref, kp_ref, vp_ref, bias_ref, do_ref, dq_ref, dkp_ref, dvp_ref, dbias_ref, dsink_ref, sk_acc):
        n = pl.program_id(0)

        @pl.when(n == 0)
        def _():
            dkp_ref[...] = jnp.zeros_like(dkp_ref)
            dvp_ref[...] = jnp.zeros_like(dvp_ref)
            dbias_ref[...] = jnp.zeros_like(dbias_ref)
            sk_acc[...] = jnp.zeros_like(sk_acc)

        start = pl.multiple_of(n * SWA_BLOCK, SWA_BLOCK)
        win = pl.ds(start, 2 * SWA_BLOCK)
        k2 = kp_ref[win, :]
        v2 = vp_ref[win, :]
        valid = _swa_valid_t(n)
        heads = range(SWA_HEADS)
        hs = lambda h: slice(h * HEAD, (h + 1) * HEAD)
        scale = jnp.asarray(0.125, BF16)
        q = [q_ref[:, hs(h)] for h in heads]
        do = [do_ref[:, hs(h)] for h in heads]
        kk = [k2[:, hs(kv)] for kv in range(2)]
        vv = [v2[:, hs(kv)] for kv in range(2)]
        kt = [(kk[kv].astype(F32) * 0.125).T.astype(BF16) for kv in range(2)]
        st = [jnp.where(valid, _dot(kk[h // 4], q[h], NT) * 0.125 + bias_ref[h], NEG) for h in heads]
        dpt = [_dot(vv[h // 4], do[h], NT) for h in heads]
        pt, dst = [], []
        for h in heads:
            sk = sink_ref[h]
            mx = jnp.maximum(jnp.max(st[h], axis=0, keepdims=True), sk)
            p = jnp.exp(st[h] - mx)
            esk = jnp.exp(sk - mx)
            den = jnp.sum(p, axis=0, keepdims=True) + esk
            p = p / den
            delta = jnp.sum(p * dpt[h], axis=0, keepdims=True)
            d = p * (dpt[h] - delta)
            sk_acc[h:h + 1, :] += -(esk / den) * delta
            dbias_ref[h] += d
            pt.append(p.astype(BF16))
            dst.append(d.astype(BF16))
        dq_t = [_dot(kt[h // 4], dst[h]) for h in heads]
        dq_ref[...] = jnp.concatenate(dq_t, axis=0).T.astype(BF16)
        for kv in range(2):
            group = range(4 * kv, 4 * kv + 4)
            dk = [_dot(dst[h], q[h] * scale) for h in group]
            dv = [_dot(pt[h], do[h]) for h in group]
            dkp_ref[win, hs(kv)] += (dk[0] + dk[1]) + (dk[2] + dk[3])
            dvp_ref[win, hs(kv)] += (dv[0] + dv[1]) + (dv[2] + dv[3])

        @pl.when(n == nb - 1)
        def _():
            dsink_ref[...] = jnp.broadcast_to(jnp.sum(sk_acc[...], axis=1, keepdims=True), dsink_ref.shape)

    return pl.pallas_call(
        body, name="swa_bwd", grid=(nb,),
        in_specs=[pl.BlockSpec(memory_space=pltpu.SMEM),
                  pl.BlockSpec((SWA_BLOCK, 512), lambda n: (n, 0)),
                  _full(kp.shape), _full(vp.shape), _full(bias_t.shape),
                  pl.BlockSpec((SWA_BLOCK, 512), lambda n: (n, 0))],
        out_specs=[pl.BlockSpec((SWA_BLOCK, 512), lambda n: (n, 0)), _full(kp.shape), _full(vp.shape),
                   _full(bias_t.shape), _full((SWA_HEADS, 128))],
        out_shape=[jax.ShapeDtypeStruct((s, 512), BF16), jax.ShapeDtypeStruct(kp.shape, F32),
                   jax.ShapeDtypeStruct(vp.shape, F32), jax.ShapeDtypeStruct(bias_t.shape, F32),
                   jax.ShapeDtypeStruct((SWA_HEADS, 128), F32)],
        scratch_shapes=[pltpu.VMEM((SWA_HEADS, 128), F32)],
        compiler_params=_params(("arbitrary",)),
    )(sink, qa, kp, vp, bias_t, doa)


def _fox_bwd(qt, k, v, dot, ot, cc4, lse4):
    s = k.shape[0]
    t = min(FOX_BWD_T, s)
    nq = s // t

    def body(qt_ref, k_ref, v_ref, dot_ref, ot_ref, cc_ref, lse_ref,
             dqt_ref, dk_ref, dv_ref, dck_ref, dcq_ref, delta_ref, dkt_acc, dvt_acc, ds0, ds1):
        j = pl.program_id(1)

        @pl.when(j == 0)
        def _():
            dqt_ref[...] = jnp.zeros_like(dqt_ref)
            dcq_ref[...] = jnp.zeros_like(dcq_ref)
            r8 = lax.broadcasted_iota(jnp.int32, (8, t), 0)

            def dl(i, c):
                cols = pl.ds(pl.multiple_of(i * t, t), t)
                pr = dot_ref[:, cols].astype(F32) * ot_ref[:, cols].astype(F32)
                d0 = jnp.sum(jnp.where(_head_rows(0), pr, 0.0), axis=0, keepdims=True)
                d1 = jnp.sum(jnp.where(_head_rows(1), pr, 0.0), axis=0, keepdims=True)
                delta_ref[:, cols] = jnp.where(r8 == 0, d0, jnp.where(r8 == 1, d1, 0.0))
                return c

            lax.fori_loop(0, nq, dl, 0)

        kj = k_ref[...]
        vj = v_ref[...]
        ks = pl.ds(pl.multiple_of(j * t, t), t)
        kt = (kj.astype(F32) * 0.125).T.astype(BF16)
        ke = [jnp.where(_head_mask(e), kj, jnp.zeros_like(kj)) for e in range(2)]
        ve = [jnp.where(_head_mask(e), vj, jnp.zeros_like(vj)) for e in range(2)]
        ck = [cc_ref[0, ks, e:e + 1] for e in range(2)]
        for r in (dkt_acc, dvt_acc, ds0, ds1):
            r[...] = jnp.zeros_like(r)

        def block(q0, nqs, k0, nks, masked):
            cols = pl.ds(pl.multiple_of(q0, 128), nqs)
            rows = slice(k0, k0 + nks)
            qti = qt_ref[:, cols]
            doti = dot_ref[:, cols]
            for e, ds_acc in enumerate((ds0, ds1)):
                dims = slice(e * HEAD, (e + 1) * HEAD)
                st = _dot(ke[e][rows, :], qti) - ck[e][rows, :]
                if masked:
                    krow = lax.broadcasted_iota(jnp.int32, (nks, nqs), 0) + (j * t + k0)
                    qcol = lax.broadcasted_iota(jnp.int32, (nks, nqs), 1) + q0
                    st = jnp.where(krow <= qcol, st, NEG)
                pt = jnp.exp(st - lse_ref[0, e:e + 1, cols])
                dpt = _dot(ve[e][rows, :], doti)
                dst = pt * (dpt - delta_ref[e:e + 1, cols])
                dsb = dst.astype(BF16)
                dvt_acc[dims, rows] += _dot(doti[dims, :], pt.astype(BF16), NT)
                dkt_acc[dims, rows] += _dot(qti[dims, :], dsb, NT)
                dqt_ref[dims, cols] += _dot(kt[dims, rows], dsb)
                ds_acc[rows, 0:nqs] += dst
                dcq_ref[0, e:e + 1, cols] += jnp.sum(dst, axis=0, keepdims=True)

        half = t // 2
        block(j * t, half, 0, half, True)
        block(j * t + half, half, 0, t, True)

        def rest(i, c):
            block(i * t, t, 0, t, False)
            return c

        lax.fori_loop(j + 1, nq, rest, 0)
        dk_ref[...] = dkt_acc[...].T.astype(BF16)
        dv_ref[...] = dvt_acc[...].T.astype(BF16)
        lane = lax.broadcasted_iota(jnp.int32, (t, 128), 1)
        c0 = jnp.sum(ds0[...], axis=-1, keepdims=True)
        c1 = jnp.sum(ds1[...], axis=-1, keepdims=True)
        dck_ref[0] = jnp.where(lane == 0, c0, jnp.where(lane == 1, c1, 0.0))

    res_t = lambda: pl.BlockSpec((128, s), lambda hp, j: (hp, 0))
    blk = lambda: pl.BlockSpec((t, 128), lambda hp, j: (j, hp))
    return pl.pallas_call(
        body, name="fox_bwd", grid=(4, nq),
        in_specs=[res_t(), blk(), blk(), res_t(), res_t(), pl.BlockSpec((1, s, 128), lambda hp, j: (hp, 0, 0)),
                  pl.BlockSpec((1, 8, s), lambda hp, j: (hp, 0, 0))],
        out_specs=[res_t(), blk(), blk(),
                   pl.BlockSpec((1, t, 128), lambda hp, j: (hp, j, 0)),
                   pl.BlockSpec((1, 8, s), lambda hp, j: (hp, 0, 0))],
        out_shape=[jax.ShapeDtypeStruct((512, s), F32), jax.ShapeDtypeStruct((s, 512), BF16),
                   jax.ShapeDtypeStruct((s, 512), BF16), jax.ShapeDtypeStruct((4, s, 128), F32),
                   jax.ShapeDtypeStruct((4, 8, s), F32)],
        scratch_shapes=[pltpu.VMEM((8, s), F32)] + [pltpu.VMEM((128, t), F32)] * 2 + [pltpu.VMEM((t, t), F32)] * 2,
        compiler_params=_params(("arbitrary", "arbitrary")),
    )(qt, k, v, dot, ot, cc4, lse4)


def _mem_bwd(qm, mk, mv, dom):
    s = qm.shape[0]
    tq = min(512, s)

    def body(q_ref, mk_ref, mv_ref, do_ref, dq_ref, dmk_ref, dmv_ref):
        i = pl.program_id(0)

        @pl.when(i == 0)
        def _():
            dmk_ref[...] = jnp.zeros_like(dmk_ref)
            dmv_ref[...] = jnp.zeros_like(dmv_ref)

        heads = range(MEM_HEADS)
        hs = lambda h: slice(h * 128, (h + 1) * 128)
        sc = [_dot(q_ref[:, hs(h)], mk_ref[:, hs(h)], NT) * MEM_SCALE for h in heads]
        dp = [_dot(do_ref[:, hs(h)], mv_ref[:, hs(h)], NT) for h in heads]
        pb, dsb = [], []
        for h in heads:
            p = jnp.exp(sc[h] - jnp.max(sc[h], axis=-1, keepdims=True))
            p = p / jnp.sum(p, axis=-1, keepdims=True)
            ds = p * (dp[h] - jnp.sum(p * dp[h], axis=-1, keepdims=True))
            pb.append(p.astype(BF16))
            dsb.append((ds * MEM_SCALE).astype(BF16))
        dq = [_dot(dsb[h], mk_ref[:, hs(h)]).astype(BF16) for h in heads]
        dmk = [_dot(dsb[h], q_ref[:, hs(h)], TN) for h in heads]
        dmv = [_dot(pb[h], do_ref[:, hs(h)], TN) for h in heads]
        for h in heads:
            dq_ref[:, hs(h)] = dq[h]
            dmk_ref[:, hs(h)] += dmk[h]
            dmv_ref[:, hs(h)] += dmv[h]

    return pl.pallas_call(
        body, name="mem_bwd", grid=(s // tq,),
        in_specs=[pl.BlockSpec((tq, 512), lambda i: (i, 0)), _full(mk.shape), _full(mv.shape),
                  pl.BlockSpec((tq, 512), lambda i: (i, 0))],
        out_specs=[pl.BlockSpec((tq, 512), lambda i: (i, 0)), _full(mk.shape), _full(mv.shape)],
        out_shape=[jax.ShapeDtypeStruct((s, 512), BF16), jax.ShapeDtypeStruct(mk.shape, F32),
                   jax.ShapeDtypeStruct(mv.shape, F32)],
        compiler_params=_params(("arbitrary",)),
    )(qm, mk, mv, dom)


def _memkv_bwd(dmk, dmv, kv_raw, kn_mem, mem, g_mem, mem_n, w_kv):
    def body(dmk_ref, dmv_ref, kv_ref, kn_ref, mem_ref, g_ref, mn_ref, w_ref, dw_ref, dkn_ref, dg_ref, dkv_ref):
        dkn = jnp.zeros((1, 128), F32)
        for h in range(MEM_HEADS):
            hs = slice(h * 128, (h + 1) * 128)
            v = kv_ref[:, hs]
            r = lax.rsqrt(jnp.mean(v * v, axis=-1, keepdims=True) + EPS)
            n = v * r
            dn = dmk_ref[:, hs]
            dkn = dkn + jnp.sum(dn * n, axis=0, keepdims=True)
            dng = dn * kn_ref[...]
            dkv_ref[:, hs] = (r * (dng - n * jnp.mean(dng * n, axis=-1, keepdims=True))).astype(BF16)
        dkv_ref[:, 512:1024] = dmv_ref[...].astype(BF16)
        dkn_ref[...] = dkn
        dkv = dkv_ref[...]
        dw_ref[...] = _dot(mn_ref[...], dkv, TN).astype(BF16)
        dmn = _dot(dkv, w_ref[...], NT)
        xv = mem_ref[...]
        r = lax.rsqrt(jnp.mean(xv * xv, axis=-1, keepdims=True) + EPS)
        dg_ref[...] = jnp.sum(dmn * (xv * r), axis=0, keepdims=True)

    m = mem.shape[0]
    return pl.pallas_call(
        body, name="memkv_bwd",
        out_shape=[jax.ShapeDtypeStruct((D_MODEL, 1024), BF16), jax.ShapeDtypeStruct((1, 128), F32),
                   jax.ShapeDtypeStruct((1, D_MODEL), F32)],
        scratch_shapes=[pltpu.VMEM((m, 1024), BF16)],
        compiler_params=pltpu.CompilerParams(vmem_limit_bytes=VMEM_LIMIT),
    )(dmk, dmv, kv_raw, kn_mem, mem, g_mem, mem_n, w_kv)


def _fox_gate_bwd(dcq4, dck4, proj, b_forget128):
    s = dck4.shape[1]
    tm = min(512, s)
    nt = s // tm

    def body(dcq_ref, dck_ref, p_ref, b_ref, dfl_ref, db_ref, carry_ref):
        i = pl.program_id(0)

        @pl.when(i == 0)
        def _():
            carry_ref[...] = jnp.zeros_like(carry_ref)
            db_ref[...] = jnp.zeros_like(db_ref)

        dcv = jnp.zeros((tm, 128), F32)
        for hp in range(4):
            by_query = jnp.concatenate([dcq_ref[hp], jnp.zeros((120, tm), F32)], axis=0).T
            d = by_query - dck_ref[hp]
            dcv = dcv + (d if hp == 0 else pltpu.roll(d, 2 * hp, 1))
        dlogf = jnp.dot(_tri(tm, False), dcv, precision=lax.Precision.HIGHEST, preferred_element_type=F32) + carry_ref[...]
        carry_ref[...] += jnp.sum(dcv, axis=0, keepdims=True)
        z = p_ref[...] + b_ref[...]
        dfl = dlogf * (1.0 / (1.0 + jnp.exp(z)))
        dfl_ref[...] = dfl.astype(BF16)
        db_ref[...] += jnp.sum(dfl, axis=0, keepdims=True)

    return pl.pallas_call(
        body, name="fox_gate_bwd", grid=(nt,),
        in_specs=[pl.BlockSpec((4, 8, tm), lambda i: (0, 0, nt - 1 - i)),
                  pl.BlockSpec((4, tm, 128), lambda i: (0, nt - 1 - i, 0)),
                  pl.BlockSpec((tm, 128), lambda i: (nt - 1 - i, 0)), _full((1, 128))],
        out_specs=[pl.BlockSpec((tm, 128), lambda i: (nt - 1 - i, 0)), _full((1, 128))],
        out_shape=[jax.ShapeDtypeStruct((s, 128), BF16), jax.ShapeDtypeStruct((1, 128), F32)],
        scratch_shapes=[pltpu.VMEM((1, 128), F32)],
        compiler_params=_params(("arbitrary",)),
    )(dcq4, dck4, proj, b_forget128)


def _proj_pre_bwd(dproj, proj, dqf, dkf, dvf, dqm, dqa, dka, dva, dfl, gq_fox, gk_fox, gq_mem, gq_swa, gk_swa):
    s = proj.shape[0]
    tm = min(512, s)

    def body(dp_in, p_ref, dqf_ref, dkf_ref, dvf_ref, dqm_ref, dqa_ref, dka_ref, dva_ref, dfl_ref,
             gqf, gkf, gqm, gqa, gka, dp_ref, dgn_ref):
        i = pl.program_id(0)

        @pl.when(i == 0)
        def _():
            dgn_ref[...] = jnp.zeros_like(dgn_ref)

        def norm_bwd(off, width, hd, g_ref, dn_ref, slot):
            acc = jnp.zeros((1, 128), F32)
            for b in range(width // 128):
                v = p_ref[:, off + b * 128: off + (b + 1) * 128].astype(F32)
                r = lax.rsqrt(_group_mean(v * v, hd) + EPS)
                n = v * r
                dn = dn_ref[b * 128:(b + 1) * 128, :].T if slot == 0 else dn_ref[:, b * 128:(b + 1) * 128].astype(F32)
                acc = acc + jnp.sum(dn * n, axis=0, keepdims=True)
                dng = dn * g_ref[...]
                dp_ref[:, off + b * 128: off + (b + 1) * 128] = (r * (dng - n * _group_mean(dng * n, hd))).astype(BF16)
            dgn_ref[slot:slot + 1, :] += acc

        norm_bwd(H_QF, 512, HEAD, gqf, dqf_ref, 0)
        norm_bwd(H_KF, 512, HEAD, gkf, dkf_ref, 1)
        dp_ref[:, H_VF:H_VF + 512] = dvf_ref[...].astype(BF16)
        norm_bwd(H_QM, 512, MEM_HEAD, gqm, dqm_ref, 2)
        norm_bwd(H_QA, 512, HEAD, gqa, dqa_ref, 3)
        norm_bwd(H_KA, 128, HEAD, gka, dka_ref, 4)
        dp_ref[:, H_VA:H_VA + 128] = dva_ref[...].astype(BF16)
        dp_ref[:, H_FL:H_FL + 128] = dfl_ref[...]
        dp_ref[:, H_FL + 128:HALF_W] = jnp.zeros((tm, HALF_W - H_FL - 128), BF16)

    row = lambda w: pl.BlockSpec((tm, w), lambda i: (i, 0))
    g_spec = _full((1, 128))
    return pl.pallas_call(
        body, name="proj_pre_bwd", grid=(s // tm,),
        in_specs=[pl.BlockSpec(memory_space=pl.ANY), pl.BlockSpec((tm, HALF_W), lambda i: (i, 1)),
                  pl.BlockSpec((512, tm), lambda i: (0, i)), row(512), row(512), row(512), row(512),
                  row(128), row(128), row(128), g_spec, g_spec, g_spec, g_spec, g_spec],
        out_specs=[pl.BlockSpec((tm, HALF_W), lambda i: (i, 1)), _full((8, 128))],
        out_shape=[jax.ShapeDtypeStruct((s, PROJ_W), BF16), jax.ShapeDtypeStruct((8, 128), F32)],
        input_output_aliases={0: 0},
        compiler_params=_params(("arbitrary",)),
    )(dproj, proj, dqf, dkf, dvf, dqm, dqa, dka, dva, dfl, gq_fox, gk_fox, gq_mem, gq_swa, gk_swa)


def _in_bwd_x(dproj, w_in_p, x, g_mix, dx1):
    s = x.shape[0]
    tm = min(256, s)

    def body(dp_ref, w_ref, x_ref, g_ref, dx1_ref, gx_ref, dg_ref):
        i = pl.program_id(0)

        @pl.when(i == 0)
        def _():
            dg_ref[...] = jnp.zeros_like(dg_ref)

        dx, dg = _rms_bwd(x_ref[...], g_ref[...], _dot(dp_ref[...], w_ref[...], NT), dx1_ref[...])
        gx_ref[...] = dx
        dg_ref[...] += dg

    row = pl.BlockSpec((tm, D_MODEL), lambda i: (i, 0))
    return pl.pallas_call(
        body, name="in_bwd_x", grid=(s // tm,),
        in_specs=[pl.BlockSpec((tm, PROJ_W), lambda i: (i, 0)), _full(w_in_p.shape), row, _full((1, D_MODEL)), row],
        out_specs=[row, _full((1, D_MODEL))],
        out_shape=[jax.ShapeDtypeStruct((s, D_MODEL), F32), jax.ShapeDtypeStruct((1, D_MODEL), F32)],
        compiler_params=_params(("arbitrary",)),
    )(dproj, w_in_p, x, g_mix, dx1)


def _rel_bias_bwd(dbias, bucket):
    def body(db_ref, bk_ref, o_ref):
        bk = bk_ref[...]
        lane = lax.broadcasted_iota(jnp.int32, (1, 128), 1)
        for b in range(REL_BUCKETS):
            sel = bk == b
            acc = jnp.zeros((1, 128), F32)
            for h in range(SWA_HEADS):
                tot = jnp.sum(jnp.sum(jnp.where(sel, db_ref[h], 0.0), axis=0, keepdims=True), axis=-1, keepdims=True)
                acc = jnp.where(lane == h, tot, acc)
            o_ref[:, b * 128:(b + 1) * 128] = acc

    return pl.pallas_call(
        body, name="rel_bias_bwd",
        out_shape=jax.ShapeDtypeStruct((1, REL_BUCKETS * 128), F32),
        compiler_params=pltpu.CompilerParams(vmem_limit_bytes=VMEM_LIMIT),
    )(dbias, bucket)


def _my_place():
    return lax.axis_index("x"), lax.axis_index("y"), lax.axis_index("c")


def _peer(place, k):
    x, y, c = place
    return (1 - x if k & 4 else x, 1 - y if k & 2 else y, 1 - c if k & 1 else c)


def _index(place):
    x, y, c = place
    return 4 * x + 2 * y + c


HBM_SPEC = pl.BlockSpec(memory_space=pltpu.HBM)
SEM_SPEC = pl.BlockSpec(memory_space=pltpu.SEMAPHORE)
DATAFLOW = pltpu.SideEffectType.DATAFLOW_SIDE_EFFECTING


ALL_PEERS = tuple(range(1, N_DEV))
SAME_CORE = (2, 4, 6)
OWN = N_DEV - 1


def _split_copy(src_ref, land_ref, send_sems, recv_sems, me, k, gather):
    peer = _peer(me, k)
    if gather:
        src, dst = src_ref, land_ref.at[_index(me)]
    else:
        src, dst = src_ref.at[_index(peer)], land_ref.at[k - 1]
    return pltpu.make_async_remote_copy(src_ref=src, dst_ref=dst, send_sem=send_sems.at[k - 1], recv_sem=recv_sems.at[k - 1],
                                        device_id=peer, device_id_type=MESH)


def _own_copy(src_ref, land_ref, recv_sems, me, gather):
    if gather:
        src, dst = src_ref, land_ref.at[_index(me)]
    else:
        src, dst = src_ref.at[_index(me)], land_ref.at[OWN]
    return pltpu.make_async_copy(src, dst, recv_sems.at[OWN])


def _split_start(srcs, gather, name, peers=ALL_PEERS, after=None):
    n = len(srcs)
    extra = [] if after is None else [after]

    def body(*refs):
        refs = refs[:2 * n] + refs[2 * n + len(extra):]
        src_refs, land_refs = refs[:n], refs[n:2 * n]
        send_sems, recv_sems, token = refs[2 * n:3 * n], refs[3 * n:4 * n], refs[-1]
        me = _my_place()
        for w in range(n):
            for k in peers:
                _split_copy(src_refs[w], land_refs[w], send_sems[w], recv_sems[w], me, k, gather).start()
            _own_copy(src_refs[w], land_refs[w], recv_sems[w], me, gather).start()
        token[...] = jnp.zeros_like(token)

    lands = [lax.empty((N_DEV,) + (a.shape if gather else a.shape[1:]), a.dtype) for a in srcs]
    sems = [pltpu.SemaphoreType.DMA((N_DEV,))] * (2 * n)
    hbm = [pltpu.HBM(a.shape, a.dtype) for a in list(srcs) + lands]
    outs = pl.pallas_call(
        body, name=name,
        out_shape=(*sems, *hbm, jax.ShapeDtypeStruct((8, 128), F32)),
        in_specs=(HBM_SPEC,) * (2 * n) + (pl.BlockSpec(memory_space=pl.ANY),) * len(extra),
        out_specs=(SEM_SPEC,) * (2 * n) + (HBM_SPEC,) * (2 * n) + (pl.BlockSpec(memory_space=pltpu.VMEM),),
        input_output_aliases={i: 2 * n + i for i in range(2 * n)},
        compiler_params=pltpu.CompilerParams(has_side_effects=DATAFLOW),
    )(*[pltpu.with_memory_space_constraint(a, pltpu.HBM) for a in list(srcs) + lands], *extra)
    return list(outs[:n]), list(outs[n:2 * n]), list(outs[2 * n:3 * n]), list(outs[3 * n:4 * n]), outs[-1]


def _split_wait(started, w, after, gather, name):
    send_sems, recv_sems, srcs, lands, _ = started

    def body(src_ref, land_ref, send_sems, recv_sems, after_ref, src_out, land_out):
        me = _my_place()
        for k in ALL_PEERS:
            cp = _split_copy(src_ref, land_ref, send_sems, recv_sems, me, k, gather)
            cp.wait_send()
            cp.wait_recv()
        _own_copy(src_ref, land_ref, recv_sems, me, gather).wait()

    return pl.pallas_call(
        body, name=name,
        out_shape=(pltpu.HBM(srcs[w].shape, srcs[w].dtype), pltpu.HBM(lands[w].shape, lands[w].dtype)),
        in_specs=(HBM_SPEC, HBM_SPEC, SEM_SPEC, SEM_SPEC, pl.BlockSpec(memory_space=pl.ANY)),
        out_specs=(HBM_SPEC, HBM_SPEC), input_output_aliases={0: 0, 1: 1},
        compiler_params=pltpu.CompilerParams(has_side_effects=DATAFLOW),
    )(srcs[w], lands[w], send_sems[w], recv_sems[w], after)[1]


def _forward_copy(land_ref, send_sems, recv_sems, me, j, incoming):
    sibling = _peer(me, 1)
    rows = land_ref.at[_index(_peer(sibling if incoming else me, SAME_CORE[j]))]
    return pltpu.make_async_remote_copy(src_ref=rows, dst_ref=rows, send_sem=send_sems.at[j], recv_sem=recv_sems.at[j],
                                        device_id=sibling, device_id_type=MESH)


def _forward_start(started, after, name):
    send_a, recv_a, srcs, lands, _ = started

    def body(src_ref, land_ref, send_a, recv_a, after_ref, send_b, recv_b, src_out, land_out):
        me = _my_place()
        for j, k in enumerate(SAME_CORE):
            _split_copy(src_ref, land_ref, send_a, recv_a, me, k, True).wait_recv()
            _forward_copy(land_ref, send_b, recv_b, me, j, False).start()

    sems = pltpu.SemaphoreType.DMA((len(SAME_CORE),))
    return pl.pallas_call(
        body, name=name,
        out_shape=(sems, sems, pltpu.HBM(srcs[0].shape, srcs[0].dtype), pltpu.HBM(lands[0].shape, lands[0].dtype)),
        in_specs=(HBM_SPEC, HBM_SPEC, SEM_SPEC, SEM_SPEC, pl.BlockSpec(memory_space=pl.ANY)),
        out_specs=(SEM_SPEC, SEM_SPEC, HBM_SPEC, HBM_SPEC), input_output_aliases={0: 2, 1: 3},
        compiler_params=pltpu.CompilerParams(has_side_effects=DATAFLOW),
    )(srcs[0], lands[0], send_a[0], recv_a[0], after)


def _forward_wait(started, forwarded, name):
    send_a, recv_a, _, _, _ = started
    send_b, recv_b, src, land = forwarded

    def body(src_ref, land_ref, send_a, recv_a, send_b, recv_b, src_out, land_out):
        me = _my_place()
        _own_copy(src_ref, land_ref, recv_a, me, True).wait()
        for k in (1,) + SAME_CORE:
            _split_copy(src_ref, land_ref, send_a, recv_a, me, k, True).wait_send()
        _split_copy(src_ref, land_ref, send_a, recv_a, me, 1, True).wait_recv()
        for j in range(len(SAME_CORE)):
            _forward_copy(land_ref, send_b, recv_b, me, j, False).wait_send()
            _forward_copy(land_ref, send_b, recv_b, me, j, True).wait_recv()

    return pl.pallas_call(
        body, name=name,
        out_shape=(pltpu.HBM(src.shape, src.dtype), pltpu.HBM(land.shape, land.dtype)),
        in_specs=(HBM_SPEC, HBM_SPEC, SEM_SPEC, SEM_SPEC, SEM_SPEC, SEM_SPEC),
        out_specs=(HBM_SPEC, HBM_SPEC), input_output_aliases={0: 0, 1: 1},
        compiler_params=pltpu.CompilerParams(has_side_effects=DATAFLOW),
    )(src, land, send_a[0], recv_a[0], send_b, recv_b)[1]


def _adam_math(w, g, m, v):
    m2 = ADAM_B1 * m + (1.0 - ADAM_B1) * g
    v2 = ADAM_B2 * v + (1.0 - ADAM_B2) * (g * g)
    m_hat = m2 / (1.0 - ADAM_B1 ** ADAM_STEP)
    v_hat = v2 / (1.0 - ADAM_B2 ** ADAM_STEP)
    delta = -ADAM_LR * (m_hat / (jnp.sqrt(v_hat) + ADAM_EPS) + ADAM_WD * w)
    return delta, m2, v2


def _adamw(lands, w, m, v, name):
    a, b = w.shape
    bp = lands[0].shape[2]
    ta = min(128, a)
    per = a // len(lands) // ta

    def body(*refs):
        p_refs = refs[:len(lands)]
        w_ref, m_ref, v_ref, g_ref, d_ref, m2_ref, v2_ref = refs[len(lands):]
        i = pl.program_id(0)

        def run(p_ref):
            g = p_ref[0, :, 0:b].astype(F32)
            for k in range(1, N_DEV):
                g = g + p_ref[k, :, 0:b].astype(F32)
            delta, m2, v2 = _adam_math(w_ref[...], g, m_ref[...], v_ref[...])
            g_ref[...] = g
            d_ref[...] = delta
            m2_ref[...] = m2
            v2_ref[...] = v2

        for part, p_ref in enumerate(p_refs):
            pl.when((i >= part * per) & (i < (part + 1) * per))(functools.partial(run, p_ref))

    land_spec = lambda part: pl.BlockSpec((N_DEV, ta, bp), lambda i: (0, jnp.clip(i - part * per, 0, per - 1), 0))
    blk = pl.BlockSpec((ta, b), lambda i: (i, 0))
    sd = jax.ShapeDtypeStruct((a, b), F32)
    return pl.pallas_call(
        body, name=name, grid=(a // ta,),
        in_specs=[land_spec(part) for part in range(len(lands))] + [blk, blk, blk],
        out_specs=[blk, blk, blk, blk], out_shape=[sd, sd, sd, sd],
        compiler_params=_params(("parallel",)),
    )(*lands, w, m, v)


def _bucket_table():
    t_loc = jnp.arange(SWA_BLOCK)[:, None] + SWA_BLOCK
    s_loc = jnp.arange(2 * SWA_BLOCK)[None, :]
    dist = t_loc - s_loc
    max_exact = REL_BUCKETS // 2
    d = jnp.maximum(dist, 0)
    df = jnp.maximum(d, 1).astype(F32)
    large = max_exact + (jnp.log(df / max_exact) / math.log(REL_MAX_DIST / max_exact) * (REL_BUCKETS - max_exact)).astype(jnp.int32)
    large = jnp.minimum(large, REL_BUCKETS - 1)
    bucket = jnp.where(d < max_exact, d, large)
    band = (dist >= 0) & (dist < SWA_BLOCK)
    return bucket, band


def _tile2(g):
    return jnp.concatenate([g, g], axis=1) if g.shape[1] == HEAD else g


SHARD_W = 737
SHARD_WP = 768
IN_WIDTH = N_DEV * SHARD_W
SEGMENTS = ((GL0, 2824, 3072), (QF0, 768, 512), (KF0, 1280, 512), (VF0, 1792, 512), (QM0, 2312, 512),
            (QA0, 0, 512), (KA0, 512, 128), (VA0, 640, 128), (FL0, 2304, 8))


def _lane_plan(sources):
    plan = []
    for t in range(len(sources) // 128):
        groups = {}
        for lane in range(128):
            src = sources[128 * t + lane]
            if src is not None:
                slab, col = src
                groups.setdefault((slab, col // 128, (lane - col) % 128), []).append(lane)
        tile = []
        for key, lanes in groups.items():
            assert lanes == list(range(lanes[0], lanes[-1] + 1))
            tile.append((key, lanes[0], lanes[-1] + 1))
        plan.append(tile)
    return plan


def _assemble(tile_plan, load, rows):
    lane = lax.broadcasted_iota(jnp.int32, (1, 128), 1)
    out = jnp.zeros((rows, 128), F32)
    for (slab, st, roll), lo, hi in tile_plan:
        v = load(slab, st)
        if roll:
            v = pltpu.roll(v, roll, 1)
        out = v if (lo, hi) == (0, 128) else jnp.where((lane >= lo) & (lane < hi), v, out)
    return out


def _w_in_from_shards(land):
    ref_col = [None] * PROJ_W
    for p0, r0, n in SEGMENTS:
        for i in range(n):
            ref_col[p0 + i] = divmod(r0 + i, SHARD_W)
    plan = _lane_plan(ref_col)
    d_model = land.shape[1]
    tm = 256

    def body(land_ref, o_ref):
        load = lambda slab, st: land_ref[slab, :, st * 128:(st + 1) * 128].astype(F32)
        for t, tile_plan in enumerate(plan):
            o_ref[:, t * 128:(t + 1) * 128] = _assemble(tile_plan, load, tm).astype(BF16)

    return pl.pallas_call(
        body, name="w_in_from_shards", grid=(d_model // tm,),
        in_specs=[pl.BlockSpec((N_DEV, tm, SHARD_WP), lambda i: (0, i, 0))],
        out_specs=pl.BlockSpec((tm, PROJ_W), lambda i: (i, 0)),
        out_shape=jax.ShapeDtypeStruct((d_model, PROJ_W), BF16),
        compiler_params=_params(("parallel",)),
    )(land)


def _dw_in_to_parts(dwp, name):
    padded_col = [None] * IN_WIDTH
    for p0, r0, n in SEGMENTS:
        for i in range(n):
            padded_col[r0 + i] = p0 + i
    sources = []
    for d in range(N_DEV):
        sources += [(0, padded_col[SHARD_W * d + c]) if c < SHARD_W else None for c in range(SHARD_WP)]
    plan = _lane_plan(sources)
    d_model = dwp.shape[0]
    tm = 256
    tiles = SHARD_WP // 128

    def body(dw_ref, o_ref):
        load = lambda slab, st: dw_ref[:, st * 128:(st + 1) * 128].astype(F32)
        for t, tile_plan in enumerate(plan):
            d, c = divmod(t, tiles)
            o_ref[d, :, c * 128:(c + 1) * 128] = _assemble(tile_plan, load, tm).astype(BF16)

    return pl.pallas_call(
        body, name=name, grid=(d_model // tm,),
        in_specs=[pl.BlockSpec((tm, PROJ_W), lambda i: (i, 0))],
        out_specs=pl.BlockSpec((N_DEV, tm, SHARD_WP), lambda i: (0, i, 0)),
        out_shape=jax.ShapeDtypeStruct((N_DEV, d_model, SHARD_WP), BF16),
        compiler_params=_params(("parallel",)),
    )(dwp)


def _cast_shards(shards):
    names = list(shards)

    def body(*refs):
        for src, dst in zip(refs[:len(names)], refs[len(names):]):
            if dst.shape != src.shape:
                dst[...] = jnp.zeros(dst.shape, BF16)
                dst[:, 0:src.shape[1]] = src[...].astype(BF16)
            else:
                dst[...] = src[...].astype(BF16)

    out_shape = [jax.ShapeDtypeStruct((shards[n].shape[0], SHARD_WP if n == "w_in" else shards[n].shape[1]), BF16)
                 for n in names]
    outs = pl.pallas_call(body, name="cast_shards", out_shape=out_shape,
                          compiler_params=pltpu.CompilerParams(vmem_limit_bytes=VMEM_LIMIT))(*[shards[n] for n in names])
    return dict(zip(names, outs))


def _tie(x, *tokens):
    for t in tokens:
        if t is not None:
            x = x + t[0:1, 0:1]
    return x


def _local_step(x, mem, target, p, getw, emit, deps=()):
    s = x.shape[0]
    bucket, band = _bucket_table()
    bucket_m = jnp.where(band, bucket, -1).astype(jnp.int32)
    bias = _bias_table(p["rel_bias"], bucket_m)
    bucket_t = jnp.transpose(bucket_m)
    bias_t = _bias_table(p["rel_bias"], bucket_t)
    gqf, gkf, gqa, gka = _tile2(p["qn_fox"]), _tile2(p["kn_fox"]), _tile2(p["qn_swa"]), _tile2(p["kn_swa"])
    gqm = p["qn_mem"]
    bf128 = jnp.pad(p["b_forget"], ((0, 0), (0, 120)))
    sink = p["sink_swa"].reshape(8)

    h = _rms_fwd(x, p["g_mix"], "rms_mix", tuple(deps) + (bias, bias_t))
    w_in = getw("w_in", h)
    proj = _mm(h, w_in, "nn", BF16, 512, 1536, 1024, "proj")
    fl = _mm(h, w_in[:, FL0:FL0 + 128], "nn", F32, 512, 128, 1024, "proj_fl")
    qf, kf, vf, qm, qa, ka, va, qf_t, vf_t = _proj_post(proj, gqf, gkf, gqm, gqa, gka)
    cc4, ca4 = _fox_gate_fwd(fl, bf128)
    w_kv = getw("w_mem_kv", cc4)
    mem_n, kv_raw, mk, mv = _memkv_fwd(mem, p["g_mem"], w_kv, p["kn_mem"])
    kp = jnp.pad(ka, ((SWA_BLOCK, 0), (0, 0)))
    vp = jnp.pad(va, ((SWA_BLOCK, 0), (0, 0)))
    oa = _swa_fwd(qa, kp, vp, bias, sink)
    of, lse4, of_t = _fox_fwd(qf, kf, vf_t, ca4)
    om = _mem_fwd(qm, mk, mv)
    wa, wf, wm, w_out = getw("w_o_swa", oa), getw("w_o_fox", oa), getw("w_o_mem", oa), getw("w_out", oa)
    x1, hm, merged = _merge_fwd(x, oa, of, om, proj, p["b_gate"], wa, wf, wm, w_out, p["g_mlp"])
    w_up = getw("w_mlp_up", of)
    u = _mlp_up(hm, w_up)
    w_down = getw("w_mlp_down", hm)
    dy, dy_b, loss = _mlp_down_loss(u, w_down, x1, target)

    da = _mlp_bwd_act(dy_b, w_down, u)
    t_down = emit({"w_mlp_down": _mm(u, dy_b, "tn", BF16, 1024, 1024, 2048, "dw_down")})
    dx1, dg_mlp = _mlp_bwd_x(da, w_up, x1, dy, _tie(p["g_mlp"], t_down))
    t_up = emit({"w_mlp_up": _mm(hm, da, "tn", BF16, 1024, 1024, 2048, "dw_up", column_chunks=True)})
    dproj, doa, dof_t, dom, dya, dyf, dym, db_gate = _merge_bwd(
        dx1, oa, of, om, proj, _tie(p["b_gate"], t_up), wa, wf, wm, w_out)
    dw_oa, dw_of, dw_om = _mm_tn3([oa, of, om], [dya, dyf, dym], "dw_o")
    t_o = emit({"w_out": _mm(merged, dx1, "tn", BF16, 1024, 1024, 2048, "dw_out"),
                "w_o_swa": dw_oa, "w_o_fox": dw_of, "w_o_mem": dw_om})

    dqm, dmk, dmv = _mem_bwd(qm, mk, mv, dom)
    dw_kv, dkn_mem, dg_mem = _memkv_bwd(dmk, dmv, kv_raw, _tie(p["kn_mem"], t_o), mem, p["g_mem"], mem_n, w_kv)
    t_kv = emit({"w_mem_kv": dw_kv})
    dqa, dkp, dvp, dbias, dsink = _swa_bwd(qa, kp, vp, bias_t, _tie(p["sink_swa"], t_kv).reshape(8), doa)
    dqf_t, dkf, dvf, dck4, dcq4 = _fox_bwd(qf_t, kf, vf, dof_t, of_t, cc4, lse4)

    dfl, db_forget = _fox_gate_bwd(dcq4, dck4, fl, bf128)

    dproj, dgn = _proj_pre_bwd(dproj, proj, dqf_t, dkf, dvf, dqm, dqa, dkp[SWA_BLOCK:], dvp[SWA_BLOCK:], dfl,
                               gqf, gkf, gqm, gqa, gka)
    t_in = emit({"w_in_a": _mm(h, dproj, "tn", BF16, 512, 3072, 1024, "dw_in_a", m_part=(0, 2))})
    t_in = emit({"w_in_b": _mm(h, dproj, "tn", BF16, 512, 3072, 1024, "dw_in_b", m_part=(1, 2), after=t_in)})
    grad_x, dg_mix = _in_bwd_x(dproj, w_in, x, _tie(p["g_mix"], t_in), dx1)
    d_rel = _rel_bias_bwd(dbias, bucket_t)

    fold = lambda r: dgn[r:r + 1, 0:HEAD] + dgn[r:r + 1, HEAD:128]
    small = {
        "g_mix": dg_mix, "b_gate": db_gate, "b_forget": db_forget[:, 0:8],
        "qn_swa": fold(3), "kn_swa": fold(4), "sink_swa": dsink[:, 0].reshape(1, 8), "rel_bias": d_rel,
        "qn_fox": fold(0), "kn_fox": fold(1), "g_mem": dg_mem, "qn_mem": dgn[2:3, :], "kn_mem": dkn_mem,
        "g_mlp": dg_mlp,
    }
    return loss, grad_x, small


SMALL = ("g_mix", "b_gate", "b_forget", "qn_swa", "kn_swa", "sink_swa", "rel_bias", "qn_fox", "kn_fox", "g_mem",
         "qn_mem", "kn_mem", "g_mlp")
BIG = ("w_in", "w_mem_kv", "w_o_swa", "w_o_fox", "w_o_mem", "w_out", "w_mlp_up", "w_mlp_down")
COL_SHARDED = ("w_in", "w_o_swa", "w_o_fox", "w_o_mem", "w_mlp_up")
WEIGHTS = ("g_mix", "w_in", "b_gate", "b_forget", "qn_swa", "kn_swa", "sink_swa", "rel_bias", "qn_fox", "kn_fox", "g_mem",
           "w_mem_kv", "qn_mem", "kn_mem", "w_o_swa", "w_o_fox", "w_o_mem", "w_out", "g_mlp", "w_mlp_up", "w_mlp_down")
SMALL_SLOTS = (("g_mix", 1024), ("b_gate", 3072), ("b_forget", 128), ("qn_swa", 128), ("kn_swa", 128), ("sink_swa", 128),
               ("rel_bias", REL_BUCKETS * 128), ("qn_fox", 128), ("kn_fox", 128), ("g_mem", 1024), ("qn_mem", 128),
               ("kn_mem", 128), ("g_mlp", 1024), ("loss", 128))
SMALL_OFF = {n: sum(w for _, w in SMALL_SLOTS[:i]) for i, (n, _) in enumerate(SMALL_SLOTS)}
SMALL_ROW = sum(w for _, w in SMALL_SLOTS)


def _gathered_to_full(name, g):
    if name in COL_SHARDED:
        return jnp.transpose(g, (1, 0, 2)).reshape(g.shape[1], N_DEV * g.shape[2])
    return g.reshape(N_DEV * g.shape[1], g.shape[2])


def _full_to_parts(name, full, b):
    if name in COL_SHARDED:
        return jnp.transpose(full.reshape(full.shape[0], N_DEV, b), (1, 0, 2)).astype(BF16)
    return full.reshape(N_DEV, full.shape[0] // N_DEV, full.shape[1]).astype(BF16)


def _pack_small(grads, loss):
    pieces = []
    for n, width in SMALL_SLOTS:
        a = loss.reshape(1, 1) if n == "loss" else grads[n].reshape(1, -1)
        pieces.append(jnp.pad(a, ((0, 0), (0, width - a.shape[1]))))
    return jnp.concatenate(pieces, axis=1)


def _adamw_small(gathered, w, m, v):
    names = list(SMALL)

    def body(*refs):
        p_ref = refs[0]
        ins = refs[1:1 + 3 * len(names)]
        outs = refs[1 + 3 * len(names):]
        g_all = p_ref[0]
        for k in range(1, N_DEV):
            g_all = g_all + p_ref[k]
        for i, n in enumerate(names):
            w_ref, m_ref, v_ref = ins[3 * i:3 * i + 3]
            out = outs[4 * i:4 * i + 4]
            rows, cols = w_ref.shape
            for r in range(rows):
                off = SMALL_OFF[n] + 128 * r
                g = g_all[:, off:off + cols]
                rs = slice(r, r + 1)
                res = (g,) + _adam_math(w_ref[rs, :], g, m_ref[rs, :], v_ref[rs, :])
                for o_ref, val in zip(out, res):
                    o_ref[rs, :] = val
        outs[-1][...] = g_all[:, SMALL_OFF["loss"]:SMALL_OFF["loss"] + 128]

    args = [gathered]
    out_shape = []
    for n in names:
        args += [w[n], m[n], v[n]]
        out_shape += [jax.ShapeDtypeStruct(w[n].shape, F32)] * 4
    out_shape.append(jax.ShapeDtypeStruct((1, 128), F32))
    outs = pl.pallas_call(body, name="adamw_small", out_shape=out_shape)(*args)
    return {n: outs[4 * i:4 * i + 4] for i, n in enumerate(names)}, outs[-1]


def kernel(x, mem, g_mix, w_in, b_gate, b_forget, qn_swa, kn_swa, sink_swa, rel_bias, qn_fox, kn_fox, g_mem, w_mem_kv, qn_mem, kn_mem, w_o_swa, w_o_fox, w_o_mem, w_out, g_mlp, w_mlp_up, w_mlp_down, loss_target, m_g_mix, m_w_in, m_b_gate, m_b_forget, m_qn_swa, m_kn_swa, m_sink_swa, m_rel_bias, m_qn_fox, m_kn_fox, m_g_mem, m_w_mem_kv, m_qn_mem, m_kn_mem, m_w_o_swa, m_w_o_fox, m_w_o_mem, m_w_out, m_g_mlp, m_w_mlp_up, m_w_mlp_down, v_g_mix, v_w_in, v_b_gate, v_b_forget, v_qn_swa, v_kn_swa, v_sink_swa, v_rel_bias, v_qn_fox, v_kn_fox, v_g_mem, v_w_mem_kv, v_qn_mem, v_kn_mem, v_w_o_swa, v_w_o_fox, v_w_o_mem, v_w_out, v_g_mlp, v_w_mlp_up, v_w_mlp_down):
    wts = dict(g_mix=g_mix, w_in=w_in, b_gate=b_gate, b_forget=b_forget, qn_swa=qn_swa, kn_swa=kn_swa, sink_swa=sink_swa,
               rel_bias=rel_bias, qn_fox=qn_fox, kn_fox=kn_fox, g_mem=g_mem, w_mem_kv=w_mem_kv, qn_mem=qn_mem, kn_mem=kn_mem,
               w_o_swa=w_o_swa, w_o_fox=w_o_fox, w_o_mem=w_o_mem, w_out=w_out, g_mlp=g_mlp, w_mlp_up=w_mlp_up,
               w_mlp_down=w_mlp_down)
    mom = dict(g_mix=m_g_mix, w_in=m_w_in, b_gate=m_b_gate, b_forget=m_b_forget, qn_swa=m_qn_swa, kn_swa=m_kn_swa,
               sink_swa=m_sink_swa, rel_bias=m_rel_bias, qn_fox=m_qn_fox, kn_fox=m_kn_fox, g_mem=m_g_mem, w_mem_kv=m_w_mem_kv,
               qn_mem=m_qn_mem, kn_mem=m_kn_mem, w_o_swa=m_w_o_swa, w_o_fox=m_w_o_fox, w_o_mem=m_w_o_mem, w_out=m_w_out,
               g_mlp=m_g_mlp, w_mlp_up=m_w_mlp_up, w_mlp_down=m_w_mlp_down)
    var = dict(g_mix=v_g_mix, w_in=v_w_in, b_gate=v_b_gate, b_forget=v_b_forget, qn_swa=v_qn_swa, kn_swa=v_kn_swa,
               sink_swa=v_sink_swa, rel_bias=v_rel_bias, qn_fox=v_qn_fox, kn_fox=v_kn_fox, g_mem=v_g_mem, w_mem_kv=v_w_mem_kv,
               qn_mem=v_qn_mem, kn_mem=v_kn_mem, w_o_swa=v_w_o_swa, w_o_fox=v_w_o_fox, w_o_mem=v_w_o_mem, w_out=v_w_out,
               g_mlp=v_g_mlp, w_mlp_up=v_w_mlp_up, w_mlp_down=v_w_mlp_down)

    shards = _cast_shards({n: wts[n][0] for n in BIG})
    first = _split_start([shards["w_in"]], True, "ag_start_w_in", peers=(1,) + SAME_CORE)
    rest = _split_start([shards[n] for n in BIG[1:]], True, "ag_start_rest", after=first[4])
    full = {}

    def getw(n, after):
        if n == "w_in" and n not in full:
            forwarded = _forward_start(first, after, "ag_forward_w_in")
            full[n] = _w_in_from_shards(_forward_wait(first, forwarded, "ag_wait_w_in"))
        elif n not in full:
            land = _split_wait(rest, BIG[1:].index(n), after, True, "ag_wait_" + n)
            full[n] = land if n == "w_mlp_up" else _gathered_to_full(n, land)
        return full[n]

    exchanges = {}

    def emit(grads_by_name):
        parts = []
        for n, grad in grads_by_name.items():
            if n.startswith("w_in"):
                parts.append(_dw_in_to_parts(grad, "d" + n + "_to_parts"))
            else:
                parts.append(grad if n == "w_mlp_up" else _full_to_parts(n, grad, wts[n].shape[2]))
        started = _split_start(parts, False, "rs_start_" + next(iter(grads_by_name)))
        for w, n in enumerate(grads_by_name):
            exchanges[n] = (started, w)
        return started[4]

    small_p = {n: wts[n] for n in SMALL}
    loss, grad_x, small_g = _local_step(x[0], mem[0], loss_target[0], small_p, getw, emit, (first[4], rest[4]))

    packed = _pack_small(small_g, loss)
    small_gather = _split_start([packed], True, "ag_start_small")

    grads, delta, new_m, new_v = {}, {}, {}, {}

    def update(n, after):
        lands = [_split_wait(*exchanges[e], after, False, "rs_wait_" + e) for e in exchanges if e.startswith(n)]
        g, d, m2, v2 = _adamw(lands, wts[n][0], mom[n][0], var[n][0], "adamw_" + n)
        grads[n], delta[n], new_m[n], new_v[n] = g[None], d[None], m2[None], v2[None]
        return d

    after = small_gather[4]
    for n in exchanges:
        if not n.startswith("w_in"):
            after = update(n, after)

    gathered = _split_wait(small_gather, 0, after, True, "ag_wait_small")
    small_out, total = _adamw_small(gathered, small_p, mom, var)
    for name, (g, d, m2, v2) in small_out.items():
        grads[name], delta[name], new_m[name], new_v[name] = g, d, m2, v2
    update("w_in", total)

    return (total[0, 0], grad_x[None], *[grads[n] for n in WEIGHTS], *[delta[n] for n in WEIGHTS],
            *[new_m[n] for n in WEIGHTS], *[new_v[n] for n in WEIGHTS])
```

```python
import functools
import math

import jax
import jax.numpy as jnp
from jax import lax
from jax.experimental import pallas as pl
from jax.experimental.pallas import tpu as pltpu

F32 = jnp.float32
BF16 = jnp.bfloat16

D_MODEL = 1024
N_MEM = 256
D_FF = 4096
HEAD = 64
SWA_HEADS = 8
SWA_BLOCK = 128
MEM_HEADS = 4
MEM_HEAD = 128
EPS = 1e-6
NEG = -1e30
REL_BUCKETS = 32
REL_MAX_DIST = 128

ADAM_LR = 0.001
ADAM_B1 = 0.9
ADAM_B2 = 0.999
ADAM_EPS = 1e-08
ADAM_WD = 0.01
ADAM_STEP = 10

GL0, QF0, KF0, VF0, QM0, QA0, KA0, VA0, FL0 = 0, 3072, 3584, 4096, 4608, 5120, 5632, 5760, 5888
PROJ_W = 6144
HALF_W = 3072
H_QF, H_KF, H_VF, H_QM, H_QA, H_KA, H_VA, H_FL = 0, 512, 1024, 1536, 2048, 2560, 2688, 2816

VMEM_LIMIT = 56 * 1024 * 1024
N_DEV = 8
MESH = pl.DeviceIdType.MESH

NN = (((1,), (0,)), ((), ()))
NT = (((1,), (1,)), ((), ()))
TN = (((0,), (0,)), ((), ()))


def _dot(a, b, dims=NN):
    return lax.dot_general(a, b, dims, preferred_element_type=F32)


def _params(sem):
    return pltpu.CompilerParams(dimension_semantics=sem, vmem_limit_bytes=VMEM_LIMIT)


def _full(shape):
    nd = len(shape)
    return pl.BlockSpec(shape, lambda *_: (0,) * nd)


def _sigmoid(z):
    return 1.0 / (1.0 + jnp.exp(-z))


def _group_mean(v, hd):
    if hd == 128:
        return jnp.mean(v, axis=-1, keepdims=True)
    r = lax.broadcasted_iota(jnp.int32, (128, 128), 0) // HEAD
    c = lax.broadcasted_iota(jnp.int32, (128, 128), 1) // HEAD
    same_head = jnp.where(r == c, 1.0 / HEAD, 0.0).astype(BF16)
    total = None
    rest = v
    for _ in range(2):
        part = rest.astype(BF16)
        rest = rest - part.astype(F32)
        term = _dot(part, same_head)
        total = term if total is None else total + term
    return total


def _mm(a, b, mode, out_dtype, tm, tn, tk, name, column_chunks=False, m_part=(0, 1), after=None):
    if mode == "nn":
        m, k = a.shape
        n = b.shape[1]
    elif mode == "nt":
        m, k = a.shape
        n = b.shape[0]
    else:
        k, m = a.shape
        n = b.shape[1]
    assert mode == "tn" or m_part == (0, 1)
    m //= m_part[1]
    tm, tn, tk = min(tm, m), min(tn, n), min(tk, k)
    m0 = m_part[0] * (m // tm)
    extra = [] if after is None else [after]
    nk = k // tk
    chunk = n // N_DEV
    per_tile = tn // chunk if column_chunks else 1
    dims = {"nn": NN, "nt": NT, "tn": TN}[mode]
    a_spec = pl.BlockSpec((tk, tm), lambda j, i, kk: (kk, m0 + i)) if mode == "tn" else pl.BlockSpec((tm, tk), lambda j, i, kk: (i, kk))
    b_spec = pl.BlockSpec((tn, tk), lambda j, i, kk: (j, kk)) if mode == "nt" else pl.BlockSpec((tk, tn), lambda j, i, kk: (kk, j))

    def body(a_ref, b_ref, *rest):
        o_ref, *acc = rest[len(extra):]
        prod = _dot(a_ref[...].astype(BF16), b_ref[...].astype(BF16), dims)

        def write(res):
            if column_chunks:
                for c in range(per_tile):
                    o_ref[c] = res[:, c * chunk:(c + 1) * chunk].astype(o_ref.dtype)
            else:
                o_ref[...] = res.astype(o_ref.dtype)

        if nk == 1:
            write(prod)
        else:
            acc_ref, = acc
            kk = pl.program_id(2)

            @pl.when(kk == 0)
            def _():
                acc_ref[...] = prod

            @pl.when(kk > 0)
            def _():
                acc_ref[...] += prod

            @pl.when(kk == nk - 1)
            def _():
                write(acc_ref[...])

    return pl.pallas_call(
        body, name=name, grid=(n // tn, m // tm, nk),
        in_specs=[a_spec, b_spec] + [pl.BlockSpec(memory_space=pl.ANY)] * len(extra),
        out_specs=(pl.BlockSpec((per_tile, tm, chunk), lambda j, i, kk: (j, i, 0)) if column_chunks
                   else pl.BlockSpec((tm, tn), lambda j, i, kk: (i, j))),
        out_shape=jax.ShapeDtypeStruct((N_DEV, m, chunk) if column_chunks else (m, n), out_dtype),
        scratch_shapes=[pltpu.VMEM((tm, tn), F32)] if nk > 1 else [],
        compiler_params=_params(("parallel", "parallel", "arbitrary")),
    )(a, b, *extra)


def _mm_tn3(a_list, b_list, name):
    s, m = a_list[0].shape
    n = b_list[0].shape[1]
    tk = min(2048, s)
    nk = s // tk

    def body(*refs):
        a_refs, b_refs, o_refs, acc_refs = refs[0:3], refs[3:6], refs[6:9], refs[9:12]
        kk = pl.program_id(0)
        for a_ref, b_ref, o_ref, acc_ref in zip(a_refs, b_refs, o_refs, acc_refs):
            prod = _dot(a_ref[...], b_ref[...], TN)
            if nk == 1:
                o_ref[...] = prod.astype(o_ref.dtype)
                continue

            @pl.when(kk == 0)
            def _(acc_ref=acc_ref, prod=prod):
                acc_ref[...] = prod

            @pl.when(kk > 0)
            def _(acc_ref=acc_ref, prod=prod):
                acc_ref[...] += prod

            @pl.when(kk == nk - 1)
            def _(acc_ref=acc_ref, o_ref=o_ref):
                o_ref[...] = acc_ref[...].astype(o_ref.dtype)

    return pl.pallas_call(
        body, name=name, grid=(nk,),
        in_specs=[pl.BlockSpec((tk, m), lambda kk: (kk, 0))] * 3 + [pl.BlockSpec((tk, n), lambda kk: (kk, 0))] * 3,
        out_specs=[_full((m, n))] * 3,
        out_shape=[jax.ShapeDtypeStruct((m, n), BF16)] * 3,
        scratch_shapes=[pltpu.VMEM((m, n), F32)] * 3,
        compiler_params=_params(("arbitrary",)),
    )(*a_list, *b_list)


def _rms_fwd(x, g, name, deps=()):
    s, d = x.shape
    tm = min(512, s)

    def body(x_ref, g_ref, *rest):
        h_ref = rest[len(deps)]
        xv = x_ref[...]
        r = lax.rsqrt(jnp.mean(xv * xv, axis=-1, keepdims=True) + EPS)
        h_ref[...] = (xv * r * g_ref[...]).astype(BF16)

    return pl.pallas_call(
        body, name=name, grid=(s // tm,),
        in_specs=[pl.BlockSpec((tm, d), lambda i: (i, 0)), _full((1, d))] + [pl.BlockSpec(memory_space=pl.ANY)] * len(deps),
        out_specs=pl.BlockSpec((tm, d), lambda i: (i, 0)),
        out_shape=jax.ShapeDtypeStruct((s, d), BF16),
        compiler_params=_params(("parallel",)),
    )(x, g, *deps)


def _proj_post(proj, gq_fox, gk_fox, gq_mem, gq_swa, gk_swa):
    s = proj.shape[0]
    tm = min(512, s)

    def body(p_ref, gqf, gkf, gqm, gqa, gka, qf_ref, kf_ref, vf_ref, qm_ref, qa_ref, ka_ref, va_ref, qft_ref, vft_ref):
        def norm(off, width, hd, g_ref, o_ref, scaled_t_ref=None):
            for b in range(width // 128):
                v = p_ref[:, off + b * 128: off + (b + 1) * 128].astype(F32)
                r = lax.rsqrt(_group_mean(v * v, hd) + EPS)
                vn = (v * r * g_ref[...]).astype(BF16)
                o_ref[:, b * 128:(b + 1) * 128] = vn
                if scaled_t_ref is not None:
                    scaled_t_ref[b * 128:(b + 1) * 128, :] = (vn.astype(F32) * 0.125).T.astype(BF16)

        norm(H_QF, 512, HEAD, gqf, qf_ref, qft_ref)
        norm(H_KF, 512, HEAD, gkf, kf_ref)
        vf_ref[...] = p_ref[:, H_VF:H_VF + 512].astype(BF16)
        for b in range(4):
            vft_ref[b * 128:(b + 1) * 128, :] = p_ref[:, H_VF + b * 128:H_VF + (b + 1) * 128].astype(F32).T.astype(BF16)
        norm(H_QM, 512, MEM_HEAD, gqm, qm_ref)
        norm(H_QA, 512, HEAD, gqa, qa_ref)
        norm(H_KA, 128, HEAD, gka, ka_ref)
        va_ref[...] = p_ref[:, H_VA:H_VA + 128].astype(BF16)

    g_spec = _full((1, 128))
    o512 = pl.BlockSpec((tm, 512), lambda i: (i, 0))
    o128 = pl.BlockSpec((tm, 128), lambda i: (i, 0))
    s512 = jax.ShapeDtypeStruct((s, 512), BF16)
    s128 = jax.ShapeDtypeStruct((s, 128), BF16)
    return pl.pallas_call(
        body, name="proj_post", grid=(s // tm,),
        in_specs=[pl.BlockSpec((tm, HALF_W), lambda i: (i, 1)), g_spec, g_spec, g_spec, g_spec, g_spec],
        out_specs=[o512, o512, o512, o512, o512, o128, o128] + [pl.BlockSpec((512, tm), lambda i: (0, i))] * 2,
        out_shape=[s512, s512, s512, s512, s512, s128, s128] + [jax.ShapeDtypeStruct((512, s), BF16)] * 2,
        compiler_params=_params(("parallel",)),
    )(proj, gq_fox, gk_fox, gq_mem, gq_swa, gk_swa)


def _tri(n, lower):
    r = lax.broadcasted_iota(jnp.int32, (n, n), 0)
    c = lax.broadcasted_iota(jnp.int32, (n, n), 1)
    return jnp.where((c <= r) if lower else (c >= r), 1.0, 0.0).astype(F32)


def _fox_gate_fwd(proj, b_forget128):
    s = proj.shape[0]
    tm = min(512, s)

    def body(p_ref, b_ref, cc_ref, ca_ref, carry_ref):
        i = pl.program_id(0)

        @pl.when(i == 0)
        def _():
            carry_ref[...] = jnp.zeros_like(carry_ref)

        z = p_ref[...] + b_ref[...]
        logf = jnp.minimum(z, 0.0) - jnp.log(1.0 + jnp.exp(-jnp.abs(z)))
        c = jnp.dot(_tri(tm, True), logf, precision=lax.Precision.HIGHEST, preferred_element_type=F32) + carry_ref[...]
        carry_ref[...] = c[tm - 1:tm, :]
        lane = lax.broadcasted_iota(jnp.int32, (tm, 128), 1)
        for hp in range(4):
            cc_ref[hp] = c if hp == 0 else pltpu.roll(c, 128 - 2 * hp, 1)
            aug = jnp.zeros((tm, 128), F32)
            for e in range(2):
                rest = jnp.broadcast_to(c[:, 2 * hp + e:2 * hp + e + 1], (tm, 128))
                for part in range(3):
                    piece = rest.astype(BF16).astype(F32)
                    aug = jnp.where(lane == HEAD * (1 - e) + part, piece, aug)
                    rest = rest - piece
            ca_ref[hp] = aug.astype(BF16)

    return pl.pallas_call(
        body, name="fox_gate_fwd", grid=(s // tm,),
        in_specs=[pl.BlockSpec((tm, 128), lambda i: (i, 0)), _full((1, 128))],
        out_specs=[pl.BlockSpec((4, tm, 128), lambda i: (0, i, 0))] * 2,
        out_shape=[jax.ShapeDtypeStruct((4, s, 128), F32), jax.ShapeDtypeStruct((4, s, 128), BF16)],
        scratch_shapes=[pltpu.VMEM((1, 128), F32)],
        compiler_params=_params(("arbitrary",)),
    )(proj, b_forget128)


def _memkv_fwd(mem, g_mem, w_kv, kn_mem):
    m = mem.shape[0]

    def body(mem_ref, g_ref, w_ref, kn_ref, memn_ref, kv_ref, mk_ref, mv_ref):
        xv = mem_ref[...]
        r = lax.rsqrt(jnp.mean(xv * xv, axis=-1, keepdims=True) + EPS)
        mn = (xv * r * g_ref[...]).astype(BF16)
        memn_ref[...] = mn
        kv = _dot(mn, w_ref[...])
        kv_ref[...] = kv
        for h in range(MEM_HEADS):
            v = kv[:, h * 128:(h + 1) * 128]
            rr = lax.rsqrt(jnp.mean(v * v, axis=-1, keepdims=True) + EPS)
            mk_ref[:, h * 128:(h + 1) * 128] = (v * rr * kn_ref[...]).astype(BF16)
        mv_ref[...] = kv[:, 512:1024].astype(BF16)

    return pl.pallas_call(
        body, name="memkv_fwd",
        out_shape=[jax.ShapeDtypeStruct((m, D_MODEL), BF16), jax.ShapeDtypeStruct((m, 1024), F32),
                   jax.ShapeDtypeStruct((m, 512), BF16), jax.ShapeDtypeStruct((m, 512), BF16)],
        compiler_params=pltpu.CompilerParams(vmem_limit_bytes=VMEM_LIMIT),
    )(mem, g_mem, w_kv, kn_mem)


def _bias_table(rel_bias, bucket):
    def body(rb_ref, bk_ref, o_ref):
        bk = bk_ref[...]
        for h in range(SWA_HEADS):
            acc = jnp.zeros(bk.shape, F32)
            for b in range(REL_BUCKETS):
                acc = jnp.where(bk == b, rb_ref[b, h], acc)
            o_ref[h] = acc

    return pl.pallas_call(
        body, name="bias_table",
        in_specs=[pl.BlockSpec(memory_space=pltpu.SMEM), pl.BlockSpec(memory_space=pltpu.VMEM)],
        out_shape=jax.ShapeDtypeStruct((SWA_HEADS,) + bucket.shape, F32),
    )(rel_bias, bucket)


def _swa_valid(n):
    row = lax.broadcasted_iota(jnp.int32, (SWA_BLOCK, 2 * SWA_BLOCK), 0)
    col = lax.broadcasted_iota(jnp.int32, (SWA_BLOCK, 2 * SWA_BLOCK), 1)
    dist = row + SWA_BLOCK - col
    return (dist >= 0) & (dist < SWA_BLOCK) & ((col >= SWA_BLOCK) | (n > 0))


def _swa_fwd(qa, kp, vp, bias, sink):
    s = qa.shape[0]
    nb = s // SWA_BLOCK

    def body(sink_ref, q_ref, kp_ref, vp_ref, bias_ref, o_ref):
        n = pl.program_id(0)
        start = pl.multiple_of(n * SWA_BLOCK, SWA_BLOCK)
        k2 = kp_ref[pl.ds(start, 2 * SWA_BLOCK), :]
        v2 = vp_ref[pl.ds(start, 2 * SWA_BLOCK), :]
        valid = _swa_valid(n)
        heads = range(SWA_HEADS)
        hs = lambda h: slice(h * HEAD, (h + 1) * HEAD)
        sc = [jnp.where(valid, _dot(q_ref[:, hs(h)], k2[:, hs(h // 4)], NT) * 0.125 + bias_ref[h], NEG) for h in heads]
        pn = []
        for h in heads:
            sk = sink_ref[h]
            mx = jnp.maximum(jnp.max(sc[h], axis=-1, keepdims=True), sk)
            p = jnp.exp(sc[h] - mx)
            den = jnp.sum(p, axis=-1, keepdims=True) + jnp.exp(sk - mx)
            pn.append((p / den).astype(BF16))
        outs = [_dot(pn[h], v2[:, hs(h // 4)]).astype(BF16) for h in heads]
        for h in heads:
            o_ref[:, hs(h)] = outs[h]

    return pl.pallas_call(
        body, name="swa_fwd", grid=(nb,),
        in_specs=[pl.BlockSpec(memory_space=pltpu.SMEM),
                  pl.BlockSpec((SWA_BLOCK, 512), lambda n: (n, 0)),
                  _full(kp.shape), _full(vp.shape), _full(bias.shape)],
        out_specs=pl.BlockSpec((SWA_BLOCK, 512), lambda n: (n, 0)),
        out_shape=jax.ShapeDtypeStruct((s, 512), BF16),
        compiler_params=_params(("parallel",)),
    )(sink, qa, kp, vp, bias)


def _head_mask(e):
    lane = lax.broadcasted_iota(jnp.int32, (1, 128), 1)
    return (lane >= e * HEAD) & (lane < (e + 1) * HEAD)


FOX_FWD_T = 1024
FOX_BWD_T = 512


def _head_rows(e):
    row = lax.broadcasted_iota(jnp.int32, (128, 1), 0)
    return (row >= e * HEAD) & (row < (e + 1) * HEAD)


def _fox_fwd(q, k, v_t, ca4):
    s = q.shape[0]
    t = min(FOX_FWD_T, s)
    nq = s // t

    def body(q_ref, k_ref, vt_ref, ca_ref, o_ref, lse_ref, ot_ref):
        i = pl.program_id(1)
        qs = q_ref[...] * jnp.asarray(0.125, BF16)
        lane = lax.broadcasted_iota(jnp.int32, (1, 128), 1)
        minus = [jnp.where((lane >= HEAD * (1 - e)) & (lane < HEAD * (1 - e) + 3), -1.0, 0.0).astype(BF16) for e in range(2)]
        qe = [jnp.where(_head_mask(e), qs, jnp.broadcast_to(minus[e], qs.shape)) for e in range(2)]

        def block(carry, key0, nkeys, q0, nqs, masked):
            ks = pl.ds(pl.multiple_of(key0, 128), nkeys)
            kj = k_ref[ks, :]
            caj = ca_ref[0, ks, :]
            vtj = vt_ref[:, ks]
            out = []
            for e in range(2):
                m_all, acc_all = carry[2 * e], carry[2 * e + 1]
                m, acc = m_all[:, q0:q0 + nqs], acc_all[:, q0:q0 + nqs]
                st = _dot(jnp.where(_head_mask(e), kj, caj), qe[e][q0:q0 + nqs, :], NT)
                if masked:
                    krow = lax.broadcasted_iota(jnp.int32, (nkeys, nqs), 0) + key0
                    qcol = lax.broadcasted_iota(jnp.int32, (nkeys, nqs), 1) + (i * t + q0)
                    st = jnp.where(krow <= qcol, st, NEG)
                m_new = jnp.maximum(m, jnp.max(st, axis=0, keepdims=True))
                alpha = jnp.exp(m - m_new)
                pt = jnp.exp(st - m_new).astype(BF16)
                vte = jnp.where(_head_rows(e), vtj, jnp.ones_like(vtj))
                acc_new = alpha * acc + _dot(vte, pt)
                if nqs < t:
                    m_new = jnp.concatenate([m_all[:, :q0], m_new], axis=1)
                    acc_new = jnp.concatenate([acc_all[:, :q0], acc_new], axis=1)
                out += [m_new, acc_new]
            return tuple(out)

        half = t // 2
        init = (jnp.full((1, t), NEG, F32), jnp.zeros((128, t), F32)) * 2
        carry = lax.fori_loop(0, i, lambda j, c: block(c, j * t, t, 0, t, False), init)
        carry = block(carry, i * t, half, 0, t, True)
        m0, a0, m1, a1 = block(carry, i * t + half, half, half, half, True)
        l0 = a0[HEAD:HEAD + 1, :]
        l1 = a1[0:1, :]
        o_t = jnp.where(_head_rows(0), a0 / l0, a1 / l1)
        o_ref[...] = o_t.T.astype(BF16)
        ot_ref[...] = o_t.astype(BF16)
        r8 = lax.broadcasted_iota(jnp.int32, (8, t), 0)
        lse_ref[0] = jnp.where(r8 == 0, m0 + jnp.log(l0), jnp.where(r8 == 1, m1 + jnp.log(l1), 0.0))

    return pl.pallas_call(
        body, name="fox_fwd", grid=(4, nq),
        in_specs=[pl.BlockSpec((t, 128), lambda hp, i: (i, hp)),
                  pl.BlockSpec((s, 128), lambda hp, i: (0, hp)),
                  pl.BlockSpec((128, s), lambda hp, i: (hp, 0)),
                  pl.BlockSpec((1, s, 128), lambda hp, i: (hp, 0, 0))],
        out_specs=[pl.BlockSpec((t, 128), lambda hp, i: (i, hp)),
                   pl.BlockSpec((1, 8, t), lambda hp, i: (hp, 0, i)),
                   pl.BlockSpec((128, t), lambda hp, i: (hp, i))],
        out_shape=[jax.ShapeDtypeStruct((s, 512), BF16), jax.ShapeDtypeStruct((4, 8, s), F32),
                   jax.ShapeDtypeStruct((512, s), BF16)],
        compiler_params=_params(("parallel", "parallel")),
    )(q, k, v_t, ca4)


MEM_SCALE = MEM_HEAD ** -0.5


def _mem_fwd(qm, mk, mv):
    s = qm.shape[0]
    tq = min(512, s)

    def body(q_ref, mk_ref, mv_ref, o_ref):
        for h in range(MEM_HEADS):
            hs = slice(h * 128, (h + 1) * 128)
            sc = _dot(q_ref[:, hs], mk_ref[:, hs], NT) * MEM_SCALE
            mx = jnp.max(sc, axis=-1, keepdims=True)
            p = jnp.exp(sc - mx)
            p = p / jnp.sum(p, axis=-1, keepdims=True)
            o_ref[:, hs] = _dot(p.astype(BF16), mv_ref[:, hs]).astype(BF16)

    return pl.pallas_call(
        body, name="mem_fwd", grid=(s // tq,),
        in_specs=[pl.BlockSpec((tq, 512), lambda i: (i, 0)), _full(mk.shape), _full(mv.shape)],
        out_specs=pl.BlockSpec((tq, 512), lambda i: (i, 0)),
        out_shape=jax.ShapeDtypeStruct((s, 512), BF16),
        compiler_params=_params(("parallel",)),
    )(qm, mk, mv)


def _merge_fwd(x, oa, of, om, proj, b_gate, wa, wf, wm, w_out, g_mlp):
    s = x.shape[0]
    tm = min(512, s)

    def body(x_ref, oa_ref, of_ref, om_ref, gl_ref, bg_ref, wa_ref, wf_ref, wm_ref, wo_ref, g_ref, x1_ref, hm_ref, mg_ref):
        merged = None
        for b, (o_ref, w_ref) in enumerate(((oa_ref, wa_ref), (of_ref, wf_ref), (om_ref, wm_ref))):
            cs = slice(b * D_MODEL, (b + 1) * D_MODEL)
            y = _dot(o_ref[...], w_ref[...])
            t = _sigmoid(gl_ref[:, cs].astype(F32) + bg_ref[:, cs]) * y
            merged = t if merged is None else merged + t
        mb = merged.astype(BF16)
        mg_ref[...] = mb
        x1 = x_ref[...] + _dot(mb, wo_ref[...])
        x1_ref[...] = x1
        r = lax.rsqrt(jnp.mean(x1 * x1, axis=-1, keepdims=True) + EPS)
        hm_ref[...] = (x1 * r * g_ref[...]).astype(BF16)

    row = lambda w: pl.BlockSpec((tm, w), lambda i: (i, 0))
    return pl.pallas_call(
        body, name="merge_fwd", grid=(s // tm,),
        in_specs=[row(D_MODEL), row(512), row(512), row(512), row(HALF_W), _full((1, HALF_W)),
                  _full(wa.shape), _full(wf.shape), _full(wm.shape), _full(w_out.shape), _full((1, D_MODEL))],
        out_specs=[row(D_MODEL), row(D_MODEL), row(D_MODEL)],
        out_shape=[jax.ShapeDtypeStruct((s, D_MODEL), F32), jax.ShapeDtypeStruct((s, D_MODEL), BF16),
                   jax.ShapeDtypeStruct((s, D_MODEL), BF16)],
        compiler_params=_params(("parallel",)),
    )(x, oa, of, om, proj, b_gate, wa, wf, wm, w_out, g_mlp)


def _mlp_up(hm, w_up):
    s = hm.shape[0]
    tm, tn = min(1024, s), w_up.shape[2]

    def body(h_ref, w_ref, u_ref):
        r = jnp.maximum(_dot(h_ref[...], w_ref[0]), 0.0)
        u_ref[...] = (r * r).astype(BF16)

    return pl.pallas_call(
        body, name="mlp_up", grid=(s // tm, D_FF // tn),
        in_specs=[pl.BlockSpec((tm, D_MODEL), lambda i, j: (i, 0)), pl.BlockSpec((1, D_MODEL, tn), lambda i, j: (j, 0, 0))],
        out_specs=pl.BlockSpec((tm, tn), lambda i, j: (i, j)),
        out_shape=jax.ShapeDtypeStruct((s, D_FF), BF16),
        compiler_params=_params(("parallel", "parallel")),
    )(hm, w_up)


def _mlp_down_loss(u, w_down, x1, target):
    s = u.shape[0]
    tm = min(512, s)

    def body(u_ref, w_ref, x1_ref, t_ref, dy_ref, dyb_ref, loss_ref):
        i = pl.program_id(0)

        @pl.when(i == 0)
        def _():
            loss_ref[...] = jnp.zeros_like(loss_ref)

        y = x1_ref[...] + _dot(u_ref[...], w_ref[...])
        err = y - t_ref[...]
        dy = err * (1.0 / D_MODEL)
        dy_ref[...] = dy
        dyb_ref[...] = dy.astype(BF16)
        part = jnp.sum(jnp.sum(err * err, axis=-1, keepdims=True) * (1.0 / D_MODEL), axis=0, keepdims=True)
        loss_ref[...] += 0.5 * part

    row = pl.BlockSpec((tm, D_MODEL), lambda i: (i, 0))
    return pl.pallas_call(
        body, name="mlp_down_loss", grid=(s // tm,),
        in_specs=[pl.BlockSpec((tm, D_FF), lambda i: (i, 0)), _full(w_down.shape), row, row],
        out_specs=[row, row, _full((1, 1))],
        out_shape=[jax.ShapeDtypeStruct((s, D_MODEL), F32), jax.ShapeDtypeStruct((s, D_MODEL), BF16),
                   jax.ShapeDtypeStruct((1, 1), F32)],
        compiler_params=_params(("arbitrary",)),
    )(u, w_down, x1, target)


def _mlp_bwd_act(dy, w_down, u):
    s = dy.shape[0]
    tm, tn = min(1024, s), 1024

    def body(dy_ref, w_ref, u_ref, da_ref):
        du = _dot(dy_ref[...], w_ref[...], NT)
        da_ref[...] = (du * (2.0 * jnp.sqrt(u_ref[...].astype(F32)))).astype(BF16)

    return pl.pallas_call(
        body, name="mlp_bwd_act", grid=(D_FF // tn, s // tm),
        in_specs=[pl.BlockSpec((tm, D_MODEL), lambda j, i: (i, 0)), pl.BlockSpec((tn, D_MODEL), lambda j, i: (j, 0)),
                  pl.BlockSpec((tm, tn), lambda j, i: (i, j))],
        out_specs=pl.BlockSpec((tm, tn), lambda j, i: (i, j)),
        out_shape=jax.ShapeDtypeStruct((s, D_FF), BF16),
        compiler_params=_params(("parallel", "parallel")),
    )(dy, w_down, u)


def _rms_bwd(xv, g, dh, skip):
    r = lax.rsqrt(jnp.mean(xv * xv, axis=-1, keepdims=True) + EPS)
    n = xv * r
    dn = dh * g
    dx = skip + r * (dn - n * jnp.mean(dn * n, axis=-1, keepdims=True))
    return dx, jnp.sum(dh * n, axis=0, keepdims=True)


def _mlp_bwd_x(da, w_up, x1, dy, g_mlp):
    s = da.shape[0]
    tm = min(512, s)

    def body(da_ref, w_ref, x1_ref, dy_ref, g_ref, dx1_ref, dg_ref):
        i = pl.program_id(0)

        @pl.when(i == 0)
        def _():
            dg_ref[...] = jnp.zeros_like(dg_ref)

        tn = w_ref.shape[2]
        dhm = _dot(da_ref[:, 0:tn], w_ref[0], NT)
        for j in range(1, N_DEV):
            dhm = dhm + _dot(da_ref[:, j * tn:(j + 1) * tn], w_ref[j], NT)
        dx, dg = _rms_bwd(x1_ref[...], g_ref[...], dhm, dy_ref[...])
        dx1_ref[...] = dx
        dg_ref[...] += dg

    row = pl.BlockSpec((tm, D_MODEL), lambda i: (i, 0))
    return pl.pallas_call(
        body, name="mlp_bwd_x", grid=(s // tm,),
        in_specs=[pl.BlockSpec((tm, D_FF), lambda i: (i, 0)), _full(w_up.shape), row, row, _full((1, D_MODEL))],
        out_specs=[row, _full((1, D_MODEL))],
        out_shape=[jax.ShapeDtypeStruct((s, D_MODEL), F32), jax.ShapeDtypeStruct((1, D_MODEL), F32)],
        compiler_params=_params(("arbitrary",)),
    )(da, w_up, x1, dy, g_mlp)


def _merge_bwd(dx1, oa, of, om, proj, b_gate, wa, wf, wm, w_out):
    s = dx1.shape[0]
    tm = min(512, s)

    def body(dx1_ref, oa_ref, of_ref, om_ref, gl_ref, bg_ref, wa_ref, wf_ref, wm_ref, wo_ref,
             dp_ref, doa_ref, dof_ref, dom_ref, dya_ref, dyf_ref, dym_ref, dbg_ref):
        i = pl.program_id(0)

        @pl.when(i == 0)
        def _():
            dbg_ref[...] = jnp.zeros_like(dbg_ref)

        dmerged = _dot(dx1_ref[...].astype(BF16), wo_ref[...], NT)
        branches = ((oa_ref, wa_ref, doa_ref, dya_ref), (of_ref, wf_ref, dof_ref, dyf_ref), (om_ref, wm_ref, dom_ref, dym_ref))
        for b, (o_ref, w_ref, do_ref, dyb_ref) in enumerate(branches):
            cs = slice(b * D_MODEL, (b + 1) * D_MODEL)
            y = _dot(o_ref[...], w_ref[...])
            g = _sigmoid(gl_ref[:, cs].astype(F32) + bg_ref[:, cs])
            dz = (dmerged * y) * g * (1.0 - g)
            dp_ref[:, cs] = dz.astype(BF16)
            dbg_ref[:, cs] += jnp.sum(dz, axis=0, keepdims=True)
            dyb = (dmerged * g).astype(BF16)
            dyb_ref[...] = dyb
            do = _dot(dyb, w_ref[...], NT)
            do_ref[...] = (do.T if b == 1 else do).astype(BF16)

    row = lambda w: pl.BlockSpec((tm, w), lambda i: (i, 0))
    sd = lambda w: jax.ShapeDtypeStruct((s, w), BF16)
    return pl.pallas_call(
        body, name="merge_bwd", grid=(s // tm,),
        in_specs=[row(D_MODEL), row(512), row(512), row(512), row(HALF_W), _full((1, HALF_W)),
                  _full(wa.shape), _full(wf.shape), _full(wm.shape), _full(w_out.shape)],
        out_specs=[row(HALF_W), row(512), pl.BlockSpec((512, tm), lambda i: (0, i)), row(512),
                   row(D_MODEL), row(D_MODEL), row(D_MODEL), _full((1, HALF_W))],
        out_shape=[sd(PROJ_W), sd(512), jax.ShapeDtypeStruct((512, s), BF16), sd(512), sd(D_MODEL), sd(D_MODEL), sd(D_MODEL),
                   jax.ShapeDtypeStruct((1, HALF_W), F32)],
        compiler_params=_params(("arbitrary",)),
    )(dx1, oa, of, om, proj, b_gate, wa, wf, wm, w_out)


def _swa_valid_t(n):
    key = lax.broadcasted_iota(jnp.int32, (2 * SWA_BLOCK, SWA_BLOCK), 0)
    qry = lax.broadcasted_iota(jnp.int32, (2 * SWA_BLOCK, SWA_BLOCK), 1)
    dist = qry + SWA_BLOCK - key
    return (dist >= 0) & (dist < SWA_BLOCK) & ((key >= SWA_BLOCK) | (n > 0))


def _swa_bwd(qa, kp, vp, bias_t, sink, doa):
    s = qa.shape[0]
    nb = s // SWA_BLOCK

    def body(sink_ref, q_ref, kp_ref, vp_ref, bias_ref, do_ref, dq_ref, dkp_ref, dvp_ref, dbias_ref, dsink_ref, sk_acc):
        n = pl.program_id(0)

        @pl.when(n == 0)
        def _():
            dkp_ref[...] = jnp.zeros_like(dkp_ref)
            dvp_ref[...] = jnp.zeros_like(dvp_ref)
            dbias_ref[...] = jnp.zeros_like(dbias_ref)
            sk_acc[...] = jnp.zeros_like(sk_acc)

        start = pl.multiple_of(n * SWA_BLOCK, SWA_BLOCK)
        win = pl.ds(start, 2 * SWA_BLOCK)
        k2 = kp_ref[win, :]
        v2 = vp_ref[win, :]
        valid = _swa_valid_t(n)
        heads = range(SWA_HEADS)
        hs = lambda h: slice(h * HEAD, (h + 1) * HEAD)
        scale = jnp.asarray(0.125, BF16)
        q = [q_ref[:, hs(h)] for h in heads]
        do = [do_ref[:, hs(h)] for h in heads]
        kk = [k2[:, hs(kv)] for kv in range(2)]
        vv = [v2[:, hs(kv)] for kv in range(2)]
        kt = [(kk[kv].astype(F32) * 0.125).T.astype(BF16) for kv in range(2)]
        st = [jnp.where(valid, _dot(kk[h // 4], q[h], NT) * 0.125 + bias_ref[h], NEG) for h in heads]
        dpt = [_dot(vv[h // 4], do[h], NT) for h in heads]
        pt, dst = [], []
        for h in heads:
            sk = sink_ref[h]
            mx = jnp.maximum(jnp.max(st[h], axis=0, keepdims=True), sk)
            p = jnp.exp(st[h] - mx)
            esk = jnp.exp(sk - mx)
            den = jnp.sum(p, axis=0, keepdims=True) + esk
            p = p / den
            delta = jnp.sum(p * dpt[h], axis=0, keepdims=True)
            d = p * (dpt[h] - delta)
            sk_acc[h:h + 1, :] += -(esk / den) * delta
            dbias_ref[h] += d
            pt.append(p.astype(BF16))
            dst.append(d.astype(BF16))
        dq_t = [_dot(kt[h // 4], dst[h]) for h in heads]
        dq_ref[...] = jnp.concatenate(dq_t, axis=0).T.astype(BF16)
        for kv in range(2):
            group = range(4 * kv, 4 * kv + 4)
            dk = [_dot(dst[h], q[h] * scale) for h in group]
            dv = [_dot(pt[h], do[h]) for h in group]
            dkp_ref[win, hs(kv)] += (dk[0] + dk[1]) + (dk[2] + dk[3])
            dvp_ref[win, hs(kv)] += (dv[0] + dv[1]) + (dv[2] + dv[3])

        @pl.when(n == nb - 1)
        def _():
            dsink_ref[...] = jnp.broadcast_to(jnp.sum(sk_acc[...], axis=1, keepdims=True), dsink_ref.shape)

    return pl.pallas_call(
        body, name="swa_bwd", grid=(nb,),
        in_specs=[pl.BlockSpec(memory_space=pltpu.SMEM),
                  pl.BlockSpec((SWA_BLOCK, 512), lambda n: (n, 0)),
                  _full(kp.shape), _full(vp.shape), _full(bias_t.shape),
                  pl.BlockSpec((SWA_BLOCK, 512), lambda n: (n, 0))],
        out_specs=[pl.BlockSpec((SWA_BLOCK, 512), lambda n: (n, 0)), _full(kp.shape), _full(vp.shape),
                   _full(bias_t.shape), _full((SWA_HEADS, 128))],
        out_shape=[jax.ShapeDtypeStruct((s, 512), BF16), jax.ShapeDtypeStruct(kp.shape, F32),
                   jax.ShapeDtypeStruct(vp.shape, F32), jax.ShapeDtypeStruct(bias_t.shape, F32),
                   jax.ShapeDtypeStruct((SWA_HEADS, 128), F32)],
        scratch_shapes=[pltpu.VMEM((SWA_HEADS, 128), F32)],
        compiler_params=_params(("arbitrary",)),
    )(sink, qa, kp, vp, bias_t, doa)


def _fox_bwd(qt, k, v, dot, ot, cc4, lse4):
    s = k.shape[0]
    t = min(FOX_BWD_T, s)
    nq = s // t

    def body(qt_ref, k_ref, v_ref, dot_ref, ot_ref, cc_ref, lse_ref,
             dqt_ref, dk_ref, dv_ref, dck_ref, dcq_ref, delta_ref, dkt_acc, dvt_acc, ds0, ds1):
        j = pl.program_id(1)

        @pl.when(j == 0)
        def _():
            dqt_ref[...] = jnp.zeros_like(dqt_ref)
            dcq_ref[...] = jnp.zeros_like(dcq_ref)
            r8 = lax.broadcasted_iota(jnp.int32, (8, t), 0)

            def dl(i, c):
                cols = pl.ds(pl.multiple_of(i * t, t), t)
                pr = dot_ref[:, cols].astype(F32) * ot_ref[:, cols].astype(F32)
                d0 = jnp.sum(jnp.where(_head_rows(0), pr, 0.0), axis=0, keepdims=True)
                d1 = jnp.sum(jnp.where(_head_rows(1), pr, 0.0), axis=0, keepdims=True)
                delta_ref[:, cols] = jnp.where(r8 == 0, d0, jnp.where(r8 == 1, d1, 0.0))
                return c

            lax.fori_loop(0, nq, dl, 0)

        kj = k_ref[...]
        vj = v_ref[...]
        ks = pl.ds(pl.multiple_of(j * t, t), t)
        kt = (kj.astype(F32) * 0.125).T.astype(BF16)
        ke = [jnp.where(_head_mask(e), kj, jnp.zeros_like(kj)) for e in range(2)]
        ve = [jnp.where(_head_mask(e), vj, jnp.zeros_like(vj)) for e in range(2)]
        ck = [cc_ref[0, ks, e:e + 1] for e in range(2)]
        for r in (dkt_acc, dvt_acc, ds0, ds1):
            r[...] = jnp.zeros_like(r)

        def block(q0, nqs, k0, nks, masked):
            cols = pl.ds(pl.multiple_of(q0, 128), nqs)
            rows = slice(k0, k0 + nks)
            qti = qt_ref[:, cols]
            doti = dot_ref[:, cols]
            for e, ds_acc in enumerate((ds0, ds1)):
                dims = slice(e * HEAD, (e + 1) * HEAD)
                st = _dot(ke[e][rows, :], qti) - ck[e][rows, :]
                if masked:
                    krow = lax.broadcasted_iota(jnp.int32, (nks, nqs), 0) + (j * t + k0)
                    qcol = lax.broadcasted_iota(jnp.int32, (nks, nqs), 1) + q0
                    st = jnp.where(krow <= qcol, st, NEG)
                pt = jnp.exp(st - lse_ref[0, e:e + 1, cols])
                dpt = _dot(ve[e][rows, :], doti)
                dst = pt * (dpt - delta_ref[e:e + 1, cols])
                dsb = dst.astype(BF16)
                dvt_acc[dims, rows] += _dot(doti[dims, :], pt.astype(BF16), NT)
                dkt_acc[dims, rows] += _dot(qti[dims, :], dsb, NT)
                dqt_ref[dims, cols] += _dot(kt[dims, rows], dsb)
                ds_acc[rows, 0:nqs] += dst
                dcq_ref[0, e:e + 1, cols] += jnp.sum(dst, axis=0, keepdims=True)

        half = t // 2
        block(j * t, half, 0, half, True)
        block(j * t + half, half, 0, t, True)

        def rest(i, c):
            block(i * t, t, 0, t, False)
            return c

        lax.fori_loop(j + 1, nq, rest, 0)
        dk_ref[...] = dkt_acc[...].T.astype(BF16)
        dv_ref[...] = dvt_acc[...].T.astype(BF16)
        lane = lax.broadcasted_iota(jnp.int32, (t, 128), 1)
        c0 = jnp.sum(ds0[...], axis=-1, keepdims=True)
        c1 = jnp.sum(ds1[...], axis=-1, keepdims=True)
        dck_ref[0] = jnp.where(lane == 0, c0, jnp.where(lane == 1, c1, 0.0))

    res_t = lambda: pl.BlockSpec((128, s), lambda hp, j: (hp, 0))
    blk = lambda: pl.BlockSpec((t, 128), lambda hp, j: (j, hp))
    return pl.pallas_call(
        body, name="fox_bwd", grid=(4, nq),
        in_specs=[res_t(), blk(), blk(), res_t(), res_t(), pl.BlockSpec((1, s, 128), lambda hp, j: (hp, 0, 0)),
                  pl.BlockSpec((1, 8, s), lambda hp, j: (hp, 0, 0))],
        out_specs=[res_t(), blk(), blk(),
                   pl.BlockSpec((1, t, 128), lambda hp, j: (hp, j, 0)),
                   pl.BlockSpec((1, 8, s), lambda hp, j: (hp, 0, 0))],
        out_shape=[jax.ShapeDtypeStruct((512, s), F32), jax.ShapeDtypeStruct((s, 512), BF16),
                   jax.ShapeDtypeStruct((s, 512), BF16), jax.ShapeDtypeStruct((4, s, 128), F32),
                   jax.ShapeDtypeStruct((4, 8, s), F32)],
        scratch_shapes=[pltpu.VMEM((8, s), F32)] + [pltpu.VMEM((128, t), F32)] * 2 + [pltpu.VMEM((t, t), F32)] * 2,
        compiler_params=_params(("arbitrary", "arbitrary")),
    )(qt, k, v, dot, ot, cc4, lse4)


def _mem_bwd(qm, mk, mv, dom):
    s = qm.shape[0]
    tq = min(512, s)

    def body(q_ref, mk_ref, mv_ref, do_ref, dq_ref, dmk_ref, dmv_ref):
        i = pl.program_id(0)

        @pl.when(i == 0)
        def _():
            dmk_ref[...] = jnp.zeros_like(dmk_ref)
            dmv_ref[...] = jnp.zeros_like(dmv_ref)

        heads = range(MEM_HEADS)
        hs = lambda h: slice(h * 128, (h + 1) * 128)
        sc = [_dot(q_ref[:, hs(h)], mk_ref[:, hs(h)], NT) * MEM_SCALE for h in heads]
        dp = [_dot(do_ref[:, hs(h)], mv_ref[:, hs(h)], NT) for h in heads]
        pb, dsb = [], []
        for h in heads:
            p = jnp.exp(sc[h] - jnp.max(sc[h], axis=-1, keepdims=True))
            p = p / jnp.sum(p, axis=-1, keepdims=True)
            ds = p * (dp[h] - jnp.sum(p * dp[h], axis=-1, keepdims=True))
            pb.append(p.astype(BF16))
            dsb.append((ds * MEM_SCALE).astype(BF16))
        dq = [_dot(dsb[h], mk_ref[:, hs(h)]).astype(BF16) for h in heads]
        dmk = [_dot(dsb[h], q_ref[:, hs(h)], TN) for h in heads]
        dmv = [_dot(pb[h], do_ref[:, hs(h)], TN) for h in heads]
        for h in heads:
            dq_ref[:, hs(h)] = dq[h]
            dmk_ref[:, hs(h)] += dmk[h]
            dmv_ref[:, hs(h)] += dmv[h]

    return pl.pallas_call(
        body, name="mem_bwd", grid=(s // tq,),
        in_specs=[pl.BlockSpec((tq, 512), lambda i: (i, 0)), _full(mk.shape), _full(mv.shape),
                  pl.BlockSpec((tq, 512), lambda i: (i, 0))],
        out_specs=[pl.BlockSpec((tq, 512), lambda i: (i, 0)), _full(mk.shape), _full(mv.shape)],
        out_shape=[jax.ShapeDtypeStruct((s, 512), BF16), jax.ShapeDtypeStruct(mk.shape, F32),
                   jax.ShapeDtypeStruct(mv.shape, F32)],
        compiler_params=_params(("arbitrary",)),
    )(qm, mk, mv, dom)


def _memkv_bwd(dmk, dmv, kv_raw, kn_mem, mem, g_mem, mem_n, w_kv):
    def body(dmk_ref, dmv_ref, kv_ref, kn_ref, mem_ref, g_ref, mn_ref, w_ref, dw_ref, dkn_ref, dg_ref, dkv_ref):
        dkn = jnp.zeros((1, 128), F32)
        for h in range(MEM_HEADS):
            hs = slice(h * 128, (h + 1) * 128)
            v = kv_ref[:, hs]
            r = lax.rsqrt(jnp.mean(v * v, axis=-1, keepdims=True) + EPS)
            n = v * r
            dn = dmk_ref[:, hs]
            dkn = dkn + jnp.sum(dn * n, axis=0, keepdims=True)
            dng = dn * kn_ref[...]
            dkv_ref[:, hs] = (r * (dng - n * jnp.mean(dng * n, axis=-1, keepdims=True))).astype(BF16)
        dkv_ref[:, 512:1024] = dmv_ref[...].astype(BF16)
        dkn_ref[...] = dkn
        dkv = dkv_ref[...]
        dw_ref[...] = _dot(mn_ref[...], dkv, TN).astype(BF16)
        dmn = _dot(dkv, w_ref[...], NT)
        xv = mem_ref[...]
        r = lax.rsqrt(jnp.mean(xv * xv, axis=-1, keepdims=True) + EPS)
        dg_ref[...] = jnp.sum(dmn * (xv * r), axis=0, keepdims=True)

    m = mem.shape[0]
    return pl.pallas_call(
        body, name="memkv_bwd",
        out_shape=[jax.ShapeDtypeStruct((D_MODEL, 1024), BF16), jax.ShapeDtypeStruct((1, 128), F32),
                   jax.ShapeDtypeStruct((1, D_MODEL), F32)],
        scratch_shapes=[pltpu.VMEM((m, 1024), BF16)],
        compiler_params=pltpu.CompilerParams(vmem_limit_bytes=VMEM_LIMIT),
    )(dmk, dmv, kv_raw, kn_mem, mem, g_mem, mem_n, w_kv)


def _fox_gate_bwd(dcq4, dck4, proj, b_forget128):
    s = dck4.shape[1]
    tm = min(512, s)
    nt = s // tm

    def body(dcq_ref, dck_ref, p_ref, b_ref, dfl_ref, db_ref, carry_ref):
        i = pl.program_id(0)

        @pl.when(i == 0)
        def _():
            carry_ref[...] = jnp.zeros_like(carry_ref)
            db_ref[...] = jnp.zeros_like(db_ref)

        dcv = jnp.zeros((tm, 128), F32)
        for hp in range(4):
            by_query = jnp.concatenate([dcq_ref[hp], jnp.zeros((120, tm), F32)], axis=0).T
            d = by_query - dck_ref[hp]
            dcv = dcv + (d if hp == 0 else pltpu.roll(d, 2 * hp, 1))
        dlogf = jnp.dot(_tri(tm, False), dcv, precision=lax.Precision.HIGHEST, preferred_element_type=F32) + carry_ref[...]
        carry_ref[...] += jnp.sum(dcv, axis=0, keepdims=True)
        z = p_ref[...] + b_ref[...]
        dfl = dlogf * (1.0 / (1.0 + jnp.exp(z)))
        dfl_ref[...] = dfl.astype(BF16)
        db_ref[...] += jnp.sum(dfl, axis=0, keepdims=True)

    return pl.pallas_call(
        body, name="fox_gate_bwd", grid=(nt,),
        in_specs=[pl.BlockSpec((4, 8, tm), lambda i: (0, 0, nt - 1 - i)),
                  pl.BlockSpec((4, tm, 128), lambda i: (0, nt - 1 - i, 0)),
                  pl.BlockSpec((tm, 128), lambda i: (nt - 1 - i, 0)), _full((1, 128))],
        out_specs=[pl.BlockSpec((tm, 128), lambda i: (nt - 1 - i, 0)), _full((1, 128))],
        out_shape=[jax.ShapeDtypeStruct((s, 128), BF16), jax.ShapeDtypeStruct((1, 128), F32)],
        scratch_shapes=[pltpu.VMEM((1, 128), F32)],
        compiler_params=_params(("arbitrary",)),
    )(dcq4, dck4, proj, b_forget128)


def _proj_pre_bwd(dproj, proj, dqf, dkf, dvf, dqm, dqa, dka, dva, dfl, gq_fox, gk_fox, gq_mem, gq_swa, gk_swa):
    s = proj.shape[0]
    tm = min(512, s)

    def body(dp_in, p_ref, dqf_ref, dkf_ref, dvf_ref, dqm_ref, dqa_ref, dka_ref, dva_ref, dfl_ref,
             gqf, gkf, gqm, gqa, gka, dp_ref, dgn_ref):
        i = pl.program_id(0)

        @pl.when(i == 0)
        def _():
            dgn_ref[...] = jnp.zeros_like(dgn_ref)

        def norm_bwd(off, width, hd, g_ref, dn_ref, slot):
            acc = jnp.zeros((1, 128), F32)
            for b in range(width // 128):
                v = p_ref[:, off + b * 128: off + (b + 1) * 128].astype(F32)
                r = lax.rsqrt(_group_mean(v * v, hd) + EPS)
                n = v * r
                dn = dn_ref[b * 128:(b + 1) * 128, :].T if slot == 0 else dn_ref[:, b * 128:(b + 1) * 128].astype(F32)
                acc = acc + jnp.sum(dn * n, axis=0, keepdims=True)
                dng = dn * g_ref[...]
                dp_ref[:, off + b * 128: off + (b + 1) * 128] = (r * (dng - n * _group_mean(dng * n, hd))).astype(BF16)
            dgn_ref[slot:slot + 1, :] += acc

        norm_bwd(H_QF, 512, HEAD, gqf, dqf_ref, 0)
        norm_bwd(H_KF, 512, HEAD, gkf, dkf_ref, 1)
        dp_ref[:, H_VF:H_VF + 512] = dvf_ref[...].astype(BF16)
        norm_bwd(H_QM, 512, MEM_HEAD, gqm, dqm_ref, 2)
        norm_bwd(H_QA, 512, HEAD, gqa, dqa_ref, 3)
        norm_bwd(H_KA, 128, HEAD, gka, dka_ref, 4)
        dp_ref[:, H_VA:H_VA + 128] = dva_ref[...].astype(BF16)
        dp_ref[:, H_FL:H_FL + 128] = dfl_ref[...]
        dp_ref[:, H_FL + 128:HALF_W] = jnp.zeros((tm, HALF_W - H_FL - 128), BF16)

    row = lambda w: pl.BlockSpec((tm, w), lambda i: (i, 0))
    g_spec = _full((1, 128))
    return pl.pallas_call(
        body, name="proj_pre_bwd", grid=(s // tm,),
        in_specs=[pl.BlockSpec(memory_space=pl.ANY), pl.BlockSpec((tm, HALF_W), lambda i: (i, 1)),
                  pl.BlockSpec((512, tm), lambda i: (0, i)), row(512), row(512), row(512), row(512),
                  row(128), row(128), row(128), g_spec, g_spec, g_spec, g_spec, g_spec],
        out_specs=[pl.BlockSpec((tm, HALF_W), lambda i: (i, 1)), _full((8, 128))],
        out_shape=[jax.ShapeDtypeStruct((s, PROJ_W), BF16), jax.ShapeDtypeStruct((8, 128), F32)],
        input_output_aliases={0: 0},
        compiler_params=_params(("arbitrary",)),
    )(dproj, proj, dqf, dkf, dvf, dqm, dqa, dka, dva, dfl, gq_fox, gk_fox, gq_mem, gq_swa, gk_swa)


def _in_bwd_x(dproj, w_in_p, x, g_mix, dx1):
    s = x.shape[0]
    tm = min(256, s)

    def body(dp_ref, w_ref, x_ref, g_ref, dx1_ref, gx_ref, dg_ref):
        i = pl.program_id(0)

        @pl.when(i == 0)
        def _():
            dg_ref[...] = jnp.zeros_like(dg_ref)

        dx, dg = _rms_bwd(x_ref[...], g_ref[...], _dot(dp_ref[...], w_ref[...], NT), dx1_ref[...])
        gx_ref[...] = dx
        dg_ref[...] += dg

    row = pl.BlockSpec((tm, D_MODEL), lambda i: (i, 0))
    return pl.pallas_call(
        body, name="in_bwd_x", grid=(s // tm,),
        in_specs=[pl.BlockSpec((tm, PROJ_W), lambda i: (i, 0)), _full(w_in_p.shape), row, _full((1, D_MODEL)), row],
        out_specs=[row, _full((1, D_MODEL))],
        out_shape=[jax.ShapeDtypeStruct((s, D_MODEL), F32), jax.ShapeDtypeStruct((1, D_MODEL), F32)],
        compiler_params=_params(("arbitrary",)),
    )(dproj, w_in_p, x, g_mix, dx1)


def _rel_bias_bwd(dbias, bucket):
    def body(db_ref, bk_ref, o_ref):
        bk = bk_ref[...]
        lane = lax.broadcasted_iota(jnp.int32, (1, 128), 1)
        for b in range(REL_BUCKETS):
            sel = bk == b
            acc = jnp.zeros((1, 128), F32)
            for h in range(SWA_HEADS):
                tot = jnp.sum(jnp.sum(jnp.where(sel, db_ref[h], 0.0), axis=0, keepdims=True), axis=-1, keepdims=True)
                acc = jnp.where(lane == h, tot, acc)
            o_ref[:, b * 128:(b + 1) * 128] = acc

    return pl.pallas_call(
        body, name="rel_bias_bwd",
        out_shape=jax.ShapeDtypeStruct((1, REL_BUCKETS * 128), F32),
        compiler_params=pltpu.CompilerParams(vmem_limit_bytes=VMEM_LIMIT),
    )(dbias, bucket)


def _my_place():
    return lax.axis_index("x"), lax.axis_index("y"), lax.axis_index("c")


def _peer(place, k):
    x, y, c = place
    return (1 - x if k & 4 else x, 1 - y if k & 2 else y, 1 - c if k & 1 else c)


def _index(place):
    x, y, c = place
    return 4 * x + 2 * y + c


HBM_SPEC = pl.BlockSpec(memory_space=pltpu.HBM)
SEM_SPEC = pl.BlockSpec(memory_space=pltpu.SEMAPHORE)
DATAFLOW = pltpu.SideEffectType.DATAFLOW_SIDE_EFFECTING


ALL_PEERS = tuple(range(1, N_DEV))
SAME_CORE = (2, 4, 6)
OWN = N_DEV - 1


def _split_copy(src_ref, land_ref, send_sems, recv_sems, me, k, gather):
    peer = _peer(me, k)
    if gather:
        src, dst = src_ref, land_ref.at[_index(me)]
    else:
        src, dst = src_ref.at[_index(peer)], land_ref.at[k - 1]
    return pltpu.make_async_remote_copy(src_ref=src, dst_ref=dst, send_sem=send_sems.at[k - 1], recv_sem=recv_sems.at[k - 1],
                                        device_id=peer, device_id_type=MESH)


def _own_copy(src_ref, land_ref, recv_sems, me, gather):
    if gather:
        src, dst = src_ref, land_ref.at[_index(me)]
    else:
        src, dst = src_ref.at[_index(me)], land_ref.at[OWN]
    return pltpu.make_async_copy(src, dst, recv_sems.at[OWN])


def _split_start(srcs, gather, name, peers=ALL_PEERS, after=None):
    n = len(srcs)
    extra = [] if after is None else [after]

    def body(*refs):
        refs = refs[:2 * n] + refs[2 * n + len(extra):]
        src_refs, land_refs = refs[:n], refs[n:2 * n]
        send_sems, recv_sems, token = refs[2 * n:3 * n], refs[3 * n:4 * n], refs[-1]
        me = _my_place()
        for w in range(n):
            for k in peers:
                _split_copy(src_refs[w], land_refs[w], send_sems[w], recv_sems[w], me, k, gather).start()
            _own_copy(src_refs[w], land_refs[w], recv_sems[w], me, gather).start()
        token[...] = jnp.zeros_like(token)

    lands = [lax.empty((N_DEV,) + (a.shape if gather else a.shape[1:]), a.dtype) for a in srcs]
    sems = [pltpu.SemaphoreType.DMA((N_DEV,))] * (2 * n)
    hbm = [pltpu.HBM(a.shape, a.dtype) for a in list(srcs) + lands]
    outs = pl.pallas_call(
        body, name=name,
        out_shape=(*sems, *hbm, jax.ShapeDtypeStruct((8, 128), F32)),
        in_specs=(HBM_SPEC,) * (2 * n) + (pl.BlockSpec(memory_space=pl.ANY),) * len(extra),
        out_specs=(SEM_SPEC,) * (2 * n) + (HBM_SPEC,) * (2 * n) + (pl.BlockSpec(memory_space=pltpu.VMEM),),
        input_output_aliases={i: 2 * n + i for i in range(2 * n)},
        compiler_params=pltpu.CompilerParams(has_side_effects=DATAFLOW),
    )(*[pltpu.with_memory_space_constraint(a, pltpu.HBM) for a in list(srcs) + lands], *extra)
    return list(outs[:n]), list(outs[n:2 * n]), list(outs[2 * n:3 * n]), list(outs[3 * n:4 * n]), outs[-1]


def _split_wait(started, w, after, gather, name):
    send_sems, recv_sems, srcs, lands, _ = started

    def body(src_ref, land_ref, send_sems, recv_sems, after_ref, src_out, land_out):
        me = _my_place()
        for k in ALL_PEERS:
            cp = _split_copy(src_ref, land_ref, send_sems, recv_sems, me, k, gather)
            cp.wait_send()
            cp.wait_recv()
        _own_copy(src_ref, land_ref, recv_sems, me, gather).wait()

    return pl.pallas_call(
        body, name=name,
        out_shape=(pltpu.HBM(srcs[w].shape, srcs[w].dtype), pltpu.HBM(lands[w].shape, lands[w].dtype)),
        in_specs=(HBM_SPEC, HBM_SPEC, SEM_SPEC, SEM_SPEC, pl.BlockSpec(memory_space=pl.ANY)),
        out_specs=(HBM_SPEC, HBM_SPEC), input_output_aliases={0: 0, 1: 1},
        compiler_params=pltpu.CompilerParams(has_side_effects=DATAFLOW),
    )(srcs[w], lands[w], send_sems[w], recv_sems[w], after)[1]


def _forward_copy(land_ref, send_sems, recv_sems, me, j, incoming):
    sibling = _peer(me, 1)
    rows = land_ref.at[_index(_peer(sibling if incoming else me, SAME_CORE[j]))]
    return pltpu.make_async_remote_copy(src_ref=rows, dst_ref=rows, send_sem=send_sems.at[j], recv_sem=recv_sems.at[j],
                                        device_id=sibling, device_id_type=MESH)


def _forward_start(started, after, name):
    send_a, recv_a, srcs, lands, _ = started

    def body(src_ref, land_ref, send_a, recv_a, after_ref, send_b, recv_b, src_out, land_out):
        me = _my_place()
        for j, k in enumerate(SAME_CORE):
            _split_copy(src_ref, land_ref, send_a, recv_a, me, k, True).wait_recv()
            _forward_copy(land_ref, send_b, recv_b, me, j, False).start()

    sems = pltpu.SemaphoreType.DMA((len(SAME_CORE),))
    return pl.pallas_call(
        body, name=name,
        out_shape=(sems, sems, pltpu.HBM(srcs[0].shape, srcs[0].dtype), pltpu.HBM(lands[0].shape, lands[0].dtype)),
        in_specs=(HBM_SPEC, HBM_SPEC, SEM_SPEC, SEM_SPEC, pl.BlockSpec(memory_space=pl.ANY)),
        out_specs=(SEM_SPEC, SEM_SPEC, HBM_SPEC, HBM_SPEC), input_output_aliases={0: 2, 1: 3},
        compiler_params=pltpu.CompilerParams(has_side_effects=DATAFLOW),
    )(srcs[0], lands[0], send_a[0], recv_a[0], after)


def _forward_wait(started, forwarded, name):
    send_a, recv_a, _, _, _ = started
    send_b, recv_b, src, land = forwarded

    def body(src_ref, land_ref, send_a, recv_a, send_b, recv_b, src_out, land_out):
        me = _my_place()
        _own_copy(src_ref, land_ref, recv_a, me, True).wait()
        for k in (1,) + SAME_CORE:
            _split_copy(src_ref, land_ref, send_a, recv_a, me, k, True).wait_send()
        _split_copy(src_ref, land_ref, send_a, recv_a, me, 1, True).wait_recv()
        for j in range(len(SAME_CORE)):
            _forward_copy(land_ref, send_b, recv_b, me, j, False).wait_send()
            _forward_copy(land_ref, send_b, recv_b, me, j, True).wait_recv()

    return pl.pallas_call(
        body, name=name,
        out_shape=(pltpu.HBM(src.shape, src.dtype), pltpu.HBM(land.shape, land.dtype)),
        in_specs=(HBM_SPEC, HBM_SPEC, SEM_SPEC, SEM_SPEC, SEM_SPEC, SEM_SPEC),
        out_specs=(HBM_SPEC, HBM_SPEC), input_output_aliases={0: 0, 1: 1},
        compiler_params=pltpu.CompilerParams(has_side_effects=DATAFLOW),
    )(src, land, send_a[0], recv_a[0], send_b, recv_b)[1]


def _adam_math(w, g, m, v):
    m2 = ADAM_B1 * m + (1.0 - ADAM_B1) * g
    v2 = ADAM_B2 * v + (1.0 - ADAM_B2) * (g * g)
    m_hat = m2 / (1.0 - ADAM_B1 ** ADAM_STEP)
    v_hat = v2 / (1.0 - ADAM_B2 ** ADAM_STEP)
    delta = -ADAM_LR * (m_hat / (jnp.sqrt(v_hat) + ADAM_EPS) + ADAM_WD * w)
    return delta, m2, v2


def _adamw(lands, w, m, v, name):
    a, b = w.shape
    bp = lands[0].shape[2]
    ta = min(128, a)
    per = a // len(lands) // ta

    def body(*refs):
        p_refs = refs[:len(lands)]
        w_ref, m_ref, v_ref, g_ref, d_ref, m2_ref, v2_ref = refs[len(lands):]
        i = pl.program_id(0)

        def run(p_ref):
            g = p_ref[0, :, 0:b].astype(F32)
            for k in range(1, N_DEV):
                g = g + p_ref[k, :, 0:b].astype(F32)
            delta, m2, v2 = _adam_math(w_ref[...], g, m_ref[...], v_ref[...])
            g_ref[...] = g
            d_ref[...] = delta
            m2_ref[...] = m2
            v2_ref[...] = v2

        for part, p_ref in enumerate(p_refs):
            pl.when((i >= part * per) & (i < (part + 1) * per))(functools.partial(run, p_ref))

    land_spec = lambda part: pl.BlockSpec((N_DEV, ta, bp), lambda i: (0, jnp.clip(i - part * per, 0, per - 1), 0))
    blk = pl.BlockSpec((ta, b), lambda i: (i, 0))
    sd = jax.ShapeDtypeStruct((a, b), F32)
    return pl.pallas_call(
        body, name=name, grid=(a // ta,),
        in_specs=[land_spec(part) for part in range(len(lands))] + [blk, blk, blk],
        out_specs=[blk, blk, blk, blk], out_shape=[sd, sd, sd, sd],
        compiler_params=_params(("parallel",)),
    )(*lands, w, m, v)


def _bucket_table():
    t_loc = jnp.arange(SWA_BLOCK)[:, None] + SWA_BLOCK
    s_loc = jnp.arange(2 * SWA_BLOCK)[None, :]
    dist = t_loc - s_loc
    max_exact = REL_BUCKETS // 2
    d = jnp.maximum(dist, 0)
    df = jnp.maximum(d, 1).astype(F32)
    large = max_exact + (jnp.log(df / max_exact) / math.log(REL_MAX_DIST / max_exact) * (REL_BUCKETS - max_exact)).astype(jnp.int32)
    large = jnp.minimum(large, REL_BUCKETS - 1)
    bucket = jnp.where(d < max_exact, d, large)
    band = (dist >= 0) & (dist < SWA_BLOCK)
    return bucket, band


def _tile2(g):
    return jnp.concatenate([g, g], axis=1) if g.shape[1] == HEAD else g


SHARD_W = 737
SHARD_WP = 768
IN_WIDTH = N_DEV * SHARD_W
SEGMENTS = ((GL0, 2824, 3072), (QF0, 768, 512), (KF0, 1280, 512), (VF0, 1792, 512), (QM0, 2312, 512),
            (QA0, 0, 512), (KA0, 512, 128), (VA0, 640, 128), (FL0, 2304, 8))


def _lane_plan(sources):
    plan = []
    for t in range(len(sources) // 128):
        groups = {}
        for lane in range(128):
            src = sources[128 * t + lane]
            if src is not None:
                slab, col = src
                groups.setdefault((slab, col // 128, (lane - col) % 128), []).append(lane)
        tile = []
        for key, lanes in groups.items():
            assert lanes == list(range(lanes[0], lanes[-1] + 1))
            tile.append((key, lanes[0], lanes[-1] + 1))
        plan.append(tile)
    return plan


def _assemble(tile_plan, load, rows):
    lane = lax.broadcasted_iota(jnp.int32, (1, 128), 1)
    out = jnp.zeros((rows, 128), F32)
    for (slab, st, roll), lo, hi in tile_plan:
        v = load(slab, st)
        if roll:
            v = pltpu.roll(v, roll, 1)
        out = v if (lo, hi) == (0, 128) else jnp.where((lane >= lo) & (lane < hi), v, out)
    return out


def _w_in_from_shards(land):
    ref_col = [None] * PROJ_W
    for p0, r0, n in SEGMENTS:
        for i in range(n):
            ref_col[p0 + i] = divmod(r0 + i, SHARD_W)
    plan = _lane_plan(ref_col)
    d_model = land.shape[1]
    tm = 256

    def body(land_ref, o_ref):
        load = lambda slab, st: land_ref[slab, :, st * 128:(st + 1) * 128].astype(F32)
        for t, tile_plan in enumerate(plan):
            o_ref[:, t * 128:(t + 1) * 128] = _assemble(tile_plan, load, tm).astype(BF16)

    return pl.pallas_call(
        body, name="w_in_from_shards", grid=(d_model // tm,),
        in_specs=[pl.BlockSpec((N_DEV, tm, SHARD_WP), lambda i: (0, i, 0))],
        out_specs=pl.BlockSpec((tm, PROJ_W), lambda i: (i, 0)),
        out_shape=jax.ShapeDtypeStruct((d_model, PROJ_W), BF16),
        compiler_params=_params(("parallel",)),
    )(land)


def _dw_in_to_parts(dwp, name):
    padded_col = [None] * IN_WIDTH
    for p0, r0, n in SEGMENTS:
        for i in range(n):
            padded_col[r0 + i] = p0 + i
    sources = []
    for d in range(N_DEV):
        sources += [(0, padded_col[SHARD_W * d + c]) if c < SHARD_W else None for c in range(SHARD_WP)]
    plan = _lane_plan(sources)
    d_model = dwp.shape[0]
    tm = 256
    tiles = SHARD_WP // 128

    def body(dw_ref, o_ref):
        load = lambda slab, st: dw_ref[:, st * 128:(st + 1) * 128].astype(F32)
        for t, tile_plan in enumerate(plan):
            d, c = divmod(t, tiles)
            o_ref[d, :, c * 128:(c + 1) * 128] = _assemble(tile_plan, load, tm).astype(BF16)

    return pl.pallas_call(
        body, name=name, grid=(d_model // tm,),
        in_specs=[pl.BlockSpec((tm, PROJ_W), lambda i: (i, 0))],
        out_specs=pl.BlockSpec((N_DEV, tm, SHARD_WP), lambda i: (0, i, 0)),
        out_shape=jax.ShapeDtypeStruct((N_DEV, d_model, SHARD_WP), BF16),
        compiler_params=_params(("parallel",)),
    )(dwp)


def _cast_shards(shards):
    names = list(shards)

    def body(*refs):
        for src, dst in zip(refs[:len(names)], refs[len(names):]):
            if dst.shape != src.shape:
                dst[...] = jnp.zeros(dst.shape, BF16)
                dst[:, 0:src.shape[1]] = src[...].astype(BF16)
            else:
                dst[...] = src[...].astype(BF16)

    out_shape = [jax.ShapeDtypeStruct((shards[n].shape[0], SHARD_WP if n == "w_in" else shards[n].shape[1]), BF16)
                 for n in names]
    outs = pl.pallas_call(body, name="cast_shards", out_shape=out_shape,
                          compiler_params=pltpu.CompilerParams(vmem_limit_bytes=VMEM_LIMIT))(*[shards[n] for n in names])
    return dict(zip(names, outs))


def _tie(x, *tokens):
    for t in tokens:
        if t is not None:
            x = x + t[0:1, 0:1]
    return x


def _local_step(x, mem, target, p, getw, emit, deps=()):
    s = x.shape[0]
    bucket, band = _bucket_table()
    bucket_m = jnp.where(band, bucket, -1).astype(jnp.int32)
    bias = _bias_table(p["rel_bias"], bucket_m)
    bucket_t = jnp.transpose(bucket_m)
    bias_t = _bias_table(p["rel_bias"], bucket_t)
    gqf, gkf, gqa, gka = _tile2(p["qn_fox"]), _tile2(p["kn_fox"]), _tile2(p["qn_swa"]), _tile2(p["kn_swa"])
    gqm = p["qn_mem"]
    bf128 = jnp.pad(p["b_forget"], ((0, 0), (0, 120)))
    sink = p["sink_swa"].reshape(8)

    h = _rms_fwd(x, p["g_mix"], "rms_mix", tuple(deps) + (bias, bias_t))
    w_in = getw("w_in", h)
    proj = _mm(h, w_in, "nn", BF16, 512, 1536, 1024, "proj")
    fl = _mm(h, w_in[:, FL0:FL0 + 128], "nn", F32, 512, 128, 1024, "proj_fl")
    qf, kf, vf, qm, qa, ka, va, qf_t, vf_t = _proj_post(proj, gqf, gkf, gqm, gqa, gka)
    cc4, ca4 = _fox_gate_fwd(fl, bf128)
    w_kv = getw("w_mem_kv", cc4)
    mem_n, kv_raw, mk, mv = _memkv_fwd(mem, p["g_mem"], w_kv, p["kn_mem"])
    kp = jnp.pad(ka, ((SWA_BLOCK, 0), (0, 0)))
    vp = jnp.pad(va, ((SWA_BLOCK, 0), (0, 0)))
    oa = _swa_fwd(qa, kp, vp, bias, sink)
    of, lse4, of_t = _fox_fwd(qf, kf, vf_t, ca4)
    om = _mem_fwd(qm, mk, mv)
    wa, wf, wm, w_out = getw("w_o_swa", oa), getw("w_o_fox", oa), getw("w_o_mem", oa), getw("w_out", oa)
    x1, hm, merged = _merge_fwd(x, oa, of, om, proj, p["b_gate"], wa, wf, wm, w_out, p["g_mlp"])
    w_up = getw("w_mlp_up", of)
    u = _mlp_up(hm, w_up)
    w_down = getw("w_mlp_down", hm)
    dy, dy_b, loss = _mlp_down_loss(u, w_down, x1, target)

    da = _mlp_bwd_act(dy_b, w_down, u)
    t_down = emit({"w_mlp_down": _mm(u, dy_b, "tn", BF16, 1024, 1024, 2048, "dw_down")})
    dx1, dg_mlp = _mlp_bwd_x(da, w_up, x1, dy, _tie(p["g_mlp"], t_down))
    t_up = emit({"w_mlp_up": _mm(hm, da, "tn", BF16, 1024, 1024, 2048, "dw_up", column_chunks=True)})
    dproj, doa, dof_t, dom, dya, dyf, dym, db_gate = _merge_bwd(
        dx1, oa, of, om, proj, _tie(p["b_gate"], t_up), wa, wf, wm, w_out)
    dw_oa, dw_of, dw_om = _mm_tn3([oa, of, om], [dya, dyf, dym], "dw_o")
    t_o = emit({"w_out": _mm(merged, dx1, "tn", BF16, 1024, 1024, 2048, "dw_out"),
                "w_o_swa": dw_oa, "w_o_fox": dw_of, "w_o_mem": dw_om})

    dqm, dmk, dmv = _mem_bwd(qm, mk, mv, dom)
    dw_kv, dkn_mem, dg_mem = _memkv_bwd(dmk, dmv, kv_raw, _tie(p["kn_mem"], t_o), mem, p["g_mem"], mem_n, w_kv)
    t_kv = emit({"w_mem_kv": dw_kv})
    dqa, dkp, dvp, dbias, dsink = _swa_bwd(qa, kp, vp, bias_t, _tie(p["sink_swa"], t_kv).reshape(8), doa)
    dqf_t, dkf, dvf, dck4, dcq4 = _fox_bwd(qf_t, kf, vf, dof_t, of_t, cc4, lse4)

    dfl, db_forget = _fox_gate_bwd(dcq4, dck4, fl, bf128)

    dproj, dgn = _proj_pre_bwd(dproj, proj, dqf_t, dkf, dvf, dqm, dqa, dkp[SWA_BLOCK:], dvp[SWA_BLOCK:], dfl,
                               gqf, gkf, gqm, gqa, gka)
    t_in = emit({"w_in_a": _mm(h, dproj, "tn", BF16, 512, 3072, 1024, "dw_in_a", m_part=(0, 2))})
    t_in = emit({"w_in_b": _mm(h, dproj, "tn", BF16, 512, 3072, 1024, "dw_in_b", m_part=(1, 2), after=t_in)})
    grad_x, dg_mix = _in_bwd_x(dproj, w_in, x, _tie(p["g_mix"], t_in), dx1)
    d_rel = _rel_bias_bwd(dbias, bucket_t)

    fold = lambda r: dgn[r:r + 1, 0:HEAD] + dgn[r:r + 1, HEAD:128]
    small = {
        "g_mix": dg_mix, "b_gate": db_gate, "b_forget": db_forget[:, 0:8],
        "qn_swa": fold(3), "kn_swa": fold(4), "sink_swa": dsink[:, 0].reshape(1, 8), "rel_bias": d_rel,
        "qn_fox": fold(0), "kn_fox": fold(1), "g_mem": dg_mem, "qn_mem": dgn[2:3, :], "kn_mem": dkn_mem,
        "g_mlp": dg_mlp,
    }
    return loss, grad_x, small


SMALL = ("g_mix", "b_gate", "b_forget", "qn_swa", "kn_swa", "sink_swa", "rel_bias", "qn_fox", "kn_fox", "g_mem",
         "qn_mem", "kn_mem", "g_mlp")
BIG = ("w_in", "w_mem_kv", "w_o_swa", "w_o_fox", "w_o_mem", "w_out", "w_mlp_up", "w_mlp_down")
COL_SHARDED = ("w_in", "w_o_swa", "w_o_fox", "w_o_mem", "w_mlp_up")
WEIGHTS = ("g_mix", "w_in", "b_gate", "b_forget", "qn_swa", "kn_swa", "sink_swa", "rel_bias", "qn_fox", "kn_fox", "g_mem",
           "w_mem_kv", "qn_mem", "kn_mem", "w_o_swa", "w_o_fox", "w_o_mem", "w_out", "g_mlp", "w_mlp_up", "w_mlp_down")
SMALL_SLOTS = (("g_mix", 1024), ("b_gate", 3072), ("b_forget", 128), ("qn_swa", 128), ("kn_swa", 128), ("sink_swa", 128),
               ("rel_bias", REL_BUCKETS * 128), ("qn_fox", 128), ("kn_fox", 128), ("g_mem", 1024), ("qn_mem", 128),
               ("kn_mem", 128), ("g_mlp", 1024), ("loss", 128))
SMALL_OFF = {n: sum(w for _, w in SMALL_SLOTS[:i]) for i, (n, _) in enumerate(SMALL_SLOTS)}
SMALL_ROW = sum(w for _, w in SMALL_SLOTS)


def _gathered_to_full(name, g):
    if name in COL_SHARDED:
        return jnp.transpose(g, (1, 0, 2)).reshape(g.shape[1], N_DEV * g.shape[2])
    return g.reshape(N_DEV * g.shape[1], g.shape[2])


def _full_to_parts(name, full, b):
    if name in COL_SHARDED:
        return jnp.transpose(full.reshape(full.shape[0], N_DEV, b), (1, 0, 2)).astype(BF16)
    return full.reshape(N_DEV, full.shape[0] // N_DEV, full.shape[1]).astype(BF16)


def _pack_small(grads, loss):
    pieces = []
    for n, width in SMALL_SLOTS:
        a = loss.reshape(1, 1) if n == "loss" else grads[n].reshape(1, -1)
        pieces.append(jnp.pad(a, ((0, 0), (0, width - a.shape[1]))))
    return jnp.concatenate(pieces, axis=1)


def _adamw_small(gathered, w, m, v):
    names = list(SMALL)

    def body(*refs):
        p_ref = refs[0]
        ins = refs[1:1 + 3 * len(names)]
        outs = refs[1 + 3 * len(names):]
        g_all = p_ref[0]
        for k in range(1, N_DEV):
            g_all = g_all + p_ref[k]
        for i, n in enumerate(names):
            w_ref, m_ref, v_ref = ins[3 * i:3 * i + 3]
            out = outs[4 * i:4 * i + 4]
            rows, cols = w_ref.shape
            for r in range(rows):
                off = SMALL_OFF[n] + 128 * r
                g = g_all[:, off:off + cols]
                rs = slice(r, r + 1)
                res = (g,) + _adam_math(w_ref[rs, :], g, m_ref[rs, :], v_ref[rs, :])
                for o_ref, val in zip(out, res):
                    o_ref[rs, :] = val
        outs[-1][...] = g_all[:, SMALL_OFF["loss"]:SMALL_OFF["loss"] + 128]

    args = [gathered]
    out_shape = []
    for n in names:
        args += [w[n], m[n], v[n]]
        out_shape += [jax.ShapeDtypeStruct(w[n].shape, F32)] * 4
    out_shape.append(jax.ShapeDtypeStruct((1, 128), F32))
    outs = pl.pallas_call(body, name="adamw_small", out_shape=out_shape)(*args)
    return {n: outs[4 * i:4 * i + 4] for i, n in enumerate(names)}, outs[-1]


def kernel(x, mem, g_mix, w_in, b_gate, b_forget, qn_swa, kn_swa, sink_swa, rel_bias, qn_fox, kn_fox, g_mem, w_mem_kv, qn_mem, kn_mem, w_o_swa, w_o_fox, w_o_mem, w_out, g_mlp, w_mlp_up, w_mlp_down, loss_target, m_g_mix, m_w_in, m_b_gate, m_b_forget, m_qn_swa, m_kn_swa, m_sink_swa, m_rel_bias, m_qn_fox, m_kn_fox, m_g_mem, m_w_mem_kv, m_qn_mem, m_kn_mem, m_w_o_swa, m_w_o_fox, m_w_o_mem, m_w_out, m_g_mlp, m_w_mlp_up, m_w_mlp_down, v_g_mix, v_w_in, v_b_gate, v_b_forget, v_qn_swa, v_kn_swa, v_sink_swa, v_rel_bias, v_qn_fox, v_kn_fox, v_g_mem, v_w_mem_kv, v_qn_mem, v_kn_mem, v_w_o_swa, v_w_o_fox, v_w_o_mem, v_w_out, v_g_mlp, v_w_mlp_up, v_w_mlp_down):
    wts = dict(g_mix=g_mix, w_in=w_in, b_gate=b_gate, b_forget=b_forget, qn_swa=qn_swa, kn_swa=kn_swa, sink_swa=sink_swa,
               rel_bias=rel_bias, qn_fox=qn_fox, kn_fox=kn_fox, g_mem=g_mem, w_mem_kv=w_mem_kv, qn_mem=qn_mem, kn_mem=kn_mem,
               w_o_swa=w_o_swa, w_o_fox=w_o_fox, w_o_mem=w_o_mem, w_out=w_out, g_mlp=g_mlp, w_mlp_up=w_mlp_up,
               w_mlp_down=w_mlp_down)
    mom = dict(g_mix=m_g_mix, w_in=m_w_in, b_gate=m_b_gate, b_forget=m_b_forget, qn_swa=m_qn_swa, kn_swa=m_kn_swa,
               sink_swa=m_sink_swa, rel_bias=m_rel_bias, qn_fox=m_qn_fox, kn_fox=m_kn_fox, g_mem=m_g_mem, w_mem_kv=m_w_mem_kv,
               qn_mem=m_qn_mem, kn_mem=m_kn_mem, w_o_swa=m_w_o_swa, w_o_fox=m_w_o_fox, w_o_mem=m_w_o_mem, w_out=m_w_out,
               g_mlp=m_g_mlp, w_mlp_up=m_w_mlp_up, w_mlp_down=m_w_mlp_down)
    var = dict(g_mix=v_g_mix, w_in=v_w_in, b_gate=v_b_gate, b_forget=v_b_forget, qn_swa=v_qn_swa, kn_swa=v_kn_swa,
               sink_swa=v_sink_swa, rel_bias=v_rel_bias, qn_fox=v_qn_fox, kn_fox=v_kn_fox, g_mem=v_g_mem, w_mem_kv=v_w_mem_kv,
               qn_mem=v_qn_mem, kn_mem=v_kn_mem, w_o_swa=v_w_o_swa, w_o_fox=v_w_o_fox, w_o_mem=v_w_o_mem, w_out=v_w_out,
               g_mlp=v_g_mlp, w_mlp_up=v_w_mlp_up, w_mlp_down=v_w_mlp_down)

    shards = _cast_shards({n: wts[n][0] for n in BIG})
    first = _split_start([shards["w_in"]], True, "ag_start_w_in", peers=(1,) + SAME_CORE)
    rest = _split_start([shards[n] for n in BIG[1:]], True, "ag_start_rest", after=first[4])
    full = {}

    def getw(n, after):
        if n == "w_in" and n not in full:
            forwarded = _forward_start(first, after, "ag_forward_w_in")
            full[n] = _w_in_from_shards(_forward_wait(first, forwarded, "ag_wait_w_in"))
        elif n not in full:
            land = _split_wait(rest, BIG[1:].index(n), after, True, "ag_wait_" + n)
            full[n] = land if n == "w_mlp_up" else _gathered_to_full(n, land)
        return full[n]

    exchanges = {}

    def emit(grads_by_name):
        parts = []
        for n, grad in grads_by_name.items():
            if n.startswith("w_in"):
                parts.append(_dw_in_to_parts(grad, "d" + n + "_to_parts"))
            else:
                parts.append(grad if n == "w_mlp_up" else _full_to_parts(n, grad, wts[n].shape[2]))
        started = _split_start(parts, False, "rs_start_" + next(iter(grads_by_name)))
        for w, n in enumerate(grads_by_name):
            exchanges[n] = (started, w)
        return started[4]

    small_p = {n: wts[n] for n in SMALL}
    loss, grad_x, small_g = _local_step(x[0], mem[0], loss_target[0], small_p, getw, emit, (first[4], rest[4]))

    packed = _pack_small(small_g, loss)
    small_gather = _split_start([packed], True, "ag_start_small")

    grads, delta, new_m, new_v = {}, {}, {}, {}

    def update(n, after):
        lands = [_split_wait(*exchanges[e], after, False, "rs_wait_" + e) for e in exchanges if e.startswith(n)]
        g, d, m2, v2 = _adamw(lands, wts[n][0], mom[n][0], var[n][0], "adamw_" + n)
        grads[n], delta[n], new_m[n], new_v[n] = g[None], d[None], m2[None], v2[None]
        return d

    after = small_gather[4]
    for n in exchanges:
        if not n.startswith("w_in"):
            after = update(n, after)

    gathered = _split_wait(small_gather, 0, after, True, "ag_wait_small")
    small_out, total = _adamw_small(gathered, small_p, mom, var)
    for name, (g, d, m2, v2) in small_out.items():
        grads[name], delta[name], new_m[name], new_v[name] = g, d, m2, v2
    update("w_in", total)

    return (total[0, 0], grad_x[None], *[grads[n] for n in WEIGHTS], *[delta[n] for n in WEIGHTS],
            *[new_m[n] for n in WEIGHTS], *[new_v[n] for n in WEIGHTS])
```

```python
import functools
import math

import jax
import jax.numpy as jnp
from jax import lax
from jax.experimental import pallas as pl
from jax.experimental.pallas import tpu as pltpu

F32 = jnp.float32
BF16 = jnp.bfloat16

D_MODEL = 1024
N_MEM = 256
D_FF = 4096
HEAD = 64
SWA_HEADS = 8
SWA_BLOCK = 128
MEM_HEADS = 4
MEM_HEAD = 128
EPS = 1e-6
NEG = -1e30
REL_BUCKETS = 32
REL_MAX_DIST = 128

ADAM_LR = 0.001
ADAM_B1 = 0.9
ADAM_B2 = 0.999
ADAM_EPS = 1e-08
ADAM_WD = 0.01
ADAM_STEP = 10

GL0, QF0, KF0, VF0, QM0, QA0, KA0, VA0, FL0 = 0, 3072, 3584, 4096, 4608, 5120, 5632, 5760, 5888
PROJ_W = 6144
HALF_W = 3072
H_QF, H_KF, H_VF, H_QM, H_QA, H_KA, H_VA, H_FL = 0, 512, 1024, 1536, 2048, 2560, 2688, 2816

VMEM_LIMIT = 56 * 1024 * 1024
N_DEV = 8
MESH = pl.DeviceIdType.MESH

NN = (((1,), (0,)), ((), ()))
NT = (((1,), (1,)), ((), ()))
TN = (((0,), (0,)), ((), ()))


def _dot(a, b, dims=NN):
    return lax.dot_general(a, b, dims, preferred_element_type=F32)


def _params(sem):
    return pltpu.CompilerParams(dimension_semantics=sem, vmem_limit_bytes=VMEM_LIMIT)


def _full(shape):
    nd = len(shape)
    return pl.BlockSpec(shape, lambda *_: (0,) * nd)


def _sigmoid(z):
    return 1.0 / (1.0 + jnp.exp(-z))


def _group_mean(v, hd):
    if hd == 128:
        return jnp.mean(v, axis=-1, keepdims=True)
    r = lax.broadcasted_iota(jnp.int32, (128, 128), 0) // HEAD
    c = lax.broadcasted_iota(jnp.int32, (128, 128), 1) // HEAD
    same_head = jnp.where(r == c, 1.0 / HEAD, 0.0).astype(BF16)
    total = None
    rest = v
    for _ in range(2):
        part = rest.astype(BF16)
        rest = rest - part.astype(F32)
        term = _dot(part, same_head)
        total = term if total is None else total + term
    return total


def _mm(a, b, mode, out_dtype, tm, tn, tk, name, column_chunks=False, m_part=(0, 1), after=None):
    if mode == "nn":
        m, k = a.shape
        n = b.shape[1]
    elif mode == "nt":
        m, k = a.shape
        n = b.shape[0]
    else:
        k, m = a.shape
        n = b.shape[1]
    assert mode == "tn" or m_part == (0, 1)
    m //= m_part[1]
    tm, tn, tk = min(tm, m), min(tn, n), min(tk, k)
    m0 = m_part[0] * (m // tm)
    extra = [] if after is None else [after]
    nk = k // tk
    chunk = n // N_DEV
    per_tile = tn // chunk if column_chunks else 1
    dims = {"nn": NN, "nt": NT, "tn": TN}[mode]
    a_spec = pl.BlockSpec((tk, tm), lambda j, i, kk: (kk, m0 + i)) if mode == "tn" else pl.BlockSpec((tm, tk), lambda j, i, kk: (i, kk))
    b_spec = pl.BlockSpec((tn, tk), lambda j, i, kk: (j, kk)) if mode == "nt" else pl.BlockSpec((tk, tn), lambda j, i, kk: (kk, j))

    def body(a_ref, b_ref, *rest):
        o_ref, *acc = rest[len(extra):]
        prod = _dot(a_ref[...].astype(BF16), b_ref[...].astype(BF16), dims)

        def write(res):
            if column_chunks:
                for c in range(per_tile):
                    o_ref[c] = res[:, c * chunk:(c + 1) * chunk].astype(o_ref.dtype)
            else:
                o_ref[...] = res.astype(o_ref.dtype)

        if nk == 1:
            write(prod)
        else:
            acc_ref, = acc
            kk = pl.program_id(2)

            @pl.when(kk == 0)
            def _():
                acc_ref[...] = prod

            @pl.when(kk > 0)
            def _():
                acc_ref[...] += prod

            @pl.when(kk == nk - 1)
            def _():
                write(acc_ref[...])

    return pl.pallas_call(
        body, name=name, grid=(n // tn, m // tm, nk),
        in_specs=[a_spec, b_spec] + [pl.BlockSpec(memory_space=pl.ANY)] * len(extra),
        out_specs=(pl.BlockSpec((per_tile, tm, chunk), lambda j, i, kk: (j, i, 0)) if column_chunks
                   else pl.BlockSpec((tm, tn), lambda j, i, kk: (i, j))),
        out_shape=jax.ShapeDtypeStruct((N_DEV, m, chunk) if column_chunks else (m, n), out_dtype),
        scratch_shapes=[pltpu.VMEM((tm, tn), F32)] if nk > 1 else [],
        compiler_params=_params(("parallel", "parallel", "arbitrary")),
    )(a, b, *extra)


def _mm_tn3(a_list, b_list, name):
    s, m = a_list[0].shape
    n = b_list[0].shape[1]
    tk = min(2048, s)
    nk = s // tk

    def body(*refs):
        a_refs, b_refs, o_refs, acc_refs = refs[0:3], refs[3:6], refs[6:9], refs[9:12]
        kk = pl.program_id(0)
        for a_ref, b_ref, o_ref, acc_ref in zip(a_refs, b_refs, o_refs, acc_refs):
            prod = _dot(a_ref[...], b_ref[...], TN)
            if nk == 1:
                o_ref[...] = prod.astype(o_ref.dtype)
                continue

            @pl.when(kk == 0)
            def _(acc_ref=acc_ref, prod=prod):
                acc_ref[...] = prod

            @pl.when(kk > 0)
            def _(acc_ref=acc_ref, prod=prod):
                acc_ref[...] += prod

            @pl.when(kk == nk - 1)
            def _(acc_ref=acc_ref, o_ref=o_ref):
                o_ref[...] = acc_ref[...].astype(o_ref.dtype)

    return pl.pallas_call(
        body, name=name, grid=(nk,),
        in_specs=[pl.BlockSpec((tk, m), lambda kk: (kk, 0))] * 3 + [pl.BlockSpec((tk, n), lambda kk: (kk, 0))] * 3,
        out_specs=[_full((m, n))] * 3,
        out_shape=[jax.ShapeDtypeStruct((m, n), BF16)] * 3,
        scratch_shapes=[pltpu.VMEM((m, n), F32)] * 3,
        compiler_params=_params(("arbitrary",)),
    )(*a_list, *b_list)


def _rms_fwd(x, g, name, deps=()):
    s, d = x.shape
    tm = min(512, s)

    def body(x_ref, g_ref, *rest):
        h_ref = rest[len(deps)]
        xv = x_ref[...]
        r = lax.rsqrt(jnp.mean(xv * xv, axis=-1, keepdims=True) + EPS)
        h_ref[...] = (xv * r * g_ref[...]).astype(BF16)

    return pl.pallas_call(
        body, name=name, grid=(s // tm,),
        in_specs=[pl.BlockSpec((tm, d), lambda i: (i, 0)), _full((1, d))] + [pl.BlockSpec(memory_space=pl.ANY)] * len(deps),
        out_specs=pl.BlockSpec((tm, d), lambda i: (i, 0)),
        out_shape=jax.ShapeDtypeStruct((s, d), BF16),
        compiler_params=_params(("parallel",)),
    )(x, g, *deps)


def _proj_post(proj, gq_fox, gk_fox, gq_mem, gq_swa, gk_swa):
    s = proj.shape[0]
    tm = min(512, s)

    def body(p_hbm, gqf, gkf, gqm, gqa, gka, qf_ref, kf_ref, vf_ref, qm_ref, qa_ref, ka_ref, va_ref, qft_ref, vft_ref,
             ring, sems):
        i = pl.program_id(0)
        steps = pl.num_programs(0)

        def fetch(step):
            slot = step % 3
            return pltpu.make_async_copy(p_hbm.at[pl.ds(step * tm, tm), pl.ds(HALF_W, HALF_W)], ring.at[slot], sems.at[slot])

        @pl.when(i == 0)
        def _():
            fetch(0).start()
            if s // tm > 1:
                fetch(1).start()

        @pl.when(i + 2 < steps)
        def _():
            fetch(i + 2).start()

        fetch(i).wait()
        p_ref = ring.at[i % 3]

        def norm(off, width, hd, g_ref, o_ref, scaled_t_ref=None):
            for b in range(width // 128):
                v = p_ref[:, off + b * 128: off + (b + 1) * 128].astype(F32)
                r = lax.rsqrt(_group_mean(v * v, hd) + EPS)
                vn = (v * r * g_ref[...]).astype(BF16)
                o_ref[:, b * 128:(b + 1) * 128] = vn
                if scaled_t_ref is not None:
                    scaled_t_ref[b * 128:(b + 1) * 128, :] = (vn.astype(F32) * 0.125).T.astype(BF16)

        norm(H_QF, 512, HEAD, gqf, qf_ref, qft_ref)
        norm(H_KF, 512, HEAD, gkf, kf_ref)
        vf_ref[...] = p_ref[:, H_VF:H_VF + 512].astype(BF16)
        for b in range(4):
            vft_ref[b * 128:(b + 1) * 128, :] = p_ref[:, H_VF + b * 128:H_VF + (b + 1) * 128].astype(F32).T.astype(BF16)
        norm(H_QM, 512, MEM_HEAD, gqm, qm_ref)
        norm(H_QA, 512, HEAD, gqa, qa_ref)
        norm(H_KA, 128, HEAD, gka, ka_ref)
        va_ref[...] = p_ref[:, H_VA:H_VA + 128].astype(BF16)

    g_spec = _full((1, 128))
    o512 = pl.BlockSpec((tm, 512), lambda i: (i, 0))
    o128 = pl.BlockSpec((tm, 128), lambda i: (i, 0))
    s512 = jax.ShapeDtypeStruct((s, 512), BF16)
    s128 = jax.ShapeDtypeStruct((s, 128), BF16)
    return pl.pallas_call(
        body, name="proj_post", grid=(s // tm,),
        in_specs=[pl.BlockSpec(memory_space=pl.ANY), g_spec, g_spec, g_spec, g_spec, g_spec],
        out_specs=[o512, o512, o512, o512, o512, o128, o128] + [pl.BlockSpec((512, tm), lambda i: (0, i))] * 2,
        out_shape=[s512, s512, s512, s512, s512, s128, s128] + [jax.ShapeDtypeStruct((512, s), BF16)] * 2,
        scratch_shapes=[pltpu.VMEM((3, tm, HALF_W), BF16), pltpu.SemaphoreType.DMA((3,))],
        compiler_params=_params(("arbitrary",)),
    )(proj, gq_fox, gk_fox, gq_mem, gq_swa, gk_swa)


def _tri(n, lower):
    r = lax.broadcasted_iota(jnp.int32, (n, n), 0)
    c = lax.broadcasted_iota(jnp.int32, (n, n), 1)
    return jnp.where((c <= r) if lower else (c >= r), 1.0, 0.0).astype(F32)


def _fox_gate_fwd(proj, b_forget128):
    s = proj.shape[0]
    tm = min(512, s)

    def body(p_ref, b_ref, cc_ref, ca_ref, carry_ref):
        i = pl.program_id(0)

        @pl.when(i == 0)
        def _():
            carry_ref[...] = jnp.zeros_like(carry_ref)

        z = p_ref[...] + b_ref[...]
        logf = jnp.minimum(z, 0.0) - jnp.log(1.0 + jnp.exp(-jnp.abs(z)))
        c = jnp.dot(_tri(tm, True), logf, precision=lax.Precision.HIGHEST, preferred_element_type=F32) + carry_ref[...]
        carry_ref[...] = c[tm - 1:tm, :]
        lane = lax.broadcasted_iota(jnp.int32, (tm, 128), 1)
        for hp in range(4):
            cc_ref[hp] = c if hp == 0 else pltpu.roll(c, 128 - 2 * hp, 1)
            aug = jnp.zeros((tm, 128), F32)
            for e in range(2):
                rest = jnp.broadcast_to(c[:, 2 * hp + e:2 * hp + e + 1], (tm, 128))
                for part in range(3):
                    piece = rest.astype(BF16).astype(F32)
                    aug = jnp.where(lane == HEAD * (1 - e) + part, piece, aug)
                    rest = rest - piece
            ca_ref[hp] = aug.astype(BF16)

    return pl.pallas_call(
        body, name="fox_gate_fwd", grid=(s // tm,),
        in_specs=[pl.BlockSpec((tm, 128), lambda i: (i, 0)), _full((1, 128))],
        out_specs=[pl.BlockSpec((4, tm, 128), lambda i: (0, i, 0))] * 2,
        out_shape=[jax.ShapeDtypeStruct((4, s, 128), F32), jax.ShapeDtypeStruct((4, s, 128), BF16)],
        scratch_shapes=[pltpu.VMEM((1, 128), F32)],
        compiler_params=_params(("arbitrary",)),
    )(proj, b_forget128)


def _memkv_fwd(mem, g_mem, w_kv, kn_mem):
    m = mem.shape[0]

    def body(mem_ref, g_ref, w_ref, kn_ref, memn_ref, kv_ref, mk_ref, mv_ref):
        xv = mem_ref[...]
        r = lax.rsqrt(jnp.mean(xv * xv, axis=-1, keepdims=True) + EPS)
        mn = (xv * r * g_ref[...]).astype(BF16)
        memn_ref[...] = mn
        kv = _dot(mn, w_ref[...])
        kv_ref[...] = kv
        for h in range(MEM_HEADS):
            v = kv[:, h * 128:(h + 1) * 128]
            rr = lax.rsqrt(jnp.mean(v * v, axis=-1, keepdims=True) + EPS)
            mk_ref[:, h * 128:(h + 1) * 128] = (v * rr * kn_ref[...]).astype(BF16)
        mv_ref[...] = kv[:, 512:1024].astype(BF16)

    return pl.pallas_call(
        body, name="memkv_fwd",
        out_shape=[jax.ShapeDtypeStruct((m, D_MODEL), BF16), jax.ShapeDtypeStruct((m, 1024), F32),
                   jax.ShapeDtypeStruct((m, 512), BF16), jax.ShapeDtypeStruct((m, 512), BF16)],
        compiler_params=pltpu.CompilerParams(vmem_limit_bytes=VMEM_LIMIT),
    )(mem, g_mem, w_kv, kn_mem)


def _bias_table(rel_bias, bucket):
    def body(rb_ref, bk_ref, o_ref):
        bk = bk_ref[...]
        for h in range(SWA_HEADS):
            acc = jnp.zeros(bk.shape, F32)
            for b in range(REL_BUCKETS):
                acc = jnp.where(bk == b, rb_ref[b, h], acc)
            o_ref[h] = acc

    return pl.pallas_call(
        body, name="bias_table",
        in_specs=[pl.BlockSpec(memory_space=pltpu.SMEM), pl.BlockSpec(memory_space=pltpu.VMEM)],
        out_shape=jax.ShapeDtypeStruct((SWA_HEADS,) + bucket.shape, F32),
    )(rel_bias, bucket)


def _swa_valid(n):
    row = lax.broadcasted_iota(jnp.int32, (SWA_BLOCK, 2 * SWA_BLOCK), 0)
    col = lax.broadcasted_iota(jnp.int32, (SWA_BLOCK, 2 * SWA_BLOCK), 1)
    dist = row + SWA_BLOCK - col
    return (dist >= 0) & (dist < SWA_BLOCK) & ((col >= SWA_BLOCK) | (n > 0))


def _swa_fwd(qa, kp, vp, bias, sink):
    s = qa.shape[0]
    nb = s // SWA_BLOCK

    def body(sink_ref, q_ref, kp_ref, vp_ref, bias_ref, o_ref):
        n = pl.program_id(0)
        start = pl.multiple_of(n * SWA_BLOCK, SWA_BLOCK)
        k2 = kp_ref[pl.ds(start, 2 * SWA_BLOCK), :]
        v2 = vp_ref[pl.ds(start, 2 * SWA_BLOCK), :]
        valid = _swa_valid(n)
        heads = range(SWA_HEADS)
        hs = lambda h: slice(h * HEAD, (h + 1) * HEAD)
        sc = [jnp.where(valid, _dot(q_ref[:, hs(h)], k2[:, hs(h // 4)], NT) * 0.125 + bias_ref[h], NEG) for h in heads]
        pn = []
        for h in heads:
            sk = sink_ref[h]
            mx = jnp.maximum(jnp.max(sc[h], axis=-1, keepdims=True), sk)
            p = jnp.exp(sc[h] - mx)
            den = jnp.sum(p, axis=-1, keepdims=True) + jnp.exp(sk - mx)
            pn.append((p / den).astype(BF16))
        outs = [_dot(pn[h], v2[:, hs(h // 4)]).astype(BF16) for h in heads]
        for h in heads:
            o_ref[:, hs(h)] = outs[h]

    return pl.pallas_call(
        body, name="swa_fwd", grid=(nb,),
        in_specs=[pl.BlockSpec(memory_space=pltpu.SMEM),
                  pl.BlockSpec((SWA_BLOCK, 512), lambda n: (n, 0)),
                  _full(kp.shape), _full(vp.shape), _full(bias.shape)],
        out_specs=pl.BlockSpec((SWA_BLOCK, 512), lambda n: (n, 0)),
        out_shape=jax.ShapeDtypeStruct((s, 512), BF16),
        compiler_params=_params(("parallel",)),
    )(sink, qa, kp, vp, bias)


def _head_mask(e):
    lane = lax.broadcasted_iota(jnp.int32, (1, 128), 1)
    return (lane >= e * HEAD) & (lane < (e + 1) * HEAD)


FOX_FWD_T = 1024
FOX_BWD_T = 512


def _head_rows(e):
    row = lax.broadcasted_iota(jnp.int32, (128, 1), 0)
    return (row >= e * HEAD) & (row < (e + 1) * HEAD)


def _fox_fwd(q, k, v_t, ca4):
    s = q.shape[0]
    t = min(FOX_FWD_T, s)
    nq = s // t

    def body(q_ref, k_ref, vt_ref, ca_ref, o_ref, lse_ref, ot_ref):
        i = pl.program_id(1)
        qs = q_ref[...] * jnp.asarray(0.125, BF16)
        lane = lax.broadcasted_iota(jnp.int32, (1, 128), 1)
        minus = [jnp.where((lane >= HEAD * (1 - e)) & (lane < HEAD * (1 - e) + 3), -1.0, 0.0).astype(BF16) for e in range(2)]
        qe = [jnp.where(_head_mask(e), qs, jnp.broadcast_to(minus[e], qs.shape)) for e in range(2)]

        def block(carry, key0, nkeys, q0, nqs, masked):
            ks = pl.ds(pl.multiple_of(key0, 128), nkeys)
            kj = k_ref[ks, :]
            caj = ca_ref[0, ks, :]
            vtj = vt_ref[:, ks]
            out = []
            for e in range(2):
                m_all, acc_all = carry[2 * e], carry[2 * e + 1]
                m, acc = m_all[:, q0:q0 + nqs], acc_all[:, q0:q0 + nqs]
                st = _dot(jnp.where(_head_mask(e), kj, caj), qe[e][q0:q0 + nqs, :], NT)
                if masked:
                    krow = lax.broadcasted_iota(jnp.int32, (nkeys, nqs), 0) + key0
                    qcol = lax.broadcasted_iota(jnp.int32, (nkeys, nqs), 1) + (i * t + q0)
                    st = jnp.where(krow <= qcol, st, NEG)
                m_new = jnp.maximum(m, jnp.max(st, axis=0, keepdims=True))
                alpha = jnp.exp(m - m_new)
                pt = jnp.exp(st - m_new).astype(BF16)
                vte = jnp.where(_head_rows(e), vtj, jnp.ones_like(vtj))
                acc_new = alpha * acc + _dot(vte, pt)
                if nqs < t:
                    m_new = jnp.concatenate([m_all[:, :q0], m_new], axis=1)
                    acc_new = jnp.concatenate([acc_all[:, :q0], acc_new], axis=1)
                out += [m_new, acc_new]
            return tuple(out)

        half = t // 2
        init = (jnp.full((1, t), NEG, F32), jnp.zeros((128, t), F32)) * 2
        carry = lax.fori_loop(0, i, lambda j, c: block(c, j * t, t, 0, t, False), init)
        carry = block(carry, i * t, half, 0, t, True)
        m0, a0, m1, a1 = block(carry, i * t + half, half, half, half, True)
        l0 = a0[HEAD:HEAD + 1, :]
        l1 = a1[0:1, :]
        o_t = jnp.where(_head_rows(0), a0 / l0, a1 / l1)
        o_ref[...] = o_t.T.astype(BF16)
        ot_ref[...] = o_t.astype(BF16)
        r8 = lax.broadcasted_iota(jnp.int32, (8, t), 0)
        lse_ref[0] = jnp.where(r8 == 0, m0 + jnp.log(l0), jnp.where(r8 == 1, m1 + jnp.log(l1), 0.0))

    return pl.pallas_call(
        body, name="fox_fwd", grid=(4, nq),
        in_specs=[pl.BlockSpec((t, 128), lambda hp, i: (i, hp)),
                  pl.BlockSpec((s, 128), lambda hp, i: (0, hp)),
                  pl.BlockSpec((128, s), lambda hp, i: (hp, 0)),
                  pl.BlockSpec((1, s, 128), lambda hp, i: (hp, 0, 0))],
        out_specs=[pl.BlockSpec((t, 128), lambda hp, i: (i, hp)),
                   pl.BlockSpec((1, 8, t), lambda hp, i: (hp, 0, i)),
                   pl.BlockSpec((128, t), lambda hp, i: (hp, i))],
        out_shape=[jax.ShapeDtypeStruct((s, 512), BF16), jax.ShapeDtypeStruct((4, 8, s), F32),
                   jax.ShapeDtypeStruct((512, s), BF16)],
        compiler_params=_params(("parallel", "parallel")),
    )(q, k, v_t, ca4)


MEM_SCALE = MEM_HEAD ** -0.5


def _mem_fwd(qm, mk, mv):
    s = qm.shape[0]
    tq = min(512, s)

    def body(q_ref, mk_ref, mv_ref, o_ref):
        for h in range(MEM_HEADS):
            hs = slice(h * 128, (h + 1) * 128)
            sc = _dot(q_ref[:, hs], mk_ref[:, hs], NT) * MEM_SCALE
            mx = jnp.max(sc, axis=-1, keepdims=True)
            p = jnp.exp(sc - mx)
            p = p / jnp.sum(p, axis=-1, keepdims=True)
            o_ref[:, hs] = _dot(p.astype(BF16), mv_ref[:, hs]).astype(BF16)

    return pl.pallas_call(
        body, name="mem_fwd", grid=(s // tq,),
        in_specs=[pl.BlockSpec((tq, 512), lambda i: (i, 0)), _full(mk.shape), _full(mv.shape)],
        out_specs=pl.BlockSpec((tq, 512), lambda i: (i, 0)),
        out_shape=jax.ShapeDtypeStruct((s, 512), BF16),
        compiler_params=_params(("parallel",)),
    )(qm, mk, mv)


def _merge_fwd(x, oa, of, om, proj, b_gate, wa, wf, wm, w_out, g_mlp):
    s = x.shape[0]
    tm = min(512, s)

    def body(x_ref, oa_ref, of_ref, om_ref, gl_ref, bg_ref, wa_ref, wf_ref, wm_ref, wo_ref, g_ref, x1_ref, hm_ref, mg_ref):
        merged = None
        for b, (o_ref, w_ref) in enumerate(((oa_ref, wa_ref), (of_ref, wf_ref), (om_ref, wm_ref))):
            cs = slice(b * D_MODEL, (b + 1) * D_MODEL)
            y = _dot(o_ref[...], w_ref[...])
            t = _sigmoid(gl_ref[:, cs].astype(F32) + bg_ref[:, cs]) * y
            merged = t if merged is None else merged + t
        mb = merged.astype(BF16)
        mg_ref[...] = mb
        x1 = x_ref[...] + _dot(mb, wo_ref[...])
        x1_ref[...] = x1
        r = lax.rsqrt(jnp.mean(x1 * x1, axis=-1, keepdims=True) + EPS)
        hm_ref[...] = (x1 * r * g_ref[...]).astype(BF16)

    row = lambda w: pl.BlockSpec((tm, w), lambda i: (i, 0))
    return pl.pallas_call(
        body, name="merge_fwd", grid=(s // tm,),
        in_specs=[row(D_MODEL), row(512), row(512), row(512), row(HALF_W), _full((1, HALF_W)),
                  _full(wa.shape), _full(wf.shape), _full(wm.shape), _full(w_out.shape), _full((1, D_MODEL))],
        out_specs=[row(D_MODEL), row(D_MODEL), row(D_MODEL)],
        out_shape=[jax.ShapeDtypeStruct((s, D_MODEL), F32), jax.ShapeDtypeStruct((s, D_MODEL), BF16),
                   jax.ShapeDtypeStruct((s, D_MODEL), BF16)],
        compiler_params=_params(("parallel",)),
    )(x, oa, of, om, proj, b_gate, wa, wf, wm, w_out, g_mlp)


def _mlp_up(hm, w_up):
    s = hm.shape[0]
    tm, tn = min(1024, s), w_up.shape[2]

    def body(h_ref, w_ref, u_ref):
        r = jnp.maximum(_dot(h_ref[...], w_ref[0]), 0.0)
        u_ref[...] = (r * r).astype(BF16)

    return pl.pallas_call(
        body, name="mlp_up", grid=(s // tm, D_FF // tn),
        in_specs=[pl.BlockSpec((tm, D_MODEL), lambda i, j: (i, 0)), pl.BlockSpec((1, D_MODEL, tn), lambda i, j: (j, 0, 0))],
        out_specs=pl.BlockSpec((tm, tn), lambda i, j: (i, j)),
        out_shape=jax.ShapeDtypeStruct((s, D_FF), BF16),
        compiler_params=_params(("parallel", "parallel")),
    )(hm, w_up)


def _mlp_down_loss(u, w_down, x1, target):
    s = u.shape[0]
    tm = min(256, s)

    def body(u_ref, w_ref, x1_ref, t_ref, dy_ref, dyb_ref, loss_ref):
        i = pl.program_id(0)

        @pl.when(i == 0)
        def _():
            loss_ref[...] = jnp.zeros_like(loss_ref)

        y = x1_ref[...] + _dot(u_ref[...], w_ref[...])
        err = y - t_ref[...]
        dy = err * (1.0 / D_MODEL)
        dy_ref[...] = dy
        dyb_ref[...] = dy.astype(BF16)
        part = jnp.sum(jnp.sum(err * err, axis=-1, keepdims=True) * (1.0 / D_MODEL), axis=0, keepdims=True)
        loss_ref[...] += 0.5 * part

    row = pl.BlockSpec((tm, D_MODEL), lambda i: (i, 0))
    return pl.pallas_call(
        body, name="mlp_down_loss", grid=(s // tm,),
        in_specs=[pl.BlockSpec((tm, D_FF), lambda i: (i, 0)), _full(w_down.shape), row, row],
        out_specs=[row, row, _full((1, 1))],
        out_shape=[jax.ShapeDtypeStruct((s, D_MODEL), F32), jax.ShapeDtypeStruct((s, D_MODEL), BF16),
                   jax.ShapeDtypeStruct((1, 1), F32)],
        compiler_params=_params(("arbitrary",)),
    )(u, w_down, x1, target)


def _mlp_bwd_act(dy, w_down, u):
    s = dy.shape[0]
    tm, tn = min(1024, s), 1024

    def body(dy_ref, w_ref, u_ref, da_ref):
        du = _dot(dy_ref[...], w_ref[...], NT)
        da_ref[...] = (du * (2.0 * jnp.sqrt(u_ref[...].astype(F32)))).astype(BF16)

    return pl.pallas_call(
        body, name="mlp_bwd_act", grid=(D_FF // tn, s // tm),
        in_specs=[pl.BlockSpec((tm, D_MODEL), lambda j, i: (i, 0)), pl.BlockSpec((tn, D_MODEL), lambda j, i: (j, 0)),
                  pl.BlockSpec((tm, tn), lambda j, i: (i, j))],
        out_specs=pl.BlockSpec((tm, tn), lambda j, i: (i, j)),
        out_shape=jax.ShapeDtypeStruct((s, D_FF), BF16),
        compiler_params=_params(("parallel", "parallel")),
    )(dy, w_down, u)


def _rms_bwd(xv, g, dh, skip):
    r = lax.rsqrt(jnp.mean(xv * xv, axis=-1, keepdims=True) + EPS)
    n = xv * r
    dn = dh * g
    dx = skip + r * (dn - n * jnp.mean(dn * n, axis=-1, keepdims=True))
    return dx, jnp.sum(dh * n, axis=0, keepdims=True)


def _mlp_bwd_x(da, w_up, x1, dy, g_mlp):
    s = da.shape[0]
    tm = min(256, s)

    def body(da_ref, w_ref, x1_ref, dy_ref, g_ref, dx1_ref, dg_ref):
        i = pl.program_id(0)

        @pl.when(i == 0)
        def _():
            dg_ref[...] = jnp.zeros_like(dg_ref)

        tn = w_ref.shape[2]
        dhm = _dot(da_ref[:, 0:tn], w_ref[0], NT)
        for j in range(1, N_DEV):
            dhm = dhm + _dot(da_ref[:, j * tn:(j + 1) * tn], w_ref[j], NT)
        dx, dg = _rms_bwd(x1_ref[...], g_ref[...], dhm, dy_ref[...])
        dx1_ref[...] = dx
        dg_ref[...] += dg

    row = pl.BlockSpec((tm, D_MODEL), lambda i: (i, 0))
    return pl.pallas_call(
        body, name="mlp_bwd_x", grid=(s // tm,),
        in_specs=[pl.BlockSpec((tm, D_FF), lambda i: (i, 0)), _full(w_up.shape), row, row, _full((1, D_MODEL))],
        out_specs=[row, _full((1, D_MODEL))],
        out_shape=[jax.ShapeDtypeStruct((s, D_MODEL), F32), jax.ShapeDtypeStruct((1, D_MODEL), F32)],
        compiler_params=_params(("arbitrary",)),
    )(da, w_up, x1, dy, g_mlp)


def _merge_bwd(dx1, oa, of, om, proj, b_gate, wa, wf, wm, w_out):
    s = dx1.shape[0]
    tm = min(512, s)

    def body(dx1_ref, oa_ref, of_ref, om_ref, gl_ref, bg_ref, wa_ref, wf_ref, wm_ref, wo_ref,
             dp_ref, doa_ref, dof_ref, dom_ref, dya_ref, dyf_ref, dym_ref, dbg_ref):
        i = pl.program_id(0)

        @pl.when(i == 0)
        def _():
            dbg_ref[...] = jnp.zeros_like(dbg_ref)

        dmerged = _dot(dx1_ref[...].astype(BF16), wo_ref[...], NT)
        branches = ((oa_ref, wa_ref, doa_ref, dya_ref), (of_ref, wf_ref, dof_ref, dyf_ref), (om_ref, wm_ref, dom_ref, dym_ref))
        for b, (o_ref, w_ref, do_ref, dyb_ref) in enumerate(branches):
            cs = slice(b * D_MODEL, (b + 1) * D_MODEL)
            y = _dot(o_ref[...], w_ref[...])
            g = _sigmoid(gl_ref[:, cs].astype(F32) + bg_ref[:, cs])
            dz = (dmerged * y) * g * (1.0 - g)
            dp_ref[:, cs] = dz.astype(BF16)
            dbg_ref[:, cs] += jnp.sum(dz, axis=0, keepdims=True)
            dyb = (dmerged * g).astype(BF16)
            dyb_ref[...] = dyb
            do = _dot(dyb, w_ref[...], NT)
            do_ref[...] = (do.T if b == 1 else do).astype(BF16)

    row = lambda w: pl.BlockSpec((tm, w), lambda i: (i, 0))
    sd = lambda w: jax.ShapeDtypeStruct((s, w), BF16)
    return pl.pallas_call(
        body, name="merge_bwd", grid=(s // tm,),
        in_specs=[row(D_MODEL), row(512), row(512), row(512), row(HALF_W), _full((1, HALF_W)),
                  _full(wa.shape), _full(wf.shape), _full(wm.shape), _full(w_out.shape)],
        out_specs=[row(HALF_W), row(512), pl.BlockSpec((512, tm), lambda i: (0, i)), row(512),
                   row(D_MODEL), row(D_MODEL), row(D_MODEL), _full((1, HALF_W))],
        out_shape=[sd(PROJ_W), sd(512), jax.ShapeDtypeStruct((512, s), BF16), sd(512), sd(D_MODEL), sd(D_MODEL), sd(D_MODEL),
                   jax.ShapeDtypeStruct((1, HALF_W), F32)],
        compiler_params=_params(("arbitrary",)),
    )(dx1, oa, of, om, proj, b_gate, wa, wf, wm, w_out)


def _swa_valid_t(n):
    key = lax.broadcasted_iota(jnp.int32, (2 * SWA_BLOCK, SWA_BLOCK), 0)
    qry = lax.broadcasted_iota(jnp.int32, (2 * SWA_BLOCK, SWA_BLOCK), 1)
    dist = qry + SWA_BLOCK - key
    return (dist >= 0) & (dist < SWA_BLOCK) & ((key >= SWA_BLOCK) | (n > 0))


def _swa_bwd(qa, kp, vp, bias_t, sink, doa):
    s = qa.shape[0]
    nb = s // SWA_BLOCK

    def body(sink_ref, q_ref, kp_ref, vp_ref, bias_ref, do_ref, dq_ref, dkp_ref, dvp_ref, dbias_ref, dsink_ref, sk_acc):
        n = pl.program_id(0)

        @pl.when(n == 0)
        def _():
            dkp_ref[...] = jnp.zeros_like(dkp_ref)
            dvp_ref[...] = jnp.zeros_like(dvp_ref)
            dbias_ref[...] = jnp.zeros_like(dbias_ref)
            sk_acc[...] = jnp.zeros_like(sk_acc)

        start = pl.multiple_of(n * SWA_BLOCK, SWA_BLOCK)
        win = pl.ds(start, 2 * SWA_BLOCK)
        k2 = kp_ref[win, :]
        v2 = vp_ref[win, :]
        valid = _swa_valid_t(n)
        heads = range(SWA_HEADS)
        hs = lambda h: slice(h * HEAD, (h + 1) * HEAD)
        scale = jnp.asarray(0.125, BF16)
        q = [q_ref[:, hs(h)] for h in heads]
        do = [do_ref[:, hs(h)] for h in heads]
        kk = [k2[:, hs(kv)] for kv in range(2)]
        vv = [v2[:, hs(kv)] for kv in range(2)]
        kt = [(kk[kv].astype(F32) * 0.125).T.astype(BF16) for kv in range(2)]
        st = [jnp.where(valid, _dot(kk[h // 4], q[h], NT) * 0.125 + bias_ref[h], NEG) for h in heads]
        dpt = [_dot(vv[h // 4], do[h], NT) for h in heads]
        pt, dst = [], []
        for h in heads:
            sk = sink_ref[h]
            mx = jnp.maximum(jnp.max(st[h], axis=0, keepdims=True), sk)
            p = jnp.exp(st[h] - mx)
            esk = jnp.exp(sk - mx)
            den = jnp.sum(p, axis=0, keepdims=True) + esk
            p = p / den
            delta = jnp.sum(p * dpt[h], axis=0, keepdims=True)
            d = p * (dpt[h] - delta)
            sk_acc[h:h + 1, :] += -(esk / den) * delta
            dbias_ref[h] += d
            pt.append(p.astype(BF16))
            dst.append(d.astype(BF16))
        dq_t = [_dot(kt[h // 4], dst[h]) for h in heads]
        dq_ref[...] = jnp.concatenate(dq_t, axis=0).T.astype(BF16)
        for kv in range(2):
            group = range(4 * kv, 4 * kv + 4)
            dk = [_dot(dst[h], q[h] * scale) for h in group]
            dv = [_dot(pt[h], do[h]) for h in group]
            dkp_ref[win, hs(kv)] += (dk[0] + dk[1]) + (dk[2] + dk[3])
            dvp_ref[win, hs(kv)] += (dv[0] + dv[1]) + (dv[2] + dv[3])

        @pl.when(n == nb - 1)
        def _():
            dsink_ref[...] = jnp.broadcast_to(jnp.sum(sk_acc[...], axis=1, keepdims=True), dsink_ref.shape)

    return pl.pallas_call(
        body, name="swa_bwd", grid=(nb,),
        in_specs=[pl.BlockSpec(memory_space=pltpu.SMEM),
                  pl.BlockSpec((SWA_BLOCK, 512), lambda n: (n, 0)),
                  _full(kp.shape), _full(vp.shape), _full(bias_t.shape),
                  pl.BlockSpec((SWA_BLOCK, 512), lambda n: (n, 0))],
        out_specs=[pl.BlockSpec((SWA_BLOCK, 512), lambda n: (n, 0)), _full(kp.shape), _full(vp.shape),
                   _full(bias_t.shape), _full((SWA_HEADS, 128))],
        out_shape=[jax.ShapeDtypeStruct((s, 512), BF16), jax.ShapeDtypeStruct(kp.shape, F32),
                   jax.ShapeDtypeStruct(vp.shape, F32), jax.ShapeDtypeStruct(bias_t.shape, F32),
                   jax.ShapeDtypeStruct((SWA_HEADS, 128), F32)],
        scratch_shapes=[pltpu.VMEM((SWA_HEADS, 128), F32)],
        compiler_params=_params(("arbitrary",)),
    )(sink, qa, kp, vp, bias_t, doa)


def _fox_bwd(qt, k, v, dot, ot, cc4, lse4):
    s = k.shape[0]
    t = min(FOX_BWD_T, s)
    nq = s // t

    def body(qt_ref, k_ref, v_ref, dot_ref, ot_ref, cc_ref, lse_ref,
             dqt_ref, dk_ref, dv_ref, dck_ref, dcq_ref, delta_ref, dkt_acc, dvt_acc, ds0, ds1):
        j = pl.program_id(1)

        @pl.when(j == 0)
        def _():
            dqt_ref[...] = jnp.zeros_like(dqt_ref)
            dcq_ref[...] = jnp.zeros_like(dcq_ref)
            r8 = lax.broadcasted_iota(jnp.int32, (8, t), 0)

            def dl(i, c):
                cols = pl.ds(pl.multiple_of(i * t, t), t)
                pr = dot_ref[:, cols].astype(F32) * ot_ref[:, cols].astype(F32)
                d0 = jnp.sum(jnp.where(_head_rows(0), pr, 0.0), axis=0, keepdims=True)
                d1 = jnp.sum(jnp.where(_head_rows(1), pr, 0.0), axis=0, keepdims=True)
                delta_ref[:, cols] = jnp.where(r8 == 0, d0, jnp.where(r8 == 1, d1, 0.0))
                return c

            lax.fori_loop(0, nq, dl, 0)

        kj = k_ref[...]
        vj = v_ref[...]
        ks = pl.ds(pl.multiple_of(j * t, t), t)
        kt = (kj.astype(F32) * 0.125).T.astype(BF16)
        ke = [jnp.where(_head_mask(e), kj, jnp.zeros_like(kj)) for e in range(2)]
        ve = [jnp.where(_head_mask(e), vj, jnp.zeros_like(vj)) for e in range(2)]
        ck = [cc_ref[0, ks, e:e + 1] for e in range(2)]
        for r in (dkt_acc, dvt_acc, ds0, ds1):
            r[...] = jnp.zeros_like(r)

        def block(q0, nqs, k0, nks, masked):
            cols = pl.ds(pl.multiple_of(q0, 128), nqs)
            rows = slice(k0, k0 + nks)
            qti = qt_ref[:, cols]
            doti = dot_ref[:, cols]
            for e, ds_acc in enumerate((ds0, ds1)):
                dims = slice(e * HEAD, (e + 1) * HEAD)
                st = _dot(ke[e][rows, :], qti) - ck[e][rows, :]
                if masked:
                    krow = lax.broadcasted_iota(jnp.int32, (nks, nqs), 0) + (j * t + k0)
                    qcol = lax.broadcasted_iota(jnp.int32, (nks, nqs), 1) + q0
                    st = jnp.where(krow <= qcol, st, NEG)
                pt = jnp.exp(st - lse_ref[0, e:e + 1, cols])
                dpt = _dot(ve[e][rows, :], doti)
                dst = pt * (dpt - delta_ref[e:e + 1, cols])
                dsb = dst.astype(BF16)
                dvt_acc[dims, rows] += _dot(doti[dims, :], pt.astype(BF16), NT)
                dkt_acc[dims, rows] += _dot(qti[dims, :], dsb, NT)
                dqt_ref[dims, cols] += _dot(kt[dims, rows], dsb)
                ds_acc[rows, 0:nqs] += dst
                dcq_ref[0, e:e + 1, cols] += jnp.sum(dst, axis=0, keepdims=True)

        half = t // 2
        block(j * t, half, 0, half, True)
        block(j * t + half, half, 0, t, True)

        def rest(i, c):
            block(i * t, t, 0, t, False)
            return c

        lax.fori_loop(j + 1, nq, rest, 0)
        dk_ref[...] = dkt_acc[...].T.astype(BF16)
        dv_ref[...] = dvt_acc[...].T.astype(BF16)
        lane = lax.broadcasted_iota(jnp.int32, (t, 128), 1)
        c0 = jnp.sum(ds0[...], axis=-1, keepdims=True)
        c1 = jnp.sum(ds1[...], axis=-1, keepdims=True)
        dck_ref[0] = jnp.where(lane == 0, c0, jnp.where(lane == 1, c1, 0.0))

    res_t = lambda: pl.BlockSpec((128, s), lambda hp, j: (hp, 0))
    blk = lambda: pl.BlockSpec((t, 128), lambda hp, j: (j, hp))
    return pl.pallas_call(
        body, name="fox_bwd", grid=(4, nq),
        in_specs=[res_t(), blk(), blk(), res_t(), res_t(), pl.BlockSpec((1, s, 128), lambda hp, j: (hp, 0, 0)),
                  pl.BlockSpec((1, 8, s), lambda hp, j: (hp, 0, 0))],
        out_specs=[res_t(), blk(), blk(),
                   pl.BlockSpec((1, t, 128), lambda hp, j: (hp, j, 0)),
                   pl.BlockSpec((1, 8, s), lambda hp, j: (hp, 0, 0))],
        out_shape=[jax.ShapeDtypeStruct((512, s), F32), jax.ShapeDtypeStruct((s, 512), BF16),
                   jax.ShapeDtypeStruct((s, 512), BF16), jax.ShapeDtypeStruct((4, s, 128), F32),
                   jax.ShapeDtypeStruct((4, 8, s), F32)],
        scratch_shapes=[pltpu.VMEM((8, s), F32)] + [pltpu.VMEM((128, t), F32)] * 2 + [pltpu.VMEM((t, t), F32)] * 2,
        compiler_params=_params(("arbitrary", "arbitrary")),
    )(qt, k, v, dot, ot, cc4, lse4)


def _mem_bwd(qm, mk, mv, dom):
    s = qm.shape[0]
    tq = min(512, s)

    def body(q_ref, mk_ref, mv_ref, do_ref, dq_ref, dmk_ref, dmv_ref):
        i = pl.program_id(0)

        @pl.when(i == 0)
        def _():
            dmk_ref[...] = jnp.zeros_like(dmk_ref)
            dmv_ref[...] = jnp.zeros_like(dmv_ref)

        heads = range(MEM_HEADS)
        hs = lambda h: slice(h * 128, (h + 1) * 128)
        sc = [_dot(q_ref[:, hs(h)], mk_ref[:, hs(h)], NT) * MEM_SCALE for h in heads]
        dp = [_dot(do_ref[:, hs(h)], mv_ref[:, hs(h)], NT) for h in heads]
        pb, dsb = [], []
        for h in heads:
            p = jnp.exp(sc[h] - jnp.max(sc[h], axis=-1, keepdims=True))
            p = p / jnp.sum(p, axis=-1, keepdims=True)
            ds = p * (dp[h] - jnp.sum(p * dp[h], axis=-1, keepdims=True))
            pb.append(p.astype(BF16))
            dsb.append((ds * MEM_SCALE).astype(BF16))
        dq = [_dot(dsb[h], mk_ref[:, hs(h)]).astype(BF16) for h in heads]
        dmk = [_dot(dsb[h], q_ref[:, hs(h)], TN) for h in heads]
        dmv = [_dot(pb[h], do_ref[:, hs(h)], TN) for h in heads]
        for h in heads:
            dq_ref[:, hs(h)] = dq[h]
            dmk_ref[:, hs(h)] += dmk[h]
            dmv_ref[:, hs(h)] += dmv[h]

    return pl.pallas_call(
        body, name="mem_bwd", grid=(s // tq,),
        in_specs=[pl.BlockSpec((tq, 512), lambda i: (i, 0)), _full(mk.shape), _full(mv.shape),
                  pl.BlockSpec((tq, 512), lambda i: (i, 0))],
        out_specs=[pl.BlockSpec((tq, 512), lambda i: (i, 0)), _full(mk.shape), _full(mv.shape)],
        out_shape=[jax.ShapeDtypeStruct((s, 512), BF16), jax.ShapeDtypeStruct(mk.shape, F32),
                   jax.ShapeDtypeStruct(mv.shape, F32)],
        compiler_params=_params(("arbitrary",)),
    )(qm, mk, mv, dom)


def _memkv_bwd(dmk, dmv, kv_raw, kn_mem, mem, g_mem, mem_n, w_kv):
    def body(dmk_ref, dmv_ref, kv_ref, kn_ref, mem_ref, g_ref, mn_ref, w_ref, dw_ref, dkn_ref, dg_ref, dkv_ref):
        dkn = jnp.zeros((1, 128), F32)
        for h in range(MEM_HEADS):
            hs = slice(h * 128, (h + 1) * 128)
            v = kv_ref[:, hs]
            r = lax.rsqrt(jnp.mean(v * v, axis=-1, keepdims=True) + EPS)
            n = v * r
            dn = dmk_ref[:, hs]
            dkn = dkn + jnp.sum(dn * n, axis=0, keepdims=True)
            dng = dn * kn_ref[...]
            dkv_ref[:, hs] = (r * (dng - n * jnp.mean(dng * n, axis=-1, keepdims=True))).astype(BF16)
        dkv_ref[:, 512:1024] = dmv_ref[...].astype(BF16)
        dkn_ref[...] = dkn
        dkv = dkv_ref[...]
        dw_ref[...] = _dot(mn_ref[...], dkv, TN).astype(BF16)
        dmn = _dot(dkv, w_ref[...], NT)
        xv = mem_ref[...]
        r = lax.rsqrt(jnp.mean(xv * xv, axis=-1, keepdims=True) + EPS)
        dg_ref[...] = jnp.sum(dmn * (xv * r), axis=0, keepdims=True)

    m = mem.shape[0]
    return pl.pallas_call(
        body, name="memkv_bwd",
        out_shape=[jax.ShapeDtypeStruct((D_MODEL, 1024), BF16), jax.ShapeDtypeStruct((1, 128), F32),
                   jax.ShapeDtypeStruct((1, D_MODEL), F32)],
        scratch_shapes=[pltpu.VMEM((m, 1024), BF16)],
        compiler_params=pltpu.CompilerParams(vmem_limit_bytes=VMEM_LIMIT),
    )(dmk, dmv, kv_raw, kn_mem, mem, g_mem, mem_n, w_kv)


def _fox_gate_bwd(dcq4, dck4, proj, b_forget128):
    s = dck4.shape[1]
    tm = min(512, s)
    nt = s // tm

    def body(dcq_ref, dck_ref, p_ref, b_ref, dfl_ref, db_ref, carry_ref):
        i = pl.program_id(0)

        @pl.when(i == 0)
        def _():
            carry_ref[...] = jnp.zeros_like(carry_ref)
            db_ref[...] = jnp.zeros_like(db_ref)

        dcv = jnp.zeros((tm, 128), F32)
        for hp in range(4):
            by_query = jnp.concatenate([dcq_ref[hp], jnp.zeros((120, tm), F32)], axis=0).T
            d = by_query - dck_ref[hp]
            dcv = dcv + (d if hp == 0 else pltpu.roll(d, 2 * hp, 1))
        dlogf = jnp.dot(_tri(tm, False), dcv, precision=lax.Precision.HIGHEST, preferred_element_type=F32) + carry_ref[...]
        carry_ref[...] += jnp.sum(dcv, axis=0, keepdims=True)
        z = p_ref[...] + b_ref[...]
        dfl = dlogf * (1.0 / (1.0 + jnp.exp(z)))
        dfl_ref[...] = dfl.astype(BF16)
        db_ref[...] += jnp.sum(dfl, axis=0, keepdims=True)

    return pl.pallas_call(
        body, name="fox_gate_bwd", grid=(nt,),
        in_specs=[pl.BlockSpec((4, 8, tm), lambda i: (0, 0, nt - 1 - i)),
                  pl.BlockSpec((4, tm, 128), lambda i: (0, nt - 1 - i, 0)),
                  pl.BlockSpec((tm, 128), lambda i: (nt - 1 - i, 0)), _full((1, 128))],
        out_specs=[pl.BlockSpec((tm, 128), lambda i: (nt - 1 - i, 0)), _full((1, 128))],
        out_shape=[jax.ShapeDtypeStruct((s, 128), BF16), jax.ShapeDtypeStruct((1, 128), F32)],
        scratch_shapes=[pltpu.VMEM((1, 128), F32)],
        compiler_params=_params(("arbitrary",)),
    )(dcq4, dck4, proj, b_forget128)


def _proj_pre_bwd(dproj, proj, dqf, dkf, dvf, dqm, dqa, dka, dva, dfl, gq_fox, gk_fox, gq_mem, gq_swa, gk_swa):
    s = proj.shape[0]
    tm = min(512, s)

    def body(dp_in, p_ref, dqf_ref, dkf_ref, dvf_ref, dqm_ref, dqa_ref, dka_ref, dva_ref, dfl_ref,
             gqf, gkf, gqm, gqa, gka, dp_ref, dgn_ref):
        i = pl.program_id(0)

        @pl.when(i == 0)
        def _():
            dgn_ref[...] = jnp.zeros_like(dgn_ref)

        def norm_bwd(off, width, hd, g_ref, dn_ref, slot):
            acc = jnp.zeros((1, 128), F32)
            for b in range(width // 128):
                v = p_ref[:, off + b * 128: off + (b + 1) * 128].astype(F32)
                r = lax.rsqrt(_group_mean(v * v, hd) + EPS)
                n = v * r
                dn = dn_ref[b * 128:(b + 1) * 128, :].T if slot == 0 else dn_ref[:, b * 128:(b + 1) * 128].astype(F32)
                acc = acc + jnp.sum(dn * n, axis=0, keepdims=True)
                dng = dn * g_ref[...]
                dp_ref[:, off + b * 128: off + (b + 1) * 128] = (r * (dng - n * _group_mean(dng * n, hd))).astype(BF16)
            dgn_ref[slot:slot + 1, :] += acc

        norm_bwd(H_QF, 512, HEAD, gqf, dqf_ref, 0)
        norm_bwd(H_KF, 512, HEAD, gkf, dkf_ref, 1)
        dp_ref[:, H_VF:H_VF + 512] = dvf_ref[...].astype(BF16)
        norm_bwd(H_QM, 512, MEM_HEAD, gqm, dqm_ref, 2)
        norm_bwd(H_QA, 512, HEAD, gqa, dqa_ref, 3)
        norm_bwd(H_KA, 128, HEAD, gka, dka_ref, 4)
        dp_ref[:, H_VA:H_VA + 128] = dva_ref[...].astype(BF16)
        dp_ref[:, H_FL:H_FL + 128] = dfl_ref[...]
        dp_ref[:, H_FL + 128:HALF_W] = jnp.zeros((tm, HALF_W - H_FL - 128), BF16)

    row = lambda w: pl.BlockSpec((tm, w), lambda i: (i, 0))
    g_spec = _full((1, 128))
    return pl.pallas_call(
        body, name="proj_pre_bwd", grid=(s // tm,),
        in_specs=[pl.BlockSpec(memory_space=pl.ANY), pl.BlockSpec((tm, HALF_W), lambda i: (i, 1)),
                  pl.BlockSpec((512, tm), lambda i: (0, i)), row(512), row(512), row(512), row(512),
                  row(128), row(128), row(128), g_spec, g_spec, g_spec, g_spec, g_spec],
        out_specs=[pl.BlockSpec((tm, HALF_W), lambda i: (i, 1)), _full((8, 128))],
        out_shape=[jax.ShapeDtypeStruct((s, PROJ_W), BF16), jax.ShapeDtypeStruct((8, 128), F32)],
        input_output_aliases={0: 0},
        compiler_params=_params(("arbitrary",)),
    )(dproj, proj, dqf, dkf, dvf, dqm, dqa, dka, dva, dfl, gq_fox, gk_fox, gq_mem, gq_swa, gk_swa)


def _in_bwd_x(dproj, w_in_p, x, g_mix, dx1):
    s = x.shape[0]
    tm = min(256, s)

    def body(dp_ref, w_ref, x_ref, g_ref, dx1_ref, gx_ref, dg_ref):
        i = pl.program_id(0)

        @pl.when(i == 0)
        def _():
            dg_ref[...] = jnp.zeros_like(dg_ref)

        dx, dg = _rms_bwd(x_ref[...], g_ref[...], _dot(dp_ref[...], w_ref[...], NT), dx1_ref[...])
        gx_ref[...] = dx
        dg_ref[...] += dg

    row = pl.BlockSpec((tm, D_MODEL), lambda i: (i, 0))
    return pl.pallas_call(
        body, name="in_bwd_x", grid=(s // tm,),
        in_specs=[pl.BlockSpec((tm, PROJ_W), lambda i: (i, 0)), _full(w_in_p.shape), row, _full((1, D_MODEL)), row],
        out_specs=[row, _full((1, D_MODEL))],
        out_shape=[jax.ShapeDtypeStruct((s, D_MODEL), F32), jax.ShapeDtypeStruct((1, D_MODEL), F32)],
        compiler_params=_params(("arbitrary",)),
    )(dproj, w_in_p, x, g_mix, dx1)


def _rel_bias_bwd(dbias, bucket):
    def body(db_ref, bk_ref, o_ref):
        bk = bk_ref[...]
        lane = lax.broadcasted_iota(jnp.int32, (1, 128), 1)
        for b in range(REL_BUCKETS):
            sel = bk == b
            acc = jnp.zeros((1, 128), F32)
            for h in range(SWA_HEADS):
                tot = jnp.sum(jnp.sum(jnp.where(sel, db_ref[h], 0.0), axis=0, keepdims=True), axis=-1, keepdims=True)
                acc = jnp.where(lane == h, tot, acc)
            o_ref[:, b * 128:(b + 1) * 128] = acc

    return pl.pallas_call(
        body, name="rel_bias_bwd",
        out_shape=jax.ShapeDtypeStruct((1, REL_BUCKETS * 128), F32),
        compiler_params=pltpu.CompilerParams(vmem_limit_bytes=VMEM_LIMIT),
    )(dbias, bucket)


def _my_place():
    return lax.axis_index("x"), lax.axis_index("y"), lax.axis_index("c")


def _peer(place, k):
    x, y, c = place
    return (1 - x if k & 4 else x, 1 - y if k & 2 else y, 1 - c if k & 1 else c)


def _index(place):
    x, y, c = place
    return 4 * x + 2 * y + c


HBM_SPEC = pl.BlockSpec(memory_space=pltpu.HBM)
SEM_SPEC = pl.BlockSpec(memory_space=pltpu.SEMAPHORE)
DATAFLOW = pltpu.SideEffectType.DATAFLOW_SIDE_EFFECTING


ALL_PEERS = tuple(range(1, N_DEV))
SAME_CORE = (2, 4, 6)
OWN = N_DEV - 1


def _split_copy(src_ref, land_ref, send_sems, recv_sems, me, k, gather):
    peer = _peer(me, k)
    if gather:
        src, dst = src_ref, land_ref.at[_index(me)]
    else:
        src, dst = src_ref.at[_index(peer)], land_ref.at[k - 1]
    return pltpu.make_async_remote_copy(src_ref=src, dst_ref=dst, send_sem=send_sems.at[k - 1], recv_sem=recv_sems.at[k - 1],
                                        device_id=peer, device_id_type=MESH)


def _own_copy(src_ref, land_ref, recv_sems, me, gather):
    if gather:
        src, dst = src_ref, land_ref.at[_index(me)]
    else:
        src, dst = src_ref.at[_index(me)], land_ref.at[OWN]
    return pltpu.make_async_copy(src, dst, recv_sems.at[OWN])


def _split_start(srcs, gather, name, peers=ALL_PEERS, after=None):
    n = len(srcs)
    extra = [] if after is None else [after]

    def body(*refs):
        refs = refs[:2 * n] + refs[2 * n + len(extra):]
        src_refs, land_refs = refs[:n], refs[n:2 * n]
        send_sems, recv_sems, token = refs[2 * n:3 * n], refs[3 * n:4 * n], refs[-1]
        me = _my_place()
        for w in range(n):
            for k in peers:
                _split_copy(src_refs[w], land_refs[w], send_sems[w], recv_sems[w], me, k, gather).start()
            _own_copy(src_refs[w], land_refs[w], recv_sems[w], me, gather).start()
        token[...] = jnp.zeros_like(token)

    lands = [lax.empty((N_DEV,) + (a.shape if gather else a.shape[1:]), a.dtype) for a in srcs]
    sems = [pltpu.SemaphoreType.DMA((N_DEV,))] * (2 * n)
    hbm = [pltpu.HBM(a.shape, a.dtype) for a in list(srcs) + lands]
    outs = pl.pallas_call(
        body, name=name,
        out_shape=(*sems, *hbm, jax.ShapeDtypeStruct((8, 128), F32)),
        in_specs=(HBM_SPEC,) * (2 * n) + (pl.BlockSpec(memory_space=pl.ANY),) * len(extra),
        out_specs=(SEM_SPEC,) * (2 * n) + (HBM_SPEC,) * (2 * n) + (pl.BlockSpec(memory_space=pltpu.VMEM),),
        input_output_aliases={i: 2 * n + i for i in range(2 * n)},
        compiler_params=pltpu.CompilerParams(has_side_effects=DATAFLOW),
    )(*[pltpu.with_memory_space_constraint(a, pltpu.HBM) for a in list(srcs) + lands], *extra)
    return list(outs[:n]), list(outs[n:2 * n]), list(outs[2 * n:3 * n]), list(outs[3 * n:4 * n]), outs[-1]


def _split_wait(started, w, after, gather, name):
    send_sems, recv_sems, srcs, lands, _ = started

    def body(src_ref, land_ref, send_sems, recv_sems, after_ref, src_out, land_out):
        me = _my_place()
        for k in ALL_PEERS:
            cp = _split_copy(src_ref, land_ref, send_sems, recv_sems, me, k, gather)
            cp.wait_send()
            cp.wait_recv()
        _own_copy(src_ref, land_ref, recv_sems, me, gather).wait()

    return pl.pallas_call(
        body, name=name,
        out_shape=(pltpu.HBM(srcs[w].shape, srcs[w].dtype), pltpu.HBM(lands[w].shape, lands[w].dtype)),
        in_specs=(HBM_SPEC, HBM_SPEC, SEM_SPEC, SEM_SPEC, pl.BlockSpec(memory_space=pl.ANY)),
        out_specs=(HBM_SPEC, HBM_SPEC), input_output_aliases={0: 0, 1: 1},
        compiler_params=pltpu.CompilerParams(has_side_effects=DATAFLOW),
    )(srcs[w], lands[w], send_sems[w], recv_sems[w], after)[1]


def _forward_copy(land_ref, send_sems, recv_sems, me, j, incoming):
    sibling = _peer(me, 1)
    rows = land_ref.at[_index(_peer(sibling if incoming else me, SAME_CORE[j]))]
    return pltpu.make_async_remote_copy(src_ref=rows, dst_ref=rows, send_sem=send_sems.at[j], recv_sem=recv_sems.at[j],
                                        device_id=sibling, device_id_type=MESH)


def _forward_start(started, after, name):
    send_a, recv_a, srcs, lands, _ = started

    def body(src_ref, land_ref, send_a, recv_a, after_ref, send_b, recv_b, src_out, land_out):
        me = _my_place()
        for j, k in enumerate(SAME_CORE):
            _split_copy(src_ref, land_ref, send_a, recv_a, me, k, True).wait_recv()
            _forward_copy(land_ref, send_b, recv_b, me, j, False).start()

    sems = pltpu.SemaphoreType.DMA((len(SAME_CORE),))
    return pl.pallas_call(
        body, name=name,
        out_shape=(sems, sems, pltpu.HBM(srcs[0].shape, srcs[0].dtype), pltpu.HBM(lands[0].shape, lands[0].dtype)),
        in_specs=(HBM_SPEC, HBM_SPEC, SEM_SPEC, SEM_SPEC, pl.BlockSpec(memory_space=pl.ANY)),
        out_specs=(SEM_SPEC, SEM_SPEC, HBM_SPEC, HBM_SPEC), input_output_aliases={0: 2, 1: 3},
        compiler_params=pltpu.CompilerParams(has_side_effects=DATAFLOW),
    )(srcs[0], lands[0], send_a[0], recv_a[0], after)


def _forward_wait(started, forwarded, name):
    send_a, recv_a, _, _, _ = started
    send_b, recv_b, src, land = forwarded

    def body(src_ref, land_ref, send_a, recv_a, send_b, recv_b, src_out, land_out):
        me = _my_place()
        _own_copy(src_ref, land_ref, recv_a, me, True).wait()
        for k in (1,) + SAME_CORE:
            _split_copy(src_ref, land_ref, send_a, recv_a, me, k, True).wait_send()
        _split_copy(src_ref, land_ref, send_a, recv_a, me, 1, True).wait_recv()
        for j in range(len(SAME_CORE)):
            _forward_copy(land_ref, send_b, recv_b, me, j, False).wait_send()
            _forward_copy(land_ref, send_b, recv_b, me, j, True).wait_recv()

    return pl.pallas_call(
        body, name=name,
        out_shape=(pltpu.HBM(src.shape, src.dtype), pltpu.HBM(land.shape, land.dtype)),
        in_specs=(HBM_SPEC, HBM_SPEC, SEM_SPEC, SEM_SPEC, SEM_SPEC, SEM_SPEC),
        out_specs=(HBM_SPEC, HBM_SPEC), input_output_aliases={0: 0, 1: 1},
        compiler_params=pltpu.CompilerParams(has_side_effects=DATAFLOW),
    )(src, land, send_a[0], recv_a[0], send_b, recv_b)[1]


def _adam_math(w, g, m, v):
    m2 = ADAM_B1 * m + (1.0 - ADAM_B1) * g
    v2 = ADAM_B2 * v + (1.0 - ADAM_B2) * (g * g)
    m_hat = m2 / (1.0 - ADAM_B1 ** ADAM_STEP)
    v_hat = v2 / (1.0 - ADAM_B2 ** ADAM_STEP)
    delta = -ADAM_LR * (m_hat / (jnp.sqrt(v_hat) + ADAM_EPS) + ADAM_WD * w)
    return delta, m2, v2


def _adamw(lands, w, m, v, name):
    a, b = w.shape
    bp = lands[0].shape[2]
    ta = min(128, a)
    per = a // len(lands) // ta

    def body(*refs):
        p_refs = refs[:len(lands)]
        w_ref, m_ref, v_ref, g_ref, d_ref, m2_ref, v2_ref = refs[len(lands):]
        i = pl.program_id(0)

        def run(p_ref):
            g = p_ref[0, :, 0:b].astype(F32)
            for k in range(1, N_DEV):
                g = g + p_ref[k, :, 0:b].astype(F32)
            delta, m2, v2 = _adam_math(w_ref[...], g, m_ref[...], v_ref[...])
            g_ref[...] = g
            d_ref[...] = delta
            m2_ref[...] = m2
            v2_ref[...] = v2

        for part, p_ref in enumerate(p_refs):
            pl.when((i >= part * per) & (i < (part + 1) * per))(functools.partial(run, p_ref))

    land_spec = lambda part: pl.BlockSpec((N_DEV, ta, bp), lambda i: (0, jnp.clip(i - part * per, 0, per - 1), 0))
    blk = pl.BlockSpec((ta, b), lambda i: (i, 0))
    sd = jax.ShapeDtypeStruct((a, b), F32)
    return pl.pallas_call(
        body, name=name, grid=(a // ta,),
        in_specs=[land_spec(part) for part in range(len(lands))] + [blk, blk, blk],
        out_specs=[blk, blk, blk, blk], out_shape=[sd, sd, sd, sd],
        compiler_params=_params(("parallel",)),
    )(*lands, w, m, v)


def _bucket_table():
    t_loc = jnp.arange(SWA_BLOCK)[:, None] + SWA_BLOCK
    s_loc = jnp.arange(2 * SWA_BLOCK)[None, :]
    dist = t_loc - s_loc
    max_exact = REL_BUCKETS // 2
    d = jnp.maximum(dist, 0)
    df = jnp.maximum(d, 1).astype(F32)
    large = max_exact + (jnp.log(df / max_exact) / math.log(REL_MAX_DIST / max_exact) * (REL_BUCKETS - max_exact)).astype(jnp.int32)
    large = jnp.minimum(large, REL_BUCKETS - 1)
    bucket = jnp.where(d < max_exact, d, large)
    band = (dist >= 0) & (dist < SWA_BLOCK)
    return bucket, band


def _tile2(g):
    return jnp.concatenate([g, g], axis=1) if g.shape[1] == HEAD else g


SHARD_W = 737
SHARD_WP = 768
IN_WIDTH = N_DEV * SHARD_W
SEGMENTS = ((GL0, 2824, 3072), (QF0, 768, 512), (KF0, 1280, 512), (VF0, 1792, 512), (QM0, 2312, 512),
            (QA0, 0, 512), (KA0, 512, 128), (VA0, 640, 128), (FL0, 2304, 8))


def _lane_plan(sources):
    plan = []
    for t in range(len(sources) // 128):
        groups = {}
        for lane in range(128):
            src = sources[128 * t + lane]
            if src is not None:
                slab, col = src
                groups.setdefault((slab, col // 128, (lane - col) % 128), []).append(lane)
        tile = []
        for key, lanes in groups.items():
            assert lanes == list(range(lanes[0], lanes[-1] + 1))
            tile.append((key, lanes[0], lanes[-1] + 1))
        plan.append(tile)
    return plan


def _assemble(tile_plan, load, rows):
    lane = lax.broadcasted_iota(jnp.int32, (1, 128), 1)
    out = jnp.zeros((rows, 128), F32)
    for (slab, st, roll), lo, hi in tile_plan:
        v = load(slab, st)
        if roll:
            v = pltpu.roll(v, roll, 1)
        out = v if (lo, hi) == (0, 128) else jnp.where((lane >= lo) & (lane < hi), v, out)
    return out


def _w_in_from_shards(land):
    ref_col = [None] * PROJ_W
    for p0, r0, n in SEGMENTS:
        for i in range(n):
            ref_col[p0 + i] = divmod(r0 + i, SHARD_W)
    plan = _lane_plan(ref_col)
    d_model = land.shape[1]
    tm = 256

    def body(land_ref, o_ref):
        load = lambda slab, st: land_ref[slab, :, st * 128:(st + 1) * 128].astype(F32)
        for t, tile_plan in enumerate(plan):
            o_ref[:, t * 128:(t + 1) * 128] = _assemble(tile_plan, load, tm).astype(BF16)

    return pl.pallas_call(
        body, name="w_in_from_shards", grid=(d_model // tm,),
        in_specs=[pl.BlockSpec((N_DEV, tm, SHARD_WP), lambda i: (0, i, 0))],
        out_specs=pl.BlockSpec((tm, PROJ_W), lambda i: (i, 0)),
        out_shape=jax.ShapeDtypeStruct((d_model, PROJ_W), BF16),
        compiler_params=_params(("parallel",)),
    )(land)


def _dw_in_to_parts(dwp, name):
    padded_col = [None] * IN_WIDTH
    for p0, r0, n in SEGMENTS:
        for i in range(n):
            padded_col[r0 + i] = p0 + i
    sources = []
    for d in range(N_DEV):
        sources += [(0, padded_col[SHARD_W * d + c]) if c < SHARD_W else None for c in range(SHARD_WP)]
    plan = _lane_plan(sources)
    d_model = dwp.shape[0]
    tm = 256
    tiles = SHARD_WP // 128

    def body(dw_ref, o_ref):
        load = lambda slab, st: dw_ref[:, st * 128:(st + 1) * 128].astype(F32)
        for t, tile_plan in enumerate(plan):
            d, c = divmod(t, tiles)
            o_ref[d, :, c * 128:(c + 1) * 128] = _assemble(tile_plan, load, tm).astype(BF16)

    return pl.pallas_call(
        body, name=name, grid=(d_model // tm,),
        in_specs=[pl.BlockSpec((tm, PROJ_W), lambda i: (i, 0))],
        out_specs=pl.BlockSpec((N_DEV, tm, SHARD_WP), lambda i: (0, i, 0)),
        out_shape=jax.ShapeDtypeStruct((N_DEV, d_model, SHARD_WP), BF16),
        compiler_params=_params(("parallel",)),
    )(dwp)


def _cast_shards(shards):
    names = list(shards)

    def body(*refs):
        for src, dst in zip(refs[:len(names)], refs[len(names):]):
            if dst.shape != src.shape:
                dst[...] = jnp.zeros(dst.shape, BF16)
                dst[:, 0:src.shape[1]] = src[...].astype(BF16)
            else:
                dst[...] = src[...].astype(BF16)

    out_shape = [jax.ShapeDtypeStruct((shards[n].shape[0], SHARD_WP if n == "w_in" else shards[n].shape[1]), BF16)
                 for n in names]
    outs = pl.pallas_call(body, name="cast_shards", out_shape=out_shape,
                          compiler_params=pltpu.CompilerParams(vmem_limit_bytes=VMEM_LIMIT))(*[shards[n] for n in names])
    return dict(zip(names, outs))


def _tie(x, *tokens):
    for t in tokens:
        if t is not None:
            x = x + t[0:1, 0:1]
    return x


def _local_step(x, mem, target, p, getw, emit, deps=()):
    s = x.shape[0]
    bucket, band = _bucket_table()
    bucket_m = jnp.where(band, bucket, -1).astype(jnp.int32)
    bias = _bias_table(p["rel_bias"], bucket_m)
    bucket_t = jnp.transpose(bucket_m)
    bias_t = _bias_table(p["rel_bias"], bucket_t)
    gqf, gkf, gqa, gka = _tile2(p["qn_fox"]), _tile2(p["kn_fox"]), _tile2(p["qn_swa"]), _tile2(p["kn_swa"])
    gqm = p["qn_mem"]
    bf128 = jnp.pad(p["b_forget"], ((0, 0), (0, 120)))
    sink = p["sink_swa"].reshape(8)

    h = _rms_fwd(x, p["g_mix"], "rms_mix", tuple(deps) + (bias, bias_t))
    w_in = getw("w_in", h)
    proj = _mm(h, w_in, "nn", BF16, 512, 1536, 1024, "proj")
    fl = _mm(h, w_in[:, FL0:FL0 + 128], "nn", F32, 512, 128, 1024, "proj_fl")
    qf, kf, vf, qm, qa, ka, va, qf_t, vf_t = _proj_post(proj, gqf, gkf, gqm, gqa, gka)
    cc4, ca4 = _fox_gate_fwd(fl, bf128)
    w_kv = getw("w_mem_kv", cc4)
    mem_n, kv_raw, mk, mv = _memkv_fwd(mem, p["g_mem"], w_kv, p["kn_mem"])
    kp = jnp.pad(ka, ((SWA_BLOCK, 0), (0, 0)))
    vp = jnp.pad(va, ((SWA_BLOCK, 0), (0, 0)))
    oa = _swa_fwd(qa, kp, vp, bias, sink)
    of, lse4, of_t = _fox_fwd(qf, kf, vf_t, ca4)
    om = _mem_fwd(qm, mk, mv)
    wa, wf, wm, w_out = getw("w_o_swa", oa), getw("w_o_fox", oa), getw("w_o_mem", oa), getw("w_out", oa)
    x1, hm, merged = _merge_fwd(x, oa, of, om, proj, p["b_gate"], wa, wf, wm, w_out, p["g_mlp"])
    w_up = getw("w_mlp_up", of)
    u = _mlp_up(hm, w_up)
    w_down = getw("w_mlp_down", hm)
    dy, dy_b, loss = _mlp_down_loss(u, w_down, x1, target)

    da = _mlp_bwd_act(dy_b, w_down, u)
    t_down = emit({"w_mlp_down": _mm(u, dy_b, "tn", BF16, 1024, 1024, 2048, "dw_down")})
    dx1, dg_mlp = _mlp_bwd_x(da, w_up, x1, dy, _tie(p["g_mlp"], t_down))
    t_up = emit({"w_mlp_up": _mm(hm, da, "tn", BF16, 1024, 1024, 2048, "dw_up", column_chunks=True)})
    dproj, doa, dof_t, dom, dya, dyf, dym, db_gate = _merge_bwd(
        dx1, oa, of, om, proj, _tie(p["b_gate"], t_up), wa, wf, wm, w_out)
    dw_oa, dw_of, dw_om = _mm_tn3([oa, of, om], [dya, dyf, dym], "dw_o")
    t_o = emit({"w_out": _mm(merged, dx1, "tn", BF16, 1024, 1024, 2048, "dw_out"),
                "w_o_swa": dw_oa, "w_o_fox": dw_of, "w_o_mem": dw_om})

    dqm, dmk, dmv = _mem_bwd(qm, mk, mv, dom)
    dw_kv, dkn_mem, dg_mem = _memkv_bwd(dmk, dmv, kv_raw, _tie(p["kn_mem"], t_o), mem, p["g_mem"], mem_n, w_kv)
    t_kv = emit({"w_mem_kv": dw_kv})
    dqa, dkp, dvp, dbias, dsink = _swa_bwd(qa, kp, vp, bias_t, _tie(p["sink_swa"], t_kv).reshape(8), doa)
    dqf_t, dkf, dvf, dck4, dcq4 = _fox_bwd(qf_t, kf, vf, dof_t, of_t, cc4, lse4)

    dfl, db_forget = _fox_gate_bwd(dcq4, dck4, fl, bf128)

    dproj, dgn = _proj_pre_bwd(dproj, proj, dqf_t, dkf, dvf, dqm, dqa, dkp[SWA_BLOCK:], dvp[SWA_BLOCK:], dfl,
                               gqf, gkf, gqm, gqa, gka)
    t_in = emit({"w_in_a": _mm(h, dproj, "tn", BF16, 512, 3072, 1024, "dw_in_a", m_part=(0, 2))})
    t_in = emit({"w_in_b": _mm(h, dproj, "tn", BF16, 512, 3072, 1024, "dw_in_b", m_part=(1, 2), after=t_in)})
    grad_x, dg_mix = _in_bwd_x(dproj, w_in, x, _tie(p["g_mix"], t_in), dx1)
    d_rel = _rel_bias_bwd(dbias, bucket_t)

    fold = lambda r: dgn[r:r + 1, 0:HEAD] + dgn[r:r + 1, HEAD:128]
    small = {
        "g_mix": dg_mix, "b_gate": db_gate, "b_forget": db_forget[:, 0:8],
        "qn_swa": fold(3), "kn_swa": fold(4), "sink_swa": dsink[:, 0].reshape(1, 8), "rel_bias": d_rel,
        "qn_fox": fold(0), "kn_fox": fold(1), "g_mem": dg_mem, "qn_mem": dgn[2:3, :], "kn_mem": dkn_mem,
        "g_mlp": dg_mlp,
    }
    return loss, grad_x, small


SMALL = ("g_mix", "b_gate", "b_forget", "qn_swa", "kn_swa", "sink_swa", "rel_bias", "qn_fox", "kn_fox", "g_mem",
         "qn_mem", "kn_mem", "g_mlp")
BIG = ("w_in", "w_mem_kv", "w_o_swa", "w_o_fox", "w_o_mem", "w_out", "w_mlp_up", "w_mlp_down")
COL_SHARDED = ("w_in", "w_o_swa", "w_o_fox", "w_o_mem", "w_mlp_up")
WEIGHTS = ("g_mix", "w_in", "b_gate", "b_forget", "qn_swa", "kn_swa", "sink_swa", "rel_bias", "qn_fox", "kn_fox", "g_mem",
           "w_mem_kv", "qn_mem", "kn_mem", "w_o_swa", "w_o_fox", "w_o_mem", "w_out", "g_mlp", "w_mlp_up", "w_mlp_down")
SMALL_SLOTS = (("g_mix", 1024), ("b_gate", 3072), ("b_forget", 128), ("qn_swa", 128), ("kn_swa", 128), ("sink_swa", 128),
               ("rel_bias", REL_BUCKETS * 128), ("qn_fox", 128), ("kn_fox", 128), ("g_mem", 1024), ("qn_mem", 128),
               ("kn_mem", 128), ("g_mlp", 1024), ("loss", 128))
SMALL_OFF = {n: sum(w for _, w in SMALL_SLOTS[:i]) for i, (n, _) in enumerate(SMALL_SLOTS)}
SMALL_ROW = sum(w for _, w in SMALL_SLOTS)


def _gathered_to_full(name, g):
    if name in COL_SHARDED:
        return jnp.transpose(g, (1, 0, 2)).reshape(g.shape[1], N_DEV * g.shape[2])
    return g.reshape(N_DEV * g.shape[1], g.shape[2])


def _full_to_parts(name, full, b):
    if name in COL_SHARDED:
        return jnp.transpose(full.reshape(full.shape[0], N_DEV, b), (1, 0, 2)).astype(BF16)
    return full.reshape(N_DEV, full.shape[0] // N_DEV, full.shape[1]).astype(BF16)


def _pack_small(grads, loss):
    pieces = []
    for n, width in SMALL_SLOTS:
        a = loss.reshape(1, 1) if n == "loss" else grads[n].reshape(1, -1)
        pieces.append(jnp.pad(a, ((0, 0), (0, width - a.shape[1]))))
    return jnp.concatenate(pieces, axis=1)


def _adamw_small(gathered, w, m, v):
    names = list(SMALL)

    def body(*refs):
        p_ref = refs[0]
        ins = refs[1:1 + 3 * len(names)]
        outs = refs[1 + 3 * len(names):]
        g_all = p_ref[0]
        for k in range(1, N_DEV):
            g_all = g_all + p_ref[k]
        for i, n in enumerate(names):
            w_ref, m_ref, v_ref = ins[3 * i:3 * i + 3]
            out = outs[4 * i:4 * i + 4]
            rows, cols = w_ref.shape
            for r in range(rows):
                off = SMALL_OFF[n] + 128 * r
                g = g_all[:, off:off + cols]
                rs = slice(r, r + 1)
                res = (g,) + _adam_math(w_ref[rs, :], g, m_ref[rs, :], v_ref[rs, :])
                for o_ref, val in zip(out, res):
                    o_ref[rs, :] = val
        outs[-1][...] = g_all[:, SMALL_OFF["loss"]:SMALL_OFF["loss"] + 128]

    args = [gathered]
    out_shape = []
    for n in names:
        args += [w[n], m[n], v[n]]
        out_shape += [jax.ShapeDtypeStruct(w[n].shape, F32)] * 4
    out_shape.append(jax.ShapeDtypeStruct((1, 128), F32))
    outs = pl.pallas_call(body, name="adamw_small", out_shape=out_shape)(*args)
    return {n: outs[4 * i:4 * i + 4] for i, n in enumerate(names)}, outs[-1]


def kernel(x, mem, g_mix, w_in, b_gate, b_forget, qn_swa, kn_swa, sink_swa, rel_bias, qn_fox, kn_fox, g_mem, w_mem_kv, qn_mem, kn_mem, w_o_swa, w_o_fox, w_o_mem, w_out, g_mlp, w_mlp_up, w_mlp_down, loss_target, m_g_mix, m_w_in, m_b_gate, m_b_forget, m_qn_swa, m_kn_swa, m_sink_swa, m_rel_bias, m_qn_fox, m_kn_fox, m_g_mem, m_w_mem_kv, m_qn_mem, m_kn_mem, m_w_o_swa, m_w_o_fox, m_w_o_mem, m_w_out, m_g_mlp, m_w_mlp_up, m_w_mlp_down, v_g_mix, v_w_in, v_b_gate, v_b_forget, v_qn_swa, v_kn_swa, v_sink_swa, v_rel_bias, v_qn_fox, v_kn_fox, v_g_mem, v_w_mem_kv, v_qn_mem, v_kn_mem, v_w_o_swa, v_w_o_fox, v_w_o_mem, v_w_out, v_g_mlp, v_w_mlp_up, v_w_mlp_down):
    wts = dict(g_mix=g_mix, w_in=w_in, b_gate=b_gate, b_forget=b_forget, qn_swa=qn_swa, kn_swa=kn_swa, sink_swa=sink_swa,
               rel_bias=rel_bias, qn_fox=qn_fox, kn_fox=kn_fox, g_mem=g_mem, w_mem_kv=w_mem_kv, qn_mem=qn_mem, kn_mem=kn_mem,
               w_o_swa=w_o_swa, w_o_fox=w_o_fox, w_o_mem=w_o_mem, w_out=w_out, g_mlp=g_mlp, w_mlp_up=w_mlp_up,
               w_mlp_down=w_mlp_down)
    mom = dict(g_mix=m_g_mix, w_in=m_w_in, b_gate=m_b_gate, b_forget=m_b_forget, qn_swa=m_qn_swa, kn_swa=m_kn_swa,
               sink_swa=m_sink_swa, rel_bias=m_rel_bias, qn_fox=m_qn_fox, kn_fox=m_kn_fox, g_mem=m_g_mem, w_mem_kv=m_w_mem_kv,
               qn_mem=m_qn_mem, kn_mem=m_kn_mem, w_o_swa=m_w_o_swa, w_o_fox=m_w_o_fox, w_o_mem=m_w_o_mem, w_out=m_w_out,
               g_mlp=m_g_mlp, w_mlp_up=m_w_mlp_up, w_mlp_down=m_w_mlp_down)
    var = dict(g_mix=v_g_mix, w_in=v_w_in, b_gate=v_b_gate, b_forget=v_b_forget, qn_swa=v_qn_swa, kn_swa=v_kn_swa,
               sink_swa=v_sink_swa, rel_bias=v_rel_bias, qn_fox=v_qn_fox, kn_fox=v_kn_fox, g_mem=v_g_mem, w_mem_kv=v_w_mem_kv,
               qn_mem=v_qn_mem, kn_mem=v_kn_mem, w_o_swa=v_w_o_swa, w_o_fox=v_w_o_fox, w_o_mem=v_w_o_mem, w_out=v_w_out,
               g_mlp=v_g_mlp, w_mlp_up=v_w_mlp_up, w_mlp_down=v_w_mlp_down)

    shards = _cast_shards({n: wts[n][0] for n in BIG})
    first = _split_start([shards["w_in"]], True, "ag_start_w_in", peers=(1,) + SAME_CORE)
    rest = _split_start([shards[n] for n in BIG[1:]], True, "ag_start_rest", after=first[4])
    full = {}

    def getw(n, after):
        if n == "w_in" and n not in full:
            forwarded = _forward_start(first, after, "ag_forward_w_in")
            full[n] = _w_in_from_shards(_forward_wait(first, forwarded, "ag_wait_w_in"))
        elif n not in full:
            land = _split_wait(rest, BIG[1:].index(n), after, True, "ag_wait_" + n)
            full[n] = land if n == "w_mlp_up" else _gathered_to_full(n, land)
        return full[n]

    exchanges = {}

    def emit(grads_by_name):
        parts = []
        for n, grad in grads_by_name.items():
            if n.startswith("w_in"):
                parts.append(_dw_in_to_parts(grad, "d" + n + "_to_parts"))
            else:
                parts.append(grad if n == "w_mlp_up" else _full_to_parts(n, grad, wts[n].shape[2]))
        started = _split_start(parts, False, "rs_start_" + next(iter(grads_by_name)))
        for w, n in enumerate(grads_by_name):
            exchanges[n] = (started, w)
        return started[4]

    small_p = {n: wts[n] for n in SMALL}
    loss, grad_x, small_g = _local_step(x[0], mem[0], loss_target[0], small_p, getw, emit, (first[4], rest[4]))

    packed = _pack_small(small_g, loss)
    small_gather = _split_start([packed], True, "ag_start_small")

    grads, delta, new_m, new_v = {}, {}, {}, {}

    def update(n, after):
        lands = [_split_wait(*exchanges[e], after, False, "rs_wait_" + e) for e in exchanges if e.startswith(n)]
        g, d, m2, v2 = _adamw(lands, wts[n][0], mom[n][0], var[n][0], "adamw_" + n)
        grads[n], delta[n], new_m[n], new_v[n] = g[None], d[None], m2[None], v2[None]
        return d

    after = small_gather[4]
    for n in exchanges:
        if not n.startswith("w_in"):
            after = update(n, after)

    gathered = _split_wait(small_gather, 0, after, True, "ag_wait_small")
    small_out, total = _adamw_small(gathered, small_p, mom, var)
    for name, (g, d, m2, v2) in small_out.items():
        grads[name], delta[name], new_m[name], new_v[name] = g, d, m2, v2
    update("w_in", total)

    return (total[0, 0], grad_x[None], *[grads[n] for n in WEIGHTS], *[delta[n] for n in WEIGHTS],
            *[new_m[n] for n in WEIGHTS], *[new_v[n] for n in WEIGHTS])
```

```python
import functools
import math

import jax
import jax.numpy as jnp
from jax import lax
from jax.experimental import pallas as pl
from jax.experimental.pallas import tpu as pltpu

F32 = jnp.float32
BF16 = jnp.bfloat16

D_MODEL = 1024
N_MEM = 256
D_FF = 4096
HEAD = 64
SWA_HEADS = 8
SWA_BLOCK = 128
MEM_HEADS = 4
MEM_HEAD = 128
EPS = 1e-6
NEG = -1e30
REL_BUCKETS = 32
REL_MAX_DIST = 128

ADAM_LR = 0.001
ADAM_B1 = 0.9
ADAM_B2 = 0.999
ADAM_EPS = 1e-08
ADAM_WD = 0.01
ADAM_STEP = 10

GL0, QF0, KF0, VF0, QM0, QA0, KA0, VA0, FL0 = 0, 3072, 3584, 4096, 4608, 5120, 5632, 5760, 5888
PROJ_W = 6144
HALF_W = 3072
H_QF, H_KF, H_VF, H_QM, H_QA, H_KA, H_VA, H_FL = 0, 512, 1024, 1536, 2048, 2560, 2688, 2816

VMEM_LIMIT = 56 * 1024 * 1024
N_DEV = 8
MESH = pl.DeviceIdType.MESH

NN = (((1,), (0,)), ((), ()))
NT = (((1,), (1,)), ((), ()))
TN = (((0,), (0,)), ((), ()))


def _dot(a, b, dims=NN):
    return lax.dot_general(a, b, dims, preferred_element_type=F32)


def _params(sem):
    return pltpu.CompilerParams(dimension_semantics=sem, vmem_limit_bytes=VMEM_LIMIT)


def _full(shape):
    nd = len(shape)
    return pl.BlockSpec(shape, lambda *_: (0,) * nd)


def _sigmoid(z):
    return 1.0 / (1.0 + jnp.exp(-z))


def _group_mean(v, hd):
    if hd == 128:
        return jnp.mean(v, axis=-1, keepdims=True)
    r = lax.broadcasted_iota(jnp.int32, (128, 128), 0) // HEAD
    c = lax.broadcasted_iota(jnp.int32, (128, 128), 1) // HEAD
    same_head = jnp.where(r == c, 1.0 / HEAD, 0.0).astype(BF16)
    total = None
    rest = v
    for _ in range(2):
        part = rest.astype(BF16)
        rest = rest - part.astype(F32)
        term = _dot(part, same_head)
        total = term if total is None else total + term
    return total


def _mm(a, b, mode, out_dtype, tm, tn, tk, name, column_chunks=False, m_part=(0, 1), after=None):
    if mode == "nn":
        m, k = a.shape
        n = b.shape[1]
    elif mode == "nt":
        m, k = a.shape
        n = b.shape[0]
    else:
        k, m = a.shape
        n = b.shape[1]
    assert mode == "tn" or m_part == (0, 1)
    m //= m_part[1]
    tm, tn, tk = min(tm, m), min(tn, n), min(tk, k)
    m0 = m_part[0] * (m // tm)
    extra = [] if after is None else [after]
    nk = k // tk
    chunk = n // N_DEV
    per_tile = tn // chunk if column_chunks else 1
    dims = {"nn": NN, "nt": NT, "tn": TN}[mode]
    a_spec = pl.BlockSpec((tk, tm), lambda j, i, kk: (kk, m0 + i)) if mode == "tn" else pl.BlockSpec((tm, tk), lambda j, i, kk: (i, kk))
    b_spec = pl.BlockSpec((tn, tk), lambda j, i, kk: (j, kk)) if mode == "nt" else pl.BlockSpec((tk, tn), lambda j, i, kk: (kk, j))

    def body(a_ref, b_ref, *rest):
        o_ref, *acc = rest[len(extra):]
        prod = _dot(a_ref[...].astype(BF16), b_ref[...].astype(BF16), dims)

        def write(res):
            if column_chunks:
                for c in range(per_tile):
                    o_ref[c] = res[:, c * chunk:(c + 1) * chunk].astype(o_ref.dtype)
            else:
                o_ref[...] = res.astype(o_ref.dtype)

        if nk == 1:
            write(prod)
        else:
            acc_ref, = acc
            kk = pl.program_id(2)

            @pl.when(kk == 0)
            def _():
                acc_ref[...] = prod

            @pl.when(kk > 0)
            def _():
                acc_ref[...] += prod

            @pl.when(kk == nk - 1)
            def _():
                write(acc_ref[...])

    return pl.pallas_call(
        body, name=name, grid=(n // tn, m // tm, nk),
        in_specs=[a_spec, b_spec] + [pl.BlockSpec(memory_space=pl.ANY)] * len(extra),
        out_specs=(pl.BlockSpec((per_tile, tm, chunk), lambda j, i, kk: (j, i, 0)) if column_chunks
                   else pl.BlockSpec((tm, tn), lambda j, i, kk: (i, j))),
        out_shape=jax.ShapeDtypeStruct((N_DEV, m, chunk) if column_chunks else (m, n), out_dtype),
        scratch_shapes=[pltpu.VMEM((tm, tn), F32)] if nk > 1 else [],
        compiler_params=_params(("parallel", "parallel", "arbitrary")),
    )(a, b, *extra)


def _mm_tn3(a_list, b_list, name):
    s, m = a_list[0].shape
    n = b_list[0].shape[1]
    tk = min(2048, s)
    nk = s // tk

    def body(*refs):
        a_refs, b_refs, o_refs, acc_refs = refs[0:3], refs[3:6], refs[6:9], refs[9:12]
        kk = pl.program_id(0)
        for a_ref, b_ref, o_ref, acc_ref in zip(a_refs, b_refs, o_refs, acc_refs):
            prod = _dot(a_ref[...], b_ref[...], TN)
            if nk == 1:
                o_ref[...] = prod.astype(o_ref.dtype)
                continue

            @pl.when(kk == 0)
            def _(acc_ref=acc_ref, prod=prod):
                acc_ref[...] = prod

            @pl.when(kk > 0)
            def _(acc_ref=acc_ref, prod=prod):
                acc_ref[...] += prod

            @pl.when(kk == nk - 1)
            def _(acc_ref=acc_ref, o_ref=o_ref):
                o_ref[...] = acc_ref[...].astype(o_ref.dtype)

    return pl.pallas_call(
        body, name=name, grid=(nk,),
        in_specs=[pl.BlockSpec((tk, m), lambda kk: (kk, 0))] * 3 + [pl.BlockSpec((tk, n), lambda kk: (kk, 0))] * 3,
        out_specs=[_full((m, n))] * 3,
        out_shape=[jax.ShapeDtypeStruct((m, n), BF16)] * 3,
        scratch_shapes=[pltpu.VMEM((m, n), F32)] * 3,
        compiler_params=_params(("arbitrary",)),
    )(*a_list, *b_list)


def _rms_fwd(x, g, name, deps=()):
    s, d = x.shape
    tm = min(512, s)

    def body(x_ref, g_ref, *rest):
        h_ref = rest[len(deps)]
        xv = x_ref[...]
        r = lax.rsqrt(jnp.mean(xv * xv, axis=-1, keepdims=True) + EPS)
        h_ref[...] = (xv * r * g_ref[...]).astype(BF16)

    return pl.pallas_call(
        body, name=name, grid=(s // tm,),
        in_specs=[pl.BlockSpec((tm, d), lambda i: (i, 0)), _full((1, d))] + [pl.BlockSpec(memory_space=pl.ANY)] * len(deps),
        out_specs=pl.BlockSpec((tm, d), lambda i: (i, 0)),
        out_shape=jax.ShapeDtypeStruct((s, d), BF16),
        compiler_params=_params(("parallel",)),
    )(x, g, *deps)


def _ring_rows(hbm, ring, sems, col0):
    i = pl.program_id(0)
    steps = pl.num_programs(0)
    _, tm, width = ring.shape

    def fetch(step):
        slot = step % 3
        return pltpu.make_async_copy(hbm.at[pl.ds(step * tm, tm), pl.ds(col0, width)], ring.at[slot], sems.at[slot])

    @pl.when(i == 0)
    def _():
        fetch(0).start()

    @pl.when((i == 0) & (steps > 1))
    def _():
        fetch(1).start()

    @pl.when(i + 2 < steps)
    def _():
        fetch(i + 2).start()

    fetch(i).wait()
    return ring.at[i % 3]


def _ring_scratch(tm, width):
    return [pltpu.VMEM((3, tm, width), BF16), pltpu.SemaphoreType.DMA((3,))]


def _proj_post(proj, gq_fox, gk_fox, gq_mem, gq_swa, gk_swa):
    s = proj.shape[0]
    tm = min(512, s)

    def body(p_hbm, gqf, gkf, gqm, gqa, gka, qf_ref, kf_ref, vf_ref, qm_ref, qa_ref, ka_ref, va_ref, qft_ref, vft_ref,
             ring, sems):
        p_ref = _ring_rows(p_hbm, ring, sems, HALF_W)

        def norm(off, width, hd, g_ref, o_ref, scaled_t_ref=None):
            for b in range(width // 128):
                v = p_ref[:, off + b * 128: off + (b + 1) * 128].astype(F32)
                r = lax.rsqrt(_group_mean(v * v, hd) + EPS)
                vn = (v * r * g_ref[...]).astype(BF16)
                o_ref[:, b * 128:(b + 1) * 128] = vn
                if scaled_t_ref is not None:
                    scaled_t_ref[b * 128:(b + 1) * 128, :] = (vn.astype(F32) * 0.125).T.astype(BF16)

        norm(H_QF, 512, HEAD, gqf, qf_ref, qft_ref)
        norm(H_KF, 512, HEAD, gkf, kf_ref)
        vf_ref[...] = p_ref[:, H_VF:H_VF + 512].astype(BF16)
        for b in range(4):
            vft_ref[b * 128:(b + 1) * 128, :] = p_ref[:, H_VF + b * 128:H_VF + (b + 1) * 128].astype(F32).T.astype(BF16)
        norm(H_QM, 512, MEM_HEAD, gqm, qm_ref)
        norm(H_QA, 512, HEAD, gqa, qa_ref)
        norm(H_KA, 128, HEAD, gka, ka_ref)
        va_ref[...] = p_ref[:, H_VA:H_VA + 128].astype(BF16)

    g_spec = _full((1, 128))
    o512 = pl.BlockSpec((tm, 512), lambda i: (i, 0))
    o128 = pl.BlockSpec((tm, 128), lambda i: (i, 0))
    s512 = jax.ShapeDtypeStruct((s, 512), BF16)
    s128 = jax.ShapeDtypeStruct((s, 128), BF16)
    return pl.pallas_call(
        body, name="proj_post", grid=(s // tm,),
        in_specs=[pl.BlockSpec(memory_space=pl.ANY), g_spec, g_spec, g_spec, g_spec, g_spec],
        out_specs=[o512, o512, o512, o512, o512, o128, o128] + [pl.BlockSpec((512, tm), lambda i: (0, i))] * 2,
        out_shape=[s512, s512, s512, s512, s512, s128, s128] + [jax.ShapeDtypeStruct((512, s), BF16)] * 2,
        scratch_shapes=_ring_scratch(tm, HALF_W),
        compiler_params=_params(("arbitrary",)),
    )(proj, gq_fox, gk_fox, gq_mem, gq_swa, gk_swa)


def _tri(n, lower):
    r = lax.broadcasted_iota(jnp.int32, (n, n), 0)
    c = lax.broadcasted_iota(jnp.int32, (n, n), 1)
    return jnp.where((c <= r) if lower else (c >= r), 1.0, 0.0).astype(F32)


def _fox_gate_fwd(proj, b_forget128):
    s = proj.shape[0]
    tm = min(512, s)

    def body(p_ref, b_ref, cc_ref, ca_ref, carry_ref):
        i = pl.program_id(0)

        @pl.when(i == 0)
        def _():
            carry_ref[...] = jnp.zeros_like(carry_ref)

        z = p_ref[...] + b_ref[...]
        logf = jnp.minimum(z, 0.0) - jnp.log(1.0 + jnp.exp(-jnp.abs(z)))
        c = jnp.dot(_tri(tm, True), logf, precision=lax.Precision.HIGHEST, preferred_element_type=F32) + carry_ref[...]
        carry_ref[...] = c[tm - 1:tm, :]
        lane = lax.broadcasted_iota(jnp.int32, (tm, 128), 1)
        for hp in range(4):
            cc_ref[hp] = c if hp == 0 else pltpu.roll(c, 128 - 2 * hp, 1)
            aug = jnp.zeros((tm, 128), F32)
            for e in range(2):
                rest = jnp.broadcast_to(c[:, 2 * hp + e:2 * hp + e + 1], (tm, 128))
                for part in range(3):
                    piece = rest.astype(BF16).astype(F32)
                    aug = jnp.where(lane == HEAD * (1 - e) + part, piece, aug)
                    rest = rest - piece
            ca_ref[hp] = aug.astype(BF16)

    return pl.pallas_call(
        body, name="fox_gate_fwd", grid=(s // tm,),
        in_specs=[pl.BlockSpec((tm, 128), lambda i: (i, 0)), _full((1, 128))],
        out_specs=[pl.BlockSpec((4, tm, 128), lambda i: (0, i, 0))] * 2,
        out_shape=[jax.ShapeDtypeStruct((4, s, 128), F32), jax.ShapeDtypeStruct((4, s, 128), BF16)],
        scratch_shapes=[pltpu.VMEM((1, 128), F32)],
        compiler_params=_params(("arbitrary",)),
    )(proj, b_forget128)


def _memkv_fwd(mem, g_mem, w_kv, kn_mem):
    m = mem.shape[0]

    def body(mem_ref, g_ref, w_ref, kn_ref, memn_ref, kv_ref, mk_ref, mv_ref):
        xv = mem_ref[...]
        r = lax.rsqrt(jnp.mean(xv * xv, axis=-1, keepdims=True) + EPS)
        mn = (xv * r * g_ref[...]).astype(BF16)
        memn_ref[...] = mn
        kv = _dot(mn, w_ref[...])
        kv_ref[...] = kv
        for h in range(MEM_HEADS):
            v = kv[:, h * 128:(h + 1) * 128]
            rr = lax.rsqrt(jnp.mean(v * v, axis=-1, keepdims=True) + EPS)
            mk_ref[:, h * 128:(h + 1) * 128] = (v * rr * kn_ref[...]).astype(BF16)
        mv_ref[...] = kv[:, 512:1024].astype(BF16)

    return pl.pallas_call(
        body, name="memkv_fwd",
        out_shape=[jax.ShapeDtypeStruct((m, D_MODEL), BF16), jax.ShapeDtypeStruct((m, 1024), F32),
                   jax.ShapeDtypeStruct((m, 512), BF16), jax.ShapeDtypeStruct((m, 512), BF16)],
        compiler_params=pltpu.CompilerParams(vmem_limit_bytes=VMEM_LIMIT),
    )(mem, g_mem, w_kv, kn_mem)


def _bias_table(rel_bias, bucket):
    def body(rb_ref, bk_ref, o_ref):
        bk = bk_ref[...]
        for h in range(SWA_HEADS):
            acc = jnp.zeros(bk.shape, F32)
            for b in range(REL_BUCKETS):
                acc = jnp.where(bk == b, rb_ref[b, h], acc)
            o_ref[h] = acc

    return pl.pallas_call(
        body, name="bias_table",
        in_specs=[pl.BlockSpec(memory_space=pltpu.SMEM), pl.BlockSpec(memory_space=pltpu.VMEM)],
        out_shape=jax.ShapeDtypeStruct((SWA_HEADS,) + bucket.shape, F32),
    )(rel_bias, bucket)


def _swa_valid(n):
    row = lax.broadcasted_iota(jnp.int32, (SWA_BLOCK, 2 * SWA_BLOCK), 0)
    col = lax.broadcasted_iota(jnp.int32, (SWA_BLOCK, 2 * SWA_BLOCK), 1)
    dist = row + SWA_BLOCK - col
    return (dist >= 0) & (dist < SWA_BLOCK) & ((col >= SWA_BLOCK) | (n > 0))


def _swa_fwd(qa, kp, vp, bias, sink):
    s = qa.shape[0]
    nb = s // SWA_BLOCK

    def body(sink_ref, q_ref, kp_ref, vp_ref, bias_ref, o_ref):
        n = pl.program_id(0)
        start = pl.multiple_of(n * SWA_BLOCK, SWA_BLOCK)
        k2 = kp_ref[pl.ds(start, 2 * SWA_BLOCK), :]
        v2 = vp_ref[pl.ds(start, 2 * SWA_BLOCK), :]
        valid = _swa_valid(n)
        heads = range(SWA_HEADS)
        hs = lambda h: slice(h * HEAD, (h + 1) * HEAD)
        sc = [jnp.where(valid, _dot(q_ref[:, hs(h)], k2[:, hs(h // 4)], NT) * 0.125 + bias_ref[h], NEG) for h in heads]
        pn = []
        for h in heads:
            sk = sink_ref[h]
            mx = jnp.maximum(jnp.max(sc[h], axis=-1, keepdims=True), sk)
            p = jnp.exp(sc[h] - mx)
            den = jnp.sum(p, axis=-1, keepdims=True) + jnp.exp(sk - mx)
            pn.append((p / den).astype(BF16))
        outs = [_dot(pn[h], v2[:, hs(h // 4)]).astype(BF16) for h in heads]
        for h in heads:
            o_ref[:, hs(h)] = outs[h]

    return pl.pallas_call(
        body, name="swa_fwd", grid=(nb,),
        in_specs=[pl.BlockSpec(memory_space=pltpu.SMEM),
                  pl.BlockSpec((SWA_BLOCK, 512), lambda n: (n, 0)),
                  _full(kp.shape), _full(vp.shape), _full(bias.shape)],
        out_specs=pl.BlockSpec((SWA_BLOCK, 512), lambda n: (n, 0)),
        out_shape=jax.ShapeDtypeStruct((s, 512), BF16),
        compiler_params=_params(("parallel",)),
    )(sink, qa, kp, vp, bias)


def _head_mask(e):
    lane = lax.broadcasted_iota(jnp.int32, (1, 128), 1)
    return (lane >= e * HEAD) & (lane < (e + 1) * HEAD)


FOX_FWD_T = 1024
FOX_BWD_T = 512


def _head_rows(e):
    row = lax.broadcasted_iota(jnp.int32, (128, 1), 0)
    return (row >= e * HEAD) & (row < (e + 1) * HEAD)


def _fox_fwd(q, k, v_t, ca4):
    s = q.shape[0]
    t = min(FOX_FWD_T, s)
    nq = s // t

    def body(q_ref, k_ref, vt_ref, ca_ref, o_ref, lse_ref, ot_ref):
        i = pl.program_id(1)
        qs = q_ref[...] * jnp.asarray(0.125, BF16)
        lane = lax.broadcasted_iota(jnp.int32, (1, 128), 1)
        minus = [jnp.where((lane >= HEAD * (1 - e)) & (lane < HEAD * (1 - e) + 3), -1.0, 0.0).astype(BF16) for e in range(2)]
        qe = [jnp.where(_head_mask(e), qs, jnp.broadcast_to(minus[e], qs.shape)) for e in range(2)]

        def block(carry, key0, nkeys, q0, nqs, masked):
            ks = pl.ds(pl.multiple_of(key0, 128), nkeys)
            kj = k_ref[ks, :]
            caj = ca_ref[0, ks, :]
            vtj = vt_ref[:, ks]
            out = []
            for e in range(2):
                m_all, acc_all = carry[2 * e], carry[2 * e + 1]
                m, acc = m_all[:, q0:q0 + nqs], acc_all[:, q0:q0 + nqs]
                st = _dot(jnp.where(_head_mask(e), kj, caj), qe[e][q0:q0 + nqs, :], NT)
                if masked:
                    krow = lax.broadcasted_iota(jnp.int32, (nkeys, nqs), 0) + key0
                    qcol = lax.broadcasted_iota(jnp.int32, (nkeys, nqs), 1) + (i * t + q0)
                    st = jnp.where(krow <= qcol, st, NEG)
                m_new = jnp.maximum(m, jnp.max(st, axis=0, keepdims=True))
                alpha = jnp.exp(m - m_new)
                pt = jnp.exp(st - m_new).astype(BF16)
                vte = jnp.where(_head_rows(e), vtj, jnp.ones_like(vtj))
                acc_new = alpha * acc + _dot(vte, pt)
                if nqs < t:
                    m_new = jnp.concatenate([m_all[:, :q0], m_new], axis=1)
                    acc_new = jnp.concatenate([acc_all[:, :q0], acc_new], axis=1)
                out += [m_new, acc_new]
            return tuple(out)

        half = t // 2
        init = (jnp.full((1, t), NEG, F32), jnp.zeros((128, t), F32)) * 2
        carry = lax.fori_loop(0, i, lambda j, c: block(c, j * t, t, 0, t, False), init)
        carry = block(carry, i * t, half, 0, t, True)
        m0, a0, m1, a1 = block(carry, i * t + half, half, half, half, True)
        l0 = a0[HEAD:HEAD + 1, :]
        l1 = a1[0:1, :]
        o_t = jnp.where(_head_rows(0), a0 / l0, a1 / l1)
        o_ref[...] = o_t.T.astype(BF16)
        ot_ref[...] = o_t.astype(BF16)
        r8 = lax.broadcasted_iota(jnp.int32, (8, t), 0)
        lse_ref[0] = jnp.where(r8 == 0, m0 + jnp.log(l0), jnp.where(r8 == 1, m1 + jnp.log(l1), 0.0))

    return pl.pallas_call(
        body, name="fox_fwd", grid=(4, nq),
        in_specs=[pl.BlockSpec((t, 128), lambda hp, i: (i, hp)),
                  pl.BlockSpec((s, 128), lambda hp, i: (0, hp)),
                  pl.BlockSpec((128, s), lambda hp, i: (hp, 0)),
                  pl.BlockSpec((1, s, 128), lambda hp, i: (hp, 0, 0))],
        out_specs=[pl.BlockSpec((t, 128), lambda hp, i: (i, hp)),
                   pl.BlockSpec((1, 8, t), lambda hp, i: (hp, 0, i)),
                   pl.BlockSpec((128, t), lambda hp, i: (hp, i))],
        out_shape=[jax.ShapeDtypeStruct((s, 512), BF16), jax.ShapeDtypeStruct((4, 8, s), F32),
                   jax.ShapeDtypeStruct((512, s), BF16)],
        compiler_params=_params(("parallel", "parallel")),
    )(q, k, v_t, ca4)


MEM_SCALE = MEM_HEAD ** -0.5


def _mem_fwd(qm, mk, mv):
    s = qm.shape[0]
    tq = min(512, s)

    def body(q_ref, mk_ref, mv_ref, o_ref):
        for h in range(MEM_HEADS):
            hs = slice(h * 128, (h + 1) * 128)
            sc = _dot(q_ref[:, hs], mk_ref[:, hs], NT) * MEM_SCALE
            mx = jnp.max(sc, axis=-1, keepdims=True)
            p = jnp.exp(sc - mx)
            p = p / jnp.sum(p, axis=-1, keepdims=True)
            o_ref[:, hs] = _dot(p.astype(BF16), mv_ref[:, hs]).astype(BF16)

    return pl.pallas_call(
        body, name="mem_fwd", grid=(s // tq,),
        in_specs=[pl.BlockSpec((tq, 512), lambda i: (i, 0)), _full(mk.shape), _full(mv.shape)],
        out_specs=pl.BlockSpec((tq, 512), lambda i: (i, 0)),
        out_shape=jax.ShapeDtypeStruct((s, 512), BF16),
        compiler_params=_params(("parallel",)),
    )(qm, mk, mv)


def _merge_fwd(x, oa, of, om, proj, b_gate, wa, wf, wm, w_out, g_mlp):
    s = x.shape[0]
    tm = min(512, s)

    def body(x_ref, oa_ref, of_ref, om_ref, gl_hbm, bg_ref, wa_ref, wf_ref, wm_ref, wo_ref, g_ref, x1_ref, hm_ref, mg_ref,
             ring, sems):
        gl_ref = _ring_rows(gl_hbm, ring, sems, 0)
        merged = None
        for b, (o_ref, w_ref) in enumerate(((oa_ref, wa_ref), (of_ref, wf_ref), (om_ref, wm_ref))):
            cs = slice(b * D_MODEL, (b + 1) * D_MODEL)
            y = _dot(o_ref[...], w_ref[...])
            t = _sigmoid(gl_ref[:, cs].astype(F32) + bg_ref[:, cs]) * y
            merged = t if merged is None else merged + t
        mb = merged.astype(BF16)
        mg_ref[...] = mb
        x1 = x_ref[...] + _dot(mb, wo_ref[...])
        x1_ref[...] = x1
        r = lax.rsqrt(jnp.mean(x1 * x1, axis=-1, keepdims=True) + EPS)
        hm_ref[...] = (x1 * r * g_ref[...]).astype(BF16)

    row = lambda w: pl.BlockSpec((tm, w), lambda i: (i, 0))
    return pl.pallas_call(
        body, name="merge_fwd", grid=(s // tm,),
        in_specs=[row(D_MODEL), row(512), row(512), row(512), pl.BlockSpec(memory_space=pl.ANY), _full((1, HALF_W)),
                  _full(wa.shape), _full(wf.shape), _full(wm.shape), _full(w_out.shape), _full((1, D_MODEL))],
        out_specs=[row(D_MODEL), row(D_MODEL), row(D_MODEL)],
        out_shape=[jax.ShapeDtypeStruct((s, D_MODEL), F32), jax.ShapeDtypeStruct((s, D_MODEL), BF16),
                   jax.ShapeDtypeStruct((s, D_MODEL), BF16)],
        scratch_shapes=_ring_scratch(tm, HALF_W),
        compiler_params=_params(("arbitrary",)),
    )(x, oa, of, om, proj, b_gate, wa, wf, wm, w_out, g_mlp)


def _mlp_up(hm, w_up):
    s = hm.shape[0]
    tm, tn = min(1024, s), w_up.shape[2]

    def body(h_ref, w_ref, u_ref):
        r = jnp.maximum(_dot(h_ref[...], w_ref[0]), 0.0)
        u_ref[...] = (r * r).astype(BF16)

    return pl.pallas_call(
        body, name="mlp_up", grid=(s // tm, D_FF // tn),
        in_specs=[pl.BlockSpec((tm, D_MODEL), lambda i, j: (i, 0)), pl.BlockSpec((1, D_MODEL, tn), lambda i, j: (j, 0, 0))],
        out_specs=pl.BlockSpec((tm, tn), lambda i, j: (i, j)),
        out_shape=jax.ShapeDtypeStruct((s, D_FF), BF16),
        compiler_params=_params(("parallel", "parallel")),
    )(hm, w_up)


def _mlp_down_loss(u, w_down, x1, target):
    s = u.shape[0]
    tm = min(256, s)

    def body(u_ref, w_ref, x1_ref, t_ref, dy_ref, dyb_ref, loss_ref):
        i = pl.program_id(0)

        @pl.when(i == 0)
        def _():
            loss_ref[...] = jnp.zeros_like(loss_ref)

        y = x1_ref[...] + _dot(u_ref[...], w_ref[...])
        err = y - t_ref[...]
        dy = err * (1.0 / D_MODEL)
        dy_ref[...] = dy
        dyb_ref[...] = dy.astype(BF16)
        part = jnp.sum(jnp.sum(err * err, axis=-1, keepdims=True) * (1.0 / D_MODEL), axis=0, keepdims=True)
        loss_ref[...] += 0.5 * part

    row = pl.BlockSpec((tm, D_MODEL), lambda i: (i, 0))
    return pl.pallas_call(
        body, name="mlp_down_loss", grid=(s // tm,),
        in_specs=[pl.BlockSpec((tm, D_FF), lambda i: (i, 0)), _full(w_down.shape), row, row],
        out_specs=[row, row, _full((1, 1))],
        out_shape=[jax.ShapeDtypeStruct((s, D_MODEL), F32), jax.ShapeDtypeStruct((s, D_MODEL), BF16),
                   jax.ShapeDtypeStruct((1, 1), F32)],
        compiler_params=_params(("arbitrary",)),
    )(u, w_down, x1, target)


def _mlp_bwd_act(dy, w_down, u):
    s = dy.shape[0]
    tm, tn = min(1024, s), 1024

    def body(dy_ref, w_ref, u_ref, da_ref):
        du = _dot(dy_ref[...], w_ref[...], NT)
        da_ref[...] = (du * (2.0 * jnp.sqrt(u_ref[...].astype(F32)))).astype(BF16)

    return pl.pallas_call(
        body, name="mlp_bwd_act", grid=(D_FF // tn, s // tm),
        in_specs=[pl.BlockSpec((tm, D_MODEL), lambda j, i: (i, 0)), pl.BlockSpec((tn, D_MODEL), lambda j, i: (j, 0)),
                  pl.BlockSpec((tm, tn), lambda j, i: (i, j))],
        out_specs=pl.BlockSpec((tm, tn), lambda j, i: (i, j)),
        out_shape=jax.ShapeDtypeStruct((s, D_FF), BF16),
        compiler_params=_params(("parallel", "parallel")),
    )(dy, w_down, u)


def _rms_bwd(xv, g, dh, skip):
    r = lax.rsqrt(jnp.mean(xv * xv, axis=-1, keepdims=True) + EPS)
    n = xv * r
    dn = dh * g
    dx = skip + r * (dn - n * jnp.mean(dn * n, axis=-1, keepdims=True))
    return dx, jnp.sum(dh * n, axis=0, keepdims=True)


def _mlp_bwd_x(da, w_up, x1, dy, g_mlp):
    s = da.shape[0]
    tm = min(256, s)

    def body(da_ref, w_ref, x1_ref, dy_ref, g_ref, dx1_ref, dg_ref):
        i = pl.program_id(0)

        @pl.when(i == 0)
        def _():
            dg_ref[...] = jnp.zeros_like(dg_ref)

        tn = w_ref.shape[2]
        dhm = _dot(da_ref[:, 0:tn], w_ref[0], NT)
        for j in range(1, N_DEV):
            dhm = dhm + _dot(da_ref[:, j * tn:(j + 1) * tn], w_ref[j], NT)
        dx, dg = _rms_bwd(x1_ref[...], g_ref[...], dhm, dy_ref[...])
        dx1_ref[...] = dx
        dg_ref[...] += dg

    row = pl.BlockSpec((tm, D_MODEL), lambda i: (i, 0))
    return pl.pallas_call(
        body, name="mlp_bwd_x", grid=(s // tm,),
        in_specs=[pl.BlockSpec((tm, D_FF), lambda i: (i, 0)), _full(w_up.shape), row, row, _full((1, D_MODEL))],
        out_specs=[row, _full((1, D_MODEL))],
        out_shape=[jax.ShapeDtypeStruct((s, D_MODEL), F32), jax.ShapeDtypeStruct((1, D_MODEL), F32)],
        compiler_params=_params(("arbitrary",)),
    )(da, w_up, x1, dy, g_mlp)


def _merge_bwd(dx1, oa, of, om, proj, b_gate, wa, wf, wm, w_out):
    s = dx1.shape[0]
    tm = min(512, s)

    def body(dx1_ref, oa_ref, of_ref, om_ref, gl_hbm, bg_ref, wa_ref, wf_ref, wm_ref, wo_ref,
             dp_ref, doa_ref, dof_ref, dom_ref, dya_ref, dyf_ref, dym_ref, dbg_ref, ring, sems):
        i = pl.program_id(0)
        gl_ref = _ring_rows(gl_hbm, ring, sems, 0)

        @pl.when(i == 0)
        def _():
            dbg_ref[...] = jnp.zeros_like(dbg_ref)

        dmerged = _dot(dx1_ref[...].astype(BF16), wo_ref[...], NT)
        branches = ((oa_ref, wa_ref, doa_ref, dya_ref), (of_ref, wf_ref, dof_ref, dyf_ref), (om_ref, wm_ref, dom_ref, dym_ref))
        for b, (o_ref, w_ref, do_ref, dyb_ref) in enumerate(branches):
            cs = slice(b * D_MODEL, (b + 1) * D_MODEL)
            y = _dot(o_ref[...], w_ref[...])
            g = _sigmoid(gl_ref[:, cs].astype(F32) + bg_ref[:, cs])
            dz = (dmerged * y) * g * (1.0 - g)
            dp_ref[:, cs] = dz.astype(BF16)
            dbg_ref[:, cs] += jnp.sum(dz, axis=0, keepdims=True)
            dyb = (dmerged * g).astype(BF16)
            dyb_ref[...] = dyb
            do = _dot(dyb, w_ref[...], NT)
            do_ref[...] = (do.T if b == 1 else do).astype(BF16)

    row = lambda w: pl.BlockSpec((tm, w), lambda i: (i, 0))
    sd = lambda w: jax.ShapeDtypeStruct((s, w), BF16)
    return pl.pallas_call(
        body, name="merge_bwd", grid=(s // tm,),
        in_specs=[row(D_MODEL), row(512), row(512), row(512), pl.BlockSpec(memory_space=pl.ANY), _full((1, HALF_W)),
                  _full(wa.shape), _full(wf.shape), _full(wm.shape), _full(w_out.shape)],
        out_specs=[row(HALF_W), row(512), pl.BlockSpec((512, tm), lambda i: (0, i)), row(512),
                   row(D_MODEL), row(D_MODEL), row(D_MODEL), _full((1, HALF_W))],
        out_shape=[sd(PROJ_W), sd(512), jax.ShapeDtypeStruct((512, s), BF16), sd(512), sd(D_MODEL), sd(D_MODEL), sd(D_MODEL),
                   jax.ShapeDtypeStruct((1, HALF_W), F32)],
        scratch_shapes=_ring_scratch(tm, HALF_W),
        compiler_params=_params(("arbitrary",)),
    )(dx1, oa, of, om, proj, b_gate, wa, wf, wm, w_out)


def _swa_valid_t(n):
    key = lax.broadcasted_iota(jnp.int32, (2 * SWA_BLOCK, SWA_BLOCK), 0)
    qry = lax.broadcasted_iota(jnp.int32, (2 * SWA_BLOCK, SWA_BLOCK), 1)
    dist = qry + SWA_BLOCK - key
    return (dist >= 0) & (dist < SWA_BLOCK) & ((key >= SWA_BLOCK) | (n > 0))


def _swa_bwd(qa, kp, vp, bias_t, sink, doa):
    s = qa.shape[0]
    nb = s // SWA_BLOCK

    def body(sink_ref, q_ref, kp_ref, vp_ref, bias_ref, do_ref, dq_ref, dkp_ref, dvp_ref, dbias_ref, dsink_ref, sk_acc):
        n = pl.program_id(0)

        @pl.when(n == 0)
        def _():
            dkp_ref[...] = jnp.zeros_like(dkp_ref)
            dvp_ref[...] = jnp.zeros_like(dvp_ref)
            dbias_ref[...] = jnp.zeros_like(dbias_ref)
            sk_acc[...] = jnp.zeros_like(sk_acc)

        start = pl.multiple_of(n * SWA_BLOCK, SWA_BLOCK)
        win = pl.ds(start, 2 * SWA_BLOCK)
        k2 = kp_ref[win, :]
        v2 = vp_ref[win, :]
        valid = _swa_valid_t(n)
        heads = range(SWA_HEADS)
        hs = lambda h: slice(h * HEAD, (h + 1) * HEAD)
        scale = jnp.asarray(0.125, BF16)
        q = [q_ref[:, hs(h)] for h in heads]
        do = [do_ref[:, hs(h)] for h in heads]
        kk = [k2[:, hs(kv)] for kv in range(2)]
        vv = [v2[:, hs(kv)] for kv in range(2)]
        kt = [(kk[kv].astype(F32) * 0.125).T.astype(BF16) for kv in range(2)]
        st = [jnp.where(valid, _dot(kk[h // 4], q[h], NT) * 0.125 + bias_ref[h], NEG) for h in heads]
        dpt = [_dot(vv[h // 4], do[h], NT) for h in heads]
        pt, dst = [], []
        for h in heads:
            sk = sink_ref[h]
            mx = jnp.maximum(jnp.max(st[h], axis=0, keepdims=True), sk)
            p = jnp.exp(st[h] - mx)
            esk = jnp.exp(sk - mx)
            den = jnp.sum(p, axis=0, keepdims=True) + esk
            p = p / den
            delta = jnp.sum(p * dpt[h], axis=0, keepdims=True)
            d = p * (dpt[h] - delta)
            sk_acc[h:h + 1, :] += -(esk / den) * delta
            dbias_ref[h] += d
            pt.append(p.astype(BF16))
            dst.append(d.astype(BF16))
        dq_t = [_dot(kt[h // 4], dst[h]) for h in heads]
        dq_ref[...] = jnp.concatenate(dq_t, axis=0).T.astype(BF16)
        for kv in range(2):
            group = range(4 * kv, 4 * kv + 4)
            dk = [_dot(dst[h], q[h] * scale) for h in group]
            dv = [_dot(pt[h], do[h]) for h in group]
            dkp_ref[win, hs(kv)] += (dk[0] + dk[1]) + (dk[2] + dk[3])
            dvp_ref[win, hs(kv)] += (dv[0] + dv[1]) + (dv[2] + dv[3])

        @pl.when(n == nb - 1)
        def _():
            dsink_ref[...] = jnp.broadcast_to(jnp.sum(sk_acc[...], axis=1, keepdims=True), dsink_ref.shape)

    return pl.pallas_call(
        body, name="swa_bwd", grid=(nb,),
        in_specs=[pl.BlockSpec(memory_space=pltpu.SMEM),
                  pl.BlockSpec((SWA_BLOCK, 512), lambda n: (n, 0)),
                  _full(kp.shape), _full(vp.shape), _full(bias_t.shape),
                  pl.BlockSpec((SWA_BLOCK, 512), lambda n: (n, 0))],
        out_specs=[pl.BlockSpec((SWA_BLOCK, 512), lambda n: (n, 0)), _full(kp.shape), _full(vp.shape),
                   _full(bias_t.shape), _full((SWA_HEADS, 128))],
        out_shape=[jax.ShapeDtypeStruct((s, 512), BF16), jax.ShapeDtypeStruct(kp.shape, F32),
                   jax.ShapeDtypeStruct(vp.shape, F32), jax.ShapeDtypeStruct(bias_t.shape, F32),
                   jax.ShapeDtypeStruct((SWA_HEADS, 128), F32)],
        scratch_shapes=[pltpu.VMEM((SWA_HEADS, 128), F32)],
        compiler_params=_params(("arbitrary",)),
    )(sink, qa, kp, vp, bias_t, doa)


def _fox_bwd(qt, k, v, dot, ot, cc4, lse4):
    s = k.shape[0]
    t = min(FOX_BWD_T, s)
    nq = s // t

    def body(qt_ref, k_ref, v_ref, dot_ref, ot_ref, cc_ref, lse_ref,
             dqt_ref, dk_ref, dv_ref, dck_ref, dcq_ref, delta_ref, dkt_acc, dvt_acc, ds0, ds1):
        j = pl.program_id(1)

        @pl.when(j == 0)
        def _():
            dqt_ref[...] = jnp.zeros_like(dqt_ref)
            dcq_ref[...] = jnp.zeros_like(dcq_ref)
            r8 = lax.broadcasted_iota(jnp.int32, (8, t), 0)

            def dl(i, c):
                cols = pl.ds(pl.multiple_of(i * t, t), t)
                pr = dot_ref[:, cols].astype(F32) * ot_ref[:, cols].astype(F32)
                d0 = jnp.sum(jnp.where(_head_rows(0), pr, 0.0), axis=0, keepdims=True)
                d1 = jnp.sum(jnp.where(_head_rows(1), pr, 0.0), axis=0, keepdims=True)
                delta_ref[:, cols] = jnp.where(r8 == 0, d0, jnp.where(r8 == 1, d1, 0.0))
                return c

            lax.fori_loop(0, nq, dl, 0)

        kj = k_ref[...]
        vj = v_ref[...]
        ks = pl.ds(pl.multiple_of(j * t, t), t)
        kt = (kj.astype(F32) * 0.125).T.astype(BF16)
        ke = [jnp.where(_head_mask(e), kj, jnp.zeros_like(kj)) for e in range(2)]
        ve = [jnp.where(_head_mask(e), vj, jnp.zeros_like(vj)) for e in range(2)]
        ck = [cc_ref[0, ks, e:e + 1] for e in range(2)]
        for r in (dkt_acc, dvt_acc, ds0, ds1):
            r[...] = jnp.zeros_like(r)

        def block(q0, nqs, k0, nks, masked):
            cols = pl.ds(pl.multiple_of(q0, 128), nqs)
            rows = slice(k0, k0 + nks)
            qti = qt_ref[:, cols]
            doti = dot_ref[:, cols]
            for e, ds_acc in enumerate((ds0, ds1)):
                dims = slice(e * HEAD, (e + 1) * HEAD)
                st = _dot(ke[e][rows, :], qti) - ck[e][rows, :]
                if masked:
                    krow = lax.broadcasted_iota(jnp.int32, (nks, nqs), 0) + (j * t + k0)
                    qcol = lax.broadcasted_iota(jnp.int32, (nks, nqs), 1) + q0
                    st = jnp.where(krow <= qcol, st, NEG)
                pt = jnp.exp(st - lse_ref[0, e:e + 1, cols])
                dpt = _dot(ve[e][rows, :], doti)
                dst = pt * (dpt - delta_ref[e:e + 1, cols])
                dsb = dst.astype(BF16)
                dvt_acc[dims, rows] += _dot(doti[dims, :], pt.astype(BF16), NT)
                dkt_acc[dims, rows] += _dot(qti[dims, :], dsb, NT)
                dqt_ref[dims, cols] += _dot(kt[dims, rows], dsb)
                ds_acc[rows, 0:nqs] += dst
                dcq_ref[0, e:e + 1, cols] += jnp.sum(dst, axis=0, keepdims=True)

        half = t // 2
        block(j * t, half, 0, half, True)
        block(j * t + half, half, 0, t, True)

        def rest(i, c):
            block(i * t, t, 0, t, False)
            return c

        lax.fori_loop(j + 1, nq, rest, 0)
        dk_ref[...] = dkt_acc[...].T.astype(BF16)
        dv_ref[...] = dvt_acc[...].T.astype(BF16)
        lane = lax.broadcasted_iota(jnp.int32, (t, 128), 1)
        c0 = jnp.sum(ds0[...], axis=-1, keepdims=True)
        c1 = jnp.sum(ds1[...], axis=-1, keepdims=True)
        dck_ref[0] = jnp.where(lane == 0, c0, jnp.where(lane == 1, c1, 0.0))

    res_t = lambda: pl.BlockSpec((128, s), lambda hp, j: (hp, 0))
    blk = lambda: pl.BlockSpec((t, 128), lambda hp, j: (j, hp))
    return pl.pallas_call(
        body, name="fox_bwd", grid=(4, nq),
        in_specs=[res_t(), blk(), blk(), res_t(), res_t(), pl.BlockSpec((1, s, 128), lambda hp, j: (hp, 0, 0)),
                  pl.BlockSpec((1, 8, s), lambda hp, j: (hp, 0, 0))],
        out_specs=[res_t(), blk(), blk(),
                   pl.BlockSpec((1, t, 128), lambda hp, j: (hp, j, 0)),
                   pl.BlockSpec((1, 8, s), lambda hp, j: (hp, 0, 0))],
        out_shape=[jax.ShapeDtypeStruct((512, s), F32), jax.ShapeDtypeStruct((s, 512), BF16),
                   jax.ShapeDtypeStruct((s, 512), BF16), jax.ShapeDtypeStruct((4, s, 128), F32),
                   jax.ShapeDtypeStruct((4, 8, s), F32)],
        scratch_shapes=[pltpu.VMEM((8, s), F32)] + [pltpu.VMEM((128, t), F32)] * 2 + [pltpu.VMEM((t, t), F32)] * 2,
        compiler_params=_params(("arbitrary", "arbitrary")),
    )(qt, k, v, dot, ot, cc4, lse4)


def _mem_bwd(qm, mk, mv, dom):
    s = qm.shape[0]
    tq = min(512, s)

    def body(q_ref, mk_ref, mv_ref, do_ref, dq_ref, dmk_ref, dmv_ref):
        i = pl.program_id(0)

        @pl.when(i == 0)
        def _():
            dmk_ref[...] = jnp.zeros_like(dmk_ref)
            dmv_ref[...] = jnp.zeros_like(dmv_ref)

        heads = range(MEM_HEADS)
        hs = lambda h: slice(h * 128, (h + 1) * 128)
        sc = [_dot(q_ref[:, hs(h)], mk_ref[:, hs(h)], NT) * MEM_SCALE for h in heads]
        dp = [_dot(do_ref[:, hs(h)], mv_ref[:, hs(h)], NT) for h in heads]
        pb, dsb = [], []
        for h in heads:
            p = jnp.exp(sc[h] - jnp.max(sc[h], axis=-1, keepdims=True))
            p = p / jnp.sum(p, axis=-1, keepdims=True)
            ds = p * (dp[h] - jnp.sum(p * dp[h], axis=-1, keepdims=True))
            pb.append(p.astype(BF16))
            dsb.append((ds * MEM_SCALE).astype(BF16))
        dq = [_dot(dsb[h], mk_ref[:, hs(h)]).astype(BF16) for h in heads]
        dmk = [_dot(dsb[h], q_ref[:, hs(h)], TN) for h in heads]
        dmv = [_dot(pb[h], do_ref[:, hs(h)], TN) for h in heads]
        for h in heads:
            dq_ref[:, hs(h)] = dq[h]
            dmk_ref[:, hs(h)] += dmk[h]
            dmv_ref[:, hs(h)] += dmv[h]

    return pl.pallas_call(
        body, name="mem_bwd", grid=(s // tq,),
        in_specs=[pl.BlockSpec((tq, 512), lambda i: (i, 0)), _full(mk.shape), _full(mv.shape),
                  pl.BlockSpec((tq, 512), lambda i: (i, 0))],
        out_specs=[pl.BlockSpec((tq, 512), lambda i: (i, 0)), _full(mk.shape), _full(mv.shape)],
        out_shape=[jax.ShapeDtypeStruct((s, 512), BF16), jax.ShapeDtypeStruct(mk.shape, F32),
                   jax.ShapeDtypeStruct(mv.shape, F32)],
        compiler_params=_params(("arbitrary",)),
    )(qm, mk, mv, dom)


def _memkv_bwd(dmk, dmv, kv_raw, kn_mem, mem, g_mem, mem_n, w_kv):
    def body(dmk_ref, dmv_ref, kv_ref, kn_ref, mem_ref, g_ref, mn_ref, w_ref, dw_ref, dkn_ref, dg_ref, dkv_ref):
        dkn = jnp.zeros((1, 128), F32)
        for h in range(MEM_HEADS):
            hs = slice(h * 128, (h + 1) * 128)
            v = kv_ref[:, hs]
            r = lax.rsqrt(jnp.mean(v * v, axis=-1, keepdims=True) + EPS)
            n = v * r
            dn = dmk_ref[:, hs]
            dkn = dkn + jnp.sum(dn * n, axis=0, keepdims=True)
            dng = dn * kn_ref[...]
            dkv_ref[:, hs] = (r * (dng - n * jnp.mean(dng * n, axis=-1, keepdims=True))).astype(BF16)
        dkv_ref[:, 512:1024] = dmv_ref[...].astype(BF16)
        dkn_ref[...] = dkn
        dkv = dkv_ref[...]
        dw_ref[...] = _dot(mn_ref[...], dkv, TN).astype(BF16)
        dmn = _dot(dkv, w_ref[...], NT)
        xv = mem_ref[...]
        r = lax.rsqrt(jnp.mean(xv * xv, axis=-1, keepdims=True) + EPS)
        dg_ref[...] = jnp.sum(dmn * (xv * r), axis=0, keepdims=True)

    m = mem.shape[0]
    return pl.pallas_call(
        body, name="memkv_bwd",
        out_shape=[jax.ShapeDtypeStruct((D_MODEL, 1024), BF16), jax.ShapeDtypeStruct((1, 128), F32),
                   jax.ShapeDtypeStruct((1, D_MODEL), F32)],
        scratch_shapes=[pltpu.VMEM((m, 1024), BF16)],
        compiler_params=pltpu.CompilerParams(vmem_limit_bytes=VMEM_LIMIT),
    )(dmk, dmv, kv_raw, kn_mem, mem, g_mem, mem_n, w_kv)


def _fox_gate_bwd(dcq4, dck4, proj, b_forget128):
    s = dck4.shape[1]
    tm = min(512, s)
    nt = s // tm

    def body(dcq_ref, dck_ref, p_ref, b_ref, dfl_ref, db_ref, carry_ref):
        i = pl.program_id(0)

        @pl.when(i == 0)
        def _():
            carry_ref[...] = jnp.zeros_like(carry_ref)
            db_ref[...] = jnp.zeros_like(db_ref)

        dcv = jnp.zeros((tm, 128), F32)
        for hp in range(4):
            by_query = jnp.concatenate([dcq_ref[hp], jnp.zeros((120, tm), F32)], axis=0).T
            d = by_query - dck_ref[hp]
            dcv = dcv + (d if hp == 0 else pltpu.roll(d, 2 * hp, 1))
        dlogf = jnp.dot(_tri(tm, False), dcv, precision=lax.Precision.HIGHEST, preferred_element_type=F32) + carry_ref[...]
        carry_ref[...] += jnp.sum(dcv, axis=0, keepdims=True)
        z = p_ref[...] + b_ref[...]
        dfl = dlogf * (1.0 / (1.0 + jnp.exp(z)))
        dfl_ref[...] = dfl.astype(BF16)
        db_ref[...] += jnp.sum(dfl, axis=0, keepdims=True)

    return pl.pallas_call(
        body, name="fox_gate_bwd", grid=(nt,),
        in_specs=[pl.BlockSpec((4, 8, tm), lambda i: (0, 0, nt - 1 - i)),
                  pl.BlockSpec((4, tm, 128), lambda i: (0, nt - 1 - i, 0)),
                  pl.BlockSpec((tm, 128), lambda i: (nt - 1 - i, 0)), _full((1, 128))],
        out_specs=[pl.BlockSpec((tm, 128), lambda i: (nt - 1 - i, 0)), _full((1, 128))],
        out_shape=[jax.ShapeDtypeStruct((s, 128), BF16), jax.ShapeDtypeStruct((1, 128), F32)],
        scratch_shapes=[pltpu.VMEM((1, 128), F32)],
        compiler_params=_params(("arbitrary",)),
    )(dcq4, dck4, proj, b_forget128)


def _proj_pre_bwd(dproj, proj, dqf, dkf, dvf, dqm, dqa, dka, dva, dfl, gq_fox, gk_fox, gq_mem, gq_swa, gk_swa):
    s = proj.shape[0]
    tm = min(512, s)

    def body(dp_in, p_ref, dqf_ref, dkf_ref, dvf_ref, dqm_ref, dqa_ref, dka_ref, dva_ref, dfl_ref,
             gqf, gkf, gqm, gqa, gka, dp_ref, dgn_ref):
        i = pl.program_id(0)

        @pl.when(i == 0)
        def _():
            dgn_ref[...] = jnp.zeros_like(dgn_ref)

        def norm_bwd(off, width, hd, g_ref, dn_ref, slot):
            acc = jnp.zeros((1, 128), F32)
            for b in range(width // 128):
                v = p_ref[:, off + b * 128: off + (b + 1) * 128].astype(F32)
                r = lax.rsqrt(_group_mean(v * v, hd) + EPS)
                n = v * r
                dn = dn_ref[b * 128:(b + 1) * 128, :].T if slot == 0 else dn_ref[:, b * 128:(b + 1) * 128].astype(F32)
                acc = acc + jnp.sum(dn * n, axis=0, keepdims=True)
                dng = dn * g_ref[...]
                dp_ref[:, off + b * 128: off + (b + 1) * 128] = (r * (dng - n * _group_mean(dng * n, hd))).astype(BF16)
            dgn_ref[slot:slot + 1, :] += acc

        norm_bwd(H_QF, 512, HEAD, gqf, dqf_ref, 0)
        norm_bwd(H_KF, 512, HEAD, gkf, dkf_ref, 1)
        dp_ref[:, H_VF:H_VF + 512] = dvf_ref[...].astype(BF16)
        norm_bwd(H_QM, 512, MEM_HEAD, gqm, dqm_ref, 2)
        norm_bwd(H_QA, 512, HEAD, gqa, dqa_ref, 3)
        norm_bwd(H_KA, 128, HEAD, gka, dka_ref, 4)
        dp_ref[:, H_VA:H_VA + 128] = dva_ref[...].astype(BF16)
        dp_ref[:, H_FL:H_FL + 128] = dfl_ref[...]
        dp_ref[:, H_FL + 128:HALF_W] = jnp.zeros((tm, HALF_W - H_FL - 128), BF16)

    row = lambda w: pl.BlockSpec((tm, w), lambda i: (i, 0))
    g_spec = _full((1, 128))
    return pl.pallas_call(
        body, name="proj_pre_bwd", grid=(s // tm,),
        in_specs=[pl.BlockSpec(memory_space=pl.ANY), pl.BlockSpec((tm, HALF_W), lambda i: (i, 1)),
                  pl.BlockSpec((512, tm), lambda i: (0, i)), row(512), row(512), row(512), row(512),
                  row(128), row(128), row(128), g_spec, g_spec, g_spec, g_spec, g_spec],
        out_specs=[pl.BlockSpec((tm, HALF_W), lambda i: (i, 1)), _full((8, 128))],
        out_shape=[jax.ShapeDtypeStruct((s, PROJ_W), BF16), jax.ShapeDtypeStruct((8, 128), F32)],
        input_output_aliases={0: 0},
        compiler_params=_params(("arbitrary",)),
    )(dproj, proj, dqf, dkf, dvf, dqm, dqa, dka, dva, dfl, gq_fox, gk_fox, gq_mem, gq_swa, gk_swa)


def _in_bwd_x(dproj, w_in_p, x, g_mix, dx1):
    s = x.shape[0]
    tm = min(256, s)

    def body(dp_ref, w_ref, x_ref, g_ref, dx1_ref, gx_ref, dg_ref):
        i = pl.program_id(0)

        @pl.when(i == 0)
        def _():
            dg_ref[...] = jnp.zeros_like(dg_ref)

        dx, dg = _rms_bwd(x_ref[...], g_ref[...], _dot(dp_ref[...], w_ref[...], NT), dx1_ref[...])
        gx_ref[...] = dx
        dg_ref[...] += dg

    row = pl.BlockSpec((tm, D_MODEL), lambda i: (i, 0))
    return pl.pallas_call(
        body, name="in_bwd_x", grid=(s // tm,),
        in_specs=[pl.BlockSpec((tm, PROJ_W), lambda i: (i, 0)), _full(w_in_p.shape), row, _full((1, D_MODEL)), row],
        out_specs=[row, _full((1, D_MODEL))],
        out_shape=[jax.ShapeDtypeStruct((s, D_MODEL), F32), jax.ShapeDtypeStruct((1, D_MODEL), F32)],
        compiler_params=_params(("arbitrary",)),
    )(dproj, w_in_p, x, g_mix, dx1)


def _rel_bias_bwd(dbias, bucket):
    def body(db_ref, bk_ref, o_ref):
        bk = bk_ref[...]
        lane = lax.broadcasted_iota(jnp.int32, (1, 128), 1)
        for b in range(REL_BUCKETS):
            sel = bk == b
            acc = jnp.zeros((1, 128), F32)
            for h in range(SWA_HEADS):
                tot = jnp.sum(jnp.sum(jnp.where(sel, db_ref[h], 0.0), axis=0, keepdims=True), axis=-1, keepdims=True)
                acc = jnp.where(lane == h, tot, acc)
            o_ref[:, b * 128:(b + 1) * 128] = acc

    return pl.pallas_call(
        body, name="rel_bias_bwd",
        out_shape=jax.ShapeDtypeStruct((1, REL_BUCKETS * 128), F32),
        compiler_params=pltpu.CompilerParams(vmem_limit_bytes=VMEM_LIMIT),
    )(dbias, bucket)


def _my_place():
    return lax.axis_index("x"), lax.axis_index("y"), lax.axis_index("c")


def _peer(place, k):
    x, y, c = place
    return (1 - x if k & 4 else x, 1 - y if k & 2 else y, 1 - c if k & 1 else c)


def _index(place):
    x, y, c = place
    return 4 * x + 2 * y + c


HBM_SPEC = pl.BlockSpec(memory_space=pltpu.HBM)
SEM_SPEC = pl.BlockSpec(memory_space=pltpu.SEMAPHORE)
DATAFLOW = pltpu.SideEffectType.DATAFLOW_SIDE_EFFECTING


ALL_PEERS = tuple(range(1, N_DEV))
SAME_CORE = (2, 4, 6)
OWN = N_DEV - 1


def _split_copy(src_ref, land_ref, send_sems, recv_sems, me, k, gather):
    peer = _peer(me, k)
    if gather:
        src, dst = src_ref, land_ref.at[_index(me)]
    else:
        src, dst = src_ref.at[_index(peer)], land_ref.at[k - 1]
    return pltpu.make_async_remote_copy(src_ref=src, dst_ref=dst, send_sem=send_sems.at[k - 1], recv_sem=recv_sems.at[k - 1],
                                        device_id=peer, device_id_type=MESH)


def _own_copy(src_ref, land_ref, recv_sems, me, gather):
    if gather:
        src, dst = src_ref, land_ref.at[_index(me)]
    else:
        src, dst = src_ref.at[_index(me)], land_ref.at[OWN]
    return pltpu.make_async_copy(src, dst, recv_sems.at[OWN])


def _split_start(srcs, gather, name, peers=ALL_PEERS, after=None):
    n = len(srcs)
    extra = [] if after is None else [after]

    def body(*refs):
        refs = refs[:2 * n] + refs[2 * n + len(extra):]
        src_refs, land_refs = refs[:n], refs[n:2 * n]
        send_sems, recv_sems, token = refs[2 * n:3 * n], refs[3 * n:4 * n], refs[-1]
        me = _my_place()
        for w in range(n):
            for k in peers:
                _split_copy(src_refs[w], land_refs[w], send_sems[w], recv_sems[w], me, k, gather).start()
            _own_copy(src_refs[w], land_refs[w], recv_sems[w], me, gather).start()
        token[...] = jnp.zeros_like(token)

    lands = [lax.empty((N_DEV,) + (a.shape if gather else a.shape[1:]), a.dtype) for a in srcs]
    sems = [pltpu.SemaphoreType.DMA((N_DEV,))] * (2 * n)
    hbm = [pltpu.HBM(a.shape, a.dtype) for a in list(srcs) + lands]
    outs = pl.pallas_call(
        body, name=name,
        out_shape=(*sems, *hbm, jax.ShapeDtypeStruct((8, 128), F32)),
        in_specs=(HBM_SPEC,) * (2 * n) + (pl.BlockSpec(memory_space=pl.ANY),) * len(extra),
        out_specs=(SEM_SPEC,) * (2 * n) + (HBM_SPEC,) * (2 * n) + (pl.BlockSpec(memory_space=pltpu.VMEM),),
        input_output_aliases={i: 2 * n + i for i in range(2 * n)},
        compiler_params=pltpu.CompilerParams(has_side_effects=DATAFLOW),
    )(*[pltpu.with_memory_space_constraint(a, pltpu.HBM) for a in list(srcs) + lands], *extra)
    return list(outs[:n]), list(outs[n:2 * n]), list(outs[2 * n:3 * n]), list(outs[3 * n:4 * n]), outs[-1]


def _split_wait(started, w, after, gather, name):
    send_sems, recv_sems, srcs, lands, _ = started

    def body(src_ref, land_ref, send_sems, recv_sems, after_ref, src_out, land_out):
        me = _my_place()
        for k in ALL_PEERS:
            cp = _split_copy(src_ref, land_ref, send_sems, recv_sems, me, k, gather)
            cp.wait_send()
            cp.wait_recv()
        _own_copy(src_ref, land_ref, recv_sems, me, gather).wait()

    return pl.pallas_call(
        body, name=name,
        out_shape=(pltpu.HBM(srcs[w].shape, srcs[w].dtype), pltpu.HBM(lands[w].shape, lands[w].dtype)),
        in_specs=(HBM_SPEC, HBM_SPEC, SEM_SPEC, SEM_SPEC, pl.BlockSpec(memory_space=pl.ANY)),
        out_specs=(HBM_SPEC, HBM_SPEC), input_output_aliases={0: 0, 1: 1},
        compiler_params=pltpu.CompilerParams(has_side_effects=DATAFLOW),
    )(srcs[w], lands[w], send_sems[w], recv_sems[w], after)[1]


def _forward_copy(land_ref, send_sems, recv_sems, me, j, incoming):
    sibling = _peer(me, 1)
    rows = land_ref.at[_index(_peer(sibling if incoming else me, SAME_CORE[j]))]
    return pltpu.make_async_remote_copy(src_ref=rows, dst_ref=rows, send_sem=send_sems.at[j], recv_sem=recv_sems.at[j],
                                        device_id=sibling, device_id_type=MESH)


def _forward_start(started, after, name):
    send_a, recv_a, srcs, lands, _ = started

    def body(src_ref, land_ref, send_a, recv_a, after_ref, send_b, recv_b, src_out, land_out):
        me = _my_place()
        for j, k in enumerate(SAME_CORE):
            _split_copy(src_ref, land_ref, send_a, recv_a, me, k, True).wait_recv()
            _forward_copy(land_ref, send_b, recv_b, me, j, False).start()

    sems = pltpu.SemaphoreType.DMA((len(SAME_CORE),))
    return pl.pallas_call(
        body, name=name,
        out_shape=(sems, sems, pltpu.HBM(srcs[0].shape, srcs[0].dtype), pltpu.HBM(lands[0].shape, lands[0].dtype)),
        in_specs=(HBM_SPEC, HBM_SPEC, SEM_SPEC, SEM_SPEC, pl.BlockSpec(memory_space=pl.ANY)),
        out_specs=(SEM_SPEC, SEM_SPEC, HBM_SPEC, HBM_SPEC), input_output_aliases={0: 2, 1: 3},
        compiler_params=pltpu.CompilerParams(has_side_effects=DATAFLOW),
    )(srcs[0], lands[0], send_a[0], recv_a[0], after)


def _forward_wait(started, forwarded, name):
    send_a, recv_a, _, _, _ = started
    send_b, recv_b, src, land = forwarded

    def body(src_ref, land_ref, send_a, recv_a, send_b, recv_b, src_out, land_out):
        me = _my_place()
        _own_copy(src_ref, land_ref, recv_a, me, True).wait()
        for k in (1,) + SAME_CORE:
            _split_copy(src_ref, land_ref, send_a, recv_a, me, k, True).wait_send()
        _split_copy(src_ref, land_ref, send_a, recv_a, me, 1, True).wait_recv()
        for j in range(len(SAME_CORE)):
            _forward_copy(land_ref, send_b, recv_b, me, j, False).wait_send()
            _forward_copy(land_ref, send_b, recv_b, me, j, True).wait_recv()

    return pl.pallas_call(
        body, name=name,
        out_shape=(pltpu.HBM(src.shape, src.dtype), pltpu.HBM(land.shape, land.dtype)),
        in_specs=(HBM_SPEC, HBM_SPEC, SEM_SPEC, SEM_SPEC, SEM_SPEC, SEM_SPEC),
        out_specs=(HBM_SPEC, HBM_SPEC), input_output_aliases={0: 0, 1: 1},
        compiler_params=pltpu.CompilerParams(has_side_effects=DATAFLOW),
    )(src, land, send_a[0], recv_a[0], send_b, recv_b)[1]


def _adam_math(w, g, m, v):
    m2 = ADAM_B1 * m + (1.0 - ADAM_B1) * g
    v2 = ADAM_B2 * v + (1.0 - ADAM_B2) * (g * g)
    m_hat = m2 / (1.0 - ADAM_B1 ** ADAM_STEP)
    v_hat = v2 / (1.0 - ADAM_B2 ** ADAM_STEP)
    delta = -ADAM_LR * (m_hat / (jnp.sqrt(v_hat) + ADAM_EPS) + ADAM_WD * w)
    return delta, m2, v2


def _adamw(lands, w, m, v, name):
    a, b = w.shape
    bp = lands[0].shape[2]
    ta = min(128, a)
    per = a // len(lands) // ta

    def body(*refs):
        p_refs = refs[:len(lands)]
        w_ref, m_ref, v_ref, g_ref, d_ref, m2_ref, v2_ref = refs[len(lands):]
        i = pl.program_id(0)

        def run(p_ref):
            g = p_ref[0, :, 0:b].astype(F32)
            for k in range(1, N_DEV):
                g = g + p_ref[k, :, 0:b].astype(F32)
            delta, m2, v2 = _adam_math(w_ref[...], g, m_ref[...], v_ref[...])
            g_ref[...] = g
            d_ref[...] = delta
            m2_ref[...] = m2
            v2_ref[...] = v2

        for part, p_ref in enumerate(p_refs):
            pl.when((i >= part * per) & (i < (part + 1) * per))(functools.partial(run, p_ref))

    land_spec = lambda part: pl.BlockSpec((N_DEV, ta, bp), lambda i: (0, jnp.clip(i - part * per, 0, per - 1), 0))
    blk = pl.BlockSpec((ta, b), lambda i: (i, 0))
    sd = jax.ShapeDtypeStruct((a, b), F32)
    return pl.pallas_call(
        body, name=name, grid=(a // ta,),
        in_specs=[land_spec(part) for part in range(len(lands))] + [blk, blk, blk],
        out_specs=[blk, blk, blk, blk], out_shape=[sd, sd, sd, sd],
        compiler_params=_params(("parallel",)),
    )(*lands, w, m, v)


def _bucket_table():
    t_loc = jnp.arange(SWA_BLOCK)[:, None] + SWA_BLOCK
    s_loc = jnp.arange(2 * SWA_BLOCK)[None, :]
    dist = t_loc - s_loc
    max_exact = REL_BUCKETS // 2
    d = jnp.maximum(dist, 0)
    df = jnp.maximum(d, 1).astype(F32)
    large = max_exact + (jnp.log(df / max_exact) / math.log(REL_MAX_DIST / max_exact) * (REL_BUCKETS - max_exact)).astype(jnp.int32)
    large = jnp.minimum(large, REL_BUCKETS - 1)
    bucket = jnp.where(d < max_exact, d, large)
    band = (dist >= 0) & (dist < SWA_BLOCK)
    return bucket, band


def _tile2(g):
    return jnp.concatenate([g, g], axis=1) if g.shape[1] == HEAD else g


SHARD_W = 737
SHARD_WP = 768
IN_WIDTH = N_DEV * SHARD_W
SEGMENTS = ((GL0, 2824, 3072), (QF0, 768, 512), (KF0, 1280, 512), (VF0, 1792, 512), (QM0, 2312, 512),
            (QA0, 0, 512), (KA0, 512, 128), (VA0, 640, 128), (FL0, 2304, 8))


def _lane_plan(sources):
    plan = []
    for t in range(len(sources) // 128):
        groups = {}
        for lane in range(128):
            src = sources[128 * t + lane]
            if src is not None:
                slab, col = src
                groups.setdefault((slab, col // 128, (lane - col) % 128), []).append(lane)
        tile = []
        for key, lanes in groups.items():
            assert lanes == list(range(lanes[0], lanes[-1] + 1))
            tile.append((key, lanes[0], lanes[-1] + 1))
        plan.append(tile)
    return plan


def _assemble(tile_plan, load, rows):
    lane = lax.broadcasted_iota(jnp.int32, (1, 128), 1)
    out = jnp.zeros((rows, 128), F32)
    for (slab, st, roll), lo, hi in tile_plan:
        v = load(slab, st)
        if roll:
            v = pltpu.roll(v, roll, 1)
        out = v if (lo, hi) == (0, 128) else jnp.where((lane >= lo) & (lane < hi), v, out)
    return out


def _w_in_from_shards(land):
    ref_col = [None] * PROJ_W
    for p0, r0, n in SEGMENTS:
        for i in range(n):
            ref_col[p0 + i] = divmod(r0 + i, SHARD_W)
    plan = _lane_plan(ref_col)
    d_model = land.shape[1]
    tm = 256

    def body(land_ref, o_ref):
        load = lambda slab, st: land_ref[slab, :, st * 128:(st + 1) * 128].astype(F32)
        for t, tile_plan in enumerate(plan):
            o_ref[:, t * 128:(t + 1) * 128] = _assemble(tile_plan, load, tm).astype(BF16)

    return pl.pallas_call(
        body, name="w_in_from_shards", grid=(d_model // tm,),
        in_specs=[pl.BlockSpec((N_DEV, tm, SHARD_WP), lambda i: (0, i, 0))],
        out_specs=pl.BlockSpec((tm, PROJ_W), lambda i: (i, 0)),
        out_shape=jax.ShapeDtypeStruct((d_model, PROJ_W), BF16),
        compiler_params=_params(("parallel",)),
    )(land)


def _dw_in_to_parts(dwp, name):
    padded_col = [None] * IN_WIDTH
    for p0, r0, n in SEGMENTS:
        for i in range(n):
            padded_col[r0 + i] = p0 + i
    sources = []
    for d in range(N_DEV):
        sources += [(0, padded_col[SHARD_W * d + c]) if c < SHARD_W else None for c in range(SHARD_WP)]
    plan = _lane_plan(sources)
    d_model = dwp.shape[0]
    tm = 256
    tiles = SHARD_WP // 128

    def body(dw_ref, o_ref):
        load = lambda slab, st: dw_ref[:, st * 128:(st + 1) * 128].astype(F32)
        for t, tile_plan in enumerate(plan):
            d, c = divmod(t, tiles)
            o_ref[d, :, c * 128:(c + 1) * 128] = _assemble(tile_plan, load, tm).astype(BF16)

    return pl.pallas_call(
        body, name=name, grid=(d_model // tm,),
        in_specs=[pl.BlockSpec((tm, PROJ_W), lambda i: (i, 0))],
        out_specs=pl.BlockSpec((N_DEV, tm, SHARD_WP), lambda i: (0, i, 0)),
        out_shape=jax.ShapeDtypeStruct((N_DEV, d_model, SHARD_WP), BF16),
        compiler_params=_params(("parallel",)),
    )(dwp)


def _cast_shards(shards):
    names = list(shards)

    def body(*refs):
        for src, dst in zip(refs[:len(names)], refs[len(names):]):
            if dst.shape != src.shape:
                dst[...] = jnp.zeros(dst.shape, BF16)
                dst[:, 0:src.shape[1]] = src[...].astype(BF16)
            else:
                dst[...] = src[...].astype(BF16)

    out_shape = [jax.ShapeDtypeStruct((shards[n].shape[0], SHARD_WP if n == "w_in" else shards[n].shape[1]), BF16)
                 for n in names]
    outs = pl.pallas_call(body, name="cast_shards", out_shape=out_shape,
                          compiler_params=pltpu.CompilerParams(vmem_limit_bytes=VMEM_LIMIT))(*[shards[n] for n in names])
    return dict(zip(names, outs))


def _tie(x, *tokens):
    for t in tokens:
        if t is not None:
            x = x + t[0:1, 0:1]
    return x


def _local_step(x, mem, target, p, getw, emit, deps=()):
    s = x.shape[0]
    bucket, band = _bucket_table()
    bucket_m = jnp.where(band, bucket, -1).astype(jnp.int32)
    bias = _bias_table(p["rel_bias"], bucket_m)
    bucket_t = jnp.transpose(bucket_m)
    bias_t = _bias_table(p["rel_bias"], bucket_t)
    gqf, gkf, gqa, gka = _tile2(p["qn_fox"]), _tile2(p["kn_fox"]), _tile2(p["qn_swa"]), _tile2(p["kn_swa"])
    gqm = p["qn_mem"]
    bf128 = jnp.pad(p["b_forget"], ((0, 0), (0, 120)))
    sink = p["sink_swa"].reshape(8)

    h = _rms_fwd(x, p["g_mix"], "rms_mix", tuple(deps) + (bias, bias_t))
    w_in = getw("w_in", h)
    proj = _mm(h, w_in, "nn", BF16, 512, 1536, 1024, "proj")
    fl = _mm(h, w_in[:, FL0:FL0 + 128], "nn", F32, 512, 128, 1024, "proj_fl")
    qf, kf, vf, qm, qa, ka, va, qf_t, vf_t = _proj_post(proj, gqf, gkf, gqm, gqa, gka)
    cc4, ca4 = _fox_gate_fwd(fl, bf128)
    w_kv = getw("w_mem_kv", cc4)
    mem_n, kv_raw, mk, mv = _memkv_fwd(mem, p["g_mem"], w_kv, p["kn_mem"])
    kp = jnp.pad(ka, ((SWA_BLOCK, 0), (0, 0)))
    vp = jnp.pad(va, ((SWA_BLOCK, 0), (0, 0)))
    oa = _swa_fwd(qa, kp, vp, bias, sink)
    of, lse4, of_t = _fox_fwd(qf, kf, vf_t, ca4)
    om = _mem_fwd(qm, mk, mv)
    wa, wf, wm, w_out = getw("w_o_swa", oa), getw("w_o_fox", oa), getw("w_o_mem", oa), getw("w_out", oa)
    x1, hm, merged = _merge_fwd(x, oa, of, om, proj, p["b_gate"], wa, wf, wm, w_out, p["g_mlp"])
    w_up = getw("w_mlp_up", of)
    u = _mlp_up(hm, w_up)
    w_down = getw("w_mlp_down", hm)
    dy, dy_b, loss = _mlp_down_loss(u, w_down, x1, target)

    da = _mlp_bwd_act(dy_b, w_down, u)
    t_down = emit({"w_mlp_down": _mm(u, dy_b, "tn", BF16, 1024, 1024, 2048, "dw_down")})
    dx1, dg_mlp = _mlp_bwd_x(da, w_up, x1, dy, _tie(p["g_mlp"], t_down))
    t_up = emit({"w_mlp_up": _mm(hm, da, "tn", BF16, 1024, 1024, 2048, "dw_up", column_chunks=True)})
    dproj, doa, dof_t, dom, dya, dyf, dym, db_gate = _merge_bwd(
        dx1, oa, of, om, proj, _tie(p["b_gate"], t_up), wa, wf, wm, w_out)
    dw_oa, dw_of, dw_om = _mm_tn3([oa, of, om], [dya, dyf, dym], "dw_o")
    t_o = emit({"w_out": _mm(merged, dx1, "tn", BF16, 1024, 1024, 2048, "dw_out"),
                "w_o_swa": dw_oa, "w_o_fox": dw_of, "w_o_mem": dw_om})

    dqm, dmk, dmv = _mem_bwd(qm, mk, mv, dom)
    dw_kv, dkn_mem, dg_mem = _memkv_bwd(dmk, dmv, kv_raw, _tie(p["kn_mem"], t_o), mem, p["g_mem"], mem_n, w_kv)
    t_kv = emit({"w_mem_kv": dw_kv})
    dqa, dkp, dvp, dbias, dsink = _swa_bwd(qa, kp, vp, bias_t, _tie(p["sink_swa"], t_kv).reshape(8), doa)
    dqf_t, dkf, dvf, dck4, dcq4 = _fox_bwd(qf_t, kf, vf, dof_t, of_t, cc4, lse4)

    dfl, db_forget = _fox_gate_bwd(dcq4, dck4, fl, bf128)

    dproj, dgn = _proj_pre_bwd(dproj, proj, dqf_t, dkf, dvf, dqm, dqa, dkp[SWA_BLOCK:], dvp[SWA_BLOCK:], dfl,
                               gqf, gkf, gqm, gqa, gka)
    t_in = emit({"w_in_a": _mm(h, dproj, "tn", BF16, 512, 3072, 1024, "dw_in_a", m_part=(0, 2))})
    t_in = emit({"w_in_b": _mm(h, dproj, "tn", BF16, 512, 3072, 1024, "dw_in_b", m_part=(1, 2), after=t_in)})
    grad_x, dg_mix = _in_bwd_x(dproj, w_in, x, _tie(p["g_mix"], t_in), dx1)
    d_rel = _rel_bias_bwd(dbias, bucket_t)

    fold = lambda r: dgn[r:r + 1, 0:HEAD] + dgn[r:r + 1, HEAD:128]
    small = {
        "g_mix": dg_mix, "b_gate": db_gate, "b_forget": db_forget[:, 0:8],
        "qn_swa": fold(3), "kn_swa": fold(4), "sink_swa": dsink[:, 0].reshape(1, 8), "rel_bias": d_rel,
        "qn_fox": fold(0), "kn_fox": fold(1), "g_mem": dg_mem, "qn_mem": dgn[2:3, :], "kn_mem": dkn_mem,
        "g_mlp": dg_mlp,
    }
    return loss, grad_x, small


SMALL = ("g_mix", "b_gate", "b_forget", "qn_swa", "kn_swa", "sink_swa", "rel_bias", "qn_fox", "kn_fox", "g_mem",
         "qn_mem", "kn_mem", "g_mlp")
BIG = ("w_in", "w_mem_kv", "w_o_swa", "w_o_fox", "w_o_mem", "w_out", "w_mlp_up", "w_mlp_down")
COL_SHARDED = ("w_in", "w_o_swa", "w_o_fox", "w_o_mem", "w_mlp_up")
WEIGHTS = ("g_mix", "w_in", "b_gate", "b_forget", "qn_swa", "kn_swa", "sink_swa", "rel_bias", "qn_fox", "kn_fox", "g_mem",
           "w_mem_kv", "qn_mem", "kn_mem", "w_o_swa", "w_o_fox", "w_o_mem", "w_out", "g_mlp", "w_mlp_up", "w_mlp_down")
SMALL_SLOTS = (("g_mix", 1024), ("b_gate", 3072), ("b_forget", 128), ("qn_swa", 128), ("kn_swa", 128), ("sink_swa", 128),
               ("rel_bias", REL_BUCKETS * 128), ("qn_fox", 128), ("kn_fox", 128), ("g_mem", 1024), ("qn_mem", 128),
               ("kn_mem", 128), ("g_mlp", 1024), ("loss", 128))
SMALL_OFF = {n: sum(w for _, w in SMALL_SLOTS[:i]) for i, (n, _) in enumerate(SMALL_SLOTS)}
SMALL_ROW = sum(w for _, w in SMALL_SLOTS)


def _gathered_to_full(name, g):
    if name in COL_SHARDED:
        return jnp.transpose(g, (1, 0, 2)).reshape(g.shape[1], N_DEV * g.shape[2])
    return g.reshape(N_DEV * g.shape[1], g.shape[2])


def _full_to_parts(name, full, b):
    if name in COL_SHARDED:
        return jnp.transpose(full.reshape(full.shape[0], N_DEV, b), (1, 0, 2)).astype(BF16)
    return full.reshape(N_DEV, full.shape[0] // N_DEV, full.shape[1]).astype(BF16)


def _pack_small(grads, loss):
    pieces = []
    for n, width in SMALL_SLOTS:
        a = loss.reshape(1, 1) if n == "loss" else grads[n].reshape(1, -1)
        pieces.append(jnp.pad(a, ((0, 0), (0, width - a.shape[1]))))
    return jnp.concatenate(pieces, axis=1)


def _adamw_small(gathered, w, m, v):
    names = list(SMALL)

    def body(*refs):
        p_ref = refs[0]
        ins = refs[1:1 + 3 * len(names)]
        outs = refs[1 + 3 * len(names):]
        g_all = p_ref[0]
        for k in range(1, N_DEV):
            g_all = g_all + p_ref[k]
        for i, n in enumerate(names):
            w_ref, m_ref, v_ref = ins[3 * i:3 * i + 3]
            out = outs[4 * i:4 * i + 4]
            rows, cols = w_ref.shape
            for r in range(rows):
                off = SMALL_OFF[n] + 128 * r
                g = g_all[:, off:off + cols]
                rs = slice(r, r + 1)
                res = (g,) + _adam_math(w_ref[rs, :], g, m_ref[rs, :], v_ref[rs, :])
                for o_ref, val in zip(out, res):
                    o_ref[rs, :] = val
        outs[-1][...] = g_all[:, SMALL_OFF["loss"]:SMALL_OFF["loss"] + 128]

    args = [gathered]
    out_shape = []
    for n in names:
        args += [w[n], m[n], v[n]]
        out_shape += [jax.ShapeDtypeStruct(w[n].shape, F32)] * 4
    out_shape.append(jax.ShapeDtypeStruct((1, 128), F32))
    outs = pl.pallas_call(body, name="adamw_small", out_shape=out_shape)(*args)
    return {n: outs[4 * i:4 * i + 4] for i, n in enumerate(names)}, outs[-1]


def kernel(x, mem, g_mix, w_in, b_gate, b_forget, qn_swa, kn_swa, sink_swa, rel_bias, qn_fox, kn_fox, g_mem, w_mem_kv, qn_mem, kn_mem, w_o_swa, w_o_fox, w_o_mem, w_out, g_mlp, w_mlp_up, w_mlp_down, loss_target, m_g_mix, m_w_in, m_b_gate, m_b_forget, m_qn_swa, m_kn_swa, m_sink_swa, m_rel_bias, m_qn_fox, m_kn_fox, m_g_mem, m_w_mem_kv, m_qn_mem, m_kn_mem, m_w_o_swa, m_w_o_fox, m_w_o_mem, m_w_out, m_g_mlp, m_w_mlp_up, m_w_mlp_down, v_g_mix, v_w_in, v_b_gate, v_b_forget, v_qn_swa, v_kn_swa, v_sink_swa, v_rel_bias, v_qn_fox, v_kn_fox, v_g_mem, v_w_mem_kv, v_qn_mem, v_kn_mem, v_w_o_swa, v_w_o_fox, v_w_o_mem, v_w_out, v_g_mlp, v_w_mlp_up, v_w_mlp_down):
    wts = dict(g_mix=g_mix, w_in=w_in, b_gate=b_gate, b_forget=b_forget, qn_swa=qn_swa, kn_swa=kn_swa, sink_swa=sink_swa,
               rel_bias=rel_bias, qn_fox=qn_fox, kn_fox=kn_fox, g_mem=g_mem, w_mem_kv=w_mem_kv, qn_mem=qn_mem, kn_mem=kn_mem,
               w_o_swa=w_o_swa, w_o_fox=w_o_fox, w_o_mem=w_o_mem, w_out=w_out, g_mlp=g_mlp, w_mlp_up=w_mlp_up,
               w_mlp_down=w_mlp_down)
    mom = dict(g_mix=m_g_mix, w_in=m_w_in, b_gate=m_b_gate, b_forget=m_b_forget, qn_swa=m_qn_swa, kn_swa=m_kn_swa,
               sink_swa=m_sink_swa, rel_bias=m_rel_bias, qn_fox=m_qn_fox, kn_fox=m_kn_fox, g_mem=m_g_mem, w_mem_kv=m_w_mem_kv,
               qn_mem=m_qn_mem, kn_mem=m_kn_mem, w_o_swa=m_w_o_swa, w_o_fox=m_w_o_fox, w_o_mem=m_w_o_mem, w_out=m_w_out,
               g_mlp=m_g_mlp, w_mlp_up=m_w_mlp_up, w_mlp_down=m_w_mlp_down)
    var = dict(g_mix=v_g_mix, w_in=v_w_in, b_gate=v_b_gate, b_forget=v_b_forget, qn_swa=v_qn_swa, kn_swa=v_kn_swa,
               sink_swa=v_sink_swa, rel_bias=v_rel_bias, qn_fox=v_qn_fox, kn_fox=v_kn_fox, g_mem=v_g_mem, w_mem_kv=v_w_mem_kv,
               qn_mem=v_qn_mem, kn_mem=v_kn_mem, w_o_swa=v_w_o_swa, w_o_fox=v_w_o_fox, w_o_mem=v_w_o_mem, w_out=v_w_out,
               g_mlp=v_g_mlp, w_mlp_up=v_w_mlp_up, w_mlp_down=v_w_mlp_down)

    shards = _cast_shards({n: wts[n][0] for n in BIG})
    first = _split_start([shards["w_in"]], True, "ag_start_w_in", peers=(1,) + SAME_CORE)
    rest = _split_start([shards[n] for n in BIG[1:]], True, "ag_start_rest", after=first[4])
    full = {}

    def getw(n, after):
        if n == "w_in" and n not in full:
            forwarded = _forward_start(first, after, "ag_forward_w_in")
            full[n] = _w_in_from_shards(_forward_wait(first, forwarded, "ag_wait_w_in"))
        elif n not in full:
            land = _split_wait(rest, BIG[1:].index(n), after, True, "ag_wait_" + n)
            full[n] = land if n == "w_mlp_up" else _gathered_to_full(n, land)
        return full[n]

    exchanges = {}

    def emit(grads_by_name):
        parts = []
        for n, grad in grads_by_name.items():
            if n.startswith("w_in"):
                parts.append(_dw_in_to_parts(grad, "d" + n + "_to_parts"))
            else:
                parts.append(grad if n == "w_mlp_up" else _full_to_parts(n, grad, wts[n].shape[2]))
        started = _split_start(parts, False, "rs_start_" + next(iter(grads_by_name)))
        for w, n in enumerate(grads_by_name):
            exchanges[n] = (started, w)
        return started[4]

    small_p = {n: wts[n] for n in SMALL}
    loss, grad_x, small_g = _local_step(x[0], mem[0], loss_target[0], small_p, getw, emit, (first[4], rest[4]))

    packed = _pack_small(small_g, loss)
    small_gather = _split_start([packed], True, "ag_start_small")

    grads, delta, new_m, new_v = {}, {}, {}, {}

    def update(n, after):
        lands = [_split_wait(*exchanges[e], after, False, "rs_wait_" + e) for e in exchanges if e.startswith(n)]
        g, d, m2, v2 = _adamw(lands, wts[n][0], mom[n][0], var[n][0], "adamw_" + n)
        grads[n], delta[n], new_m[n], new_v[n] = g[None], d[None], m2[None], v2[None]
        return d

    after = small_gather[4]
    for n in exchanges:
        if not n.startswith("w_in"):
            after = update(n, after)

    gathered = _split_wait(small_gather, 0, after, True, "ag_wait_small")
    small_out, total = _adamw_small(gathered, small_p, mom, var)
    for name, (g, d, m2, v2) in small_out.items():
        grads[name], delta[name], new_m[name], new_v[name] = g, d, m2, v2
    update("w_in", total)

    return (total[0, 0], grad_x[None], *[grads[n] for n in WEIGHTS], *[delta[n] for n in WEIGHTS],
            *[new_m[n] for n in WEIGHTS], *[new_v[n] for n in WEIGHTS])
```
